```python
import jax, jax.numpy as jnp
from jax import lax
import numpy as np

D_MODEL = 1024
BATCH = 16
SEQ = 2048
DEPTH = 1

GRID_W = 64
CTX_LEN = 256
EPS = 1e-6
CONV_WIDTH = D_MODEL
CONV_K = 31
GLA_HEADS = 4
GLA_DK = D_MODEL // 2
GLA_DV = D_MODEL
HEAD_K = GLA_DK // GLA_HEADS
HEAD_V = GLA_DV // GLA_HEADS
GATE_RANK = 16
GATE_TAU = 16.0
CHUNK = 64
IN_SIZES = (CONV_WIDTH, CONV_WIDTH, CONV_WIDTH, GLA_DK, GLA_DK, GLA_DV, GATE_RANK, GATE_RANK, GLA_DV, D_MODEL, D_MODEL)
N_IN = 3 * CONV_WIDTH + 2 * GLA_DK + 2 * GLA_DV + 2 * GATE_RANK + 2 * D_MODEL
STATE_SIZES = (GLA_DK, GLA_DV, GATE_RANK, GATE_RANK)
STATE_LO = 3 * CONV_WIDTH + GLA_DK
STATE_HI = STATE_LO + GLA_DK + GLA_DV + 2 * GATE_RANK

kernel_name = "hybrid_conformer_gla_dit_block"


def split_cols(p, sizes):
    idx = [int(i) for i in np.cumsum(sizes)[:-1]]
    return jnp.split(p, idx, axis=-1)


def rmsnorm(x, g):
    xf = x.astype(jnp.float32)
    y = xf * lax.rsqrt(jnp.mean(xf * xf, axis=-1, keepdims=True) + EPS)
    return y * g.astype(jnp.float32)


def layernorm(x, g, b):
    xf = x.astype(jnp.float32)
    mu = jnp.mean(xf, axis=-1, keepdims=True)
    var = jnp.mean(jnp.square(xf - mu), axis=-1, keepdims=True)
    return (xf - mu) * lax.rsqrt(var + EPS) * g + b


def ada_mod(cvec, w, b):
    m = jax.nn.silu(cvec) @ w + b
    return jnp.split(m, 3, axis=-1)


def dwconv(x, w, b):
    C = x.shape[-1]
    y = lax.conv_general_dilated(x, w.astype(x.dtype)[:, None, :], (1,), [(CONV_K // 2, CONV_K // 2)],
                                 dimension_numbers=('NWC', 'WIO', 'NWC'), feature_group_count=C)
    return y + b.astype(x.dtype)


def axial_dwconv(a, w, b, rows):
    Bn, S, C = a.shape
    half = C // 2
    ah = a[..., :half].reshape(Bn * rows, GRID_W, half)
    yh = dwconv(ah, w[:, :half], b[:half]).reshape(Bn, S, half)
    av = a[..., half:].reshape(Bn, rows, GRID_W, C - half).transpose(0, 2, 1, 3).reshape(Bn * GRID_W, rows, C - half)
    yv = dwconv(av, w[:, half:], b[half:]).reshape(Bn, GRID_W, rows, C - half).transpose(0, 2, 1, 3).reshape(Bn, S, C - half)
    return jnp.concatenate([yh, yv], axis=-1)


def conv_branch(glu_v, glu_g, z, conv_fn, ln_g, ln_b, proj):
    a = glu_v * jax.nn.sigmoid(glu_g)
    a = conv_fn(a)
    a = jax.nn.silu(layernorm(a, ln_g, ln_b))
    return (a * jax.nn.silu(z)) @ proj


def heads(t, hd):
    Bn, T, _ = t.shape
    return t.reshape(Bn, T, GLA_HEADS, hd).transpose(0, 2, 1, 3).astype(jnp.float32)


def log_decay(lr, up, bias):
    return heads(jax.nn.log_sigmoid((lr @ up + bias).astype(jnp.float32)) / GATE_TAU, HEAD_K)


def flip(t):
    return jnp.flip(t, axis=2)


def gla_final_state(k, v, g):
    b = jnp.cumsum(g, axis=2)
    w = jnp.exp(b[:, :, -1:, :] - b)
    return jnp.einsum('bhtd,bhte->bhde', k * w, v)


def gla_chunk_scan(q, k, v, g, s0):
    Bn, H, T, _ = q.shape
    dv = v.shape[-1]
    n = T // CHUNK

    def chunks(t):
        return jnp.moveaxis(t.reshape(Bn, H, n, CHUNK, t.shape[-1]), 2, 0)

    lower = jnp.tril(jnp.ones((CHUNK, CHUNK), dtype=bool))

    def step(S, inp):
        qc, kc, vc, gc = inp
        b = jnp.cumsum(gc, axis=2)
        o_inter = jnp.einsum('bhld,bhde->bhle', qc * jnp.exp(b), S)
        rel = jnp.where(lower[:, :, None], b[:, :, :, None, :] - b[:, :, None, :, :], -jnp.inf)
        A = jnp.einsum('bhid,bhjd,bhijd->bhij', qc, kc, jnp.exp(rel))
        o_intra = jnp.einsum('bhij,bhje->bhie', A, vc)
        b_last = b[:, :, -1:, :]
        S_new = jnp.exp(b_last[:, :, 0, :, None]) * S + jnp.einsum('bhld,bhle->bhde', kc * jnp.exp(b_last - b), vc)
        return S_new, o_inter + o_intra

    S_fin, o = lax.scan(step, s0, (chunks(q), chunks(k), chunks(v), chunks(g)))
    return jnp.moveaxis(o, 0, 2).reshape(Bn, H, T, dv), S_fin


def bidir_gla(q, k, v, gf, gb, s_f, s_b):
    o_f, _ = gla_chunk_scan(q, k, v, gf, s_f)
    o_b, _ = gla_chunk_scan(flip(q), flip(k), flip(v), flip(gb), s_b)
    return o_f + flip(o_b)


def gla_output(o, r, norm_g, proj):
    o = o * lax.rsqrt(jnp.mean(o * o, axis=-1, keepdims=True) + EPS) * norm_g
    Bn, H, T, dv = o.shape
    o = o.transpose(0, 2, 1, 3).reshape(Bn, T, H * dv)
    return (o * jax.nn.silu(r)) @ proj


def _fwd_setup_inputs(seed: int = 0) -> dict:
    key = jax.random.key(seed)
    ks = jax.random.split(key, 24)

    def nrm(k, shape, scale=1.0):
        return jax.random.normal(k, shape, jnp.float32) * scale

    return {
        "x": nrm(ks[0], (BATCH, SEQ, D_MODEL)),
        "c": nrm(ks[1], (BATCH, D_MODEL)),
        "ctx": nrm(ks[2], (BATCH, CTX_LEN, D_MODEL)),
        "c_ctx": nrm(ks[3], (D_MODEL,)),
        "ada_w": nrm(ks[4], (DEPTH, D_MODEL, 3 * D_MODEL), D_MODEL ** -0.5),
        "ada_b": nrm(ks[5], (DEPTH, 3 * D_MODEL), 0.02),
        "norm_g": 1.0 + nrm(ks[6], (DEPTH, D_MODEL), 0.02),
        "w_in": nrm(ks[7], (DEPTH, D_MODEL, N_IN), D_MODEL ** -0.5),
        "b_in": nrm(ks[8], (DEPTH, N_IN), 0.02),
        "conv_w": nrm(ks[9], (DEPTH, CONV_K, CONV_WIDTH), CONV_K ** -0.5),
        "conv_b": nrm(ks[10], (DEPTH, CONV_WIDTH), 0.02),
        "conv_ln_g": 1.0 + nrm(ks[11], (DEPTH, CONV_WIDTH), 0.02),
        "conv_ln_b": nrm(ks[12], (DEPTH, CONV_WIDTH), 0.02),
        "conv_proj": nrm(ks[13], (DEPTH, CONV_WIDTH, D_MODEL), CONV_WIDTH ** -0.5),
        "decay_up_fwd": nrm(ks[14], (DEPTH, GATE_RANK, GLA_DK), GATE_RANK ** -0.5),
        "decay_bias_fwd": nrm(ks[15], (DEPTH, GLA_DK), 0.1),
        "decay_up_bwd": nrm(ks[16], (DEPTH, GATE_RANK, GLA_DK), GATE_RANK ** -0.5),
        "decay_bias_bwd": nrm(ks[17], (DEPTH, GLA_DK), 0.1),
        "gla_norm_g": 1.0 + nrm(ks[18], (DEPTH, HEAD_V), 0.02),
        "gla_proj": nrm(ks[19], (DEPTH, GLA_DV, D_MODEL), GLA_DV ** -0.5),
        "w_out": nrm(ks[20], (DEPTH, D_MODEL, D_MODEL), D_MODEL ** -0.5),
        "final_norm_g": 1.0 + nrm(ks[21], (D_MODEL,), 0.02),
    }


def _fwd_reference(x, c, ctx, c_ctx, ada_w, ada_b, norm_g, w_in, b_in, conv_w, conv_b, conv_ln_g, conv_ln_b,
              conv_proj, decay_up_fwd, decay_bias_fwd, decay_up_bwd, decay_bias_bwd, gla_norm_g, gla_proj,
              w_out, final_norm_g):
    Bn, S, _ = x.shape
    rows = S // GRID_W
    h = x
    hc = ctx
    for l in range(DEPTH):
        last = l == DEPTH - 1
        shift, scale, gate = ada_mod(c, ada_w[l], ada_b[l])
        shift_c, scale_c, gate_c = ada_mod(c_ctx, ada_w[l], ada_b[l])
        u = rmsnorm(h, norm_g[l]) * (1.0 + scale[:, None, :]) + shift[:, None, :]
        uc = rmsnorm(hc, norm_g[l]) * (1.0 + scale_c) + shift_c

        if last:
            pc = uc @ w_in[l][:, STATE_LO:STATE_HI] + b_in[l][STATE_LO:STATE_HI]
            kc_, vc_, afc, abc = split_cols(pc, STATE_SIZES)
        else:
            (gvc, ggc, zc, qc_, kc_, vc_, afc, abc, rc, mgc_conv, mgc_gla) = split_cols(uc @ w_in[l] + b_in[l], IN_SIZES)
        k_ctx = heads(kc_, HEAD_K)
        v_ctx = heads(vc_, HEAD_V)
        gf_ctx = log_decay(afc, decay_up_fwd[l], decay_bias_fwd[l])
        gb_ctx = log_decay(abc, decay_up_bwd[l], decay_bias_bwd[l])
        s_f = gla_final_state(k_ctx, v_ctx, gf_ctx)
        s_b = gla_final_state(flip(k_ctx), flip(v_ctx), flip(gb_ctx))

        (gv, gg, z, q_, k_, v_, af, ab, r, mg_conv, mg_gla) = split_cols(u @ w_in[l] + b_in[l], IN_SIZES)
        y_conv = conv_branch(gv, gg, z, lambda a: axial_dwconv(a, conv_w[l], conv_b[l], rows),
                             conv_ln_g[l], conv_ln_b[l], conv_proj[l])
        q = heads(q_, HEAD_K) * (HEAD_K ** -0.5)
        k = heads(k_, HEAD_K)
        v = heads(v_, HEAD_V)
        gf = log_decay(af, decay_up_fwd[l], decay_bias_fwd[l])
        gb = log_decay(ab, decay_up_bwd[l], decay_bias_bwd[l])
        o = bidir_gla(q, k, v, gf, gb, s_f, s_b)
        y_gla = gla_output(o, r, gla_norm_g[l], gla_proj[l])
        merged = jax.nn.sigmoid(mg_conv) * y_conv + jax.nn.sigmoid(mg_gla) * y_gla
        h_new = h + gate[:, None, :] * (merged @ w_out[l])

        if not last:
            yc_conv = conv_branch(gvc, ggc, zc, lambda a: dwconv(a, conv_w[l], conv_b[l]),
                                  conv_ln_g[l], conv_ln_b[l], conv_proj[l])
            q_ctx = heads(qc_, HEAD_K) * (HEAD_K ** -0.5)
            zero_state = jnp.zeros((Bn, GLA_HEADS, HEAD_K, HEAD_V), jnp.float32)
            oc = bidir_gla(q_ctx, k_ctx, v_ctx, gf_ctx, gb_ctx, zero_state, zero_state)
            yc_gla = gla_output(oc, rc, gla_norm_g[l], gla_proj[l])
            merged_c = jax.nn.sigmoid(mgc_conv) * yc_conv + jax.nn.sigmoid(mgc_gla) * yc_gla
            hc = hc + gate_c * (merged_c @ w_out[l])
        h = h_new
    return rmsnorm(h, final_norm_g)


import jax as _jax
import jax.numpy as _jnp

TWIN_FORMAT = 'train_step'
FWD_PARAMS = ['x', 'c', 'ctx', 'c_ctx', 'ada_w', 'ada_b', 'norm_g', 'w_in', 'b_in', 'conv_w', 'conv_b', 'conv_ln_g', 'conv_ln_b', 'conv_proj', 'decay_up_fwd', 'decay_bias_fwd', 'decay_up_bwd', 'decay_bias_bwd', 'gla_norm_g', 'gla_proj', 'w_out', 'final_norm_g']
TWIN_WEIGHTS = ['c_ctx', 'ada_w', 'ada_b', 'norm_g', 'w_in', 'b_in', 'conv_w', 'conv_b', 'conv_ln_g', 'conv_ln_b', 'conv_proj', 'decay_up_fwd', 'decay_bias_fwd', 'decay_up_bwd', 'decay_bias_bwd', 'gla_norm_g', 'gla_proj', 'w_out', 'final_norm_g']
TWIN_DIFF_INPUT = 'x'
TWIN_INPUTS = ['x', 'c', 'ctx', 'c_ctx', 'ada_w', 'ada_b', 'norm_g', 'w_in', 'b_in', 'conv_w', 'conv_b', 'conv_ln_g', 'conv_ln_b', 'conv_proj', 'decay_up_fwd', 'decay_bias_fwd', 'decay_up_bwd', 'decay_bias_bwd', 'gla_norm_g', 'gla_proj', 'w_out', 'final_norm_g', 'loss_target', 'm_c_ctx', 'm_ada_w', 'm_ada_b', 'm_norm_g', 'm_w_in', 'm_b_in', 'm_conv_w', 'm_conv_b', 'm_conv_ln_g', 'm_conv_ln_b', 'm_conv_proj', 'm_decay_up_fwd', 'm_decay_bias_fwd', 'm_decay_up_bwd', 'm_decay_bias_bwd', 'm_gla_norm_g', 'm_gla_proj', 'm_w_out', 'm_final_norm_g', 'v_c_ctx', 'v_ada_w', 'v_ada_b', 'v_norm_g', 'v_w_in', 'v_b_in', 'v_conv_w', 'v_conv_b', 'v_conv_ln_g', 'v_conv_ln_b', 'v_conv_proj', 'v_decay_up_fwd', 'v_decay_bias_fwd', 'v_decay_up_bwd', 'v_decay_bias_bwd', 'v_gla_norm_g', 'v_gla_proj', 'v_w_out', 'v_final_norm_g']
TWIN_OUTPUTS = ['loss', 'grad_x', 'grad_c_ctx', 'grad_ada_w', 'grad_ada_b', 'grad_norm_g', 'grad_w_in', 'grad_b_in', 'grad_conv_w', 'grad_conv_b', 'grad_conv_ln_g', 'grad_conv_ln_b', 'grad_conv_proj', 'grad_decay_up_fwd', 'grad_decay_bias_fwd', 'grad_decay_up_bwd', 'grad_decay_bias_bwd', 'grad_gla_norm_g', 'grad_gla_proj', 'grad_w_out', 'grad_final_norm_g', 'delta_c_ctx', 'delta_ada_w', 'delta_ada_b', 'delta_norm_g', 'delta_w_in', 'delta_b_in', 'delta_conv_w', 'delta_conv_b', 'delta_conv_ln_g', 'delta_conv_ln_b', 'delta_conv_proj', 'delta_decay_up_fwd', 'delta_decay_bias_fwd', 'delta_decay_up_bwd', 'delta_decay_bias_bwd', 'delta_gla_norm_g', 'delta_gla_proj', 'delta_w_out', 'delta_final_norm_g', 'new_m_c_ctx', 'new_m_ada_w', 'new_m_ada_b', 'new_m_norm_g', 'new_m_w_in', 'new_m_b_in', 'new_m_conv_w', 'new_m_conv_b', 'new_m_conv_ln_g', 'new_m_conv_ln_b', 'new_m_conv_proj', 'new_m_decay_up_fwd', 'new_m_decay_bias_fwd', 'new_m_decay_up_bwd', 'new_m_decay_bias_bwd', 'new_m_gla_norm_g', 'new_m_gla_proj', 'new_m_w_out', 'new_m_final_norm_g', 'new_v_c_ctx', 'new_v_ada_w', 'new_v_ada_b', 'new_v_norm_g', 'new_v_w_in', 'new_v_b_in', 'new_v_conv_w', 'new_v_conv_b', 'new_v_conv_ln_g', 'new_v_conv_ln_b', 'new_v_conv_proj', 'new_v_decay_up_fwd', 'new_v_decay_bias_fwd', 'new_v_decay_up_bwd', 'new_v_decay_bias_bwd', 'new_v_gla_norm_g', 'new_v_gla_proj', 'new_v_w_out', 'new_v_final_norm_g']
TWIN_LEAF_KINDS = {'loss': 'loss', 'grad_x': 'grad_x', 'grad_c_ctx': 'grad_w', 'grad_ada_w': 'grad_w', 'grad_ada_b': 'grad_w', 'grad_norm_g': 'grad_w', 'grad_w_in': 'grad_w', 'grad_b_in': 'grad_w', 'grad_conv_w': 'grad_w', 'grad_conv_b': 'grad_w', 'grad_conv_ln_g': 'grad_w', 'grad_conv_ln_b': 'grad_w', 'grad_conv_proj': 'grad_w', 'grad_decay_up_fwd': 'grad_w', 'grad_decay_bias_fwd': 'grad_w', 'grad_decay_up_bwd': 'grad_w', 'grad_decay_bias_bwd': 'grad_w', 'grad_gla_norm_g': 'grad_w', 'grad_gla_proj': 'grad_w', 'grad_w_out': 'grad_w', 'grad_final_norm_g': 'grad_w', 'delta_c_ctx': 'delta_w', 'delta_ada_w': 'delta_w', 'delta_ada_b': 'delta_w', 'delta_norm_g': 'delta_w', 'delta_w_in': 'delta_w', 'delta_b_in': 'delta_w', 'delta_conv_w': 'delta_w', 'delta_conv_b': 'delta_w', 'delta_conv_ln_g': 'delta_w', 'delta_conv_ln_b': 'delta_w', 'delta_conv_proj': 'delta_w', 'delta_decay_up_fwd': 'delta_w', 'delta_decay_bias_fwd': 'delta_w', 'delta_decay_up_bwd': 'delta_w', 'delta_decay_bias_bwd': 'delta_w', 'delta_gla_norm_g': 'delta_w', 'delta_gla_proj': 'delta_w', 'delta_w_out': 'delta_w', 'delta_final_norm_g': 'delta_w', 'new_m_c_ctx': 'new_m', 'new_m_ada_w': 'new_m', 'new_m_ada_b': 'new_m', 'new_m_norm_g': 'new_m', 'new_m_w_in': 'new_m', 'new_m_b_in': 'new_m', 'new_m_conv_w': 'new_m', 'new_m_conv_b': 'new_m', 'new_m_conv_ln_g': 'new_m', 'new_m_conv_ln_b': 'new_m', 'new_m_conv_proj': 'new_m', 'new_m_decay_up_fwd': 'new_m', 'new_m_decay_bias_fwd': 'new_m', 'new_m_decay_up_bwd': 'new_m', 'new_m_decay_bias_bwd': 'new_m', 'new_m_gla_norm_g': 'new_m', 'new_m_gla_proj': 'new_m', 'new_m_w_out': 'new_m', 'new_m_final_norm_g': 'new_m', 'new_v_c_ctx': 'new_v', 'new_v_ada_w': 'new_v', 'new_v_ada_b': 'new_v', 'new_v_norm_g': 'new_v', 'new_v_w_in': 'new_v', 'new_v_b_in': 'new_v', 'new_v_conv_w': 'new_v', 'new_v_conv_b': 'new_v', 'new_v_conv_ln_g': 'new_v', 'new_v_conv_ln_b': 'new_v', 'new_v_conv_proj': 'new_v', 'new_v_decay_up_fwd': 'new_v', 'new_v_decay_bias_fwd': 'new_v', 'new_v_decay_up_bwd': 'new_v', 'new_v_decay_bias_bwd': 'new_v', 'new_v_gla_norm_g': 'new_v', 'new_v_gla_proj': 'new_v', 'new_v_w_out': 'new_v', 'new_v_final_norm_g': 'new_v'}


def _forward(args):
    return _fwd_reference(*[args[k] for k in FWD_PARAMS])


def _output_shape():
    out = _jax.eval_shape(lambda: _forward(_fwd_setup_inputs(0)))
    return out.shape, out.dtype

N_MICROBATCH = 1
ADAM_LR = 0.001
ADAM_B1 = 0.9
ADAM_B2 = 0.999
ADAM_EPS = 1e-08
ADAM_WD = 0.01
ADAM_STEP = 10
PER_EXAMPLE_BATCH_AXIS = {'x': 0, 'c': 0, 'ctx': 0, 'loss_target': 0}
SHARED_INPUTS = []
_WEIGHT_DTYPES = {'c_ctx': _jnp.float32, 'ada_w': _jnp.float32, 'ada_b': _jnp.float32, 'norm_g': _jnp.float32, 'w_in': _jnp.float32, 'b_in': _jnp.float32, 'conv_w': _jnp.float32, 'conv_b': _jnp.float32, 'conv_ln_g': _jnp.float32, 'conv_ln_b': _jnp.float32, 'conv_proj': _jnp.float32, 'decay_up_fwd': _jnp.float32, 'decay_bias_fwd': _jnp.float32, 'decay_up_bwd': _jnp.float32, 'decay_bias_bwd': _jnp.float32, 'gla_norm_g': _jnp.float32, 'gla_proj': _jnp.float32, 'w_out': _jnp.float32, 'final_norm_g': _jnp.float32}
MOMENT_SCALE = {'c_ctx': 1.399276e-02, 'ada_w': 6.628277e-02, 'ada_b': 1.129580e-01, 'norm_g': 1.002958e-01, 'w_in': 4.495936e-02, 'b_in': 4.636620e-02, 'conv_w': 3.162009e-02, 'conv_b': 5.502699e-02, 'conv_ln_g': 3.663925e-02, 'conv_ln_b': 3.215171e-02, 'conv_proj': 2.983632e-02, 'decay_up_fwd': 2.165677e-02, 'decay_bias_fwd': 3.347405e-02, 'decay_up_bwd': 1.840284e-02, 'decay_bias_bwd': 3.156863e-02, 'gla_norm_g': 1.075168e-01, 'gla_proj': 4.882545e-02, 'w_out': 5.670662e-02, 'final_norm_g': 3.208379e+01}


def _to_microbatches(a, axis):
    t = _jnp.moveaxis(a, axis, 0)
    t = t.reshape((N_MICROBATCH, t.shape[0] // N_MICROBATCH) + t.shape[1:])
    return _jnp.moveaxis(t, 1, axis + 1)


def setup_inputs(seed: int = 0) -> dict:
    inp = _fwd_setup_inputs(seed)
    key = _jax.random.fold_in(_jax.random.key(seed), 7919)
    shape, _ = _output_shape()
    out = dict(inp)
    out["loss_target"] = _jax.random.normal(_jax.random.fold_in(key, 0), shape, _jnp.float32)
    for i, name in enumerate(TWIN_WEIGHTS):
        w = inp[name].astype(_jnp.float32)
        if MOMENT_SCALE is None:
            s = _jnp.sqrt(_jnp.mean(_jnp.square(w)) + 1e-30)
        else:
            s = MOMENT_SCALE[name]
        km, kv = _jax.random.split(_jax.random.fold_in(key, i + 1))
        out[name] = w
        out["m_" + name] = s * _jax.random.normal(km, w.shape, _jnp.float32)
        out["v_" + name] = (s * s) * _jax.random.uniform(kv, w.shape, _jnp.float32, 0.5, 1.5)
    if N_MICROBATCH > 1:
        for name, axis in PER_EXAMPLE_BATCH_AXIS.items():
            out[name] = _to_microbatches(out[name], axis)
    return {'x': out['x'], 'c': out['c'], 'ctx': out['ctx'], 'c_ctx': out['c_ctx'], 'ada_w': out['ada_w'], 'ada_b': out['ada_b'], 'norm_g': out['norm_g'], 'w_in': out['w_in'], 'b_in': out['b_in'], 'conv_w': out['conv_w'], 'conv_b': out['conv_b'], 'conv_ln_g': out['conv_ln_g'], 'conv_ln_b': out['conv_ln_b'], 'conv_proj': out['conv_proj'], 'decay_up_fwd': out['decay_up_fwd'], 'decay_bias_fwd': out['decay_bias_fwd'], 'decay_up_bwd': out['decay_up_bwd'], 'decay_bias_bwd': out['decay_bias_bwd'], 'gla_norm_g': out['gla_norm_g'], 'gla_proj': out['gla_proj'], 'w_out': out['w_out'], 'final_norm_g': out['final_norm_g'], 'loss_target': out['loss_target'], 'm_c_ctx': out['m_c_ctx'], 'm_ada_w': out['m_ada_w'], 'm_ada_b': out['m_ada_b'], 'm_norm_g': out['m_norm_g'], 'm_w_in': out['m_w_in'], 'm_b_in': out['m_b_in'], 'm_conv_w': out['m_conv_w'], 'm_conv_b': out['m_conv_b'], 'm_conv_ln_g': out['m_conv_ln_g'], 'm_conv_ln_b': out['m_conv_ln_b'], 'm_conv_proj': out['m_conv_proj'], 'm_decay_up_fwd': out['m_decay_up_fwd'], 'm_decay_bias_fwd': out['m_decay_bias_fwd'], 'm_decay_up_bwd': out['m_decay_up_bwd'], 'm_decay_bias_bwd': out['m_decay_bias_bwd'], 'm_gla_norm_g': out['m_gla_norm_g'], 'm_gla_proj': out['m_gla_proj'], 'm_w_out': out['m_w_out'], 'm_final_norm_g': out['m_final_norm_g'], 'v_c_ctx': out['v_c_ctx'], 'v_ada_w': out['v_ada_w'], 'v_ada_b': out['v_ada_b'], 'v_norm_g': out['v_norm_g'], 'v_w_in': out['v_w_in'], 'v_b_in': out['v_b_in'], 'v_conv_w': out['v_conv_w'], 'v_conv_b': out['v_conv_b'], 'v_conv_ln_g': out['v_conv_ln_g'], 'v_conv_ln_b': out['v_conv_ln_b'], 'v_conv_proj': out['v_conv_proj'], 'v_decay_up_fwd': out['v_decay_up_fwd'], 'v_decay_bias_fwd': out['v_decay_bias_fwd'], 'v_decay_up_bwd': out['v_decay_up_bwd'], 'v_decay_bias_bwd': out['v_decay_bias_bwd'], 'v_gla_norm_g': out['v_gla_norm_g'], 'v_gla_proj': out['v_gla_proj'], 'v_w_out': out['v_w_out'], 'v_final_norm_g': out['v_final_norm_g']}


def _loss(weights, diff, rest, loss_target):
    with _jax.named_scope("forward"):
        args = {**rest, TWIN_DIFF_INPUT: diff, **{k: w.astype(_WEIGHT_DTYPES[k]) for k, w in weights.items()}}
        y = _forward(args)
    with _jax.named_scope("loss_head"):
        err = _jnp.square(y.astype(_jnp.float32) - loss_target)
        return 0.5 * _jnp.sum(_jnp.mean(err, axis=-1)) if err.ndim else 0.5 * err


def _adamw(w, g, m, v):
    m = ADAM_B1 * m + (1.0 - ADAM_B1) * g
    v = ADAM_B2 * v + (1.0 - ADAM_B2) * _jnp.square(g)
    m_hat = m / (1.0 - ADAM_B1 ** ADAM_STEP)
    v_hat = v / (1.0 - ADAM_B2 ** ADAM_STEP)
    delta = -ADAM_LR * (m_hat / (_jnp.sqrt(v_hat) + ADAM_EPS) + ADAM_WD * w)
    return delta, m, v


def reference(x, c, ctx, c_ctx, ada_w, ada_b, norm_g, w_in, b_in, conv_w, conv_b, conv_ln_g, conv_ln_b, conv_proj, decay_up_fwd, decay_bias_fwd, decay_up_bwd, decay_bias_bwd, gla_norm_g, gla_proj, w_out, final_norm_g, loss_target, m_c_ctx, m_ada_w, m_ada_b, m_norm_g, m_w_in, m_b_in, m_conv_w, m_conv_b, m_conv_ln_g, m_conv_ln_b, m_conv_proj, m_decay_up_fwd, m_decay_bias_fwd, m_decay_up_bwd, m_decay_bias_bwd, m_gla_norm_g, m_gla_proj, m_w_out, m_final_norm_g, v_c_ctx, v_ada_w, v_ada_b, v_norm_g, v_w_in, v_b_in, v_conv_w, v_conv_b, v_conv_ln_g, v_conv_ln_b, v_conv_proj, v_decay_up_fwd, v_decay_bias_fwd, v_decay_up_bwd, v_decay_bias_bwd, v_gla_norm_g, v_gla_proj, v_w_out, v_final_norm_g):
    given = dict(x=x, c=c, ctx=ctx, c_ctx=c_ctx, ada_w=ada_w, ada_b=ada_b, norm_g=norm_g, w_in=w_in, b_in=b_in, conv_w=conv_w, conv_b=conv_b, conv_ln_g=conv_ln_g, conv_ln_b=conv_ln_b, conv_proj=conv_proj, decay_up_fwd=decay_up_fwd, decay_bias_fwd=decay_bias_fwd, decay_up_bwd=decay_up_bwd, decay_bias_bwd=decay_bias_bwd, gla_norm_g=gla_norm_g, gla_proj=gla_proj, w_out=w_out, final_norm_g=final_norm_g, loss_target=loss_target, m_c_ctx=m_c_ctx, m_ada_w=m_ada_w, m_ada_b=m_ada_b, m_norm_g=m_norm_g, m_w_in=m_w_in, m_b_in=m_b_in, m_conv_w=m_conv_w, m_conv_b=m_conv_b, m_conv_ln_g=m_conv_ln_g, m_conv_ln_b=m_conv_ln_b, m_conv_proj=m_conv_proj, m_decay_up_fwd=m_decay_up_fwd, m_decay_bias_fwd=m_decay_bias_fwd, m_decay_up_bwd=m_decay_up_bwd, m_decay_bias_bwd=m_decay_bias_bwd, m_gla_norm_g=m_gla_norm_g, m_gla_proj=m_gla_proj, m_w_out=m_w_out, m_final_norm_g=m_final_norm_g, v_c_ctx=v_c_ctx, v_ada_w=v_ada_w, v_ada_b=v_ada_b, v_norm_g=v_norm_g, v_w_in=v_w_in, v_b_in=v_b_in, v_conv_w=v_conv_w, v_conv_b=v_conv_b, v_conv_ln_g=v_conv_ln_g, v_conv_ln_b=v_conv_ln_b, v_conv_proj=v_conv_proj, v_decay_up_fwd=v_decay_up_fwd, v_decay_bias_fwd=v_decay_bias_fwd, v_decay_up_bwd=v_decay_up_bwd, v_decay_bias_bwd=v_decay_bias_bwd, v_gla_norm_g=v_gla_norm_g, v_gla_proj=v_gla_proj, v_w_out=v_w_out, v_final_norm_g=v_final_norm_g)
    weights = {n: given[n] for n in TWIN_WEIGHTS}
    shared = {n: given[n] for n in SHARED_INPUTS}
    per_example = {n: given[n] for n in ['x', 'c', 'ctx']}
    grad_fn = _jax.value_and_grad(_loss, argnums=(0, 1))

    def one_microbatch(ex, loss_target):
        ex = dict(ex)
        diff = ex.pop(TWIN_DIFF_INPUT)
        return grad_fn(weights, diff, {**shared, **ex}, loss_target)

    if N_MICROBATCH == 1:
        loss, (grad_w, grad_x) = one_microbatch(per_example, given["loss_target"])
    else:
        def body(carry, xs):
            loss_sum, grad_sum = carry
            l_k, (gw_k, gx_k) = one_microbatch(xs[0], xs[1])
            with _jax.named_scope("update"):
                return (loss_sum + l_k, _jax.tree.map(_jnp.add, grad_sum, gw_k)), gx_k

        init = (_jnp.zeros((), _jnp.float32), _jax.tree.map(_jnp.zeros_like, weights))
        (loss, grad_w), grad_x = _jax.lax.scan(body, init, (per_example, given["loss_target"]))
    with _jax.named_scope("update"):
        delta_w, new_m, new_v = {}, {}, {}
        for n in TWIN_WEIGHTS:
            delta_w[n], new_m[n], new_v[n] = _adamw(weights[n], grad_w[n], given["m_" + n], given["v_" + n])
    return (loss, grad_x, *[grad_w[n] for n in TWIN_WEIGHTS], *[delta_w[n] for n in TWIN_WEIGHTS],
            *[new_m[n] for n in TWIN_WEIGHTS], *[new_v[n] for n in TWIN_WEIGHTS])
```

```python
import jax
import jax.numpy as jnp
from jax import lax
from jax.experimental import pallas as pl
from jax.experimental.pallas import tpu as pltpu

F32 = jnp.float32
BF16 = jnp.bfloat16
MESH = pl.DeviceIdType.MESH
HI = lax.Precision.HIGHEST

D = 1024
SEQ = 2048
GRID_W = 64
GRID_H = SEQ // GRID_W
NCTX = 256
SEQ_ALL = SEQ + NCTX
EPS = 1e-6
CONV_K = 31
CONV_PAD = CONV_K // 2
HEADS = 4
HEAD_K = 128
HEAD_V = 256
GLA_DK = HEADS * HEAD_K
GATE_TAU = 16.0
Q_SCALE = HEAD_K ** -0.5
CHUNK = 64
NCHUNK = SEQ_ALL // CHUNK
NCHUNK_LAT = SEQ // CHUNK
NCHUNK_CTX = NCHUNK - NCHUNK_LAT
SUB = 16
NSUB = CHUNK // SUB
N_IN = 8224
W3 = 2176
O3_V, O3_Q, O3_K, O3_AB = 0, 1024, 1536, 2048

ADAM_LR, ADAM_B1, ADAM_B2, ADAM_EPS, ADAM_WD, ADAM_STEP = 0.001, 0.9, 0.999, 1e-08, 0.01, 10
VMEM_LIMIT = 56 * 1024 * 1024

N_CHIPS = 4
N_DEV = 8
W_IN_SHARD = N_IN // N_CHIPS
W_USED = W_IN_SHARD + 3 * 256
W_ROWS = 2848
W_HALF = W_ROWS // 2
SMALL_ROWS = 72
MOD_ROWS = 72


def _pallas(body, **kw):
    return pl.pallas_call(body, **kw)


def _params(sem=None, **kw):
    if sem is not None:
        kw["dimension_semantics"] = sem
    return pltpu.CompilerParams(vmem_limit_bytes=VMEM_LIMIT, **kw)


def _sigmoid(v):
    return 1.0 / (1.0 + jnp.exp(-v))


def _silu(v):
    return v * _sigmoid(v)


def _dsilu(v):
    s = _sigmoid(v)
    return s * (1.0 + v * (1.0 - s))


def _log_sigmoid(v):
    return jnp.minimum(v, 0.0) - jnp.log(1.0 + jnp.exp(-jnp.abs(v)))


def _dot(a, b, dims, precision=None):
    return lax.dot_general(a, b, (dims, ((), ())), preferred_element_type=F32, precision=precision)


def _nn(a, b, precision=None):
    return _dot(a, b, ((1,), (0,)), precision)


def _nt(a, b, precision=None):
    return _dot(a, b, ((1,), (1,)), precision)


def _tn(a, b, precision=None):
    return _dot(a, b, ((0,), (0,)), precision)


def _b16(v):
    return v.astype(BF16)


def matmul_nn(a, b, bias, *, name, m, tm, tn, out_dtype):
    k = a.shape[1]
    n = b.shape[1]
    has_bias = bias is not None

    def body(*refs):
        if has_bias:
            a_ref, b_ref, bias_ref, o_ref = refs
            acc = _nn(a_ref[...], b_ref[...]) + bias_ref[...]
        else:
            a_ref, b_ref, o_ref = refs
            acc = _nn(a_ref[...], b_ref[...])
        o_ref[...] = acc.astype(o_ref.dtype)

    in_specs = [pl.BlockSpec((tm, k), lambda j, i: (i, 0)), pl.BlockSpec((k, tn), lambda j, i: (0, j))]
    args = [a, b]
    if has_bias:
        in_specs.append(pl.BlockSpec((1, tn), lambda j, i: (0, j)))
        args.append(bias)
    return _pallas(
        body, name=name, grid=(n // tn, m // tm), in_specs=in_specs,
        out_specs=pl.BlockSpec((tm, tn), lambda j, i: (i, j)),
        out_shape=jax.ShapeDtypeStruct((m, n), out_dtype),
        compiler_params=_params(("parallel", "parallel")),
    )(*args)


def matmul_nt(a, b, *, name, tm):
    m, k = a.shape
    n = b.shape[0]

    def body(a_ref, b_ref, o_ref):
        o_ref[...] = _nt(a_ref[...], b_ref[...])

    return _pallas(
        body, name=name, grid=(m // tm,),
        in_specs=[pl.BlockSpec((tm, k), lambda i: (i, 0)), pl.BlockSpec((n, k), lambda i: (0, 0))],
        out_specs=pl.BlockSpec((tm, n), lambda i: (i, 0)),
        out_shape=jax.ShapeDtypeStruct((m, n), F32),
        compiler_params=_params(("parallel",)),
    )(a, b)


def matmul_tn(a, b, *, name, t, tn, tt, colsum=False):
    m = a.shape[1]
    n = b.shape[1]

    def body(a_ref, b_ref, o_ref, *rest):
        @pl.when(pl.program_id(1) == 0)
        def _():
            o_ref[...] = jnp.zeros_like(o_ref)
            if colsum:
                rest[0][...] = jnp.zeros_like(rest[0])
        o_ref[...] += _tn(a_ref[...], b_ref[...])
        if colsum:
            rest[0][...] += jnp.sum(b_ref[...].astype(F32), axis=0, keepdims=True)

    out_specs = [pl.BlockSpec((m, tn), lambda j, s: (0, j))]
    out_shape = [jax.ShapeDtypeStruct((m, n), F32)]
    if colsum:
        out_specs.append(pl.BlockSpec((1, tn), lambda j, s: (0, j)))
        out_shape.append(jax.ShapeDtypeStruct((1, n), F32))
    return _pallas(
        body, name=name, grid=(n // tn, t // tt),
        in_specs=[pl.BlockSpec((tt, m), lambda j, s: (s, 0)), pl.BlockSpec((tt, tn), lambda j, s: (s, j))],
        out_specs=out_specs, out_shape=out_shape,
        compiler_params=_params(("parallel", "arbitrary")),
    )(a, b)


def dgrad_multi(dps, wts, *, t_all, t_lat, tm):
    steps = []
    tks = []
    for g, dp in enumerate(dps):
        width = dp.shape[1]
        tk = width if width % 1024 else 1024
        tks.append(tk)
        steps += [(g, s) for s in range(width // tk)]
    n_steps = len(steps)
    lo = [min(i for i, (g, _) in enumerate(steps) if g == gg) for gg in range(len(dps))]
    cnt = [sum(1 for (g, _) in steps if g == gg) for gg in range(len(dps))]
    n_lat = t_lat // tm
    n_g = len(dps)

    def body(*refs):
        dp_refs, w_refs, o_ref = refs[:n_g], refs[n_g:2 * n_g], refs[2 * n_g]
        i = pl.program_id(0)
        s = pl.program_id(1)

        @pl.when(s == 0)
        def _():
            o_ref[...] = jnp.zeros_like(o_ref)

        for g in range(n_g):
            in_rows = dps[g].shape[0] == t_all
            cond = (s >= lo[g]) & (s < lo[g] + cnt[g])
            if not in_rows:
                cond = cond & (i < n_lat)

            @pl.when(cond)
            def _(g=g):
                o_ref[...] += _nt(dp_refs[g][...], w_refs[g][...])

    in_specs = []
    for g, dp in enumerate(dps):
        nrow = dp.shape[0] // tm
        in_specs.append(pl.BlockSpec(
            (tm, tks[g]), lambda i, s, g=g, nrow=nrow: (jnp.minimum(i, nrow - 1), jnp.clip(s - lo[g], 0, cnt[g] - 1))))
    for g, w in enumerate(wts):
        in_specs.append(pl.BlockSpec((D, tks[g]), lambda i, s, g=g: (0, jnp.clip(s - lo[g], 0, cnt[g] - 1))))
    return _pallas(
        body, name="dgrad_w_in", grid=(t_all // tm, n_steps), in_specs=in_specs,
        out_specs=pl.BlockSpec((tm, D), lambda i, s: (i, 0)),
        out_shape=jax.ShapeDtypeStruct((t_all, D), F32),
        compiler_params=_params(("parallel", "arbitrary")),
    )(*dps, *wts)


TM_NORM = 512


def norm_mod_fwd(x2, ctx2, scale1, shift, norm_g):
    t = x2.shape[0]
    n_lat = t // TM_NORM
    assert ctx2.shape[0] == TM_NORM
    n_samples = scale1.shape[0] - 1
    tps = n_lat // n_samples

    def body(x_ref, c_ref, sc_ref, sh_ref, g_ref, u_ref):
        i = pl.program_id(0)
        xv = jnp.where(i < n_lat, x_ref[...], c_ref[...])
        rs = lax.rsqrt(jnp.mean(xv * xv, axis=-1, keepdims=True) + EPS)
        u = xv * rs * g_ref[...] * sc_ref[0] + sh_ref[0]
        u_ref[...] = u.astype(u_ref.dtype)

    grp = lambda i: (jnp.minimum(i // tps, n_samples), 0, 0)
    return _pallas(
        body, name="norm_mod_fwd", grid=(n_lat + 1,),
        in_specs=[pl.BlockSpec((TM_NORM, D), lambda i: (jnp.minimum(i, n_lat - 1), 0)),
                  pl.BlockSpec((TM_NORM, D), lambda i: (0, 0)),
                  pl.BlockSpec((1, 1, D), grp), pl.BlockSpec((1, 1, D), grp),
                  pl.BlockSpec((1, D), lambda i: (0, 0))],
        out_specs=pl.BlockSpec((TM_NORM, D), lambda i: (i, 0)),
        out_shape=jax.ShapeDtypeStruct((t + TM_NORM, D), BF16),
        compiler_params=_params(("parallel",)),
    )(x2, ctx2, scale1, shift, norm_g)


def norm_mod_bwd(x2, ctx2, du, dh, scale1, norm_g):
    t = x2.shape[0]
    n_lat = t // TM_NORM
    n_samples = scale1.shape[0] - 1
    tps = n_lat // n_samples
    n_grp = n_samples + 1

    def body(x_ref, c_ref, du_ref, dh_ref, sc_ref, g_ref, dx_ref, dsh_ref, dsc_ref, dg_ref):
        i = pl.program_id(0)
        xv = jnp.where(i < n_lat, x_ref[...], c_ref[...])
        rs = lax.rsqrt(jnp.mean(xv * xv, axis=-1, keepdims=True) + EPS)
        xh = xv * rs
        duv = du_ref[...]
        n = xh * g_ref[...]
        dn = duv * sc_ref[0]
        dxh = dn * g_ref[...]
        dx = rs * (dxh - xh * jnp.mean(dxh * xh, axis=-1, keepdims=True))
        @pl.when(i < n_lat)
        def _():
            dx_ref[...] = dx + dh_ref[...]

        @pl.when(i % tps == 0)
        def _():
            dsh_ref[...] = jnp.zeros_like(dsh_ref)
            dsc_ref[...] = jnp.zeros_like(dsc_ref)

        @pl.when(i == 0)
        def _():
            dg_ref[...] = jnp.zeros_like(dg_ref)

        dsh_ref[0] += jnp.sum(duv, axis=0, keepdims=True)
        dsc_ref[0] += jnp.sum(duv * n, axis=0, keepdims=True)
        dg_ref[...] += jnp.sum(dn * xh, axis=0, keepdims=True)

    grp = lambda i: (jnp.minimum(i // tps, n_samples), 0, 0)
    lat = lambda i: (jnp.minimum(i, n_lat - 1), 0)
    return _pallas(
        body, name="norm_mod_bwd", grid=(n_lat + 1,),
        in_specs=[pl.BlockSpec((TM_NORM, D), lat),
                  pl.BlockSpec((TM_NORM, D), lambda i: (0, 0)),
                  pl.BlockSpec((TM_NORM, D), lambda i: (i, 0)),
                  pl.BlockSpec((TM_NORM, D), lat),
                  pl.BlockSpec((1, 1, D), grp),
                  pl.BlockSpec((1, D), lambda i: (0, 0))],
        out_specs=[pl.BlockSpec((TM_NORM, D), lat),
                   pl.BlockSpec((1, 1, D), grp), pl.BlockSpec((1, 1, D), grp),
                   pl.BlockSpec((1, D), lambda i: (0, 0))],
        out_shape=[jax.ShapeDtypeStruct((t, D), F32),
                   jax.ShapeDtypeStruct((n_grp, 1, D), F32), jax.ShapeDtypeStruct((n_grp, 1, D), F32),
                   jax.ShapeDtypeStruct((1, D), F32)],
        compiler_params=_params(("arbitrary",)),
    )(x2, ctx2, du, dh, scale1, norm_g)


CONV_CB = 256
CONV_NCB = D // CONV_CB
H_OFF = 16


def _conv_pad_shape(vertical):
    if vertical:
        return (GRID_H + 2 * CONV_PAD, GRID_W, CONV_CB)
    return (GRID_H, GRID_W + 2 * H_OFF, CONV_CB)


def _conv_store(pad_ref, img, vertical):
    if vertical:
        pad_ref[pl.ds(CONV_PAD, GRID_H)] = img
    else:
        pad_ref[:, pl.ds(H_OFF, GRID_W), :] = img


def _conv_window(pad_ref, k, vertical, r):
    if vertical:
        return pad_ref[r + k]
    return pad_ref[r, pl.ds(H_OFF - CONV_PAD + k, GRID_W), :]


def _rows(r):
    return pl.ds(pl.multiple_of(r * GRID_W, GRID_W), GRID_W)


def conv_fwd(p1, conv_w, conv_b, n_samples):
    t = n_samples * SEQ

    def make(vertical, prev):
        def body(gv_ref, gg_ref, w_ref, b_ref, *rest):
            o_ref, pad_ref = rest[-2], rest[-1]
            pad_ref[...] = jnp.zeros_like(pad_ref)
            a = gv_ref[...] * _sigmoid(gg_ref[...])
            _conv_store(pad_ref, a.reshape(GRID_H, GRID_W, CONV_CB), vertical)

            def row(r, carry):
                acc = jnp.zeros((GRID_W, CONV_CB), F32) + b_ref[...]
                for k in range(CONV_K):
                    acc = acc + _conv_window(pad_ref, k, vertical, r) * w_ref[pl.ds(k, 1), :]
                o_ref[_rows(r), :] = acc
                return carry

            lax.fori_loop(0, GRID_H, row, 0)

        cb0 = CONV_NCB // 2 if vertical else 0
        in_specs = [pl.BlockSpec((SEQ, CONV_CB), lambda b, j: (b, 2 * (cb0 + j))),
                    pl.BlockSpec((SEQ, CONV_CB), lambda b, j: (b, 2 * (cb0 + j) + 1)),
                    pl.BlockSpec((CONV_K + 1, CONV_CB), lambda b, j: (0, cb0 + j)),
                    pl.BlockSpec((1, CONV_CB), lambda b, j: (0, cb0 + j))]
        args = [p1, p1, conv_w, conv_b]
        aliases = {}
        if prev is not None:
            in_specs.append(pl.BlockSpec(memory_space=pl.ANY))
            args.append(prev)
            aliases = {4: 0}
        return _pallas(
            body, name="conv_fwd_v" if vertical else "conv_fwd_h", grid=(n_samples, CONV_NCB // 2),
            in_specs=in_specs,
            out_specs=pl.BlockSpec((SEQ, CONV_CB), lambda b, j: (b, cb0 + j)),
            out_shape=jax.ShapeDtypeStruct((t, D), F32),
            scratch_shapes=[pltpu.VMEM(_conv_pad_shape(vertical), F32)],
            input_output_aliases=aliases,
            compiler_params=_params(("parallel", "parallel")),
        )(*args)

    return make(True, make(False, None))


def conv_bwd(p1, daconv, conv_w, n_samples):
    t = n_samples * SEQ

    def make(vertical, prev):
        def body(gv_ref, gg_ref, dy_ref, w_ref, *rest):
            dp_ref, dw_ref, db_ref, pad_ref, dpad_ref, da_ref = rest[-6:]
            pad_ref[...] = jnp.zeros_like(pad_ref)
            dpad_ref[...] = jnp.zeros_like(dpad_ref)
            _conv_store(pad_ref, (gv_ref[...] * _sigmoid(gg_ref[...])).reshape(GRID_H, GRID_W, CONV_CB), vertical)
            _conv_store(dpad_ref, dy_ref[...].reshape(GRID_H, GRID_W, CONV_CB), vertical)

            def row(r, carry):
                acc = jnp.zeros((GRID_W, CONV_CB), F32)
                for k in range(CONV_K):
                    acc = acc + _conv_window(dpad_ref, CONV_K - 1 - k, vertical, r) * w_ref[pl.ds(k, 1), :]
                da_ref[_rows(r), :] = acc
                return carry

            lax.fori_loop(0, GRID_H, row, 0)
            da = da_ref[...]
            gv = gv_ref[...]
            sg = _sigmoid(gg_ref[...])
            dp_ref[:, pl.ds(0, CONV_CB)] = (da * sg).astype(dp_ref.dtype)
            dp_ref[:, pl.ds(CONV_CB, CONV_CB)] = (da * gv * sg * (1.0 - sg)).astype(dp_ref.dtype)

            for k in range(CONV_K):
                def wrow(r, acc, k=k):
                    return acc + _conv_window(pad_ref, k, vertical, r) * dy_ref[_rows(r), :]
                acc = lax.fori_loop(0, GRID_H, wrow, jnp.zeros((GRID_W, CONV_CB), F32))
                dw_ref[0, pl.ds(k, 1), :] = jnp.sum(acc, axis=0, keepdims=True)
            dw_ref[0, pl.ds(CONV_K, 1), :] = jnp.zeros((1, CONV_CB), F32)
            db_ref[0] = jnp.sum(dy_ref[...], axis=0, keepdims=True)

        cb0 = CONV_NCB // 2 if vertical else 0
        in_specs = [pl.BlockSpec((SEQ, CONV_CB), lambda b, j: (b, 2 * (cb0 + j))),
                    pl.BlockSpec((SEQ, CONV_CB), lambda b, j: (b, 2 * (cb0 + j) + 1)),
                    pl.BlockSpec((SEQ, CONV_CB), lambda b, j: (b, cb0 + j)),
                    pl.BlockSpec((CONV_K + 1, CONV_CB), lambda b, j: (0, cb0 + j))]
        args = [p1, p1, daconv, conv_w]
        aliases = {}
        if prev is not None:
            in_specs += [pl.BlockSpec(memory_space=pl.ANY)] * 3
            args += list(prev)
            aliases = {4: 0, 5: 1, 6: 2}
        return _pallas(
            body, name="conv_bwd_v" if vertical else "conv_bwd_h", grid=(n_samples, CONV_NCB // 2),
            in_specs=in_specs,
            out_specs=[pl.BlockSpec((SEQ, 2 * CONV_CB), lambda b, j: (b, cb0 + j)),
                       pl.BlockSpec((1, CONV_K + 1, CONV_CB), lambda b, j: (b, 0, cb0 + j)),
                       pl.BlockSpec((1, 1, CONV_CB), lambda b, j: (b, 0, cb0 + j))],
            out_shape=[jax.ShapeDtypeStruct((t, 2 * D), BF16),
                       jax.ShapeDtypeStruct((n_samples, CONV_K + 1, D), F32),
                       jax.ShapeDtypeStruct((n_samples, 1, D), F32)],
            scratch_shapes=[pltpu.VMEM(_conv_pad_shape(vertical), F32), pltpu.VMEM(_conv_pad_shape(vertical), F32),
                            pltpu.VMEM((SEQ, CONV_CB), F32)],
            input_output_aliases=aliases,
            compiler_params=_params(("parallel", "parallel")),
        )(*args)

    return make(True, make(False, None))


TM_EW = 256


def ln_gate_fwd(aconv, z, ln_g, ln_b):
    t = aconv.shape[0]

    def body(a_ref, z_ref, g_ref, b_ref, o_ref):
        a = a_ref[...]
        mu = jnp.mean(a, axis=-1, keepdims=True)
        xc = a - mu
        rstd = lax.rsqrt(jnp.mean(xc * xc, axis=-1, keepdims=True) + EPS)
        l = xc * rstd * g_ref[...] + b_ref[...]
        o_ref[...] = (_silu(l) * _silu(z_ref[...])).astype(o_ref.dtype)

    row = pl.BlockSpec((TM_EW, D), lambda i: (i, 0))
    vec = pl.BlockSpec((1, D), lambda i: (0, 0))
    return _pallas(
        body, name="ln_gate_fwd", grid=(t // TM_EW,), in_specs=[row, row, vec, vec], out_specs=row,
        out_shape=jax.ShapeDtypeStruct((t, D), BF16), compiler_params=_params(("parallel",)),
    )(aconv, z, ln_g, ln_b)


def ln_gate_bwd(aconv, z, dac, ln_g, ln_b):
    t = aconv.shape[0]

    def body(a_ref, z_ref, d_ref, g_ref, b_ref, da_ref, dz_ref, dg_ref, db_ref):
        a = a_ref[...]
        zv = z_ref[...]
        dac_v = d_ref[...]
        mu = jnp.mean(a, axis=-1, keepdims=True)
        xc = a - mu
        rstd = lax.rsqrt(jnp.mean(xc * xc, axis=-1, keepdims=True) + EPS)
        xh = xc * rstd
        l = xh * g_ref[...] + b_ref[...]
        dz_ref[...] = (dac_v * _silu(l) * _dsilu(zv)).astype(dz_ref.dtype)
        dl = dac_v * _silu(zv) * _dsilu(l)
        dxh = dl * g_ref[...]
        da_ref[...] = rstd * (dxh - jnp.mean(dxh, axis=-1, keepdims=True)
                              - xh * jnp.mean(dxh * xh, axis=-1, keepdims=True))

        @pl.when(pl.program_id(0) == 0)
        def _():
            dg_ref[...] = jnp.zeros_like(dg_ref)
            db_ref[...] = jnp.zeros_like(db_ref)

        dg_ref[...] += jnp.sum(dl * xh, axis=0, keepdims=True)
        db_ref[...] += jnp.sum(dl, axis=0, keepdims=True)

    row = pl.BlockSpec((TM_EW, D), lambda i: (i, 0))
    vec = pl.BlockSpec((1, D), lambda i: (0, 0))
    return _pallas(
        body, name="ln_gate_bwd", grid=(t // TM_EW,), in_specs=[row, row, row, vec, vec],
        out_specs=[row, row, vec, vec],
        out_shape=[jax.ShapeDtypeStruct((t, D), F32), jax.ShapeDtypeStruct((t, D), BF16),
                   jax.ShapeDtypeStruct((1, D), F32), jax.ShapeDtypeStruct((1, D), F32)],
        compiler_params=_params(("arbitrary",)),
    )(aconv, z, dac, ln_g, ln_b)


TM_PREP = 256
PREP_LAT = SEQ // TM_PREP
PREP_ALL = SEQ_ALL // TM_PREP


def _chunk_tri(n, upper):
    r = lax.broadcasted_iota(jnp.int32, (n, n), 0)
    c = lax.broadcasted_iota(jnp.int32, (n, n), 1)
    same = (r // CHUNK) == (c // CHUNK)
    keep = (c >= r) if upper else (c <= r)
    return jnp.where(same & keep, 1.0, 0.0).astype(F32)


def _prep_tile_maps(n_samples):
    n_lat = n_samples * PREP_LAT

    def seq_map(i):
        return jnp.where(i < n_lat, i // PREP_LAT, i - n_lat), jnp.where(i < n_lat, i % PREP_LAT, PREP_LAT)

    return n_lat, seq_map


def gla_prep_fwd(p3, upf, upb, bias_f, bias_b, n_samples):
    n_lat, seq_map = _prep_tile_maps(n_samples)
    n_tiles = n_lat + n_samples

    def body(v_ref, q_ref, k_ref, ab_ref, upf_ref, upb_ref, bf_ref, bb_ref, qo, ko, vo, cf, cb):
        i = pl.program_id(0)
        qo[0] = jnp.where(i < n_lat, q_ref[...] * Q_SCALE, 0.0)
        ko[0] = k_ref[...]
        vo[0] = v_ref[...]
        ab = ab_ref[...]
        gf = _log_sigmoid(_nn(ab, upf_ref[...], HI) + bf_ref[...]) * (1.0 / GATE_TAU)
        gb = _log_sigmoid(_nn(ab, upb_ref[...], HI) + bb_ref[...]) * (1.0 / GATE_TAU)
        cf[0] = _nn(_chunk_tri(TM_PREP, False), gf, HI)
        cb[0] = _nn(_chunk_tri(TM_PREP, True), gb, HI)

    def o_spec(w):
        return pl.BlockSpec((1, TM_PREP, w), lambda i: (*seq_map(i), 0))

    full = lambda shape: pl.BlockSpec(shape, lambda i: (0,) * len(shape))
    return _pallas(
        body, name="gla_prep_fwd", grid=(n_tiles,),
        in_specs=[pl.BlockSpec((TM_PREP, 1024), lambda i: (i, O3_V // 1024)),
                  pl.BlockSpec((TM_PREP, 512), lambda i: (i, O3_Q // 512)),
                  pl.BlockSpec((TM_PREP, 512), lambda i: (i, O3_K // 512)),
                  pl.BlockSpec((TM_PREP, 128), lambda i: (i, O3_AB // 128)),
                  full((128, GLA_DK)), full((128, GLA_DK)), full((1, GLA_DK)), full((1, GLA_DK))],
        out_specs=[o_spec(GLA_DK), o_spec(GLA_DK), o_spec(D), o_spec(GLA_DK), o_spec(GLA_DK)],
        out_shape=[jax.ShapeDtypeStruct((n_samples, SEQ_ALL, GLA_DK), F32),
                   jax.ShapeDtypeStruct((n_samples, SEQ_ALL, GLA_DK), F32),
                   jax.ShapeDtypeStruct((n_samples, SEQ_ALL, D), F32),
                   jax.ShapeDtypeStruct((n_samples, SEQ_ALL, GLA_DK), F32),
                   jax.ShapeDtypeStruct((n_samples, SEQ_ALL, GLA_DK), F32)],
        compiler_params=_params(("parallel",)),
    )(p3, p3, p3, p3, upf, upb, bias_f, bias_b)


def gla_prep_bwd(p3, dq_f, dq_b, dk_f, dk_b, dv_f, dv_b, dc_f, dc_b, upf, upb, bias_f, bias_b, n_samples):
    n_lat, seq_map = _prep_tile_maps(n_samples)
    n_tiles = n_lat + n_samples

    def body(ab_ref, dqf, dqb, dkf, dkb, dvf, dvb, dcf, dcb, upf_ref, upb_ref, bf_ref, bb_ref,
             dp_ref, duf_ref, dub_ref, dbf_ref, dbb_ref):
        i = pl.program_id(0)
        dp_ref[:, pl.ds(O3_V, D)] = (dvf[0] + dvb[0]).astype(dp_ref.dtype)
        dq = jnp.where(i < n_lat, (dqf[0] + dqb[0]) * Q_SCALE, 0.0)
        dp_ref[:, pl.ds(O3_Q, GLA_DK)] = dq.astype(dp_ref.dtype)
        dp_ref[:, pl.ds(O3_K, GLA_DK)] = (dkf[0] + dkb[0]).astype(dp_ref.dtype)
        ab = ab_ref[...]
        zf = _nn(ab, upf_ref[...], HI) + bf_ref[...]
        zb = _nn(ab, upb_ref[...], HI) + bb_ref[...]
        dgf = _nn(_chunk_tri(TM_PREP, True), dcf[0], HI)
        dgb = _nn(_chunk_tri(TM_PREP, False), dcb[0], HI)
        dzf = dgf * (1.0 / GATE_TAU) * _sigmoid(-zf)
        dzb = dgb * (1.0 / GATE_TAU) * _sigmoid(-zb)
        dab = _nt(dzf, upf_ref[...], HI) + _nt(dzb, upb_ref[...], HI)
        dp_ref[:, pl.ds(O3_AB, 128)] = dab.astype(dp_ref.dtype)

        @pl.when(i == 0)
        def _():
            duf_ref[...] = jnp.zeros_like(duf_ref)
            dub_ref[...] = jnp.zeros_like(dub_ref)
            dbf_ref[...] = jnp.zeros_like(dbf_ref)
            dbb_ref[...] = jnp.zeros_like(dbb_ref)

        duf_ref[...] += _tn(ab, dzf, HI)
        dub_ref[...] += _tn(ab, dzb, HI)
        dbf_ref[...] += jnp.sum(dzf, axis=0, keepdims=True)
        dbb_ref[...] += jnp.sum(dzb, axis=0, keepdims=True)

    def s_spec(w):
        return pl.BlockSpec((1, TM_PREP, w), lambda i: (*seq_map(i), 0))

    full = lambda shape: pl.BlockSpec(shape, lambda i: (0,) * len(shape))
    return _pallas(
        body, name="gla_prep_bwd", grid=(n_tiles,),
        in_specs=[pl.BlockSpec((TM_PREP, 128), lambda i: (i, O3_AB // 128)),
                  s_spec(GLA_DK), s_spec(GLA_DK), s_spec(GLA_DK), s_spec(GLA_DK), s_spec(D), s_spec(D),
                  s_spec(GLA_DK), s_spec(GLA_DK),
                  full((128, GLA_DK)), full((128, GLA_DK)), full((1, GLA_DK)), full((1, GLA_DK))],
        out_specs=[pl.BlockSpec((TM_PREP, W3), lambda i: (i, 0)),
                   full((128, GLA_DK)), full((128, GLA_DK)), full((1, GLA_DK)), full((1, GLA_DK))],
        out_shape=[jax.ShapeDtypeStruct((n_tiles * TM_PREP, W3), BF16),
                   jax.ShapeDtypeStruct((128, GLA_DK), F32), jax.ShapeDtypeStruct((128, GLA_DK), F32),
                   jax.ShapeDtypeStruct((1, GLA_DK), F32), jax.ShapeDtypeStruct((1, GLA_DK), F32)],
        compiler_params=_params(("arbitrary",)),
    )(p3, dq_f, dq_b, dk_f, dk_b, dv_f, dv_b, dc_f, dc_b, upf, upb, bias_f, bias_b)


def _scan_chunk(i, rev):
    if rev:
        return NCHUNK - 1 - i
    return jnp.where(i < NCHUNK_CTX, NCHUNK_LAT + i, i - NCHUNK_CTX)


def _sub_blocks(rev):
    out = []
    for s in range(NSUB):
        rows = (s * SUB, SUB)
        if rev:
            ref = (s + 1) * SUB if s < NSUB - 1 else None
            cols = (s * SUB, CHUNK - s * SUB)
        else:
            ref = s * SUB - 1 if s > 0 else None
            cols = (0, (s + 1) * SUB)
        out.append((rows, ref, cols))
    return out


def _sub_mask(rows, cols, rev):
    r = rows[0] + lax.broadcasted_iota(jnp.int32, (rows[1], cols[1]), 0)
    c = cols[0] + lax.broadcasted_iota(jnp.int32, (rows[1], cols[1]), 1)
    return (c >= r) if rev else (c <= r)


def _sub_operands(qc, kc, cc, rows, ref, cols):
    cref = jnp.zeros((1, HEAD_K), F32) if ref is None else cc[ref:ref + 1]
    eq = jnp.exp(cc[rows[0]:rows[0] + rows[1]] - cref)
    ek = jnp.exp(cref - cc[cols[0]:cols[0] + cols[1]])
    qs = qc[rows[0]:rows[0] + rows[1]] * eq
    kk = kc[cols[0]:cols[0] + cols[1]] * ek
    return qs, kk, eq, ek


def gla_scan_fwd(q, k, v, cum, *, rev, name):
    n = q.shape[0]

    def body(q_ref, k_ref, v_ref, c_ref, o_ref, s_ref, st):
        st[...] = jnp.zeros_like(st)

        def step(i, carry):
            ci = _scan_chunk(i, rev)
            rws = pl.ds(pl.multiple_of(ci * CHUNK, CHUNK), CHUNK)
            qc, kc, vc, cc = q_ref[0, rws, :], k_ref[0, rws, :], v_ref[0, rws, :], c_ref[0, rws, :]
            s_in = st[...]
            s_ref[0, 0, i] = s_in
            edge = cc[0:1] if rev else cc[CHUNK - 1:CHUNK]
            ke = kc * jnp.exp(edge - cc)
            st[...] = s_in * jnp.exp(edge) + _tn(_b16(vc), _b16(ke))

            @pl.when(ci < NCHUNK_LAT)
            def _():
                o_inter = _nt(_b16(qc * jnp.exp(cc)), _b16(s_in))
                vb = _b16(vc)
                for rows, ref, cols in _sub_blocks(rev):
                    qs, kk, _, _ = _sub_operands(qc, kc, cc, rows, ref, cols)
                    a = jnp.where(_sub_mask(rows, cols, rev), _nt(_b16(qs), _b16(kk)), 0.0)
                    o_s = _nn(_b16(a), vb[cols[0]:cols[0] + cols[1]])
                    o_ref[0, pl.ds(pl.multiple_of(ci * CHUNK, CHUNK) + rows[0], rows[1]), :] = (
                        o_inter[rows[0]:rows[0] + rows[1]] + o_s)
            return carry

        lax.fori_loop(0, NCHUNK, step, 0)
        s_ref[0, 0, NCHUNK] = st[...]

    kspec = pl.BlockSpec((1, SEQ_ALL, HEAD_K), lambda b, h: (b, 0, h))
    return _pallas(
        body, name=name, grid=(n, HEADS),
        in_specs=[kspec, kspec, pl.BlockSpec((1, SEQ_ALL, HEAD_V), lambda b, h: (b, 0, h)), kspec],
        out_specs=[pl.BlockSpec((1, SEQ, HEAD_V), lambda b, h: (b, 0, h)),
                   pl.BlockSpec((1, 1, NCHUNK + 1, HEAD_V, HEAD_K), lambda b, h: (b, h, 0, 0, 0))],
        out_shape=[jax.ShapeDtypeStruct((n, SEQ, D), F32),
                   jax.ShapeDtypeStruct((n, HEADS, NCHUNK + 1, HEAD_V, HEAD_K), F32)],
        scratch_shapes=[pltpu.VMEM((HEAD_V, HEAD_K), F32)],
        compiler_params=_params(("parallel", "parallel")),
    )(q, k, v, cum)


def gla_scan_bwd(q, k, v, cum, s_all, do, *, rev, name):
    n = q.shape[0]

    def body(q_ref, k_ref, v_ref, c_ref, s_ref, do_ref, dq_ref, dk_ref, dv_ref, dc_ref, dst, dq_acc, dk_acc, dv_acc):
        dst[...] = jnp.zeros_like(dst)

        def step(j, carry):
            i = NCHUNK - 1 - j
            ci = _scan_chunk(i, rev)
            rws = pl.ds(pl.multiple_of(ci * CHUNK, CHUNK), CHUNK)
            qc, kc, vc, cc = q_ref[0, rws, :], k_ref[0, rws, :], v_ref[0, rws, :], c_ref[0, rws, :]
            lat = ci < NCHUNK_LAT
            do_rows = pl.ds(pl.multiple_of(jnp.minimum(ci, NCHUNK_LAT - 1) * CHUNK, CHUNK), CHUNK)
            doc = jnp.where(lat, do_ref[0, do_rows, :], 0.0)
            s_in = s_ref[0, 0, i]
            s_out = s_ref[0, 0, i + 1]
            ds_out = dst[...]
            edge = cc[0:1] if rev else cc[CHUNK - 1:CHUNK]
            e_q = jnp.exp(cc)
            e_k = jnp.exp(edge - cc)
            dob = _b16(doc)
            dsb = _b16(ds_out)
            dst[...] = ds_out * jnp.exp(edge) + _tn(dob, _b16(qc * e_q))
            dq_acc[...] = e_q * _nn(dob, _b16(s_in))
            dk_acc[...] = e_k * _nn(_b16(vc), dsb)
            dv_acc[...] = _nt(_b16(kc * e_k), dsb)
            vb = _b16(vc)
            for rows, ref, cols in _sub_blocks(rev):
                qs, kk, eq, ek = _sub_operands(qc, kc, cc, rows, ref, cols)
                mask = _sub_mask(rows, cols, rev)
                rsl = slice(rows[0], rows[0] + rows[1])
                csl = pl.ds(cols[0], cols[1])
                qsb, kkb = _b16(qs), _b16(kk)
                a = jnp.where(mask, _nt(qsb, kkb), 0.0)
                da = _b16(jnp.where(mask, _nt(dob[rsl], vb[cols[0]:cols[0] + cols[1]]), 0.0))
                dq_acc[pl.ds(rows[0], rows[1]), :] += _nn(da, kkb) * eq
                dk_acc[csl, :] += _tn(da, qsb) * ek
                dv_acc[csl, :] += _tn(_b16(a), dob[rsl])
            dq = dq_acc[...]
            dk = dk_acc[...]
            dc = qc * dq - kc * dk
            bnd = jnp.sum(ds_out * s_out, axis=0, keepdims=True)
            edge_row = 0 if rev else CHUNK - 1
            is_edge = lax.broadcasted_iota(jnp.int32, (CHUNK, HEAD_K), 0) == edge_row
            dq_ref[0, rws, :] = dq
            dk_ref[0, rws, :] = dk
            dv_ref[0, rws, :] = dv_acc[...]
            dc_ref[0, rws, :] = dc + jnp.where(is_edge, bnd, 0.0)
            return carry

        lax.fori_loop(0, NCHUNK, step, 0)

    kspec = pl.BlockSpec((1, SEQ_ALL, HEAD_K), lambda b, h: (b, 0, h))
    vspec = pl.BlockSpec((1, SEQ_ALL, HEAD_V), lambda b, h: (b, 0, h))
    return _pallas(
        body, name=name, grid=(n, HEADS),
        in_specs=[kspec, kspec, vspec, kspec,
                  pl.BlockSpec((1, 1, NCHUNK + 1, HEAD_V, HEAD_K), lambda b, h: (b, h, 0, 0, 0)),
                  pl.BlockSpec((1, SEQ, HEAD_V), lambda b, h: (b, 0, h))],
        out_specs=[kspec, kspec, vspec, kspec],
        out_shape=[jax.ShapeDtypeStruct((n, SEQ_ALL, GLA_DK), F32), jax.ShapeDtypeStruct((n, SEQ_ALL, GLA_DK), F32),
                   jax.ShapeDtypeStruct((n, SEQ_ALL, D), F32), jax.ShapeDtypeStruct((n, SEQ_ALL, GLA_DK), F32)],
        scratch_shapes=[pltpu.VMEM((HEAD_V, HEAD_K), F32), pltpu.VMEM((CHUNK, HEAD_K), F32),
                        pltpu.VMEM((CHUNK, HEAD_K), F32), pltpu.VMEM((CHUNK, HEAD_V), F32)],
        compiler_params=_params(("parallel", "parallel")),
    )(q, k, v, cum, s_all, do)


def gla_out_fwd(o_f, o_b, r, gnorm):
    n = o_f.shape[0]
    tiles = SEQ // TM_EW

    def body(of_ref, ob_ref, r_ref, g_ref, og_ref):
        for h in range(HEADS):
            cols = pl.ds(h * HEAD_V, HEAD_V)
            o = of_ref[0, :, cols] + ob_ref[0, :, cols]
            rs = lax.rsqrt(jnp.mean(o * o, axis=-1, keepdims=True) + EPS)
            og_ref[:, cols] = (o * rs * g_ref[...] * _silu(r_ref[:, cols])).astype(og_ref.dtype)

    ospec = pl.BlockSpec((1, TM_EW, D), lambda b, j: (b, j, 0))
    row = pl.BlockSpec((TM_EW, D), lambda b, j: (b * tiles + j, 0))
    return _pallas(
        body, name="gla_out_fwd", grid=(n, tiles),
        in_specs=[ospec, ospec, row, pl.BlockSpec((1, HEAD_V), lambda b, j: (0, 0))],
        out_specs=row, out_shape=jax.ShapeDtypeStruct((n * SEQ, D), BF16),
        compiler_params=_params(("parallel", "parallel")),
    )(o_f, o_b, r, gnorm)


def gla_out_bwd(o_f, o_b, r, dog, gnorm):
    n = o_f.shape[0]
    tiles = SEQ // TM_EW

    def body(of_ref, ob_ref, r_ref, d_ref, g_ref, do_ref, dr_ref, dg_ref):
        @pl.when((pl.program_id(0) == 0) & (pl.program_id(1) == 0))
        def _():
            dg_ref[...] = jnp.zeros_like(dg_ref)

        for h in range(HEADS):
            cols = pl.ds(h * HEAD_V, HEAD_V)
            o = of_ref[0, :, cols] + ob_ref[0, :, cols]
            rv = r_ref[:, cols]
            dv = d_ref[:, cols]
            rs = lax.rsqrt(jnp.mean(o * o, axis=-1, keepdims=True) + EPS)
            oh = o * rs
            dr_ref[:, cols] = (dv * oh * g_ref[...] * _dsilu(rv)).astype(dr_ref.dtype)
            dn = dv * _silu(rv)
            dg_ref[...] += jnp.sum(dn * oh, axis=0, keepdims=True)
            doh = dn * g_ref[...]
            do_ref[0, :, cols] = rs * (doh - oh * jnp.mean(doh * oh, axis=-1, keepdims=True))

    ospec = pl.BlockSpec((1, TM_EW, D), lambda b, j: (b, j, 0))
    row = pl.BlockSpec((TM_EW, D), lambda b, j: (b * tiles + j, 0))
    vec = pl.BlockSpec((1, HEAD_V), lambda b, j: (0, 0))
    return _pallas(
        body, name="gla_out_bwd", grid=(n, tiles),
        in_specs=[ospec, ospec, row, row, vec],
        out_specs=[ospec, row, vec],
        out_shape=[jax.ShapeDtypeStruct((n, SEQ, D), F32), jax.ShapeDtypeStruct((n * SEQ, D), BF16),
                   jax.ShapeDtypeStruct((1, HEAD_V), F32)],
        compiler_params=_params(("arbitrary", "arbitrary")),
    )(o_f, o_b, r, dog, gnorm)


def merge_fwd(p5, y_conv, y_gla):
    t = y_conv.shape[0]

    def body(mc_ref, mg_ref, yc_ref, yg_ref, o_ref):
        o_ref[...] = (_sigmoid(mc_ref[...]) * yc_ref[...] + _sigmoid(mg_ref[...]) * yg_ref[...]).astype(o_ref.dtype)

    row = pl.BlockSpec((TM_EW, D), lambda i: (i, 0))
    return _pallas(
        body, name="merge_fwd", grid=(t // TM_EW,),
        in_specs=[row, pl.BlockSpec((TM_EW, D), lambda i: (i, 1)), row, row], out_specs=row,
        out_shape=jax.ShapeDtypeStruct((t, D), BF16), compiler_params=_params(("parallel",)),
    )(p5, p5, y_conv, y_gla)


def merge_bwd(p5, y_conv, y_gla, dmerged):
    t = y_conv.shape[0]

    def body(mc_ref, mg_ref, yc_ref, yg_ref, d_ref, dyc_ref, dyg_ref, dp_ref):
        d = d_ref[...]
        sc = _sigmoid(mc_ref[...])
        sg = _sigmoid(mg_ref[...])
        dyc_ref[...] = (d * sc).astype(dyc_ref.dtype)
        dyg_ref[...] = (d * sg).astype(dyg_ref.dtype)
        dp_ref[:, pl.ds(0, D)] = (d * yc_ref[...] * sc * (1.0 - sc)).astype(dp_ref.dtype)
        dp_ref[:, pl.ds(D, D)] = (d * yg_ref[...] * sg * (1.0 - sg)).astype(dp_ref.dtype)

    row = pl.BlockSpec((TM_EW, D), lambda i: (i, 0))
    return _pallas(
        body, name="merge_bwd", grid=(t // TM_EW,),
        in_specs=[row, pl.BlockSpec((TM_EW, D), lambda i: (i, 1)), row, row, row],
        out_specs=[row, row, pl.BlockSpec((TM_EW, 2 * D), lambda i: (i, 0))],
        out_shape=[jax.ShapeDtypeStruct((t, D), BF16), jax.ShapeDtypeStruct((t, D), BF16),
                   jax.ShapeDtypeStruct((t, 2 * D), BF16)],
        compiler_params=_params(("parallel",)),
    )(p5, p5, y_conv, y_gla, dmerged)


def final_fwd_bwd(x2, mo, gate, final_g, target, n_samples):
    t = x2.shape[0]
    tiles = SEQ // TM_EW

    def body(x_ref, mo_ref, gate_ref, g_ref, t_ref, dh_ref, dmo_ref, dgate_ref, dg_ref, loss_ref):
        b, j = pl.program_id(0), pl.program_id(1)
        mo_v = mo_ref[...]
        h = x_ref[...] + gate_ref[0] * mo_v
        rs = lax.rsqrt(jnp.mean(h * h, axis=-1, keepdims=True) + EPS)
        nh = h * rs
        err = nh * g_ref[...] - t_ref[...]
        dy = err * (1.0 / D)
        dn = dy * g_ref[...]
        dh = rs * (dn - nh * jnp.mean(dn * nh, axis=-1, keepdims=True))
        dh_ref[...] = dh
        dmo_ref[...] = (dh * gate_ref[0]).astype(dmo_ref.dtype)

        @pl.when(j == 0)
        def _():
            dgate_ref[...] = jnp.zeros_like(dgate_ref)

        @pl.when((b == 0) & (j == 0))
        def _():
            dg_ref[...] = jnp.zeros_like(dg_ref)
            loss_ref[...] = jnp.zeros_like(loss_ref)

        dgate_ref[0] += jnp.sum(dh * mo_v, axis=0, keepdims=True)
        dg_ref[...] += jnp.sum(dy * nh, axis=0, keepdims=True)
        loss_ref[...] += (0.5 / D) * jnp.sum(err * err)

    row = pl.BlockSpec((TM_EW, D), lambda b, j: (b * tiles + j, 0))
    per = pl.BlockSpec((1, 1, D), lambda b, j: (b, 0, 0))
    vec = pl.BlockSpec((1, D), lambda b, j: (0, 0))
    return _pallas(
        body, name="final_fwd_bwd", grid=(n_samples, tiles),
        in_specs=[row, row, per, vec, row],
        out_specs=[row, row, per, vec, pl.BlockSpec((8, 128), lambda b, j: (0, 0))],
        out_shape=[jax.ShapeDtypeStruct((t, D), F32), jax.ShapeDtypeStruct((t, D), BF16),
                   jax.ShapeDtypeStruct((n_samples, 1, D), F32), jax.ShapeDtypeStruct((1, D), F32),
                   jax.ShapeDtypeStruct((8, 128), F32)],
        compiler_params=_params(("arbitrary", "arbitrary")),
    )(x2, mo, gate, final_g, target)


def local_step(x, ctx, target, mod, wts, small):
    n = x.shape[0]
    t = n * SEQ
    t_all = t + n * NCTX
    x2 = x.reshape(t, D)
    ctx2 = ctx.reshape(n * NCTX, D)
    tgt2 = target.reshape(t, D)
    scale1, shift, gate = mod

    u = norm_mod_fwd(x2, ctx2, scale1, shift, small["norm_g"])
    p1 = matmul_nn(u, wts["w1"], small["b1"], name="proj_conv", m=t, tm=512, tn=512, out_dtype=F32)
    p2 = matmul_nn(u, wts["w2"], small["b2"], name="proj_z", m=t, tm=512, tn=512, out_dtype=F32)
    p3 = matmul_nn(u, wts["w3"], small["b3"], name="proj_gla", m=t_all, tm=512, tn=W3, out_dtype=F32)
    p4 = matmul_nn(u, wts["w4"], small["b4"], name="proj_r", m=t, tm=512, tn=512, out_dtype=F32)
    p5 = matmul_nn(u, wts["w5"], small["b5"], name="proj_merge", m=t, tm=512, tn=512, out_dtype=F32)

    aconv = conv_fwd(p1, small["conv_w"], small["conv_b"], n)
    ac = ln_gate_fwd(aconv, p2, small["conv_ln_g"], small["conv_ln_b"])
    y_conv = matmul_nn(ac, wts["conv_proj"], None, name="conv_proj_fwd", m=t, tm=512, tn=512, out_dtype=F32)

    qs, ks, vs, cum_f, cum_b = gla_prep_fwd(p3, small["upf"], small["upb"], small["bias_f"], small["bias_b"], n)
    o_f, s_f = gla_scan_fwd(qs, ks, vs, cum_f, rev=False, name="gla_scan_fwd_f")
    o_b, s_b = gla_scan_fwd(qs, ks, vs, cum_b, rev=True, name="gla_scan_fwd_b")
    og = gla_out_fwd(o_f, o_b, p4, small["gla_norm_g"])
    y_gla = matmul_nn(og, wts["gla_proj"], None, name="gla_proj_fwd", m=t, tm=512, tn=512, out_dtype=F32)

    merged = merge_fwd(p5, y_conv, y_gla)
    mo = matmul_nn(merged, wts["w_out"], None, name="w_out_fwd", m=t, tm=512, tn=512, out_dtype=F32)
    dh, dmo, dgate, d_final_g, loss = final_fwd_bwd(x2, mo, gate, small["final_norm_g"], tgt2, n)

    g = {"final_norm_g": d_final_g}
    dmerged = matmul_nt(dmo, wts["w_out"], name="w_out_dgrad", tm=512)
    g["w_out"] = matmul_tn(merged, dmo, name="w_out_wgrad", t=t, tn=512, tt=512)[0]
    dyc, dyg, dp5 = merge_bwd(p5, y_conv, y_gla, dmerged)

    dac = matmul_nt(dyc, wts["conv_proj"], name="conv_proj_dgrad", tm=512)
    g["conv_proj"] = matmul_tn(ac, dyc, name="conv_proj_wgrad", t=t, tn=512, tt=512)[0]
    daconv, dp2, g["conv_ln_g"], g["conv_ln_b"] = ln_gate_bwd(aconv, p2, dac, small["conv_ln_g"], small["conv_ln_b"])
    dp1, dconv_w, dconv_b = conv_bwd(p1, daconv, small["conv_w"], n)
    g["conv_w"], g["conv_b"] = dconv_w, dconv_b

    dog = matmul_nt(dyg, wts["gla_proj"], name="gla_proj_dgrad", tm=512)
    g["gla_proj"] = matmul_tn(og, dyg, name="gla_proj_wgrad", t=t, tn=512, tt=512)[0]
    do, dp4, g["gla_norm_g"] = gla_out_bwd(o_f, o_b, p4, dog, small["gla_norm_g"])
    dq_f, dk_f, dv_f, dc_f = gla_scan_bwd(qs, ks, vs, cum_f, s_f, do, rev=False, name="gla_scan_bwd_f")
    dq_b, dk_b, dv_b, dc_b = gla_scan_bwd(qs, ks, vs, cum_b, s_b, do, rev=True, name="gla_scan_bwd_b")
    dp3, g["upf"], g["upb"], g["bias_f"], g["bias_b"] = gla_prep_bwd(
        p3, dq_f, dq_b, dk_f, dk_b, dv_f, dv_b, dc_f, dc_b,
        small["upf"], small["upb"], small["bias_f"], small["bias_b"], n)

    dps = [dp1, dp2, dp3, dp4, dp5]
    du = dgrad_multi(dps, [wts["w%d" % (i + 1)] for i in range(5)], t_all=t_all, t_lat=t, tm=256)
    for i, dp in enumerate(dps):
        rows = dp.shape[0]
        tn = W3 if dp.shape[1] == W3 else 512
        g["w%d" % (i + 1)], g["b%d" % (i + 1)] = matmul_tn(
            u, dp, name="w_in_wgrad_%d" % (i + 1), t=rows, tn=tn, tt=512, colsum=True)
    grad_x, dshift, dscale, g["norm_g"] = norm_mod_bwd(x2, ctx2, du, dh, scale1, small["norm_g"])
    g["shift"], g["scale"], g["gate"] = dshift, dscale, dgate
    return loss, grad_x, g


def _group_cols(w):
    gv, gg, z = w[..., 0:1024], w[..., 1024:2048], w[..., 2048:3072]
    q, k, v = w[..., 3072:3584], w[..., 3584:4096], w[..., 4096:5120]
    ab = w[..., 5120:5152]
    r, mc, mg = w[..., 5152:6176], w[..., 6176:7200], w[..., 7200:8224]
    g1 = jnp.concatenate([p for j in range(CONV_NCB)
                          for p in (gv[..., CONV_CB * j:CONV_CB * (j + 1)], gg[..., CONV_CB * j:CONV_CB * (j + 1)])], -1)
    pad = jnp.zeros(w.shape[:-1] + (W3 - 2080,), w.dtype)
    g3 = jnp.concatenate([v, q, k, ab, pad], -1)
    return g1, z, g3, r, jnp.concatenate([mc, mg], -1)


def _ungroup_cols(g1, g2, g3, g4, g5):
    gv = jnp.concatenate([g1[..., 2 * CONV_CB * j:2 * CONV_CB * j + CONV_CB] for j in range(CONV_NCB)], -1)
    gg = jnp.concatenate([g1[..., 2 * CONV_CB * j + CONV_CB:2 * CONV_CB * (j + 1)] for j in range(CONV_NCB)], -1)
    v, q, k, ab = g3[..., 0:1024], g3[..., 1024:1536], g3[..., 1536:2048], g3[..., 2048:2080]
    return jnp.concatenate([gv, gg, g2, q, k, v, ab, g4, g5[..., 0:1024], g5[..., 1024:2048]], -1)


def _pad_up(up, row0):
    return jnp.zeros((128, GLA_DK), F32).at[row0:row0 + up.shape[0]].set(up)


def _adamw_math(w, g, m, v):
    m = ADAM_B1 * m + (1.0 - ADAM_B1) * g
    v = ADAM_B2 * v + (1.0 - ADAM_B2) * (g * g)
    m_hat = m / (1.0 - ADAM_B1 ** ADAM_STEP)
    v_hat = v / (1.0 - ADAM_B2 ** ADAM_STEP)
    delta = -ADAM_LR * (m_hat / (jnp.sqrt(v_hat) + ADAM_EPS) + ADAM_WD * w)
    return delta, m, v


def adamw2d(w, g, m, v, *, name, tr):
    rows, cols = w.shape

    def body(w_ref, g_ref, m_ref, v_ref, d_ref, nm_ref, nv_ref):
        d_ref[...], nm_ref[...], nv_ref[...] = _adamw_math(w_ref[...], g_ref[...], m_ref[...], v_ref[...])

    spec = pl.BlockSpec((tr, cols), lambda i: (i, 0))
    return _pallas(
        body, name=name, grid=(rows // tr,), in_specs=[spec] * 4, out_specs=[spec] * 3,
        out_shape=[jax.ShapeDtypeStruct((rows, cols), F32)] * 3, compiler_params=_params(("parallel",)),
    )(w, g, m, v)


def sum_devices(sall):
    rows = sall.shape[1]

    def body(s_ref, o_ref):
        acc = s_ref[0]
        for d in range(1, N_DEV):
            acc = acc + s_ref[d]
        o_ref[...] = acc

    return _pallas(body, name="sum_devices", out_shape=jax.ShapeDtypeStruct((rows, D), F32),
                   compiler_params=_params())(sall)


def add2(a, b, *, name, tr):
    n, rows, _ = a.shape

    def body(a_ref, b_ref, o_ref):
        o_ref[...] = a_ref[...] + b_ref[...]

    spec = pl.BlockSpec((1, tr, D), lambda i, t: (i, t, 0))
    return _pallas(body, name=name, grid=(n, rows // tr), in_specs=[spec, spec], out_specs=spec,
                   out_shape=jax.ShapeDtypeStruct(a.shape, F32), compiler_params=_params(("parallel", "parallel")))(a, b)


def add4(mine, rb, *, name, tr):
    rows = mine.shape[0]

    def body(m_ref, r_ref, o_ref):
        o_ref[...] = ((m_ref[...] + r_ref[0]) + r_ref[1]) + r_ref[2]

    return _pallas(body, name=name, grid=(rows // tr,),
                   in_specs=[pl.BlockSpec((tr, D), lambda t: (t, 0)), pl.BlockSpec((3, tr, D), lambda t: (0, t, 0))],
                   out_specs=pl.BlockSpec((tr, D), lambda t: (t, 0)),
                   out_shape=jax.ShapeDtypeStruct((rows, D), F32), compiler_params=_params(("parallel",)))(mine, rb)


def ada_bwd(call, cctx_rows, dm_shard, dm_full, adaw):
    nsh = adaw.shape[1]

    def body(c_ref, cc_ref, dms_ref, dmf_ref, w_ref, gw_ref, gb_ref, pq_ref):
        a_lat = _silu(c_ref[...])
        a_ctx = _silu(cc_ref[...])
        dms = dms_ref[...]
        gw_ref[...] = _tn(a_lat, dms[0:64], HI) + _tn(a_ctx, dms[64:72], HI)
        gb_ref[...] = jnp.sum(dmf_ref[...], axis=0, keepdims=True)
        part = _nt(dms[64:72], w_ref[...], HI)
        pq_ref[...] = jnp.zeros_like(pq_ref) + jnp.sum(part, axis=0, keepdims=True)

    return _pallas(body, name="ada_bwd",
                   out_shape=[jax.ShapeDtypeStruct((D, nsh), F32), jax.ShapeDtypeStruct((1, 3 * D), F32),
                              jax.ShapeDtypeStruct((8, D), F32)],
                   compiler_params=_params())(call, cctx_rows, dm_shard, dm_full, adaw)


def cctx_grad(pq_all, cctx_rows):
    def body(p_ref, c_ref, o_ref):
        acc = p_ref[0]
        for qi in range(1, N_CHIPS):
            acc = acc + p_ref[qi]
        o_ref[...] = acc * _dsilu(c_ref[...])

    return _pallas(body, name="cctx_grad", out_shape=jax.ShapeDtypeStruct((8, D), F32),
                   compiler_params=_params())(pq_all, cctx_rows)


def _place():
    x, y, c = lax.axis_index("x"), lax.axis_index("y"), lax.axis_index("c")
    chips = [(1 - x, y), (x, 1 - y), (1 - x, 1 - y)]
    return x, y, c, chips


def _all_peers(x, y, c):
    return [((1 - x) if r & 4 else x, (1 - y) if r & 2 else y, (1 - c) if r & 1 else c) for r in range(1, N_DEV)]


def _remote(src, dst, send_sem, recv_sem, dev):
    return pltpu.make_async_remote_copy(src_ref=src, dst_ref=dst, send_sem=send_sem, recv_sem=recv_sem,
                                        device_id=dev, device_id_type=MESH)


ANY = pl.BlockSpec(memory_space=pl.ANY)
VMEM = pl.BlockSpec(memory_space=pltpu.VMEM)
F_ROWS = 16


def gather_weights(c8, cctx8, adaw, adab, wp, fp):
    nsh = adaw.shape[1]

    def body(c_ref, cctx_ref, adaw_ref, adab_ref, wp_ref, fp_ref, wall_ref, fall_ref, call_ref, mall_ref,
             abuf, w_send, w_recv, h_send, h_recv, c_send, c_recv, m_send, m_recv, f_send, f_recv, l_sem):
        x, y, c, chips = _place()
        q = 2 * x + y
        dev = 4 * x + 2 * y + c
        qs = [2 * cx + cy for cx, cy in chips]
        sib = (x, y, 1 - c)
        mine = pl.ds(pl.multiple_of(c * W_HALF, 16), W_HALF)
        other = pl.ds(pl.multiple_of((1 - c) * W_HALF, 16), W_HALF)

        local = pltpu.make_async_copy(wp_ref, wall_ref.at[q], l_sem)
        local.start()
        bulk = [_remote(wp_ref.at[mine], wall_ref.at[q, mine], w_send.at[j], w_recv.at[j], (*chips[j], c))
                for j in range(3)]
        fall_ref[q] = fp_ref[...]
        small = [_remote(fp_ref, fall_ref.at[q], f_send.at[j], f_recv.at[j], (*chips[j], c)) for j in range(3)]
        my_rows = pl.ds(pl.multiple_of(8 * dev, 8), 8)
        call_ref[my_rows, :] = c_ref[...]
        cond = [_remote(c_ref, call_ref.at[my_rows, :], c_send.at[r], c_recv.at[r], peer)
                for r, peer in enumerate(_all_peers(x, y, c))]
        for cp in bulk + small + cond:
            cp.start()
        for cp in cond:
            cp.wait_recv()

        abuf[pl.ds(0, 64), :] = _silu(call_ref[...])
        abuf[pl.ds(64, 8), :] = _silu(cctx_ref[...])
        mall_ref[q] = _nn(abuf[...], adaw_ref[...], HI) + adab_ref[...]
        mod = [_remote(mall_ref.at[q], mall_ref.at[q], m_send.at[j], m_recv.at[j], (*chips[j], c)) for j in range(3)]
        for cp in mod:
            cp.start()

        handed = []
        for j in range(3):
            bulk[j].wait_recv()
            cp = _remote(wall_ref.at[qs[j], mine], wall_ref.at[qs[j], mine], h_send.at[j], h_recv.at[j], sib)
            cp.start()
            handed.append(cp)
        for j in range(3):
            _remote(wall_ref.at[qs[j], other], wall_ref.at[qs[j], other], h_send.at[j], h_recv.at[j], sib).wait_recv()
        for cp in mod + small:
            cp.wait_recv()
        for cp in bulk + small + cond + mod + handed:
            cp.wait_send()
        local.wait()

    def dma(n):
        return pltpu.SemaphoreType.DMA((n,))

    return _pallas(
        body, name="gather_weights",
        in_specs=[VMEM, VMEM, VMEM, VMEM, ANY, VMEM],
        out_specs=[ANY, VMEM, VMEM, VMEM],
        out_shape=[jax.ShapeDtypeStruct((N_CHIPS, W_ROWS, D), BF16), jax.ShapeDtypeStruct((N_CHIPS, F_ROWS, D), F32),
                   jax.ShapeDtypeStruct((8 * N_DEV, D), F32), jax.ShapeDtypeStruct((N_CHIPS, MOD_ROWS, nsh), F32)],
        scratch_shapes=[pltpu.VMEM((MOD_ROWS, D), F32), dma(3), dma(3), dma(3), dma(3), dma(7), dma(7), dma(3), dma(3),
                        dma(3), dma(3), pltpu.SemaphoreType.DMA(())],
        compiler_params=_params(),
    )(c8, cctx8, adaw, adab, wp, fp)


def gather_small_and_pair(sm, gpack):
    rows = sm.shape[0]

    def body(sm_ref, g_ref, sall_ref, mine_ref, got_ref, s_send, s_recv, a_send, a_recv, l_sem):
        x, y, c, _ = _place()
        dev = 4 * x + 2 * y + c
        mine = pl.ds(pl.multiple_of(c * W_HALF, 8), W_HALF)
        other = pl.ds(pl.multiple_of((1 - c) * W_HALF, 8), W_HALF)
        local = pltpu.make_async_copy(g_ref.at[:, mine, :], mine_ref, l_sem)
        local.start()
        pair = _remote(g_ref.at[:, other, :], got_ref, a_send, a_recv, (x, y, 1 - c))
        pair.start()
        sall_ref[dev] = sm_ref[...]
        small = [_remote(sm_ref, sall_ref.at[dev], s_send.at[r], s_recv.at[r], peer)
                 for r, peer in enumerate(_all_peers(x, y, c))]
        for cp in small:
            cp.start()
        for cp in small:
            cp.wait_recv()
        for cp in small:
            cp.wait_send()
        pair.wait()
        local.wait()

    return _pallas(
        body, name="gather_small_and_pair", in_specs=[VMEM, ANY], out_specs=[VMEM, ANY, ANY],
        out_shape=[jax.ShapeDtypeStruct((N_DEV, rows, D), F32), jax.ShapeDtypeStruct((N_CHIPS, W_HALF, D), F32),
                   jax.ShapeDtypeStruct((N_CHIPS, W_HALF, D), F32)],
        scratch_shapes=[pltpu.SemaphoreType.DMA((7,)), pltpu.SemaphoreType.DMA((7,)), pltpu.SemaphoreType.DMA(()),
                        pltpu.SemaphoreType.DMA(()), pltpu.SemaphoreType.DMA(())],
        compiler_params=_params(),
    )(sm, gpack)


def chip_exchange(pa, pq):
    def body(pa_ref, pq_ref, rb_ref, mine_ref, pqa_ref, b_send, b_recv, p_send, p_recv, l_sem):
        x, y, c, chips = _place()
        q = 2 * x + y
        qs = [2 * cx + cy for cx, cy in chips]
        local = pltpu.make_async_copy(pa_ref.at[q], mine_ref, l_sem)
        local.start()
        big = [_remote(pa_ref.at[qs[j]], rb_ref.at[j], b_send.at[j], b_recv.at[j], (*chips[j], c)) for j in range(3)]
        pqa_ref[q] = pq_ref[...]
        small = [_remote(pq_ref, pqa_ref.at[q], p_send.at[j], p_recv.at[j], (*chips[j], c)) for j in range(3)]
        for cp in big + small:
            cp.start()
        for cp in small + big:
            cp.wait_recv()
        for cp in small + big:
            cp.wait_send()
        local.wait()

    return _pallas(
        body, name="chip_exchange", in_specs=[ANY, VMEM], out_specs=[ANY, ANY, VMEM],
        out_shape=[jax.ShapeDtypeStruct((3, W_HALF, D), F32), jax.ShapeDtypeStruct((W_HALF, D), F32),
                   jax.ShapeDtypeStruct((N_CHIPS, 8, D), F32)],
        scratch_shapes=[pltpu.SemaphoreType.DMA((3,)), pltpu.SemaphoreType.DMA((3,)), pltpu.SemaphoreType.DMA((3,)),
                        pltpu.SemaphoreType.DMA((3,)), pltpu.SemaphoreType.DMA(())],
        compiler_params=_params(),
    )(pa, pq)


def pair_share(gh):
    def body(gh_ref, out_ref, send, recv, l_sem):
        x, y, c, _ = _place()
        local = pltpu.make_async_copy(gh_ref, out_ref.at[c], l_sem)
        local.start()
        pair = _remote(gh_ref, out_ref.at[c], send, recv, (x, y, 1 - c))
        pair.start()
        pair.wait()
        local.wait()

    return _pallas(
        body, name="pair_share", in_specs=[ANY], out_specs=ANY,
        out_shape=jax.ShapeDtypeStruct((2, W_HALF, D), F32),
        scratch_shapes=[pltpu.SemaphoreType.DMA(()), pltpu.SemaphoreType.DMA(()), pltpu.SemaphoreType.DMA(())],
        compiler_params=_params(),
    )(gh)


def _rows_of(shape):
    size = 1
    for s in shape:
        size *= s
    return -(-size // D)


def _pack(arrs, rows_multiple=8):
    parts = []
    total = 0
    for a in arrs:
        f = a.reshape(-1).astype(F32)
        r = _rows_of(a.shape)
        parts.append(jnp.pad(f, (0, r * D - f.shape[0])))
        total += r
    pad_rows = (-total) % rows_multiple
    if pad_rows:
        parts.append(jnp.zeros((pad_rows * D,), F32))
    return jnp.concatenate(parts).reshape(-1, D)


def _unpack(p, shapes):
    out = []
    r0 = 0
    for shp in shapes:
        r = _rows_of(shp)
        size = 1
        for s in shp:
            size *= s
        out.append(p[r0:r0 + r].reshape(-1)[:size].reshape(shp))
        r0 += r
    return out


WEIGHT_NAMES = ['c_ctx', 'ada_w', 'ada_b', 'norm_g', 'w_in', 'b_in', 'conv_w', 'conv_b', 'conv_ln_g', 'conv_ln_b',
                'conv_proj', 'decay_up_fwd', 'decay_bias_fwd', 'decay_up_bwd', 'decay_bias_bwd', 'gla_norm_g', 'gla_proj',
                'w_out', 'final_norm_g']
SMALL_NAMES = ['c_ctx', 'ada_b', 'norm_g', 'b_in', 'conv_w', 'conv_b', 'conv_ln_g', 'conv_ln_b', 'decay_up_fwd',
               'decay_bias_fwd', 'decay_up_bwd', 'decay_bias_bwd', 'gla_norm_g', 'final_norm_g']


def kernel(x, c, ctx, c_ctx, ada_w, ada_b, norm_g, w_in, b_in, conv_w, conv_b, conv_ln_g, conv_ln_b, conv_proj, decay_up_fwd, decay_bias_fwd, decay_up_bwd, decay_bias_bwd, gla_norm_g, gla_proj, w_out, final_norm_g, loss_target, m_c_ctx, m_ada_w, m_ada_b, m_norm_g, m_w_in, m_b_in, m_conv_w, m_conv_b, m_conv_ln_g, m_conv_ln_b, m_conv_proj, m_decay_up_fwd, m_decay_bias_fwd, m_decay_up_bwd, m_decay_bias_bwd, m_gla_norm_g, m_gla_proj, m_w_out, m_final_norm_g, v_c_ctx, v_ada_w, v_ada_b, v_norm_g, v_w_in, v_b_in, v_conv_w, v_conv_b, v_conv_ln_g, v_conv_ln_b, v_conv_proj, v_decay_up_fwd, v_decay_bias_fwd, v_decay_up_bwd, v_decay_bias_bwd, v_gla_norm_g, v_gla_proj, v_w_out, v_final_norm_g):
    w = dict(c_ctx=c_ctx, ada_w=ada_w, ada_b=ada_b, norm_g=norm_g, w_in=w_in, b_in=b_in, conv_w=conv_w, conv_b=conv_b,
             conv_ln_g=conv_ln_g, conv_ln_b=conv_ln_b, conv_proj=conv_proj, decay_up_fwd=decay_up_fwd,
             decay_bias_fwd=decay_bias_fwd, decay_up_bwd=decay_up_bwd, decay_bias_bwd=decay_bias_bwd,
             gla_norm_g=gla_norm_g, gla_proj=gla_proj, w_out=w_out, final_norm_g=final_norm_g)
    m = dict(c_ctx=m_c_ctx, ada_w=m_ada_w, ada_b=m_ada_b, norm_g=m_norm_g, w_in=m_w_in, b_in=m_b_in, conv_w=m_conv_w,
             conv_b=m_conv_b, conv_ln_g=m_conv_ln_g, conv_ln_b=m_conv_ln_b, conv_proj=m_conv_proj,
             decay_up_fwd=m_decay_up_fwd, decay_bias_fwd=m_decay_bias_fwd, decay_up_bwd=m_decay_up_bwd,
             decay_bias_bwd=m_decay_bias_bwd, gla_norm_g=m_gla_norm_g, gla_proj=m_gla_proj, w_out=m_w_out,
             final_norm_g=m_final_norm_g)
    v = dict(c_ctx=v_c_ctx, ada_w=v_ada_w, ada_b=v_ada_b, norm_g=v_norm_g, w_in=v_w_in, b_in=v_b_in, conv_w=v_conv_w,
             conv_b=v_conv_b, conv_ln_g=v_conv_ln_g, conv_ln_b=v_conv_ln_b, conv_proj=v_conv_proj,
             decay_up_fwd=v_decay_up_fwd, decay_bias_fwd=v_decay_bias_fwd, decay_up_bwd=v_decay_up_bwd,
             decay_bias_bwd=v_decay_bias_bwd, gla_norm_g=v_gla_norm_g, gla_proj=v_gla_proj, w_out=v_w_out,
             final_norm_g=v_final_norm_g)
    n = x.shape[0]
    ax, ay, ac = lax.axis_index("x"), lax.axis_index("y"), lax.axis_index("c")
    q = 2 * ax + ay
    dev = 4 * ax + 2 * ay + ac
    nsh = ada_w.shape[2]

    wp = jnp.concatenate([w_in[0].astype(BF16).reshape(W_IN_SHARD, D), conv_proj[0].astype(BF16),
                          gla_proj[0].astype(BF16), w_out[0].astype(BF16), jnp.zeros((W_ROWS - W_USED, D), BF16)], 0)
    fp = _pack([conv_w[0], decay_up_fwd[0], decay_up_bwd[0]], F_ROWS)
    c8 = jnp.pad(c, ((0, 8 - n), (0, 0)))
    cctx8 = jnp.pad(c_ctx[None], ((0, 7), (0, 0)))
    adab_sh = lax.dynamic_slice(ada_b, (0, q * nsh), (1, nsh))
    wall, fall, call, mall = gather_weights(c8, cctx8, ada_w[0], adab_sh, wp, fp)

    mod_all = jnp.transpose(mall, (1, 0, 2)).reshape(MOD_ROWS, 3 * D)
    mod_mine = lax.dynamic_slice(mod_all, (8 * dev, 0), (n, 3 * D))
    mod_ctx = mod_all[64:65]
    shift = jnp.concatenate([mod_mine[:, 0:D], mod_ctx[:, 0:D]], 0)[:, None, :]
    scale1 = 1.0 + jnp.concatenate([mod_mine[:, D:2 * D], mod_ctx[:, D:2 * D]], 0)[:, None, :]
    gate = mod_mine[:, 2 * D:3 * D][:, None, :]

    w_in_full = jnp.concatenate([wall[i, 0:W_IN_SHARD].reshape(D, W_IN_SHARD) for i in range(N_CHIPS)], 1)
    g1, g2, g3, g4, g5 = _group_cols(w_in_full)
    r0 = W_IN_SHARD
    wts = dict(w1=g1, w2=g2, w3=g3, w4=g4, w5=g5,
               conv_proj=wall[:, r0:r0 + 256].reshape(D, D), gla_proj=wall[:, r0 + 256:r0 + 512].reshape(D, D),
               w_out=wall[:, r0 + 512:r0 + 768].reshape(D, D))
    f_parts = [_unpack(fall[i], [conv_w.shape[1:], decay_up_fwd.shape[1:], decay_up_bwd.shape[1:]]) for i in range(N_CHIPS)]
    conv_w_full = jnp.concatenate([p[0] for p in f_parts], 1)
    upf_full = jnp.concatenate([p[1] for p in f_parts], 1)
    upb_full = jnp.concatenate([p[2] for p in f_parts], 1)
    b1, b2, b3, b4, b5 = _group_cols(b_in)
    small = dict(b1=b1, b2=b2, b3=b3, b4=b4, b5=b5, norm_g=norm_g,
                 conv_w=jnp.pad(conv_w_full, ((0, 1), (0, 0))), conv_b=conv_b, conv_ln_g=conv_ln_g, conv_ln_b=conv_ln_b,
                 upf=_pad_up(upf_full, 0), upb=_pad_up(upb_full, 16), bias_f=decay_bias_fwd, bias_b=decay_bias_bwd,
                 gla_norm_g=gla_norm_g, final_norm_g=final_norm_g[None])

    loss_part, grad_x2, g = local_step(x, ctx, loss_target, (scale1, shift, gate), wts, small)
    loss = lax.psum(loss_part[0, 0], ("x", "y", "c"))

    dm_mine = jnp.concatenate([g["shift"][:n, 0], g["scale"][:n, 0], g["gate"][:, 0]], -1)
    dm_ctx = jnp.concatenate([g["shift"][n, 0], g["scale"][n, 0], jnp.zeros((D,), F32)], -1)
    d_b_in = _ungroup_cols(*[g["b%d" % i] for i in range(1, 6)])
    small_grads = [d_b_in, g["norm_g"], g["conv_b"].sum(0), g["conv_ln_g"], g["conv_ln_b"], g["bias_f"], g["bias_b"],
                   g["gla_norm_g"], g["final_norm_g"], g["conv_w"].sum(0)[:CONV_K], g["upf"][0:16], g["upb"][16:32],
                   dm_mine, dm_ctx]
    small_shapes = [a.shape for a in small_grads]
    sm = _pack(small_grads)
    d_w_in = _ungroup_cols(*[g["w%d" % i] for i in range(1, 6)])
    gpack = jnp.concatenate([
        jnp.transpose(d_w_in.reshape(D, N_CHIPS, W_IN_SHARD), (1, 0, 2)).reshape(N_CHIPS, W_IN_SHARD, D),
        g["conv_proj"].reshape(N_CHIPS, 256, D), g["gla_proj"].reshape(N_CHIPS, 256, D),
        g["w_out"].reshape(N_CHIPS, 256, D), jnp.zeros((N_CHIPS, W_ROWS - W_USED, D), F32)], 1)
    sall, ga_mine, ga_got = gather_small_and_pair(sm, gpack)
    pa = add2(ga_mine, ga_got, name="pair_add", tr=W_HALF // 2)

    gsum = sum_devices(sall)
    (s_b_in, s_norm_g, s_conv_b, s_ln_g, s_ln_b, s_bias_f, s_bias_b, s_gla_g, s_final_g, s_conv_w, s_upf, s_upb,
     _, _) = _unpack(gsum, small_shapes)
    dm_rows = [_unpack(sall[i], small_shapes)[-2:] for i in range(N_DEV)]
    dm_full = jnp.concatenate(
        [jnp.pad(jnp.stack([r[0] for r in dm_rows]), ((0, 0), (0, 8 - n), (0, 0))).reshape(8 * N_DEV, 3 * D),
         jnp.stack([r[1] for r in dm_rows])], 0)
    dm_shard = lax.dynamic_slice(dm_full, (0, q * nsh), (MOD_ROWS, nsh))
    cctx_rows = jnp.broadcast_to(c_ctx[None], (8, D))
    g_ada_w, g_ada_b, pq = ada_bwd(call, cctx_rows, dm_shard, dm_full, ada_w[0])

    rb, pa_mine, pq_all = chip_exchange(pa, pq)
    gh = add4(pa_mine, rb, name="chip_add", tr=W_HALF // 2)
    gfull = pair_share(gh).reshape(W_ROWS, D)
    g_c_ctx = cctx_grad(pq_all, cctx_rows)[0]

    grads = dict(
        c_ctx=g_c_ctx, ada_w=g_ada_w[None], ada_b=g_ada_b, norm_g=s_norm_g,
        w_in=gfull[0:W_IN_SHARD].reshape(1, D, W_IN_SHARD), b_in=s_b_in,
        conv_w=lax.dynamic_slice(s_conv_w, (0, q * 256), (CONV_K, 256))[None], conv_b=s_conv_b,
        conv_ln_g=s_ln_g, conv_ln_b=s_ln_b, conv_proj=gfull[r0:r0 + 256][None],
        decay_up_fwd=lax.dynamic_slice(s_upf, (0, q * 128), (16, 128))[None], decay_bias_fwd=s_bias_f,
        decay_up_bwd=lax.dynamic_slice(s_upb, (0, q * 128), (16, 128))[None], decay_bias_bwd=s_bias_b,
        gla_norm_g=s_gla_g, gla_proj=gfull[r0 + 256:r0 + 512][None], w_out=gfull[r0 + 512:r0 + 768][None],
        final_norm_g=s_final_g[0])

    delta, new_m, new_v = {}, {}, {}
    for name, tr in [("ada_w", 128), ("w_in", 128), ("conv_proj", 128), ("gla_proj", 128), ("w_out", 128)]:
        shp = w[name].shape
        two = lambda a: a.reshape(shp[-2], shp[-1])
        d_, m_, v_ = adamw2d(two(w[name]), two(grads[name]), two(m[name]), two(v[name]), name="adamw_" + name, tr=tr)
        delta[name], new_m[name], new_v[name] = d_.reshape(shp), m_.reshape(shp), v_.reshape(shp)
    shapes = [w[nm].shape for nm in SMALL_NAMES]
    packs = [_pack([src[nm] for nm in SMALL_NAMES]) for src in (w, grads, m, v)]
    d_, m_, v_ = adamw2d(*packs, name="adamw_small", tr=packs[0].shape[0])
    for nm, a, b, cc in zip(SMALL_NAMES, _unpack(d_, shapes), _unpack(m_, shapes), _unpack(v_, shapes)):
        delta[nm], new_m[nm], new_v[nm] = a, b, cc

    grad_x = grad_x2.reshape(x.shape)
    return (loss, grad_x, *[grads[nm].reshape(w[nm].shape) for nm in WEIGHT_NAMES], *[delta[nm] for nm in WEIGHT_NAMES],
            *[new_m[nm] for nm in WEIGHT_NAMES], *[new_v[nm] for nm in WEIGHT_NAMES])
```

```python
import jax
import jax.numpy as jnp
from jax import lax
from jax.experimental import pallas as pl
from jax.experimental.pallas import tpu as pltpu

F32 = jnp.float32
BF16 = jnp.bfloat16
MESH = pl.DeviceIdType.MESH
HI = lax.Precision.HIGHEST

D = 1024
SEQ = 2048
GRID_W = 64
GRID_H = SEQ // GRID_W
NCTX = 256
SEQ_ALL = SEQ + NCTX
EPS = 1e-6
CONV_K = 31
CONV_PAD = CONV_K // 2
HEADS = 4
HEAD_K = 128
HEAD_V = 256
GLA_DK = HEADS * HEAD_K
GATE_TAU = 16.0
Q_SCALE = HEAD_K ** -0.5
CHUNK = 64
NCHUNK = SEQ_ALL // CHUNK
NCHUNK_LAT = SEQ // CHUNK
NCHUNK_CTX = NCHUNK - NCHUNK_LAT
SUB = 16
NSUB = CHUNK // SUB
N_IN = 8224
W3 = 2176
O3_V, O3_Q, O3_K, O3_AB = 0, 1024, 1536, 2048

ADAM_LR, ADAM_B1, ADAM_B2, ADAM_EPS, ADAM_WD, ADAM_STEP = 0.001, 0.9, 0.999, 1e-08, 0.01, 10
VMEM_LIMIT = 56 * 1024 * 1024

N_CHIPS = 4
N_DEV = 8
W_IN_SHARD = N_IN // N_CHIPS
MOD_ROWS = 72


def _pallas(body, **kw):
    return pl.pallas_call(body, **kw)


def _params(sem=None, **kw):
    if sem is not None:
        kw["dimension_semantics"] = sem
    return pltpu.CompilerParams(vmem_limit_bytes=VMEM_LIMIT, **kw)


def _sigmoid(v):
    return 1.0 / (1.0 + jnp.exp(-v))


def _silu(v):
    return v * _sigmoid(v)


def _dsilu(v):
    s = _sigmoid(v)
    return s * (1.0 + v * (1.0 - s))


def _log_sigmoid(v):
    return jnp.minimum(v, 0.0) - jnp.log(1.0 + jnp.exp(-jnp.abs(v)))


def _dot(a, b, dims, precision=None):
    return lax.dot_general(a, b, (dims, ((), ())), preferred_element_type=F32, precision=precision)


def _nn(a, b, precision=None):
    return _dot(a, b, ((1,), (0,)), precision)


def _nt(a, b, precision=None):
    return _dot(a, b, ((1,), (1,)), precision)


def _tn(a, b, precision=None):
    return _dot(a, b, ((0,), (0,)), precision)


def _b16(v):
    return v.astype(BF16)


def matmul_nn(a, b, bias, *, name, m, tm, tn, out_dtype):
    k = a.shape[1]
    n = b.shape[1]
    has_bias = bias is not None

    def body(*refs):
        if has_bias:
            a_ref, b_ref, bias_ref, o_ref = refs
            acc = _nn(a_ref[...], b_ref[...]) + bias_ref[...]
        else:
            a_ref, b_ref, o_ref = refs
            acc = _nn(a_ref[...], b_ref[...])
        o_ref[...] = acc.astype(o_ref.dtype)

    in_specs = [pl.BlockSpec((tm, k), lambda j, i: (i, 0)), pl.BlockSpec((k, tn), lambda j, i: (0, j))]
    args = [a, b]
    if has_bias:
        in_specs.append(pl.BlockSpec((1, tn), lambda j, i: (0, j)))
        args.append(bias)
    return _pallas(
        body, name=name, grid=(n // tn, m // tm), in_specs=in_specs,
        out_specs=pl.BlockSpec((tm, tn), lambda j, i: (i, j)),
        out_shape=jax.ShapeDtypeStruct((m, n), out_dtype),
        compiler_params=_params(("parallel", "parallel")),
    )(*args)


def matmul_nt(a, b, *, name, tm):
    m, k = a.shape
    n = b.shape[0]

    def body(a_ref, b_ref, o_ref):
        o_ref[...] = _nt(a_ref[...], b_ref[...])

    return _pallas(
        body, name=name, grid=(m // tm,),
        in_specs=[pl.BlockSpec((tm, k), lambda i: (i, 0)), pl.BlockSpec((n, k), lambda i: (0, 0))],
        out_specs=pl.BlockSpec((tm, n), lambda i: (i, 0)),
        out_shape=jax.ShapeDtypeStruct((m, n), F32),
        compiler_params=_params(("parallel",)),
    )(a, b)


def matmul_tn(a, b, *, name, t, tn, tt, colsum=False):
    m = a.shape[1]
    n = b.shape[1]

    def body(a_ref, b_ref, o_ref, *rest):
        @pl.when(pl.program_id(1) == 0)
        def _():
            o_ref[...] = jnp.zeros_like(o_ref)
            if colsum:
                rest[0][...] = jnp.zeros_like(rest[0])
        o_ref[...] += _tn(a_ref[...], b_ref[...])
        if colsum:
            rest[0][...] += jnp.sum(b_ref[...].astype(F32), axis=0, keepdims=True)

    out_specs = [pl.BlockSpec((m, tn), lambda j, s: (0, j))]
    out_shape = [jax.ShapeDtypeStruct((m, n), F32)]
    if colsum:
        out_specs.append(pl.BlockSpec((1, tn), lambda j, s: (0, j)))
        out_shape.append(jax.ShapeDtypeStruct((1, n), F32))
    return _pallas(
        body, name=name, grid=(n // tn, t // tt),
        in_specs=[pl.BlockSpec((tt, m), lambda j, s: (s, 0)), pl.BlockSpec((tt, tn), lambda j, s: (s, j))],
        out_specs=out_specs, out_shape=out_shape,
        compiler_params=_params(("parallel", "arbitrary")),
    )(a, b)


def dgrad_multi(dps, wts, *, t_all, t_lat, tm):
    steps = []
    tks = []
    for g, dp in enumerate(dps):
        width = dp.shape[1]
        tk = width if width % 1024 else 1024
        tks.append(tk)
        steps += [(g, s) for s in range(width // tk)]
    n_steps = len(steps)
    lo = [min(i for i, (g, _) in enumerate(steps) if g == gg) for gg in range(len(dps))]
    cnt = [sum(1 for (g, _) in steps if g == gg) for gg in range(len(dps))]
    n_lat = t_lat // tm
    n_g = len(dps)

    def body(*refs):
        dp_refs, w_refs, o_ref = refs[:n_g], refs[n_g:2 * n_g], refs[2 * n_g]
        i = pl.program_id(0)
        s = pl.program_id(1)

        @pl.when(s == 0)
        def _():
            o_ref[...] = jnp.zeros_like(o_ref)

        for g in range(n_g):
            in_rows = dps[g].shape[0] == t_all
            cond = (s >= lo[g]) & (s < lo[g] + cnt[g])
            if not in_rows:
                cond = cond & (i < n_lat)

            @pl.when(cond)
            def _(g=g):
                o_ref[...] += _nt(dp_refs[g][...], w_refs[g][...])

    in_specs = []
    for g, dp in enumerate(dps):
        nrow = dp.shape[0] // tm
        in_specs.append(pl.BlockSpec(
            (tm, tks[g]), lambda i, s, g=g, nrow=nrow: (jnp.minimum(i, nrow - 1), jnp.clip(s - lo[g], 0, cnt[g] - 1))))
    for g, w in enumerate(wts):
        in_specs.append(pl.BlockSpec((D, tks[g]), lambda i, s, g=g: (0, jnp.clip(s - lo[g], 0, cnt[g] - 1))))
    return _pallas(
        body, name="dgrad_w_in", grid=(t_all // tm, n_steps), in_specs=in_specs,
        out_specs=pl.BlockSpec((tm, D), lambda i, s: (i, 0)),
        out_shape=jax.ShapeDtypeStruct((t_all, D), F32),
        compiler_params=_params(("parallel", "arbitrary")),
    )(*dps, *wts)


TM_NORM = 512


def norm_mod_fwd(x2, ctx2, scale1, shift, norm_g):
    t = x2.shape[0]
    n_lat = t // TM_NORM
    assert ctx2.shape[0] == TM_NORM
    n_samples = scale1.shape[0] - 1
    tps = n_lat // n_samples

    def body(x_ref, c_ref, sc_ref, sh_ref, g_ref, u_ref):
        i = pl.program_id(0)
        xv = jnp.where(i < n_lat, x_ref[...], c_ref[...])
        rs = lax.rsqrt(jnp.mean(xv * xv, axis=-1, keepdims=True) + EPS)
        u = xv * rs * g_ref[...] * sc_ref[0] + sh_ref[0]
        u_ref[...] = u.astype(u_ref.dtype)

    grp = lambda i: (jnp.minimum(i // tps, n_samples), 0, 0)
    return _pallas(
        body, name="norm_mod_fwd", grid=(n_lat + 1,),
        in_specs=[pl.BlockSpec((TM_NORM, D), lambda i: (jnp.minimum(i, n_lat - 1), 0)),
                  pl.BlockSpec((TM_NORM, D), lambda i: (0, 0)),
                  pl.BlockSpec((1, 1, D), grp), pl.BlockSpec((1, 1, D), grp),
                  pl.BlockSpec((1, D), lambda i: (0, 0))],
        out_specs=pl.BlockSpec((TM_NORM, D), lambda i: (i, 0)),
        out_shape=jax.ShapeDtypeStruct((t + TM_NORM, D), BF16),
        compiler_params=_params(("parallel",)),
    )(x2, ctx2, scale1, shift, norm_g)


def norm_mod_bwd(x2, ctx2, du, dh, scale1, norm_g):
    t = x2.shape[0]
    n_lat = t // TM_NORM
    n_samples = scale1.shape[0] - 1
    tps = n_lat // n_samples
    n_grp = n_samples + 1

    def body(x_ref, c_ref, du_ref, dh_ref, sc_ref, g_ref, dx_ref, dsh_ref, dsc_ref, dg_ref):
        i = pl.program_id(0)
        xv = jnp.where(i < n_lat, x_ref[...], c_ref[...])
        rs = lax.rsqrt(jnp.mean(xv * xv, axis=-1, keepdims=True) + EPS)
        xh = xv * rs
        duv = du_ref[...]
        n = xh * g_ref[...]
        dn = duv * sc_ref[0]
        dxh = dn * g_ref[...]
        dx = rs * (dxh - xh * jnp.mean(dxh * xh, axis=-1, keepdims=True))
        @pl.when(i < n_lat)
        def _():
            dx_ref[...] = dx + dh_ref[...]

        @pl.when(i % tps == 0)
        def _():
            dsh_ref[...] = jnp.zeros_like(dsh_ref)
            dsc_ref[...] = jnp.zeros_like(dsc_ref)

        @pl.when(i == 0)
        def _():
            dg_ref[...] = jnp.zeros_like(dg_ref)

        dsh_ref[0] += jnp.sum(duv, axis=0, keepdims=True)
        dsc_ref[0] += jnp.sum(duv * n, axis=0, keepdims=True)
        dg_ref[...] += jnp.sum(dn * xh, axis=0, keepdims=True)

    grp = lambda i: (jnp.minimum(i // tps, n_samples), 0, 0)
    lat = lambda i: (jnp.minimum(i, n_lat - 1), 0)
    return _pallas(
        body, name="norm_mod_bwd", grid=(n_lat + 1,),
        in_specs=[pl.BlockSpec((TM_NORM, D), lat),
                  pl.BlockSpec((TM_NORM, D), lambda i: (0, 0)),
                  pl.BlockSpec((TM_NORM, D), lambda i: (i, 0)),
                  pl.BlockSpec((TM_NORM, D), lat),
                  pl.BlockSpec((1, 1, D), grp),
                  pl.BlockSpec((1, D), lambda i: (0, 0))],
        out_specs=[pl.BlockSpec((TM_NORM, D), lat),
                   pl.BlockSpec((1, 1, D), grp), pl.BlockSpec((1, 1, D), grp),
                   pl.BlockSpec((1, D), lambda i: (0, 0))],
        out_shape=[jax.ShapeDtypeStruct((t, D), F32),
                   jax.ShapeDtypeStruct((n_grp, 1, D), F32), jax.ShapeDtypeStruct((n_grp, 1, D), F32),
                   jax.ShapeDtypeStruct((1, D), F32)],
        compiler_params=_params(("arbitrary",)),
    )(x2, ctx2, du, dh, scale1, norm_g)


CONV_CB = 256
CONV_NCB = D // CONV_CB
H_OFF = 16


def _conv_pad_shape(vertical):
    if vertical:
        return (GRID_H + 2 * CONV_PAD, GRID_W, CONV_CB)
    return (GRID_H, GRID_W + 2 * H_OFF, CONV_CB)


def _conv_store(pad_ref, img, vertical):
    if vertical:
        pad_ref[pl.ds(CONV_PAD, GRID_H)] = img
    else:
        pad_ref[:, pl.ds(H_OFF, GRID_W), :] = img


def _conv_window(pad_ref, k, vertical, r):
    if vertical:
        return pad_ref[r + k]
    return pad_ref[r, pl.ds(H_OFF - CONV_PAD + k, GRID_W), :]


def _rows(r):
    return pl.ds(pl.multiple_of(r * GRID_W, GRID_W), GRID_W)


def conv_fwd(p1, conv_w, conv_b, n_samples):
    t = n_samples * SEQ

    def make(vertical, prev):
        def body(gv_ref, gg_ref, w_ref, b_ref, *rest):
            o_ref, pad_ref = rest[-2], rest[-1]
            pad_ref[...] = jnp.zeros_like(pad_ref)
            a = gv_ref[...] * _sigmoid(gg_ref[...])
            _conv_store(pad_ref, a.reshape(GRID_H, GRID_W, CONV_CB), vertical)

            def row(r, carry):
                acc = jnp.zeros((GRID_W, CONV_CB), F32) + b_ref[...]
                for k in range(CONV_K):
                    acc = acc + _conv_window(pad_ref, k, vertical, r) * w_ref[pl.ds(k, 1), :]
                o_ref[_rows(r), :] = acc
                return carry

            lax.fori_loop(0, GRID_H, row, 0)

        cb0 = CONV_NCB // 2 if vertical else 0
        in_specs = [pl.BlockSpec((SEQ, CONV_CB), lambda b, j: (b, 2 * (cb0 + j))),
                    pl.BlockSpec((SEQ, CONV_CB), lambda b, j: (b, 2 * (cb0 + j) + 1)),
                    pl.BlockSpec((CONV_K + 1, CONV_CB), lambda b, j: (0, cb0 + j)),
                    pl.BlockSpec((1, CONV_CB), lambda b, j: (0, cb0 + j))]
        args = [p1, p1, conv_w, conv_b]
        aliases = {}
        if prev is not None:
            in_specs.append(pl.BlockSpec(memory_space=pl.ANY))
            args.append(prev)
            aliases = {4: 0}
        return _pallas(
            body, name="conv_fwd_v" if vertical else "conv_fwd_h", grid=(n_samples, CONV_NCB // 2),
            in_specs=in_specs,
            out_specs=pl.BlockSpec((SEQ, CONV_CB), lambda b, j: (b, cb0 + j)),
            out_shape=jax.ShapeDtypeStruct((t, D), F32),
            scratch_shapes=[pltpu.VMEM(_conv_pad_shape(vertical), F32)],
            input_output_aliases=aliases,
            compiler_params=_params(("parallel", "parallel")),
        )(*args)

    return make(True, make(False, None))


def conv_bwd(p1, daconv, conv_w, n_samples):
    t = n_samples * SEQ

    def make(vertical, prev):
        def body(gv_ref, gg_ref, dy_ref, w_ref, *rest):
            dp_ref, dw_ref, db_ref, pad_ref, dpad_ref, da_ref = rest[-6:]
            pad_ref[...] = jnp.zeros_like(pad_ref)
            dpad_ref[...] = jnp.zeros_like(dpad_ref)
            _conv_store(pad_ref, (gv_ref[...] * _sigmoid(gg_ref[...])).reshape(GRID_H, GRID_W, CONV_CB), vertical)
            _conv_store(dpad_ref, dy_ref[...].reshape(GRID_H, GRID_W, CONV_CB), vertical)

            def row(r, carry):
                acc = jnp.zeros((GRID_W, CONV_CB), F32)
                for k in range(CONV_K):
                    acc = acc + _conv_window(dpad_ref, CONV_K - 1 - k, vertical, r) * w_ref[pl.ds(k, 1), :]
                da_ref[_rows(r), :] = acc
                return carry

            lax.fori_loop(0, GRID_H, row, 0)
            da = da_ref[...]
            gv = gv_ref[...]
            sg = _sigmoid(gg_ref[...])
            dp_ref[:, pl.ds(0, CONV_CB)] = (da * sg).astype(dp_ref.dtype)
            dp_ref[:, pl.ds(CONV_CB, CONV_CB)] = (da * gv * sg * (1.0 - sg)).astype(dp_ref.dtype)

            for k in range(CONV_K):
                def wrow(r, acc, k=k):
                    return acc + _conv_window(pad_ref, k, vertical, r) * dy_ref[_rows(r), :]
                acc = lax.fori_loop(0, GRID_H, wrow, jnp.zeros((GRID_W, CONV_CB), F32))
                dw_ref[0, pl.ds(k, 1), :] = jnp.sum(acc, axis=0, keepdims=True)
            dw_ref[0, pl.ds(CONV_K, 1), :] = jnp.zeros((1, CONV_CB), F32)
            db_ref[0] = jnp.sum(dy_ref[...], axis=0, keepdims=True)

        cb0 = CONV_NCB // 2 if vertical else 0
        in_specs = [pl.BlockSpec((SEQ, CONV_CB), lambda b, j: (b, 2 * (cb0 + j))),
                    pl.BlockSpec((SEQ, CONV_CB), lambda b, j: (b, 2 * (cb0 + j) + 1)),
                    pl.BlockSpec((SEQ, CONV_CB), lambda b, j: (b, cb0 + j)),
                    pl.BlockSpec((CONV_K + 1, CONV_CB), lambda b, j: (0, cb0 + j))]
        args = [p1, p1, daconv, conv_w]
        aliases = {}
        if prev is not None:
            in_specs += [pl.BlockSpec(memory_space=pl.ANY)] * 3
            args += list(prev)
            aliases = {4: 0, 5: 1, 6: 2}
        return _pallas(
            body, name="conv_bwd_v" if vertical else "conv_bwd_h", grid=(n_samples, CONV_NCB // 2),
            in_specs=in_specs,
            out_specs=[pl.BlockSpec((SEQ, 2 * CONV_CB), lambda b, j: (b, cb0 + j)),
                       pl.BlockSpec((1, CONV_K + 1, CONV_CB), lambda b, j: (b, 0, cb0 + j)),
                       pl.BlockSpec((1, 1, CONV_CB), lambda b, j: (b, 0, cb0 + j))],
            out_shape=[jax.ShapeDtypeStruct((t, 2 * D), BF16),
                       jax.ShapeDtypeStruct((n_samples, CONV_K + 1, D), F32),
                       jax.ShapeDtypeStruct((n_samples, 1, D), F32)],
            scratch_shapes=[pltpu.VMEM(_conv_pad_shape(vertical), F32), pltpu.VMEM(_conv_pad_shape(vertical), F32),
                            pltpu.VMEM((SEQ, CONV_CB), F32)],
            input_output_aliases=aliases,
            compiler_params=_params(("parallel", "parallel")),
        )(*args)

    return make(True, make(False, None))


TM_EW = 256


def ln_gate_fwd(aconv, z, ln_g, ln_b):
    t = aconv.shape[0]

    def body(a_ref, z_ref, g_ref, b_ref, o_ref):
        a = a_ref[...]
        mu = jnp.mean(a, axis=-1, keepdims=True)
        xc = a - mu
        rstd = lax.rsqrt(jnp.mean(xc * xc, axis=-1, keepdims=True) + EPS)
        l = xc * rstd * g_ref[...] + b_ref[...]
        o_ref[...] = (_silu(l) * _silu(z_ref[...])).astype(o_ref.dtype)

    row = pl.BlockSpec((TM_EW, D), lambda i: (i, 0))
    vec = pl.BlockSpec((1, D), lambda i: (0, 0))
    return _pallas(
        body, name="ln_gate_fwd", grid=(t // TM_EW,), in_specs=[row, row, vec, vec], out_specs=row,
        out_shape=jax.ShapeDtypeStruct((t, D), BF16), compiler_params=_params(("parallel",)),
    )(aconv, z, ln_g, ln_b)


def ln_gate_bwd(aconv, z, dac, ln_g, ln_b):
    t = aconv.shape[0]

    def body(a_ref, z_ref, d_ref, g_ref, b_ref, da_ref, dz_ref, dg_ref, db_ref):
        a = a_ref[...]
        zv = z_ref[...]
        dac_v = d_ref[...]
        mu = jnp.mean(a, axis=-1, keepdims=True)
        xc = a - mu
        rstd = lax.rsqrt(jnp.mean(xc * xc, axis=-1, keepdims=True) + EPS)
        xh = xc * rstd
        l = xh * g_ref[...] + b_ref[...]
        dz_ref[...] = (dac_v * _silu(l) * _dsilu(zv)).astype(dz_ref.dtype)
        dl = dac_v * _silu(zv) * _dsilu(l)
        dxh = dl * g_ref[...]
        da_ref[...] = rstd * (dxh - jnp.mean(dxh, axis=-1, keepdims=True)
                              - xh * jnp.mean(dxh * xh, axis=-1, keepdims=True))

        @pl.when(pl.program_id(0) == 0)
        def _():
            dg_ref[...] = jnp.zeros_like(dg_ref)
            db_ref[...] = jnp.zeros_like(db_ref)

        dg_ref[...] += jnp.sum(dl * xh, axis=0, keepdims=True)
        db_ref[...] += jnp.sum(dl, axis=0, keepdims=True)

    row = pl.BlockSpec((TM_EW, D), lambda i: (i, 0))
    vec = pl.BlockSpec((1, D), lambda i: (0, 0))
    return _pallas(
        body, name="ln_gate_bwd", grid=(t // TM_EW,), in_specs=[row, row, row, vec, vec],
        out_specs=[row, row, vec, vec],
        out_shape=[jax.ShapeDtypeStruct((t, D), F32), jax.ShapeDtypeStruct((t, D), BF16),
                   jax.ShapeDtypeStruct((1, D), F32), jax.ShapeDtypeStruct((1, D), F32)],
        compiler_params=_params(("arbitrary",)),
    )(aconv, z, dac, ln_g, ln_b)


TM_PREP = 256
PREP_LAT = SEQ // TM_PREP
PREP_ALL = SEQ_ALL // TM_PREP


def _chunk_tri(n, upper):
    r = lax.broadcasted_iota(jnp.int32, (n, n), 0)
    c = lax.broadcasted_iota(jnp.int32, (n, n), 1)
    same = (r // CHUNK) == (c // CHUNK)
    keep = (c >= r) if upper else (c <= r)
    return jnp.where(same & keep, 1.0, 0.0).astype(F32)


def _prep_tile_maps(n_samples):
    n_lat = n_samples * PREP_LAT

    def seq_map(i):
        return jnp.where(i < n_lat, i // PREP_LAT, i - n_lat), jnp.where(i < n_lat, i % PREP_LAT, PREP_LAT)

    return n_lat, seq_map


def gla_prep_fwd(p3, upf, upb, bias_f, bias_b, n_samples):
    n_lat, seq_map = _prep_tile_maps(n_samples)
    n_tiles = n_lat + n_samples

    def body(v_ref, q_ref, k_ref, ab_ref, upf_ref, upb_ref, bf_ref, bb_ref, qo, ko, vo, cf, cb):
        i = pl.program_id(0)
        qo[0] = jnp.where(i < n_lat, q_ref[...] * Q_SCALE, 0.0)
        ko[0] = k_ref[...]
        vo[0] = v_ref[...]
        ab = ab_ref[...]
        gf = _log_sigmoid(_nn(ab, upf_ref[...], HI) + bf_ref[...]) * (1.0 / GATE_TAU)
        gb = _log_sigmoid(_nn(ab, upb_ref[...], HI) + bb_ref[...]) * (1.0 / GATE_TAU)
        cf[0] = _nn(_chunk_tri(TM_PREP, False), gf, HI)
        cb[0] = _nn(_chunk_tri(TM_PREP, True), gb, HI)

    def o_spec(w):
        return pl.BlockSpec((1, TM_PREP, w), lambda i: (*seq_map(i), 0))

    full = lambda shape: pl.BlockSpec(shape, lambda i: (0,) * len(shape))
    return _pallas(
        body, name="gla_prep_fwd", grid=(n_tiles,),
        in_specs=[pl.BlockSpec((TM_PREP, 1024), lambda i: (i, O3_V // 1024)),
                  pl.BlockSpec((TM_PREP, 512), lambda i: (i, O3_Q // 512)),
                  pl.BlockSpec((TM_PREP, 512), lambda i: (i, O3_K // 512)),
                  pl.BlockSpec((TM_PREP, 128), lambda i: (i, O3_AB // 128)),
                  full((128, GLA_DK)), full((128, GLA_DK)), full((1, GLA_DK)), full((1, GLA_DK))],
        out_specs=[o_spec(GLA_DK), o_spec(GLA_DK), o_spec(D), o_spec(GLA_DK), o_spec(GLA_DK)],
        out_shape=[jax.ShapeDtypeStruct((n_samples, SEQ_ALL, GLA_DK), F32),
                   jax.ShapeDtypeStruct((n_samples, SEQ_ALL, GLA_DK), F32),
                   jax.ShapeDtypeStruct((n_samples, SEQ_ALL, D), F32),
                   jax.ShapeDtypeStruct((n_samples, SEQ_ALL, GLA_DK), F32),
                   jax.ShapeDtypeStruct((n_samples, SEQ_ALL, GLA_DK), F32)],
        compiler_params=_params(("parallel",)),
    )(p3, p3, p3, p3, upf, upb, bias_f, bias_b)


def gla_prep_bwd(p3, dq_f, dq_b, dk_f, dk_b, dv_f, dv_b, dc_f, dc_b, upf, upb, bias_f, bias_b, n_samples):
    n_lat, seq_map = _prep_tile_maps(n_samples)
    n_tiles = n_lat + n_samples

    def body(ab_ref, dqf, dqb, dkf, dkb, dvf, dvb, dcf, dcb, upf_ref, upb_ref, bf_ref, bb_ref,
             dp_ref, duf_ref, dub_ref, dbf_ref, dbb_ref):
        i = pl.program_id(0)
        dp_ref[:, pl.ds(O3_V, D)] = (dvf[0] + dvb[0]).astype(dp_ref.dtype)
        dq = jnp.where(i < n_lat, (dqf[0] + dqb[0]) * Q_SCALE, 0.0)
        dp_ref[:, pl.ds(O3_Q, GLA_DK)] = dq.astype(dp_ref.dtype)
        dp_ref[:, pl.ds(O3_K, GLA_DK)] = (dkf[0] + dkb[0]).astype(dp_ref.dtype)
        ab = ab_ref[...]
        zf = _nn(ab, upf_ref[...], HI) + bf_ref[...]
        zb = _nn(ab, upb_ref[...], HI) + bb_ref[...]
        dgf = _nn(_chunk_tri(TM_PREP, True), dcf[0], HI)
        dgb = _nn(_chunk_tri(TM_PREP, False), dcb[0], HI)
        dzf = dgf * (1.0 / GATE_TAU) * _sigmoid(-zf)
        dzb = dgb * (1.0 / GATE_TAU) * _sigmoid(-zb)
        dab = _nt(dzf, upf_ref[...], HI) + _nt(dzb, upb_ref[...], HI)
        dp_ref[:, pl.ds(O3_AB, 128)] = dab.astype(dp_ref.dtype)

        @pl.when(i == 0)
        def _():
            duf_ref[...] = jnp.zeros_like(duf_ref)
            dub_ref[...] = jnp.zeros_like(dub_ref)
            dbf_ref[...] = jnp.zeros_like(dbf_ref)
            dbb_ref[...] = jnp.zeros_like(dbb_ref)

        duf_ref[...] += _tn(ab, dzf, HI)
        dub_ref[...] += _tn(ab, dzb, HI)
        dbf_ref[...] += jnp.sum(dzf, axis=0, keepdims=True)
        dbb_ref[...] += jnp.sum(dzb, axis=0, keepdims=True)

    def s_spec(w):
        return pl.BlockSpec((1, TM_PREP, w), lambda i: (*seq_map(i), 0))

    full = lambda shape: pl.BlockSpec(shape, lambda i: (0,) * len(shape))
    return _pallas(
        body, name="gla_prep_bwd", grid=(n_tiles,),
        in_specs=[pl.BlockSpec((TM_PREP, 128), lambda i: (i, O3_AB // 128)),
                  s_spec(GLA_DK), s_spec(GLA_DK), s_spec(GLA_DK), s_spec(GLA_DK), s_spec(D), s_spec(D),
                  s_spec(GLA_DK), s_spec(GLA_DK),
                  full((128, GLA_DK)), full((128, GLA_DK)), full((1, GLA_DK)), full((1, GLA_DK))],
        out_specs=[pl.BlockSpec((TM_PREP, W3), lambda i: (i, 0)),
                   full((128, GLA_DK)), full((128, GLA_DK)), full((1, GLA_DK)), full((1, GLA_DK))],
        out_shape=[jax.ShapeDtypeStruct((n_tiles * TM_PREP, W3), BF16),
                   jax.ShapeDtypeStruct((128, GLA_DK), F32), jax.ShapeDtypeStruct((128, GLA_DK), F32),
                   jax.ShapeDtypeStruct((1, GLA_DK), F32), jax.ShapeDtypeStruct((1, GLA_DK), F32)],
        compiler_params=_params(("arbitrary",)),
    )(p3, dq_f, dq_b, dk_f, dk_b, dv_f, dv_b, dc_f, dc_b, upf, upb, bias_f, bias_b)


def _scan_chunk(i, rev):
    if rev:
        return NCHUNK - 1 - i
    return jnp.where(i < NCHUNK_CTX, NCHUNK_LAT + i, i - NCHUNK_CTX)


def _sub_blocks(rev):
    out = []
    for s in range(NSUB):
        rows = (s * SUB, SUB)
        if rev:
            ref = (s + 1) * SUB if s < NSUB - 1 else None
            cols = (s * SUB, CHUNK - s * SUB)
        else:
            ref = s * SUB - 1 if s > 0 else None
            cols = (0, (s + 1) * SUB)
        out.append((rows, ref, cols))
    return out


def _sub_mask(rows, cols, rev):
    r = rows[0] + lax.broadcasted_iota(jnp.int32, (rows[1], cols[1]), 0)
    c = cols[0] + lax.broadcasted_iota(jnp.int32, (rows[1], cols[1]), 1)
    return (c >= r) if rev else (c <= r)


def _sub_operands(qc, kc, cc, rows, ref, cols):
    cref = jnp.zeros((1, HEAD_K), F32) if ref is None else cc[ref:ref + 1]
    eq = jnp.exp(cc[rows[0]:rows[0] + rows[1]] - cref)
    ek = jnp.exp(cref - cc[cols[0]:cols[0] + cols[1]])
    qs = qc[rows[0]:rows[0] + rows[1]] * eq
    kk = kc[cols[0]:cols[0] + cols[1]] * ek
    return qs, kk, eq, ek


def gla_scan_fwd(q, k, v, cum, *, rev, name):
    n = q.shape[0]

    def body(q_ref, k_ref, v_ref, c_ref, o_ref, s_ref, st):
        st[...] = jnp.zeros_like(st)

        def step(i, carry):
            ci = _scan_chunk(i, rev)
            rws = pl.ds(pl.multiple_of(ci * CHUNK, CHUNK), CHUNK)
            qc, kc, vc, cc = q_ref[0, rws, :], k_ref[0, rws, :], v_ref[0, rws, :], c_ref[0, rws, :]
            s_in = st[...]
            s_ref[0, 0, i] = s_in
            edge = cc[0:1] if rev else cc[CHUNK - 1:CHUNK]
            ke = kc * jnp.exp(edge - cc)
            st[...] = s_in * jnp.exp(edge) + _tn(_b16(vc), _b16(ke))

            @pl.when(ci < NCHUNK_LAT)
            def _():
                o_inter = _nt(_b16(qc * jnp.exp(cc)), _b16(s_in))
                vb = _b16(vc)
                for rows, ref, cols in _sub_blocks(rev):
                    qs, kk, _, _ = _sub_operands(qc, kc, cc, rows, ref, cols)
                    a = jnp.where(_sub_mask(rows, cols, rev), _nt(_b16(qs), _b16(kk)), 0.0)
                    o_s = _nn(_b16(a), vb[cols[0]:cols[0] + cols[1]])
                    o_ref[0, pl.ds(pl.multiple_of(ci * CHUNK, CHUNK) + rows[0], rows[1]), :] = (
                        o_inter[rows[0]:rows[0] + rows[1]] + o_s)
            return carry

        lax.fori_loop(0, NCHUNK, step, 0)
        s_ref[0, 0, NCHUNK] = st[...]

    kspec = pl.BlockSpec((1, SEQ_ALL, HEAD_K), lambda b, h: (b, 0, h))
    return _pallas(
        body, name=name, grid=(n, HEADS),
        in_specs=[kspec, kspec, pl.BlockSpec((1, SEQ_ALL, HEAD_V), lambda b, h: (b, 0, h)), kspec],
        out_specs=[pl.BlockSpec((1, SEQ, HEAD_V), lambda b, h: (b, 0, h)),
                   pl.BlockSpec((1, 1, NCHUNK + 1, HEAD_V, HEAD_K), lambda b, h: (b, h, 0, 0, 0))],
        out_shape=[jax.ShapeDtypeStruct((n, SEQ, D), F32),
                   jax.ShapeDtypeStruct((n, HEADS, NCHUNK + 1, HEAD_V, HEAD_K), F32)],
        scratch_shapes=[pltpu.VMEM((HEAD_V, HEAD_K), F32)],
        compiler_params=_params(("parallel", "parallel")),
    )(q, k, v, cum)


def gla_scan_bwd(q, k, v, cum, s_all, do, *, rev, name):
    n = q.shape[0]

    def body(q_ref, k_ref, v_ref, c_ref, s_ref, do_ref, dq_ref, dk_ref, dv_ref, dc_ref, dst, dq_acc, dk_acc, dv_acc):
        dst[...] = jnp.zeros_like(dst)

        def step(j, carry):
            i = NCHUNK - 1 - j
            ci = _scan_chunk(i, rev)
            rws = pl.ds(pl.multiple_of(ci * CHUNK, CHUNK), CHUNK)
            qc, kc, vc, cc = q_ref[0, rws, :], k_ref[0, rws, :], v_ref[0, rws, :], c_ref[0, rws, :]
            lat = ci < NCHUNK_LAT
            do_rows = pl.ds(pl.multiple_of(jnp.minimum(ci, NCHUNK_LAT - 1) * CHUNK, CHUNK), CHUNK)
            doc = jnp.where(lat, do_ref[0, do_rows, :], 0.0)
            s_in = s_ref[0, 0, i]
            s_out = s_ref[0, 0, i + 1]
            ds_out = dst[...]
            edge = cc[0:1] if rev else cc[CHUNK - 1:CHUNK]
            e_q = jnp.exp(cc)
            e_k = jnp.exp(edge - cc)
            dob = _b16(doc)
            dsb = _b16(ds_out)
            dst[...] = ds_out * jnp.exp(edge) + _tn(dob, _b16(qc * e_q))
            dq_acc[...] = e_q * _nn(dob, _b16(s_in))
            dk_acc[...] = e_k * _nn(_b16(vc), dsb)
            dv_acc[...] = _nt(_b16(kc * e_k), dsb)
            vb = _b16(vc)
            for rows, ref, cols in _sub_blocks(rev):
                qs, kk, eq, ek = _sub_operands(qc, kc, cc, rows, ref, cols)
                mask = _sub_mask(rows, cols, rev)
                rsl = slice(rows[0], rows[0] + rows[1])
                csl = pl.ds(cols[0], cols[1])
                qsb, kkb = _b16(qs), _b16(kk)
                a = jnp.where(mask, _nt(qsb, kkb), 0.0)
                da = _b16(jnp.where(mask, _nt(dob[rsl], vb[cols[0]:cols[0] + cols[1]]), 0.0))
                dq_acc[pl.ds(rows[0], rows[1]), :] += _nn(da, kkb) * eq
                dk_acc[csl, :] += _tn(da, qsb) * ek
                dv_acc[csl, :] += _tn(_b16(a), dob[rsl])
            dq = dq_acc[...]
            dk = dk_acc[...]
            dc = qc * dq - kc * dk
            bnd = jnp.sum(ds_out * s_out, axis=0, keepdims=True)
            edge_row = 0 if rev else CHUNK - 1
            is_edge = lax.broadcasted_iota(jnp.int32, (CHUNK, HEAD_K), 0) == edge_row
            dq_ref[0, rws, :] = dq
            dk_ref[0, rws, :] = dk
            dv_ref[0, rws, :] = dv_acc[...]
            dc_ref[0, rws, :] = dc + jnp.where(is_edge, bnd, 0.0)
            return carry

        lax.fori_loop(0, NCHUNK, step, 0)

    kspec = pl.BlockSpec((1, SEQ_ALL, HEAD_K), lambda b, h: (b, 0, h))
    vspec = pl.BlockSpec((1, SEQ_ALL, HEAD_V), lambda b, h: (b, 0, h))
    return _pallas(
        body, name=name, grid=(n, HEADS),
        in_specs=[kspec, kspec, vspec, kspec,
                  pl.BlockSpec((1, 1, NCHUNK + 1, HEAD_V, HEAD_K), lambda b, h: (b, h, 0, 0, 0)),
                  pl.BlockSpec((1, SEQ, HEAD_V), lambda b, h: (b, 0, h))],
        out_specs=[kspec, kspec, vspec, kspec],
        out_shape=[jax.ShapeDtypeStruct((n, SEQ_ALL, GLA_DK), F32), jax.ShapeDtypeStruct((n, SEQ_ALL, GLA_DK), F32),
                   jax.ShapeDtypeStruct((n, SEQ_ALL, D), F32), jax.ShapeDtypeStruct((n, SEQ_ALL, GLA_DK), F32)],
        scratch_shapes=[pltpu.VMEM((HEAD_V, HEAD_K), F32), pltpu.VMEM((CHUNK, HEAD_K), F32),
                        pltpu.VMEM((CHUNK, HEAD_K), F32), pltpu.VMEM((CHUNK, HEAD_V), F32)],
        compiler_params=_params(("parallel", "parallel")),
    )(q, k, v, cum, s_all, do)


def gla_out_fwd(o_f, o_b, r, gnorm):
    n = o_f.shape[0]
    tiles = SEQ // TM_EW

    def body(of_ref, ob_ref, r_ref, g_ref, og_ref):
        for h in range(HEADS):
            cols = pl.ds(h * HEAD_V, HEAD_V)
            o = of_ref[0, :, cols] + ob_ref[0, :, cols]
            rs = lax.rsqrt(jnp.mean(o * o, axis=-1, keepdims=True) + EPS)
            og_ref[:, cols] = (o * rs * g_ref[...] * _silu(r_ref[:, cols])).astype(og_ref.dtype)

    ospec = pl.BlockSpec((1, TM_EW, D), lambda b, j: (b, j, 0))
    row = pl.BlockSpec((TM_EW, D), lambda b, j: (b * tiles + j, 0))
    return _pallas(
        body, name="gla_out_fwd", grid=(n, tiles),
        in_specs=[ospec, ospec, row, pl.BlockSpec((1, HEAD_V), lambda b, j: (0, 0))],
        out_specs=row, out_shape=jax.ShapeDtypeStruct((n * SEQ, D), BF16),
        compiler_params=_params(("parallel", "parallel")),
    )(o_f, o_b, r, gnorm)


def gla_out_bwd(o_f, o_b, r, dog, gnorm):
    n = o_f.shape[0]
    tiles = SEQ // TM_EW

    def body(of_ref, ob_ref, r_ref, d_ref, g_ref, do_ref, dr_ref, dg_ref):
        @pl.when((pl.program_id(0) == 0) & (pl.program_id(1) == 0))
        def _():
            dg_ref[...] = jnp.zeros_like(dg_ref)

        for h in range(HEADS):
            cols = pl.ds(h * HEAD_V, HEAD_V)
            o = of_ref[0, :, cols] + ob_ref[0, :, cols]
            rv = r_ref[:, cols]
            dv = d_ref[:, cols]
            rs = lax.rsqrt(jnp.mean(o * o, axis=-1, keepdims=True) + EPS)
            oh = o * rs
            dr_ref[:, cols] = (dv * oh * g_ref[...] * _dsilu(rv)).astype(dr_ref.dtype)
            dn = dv * _silu(rv)
            dg_ref[...] += jnp.sum(dn * oh, axis=0, keepdims=True)
            doh = dn * g_ref[...]
            do_ref[0, :, cols] = rs * (doh - oh * jnp.mean(doh * oh, axis=-1, keepdims=True))

    ospec = pl.BlockSpec((1, TM_EW, D), lambda b, j: (b, j, 0))
    row = pl.BlockSpec((TM_EW, D), lambda b, j: (b * tiles + j, 0))
    vec = pl.BlockSpec((1, HEAD_V), lambda b, j: (0, 0))
    return _pallas(
        body, name="gla_out_bwd", grid=(n, tiles),
        in_specs=[ospec, ospec, row, row, vec],
        out_specs=[ospec, row, vec],
        out_shape=[jax.ShapeDtypeStruct((n, SEQ, D), F32), jax.ShapeDtypeStruct((n * SEQ, D), BF16),
                   jax.ShapeDtypeStruct((1, HEAD_V), F32)],
        compiler_params=_params(("arbitrary", "arbitrary")),
    )(o_f, o_b, r, dog, gnorm)


def merge_fwd(p5, y_conv, y_gla):
    t = y_conv.shape[0]

    def body(mc_ref, mg_ref, yc_ref, yg_ref, o_ref):
        o_ref[...] = (_sigmoid(mc_ref[...]) * yc_ref[...] + _sigmoid(mg_ref[...]) * yg_ref[...]).astype(o_ref.dtype)

    row = pl.BlockSpec((TM_EW, D), lambda i: (i, 0))
    return _pallas(
        body, name="merge_fwd", grid=(t // TM_EW,),
        in_specs=[row, pl.BlockSpec((TM_EW, D), lambda i: (i, 1)), row, row], out_specs=row,
        out_shape=jax.ShapeDtypeStruct((t, D), BF16), compiler_params=_params(("parallel",)),
    )(p5, p5, y_conv, y_gla)


def merge_bwd(p5, y_conv, y_gla, dmerged):
    t = y_conv.shape[0]

    def body(mc_ref, mg_ref, yc_ref, yg_ref, d_ref, dyc_ref, dyg_ref, dp_ref):
        d = d_ref[...]
        sc = _sigmoid(mc_ref[...])
        sg = _sigmoid(mg_ref[...])
        dyc_ref[...] = (d * sc).astype(dyc_ref.dtype)
        dyg_ref[...] = (d * sg).astype(dyg_ref.dtype)
        dp_ref[:, pl.ds(0, D)] = (d * yc_ref[...] * sc * (1.0 - sc)).astype(dp_ref.dtype)
        dp_ref[:, pl.ds(D, D)] = (d * yg_ref[...] * sg * (1.0 - sg)).astype(dp_ref.dtype)

    row = pl.BlockSpec((TM_EW, D), lambda i: (i, 0))
    return _pallas(
        body, name="merge_bwd", grid=(t // TM_EW,),
        in_specs=[row, pl.BlockSpec((TM_EW, D), lambda i: (i, 1)), row, row, row],
        out_specs=[row, row, pl.BlockSpec((TM_EW, 2 * D), lambda i: (i, 0))],
        out_shape=[jax.ShapeDtypeStruct((t, D), BF16), jax.ShapeDtypeStruct((t, D), BF16),
                   jax.ShapeDtypeStruct((t, 2 * D), BF16)],
        compiler_params=_params(("parallel",)),
    )(p5, p5, y_conv, y_gla, dmerged)


def final_fwd_bwd(x2, mo, gate, final_g, target, n_samples):
    t = x2.shape[0]
    tiles = SEQ // TM_EW

    def body(x_ref, mo_ref, gate_ref, g_ref, t_ref, dh_ref, dmo_ref, dgate_ref, dg_ref, loss_ref):
        b, j = pl.program_id(0), pl.program_id(1)
        mo_v = mo_ref[...]
        h = x_ref[...] + gate_ref[0] * mo_v
        rs = lax.rsqrt(jnp.mean(h * h, axis=-1, keepdims=True) + EPS)
        nh = h * rs
        err = nh * g_ref[...] - t_ref[...]
        dy = err * (1.0 / D)
        dn = dy * g_ref[...]
        dh = rs * (dn - nh * jnp.mean(dn * nh, axis=-1, keepdims=True))
        dh_ref[...] = dh
        dmo_ref[...] = (dh * gate_ref[0]).astype(dmo_ref.dtype)

        @pl.when(j == 0)
        def _():
            dgate_ref[...] = jnp.zeros_like(dgate_ref)

        @pl.when((b == 0) & (j == 0))
        def _():
            dg_ref[...] = jnp.zeros_like(dg_ref)
            loss_ref[...] = jnp.zeros_like(loss_ref)

        dgate_ref[0] += jnp.sum(dh * mo_v, axis=0, keepdims=True)
        dg_ref[...] += jnp.sum(dy * nh, axis=0, keepdims=True)
        loss_ref[...] += (0.5 / D) * jnp.sum(err * err)

    row = pl.BlockSpec((TM_EW, D), lambda b, j: (b * tiles + j, 0))
    per = pl.BlockSpec((1, 1, D), lambda b, j: (b, 0, 0))
    vec = pl.BlockSpec((1, D), lambda b, j: (0, 0))
    return _pallas(
        body, name="final_fwd_bwd", grid=(n_samples, tiles),
        in_specs=[row, row, per, vec, row],
        out_specs=[row, row, per, vec, pl.BlockSpec((8, 128), lambda b, j: (0, 0))],
        out_shape=[jax.ShapeDtypeStruct((t, D), F32), jax.ShapeDtypeStruct((t, D), BF16),
                   jax.ShapeDtypeStruct((n_samples, 1, D), F32), jax.ShapeDtypeStruct((1, D), F32),
                   jax.ShapeDtypeStruct((8, 128), F32)],
        compiler_params=_params(("arbitrary", "arbitrary")),
    )(x2, mo, gate, final_g, target)


def local_step(x, ctx, target, mod, wts, small):
    n = x.shape[0]
    t = n * SEQ
    t_all = t + n * NCTX
    x2 = x.reshape(t, D)
    ctx2 = ctx.reshape(n * NCTX, D)
    tgt2 = target.reshape(t, D)
    scale1, shift, gate = mod

    u = norm_mod_fwd(x2, ctx2, scale1, shift, small["norm_g"])
    p1 = matmul_nn(u, wts["w1"], small["b1"], name="proj_conv", m=t, tm=512, tn=512, out_dtype=F32)
    p2 = matmul_nn(u, wts["w2"], small["b2"], name="proj_z", m=t, tm=512, tn=512, out_dtype=F32)
    p3 = matmul_nn(u, wts["w3"], small["b3"], name="proj_gla", m=t_all, tm=512, tn=W3, out_dtype=F32)
    p4 = matmul_nn(u, wts["w4"], small["b4"], name="proj_r", m=t, tm=512, tn=512, out_dtype=F32)
    p5 = matmul_nn(u, wts["w5"], small["b5"], name="proj_merge", m=t, tm=512, tn=512, out_dtype=F32)

    aconv = conv_fwd(p1, small["conv_w"], small["conv_b"], n)
    ac = ln_gate_fwd(aconv, p2, small["conv_ln_g"], small["conv_ln_b"])
    y_conv = matmul_nn(ac, wts["conv_proj"], None, name="conv_proj_fwd", m=t, tm=512, tn=512, out_dtype=F32)

    qs, ks, vs, cum_f, cum_b = gla_prep_fwd(p3, small["upf"], small["upb"], small["bias_f"], small["bias_b"], n)
    o_f, s_f = gla_scan_fwd(qs, ks, vs, cum_f, rev=False, name="gla_scan_fwd_f")
    o_b, s_b = gla_scan_fwd(qs, ks, vs, cum_b, rev=True, name="gla_scan_fwd_b")
    og = gla_out_fwd(o_f, o_b, p4, small["gla_norm_g"])
    y_gla = matmul_nn(og, wts["gla_proj"], None, name="gla_proj_fwd", m=t, tm=512, tn=512, out_dtype=F32)

    merged = merge_fwd(p5, y_conv, y_gla)
    mo = matmul_nn(merged, wts["w_out"], None, name="w_out_fwd", m=t, tm=512, tn=512, out_dtype=F32)
    dh, dmo, dgate, d_final_g, loss = final_fwd_bwd(x2, mo, gate, small["final_norm_g"], tgt2, n)

    g = {"final_norm_g": d_final_g}
    dmerged = matmul_nt(dmo, wts["w_out"], name="w_out_dgrad", tm=512)
    g["w_out"] = matmul_tn(merged, dmo, name="w_out_wgrad", t=t, tn=512, tt=512)[0]
    dyc, dyg, dp5 = merge_bwd(p5, y_conv, y_gla, dmerged)

    dac = matmul_nt(dyc, wts["conv_proj"], name="conv_proj_dgrad", tm=512)
    g["conv_proj"] = matmul_tn(ac, dyc, name="conv_proj_wgrad", t=t, tn=512, tt=512)[0]
    daconv, dp2, g["conv_ln_g"], g["conv_ln_b"] = ln_gate_bwd(aconv, p2, dac, small["conv_ln_g"], small["conv_ln_b"])
    dp1, dconv_w, dconv_b = conv_bwd(p1, daconv, small["conv_w"], n)
    g["conv_w"], g["conv_b"] = dconv_w, dconv_b

    dog = matmul_nt(dyg, wts["gla_proj"], name="gla_proj_dgrad", tm=512)
    g["gla_proj"] = matmul_tn(og, dyg, name="gla_proj_wgrad", t=t, tn=512, tt=512)[0]
    do, dp4, g["gla_norm_g"] = gla_out_bwd(o_f, o_b, p4, dog, small["gla_norm_g"])
    dq_f, dk_f, dv_f, dc_f = gla_scan_bwd(qs, ks, vs, cum_f, s_f, do, rev=False, name="gla_scan_bwd_f")
    dq_b, dk_b, dv_b, dc_b = gla_scan_bwd(qs, ks, vs, cum_b, s_b, do, rev=True, name="gla_scan_bwd_b")
    dp3, g["upf"], g["upb"], g["bias_f"], g["bias_b"] = gla_prep_bwd(
        p3, dq_f, dq_b, dk_f, dk_b, dv_f, dv_b, dc_f, dc_b,
        small["upf"], small["upb"], small["bias_f"], small["bias_b"], n)

    dps = [dp1, dp2, dp3, dp4, dp5]
    du = dgrad_multi(dps, [wts["w%d" % (i + 1)] for i in range(5)], t_all=t_all, t_lat=t, tm=256)
    for i, dp in enumerate(dps):
        rows = dp.shape[0]
        tn = W3 if dp.shape[1] == W3 else 512
        g["w%d" % (i + 1)], g["b%d" % (i + 1)] = matmul_tn(
            u, dp, name="w_in_wgrad_%d" % (i + 1), t=rows, tn=tn, tt=512, colsum=True)
    grad_x, dshift, dscale, g["norm_g"] = norm_mod_bwd(x2, ctx2, du, dh, scale1, small["norm_g"])
    g["shift"], g["scale"], g["gate"] = dshift, dscale, dgate
    return loss, grad_x, g


def _group_cols(w):
    gv, gg, z = w[..., 0:1024], w[..., 1024:2048], w[..., 2048:3072]
    q, k, v = w[..., 3072:3584], w[..., 3584:4096], w[..., 4096:5120]
    ab = w[..., 5120:5152]
    r, mc, mg = w[..., 5152:6176], w[..., 6176:7200], w[..., 7200:8224]
    g1 = jnp.concatenate([p for j in range(CONV_NCB)
                          for p in (gv[..., CONV_CB * j:CONV_CB * (j + 1)], gg[..., CONV_CB * j:CONV_CB * (j + 1)])], -1)
    pad = jnp.zeros(w.shape[:-1] + (W3 - 2080,), w.dtype)
    g3 = jnp.concatenate([v, q, k, ab, pad], -1)
    return g1, z, g3, r, jnp.concatenate([mc, mg], -1)


def _ungroup_cols(g1, g2, g3, g4, g5):
    gv = jnp.concatenate([g1[..., 2 * CONV_CB * j:2 * CONV_CB * j + CONV_CB] for j in range(CONV_NCB)], -1)
    gg = jnp.concatenate([g1[..., 2 * CONV_CB * j + CONV_CB:2 * CONV_CB * (j + 1)] for j in range(CONV_NCB)], -1)
    v, q, k, ab = g3[..., 0:1024], g3[..., 1024:1536], g3[..., 1536:2048], g3[..., 2048:2080]
    return jnp.concatenate([gv, gg, g2, q, k, v, ab, g4, g5[..., 0:1024], g5[..., 1024:2048]], -1)


def _pad_up(up, row0):
    return jnp.zeros((128, GLA_DK), F32).at[row0:row0 + up.shape[0]].set(up)


def _adamw_math(w, g, m, v):
    m = ADAM_B1 * m + (1.0 - ADAM_B1) * g
    v = ADAM_B2 * v + (1.0 - ADAM_B2) * (g * g)
    m_hat = m / (1.0 - ADAM_B1 ** ADAM_STEP)
    v_hat = v / (1.0 - ADAM_B2 ** ADAM_STEP)
    delta = -ADAM_LR * (m_hat / (jnp.sqrt(v_hat) + ADAM_EPS) + ADAM_WD * w)
    return delta, m, v


def adamw2d(w, g, m, v, *, name, tr):
    rows, cols = w.shape

    def body(w_ref, g_ref, m_ref, v_ref, d_ref, nm_ref, nv_ref):
        d_ref[...], nm_ref[...], nv_ref[...] = _adamw_math(w_ref[...], g_ref[...], m_ref[...], v_ref[...])

    spec = pl.BlockSpec((tr, cols), lambda i: (i, 0))
    return _pallas(
        body, name=name, grid=(rows // tr,), in_specs=[spec] * 4, out_specs=[spec] * 3,
        out_shape=[jax.ShapeDtypeStruct((rows, cols), F32)] * 3, compiler_params=_params(("parallel",)),
    )(w, g, m, v)


def sum_devices(sall):
    rows = sall.shape[1]

    def body(s_ref, o_ref):
        acc = s_ref[0]
        for d in range(1, N_DEV):
            acc = acc + s_ref[d]
        o_ref[...] = acc

    return _pallas(body, name="sum_devices", out_shape=jax.ShapeDtypeStruct((rows, D), F32),
                   compiler_params=_params())(sall)


def pair_add(core, g, got, *, name, tr):
    n, rows, cols = got.shape
    g4 = g.reshape(n, 2, rows, cols)

    def body(core_ref, g_ref, got_ref, o_ref, ob_ref):
        del core_ref
        s = g_ref[0, 0] + got_ref[0]
        o_ref[0] = s
        ob_ref[0] = s.astype(BF16)

    spec = pl.BlockSpec((1, tr, cols), lambda i, t, core_ref: (i, t, 0))
    return _pallas(
        body, name=name,
        grid_spec=pltpu.PrefetchScalarGridSpec(
            num_scalar_prefetch=1, grid=(n, rows // tr),
            in_specs=[pl.BlockSpec((1, 1, tr, cols), lambda i, t, core_ref: (i, core_ref[0], t, 0)), spec],
            out_specs=[spec, spec]),
        out_shape=[jax.ShapeDtypeStruct(got.shape, F32), jax.ShapeDtypeStruct(got.shape, BF16)],
        compiler_params=_params(("parallel", "parallel")))(core, g4, got)


def chip_add(chip, pa, rb, *, name, tr):
    _, rows, cols = pa.shape

    def body(chip_ref, m_ref, r_ref, o_ref):
        del chip_ref
        o_ref[...] = ((m_ref[0] + r_ref[0].astype(F32)) + r_ref[1].astype(F32)) + r_ref[2].astype(F32)

    return _pallas(
        body, name=name,
        grid_spec=pltpu.PrefetchScalarGridSpec(
            num_scalar_prefetch=1, grid=(rows // tr,),
            in_specs=[pl.BlockSpec((1, tr, cols), lambda t, chip_ref: (chip_ref[0], t, 0)),
                      pl.BlockSpec((3, tr, cols), lambda t, chip_ref: (0, t, 0))],
            out_specs=pl.BlockSpec((tr, cols), lambda t, chip_ref: (t, 0))),
        out_shape=jax.ShapeDtypeStruct((rows, cols), F32),
        compiler_params=_params(("parallel",)))(chip, pa, rb)


def ada_bwd(call, cctx_rows, dm_shard, dm_full, adaw):
    nsh = adaw.shape[1]

    def body(c_ref, cc_ref, dms_ref, dmf_ref, w_ref, gw_ref, gb_ref, pq_ref):
        a_lat = _silu(c_ref[...])
        a_ctx = _silu(cc_ref[...])
        dms = dms_ref[...]
        gw_ref[...] = _tn(a_lat, dms[0:64], HI) + _tn(a_ctx, dms[64:72], HI)
        gb_ref[...] = jnp.sum(dmf_ref[...], axis=0, keepdims=True)
        part = _nt(dms[64:72], w_ref[...], HI)
        pq_ref[...] = jnp.zeros_like(pq_ref) + jnp.sum(part, axis=0, keepdims=True)

    return _pallas(body, name="ada_bwd",
                   out_shape=[jax.ShapeDtypeStruct((D, nsh), F32), jax.ShapeDtypeStruct((1, 3 * D), F32),
                              jax.ShapeDtypeStruct((8, D), F32)],
                   compiler_params=_params())(call, cctx_rows, dm_shard, dm_full, adaw)


def cctx_grad(pq_all, cctx_rows):
    def body(p_ref, c_ref, o_ref):
        acc = p_ref[0]
        for qi in range(1, N_CHIPS):
            acc = acc + p_ref[qi]
        o_ref[...] = acc * _dsilu(c_ref[...])

    return _pallas(body, name="cctx_grad", out_shape=jax.ShapeDtypeStruct((8, D), F32),
                   compiler_params=_params())(pq_all, cctx_rows)


def _place():
    x, y, c = lax.axis_index("x"), lax.axis_index("y"), lax.axis_index("c")
    chips = [(1 - x, y), (x, 1 - y), (1 - x, 1 - y)]
    return x, y, c, chips


def _all_peers(x, y, c):
    return [((1 - x) if r & 4 else x, (1 - y) if r & 2 else y, (1 - c) if r & 1 else c) for r in range(1, N_DEV)]


def _remote(src, dst, send_sem, recv_sem, dev):
    return pltpu.make_async_remote_copy(src_ref=src, dst_ref=dst, send_sem=send_sem, recv_sem=recv_sem,
                                        device_id=dev, device_id_type=MESH)


ANY = pl.BlockSpec(memory_space=pl.ANY)
VMEM = pl.BlockSpec(memory_space=pltpu.VMEM)
F_ROWS = 16


W_ROW_CHUNKS = 4
P_ROW_CHUNKS = 2
N_BULK = W_ROW_CHUNKS + P_ROW_CHUNKS


def _half_chunks(core, n_rows, align):
    out = []
    for a, k in ((0, W_ROW_CHUNKS), (1, P_ROW_CHUNKS)):
        half = n_rows[a] // 2
        size = half // k
        for i in range(k):
            start = core * half + i * size
            out.append((a, pl.ds(start if isinstance(start, int) else pl.multiple_of(start, align), size)))
    return out


def gather_weights(c8, cctx8, adaw, adab, w_sh, p_sh, fp):
    nsh = adaw.shape[1]

    def body(c_ref, cctx_ref, adaw_ref, adab_ref, w_ref, p_ref, fp_ref, wall_ref, pall_ref, fall_ref, call_ref, mall_ref,
             abuf, w_send, w_recv, h_send, h_recv, c_send, c_recv, m_send, m_recv, f_send, f_recv, l_sem):
        x, y, c, chips = _place()
        q = 2 * x + y
        dev = 4 * x + 2 * y + c
        qs = [2 * cx + cy for cx, cy in chips]
        sib = (x, y, 1 - c)
        srcs, dsts = (w_ref, p_ref), (wall_ref, pall_ref)
        n_rows = (w_ref.shape[0], p_ref.shape[0])
        mine = _half_chunks(c, n_rows, 16)
        other = _half_chunks(1 - c, n_rows, 16)

        local = [pltpu.make_async_copy(srcs[a].at[rows], dsts[a].at[q, rows], l_sem.at[i])
                 for i, (a, rows) in enumerate(mine + other)]
        for cp in local:
            cp.start()
        bulk = [[_remote(srcs[a].at[rows], dsts[a].at[q, rows], w_send.at[j * N_BULK + i], w_recv.at[j * N_BULK + i],
                         (*chips[j], c)) for i, (a, rows) in enumerate(mine)] for j in range(3)]
        fall_ref[q] = fp_ref[...]
        small = [_remote(fp_ref, fall_ref.at[q], f_send.at[j], f_recv.at[j], (*chips[j], c)) for j in range(3)]
        my_rows = pl.ds(pl.multiple_of(8 * dev, 8), 8)
        call_ref[my_rows, :] = c_ref[...]
        cond = [_remote(c_ref, call_ref.at[my_rows, :], c_send.at[r], c_recv.at[r], peer)
                for r, peer in enumerate(_all_peers(x, y, c))]
        for cp in sum(bulk, []) + small + cond:
            cp.start()
        for cp in cond:
            cp.wait_recv()

        abuf[pl.ds(0, 64), :] = _silu(call_ref[...])
        abuf[pl.ds(64, 8), :] = _silu(cctx_ref[...])
        mall_ref[q] = _nn(abuf[...], adaw_ref[...], HI) + adab_ref[...]
        mod = [_remote(mall_ref.at[q], mall_ref.at[q], m_send.at[j], m_recv.at[j], (*chips[j], c)) for j in range(3)]
        for cp in mod:
            cp.start()

        handed = []
        for j in range(3):
            for i, (a, rows) in enumerate(mine):
                bulk[j][i].wait_recv()
                cp = _remote(dsts[a].at[qs[j], rows], dsts[a].at[qs[j], rows],
                             h_send.at[j * N_BULK + i], h_recv.at[j * N_BULK + i], sib)
                cp.start()
                handed.append(cp)
        for j in range(3):
            for i, (a, rows) in enumerate(other):
                _remote(dsts[a].at[qs[j], rows], dsts[a].at[qs[j], rows],
                        h_send.at[j * N_BULK + i], h_recv.at[j * N_BULK + i], sib).wait_recv()
        for cp in mod + small:
            cp.wait_recv()
        for cp in sum(bulk, []) + small + cond + mod + handed:
            cp.wait_send()
        for cp in local:
            cp.wait()

    def dma(n):
        return pltpu.SemaphoreType.DMA((n,))

    return _pallas(
        body, name="gather_weights",
        in_specs=[VMEM, VMEM, VMEM, VMEM, ANY, ANY, VMEM],
        out_specs=[ANY, ANY, VMEM, VMEM, VMEM],
        out_shape=[jax.ShapeDtypeStruct((N_CHIPS,) + w_sh.shape, BF16), jax.ShapeDtypeStruct((N_CHIPS,) + p_sh.shape, BF16),
                   jax.ShapeDtypeStruct((N_CHIPS, F_ROWS, D), F32),
                   jax.ShapeDtypeStruct((8 * N_DEV, D), F32), jax.ShapeDtypeStruct((N_CHIPS, MOD_ROWS, nsh), F32)],
        scratch_shapes=[pltpu.VMEM((MOD_ROWS, D), F32), dma(3 * N_BULK), dma(3 * N_BULK), dma(3 * N_BULK), dma(3 * N_BULK),
                        dma(7), dma(7), dma(3), dma(3), dma(3), dma(3), dma(2 * N_BULK)],
        compiler_params=_params(),
    )(c8, cctx8, adaw, adab, w_sh, p_sh, fp)


def gather_small_and_pair(sm, gw, gp):
    rows = sm.shape[0]
    n_pair = N_CHIPS * N_BULK

    def body(sm_ref, gw_ref, gp_ref, sall_ref, gotw_ref, gotp_ref, s_send, s_recv, a_send, a_recv):
        x, y, c, _ = _place()
        dev = 4 * x + 2 * y + c
        srcs, dsts = (gw_ref, gp_ref), (gotw_ref, gotp_ref)
        n_rows = (gw_ref.shape[1], gp_ref.shape[1])
        pair = []
        for i, ((a, rows_o), (_, rows_0)) in enumerate(zip(_half_chunks(1 - c, n_rows, 8), _half_chunks(0, n_rows, 8))):
            for s in range(N_CHIPS):
                k = s * N_BULK + i
                pair.append(_remote(srcs[a].at[s, rows_o], dsts[a].at[s, rows_0], a_send.at[k], a_recv.at[k], (x, y, 1 - c)))
        for cp in pair:
            cp.start()
        sall_ref[dev] = sm_ref[...]
        small = [_remote(sm_ref, sall_ref.at[dev], s_send.at[r], s_recv.at[r], peer)
                 for r, peer in enumerate(_all_peers(x, y, c))]
        for cp in small:
            cp.start()
        for cp in small + pair:
            cp.wait_recv()
        for cp in small + pair:
            cp.wait_send()

    return _pallas(
        body, name="gather_small_and_pair", in_specs=[VMEM, ANY, ANY], out_specs=[VMEM, ANY, ANY],
        out_shape=[jax.ShapeDtypeStruct((N_DEV, rows, D), F32),
                   jax.ShapeDtypeStruct((N_CHIPS, gw.shape[1] // 2, gw.shape[2]), F32),
                   jax.ShapeDtypeStruct((N_CHIPS, gp.shape[1] // 2, gp.shape[2]), F32)],
        scratch_shapes=[pltpu.SemaphoreType.DMA((7,)), pltpu.SemaphoreType.DMA((7,)),
                        pltpu.SemaphoreType.DMA((n_pair,)), pltpu.SemaphoreType.DMA((n_pair,))],
        compiler_params=_params(),
    )(sm, gw, gp)


def chip_exchange(paw, pap, pq):
    def body(paw_ref, pap_ref, pq_ref, rbw_ref, rbp_ref, pqa_ref, b_send, b_recv, p_send, p_recv):
        x, y, c, chips = _place()
        q = 2 * x + y
        qs = [2 * cx + cy for cx, cy in chips]
        srcs, dsts = (paw_ref, pap_ref), (rbw_ref, rbp_ref)
        n_rows = (2 * paw_ref.shape[1], 2 * pap_ref.shape[1])
        big = [_remote(srcs[a].at[qs[j], rows], dsts[a].at[j, rows], b_send.at[j * N_BULK + i], b_recv.at[j * N_BULK + i],
                       (*chips[j], c))
               for j in range(3) for i, (a, rows) in enumerate(_half_chunks(0, n_rows, 16))]
        pqa_ref[q] = pq_ref[...]
        small = [_remote(pq_ref, pqa_ref.at[q], p_send.at[j], p_recv.at[j], (*chips[j], c)) for j in range(3)]
        for cp in big + small:
            cp.start()
        for cp in small + big:
            cp.wait_recv()
        for cp in small + big:
            cp.wait_send()

    return _pallas(
        body, name="chip_exchange", in_specs=[ANY, ANY, VMEM], out_specs=[ANY, ANY, VMEM],
        out_shape=[jax.ShapeDtypeStruct((3,) + paw.shape[1:], BF16), jax.ShapeDtypeStruct((3,) + pap.shape[1:], BF16),
                   jax.ShapeDtypeStruct((N_CHIPS, 8, D), F32)],
        scratch_shapes=[pltpu.SemaphoreType.DMA((3 * N_BULK,)), pltpu.SemaphoreType.DMA((3 * N_BULK,)),
                        pltpu.SemaphoreType.DMA((3,)), pltpu.SemaphoreType.DMA((3,))],
        compiler_params=_params(),
    )(paw, pap, pq)


def pair_share(ghw, ghp):
    def body(ghw_ref, ghp_ref, outw_ref, outp_ref, send, recv, l_sem):
        x, y, c, _ = _place()
        srcs, dsts = (ghw_ref, ghp_ref), (outw_ref, outp_ref)
        n_rows = (2 * ghw_ref.shape[0], 2 * ghp_ref.shape[0])
        chunks = _half_chunks(0, n_rows, 8)
        local = [pltpu.make_async_copy(srcs[a].at[rows], dsts[a].at[c, rows], l_sem.at[i]) for i, (a, rows) in enumerate(chunks)]
        pair = [_remote(srcs[a].at[rows], dsts[a].at[c, rows], send.at[i], recv.at[i], (x, y, 1 - c))
                for i, (a, rows) in enumerate(chunks)]
        for cp in pair + local:
            cp.start()
        for cp in pair:
            cp.wait_recv()
        for cp in pair:
            cp.wait_send()
        for cp in local:
            cp.wait()

    return _pallas(
        body, name="pair_share", in_specs=[ANY, ANY], out_specs=[ANY, ANY],
        out_shape=[jax.ShapeDtypeStruct((2,) + ghw.shape, F32), jax.ShapeDtypeStruct((2,) + ghp.shape, F32)],
        scratch_shapes=[pltpu.SemaphoreType.DMA((N_BULK,)), pltpu.SemaphoreType.DMA((N_BULK,)),
                        pltpu.SemaphoreType.DMA((N_BULK,))],
        compiler_params=_params(),
    )(ghw, ghp)


def _rows_of(shape):
    size = 1
    for s in shape:
        size *= s
    return -(-size // D)


def _pack(arrs, rows_multiple=8):
    parts = []
    total = 0
    for a in arrs:
        f = a.reshape(-1).astype(F32)
        r = _rows_of(a.shape)
        parts.append(jnp.pad(f, (0, r * D - f.shape[0])))
        total += r
    pad_rows = (-total) % rows_multiple
    if pad_rows:
        parts.append(jnp.zeros((pad_rows * D,), F32))
    return jnp.concatenate(parts).reshape(-1, D)


def _unpack(p, shapes):
    out = []
    r0 = 0
    for shp in shapes:
        r = _rows_of(shp)
        size = 1
        for s in shp:
            size *= s
        out.append(p[r0:r0 + r].reshape(-1)[:size].reshape(shp))
        r0 += r
    return out


WEIGHT_NAMES = ['c_ctx', 'ada_w', 'ada_b', 'norm_g', 'w_in', 'b_in', 'conv_w', 'conv_b', 'conv_ln_g', 'conv_ln_b',
                'conv_proj', 'decay_up_fwd', 'decay_bias_fwd', 'decay_up_bwd', 'decay_bias_bwd', 'gla_norm_g', 'gla_proj',
                'w_out', 'final_norm_g']
SMALL_NAMES = ['c_ctx', 'ada_b', 'norm_g', 'b_in', 'conv_w', 'conv_b', 'conv_ln_g', 'conv_ln_b', 'decay_up_fwd',
               'decay_bias_fwd', 'decay_up_bwd', 'decay_bias_bwd', 'gla_norm_g', 'final_norm_g']


def kernel(x, c, ctx, c_ctx, ada_w, ada_b, norm_g, w_in, b_in, conv_w, conv_b, conv_ln_g, conv_ln_b, conv_proj, decay_up_fwd, decay_bias_fwd, decay_up_bwd, decay_bias_bwd, gla_norm_g, gla_proj, w_out, final_norm_g, loss_target, m_c_ctx, m_ada_w, m_ada_b, m_norm_g, m_w_in, m_b_in, m_conv_w, m_conv_b, m_conv_ln_g, m_conv_ln_b, m_conv_proj, m_decay_up_fwd, m_decay_bias_fwd, m_decay_up_bwd, m_decay_bias_bwd, m_gla_norm_g, m_gla_proj, m_w_out, m_final_norm_g, v_c_ctx, v_ada_w, v_ada_b, v_norm_g, v_w_in, v_b_in, v_conv_w, v_conv_b, v_conv_ln_g, v_conv_ln_b, v_conv_proj, v_decay_up_fwd, v_decay_bias_fwd, v_decay_up_bwd, v_decay_bias_bwd, v_gla_norm_g, v_gla_proj, v_w_out, v_final_norm_g):
    w = dict(c_ctx=c_ctx, ada_w=ada_w, ada_b=ada_b, norm_g=norm_g, w_in=w_in, b_in=b_in, conv_w=conv_w, conv_b=conv_b,
             conv_ln_g=conv_ln_g, conv_ln_b=conv_ln_b, conv_proj=conv_proj, decay_up_fwd=decay_up_fwd,
             decay_bias_fwd=decay_bias_fwd, decay_up_bwd=decay_up_bwd, decay_bias_bwd=decay_bias_bwd,
             gla_norm_g=gla_norm_g, gla_proj=gla_proj, w_out=w_out, final_norm_g=final_norm_g)
    m = dict(c_ctx=m_c_ctx, ada_w=m_ada_w, ada_b=m_ada_b, norm_g=m_norm_g, w_in=m_w_in, b_in=m_b_in, conv_w=m_conv_w,
             conv_b=m_conv_b, conv_ln_g=m_conv_ln_g, conv_ln_b=m_conv_ln_b, conv_proj=m_conv_proj,
             decay_up_fwd=m_decay_up_fwd, decay_bias_fwd=m_decay_bias_fwd, decay_up_bwd=m_decay_up_bwd,
             decay_bias_bwd=m_decay_bias_bwd, gla_norm_g=m_gla_norm_g, gla_proj=m_gla_proj, w_out=m_w_out,
             final_norm_g=m_final_norm_g)
    v = dict(c_ctx=v_c_ctx, ada_w=v_ada_w, ada_b=v_ada_b, norm_g=v_norm_g, w_in=v_w_in, b_in=v_b_in, conv_w=v_conv_w,
             conv_b=v_conv_b, conv_ln_g=v_conv_ln_g, conv_ln_b=v_conv_ln_b, conv_proj=v_conv_proj,
             decay_up_fwd=v_decay_up_fwd, decay_bias_fwd=v_decay_bias_fwd, decay_up_bwd=v_decay_up_bwd,
             decay_bias_bwd=v_decay_bias_bwd, gla_norm_g=v_gla_norm_g, gla_proj=v_gla_proj, w_out=v_w_out,
             final_norm_g=v_final_norm_g)
    n = x.shape[0]
    ax, ay, ac = lax.axis_index("x"), lax.axis_index("y"), lax.axis_index("c")
    q = 2 * ax + ay
    dev = 4 * ax + 2 * ay + ac
    nsh = ada_w.shape[2]

    w_sh = w_in[0].astype(BF16)
    p_sh = jnp.concatenate([conv_proj[0], gla_proj[0], w_out[0]], 0).astype(BF16)
    fp = _pack([conv_w[0], decay_up_fwd[0], decay_up_bwd[0]], F_ROWS)
    c8 = jnp.pad(c, ((0, 8 - n), (0, 0)))
    cctx8 = jnp.pad(c_ctx[None], ((0, 7), (0, 0)))
    adab_sh = lax.dynamic_slice(ada_b, (0, q * nsh), (1, nsh))
    w_all, p_all, fall, call, mall = gather_weights(c8, cctx8, ada_w[0], adab_sh, w_sh, p_sh, fp)

    mod_all = jnp.transpose(mall, (1, 0, 2)).reshape(MOD_ROWS, 3 * D)
    mod_mine = lax.dynamic_slice(mod_all, (8 * dev, 0), (n, 3 * D))
    mod_ctx = mod_all[64:65]
    shift = jnp.concatenate([mod_mine[:, 0:D], mod_ctx[:, 0:D]], 0)[:, None, :]
    scale1 = 1.0 + jnp.concatenate([mod_mine[:, D:2 * D], mod_ctx[:, D:2 * D]], 0)[:, None, :]
    gate = mod_mine[:, 2 * D:3 * D][:, None, :]

    g1, g2, g3, g4, g5 = _group_cols(jnp.concatenate([w_all[i] for i in range(N_CHIPS)], 1))
    wts = dict(w1=g1, w2=g2, w3=g3, w4=g4, w5=g5,
               conv_proj=p_all[:, 0:256].reshape(D, D), gla_proj=p_all[:, 256:512].reshape(D, D),
               w_out=p_all[:, 512:768].reshape(D, D))
    f_parts = [_unpack(fall[i], [conv_w.shape[1:], decay_up_fwd.shape[1:], decay_up_bwd.shape[1:]]) for i in range(N_CHIPS)]
    conv_w_full = jnp.concatenate([p[0] for p in f_parts], 1)
    upf_full = jnp.concatenate([p[1] for p in f_parts], 1)
    upb_full = jnp.concatenate([p[2] for p in f_parts], 1)
    b1, b2, b3, b4, b5 = _group_cols(b_in)
    small = dict(b1=b1, b2=b2, b3=b3, b4=b4, b5=b5, norm_g=norm_g,
                 conv_w=jnp.pad(conv_w_full, ((0, 1), (0, 0))), conv_b=conv_b, conv_ln_g=conv_ln_g, conv_ln_b=conv_ln_b,
                 upf=_pad_up(upf_full, 0), upb=_pad_up(upb_full, 16), bias_f=decay_bias_fwd, bias_b=decay_bias_bwd,
                 gla_norm_g=gla_norm_g, final_norm_g=final_norm_g[None])

    loss_part, grad_x2, g = local_step(x, ctx, loss_target, (scale1, shift, gate), wts, small)
    loss = lax.psum(loss_part[0, 0], ("x", "y", "c"))

    dm_mine = jnp.concatenate([g["shift"][:n, 0], g["scale"][:n, 0], g["gate"][:, 0]], -1)
    dm_ctx = jnp.concatenate([g["shift"][n, 0], g["scale"][n, 0], jnp.zeros((D,), F32)], -1)
    d_b_in = _ungroup_cols(*[g["b%d" % i] for i in range(1, 6)])
    small_grads = [d_b_in, g["norm_g"], g["conv_b"].sum(0), g["conv_ln_g"], g["conv_ln_b"], g["bias_f"], g["bias_b"],
                   g["gla_norm_g"], g["final_norm_g"], g["conv_w"].sum(0)[:CONV_K], g["upf"][0:16], g["upb"][16:32],
                   dm_mine, dm_ctx]
    small_shapes = [a.shape for a in small_grads]
    sm = _pack(small_grads)
    d_w_in = _ungroup_cols(*[g["w%d" % i] for i in range(1, 6)])
    gw = jnp.stack([d_w_in[:, i * W_IN_SHARD:(i + 1) * W_IN_SHARD] for i in range(N_CHIPS)])
    gp = jnp.concatenate([g["conv_proj"].reshape(N_CHIPS, 256, D), g["gla_proj"].reshape(N_CHIPS, 256, D),
                          g["w_out"].reshape(N_CHIPS, 256, D)], 1)
    sall, gotw, gotp = gather_small_and_pair(sm, gw, gp)
    core = ac.astype(jnp.int32).reshape(1)
    chip = q.astype(jnp.int32).reshape(1)
    paw, paw16 = pair_add(core, gw, gotw, name="pair_add_w", tr=128)
    pap, pap16 = pair_add(core, gp, gotp, name="pair_add_p", tr=384)

    gsum = sum_devices(sall)
    (s_b_in, s_norm_g, s_conv_b, s_ln_g, s_ln_b, s_bias_f, s_bias_b, s_gla_g, s_final_g, s_conv_w, s_upf, s_upb,
     _, _) = _unpack(gsum, small_shapes)
    dm_rows = [_unpack(sall[i], small_shapes)[-2:] for i in range(N_DEV)]
    dm_full = jnp.concatenate(
        [jnp.pad(jnp.stack([r[0] for r in dm_rows]), ((0, 0), (0, 8 - n), (0, 0))).reshape(8 * N_DEV, 3 * D),
         jnp.stack([r[1] for r in dm_rows])], 0)
    dm_shard = lax.dynamic_slice(dm_full, (0, q * nsh), (MOD_ROWS, nsh))
    cctx_rows = jnp.broadcast_to(c_ctx[None], (8, D))
    g_ada_w, g_ada_b, pq = ada_bwd(call, cctx_rows, dm_shard, dm_full, ada_w[0])

    rbw, rbp, pq_all = chip_exchange(paw16, pap16, pq)
    ghw = chip_add(chip, paw, rbw, name="chip_add_w", tr=128)
    ghp = chip_add(chip, pap, rbp, name="chip_add_p", tr=384)
    gw_mine, gp_mine = pair_share(ghw, ghp)
    gp_mine = gp_mine.reshape(768, D)
    g_c_ctx = cctx_grad(pq_all, cctx_rows)[0]

    grads = dict(
        c_ctx=g_c_ctx, ada_w=g_ada_w[None], ada_b=g_ada_b, norm_g=s_norm_g,
        w_in=gw_mine.reshape(1, D, W_IN_SHARD), b_in=s_b_in,
        conv_w=lax.dynamic_slice(s_conv_w, (0, q * 256), (CONV_K, 256))[None], conv_b=s_conv_b,
        conv_ln_g=s_ln_g, conv_ln_b=s_ln_b, conv_proj=gp_mine[0:256][None],
        decay_up_fwd=lax.dynamic_slice(s_upf, (0, q * 128), (16, 128))[None], decay_bias_fwd=s_bias_f,
        decay_up_bwd=lax.dynamic_slice(s_upb, (0, q * 128), (16, 128))[None], decay_bias_bwd=s_bias_b,
        gla_norm_g=s_gla_g, gla_proj=gp_mine[256:512][None], w_out=gp_mine[512:768][None],
        final_norm_g=s_final_g[0])

    delta, new_m, new_v = {}, {}, {}
    for name, tr in [("ada_w", 128), ("w_in", 128), ("conv_proj", 128), ("gla_proj", 128), ("w_out", 128)]:
        shp = w[name].shape
        two = lambda a: a.reshape(shp[-2], shp[-1])
        d_, m_, v_ = adamw2d(two(w[name]), two(grads[name]), two(m[name]), two(v[name]), name="adamw_" + name, tr=tr)
        delta[name], new_m[name], new_v[name] = d_.reshape(shp), m_.reshape(shp), v_.reshape(shp)
    shapes = [w[nm].shape for nm in SMALL_NAMES]
    packs = [_pack([src[nm] for nm in SMALL_NAMES]) for src in (w, grads, m, v)]
    d_, m_, v_ = adamw2d(*packs, name="adamw_small", tr=packs[0].shape[0])
    for nm, a, b, cc in zip(SMALL_NAMES, _unpack(d_, shapes), _unpack(m_, shapes), _unpack(v_, shapes)):
        delta[nm], new_m[nm], new_v[nm] = a, b, cc

    grad_x = grad_x2.reshape(x.shape)
    return (loss, grad_x, *[grads[nm].reshape(w[nm].shape) for nm in WEIGHT_NAMES], *[delta[nm] for nm in WEIGHT_NAMES],
            *[new_m[nm] for nm in WEIGHT_NAMES], *[new_v[nm] for nm in WEIGHT_NAMES])
```

```python
import jax
import jax.numpy as jnp
from jax import lax
from jax.experimental import pallas as pl
from jax.experimental.pallas import tpu as pltpu

F32 = jnp.float32
BF16 = jnp.bfloat16
MESH = pl.DeviceIdType.MESH
HI = lax.Precision.HIGHEST

D = 1024
SEQ = 2048
GRID_W = 64
GRID_H = SEQ // GRID_W
NCTX = 256
SEQ_ALL = SEQ + NCTX
EPS = 1e-6
CONV_K = 31
CONV_PAD = CONV_K // 2
HEADS = 4
HEAD_K = 128
HEAD_V = 256
GLA_DK = HEADS * HEAD_K
GATE_TAU = 16.0
Q_SCALE = HEAD_K ** -0.5
CHUNK = 64
NCHUNK = SEQ_ALL // CHUNK
NCHUNK_LAT = SEQ // CHUNK
NCHUNK_CTX = NCHUNK - NCHUNK_LAT
SUB = 16
NSUB = CHUNK // SUB
N_IN = 8224
W3 = 2176
O3_V, O3_Q, O3_K, O3_AB = 0, 1024, 1536, 2048

ADAM_LR, ADAM_B1, ADAM_B2, ADAM_EPS, ADAM_WD, ADAM_STEP = 0.001, 0.9, 0.999, 1e-08, 0.01, 10
VMEM_LIMIT = 56 * 1024 * 1024

N_CHIPS = 4
N_DEV = 8
W_IN_SHARD = N_IN // N_CHIPS
MOD_ROWS = 72


def _pallas(body, **kw):
    return pl.pallas_call(body, **kw)


def _params(sem=None, **kw):
    if sem is not None:
        kw["dimension_semantics"] = sem
    return pltpu.CompilerParams(vmem_limit_bytes=VMEM_LIMIT, **kw)


def _sigmoid(v):
    return 1.0 / (1.0 + jnp.exp(-v))


def _silu(v):
    return v * _sigmoid(v)


def _dsilu(v):
    s = _sigmoid(v)
    return s * (1.0 + v * (1.0 - s))


def _log_sigmoid(v):
    return jnp.minimum(v, 0.0) - jnp.log(1.0 + jnp.exp(-jnp.abs(v)))


def _dot(a, b, dims, precision=None):
    return lax.dot_general(a, b, (dims, ((), ())), preferred_element_type=F32, precision=precision)


def _nn(a, b, precision=None):
    return _dot(a, b, ((1,), (0,)), precision)


def _nt(a, b, precision=None):
    return _dot(a, b, ((1,), (1,)), precision)


def _tn(a, b, precision=None):
    return _dot(a, b, ((0,), (0,)), precision)


def _b16(v):
    return v.astype(BF16)


def matmul_nn(a, b, bias, *, name, m, tm, tn, out_dtype):
    k = a.shape[1]
    n = b.shape[1]
    has_bias = bias is not None

    def body(*refs):
        if has_bias:
            a_ref, b_ref, bias_ref, o_ref = refs
            acc = _nn(a_ref[...], b_ref[...]) + bias_ref[...]
        else:
            a_ref, b_ref, o_ref = refs
            acc = _nn(a_ref[...], b_ref[...])
        o_ref[...] = acc.astype(o_ref.dtype)

    in_specs = [pl.BlockSpec((tm, k), lambda j, i: (i, 0)), pl.BlockSpec((k, tn), lambda j, i: (0, j))]
    args = [a, b]
    if has_bias:
        in_specs.append(pl.BlockSpec((1, tn), lambda j, i: (0, j)))
        args.append(bias)
    return _pallas(
        body, name=name, grid=(n // tn, m // tm), in_specs=in_specs,
        out_specs=pl.BlockSpec((tm, tn), lambda j, i: (i, j)),
        out_shape=jax.ShapeDtypeStruct((m, n), out_dtype),
        compiler_params=_params(("parallel", "parallel")),
    )(*args)


def matmul_nt(a, b, *, name, tm):
    m, k = a.shape
    n = b.shape[0]

    def body(a_ref, b_ref, o_ref):
        o_ref[...] = _nt(a_ref[...], b_ref[...])

    return _pallas(
        body, name=name, grid=(m // tm,),
        in_specs=[pl.BlockSpec((tm, k), lambda i: (i, 0)), pl.BlockSpec((n, k), lambda i: (0, 0))],
        out_specs=pl.BlockSpec((tm, n), lambda i: (i, 0)),
        out_shape=jax.ShapeDtypeStruct((m, n), F32),
        compiler_params=_params(("parallel",)),
    )(a, b)


def matmul_tn(a, b, *, name, t, tn, tt, colsum=False):
    m = a.shape[1]
    n = b.shape[1]

    def body(a_ref, b_ref, o_ref, *rest):
        @pl.when(pl.program_id(1) == 0)
        def _():
            o_ref[...] = jnp.zeros_like(o_ref)
            if colsum:
                rest[0][...] = jnp.zeros_like(rest[0])
        o_ref[...] += _tn(a_ref[...], b_ref[...])
        if colsum:
            rest[0][...] += jnp.sum(b_ref[...].astype(F32), axis=0, keepdims=True)

    out_specs = [pl.BlockSpec((m, tn), lambda j, s: (0, j))]
    out_shape = [jax.ShapeDtypeStruct((m, n), F32)]
    if colsum:
        out_specs.append(pl.BlockSpec((1, tn), lambda j, s: (0, j)))
        out_shape.append(jax.ShapeDtypeStruct((1, n), F32))
    return _pallas(
        body, name=name, grid=(n // tn, t // tt),
        in_specs=[pl.BlockSpec((tt, m), lambda j, s: (s, 0)), pl.BlockSpec((tt, tn), lambda j, s: (s, j))],
        out_specs=out_specs, out_shape=out_shape,
        compiler_params=_params(("parallel", "arbitrary")),
    )(a, b)


def dgrad_multi(dps, wts, *, t_all, t_lat, tm):
    steps = []
    tks = []
    for g, dp in enumerate(dps):
        width = dp.shape[1]
        tk = width if width % 1024 else 1024
        tks.append(tk)
        steps += [(g, s) for s in range(width // tk)]
    n_steps = len(steps)
    lo = [min(i for i, (g, _) in enumerate(steps) if g == gg) for gg in range(len(dps))]
    cnt = [sum(1 for (g, _) in steps if g == gg) for gg in range(len(dps))]
    n_lat = t_lat // tm
    n_g = len(dps)

    def body(*refs):
        dp_refs, w_refs, o_ref = refs[:n_g], refs[n_g:2 * n_g], refs[2 * n_g]
        i = pl.program_id(0)
        s = pl.program_id(1)

        @pl.when(s == 0)
        def _():
            o_ref[...] = jnp.zeros_like(o_ref)

        for g in range(n_g):
            in_rows = dps[g].shape[0] == t_all
            cond = (s >= lo[g]) & (s < lo[g] + cnt[g])
            if not in_rows:
                cond = cond & (i < n_lat)

            @pl.when(cond)
            def _(g=g):
                o_ref[...] += _nt(dp_refs[g][...], w_refs[g][...])

    in_specs = []
    for g, dp in enumerate(dps):
        nrow = dp.shape[0] // tm
        in_specs.append(pl.BlockSpec(
            (tm, tks[g]), lambda i, s, g=g, nrow=nrow: (jnp.minimum(i, nrow - 1), jnp.clip(s - lo[g], 0, cnt[g] - 1))))
    for g, w in enumerate(wts):
        in_specs.append(pl.BlockSpec((D, tks[g]), lambda i, s, g=g: (0, jnp.clip(s - lo[g], 0, cnt[g] - 1))))
    return _pallas(
        body, name="dgrad_w_in", grid=(t_all // tm, n_steps), in_specs=in_specs,
        out_specs=pl.BlockSpec((tm, D), lambda i, s: (i, 0)),
        out_shape=jax.ShapeDtypeStruct((t_all, D), F32),
        compiler_params=_params(("parallel", "arbitrary")),
    )(*dps, *wts)


TM_NORM = 512


def norm_mod_fwd(x2, ctx2, scale1, shift, norm_g):
    t = x2.shape[0]
    n_lat = t // TM_NORM
    assert ctx2.shape[0] == TM_NORM
    n_samples = scale1.shape[0] - 1
    tps = n_lat // n_samples

    def body(x_ref, c_ref, sc_ref, sh_ref, g_ref, u_ref):
        i = pl.program_id(0)
        xv = jnp.where(i < n_lat, x_ref[...], c_ref[...])
        rs = lax.rsqrt(jnp.mean(xv * xv, axis=-1, keepdims=True) + EPS)
        u = xv * rs * g_ref[...] * sc_ref[0] + sh_ref[0]
        u_ref[...] = u.astype(u_ref.dtype)

    grp = lambda i: (jnp.minimum(i // tps, n_samples), 0, 0)
    return _pallas(
        body, name="norm_mod_fwd", grid=(n_lat + 1,),
        in_specs=[pl.BlockSpec((TM_NORM, D), lambda i: (jnp.minimum(i, n_lat - 1), 0)),
                  pl.BlockSpec((TM_NORM, D), lambda i: (0, 0)),
                  pl.BlockSpec((1, 1, D), grp), pl.BlockSpec((1, 1, D), grp),
                  pl.BlockSpec((1, D), lambda i: (0, 0))],
        out_specs=pl.BlockSpec((TM_NORM, D), lambda i: (i, 0)),
        out_shape=jax.ShapeDtypeStruct((t + TM_NORM, D), BF16),
        compiler_params=_params(("parallel",)),
    )(x2, ctx2, scale1, shift, norm_g)


def norm_mod_bwd(x2, ctx2, du, dh, scale1, norm_g):
    t = x2.shape[0]
    n_lat = t // TM_NORM
    n_samples = scale1.shape[0] - 1
    tps = n_lat // n_samples
    n_grp = n_samples + 1

    def body(x_ref, c_ref, du_ref, dh_ref, sc_ref, g_ref, dx_ref, dsh_ref, dsc_ref, dg_ref):
        i = pl.program_id(0)
        xv = jnp.where(i < n_lat, x_ref[...], c_ref[...])
        rs = lax.rsqrt(jnp.mean(xv * xv, axis=-1, keepdims=True) + EPS)
        xh = xv * rs
        duv = du_ref[...]
        n = xh * g_ref[...]
        dn = duv * sc_ref[0]
        dxh = dn * g_ref[...]
        dx = rs * (dxh - xh * jnp.mean(dxh * xh, axis=-1, keepdims=True))
        @pl.when(i < n_lat)
        def _():
            dx_ref[...] = dx + dh_ref[...]

        @pl.when(i % tps == 0)
        def _():
            dsh_ref[...] = jnp.zeros_like(dsh_ref)
            dsc_ref[...] = jnp.zeros_like(dsc_ref)

        @pl.when(i == 0)
        def _():
            dg_ref[...] = jnp.zeros_like(dg_ref)

        dsh_ref[0] += jnp.sum(duv, axis=0, keepdims=True)
        dsc_ref[0] += jnp.sum(duv * n, axis=0, keepdims=True)
        dg_ref[...] += jnp.sum(dn * xh, axis=0, keepdims=True)

    grp = lambda i: (jnp.minimum(i // tps, n_samples), 0, 0)
    lat = lambda i: (jnp.minimum(i, n_lat - 1), 0)
    return _pallas(
        body, name="norm_mod_bwd", grid=(n_lat + 1,),
        in_specs=[pl.BlockSpec((TM_NORM, D), lat),
                  pl.BlockSpec((TM_NORM, D), lambda i: (0, 0)),
                  pl.BlockSpec((TM_NORM, D), lambda i: (i, 0)),
                  pl.BlockSpec((TM_NORM, D), lat),
                  pl.BlockSpec((1, 1, D), grp),
                  pl.BlockSpec((1, D), lambda i: (0, 0))],
        out_specs=[pl.BlockSpec((TM_NORM, D), lat),
                   pl.BlockSpec((1, 1, D), grp), pl.BlockSpec((1, 1, D), grp),
                   pl.BlockSpec((1, D), lambda i: (0, 0))],
        out_shape=[jax.ShapeDtypeStruct((t, D), F32),
                   jax.ShapeDtypeStruct((n_grp, 1, D), F32), jax.ShapeDtypeStruct((n_grp, 1, D), F32),
                   jax.ShapeDtypeStruct((1, D), F32)],
        compiler_params=_params(("arbitrary",)),
    )(x2, ctx2, du, dh, scale1, norm_g)


CONV_CB = 256
CONV_NCB = D // CONV_CB
H_OFF = 16


def _conv_pad_shape(vertical):
    if vertical:
        return (GRID_H + 2 * CONV_PAD, GRID_W, CONV_CB)
    return (GRID_H, GRID_W + 2 * H_OFF, CONV_CB)


def _conv_store(pad_ref, img, vertical):
    if vertical:
        pad_ref[pl.ds(CONV_PAD, GRID_H)] = img
    else:
        pad_ref[:, pl.ds(H_OFF, GRID_W), :] = img


def _conv_window(pad_ref, k, vertical, r):
    if vertical:
        return pad_ref[r + k]
    return pad_ref[r, pl.ds(H_OFF - CONV_PAD + k, GRID_W), :]


def _rows(r):
    return pl.ds(pl.multiple_of(r * GRID_W, GRID_W), GRID_W)


def conv_fwd(p1, conv_w, conv_b, n_samples):
    t = n_samples * SEQ

    def make(vertical, prev):
        def body(gv_ref, gg_ref, w_ref, b_ref, *rest):
            o_ref, pad_ref = rest[-2], rest[-1]
            pad_ref[...] = jnp.zeros_like(pad_ref)
            a = gv_ref[...] * _sigmoid(gg_ref[...])
            _conv_store(pad_ref, a.reshape(GRID_H, GRID_W, CONV_CB), vertical)

            def row(r, carry):
                acc = jnp.zeros((GRID_W, CONV_CB), F32) + b_ref[...]
                for k in range(CONV_K):
                    acc = acc + _conv_window(pad_ref, k, vertical, r) * w_ref[pl.ds(k, 1), :]
                o_ref[_rows(r), :] = acc
                return carry

            lax.fori_loop(0, GRID_H, row, 0)

        cb0 = CONV_NCB // 2 if vertical else 0
        in_specs = [pl.BlockSpec((SEQ, CONV_CB), lambda b, j: (b, 2 * (cb0 + j))),
                    pl.BlockSpec((SEQ, CONV_CB), lambda b, j: (b, 2 * (cb0 + j) + 1)),
                    pl.BlockSpec((CONV_K + 1, CONV_CB), lambda b, j: (0, cb0 + j)),
                    pl.BlockSpec((1, CONV_CB), lambda b, j: (0, cb0 + j))]
        args = [p1, p1, conv_w, conv_b]
        aliases = {}
        if prev is not None:
            in_specs.append(pl.BlockSpec(memory_space=pl.ANY))
            args.append(prev)
            aliases = {4: 0}
        return _pallas(
            body, name="conv_fwd_v" if vertical else "conv_fwd_h", grid=(n_samples, CONV_NCB // 2),
            in_specs=in_specs,
            out_specs=pl.BlockSpec((SEQ, CONV_CB), lambda b, j: (b, cb0 + j)),
            out_shape=jax.ShapeDtypeStruct((t, D), F32),
            scratch_shapes=[pltpu.VMEM(_conv_pad_shape(vertical), F32)],
            input_output_aliases=aliases,
            compiler_params=_params(("parallel", "parallel")),
        )(*args)

    return make(True, make(False, None))


def conv_bwd(p1, daconv, conv_w, n_samples):
    t = n_samples * SEQ

    def make(vertical, prev):
        def body(gv_ref, gg_ref, dy_ref, w_ref, *rest):
            dp_ref, dw_ref, db_ref, pad_ref, dpad_ref, da_ref = rest[-6:]
            pad_ref[...] = jnp.zeros_like(pad_ref)
            dpad_ref[...] = jnp.zeros_like(dpad_ref)
            _conv_store(pad_ref, (gv_ref[...] * _sigmoid(gg_ref[...])).reshape(GRID_H, GRID_W, CONV_CB), vertical)
            _conv_store(dpad_ref, dy_ref[...].reshape(GRID_H, GRID_W, CONV_CB), vertical)

            def row(r, carry):
                acc = jnp.zeros((GRID_W, CONV_CB), F32)
                for k in range(CONV_K):
                    acc = acc + _conv_window(dpad_ref, CONV_K - 1 - k, vertical, r) * w_ref[pl.ds(k, 1), :]
                da_ref[_rows(r), :] = acc
                return carry

            lax.fori_loop(0, GRID_H, row, 0)
            da = da_ref[...]
            gv = gv_ref[...]
            sg = _sigmoid(gg_ref[...])
            dp_ref[:, pl.ds(0, CONV_CB)] = (da * sg).astype(dp_ref.dtype)
            dp_ref[:, pl.ds(CONV_CB, CONV_CB)] = (da * gv * sg * (1.0 - sg)).astype(dp_ref.dtype)

            for k in range(CONV_K):
                def wrow(r, acc, k=k):
                    return acc + _conv_window(pad_ref, k, vertical, r) * dy_ref[_rows(r), :]
                acc = lax.fori_loop(0, GRID_H, wrow, jnp.zeros((GRID_W, CONV_CB), F32))
                dw_ref[0, pl.ds(k, 1), :] = jnp.sum(acc, axis=0, keepdims=True)
            dw_ref[0, pl.ds(CONV_K, 1), :] = jnp.zeros((1, CONV_CB), F32)
            db_ref[0] = jnp.sum(dy_ref[...], axis=0, keepdims=True)

        cb0 = CONV_NCB // 2 if vertical else 0
        in_specs = [pl.BlockSpec((SEQ, CONV_CB), lambda b, j: (b, 2 * (cb0 + j))),
                    pl.BlockSpec((SEQ, CONV_CB), lambda b, j: (b, 2 * (cb0 + j) + 1)),
                    pl.BlockSpec((SEQ, CONV_CB), lambda b, j: (b, cb0 + j)),
                    pl.BlockSpec((CONV_K + 1, CONV_CB), lambda b, j: (0, cb0 + j))]
        args = [p1, p1, daconv, conv_w]
        aliases = {}
        if prev is not None:
            in_specs += [pl.BlockSpec(memory_space=pl.ANY)] * 3
            args += list(prev)
            aliases = {4: 0, 5: 1, 6: 2}
        return _pallas(
            body, name="conv_bwd_v" if vertical else "conv_bwd_h", grid=(n_samples, CONV_NCB // 2),
            in_specs=in_specs,
            out_specs=[pl.BlockSpec((SEQ, 2 * CONV_CB), lambda b, j: (b, cb0 + j)),
                       pl.BlockSpec((1, CONV_K + 1, CONV_CB), lambda b, j: (b, 0, cb0 + j)),
                       pl.BlockSpec((1, 1, CONV_CB), lambda b, j: (b, 0, cb0 + j))],
            out_shape=[jax.ShapeDtypeStruct((t, 2 * D), BF16),
                       jax.ShapeDtypeStruct((n_samples, CONV_K + 1, D), F32),
                       jax.ShapeDtypeStruct((n_samples, 1, D), F32)],
            scratch_shapes=[pltpu.VMEM(_conv_pad_shape(vertical), F32), pltpu.VMEM(_conv_pad_shape(vertical), F32),
                            pltpu.VMEM((SEQ, CONV_CB), F32)],
            input_output_aliases=aliases,
            compiler_params=_params(("parallel", "parallel")),
        )(*args)

    return make(True, make(False, None))


TM_EW = 256


def ln_gate_fwd(aconv, z, ln_g, ln_b):
    t = aconv.shape[0]

    def body(a_ref, z_ref, g_ref, b_ref, o_ref):
        a = a_ref[...]
        mu = jnp.mean(a, axis=-1, keepdims=True)
        xc = a - mu
        rstd = lax.rsqrt(jnp.mean(xc * xc, axis=-1, keepdims=True) + EPS)
        l = xc * rstd * g_ref[...] + b_ref[...]
        o_ref[...] = (_silu(l) * _silu(z_ref[...])).astype(o_ref.dtype)

    row = pl.BlockSpec((TM_EW, D), lambda i: (i, 0))
    vec = pl.BlockSpec((1, D), lambda i: (0, 0))
    return _pallas(
        body, name="ln_gate_fwd", grid=(t // TM_EW,), in_specs=[row, row, vec, vec], out_specs=row,
        out_shape=jax.ShapeDtypeStruct((t, D), BF16), compiler_params=_params(("parallel",)),
    )(aconv, z, ln_g, ln_b)


def ln_gate_bwd(aconv, z, dac, ln_g, ln_b):
    t = aconv.shape[0]

    def body(a_ref, z_ref, d_ref, g_ref, b_ref, da_ref, dz_ref, dg_ref, db_ref):
        a = a_ref[...]
        zv = z_ref[...]
        dac_v = d_ref[...]
        mu = jnp.mean(a, axis=-1, keepdims=True)
        xc = a - mu
        rstd = lax.rsqrt(jnp.mean(xc * xc, axis=-1, keepdims=True) + EPS)
        xh = xc * rstd
        l = xh * g_ref[...] + b_ref[...]
        dz_ref[...] = (dac_v * _silu(l) * _dsilu(zv)).astype(dz_ref.dtype)
        dl = dac_v * _silu(zv) * _dsilu(l)
        dxh = dl * g_ref[...]
        da_ref[...] = rstd * (dxh - jnp.mean(dxh, axis=-1, keepdims=True)
                              - xh * jnp.mean(dxh * xh, axis=-1, keepdims=True))

        @pl.when(pl.program_id(0) == 0)
        def _():
            dg_ref[...] = jnp.zeros_like(dg_ref)
            db_ref[...] = jnp.zeros_like(db_ref)

        dg_ref[...] += jnp.sum(dl * xh, axis=0, keepdims=True)
        db_ref[...] += jnp.sum(dl, axis=0, keepdims=True)

    row = pl.BlockSpec((TM_EW, D), lambda i: (i, 0))
    vec = pl.BlockSpec((1, D), lambda i: (0, 0))
    return _pallas(
        body, name="ln_gate_bwd", grid=(t // TM_EW,), in_specs=[row, row, row, vec, vec],
        out_specs=[row, row, vec, vec],
        out_shape=[jax.ShapeDtypeStruct((t, D), F32), jax.ShapeDtypeStruct((t, D), BF16),
                   jax.ShapeDtypeStruct((1, D), F32), jax.ShapeDtypeStruct((1, D), F32)],
        compiler_params=_params(("arbitrary",)),
    )(aconv, z, dac, ln_g, ln_b)


TM_PREP = 256
PREP_LAT = SEQ // TM_PREP
PREP_ALL = SEQ_ALL // TM_PREP


def _chunk_tri(n, upper):
    r = lax.broadcasted_iota(jnp.int32, (n, n), 0)
    c = lax.broadcasted_iota(jnp.int32, (n, n), 1)
    same = (r // CHUNK) == (c // CHUNK)
    keep = (c >= r) if upper else (c <= r)
    return jnp.where(same & keep, 1.0, 0.0).astype(F32)


def _prep_tile_maps(n_samples):
    n_lat = n_samples * PREP_LAT

    def seq_map(i):
        return jnp.where(i < n_lat, i // PREP_LAT, i - n_lat), jnp.where(i < n_lat, i % PREP_LAT, PREP_LAT)

    return n_lat, seq_map


def gla_prep_fwd(p3, upf, upb, bias_f, bias_b, n_samples):
    n_lat, seq_map = _prep_tile_maps(n_samples)
    n_tiles = n_lat + n_samples

    def body(v_ref, q_ref, k_ref, ab_ref, upf_ref, upb_ref, bf_ref, bb_ref, qo, ko, vo, cf, cb):
        i = pl.program_id(0)
        qo[0] = jnp.where(i < n_lat, q_ref[...] * Q_SCALE, 0.0)
        ko[0] = k_ref[...]
        vo[0] = v_ref[...]
        ab = ab_ref[...]
        gf = _log_sigmoid(_nn(ab, upf_ref[...], HI) + bf_ref[...]) * (1.0 / GATE_TAU)
        gb = _log_sigmoid(_nn(ab, upb_ref[...], HI) + bb_ref[...]) * (1.0 / GATE_TAU)
        cf[0] = _nn(_chunk_tri(TM_PREP, False), gf, HI)
        cb[0] = _nn(_chunk_tri(TM_PREP, True), gb, HI)

    def o_spec(w):
        return pl.BlockSpec((1, TM_PREP, w), lambda i: (*seq_map(i), 0))

    full = lambda shape: pl.BlockSpec(shape, lambda i: (0,) * len(shape))
    return _pallas(
        body, name="gla_prep_fwd", grid=(n_tiles,),
        in_specs=[pl.BlockSpec((TM_PREP, 1024), lambda i: (i, O3_V // 1024)),
                  pl.BlockSpec((TM_PREP, 512), lambda i: (i, O3_Q // 512)),
                  pl.BlockSpec((TM_PREP, 512), lambda i: (i, O3_K // 512)),
                  pl.BlockSpec((TM_PREP, 128), lambda i: (i, O3_AB // 128)),
                  full((128, GLA_DK)), full((128, GLA_DK)), full((1, GLA_DK)), full((1, GLA_DK))],
        out_specs=[o_spec(GLA_DK), o_spec(GLA_DK), o_spec(D), o_spec(GLA_DK), o_spec(GLA_DK)],
        out_shape=[jax.ShapeDtypeStruct((n_samples, SEQ_ALL, GLA_DK), F32),
                   jax.ShapeDtypeStruct((n_samples, SEQ_ALL, GLA_DK), F32),
                   jax.ShapeDtypeStruct((n_samples, SEQ_ALL, D), F32),
                   jax.ShapeDtypeStruct((n_samples, SEQ_ALL, GLA_DK), F32),
                   jax.ShapeDtypeStruct((n_samples, SEQ_ALL, GLA_DK), F32)],
        compiler_params=_params(("parallel",)),
    )(p3, p3, p3, p3, upf, upb, bias_f, bias_b)


def gla_prep_bwd(p3, dq_f, dq_b, dk_f, dk_b, dv_f, dv_b, dc_f, dc_b, upf, upb, bias_f, bias_b, n_samples):
    n_lat, seq_map = _prep_tile_maps(n_samples)
    n_tiles = n_lat + n_samples

    def body(ab_ref, dqf, dqb, dkf, dkb, dvf, dvb, dcf, dcb, upf_ref, upb_ref, bf_ref, bb_ref,
             dp_ref, duf_ref, dub_ref, dbf_ref, dbb_ref):
        i = pl.program_id(0)
        dp_ref[:, pl.ds(O3_V, D)] = (dvf[0] + dvb[0]).astype(dp_ref.dtype)
        dq = jnp.where(i < n_lat, (dqf[0] + dqb[0]) * Q_SCALE, 0.0)
        dp_ref[:, pl.ds(O3_Q, GLA_DK)] = dq.astype(dp_ref.dtype)
        dp_ref[:, pl.ds(O3_K, GLA_DK)] = (dkf[0] + dkb[0]).astype(dp_ref.dtype)
        ab = ab_ref[...]
        zf = _nn(ab, upf_ref[...], HI) + bf_ref[...]
        zb = _nn(ab, upb_ref[...], HI) + bb_ref[...]
        dgf = _nn(_chunk_tri(TM_PREP, True), dcf[0], HI)
        dgb = _nn(_chunk_tri(TM_PREP, False), dcb[0], HI)
        dzf = dgf * (1.0 / GATE_TAU) * _sigmoid(-zf)
        dzb = dgb * (1.0 / GATE_TAU) * _sigmoid(-zb)
        dab = _nt(dzf, upf_ref[...], HI) + _nt(dzb, upb_ref[...], HI)
        dp_ref[:, pl.ds(O3_AB, 128)] = dab.astype(dp_ref.dtype)

        @pl.when(i == 0)
        def _():
            duf_ref[...] = jnp.zeros_like(duf_ref)
            dub_ref[...] = jnp.zeros_like(dub_ref)
            dbf_ref[...] = jnp.zeros_like(dbf_ref)
            dbb_ref[...] = jnp.zeros_like(dbb_ref)

        duf_ref[...] += _tn(ab, dzf, HI)
        dub_ref[...] += _tn(ab, dzb, HI)
        dbf_ref[...] += jnp.sum(dzf, axis=0, keepdims=True)
        dbb_ref[...] += jnp.sum(dzb, axis=0, keepdims=True)

    def s_spec(w):
        return pl.BlockSpec((1, TM_PREP, w), lambda i: (*seq_map(i), 0))

    full = lambda shape: pl.BlockSpec(shape, lambda i: (0,) * len(shape))
    return _pallas(
        body, name="gla_prep_bwd", grid=(n_tiles,),
        in_specs=[pl.BlockSpec((TM_PREP, 128), lambda i: (i, O3_AB // 128)),
                  s_spec(GLA_DK), s_spec(GLA_DK), s_spec(GLA_DK), s_spec(GLA_DK), s_spec(D), s_spec(D),
                  s_spec(GLA_DK), s_spec(GLA_DK),
                  full((128, GLA_DK)), full((128, GLA_DK)), full((1, GLA_DK)), full((1, GLA_DK))],
        out_specs=[pl.BlockSpec((TM_PREP, W3), lambda i: (i, 0)),
                   full((128, GLA_DK)), full((128, GLA_DK)), full((1, GLA_DK)), full((1, GLA_DK))],
        out_shape=[jax.ShapeDtypeStruct((n_tiles * TM_PREP, W3), BF16),
                   jax.ShapeDtypeStruct((128, GLA_DK), F32), jax.ShapeDtypeStruct((128, GLA_DK), F32),
                   jax.ShapeDtypeStruct((1, GLA_DK), F32), jax.ShapeDtypeStruct((1, GLA_DK), F32)],
        compiler_params=_params(("arbitrary",)),
    )(p3, dq_f, dq_b, dk_f, dk_b, dv_f, dv_b, dc_f, dc_b, upf, upb, bias_f, bias_b)


def _sub_blocks(rev):
    out = []
    for s in range(NSUB):
        rows = (s * SUB, SUB)
        if rev:
            ref = (s + 1) * SUB if s < NSUB - 1 else None
            cols = (s * SUB, CHUNK - s * SUB)
        else:
            ref = s * SUB - 1 if s > 0 else None
            cols = (0, (s + 1) * SUB)
        out.append((rows, ref, cols))
    return out


def _sub_mask(rows, cols, rev):
    r = rows[0] + lax.broadcasted_iota(jnp.int32, (rows[1], cols[1]), 0)
    c = cols[0] + lax.broadcasted_iota(jnp.int32, (rows[1], cols[1]), 1)
    return (c >= r) if rev else (c <= r)


def _sub_operands(qc, kc, cc, rows, ref, cols):
    cref = jnp.zeros((1, HEAD_K), F32) if ref is None else cc[ref:ref + 1]
    eq = jnp.exp(cc[rows[0]:rows[0] + rows[1]] - cref)
    ek = jnp.exp(cref - cc[cols[0]:cols[0] + cols[1]])
    qs = qc[rows[0]:rows[0] + rows[1]] * eq
    kk = kc[cols[0]:cols[0] + cols[1]] * ek
    return qs, kk, eq, ek


SCAN_ROWS = 256
SCAN_CHUNKS = SCAN_ROWS // CHUNK
SCAN_STEPS = SEQ_ALL // SCAN_ROWS
LAT_BLOCKS = SEQ // SCAN_ROWS


def _scan_block(t, rev):
    if rev:
        return SCAN_STEPS - 1 - t
    return jnp.where(t == 0, SCAN_STEPS - 1, t - 1)


def _scan_lat_block(t, rev):
    if rev:
        return jnp.minimum(SCAN_STEPS - 1 - t, LAT_BLOCKS - 1)
    return jnp.maximum(t - 1, 0)


def _head_cols(h):
    return pl.ds(h * HEAD_K, HEAD_K), pl.ds(h * HEAD_V, HEAD_V)


def gla_scan_fwd(q, k, v, cum, *, rev, name):
    n = q.shape[0]

    def body(q_ref, k_ref, v_ref, c_ref, o_ref, s_ref, sfin_ref, st):
        t = pl.program_id(1)

        @pl.when(t == 0)
        def _():
            st[...] = jnp.zeros_like(st)

        def chunk(j, carry):
            lj = SCAN_CHUNKS - 1 - j if rev else j
            r0 = pl.multiple_of(lj * CHUNK, CHUNK)
            rws = pl.ds(r0, CHUNK)
            for h in range(HEADS):
                kcols, vcols = _head_cols(h)
                qc, kc, cc = q_ref[0, rws, kcols], k_ref[0, rws, kcols], c_ref[0, rws, kcols]
                vc = v_ref[0, rws, vcols]
                s_in = st[h]
                s_ref[0, h, j] = s_in
                edge = cc[0:1] if rev else cc[CHUNK - 1:CHUNK]
                ke = kc * jnp.exp(edge - cc)
                st[h] = s_in * jnp.exp(edge) + _tn(_b16(vc), _b16(ke))

                @pl.when(t > 0)
                def _(qc=qc, kc=kc, cc=cc, vc=vc, s_in=s_in, vcols=vcols):
                    o_inter = _nt(_b16(qc * jnp.exp(cc)), _b16(s_in))
                    vb = _b16(vc)
                    for rows, ref, cols in _sub_blocks(rev):
                        qs, kk, _, _ = _sub_operands(qc, kc, cc, rows, ref, cols)
                        a = jnp.where(_sub_mask(rows, cols, rev), _nt(_b16(qs), _b16(kk)), 0.0)
                        o_s = _nn(_b16(a), vb[cols[0]:cols[0] + cols[1]])
                        o_ref[0, pl.ds(r0 + rows[0], rows[1]), vcols] = o_inter[rows[0]:rows[0] + rows[1]] + o_s
            return carry

        lax.fori_loop(0, SCAN_CHUNKS, chunk, 0)

        @pl.when(t == SCAN_STEPS - 1)
        def _():
            sfin_ref[0] = st[...]

    def spec(w):
        return pl.BlockSpec((1, SCAN_ROWS, w), lambda b, t: (b, _scan_block(t, rev), 0))

    return _pallas(
        body, name=name, grid=(n, SCAN_STEPS),
        in_specs=[spec(GLA_DK), spec(GLA_DK), spec(D), spec(GLA_DK)],
        out_specs=[pl.BlockSpec((1, SCAN_ROWS, D), lambda b, t: (b, _scan_lat_block(t, rev), 0)),
                   pl.BlockSpec((1, HEADS, SCAN_CHUNKS, HEAD_V, HEAD_K), lambda b, t: (b, 0, t, 0, 0)),
                   pl.BlockSpec((1, HEADS, HEAD_V, HEAD_K), lambda b, t: (b, 0, 0, 0))],
        out_shape=[jax.ShapeDtypeStruct((n, SEQ, D), F32),
                   jax.ShapeDtypeStruct((n, HEADS, NCHUNK, HEAD_V, HEAD_K), F32),
                   jax.ShapeDtypeStruct((n, HEADS, HEAD_V, HEAD_K), F32)],
        scratch_shapes=[pltpu.VMEM((HEADS, HEAD_V, HEAD_K), F32)],
        compiler_params=_params(("parallel", "arbitrary")),
    )(q, k, v, cum)


def gla_scan_bwd(q, k, v, cum, s_all, s_fin, do, *, rev, name):
    n = q.shape[0]

    def body(q_ref, k_ref, v_ref, c_ref, s_ref, sfin_ref, do_ref, dq_ref, dk_ref, dv_ref, dc_ref,
             dst, s_next, dq_acc, dk_acc, dv_acc):
        t = SCAN_STEPS - 1 - pl.program_id(1)

        @pl.when(pl.program_id(1) == 0)
        def _():
            dst[...] = jnp.zeros_like(dst)
            s_next[...] = sfin_ref[0]

        def chunk(jj, carry):
            j = SCAN_CHUNKS - 1 - jj
            lj = SCAN_CHUNKS - 1 - j if rev else j
            rws = pl.ds(pl.multiple_of(lj * CHUNK, CHUNK), CHUNK)
            for h in range(HEADS):
                kcols, vcols = _head_cols(h)
                qc, kc, cc = q_ref[0, rws, kcols], k_ref[0, rws, kcols], c_ref[0, rws, kcols]
                vc = v_ref[0, rws, vcols]
                doc = jnp.where(t > 0, do_ref[0, rws, vcols], 0.0)
                s_in = s_ref[0, h, j]
                s_out = s_next[h]
                ds_out = dst[h]
                edge = cc[0:1] if rev else cc[CHUNK - 1:CHUNK]
                e_q = jnp.exp(cc)
                e_k = jnp.exp(edge - cc)
                dob = _b16(doc)
                dsb = _b16(ds_out)
                dst[h] = ds_out * jnp.exp(edge) + _tn(dob, _b16(qc * e_q))
                s_next[h] = s_in
                dq_acc[h] = e_q * _nn(dob, _b16(s_in))
                dk_acc[h] = e_k * _nn(_b16(vc), dsb)
                dv_acc[h] = _nt(_b16(kc * e_k), dsb)
                vb = _b16(vc)
                for rows, ref, cols in _sub_blocks(rev):
                    qs, kk, eq, ek = _sub_operands(qc, kc, cc, rows, ref, cols)
                    mask = _sub_mask(rows, cols, rev)
                    rsl = slice(rows[0], rows[0] + rows[1])
                    csl = pl.ds(cols[0], cols[1])
                    qsb, kkb = _b16(qs), _b16(kk)
                    a = jnp.where(mask, _nt(qsb, kkb), 0.0)
                    da = _b16(jnp.where(mask, _nt(dob[rsl], vb[cols[0]:cols[0] + cols[1]]), 0.0))
                    dq_acc[h, pl.ds(rows[0], rows[1]), :] += _nn(da, kkb) * eq
                    dk_acc[h, csl, :] += _tn(da, qsb) * ek
                    dv_acc[h, csl, :] += _tn(_b16(a), dob[rsl])
                dq = dq_acc[h]
                dk = dk_acc[h]
                dc = qc * dq - kc * dk
                bnd = jnp.sum(ds_out * s_out, axis=0, keepdims=True)
                edge_row = 0 if rev else CHUNK - 1
                is_edge = lax.broadcasted_iota(jnp.int32, (CHUNK, HEAD_K), 0) == edge_row
                dq_ref[0, rws, kcols] = dq
                dk_ref[0, rws, kcols] = dk
                dv_ref[0, rws, vcols] = dv_acc[h]
                dc_ref[0, rws, kcols] = dc + jnp.where(is_edge, bnd, 0.0)
            return carry

        lax.fori_loop(0, SCAN_CHUNKS, chunk, 0)

    def step_of(u):
        return SCAN_STEPS - 1 - u

    def spec(w):
        return pl.BlockSpec((1, SCAN_ROWS, w), lambda b, u: (b, _scan_block(step_of(u), rev), 0))

    return _pallas(
        body, name=name, grid=(n, SCAN_STEPS),
        in_specs=[spec(GLA_DK), spec(GLA_DK), spec(D), spec(GLA_DK),
                  pl.BlockSpec((1, HEADS, SCAN_CHUNKS, HEAD_V, HEAD_K), lambda b, u: (b, 0, step_of(u), 0, 0)),
                  pl.BlockSpec((1, HEADS, HEAD_V, HEAD_K), lambda b, u: (b, 0, 0, 0)),
                  pl.BlockSpec((1, SCAN_ROWS, D), lambda b, u: (b, _scan_lat_block(step_of(u), rev), 0))],
        out_specs=[spec(GLA_DK), spec(GLA_DK), spec(D), spec(GLA_DK)],
        out_shape=[jax.ShapeDtypeStruct((n, SEQ_ALL, GLA_DK), F32), jax.ShapeDtypeStruct((n, SEQ_ALL, GLA_DK), F32),
                   jax.ShapeDtypeStruct((n, SEQ_ALL, D), F32), jax.ShapeDtypeStruct((n, SEQ_ALL, GLA_DK), F32)],
        scratch_shapes=[pltpu.VMEM((HEADS, HEAD_V, HEAD_K), F32), pltpu.VMEM((HEADS, HEAD_V, HEAD_K), F32),
                        pltpu.VMEM((HEADS, CHUNK, HEAD_K), F32), pltpu.VMEM((HEADS, CHUNK, HEAD_K), F32),
                        pltpu.VMEM((HEADS, CHUNK, HEAD_V), F32)],
        compiler_params=_params(("parallel", "arbitrary")),
    )(q, k, v, cum, s_all, s_fin, do)


def gla_out_fwd(o_f, o_b, r, gnorm):
    n = o_f.shape[0]
    tiles = SEQ // TM_EW

    def body(of_ref, ob_ref, r_ref, g_ref, og_ref):
        for h in range(HEADS):
            cols = pl.ds(h * HEAD_V, HEAD_V)
            o = of_ref[0, :, cols] + ob_ref[0, :, cols]
            rs = lax.rsqrt(jnp.mean(o * o, axis=-1, keepdims=True) + EPS)
            og_ref[:, cols] = (o * rs * g_ref[...] * _silu(r_ref[:, cols])).astype(og_ref.dtype)

    ospec = pl.BlockSpec((1, TM_EW, D), lambda b, j: (b, j, 0))
    row = pl.BlockSpec((TM_EW, D), lambda b, j: (b * tiles + j, 0))
    return _pallas(
        body, name="gla_out_fwd", grid=(n, tiles),
        in_specs=[ospec, ospec, row, pl.BlockSpec((1, HEAD_V), lambda b, j: (0, 0))],
        out_specs=row, out_shape=jax.ShapeDtypeStruct((n * SEQ, D), BF16),
        compiler_params=_params(("parallel", "parallel")),
    )(o_f, o_b, r, gnorm)


def gla_out_bwd(o_f, o_b, r, dog, gnorm):
    n = o_f.shape[0]
    tiles = SEQ // TM_EW

    def body(of_ref, ob_ref, r_ref, d_ref, g_ref, do_ref, dr_ref, dg_ref):
        @pl.when((pl.program_id(0) == 0) & (pl.program_id(1) == 0))
        def _():
            dg_ref[...] = jnp.zeros_like(dg_ref)

        for h in range(HEADS):
            cols = pl.ds(h * HEAD_V, HEAD_V)
            o = of_ref[0, :, cols] + ob_ref[0, :, cols]
            rv = r_ref[:, cols]
            dv = d_ref[:, cols]
            rs = lax.rsqrt(jnp.mean(o * o, axis=-1, keepdims=True) + EPS)
            oh = o * rs
            dr_ref[:, cols] = (dv * oh * g_ref[...] * _dsilu(rv)).astype(dr_ref.dtype)
            dn = dv * _silu(rv)
            dg_ref[...] += jnp.sum(dn * oh, axis=0, keepdims=True)
            doh = dn * g_ref[...]
            do_ref[0, :, cols] = rs * (doh - oh * jnp.mean(doh * oh, axis=-1, keepdims=True))

    ospec = pl.BlockSpec((1, TM_EW, D), lambda b, j: (b, j, 0))
    row = pl.BlockSpec((TM_EW, D), lambda b, j: (b * tiles + j, 0))
    vec = pl.BlockSpec((1, HEAD_V), lambda b, j: (0, 0))
    return _pallas(
        body, name="gla_out_bwd", grid=(n, tiles),
        in_specs=[ospec, ospec, row, row, vec],
        out_specs=[ospec, row, vec],
        out_shape=[jax.ShapeDtypeStruct((n, SEQ, D), F32), jax.ShapeDtypeStruct((n * SEQ, D), BF16),
                   jax.ShapeDtypeStruct((1, HEAD_V), F32)],
        compiler_params=_params(("arbitrary", "arbitrary")),
    )(o_f, o_b, r, dog, gnorm)


def merge_fwd(p5, y_conv, y_gla):
    t = y_conv.shape[0]

    def body(mc_ref, mg_ref, yc_ref, yg_ref, o_ref):
        o_ref[...] = (_sigmoid(mc_ref[...]) * yc_ref[...] + _sigmoid(mg_ref[...]) * yg_ref[...]).astype(o_ref.dtype)

    row = pl.BlockSpec((TM_EW, D), lambda i: (i, 0))
    return _pallas(
        body, name="merge_fwd", grid=(t // TM_EW,),
        in_specs=[row, pl.BlockSpec((TM_EW, D), lambda i: (i, 1)), row, row], out_specs=row,
        out_shape=jax.ShapeDtypeStruct((t, D), BF16), compiler_params=_params(("parallel",)),
    )(p5, p5, y_conv, y_gla)


def merge_bwd(p5, y_conv, y_gla, dmerged):
    t = y_conv.shape[0]

    def body(mc_ref, mg_ref, yc_ref, yg_ref, d_ref, dyc_ref, dyg_ref, dp_ref):
        d = d_ref[...]
        sc = _sigmoid(mc_ref[...])
        sg = _sigmoid(mg_ref[...])
        dyc_ref[...] = (d * sc).astype(dyc_ref.dtype)
        dyg_ref[...] = (d * sg).astype(dyg_ref.dtype)
        dp_ref[:, pl.ds(0, D)] = (d * yc_ref[...] * sc * (1.0 - sc)).astype(dp_ref.dtype)
        dp_ref[:, pl.ds(D, D)] = (d * yg_ref[...] * sg * (1.0 - sg)).astype(dp_ref.dtype)

    row = pl.BlockSpec((TM_EW, D), lambda i: (i, 0))
    return _pallas(
        body, name="merge_bwd", grid=(t // TM_EW,),
        in_specs=[row, pl.BlockSpec((TM_EW, D), lambda i: (i, 1)), row, row, row],
        out_specs=[row, row, pl.BlockSpec((TM_EW, 2 * D), lambda i: (i, 0))],
        out_shape=[jax.ShapeDtypeStruct((t, D), BF16), jax.ShapeDtypeStruct((t, D), BF16),
                   jax.ShapeDtypeStruct((t, 2 * D), BF16)],
        compiler_params=_params(("parallel",)),
    )(p5, p5, y_conv, y_gla, dmerged)


def final_fwd_bwd(x2, mo, gate, final_g, target, n_samples):
    t = x2.shape[0]
    tiles = SEQ // TM_EW

    def body(x_ref, mo_ref, gate_ref, g_ref, t_ref, dh_ref, dmo_ref, dgate_ref, dg_ref, loss_ref):
        b, j = pl.program_id(0), pl.program_id(1)
        mo_v = mo_ref[...]
        h = x_ref[...] + gate_ref[0] * mo_v
        rs = lax.rsqrt(jnp.mean(h * h, axis=-1, keepdims=True) + EPS)
        nh = h * rs
        err = nh * g_ref[...] - t_ref[...]
        dy = err * (1.0 / D)
        dn = dy * g_ref[...]
        dh = rs * (dn - nh * jnp.mean(dn * nh, axis=-1, keepdims=True))
        dh_ref[...] = dh
        dmo_ref[...] = (dh * gate_ref[0]).astype(dmo_ref.dtype)

        @pl.when(j == 0)
        def _():
            dgate_ref[...] = jnp.zeros_like(dgate_ref)

        @pl.when((b == 0) & (j == 0))
        def _():
            dg_ref[...] = jnp.zeros_like(dg_ref)
            loss_ref[...] = jnp.zeros_like(loss_ref)

        dgate_ref[0] += jnp.sum(dh * mo_v, axis=0, keepdims=True)
        dg_ref[...] += jnp.sum(dy * nh, axis=0, keepdims=True)
        loss_ref[...] += (0.5 / D) * jnp.sum(err * err)

    row = pl.BlockSpec((TM_EW, D), lambda b, j: (b * tiles + j, 0))
    per = pl.BlockSpec((1, 1, D), lambda b, j: (b, 0, 0))
    vec = pl.BlockSpec((1, D), lambda b, j: (0, 0))
    return _pallas(
        body, name="final_fwd_bwd", grid=(n_samples, tiles),
        in_specs=[row, row, per, vec, row],
        out_specs=[row, row, per, vec, pl.BlockSpec((8, 128), lambda b, j: (0, 0))],
        out_shape=[jax.ShapeDtypeStruct((t, D), F32), jax.ShapeDtypeStruct((t, D), BF16),
                   jax.ShapeDtypeStruct((n_samples, 1, D), F32), jax.ShapeDtypeStruct((1, D), F32),
                   jax.ShapeDtypeStruct((8, 128), F32)],
        compiler_params=_params(("arbitrary", "arbitrary")),
    )(x2, mo, gate, final_g, target)


def local_step(x, ctx, target, mod, wts, small):
    n = x.shape[0]
    t = n * SEQ
    t_all = t + n * NCTX
    x2 = x.reshape(t, D)
    ctx2 = ctx.reshape(n * NCTX, D)
    tgt2 = target.reshape(t, D)
    scale1, shift, gate = mod

    u = norm_mod_fwd(x2, ctx2, scale1, shift, small["norm_g"])
    p1 = matmul_nn(u, wts["w1"], small["b1"], name="proj_conv", m=t, tm=512, tn=512, out_dtype=F32)
    p2 = matmul_nn(u, wts["w2"], small["b2"], name="proj_z", m=t, tm=512, tn=512, out_dtype=F32)
    p3 = matmul_nn(u, wts["w3"], small["b3"], name="proj_gla", m=t_all, tm=512, tn=W3, out_dtype=F32)
    p4 = matmul_nn(u, wts["w4"], small["b4"], name="proj_r", m=t, tm=512, tn=512, out_dtype=F32)
    p5 = matmul_nn(u, wts["w5"], small["b5"], name="proj_merge", m=t, tm=512, tn=512, out_dtype=F32)

    aconv = conv_fwd(p1, small["conv_w"], small["conv_b"], n)
    ac = ln_gate_fwd(aconv, p2, small["conv_ln_g"], small["conv_ln_b"])
    y_conv = matmul_nn(ac, wts["conv_proj"], None, name="conv_proj_fwd", m=t, tm=512, tn=512, out_dtype=F32)

    qs, ks, vs, cum_f, cum_b = gla_prep_fwd(p3, small["upf"], small["upb"], small["bias_f"], small["bias_b"], n)
    o_f, s_f, sfin_f = gla_scan_fwd(qs, ks, vs, cum_f, rev=False, name="gla_scan_fwd_f")
    o_b, s_b, sfin_b = gla_scan_fwd(qs, ks, vs, cum_b, rev=True, name="gla_scan_fwd_b")
    og = gla_out_fwd(o_f, o_b, p4, small["gla_norm_g"])
    y_gla = matmul_nn(og, wts["gla_proj"], None, name="gla_proj_fwd", m=t, tm=512, tn=512, out_dtype=F32)

    merged = merge_fwd(p5, y_conv, y_gla)
    mo = matmul_nn(merged, wts["w_out"], None, name="w_out_fwd", m=t, tm=512, tn=512, out_dtype=F32)
    dh, dmo, dgate, d_final_g, loss = final_fwd_bwd(x2, mo, gate, small["final_norm_g"], tgt2, n)

    g = {"final_norm_g": d_final_g}
    dmerged = matmul_nt(dmo, wts["w_out"], name="w_out_dgrad", tm=512)
    g["w_out"] = matmul_tn(merged, dmo, name="w_out_wgrad", t=t, tn=512, tt=512)[0]
    dyc, dyg, dp5 = merge_bwd(p5, y_conv, y_gla, dmerged)

    dac = matmul_nt(dyc, wts["conv_proj"], name="conv_proj_dgrad", tm=512)
    g["conv_proj"] = matmul_tn(ac, dyc, name="conv_proj_wgrad", t=t, tn=512, tt=512)[0]
    daconv, dp2, g["conv_ln_g"], g["conv_ln_b"] = ln_gate_bwd(aconv, p2, dac, small["conv_ln_g"], small["conv_ln_b"])
    dp1, dconv_w, dconv_b = conv_bwd(p1, daconv, small["conv_w"], n)
    g["conv_w"], g["conv_b"] = dconv_w, dconv_b

    dog = matmul_nt(dyg, wts["gla_proj"], name="gla_proj_dgrad", tm=512)
    g["gla_proj"] = matmul_tn(og, dyg, name="gla_proj_wgrad", t=t, tn=512, tt=512)[0]
    do, dp4, g["gla_norm_g"] = gla_out_bwd(o_f, o_b, p4, dog, small["gla_norm_g"])
    dq_f, dk_f, dv_f, dc_f = gla_scan_bwd(qs, ks, vs, cum_f, s_f, sfin_f, do, rev=False, name="gla_scan_bwd_f")
    dq_b, dk_b, dv_b, dc_b = gla_scan_bwd(qs, ks, vs, cum_b, s_b, sfin_b, do, rev=True, name="gla_scan_bwd_b")
    dp3, g["upf"], g["upb"], g["bias_f"], g["bias_b"] = gla_prep_bwd(
        p3, dq_f, dq_b, dk_f, dk_b, dv_f, dv_b, dc_f, dc_b,
        small["upf"], small["upb"], small["bias_f"], small["bias_b"], n)

    dps = [dp1, dp2, dp3, dp4, dp5]
    du = dgrad_multi(dps, [wts["w%d" % (i + 1)] for i in range(5)], t_all=t_all, t_lat=t, tm=256)
    for i, dp in enumerate(dps):
        rows = dp.shape[0]
        tn = W3 if dp.shape[1] == W3 else 512
        g["w%d" % (i + 1)], g["b%d" % (i + 1)] = matmul_tn(
            u, dp, name="w_in_wgrad_%d" % (i + 1), t=rows, tn=tn, tt=512, colsum=True)
    grad_x, dshift, dscale, g["norm_g"] = norm_mod_bwd(x2, ctx2, du, dh, scale1, small["norm_g"])
    g["shift"], g["scale"], g["gate"] = dshift, dscale, dgate
    return loss, grad_x, g


def _group_cols(w):
    gv, gg, z = w[..., 0:1024], w[..., 1024:2048], w[..., 2048:3072]
    q, k, v = w[..., 3072:3584], w[..., 3584:4096], w[..., 4096:5120]
    ab = w[..., 5120:5152]
    r, mc, mg = w[..., 5152:6176], w[..., 6176:7200], w[..., 7200:8224]
    g1 = jnp.concatenate([p for j in range(CONV_NCB)
                          for p in (gv[..., CONV_CB * j:CONV_CB * (j + 1)], gg[..., CONV_CB * j:CONV_CB * (j + 1)])], -1)
    pad = jnp.zeros(w.shape[:-1] + (W3 - 2080,), w.dtype)
    g3 = jnp.concatenate([v, q, k, ab, pad], -1)
    return g1, z, g3, r, jnp.concatenate([mc, mg], -1)


def _ungroup_cols(g1, g2, g3, g4, g5):
    gv = jnp.concatenate([g1[..., 2 * CONV_CB * j:2 * CONV_CB * j + CONV_CB] for j in range(CONV_NCB)], -1)
    gg = jnp.concatenate([g1[..., 2 * CONV_CB * j + CONV_CB:2 * CONV_CB * (j + 1)] for j in range(CONV_NCB)], -1)
    v, q, k, ab = g3[..., 0:1024], g3[..., 1024:1536], g3[..., 1536:2048], g3[..., 2048:2080]
    return jnp.concatenate([gv, gg, g2, q, k, v, ab, g4, g5[..., 0:1024], g5[..., 1024:2048]], -1)


def _pad_up(up, row0):
    return jnp.zeros((128, GLA_DK), F32).at[row0:row0 + up.shape[0]].set(up)


def _adamw_math(w, g, m, v):
    m = ADAM_B1 * m + (1.0 - ADAM_B1) * g
    v = ADAM_B2 * v + (1.0 - ADAM_B2) * (g * g)
    m_hat = m / (1.0 - ADAM_B1 ** ADAM_STEP)
    v_hat = v / (1.0 - ADAM_B2 ** ADAM_STEP)
    delta = -ADAM_LR * (m_hat / (jnp.sqrt(v_hat) + ADAM_EPS) + ADAM_WD * w)
    return delta, m, v


def adamw2d(w, g, m, v, *, name, tr):
    rows, cols = w.shape

    def body(w_ref, g_ref, m_ref, v_ref, d_ref, nm_ref, nv_ref):
        d_ref[...], nm_ref[...], nv_ref[...] = _adamw_math(w_ref[...], g_ref[...], m_ref[...], v_ref[...])

    spec = pl.BlockSpec((tr, cols), lambda i: (i, 0))
    return _pallas(
        body, name=name, grid=(rows // tr,), in_specs=[spec] * 4, out_specs=[spec] * 3,
        out_shape=[jax.ShapeDtypeStruct((rows, cols), F32)] * 3, compiler_params=_params(("parallel",)),
    )(w, g, m, v)


def sum_devices(sall):
    rows = sall.shape[1]

    def body(s_ref, o_ref):
        acc = s_ref[0]
        for d in range(1, N_DEV):
            acc = acc + s_ref[d]
        o_ref[...] = acc

    return _pallas(body, name="sum_devices", out_shape=jax.ShapeDtypeStruct((rows, D), F32),
                   compiler_params=_params())(sall)


def pair_add(core, g, got, *, name, tr):
    n, rows, cols = got.shape
    g4 = g.reshape(n, 2, rows, cols)

    def body(core_ref, g_ref, got_ref, o_ref, ob_ref):
        del core_ref
        s = g_ref[0, 0] + got_ref[0]
        o_ref[0] = s
        ob_ref[0] = s.astype(BF16)

    spec = pl.BlockSpec((1, tr, cols), lambda i, t, core_ref: (i, t, 0))
    return _pallas(
        body, name=name,
        grid_spec=pltpu.PrefetchScalarGridSpec(
            num_scalar_prefetch=1, grid=(n, rows // tr),
            in_specs=[pl.BlockSpec((1, 1, tr, cols), lambda i, t, core_ref: (i, core_ref[0], t, 0)), spec],
            out_specs=[spec, spec]),
        out_shape=[jax.ShapeDtypeStruct(got.shape, F32), jax.ShapeDtypeStruct(got.shape, BF16)],
        compiler_params=_params(("parallel", "parallel")))(core, g4, got)


def chip_add(place, pa, rb, *, name, tr):
    _, rows, cols = pa.shape

    def body(place_ref, m_ref, r_ref, o_ref):
        del place_ref
        o_ref[0] = ((m_ref[0] + r_ref[0].astype(F32)) + r_ref[1].astype(F32)) + r_ref[2].astype(F32)

    return _pallas(
        body, name=name,
        grid_spec=pltpu.PrefetchScalarGridSpec(
            num_scalar_prefetch=1, grid=(rows // tr,),
            in_specs=[pl.BlockSpec((1, tr, cols), lambda t, place_ref: (place_ref[0], t, 0)),
                      pl.BlockSpec((3, tr, cols), lambda t, place_ref: (0, t, 0))],
            out_specs=pl.BlockSpec((1, tr, cols), lambda t, place_ref: (place_ref[1], t, 0))),
        out_shape=jax.ShapeDtypeStruct((2, rows, cols), F32),
        compiler_params=_params(("parallel",)))(place, pa, rb)


def ada_bwd(call, cctx_rows, dm_shard, dm_full, adaw):
    nsh = adaw.shape[1]

    def body(c_ref, cc_ref, dms_ref, dmf_ref, w_ref, gw_ref, gb_ref, pq_ref):
        a_lat = _silu(c_ref[...])
        a_ctx = _silu(cc_ref[...])
        dms = dms_ref[...]
        gw_ref[...] = _tn(a_lat, dms[0:64], HI) + _tn(a_ctx, dms[64:72], HI)
        gb_ref[...] = jnp.sum(dmf_ref[...], axis=0, keepdims=True)
        part = _nt(dms[64:72], w_ref[...], HI)
        pq_ref[...] = jnp.zeros_like(pq_ref) + jnp.sum(part, axis=0, keepdims=True)

    return _pallas(body, name="ada_bwd",
                   out_shape=[jax.ShapeDtypeStruct((D, nsh), F32), jax.ShapeDtypeStruct((1, 3 * D), F32),
                              jax.ShapeDtypeStruct((8, D), F32)],
                   compiler_params=_params())(call, cctx_rows, dm_shard, dm_full, adaw)


def cctx_grad(pq_all, cctx_rows):
    def body(p_ref, c_ref, o_ref):
        acc = p_ref[0]
        for qi in range(1, N_CHIPS):
            acc = acc + p_ref[qi]
        o_ref[...] = acc * _dsilu(c_ref[...])

    return _pallas(body, name="cctx_grad", out_shape=jax.ShapeDtypeStruct((8, D), F32),
                   compiler_params=_params())(pq_all, cctx_rows)


def _place():
    x, y, c = lax.axis_index("x"), lax.axis_index("y"), lax.axis_index("c")
    chips = [(1 - x, y), (x, 1 - y), (1 - x, 1 - y)]
    return x, y, c, chips


def _all_peers(x, y, c):
    return [((1 - x) if r & 4 else x, (1 - y) if r & 2 else y, (1 - c) if r & 1 else c) for r in range(1, N_DEV)]


def _remote(src, dst, send_sem, recv_sem, dev):
    return pltpu.make_async_remote_copy(src_ref=src, dst_ref=dst, send_sem=send_sem, recv_sem=recv_sem,
                                        device_id=dev, device_id_type=MESH)


ANY = pl.BlockSpec(memory_space=pl.ANY)
VMEM = pl.BlockSpec(memory_space=pltpu.VMEM)
F_ROWS = 16


W_ROW_CHUNKS = 4
P_ROW_CHUNKS = 2
N_BULK = W_ROW_CHUNKS + P_ROW_CHUNKS


def _half_chunks(core, n_rows, align):
    out = []
    for a, k in ((0, W_ROW_CHUNKS), (1, P_ROW_CHUNKS)):
        half = n_rows[a] // 2
        size = half // k
        for i in range(k):
            start = core * half + i * size
            out.append((a, pl.ds(start if isinstance(start, int) else pl.multiple_of(start, align), size)))
    return out


def gather_weights(c8, cctx8, adaw, adab, w_sh, p_sh, fp):
    nsh = adaw.shape[1]

    def body(c_ref, cctx_ref, adaw_ref, adab_ref, w_ref, p_ref, fp_ref, wall_ref, pall_ref, fall_ref, call_ref, mall_ref,
             abuf, w_send, w_recv, h_send, h_recv, c_send, c_recv, m_send, m_recv, f_send, f_recv):
        x, y, c, chips = _place()
        q = 2 * x + y
        dev = 4 * x + 2 * y + c
        qs = [2 * cx + cy for cx, cy in chips]
        sib = (x, y, 1 - c)
        srcs, dsts = (w_ref, p_ref), (wall_ref, pall_ref)
        n_rows = (w_ref.shape[0], p_ref.shape[0])
        mine = _half_chunks(c, n_rows, 16)
        other = _half_chunks(1 - c, n_rows, 16)

        bulk = [[_remote(srcs[a].at[rows], dsts[a].at[q, rows], w_send.at[j * N_BULK + i], w_recv.at[j * N_BULK + i],
                         (*chips[j], c)) for i, (a, rows) in enumerate(mine)] for j in range(3)]
        fall_ref[q] = fp_ref[...]
        small = [_remote(fp_ref, fall_ref.at[q], f_send.at[j], f_recv.at[j], (*chips[j], c)) for j in range(3)]
        my_rows = pl.ds(pl.multiple_of(8 * dev, 8), 8)
        call_ref[my_rows, :] = c_ref[...]
        cond = [_remote(c_ref, call_ref.at[my_rows, :], c_send.at[r], c_recv.at[r], peer)
                for r, peer in enumerate(_all_peers(x, y, c))]
        for cp in sum(bulk, []) + small + cond:
            cp.start()
        for cp in cond:
            cp.wait_recv()

        abuf[pl.ds(0, 64), :] = _silu(call_ref[...])
        abuf[pl.ds(64, 8), :] = _silu(cctx_ref[...])
        mall_ref[q] = _nn(abuf[...], adaw_ref[...], HI) + adab_ref[...]
        mod = [_remote(mall_ref.at[q], mall_ref.at[q], m_send.at[j], m_recv.at[j], (*chips[j], c)) for j in range(3)]
        for cp in mod:
            cp.start()

        handed = []
        for j in range(3):
            for i, (a, rows) in enumerate(mine):
                bulk[j][i].wait_recv()
                cp = _remote(dsts[a].at[qs[j], rows], dsts[a].at[qs[j], rows],
                             h_send.at[j * N_BULK + i], h_recv.at[j * N_BULK + i], sib)
                cp.start()
                handed.append(cp)
        for j in range(3):
            for i, (a, rows) in enumerate(other):
                _remote(dsts[a].at[qs[j], rows], dsts[a].at[qs[j], rows],
                        h_send.at[j * N_BULK + i], h_recv.at[j * N_BULK + i], sib).wait_recv()
        for cp in mod + small:
            cp.wait_recv()
        for cp in sum(bulk, []) + small + cond + mod + handed:
            cp.wait_send()

    def dma(n):
        return pltpu.SemaphoreType.DMA((n,))

    return _pallas(
        body, name="gather_weights",
        in_specs=[VMEM, VMEM, VMEM, VMEM, ANY, ANY, VMEM],
        out_specs=[ANY, ANY, VMEM, VMEM, VMEM],
        out_shape=[jax.ShapeDtypeStruct((N_CHIPS,) + w_sh.shape, BF16), jax.ShapeDtypeStruct((N_CHIPS,) + p_sh.shape, BF16),
                   jax.ShapeDtypeStruct((N_CHIPS, F_ROWS, D), F32),
                   jax.ShapeDtypeStruct((8 * N_DEV, D), F32), jax.ShapeDtypeStruct((N_CHIPS, MOD_ROWS, nsh), F32)],
        scratch_shapes=[pltpu.VMEM((MOD_ROWS, D), F32), dma(3 * N_BULK), dma(3 * N_BULK), dma(3 * N_BULK), dma(3 * N_BULK),
                        dma(7), dma(7), dma(3), dma(3), dma(3), dma(3)],
        compiler_params=_params(),
    )(c8, cctx8, adaw, adab, w_sh, p_sh, fp)


def gather_small_and_pair(sm, gw, gp):
    rows = sm.shape[0]
    n_pair = N_CHIPS * N_BULK

    def body(sm_ref, gw_ref, gp_ref, sall_ref, gotw_ref, gotp_ref, s_send, s_recv, a_send, a_recv):
        x, y, c, _ = _place()
        dev = 4 * x + 2 * y + c
        srcs, dsts = (gw_ref, gp_ref), (gotw_ref, gotp_ref)
        n_rows = (gw_ref.shape[1], gp_ref.shape[1])
        pair = []
        for i, ((a, rows_o), (_, rows_0)) in enumerate(zip(_half_chunks(1 - c, n_rows, 8), _half_chunks(0, n_rows, 8))):
            for s in range(N_CHIPS):
                k = s * N_BULK + i
                pair.append(_remote(srcs[a].at[s, rows_o], dsts[a].at[s, rows_0], a_send.at[k], a_recv.at[k], (x, y, 1 - c)))
        for cp in pair:
            cp.start()
        sall_ref[dev] = sm_ref[...]
        small = [_remote(sm_ref, sall_ref.at[dev], s_send.at[r], s_recv.at[r], peer)
                 for r, peer in enumerate(_all_peers(x, y, c))]
        for cp in small:
            cp.start()
        for cp in small + pair:
            cp.wait_recv()
        for cp in small + pair:
            cp.wait_send()

    return _pallas(
        body, name="gather_small_and_pair", in_specs=[VMEM, ANY, ANY], out_specs=[VMEM, ANY, ANY],
        out_shape=[jax.ShapeDtypeStruct((N_DEV, rows, D), F32),
                   jax.ShapeDtypeStruct((N_CHIPS, gw.shape[1] // 2, gw.shape[2]), F32),
                   jax.ShapeDtypeStruct((N_CHIPS, gp.shape[1] // 2, gp.shape[2]), F32)],
        scratch_shapes=[pltpu.SemaphoreType.DMA((7,)), pltpu.SemaphoreType.DMA((7,)),
                        pltpu.SemaphoreType.DMA((n_pair,)), pltpu.SemaphoreType.DMA((n_pair,))],
        compiler_params=_params(),
    )(sm, gw, gp)


def chip_exchange(paw, pap, pq):
    def body(paw_ref, pap_ref, pq_ref, rbw_ref, rbp_ref, pqa_ref, b_send, b_recv, p_send, p_recv):
        x, y, c, chips = _place()
        q = 2 * x + y
        qs = [2 * cx + cy for cx, cy in chips]
        srcs, dsts = (paw_ref, pap_ref), (rbw_ref, rbp_ref)
        n_rows = (2 * paw_ref.shape[1], 2 * pap_ref.shape[1])
        big = [_remote(srcs[a].at[qs[j], rows], dsts[a].at[j, rows], b_send.at[j * N_BULK + i], b_recv.at[j * N_BULK + i],
                       (*chips[j], c))
               for j in range(3) for i, (a, rows) in enumerate(_half_chunks(0, n_rows, 16))]
        pqa_ref[q] = pq_ref[...]
        small = [_remote(pq_ref, pqa_ref.at[q], p_send.at[j], p_recv.at[j], (*chips[j], c)) for j in range(3)]
        for cp in big + small:
            cp.start()
        for cp in small + big:
            cp.wait_recv()
        for cp in small + big:
            cp.wait_send()

    return _pallas(
        body, name="chip_exchange", in_specs=[ANY, ANY, VMEM], out_specs=[ANY, ANY, VMEM],
        out_shape=[jax.ShapeDtypeStruct((3,) + paw.shape[1:], BF16), jax.ShapeDtypeStruct((3,) + pap.shape[1:], BF16),
                   jax.ShapeDtypeStruct((N_CHIPS, 8, D), F32)],
        scratch_shapes=[pltpu.SemaphoreType.DMA((3 * N_BULK,)), pltpu.SemaphoreType.DMA((3 * N_BULK,)),
                        pltpu.SemaphoreType.DMA((3,)), pltpu.SemaphoreType.DMA((3,))],
        compiler_params=_params(),
    )(paw, pap, pq)


def pair_share(ghw, ghp):
    def body(ghw_ref, ghp_ref, outw_ref, outp_ref, send, recv):
        del ghw_ref, ghp_ref
        x, y, c, _ = _place()
        refs = (outw_ref, outp_ref)
        n_rows = (2 * outw_ref.shape[1], 2 * outp_ref.shape[1])
        pair = [_remote(refs[a].at[c, rows], refs[a].at[c, rows], send.at[i], recv.at[i], (x, y, 1 - c))
                for i, (a, rows) in enumerate(_half_chunks(0, n_rows, 8))]
        for cp in pair:
            cp.start()
        for i, (a, rows) in enumerate(_half_chunks(0, n_rows, 8)):
            _remote(refs[a].at[1 - c, rows], refs[a].at[1 - c, rows], send.at[i], recv.at[i], (x, y, 1 - c)).wait_recv()
        for cp in pair:
            cp.wait_send()

    return _pallas(
        body, name="pair_share", in_specs=[ANY, ANY], out_specs=[ANY, ANY],
        out_shape=[jax.ShapeDtypeStruct(ghw.shape, F32), jax.ShapeDtypeStruct(ghp.shape, F32)],
        scratch_shapes=[pltpu.SemaphoreType.DMA((N_BULK,)), pltpu.SemaphoreType.DMA((N_BULK,))],
        input_output_aliases={0: 0, 1: 1},
        compiler_params=_params(),
    )(ghw, ghp)


def _rows_of(shape):
    size = 1
    for s in shape:
        size *= s
    return -(-size // D)


def _pack(arrs, rows_multiple=8):
    parts = []
    total = 0
    for a in arrs:
        f = a.reshape(-1).astype(F32)
        r = _rows_of(a.shape)
        parts.append(jnp.pad(f, (0, r * D - f.shape[0])))
        total += r
    pad_rows = (-total) % rows_multiple
    if pad_rows:
        parts.append(jnp.zeros((pad_rows * D,), F32))
    return jnp.concatenate(parts).reshape(-1, D)


def _unpack(p, shapes):
    out = []
    r0 = 0
    for shp in shapes:
        r = _rows_of(shp)
        size = 1
        for s in shp:
            size *= s
        out.append(p[r0:r0 + r].reshape(-1)[:size].reshape(shp))
        r0 += r
    return out


WEIGHT_NAMES = ['c_ctx', 'ada_w', 'ada_b', 'norm_g', 'w_in', 'b_in', 'conv_w', 'conv_b', 'conv_ln_g', 'conv_ln_b',
                'conv_proj', 'decay_up_fwd', 'decay_bias_fwd', 'decay_up_bwd', 'decay_bias_bwd', 'gla_norm_g', 'gla_proj',
                'w_out', 'final_norm_g']
SMALL_NAMES = ['c_ctx', 'ada_b', 'norm_g', 'b_in', 'conv_w', 'conv_b', 'conv_ln_g', 'conv_ln_b', 'decay_up_fwd',
               'decay_bias_fwd', 'decay_up_bwd', 'decay_bias_bwd', 'gla_norm_g', 'final_norm_g']


def kernel(x, c, ctx, c_ctx, ada_w, ada_b, norm_g, w_in, b_in, conv_w, conv_b, conv_ln_g, conv_ln_b, conv_proj, decay_up_fwd, decay_bias_fwd, decay_up_bwd, decay_bias_bwd, gla_norm_g, gla_proj, w_out, final_norm_g, loss_target, m_c_ctx, m_ada_w, m_ada_b, m_norm_g, m_w_in, m_b_in, m_conv_w, m_conv_b, m_conv_ln_g, m_conv_ln_b, m_conv_proj, m_decay_up_fwd, m_decay_bias_fwd, m_decay_up_bwd, m_decay_bias_bwd, m_gla_norm_g, m_gla_proj, m_w_out, m_final_norm_g, v_c_ctx, v_ada_w, v_ada_b, v_norm_g, v_w_in, v_b_in, v_conv_w, v_conv_b, v_conv_ln_g, v_conv_ln_b, v_conv_proj, v_decay_up_fwd, v_decay_bias_fwd, v_decay_up_bwd, v_decay_bias_bwd, v_gla_norm_g, v_gla_proj, v_w_out, v_final_norm_g):
    w = dict(c_ctx=c_ctx, ada_w=ada_w, ada_b=ada_b, norm_g=norm_g, w_in=w_in, b_in=b_in, conv_w=conv_w, conv_b=conv_b,
             conv_ln_g=conv_ln_g, conv_ln_b=conv_ln_b, conv_proj=conv_proj, decay_up_fwd=decay_up_fwd,
             decay_bias_fwd=decay_bias_fwd, decay_up_bwd=decay_up_bwd, decay_bias_bwd=decay_bias_bwd,
             gla_norm_g=gla_norm_g, gla_proj=gla_proj, w_out=w_out, final_norm_g=final_norm_g)
    m = dict(c_ctx=m_c_ctx, ada_w=m_ada_w, ada_b=m_ada_b, norm_g=m_norm_g, w_in=m_w_in, b_in=m_b_in, conv_w=m_conv_w,
             conv_b=m_conv_b, conv_ln_g=m_conv_ln_g, conv_ln_b=m_conv_ln_b, conv_proj=m_conv_proj,
             decay_up_fwd=m_decay_up_fwd, decay_bias_fwd=m_decay_bias_fwd, decay_up_bwd=m_decay_up_bwd,
             decay_bias_bwd=m_decay_bias_bwd, gla_norm_g=m_gla_norm_g, gla_proj=m_gla_proj, w_out=m_w_out,
             final_norm_g=m_final_norm_g)
    v = dict(c_ctx=v_c_ctx, ada_w=v_ada_w, ada_b=v_ada_b, norm_g=v_norm_g, w_in=v_w_in, b_in=v_b_in, conv_w=v_conv_w,
             conv_b=v_conv_b, conv_ln_g=v_conv_ln_g, conv_ln_b=v_conv_ln_b, conv_proj=v_conv_proj,
             decay_up_fwd=v_decay_up_fwd, decay_bias_fwd=v_decay_bias_fwd, decay_up_bwd=v_decay_up_bwd,
             decay_bias_bwd=v_decay_bias_bwd, gla_norm_g=v_gla_norm_g, gla_proj=v_gla_proj, w_out=v_w_out,
             final_norm_g=v_final_norm_g)
    n = x.shape[0]
    ax, ay, ac = lax.axis_index("x"), lax.axis_index("y"), lax.axis_index("c")
    q = 2 * ax + ay
    dev = 4 * ax + 2 * ay + ac
    nsh = ada_w.shape[2]

    w_sh = w_in[0].astype(BF16)
    p_sh = jnp.concatenate([conv_proj[0], gla_proj[0], w_out[0]], 0).astype(BF16)
    fp = _pack([conv_w[0], decay_up_fwd[0], decay_up_bwd[0]], F_ROWS)
    c8 = jnp.pad(c, ((0, 8 - n), (0, 0)))
    cctx8 = jnp.pad(c_ctx[None], ((0, 7), (0, 0)))
    adab_sh = lax.dynamic_slice(ada_b, (0, q * nsh), (1, nsh))
    w_all, p_all, fall, call, mall = gather_weights(c8, cctx8, ada_w[0], adab_sh, w_sh, p_sh, fp)

    mod_all = jnp.transpose(mall, (1, 0, 2)).reshape(MOD_ROWS, 3 * D)
    mod_mine = lax.dynamic_slice(mod_all, (8 * dev, 0), (n, 3 * D))
    mod_ctx = mod_all[64:65]
    shift = jnp.concatenate([mod_mine[:, 0:D], mod_ctx[:, 0:D]], 0)[:, None, :]
    scale1 = 1.0 + jnp.concatenate([mod_mine[:, D:2 * D], mod_ctx[:, D:2 * D]], 0)[:, None, :]
    gate = mod_mine[:, 2 * D:3 * D][:, None, :]

    own = lambda i, mine, got: jnp.where(q == i, mine, got)
    g1, g2, g3, g4, g5 = _group_cols(jnp.concatenate([own(i, w_sh, w_all[i]) for i in range(N_CHIPS)], 1))
    p_full = jnp.stack([own(i, p_sh, p_all[i]) for i in range(N_CHIPS)])
    wts = dict(w1=g1, w2=g2, w3=g3, w4=g4, w5=g5,
               conv_proj=p_full[:, 0:256].reshape(D, D), gla_proj=p_full[:, 256:512].reshape(D, D),
               w_out=p_full[:, 512:768].reshape(D, D))
    f_parts = [_unpack(fall[i], [conv_w.shape[1:], decay_up_fwd.shape[1:], decay_up_bwd.shape[1:]]) for i in range(N_CHIPS)]
    conv_w_full = jnp.concatenate([p[0] for p in f_parts], 1)
    upf_full = jnp.concatenate([p[1] for p in f_parts], 1)
    upb_full = jnp.concatenate([p[2] for p in f_parts], 1)
    b1, b2, b3, b4, b5 = _group_cols(b_in)
    small = dict(b1=b1, b2=b2, b3=b3, b4=b4, b5=b5, norm_g=norm_g,
                 conv_w=jnp.pad(conv_w_full, ((0, 1), (0, 0))), conv_b=conv_b, conv_ln_g=conv_ln_g, conv_ln_b=conv_ln_b,
                 upf=_pad_up(upf_full, 0), upb=_pad_up(upb_full, 16), bias_f=decay_bias_fwd, bias_b=decay_bias_bwd,
                 gla_norm_g=gla_norm_g, final_norm_g=final_norm_g[None])

    loss_part, grad_x2, g = local_step(x, ctx, loss_target, (scale1, shift, gate), wts, small)
    loss = lax.psum(loss_part[0, 0], ("x", "y", "c"))

    dm_mine = jnp.concatenate([g["shift"][:n, 0], g["scale"][:n, 0], g["gate"][:, 0]], -1)
    dm_ctx = jnp.concatenate([g["shift"][n, 0], g["scale"][n, 0], jnp.zeros((D,), F32)], -1)
    d_b_in = _ungroup_cols(*[g["b%d" % i] for i in range(1, 6)])
    small_grads = [d_b_in, g["norm_g"], g["conv_b"].sum(0), g["conv_ln_g"], g["conv_ln_b"], g["bias_f"], g["bias_b"],
                   g["gla_norm_g"], g["final_norm_g"], g["conv_w"].sum(0)[:CONV_K], g["upf"][0:16], g["upb"][16:32],
                   dm_mine, dm_ctx]
    small_shapes = [a.shape for a in small_grads]
    sm = _pack(small_grads)
    d_w_in = _ungroup_cols(*[g["w%d" % i] for i in range(1, 6)])
    gw = jnp.stack([d_w_in[:, i * W_IN_SHARD:(i + 1) * W_IN_SHARD] for i in range(N_CHIPS)])
    gp = jnp.concatenate([g["conv_proj"].reshape(N_CHIPS, 256, D), g["gla_proj"].reshape(N_CHIPS, 256, D),
                          g["w_out"].reshape(N_CHIPS, 256, D)], 1)
    sall, gotw, gotp = gather_small_and_pair(sm, gw, gp)
    core = ac.astype(jnp.int32).reshape(1)
    chip = q.astype(jnp.int32).reshape(1)
    paw, paw16 = pair_add(core, gw, gotw, name="pair_add_w", tr=128)
    pap, pap16 = pair_add(core, gp, gotp, name="pair_add_p", tr=384)

    gsum = sum_devices(sall)
    (s_b_in, s_norm_g, s_conv_b, s_ln_g, s_ln_b, s_bias_f, s_bias_b, s_gla_g, s_final_g, s_conv_w, s_upf, s_upb,
     _, _) = _unpack(gsum, small_shapes)
    dm_rows = [_unpack(sall[i], small_shapes)[-2:] for i in range(N_DEV)]
    dm_full = jnp.concatenate(
        [jnp.pad(jnp.stack([r[0] for r in dm_rows]), ((0, 0), (0, 8 - n), (0, 0))).reshape(8 * N_DEV, 3 * D),
         jnp.stack([r[1] for r in dm_rows])], 0)
    dm_shard = lax.dynamic_slice(dm_full, (0, q * nsh), (MOD_ROWS, nsh))
    cctx_rows = jnp.broadcast_to(c_ctx[None], (8, D))
    g_ada_w, g_ada_b, pq = ada_bwd(call, cctx_rows, dm_shard, dm_full, ada_w[0])

    rbw, rbp, pq_all = chip_exchange(paw16, pap16, pq)
    place = jnp.concatenate([chip, core])
    ghw = chip_add(place, paw, rbw, name="chip_add_w", tr=128)
    ghp = chip_add(place, pap, rbp, name="chip_add_p", tr=384)
    gw_mine, gp_mine = pair_share(ghw, ghp)
    gp_mine = gp_mine.reshape(768, D)
    g_c_ctx = cctx_grad(pq_all, cctx_rows)[0]

    grads = dict(
        c_ctx=g_c_ctx, ada_w=g_ada_w[None], ada_b=g_ada_b, norm_g=s_norm_g,
        w_in=gw_mine.reshape(1, D, W_IN_SHARD), b_in=s_b_in,
        conv_w=lax.dynamic_slice(s_conv_w, (0, q * 256), (CONV_K, 256))[None], conv_b=s_conv_b,
        conv_ln_g=s_ln_g, conv_ln_b=s_ln_b, conv_proj=gp_mine[0:256][None],
        decay_up_fwd=lax.dynamic_slice(s_upf, (0, q * 128), (16, 128))[None], decay_bias_fwd=s_bias_f,
        decay_up_bwd=lax.dynamic_slice(s_upb, (0, q * 128), (16, 128))[None], decay_bias_bwd=s_bias_b,
        gla_norm_g=s_gla_g, gla_proj=gp_mine[256:512][None], w_out=gp_mine[512:768][None],
        final_norm_g=s_final_g[0])

    delta, new_m, new_v = {}, {}, {}
    for name, tr in [("ada_w", 128), ("w_in", 128), ("conv_proj", 128), ("gla_proj", 128), ("w_out", 128)]:
        shp = w[name].shape
        two = lambda a: a.reshape(shp[-2], shp[-1])
        d_, m_, v_ = adamw2d(two(w[name]), two(grads[name]), two(m[name]), two(v[name]), name="adamw_" + name, tr=tr)
        delta[name], new_m[name], new_v[name] = d_.reshape(shp), m_.reshape(shp), v_.reshape(shp)
    shapes = [w[nm].shape for nm in SMALL_NAMES]
    packs = [_pack([src[nm] for nm in SMALL_NAMES]) for src in (w, grads, m, v)]
    d_, m_, v_ = adamw2d(*packs, name="adamw_small", tr=packs[0].shape[0])
    for nm, a, b, cc in zip(SMALL_NAMES, _unpack(d_, shapes), _unpack(m_, shapes), _unpack(v_, shapes)):
        delta[nm], new_m[nm], new_v[nm] = a, b, cc

    grad_x = grad_x2.reshape(x.shape)
    return (loss, grad_x, *[grads[nm].reshape(w[nm].shape) for nm in WEIGHT_NAMES], *[delta[nm] for nm in WEIGHT_NAMES],
            *[new_m[nm] for nm in WEIGHT_NAMES], *[new_v[nm] for nm in WEIGHT_NAMES])
```

```python
import jax
import jax.numpy as jnp
from jax import lax
from jax.experimental import pallas as pl
from jax.experimental.pallas import tpu as pltpu

F32 = jnp.float32
BF16 = jnp.bfloat16
MESH = pl.DeviceIdType.MESH
HI = lax.Precision.HIGHEST

D = 1024
SEQ = 2048
GRID_W = 64
GRID_H = SEQ // GRID_W
NCTX = 256
SEQ_ALL = SEQ + NCTX
EPS = 1e-6
CONV_K = 31
CONV_PAD = CONV_K // 2
HEADS = 4
HEAD_K = 128
HEAD_V = 256
GLA_DK = HEADS * HEAD_K
GATE_TAU = 16.0
Q_SCALE = HEAD_K ** -0.5
CHUNK = 64
NCHUNK = SEQ_ALL // CHUNK
NCHUNK_LAT = SEQ // CHUNK
NCHUNK_CTX = NCHUNK - NCHUNK_LAT
SUB = 64
NSUB = CHUNK // SUB
N_IN = 8224
W3 = 2176
O3_V, O3_Q, O3_K, O3_AB = 0, 1024, 1536, 2048

ADAM_LR, ADAM_B1, ADAM_B2, ADAM_EPS, ADAM_WD, ADAM_STEP = 0.001, 0.9, 0.999, 1e-08, 0.01, 10
VMEM_LIMIT = 56 * 1024 * 1024

N_CHIPS = 4
N_DEV = 8
W_IN_SHARD = N_IN // N_CHIPS
MOD_ROWS = 72


def _pallas(body, **kw):
    return pl.pallas_call(body, **kw)


def _params(sem=None, **kw):
    if sem is not None:
        kw["dimension_semantics"] = sem
    return pltpu.CompilerParams(vmem_limit_bytes=VMEM_LIMIT, **kw)


def _sigmoid(v):
    return 1.0 / (1.0 + jnp.exp(-v))


def _silu(v):
    return v * _sigmoid(v)


def _dsilu(v):
    s = _sigmoid(v)
    return s * (1.0 + v * (1.0 - s))


def _log_sigmoid(v):
    return jnp.minimum(v, 0.0) - jnp.log(1.0 + jnp.exp(-jnp.abs(v)))


def _dot(a, b, dims, precision=None):
    return lax.dot_general(a, b, (dims, ((), ())), preferred_element_type=F32, precision=precision)


def _nn(a, b, precision=None):
    return _dot(a, b, ((1,), (0,)), precision)


def _nt(a, b, precision=None):
    return _dot(a, b, ((1,), (1,)), precision)


def _tn(a, b, precision=None):
    return _dot(a, b, ((0,), (0,)), precision)


def _b16(v):
    return v.astype(BF16)


def matmul_nn(a, b, bias, *, name, m, tm, tn, out_dtype):
    k = a.shape[1]
    n = b.shape[1]
    has_bias = bias is not None

    def body(*refs):
        if has_bias:
            a_ref, b_ref, bias_ref, o_ref = refs
            acc = _nn(a_ref[...], b_ref[...]) + bias_ref[...]
        else:
            a_ref, b_ref, o_ref = refs
            acc = _nn(a_ref[...], b_ref[...])
        o_ref[...] = acc.astype(o_ref.dtype)

    in_specs = [pl.BlockSpec((tm, k), lambda j, i: (i, 0)), pl.BlockSpec((k, tn), lambda j, i: (0, j))]
    args = [a, b]
    if has_bias:
        in_specs.append(pl.BlockSpec((1, tn), lambda j, i: (0, j)))
        args.append(bias)
    return _pallas(
        body, name=name, grid=(n // tn, m // tm), in_specs=in_specs,
        out_specs=pl.BlockSpec((tm, tn), lambda j, i: (i, j)),
        out_shape=jax.ShapeDtypeStruct((m, n), out_dtype),
        compiler_params=_params(("parallel", "parallel")),
    )(*args)


def matmul_nt(a, b, *, name, tm, out_dtype):
    m, k = a.shape
    n = b.shape[0]

    def body(a_ref, b_ref, o_ref):
        o_ref[...] = _nt(a_ref[...], b_ref[...]).astype(o_ref.dtype)

    return _pallas(
        body, name=name, grid=(m // tm,),
        in_specs=[pl.BlockSpec((tm, k), lambda i: (i, 0)), pl.BlockSpec((n, k), lambda i: (0, 0))],
        out_specs=pl.BlockSpec((tm, n), lambda i: (i, 0)),
        out_shape=jax.ShapeDtypeStruct((m, n), out_dtype),
        compiler_params=_params(("parallel",)),
    )(a, b)


def matmul_tn(a, b, *, name, t, tn, tt, colsum=False):
    m = a.shape[1]
    n = b.shape[1]

    def body(a_ref, b_ref, o_ref, *rest):
        @pl.when(pl.program_id(1) == 0)
        def _():
            o_ref[...] = jnp.zeros_like(o_ref)
            if colsum:
                rest[0][...] = jnp.zeros_like(rest[0])
        o_ref[...] += _tn(a_ref[...], b_ref[...])
        if colsum:
            rest[0][...] += jnp.sum(b_ref[...].astype(F32), axis=0, keepdims=True)

    out_specs = [pl.BlockSpec((m, tn), lambda j, s: (0, j))]
    out_shape = [jax.ShapeDtypeStruct((m, n), F32)]
    if colsum:
        out_specs.append(pl.BlockSpec((1, tn), lambda j, s: (0, j)))
        out_shape.append(jax.ShapeDtypeStruct((1, n), F32))
    return _pallas(
        body, name=name, grid=(n // tn, t // tt),
        in_specs=[pl.BlockSpec((tt, m), lambda j, s: (s, 0)), pl.BlockSpec((tt, tn), lambda j, s: (s, j))],
        out_specs=out_specs, out_shape=out_shape,
        compiler_params=_params(("parallel", "arbitrary")),
    )(a, b)


def dgrad_multi(dps, wts, *, t_all, t_lat, tm):
    steps = []
    tks = []
    for g, dp in enumerate(dps):
        width = dp.shape[1]
        tk = width if width % 1024 else 1024
        tks.append(tk)
        steps += [(g, s) for s in range(width // tk)]
    n_steps = len(steps)
    lo = [min(i for i, (g, _) in enumerate(steps) if g == gg) for gg in range(len(dps))]
    cnt = [sum(1 for (g, _) in steps if g == gg) for gg in range(len(dps))]
    n_lat = t_lat // tm
    n_g = len(dps)

    def body(*refs):
        dp_refs, w_refs, o_ref = refs[:n_g], refs[n_g:2 * n_g], refs[2 * n_g]
        i = pl.program_id(0)
        s = pl.program_id(1)

        @pl.when(s == 0)
        def _():
            o_ref[...] = jnp.zeros_like(o_ref)

        for g in range(n_g):
            in_rows = dps[g].shape[0] == t_all
            cond = (s >= lo[g]) & (s < lo[g] + cnt[g])
            if not in_rows:
                cond = cond & (i < n_lat)

            @pl.when(cond)
            def _(g=g):
                o_ref[...] += _nt(dp_refs[g][...], w_refs[g][...])

    in_specs = []
    for g, dp in enumerate(dps):
        nrow = dp.shape[0] // tm
        in_specs.append(pl.BlockSpec(
            (tm, tks[g]), lambda i, s, g=g, nrow=nrow: (jnp.minimum(i, nrow - 1), jnp.clip(s - lo[g], 0, cnt[g] - 1))))
    for g, w in enumerate(wts):
        in_specs.append(pl.BlockSpec((D, tks[g]), lambda i, s, g=g: (0, jnp.clip(s - lo[g], 0, cnt[g] - 1))))
    return _pallas(
        body, name="dgrad_w_in", grid=(t_all // tm, n_steps), in_specs=in_specs,
        out_specs=pl.BlockSpec((tm, D), lambda i, s: (i, 0)),
        out_shape=jax.ShapeDtypeStruct((t_all, D), F32),
        compiler_params=_params(("parallel", "arbitrary")),
    )(*dps, *wts)


TM_NORM = 512


def norm_mod_fwd(x2, ctx2, scale1, shift, norm_g):
    t = x2.shape[0]
    n_lat = t // TM_NORM
    assert ctx2.shape[0] == TM_NORM
    n_samples = scale1.shape[0] - 1
    tps = n_lat // n_samples

    def body(x_ref, c_ref, sc_ref, sh_ref, g_ref, u_ref):
        i = pl.program_id(0)
        xv = jnp.where(i < n_lat, x_ref[...], c_ref[...])
        rs = lax.rsqrt(jnp.mean(xv * xv, axis=-1, keepdims=True) + EPS)
        u = xv * rs * g_ref[...] * sc_ref[0] + sh_ref[0]
        u_ref[...] = u.astype(u_ref.dtype)

    grp = lambda i: (jnp.minimum(i // tps, n_samples), 0, 0)
    return _pallas(
        body, name="norm_mod_fwd", grid=(n_lat + 1,),
        in_specs=[pl.BlockSpec((TM_NORM, D), lambda i: (jnp.minimum(i, n_lat - 1), 0)),
                  pl.BlockSpec((TM_NORM, D), lambda i: (0, 0)),
                  pl.BlockSpec((1, 1, D), grp), pl.BlockSpec((1, 1, D), grp),
                  pl.BlockSpec((1, D), lambda i: (0, 0))],
        out_specs=pl.BlockSpec((TM_NORM, D), lambda i: (i, 0)),
        out_shape=jax.ShapeDtypeStruct((t + TM_NORM, D), BF16),
        compiler_params=_params(("parallel",)),
    )(x2, ctx2, scale1, shift, norm_g)


def norm_mod_bwd(x2, ctx2, du, dh, scale1, norm_g):
    t = x2.shape[0]
    n_lat = t // TM_NORM
    n_samples = scale1.shape[0] - 1
    tps = n_lat // n_samples
    n_grp = n_samples + 1

    def body(x_ref, c_ref, du_ref, dh_ref, sc_ref, g_ref, dx_ref, dsh_ref, dsc_ref, dg_ref):
        i = pl.program_id(0)
        xv = jnp.where(i < n_lat, x_ref[...], c_ref[...])
        rs = lax.rsqrt(jnp.mean(xv * xv, axis=-1, keepdims=True) + EPS)
        xh = xv * rs
        duv = du_ref[...]
        n = xh * g_ref[...]
        dn = duv * sc_ref[0]
        dxh = dn * g_ref[...]
        dx = rs * (dxh - xh * jnp.mean(dxh * xh, axis=-1, keepdims=True))
        @pl.when(i < n_lat)
        def _():
            dx_ref[...] = dx + dh_ref[...]

        @pl.when(i % tps == 0)
        def _():
            dsh_ref[...] = jnp.zeros_like(dsh_ref)
            dsc_ref[...] = jnp.zeros_like(dsc_ref)

        @pl.when(i == 0)
        def _():
            dg_ref[...] = jnp.zeros_like(dg_ref)

        dsh_ref[0] += jnp.sum(duv, axis=0, keepdims=True)
        dsc_ref[0] += jnp.sum(duv * n, axis=0, keepdims=True)
        dg_ref[...] += jnp.sum(dn * xh, axis=0, keepdims=True)

    grp = lambda i: (jnp.minimum(i // tps, n_samples), 0, 0)
    lat = lambda i: (jnp.minimum(i, n_lat - 1), 0)
    return _pallas(
        body, name="norm_mod_bwd", grid=(n_lat + 1,),
        in_specs=[pl.BlockSpec((TM_NORM, D), lat),
                  pl.BlockSpec((TM_NORM, D), lambda i: (0, 0)),
                  pl.BlockSpec((TM_NORM, D), lambda i: (i, 0)),
                  pl.BlockSpec((TM_NORM, D), lat),
                  pl.BlockSpec((1, 1, D), grp),
                  pl.BlockSpec((1, D), lambda i: (0, 0))],
        out_specs=[pl.BlockSpec((TM_NORM, D), lat),
                   pl.BlockSpec((1, 1, D), grp), pl.BlockSpec((1, 1, D), grp),
                   pl.BlockSpec((1, D), lambda i: (0, 0))],
        out_shape=[jax.ShapeDtypeStruct((t, D), F32),
                   jax.ShapeDtypeStruct((n_grp, 1, D), F32), jax.ShapeDtypeStruct((n_grp, 1, D), F32),
                   jax.ShapeDtypeStruct((1, D), F32)],
        compiler_params=_params(("arbitrary",)),
    )(x2, ctx2, du, dh, scale1, norm_g)


CONV_CB = 256
CONV_NCB = D // CONV_CB
H_OFF = 16


def _conv_pad_shape(vertical):
    if vertical:
        return (GRID_H + 2 * CONV_PAD, GRID_W, CONV_CB)
    return (GRID_H, GRID_W + 2 * H_OFF, CONV_CB)


def _conv_store(pad_ref, img, vertical):
    if vertical:
        pad_ref[pl.ds(CONV_PAD, GRID_H)] = img
    else:
        pad_ref[:, pl.ds(H_OFF, GRID_W), :] = img


def _conv_window(pad_ref, k, vertical, r):
    if vertical:
        return pad_ref[r + k]
    return pad_ref[r, pl.ds(H_OFF - CONV_PAD + k, GRID_W), :]


def _rows(r):
    return pl.ds(pl.multiple_of(r * GRID_W, GRID_W), GRID_W)


def conv_fwd(p1, conv_w, conv_b, n_samples):
    t = n_samples * SEQ

    def make(vertical, prev):
        def body(gv_ref, gg_ref, w_ref, b_ref, *rest):
            o_ref, pad_ref = rest[-2], rest[-1]
            pad_ref[...] = jnp.zeros_like(pad_ref)
            a = gv_ref[...].astype(F32) * _sigmoid(gg_ref[...].astype(F32))
            _conv_store(pad_ref, a.reshape(GRID_H, GRID_W, CONV_CB), vertical)

            def row(r, carry):
                acc = jnp.zeros((GRID_W, CONV_CB), F32) + b_ref[...]
                for k in range(CONV_K):
                    acc = acc + _conv_window(pad_ref, k, vertical, r) * w_ref[pl.ds(k, 1), :]
                o_ref[_rows(r), :] = acc
                return carry

            lax.fori_loop(0, GRID_H, row, 0)

        cb0 = CONV_NCB // 2 if vertical else 0
        in_specs = [pl.BlockSpec((SEQ, CONV_CB), lambda b, j: (b, 2 * (cb0 + j))),
                    pl.BlockSpec((SEQ, CONV_CB), lambda b, j: (b, 2 * (cb0 + j) + 1)),
                    pl.BlockSpec((CONV_K + 1, CONV_CB), lambda b, j: (0, cb0 + j)),
                    pl.BlockSpec((1, CONV_CB), lambda b, j: (0, cb0 + j))]
        args = [p1, p1, conv_w, conv_b]
        aliases = {}
        if prev is not None:
            in_specs.append(pl.BlockSpec(memory_space=pl.ANY))
            args.append(prev)
            aliases = {4: 0}
        return _pallas(
            body, name="conv_fwd_v" if vertical else "conv_fwd_h", grid=(n_samples, CONV_NCB // 2),
            in_specs=in_specs,
            out_specs=pl.BlockSpec((SEQ, CONV_CB), lambda b, j: (b, cb0 + j)),
            out_shape=jax.ShapeDtypeStruct((t, D), F32),
            scratch_shapes=[pltpu.VMEM(_conv_pad_shape(vertical), F32)],
            input_output_aliases=aliases,
            compiler_params=_params(("parallel", "parallel")),
        )(*args)

    return make(True, make(False, None))


def conv_bwd(p1, daconv, conv_w, n_samples):
    t = n_samples * SEQ

    def make(vertical, prev):
        def body(gv_ref, gg_ref, dy_ref, w_ref, *rest):
            dp_ref, dw_ref, db_ref, pad_ref, dpad_ref, da_ref = rest[-6:]
            pad_ref[...] = jnp.zeros_like(pad_ref)
            dpad_ref[...] = jnp.zeros_like(dpad_ref)
            gv = gv_ref[...].astype(F32)
            sg = _sigmoid(gg_ref[...].astype(F32))
            _conv_store(pad_ref, (gv * sg).reshape(GRID_H, GRID_W, CONV_CB), vertical)
            _conv_store(dpad_ref, dy_ref[...].reshape(GRID_H, GRID_W, CONV_CB), vertical)

            def row(r, carry):
                acc = jnp.zeros((GRID_W, CONV_CB), F32)
                for k in range(CONV_K):
                    acc = acc + _conv_window(dpad_ref, CONV_K - 1 - k, vertical, r) * w_ref[pl.ds(k, 1), :]
                da_ref[_rows(r), :] = acc
                return carry

            lax.fori_loop(0, GRID_H, row, 0)
            da = da_ref[...]
            dp_ref[:, pl.ds(0, CONV_CB)] = (da * sg).astype(dp_ref.dtype)
            dp_ref[:, pl.ds(CONV_CB, CONV_CB)] = (da * gv * sg * (1.0 - sg)).astype(dp_ref.dtype)

            for k in range(CONV_K):
                def wrow(r, acc, k=k):
                    return acc + _conv_window(pad_ref, k, vertical, r) * dy_ref[_rows(r), :]
                acc = lax.fori_loop(0, GRID_H, wrow, jnp.zeros((GRID_W, CONV_CB), F32))
                dw_ref[0, pl.ds(k, 1), :] = jnp.sum(acc, axis=0, keepdims=True)
            dw_ref[0, pl.ds(CONV_K, 1), :] = jnp.zeros((1, CONV_CB), F32)
            db_ref[0] = jnp.sum(dy_ref[...], axis=0, keepdims=True)

        cb0 = CONV_NCB // 2 if vertical else 0
        in_specs = [pl.BlockSpec((SEQ, CONV_CB), lambda b, j: (b, 2 * (cb0 + j))),
                    pl.BlockSpec((SEQ, CONV_CB), lambda b, j: (b, 2 * (cb0 + j) + 1)),
                    pl.BlockSpec((SEQ, CONV_CB), lambda b, j: (b, cb0 + j)),
                    pl.BlockSpec((CONV_K + 1, CONV_CB), lambda b, j: (0, cb0 + j))]
        args = [p1, p1, daconv, conv_w]
        aliases = {}
        if prev is not None:
            in_specs += [pl.BlockSpec(memory_space=pl.ANY)] * 3
            args += list(prev)
            aliases = {4: 0, 5: 1, 6: 2}
        return _pallas(
            body, name="conv_bwd_v" if vertical else "conv_bwd_h", grid=(n_samples, CONV_NCB // 2),
            in_specs=in_specs,
            out_specs=[pl.BlockSpec((SEQ, 2 * CONV_CB), lambda b, j: (b, cb0 + j)),
                       pl.BlockSpec((1, CONV_K + 1, CONV_CB), lambda b, j: (b, 0, cb0 + j)),
                       pl.BlockSpec((1, 1, CONV_CB), lambda b, j: (b, 0, cb0 + j))],
            out_shape=[jax.ShapeDtypeStruct((t, 2 * D), BF16),
                       jax.ShapeDtypeStruct((n_samples, CONV_K + 1, D), F32),
                       jax.ShapeDtypeStruct((n_samples, 1, D), F32)],
            scratch_shapes=[pltpu.VMEM(_conv_pad_shape(vertical), F32), pltpu.VMEM(_conv_pad_shape(vertical), F32),
                            pltpu.VMEM((SEQ, CONV_CB), F32)],
            input_output_aliases=aliases,
            compiler_params=_params(("parallel", "parallel")),
        )(*args)

    return make(True, make(False, None))


TM_EW = 256


def ln_gate_fwd(aconv, z, ln_g, ln_b):
    t = aconv.shape[0]

    def body(a_ref, z_ref, g_ref, b_ref, o_ref):
        a = a_ref[...]
        mu = jnp.mean(a, axis=-1, keepdims=True)
        xc = a - mu
        rstd = lax.rsqrt(jnp.mean(xc * xc, axis=-1, keepdims=True) + EPS)
        l = xc * rstd * g_ref[...] + b_ref[...]
        o_ref[...] = (_silu(l) * _silu(z_ref[...].astype(F32))).astype(o_ref.dtype)

    row = pl.BlockSpec((TM_EW, D), lambda i: (i, 0))
    vec = pl.BlockSpec((1, D), lambda i: (0, 0))
    return _pallas(
        body, name="ln_gate_fwd", grid=(t // TM_EW,), in_specs=[row, row, vec, vec], out_specs=row,
        out_shape=jax.ShapeDtypeStruct((t, D), BF16), compiler_params=_params(("parallel",)),
    )(aconv, z, ln_g, ln_b)


def ln_gate_bwd(aconv, z, dac, ln_g, ln_b):
    t = aconv.shape[0]

    def body(a_ref, z_ref, d_ref, g_ref, b_ref, da_ref, dz_ref, dg_ref, db_ref):
        a = a_ref[...]
        zv = z_ref[...].astype(F32)
        dac_v = d_ref[...].astype(F32)
        mu = jnp.mean(a, axis=-1, keepdims=True)
        xc = a - mu
        rstd = lax.rsqrt(jnp.mean(xc * xc, axis=-1, keepdims=True) + EPS)
        xh = xc * rstd
        l = xh * g_ref[...] + b_ref[...]
        dz_ref[...] = (dac_v * _silu(l) * _dsilu(zv)).astype(dz_ref.dtype)
        dl = dac_v * _silu(zv) * _dsilu(l)
        dxh = dl * g_ref[...]
        da_ref[...] = rstd * (dxh - jnp.mean(dxh, axis=-1, keepdims=True)
                              - xh * jnp.mean(dxh * xh, axis=-1, keepdims=True))

        @pl.when(pl.program_id(0) == 0)
        def _():
            dg_ref[...] = jnp.zeros_like(dg_ref)
            db_ref[...] = jnp.zeros_like(db_ref)

        dg_ref[...] += jnp.sum(dl * xh, axis=0, keepdims=True)
        db_ref[...] += jnp.sum(dl, axis=0, keepdims=True)

    row = pl.BlockSpec((TM_EW, D), lambda i: (i, 0))
    vec = pl.BlockSpec((1, D), lambda i: (0, 0))
    return _pallas(
        body, name="ln_gate_bwd", grid=(t // TM_EW,), in_specs=[row, row, row, vec, vec],
        out_specs=[row, row, vec, vec],
        out_shape=[jax.ShapeDtypeStruct((t, D), F32), jax.ShapeDtypeStruct((t, D), BF16),
                   jax.ShapeDtypeStruct((1, D), F32), jax.ShapeDtypeStruct((1, D), F32)],
        compiler_params=_params(("arbitrary",)),
    )(aconv, z, dac, ln_g, ln_b)


TM_PREP = 256
PREP_LAT = SEQ // TM_PREP
PREP_ALL = SEQ_ALL // TM_PREP


def _chunk_tri(n, upper):
    r = lax.broadcasted_iota(jnp.int32, (n, n), 0)
    c = lax.broadcasted_iota(jnp.int32, (n, n), 1)
    same = (r // CHUNK) == (c // CHUNK)
    keep = (c >= r) if upper else (c <= r)
    return jnp.where(same & keep, 1.0, 0.0).astype(F32)


def _prep_tile_maps(n_samples):
    n_lat = n_samples * PREP_LAT

    def seq_map(i):
        return jnp.where(i < n_lat, i // PREP_LAT, i - n_lat), jnp.where(i < n_lat, i % PREP_LAT, PREP_LAT)

    return n_lat, seq_map


def gla_prep_fwd(p3, upf, upb, bias_f, bias_b, n_samples):
    n_lat, seq_map = _prep_tile_maps(n_samples)
    n_tiles = n_lat + n_samples

    def body(v_ref, q_ref, k_ref, ab_ref, upf_ref, upb_ref, bf_ref, bb_ref, qo, ko, vo, cf, cb):
        i = pl.program_id(0)
        qo[0] = jnp.where(i < n_lat, q_ref[...].astype(F32) * Q_SCALE, 0.0)
        ko[0] = k_ref[...].astype(F32)
        vo[0] = v_ref[...].astype(F32)
        ab = ab_ref[...].astype(F32)
        gf = _log_sigmoid(_nn(ab, upf_ref[...], HI) + bf_ref[...]) * (1.0 / GATE_TAU)
        gb = _log_sigmoid(_nn(ab, upb_ref[...], HI) + bb_ref[...]) * (1.0 / GATE_TAU)
        cf[0] = _nn(_chunk_tri(TM_PREP, False), gf, HI)
        cb[0] = _nn(_chunk_tri(TM_PREP, True), gb, HI)

    def o_spec(w):
        return pl.BlockSpec((1, TM_PREP, w), lambda i: (*seq_map(i), 0))

    full = lambda shape: pl.BlockSpec(shape, lambda i: (0,) * len(shape))
    return _pallas(
        body, name="gla_prep_fwd", grid=(n_tiles,),
        in_specs=[pl.BlockSpec((TM_PREP, 1024), lambda i: (i, O3_V // 1024)),
                  pl.BlockSpec((TM_PREP, 512), lambda i: (i, O3_Q // 512)),
                  pl.BlockSpec((TM_PREP, 512), lambda i: (i, O3_K // 512)),
                  pl.BlockSpec((TM_PREP, 128), lambda i: (i, O3_AB // 128)),
                  full((128, GLA_DK)), full((128, GLA_DK)), full((1, GLA_DK)), full((1, GLA_DK))],
        out_specs=[o_spec(GLA_DK), o_spec(GLA_DK), o_spec(D), o_spec(GLA_DK), o_spec(GLA_DK)],
        out_shape=[jax.ShapeDtypeStruct((n_samples, SEQ_ALL, GLA_DK), F32),
                   jax.ShapeDtypeStruct((n_samples, SEQ_ALL, GLA_DK), F32),
                   jax.ShapeDtypeStruct((n_samples, SEQ_ALL, D), F32),
                   jax.ShapeDtypeStruct((n_samples, SEQ_ALL, GLA_DK), F32),
                   jax.ShapeDtypeStruct((n_samples, SEQ_ALL, GLA_DK), F32)],
        compiler_params=_params(("parallel",)),
    )(p3, p3, p3, p3, upf, upb, bias_f, bias_b)


def gla_prep_bwd(p3, dq_f, dq_b, dk_f, dk_b, dv_f, dv_b, dc_f, dc_b, upf, upb, bias_f, bias_b, n_samples):
    n_lat, seq_map = _prep_tile_maps(n_samples)
    n_tiles = n_lat + n_samples

    def body(ab_ref, dqf, dqb, dkf, dkb, dvf, dvb, dcf, dcb, upf_ref, upb_ref, bf_ref, bb_ref,
             dp_ref, duf_ref, dub_ref, dbf_ref, dbb_ref):
        i = pl.program_id(0)
        dp_ref[:, pl.ds(O3_V, D)] = (dvf[0] + dvb[0]).astype(dp_ref.dtype)
        dq = jnp.where(i < n_lat, (dqf[0] + dqb[0]) * Q_SCALE, 0.0)
        dp_ref[:, pl.ds(O3_Q, GLA_DK)] = dq.astype(dp_ref.dtype)
        dp_ref[:, pl.ds(O3_K, GLA_DK)] = (dkf[0] + dkb[0]).astype(dp_ref.dtype)
        ab = ab_ref[...].astype(F32)
        zf = _nn(ab, upf_ref[...], HI) + bf_ref[...]
        zb = _nn(ab, upb_ref[...], HI) + bb_ref[...]
        dgf = _nn(_chunk_tri(TM_PREP, True), dcf[0], HI)
        dgb = _nn(_chunk_tri(TM_PREP, False), dcb[0], HI)
        dzf = dgf * (1.0 / GATE_TAU) * _sigmoid(-zf)
        dzb = dgb * (1.0 / GATE_TAU) * _sigmoid(-zb)
        dab = _nt(dzf, upf_ref[...], HI) + _nt(dzb, upb_ref[...], HI)
        dp_ref[:, pl.ds(O3_AB, 128)] = dab.astype(dp_ref.dtype)

        @pl.when(i == 0)
        def _():
            duf_ref[...] = jnp.zeros_like(duf_ref)
            dub_ref[...] = jnp.zeros_like(dub_ref)
            dbf_ref[...] = jnp.zeros_like(dbf_ref)
            dbb_ref[...] = jnp.zeros_like(dbb_ref)

        duf_ref[...] += _tn(ab, dzf, HI)
        dub_ref[...] += _tn(ab, dzb, HI)
        dbf_ref[...] += jnp.sum(dzf, axis=0, keepdims=True)
        dbb_ref[...] += jnp.sum(dzb, axis=0, keepdims=True)

    def s_spec(w):
        return pl.BlockSpec((1, TM_PREP, w), lambda i: (*seq_map(i), 0))

    full = lambda shape: pl.BlockSpec(shape, lambda i: (0,) * len(shape))
    return _pallas(
        body, name="gla_prep_bwd", grid=(n_tiles,),
        in_specs=[pl.BlockSpec((TM_PREP, 128), lambda i: (i, O3_AB // 128)),
                  s_spec(GLA_DK), s_spec(GLA_DK), s_spec(GLA_DK), s_spec(GLA_DK), s_spec(D), s_spec(D),
                  s_spec(GLA_DK), s_spec(GLA_DK),
                  full((128, GLA_DK)), full((128, GLA_DK)), full((1, GLA_DK)), full((1, GLA_DK))],
        out_specs=[pl.BlockSpec((TM_PREP, W3), lambda i: (i, 0)),
                   full((128, GLA_DK)), full((128, GLA_DK)), full((1, GLA_DK)), full((1, GLA_DK))],
        out_shape=[jax.ShapeDtypeStruct((n_tiles * TM_PREP, W3), BF16),
                   jax.ShapeDtypeStruct((128, GLA_DK), F32), jax.ShapeDtypeStruct((128, GLA_DK), F32),
                   jax.ShapeDtypeStruct((1, GLA_DK), F32), jax.ShapeDtypeStruct((1, GLA_DK), F32)],
        compiler_params=_params(("arbitrary",)),
    )(p3, dq_f, dq_b, dk_f, dk_b, dv_f, dv_b, dc_f, dc_b, upf, upb, bias_f, bias_b)


def _sub_blocks(rev):
    if NSUB == 1:
        return [((0, CHUNK), CHUNK // 2, (0, CHUNK))]
    out = []
    for s in range(NSUB):
        rows = (s * SUB, SUB)
        if rev:
            ref = (s + 1) * SUB if s < NSUB - 1 else None
            cols = (s * SUB, CHUNK - s * SUB)
        else:
            ref = s * SUB - 1 if s > 0 else None
            cols = (0, (s + 1) * SUB)
        out.append((rows, ref, cols))
    return out


def _sub_mask(rows, cols, rev):
    r = rows[0] + lax.broadcasted_iota(jnp.int32, (rows[1], cols[1]), 0)
    c = cols[0] + lax.broadcasted_iota(jnp.int32, (rows[1], cols[1]), 1)
    return (c >= r) if rev else (c <= r)


def _sub_operands(qc, kc, cc, rows, ref, cols):
    cref = jnp.zeros((1, HEAD_K), F32) if ref is None else cc[ref:ref + 1]
    eq = jnp.exp(cc[rows[0]:rows[0] + rows[1]] - cref)
    ek = jnp.exp(cref - cc[cols[0]:cols[0] + cols[1]])
    qs = qc[rows[0]:rows[0] + rows[1]] * eq
    kk = kc[cols[0]:cols[0] + cols[1]] * ek
    return qs, kk, eq, ek


SCAN_ROWS = 256
SCAN_CHUNKS = SCAN_ROWS // CHUNK
SCAN_STEPS = SEQ_ALL // SCAN_ROWS
LAT_BLOCKS = SEQ // SCAN_ROWS


def _scan_block(t, rev):
    if rev:
        return SCAN_STEPS - 1 - t
    return jnp.where(t == 0, SCAN_STEPS - 1, t - 1)


def _scan_lat_block(t, rev):
    if rev:
        return jnp.minimum(SCAN_STEPS - 1 - t, LAT_BLOCKS - 1)
    return jnp.maximum(t - 1, 0)


def _head_cols(h):
    return pl.ds(h * HEAD_K, HEAD_K), pl.ds(h * HEAD_V, HEAD_V)


def gla_scan_fwd(q, k, v, cum, *, rev, name):
    n = q.shape[0]

    def body(q_ref, k_ref, v_ref, c_ref, o_ref, s_ref, sfin_ref, st):
        t = pl.program_id(1)

        @pl.when(t == 0)
        def _():
            st[...] = jnp.zeros_like(st)

        def chunk(j, carry):
            lj = SCAN_CHUNKS - 1 - j if rev else j
            r0 = pl.multiple_of(lj * CHUNK, CHUNK)
            rws = pl.ds(r0, CHUNK)
            for h in range(HEADS):
                kcols, vcols = _head_cols(h)
                qc, kc, cc = q_ref[0, rws, kcols], k_ref[0, rws, kcols], c_ref[0, rws, kcols]
                vc = v_ref[0, rws, vcols]
                s_in = st[h]
                s_ref[0, h, j] = s_in
                edge = cc[0:1] if rev else cc[CHUNK - 1:CHUNK]
                ke = kc * jnp.exp(edge - cc)
                st[h] = s_in * jnp.exp(edge) + _tn(_b16(vc), _b16(ke))
                o_inter = _nt(_b16(qc * jnp.exp(cc)), _b16(s_in))
                vb = _b16(vc)
                for rows, ref, cols in _sub_blocks(rev):
                    qs, kk, _, _ = _sub_operands(qc, kc, cc, rows, ref, cols)
                    a = jnp.where(_sub_mask(rows, cols, rev), _nt(_b16(qs), _b16(kk)), 0.0)
                    o_s = _nn(_b16(a), vb[cols[0]:cols[0] + cols[1]])
                    o_ref[0, pl.ds(r0 + rows[0], rows[1]), vcols] = o_inter[rows[0]:rows[0] + rows[1]] + o_s
            return carry

        lax.fori_loop(0, SCAN_CHUNKS, chunk, 0)

        @pl.when(t == SCAN_STEPS - 1)
        def _():
            sfin_ref[0] = st[...]

    def spec(w):
        return pl.BlockSpec((1, SCAN_ROWS, w), lambda b, t: (b, _scan_block(t, rev), 0))

    return _pallas(
        body, name=name, grid=(n, SCAN_STEPS),
        in_specs=[spec(GLA_DK), spec(GLA_DK), spec(D), spec(GLA_DK)],
        out_specs=[pl.BlockSpec((1, SCAN_ROWS, D), lambda b, t: (b, _scan_lat_block(t, rev), 0)),
                   pl.BlockSpec((1, HEADS, SCAN_CHUNKS, HEAD_V, HEAD_K), lambda b, t: (b, 0, t, 0, 0)),
                   pl.BlockSpec((1, HEADS, HEAD_V, HEAD_K), lambda b, t: (b, 0, 0, 0))],
        out_shape=[jax.ShapeDtypeStruct((n, SEQ, D), F32),
                   jax.ShapeDtypeStruct((n, HEADS, NCHUNK, HEAD_V, HEAD_K), F32),
                   jax.ShapeDtypeStruct((n, HEADS, HEAD_V, HEAD_K), F32)],
        scratch_shapes=[pltpu.VMEM((HEADS, HEAD_V, HEAD_K), F32)],
        compiler_params=_params(("parallel", "arbitrary")),
    )(q, k, v, cum)


def gla_scan_bwd(q, k, v, cum, s_all, s_fin, do, *, rev, name):
    n = q.shape[0]

    def body(q_ref, k_ref, v_ref, c_ref, s_ref, sfin_ref, do_ref, dq_ref, dk_ref, dv_ref, dc_ref,
             dst, s_next, dq_acc, dk_acc, dv_acc):
        t = SCAN_STEPS - 1 - pl.program_id(1)

        @pl.when(pl.program_id(1) == 0)
        def _():
            dst[...] = jnp.zeros_like(dst)
            s_next[...] = sfin_ref[0]

        def chunk(jj, carry):
            j = SCAN_CHUNKS - 1 - jj
            lj = SCAN_CHUNKS - 1 - j if rev else j
            rws = pl.ds(pl.multiple_of(lj * CHUNK, CHUNK), CHUNK)
            for h in range(HEADS):
                kcols, vcols = _head_cols(h)
                qc, kc, cc = q_ref[0, rws, kcols], k_ref[0, rws, kcols], c_ref[0, rws, kcols]
                vc = v_ref[0, rws, vcols]
                doc = jnp.where(t > 0, do_ref[0, rws, vcols], 0.0)
                s_in = s_ref[0, h, j]
                s_out = s_next[h]
                ds_out = dst[h]
                edge = cc[0:1] if rev else cc[CHUNK - 1:CHUNK]
                e_q = jnp.exp(cc)
                e_k = jnp.exp(edge - cc)
                dob = _b16(doc)
                dsb = _b16(ds_out)
                dst[h] = ds_out * jnp.exp(edge) + _tn(dob, _b16(qc * e_q))
                s_next[h] = s_in
                dq_acc[h] = e_q * _nn(dob, _b16(s_in))
                dk_acc[h] = e_k * _nn(_b16(vc), dsb)
                dv_acc[h] = _nt(_b16(kc * e_k), dsb)
                vb = _b16(vc)
                for rows, ref, cols in _sub_blocks(rev):
                    qs, kk, eq, ek = _sub_operands(qc, kc, cc, rows, ref, cols)
                    mask = _sub_mask(rows, cols, rev)
                    rsl = slice(rows[0], rows[0] + rows[1])
                    csl = pl.ds(cols[0], cols[1])
                    qsb, kkb = _b16(qs), _b16(kk)
                    a = jnp.where(mask, _nt(qsb, kkb), 0.0)
                    da = _b16(jnp.where(mask, _nt(dob[rsl], vb[cols[0]:cols[0] + cols[1]]), 0.0))
                    dq_acc[h, pl.ds(rows[0], rows[1]), :] += _nn(da, kkb) * eq
                    dk_acc[h, csl, :] += _tn(da, qsb) * ek
                    dv_acc[h, csl, :] += _tn(_b16(a), dob[rsl])
                dq = dq_acc[h]
                dk = dk_acc[h]
                dc = qc * dq - kc * dk
                bnd = jnp.sum(ds_out * s_out, axis=0, keepdims=True)
                edge_row = 0 if rev else CHUNK - 1
                is_edge = lax.broadcasted_iota(jnp.int32, (CHUNK, HEAD_K), 0) == edge_row
                dq_ref[0, rws, kcols] = dq
                dk_ref[0, rws, kcols] = dk
                dv_ref[0, rws, vcols] = dv_acc[h]
                dc_ref[0, rws, kcols] = dc + jnp.where(is_edge, bnd, 0.0)
            return carry

        lax.fori_loop(0, SCAN_CHUNKS, chunk, 0)

    def step_of(u):
        return SCAN_STEPS - 1 - u

    def spec(w):
        return pl.BlockSpec((1, SCAN_ROWS, w), lambda b, u: (b, _scan_block(step_of(u), rev), 0))

    return _pallas(
        body, name=name, grid=(n, SCAN_STEPS),
        in_specs=[spec(GLA_DK), spec(GLA_DK), spec(D), spec(GLA_DK),
                  pl.BlockSpec((1, HEADS, SCAN_CHUNKS, HEAD_V, HEAD_K), lambda b, u: (b, 0, step_of(u), 0, 0)),
                  pl.BlockSpec((1, HEADS, HEAD_V, HEAD_K), lambda b, u: (b, 0, 0, 0)),
                  pl.BlockSpec((1, SCAN_ROWS, D), lambda b, u: (b, _scan_lat_block(step_of(u), rev), 0))],
        out_specs=[spec(GLA_DK), spec(GLA_DK), spec(D), spec(GLA_DK)],
        out_shape=[jax.ShapeDtypeStruct((n, SEQ_ALL, GLA_DK), F32), jax.ShapeDtypeStruct((n, SEQ_ALL, GLA_DK), F32),
                   jax.ShapeDtypeStruct((n, SEQ_ALL, D), F32), jax.ShapeDtypeStruct((n, SEQ_ALL, GLA_DK), F32)],
        scratch_shapes=[pltpu.VMEM((HEADS, HEAD_V, HEAD_K), F32), pltpu.VMEM((HEADS, HEAD_V, HEAD_K), F32),
                        pltpu.VMEM((HEADS, CHUNK, HEAD_K), F32), pltpu.VMEM((HEADS, CHUNK, HEAD_K), F32),
                        pltpu.VMEM((HEADS, CHUNK, HEAD_V), F32)],
        compiler_params=_params(("parallel", "arbitrary")),
    )(q, k, v, cum, s_all, s_fin, do)


def gla_out_fwd(o_f, o_b, r, gnorm):
    n = o_f.shape[0]
    tiles = SEQ // TM_EW

    def body(of_ref, ob_ref, r_ref, g_ref, og_ref):
        for h in range(HEADS):
            cols = pl.ds(h * HEAD_V, HEAD_V)
            o = of_ref[0, :, cols] + ob_ref[0, :, cols]
            rs = lax.rsqrt(jnp.mean(o * o, axis=-1, keepdims=True) + EPS)
            og_ref[:, cols] = (o * rs * g_ref[...] * _silu(r_ref[:, cols].astype(F32))).astype(og_ref.dtype)

    ospec = pl.BlockSpec((1, TM_EW, D), lambda b, j: (b, j, 0))
    row = pl.BlockSpec((TM_EW, D), lambda b, j: (b * tiles + j, 0))
    return _pallas(
        body, name="gla_out_fwd", grid=(n, tiles),
        in_specs=[ospec, ospec, row, pl.BlockSpec((1, HEAD_V), lambda b, j: (0, 0))],
        out_specs=row, out_shape=jax.ShapeDtypeStruct((n * SEQ, D), BF16),
        compiler_params=_params(("parallel", "parallel")),
    )(o_f, o_b, r, gnorm)


def gla_out_bwd(o_f, o_b, r, dog, gnorm):
    n = o_f.shape[0]
    tiles = SEQ // TM_EW

    def body(of_ref, ob_ref, r_ref, d_ref, g_ref, do_ref, dr_ref, dg_ref):
        @pl.when((pl.program_id(0) == 0) & (pl.program_id(1) == 0))
        def _():
            dg_ref[...] = jnp.zeros_like(dg_ref)

        for h in range(HEADS):
            cols = pl.ds(h * HEAD_V, HEAD_V)
            o = of_ref[0, :, cols] + ob_ref[0, :, cols]
            rv = r_ref[:, cols].astype(F32)
            dv = d_ref[:, cols].astype(F32)
            rs = lax.rsqrt(jnp.mean(o * o, axis=-1, keepdims=True) + EPS)
            oh = o * rs
            dr_ref[:, cols] = (dv * oh * g_ref[...] * _dsilu(rv)).astype(dr_ref.dtype)
            dn = dv * _silu(rv)
            dg_ref[...] += jnp.sum(dn * oh, axis=0, keepdims=True)
            doh = dn * g_ref[...]
            do_ref[0, :, cols] = rs * (doh - oh * jnp.mean(doh * oh, axis=-1, keepdims=True))

    ospec = pl.BlockSpec((1, TM_EW, D), lambda b, j: (b, j, 0))
    row = pl.BlockSpec((TM_EW, D), lambda b, j: (b * tiles + j, 0))
    vec = pl.BlockSpec((1, HEAD_V), lambda b, j: (0, 0))
    return _pallas(
        body, name="gla_out_bwd", grid=(n, tiles),
        in_specs=[ospec, ospec, row, row, vec],
        out_specs=[ospec, row, vec],
        out_shape=[jax.ShapeDtypeStruct((n, SEQ, D), F32), jax.ShapeDtypeStruct((n * SEQ, D), BF16),
                   jax.ShapeDtypeStruct((1, HEAD_V), F32)],
        compiler_params=_params(("arbitrary", "arbitrary")),
    )(o_f, o_b, r, dog, gnorm)


def merge_fwd(p5, y_conv, y_gla):
    t = y_conv.shape[0]

    def body(mc_ref, mg_ref, yc_ref, yg_ref, o_ref):
        f = lambda ref: ref[...].astype(F32)
        o_ref[...] = (_sigmoid(f(mc_ref)) * f(yc_ref) + _sigmoid(f(mg_ref)) * f(yg_ref)).astype(o_ref.dtype)

    row = pl.BlockSpec((TM_EW, D), lambda i: (i, 0))
    return _pallas(
        body, name="merge_fwd", grid=(t // TM_EW,),
        in_specs=[row, pl.BlockSpec((TM_EW, D), lambda i: (i, 1)), row, row], out_specs=row,
        out_shape=jax.ShapeDtypeStruct((t, D), BF16), compiler_params=_params(("parallel",)),
    )(p5, p5, y_conv, y_gla)


def merge_bwd(p5, y_conv, y_gla, dmerged):
    t = y_conv.shape[0]

    def body(mc_ref, mg_ref, yc_ref, yg_ref, d_ref, dyc_ref, dyg_ref, dp_ref):
        f = lambda ref: ref[...].astype(F32)
        d = f(d_ref)
        sc = _sigmoid(f(mc_ref))
        sg = _sigmoid(f(mg_ref))
        dyc_ref[...] = (d * sc).astype(dyc_ref.dtype)
        dyg_ref[...] = (d * sg).astype(dyg_ref.dtype)
        dp_ref[:, pl.ds(0, D)] = (d * f(yc_ref) * sc * (1.0 - sc)).astype(dp_ref.dtype)
        dp_ref[:, pl.ds(D, D)] = (d * f(yg_ref) * sg * (1.0 - sg)).astype(dp_ref.dtype)

    row = pl.BlockSpec((TM_EW, D), lambda i: (i, 0))
    return _pallas(
        body, name="merge_bwd", grid=(t // TM_EW,),
        in_specs=[row, pl.BlockSpec((TM_EW, D), lambda i: (i, 1)), row, row, row],
        out_specs=[row, row, pl.BlockSpec((TM_EW, 2 * D), lambda i: (i, 0))],
        out_shape=[jax.ShapeDtypeStruct((t, D), BF16), jax.ShapeDtypeStruct((t, D), BF16),
                   jax.ShapeDtypeStruct((t, 2 * D), BF16)],
        compiler_params=_params(("parallel",)),
    )(p5, p5, y_conv, y_gla, dmerged)


def final_fwd_bwd(x2, mo, gate, final_g, target, n_samples):
    t = x2.shape[0]
    tiles = SEQ // TM_EW

    def body(x_ref, mo_ref, gate_ref, g_ref, t_ref, dh_ref, dmo_ref, dgate_ref, dg_ref, loss_ref):
        b, j = pl.program_id(0), pl.program_id(1)
        mo_v = mo_ref[...]
        h = x_ref[...] + gate_ref[0] * mo_v
        rs = lax.rsqrt(jnp.mean(h * h, axis=-1, keepdims=True) + EPS)
        nh = h * rs
        err = nh * g_ref[...] - t_ref[...]
        dy = err * (1.0 / D)
        dn = dy * g_ref[...]
        dh = rs * (dn - nh * jnp.mean(dn * nh, axis=-1, keepdims=True))
        dh_ref[...] = dh
        dmo_ref[...] = (dh * gate_ref[0]).astype(dmo_ref.dtype)

        @pl.when(j == 0)
        def _():
            dgate_ref[...] = jnp.zeros_like(dgate_ref)

        @pl.when((b == 0) & (j == 0))
        def _():
            dg_ref[...] = jnp.zeros_like(dg_ref)
            loss_ref[...] = jnp.zeros_like(loss_ref)

        dgate_ref[0] += jnp.sum(dh * mo_v, axis=0, keepdims=True)
        dg_ref[...] += jnp.sum(dy * nh, axis=0, keepdims=True)
        loss_ref[...] += (0.5 / D) * jnp.sum(err * err)

    row = pl.BlockSpec((TM_EW, D), lambda b, j: (b * tiles + j, 0))
    per = pl.BlockSpec((1, 1, D), lambda b, j: (b, 0, 0))
    vec = pl.BlockSpec((1, D), lambda b, j: (0, 0))
    return _pallas(
        body, name="final_fwd_bwd", grid=(n_samples, tiles),
        in_specs=[row, row, per, vec, row],
        out_specs=[row, row, per, vec, pl.BlockSpec((8, 128), lambda b, j: (0, 0))],
        out_shape=[jax.ShapeDtypeStruct((t, D), F32), jax.ShapeDtypeStruct((t, D), BF16),
                   jax.ShapeDtypeStruct((n_samples, 1, D), F32), jax.ShapeDtypeStruct((1, D), F32),
                   jax.ShapeDtypeStruct((8, 128), F32)],
        compiler_params=_params(("arbitrary", "arbitrary")),
    )(x2, mo, gate, final_g, target)


def local_step(x, ctx, target, mod, wts, small):
    n = x.shape[0]
    t = n * SEQ
    t_all = t + n * NCTX
    x2 = x.reshape(t, D)
    ctx2 = ctx.reshape(n * NCTX, D)
    tgt2 = target.reshape(t, D)
    scale1, shift, gate = mod

    u = norm_mod_fwd(x2, ctx2, scale1, shift, small["norm_g"])
    p1 = matmul_nn(u, wts["w1"], small["b1"], name="proj_conv", m=t, tm=512, tn=1024, out_dtype=BF16)
    p2 = matmul_nn(u, wts["w2"], small["b2"], name="proj_z", m=t, tm=512, tn=1024, out_dtype=BF16)
    p3 = matmul_nn(u, wts["w3"], small["b3"], name="proj_gla", m=t_all, tm=512, tn=W3, out_dtype=BF16)
    p4 = matmul_nn(u, wts["w4"], small["b4"], name="proj_r", m=t, tm=512, tn=1024, out_dtype=BF16)
    p5 = matmul_nn(u, wts["w5"], small["b5"], name="proj_merge", m=t, tm=512, tn=1024, out_dtype=BF16)

    aconv = conv_fwd(p1, small["conv_w"], small["conv_b"], n)
    ac = ln_gate_fwd(aconv, p2, small["conv_ln_g"], small["conv_ln_b"])
    y_conv = matmul_nn(ac, wts["conv_proj"], None, name="conv_proj_fwd", m=t, tm=512, tn=1024, out_dtype=BF16)

    qs, ks, vs, cum_f, cum_b = gla_prep_fwd(p3, small["upf"], small["upb"], small["bias_f"], small["bias_b"], n)
    o_f, s_f, sfin_f = gla_scan_fwd(qs, ks, vs, cum_f, rev=False, name="gla_scan_fwd_f")
    o_b, s_b, sfin_b = gla_scan_fwd(qs, ks, vs, cum_b, rev=True, name="gla_scan_fwd_b")
    og = gla_out_fwd(o_f, o_b, p4, small["gla_norm_g"])
    y_gla = matmul_nn(og, wts["gla_proj"], None, name="gla_proj_fwd", m=t, tm=512, tn=1024, out_dtype=BF16)

    merged = merge_fwd(p5, y_conv, y_gla)
    mo = matmul_nn(merged, wts["w_out"], None, name="w_out_fwd", m=t, tm=512, tn=1024, out_dtype=F32)
    dh, dmo, dgate, d_final_g, loss = final_fwd_bwd(x2, mo, gate, small["final_norm_g"], tgt2, n)

    g = {"final_norm_g": d_final_g}
    dmerged = matmul_nt(dmo, wts["w_out"], name="w_out_dgrad", tm=512, out_dtype=BF16)
    g["w_out"] = matmul_tn(merged, dmo, name="w_out_wgrad", t=t, tn=1024, tt=512)[0]
    dyc, dyg, dp5 = merge_bwd(p5, y_conv, y_gla, dmerged)

    dac = matmul_nt(dyc, wts["conv_proj"], name="conv_proj_dgrad", tm=512, out_dtype=BF16)
    g["conv_proj"] = matmul_tn(ac, dyc, name="conv_proj_wgrad", t=t, tn=1024, tt=512)[0]
    daconv, dp2, g["conv_ln_g"], g["conv_ln_b"] = ln_gate_bwd(aconv, p2, dac, small["conv_ln_g"], small["conv_ln_b"])
    dp1, dconv_w, dconv_b = conv_bwd(p1, daconv, small["conv_w"], n)
    g["conv_w"], g["conv_b"] = dconv_w, dconv_b

    dog = matmul_nt(dyg, wts["gla_proj"], name="gla_proj_dgrad", tm=512, out_dtype=BF16)
    g["gla_proj"] = matmul_tn(og, dyg, name="gla_proj_wgrad", t=t, tn=1024, tt=512)[0]
    do, dp4, g["gla_norm_g"] = gla_out_bwd(o_f, o_b, p4, dog, small["gla_norm_g"])
    dq_f, dk_f, dv_f, dc_f = gla_scan_bwd(qs, ks, vs, cum_f, s_f, sfin_f, do, rev=False, name="gla_scan_bwd_f")
    dq_b, dk_b, dv_b, dc_b = gla_scan_bwd(qs, ks, vs, cum_b, s_b, sfin_b, do, rev=True, name="gla_scan_bwd_b")
    dp3, g["upf"], g["upb"], g["bias_f"], g["bias_b"] = gla_prep_bwd(
        p3, dq_f, dq_b, dk_f, dk_b, dv_f, dv_b, dc_f, dc_b,
        small["upf"], small["upb"], small["bias_f"], small["bias_b"], n)

    dps = [dp1, dp2, dp3, dp4, dp5]
    du = dgrad_multi(dps, [wts["w%d" % (i + 1)] for i in range(5)], t_all=t_all, t_lat=t, tm=512)
    for i, dp in enumerate(dps):
        rows = dp.shape[0]
        tn = W3 if dp.shape[1] == W3 else 1024
        g["w%d" % (i + 1)], g["b%d" % (i + 1)] = matmul_tn(
            u, dp, name="w_in_wgrad_%d" % (i + 1), t=rows, tn=tn, tt=512, colsum=True)
    grad_x, dshift, dscale, g["norm_g"] = norm_mod_bwd(x2, ctx2, du, dh, scale1, small["norm_g"])
    g["shift"], g["scale"], g["gate"] = dshift, dscale, dgate
    return loss, grad_x, g


def _group_cols(w):
    gv, gg, z = w[..., 0:1024], w[..., 1024:2048], w[..., 2048:3072]
    q, k, v = w[..., 3072:3584], w[..., 3584:4096], w[..., 4096:5120]
    ab = w[..., 5120:5152]
    r, mc, mg = w[..., 5152:6176], w[..., 6176:7200], w[..., 7200:8224]
    g1 = jnp.concatenate([p for j in range(CONV_NCB)
                          for p in (gv[..., CONV_CB * j:CONV_CB * (j + 1)], gg[..., CONV_CB * j:CONV_CB * (j + 1)])], -1)
    pad = jnp.zeros(w.shape[:-1] + (W3 - 2080,), w.dtype)
    g3 = jnp.concatenate([v, q, k, ab, pad], -1)
    return g1, z, g3, r, jnp.concatenate([mc, mg], -1)


def _ungroup_cols(g1, g2, g3, g4, g5):
    gv = jnp.concatenate([g1[..., 2 * CONV_CB * j:2 * CONV_CB * j + CONV_CB] for j in range(CONV_NCB)], -1)
    gg = jnp.concatenate([g1[..., 2 * CONV_CB * j + CONV_CB:2 * CONV_CB * (j + 1)] for j in range(CONV_NCB)], -1)
    v, q, k, ab = g3[..., 0:1024], g3[..., 1024:1536], g3[..., 1536:2048], g3[..., 2048:2080]
    return jnp.concatenate([gv, gg, g2, q, k, v, ab, g4, g5[..., 0:1024], g5[..., 1024:2048]], -1)


def _pad_up(up, row0):
    return jnp.zeros((128, GLA_DK), F32).at[row0:row0 + up.shape[0]].set(up)


def _adamw_math(w, g, m, v):
    m = ADAM_B1 * m + (1.0 - ADAM_B1) * g
    v = ADAM_B2 * v + (1.0 - ADAM_B2) * (g * g)
    m_hat = m / (1.0 - ADAM_B1 ** ADAM_STEP)
    v_hat = v / (1.0 - ADAM_B2 ** ADAM_STEP)
    delta = -ADAM_LR * (m_hat / (jnp.sqrt(v_hat) + ADAM_EPS) + ADAM_WD * w)
    return delta, m, v


def adamw2d(w, g, m, v, *, name, tr):
    rows, cols = w.shape

    def body(w_ref, g_ref, m_ref, v_ref, d_ref, nm_ref, nv_ref):
        d_ref[...], nm_ref[...], nv_ref[...] = _adamw_math(w_ref[...], g_ref[...], m_ref[...], v_ref[...])

    spec = pl.BlockSpec((tr, cols), lambda i: (i, 0))
    return _pallas(
        body, name=name, grid=(rows // tr,), in_specs=[spec] * 4, out_specs=[spec] * 3,
        out_shape=[jax.ShapeDtypeStruct((rows, cols), F32)] * 3, compiler_params=_params(("parallel",)),
    )(w, g, m, v)


def sum_devices(sall):
    rows = sall.shape[1]

    def body(s_ref, o_ref):
        acc = s_ref[0]
        for d in range(1, N_DEV):
            acc = acc + s_ref[d]
        o_ref[...] = acc

    return _pallas(body, name="sum_devices", out_shape=jax.ShapeDtypeStruct((rows, D), F32),
                   compiler_params=_params())(sall)


def pair_add(core, g, got, *, name, tr):
    n, rows, cols = got.shape
    g4 = g.reshape(n, 2, rows, cols)

    def body(core_ref, g_ref, got_ref, o_ref, ob_ref):
        del core_ref
        s = g_ref[0, 0] + got_ref[0]
        o_ref[0] = s
        ob_ref[0] = s.astype(BF16)

    spec = pl.BlockSpec((1, tr, cols), lambda i, t, core_ref: (i, t, 0))
    return _pallas(
        body, name=name,
        grid_spec=pltpu.PrefetchScalarGridSpec(
            num_scalar_prefetch=1, grid=(n, rows // tr),
            in_specs=[pl.BlockSpec((1, 1, tr, cols), lambda i, t, core_ref: (i, core_ref[0], t, 0)), spec],
            out_specs=[spec, spec]),
        out_shape=[jax.ShapeDtypeStruct(got.shape, F32), jax.ShapeDtypeStruct(got.shape, BF16)],
        compiler_params=_params(("parallel", "parallel")))(core, g4, got)


def chip_add(place, pa, rb, *, name, tr):
    _, rows, cols = pa.shape

    def body(place_ref, m_ref, r_ref, o_ref):
        del place_ref
        o_ref[0] = ((m_ref[0] + r_ref[0].astype(F32)) + r_ref[1].astype(F32)) + r_ref[2].astype(F32)

    return _pallas(
        body, name=name,
        grid_spec=pltpu.PrefetchScalarGridSpec(
            num_scalar_prefetch=1, grid=(rows // tr,),
            in_specs=[pl.BlockSpec((1, tr, cols), lambda t, place_ref: (place_ref[0], t, 0)),
                      pl.BlockSpec((3, tr, cols), lambda t, place_ref: (0, t, 0))],
            out_specs=pl.BlockSpec((1, tr, cols), lambda t, place_ref: (place_ref[1], t, 0))),
        out_shape=jax.ShapeDtypeStruct((2, rows, cols), F32),
        compiler_params=_params(("parallel",)))(place, pa, rb)


def ada_bwd(call, cctx_rows, dm_shard, dm_full, adaw):
    nsh = adaw.shape[1]

    def body(c_ref, cc_ref, dms_ref, dmf_ref, w_ref, gw_ref, gb_ref, pq_ref):
        a_lat = _silu(c_ref[...])
        a_ctx = _silu(cc_ref[...])
        dms = dms_ref[...]
        gw_ref[...] = _tn(a_lat, dms[0:64], HI) + _tn(a_ctx, dms[64:72], HI)
        gb_ref[...] = jnp.sum(dmf_ref[...], axis=0, keepdims=True)
        part = _nt(dms[64:72], w_ref[...], HI)
        pq_ref[...] = jnp.zeros_like(pq_ref) + jnp.sum(part, axis=0, keepdims=True)

    return _pallas(body, name="ada_bwd",
                   out_shape=[jax.ShapeDtypeStruct((D, nsh), F32), jax.ShapeDtypeStruct((1, 3 * D), F32),
                              jax.ShapeDtypeStruct((8, D), F32)],
                   compiler_params=_params())(call, cctx_rows, dm_shard, dm_full, adaw)


def cctx_grad(pq_all, cctx_rows):
    def body(p_ref, c_ref, o_ref):
        acc = p_ref[0]
        for qi in range(1, N_CHIPS):
            acc = acc + p_ref[qi]
        o_ref[...] = acc * _dsilu(c_ref[...])

    return _pallas(body, name="cctx_grad", out_shape=jax.ShapeDtypeStruct((8, D), F32),
                   compiler_params=_params())(pq_all, cctx_rows)


def _place():
    x, y, c = lax.axis_index("x"), lax.axis_index("y"), lax.axis_index("c")
    chips = [(1 - x, y), (x, 1 - y), (1 - x, 1 - y)]
    return x, y, c, chips


def _all_peers(x, y, c):
    return [((1 - x) if r & 4 else x, (1 - y) if r & 2 else y, (1 - c) if r & 1 else c) for r in range(1, N_DEV)]


def _remote(src, dst, send_sem, recv_sem, dev):
    return pltpu.make_async_remote_copy(src_ref=src, dst_ref=dst, send_sem=send_sem, recv_sem=recv_sem,
                                        device_id=dev, device_id_type=MESH)


ANY = pl.BlockSpec(memory_space=pl.ANY)
VMEM = pl.BlockSpec(memory_space=pltpu.VMEM)
F_ROWS = 16


W_ROW_CHUNKS = 4
P_ROW_CHUNKS = 2
N_BULK = W_ROW_CHUNKS + P_ROW_CHUNKS


def _half_chunks(core, n_rows, align):
    out = []
    for a, k in ((0, W_ROW_CHUNKS), (1, P_ROW_CHUNKS)):
        half = n_rows[a] // 2
        size = half // k
        for i in range(k):
            start = core * half + i * size
            out.append((a, pl.ds(start if isinstance(start, int) else pl.multiple_of(start, align), size)))
    return out


def gather_weights(c8, cctx8, adaw, adab, w_sh, p_sh, fp):
    nsh = adaw.shape[1]

    def body(c_ref, cctx_ref, adaw_ref, adab_ref, w_ref, p_ref, fp_ref, wall_ref, pall_ref, fall_ref, call_ref, mall_ref,
             abuf, w_send, w_recv, h_send, h_recv, c_send, c_recv, m_send, m_recv, f_send, f_recv):
        x, y, c, chips = _place()
        q = 2 * x + y
        dev = 4 * x + 2 * y + c
        qs = [2 * cx + cy for cx, cy in chips]
        sib = (x, y, 1 - c)
        srcs, dsts = (w_ref, p_ref), (wall_ref, pall_ref)
        n_rows = (w_ref.shape[0], p_ref.shape[0])
        mine = _half_chunks(c, n_rows, 16)
        other = _half_chunks(1 - c, n_rows, 16)

        bulk = [[_remote(srcs[a].at[rows], dsts[a].at[q, rows], w_send.at[j * N_BULK + i], w_recv.at[j * N_BULK + i],
                         (*chips[j], c)) for i, (a, rows) in enumerate(mine)] for j in range(3)]
        fall_ref[q] = fp_ref[...]
        small = [_remote(fp_ref, fall_ref.at[q], f_send.at[j], f_recv.at[j], (*chips[j], c)) for j in range(3)]
        my_rows = pl.ds(pl.multiple_of(8 * dev, 8), 8)
        call_ref[my_rows, :] = c_ref[...]
        cond = [_remote(c_ref, call_ref.at[my_rows, :], c_send.at[r], c_recv.at[r], peer)
                for r, peer in enumerate(_all_peers(x, y, c))]
        for cp in sum(bulk, []) + small + cond:
            cp.start()
        for cp in cond:
            cp.wait_recv()

        abuf[pl.ds(0, 64), :] = _silu(call_ref[...])
        abuf[pl.ds(64, 8), :] = _silu(cctx_ref[...])
        mall_ref[q] = _nn(abuf[...], adaw_ref[...], HI) + adab_ref[...]
        mod = [_remote(mall_ref.at[q], mall_ref.at[q], m_send.at[j], m_recv.at[j], (*chips[j], c)) for j in range(3)]
        for cp in mod:
            cp.start()

        handed = []
        for j in range(3):
            for i, (a, rows) in enumerate(mine):
                bulk[j][i].wait_recv()
                cp = _remote(dsts[a].at[qs[j], rows], dsts[a].at[qs[j], rows],
                             h_send.at[j * N_BULK + i], h_recv.at[j * N_BULK + i], sib)
                cp.start()
                handed.append(cp)
        for j in range(3):
            for i, (a, rows) in enumerate(other):
                _remote(dsts[a].at[qs[j], rows], dsts[a].at[qs[j], rows],
                        h_send.at[j * N_BULK + i], h_recv.at[j * N_BULK + i], sib).wait_recv()
        for cp in mod + small:
            cp.wait_recv()
        for cp in sum(bulk, []) + small + cond + mod + handed:
            cp.wait_send()

    def dma(n):
        return pltpu.SemaphoreType.DMA((n,))

    return _pallas(
        body, name="gather_weights",
        in_specs=[VMEM, VMEM, VMEM, VMEM, ANY, ANY, VMEM],
        out_specs=[ANY, ANY, VMEM, VMEM, VMEM],
        out_shape=[jax.ShapeDtypeStruct((N_CHIPS,) + w_sh.shape, BF16), jax.ShapeDtypeStruct((N_CHIPS,) + p_sh.shape, BF16),
                   jax.ShapeDtypeStruct((N_CHIPS, F_ROWS, D), F32),
                   jax.ShapeDtypeStruct((8 * N_DEV, D), F32), jax.ShapeDtypeStruct((N_CHIPS, MOD_ROWS, nsh), F32)],
        scratch_shapes=[pltpu.VMEM((MOD_ROWS, D), F32), dma(3 * N_BULK), dma(3 * N_BULK), dma(3 * N_BULK), dma(3 * N_BULK),
                        dma(7), dma(7), dma(3), dma(3), dma(3), dma(3)],
        compiler_params=_params(),
    )(c8, cctx8, adaw, adab, w_sh, p_sh, fp)


def gather_small_and_pair(sm, gw, gp):
    rows = sm.shape[0]
    n_pair = N_CHIPS * N_BULK

    def body(sm_ref, gw_ref, gp_ref, sall_ref, gotw_ref, gotp_ref, s_send, s_recv, a_send, a_recv):
        x, y, c, _ = _place()
        dev = 4 * x + 2 * y + c
        srcs, dsts = (gw_ref, gp_ref), (gotw_ref, gotp_ref)
        n_rows = (gw_ref.shape[1], gp_ref.shape[1])
        pair = []
        for i, ((a, rows_o), (_, rows_0)) in enumerate(zip(_half_chunks(1 - c, n_rows, 8), _half_chunks(0, n_rows, 8))):
            for s in range(N_CHIPS):
                k = s * N_BULK + i
                pair.append(_remote(srcs[a].at[s, rows_o], dsts[a].at[s, rows_0], a_send.at[k], a_recv.at[k], (x, y, 1 - c)))
        for cp in pair:
            cp.start()
        sall_ref[dev] = sm_ref[...]
        small = [_remote(sm_ref, sall_ref.at[dev], s_send.at[r], s_recv.at[r], peer)
                 for r, peer in enumerate(_all_peers(x, y, c))]
        for cp in small:
            cp.start()
        for cp in small + pair:
            cp.wait_recv()
        for cp in small + pair:
            cp.wait_send()

    return _pallas(
        body, name="gather_small_and_pair", in_specs=[VMEM, ANY, ANY], out_specs=[VMEM, ANY, ANY],
        out_shape=[jax.ShapeDtypeStruct((N_DEV, rows, D), F32),
                   jax.ShapeDtypeStruct((N_CHIPS, gw.shape[1] // 2, gw.shape[2]), F32),
                   jax.ShapeDtypeStruct((N_CHIPS, gp.shape[1] // 2, gp.shape[2]), F32)],
        scratch_shapes=[pltpu.SemaphoreType.DMA((7,)), pltpu.SemaphoreType.DMA((7,)),
                        pltpu.SemaphoreType.DMA((n_pair,)), pltpu.SemaphoreType.DMA((n_pair,))],
        compiler_params=_params(),
    )(sm, gw, gp)


def chip_exchange(paw, pap, pq):
    def body(paw_ref, pap_ref, pq_ref, rbw_ref, rbp_ref, pqa_ref, b_send, b_recv, p_send, p_recv):
        x, y, c, chips = _place()
        q = 2 * x + y
        qs = [2 * cx + cy for cx, cy in chips]
        srcs, dsts = (paw_ref, pap_ref), (rbw_ref, rbp_ref)
        n_rows = (2 * paw_ref.shape[1], 2 * pap_ref.shape[1])
        big = [_remote(srcs[a].at[qs[j], rows], dsts[a].at[j, rows], b_send.at[j * N_BULK + i], b_recv.at[j * N_BULK + i],
                       (*chips[j], c))
               for j in range(3) for i, (a, rows) in enumerate(_half_chunks(0, n_rows, 16))]
        pqa_ref[q] = pq_ref[...]
        small = [_remote(pq_ref, pqa_ref.at[q], p_send.at[j], p_recv.at[j], (*chips[j], c)) for j in range(3)]
        for cp in big + small:
            cp.start()
        for cp in small + big:
            cp.wait_recv()
        for cp in small + big:
            cp.wait_send()

    return _pallas(
        body, name="chip_exchange", in_specs=[ANY, ANY, VMEM], out_specs=[ANY, ANY, VMEM],
        out_shape=[jax.ShapeDtypeStruct((3,) + paw.shape[1:], BF16), jax.ShapeDtypeStruct((3,) + pap.shape[1:], BF16),
                   jax.ShapeDtypeStruct((N_CHIPS, 8, D), F32)],
        scratch_shapes=[pltpu.SemaphoreType.DMA((3 * N_BULK,)), pltpu.SemaphoreType.DMA((3 * N_BULK,)),
                        pltpu.SemaphoreType.DMA((3,)), pltpu.SemaphoreType.DMA((3,))],
        compiler_params=_params(),
    )(paw, pap, pq)


def pair_share(ghw, ghp):
    def body(ghw_ref, ghp_ref, outw_ref, outp_ref, send, recv):
        del ghw_ref, ghp_ref
        x, y, c, _ = _place()
        refs = (outw_ref, outp_ref)
        n_rows = (2 * outw_ref.shape[1], 2 * outp_ref.shape[1])
        pair = [_remote(refs[a].at[c, rows], refs[a].at[c, rows], send.at[i], recv.at[i], (x, y, 1 - c))
                for i, (a, rows) in enumerate(_half_chunks(0, n_rows, 8))]
        for cp in pair:
            cp.start()
        for i, (a, rows) in enumerate(_half_chunks(0, n_rows, 8)):
            _remote(refs[a].at[1 - c, rows], refs[a].at[1 - c, rows], send.at[i], recv.at[i], (x, y, 1 - c)).wait_recv()
        for cp in pair:
            cp.wait_send()

    return _pallas(
        body, name="pair_share", in_specs=[ANY, ANY], out_specs=[ANY, ANY],
        out_shape=[jax.ShapeDtypeStruct(ghw.shape, F32), jax.ShapeDtypeStruct(ghp.shape, F32)],
        scratch_shapes=[pltpu.SemaphoreType.DMA((N_BULK,)), pltpu.SemaphoreType.DMA((N_BULK,))],
        input_output_aliases={0: 0, 1: 1},
        compiler_params=_params(),
    )(ghw, ghp)


def _rows_of(shape):
    size = 1
    for s in shape:
        size *= s
    return -(-size // D)


def _pack(arrs, rows_multiple=8):
    parts = []
    total = 0
    for a in arrs:
        f = a.reshape(-1).astype(F32)
        r = _rows_of(a.shape)
        parts.append(jnp.pad(f, (0, r * D - f.shape[0])))
        total += r
    pad_rows = (-total) % rows_multiple
    if pad_rows:
        parts.append(jnp.zeros((pad_rows * D,), F32))
    return jnp.concatenate(parts).reshape(-1, D)


def _unpack(p, shapes):
    out = []
    r0 = 0
    for shp in shapes:
        r = _rows_of(shp)
        size = 1
        for s in shp:
            size *= s
        out.append(p[r0:r0 + r].reshape(-1)[:size].reshape(shp))
        r0 += r
    return out


WEIGHT_NAMES = ['c_ctx', 'ada_w', 'ada_b', 'norm_g', 'w_in', 'b_in', 'conv_w', 'conv_b', 'conv_ln_g', 'conv_ln_b',
                'conv_proj', 'decay_up_fwd', 'decay_bias_fwd', 'decay_up_bwd', 'decay_bias_bwd', 'gla_norm_g', 'gla_proj',
                'w_out', 'final_norm_g']
SMALL_NAMES = ['c_ctx', 'ada_b', 'norm_g', 'b_in', 'conv_w', 'conv_b', 'conv_ln_g', 'conv_ln_b', 'decay_up_fwd',
               'decay_bias_fwd', 'decay_up_bwd', 'decay_bias_bwd', 'gla_norm_g', 'final_norm_g']


def kernel(x, c, ctx, c_ctx, ada_w, ada_b, norm_g, w_in, b_in, conv_w, conv_b, conv_ln_g, conv_ln_b, conv_proj, decay_up_fwd, decay_bias_fwd, decay_up_bwd, decay_bias_bwd, gla_norm_g, gla_proj, w_out, final_norm_g, loss_target, m_c_ctx, m_ada_w, m_ada_b, m_norm_g, m_w_in, m_b_in, m_conv_w, m_conv_b, m_conv_ln_g, m_conv_ln_b, m_conv_proj, m_decay_up_fwd, m_decay_bias_fwd, m_decay_up_bwd, m_decay_bias_bwd, m_gla_norm_g, m_gla_proj, m_w_out, m_final_norm_g, v_c_ctx, v_ada_w, v_ada_b, v_norm_g, v_w_in, v_b_in, v_conv_w, v_conv_b, v_conv_ln_g, v_conv_ln_b, v_conv_proj, v_decay_up_fwd, v_decay_bias_fwd, v_decay_up_bwd, v_decay_bias_bwd, v_gla_norm_g, v_gla_proj, v_w_out, v_final_norm_g):
    w = dict(c_ctx=c_ctx, ada_w=ada_w, ada_b=ada_b, norm_g=norm_g, w_in=w_in, b_in=b_in, conv_w=conv_w, conv_b=conv_b,
             conv_ln_g=conv_ln_g, conv_ln_b=conv_ln_b, conv_proj=conv_proj, decay_up_fwd=decay_up_fwd,
             decay_bias_fwd=decay_bias_fwd, decay_up_bwd=decay_up_bwd, decay_bias_bwd=decay_bias_bwd,
             gla_norm_g=gla_norm_g, gla_proj=gla_proj, w_out=w_out, final_norm_g=final_norm_g)
    m = dict(c_ctx=m_c_ctx, ada_w=m_ada_w, ada_b=m_ada_b, norm_g=m_norm_g, w_in=m_w_in, b_in=m_b_in, conv_w=m_conv_w,
             conv_b=m_conv_b, conv_ln_g=m_conv_ln_g, conv_ln_b=m_conv_ln_b, conv_proj=m_conv_proj,
             decay_up_fwd=m_decay_up_fwd, decay_bias_fwd=m_decay_bias_fwd, decay_up_bwd=m_decay_up_bwd,
             decay_bias_bwd=m_decay_bias_bwd, gla_norm_g=m_gla_norm_g, gla_proj=m_gla_proj, w_out=m_w_out,
             final_norm_g=m_final_norm_g)
    v = dict(c_ctx=v_c_ctx, ada_w=v_ada_w, ada_b=v_ada_b, norm_g=v_norm_g, w_in=v_w_in, b_in=v_b_in, conv_w=v_conv_w,
             conv_b=v_conv_b, conv_ln_g=v_conv_ln_g, conv_ln_b=v_conv_ln_b, conv_proj=v_conv_proj,
             decay_up_fwd=v_decay_up_fwd, decay_bias_fwd=v_decay_bias_fwd, decay_up_bwd=v_decay_up_bwd,
             decay_bias_bwd=v_decay_bias_bwd, gla_norm_g=v_gla_norm_g, gla_proj=v_gla_proj, w_out=v_w_out,
             final_norm_g=v_final_norm_g)
    n = x.shape[0]
    ax, ay, ac = lax.axis_index("x"), lax.axis_index("y"), lax.axis_index("c")
    q = 2 * ax + ay
    dev = 4 * ax + 2 * ay + ac
    nsh = ada_w.shape[2]

    w_sh = w_in[0].astype(BF16)
    p_sh = jnp.concatenate([conv_proj[0], gla_proj[0], w_out[0]], 0).astype(BF16)
    fp = _pack([conv_w[0], decay_up_fwd[0], decay_up_bwd[0]], F_ROWS)
    c8 = jnp.pad(c, ((0, 8 - n), (0, 0)))
    cctx8 = jnp.pad(c_ctx[None], ((0, 7), (0, 0)))
    adab_sh = lax.dynamic_slice(ada_b, (0, q * nsh), (1, nsh))
    w_all, p_all, fall, call, mall = gather_weights(c8, cctx8, ada_w[0], adab_sh, w_sh, p_sh, fp)

    mod_all = jnp.transpose(mall, (1, 0, 2)).reshape(MOD_ROWS, 3 * D)
    mod_mine = lax.dynamic_slice(mod_all, (8 * dev, 0), (n, 3 * D))
    mod_ctx = mod_all[64:65]
    shift = jnp.concatenate([mod_mine[:, 0:D], mod_ctx[:, 0:D]], 0)[:, None, :]
    scale1 = 1.0 + jnp.concatenate([mod_mine[:, D:2 * D], mod_ctx[:, D:2 * D]], 0)[:, None, :]
    gate = mod_mine[:, 2 * D:3 * D][:, None, :]

    own = lambda i, mine, got: jnp.where(q == i, mine, got)
    g1, g2, g3, g4, g5 = _group_cols(jnp.concatenate([own(i, w_sh, w_all[i]) for i in range(N_CHIPS)], 1))
    p_full = jnp.stack([own(i, p_sh, p_all[i]) for i in range(N_CHIPS)])
    wts = dict(w1=g1, w2=g2, w3=g3, w4=g4, w5=g5,
               conv_proj=p_full[:, 0:256].reshape(D, D), gla_proj=p_full[:, 256:512].reshape(D, D),
               w_out=p_full[:, 512:768].reshape(D, D))
    f_parts = [_unpack(fall[i], [conv_w.shape[1:], decay_up_fwd.shape[1:], decay_up_bwd.shape[1:]]) for i in range(N_CHIPS)]
    conv_w_full = jnp.concatenate([p[0] for p in f_parts], 1)
    upf_full = jnp.concatenate([p[1] for p in f_parts], 1)
    upb_full = jnp.concatenate([p[2] for p in f_parts], 1)
    b1, b2, b3, b4, b5 = _group_cols(b_in)
    small = dict(b1=b1, b2=b2, b3=b3, b4=b4, b5=b5, norm_g=norm_g,
                 conv_w=jnp.pad(conv_w_full, ((0, 1), (0, 0))), conv_b=conv_b, conv_ln_g=conv_ln_g, conv_ln_b=conv_ln_b,
                 upf=_pad_up(upf_full, 0), upb=_pad_up(upb_full, 16), bias_f=decay_bias_fwd, bias_b=decay_bias_bwd,
                 gla_norm_g=gla_norm_g, final_norm_g=final_norm_g[None])

    loss_part, grad_x2, g = local_step(x, ctx, loss_target, (scale1, shift, gate), wts, small)
    loss = lax.psum(loss_part[0, 0], ("x", "y", "c"))

    dm_mine = jnp.concatenate([g["shift"][:n, 0], g["scale"][:n, 0], g["gate"][:, 0]], -1)
    dm_ctx = jnp.concatenate([g["shift"][n, 0], g["scale"][n, 0], jnp.zeros((D,), F32)], -1)
    d_b_in = _ungroup_cols(*[g["b%d" % i] for i in range(1, 6)])
    small_grads = [d_b_in, g["norm_g"], g["conv_b"].sum(0), g["conv_ln_g"], g["conv_ln_b"], g["bias_f"], g["bias_b"],
                   g["gla_norm_g"], g["final_norm_g"], g["conv_w"].sum(0)[:CONV_K], g["upf"][0:16], g["upb"][16:32],
                   dm_mine, dm_ctx]
    small_shapes = [a.shape for a in small_grads]
    sm = _pack(small_grads)
    d_w_in = _ungroup_cols(*[g["w%d" % i] for i in range(1, 6)])
    gw = jnp.stack([d_w_in[:, i * W_IN_SHARD:(i + 1) * W_IN_SHARD] for i in range(N_CHIPS)])
    gp = jnp.concatenate([g["conv_proj"].reshape(N_CHIPS, 256, D), g["gla_proj"].reshape(N_CHIPS, 256, D),
                          g["w_out"].reshape(N_CHIPS, 256, D)], 1)
    sall, gotw, gotp = gather_small_and_pair(sm, gw, gp)
    core = ac.astype(jnp.int32).reshape(1)
    chip = q.astype(jnp.int32).reshape(1)
    paw, paw16 = pair_add(core, gw, gotw, name="pair_add_w", tr=128)
    pap, pap16 = pair_add(core, gp, gotp, name="pair_add_p", tr=384)

    gsum = sum_devices(sall)
    (s_b_in, s_norm_g, s_conv_b, s_ln_g, s_ln_b, s_bias_f, s_bias_b, s_gla_g, s_final_g, s_conv_w, s_upf, s_upb,
     _, _) = _unpack(gsum, small_shapes)
    dm_rows = [_unpack(sall[i], small_shapes)[-2:] for i in range(N_DEV)]
    dm_full = jnp.concatenate(
        [jnp.pad(jnp.stack([r[0] for r in dm_rows]), ((0, 0), (0, 8 - n), (0, 0))).reshape(8 * N_DEV, 3 * D),
         jnp.stack([r[1] for r in dm_rows])], 0)
    dm_shard = lax.dynamic_slice(dm_full, (0, q * nsh), (MOD_ROWS, nsh))
    cctx_rows = jnp.broadcast_to(c_ctx[None], (8, D))
    g_ada_w, g_ada_b, pq = ada_bwd(call, cctx_rows, dm_shard, dm_full, ada_w[0])

    rbw, rbp, pq_all = chip_exchange(paw16, pap16, pq)
    place = jnp.concatenate([chip, core])
    ghw = chip_add(place, paw, rbw, name="chip_add_w", tr=128)
    ghp = chip_add(place, pap, rbp, name="chip_add_p", tr=384)
    gw_mine, gp_mine = pair_share(ghw, ghp)
    gp_mine = gp_mine.reshape(768, D)
    g_c_ctx = cctx_grad(pq_all, cctx_rows)[0]

    grads = dict(
        c_ctx=g_c_ctx, ada_w=g_ada_w[None], ada_b=g_ada_b, norm_g=s_norm_g,
        w_in=gw_mine.reshape(1, D, W_IN_SHARD), b_in=s_b_in,
        conv_w=lax.dynamic_slice(s_conv_w, (0, q * 256), (CONV_K, 256))[None], conv_b=s_conv_b,
        conv_ln_g=s_ln_g, conv_ln_b=s_ln_b, conv_proj=gp_mine[0:256][None],
        decay_up_fwd=lax.dynamic_slice(s_upf, (0, q * 128), (16, 128))[None], decay_bias_fwd=s_bias_f,
        decay_up_bwd=lax.dynamic_slice(s_upb, (0, q * 128), (16, 128))[None], decay_bias_bwd=s_bias_b,
        gla_norm_g=s_gla_g, gla_proj=gp_mine[256:512][None], w_out=gp_mine[512:768][None],
        final_norm_g=s_final_g[0])

    delta, new_m, new_v = {}, {}, {}
    for name, tr in [("ada_w", 128), ("w_in", 128), ("conv_proj", 128), ("gla_proj", 128), ("w_out", 128)]:
        shp = w[name].shape
        two = lambda a: a.reshape(shp[-2], shp[-1])
        d_, m_, v_ = adamw2d(two(w[name]), two(grads[name]), two(m[name]), two(v[name]), name="adamw_" + name, tr=tr)
        delta[name], new_m[name], new_v[name] = d_.reshape(shp), m_.reshape(shp), v_.reshape(shp)
    shapes = [w[nm].shape for nm in SMALL_NAMES]
    packs = [_pack([src[nm] for nm in SMALL_NAMES]) for src in (w, grads, m, v)]
    d_, m_, v_ = adamw2d(*packs, name="adamw_small", tr=packs[0].shape[0])
    for nm, a, b, cc in zip(SMALL_NAMES, _unpack(d_, shapes), _unpack(m_, shapes), _unpack(v_, shapes)):
        delta[nm], new_m[nm], new_v[nm] = a, b, cc

    grad_x = grad_x2.reshape(x.shape)
    return (loss, grad_x, *[grads[nm].reshape(w[nm].shape) for nm in WEIGHT_NAMES], *[delta[nm] for nm in WEIGHT_NAMES],
            *[new_m[nm] for nm in WEIGHT_NAMES], *[new_v[nm] for nm in WEIGHT_NAMES])
```

```python
import jax
import jax.numpy as jnp
from jax import lax
from jax.experimental import pallas as pl
from jax.experimental.pallas import tpu as pltpu

F32 = jnp.float32
BF16 = jnp.bfloat16
MESH = pl.DeviceIdType.MESH
HI = lax.Precision.HIGHEST

D = 1024
SEQ = 2048
GRID_W = 64
GRID_H = SEQ // GRID_W
NCTX = 256
SEQ_ALL = SEQ + NCTX
EPS = 1e-6
CONV_K = 31
CONV_PAD = CONV_K // 2
HEADS = 4
HEAD_K = 128
HEAD_V = 256
GLA_DK = HEADS * HEAD_K
GATE_TAU = 16.0
Q_SCALE = HEAD_K ** -0.5
CHUNK = 64
NCHUNK = SEQ_ALL // CHUNK
NCHUNK_LAT = SEQ // CHUNK
NCHUNK_CTX = NCHUNK - NCHUNK_LAT
SUB = 64
NSUB = CHUNK // SUB
N_IN = 8224
W3 = 2176
O3_V, O3_Q, O3_K, O3_AB = 0, 1024, 1536, 2048

ADAM_LR, ADAM_B1, ADAM_B2, ADAM_EPS, ADAM_WD, ADAM_STEP = 0.001, 0.9, 0.999, 1e-08, 0.01, 10
VMEM_LIMIT = 56 * 1024 * 1024

N_CHIPS = 4
N_DEV = 8
W_IN_SHARD = N_IN // N_CHIPS
MOD_ROWS = 72


def _pallas(body, **kw):
    return pl.pallas_call(body, **kw)


def _params(sem=None, **kw):
    if sem is not None:
        kw["dimension_semantics"] = sem
    return pltpu.CompilerParams(vmem_limit_bytes=VMEM_LIMIT, **kw)


def _sigmoid(v):
    return 1.0 / (1.0 + jnp.exp(-v))


def _silu(v):
    return v * _sigmoid(v)


def _dsilu(v):
    s = _sigmoid(v)
    return s * (1.0 + v * (1.0 - s))


def _log_sigmoid(v):
    return jnp.minimum(v, 0.0) - jnp.log(1.0 + jnp.exp(-jnp.abs(v)))


def _dot(a, b, dims, precision=None):
    return lax.dot_general(a, b, (dims, ((), ())), preferred_element_type=F32, precision=precision)


def _nn(a, b, precision=None):
    return _dot(a, b, ((1,), (0,)), precision)


def _nt(a, b, precision=None):
    return _dot(a, b, ((1,), (1,)), precision)


def _tn(a, b, precision=None):
    return _dot(a, b, ((0,), (0,)), precision)


def _b16(v):
    return v.astype(BF16)


def matmul_nn(a, b, bias, *, name, m, tm, tn, out_dtype):
    k = a.shape[1]
    n = b.shape[1]
    has_bias = bias is not None

    def body(*refs):
        if has_bias:
            a_ref, b_ref, bias_ref, o_ref = refs
            acc = _nn(a_ref[...], b_ref[...]) + bias_ref[...]
        else:
            a_ref, b_ref, o_ref = refs
            acc = _nn(a_ref[...], b_ref[...])
        o_ref[...] = acc.astype(o_ref.dtype)

    in_specs = [pl.BlockSpec((tm, k), lambda j, i: (i, 0)), pl.BlockSpec((k, tn), lambda j, i: (0, j))]
    args = [a, b]
    if has_bias:
        in_specs.append(pl.BlockSpec((1, tn), lambda j, i: (0, j)))
        args.append(bias)
    return _pallas(
        body, name=name, grid=(n // tn, m // tm), in_specs=in_specs,
        out_specs=pl.BlockSpec((tm, tn), lambda j, i: (i, j)),
        out_shape=jax.ShapeDtypeStruct((m, n), out_dtype),
        compiler_params=_params(("parallel", "parallel")),
    )(*args)


def matmul_nt(a, b, *, name, tm, out_dtype):
    m, k = a.shape
    n = b.shape[0]

    def body(a_ref, b_ref, o_ref):
        o_ref[...] = _nt(a_ref[...], b_ref[...]).astype(o_ref.dtype)

    return _pallas(
        body, name=name, grid=(m // tm,),
        in_specs=[pl.BlockSpec((tm, k), lambda i: (i, 0)), pl.BlockSpec((n, k), lambda i: (0, 0))],
        out_specs=pl.BlockSpec((tm, n), lambda i: (i, 0)),
        out_shape=jax.ShapeDtypeStruct((m, n), out_dtype),
        compiler_params=_params(("parallel",)),
    )(a, b)


def matmul_tn(a, b, *, name, t, tn, tt, colsum=False):
    m = a.shape[1]
    n = b.shape[1]

    def body(a_ref, b_ref, o_ref, *rest):
        @pl.when(pl.program_id(1) == 0)
        def _():
            o_ref[...] = jnp.zeros_like(o_ref)
            if colsum:
                rest[0][...] = jnp.zeros_like(rest[0])
        o_ref[...] += _tn(a_ref[...], b_ref[...])
        if colsum:
            rest[0][...] += jnp.sum(b_ref[...].astype(F32), axis=0, keepdims=True)

    out_specs = [pl.BlockSpec((m, tn), lambda j, s: (0, j))]
    out_shape = [jax.ShapeDtypeStruct((m, n), F32)]
    if colsum:
        out_specs.append(pl.BlockSpec((1, tn), lambda j, s: (0, j)))
        out_shape.append(jax.ShapeDtypeStruct((1, n), F32))
    return _pallas(
        body, name=name, grid=(n // tn, t // tt),
        in_specs=[pl.BlockSpec((tt, m), lambda j, s: (s, 0)), pl.BlockSpec((tt, tn), lambda j, s: (s, j))],
        out_specs=out_specs, out_shape=out_shape,
        compiler_params=_params(("parallel", "arbitrary")),
    )(a, b)


def dgrad_multi(dps, wts, paw, pap, *, t_all, t_lat, tm):
    steps = []
    tks = []
    for g, dp in enumerate(dps):
        width = dp.shape[1]
        tk = width if width % 1024 else 1024
        tks.append(tk)
        steps += [(g, s) for s in range(width // tk)]
    n_steps = len(steps)
    lo = [min(i for i, (g, _) in enumerate(steps) if g == gg) for gg in range(len(dps))]
    cnt = [sum(1 for (g, _) in steps if g == gg) for gg in range(len(dps))]
    n_lat = t_lat // tm
    n_g = len(dps)

    n_tiles = t_all // tm

    def body(*refs):
        dp_refs, w_refs = refs[:n_g], refs[n_g:2 * n_g]
        paw_ref, pap_ref, o_ref, rbw_ref, rbp_ref, b_send, b_recv = refs[2 * n_g:]
        i = pl.program_id(0)
        s = pl.program_id(1)

        def exchange():
            x, y, c, chips = _place()
            srcs, dsts = (paw_ref, pap_ref), (rbw_ref, rbp_ref)
            n_rows = (2 * paw_ref.shape[1], 2 * pap_ref.shape[1])
            return [_remote(srcs[a].at[2 * cx + cy, rows], dsts[a].at[j, rows],
                            b_send.at[j * N_BULK + k], b_recv.at[j * N_BULK + k], (cx, cy, c))
                    for j, (cx, cy) in enumerate(chips) for k, (a, rows) in enumerate(_half_chunks(0, n_rows, 16))]

        @pl.when((i == 0) & (s == 0))
        def _():
            for cp in exchange():
                cp.start()

        @pl.when((i == n_tiles - 1) & (s == n_steps - 1))
        def _():
            for cp in exchange():
                cp.wait_recv()
            for cp in exchange():
                cp.wait_send()

        @pl.when(s == 0)
        def _():
            o_ref[...] = jnp.zeros_like(o_ref)

        for g in range(n_g):
            in_rows = dps[g].shape[0] == t_all
            cond = (s >= lo[g]) & (s < lo[g] + cnt[g])
            if not in_rows:
                cond = cond & (i < n_lat)

            @pl.when(cond)
            def _(g=g):
                o_ref[...] += _nt(dp_refs[g][...], w_refs[g][...])

    in_specs = []
    for g, dp in enumerate(dps):
        nrow = dp.shape[0] // tm
        in_specs.append(pl.BlockSpec(
            (tm, tks[g]), lambda i, s, g=g, nrow=nrow: (jnp.minimum(i, nrow - 1), jnp.clip(s - lo[g], 0, cnt[g] - 1))))
    for g, w in enumerate(wts):
        in_specs.append(pl.BlockSpec((D, tks[g]), lambda i, s, g=g: (0, jnp.clip(s - lo[g], 0, cnt[g] - 1))))
    any_spec = pl.BlockSpec(memory_space=pl.ANY)
    return _pallas(
        body, name="dgrad_w_in", grid=(n_tiles, n_steps), in_specs=in_specs + [any_spec, any_spec],
        out_specs=[pl.BlockSpec((tm, D), lambda i, s: (i, 0)), any_spec, any_spec],
        out_shape=[jax.ShapeDtypeStruct((t_all, D), F32), jax.ShapeDtypeStruct((3,) + paw.shape[1:], paw.dtype),
                   jax.ShapeDtypeStruct((3,) + pap.shape[1:], pap.dtype)],
        scratch_shapes=[pltpu.SemaphoreType.DMA((3 * N_BULK,)), pltpu.SemaphoreType.DMA((3 * N_BULK,))],
        compiler_params=_params(("arbitrary", "arbitrary")),
    )(*dps, *wts, paw, pap)


TM_NORM = 512


def norm_mod_fwd(x2, ctx2, scale1, shift, norm_g):
    t = x2.shape[0]
    n_lat = t // TM_NORM
    assert ctx2.shape[0] == TM_NORM
    n_samples = scale1.shape[0] - 1
    tps = n_lat // n_samples

    def body(x_ref, c_ref, sc_ref, sh_ref, g_ref, u_ref):
        i = pl.program_id(0)
        xv = jnp.where(i < n_lat, x_ref[...], c_ref[...])
        rs = lax.rsqrt(jnp.mean(xv * xv, axis=-1, keepdims=True) + EPS)
        u = xv * rs * g_ref[...] * sc_ref[0] + sh_ref[0]
        u_ref[...] = u.astype(u_ref.dtype)

    grp = lambda i: (jnp.minimum(i // tps, n_samples), 0, 0)
    return _pallas(
        body, name="norm_mod_fwd", grid=(n_lat + 1,),
        in_specs=[pl.BlockSpec((TM_NORM, D), lambda i: (jnp.minimum(i, n_lat - 1), 0)),
                  pl.BlockSpec((TM_NORM, D), lambda i: (0, 0)),
                  pl.BlockSpec((1, 1, D), grp), pl.BlockSpec((1, 1, D), grp),
                  pl.BlockSpec((1, D), lambda i: (0, 0))],
        out_specs=pl.BlockSpec((TM_NORM, D), lambda i: (i, 0)),
        out_shape=jax.ShapeDtypeStruct((t + TM_NORM, D), BF16),
        compiler_params=_params(("parallel",)),
    )(x2, ctx2, scale1, shift, norm_g)


def norm_mod_bwd(x2, ctx2, du, dh, scale1, norm_g):
    t = x2.shape[0]
    n_lat = t // TM_NORM
    n_samples = scale1.shape[0] - 1
    tps = n_lat // n_samples
    n_grp = n_samples + 1

    def body(x_ref, c_ref, du_ref, dh_ref, sc_ref, g_ref, dx_ref, dsh_ref, dsc_ref, dg_ref):
        i = pl.program_id(0)
        xv = jnp.where(i < n_lat, x_ref[...], c_ref[...])
        rs = lax.rsqrt(jnp.mean(xv * xv, axis=-1, keepdims=True) + EPS)
        xh = xv * rs
        duv = du_ref[...]
        n = xh * g_ref[...]
        dn = duv * sc_ref[0]
        dxh = dn * g_ref[...]
        dx = rs * (dxh - xh * jnp.mean(dxh * xh, axis=-1, keepdims=True))
        @pl.when(i < n_lat)
        def _():
            dx_ref[...] = dx + dh_ref[...]

        @pl.when(i % tps == 0)
        def _():
            dsh_ref[...] = jnp.zeros_like(dsh_ref)
            dsc_ref[...] = jnp.zeros_like(dsc_ref)

        @pl.when(i == 0)
        def _():
            dg_ref[...] = jnp.zeros_like(dg_ref)

        dsh_ref[0] += jnp.sum(duv, axis=0, keepdims=True)
        dsc_ref[0] += jnp.sum(duv * n, axis=0, keepdims=True)
        dg_ref[...] += jnp.sum(dn * xh, axis=0, keepdims=True)

    grp = lambda i: (jnp.minimum(i // tps, n_samples), 0, 0)
    lat = lambda i: (jnp.minimum(i, n_lat - 1), 0)
    return _pallas(
        body, name="norm_mod_bwd", grid=(n_lat + 1,),
        in_specs=[pl.BlockSpec((TM_NORM, D), lat),
                  pl.BlockSpec((TM_NORM, D), lambda i: (0, 0)),
                  pl.BlockSpec((TM_NORM, D), lambda i: (i, 0)),
                  pl.BlockSpec((TM_NORM, D), lat),
                  pl.BlockSpec((1, 1, D), grp),
                  pl.BlockSpec((1, D), lambda i: (0, 0))],
        out_specs=[pl.BlockSpec((TM_NORM, D), lat),
                   pl.BlockSpec((1, 1, D), grp), pl.BlockSpec((1, 1, D), grp),
                   pl.BlockSpec((1, D), lambda i: (0, 0))],
        out_shape=[jax.ShapeDtypeStruct((t, D), F32),
                   jax.ShapeDtypeStruct((n_grp, 1, D), F32), jax.ShapeDtypeStruct((n_grp, 1, D), F32),
                   jax.ShapeDtypeStruct((1, D), F32)],
        compiler_params=_params(("arbitrary",)),
    )(x2, ctx2, du, dh, scale1, norm_g)


CONV_CB = 256
CONV_NCB = D // CONV_CB
H_OFF = 16


def _conv_pad_shape(vertical):
    if vertical:
        return (GRID_H + 2 * CONV_PAD, GRID_W, CONV_CB)
    return (GRID_H, GRID_W + 2 * H_OFF, CONV_CB)


def _conv_store(pad_ref, img, vertical):
    if vertical:
        pad_ref[pl.ds(CONV_PAD, GRID_H)] = img
    else:
        pad_ref[:, pl.ds(H_OFF, GRID_W), :] = img


def _conv_window(pad_ref, k, vertical, r):
    if vertical:
        return pad_ref[r + k]
    return pad_ref[r, pl.ds(H_OFF - CONV_PAD + k, GRID_W), :]


def _rows(r):
    return pl.ds(pl.multiple_of(r * GRID_W, GRID_W), GRID_W)


def conv_fwd(p1, conv_w, conv_b, n_samples):
    t = n_samples * SEQ

    def make(vertical, prev):
        def body(gv_ref, gg_ref, w_ref, b_ref, *rest):
            o_ref, pad_ref = rest[-2], rest[-1]
            pad_ref[...] = jnp.zeros_like(pad_ref)
            a = gv_ref[...].astype(F32) * _sigmoid(gg_ref[...].astype(F32))
            _conv_store(pad_ref, a.reshape(GRID_H, GRID_W, CONV_CB), vertical)

            def row(r, carry):
                acc = jnp.zeros((GRID_W, CONV_CB), F32) + b_ref[...]
                for k in range(CONV_K):
                    acc = acc + _conv_window(pad_ref, k, vertical, r) * w_ref[pl.ds(k, 1), :]
                o_ref[_rows(r), :] = acc
                return carry

            lax.fori_loop(0, GRID_H, row, 0)

        cb0 = CONV_NCB // 2 if vertical else 0
        in_specs = [pl.BlockSpec((SEQ, CONV_CB), lambda b, j: (b, 2 * (cb0 + j))),
                    pl.BlockSpec((SEQ, CONV_CB), lambda b, j: (b, 2 * (cb0 + j) + 1)),
                    pl.BlockSpec((CONV_K + 1, CONV_CB), lambda b, j: (0, cb0 + j)),
                    pl.BlockSpec((1, CONV_CB), lambda b, j: (0, cb0 + j))]
        args = [p1, p1, conv_w, conv_b]
        aliases = {}
        if prev is not None:
            in_specs.append(pl.BlockSpec(memory_space=pl.ANY))
            args.append(prev)
            aliases = {4: 0}
        return _pallas(
            body, name="conv_fwd_v" if vertical else "conv_fwd_h", grid=(n_samples, CONV_NCB // 2),
            in_specs=in_specs,
            out_specs=pl.BlockSpec((SEQ, CONV_CB), lambda b, j: (b, cb0 + j)),
            out_shape=jax.ShapeDtypeStruct((t, D), F32),
            scratch_shapes=[pltpu.VMEM(_conv_pad_shape(vertical), F32)],
            input_output_aliases=aliases,
            compiler_params=_params(("parallel", "parallel")),
        )(*args)

    return make(True, make(False, None))


def conv_bwd(p1, daconv, conv_w, n_samples):
    t = n_samples * SEQ

    def make(vertical, prev):
        def body(gv_ref, gg_ref, dy_ref, w_ref, *rest):
            dp_ref, dw_ref, db_ref, pad_ref, dpad_ref, da_ref = rest[-6:]
            pad_ref[...] = jnp.zeros_like(pad_ref)
            dpad_ref[...] = jnp.zeros_like(dpad_ref)
            gv = gv_ref[...].astype(F32)
            sg = _sigmoid(gg_ref[...].astype(F32))
            _conv_store(pad_ref, (gv * sg).reshape(GRID_H, GRID_W, CONV_CB), vertical)
            _conv_store(dpad_ref, dy_ref[...].reshape(GRID_H, GRID_W, CONV_CB), vertical)

            def row(r, carry):
                acc = jnp.zeros((GRID_W, CONV_CB), F32)
                for k in range(CONV_K):
                    acc = acc + _conv_window(dpad_ref, CONV_K - 1 - k, vertical, r) * w_ref[pl.ds(k, 1), :]
                da_ref[_rows(r), :] = acc
                return carry

            lax.fori_loop(0, GRID_H, row, 0)
            da = da_ref[...]
            dp_ref[:, pl.ds(0, CONV_CB)] = (da * sg).astype(dp_ref.dtype)
            dp_ref[:, pl.ds(CONV_CB, CONV_CB)] = (da * gv * sg * (1.0 - sg)).astype(dp_ref.dtype)

            for k in range(CONV_K):
                def wrow(r, acc, k=k):
                    return acc + _conv_window(pad_ref, k, vertical, r) * dy_ref[_rows(r), :]
                acc = lax.fori_loop(0, GRID_H, wrow, jnp.zeros((GRID_W, CONV_CB), F32))
                dw_ref[0, pl.ds(k, 1), :] = jnp.sum(acc, axis=0, keepdims=True)
            dw_ref[0, pl.ds(CONV_K, 1), :] = jnp.zeros((1, CONV_CB), F32)
            db_ref[0] = jnp.sum(dy_ref[...], axis=0, keepdims=True)

        cb0 = CONV_NCB // 2 if vertical else 0
        in_specs = [pl.BlockSpec((SEQ, CONV_CB), lambda b, j: (b, 2 * (cb0 + j))),
                    pl.BlockSpec((SEQ, CONV_CB), lambda b, j: (b, 2 * (cb0 + j) + 1)),
                    pl.BlockSpec((SEQ, CONV_CB), lambda b, j: (b, cb0 + j)),
                    pl.BlockSpec((CONV_K + 1, CONV_CB), lambda b, j: (0, cb0 + j))]
        args = [p1, p1, daconv, conv_w]
        aliases = {}
        if prev is not None:
            in_specs += [pl.BlockSpec(memory_space=pl.ANY)] * 3
            args += list(prev)
            aliases = {4: 0, 5: 1, 6: 2}
        return _pallas(
            body, name="conv_bwd_v" if vertical else "conv_bwd_h", grid=(n_samples, CONV_NCB // 2),
            in_specs=in_specs,
            out_specs=[pl.BlockSpec((SEQ, 2 * CONV_CB), lambda b, j: (b, cb0 + j)),
                       pl.BlockSpec((1, CONV_K + 1, CONV_CB), lambda b, j: (b, 0, cb0 + j)),
                       pl.BlockSpec((1, 1, CONV_CB), lambda b, j: (b, 0, cb0 + j))],
            out_shape=[jax.ShapeDtypeStruct((t, 2 * D), BF16),
                       jax.ShapeDtypeStruct((n_samples, CONV_K + 1, D), F32),
                       jax.ShapeDtypeStruct((n_samples, 1, D), F32)],
            scratch_shapes=[pltpu.VMEM(_conv_pad_shape(vertical), F32), pltpu.VMEM(_conv_pad_shape(vertical), F32),
                            pltpu.VMEM((SEQ, CONV_CB), F32)],
            input_output_aliases=aliases,
            compiler_params=_params(("parallel", "parallel")),
        )(*args)

    return make(True, make(False, None))


TM_EW = 256


def ln_gate_fwd(aconv, z, ln_g, ln_b):
    t = aconv.shape[0]

    def body(a_ref, z_ref, g_ref, b_ref, o_ref):
        a = a_ref[...]
        mu = jnp.mean(a, axis=-1, keepdims=True)
        xc = a - mu
        rstd = lax.rsqrt(jnp.mean(xc * xc, axis=-1, keepdims=True) + EPS)
        l = xc * rstd * g_ref[...] + b_ref[...]
        o_ref[...] = (_silu(l) * _silu(z_ref[...].astype(F32))).astype(o_ref.dtype)

    row = pl.BlockSpec((TM_EW, D), lambda i: (i, 0))
    vec = pl.BlockSpec((1, D), lambda i: (0, 0))
    return _pallas(
        body, name="ln_gate_fwd", grid=(t // TM_EW,), in_specs=[row, row, vec, vec], out_specs=row,
        out_shape=jax.ShapeDtypeStruct((t, D), BF16), compiler_params=_params(("parallel",)),
    )(aconv, z, ln_g, ln_b)


def ln_gate_bwd(aconv, z, dac, ln_g, ln_b):
    t = aconv.shape[0]

    def body(a_ref, z_ref, d_ref, g_ref, b_ref, da_ref, dz_ref, dg_ref, db_ref):
        a = a_ref[...]
        zv = z_ref[...].astype(F32)
        dac_v = d_ref[...].astype(F32)
        mu = jnp.mean(a, axis=-1, keepdims=True)
        xc = a - mu
        rstd = lax.rsqrt(jnp.mean(xc * xc, axis=-1, keepdims=True) + EPS)
        xh = xc * rstd
        l = xh * g_ref[...] + b_ref[...]
        dz_ref[...] = (dac_v * _silu(l) * _dsilu(zv)).astype(dz_ref.dtype)
        dl = dac_v * _silu(zv) * _dsilu(l)
        dxh = dl * g_ref[...]
        da_ref[...] = rstd * (dxh - jnp.mean(dxh, axis=-1, keepdims=True)
                              - xh * jnp.mean(dxh * xh, axis=-1, keepdims=True))

        @pl.when(pl.program_id(0) == 0)
        def _():
            dg_ref[...] = jnp.zeros_like(dg_ref)
            db_ref[...] = jnp.zeros_like(db_ref)

        dg_ref[...] += jnp.sum(dl * xh, axis=0, keepdims=True)
        db_ref[...] += jnp.sum(dl, axis=0, keepdims=True)

    row = pl.BlockSpec((TM_EW, D), lambda i: (i, 0))
    vec = pl.BlockSpec((1, D), lambda i: (0, 0))
    return _pallas(
        body, name="ln_gate_bwd", grid=(t // TM_EW,), in_specs=[row, row, row, vec, vec],
        out_specs=[row, row, vec, vec],
        out_shape=[jax.ShapeDtypeStruct((t, D), F32), jax.ShapeDtypeStruct((t, D), BF16),
                   jax.ShapeDtypeStruct((1, D), F32), jax.ShapeDtypeStruct((1, D), F32)],
        compiler_params=_params(("arbitrary",)),
    )(aconv, z, dac, ln_g, ln_b)


TM_PREP = 256
PREP_LAT = SEQ // TM_PREP
PREP_ALL = SEQ_ALL // TM_PREP


def _chunk_tri(n, upper):
    r = lax.broadcasted_iota(jnp.int32, (n, n), 0)
    c = lax.broadcasted_iota(jnp.int32, (n, n), 1)
    same = (r // CHUNK) == (c // CHUNK)
    keep = (c >= r) if upper else (c <= r)
    return jnp.where(same & keep, 1.0, 0.0).astype(F32)


def _chunk_sums(v, upper):
    tri = _chunk_tri(v.shape[0], upper).astype(BF16)
    hi = v.astype(BF16)
    r1 = v - hi.astype(F32)
    mid = r1.astype(BF16)
    lo = (r1 - mid.astype(F32)).astype(BF16)
    return (_nn(tri, hi) + _nn(tri, mid)) + _nn(tri, lo)


def _prep_tile_maps(n_samples):
    n_lat = n_samples * PREP_LAT

    def seq_map(i):
        return jnp.where(i < n_lat, i // PREP_LAT, i - n_lat), jnp.where(i < n_lat, i % PREP_LAT, PREP_LAT)

    return n_lat, seq_map


def gla_prep_fwd(p3, upf, upb, bias_f, bias_b, n_samples):
    n_lat, seq_map = _prep_tile_maps(n_samples)
    n_tiles = n_lat + n_samples

    def body(v_ref, q_ref, k_ref, ab_ref, upf_ref, upb_ref, bf_ref, bb_ref, qo, ko, vo, cf, cb):
        i = pl.program_id(0)
        qo[0] = jnp.where(i < n_lat, q_ref[...].astype(F32) * Q_SCALE, 0.0)
        ko[0] = k_ref[...].astype(F32)
        vo[0] = v_ref[...].astype(F32)
        ab = ab_ref[...].astype(F32)
        gf = _log_sigmoid(_nn(ab, upf_ref[...], HI) + bf_ref[...]) * (1.0 / GATE_TAU)
        gb = _log_sigmoid(_nn(ab, upb_ref[...], HI) + bb_ref[...]) * (1.0 / GATE_TAU)
        cf[0] = _chunk_sums(gf, False)
        cb[0] = _chunk_sums(gb, True)

    def o_spec(w):
        return pl.BlockSpec((1, TM_PREP, w), lambda i: (*seq_map(i), 0))

    full = lambda shape: pl.BlockSpec(shape, lambda i: (0,) * len(shape))
    return _pallas(
        body, name="gla_prep_fwd", grid=(n_tiles,),
        in_specs=[pl.BlockSpec((TM_PREP, 1024), lambda i: (i, O3_V // 1024)),
                  pl.BlockSpec((TM_PREP, 512), lambda i: (i, O3_Q // 512)),
                  pl.BlockSpec((TM_PREP, 512), lambda i: (i, O3_K // 512)),
                  pl.BlockSpec((TM_PREP, 128), lambda i: (i, O3_AB // 128)),
                  full((128, GLA_DK)), full((128, GLA_DK)), full((1, GLA_DK)), full((1, GLA_DK))],
        out_specs=[o_spec(GLA_DK), o_spec(GLA_DK), o_spec(D), o_spec(GLA_DK), o_spec(GLA_DK)],
        out_shape=[jax.ShapeDtypeStruct((n_samples, SEQ_ALL, GLA_DK), F32),
                   jax.ShapeDtypeStruct((n_samples, SEQ_ALL, GLA_DK), F32),
                   jax.ShapeDtypeStruct((n_samples, SEQ_ALL, D), F32),
                   jax.ShapeDtypeStruct((n_samples, SEQ_ALL, GLA_DK), F32),
                   jax.ShapeDtypeStruct((n_samples, SEQ_ALL, GLA_DK), F32)],
        compiler_params=_params(("parallel",)),
    )(p3, p3, p3, p3, upf, upb, bias_f, bias_b)


def gla_prep_bwd(p3, dq_f, dq_b, dk_f, dk_b, dv_f, dv_b, dc_f, dc_b, upf, upb, bias_f, bias_b, n_samples):
    n_lat, seq_map = _prep_tile_maps(n_samples)
    n_tiles = n_lat + n_samples

    def body(ab_ref, dqf, dqb, dkf, dkb, dvf, dvb, dcf, dcb, upf_ref, upb_ref, bf_ref, bb_ref,
             dp_ref, duf_ref, dub_ref, dbf_ref, dbb_ref):
        i = pl.program_id(0)
        dp_ref[:, pl.ds(O3_V, D)] = (dvf[0] + dvb[0]).astype(dp_ref.dtype)
        dq = jnp.where(i < n_lat, (dqf[0] + dqb[0]) * Q_SCALE, 0.0)
        dp_ref[:, pl.ds(O3_Q, GLA_DK)] = dq.astype(dp_ref.dtype)
        dp_ref[:, pl.ds(O3_K, GLA_DK)] = (dkf[0] + dkb[0]).astype(dp_ref.dtype)
        ab = ab_ref[...].astype(F32)
        zf = _nn(ab, upf_ref[...], HI) + bf_ref[...]
        zb = _nn(ab, upb_ref[...], HI) + bb_ref[...]
        dgf = _chunk_sums(dcf[0], True)
        dgb = _chunk_sums(dcb[0], False)
        dzf = dgf * (1.0 / GATE_TAU) * _sigmoid(-zf)
        dzb = dgb * (1.0 / GATE_TAU) * _sigmoid(-zb)
        dab = _nt(dzf, upf_ref[...], HI) + _nt(dzb, upb_ref[...], HI)
        dp_ref[:, pl.ds(O3_AB, 128)] = dab.astype(dp_ref.dtype)

        @pl.when(i == 0)
        def _():
            duf_ref[...] = jnp.zeros_like(duf_ref)
            dub_ref[...] = jnp.zeros_like(dub_ref)
            dbf_ref[...] = jnp.zeros_like(dbf_ref)
            dbb_ref[...] = jnp.zeros_like(dbb_ref)

        duf_ref[...] += _tn(ab, dzf, HI)
        dub_ref[...] += _tn(ab, dzb, HI)
        dbf_ref[...] += jnp.sum(dzf, axis=0, keepdims=True)
        dbb_ref[...] += jnp.sum(dzb, axis=0, keepdims=True)

    def s_spec(w):
        return pl.BlockSpec((1, TM_PREP, w), lambda i: (*seq_map(i), 0))

    full = lambda shape: pl.BlockSpec(shape, lambda i: (0,) * len(shape))
    return _pallas(
        body, name="gla_prep_bwd", grid=(n_tiles,),
        in_specs=[pl.BlockSpec((TM_PREP, 128), lambda i: (i, O3_AB // 128)),
                  s_spec(GLA_DK), s_spec(GLA_DK), s_spec(GLA_DK), s_spec(GLA_DK), s_spec(D), s_spec(D),
                  s_spec(GLA_DK), s_spec(GLA_DK),
                  full((128, GLA_DK)), full((128, GLA_DK)), full((1, GLA_DK)), full((1, GLA_DK))],
        out_specs=[pl.BlockSpec((TM_PREP, W3), lambda i: (i, 0)),
                   full((128, GLA_DK)), full((128, GLA_DK)), full((1, GLA_DK)), full((1, GLA_DK))],
        out_shape=[jax.ShapeDtypeStruct((n_tiles * TM_PREP, W3), BF16),
                   jax.ShapeDtypeStruct((128, GLA_DK), F32), jax.ShapeDtypeStruct((128, GLA_DK), F32),
                   jax.ShapeDtypeStruct((1, GLA_DK), F32), jax.ShapeDtypeStruct((1, GLA_DK), F32)],
        compiler_params=_params(("arbitrary",)),
    )(p3, dq_f, dq_b, dk_f, dk_b, dv_f, dv_b, dc_f, dc_b, upf, upb, bias_f, bias_b)


def _sub_blocks(rev):
    if NSUB == 1:
        return [((0, CHUNK), CHUNK // 2, (0, CHUNK))]
    out = []
    for s in range(NSUB):
        rows = (s * SUB, SUB)
        if rev:
            ref = (s + 1) * SUB if s < NSUB - 1 else None
            cols = (s * SUB, CHUNK - s * SUB)
        else:
            ref = s * SUB - 1 if s > 0 else None
            cols = (0, (s + 1) * SUB)
        out.append((rows, ref, cols))
    return out


def _sub_mask(rows, cols, rev):
    r = rows[0] + lax.broadcasted_iota(jnp.int32, (rows[1], cols[1]), 0)
    c = cols[0] + lax.broadcasted_iota(jnp.int32, (rows[1], cols[1]), 1)
    return (c >= r) if rev else (c <= r)


def _sub_operands(qc, kc, cc, rows, ref, cols):
    cref = jnp.zeros((1, HEAD_K), F32) if ref is None else cc[ref:ref + 1]
    eq = jnp.exp(cc[rows[0]:rows[0] + rows[1]] - cref)
    ek = jnp.exp(cref - cc[cols[0]:cols[0] + cols[1]])
    qs = qc[rows[0]:rows[0] + rows[1]] * eq
    kk = kc[cols[0]:cols[0] + cols[1]] * ek
    return qs, kk, eq, ek


SCAN_ROWS = 256
SCAN_CHUNKS = SCAN_ROWS // CHUNK
SCAN_STEPS = SEQ_ALL // SCAN_ROWS
LAT_BLOCKS = SEQ // SCAN_ROWS


def _scan_block(t, rev):
    if rev:
        return SCAN_STEPS - 1 - t
    return jnp.where(t == 0, SCAN_STEPS - 1, t - 1)


def _scan_lat_block(t, rev):
    if rev:
        return jnp.minimum(SCAN_STEPS - 1 - t, LAT_BLOCKS - 1)
    return jnp.maximum(t - 1, 0)


def _head_cols(h):
    return pl.ds(h * HEAD_K, HEAD_K), pl.ds(h * HEAD_V, HEAD_V)


def gla_scan_fwd(q, k, v, cum, *, rev, name):
    n = q.shape[0]

    def body(q_ref, k_ref, v_ref, c_ref, o_ref, s_ref, sfin_ref, st):
        t = pl.program_id(1)

        @pl.when(t == 0)
        def _():
            st[...] = jnp.zeros_like(st)

        def chunk(j, carry):
            lj = SCAN_CHUNKS - 1 - j if rev else j
            r0 = pl.multiple_of(lj * CHUNK, CHUNK)
            rws = pl.ds(r0, CHUNK)
            for h in range(HEADS):
                kcols, vcols = _head_cols(h)
                qc, kc, cc = q_ref[0, rws, kcols], k_ref[0, rws, kcols], c_ref[0, rws, kcols]
                vc = v_ref[0, rws, vcols]
                s_in = st[h]
                s_ref[0, h, j] = s_in
                edge = cc[0:1] if rev else cc[CHUNK - 1:CHUNK]
                ke = kc * jnp.exp(edge - cc)
                st[h] = s_in * jnp.exp(edge) + _tn(_b16(vc), _b16(ke))
                o_inter = _nt(_b16(qc * jnp.exp(cc)), _b16(s_in))
                vb = _b16(vc)
                for rows, ref, cols in _sub_blocks(rev):
                    qs, kk, _, _ = _sub_operands(qc, kc, cc, rows, ref, cols)
                    a = jnp.where(_sub_mask(rows, cols, rev), _nt(_b16(qs), _b16(kk)), 0.0)
                    o_s = _nn(_b16(a), vb[cols[0]:cols[0] + cols[1]])
                    o_ref[0, pl.ds(r0 + rows[0], rows[1]), vcols] = o_inter[rows[0]:rows[0] + rows[1]] + o_s
            return carry

        lax.fori_loop(0, SCAN_CHUNKS, chunk, 0)

        @pl.when(t == SCAN_STEPS - 1)
        def _():
            sfin_ref[0] = st[...]

    def spec(w):
        return pl.BlockSpec((1, SCAN_ROWS, w), lambda b, t: (b, _scan_block(t, rev), 0))

    return _pallas(
        body, name=name, grid=(n, SCAN_STEPS),
        in_specs=[spec(GLA_DK), spec(GLA_DK), spec(D), spec(GLA_DK)],
        out_specs=[pl.BlockSpec((1, SCAN_ROWS, D), lambda b, t: (b, _scan_lat_block(t, rev), 0)),
                   pl.BlockSpec((1, HEADS, SCAN_CHUNKS, HEAD_V, HEAD_K), lambda b, t: (b, 0, t, 0, 0)),
                   pl.BlockSpec((1, HEADS, HEAD_V, HEAD_K), lambda b, t: (b, 0, 0, 0))],
        out_shape=[jax.ShapeDtypeStruct((n, SEQ, D), F32),
                   jax.ShapeDtypeStruct((n, HEADS, NCHUNK, HEAD_V, HEAD_K), F32),
                   jax.ShapeDtypeStruct((n, HEADS, HEAD_V, HEAD_K), F32)],
        scratch_shapes=[pltpu.VMEM((HEADS, HEAD_V, HEAD_K), F32)],
        compiler_params=_params(("parallel", "arbitrary")),
    )(q, k, v, cum)


def gla_scan_bwd(q, k, v, cum, s_all, s_fin, do, *, rev, name):
    n = q.shape[0]

    def body(q_ref, k_ref, v_ref, c_ref, s_ref, sfin_ref, do_ref, dq_ref, dk_ref, dv_ref, dc_ref,
             dst, s_next, dq_acc, dk_acc, dv_acc):
        t = SCAN_STEPS - 1 - pl.program_id(1)

        @pl.when(pl.program_id(1) == 0)
        def _():
            dst[...] = jnp.zeros_like(dst)
            s_next[...] = sfin_ref[0]

        def chunk(jj, carry):
            j = SCAN_CHUNKS - 1 - jj
            lj = SCAN_CHUNKS - 1 - j if rev else j
            rws = pl.ds(pl.multiple_of(lj * CHUNK, CHUNK), CHUNK)
            for h in range(HEADS):
                kcols, vcols = _head_cols(h)
                qc, kc, cc = q_ref[0, rws, kcols], k_ref[0, rws, kcols], c_ref[0, rws, kcols]
                vc = v_ref[0, rws, vcols]
                doc = jnp.where(t > 0, do_ref[0, rws, vcols], 0.0)
                s_in = s_ref[0, h, j]
                s_out = s_next[h]
                ds_out = dst[h]
                edge = cc[0:1] if rev else cc[CHUNK - 1:CHUNK]
                e_q = jnp.exp(cc)
                e_k = jnp.exp(edge - cc)
                dob = _b16(doc)
                dsb = _b16(ds_out)
                dst[h] = ds_out * jnp.exp(edge) + _tn(dob, _b16(qc * e_q))
                s_next[h] = s_in
                dq_acc[h] = e_q * _nn(dob, _b16(s_in))
                dk_acc[h] = e_k * _nn(_b16(vc), dsb)
                dv_acc[h] = _nt(_b16(kc * e_k), dsb)
                vb = _b16(vc)
                for rows, ref, cols in _sub_blocks(rev):
                    qs, kk, eq, ek = _sub_operands(qc, kc, cc, rows, ref, cols)
                    mask = _sub_mask(rows, cols, rev)
                    rsl = slice(rows[0], rows[0] + rows[1])
                    csl = pl.ds(cols[0], cols[1])
                    qsb, kkb = _b16(qs), _b16(kk)
                    a = jnp.where(mask, _nt(qsb, kkb), 0.0)
                    da = _b16(jnp.where(mask, _nt(dob[rsl], vb[cols[0]:cols[0] + cols[1]]), 0.0))
                    dq_acc[h, pl.ds(rows[0], rows[1]), :] += _nn(da, kkb) * eq
                    dk_acc[h, csl, :] += _tn(da, qsb) * ek
                    dv_acc[h, csl, :] += _tn(_b16(a), dob[rsl])
                dq = dq_acc[h]
                dk = dk_acc[h]
                dc = qc * dq - kc * dk
                bnd = jnp.sum(ds_out * s_out, axis=0, keepdims=True)
                edge_row = 0 if rev else CHUNK - 1
                is_edge = lax.broadcasted_iota(jnp.int32, (CHUNK, HEAD_K), 0) == edge_row
                dq_ref[0, rws, kcols] = dq
                dk_ref[0, rws, kcols] = dk
                dv_ref[0, rws, vcols] = dv_acc[h]
                dc_ref[0, rws, kcols] = dc + jnp.where(is_edge, bnd, 0.0)
            return carry

        lax.fori_loop(0, SCAN_CHUNKS, chunk, 0)

    def step_of(u):
        return SCAN_STEPS - 1 - u

    def spec(w):
        return pl.BlockSpec((1, SCAN_ROWS, w), lambda b, u: (b, _scan_block(step_of(u), rev), 0))

    return _pallas(
        body, name=name, grid=(n, SCAN_STEPS),
        in_specs=[spec(GLA_DK), spec(GLA_DK), spec(D), spec(GLA_DK),
                  pl.BlockSpec((1, HEADS, SCAN_CHUNKS, HEAD_V, HEAD_K), lambda b, u: (b, 0, step_of(u), 0, 0)),
                  pl.BlockSpec((1, HEADS, HEAD_V, HEAD_K), lambda b, u: (b, 0, 0, 0)),
                  pl.BlockSpec((1, SCAN_ROWS, D), lambda b, u: (b, _scan_lat_block(step_of(u), rev), 0))],
        out_specs=[spec(GLA_DK), spec(GLA_DK), spec(D), spec(GLA_DK)],
        out_shape=[jax.ShapeDtypeStruct((n, SEQ_ALL, GLA_DK), F32), jax.ShapeDtypeStruct((n, SEQ_ALL, GLA_DK), F32),
                   jax.ShapeDtypeStruct((n, SEQ_ALL, D), F32), jax.ShapeDtypeStruct((n, SEQ_ALL, GLA_DK), F32)],
        scratch_shapes=[pltpu.VMEM((HEADS, HEAD_V, HEAD_K), F32), pltpu.VMEM((HEADS, HEAD_V, HEAD_K), F32),
                        pltpu.VMEM((HEADS, CHUNK, HEAD_K), F32), pltpu.VMEM((HEADS, CHUNK, HEAD_K), F32),
                        pltpu.VMEM((HEADS, CHUNK, HEAD_V), F32)],
        compiler_params=_params(("parallel", "arbitrary")),
    )(q, k, v, cum, s_all, s_fin, do)


def gla_out_fwd(o_f, o_b, r, gnorm):
    n = o_f.shape[0]
    tiles = SEQ // TM_EW

    def body(of_ref, ob_ref, r_ref, g_ref, og_ref):
        for h in range(HEADS):
            cols = pl.ds(h * HEAD_V, HEAD_V)
            o = of_ref[0, :, cols] + ob_ref[0, :, cols]
            rs = lax.rsqrt(jnp.mean(o * o, axis=-1, keepdims=True) + EPS)
            og_ref[:, cols] = (o * rs * g_ref[...] * _silu(r_ref[:, cols].astype(F32))).astype(og_ref.dtype)

    ospec = pl.BlockSpec((1, TM_EW, D), lambda b, j: (b, j, 0))
    row = pl.BlockSpec((TM_EW, D), lambda b, j: (b * tiles + j, 0))
    return _pallas(
        body, name="gla_out_fwd", grid=(n, tiles),
        in_specs=[ospec, ospec, row, pl.BlockSpec((1, HEAD_V), lambda b, j: (0, 0))],
        out_specs=row, out_shape=jax.ShapeDtypeStruct((n * SEQ, D), BF16),
        compiler_params=_params(("parallel", "parallel")),
    )(o_f, o_b, r, gnorm)


def gla_out_bwd(o_f, o_b, r, dog, gnorm):
    n = o_f.shape[0]
    tiles = SEQ // TM_EW

    def body(of_ref, ob_ref, r_ref, d_ref, g_ref, do_ref, dr_ref, dg_ref):
        @pl.when((pl.program_id(0) == 0) & (pl.program_id(1) == 0))
        def _():
            dg_ref[...] = jnp.zeros_like(dg_ref)

        for h in range(HEADS):
            cols = pl.ds(h * HEAD_V, HEAD_V)
            o = of_ref[0, :, cols] + ob_ref[0, :, cols]
            rv = r_ref[:, cols].astype(F32)
            dv = d_ref[:, cols].astype(F32)
            rs = lax.rsqrt(jnp.mean(o * o, axis=-1, keepdims=True) + EPS)
            oh = o * rs
            dr_ref[:, cols] = (dv * oh * g_ref[...] * _dsilu(rv)).astype(dr_ref.dtype)
            dn = dv * _silu(rv)
            dg_ref[...] += jnp.sum(dn * oh, axis=0, keepdims=True)
            doh = dn * g_ref[...]
            do_ref[0, :, cols] = rs * (doh - oh * jnp.mean(doh * oh, axis=-1, keepdims=True))

    ospec = pl.BlockSpec((1, TM_EW, D), lambda b, j: (b, j, 0))
    row = pl.BlockSpec((TM_EW, D), lambda b, j: (b * tiles + j, 0))
    vec = pl.BlockSpec((1, HEAD_V), lambda b, j: (0, 0))
    return _pallas(
        body, name="gla_out_bwd", grid=(n, tiles),
        in_specs=[ospec, ospec, row, row, vec],
        out_specs=[ospec, row, vec],
        out_shape=[jax.ShapeDtypeStruct((n, SEQ, D), F32), jax.ShapeDtypeStruct((n * SEQ, D), BF16),
                   jax.ShapeDtypeStruct((1, HEAD_V), F32)],
        compiler_params=_params(("arbitrary", "arbitrary")),
    )(o_f, o_b, r, dog, gnorm)


def merge_fwd(p5, y_conv, y_gla):
    t = y_conv.shape[0]

    def body(mc_ref, mg_ref, yc_ref, yg_ref, o_ref):
        f = lambda ref: ref[...].astype(F32)
        o_ref[...] = (_sigmoid(f(mc_ref)) * f(yc_ref) + _sigmoid(f(mg_ref)) * f(yg_ref)).astype(o_ref.dtype)

    row = pl.BlockSpec((TM_EW, D), lambda i: (i, 0))
    return _pallas(
        body, name="merge_fwd", grid=(t // TM_EW,),
        in_specs=[row, pl.BlockSpec((TM_EW, D), lambda i: (i, 1)), row, row], out_specs=row,
        out_shape=jax.ShapeDtypeStruct((t, D), BF16), compiler_params=_params(("parallel",)),
    )(p5, p5, y_conv, y_gla)


def merge_bwd(p5, y_conv, y_gla, dmerged):
    t = y_conv.shape[0]

    def body(mc_ref, mg_ref, yc_ref, yg_ref, d_ref, dyc_ref, dyg_ref, dp_ref):
        f = lambda ref: ref[...].astype(F32)
        d = f(d_ref)
        sc = _sigmoid(f(mc_ref))
        sg = _sigmoid(f(mg_ref))
        dyc_ref[...] = (d * sc).astype(dyc_ref.dtype)
        dyg_ref[...] = (d * sg).astype(dyg_ref.dtype)
        dp_ref[:, pl.ds(0, D)] = (d * f(yc_ref) * sc * (1.0 - sc)).astype(dp_ref.dtype)
        dp_ref[:, pl.ds(D, D)] = (d * f(yg_ref) * sg * (1.0 - sg)).astype(dp_ref.dtype)

    row = pl.BlockSpec((TM_EW, D), lambda i: (i, 0))
    return _pallas(
        body, name="merge_bwd", grid=(t // TM_EW,),
        in_specs=[row, pl.BlockSpec((TM_EW, D), lambda i: (i, 1)), row, row, row],
        out_specs=[row, row, pl.BlockSpec((TM_EW, 2 * D), lambda i: (i, 0))],
        out_shape=[jax.ShapeDtypeStruct((t, D), BF16), jax.ShapeDtypeStruct((t, D), BF16),
                   jax.ShapeDtypeStruct((t, 2 * D), BF16)],
        compiler_params=_params(("parallel",)),
    )(p5, p5, y_conv, y_gla, dmerged)


def final_fwd_bwd(x2, mo, gate, final_g, target, n_samples):
    t = x2.shape[0]
    tiles = SEQ // TM_EW

    def body(x_ref, mo_ref, gate_ref, g_ref, t_ref, dh_ref, dmo_ref, dgate_ref, dg_ref, loss_ref):
        b, j = pl.program_id(0), pl.program_id(1)
        mo_v = mo_ref[...]
        h = x_ref[...] + gate_ref[0] * mo_v
        rs = lax.rsqrt(jnp.mean(h * h, axis=-1, keepdims=True) + EPS)
        nh = h * rs
        err = nh * g_ref[...] - t_ref[...]
        dy = err * (1.0 / D)
        dn = dy * g_ref[...]
        dh = rs * (dn - nh * jnp.mean(dn * nh, axis=-1, keepdims=True))
        dh_ref[...] = dh
        dmo_ref[...] = (dh * gate_ref[0]).astype(dmo_ref.dtype)

        @pl.when(j == 0)
        def _():
            dgate_ref[...] = jnp.zeros_like(dgate_ref)

        @pl.when((b == 0) & (j == 0))
        def _():
            dg_ref[...] = jnp.zeros_like(dg_ref)
            loss_ref[...] = jnp.zeros_like(loss_ref)

        dgate_ref[0] += jnp.sum(dh * mo_v, axis=0, keepdims=True)
        dg_ref[...] += jnp.sum(dy * nh, axis=0, keepdims=True)
        loss_ref[...] += (0.5 / D) * jnp.sum(err * err)

    row = pl.BlockSpec((TM_EW, D), lambda b, j: (b * tiles + j, 0))
    per = pl.BlockSpec((1, 1, D), lambda b, j: (b, 0, 0))
    vec = pl.BlockSpec((1, D), lambda b, j: (0, 0))
    return _pallas(
        body, name="final_fwd_bwd", grid=(n_samples, tiles),
        in_specs=[row, row, per, vec, row],
        out_specs=[row, row, per, vec, pl.BlockSpec((8, 128), lambda b, j: (0, 0))],
        out_shape=[jax.ShapeDtypeStruct((t, D), F32), jax.ShapeDtypeStruct((t, D), BF16),
                   jax.ShapeDtypeStruct((n_samples, 1, D), F32), jax.ShapeDtypeStruct((1, D), F32),
                   jax.ShapeDtypeStruct((8, 128), F32)],
        compiler_params=_params(("arbitrary", "arbitrary")),
    )(x2, mo, gate, final_g, target)


def local_step(x, ctx, target, mod, wts, small):
    n = x.shape[0]
    t = n * SEQ
    t_all = t + n * NCTX
    x2 = x.reshape(t, D)
    ctx2 = ctx.reshape(n * NCTX, D)
    tgt2 = target.reshape(t, D)
    scale1, shift, gate = mod

    u = norm_mod_fwd(x2, ctx2, scale1, shift, small["norm_g"])
    p1 = matmul_nn(u, wts["w1"], small["b1"], name="proj_conv", m=t, tm=512, tn=1024, out_dtype=BF16)
    p2 = matmul_nn(u, wts["w2"], small["b2"], name="proj_z", m=t, tm=512, tn=1024, out_dtype=BF16)
    p3 = matmul_nn(u, wts["w3"], small["b3"], name="proj_gla", m=t_all, tm=512, tn=W3, out_dtype=BF16)
    p4 = matmul_nn(u, wts["w4"], small["b4"], name="proj_r", m=t, tm=512, tn=1024, out_dtype=BF16)
    p5 = matmul_nn(u, wts["w5"], small["b5"], name="proj_merge", m=t, tm=512, tn=1024, out_dtype=BF16)

    aconv = conv_fwd(p1, small["conv_w"], small["conv_b"], n)
    ac = ln_gate_fwd(aconv, p2, small["conv_ln_g"], small["conv_ln_b"])
    y_conv = matmul_nn(ac, wts["conv_proj"], None, name="conv_proj_fwd", m=t, tm=512, tn=1024, out_dtype=BF16)

    qs, ks, vs, cum_f, cum_b = gla_prep_fwd(p3, small["upf"], small["upb"], small["bias_f"], small["bias_b"], n)
    o_f, s_f, sfin_f = gla_scan_fwd(qs, ks, vs, cum_f, rev=False, name="gla_scan_fwd_f")
    o_b, s_b, sfin_b = gla_scan_fwd(qs, ks, vs, cum_b, rev=True, name="gla_scan_fwd_b")
    og = gla_out_fwd(o_f, o_b, p4, small["gla_norm_g"])
    y_gla = matmul_nn(og, wts["gla_proj"], None, name="gla_proj_fwd", m=t, tm=512, tn=1024, out_dtype=BF16)

    merged = merge_fwd(p5, y_conv, y_gla)
    mo = matmul_nn(merged, wts["w_out"], None, name="w_out_fwd", m=t, tm=512, tn=1024, out_dtype=F32)
    dh, dmo, dgate, d_final_g, loss = final_fwd_bwd(x2, mo, gate, small["final_norm_g"], tgt2, n)

    g = {"final_norm_g": d_final_g}
    dmerged = matmul_nt(dmo, wts["w_out"], name="w_out_dgrad", tm=512, out_dtype=BF16)
    g["w_out"] = matmul_tn(merged, dmo, name="w_out_wgrad", t=t, tn=1024, tt=512)[0]
    dyc, dyg, dp5 = merge_bwd(p5, y_conv, y_gla, dmerged)

    dac = matmul_nt(dyc, wts["conv_proj"], name="conv_proj_dgrad", tm=512, out_dtype=BF16)
    g["conv_proj"] = matmul_tn(ac, dyc, name="conv_proj_wgrad", t=t, tn=1024, tt=512)[0]
    daconv, dp2, g["conv_ln_g"], g["conv_ln_b"] = ln_gate_bwd(aconv, p2, dac, small["conv_ln_g"], small["conv_ln_b"])
    dp1, dconv_w, dconv_b = conv_bwd(p1, daconv, small["conv_w"], n)
    g["conv_w"], g["conv_b"] = dconv_w, dconv_b

    dog = matmul_nt(dyg, wts["gla_proj"], name="gla_proj_dgrad", tm=512, out_dtype=BF16)
    g["gla_proj"] = matmul_tn(og, dyg, name="gla_proj_wgrad", t=t, tn=1024, tt=512)[0]
    do, dp4, g["gla_norm_g"] = gla_out_bwd(o_f, o_b, p4, dog, small["gla_norm_g"])
    dq_f, dk_f, dv_f, dc_f = gla_scan_bwd(qs, ks, vs, cum_f, s_f, sfin_f, do, rev=False, name="gla_scan_bwd_f")
    dq_b, dk_b, dv_b, dc_b = gla_scan_bwd(qs, ks, vs, cum_b, s_b, sfin_b, do, rev=True, name="gla_scan_bwd_b")
    dp3, g["upf"], g["upb"], g["bias_f"], g["bias_b"] = gla_prep_bwd(
        p3, dq_f, dq_b, dk_f, dk_b, dv_f, dv_b, dc_f, dc_b,
        small["upf"], small["upb"], small["bias_f"], small["bias_b"], n)

    dps = [dp1, dp2, dp3, dp4, dp5]
    for i, dp in enumerate(dps):
        rows = dp.shape[0]
        tn = W3 if dp.shape[1] == W3 else 1024
        g["w%d" % (i + 1)], g["b%d" % (i + 1)] = matmul_tn(
            u, dp, name="w_in_wgrad_%d" % (i + 1), t=rows, tn=tn, tt=512, colsum=True)
    g["gate"] = dgate
    return loss, dh, dps, g


def _group_cols(w):
    gv, gg, z = w[..., 0:1024], w[..., 1024:2048], w[..., 2048:3072]
    q, k, v = w[..., 3072:3584], w[..., 3584:4096], w[..., 4096:5120]
    ab = w[..., 5120:5152]
    r, mc, mg = w[..., 5152:6176], w[..., 6176:7200], w[..., 7200:8224]
    g1 = jnp.concatenate([p for j in range(CONV_NCB)
                          for p in (gv[..., CONV_CB * j:CONV_CB * (j + 1)], gg[..., CONV_CB * j:CONV_CB * (j + 1)])], -1)
    pad = jnp.zeros(w.shape[:-1] + (W3 - 2080,), w.dtype)
    g3 = jnp.concatenate([v, q, k, ab, pad], -1)
    return g1, z, g3, r, jnp.concatenate([mc, mg], -1)


def _ungroup_cols(g1, g2, g3, g4, g5):
    gv = jnp.concatenate([g1[..., 2 * CONV_CB * j:2 * CONV_CB * j + CONV_CB] for j in range(CONV_NCB)], -1)
    gg = jnp.concatenate([g1[..., 2 * CONV_CB * j + CONV_CB:2 * CONV_CB * (j + 1)] for j in range(CONV_NCB)], -1)
    v, q, k, ab = g3[..., 0:1024], g3[..., 1024:1536], g3[..., 1536:2048], g3[..., 2048:2080]
    return jnp.concatenate([gv, gg, g2, q, k, v, ab, g4, g5[..., 0:1024], g5[..., 1024:2048]], -1)


def _natural_pieces():
    pieces = [(CONV_CB * j, CONV_CB, 0, 2 * CONV_CB * j) for j in range(CONV_NCB)]
    pieces += [(1024 + CONV_CB * j, CONV_CB, 0, 2 * CONV_CB * j + CONV_CB) for j in range(CONV_NCB)]
    pieces += [(2048, 1024, 1, 0), (3072, 512, 2, O3_Q), (3584, 512, 2, O3_K), (4096, 1024, 2, O3_V), (5120, 32, 2, O3_AB),
               (5152, 1024, 3, 0), (6176, 1024, 4, 0), (7200, 1024, 4, 1024)]
    return sorted(pieces)


def _ungroup_to_shards(groups):
    shards = []
    for i in range(N_CHIPS):
        lo, hi = i * W_IN_SHARD, (i + 1) * W_IN_SHARD
        parts = []
        for nat, width, g, gcol in _natural_pieces():
            a, b = max(nat, lo), min(nat + width, hi)
            if a < b:
                parts.append(groups[g][:, gcol + a - nat:gcol + b - nat])
        shards.append(jnp.concatenate(parts, 1))
    return jnp.stack(shards)


def _pad_up(up, row0):
    return jnp.zeros((128, GLA_DK), F32).at[row0:row0 + up.shape[0]].set(up)


def _adamw_math(w, g, m, v):
    m = ADAM_B1 * m + (1.0 - ADAM_B1) * g
    v = ADAM_B2 * v + (1.0 - ADAM_B2) * (g * g)
    m_hat = m / (1.0 - ADAM_B1 ** ADAM_STEP)
    v_hat = v / (1.0 - ADAM_B2 ** ADAM_STEP)
    delta = -ADAM_LR * (m_hat / (jnp.sqrt(v_hat) + ADAM_EPS) + ADAM_WD * w)
    return delta, m, v


def adamw2d(w, g, m, v, *, name, tr):
    rows, cols = w.shape

    def body(w_ref, g_ref, m_ref, v_ref, d_ref, nm_ref, nv_ref):
        d_ref[...], nm_ref[...], nv_ref[...] = _adamw_math(w_ref[...], g_ref[...], m_ref[...], v_ref[...])

    spec = pl.BlockSpec((tr, cols), lambda i: (i, 0))
    return _pallas(
        body, name=name, grid=(rows // tr,), in_specs=[spec] * 4, out_specs=[spec] * 3,
        out_shape=[jax.ShapeDtypeStruct((rows, cols), F32)] * 3, compiler_params=_params(("parallel",)),
    )(w, g, m, v)


def sum_devices(sall):
    rows = sall.shape[1]

    def body(s_ref, o_ref):
        acc = s_ref[0]
        for d in range(1, N_DEV):
            acc = acc + s_ref[d]
        o_ref[...] = acc

    return _pallas(body, name="sum_devices", out_shape=jax.ShapeDtypeStruct((rows, D), F32),
                   compiler_params=_params())(sall)


def pair_add(core, g, got, *, name, tr):
    n, rows, cols = got.shape
    g4 = g.reshape(n, 2, rows, cols)

    def body(core_ref, g_ref, got_ref, o_ref, ob_ref):
        del core_ref
        s = g_ref[0, 0] + got_ref[0]
        o_ref[0] = s
        ob_ref[0] = s.astype(BF16)

    spec = pl.BlockSpec((1, tr, cols), lambda i, t, core_ref: (i, t, 0))
    return _pallas(
        body, name=name,
        grid_spec=pltpu.PrefetchScalarGridSpec(
            num_scalar_prefetch=1, grid=(n, rows // tr),
            in_specs=[pl.BlockSpec((1, 1, tr, cols), lambda i, t, core_ref: (i, core_ref[0], t, 0)), spec],
            out_specs=[spec, spec]),
        out_shape=[jax.ShapeDtypeStruct(got.shape, F32), jax.ShapeDtypeStruct(got.shape, BF16)],
        compiler_params=_params(("parallel", "parallel")))(core, g4, got)


def chip_add(place, pa, rb, *, name, tr):
    _, rows, cols = pa.shape

    def body(place_ref, m_ref, r_ref, o_ref):
        del place_ref
        o_ref[0] = ((m_ref[0] + r_ref[0].astype(F32)) + r_ref[1].astype(F32)) + r_ref[2].astype(F32)

    return _pallas(
        body, name=name,
        grid_spec=pltpu.PrefetchScalarGridSpec(
            num_scalar_prefetch=1, grid=(rows // tr,),
            in_specs=[pl.BlockSpec((1, tr, cols), lambda t, place_ref: (place_ref[0], t, 0)),
                      pl.BlockSpec((3, tr, cols), lambda t, place_ref: (0, t, 0))],
            out_specs=pl.BlockSpec((1, tr, cols), lambda t, place_ref: (place_ref[1], t, 0))),
        out_shape=jax.ShapeDtypeStruct((2, rows, cols), F32),
        compiler_params=_params(("parallel",)))(place, pa, rb)


def ada_bwd(call, cctx_rows, dm_shard, dm_full, adaw):
    nsh = adaw.shape[1]

    def body(c_ref, cc_ref, dms_ref, dmf_ref, w_ref, gw_ref, gb_ref, pq_ref):
        a_lat = _silu(c_ref[...])
        a_ctx = _silu(cc_ref[...])
        dms = dms_ref[...]
        gw_ref[...] = _tn(a_lat, dms[0:64], HI) + _tn(a_ctx, dms[64:72], HI)
        gb_ref[...] = jnp.sum(dmf_ref[...], axis=0, keepdims=True)
        part = _nt(dms[64:72], w_ref[...], HI)
        pq_ref[...] = jnp.zeros_like(pq_ref) + jnp.sum(part, axis=0, keepdims=True)

    return _pallas(body, name="ada_bwd",
                   out_shape=[jax.ShapeDtypeStruct((D, nsh), F32), jax.ShapeDtypeStruct((1, 3 * D), F32),
                              jax.ShapeDtypeStruct((8, D), F32)],
                   compiler_params=_params())(call, cctx_rows, dm_shard, dm_full, adaw)


def cctx_grad(pq_all, cctx_rows):
    def body(p_ref, c_ref, o_ref):
        acc = p_ref[0]
        for qi in range(1, N_CHIPS):
            acc = acc + p_ref[qi]
        o_ref[...] = acc * _dsilu(c_ref[...])

    return _pallas(body, name="cctx_grad", out_shape=jax.ShapeDtypeStruct((8, D), F32),
                   compiler_params=_params())(pq_all, cctx_rows)


def _place():
    x, y, c = lax.axis_index("x"), lax.axis_index("y"), lax.axis_index("c")
    chips = [(1 - x, y), (x, 1 - y), (1 - x, 1 - y)]
    return x, y, c, chips


def _all_peers(x, y, c):
    return [((1 - x) if r & 4 else x, (1 - y) if r & 2 else y, (1 - c) if r & 1 else c) for r in range(1, N_DEV)]


def _remote(src, dst, send_sem, recv_sem, dev):
    return pltpu.make_async_remote_copy(src_ref=src, dst_ref=dst, send_sem=send_sem, recv_sem=recv_sem,
                                        device_id=dev, device_id_type=MESH)


ANY = pl.BlockSpec(memory_space=pl.ANY)
VMEM = pl.BlockSpec(memory_space=pltpu.VMEM)
F_ROWS = 16


W_ROW_CHUNKS = 4
P_ROW_CHUNKS = 2
N_BULK = W_ROW_CHUNKS + P_ROW_CHUNKS


def _half_chunks(core, n_rows, align):
    out = []
    for a, k in ((0, W_ROW_CHUNKS), (1, P_ROW_CHUNKS)):
        half = n_rows[a] // 2
        size = half // k
        for i in range(k):
            start = core * half + i * size
            out.append((a, pl.ds(start if isinstance(start, int) else pl.multiple_of(start, align), size)))
    return out


def gather_weights(c8, cctx8, adaw, adab, w_sh, p_sh, fp):
    nsh = adaw.shape[1]

    def body(c_ref, cctx_ref, adaw_ref, adab_ref, w_ref, p_ref, fp_ref, wall_ref, pall_ref, fall_ref, call_ref, mall_ref,
             abuf, w_send, w_recv, h_send, h_recv, c_send, c_recv, m_send, m_recv, f_send, f_recv):
        x, y, c, chips = _place()
        q = 2 * x + y
        dev = 4 * x + 2 * y + c
        qs = [2 * cx + cy for cx, cy in chips]
        sib = (x, y, 1 - c)
        srcs, dsts = (w_ref, p_ref), (wall_ref, pall_ref)
        n_rows = (w_ref.shape[0], p_ref.shape[0])
        mine = _half_chunks(c, n_rows, 16)
        other = _half_chunks(1 - c, n_rows, 16)

        bulk = [[_remote(srcs[a].at[rows], dsts[a].at[q, rows], w_send.at[j * N_BULK + i], w_recv.at[j * N_BULK + i],
                         (*chips[j], c)) for i, (a, rows) in enumerate(mine)] for j in range(3)]
        fall_ref[q] = fp_ref[...]
        small = [_remote(fp_ref, fall_ref.at[q], f_send.at[j], f_recv.at[j], (*chips[j], c)) for j in range(3)]
        my_rows = pl.ds(pl.multiple_of(8 * dev, 8), 8)
        call_ref[my_rows, :] = c_ref[...]
        cond = [_remote(c_ref, call_ref.at[my_rows, :], c_send.at[r], c_recv.at[r], peer)
                for r, peer in enumerate(_all_peers(x, y, c))]
        for cp in sum(bulk, []) + small + cond:
            cp.start()
        for cp in cond:
            cp.wait_recv()

        abuf[pl.ds(0, 64), :] = _silu(call_ref[...])
        abuf[pl.ds(64, 8), :] = _silu(cctx_ref[...])
        mall_ref[q] = _nn(abuf[...], adaw_ref[...], HI) + adab_ref[...]
        mod = [_remote(mall_ref.at[q], mall_ref.at[q], m_send.at[j], m_recv.at[j], (*chips[j], c)) for j in range(3)]
        for cp in mod:
            cp.start()

        handed = []
        for j in range(3):
            for i, (a, rows) in enumerate(mine):
                bulk[j][i].wait_recv()
                cp = _remote(dsts[a].at[qs[j], rows], dsts[a].at[qs[j], rows],
                             h_send.at[j * N_BULK + i], h_recv.at[j * N_BULK + i], sib)
                cp.start()
                handed.append(cp)
        for j in range(3):
            for i, (a, rows) in enumerate(other):
                _remote(dsts[a].at[qs[j], rows], dsts[a].at[qs[j], rows],
                        h_send.at[j * N_BULK + i], h_recv.at[j * N_BULK + i], sib).wait_recv()
        for cp in mod + small:
            cp.wait_recv()
        for cp in sum(bulk, []) + small + cond + mod + handed:
            cp.wait_send()

    def dma(n):
        return pltpu.SemaphoreType.DMA((n,))

    return _pallas(
        body, name="gather_weights",
        in_specs=[VMEM, VMEM, VMEM, VMEM, ANY, ANY, VMEM],
        out_specs=[ANY, ANY, VMEM, VMEM, VMEM],
        out_shape=[jax.ShapeDtypeStruct((N_CHIPS,) + w_sh.shape, BF16), jax.ShapeDtypeStruct((N_CHIPS,) + p_sh.shape, BF16),
                   jax.ShapeDtypeStruct((N_CHIPS, F_ROWS, D), F32),
                   jax.ShapeDtypeStruct((8 * N_DEV, D), F32), jax.ShapeDtypeStruct((N_CHIPS, MOD_ROWS, nsh), F32)],
        scratch_shapes=[pltpu.VMEM((MOD_ROWS, D), F32), dma(3 * N_BULK), dma(3 * N_BULK), dma(3 * N_BULK), dma(3 * N_BULK),
                        dma(7), dma(7), dma(3), dma(3), dma(3), dma(3)],
        compiler_params=_params(),
    )(c8, cctx8, adaw, adab, w_sh, p_sh, fp)


def pair_swap(gw, gp):
    n_pair = N_CHIPS * N_BULK

    def body(gw_ref, gp_ref, gotw_ref, gotp_ref, a_send, a_recv):
        x, y, c, _ = _place()
        srcs, dsts = (gw_ref, gp_ref), (gotw_ref, gotp_ref)
        n_rows = (gw_ref.shape[1], gp_ref.shape[1])
        pair = []
        for i, ((a, rows_o), (_, rows_0)) in enumerate(zip(_half_chunks(1 - c, n_rows, 8), _half_chunks(0, n_rows, 8))):
            for s in range(N_CHIPS):
                k = s * N_BULK + i
                pair.append(_remote(srcs[a].at[s, rows_o], dsts[a].at[s, rows_0], a_send.at[k], a_recv.at[k], (x, y, 1 - c)))
        for cp in pair:
            cp.start()
        for cp in pair:
            cp.wait_recv()
        for cp in pair:
            cp.wait_send()

    return _pallas(
        body, name="pair_swap", in_specs=[ANY, ANY], out_specs=[ANY, ANY],
        out_shape=[jax.ShapeDtypeStruct((N_CHIPS, gw.shape[1] // 2, gw.shape[2]), F32),
                   jax.ShapeDtypeStruct((N_CHIPS, gp.shape[1] // 2, gp.shape[2]), F32)],
        scratch_shapes=[pltpu.SemaphoreType.DMA((n_pair,)), pltpu.SemaphoreType.DMA((n_pair,))],
        compiler_params=_params(),
    )(gw, gp)


def gather_small(sm):
    rows = sm.shape[0]

    def body(sm_ref, sall_ref, s_send, s_recv):
        x, y, c, _ = _place()
        dev = 4 * x + 2 * y + c
        sall_ref[dev] = sm_ref[...]
        small = [_remote(sm_ref, sall_ref.at[dev], s_send.at[r], s_recv.at[r], peer)
                 for r, peer in enumerate(_all_peers(x, y, c))]
        for cp in small:
            cp.start()
        for cp in small:
            cp.wait_recv()
        for cp in small:
            cp.wait_send()

    return _pallas(
        body, name="gather_small", in_specs=[VMEM], out_specs=VMEM,
        out_shape=jax.ShapeDtypeStruct((N_DEV, rows, D), F32),
        scratch_shapes=[pltpu.SemaphoreType.DMA((7,)), pltpu.SemaphoreType.DMA((7,))],
        compiler_params=_params(),
    )(sm)


def pair_share(ghw, ghp, pq):
    def body(ghw_ref, ghp_ref, pq_ref, outw_ref, outp_ref, pqa_ref, send, recv, p_send, p_recv):
        del ghw_ref, ghp_ref
        x, y, c, chips = _place()
        q = 2 * x + y
        refs = (outw_ref, outp_ref)
        n_rows = (2 * outw_ref.shape[1], 2 * outp_ref.shape[1])
        pair = [_remote(refs[a].at[c, rows], refs[a].at[c, rows], send.at[i], recv.at[i], (x, y, 1 - c))
                for i, (a, rows) in enumerate(_half_chunks(0, n_rows, 8))]
        pqa_ref[q] = pq_ref[...]
        small = [_remote(pq_ref, pqa_ref.at[q], p_send.at[j], p_recv.at[j], (*chips[j], c)) for j in range(3)]
        for cp in pair + small:
            cp.start()
        for i, (a, rows) in enumerate(_half_chunks(0, n_rows, 8)):
            _remote(refs[a].at[1 - c, rows], refs[a].at[1 - c, rows], send.at[i], recv.at[i], (x, y, 1 - c)).wait_recv()
        for cp in small:
            cp.wait_recv()
        for cp in pair + small:
            cp.wait_send()

    return _pallas(
        body, name="pair_share", in_specs=[ANY, ANY, VMEM], out_specs=[ANY, ANY, VMEM],
        out_shape=[jax.ShapeDtypeStruct(ghw.shape, F32), jax.ShapeDtypeStruct(ghp.shape, F32),
                   jax.ShapeDtypeStruct((N_CHIPS, 8, D), F32)],
        scratch_shapes=[pltpu.SemaphoreType.DMA((N_BULK,)), pltpu.SemaphoreType.DMA((N_BULK,)),
                        pltpu.SemaphoreType.DMA((3,)), pltpu.SemaphoreType.DMA((3,))],
        input_output_aliases={0: 0, 1: 1},
        compiler_params=_params(),
    )(ghw, ghp, pq)


def _rows_of(shape):
    size = 1
    for s in shape:
        size *= s
    return -(-size // D)


def _pack(arrs, rows_multiple=8):
    parts = []
    total = 0
    for a in arrs:
        f = a.reshape(-1).astype(F32)
        r = _rows_of(a.shape)
        parts.append(jnp.pad(f, (0, r * D - f.shape[0])))
        total += r
    pad_rows = (-total) % rows_multiple
    if pad_rows:
        parts.append(jnp.zeros((pad_rows * D,), F32))
    return jnp.concatenate(parts).reshape(-1, D)


def _unpack(p, shapes):
    out = []
    r0 = 0
    for shp in shapes:
        r = _rows_of(shp)
        size = 1
        for s in shp:
            size *= s
        out.append(p[r0:r0 + r].reshape(-1)[:size].reshape(shp))
        r0 += r
    return out


WEIGHT_NAMES = ['c_ctx', 'ada_w', 'ada_b', 'norm_g', 'w_in', 'b_in', 'conv_w', 'conv_b', 'conv_ln_g', 'conv_ln_b',
                'conv_proj', 'decay_up_fwd', 'decay_bias_fwd', 'decay_up_bwd', 'decay_bias_bwd', 'gla_norm_g', 'gla_proj',
                'w_out', 'final_norm_g']
SMALL_NAMES = ['c_ctx', 'ada_b', 'norm_g', 'b_in', 'conv_w', 'conv_b', 'conv_ln_g', 'conv_ln_b', 'decay_up_fwd',
               'decay_bias_fwd', 'decay_up_bwd', 'decay_bias_bwd', 'gla_norm_g', 'final_norm_g']


def kernel(x, c, ctx, c_ctx, ada_w, ada_b, norm_g, w_in, b_in, conv_w, conv_b, conv_ln_g, conv_ln_b, conv_proj, decay_up_fwd, decay_bias_fwd, decay_up_bwd, decay_bias_bwd, gla_norm_g, gla_proj, w_out, final_norm_g, loss_target, m_c_ctx, m_ada_w, m_ada_b, m_norm_g, m_w_in, m_b_in, m_conv_w, m_conv_b, m_conv_ln_g, m_conv_ln_b, m_conv_proj, m_decay_up_fwd, m_decay_bias_fwd, m_decay_up_bwd, m_decay_bias_bwd, m_gla_norm_g, m_gla_proj, m_w_out, m_final_norm_g, v_c_ctx, v_ada_w, v_ada_b, v_norm_g, v_w_in, v_b_in, v_conv_w, v_conv_b, v_conv_ln_g, v_conv_ln_b, v_conv_proj, v_decay_up_fwd, v_decay_bias_fwd, v_decay_up_bwd, v_decay_bias_bwd, v_gla_norm_g, v_gla_proj, v_w_out, v_final_norm_g):
    w = dict(c_ctx=c_ctx, ada_w=ada_w, ada_b=ada_b, norm_g=norm_g, w_in=w_in, b_in=b_in, conv_w=conv_w, conv_b=conv_b,
             conv_ln_g=conv_ln_g, conv_ln_b=conv_ln_b, conv_proj=conv_proj, decay_up_fwd=decay_up_fwd,
             decay_bias_fwd=decay_bias_fwd, decay_up_bwd=decay_up_bwd, decay_bias_bwd=decay_bias_bwd,
             gla_norm_g=gla_norm_g, gla_proj=gla_proj, w_out=w_out, final_norm_g=final_norm_g)
    m = dict(c_ctx=m_c_ctx, ada_w=m_ada_w, ada_b=m_ada_b, norm_g=m_norm_g, w_in=m_w_in, b_in=m_b_in, conv_w=m_conv_w,
             conv_b=m_conv_b, conv_ln_g=m_conv_ln_g, conv_ln_b=m_conv_ln_b, conv_proj=m_conv_proj,
             decay_up_fwd=m_decay_up_fwd, decay_bias_fwd=m_decay_bias_fwd, decay_up_bwd=m_decay_up_bwd,
             decay_bias_bwd=m_decay_bias_bwd, gla_norm_g=m_gla_norm_g, gla_proj=m_gla_proj, w_out=m_w_out,
             final_norm_g=m_final_norm_g)
    v = dict(c_ctx=v_c_ctx, ada_w=v_ada_w, ada_b=v_ada_b, norm_g=v_norm_g, w_in=v_w_in, b_in=v_b_in, conv_w=v_conv_w,
             conv_b=v_conv_b, conv_ln_g=v_conv_ln_g, conv_ln_b=v_conv_ln_b, conv_proj=v_conv_proj,
             decay_up_fwd=v_decay_up_fwd, decay_bias_fwd=v_decay_bias_fwd, decay_up_bwd=v_decay_up_bwd,
             decay_bias_bwd=v_decay_bias_bwd, gla_norm_g=v_gla_norm_g, gla_proj=v_gla_proj, w_out=v_w_out,
             final_norm_g=v_final_norm_g)
    n = x.shape[0]
    ax, ay, ac = lax.axis_index("x"), lax.axis_index("y"), lax.axis_index("c")
    q = 2 * ax + ay
    dev = 4 * ax + 2 * ay + ac
    nsh = ada_w.shape[2]

    w_sh = w_in[0].astype(BF16)
    p_sh = jnp.concatenate([conv_proj[0], gla_proj[0], w_out[0]], 0).astype(BF16)
    fp = _pack([conv_w[0], decay_up_fwd[0], decay_up_bwd[0]], F_ROWS)
    c8 = jnp.pad(c, ((0, 8 - n), (0, 0)))
    cctx8 = jnp.pad(c_ctx[None], ((0, 7), (0, 0)))
    adab_sh = lax.dynamic_slice(ada_b, (0, q * nsh), (1, nsh))
    w_all, p_all, fall, call, mall = gather_weights(c8, cctx8, ada_w[0], adab_sh, w_sh, p_sh, fp)

    mod_all = jnp.transpose(mall, (1, 0, 2)).reshape(MOD_ROWS, 3 * D)
    mod_mine = lax.dynamic_slice(mod_all, (8 * dev, 0), (n, 3 * D))
    mod_ctx = mod_all[64:65]
    shift = jnp.concatenate([mod_mine[:, 0:D], mod_ctx[:, 0:D]], 0)[:, None, :]
    scale1 = 1.0 + jnp.concatenate([mod_mine[:, D:2 * D], mod_ctx[:, D:2 * D]], 0)[:, None, :]
    gate = mod_mine[:, 2 * D:3 * D][:, None, :]

    own = lambda i, mine, got: jnp.where(q == i, mine, got)
    g1, g2, g3, g4, g5 = _group_cols(jnp.concatenate([own(i, w_sh, w_all[i]) for i in range(N_CHIPS)], 1))
    p_full = jnp.stack([own(i, p_sh, p_all[i]) for i in range(N_CHIPS)])
    wts = dict(w1=g1, w2=g2, w3=g3, w4=g4, w5=g5,
               conv_proj=p_full[:, 0:256].reshape(D, D), gla_proj=p_full[:, 256:512].reshape(D, D),
               w_out=p_full[:, 512:768].reshape(D, D))
    f_parts = [_unpack(fall[i], [conv_w.shape[1:], decay_up_fwd.shape[1:], decay_up_bwd.shape[1:]]) for i in range(N_CHIPS)]
    conv_w_full = jnp.concatenate([p[0] for p in f_parts], 1)
    upf_full = jnp.concatenate([p[1] for p in f_parts], 1)
    upb_full = jnp.concatenate([p[2] for p in f_parts], 1)
    b1, b2, b3, b4, b5 = _group_cols(b_in)
    small = dict(b1=b1, b2=b2, b3=b3, b4=b4, b5=b5, norm_g=norm_g,
                 conv_w=jnp.pad(conv_w_full, ((0, 1), (0, 0))), conv_b=conv_b, conv_ln_g=conv_ln_g, conv_ln_b=conv_ln_b,
                 upf=_pad_up(upf_full, 0), upb=_pad_up(upb_full, 16), bias_f=decay_bias_fwd, bias_b=decay_bias_bwd,
                 gla_norm_g=gla_norm_g, final_norm_g=final_norm_g[None])

    loss_part, dh, dps, g = local_step(x, ctx, loss_target, (scale1, shift, gate), wts, small)
    loss = lax.psum(loss_part[0, 0], ("x", "y", "c"))

    gw = _ungroup_to_shards([g["w%d" % i] for i in range(1, 6)])
    gp = jnp.concatenate([g["conv_proj"].reshape(N_CHIPS, 256, D), g["gla_proj"].reshape(N_CHIPS, 256, D),
                          g["w_out"].reshape(N_CHIPS, 256, D)], 1)
    gotw, gotp = pair_swap(gw, gp)
    core = ac.astype(jnp.int32).reshape(1)
    chip = q.astype(jnp.int32).reshape(1)
    paw, paw16 = pair_add(core, gw, gotw, name="pair_add_w", tr=128)
    pap, pap16 = pair_add(core, gp, gotp, name="pair_add_p", tr=384)
    du, rbw, rbp = dgrad_multi(dps, [wts["w%d" % i] for i in range(1, 6)], paw16, pap16,
                               t_all=n * SEQ_ALL, t_lat=n * SEQ, tm=512)
    grad_x2, dshift, dscale, g["norm_g"] = norm_mod_bwd(x.reshape(n * SEQ, D), ctx.reshape(n * NCTX, D), du, dh,
                                                        scale1, norm_g)

    dm_mine = jnp.concatenate([dshift[:n, 0], dscale[:n, 0], g["gate"][:, 0]], -1)
    dm_ctx = jnp.concatenate([dshift[n, 0], dscale[n, 0], jnp.zeros((D,), F32)], -1)
    d_b_in = _ungroup_cols(*[g["b%d" % i] for i in range(1, 6)])
    small_grads = [d_b_in, g["norm_g"], g["conv_b"].sum(0), g["conv_ln_g"], g["conv_ln_b"], g["bias_f"], g["bias_b"],
                   g["gla_norm_g"], g["final_norm_g"], g["conv_w"].sum(0)[:CONV_K], g["upf"][0:16], g["upb"][16:32],
                   dm_mine, dm_ctx]
    small_shapes = [a.shape for a in small_grads]
    sall = gather_small(_pack(small_grads))
    gsum = sum_devices(sall)
    (s_b_in, s_norm_g, s_conv_b, s_ln_g, s_ln_b, s_bias_f, s_bias_b, s_gla_g, s_final_g, s_conv_w, s_upf, s_upb,
     _, _) = _unpack(gsum, small_shapes)
    dm_rows = [_unpack(sall[i], small_shapes)[-2:] for i in range(N_DEV)]
    dm_full = jnp.concatenate(
        [jnp.pad(jnp.stack([r[0] for r in dm_rows]), ((0, 0), (0, 8 - n), (0, 0))).reshape(8 * N_DEV, 3 * D),
         jnp.stack([r[1] for r in dm_rows])], 0)
    dm_shard = lax.dynamic_slice(dm_full, (0, q * nsh), (MOD_ROWS, nsh))
    cctx_rows = jnp.broadcast_to(c_ctx[None], (8, D))
    g_ada_w, g_ada_b, pq = ada_bwd(call, cctx_rows, dm_shard, dm_full, ada_w[0])

    place = jnp.concatenate([chip, core])
    ghw = chip_add(place, paw, rbw, name="chip_add_w", tr=128)
    ghp = chip_add(place, pap, rbp, name="chip_add_p", tr=384)
    gw_mine, gp_mine, pq_all = pair_share(ghw, ghp, pq)
    gp_mine = gp_mine.reshape(768, D)
    g_c_ctx = cctx_grad(pq_all, cctx_rows)[0]

    grads = dict(
        c_ctx=g_c_ctx, ada_w=g_ada_w[None], ada_b=g_ada_b, norm_g=s_norm_g,
        w_in=gw_mine.reshape(1, D, W_IN_SHARD), b_in=s_b_in,
        conv_w=lax.dynamic_slice(s_conv_w, (0, q * 256), (CONV_K, 256))[None], conv_b=s_conv_b,
        conv_ln_g=s_ln_g, conv_ln_b=s_ln_b, conv_proj=gp_mine[0:256][None],
        decay_up_fwd=lax.dynamic_slice(s_upf, (0, q * 128), (16, 128))[None], decay_bias_fwd=s_bias_f,
        decay_up_bwd=lax.dynamic_slice(s_upb, (0, q * 128), (16, 128))[None], decay_bias_bwd=s_bias_b,
        gla_norm_g=s_gla_g, gla_proj=gp_mine[256:512][None], w_out=gp_mine[512:768][None],
        final_norm_g=s_final_g[0])

    delta, new_m, new_v = {}, {}, {}
    for name, tr in [("ada_w", 128), ("w_in", 128), ("conv_proj", 128), ("gla_proj", 128), ("w_out", 128)]:
        shp = w[name].shape
        two = lambda a: a.reshape(shp[-2], shp[-1])
        d_, m_, v_ = adamw2d(two(w[name]), two(grads[name]), two(m[name]), two(v[name]), name="adamw_" + name, tr=tr)
        delta[name], new_m[name], new_v[name] = d_.reshape(shp), m_.reshape(shp), v_.reshape(shp)
    shapes = [w[nm].shape for nm in SMALL_NAMES]
    packs = [_pack([src[nm] for nm in SMALL_NAMES]) for src in (w, grads, m, v)]
    d_, m_, v_ = adamw2d(*packs, name="adamw_small", tr=packs[0].shape[0])
    for nm, a, b, cc in zip(SMALL_NAMES, _unpack(d_, shapes), _unpack(m_, shapes), _unpack(v_, shapes)):
        delta[nm], new_m[nm], new_v[nm] = a, b, cc

    grad_x = grad_x2.reshape(x.shape)
    return (loss, grad_x, *[grads[nm].reshape(w[nm].shape) for nm in WEIGHT_NAMES], *[delta[nm] for nm in WEIGHT_NAMES],
            *[new_m[nm] for nm in WEIGHT_NAMES], *[new_v[nm] for nm in WEIGHT_NAMES])
```

```python
import jax
import jax.numpy as jnp
from jax import lax
from jax.experimental import pallas as pl
from jax.experimental.pallas import tpu as pltpu

F32 = jnp.float32
BF16 = jnp.bfloat16
MESH = pl.DeviceIdType.MESH
HI = lax.Precision.HIGHEST

D = 1024
SEQ = 2048
GRID_W = 64
GRID_H = SEQ // GRID_W
NCTX = 256
SEQ_ALL = SEQ + NCTX
EPS = 1e-6
CONV_K = 31
CONV_PAD = CONV_K // 2
HEADS = 4
HEAD_K = 128
HEAD_V = 256
GLA_DK = HEADS * HEAD_K
GATE_TAU = 16.0
Q_SCALE = HEAD_K ** -0.5
CHUNK = 64
NCHUNK = SEQ_ALL // CHUNK
NCHUNK_LAT = SEQ // CHUNK
NCHUNK_CTX = NCHUNK - NCHUNK_LAT
SUB = 64
NSUB = CHUNK // SUB
N_IN = 8224
W3 = 2176
O3_V, O3_Q, O3_K, O3_AB = 0, 1024, 1536, 2048

ADAM_LR, ADAM_B1, ADAM_B2, ADAM_EPS, ADAM_WD, ADAM_STEP = 0.001, 0.9, 0.999, 1e-08, 0.01, 10
VMEM_LIMIT = 56 * 1024 * 1024

N_CHIPS = 4
N_DEV = 8
W_IN_SHARD = N_IN // N_CHIPS
MOD_ROWS = 72


def _pallas(body, **kw):
    return pl.pallas_call(body, **kw)


def _params(sem=None, **kw):
    if sem is not None:
        kw["dimension_semantics"] = sem
    return pltpu.CompilerParams(vmem_limit_bytes=VMEM_LIMIT, **kw)


def _sigmoid(v):
    return 1.0 / (1.0 + jnp.exp(-v))


def _silu(v):
    return v * _sigmoid(v)


def _dsilu(v):
    s = _sigmoid(v)
    return s * (1.0 + v * (1.0 - s))


def _log_sigmoid(v):
    return jnp.minimum(v, 0.0) - jnp.log(1.0 + jnp.exp(-jnp.abs(v)))


def _dot(a, b, dims, precision=None):
    return lax.dot_general(a, b, (dims, ((), ())), preferred_element_type=F32, precision=precision)


def _nn(a, b, precision=None):
    return _dot(a, b, ((1,), (0,)), precision)


def _nt(a, b, precision=None):
    return _dot(a, b, ((1,), (1,)), precision)


def _tn(a, b, precision=None):
    return _dot(a, b, ((0,), (0,)), precision)


def _b16(v):
    return v.astype(BF16)


def matmul_nn(a, b, bias, *, name, m, tm, tn, out_dtype):
    k = a.shape[1]
    n = b.shape[1]
    has_bias = bias is not None

    def body(*refs):
        if has_bias:
            a_ref, b_ref, bias_ref, o_ref = refs
            acc = _nn(a_ref[...], b_ref[...]) + bias_ref[...]
        else:
            a_ref, b_ref, o_ref = refs
            acc = _nn(a_ref[...], b_ref[...])
        o_ref[...] = acc.astype(o_ref.dtype)

    in_specs = [pl.BlockSpec((tm, k), lambda j, i: (i, 0)), pl.BlockSpec((k, tn), lambda j, i: (0, j))]
    args = [a, b]
    if has_bias:
        in_specs.append(pl.BlockSpec((1, tn), lambda j, i: (0, j)))
        args.append(bias)
    return _pallas(
        body, name=name, grid=(n // tn, m // tm), in_specs=in_specs,
        out_specs=pl.BlockSpec((tm, tn), lambda j, i: (i, j)),
        out_shape=jax.ShapeDtypeStruct((m, n), out_dtype),
        compiler_params=_params(("parallel", "parallel")),
    )(*args)


def matmul_nt(a, b, *, name, tm, out_dtype):
    m, k = a.shape
    n = b.shape[0]

    def body(a_ref, b_ref, o_ref):
        o_ref[...] = _nt(a_ref[...], b_ref[...]).astype(o_ref.dtype)

    return _pallas(
        body, name=name, grid=(m // tm,),
        in_specs=[pl.BlockSpec((tm, k), lambda i: (i, 0)), pl.BlockSpec((n, k), lambda i: (0, 0))],
        out_specs=pl.BlockSpec((tm, n), lambda i: (i, 0)),
        out_shape=jax.ShapeDtypeStruct((m, n), out_dtype),
        compiler_params=_params(("parallel",)),
    )(a, b)


def matmul_tn(a, b, *, name, t, tn, tt, colsum=False):
    m = a.shape[1]
    n = b.shape[1]

    def body(a_ref, b_ref, o_ref, *rest):
        @pl.when(pl.program_id(1) == 0)
        def _():
            o_ref[...] = jnp.zeros_like(o_ref)
            if colsum:
                rest[0][...] = jnp.zeros_like(rest[0])
        o_ref[...] += _tn(a_ref[...], b_ref[...])
        if colsum:
            rest[0][...] += jnp.sum(b_ref[...].astype(F32), axis=0, keepdims=True)

    out_specs = [pl.BlockSpec((m, tn), lambda j, s: (0, j))]
    out_shape = [jax.ShapeDtypeStruct((m, n), F32)]
    if colsum:
        out_specs.append(pl.BlockSpec((1, tn), lambda j, s: (0, j)))
        out_shape.append(jax.ShapeDtypeStruct((1, n), F32))
    return _pallas(
        body, name=name, grid=(n // tn, t // tt),
        in_specs=[pl.BlockSpec((tt, m), lambda j, s: (s, 0)), pl.BlockSpec((tt, tn), lambda j, s: (s, j))],
        out_specs=out_specs, out_shape=out_shape,
        compiler_params=_params(("parallel", "arbitrary")),
    )(a, b)


def dgrad_multi(dps, wts, paw, pap, sm, *, t_all, t_lat, tm):
    n_lat = t_lat // tm
    n_g = len(dps)
    n_tiles = t_all // tm
    whole = [g for g in range(n_g) if dps[g].shape[0] == t_all]
    latent = [g for g in range(n_g) if dps[g].shape[0] != t_all]

    def body(*refs):
        dp_refs, w_refs = refs[:n_g], refs[n_g:2 * n_g]
        paw_ref, pap_ref, sm_ref, o_ref, rbw_ref, rbp_ref, sall_ref, b_send, b_recv, s_send, s_recv, l_sem = refs[2 * n_g:]
        i = pl.program_id(0)

        def exchange():
            x, y, c, chips = _place()
            dev = 4 * x + 2 * y + c
            srcs, dsts = (paw_ref, pap_ref), (rbw_ref, rbp_ref)
            n_rows = (2 * paw_ref.shape[1], 2 * pap_ref.shape[1])
            big = [_remote(srcs[a].at[2 * cx + cy, rows], dsts[a].at[j, rows],
                           b_send.at[j * N_BULK + k], b_recv.at[j * N_BULK + k], (cx, cy, c))
                   for j, (cx, cy) in enumerate(chips) for k, (a, rows) in enumerate(_half_chunks(0, n_rows, 16))]
            small = [_remote(sm_ref, sall_ref.at[dev], s_send.at[r], s_recv.at[r], peer)
                     for r, peer in enumerate(_all_peers(x, y, c))]
            return big + small, pltpu.make_async_copy(sm_ref, sall_ref.at[dev], l_sem)

        @pl.when(i == 0)
        def _():
            remote, own = exchange()
            own.start()
            for cp in remote:
                cp.start()

        acc = None
        for g in whole:
            part = _nt(dp_refs[g][...], w_refs[g][...])
            acc = part if acc is None else acc + part
        o_ref[...] = acc

        @pl.when(i < n_lat)
        def _():
            lat = None
            for g in latent:
                part = _nt(dp_refs[g][...], w_refs[g][...])
                lat = part if lat is None else lat + part
            o_ref[...] += lat

        @pl.when(i == n_tiles - 1)
        def _():
            remote, own = exchange()
            for cp in remote:
                cp.wait_recv()
            for cp in remote:
                cp.wait_send()
            own.wait()

    in_specs = []
    for g, dp in enumerate(dps):
        nrow = dp.shape[0] // tm
        in_specs.append(pl.BlockSpec((tm, dp.shape[1]), lambda i, nrow=nrow: (jnp.minimum(i, nrow - 1), 0)))
    for w in wts:
        in_specs.append(pl.BlockSpec(w.shape, lambda i: (0, 0), pipeline_mode=pl.Buffered(1)))
    any_spec = pl.BlockSpec(memory_space=pl.ANY)
    return _pallas(
        body, name="dgrad_w_in", grid=(n_tiles,), in_specs=in_specs + [any_spec, any_spec, any_spec],
        out_specs=[pl.BlockSpec((tm, D), lambda i: (i, 0)), any_spec, any_spec, any_spec],
        out_shape=[jax.ShapeDtypeStruct((t_all, D), F32), jax.ShapeDtypeStruct((3,) + paw.shape[1:], paw.dtype),
                   jax.ShapeDtypeStruct((3,) + pap.shape[1:], pap.dtype),
                   jax.ShapeDtypeStruct((N_DEV,) + sm.shape, sm.dtype)],
        scratch_shapes=[pltpu.SemaphoreType.DMA((3 * N_BULK,)), pltpu.SemaphoreType.DMA((3 * N_BULK,)),
                        pltpu.SemaphoreType.DMA((N_DEV - 1,)), pltpu.SemaphoreType.DMA((N_DEV - 1,)),
                        pltpu.SemaphoreType.DMA(())],
        compiler_params=_params(("arbitrary",)),
    )(*dps, *wts, paw, pap, sm)


TM_NORM = 512


def norm_mod_fwd(x2, ctx2, scale1, shift, norm_g):
    t = x2.shape[0]
    n_lat = t // TM_NORM
    assert ctx2.shape[0] == TM_NORM
    n_samples = scale1.shape[0] - 1
    tps = n_lat // n_samples

    def body(x_ref, c_ref, sc_ref, sh_ref, g_ref, u_ref):
        i = pl.program_id(0)
        xv = jnp.where(i < n_lat, x_ref[...], c_ref[...])
        rs = lax.rsqrt(jnp.mean(xv * xv, axis=-1, keepdims=True) + EPS)
        u = xv * rs * g_ref[...] * sc_ref[0] + sh_ref[0]
        u_ref[...] = u.astype(u_ref.dtype)

    grp = lambda i: (jnp.minimum(i // tps, n_samples), 0, 0)
    return _pallas(
        body, name="norm_mod_fwd", grid=(n_lat + 1,),
        in_specs=[pl.BlockSpec((TM_NORM, D), lambda i: (jnp.minimum(i, n_lat - 1), 0)),
                  pl.BlockSpec((TM_NORM, D), lambda i: (0, 0)),
                  pl.BlockSpec((1, 1, D), grp), pl.BlockSpec((1, 1, D), grp),
                  pl.BlockSpec((1, D), lambda i: (0, 0))],
        out_specs=pl.BlockSpec((TM_NORM, D), lambda i: (i, 0)),
        out_shape=jax.ShapeDtypeStruct((t + TM_NORM, D), BF16),
        compiler_params=_params(("parallel",)),
    )(x2, ctx2, scale1, shift, norm_g)


def norm_mod_bwd(x2, ctx2, du, dh, scale1, norm_g):
    t = x2.shape[0]
    n_lat = t // TM_NORM
    n_samples = scale1.shape[0] - 1
    tps = n_lat // n_samples
    n_grp = n_samples + 1

    def body(x_ref, c_ref, du_ref, dh_ref, sc_ref, g_ref, dx_ref, dsh_ref, dsc_ref, dg_ref):
        i = pl.program_id(0)
        xv = jnp.where(i < n_lat, x_ref[...], c_ref[...])
        rs = lax.rsqrt(jnp.mean(xv * xv, axis=-1, keepdims=True) + EPS)
        xh = xv * rs
        duv = du_ref[...]
        n = xh * g_ref[...]
        dn = duv * sc_ref[0]
        dxh = dn * g_ref[...]
        dx = rs * (dxh - xh * jnp.mean(dxh * xh, axis=-1, keepdims=True))
        @pl.when(i < n_lat)
        def _():
            dx_ref[...] = dx + dh_ref[...]

        @pl.when(i % tps == 0)
        def _():
            dsh_ref[...] = jnp.zeros_like(dsh_ref)
            dsc_ref[...] = jnp.zeros_like(dsc_ref)

        @pl.when(i == 0)
        def _():
            dg_ref[...] = jnp.zeros_like(dg_ref)

        dsh_ref[0] += jnp.sum(duv, axis=0, keepdims=True)
        dsc_ref[0] += jnp.sum(duv * n, axis=0, keepdims=True)
        dg_ref[...] += jnp.sum(dn * xh, axis=0, keepdims=True)

    grp = lambda i: (jnp.minimum(i // tps, n_samples), 0, 0)
    lat = lambda i: (jnp.minimum(i, n_lat - 1), 0)
    return _pallas(
        body, name="norm_mod_bwd", grid=(n_lat + 1,),
        in_specs=[pl.BlockSpec((TM_NORM, D), lat),
                  pl.BlockSpec((TM_NORM, D), lambda i: (0, 0)),
                  pl.BlockSpec((TM_NORM, D), lambda i: (i, 0)),
                  pl.BlockSpec((TM_NORM, D), lat),
                  pl.BlockSpec((1, 1, D), grp),
                  pl.BlockSpec((1, D), lambda i: (0, 0))],
        out_specs=[pl.BlockSpec((TM_NORM, D), lat),
                   pl.BlockSpec((1, 1, D), grp), pl.BlockSpec((1, 1, D), grp),
                   pl.BlockSpec((1, D), lambda i: (0, 0))],
        out_shape=[jax.ShapeDtypeStruct((t, D), F32),
                   jax.ShapeDtypeStruct((n_grp, 1, D), F32), jax.ShapeDtypeStruct((n_grp, 1, D), F32),
                   jax.ShapeDtypeStruct((1, D), F32)],
        compiler_params=_params(("arbitrary",)),
    )(x2, ctx2, du, dh, scale1, norm_g)


CONV_CB = 256
CONV_NCB = D // CONV_CB
H_OFF = 16


def _conv_pad_shape(vertical):
    if vertical:
        return (GRID_H + 2 * CONV_PAD, GRID_W, CONV_CB)
    return (GRID_H, GRID_W + 2 * H_OFF, CONV_CB)


def _conv_store(pad_ref, img, vertical):
    if vertical:
        pad_ref[pl.ds(CONV_PAD, GRID_H)] = img
    else:
        pad_ref[:, pl.ds(H_OFF, GRID_W), :] = img


def _conv_window(pad_ref, k, vertical, r):
    if vertical:
        return pad_ref[r + k]
    return pad_ref[r, pl.ds(H_OFF - CONV_PAD + k, GRID_W), :]


def _rows(r):
    return pl.ds(pl.multiple_of(r * GRID_W, GRID_W), GRID_W)


def conv_fwd(p1, conv_w, conv_b, n_samples):
    t = n_samples * SEQ

    def make(vertical, prev):
        def body(gv_ref, gg_ref, w_ref, b_ref, *rest):
            o_ref, pad_ref = rest[-2], rest[-1]
            pad_ref[...] = jnp.zeros_like(pad_ref)
            a = gv_ref[...].astype(F32) * _sigmoid(gg_ref[...].astype(F32))
            _conv_store(pad_ref, a.reshape(GRID_H, GRID_W, CONV_CB), vertical)

            def row(r, carry):
                acc = jnp.zeros((GRID_W, CONV_CB), F32) + b_ref[...]
                for k in range(CONV_K):
                    acc = acc + _conv_window(pad_ref, k, vertical, r) * w_ref[pl.ds(k, 1), :]
                o_ref[_rows(r), :] = acc
                return carry

            lax.fori_loop(0, GRID_H, row, 0)

        cb0 = CONV_NCB // 2 if vertical else 0
        in_specs = [pl.BlockSpec((SEQ, CONV_CB), lambda b, j: (b, 2 * (cb0 + j))),
                    pl.BlockSpec((SEQ, CONV_CB), lambda b, j: (b, 2 * (cb0 + j) + 1)),
                    pl.BlockSpec((CONV_K + 1, CONV_CB), lambda b, j: (0, cb0 + j)),
                    pl.BlockSpec((1, CONV_CB), lambda b, j: (0, cb0 + j))]
        args = [p1, p1, conv_w, conv_b]
        aliases = {}
        if prev is not None:
            in_specs.append(pl.BlockSpec(memory_space=pl.ANY))
            args.append(prev)
            aliases = {4: 0}
        return _pallas(
            body, name="conv_fwd_v" if vertical else "conv_fwd_h", grid=(n_samples, CONV_NCB // 2),
            in_specs=in_specs,
            out_specs=pl.BlockSpec((SEQ, CONV_CB), lambda b, j: (b, cb0 + j)),
            out_shape=jax.ShapeDtypeStruct((t, D), F32),
            scratch_shapes=[pltpu.VMEM(_conv_pad_shape(vertical), F32)],
            input_output_aliases=aliases,
            compiler_params=_params(("parallel", "parallel")),
        )(*args)

    return make(True, make(False, None))


def conv_bwd(p1, daconv, conv_w, n_samples):
    t = n_samples * SEQ

    def make(vertical, prev):
        def body(gv_ref, gg_ref, dy_ref, w_ref, *rest):
            dp_ref, dw_ref, db_ref, pad_ref, dpad_ref, da_ref = rest[-6:]
            pad_ref[...] = jnp.zeros_like(pad_ref)
            dpad_ref[...] = jnp.zeros_like(dpad_ref)
            gv = gv_ref[...].astype(F32)
            sg = _sigmoid(gg_ref[...].astype(F32))
            _conv_store(pad_ref, (gv * sg).reshape(GRID_H, GRID_W, CONV_CB), vertical)
            _conv_store(dpad_ref, dy_ref[...].reshape(GRID_H, GRID_W, CONV_CB), vertical)

            def row(r, carry):
                acc = jnp.zeros((GRID_W, CONV_CB), F32)
                for k in range(CONV_K):
                    acc = acc + _conv_window(dpad_ref, CONV_K - 1 - k, vertical, r) * w_ref[pl.ds(k, 1), :]
                da_ref[_rows(r), :] = acc
                return carry

            lax.fori_loop(0, GRID_H, row, 0)
            da = da_ref[...]
            dp_ref[:, pl.ds(0, CONV_CB)] = (da * sg).astype(dp_ref.dtype)
            dp_ref[:, pl.ds(CONV_CB, CONV_CB)] = (da * gv * sg * (1.0 - sg)).astype(dp_ref.dtype)

            for k in range(CONV_K):
                def wrow(r, acc, k=k):
                    return acc + _conv_window(pad_ref, k, vertical, r) * dy_ref[_rows(r), :]
                acc = lax.fori_loop(0, GRID_H, wrow, jnp.zeros((GRID_W, CONV_CB), F32))
                dw_ref[0, pl.ds(k, 1), :] = jnp.sum(acc, axis=0, keepdims=True)
            dw_ref[0, pl.ds(CONV_K, 1), :] = jnp.zeros((1, CONV_CB), F32)
            db_ref[0] = jnp.sum(dy_ref[...], axis=0, keepdims=True)

        cb0 = CONV_NCB // 2 if vertical else 0
        in_specs = [pl.BlockSpec((SEQ, CONV_CB), lambda b, j: (b, 2 * (cb0 + j))),
                    pl.BlockSpec((SEQ, CONV_CB), lambda b, j: (b, 2 * (cb0 + j) + 1)),
                    pl.BlockSpec((SEQ, CONV_CB), lambda b, j: (b, cb0 + j)),
                    pl.BlockSpec((CONV_K + 1, CONV_CB), lambda b, j: (0, cb0 + j))]
        args = [p1, p1, daconv, conv_w]
        aliases = {}
        if prev is not None:
            in_specs += [pl.BlockSpec(memory_space=pl.ANY)] * 3
            args += list(prev)
            aliases = {4: 0, 5: 1, 6: 2}
        return _pallas(
            body, name="conv_bwd_v" if vertical else "conv_bwd_h", grid=(n_samples, CONV_NCB // 2),
            in_specs=in_specs,
            out_specs=[pl.BlockSpec((SEQ, 2 * CONV_CB), lambda b, j: (b, cb0 + j)),
                       pl.BlockSpec((1, CONV_K + 1, CONV_CB), lambda b, j: (b, 0, cb0 + j)),
                       pl.BlockSpec((1, 1, CONV_CB), lambda b, j: (b, 0, cb0 + j))],
            out_shape=[jax.ShapeDtypeStruct((t, 2 * D), BF16),
                       jax.ShapeDtypeStruct((n_samples, CONV_K + 1, D), F32),
                       jax.ShapeDtypeStruct((n_samples, 1, D), F32)],
            scratch_shapes=[pltpu.VMEM(_conv_pad_shape(vertical), F32), pltpu.VMEM(_conv_pad_shape(vertical), F32),
                            pltpu.VMEM((SEQ, CONV_CB), F32)],
            input_output_aliases=aliases,
            compiler_params=_params(("parallel", "parallel")),
        )(*args)

    return make(True, make(False, None))


TM_EW = 256


def ln_gate_fwd(aconv, z, ln_g, ln_b):
    t = aconv.shape[0]

    def body(a_ref, z_ref, g_ref, b_ref, o_ref):
        a = a_ref[...]
        mu = jnp.mean(a, axis=-1, keepdims=True)
        xc = a - mu
        rstd = lax.rsqrt(jnp.mean(xc * xc, axis=-1, keepdims=True) + EPS)
        l = xc * rstd * g_ref[...] + b_ref[...]
        o_ref[...] = (_silu(l) * _silu(z_ref[...].astype(F32))).astype(o_ref.dtype)

    row = pl.BlockSpec((TM_EW, D), lambda i: (i, 0))
    vec = pl.BlockSpec((1, D), lambda i: (0, 0))
    return _pallas(
        body, name="ln_gate_fwd", grid=(t // TM_EW,), in_specs=[row, row, vec, vec], out_specs=row,
        out_shape=jax.ShapeDtypeStruct((t, D), BF16), compiler_params=_params(("parallel",)),
    )(aconv, z, ln_g, ln_b)


def ln_gate_bwd(aconv, z, dac, ln_g, ln_b):
    t = aconv.shape[0]

    def body(a_ref, z_ref, d_ref, g_ref, b_ref, da_ref, dz_ref, dg_ref, db_ref):
        a = a_ref[...]
        zv = z_ref[...].astype(F32)
        dac_v = d_ref[...].astype(F32)
        mu = jnp.mean(a, axis=-1, keepdims=True)
        xc = a - mu
        rstd = lax.rsqrt(jnp.mean(xc * xc, axis=-1, keepdims=True) + EPS)
        xh = xc * rstd
        l = xh * g_ref[...] + b_ref[...]
        dz_ref[...] = (dac_v * _silu(l) * _dsilu(zv)).astype(dz_ref.dtype)
        dl = dac_v * _silu(zv) * _dsilu(l)
        dxh = dl * g_ref[...]
        da_ref[...] = rstd * (dxh - jnp.mean(dxh, axis=-1, keepdims=True)
                              - xh * jnp.mean(dxh * xh, axis=-1, keepdims=True))

        @pl.when(pl.program_id(0) == 0)
        def _():
            dg_ref[...] = jnp.zeros_like(dg_ref)
            db_ref[...] = jnp.zeros_like(db_ref)

        dg_ref[...] += jnp.sum(dl * xh, axis=0, keepdims=True)
        db_ref[...] += jnp.sum(dl, axis=0, keepdims=True)

    row = pl.BlockSpec((TM_EW, D), lambda i: (i, 0))
    vec = pl.BlockSpec((1, D), lambda i: (0, 0))
    return _pallas(
        body, name="ln_gate_bwd", grid=(t // TM_EW,), in_specs=[row, row, row, vec, vec],
        out_specs=[row, row, vec, vec],
        out_shape=[jax.ShapeDtypeStruct((t, D), F32), jax.ShapeDtypeStruct((t, D), BF16),
                   jax.ShapeDtypeStruct((1, D), F32), jax.ShapeDtypeStruct((1, D), F32)],
        compiler_params=_params(("arbitrary",)),
    )(aconv, z, dac, ln_g, ln_b)


TM_PREP = 256
PREP_LAT = SEQ // TM_PREP
PREP_ALL = SEQ_ALL // TM_PREP


def _chunk_tri(n, upper):
    r = lax.broadcasted_iota(jnp.int32, (n, n), 0)
    c = lax.broadcasted_iota(jnp.int32, (n, n), 1)
    same = (r // CHUNK) == (c // CHUNK)
    keep = (c >= r) if upper else (c <= r)
    return jnp.where(same & keep, 1.0, 0.0).astype(F32)


def _chunk_sums(v, upper):
    tri = _chunk_tri(v.shape[0], upper).astype(BF16)
    hi = v.astype(BF16)
    r1 = v - hi.astype(F32)
    mid = r1.astype(BF16)
    lo = (r1 - mid.astype(F32)).astype(BF16)
    return (_nn(tri, hi) + _nn(tri, mid)) + _nn(tri, lo)


def _prep_tile_maps(n_samples):
    n_lat = n_samples * PREP_LAT

    def seq_map(i):
        return jnp.where(i < n_lat, i // PREP_LAT, i - n_lat), jnp.where(i < n_lat, i % PREP_LAT, PREP_LAT)

    return n_lat, seq_map


def gla_prep_fwd(p3, upf, upb, bias_f, bias_b, n_samples):
    n_lat, seq_map = _prep_tile_maps(n_samples)
    n_tiles = n_lat + n_samples

    def body(v_ref, q_ref, k_ref, ab_ref, upf_ref, upb_ref, bf_ref, bb_ref, qo, ko, vo, cf, cb):
        i = pl.program_id(0)
        qo[0] = jnp.where(i < n_lat, q_ref[...].astype(F32) * Q_SCALE, 0.0)
        ko[0] = k_ref[...].astype(F32)
        vo[0] = v_ref[...].astype(F32)
        ab = ab_ref[...].astype(F32)
        gf = _log_sigmoid(_nn(ab, upf_ref[...], HI) + bf_ref[...]) * (1.0 / GATE_TAU)
        gb = _log_sigmoid(_nn(ab, upb_ref[...], HI) + bb_ref[...]) * (1.0 / GATE_TAU)
        cf[0] = _chunk_sums(gf, False)
        cb[0] = _chunk_sums(gb, True)

    def o_spec(w):
        return pl.BlockSpec((1, TM_PREP, w), lambda i: (*seq_map(i), 0))

    full = lambda shape: pl.BlockSpec(shape, lambda i: (0,) * len(shape))
    return _pallas(
        body, name="gla_prep_fwd", grid=(n_tiles,),
        in_specs=[pl.BlockSpec((TM_PREP, 1024), lambda i: (i, O3_V // 1024)),
                  pl.BlockSpec((TM_PREP, 512), lambda i: (i, O3_Q // 512)),
                  pl.BlockSpec((TM_PREP, 512), lambda i: (i, O3_K // 512)),
                  pl.BlockSpec((TM_PREP, 128), lambda i: (i, O3_AB // 128)),
                  full((128, GLA_DK)), full((128, GLA_DK)), full((1, GLA_DK)), full((1, GLA_DK))],
        out_specs=[o_spec(GLA_DK), o_spec(GLA_DK), o_spec(D), o_spec(GLA_DK), o_spec(GLA_DK)],
        out_shape=[jax.ShapeDtypeStruct((n_samples, SEQ_ALL, GLA_DK), F32),
                   jax.ShapeDtypeStruct((n_samples, SEQ_ALL, GLA_DK), F32),
                   jax.ShapeDtypeStruct((n_samples, SEQ_ALL, D), F32),
                   jax.ShapeDtypeStruct((n_samples, SEQ_ALL, GLA_DK), F32),
                   jax.ShapeDtypeStruct((n_samples, SEQ_ALL, GLA_DK), F32)],
        compiler_params=_params(("parallel",)),
    )(p3, p3, p3, p3, upf, upb, bias_f, bias_b)


def gla_prep_bwd(p3, dq_f, dq_b, dk_f, dk_b, dv_f, dv_b, dc_f, dc_b, upf, upb, bias_f, bias_b, n_samples):
    n_lat, seq_map = _prep_tile_maps(n_samples)
    n_tiles = n_lat + n_samples

    def body(ab_ref, dqf, dqb, dkf, dkb, dvf, dvb, dcf, dcb, upf_ref, upb_ref, bf_ref, bb_ref,
             dp_ref, duf_ref, dub_ref, dbf_ref, dbb_ref):
        i = pl.program_id(0)
        dp_ref[:, pl.ds(O3_V, D)] = (dvf[0] + dvb[0]).astype(dp_ref.dtype)
        dq = jnp.where(i < n_lat, (dqf[0] + dqb[0]) * Q_SCALE, 0.0)
        dp_ref[:, pl.ds(O3_Q, GLA_DK)] = dq.astype(dp_ref.dtype)
        dp_ref[:, pl.ds(O3_K, GLA_DK)] = (dkf[0] + dkb[0]).astype(dp_ref.dtype)
        ab = ab_ref[...].astype(F32)
        zf = _nn(ab, upf_ref[...], HI) + bf_ref[...]
        zb = _nn(ab, upb_ref[...], HI) + bb_ref[...]
        dgf = _chunk_sums(dcf[0], True)
        dgb = _chunk_sums(dcb[0], False)
        dzf = dgf * (1.0 / GATE_TAU) * _sigmoid(-zf)
        dzb = dgb * (1.0 / GATE_TAU) * _sigmoid(-zb)
        dab = _nt(dzf, upf_ref[...], HI) + _nt(dzb, upb_ref[...], HI)
        dp_ref[:, pl.ds(O3_AB, 128)] = dab.astype(dp_ref.dtype)

        @pl.when(i == 0)
        def _():
            duf_ref[...] = jnp.zeros_like(duf_ref)
            dub_ref[...] = jnp.zeros_like(dub_ref)
            dbf_ref[...] = jnp.zeros_like(dbf_ref)
            dbb_ref[...] = jnp.zeros_like(dbb_ref)

        duf_ref[...] += _tn(ab, dzf, HI)
        dub_ref[...] += _tn(ab, dzb, HI)
        dbf_ref[...] += jnp.sum(dzf, axis=0, keepdims=True)
        dbb_ref[...] += jnp.sum(dzb, axis=0, keepdims=True)

    def s_spec(w):
        return pl.BlockSpec((1, TM_PREP, w), lambda i: (*seq_map(i), 0))

    full = lambda shape: pl.BlockSpec(shape, lambda i: (0,) * len(shape))
    return _pallas(
        body, name="gla_prep_bwd", grid=(n_tiles,),
        in_specs=[pl.BlockSpec((TM_PREP, 128), lambda i: (i, O3_AB // 128)),
                  s_spec(GLA_DK), s_spec(GLA_DK), s_spec(GLA_DK), s_spec(GLA_DK), s_spec(D), s_spec(D),
                  s_spec(GLA_DK), s_spec(GLA_DK),
                  full((128, GLA_DK)), full((128, GLA_DK)), full((1, GLA_DK)), full((1, GLA_DK))],
        out_specs=[pl.BlockSpec((TM_PREP, W3), lambda i: (i, 0)),
                   full((128, GLA_DK)), full((128, GLA_DK)), full((1, GLA_DK)), full((1, GLA_DK))],
        out_shape=[jax.ShapeDtypeStruct((n_tiles * TM_PREP, W3), BF16),
                   jax.ShapeDtypeStruct((128, GLA_DK), F32), jax.ShapeDtypeStruct((128, GLA_DK), F32),
                   jax.ShapeDtypeStruct((1, GLA_DK), F32), jax.ShapeDtypeStruct((1, GLA_DK), F32)],
        compiler_params=_params(("arbitrary",)),
    )(p3, dq_f, dq_b, dk_f, dk_b, dv_f, dv_b, dc_f, dc_b, upf, upb, bias_f, bias_b)


def _sub_blocks(rev):
    if NSUB == 1:
        return [((0, CHUNK), CHUNK // 2, (0, CHUNK))]
    out = []
    for s in range(NSUB):
        rows = (s * SUB, SUB)
        if rev:
            ref = (s + 1) * SUB if s < NSUB - 1 else None
            cols = (s * SUB, CHUNK - s * SUB)
        else:
            ref = s * SUB - 1 if s > 0 else None
            cols = (0, (s + 1) * SUB)
        out.append((rows, ref, cols))
    return out


def _sub_mask(rows, cols, rev):
    r = rows[0] + lax.broadcasted_iota(jnp.int32, (rows[1], cols[1]), 0)
    c = cols[0] + lax.broadcasted_iota(jnp.int32, (rows[1], cols[1]), 1)
    return (c >= r) if rev else (c <= r)


def _sub_operands(qc, kc, cc, rows, ref, cols):
    cref = jnp.zeros((1, HEAD_K), F32) if ref is None else cc[ref:ref + 1]
    eq = jnp.exp(cc[rows[0]:rows[0] + rows[1]] - cref)
    ek = jnp.exp(cref - cc[cols[0]:cols[0] + cols[1]])
    qs = qc[rows[0]:rows[0] + rows[1]] * eq
    kk = kc[cols[0]:cols[0] + cols[1]] * ek
    return qs, kk, eq, ek


SCAN_ROWS = 256
SCAN_CHUNKS = SCAN_ROWS // CHUNK
SCAN_STEPS = SEQ_ALL // SCAN_ROWS
LAT_BLOCKS = SEQ // SCAN_ROWS


def _scan_block(t, rev):
    if rev:
        return SCAN_STEPS - 1 - t
    return jnp.where(t == 0, SCAN_STEPS - 1, t - 1)


def _scan_lat_block(t, rev):
    if rev:
        return jnp.minimum(SCAN_STEPS - 1 - t, LAT_BLOCKS - 1)
    return jnp.maximum(t - 1, 0)


def _head_cols(h):
    return pl.ds(h * HEAD_K, HEAD_K), pl.ds(h * HEAD_V, HEAD_V)


def gla_scan_fwd(q, k, v, cum, *, rev, name):
    n = q.shape[0]

    def body(q_ref, k_ref, v_ref, c_ref, o_ref, s_ref, sfin_ref, st):
        t = pl.program_id(1)

        @pl.when(t == 0)
        def _():
            st[...] = jnp.zeros_like(st)

        def chunk(j, carry):
            lj = SCAN_CHUNKS - 1 - j if rev else j
            r0 = pl.multiple_of(lj * CHUNK, CHUNK)
            rws = pl.ds(r0, CHUNK)
            for h in range(HEADS):
                kcols, vcols = _head_cols(h)
                qc, kc, cc = q_ref[0, rws, kcols], k_ref[0, rws, kcols], c_ref[0, rws, kcols]
                vc = v_ref[0, rws, vcols]
                s_in = st[h]
                s_ref[0, h, j] = s_in
                edge = cc[0:1] if rev else cc[CHUNK - 1:CHUNK]
                ke = kc * jnp.exp(edge - cc)
                st[h] = s_in * jnp.exp(edge) + _tn(_b16(vc), _b16(ke))
                o_inter = _nt(_b16(qc * jnp.exp(cc)), _b16(s_in))
                vb = _b16(vc)
                for rows, ref, cols in _sub_blocks(rev):
                    qs, kk, _, _ = _sub_operands(qc, kc, cc, rows, ref, cols)
                    a = jnp.where(_sub_mask(rows, cols, rev), _nt(_b16(qs), _b16(kk)), 0.0)
                    o_s = _nn(_b16(a), vb[cols[0]:cols[0] + cols[1]])
                    o_ref[0, pl.ds(r0 + rows[0], rows[1]), vcols] = o_inter[rows[0]:rows[0] + rows[1]] + o_s
            return carry

        lax.fori_loop(0, SCAN_CHUNKS, chunk, 0)

        @pl.when(t == SCAN_STEPS - 1)
        def _():
            sfin_ref[0] = st[...]

    def spec(w):
        return pl.BlockSpec((1, SCAN_ROWS, w), lambda b, t: (b, _scan_block(t, rev), 0))

    return _pallas(
        body, name=name, grid=(n, SCAN_STEPS),
        in_specs=[spec(GLA_DK), spec(GLA_DK), spec(D), spec(GLA_DK)],
        out_specs=[pl.BlockSpec((1, SCAN_ROWS, D), lambda b, t: (b, _scan_lat_block(t, rev), 0)),
                   pl.BlockSpec((1, HEADS, SCAN_CHUNKS, HEAD_V, HEAD_K), lambda b, t: (b, 0, t, 0, 0)),
                   pl.BlockSpec((1, HEADS, HEAD_V, HEAD_K), lambda b, t: (b, 0, 0, 0))],
        out_shape=[jax.ShapeDtypeStruct((n, SEQ, D), F32),
                   jax.ShapeDtypeStruct((n, HEADS, NCHUNK, HEAD_V, HEAD_K), F32),
                   jax.ShapeDtypeStruct((n, HEADS, HEAD_V, HEAD_K), F32)],
        scratch_shapes=[pltpu.VMEM((HEADS, HEAD_V, HEAD_K), F32)],
        compiler_params=_params(("parallel", "arbitrary")),
    )(q, k, v, cum)


def gla_scan_bwd(q, k, v, cum, s_all, s_fin, do, *, rev, name):
    n = q.shape[0]

    def body(q_ref, k_ref, v_ref, c_ref, s_ref, sfin_ref, do_ref, dq_ref, dk_ref, dv_ref, dc_ref,
             dst, s_next, dq_acc, dk_acc, dv_acc):
        t = SCAN_STEPS - 1 - pl.program_id(1)

        @pl.when(pl.program_id(1) == 0)
        def _():
            dst[...] = jnp.zeros_like(dst)
            s_next[...] = sfin_ref[0]

        def chunk(jj, carry):
            j = SCAN_CHUNKS - 1 - jj
            lj = SCAN_CHUNKS - 1 - j if rev else j
            rws = pl.ds(pl.multiple_of(lj * CHUNK, CHUNK), CHUNK)
            for h in range(HEADS):
                kcols, vcols = _head_cols(h)
                qc, kc, cc = q_ref[0, rws, kcols], k_ref[0, rws, kcols], c_ref[0, rws, kcols]
                vc = v_ref[0, rws, vcols]
                doc = jnp.where(t > 0, do_ref[0, rws, vcols], 0.0)
                s_in = s_ref[0, h, j]
                s_out = s_next[h]
                ds_out = dst[h]
                edge = cc[0:1] if rev else cc[CHUNK - 1:CHUNK]
                e_q = jnp.exp(cc)
                e_k = jnp.exp(edge - cc)
                dob = _b16(doc)
                dsb = _b16(ds_out)
                dst[h] = ds_out * jnp.exp(edge) + _tn(dob, _b16(qc * e_q))
                s_next[h] = s_in
                dq_acc[h] = e_q * _nn(dob, _b16(s_in))
                dk_acc[h] = e_k * _nn(_b16(vc), dsb)
                dv_acc[h] = _nt(_b16(kc * e_k), dsb)
                vb = _b16(vc)
                for rows, ref, cols in _sub_blocks(rev):
                    qs, kk, eq, ek = _sub_operands(qc, kc, cc, rows, ref, cols)
                    mask = _sub_mask(rows, cols, rev)
                    rsl = slice(rows[0], rows[0] + rows[1])
                    csl = pl.ds(cols[0], cols[1])
                    qsb, kkb = _b16(qs), _b16(kk)
                    a = jnp.where(mask, _nt(qsb, kkb), 0.0)
                    da = _b16(jnp.where(mask, _nt(dob[rsl], vb[cols[0]:cols[0] + cols[1]]), 0.0))
                    dq_acc[h, pl.ds(rows[0], rows[1]), :] += _nn(da, kkb) * eq
                    dk_acc[h, csl, :] += _tn(da, qsb) * ek
                    dv_acc[h, csl, :] += _tn(_b16(a), dob[rsl])
                dq = dq_acc[h]
                dk = dk_acc[h]
                dc = qc * dq - kc * dk
                bnd = jnp.sum(ds_out * s_out, axis=0, keepdims=True)
                edge_row = 0 if rev else CHUNK - 1
                is_edge = lax.broadcasted_iota(jnp.int32, (CHUNK, HEAD_K), 0) == edge_row
                dq_ref[0, rws, kcols] = dq
                dk_ref[0, rws, kcols] = dk
                dv_ref[0, rws, vcols] = dv_acc[h]
                dc_ref[0, rws, kcols] = dc + jnp.where(is_edge, bnd, 0.0)
            return carry

        lax.fori_loop(0, SCAN_CHUNKS, chunk, 0)

    def step_of(u):
        return SCAN_STEPS - 1 - u

    def spec(w):
        return pl.BlockSpec((1, SCAN_ROWS, w), lambda b, u: (b, _scan_block(step_of(u), rev), 0))

    return _pallas(
        body, name=name, grid=(n, SCAN_STEPS),
        in_specs=[spec(GLA_DK), spec(GLA_DK), spec(D), spec(GLA_DK),
                  pl.BlockSpec((1, HEADS, SCAN_CHUNKS, HEAD_V, HEAD_K), lambda b, u: (b, 0, step_of(u), 0, 0)),
                  pl.BlockSpec((1, HEADS, HEAD_V, HEAD_K), lambda b, u: (b, 0, 0, 0)),
                  pl.BlockSpec((1, SCAN_ROWS, D), lambda b, u: (b, _scan_lat_block(step_of(u), rev), 0))],
        out_specs=[spec(GLA_DK), spec(GLA_DK), spec(D), spec(GLA_DK)],
        out_shape=[jax.ShapeDtypeStruct((n, SEQ_ALL, GLA_DK), F32), jax.ShapeDtypeStruct((n, SEQ_ALL, GLA_DK), F32),
                   jax.ShapeDtypeStruct((n, SEQ_ALL, D), F32), jax.ShapeDtypeStruct((n, SEQ_ALL, GLA_DK), F32)],
        scratch_shapes=[pltpu.VMEM((HEADS, HEAD_V, HEAD_K), F32), pltpu.VMEM((HEADS, HEAD_V, HEAD_K), F32),
                        pltpu.VMEM((HEADS, CHUNK, HEAD_K), F32), pltpu.VMEM((HEADS, CHUNK, HEAD_K), F32),
                        pltpu.VMEM((HEADS, CHUNK, HEAD_V), F32)],
        compiler_params=_params(("parallel", "arbitrary")),
    )(q, k, v, cum, s_all, s_fin, do)


def gla_out_fwd(o_f, o_b, r, gnorm):
    n = o_f.shape[0]
    tiles = SEQ // TM_EW

    def body(of_ref, ob_ref, r_ref, g_ref, og_ref):
        for h in range(HEADS):
            cols = pl.ds(h * HEAD_V, HEAD_V)
            o = of_ref[0, :, cols] + ob_ref[0, :, cols]
            rs = lax.rsqrt(jnp.mean(o * o, axis=-1, keepdims=True) + EPS)
            og_ref[:, cols] = (o * rs * g_ref[...] * _silu(r_ref[:, cols].astype(F32))).astype(og_ref.dtype)

    ospec = pl.BlockSpec((1, TM_EW, D), lambda b, j: (b, j, 0))
    row = pl.BlockSpec((TM_EW, D), lambda b, j: (b * tiles + j, 0))
    return _pallas(
        body, name="gla_out_fwd", grid=(n, tiles),
        in_specs=[ospec, ospec, row, pl.BlockSpec((1, HEAD_V), lambda b, j: (0, 0))],
        out_specs=row, out_shape=jax.ShapeDtypeStruct((n * SEQ, D), BF16),
        compiler_params=_params(("parallel", "parallel")),
    )(o_f, o_b, r, gnorm)


def gla_out_bwd(o_f, o_b, r, dog, gnorm):
    n = o_f.shape[0]
    tiles = SEQ // TM_EW

    def body(of_ref, ob_ref, r_ref, d_ref, g_ref, do_ref, dr_ref, dg_ref):
        @pl.when((pl.program_id(0) == 0) & (pl.program_id(1) == 0))
        def _():
            dg_ref[...] = jnp.zeros_like(dg_ref)

        for h in range(HEADS):
            cols = pl.ds(h * HEAD_V, HEAD_V)
            o = of_ref[0, :, cols] + ob_ref[0, :, cols]
            rv = r_ref[:, cols].astype(F32)
            dv = d_ref[:, cols].astype(F32)
            rs = lax.rsqrt(jnp.mean(o * o, axis=-1, keepdims=True) + EPS)
            oh = o * rs
            dr_ref[:, cols] = (dv * oh * g_ref[...] * _dsilu(rv)).astype(dr_ref.dtype)
            dn = dv * _silu(rv)
            dg_ref[...] += jnp.sum(dn * oh, axis=0, keepdims=True)
            doh = dn * g_ref[...]
            do_ref[0, :, cols] = rs * (doh - oh * jnp.mean(doh * oh, axis=-1, keepdims=True))

    ospec = pl.BlockSpec((1, TM_EW, D), lambda b, j: (b, j, 0))
    row = pl.BlockSpec((TM_EW, D), lambda b, j: (b * tiles + j, 0))
    vec = pl.BlockSpec((1, HEAD_V), lambda b, j: (0, 0))
    return _pallas(
        body, name="gla_out_bwd", grid=(n, tiles),
        in_specs=[ospec, ospec, row, row, vec],
        out_specs=[ospec, row, vec],
        out_shape=[jax.ShapeDtypeStruct((n, SEQ, D), F32), jax.ShapeDtypeStruct((n * SEQ, D), BF16),
                   jax.ShapeDtypeStruct((1, HEAD_V), F32)],
        compiler_params=_params(("arbitrary", "arbitrary")),
    )(o_f, o_b, r, dog, gnorm)


def merge_fwd(p5, y_conv, y_gla):
    t = y_conv.shape[0]

    def body(mc_ref, mg_ref, yc_ref, yg_ref, o_ref):
        f = lambda ref: ref[...].astype(F32)
        o_ref[...] = (_sigmoid(f(mc_ref)) * f(yc_ref) + _sigmoid(f(mg_ref)) * f(yg_ref)).astype(o_ref.dtype)

    row = pl.BlockSpec((TM_EW, D), lambda i: (i, 0))
    return _pallas(
        body, name="merge_fwd", grid=(t // TM_EW,),
        in_specs=[row, pl.BlockSpec((TM_EW, D), lambda i: (i, 1)), row, row], out_specs=row,
        out_shape=jax.ShapeDtypeStruct((t, D), BF16), compiler_params=_params(("parallel",)),
    )(p5, p5, y_conv, y_gla)


def merge_bwd(p5, y_conv, y_gla, dmerged):
    t = y_conv.shape[0]

    def body(mc_ref, mg_ref, yc_ref, yg_ref, d_ref, dyc_ref, dyg_ref, dp_ref):
        f = lambda ref: ref[...].astype(F32)
        d = f(d_ref)
        sc = _sigmoid(f(mc_ref))
        sg = _sigmoid(f(mg_ref))
        dyc_ref[...] = (d * sc).astype(dyc_ref.dtype)
        dyg_ref[...] = (d * sg).astype(dyg_ref.dtype)
        dp_ref[:, pl.ds(0, D)] = (d * f(yc_ref) * sc * (1.0 - sc)).astype(dp_ref.dtype)
        dp_ref[:, pl.ds(D, D)] = (d * f(yg_ref) * sg * (1.0 - sg)).astype(dp_ref.dtype)

    row = pl.BlockSpec((TM_EW, D), lambda i: (i, 0))
    return _pallas(
        body, name="merge_bwd", grid=(t // TM_EW,),
        in_specs=[row, pl.BlockSpec((TM_EW, D), lambda i: (i, 1)), row, row, row],
        out_specs=[row, row, pl.BlockSpec((TM_EW, 2 * D), lambda i: (i, 0))],
        out_shape=[jax.ShapeDtypeStruct((t, D), BF16), jax.ShapeDtypeStruct((t, D), BF16),
                   jax.ShapeDtypeStruct((t, 2 * D), BF16)],
        compiler_params=_params(("parallel",)),
    )(p5, p5, y_conv, y_gla, dmerged)


def final_fwd_bwd(x2, mo, gate, final_g, target, n_samples):
    t = x2.shape[0]
    tiles = SEQ // TM_EW

    def body(x_ref, mo_ref, gate_ref, g_ref, t_ref, dh_ref, dmo_ref, dgate_ref, dg_ref, loss_ref):
        b, j = pl.program_id(0), pl.program_id(1)
        mo_v = mo_ref[...]
        h = x_ref[...] + gate_ref[0] * mo_v
        rs = lax.rsqrt(jnp.mean(h * h, axis=-1, keepdims=True) + EPS)
        nh = h * rs
        err = nh * g_ref[...] - t_ref[...]
        dy = err * (1.0 / D)
        dn = dy * g_ref[...]
        dh = rs * (dn - nh * jnp.mean(dn * nh, axis=-1, keepdims=True))
        dh_ref[...] = dh
        dmo_ref[...] = (dh * gate_ref[0]).astype(dmo_ref.dtype)

        @pl.when(j == 0)
        def _():
            dgate_ref[...] = jnp.zeros_like(dgate_ref)

        @pl.when((b == 0) & (j == 0))
        def _():
            dg_ref[...] = jnp.zeros_like(dg_ref)
            loss_ref[...] = jnp.zeros_like(loss_ref)

        dgate_ref[0] += jnp.sum(dh * mo_v, axis=0, keepdims=True)
        dg_ref[...] += jnp.sum(dy * nh, axis=0, keepdims=True)
        loss_ref[...] += (0.5 / D) * jnp.sum(err * err)

    row = pl.BlockSpec((TM_EW, D), lambda b, j: (b * tiles + j, 0))
    per = pl.BlockSpec((1, 1, D), lambda b, j: (b, 0, 0))
    vec = pl.BlockSpec((1, D), lambda b, j: (0, 0))
    return _pallas(
        body, name="final_fwd_bwd", grid=(n_samples, tiles),
        in_specs=[row, row, per, vec, row],
        out_specs=[row, row, per, vec, pl.BlockSpec((8, 128), lambda b, j: (0, 0))],
        out_shape=[jax.ShapeDtypeStruct((t, D), F32), jax.ShapeDtypeStruct((t, D), BF16),
                   jax.ShapeDtypeStruct((n_samples, 1, D), F32), jax.ShapeDtypeStruct((1, D), F32),
                   jax.ShapeDtypeStruct((8, 128), F32)],
        compiler_params=_params(("arbitrary", "arbitrary")),
    )(x2, mo, gate, final_g, target)


def local_step(x, ctx, target, mod, wts, small):
    n = x.shape[0]
    t = n * SEQ
    t_all = t + n * NCTX
    x2 = x.reshape(t, D)
    ctx2 = ctx.reshape(n * NCTX, D)
    tgt2 = target.reshape(t, D)
    scale1, shift, gate = mod

    u = norm_mod_fwd(x2, ctx2, scale1, shift, small["norm_g"])
    p1 = matmul_nn(u, wts["w1"], small["b1"], name="proj_conv", m=t, tm=1024, tn=1024, out_dtype=BF16)
    p2 = matmul_nn(u, wts["w2"], small["b2"], name="proj_z", m=t, tm=1024, tn=1024, out_dtype=BF16)
    p3 = matmul_nn(u, wts["w3"], small["b3"], name="proj_gla", m=t_all, tm=512, tn=W3, out_dtype=BF16)
    p4 = matmul_nn(u, wts["w4"], small["b4"], name="proj_r", m=t, tm=1024, tn=1024, out_dtype=BF16)
    p5 = matmul_nn(u, wts["w5"], small["b5"], name="proj_merge", m=t, tm=1024, tn=1024, out_dtype=BF16)

    aconv = conv_fwd(p1, small["conv_w"], small["conv_b"], n)
    ac = ln_gate_fwd(aconv, p2, small["conv_ln_g"], small["conv_ln_b"])
    y_conv = matmul_nn(ac, wts["conv_proj"], None, name="conv_proj_fwd", m=t, tm=1024, tn=1024, out_dtype=BF16)

    qs, ks, vs, cum_f, cum_b = gla_prep_fwd(p3, small["upf"], small["upb"], small["bias_f"], small["bias_b"], n)
    o_f, s_f, sfin_f = gla_scan_fwd(qs, ks, vs, cum_f, rev=False, name="gla_scan_fwd_f")
    o_b, s_b, sfin_b = gla_scan_fwd(qs, ks, vs, cum_b, rev=True, name="gla_scan_fwd_b")
    og = gla_out_fwd(o_f, o_b, p4, small["gla_norm_g"])
    y_gla = matmul_nn(og, wts["gla_proj"], None, name="gla_proj_fwd", m=t, tm=1024, tn=1024, out_dtype=BF16)

    merged = merge_fwd(p5, y_conv, y_gla)
    mo = matmul_nn(merged, wts["w_out"], None, name="w_out_fwd", m=t, tm=1024, tn=1024, out_dtype=F32)
    dh, dmo, dgate, d_final_g, loss = final_fwd_bwd(x2, mo, gate, small["final_norm_g"], tgt2, n)

    g = {"final_norm_g": d_final_g}
    dmerged = matmul_nt(dmo, wts["w_out"], name="w_out_dgrad", tm=512, out_dtype=BF16)
    g["w_out"] = matmul_tn(merged, dmo, name="w_out_wgrad", t=t, tn=1024, tt=1024)[0]
    dyc, dyg, dp5 = merge_bwd(p5, y_conv, y_gla, dmerged)

    dac = matmul_nt(dyc, wts["conv_proj"], name="conv_proj_dgrad", tm=512, out_dtype=BF16)
    g["conv_proj"] = matmul_tn(ac, dyc, name="conv_proj_wgrad", t=t, tn=1024, tt=1024)[0]
    daconv, dp2, g["conv_ln_g"], g["conv_ln_b"] = ln_gate_bwd(aconv, p2, dac, small["conv_ln_g"], small["conv_ln_b"])
    dp1, dconv_w, dconv_b = conv_bwd(p1, daconv, small["conv_w"], n)
    g["conv_w"], g["conv_b"] = dconv_w, dconv_b

    dog = matmul_nt(dyg, wts["gla_proj"], name="gla_proj_dgrad", tm=512, out_dtype=BF16)
    g["gla_proj"] = matmul_tn(og, dyg, name="gla_proj_wgrad", t=t, tn=1024, tt=1024)[0]
    do, dp4, g["gla_norm_g"] = gla_out_bwd(o_f, o_b, p4, dog, small["gla_norm_g"])
    dq_f, dk_f, dv_f, dc_f = gla_scan_bwd(qs, ks, vs, cum_f, s_f, sfin_f, do, rev=False, name="gla_scan_bwd_f")
    dq_b, dk_b, dv_b, dc_b = gla_scan_bwd(qs, ks, vs, cum_b, s_b, sfin_b, do, rev=True, name="gla_scan_bwd_b")
    dp3, g["upf"], g["upb"], g["bias_f"], g["bias_b"] = gla_prep_bwd(
        p3, dq_f, dq_b, dk_f, dk_b, dv_f, dv_b, dc_f, dc_b,
        small["upf"], small["upb"], small["bias_f"], small["bias_b"], n)

    dps = [dp1, dp2, dp3, dp4, dp5]
    for i, dp in enumerate(dps):
        rows = dp.shape[0]
        tn = W3 if dp.shape[1] == W3 else 1024
        g["w%d" % (i + 1)], g["b%d" % (i + 1)] = matmul_tn(
            u, dp, name="w_in_wgrad_%d" % (i + 1), t=rows, tn=tn, tt=1024 if rows % 1024 == 0 else 768, colsum=True)
    g["gate"] = dgate
    return loss, dh, dps, g


def _group_cols(w):
    gv, gg, z = w[..., 0:1024], w[..., 1024:2048], w[..., 2048:3072]
    q, k, v = w[..., 3072:3584], w[..., 3584:4096], w[..., 4096:5120]
    ab = w[..., 5120:5152]
    r, mc, mg = w[..., 5152:6176], w[..., 6176:7200], w[..., 7200:8224]
    g1 = jnp.concatenate([p for j in range(CONV_NCB)
                          for p in (gv[..., CONV_CB * j:CONV_CB * (j + 1)], gg[..., CONV_CB * j:CONV_CB * (j + 1)])], -1)
    pad = jnp.zeros(w.shape[:-1] + (W3 - 2080,), w.dtype)
    g3 = jnp.concatenate([v, q, k, ab, pad], -1)
    return g1, z, g3, r, jnp.concatenate([mc, mg], -1)


def _ungroup_cols(g1, g2, g3, g4, g5):
    gv = jnp.concatenate([g1[..., 2 * CONV_CB * j:2 * CONV_CB * j + CONV_CB] for j in range(CONV_NCB)], -1)
    gg = jnp.concatenate([g1[..., 2 * CONV_CB * j + CONV_CB:2 * CONV_CB * (j + 1)] for j in range(CONV_NCB)], -1)
    v, q, k, ab = g3[..., 0:1024], g3[..., 1024:1536], g3[..., 1536:2048], g3[..., 2048:2080]
    return jnp.concatenate([gv, gg, g2, q, k, v, ab, g4, g5[..., 0:1024], g5[..., 1024:2048]], -1)


def _natural_pieces():
    pieces = [(CONV_CB * j, CONV_CB, 0, 2 * CONV_CB * j) for j in range(CONV_NCB)]
    pieces += [(1024 + CONV_CB * j, CONV_CB, 0, 2 * CONV_CB * j + CONV_CB) for j in range(CONV_NCB)]
    pieces += [(2048, 1024, 1, 0), (3072, 512, 2, O3_Q), (3584, 512, 2, O3_K), (4096, 1024, 2, O3_V), (5120, 32, 2, O3_AB),
               (5152, 1024, 3, 0), (6176, 1024, 4, 0), (7200, 1024, 4, 1024)]
    return sorted(pieces)


def _ungroup_to_shards(groups):
    shards = []
    for i in range(N_CHIPS):
        lo, hi = i * W_IN_SHARD, (i + 1) * W_IN_SHARD
        parts = []
        for nat, width, g, gcol in _natural_pieces():
            a, b = max(nat, lo), min(nat + width, hi)
            if a < b:
                parts.append(groups[g][:, gcol + a - nat:gcol + b - nat])
        shards.append(jnp.concatenate(parts, 1))
    return jnp.stack(shards)


def _pad_up(up, row0):
    return jnp.zeros((128, GLA_DK), F32).at[row0:row0 + up.shape[0]].set(up)


def _adamw_math(w, g, m, v):
    m = ADAM_B1 * m + (1.0 - ADAM_B1) * g
    v = ADAM_B2 * v + (1.0 - ADAM_B2) * (g * g)
    m_hat = m / (1.0 - ADAM_B1 ** ADAM_STEP)
    v_hat = v / (1.0 - ADAM_B2 ** ADAM_STEP)
    delta = -ADAM_LR * (m_hat / (jnp.sqrt(v_hat) + ADAM_EPS) + ADAM_WD * w)
    return delta, m, v


def adamw2d(w, g, m, v, *, name, tr):
    rows, cols = w.shape[-2:]

    def body(w_ref, g_ref, m_ref, v_ref, d_ref, nm_ref, nv_ref):
        d_ref[...], nm_ref[...], nv_ref[...] = _adamw_math(w_ref[...], g_ref[...], m_ref[...], v_ref[...])

    if w.ndim == 3:
        spec = pl.BlockSpec((1, tr, cols), lambda i: (0, i, 0))
    else:
        spec = pl.BlockSpec((tr, cols), lambda i: (i, 0))
    return _pallas(
        body, name=name, grid=(rows // tr,), in_specs=[spec] * 4, out_specs=[spec] * 3,
        out_shape=[jax.ShapeDtypeStruct(w.shape, F32)] * 3, compiler_params=_params(("parallel",)),
    )(w, g, m, v)


def sum_devices(sall, *, name):
    rows = sall.shape[1]

    def body(s_ref, o_ref):
        acc = s_ref[0]
        for d in range(1, N_DEV):
            acc = acc + s_ref[d]
        o_ref[...] = acc

    return _pallas(body, name=name, out_shape=jax.ShapeDtypeStruct((rows, D), F32),
                   compiler_params=_params())(sall)


def pair_add(core, g, got, *, name, tr):
    n, rows, cols = got.shape
    g4 = g.reshape(n, 2, rows, cols)

    def body(core_ref, g_ref, got_ref, o_ref, ob_ref):
        del core_ref
        s = g_ref[0, 0] + got_ref[0]
        o_ref[0] = s
        ob_ref[0] = s.astype(BF16)

    spec = pl.BlockSpec((1, tr, cols), lambda i, t, core_ref: (i, t, 0))
    return _pallas(
        body, name=name,
        grid_spec=pltpu.PrefetchScalarGridSpec(
            num_scalar_prefetch=1, grid=(n, rows // tr),
            in_specs=[pl.BlockSpec((1, 1, tr, cols), lambda i, t, core_ref: (i, core_ref[0], t, 0)), spec],
            out_specs=[spec, spec]),
        out_shape=[jax.ShapeDtypeStruct(got.shape, F32), jax.ShapeDtypeStruct(got.shape, BF16)],
        compiler_params=_params(("parallel", "parallel")))(core, g4, got)


def chip_add(place, pa, rb, *, name, tr):
    _, rows, cols = pa.shape

    def body(place_ref, m_ref, r_ref, o_ref):
        del place_ref
        o_ref[0] = ((m_ref[0] + r_ref[0].astype(F32)) + r_ref[1].astype(F32)) + r_ref[2].astype(F32)

    return _pallas(
        body, name=name,
        grid_spec=pltpu.PrefetchScalarGridSpec(
            num_scalar_prefetch=1, grid=(rows // tr,),
            in_specs=[pl.BlockSpec((1, tr, cols), lambda t, place_ref: (place_ref[0], t, 0)),
                      pl.BlockSpec((3, tr, cols), lambda t, place_ref: (0, t, 0))],
            out_specs=pl.BlockSpec((1, tr, cols), lambda t, place_ref: (place_ref[1], t, 0))),
        out_shape=jax.ShapeDtypeStruct((2, rows, cols), F32),
        compiler_params=_params(("parallel",)))(place, pa, rb)


def ada_bwd(call, cctx_rows, dm_shard, dm_full, adaw):
    nsh = adaw.shape[1]

    def body(c_ref, cc_ref, dms_ref, dmf_ref, w_ref, gw_ref, gb_ref, pq_ref):
        a_lat = _silu(c_ref[...])
        a_ctx = _silu(cc_ref[...])
        dms = dms_ref[...]
        gw_ref[...] = _tn(a_lat, dms[0:64], HI) + _tn(a_ctx, dms[64:72], HI)
        gb_ref[...] = jnp.sum(dmf_ref[...], axis=0, keepdims=True)
        part = _nt(dms[64:72], w_ref[...], HI)
        pq_ref[...] = jnp.zeros_like(pq_ref) + jnp.sum(part, axis=0, keepdims=True)

    return _pallas(body, name="ada_bwd",
                   out_shape=[jax.ShapeDtypeStruct((D, nsh), F32), jax.ShapeDtypeStruct((1, 3 * D), F32),
                              jax.ShapeDtypeStruct((8, D), F32)],
                   compiler_params=_params())(call, cctx_rows, dm_shard, dm_full, adaw)


def cctx_grad(pq_all, cctx_rows):
    def body(p_ref, c_ref, o_ref):
        acc = p_ref[0]
        for qi in range(1, N_CHIPS):
            acc = acc + p_ref[qi]
        o_ref[...] = acc * _dsilu(c_ref[...])

    return _pallas(body, name="cctx_grad", out_shape=jax.ShapeDtypeStruct((8, D), F32),
                   compiler_params=_params())(pq_all, cctx_rows)


def _place():
    x, y, c = lax.axis_index("x"), lax.axis_index("y"), lax.axis_index("c")
    chips = [(1 - x, y), (x, 1 - y), (1 - x, 1 - y)]
    return x, y, c, chips


def _all_peers(x, y, c):
    return [((1 - x) if r & 4 else x, (1 - y) if r & 2 else y, (1 - c) if r & 1 else c) for r in range(1, N_DEV)]


def _remote(src, dst, send_sem, recv_sem, dev):
    return pltpu.make_async_remote_copy(src_ref=src, dst_ref=dst, send_sem=send_sem, recv_sem=recv_sem,
                                        device_id=dev, device_id_type=MESH)


ANY = pl.BlockSpec(memory_space=pl.ANY)
VMEM = pl.BlockSpec(memory_space=pltpu.VMEM)
F_ROWS = 16


W_ROW_CHUNKS = 4
P_ROW_CHUNKS = 2
N_BULK = W_ROW_CHUNKS + P_ROW_CHUNKS


def _half_chunks(core, n_rows, align):
    out = []
    for a, k in ((0, W_ROW_CHUNKS), (1, P_ROW_CHUNKS)):
        half = n_rows[a] // 2
        size = half // k
        for i in range(k):
            start = core * half + i * size
            out.append((a, pl.ds(start if isinstance(start, int) else pl.multiple_of(start, align), size)))
    return out


def gather_weights(c8, cctx8, adaw, adab, w_sh, p_sh, fp):
    nsh = adaw.shape[1]

    def body(c_ref, cctx_ref, adaw_ref, adab_ref, w_ref, p_ref, fp_ref, wall_ref, pall_ref, fall_ref, call_ref, mall_ref,
             abuf, w_send, w_recv, h_send, h_recv, c_send, c_recv, m_send, m_recv, f_send, f_recv):
        x, y, c, chips = _place()
        q = 2 * x + y
        dev = 4 * x + 2 * y + c
        qs = [2 * cx + cy for cx, cy in chips]
        sib = (x, y, 1 - c)
        srcs, dsts = (w_ref, p_ref), (wall_ref, pall_ref)
        n_rows = (w_ref.shape[0], p_ref.shape[0])
        mine = _half_chunks(c, n_rows, 16)
        other = _half_chunks(1 - c, n_rows, 16)

        bulk = [[_remote(srcs[a].at[rows], dsts[a].at[q, rows], w_send.at[j * N_BULK + i], w_recv.at[j * N_BULK + i],
                         (*chips[j], c)) for i, (a, rows) in enumerate(mine)] for j in range(3)]
        fall_ref[q] = fp_ref[...]
        small = [_remote(fp_ref, fall_ref.at[q], f_send.at[j], f_recv.at[j], (*chips[j], c)) for j in range(3)]
        my_rows = pl.ds(pl.multiple_of(8 * dev, 8), 8)
        call_ref[my_rows, :] = c_ref[...]
        cond = [_remote(c_ref, call_ref.at[my_rows, :], c_send.at[r], c_recv.at[r], peer)
                for r, peer in enumerate(_all_peers(x, y, c))]
        for cp in sum(bulk, []) + small + cond:
            cp.start()
        for cp in cond:
            cp.wait_recv()

        abuf[pl.ds(0, 64), :] = _silu(call_ref[...])
        abuf[pl.ds(64, 8), :] = _silu(cctx_ref[...])
        mall_ref[q] = _nn(abuf[...], adaw_ref[...], HI) + adab_ref[...]
        mod = [_remote(mall_ref.at[q], mall_ref.at[q], m_send.at[j], m_recv.at[j], (*chips[j], c)) for j in range(3)]
        for cp in mod:
            cp.start()

        handed = []
        for j in range(3):
            for i, (a, rows) in enumerate(mine):
                bulk[j][i].wait_recv()
                cp = _remote(dsts[a].at[qs[j], rows], dsts[a].at[qs[j], rows],
                             h_send.at[j * N_BULK + i], h_recv.at[j * N_BULK + i], sib)
                cp.start()
                handed.append(cp)
        for j in range(3):
            for i, (a, rows) in enumerate(other):
                _remote(dsts[a].at[qs[j], rows], dsts[a].at[qs[j], rows],
                        h_send.at[j * N_BULK + i], h_recv.at[j * N_BULK + i], sib).wait_recv()
        for cp in mod + small:
            cp.wait_recv()
        for cp in sum(bulk, []) + small + cond + mod + handed:
            cp.wait_send()

    def dma(n):
        return pltpu.SemaphoreType.DMA((n,))

    return _pallas(
        body, name="gather_weights",
        in_specs=[VMEM, VMEM, VMEM, VMEM, ANY, ANY, VMEM],
        out_specs=[ANY, ANY, VMEM, VMEM, VMEM],
        out_shape=[jax.ShapeDtypeStruct((N_CHIPS,) + w_sh.shape, BF16), jax.ShapeDtypeStruct((N_CHIPS,) + p_sh.shape, BF16),
                   jax.ShapeDtypeStruct((N_CHIPS, F_ROWS, D), F32),
                   jax.ShapeDtypeStruct((8 * N_DEV, D), F32), jax.ShapeDtypeStruct((N_CHIPS, MOD_ROWS, nsh), F32)],
        scratch_shapes=[pltpu.VMEM((MOD_ROWS, D), F32), dma(3 * N_BULK), dma(3 * N_BULK), dma(3 * N_BULK), dma(3 * N_BULK),
                        dma(7), dma(7), dma(3), dma(3), dma(3), dma(3)],
        compiler_params=_params(),
    )(c8, cctx8, adaw, adab, w_sh, p_sh, fp)


def pair_swap(gw, gp):
    n_pair = N_CHIPS * N_BULK

    def body(gw_ref, gp_ref, gotw_ref, gotp_ref, a_send, a_recv):
        x, y, c, _ = _place()
        srcs, dsts = (gw_ref, gp_ref), (gotw_ref, gotp_ref)
        n_rows = (gw_ref.shape[1], gp_ref.shape[1])
        pair = []
        for i, ((a, rows_o), (_, rows_0)) in enumerate(zip(_half_chunks(1 - c, n_rows, 8), _half_chunks(0, n_rows, 8))):
            for s in range(N_CHIPS):
                k = s * N_BULK + i
                pair.append(_remote(srcs[a].at[s, rows_o], dsts[a].at[s, rows_0], a_send.at[k], a_recv.at[k], (x, y, 1 - c)))
        for cp in pair:
            cp.start()
        for cp in pair:
            cp.wait_recv()
        for cp in pair:
            cp.wait_send()

    return _pallas(
        body, name="pair_swap", in_specs=[ANY, ANY], out_specs=[ANY, ANY],
        out_shape=[jax.ShapeDtypeStruct((N_CHIPS, gw.shape[1] // 2, gw.shape[2]), F32),
                   jax.ShapeDtypeStruct((N_CHIPS, gp.shape[1] // 2, gp.shape[2]), F32)],
        scratch_shapes=[pltpu.SemaphoreType.DMA((n_pair,)), pltpu.SemaphoreType.DMA((n_pair,))],
        compiler_params=_params(),
    )(gw, gp)


def gather_small(sm):
    rows = sm.shape[0]

    def body(sm_ref, sall_ref, s_send, s_recv):
        x, y, c, _ = _place()
        dev = 4 * x + 2 * y + c
        sall_ref[dev] = sm_ref[...]
        small = [_remote(sm_ref, sall_ref.at[dev], s_send.at[r], s_recv.at[r], peer)
                 for r, peer in enumerate(_all_peers(x, y, c))]
        for cp in small:
            cp.start()
        for cp in small:
            cp.wait_recv()
        for cp in small:
            cp.wait_send()

    return _pallas(
        body, name="gather_small", in_specs=[VMEM], out_specs=VMEM,
        out_shape=jax.ShapeDtypeStruct((N_DEV, rows, D), F32),
        scratch_shapes=[pltpu.SemaphoreType.DMA((7,)), pltpu.SemaphoreType.DMA((7,))],
        compiler_params=_params(),
    )(sm)


def pair_share(ghw, ghp, pq):
    def body(ghw_ref, ghp_ref, pq_ref, outw_ref, outp_ref, pqa_ref, send, recv, p_send, p_recv):
        del ghw_ref, ghp_ref
        x, y, c, chips = _place()
        q = 2 * x + y
        refs = (outw_ref, outp_ref)
        n_rows = (2 * outw_ref.shape[1], 2 * outp_ref.shape[1])
        pair = [_remote(refs[a].at[c, rows], refs[a].at[c, rows], send.at[i], recv.at[i], (x, y, 1 - c))
                for i, (a, rows) in enumerate(_half_chunks(0, n_rows, 8))]
        pqa_ref[q] = pq_ref[...]
        small = [_remote(pq_ref, pqa_ref.at[q], p_send.at[j], p_recv.at[j], (*chips[j], c)) for j in range(3)]
        for cp in pair + small:
            cp.start()
        for i, (a, rows) in enumerate(_half_chunks(0, n_rows, 8)):
            _remote(refs[a].at[1 - c, rows], refs[a].at[1 - c, rows], send.at[i], recv.at[i], (x, y, 1 - c)).wait_recv()
        for cp in small:
            cp.wait_recv()
        for cp in pair + small:
            cp.wait_send()

    return _pallas(
        body, name="pair_share", in_specs=[ANY, ANY, VMEM], out_specs=[ANY, ANY, VMEM],
        out_shape=[jax.ShapeDtypeStruct(ghw.shape, F32), jax.ShapeDtypeStruct(ghp.shape, F32),
                   jax.ShapeDtypeStruct((N_CHIPS, 8, D), F32)],
        scratch_shapes=[pltpu.SemaphoreType.DMA((N_BULK,)), pltpu.SemaphoreType.DMA((N_BULK,)),
                        pltpu.SemaphoreType.DMA((3,)), pltpu.SemaphoreType.DMA((3,))],
        input_output_aliases={0: 0, 1: 1},
        compiler_params=_params(),
    )(ghw, ghp, pq)


def _rows_of(shape):
    size = 1
    for s in shape:
        size *= s
    return -(-size // D)


def _pack(arrs, rows_multiple=8):
    parts = []
    total = 0
    for a in arrs:
        f = a.reshape(-1).astype(F32)
        r = _rows_of(a.shape)
        parts.append(jnp.pad(f, (0, r * D - f.shape[0])))
        total += r
    pad_rows = (-total) % rows_multiple
    if pad_rows:
        parts.append(jnp.zeros((pad_rows * D,), F32))
    return jnp.concatenate(parts).reshape(-1, D)


def _unpack(p, shapes):
    out = []
    r0 = 0
    for shp in shapes:
        r = _rows_of(shp)
        size = 1
        for s in shp:
            size *= s
        out.append(p[r0:r0 + r].reshape(-1)[:size].reshape(shp))
        r0 += r
    return out


WEIGHT_NAMES = ['c_ctx', 'ada_w', 'ada_b', 'norm_g', 'w_in', 'b_in', 'conv_w', 'conv_b', 'conv_ln_g', 'conv_ln_b',
                'conv_proj', 'decay_up_fwd', 'decay_bias_fwd', 'decay_up_bwd', 'decay_bias_bwd', 'gla_norm_g', 'gla_proj',
                'w_out', 'final_norm_g']
SMALL_NAMES = ['c_ctx', 'ada_b', 'norm_g', 'b_in', 'conv_w', 'conv_b', 'conv_ln_g', 'conv_ln_b', 'decay_up_fwd',
               'decay_bias_fwd', 'decay_up_bwd', 'decay_bias_bwd', 'gla_norm_g', 'final_norm_g']


def kernel(x, c, ctx, c_ctx, ada_w, ada_b, norm_g, w_in, b_in, conv_w, conv_b, conv_ln_g, conv_ln_b, conv_proj, decay_up_fwd, decay_bias_fwd, decay_up_bwd, decay_bias_bwd, gla_norm_g, gla_proj, w_out, final_norm_g, loss_target, m_c_ctx, m_ada_w, m_ada_b, m_norm_g, m_w_in, m_b_in, m_conv_w, m_conv_b, m_conv_ln_g, m_conv_ln_b, m_conv_proj, m_decay_up_fwd, m_decay_bias_fwd, m_decay_up_bwd, m_decay_bias_bwd, m_gla_norm_g, m_gla_proj, m_w_out, m_final_norm_g, v_c_ctx, v_ada_w, v_ada_b, v_norm_g, v_w_in, v_b_in, v_conv_w, v_conv_b, v_conv_ln_g, v_conv_ln_b, v_conv_proj, v_decay_up_fwd, v_decay_bias_fwd, v_decay_up_bwd, v_decay_bias_bwd, v_gla_norm_g, v_gla_proj, v_w_out, v_final_norm_g):
    w = dict(c_ctx=c_ctx, ada_w=ada_w, ada_b=ada_b, norm_g=norm_g, w_in=w_in, b_in=b_in, conv_w=conv_w, conv_b=conv_b,
             conv_ln_g=conv_ln_g, conv_ln_b=conv_ln_b, conv_proj=conv_proj, decay_up_fwd=decay_up_fwd,
             decay_bias_fwd=decay_bias_fwd, decay_up_bwd=decay_up_bwd, decay_bias_bwd=decay_bias_bwd,
             gla_norm_g=gla_norm_g, gla_proj=gla_proj, w_out=w_out, final_norm_g=final_norm_g)
    m = dict(c_ctx=m_c_ctx, ada_w=m_ada_w, ada_b=m_ada_b, norm_g=m_norm_g, w_in=m_w_in, b_in=m_b_in, conv_w=m_conv_w,
             conv_b=m_conv_b, conv_ln_g=m_conv_ln_g, conv_ln_b=m_conv_ln_b, conv_proj=m_conv_proj,
             decay_up_fwd=m_decay_up_fwd, decay_bias_fwd=m_decay_bias_fwd, decay_up_bwd=m_decay_up_bwd,
             decay_bias_bwd=m_decay_bias_bwd, gla_norm_g=m_gla_norm_g, gla_proj=m_gla_proj, w_out=m_w_out,
             final_norm_g=m_final_norm_g)
    v = dict(c_ctx=v_c_ctx, ada_w=v_ada_w, ada_b=v_ada_b, norm_g=v_norm_g, w_in=v_w_in, b_in=v_b_in, conv_w=v_conv_w,
             conv_b=v_conv_b, conv_ln_g=v_conv_ln_g, conv_ln_b=v_conv_ln_b, conv_proj=v_conv_proj,
             decay_up_fwd=v_decay_up_fwd, decay_bias_fwd=v_decay_bias_fwd, decay_up_bwd=v_decay_up_bwd,
             decay_bias_bwd=v_decay_bias_bwd, gla_norm_g=v_gla_norm_g, gla_proj=v_gla_proj, w_out=v_w_out,
             final_norm_g=v_final_norm_g)
    n = x.shape[0]
    ax, ay, ac = lax.axis_index("x"), lax.axis_index("y"), lax.axis_index("c")
    q = 2 * ax + ay
    dev = 4 * ax + 2 * ay + ac
    nsh = ada_w.shape[2]

    w_sh = w_in[0].astype(BF16)
    p_sh = jnp.concatenate([conv_proj[0], gla_proj[0], w_out[0]], 0).astype(BF16)
    fp = _pack([conv_w[0], decay_up_fwd[0], decay_up_bwd[0]], F_ROWS)
    c8 = jnp.pad(c, ((0, 8 - n), (0, 0)))
    cctx8 = jnp.pad(c_ctx[None], ((0, 7), (0, 0)))
    adab_sh = lax.dynamic_slice(ada_b, (0, q * nsh), (1, nsh))
    w_all, p_all, fall, call, mall = gather_weights(c8, cctx8, ada_w[0], adab_sh, w_sh, p_sh, fp)

    mod_all = jnp.transpose(mall, (1, 0, 2)).reshape(MOD_ROWS, 3 * D)
    mod_mine = lax.dynamic_slice(mod_all, (8 * dev, 0), (n, 3 * D))
    mod_ctx = mod_all[64:65]
    shift = jnp.concatenate([mod_mine[:, 0:D], mod_ctx[:, 0:D]], 0)[:, None, :]
    scale1 = 1.0 + jnp.concatenate([mod_mine[:, D:2 * D], mod_ctx[:, D:2 * D]], 0)[:, None, :]
    gate = mod_mine[:, 2 * D:3 * D][:, None, :]

    own = lambda i, mine, got: jnp.where(q == i, mine, got)
    g1, g2, g3, g4, g5 = _group_cols(jnp.concatenate([own(i, w_sh, w_all[i]) for i in range(N_CHIPS)], 1))
    p_full = jnp.stack([own(i, p_sh, p_all[i]) for i in range(N_CHIPS)])
    wts = dict(w1=g1, w2=g2, w3=g3, w4=g4, w5=g5,
               conv_proj=p_full[:, 0:256].reshape(D, D), gla_proj=p_full[:, 256:512].reshape(D, D),
               w_out=p_full[:, 512:768].reshape(D, D))
    f_parts = [_unpack(fall[i], [conv_w.shape[1:], decay_up_fwd.shape[1:], decay_up_bwd.shape[1:]]) for i in range(N_CHIPS)]
    conv_w_full = jnp.concatenate([p[0] for p in f_parts], 1)
    upf_full = jnp.concatenate([p[1] for p in f_parts], 1)
    upb_full = jnp.concatenate([p[2] for p in f_parts], 1)
    b1, b2, b3, b4, b5 = _group_cols(b_in)
    small = dict(b1=b1, b2=b2, b3=b3, b4=b4, b5=b5, norm_g=norm_g,
                 conv_w=jnp.pad(conv_w_full, ((0, 1), (0, 0))), conv_b=conv_b, conv_ln_g=conv_ln_g, conv_ln_b=conv_ln_b,
                 upf=_pad_up(upf_full, 0), upb=_pad_up(upb_full, 16), bias_f=decay_bias_fwd, bias_b=decay_bias_bwd,
                 gla_norm_g=gla_norm_g, final_norm_g=final_norm_g[None])

    loss_part, dh, dps, g = local_step(x, ctx, loss_target, (scale1, shift, gate), wts, small)
    loss = lax.psum(loss_part[0, 0], ("x", "y", "c"))

    gw = _ungroup_to_shards([g["w%d" % i] for i in range(1, 6)])
    gp = jnp.concatenate([g["conv_proj"].reshape(N_CHIPS, 256, D), g["gla_proj"].reshape(N_CHIPS, 256, D),
                          g["w_out"].reshape(N_CHIPS, 256, D)], 1)
    gotw, gotp = pair_swap(gw, gp)
    core = ac.astype(jnp.int32).reshape(1)
    chip = q.astype(jnp.int32).reshape(1)
    paw, paw16 = pair_add(core, gw, gotw, name="pair_add_w", tr=128)
    pap, pap16 = pair_add(core, gp, gotp, name="pair_add_p", tr=384)
    d_b_in = _ungroup_cols(*[g["b%d" % i] for i in range(1, 6)])
    early = [d_b_in, g["conv_b"].sum(0), g["conv_ln_g"], g["conv_ln_b"], g["bias_f"], g["bias_b"],
             g["gla_norm_g"], g["final_norm_g"], g["conv_w"].sum(0)[:CONV_K], g["upf"][0:16], g["upb"][16:32]]
    early_shapes = [a.shape for a in early]
    du, rbw, rbp, sall1 = dgrad_multi(dps, [wts["w%d" % i] for i in range(1, 6)], paw16, pap16, _pack(early),
                                      t_all=n * SEQ_ALL, t_lat=n * SEQ, tm=512)
    grad_x2, dshift, dscale, g["norm_g"] = norm_mod_bwd(x.reshape(n * SEQ, D), ctx.reshape(n * NCTX, D), du, dh,
                                                        scale1, norm_g)

    dm_mine = jnp.concatenate([dshift[:n, 0], dscale[:n, 0], g["gate"][:, 0]], -1)
    dm_ctx = jnp.concatenate([dshift[n, 0], dscale[n, 0], jnp.zeros((D,), F32)], -1)
    late = [g["norm_g"], dm_mine, dm_ctx]
    late_shapes = [a.shape for a in late]
    sall2 = gather_small(_pack(late))
    (s_b_in, s_conv_b, s_ln_g, s_ln_b, s_bias_f, s_bias_b, s_gla_g, s_final_g, s_conv_w, s_upf,
     s_upb) = _unpack(sum_devices(sall1, name="sum_devices_early"), early_shapes)
    s_norm_g = _unpack(sum_devices(sall2, name="sum_devices_late"), late_shapes)[0]
    dm_rows = [_unpack(sall2[i], late_shapes)[-2:] for i in range(N_DEV)]
    dm_full = jnp.concatenate(
        [jnp.pad(jnp.stack([r[0] for r in dm_rows]), ((0, 0), (0, 8 - n), (0, 0))).reshape(8 * N_DEV, 3 * D),
         jnp.stack([r[1] for r in dm_rows])], 0)
    dm_shard = lax.dynamic_slice(dm_full, (0, q * nsh), (MOD_ROWS, nsh))
    cctx_rows = jnp.broadcast_to(c_ctx[None], (8, D))
    g_ada_w, g_ada_b, pq = ada_bwd(call, cctx_rows, dm_shard, dm_full, ada_w[0])

    place = jnp.concatenate([chip, core])
    ghw = chip_add(place, paw, rbw, name="chip_add_w", tr=128)
    ghp = chip_add(place, pap, rbp, name="chip_add_p", tr=384)
    gw_mine, gp_mine, pq_all = pair_share(ghw, ghp, pq)
    gp_mine = gp_mine.reshape(768, D)
    g_c_ctx = cctx_grad(pq_all, cctx_rows)[0]

    grads = dict(
        c_ctx=g_c_ctx, ada_w=g_ada_w[None], ada_b=g_ada_b, norm_g=s_norm_g,
        w_in=gw_mine.reshape(1, D, W_IN_SHARD), b_in=s_b_in,
        conv_w=lax.dynamic_slice(s_conv_w, (0, q * 256), (CONV_K, 256))[None], conv_b=s_conv_b,
        conv_ln_g=s_ln_g, conv_ln_b=s_ln_b, conv_proj=gp_mine[0:256][None],
        decay_up_fwd=lax.dynamic_slice(s_upf, (0, q * 128), (16, 128))[None], decay_bias_fwd=s_bias_f,
        decay_up_bwd=lax.dynamic_slice(s_upb, (0, q * 128), (16, 128))[None], decay_bias_bwd=s_bias_b,
        gla_norm_g=s_gla_g, gla_proj=gp_mine[256:512][None], w_out=gp_mine[512:768][None],
        final_norm_g=s_final_g[0])

    delta, new_m, new_v = {}, {}, {}
    for name, tr in [("ada_w", 128), ("w_in", 128), ("conv_proj", 128), ("gla_proj", 128), ("w_out", 128)]:
        delta[name], new_m[name], new_v[name] = adamw2d(w[name], grads[name].reshape(w[name].shape), m[name], v[name],
                                                        name="adamw_" + name, tr=tr)
    shapes = [w[nm].shape for nm in SMALL_NAMES]
    packs = [_pack([src[nm] for nm in SMALL_NAMES]) for src in (w, grads, m, v)]
    d_, m_, v_ = adamw2d(*packs, name="adamw_small", tr=packs[0].shape[0])
    for nm, a, b, cc in zip(SMALL_NAMES, _unpack(d_, shapes), _unpack(m_, shapes), _unpack(v_, shapes)):
        delta[nm], new_m[nm], new_v[nm] = a, b, cc

    grad_x = grad_x2.reshape(x.shape)
    return (loss, grad_x, *[grads[nm].reshape(w[nm].shape) for nm in WEIGHT_NAMES], *[delta[nm] for nm in WEIGHT_NAMES],
            *[new_m[nm] for nm in WEIGHT_NAMES], *[new_v[nm] for nm in WEIGHT_NAMES])
```

```python
import jax
import jax.numpy as jnp
from jax import lax
from jax.experimental import pallas as pl
from jax.experimental.pallas import tpu as pltpu

F32 = jnp.float32
BF16 = jnp.bfloat16
MESH = pl.DeviceIdType.MESH
HI = lax.Precision.HIGHEST

D = 1024
SEQ = 2048
GRID_W = 64
GRID_H = SEQ // GRID_W
NCTX = 256
SEQ_ALL = SEQ + NCTX
EPS = 1e-6
CONV_K = 31
CONV_PAD = CONV_K // 2
HEADS = 4
HEAD_K = 128
HEAD_V = 256
GLA_DK = HEADS * HEAD_K
GATE_TAU = 16.0
Q_SCALE = HEAD_K ** -0.5
CHUNK = 64
NCHUNK = SEQ_ALL // CHUNK
NCHUNK_LAT = SEQ // CHUNK
NCHUNK_CTX = NCHUNK - NCHUNK_LAT
SUB = 64
NSUB = CHUNK // SUB
N_IN = 8224
W3 = 2176
O3_V, O3_Q, O3_K, O3_AB = 0, 1024, 1536, 2048

ADAM_LR, ADAM_B1, ADAM_B2, ADAM_EPS, ADAM_WD, ADAM_STEP = 0.001, 0.9, 0.999, 1e-08, 0.01, 10
VMEM_LIMIT = 56 * 1024 * 1024

N_CHIPS = 4
N_DEV = 8
W_IN_SHARD = N_IN // N_CHIPS
MOD_ROWS = 72


def _pallas(body, **kw):
    return pl.pallas_call(body, **kw)


def _params(sem=None, **kw):
    if sem is not None:
        kw["dimension_semantics"] = sem
    return pltpu.CompilerParams(vmem_limit_bytes=VMEM_LIMIT, **kw)


def _sigmoid(v):
    return 1.0 / (1.0 + jnp.exp(-v))


def _silu(v):
    return v * _sigmoid(v)


def _dsilu(v):
    s = _sigmoid(v)
    return s * (1.0 + v * (1.0 - s))


def _log_sigmoid(v):
    return jnp.minimum(v, 0.0) - jnp.log(1.0 + jnp.exp(-jnp.abs(v)))


def _dot(a, b, dims, precision=None):
    return lax.dot_general(a, b, (dims, ((), ())), preferred_element_type=F32, precision=precision)


def _nn(a, b, precision=None):
    return _dot(a, b, ((1,), (0,)), precision)


def _nt(a, b, precision=None):
    return _dot(a, b, ((1,), (1,)), precision)


def _tn(a, b, precision=None):
    return _dot(a, b, ((0,), (0,)), precision)


def _b16(v):
    return v.astype(BF16)


def matmul_nn(a, b, bias, *, name, m, tm, tn, out_dtype):
    k = a.shape[1]
    n = b.shape[1]
    has_bias = bias is not None

    def body(*refs):
        if has_bias:
            a_ref, b_ref, bias_ref, o_ref = refs
            acc = _nn(a_ref[...], b_ref[...]) + bias_ref[...]
        else:
            a_ref, b_ref, o_ref = refs
            acc = _nn(a_ref[...], b_ref[...])
        o_ref[...] = acc.astype(o_ref.dtype)

    in_specs = [pl.BlockSpec((tm, k), lambda j, i: (i, 0)), pl.BlockSpec((k, tn), lambda j, i: (0, j))]
    args = [a, b]
    if has_bias:
        in_specs.append(pl.BlockSpec((1, tn), lambda j, i: (0, j)))
        args.append(bias)
    return _pallas(
        body, name=name, grid=(n // tn, m // tm), in_specs=in_specs,
        out_specs=pl.BlockSpec((tm, tn), lambda j, i: (i, j)),
        out_shape=jax.ShapeDtypeStruct((m, n), out_dtype),
        compiler_params=_params(("parallel", "parallel")),
    )(*args)


def matmul_nt(a, b, *, name, tm, out_dtype):
    m, k = a.shape
    n = b.shape[0]

    def body(a_ref, b_ref, o_ref):
        o_ref[...] = _nt(a_ref[...], b_ref[...]).astype(o_ref.dtype)

    return _pallas(
        body, name=name, grid=(m // tm,),
        in_specs=[pl.BlockSpec((tm, k), lambda i: (i, 0)), pl.BlockSpec((n, k), lambda i: (0, 0))],
        out_specs=pl.BlockSpec((tm, n), lambda i: (i, 0)),
        out_shape=jax.ShapeDtypeStruct((m, n), out_dtype),
        compiler_params=_params(("parallel",)),
    )(a, b)


def matmul_tn(a, b, *, name, t, tn, tt, colsum=False):
    m = a.shape[1]
    n = b.shape[1]

    def body(a_ref, b_ref, o_ref, *rest):
        @pl.when(pl.program_id(1) == 0)
        def _():
            o_ref[...] = jnp.zeros_like(o_ref)
            if colsum:
                rest[0][...] = jnp.zeros_like(rest[0])
        o_ref[...] += _tn(a_ref[...], b_ref[...])
        if colsum:
            rest[0][...] += jnp.sum(b_ref[...].astype(F32), axis=0, keepdims=True)

    out_specs = [pl.BlockSpec((m, tn), lambda j, s: (0, j))]
    out_shape = [jax.ShapeDtypeStruct((m, n), F32)]
    if colsum:
        out_specs.append(pl.BlockSpec((1, tn), lambda j, s: (0, j)))
        out_shape.append(jax.ShapeDtypeStruct((1, n), F32))
    return _pallas(
        body, name=name, grid=(n // tn, t // tt),
        in_specs=[pl.BlockSpec((tt, m), lambda j, s: (s, 0)), pl.BlockSpec((tt, tn), lambda j, s: (s, j))],
        out_specs=out_specs, out_shape=out_shape,
        compiler_params=_params(("parallel", "arbitrary")),
    )(a, b)


def dgrad_norm_bwd(dps, wts, paw, pap, x2, ctx2, dh, scale1, norm_g, *, tm):
    t, tc = x2.shape[0], ctx2.shape[0]
    t_all = t + tc
    n_lat, n_ctx = t // tm, tc // tm
    n_tiles = n_lat + n_ctx
    n_samples = scale1.shape[0] - 1
    tps = n_lat // n_samples
    n_grp = n_samples + 1
    n_g = len(dps)
    whole = [g for g in range(n_g) if dps[g].shape[0] == t_all]
    latent = [g for g in range(n_g) if dps[g].shape[0] != t_all]

    def body(*refs):
        dp_refs, w_refs = refs[:n_g], refs[n_g:2 * n_g]
        (paw_ref, pap_ref, x_ref, c_ref, dh_ref, sc_ref, g_ref, dx_ref, dsh_ref, dsc_ref, dg_ref, rbw_ref, rbp_ref,
         du_buf, b_send, b_recv) = refs[2 * n_g:]
        i = pl.program_id(0)

        def exchange():
            x, y, c, chips = _place()
            srcs, dsts = (paw_ref, pap_ref), (rbw_ref, rbp_ref)
            n_rows = (2 * paw_ref.shape[1], 2 * pap_ref.shape[1])
            return [_remote(srcs[a].at[2 * cx + cy, rows], dsts[a].at[j, rows],
                            b_send.at[j * N_BULK + k], b_recv.at[j * N_BULK + k], (cx, cy, c))
                    for j, (cx, cy) in enumerate(chips) for k, (a, rows) in enumerate(_half_chunks(0, n_rows, 16))]

        @pl.when(i == 0)
        def _():
            for cp in exchange():
                cp.start()

        acc = None
        for g in whole:
            part = _nt(dp_refs[g][...], w_refs[g][...])
            acc = part if acc is None else acc + part
        du_buf[...] = acc

        @pl.when(i < n_lat)
        def _():
            lat = None
            for g in latent:
                part = _nt(dp_refs[g][...], w_refs[g][...])
                lat = part if lat is None else lat + part
            du_buf[...] += lat

        duv = du_buf[...]
        xv = jnp.where(i < n_lat, x_ref[...], c_ref[...])
        rs = lax.rsqrt(jnp.mean(xv * xv, axis=-1, keepdims=True) + EPS)
        xh = xv * rs
        n = xh * g_ref[...]
        dn = duv * sc_ref[0]
        dxh = dn * g_ref[...]
        dx = rs * (dxh - xh * jnp.mean(dxh * xh, axis=-1, keepdims=True))

        @pl.when(i < n_lat)
        def _():
            dx_ref[...] = dx + dh_ref[...]

        @pl.when((i % tps == 0) & (i <= n_lat))
        def _():
            dsh_ref[...] = jnp.zeros_like(dsh_ref)
            dsc_ref[...] = jnp.zeros_like(dsc_ref)

        @pl.when(i == 0)
        def _():
            dg_ref[...] = jnp.zeros_like(dg_ref)

        dsh_ref[0] += jnp.sum(duv, axis=0, keepdims=True)
        dsc_ref[0] += jnp.sum(duv * n, axis=0, keepdims=True)
        dg_ref[...] += jnp.sum(dn * xh, axis=0, keepdims=True)

        @pl.when(i == n_tiles - 1)
        def _():
            for cp in exchange():
                cp.wait_recv()
            for cp in exchange():
                cp.wait_send()

    lat = lambda i: (jnp.minimum(i, n_lat - 1), 0)
    grp = lambda i: (jnp.minimum(i // tps, n_samples), 0, 0)
    in_specs = []
    for g, dp in enumerate(dps):
        nrow = dp.shape[0] // tm
        in_specs.append(pl.BlockSpec((tm, dp.shape[1]), lambda i, nrow=nrow: (jnp.minimum(i, nrow - 1), 0)))
    for w in wts:
        in_specs.append(pl.BlockSpec(w.shape, lambda i: (0, 0), pipeline_mode=pl.Buffered(1)))
    any_spec = pl.BlockSpec(memory_space=pl.ANY)
    in_specs += [any_spec, any_spec,
                 pl.BlockSpec((tm, D), lat), pl.BlockSpec((tm, D), lambda i: (jnp.maximum(i - n_lat, 0), 0)),
                 pl.BlockSpec((tm, D), lat), pl.BlockSpec((1, 1, D), grp), pl.BlockSpec((1, D), lambda i: (0, 0))]
    return _pallas(
        body, name="dgrad_norm_bwd", grid=(n_tiles,), in_specs=in_specs,
        out_specs=[pl.BlockSpec((tm, D), lat), pl.BlockSpec((1, 1, D), grp), pl.BlockSpec((1, 1, D), grp),
                   pl.BlockSpec((1, D), lambda i: (0, 0)), any_spec, any_spec],
        out_shape=[jax.ShapeDtypeStruct((t, D), F32), jax.ShapeDtypeStruct((n_grp, 1, D), F32),
                   jax.ShapeDtypeStruct((n_grp, 1, D), F32), jax.ShapeDtypeStruct((1, D), F32),
                   jax.ShapeDtypeStruct((3,) + paw.shape[1:], paw.dtype),
                   jax.ShapeDtypeStruct((3,) + pap.shape[1:], pap.dtype)],
        scratch_shapes=[pltpu.VMEM((tm, D), F32), pltpu.SemaphoreType.DMA((3 * N_BULK,)),
                        pltpu.SemaphoreType.DMA((3 * N_BULK,))],
        compiler_params=_params(("arbitrary",)),
    )(*dps, *wts, paw, pap, x2, ctx2, dh, scale1, norm_g)


TM_NORM = 512


def norm_mod_fwd(x2, ctx2, scale1, shift, norm_g):
    t = x2.shape[0]
    n_lat = t // TM_NORM
    assert ctx2.shape[0] == TM_NORM
    n_samples = scale1.shape[0] - 1
    tps = n_lat // n_samples

    def body(x_ref, c_ref, sc_ref, sh_ref, g_ref, u_ref):
        i = pl.program_id(0)
        xv = jnp.where(i < n_lat, x_ref[...], c_ref[...])
        rs = lax.rsqrt(jnp.mean(xv * xv, axis=-1, keepdims=True) + EPS)
        u = xv * rs * g_ref[...] * sc_ref[0] + sh_ref[0]
        u_ref[...] = u.astype(u_ref.dtype)

    grp = lambda i: (jnp.minimum(i // tps, n_samples), 0, 0)
    return _pallas(
        body, name="norm_mod_fwd", grid=(n_lat + 1,),
        in_specs=[pl.BlockSpec((TM_NORM, D), lambda i: (jnp.minimum(i, n_lat - 1), 0)),
                  pl.BlockSpec((TM_NORM, D), lambda i: (0, 0)),
                  pl.BlockSpec((1, 1, D), grp), pl.BlockSpec((1, 1, D), grp),
                  pl.BlockSpec((1, D), lambda i: (0, 0))],
        out_specs=pl.BlockSpec((TM_NORM, D), lambda i: (i, 0)),
        out_shape=jax.ShapeDtypeStruct((t + TM_NORM, D), BF16),
        compiler_params=_params(("parallel",)),
    )(x2, ctx2, scale1, shift, norm_g)


CONV_CB = 256
CONV_NCB = D // CONV_CB
H_OFF = 16


def _conv_pad_shape(vertical):
    if vertical:
        return (GRID_H + 2 * CONV_PAD, GRID_W, CONV_CB)
    return (GRID_H, GRID_W + 2 * H_OFF, CONV_CB)


def _conv_store(pad_ref, img, vertical):
    if vertical:
        pad_ref[pl.ds(CONV_PAD, GRID_H)] = img
    else:
        pad_ref[:, pl.ds(H_OFF, GRID_W), :] = img


def _conv_window(pad_ref, k, vertical, r):
    if vertical:
        return pad_ref[r + k]
    return pad_ref[r, pl.ds(H_OFF - CONV_PAD + k, GRID_W), :]


def _rows(r):
    return pl.ds(pl.multiple_of(r * GRID_W, GRID_W), GRID_W)


def conv_fwd(p1, conv_w, conv_b, n_samples):
    t = n_samples * SEQ

    def make(vertical, prev):
        def body(gv_ref, gg_ref, w_ref, b_ref, *rest):
            o_ref, pad_ref = rest[-2], rest[-1]
            pad_ref[...] = jnp.zeros_like(pad_ref)
            a = gv_ref[...].astype(F32) * _sigmoid(gg_ref[...].astype(F32))
            _conv_store(pad_ref, a.reshape(GRID_H, GRID_W, CONV_CB), vertical)

            def row(r, carry):
                acc = jnp.zeros((GRID_W, CONV_CB), F32) + b_ref[...]
                for k in range(CONV_K):
                    acc = acc + _conv_window(pad_ref, k, vertical, r) * w_ref[pl.ds(k, 1), :]
                o_ref[_rows(r), :] = acc
                return carry

            lax.fori_loop(0, GRID_H, row, 0)

        cb0 = CONV_NCB // 2 if vertical else 0
        in_specs = [pl.BlockSpec((SEQ, CONV_CB), lambda b, j: (b, 2 * (cb0 + j))),
                    pl.BlockSpec((SEQ, CONV_CB), lambda b, j: (b, 2 * (cb0 + j) + 1)),
                    pl.BlockSpec((CONV_K + 1, CONV_CB), lambda b, j: (0, cb0 + j)),
                    pl.BlockSpec((1, CONV_CB), lambda b, j: (0, cb0 + j))]
        args = [p1, p1, conv_w, conv_b]
        aliases = {}
        if prev is not None:
            in_specs.append(pl.BlockSpec(memory_space=pl.ANY))
            args.append(prev)
            aliases = {4: 0}
        return _pallas(
            body, name="conv_fwd_v" if vertical else "conv_fwd_h", grid=(n_samples, CONV_NCB // 2),
            in_specs=in_specs,
            out_specs=pl.BlockSpec((SEQ, CONV_CB), lambda b, j: (b, cb0 + j)),
            out_shape=jax.ShapeDtypeStruct((t, D), F32),
            scratch_shapes=[pltpu.VMEM(_conv_pad_shape(vertical), F32)],
            input_output_aliases=aliases,
            compiler_params=_params(("parallel", "parallel")),
        )(*args)

    return make(True, make(False, None))


def conv_bwd(p1, daconv, conv_w, n_samples):
    t = n_samples * SEQ

    def make(vertical, prev):
        def body(gv_ref, gg_ref, dy_ref, w_ref, *rest):
            dp_ref, dw_ref, db_ref, pad_ref, dpad_ref, da_ref = rest[-6:]
            pad_ref[...] = jnp.zeros_like(pad_ref)
            dpad_ref[...] = jnp.zeros_like(dpad_ref)
            gv = gv_ref[...].astype(F32)
            sg = _sigmoid(gg_ref[...].astype(F32))
            _conv_store(pad_ref, (gv * sg).reshape(GRID_H, GRID_W, CONV_CB), vertical)
            _conv_store(dpad_ref, dy_ref[...].reshape(GRID_H, GRID_W, CONV_CB), vertical)

            def row(r, carry):
                acc = jnp.zeros((GRID_W, CONV_CB), F32)
                for k in range(CONV_K):
                    acc = acc + _conv_window(dpad_ref, CONV_K - 1 - k, vertical, r) * w_ref[pl.ds(k, 1), :]
                da_ref[_rows(r), :] = acc
                return carry

            lax.fori_loop(0, GRID_H, row, 0)
            da = da_ref[...]
            dp_ref[:, pl.ds(0, CONV_CB)] = (da * sg).astype(dp_ref.dtype)
            dp_ref[:, pl.ds(CONV_CB, CONV_CB)] = (da * gv * sg * (1.0 - sg)).astype(dp_ref.dtype)

            for k in range(CONV_K):
                def wrow(r, acc, k=k):
                    return acc + _conv_window(pad_ref, k, vertical, r) * dy_ref[_rows(r), :]
                acc = lax.fori_loop(0, GRID_H, wrow, jnp.zeros((GRID_W, CONV_CB), F32))
                dw_ref[0, pl.ds(k, 1), :] = jnp.sum(acc, axis=0, keepdims=True)
            dw_ref[0, pl.ds(CONV_K, 1), :] = jnp.zeros((1, CONV_CB), F32)
            db_ref[0] = jnp.sum(dy_ref[...], axis=0, keepdims=True)

        cb0 = CONV_NCB // 2 if vertical else 0
        in_specs = [pl.BlockSpec((SEQ, CONV_CB), lambda b, j: (b, 2 * (cb0 + j))),
                    pl.BlockSpec((SEQ, CONV_CB), lambda b, j: (b, 2 * (cb0 + j) + 1)),
                    pl.BlockSpec((SEQ, CONV_CB), lambda b, j: (b, cb0 + j)),
                    pl.BlockSpec((CONV_K + 1, CONV_CB), lambda b, j: (0, cb0 + j))]
        args = [p1, p1, daconv, conv_w]
        aliases = {}
        if prev is not None:
            in_specs += [pl.BlockSpec(memory_space=pl.ANY)] * 3
            args += list(prev)
            aliases = {4: 0, 5: 1, 6: 2}
        return _pallas(
            body, name="conv_bwd_v" if vertical else "conv_bwd_h", grid=(n_samples, CONV_NCB // 2),
            in_specs=in_specs,
            out_specs=[pl.BlockSpec((SEQ, 2 * CONV_CB), lambda b, j: (b, cb0 + j)),
                       pl.BlockSpec((1, CONV_K + 1, CONV_CB), lambda b, j: (b, 0, cb0 + j)),
                       pl.BlockSpec((1, 1, CONV_CB), lambda b, j: (b, 0, cb0 + j))],
            out_shape=[jax.ShapeDtypeStruct((t, 2 * D), BF16),
                       jax.ShapeDtypeStruct((n_samples, CONV_K + 1, D), F32),
                       jax.ShapeDtypeStruct((n_samples, 1, D), F32)],
            scratch_shapes=[pltpu.VMEM(_conv_pad_shape(vertical), F32), pltpu.VMEM(_conv_pad_shape(vertical), F32),
                            pltpu.VMEM((SEQ, CONV_CB), F32)],
            input_output_aliases=aliases,
            compiler_params=_params(("parallel", "parallel")),
        )(*args)

    return make(True, make(False, None))


TM_EW = 256


def ln_gate_fwd(aconv, z, ln_g, ln_b):
    t = aconv.shape[0]

    def body(a_ref, z_ref, g_ref, b_ref, o_ref):
        a = a_ref[...]
        mu = jnp.mean(a, axis=-1, keepdims=True)
        xc = a - mu
        rstd = lax.rsqrt(jnp.mean(xc * xc, axis=-1, keepdims=True) + EPS)
        l = xc * rstd * g_ref[...] + b_ref[...]
        o_ref[...] = (_silu(l) * _silu(z_ref[...].astype(F32))).astype(o_ref.dtype)

    row = pl.BlockSpec((TM_EW, D), lambda i: (i, 0))
    vec = pl.BlockSpec((1, D), lambda i: (0, 0))
    return _pallas(
        body, name="ln_gate_fwd", grid=(t // TM_EW,), in_specs=[row, row, vec, vec], out_specs=row,
        out_shape=jax.ShapeDtypeStruct((t, D), BF16), compiler_params=_params(("parallel",)),
    )(aconv, z, ln_g, ln_b)


def ln_gate_bwd(aconv, z, dac, ln_g, ln_b):
    t = aconv.shape[0]

    def body(a_ref, z_ref, d_ref, g_ref, b_ref, da_ref, dz_ref, dg_ref, db_ref):
        a = a_ref[...]
        zv = z_ref[...].astype(F32)
        dac_v = d_ref[...].astype(F32)
        mu = jnp.mean(a, axis=-1, keepdims=True)
        xc = a - mu
        rstd = lax.rsqrt(jnp.mean(xc * xc, axis=-1, keepdims=True) + EPS)
        xh = xc * rstd
        l = xh * g_ref[...] + b_ref[...]
        dz_ref[...] = (dac_v * _silu(l) * _dsilu(zv)).astype(dz_ref.dtype)
        dl = dac_v * _silu(zv) * _dsilu(l)
        dxh = dl * g_ref[...]
        da_ref[...] = rstd * (dxh - jnp.mean(dxh, axis=-1, keepdims=True)
                              - xh * jnp.mean(dxh * xh, axis=-1, keepdims=True))

        @pl.when(pl.program_id(0) == 0)
        def _():
            dg_ref[...] = jnp.zeros_like(dg_ref)
            db_ref[...] = jnp.zeros_like(db_ref)

        dg_ref[...] += jnp.sum(dl * xh, axis=0, keepdims=True)
        db_ref[...] += jnp.sum(dl, axis=0, keepdims=True)

    row = pl.BlockSpec((TM_EW, D), lambda i: (i, 0))
    vec = pl.BlockSpec((1, D), lambda i: (0, 0))
    return _pallas(
        body, name="ln_gate_bwd", grid=(t // TM_EW,), in_specs=[row, row, row, vec, vec],
        out_specs=[row, row, vec, vec],
        out_shape=[jax.ShapeDtypeStruct((t, D), F32), jax.ShapeDtypeStruct((t, D), BF16),
                   jax.ShapeDtypeStruct((1, D), F32), jax.ShapeDtypeStruct((1, D), F32)],
        compiler_params=_params(("arbitrary",)),
    )(aconv, z, dac, ln_g, ln_b)


TM_PREP = 256
PREP_LAT = SEQ // TM_PREP
PREP_ALL = SEQ_ALL // TM_PREP


def _chunk_tri(n, upper):
    r = lax.broadcasted_iota(jnp.int32, (n, n), 0)
    c = lax.broadcasted_iota(jnp.int32, (n, n), 1)
    same = (r // CHUNK) == (c // CHUNK)
    keep = (c >= r) if upper else (c <= r)
    return jnp.where(same & keep, 1.0, 0.0).astype(F32)


def _chunk_sums(v, upper):
    tri = _chunk_tri(v.shape[0], upper).astype(BF16)
    hi = v.astype(BF16)
    r1 = v - hi.astype(F32)
    mid = r1.astype(BF16)
    lo = (r1 - mid.astype(F32)).astype(BF16)
    return (_nn(tri, hi) + _nn(tri, mid)) + _nn(tri, lo)


def _prep_tile_maps(n_samples):
    n_lat = n_samples * PREP_LAT

    def seq_map(i):
        return jnp.where(i < n_lat, i // PREP_LAT, i - n_lat), jnp.where(i < n_lat, i % PREP_LAT, PREP_LAT)

    return n_lat, seq_map


def gla_prep_fwd(p3, upf, upb, bias_f, bias_b, n_samples):
    n_lat, seq_map = _prep_tile_maps(n_samples)
    n_tiles = n_lat + n_samples

    def body(v_ref, q_ref, k_ref, ab_ref, upf_ref, upb_ref, bf_ref, bb_ref, qo, ko, vo, cf, cb):
        i = pl.program_id(0)
        qo[0] = jnp.where(i < n_lat, q_ref[...].astype(F32) * Q_SCALE, 0.0)
        ko[0] = k_ref[...].astype(F32)
        vo[0] = v_ref[...].astype(F32)
        ab = ab_ref[...].astype(F32)
        gf = _log_sigmoid(_nn(ab, upf_ref[...], HI) + bf_ref[...]) * (1.0 / GATE_TAU)
        gb = _log_sigmoid(_nn(ab, upb_ref[...], HI) + bb_ref[...]) * (1.0 / GATE_TAU)
        cf[0] = _chunk_sums(gf, False)
        cb[0] = _chunk_sums(gb, True)

    def o_spec(w):
        return pl.BlockSpec((1, TM_PREP, w), lambda i: (*seq_map(i), 0))

    full = lambda shape: pl.BlockSpec(shape, lambda i: (0,) * len(shape))
    return _pallas(
        body, name="gla_prep_fwd", grid=(n_tiles,),
        in_specs=[pl.BlockSpec((TM_PREP, 1024), lambda i: (i, O3_V // 1024)),
                  pl.BlockSpec((TM_PREP, 512), lambda i: (i, O3_Q // 512)),
                  pl.BlockSpec((TM_PREP, 512), lambda i: (i, O3_K // 512)),
                  pl.BlockSpec((TM_PREP, 128), lambda i: (i, O3_AB // 128)),
                  full((128, GLA_DK)), full((128, GLA_DK)), full((1, GLA_DK)), full((1, GLA_DK))],
        out_specs=[o_spec(GLA_DK), o_spec(GLA_DK), o_spec(D), o_spec(GLA_DK), o_spec(GLA_DK)],
        out_shape=[jax.ShapeDtypeStruct((n_samples, SEQ_ALL, GLA_DK), F32),
                   jax.ShapeDtypeStruct((n_samples, SEQ_ALL, GLA_DK), F32),
                   jax.ShapeDtypeStruct((n_samples, SEQ_ALL, D), F32),
                   jax.ShapeDtypeStruct((n_samples, SEQ_ALL, GLA_DK), F32),
                   jax.ShapeDtypeStruct((n_samples, SEQ_ALL, GLA_DK), F32)],
        compiler_params=_params(("parallel",)),
    )(p3, p3, p3, p3, upf, upb, bias_f, bias_b)


def gla_prep_bwd(p3, dq_f, dq_b, dk_f, dk_b, dv_f, dv_b, dc_f, dc_b, upf, upb, bias_f, bias_b, n_samples):
    n_lat, seq_map = _prep_tile_maps(n_samples)
    n_tiles = n_lat + n_samples

    def body(ab_ref, dqf, dqb, dkf, dkb, dvf, dvb, dcf, dcb, upf_ref, upb_ref, bf_ref, bb_ref,
             dp_ref, duf_ref, dub_ref, dbf_ref, dbb_ref):
        i = pl.program_id(0)
        dp_ref[:, pl.ds(O3_V, D)] = (dvf[0] + dvb[0]).astype(dp_ref.dtype)
        dq = jnp.where(i < n_lat, (dqf[0] + dqb[0]) * Q_SCALE, 0.0)
        dp_ref[:, pl.ds(O3_Q, GLA_DK)] = dq.astype(dp_ref.dtype)
        dp_ref[:, pl.ds(O3_K, GLA_DK)] = (dkf[0] + dkb[0]).astype(dp_ref.dtype)
        ab = ab_ref[...].astype(F32)
        zf = _nn(ab, upf_ref[...], HI) + bf_ref[...]
        zb = _nn(ab, upb_ref[...], HI) + bb_ref[...]
        dgf = _chunk_sums(dcf[0], True)
        dgb = _chunk_sums(dcb[0], False)
        dzf = dgf * (1.0 / GATE_TAU) * _sigmoid(-zf)
        dzb = dgb * (1.0 / GATE_TAU) * _sigmoid(-zb)
        dab = _nt(dzf, upf_ref[...], HI) + _nt(dzb, upb_ref[...], HI)
        dp_ref[:, pl.ds(O3_AB, 128)] = dab.astype(dp_ref.dtype)

        @pl.when(i == 0)
        def _():
            duf_ref[...] = jnp.zeros_like(duf_ref)
            dub_ref[...] = jnp.zeros_like(dub_ref)
            dbf_ref[...] = jnp.zeros_like(dbf_ref)
            dbb_ref[...] = jnp.zeros_like(dbb_ref)

        duf_ref[...] += _tn(ab, dzf, HI)
        dub_ref[...] += _tn(ab, dzb, HI)
        dbf_ref[...] += jnp.sum(dzf, axis=0, keepdims=True)
        dbb_ref[...] += jnp.sum(dzb, axis=0, keepdims=True)

    def s_spec(w):
        return pl.BlockSpec((1, TM_PREP, w), lambda i: (*seq_map(i), 0))

    full = lambda shape: pl.BlockSpec(shape, lambda i: (0,) * len(shape))
    return _pallas(
        body, name="gla_prep_bwd", grid=(n_tiles,),
        in_specs=[pl.BlockSpec((TM_PREP, 128), lambda i: (i, O3_AB // 128)),
                  s_spec(GLA_DK), s_spec(GLA_DK), s_spec(GLA_DK), s_spec(GLA_DK), s_spec(D), s_spec(D),
                  s_spec(GLA_DK), s_spec(GLA_DK),
                  full((128, GLA_DK)), full((128, GLA_DK)), full((1, GLA_DK)), full((1, GLA_DK))],
        out_specs=[pl.BlockSpec((TM_PREP, W3), lambda i: (i, 0)),
                   full((128, GLA_DK)), full((128, GLA_DK)), full((1, GLA_DK)), full((1, GLA_DK))],
        out_shape=[jax.ShapeDtypeStruct((n_tiles * TM_PREP, W3), BF16),
                   jax.ShapeDtypeStruct((128, GLA_DK), F32), jax.ShapeDtypeStruct((128, GLA_DK), F32),
                   jax.ShapeDtypeStruct((1, GLA_DK), F32), jax.ShapeDtypeStruct((1, GLA_DK), F32)],
        compiler_params=_params(("arbitrary",)),
    )(p3, dq_f, dq_b, dk_f, dk_b, dv_f, dv_b, dc_f, dc_b, upf, upb, bias_f, bias_b)


def _sub_blocks(rev):
    if NSUB == 1:
        return [((0, CHUNK), CHUNK // 2, (0, CHUNK))]
    out = []
    for s in range(NSUB):
        rows = (s * SUB, SUB)
        if rev:
            ref = (s + 1) * SUB if s < NSUB - 1 else None
            cols = (s * SUB, CHUNK - s * SUB)
        else:
            ref = s * SUB - 1 if s > 0 else None
            cols = (0, (s + 1) * SUB)
        out.append((rows, ref, cols))
    return out


def _sub_mask(rows, cols, rev):
    r = rows[0] + lax.broadcasted_iota(jnp.int32, (rows[1], cols[1]), 0)
    c = cols[0] + lax.broadcasted_iota(jnp.int32, (rows[1], cols[1]), 1)
    return (c >= r) if rev else (c <= r)


def _sub_operands(qc, kc, cc, rows, ref, cols):
    cref = jnp.zeros((1, HEAD_K), F32) if ref is None else cc[ref:ref + 1]
    eq = jnp.exp(cc[rows[0]:rows[0] + rows[1]] - cref)
    ek = jnp.exp(cref - cc[cols[0]:cols[0] + cols[1]])
    qs = qc[rows[0]:rows[0] + rows[1]] * eq
    kk = kc[cols[0]:cols[0] + cols[1]] * ek
    return qs, kk, eq, ek


SCAN_ROWS = 256
SCAN_CHUNKS = SCAN_ROWS // CHUNK
SCAN_STEPS = SEQ_ALL // SCAN_ROWS
LAT_BLOCKS = SEQ // SCAN_ROWS


def _scan_block(t, rev):
    if rev:
        return SCAN_STEPS - 1 - t
    return jnp.where(t == 0, SCAN_STEPS - 1, t - 1)


def _scan_lat_block(t, rev):
    if rev:
        return jnp.minimum(SCAN_STEPS - 1 - t, LAT_BLOCKS - 1)
    return jnp.maximum(t - 1, 0)


def _head_cols(h):
    return pl.ds(h * HEAD_K, HEAD_K), pl.ds(h * HEAD_V, HEAD_V)


def gla_scan_fwd(q, k, v, cum, *, rev, name):
    n = q.shape[0]

    def body(q_ref, k_ref, v_ref, c_ref, o_ref, s_ref, sfin_ref, st):
        t = pl.program_id(1)

        @pl.when(t == 0)
        def _():
            st[...] = jnp.zeros_like(st)

        def chunk(j, carry):
            lj = SCAN_CHUNKS - 1 - j if rev else j
            r0 = pl.multiple_of(lj * CHUNK, CHUNK)
            rws = pl.ds(r0, CHUNK)
            for h in range(HEADS):
                kcols, vcols = _head_cols(h)
                qc, kc, cc = q_ref[0, rws, kcols], k_ref[0, rws, kcols], c_ref[0, rws, kcols]
                vc = v_ref[0, rws, vcols]
                s_in = st[h]
                s_ref[0, h, j] = s_in
                edge = cc[0:1] if rev else cc[CHUNK - 1:CHUNK]
                ke = kc * jnp.exp(edge - cc)
                st[h] = s_in * jnp.exp(edge) + _tn(_b16(vc), _b16(ke))
                o_inter = _nt(_b16(qc * jnp.exp(cc)), _b16(s_in))
                vb = _b16(vc)
                for rows, ref, cols in _sub_blocks(rev):
                    qs, kk, _, _ = _sub_operands(qc, kc, cc, rows, ref, cols)
                    a = jnp.where(_sub_mask(rows, cols, rev), _nt(_b16(qs), _b16(kk)), 0.0)
                    o_s = _nn(_b16(a), vb[cols[0]:cols[0] + cols[1]])
                    o_ref[0, pl.ds(r0 + rows[0], rows[1]), vcols] = o_inter[rows[0]:rows[0] + rows[1]] + o_s
            return carry

        lax.fori_loop(0, SCAN_CHUNKS, chunk, 0)

        @pl.when(t == SCAN_STEPS - 1)
        def _():
            sfin_ref[0] = st[...]

    def spec(w):
        return pl.BlockSpec((1, SCAN_ROWS, w), lambda b, t: (b, _scan_block(t, rev), 0))

    return _pallas(
        body, name=name, grid=(n, SCAN_STEPS),
        in_specs=[spec(GLA_DK), spec(GLA_DK), spec(D), spec(GLA_DK)],
        out_specs=[pl.BlockSpec((1, SCAN_ROWS, D), lambda b, t: (b, _scan_lat_block(t, rev), 0)),
                   pl.BlockSpec((1, HEADS, SCAN_CHUNKS, HEAD_V, HEAD_K), lambda b, t: (b, 0, t, 0, 0)),
                   pl.BlockSpec((1, HEADS, HEAD_V, HEAD_K), lambda b, t: (b, 0, 0, 0))],
        out_shape=[jax.ShapeDtypeStruct((n, SEQ, D), F32),
                   jax.ShapeDtypeStruct((n, HEADS, NCHUNK, HEAD_V, HEAD_K), F32),
                   jax.ShapeDtypeStruct((n, HEADS, HEAD_V, HEAD_K), F32)],
        scratch_shapes=[pltpu.VMEM((HEADS, HEAD_V, HEAD_K), F32)],
        compiler_params=_params(("parallel", "arbitrary")),
    )(q, k, v, cum)


def gla_scan_bwd(q, k, v, cum, s_all, s_fin, do, *, rev, name):
    n = q.shape[0]

    def body(q_ref, k_ref, v_ref, c_ref, s_ref, sfin_ref, do_ref, dq_ref, dk_ref, dv_ref, dc_ref,
             dst, s_next, dq_acc, dk_acc, dv_acc):
        t = SCAN_STEPS - 1 - pl.program_id(1)

        @pl.when(pl.program_id(1) == 0)
        def _():
            dst[...] = jnp.zeros_like(dst)
            s_next[...] = sfin_ref[0]

        def chunk(jj, carry):
            j = SCAN_CHUNKS - 1 - jj
            lj = SCAN_CHUNKS - 1 - j if rev else j
            rws = pl.ds(pl.multiple_of(lj * CHUNK, CHUNK), CHUNK)
            for h in range(HEADS):
                kcols, vcols = _head_cols(h)
                qc, kc, cc = q_ref[0, rws, kcols], k_ref[0, rws, kcols], c_ref[0, rws, kcols]
                vc = v_ref[0, rws, vcols]
                doc = jnp.where(t > 0, do_ref[0, rws, vcols], 0.0)
                s_in = s_ref[0, h, j]
                s_out = s_next[h]
                ds_out = dst[h]
                edge = cc[0:1] if rev else cc[CHUNK - 1:CHUNK]
                e_q = jnp.exp(cc)
                e_k = jnp.exp(edge - cc)
                dob = _b16(doc)
                dsb = _b16(ds_out)
                dst[h] = ds_out * jnp.exp(edge) + _tn(dob, _b16(qc * e_q))
                s_next[h] = s_in
                dq_acc[h] = e_q * _nn(dob, _b16(s_in))
                dk_acc[h] = e_k * _nn(_b16(vc), dsb)
                dv_acc[h] = _nt(_b16(kc * e_k), dsb)
                vb = _b16(vc)
                for rows, ref, cols in _sub_blocks(rev):
                    qs, kk, eq, ek = _sub_operands(qc, kc, cc, rows, ref, cols)
                    mask = _sub_mask(rows, cols, rev)
                    rsl = slice(rows[0], rows[0] + rows[1])
                    csl = pl.ds(cols[0], cols[1])
                    qsb, kkb = _b16(qs), _b16(kk)
                    a = jnp.where(mask, _nt(qsb, kkb), 0.0)
                    da = _b16(jnp.where(mask, _nt(dob[rsl], vb[cols[0]:cols[0] + cols[1]]), 0.0))
                    dq_acc[h, pl.ds(rows[0], rows[1]), :] += _nn(da, kkb) * eq
                    dk_acc[h, csl, :] += _tn(da, qsb) * ek
                    dv_acc[h, csl, :] += _tn(_b16(a), dob[rsl])
                dq = dq_acc[h]
                dk = dk_acc[h]
                dc = qc * dq - kc * dk
                bnd = jnp.sum(ds_out * s_out, axis=0, keepdims=True)
                edge_row = 0 if rev else CHUNK - 1
                is_edge = lax.broadcasted_iota(jnp.int32, (CHUNK, HEAD_K), 0) == edge_row
                dq_ref[0, rws, kcols] = dq
                dk_ref[0, rws, kcols] = dk
                dv_ref[0, rws, vcols] = dv_acc[h]
                dc_ref[0, rws, kcols] = dc + jnp.where(is_edge, bnd, 0.0)
            return carry

        lax.fori_loop(0, SCAN_CHUNKS, chunk, 0)

    def step_of(u):
        return SCAN_STEPS - 1 - u

    def spec(w):
        return pl.BlockSpec((1, SCAN_ROWS, w), lambda b, u: (b, _scan_block(step_of(u), rev), 0))

    return _pallas(
        body, name=name, grid=(n, SCAN_STEPS),
        in_specs=[spec(GLA_DK), spec(GLA_DK), spec(D), spec(GLA_DK),
                  pl.BlockSpec((1, HEADS, SCAN_CHUNKS, HEAD_V, HEAD_K), lambda b, u: (b, 0, step_of(u), 0, 0)),
                  pl.BlockSpec((1, HEADS, HEAD_V, HEAD_K), lambda b, u: (b, 0, 0, 0)),
                  pl.BlockSpec((1, SCAN_ROWS, D), lambda b, u: (b, _scan_lat_block(step_of(u), rev), 0))],
        out_specs=[spec(GLA_DK), spec(GLA_DK), spec(D), spec(GLA_DK)],
        out_shape=[jax.ShapeDtypeStruct((n, SEQ_ALL, GLA_DK), F32), jax.ShapeDtypeStruct((n, SEQ_ALL, GLA_DK), F32),
                   jax.ShapeDtypeStruct((n, SEQ_ALL, D), F32), jax.ShapeDtypeStruct((n, SEQ_ALL, GLA_DK), F32)],
        scratch_shapes=[pltpu.VMEM((HEADS, HEAD_V, HEAD_K), F32), pltpu.VMEM((HEADS, HEAD_V, HEAD_K), F32),
                        pltpu.VMEM((HEADS, CHUNK, HEAD_K), F32), pltpu.VMEM((HEADS, CHUNK, HEAD_K), F32),
                        pltpu.VMEM((HEADS, CHUNK, HEAD_V), F32)],
        compiler_params=_params(("parallel", "arbitrary")),
    )(q, k, v, cum, s_all, s_fin, do)


def gla_out_fwd(o_f, o_b, r, gnorm):
    n = o_f.shape[0]
    tiles = SEQ // TM_EW

    def body(of_ref, ob_ref, r_ref, g_ref, og_ref):
        for h in range(HEADS):
            cols = pl.ds(h * HEAD_V, HEAD_V)
            o = of_ref[0, :, cols] + ob_ref[0, :, cols]
            rs = lax.rsqrt(jnp.mean(o * o, axis=-1, keepdims=True) + EPS)
            og_ref[:, cols] = (o * rs * g_ref[...] * _silu(r_ref[:, cols].astype(F32))).astype(og_ref.dtype)

    ospec = pl.BlockSpec((1, TM_EW, D), lambda b, j: (b, j, 0))
    row = pl.BlockSpec((TM_EW, D), lambda b, j: (b * tiles + j, 0))
    return _pallas(
        body, name="gla_out_fwd", grid=(n, tiles),
        in_specs=[ospec, ospec, row, pl.BlockSpec((1, HEAD_V), lambda b, j: (0, 0))],
        out_specs=row, out_shape=jax.ShapeDtypeStruct((n * SEQ, D), BF16),
        compiler_params=_params(("parallel", "parallel")),
    )(o_f, o_b, r, gnorm)


def gla_out_bwd(o_f, o_b, r, dog, gnorm):
    n = o_f.shape[0]
    tiles = SEQ // TM_EW

    def body(of_ref, ob_ref, r_ref, d_ref, g_ref, do_ref, dr_ref, dg_ref):
        @pl.when((pl.program_id(0) == 0) & (pl.program_id(1) == 0))
        def _():
            dg_ref[...] = jnp.zeros_like(dg_ref)

        for h in range(HEADS):
            cols = pl.ds(h * HEAD_V, HEAD_V)
            o = of_ref[0, :, cols] + ob_ref[0, :, cols]
            rv = r_ref[:, cols].astype(F32)
            dv = d_ref[:, cols].astype(F32)
            rs = lax.rsqrt(jnp.mean(o * o, axis=-1, keepdims=True) + EPS)
            oh = o * rs
            dr_ref[:, cols] = (dv * oh * g_ref[...] * _dsilu(rv)).astype(dr_ref.dtype)
            dn = dv * _silu(rv)
            dg_ref[...] += jnp.sum(dn * oh, axis=0, keepdims=True)
            doh = dn * g_ref[...]
            do_ref[0, :, cols] = rs * (doh - oh * jnp.mean(doh * oh, axis=-1, keepdims=True))

    ospec = pl.BlockSpec((1, TM_EW, D), lambda b, j: (b, j, 0))
    row = pl.BlockSpec((TM_EW, D), lambda b, j: (b * tiles + j, 0))
    vec = pl.BlockSpec((1, HEAD_V), lambda b, j: (0, 0))
    return _pallas(
        body, name="gla_out_bwd", grid=(n, tiles),
        in_specs=[ospec, ospec, row, row, vec],
        out_specs=[ospec, row, vec],
        out_shape=[jax.ShapeDtypeStruct((n, SEQ, D), F32), jax.ShapeDtypeStruct((n * SEQ, D), BF16),
                   jax.ShapeDtypeStruct((1, HEAD_V), F32)],
        compiler_params=_params(("arbitrary", "arbitrary")),
    )(o_f, o_b, r, dog, gnorm)


def merge_fwd(p5, y_conv, y_gla):
    t = y_conv.shape[0]

    def body(mc_ref, mg_ref, yc_ref, yg_ref, o_ref):
        f = lambda ref: ref[...].astype(F32)
        o_ref[...] = (_sigmoid(f(mc_ref)) * f(yc_ref) + _sigmoid(f(mg_ref)) * f(yg_ref)).astype(o_ref.dtype)

    row = pl.BlockSpec((TM_EW, D), lambda i: (i, 0))
    return _pallas(
        body, name="merge_fwd", grid=(t // TM_EW,),
        in_specs=[row, pl.BlockSpec((TM_EW, D), lambda i: (i, 1)), row, row], out_specs=row,
        out_shape=jax.ShapeDtypeStruct((t, D), BF16), compiler_params=_params(("parallel",)),
    )(p5, p5, y_conv, y_gla)


def merge_bwd(p5, y_conv, y_gla, dmerged):
    t = y_conv.shape[0]

    def body(mc_ref, mg_ref, yc_ref, yg_ref, d_ref, dyc_ref, dyg_ref, dp_ref):
        f = lambda ref: ref[...].astype(F32)
        d = f(d_ref)
        sc = _sigmoid(f(mc_ref))
        sg = _sigmoid(f(mg_ref))
        dyc_ref[...] = (d * sc).astype(dyc_ref.dtype)
        dyg_ref[...] = (d * sg).astype(dyg_ref.dtype)
        dp_ref[:, pl.ds(0, D)] = (d * f(yc_ref) * sc * (1.0 - sc)).astype(dp_ref.dtype)
        dp_ref[:, pl.ds(D, D)] = (d * f(yg_ref) * sg * (1.0 - sg)).astype(dp_ref.dtype)

    row = pl.BlockSpec((TM_EW, D), lambda i: (i, 0))
    return _pallas(
        body, name="merge_bwd", grid=(t // TM_EW,),
        in_specs=[row, pl.BlockSpec((TM_EW, D), lambda i: (i, 1)), row, row, row],
        out_specs=[row, row, pl.BlockSpec((TM_EW, 2 * D), lambda i: (i, 0))],
        out_shape=[jax.ShapeDtypeStruct((t, D), BF16), jax.ShapeDtypeStruct((t, D), BF16),
                   jax.ShapeDtypeStruct((t, 2 * D), BF16)],
        compiler_params=_params(("parallel",)),
    )(p5, p5, y_conv, y_gla, dmerged)


def final_fwd_bwd(x2, mo, gate, final_g, target, n_samples):
    t = x2.shape[0]
    tiles = SEQ // TM_EW

    def body(x_ref, mo_ref, gate_ref, g_ref, t_ref, dh_ref, dmo_ref, dgate_ref, dg_ref, loss_ref):
        b, j = pl.program_id(0), pl.program_id(1)
        mo_v = mo_ref[...]
        h = x_ref[...] + gate_ref[0] * mo_v
        rs = lax.rsqrt(jnp.mean(h * h, axis=-1, keepdims=True) + EPS)
        nh = h * rs
        err = nh * g_ref[...] - t_ref[...]
        dy = err * (1.0 / D)
        dn = dy * g_ref[...]
        dh = rs * (dn - nh * jnp.mean(dn * nh, axis=-1, keepdims=True))
        dh_ref[...] = dh
        dmo_ref[...] = (dh * gate_ref[0]).astype(dmo_ref.dtype)

        @pl.when(j == 0)
        def _():
            dgate_ref[...] = jnp.zeros_like(dgate_ref)

        @pl.when((b == 0) & (j == 0))
        def _():
            dg_ref[...] = jnp.zeros_like(dg_ref)
            loss_ref[...] = jnp.zeros_like(loss_ref)

        dgate_ref[0] += jnp.sum(dh * mo_v, axis=0, keepdims=True)
        dg_ref[...] += jnp.sum(dy * nh, axis=0, keepdims=True)
        loss_ref[...] += (0.5 / D) * jnp.sum(err * err)

    row = pl.BlockSpec((TM_EW, D), lambda b, j: (b * tiles + j, 0))
    per = pl.BlockSpec((1, 1, D), lambda b, j: (b, 0, 0))
    vec = pl.BlockSpec((1, D), lambda b, j: (0, 0))
    return _pallas(
        body, name="final_fwd_bwd", grid=(n_samples, tiles),
        in_specs=[row, row, per, vec, row],
        out_specs=[row, row, per, vec, pl.BlockSpec((8, 128), lambda b, j: (0, 0))],
        out_shape=[jax.ShapeDtypeStruct((t, D), F32), jax.ShapeDtypeStruct((t, D), BF16),
                   jax.ShapeDtypeStruct((n_samples, 1, D), F32), jax.ShapeDtypeStruct((1, D), F32),
                   jax.ShapeDtypeStruct((8, 128), F32)],
        compiler_params=_params(("arbitrary", "arbitrary")),
    )(x2, mo, gate, final_g, target)


def local_step(x, ctx, target, mod, wts, small):
    n = x.shape[0]
    t = n * SEQ
    t_all = t + n * NCTX
    x2 = x.reshape(t, D)
    ctx2 = ctx.reshape(n * NCTX, D)
    tgt2 = target.reshape(t, D)
    scale1, shift, gate = mod

    u = norm_mod_fwd(x2, ctx2, scale1, shift, small["norm_g"])
    p1 = matmul_nn(u, wts["w1"], small["b1"], name="proj_conv", m=t, tm=1024, tn=1024, out_dtype=BF16)
    p2 = matmul_nn(u, wts["w2"], small["b2"], name="proj_z", m=t, tm=1024, tn=1024, out_dtype=BF16)
    p3 = matmul_nn(u, wts["w3"], small["b3"], name="proj_gla", m=t_all, tm=512, tn=W3, out_dtype=BF16)
    p4 = matmul_nn(u, wts["w4"], small["b4"], name="proj_r", m=t, tm=1024, tn=1024, out_dtype=BF16)
    p5 = matmul_nn(u, wts["w5"], small["b5"], name="proj_merge", m=t, tm=1024, tn=1024, out_dtype=BF16)

    aconv = conv_fwd(p1, small["conv_w"], small["conv_b"], n)
    ac = ln_gate_fwd(aconv, p2, small["conv_ln_g"], small["conv_ln_b"])
    y_conv = matmul_nn(ac, wts["conv_proj"], None, name="conv_proj_fwd", m=t, tm=1024, tn=1024, out_dtype=BF16)

    qs, ks, vs, cum_f, cum_b = gla_prep_fwd(p3, small["upf"], small["upb"], small["bias_f"], small["bias_b"], n)
    o_f, s_f, sfin_f = gla_scan_fwd(qs, ks, vs, cum_f, rev=False, name="gla_scan_fwd_f")
    o_b, s_b, sfin_b = gla_scan_fwd(qs, ks, vs, cum_b, rev=True, name="gla_scan_fwd_b")
    og = gla_out_fwd(o_f, o_b, p4, small["gla_norm_g"])
    y_gla = matmul_nn(og, wts["gla_proj"], None, name="gla_proj_fwd", m=t, tm=1024, tn=1024, out_dtype=BF16)

    merged = merge_fwd(p5, y_conv, y_gla)
    mo = matmul_nn(merged, wts["w_out"], None, name="w_out_fwd", m=t, tm=1024, tn=1024, out_dtype=F32)
    dh, dmo, dgate, d_final_g, loss = final_fwd_bwd(x2, mo, gate, small["final_norm_g"], tgt2, n)

    g = {"final_norm_g": d_final_g}
    dmerged = matmul_nt(dmo, wts["w_out"], name="w_out_dgrad", tm=512, out_dtype=BF16)
    g["w_out"] = matmul_tn(merged, dmo, name="w_out_wgrad", t=t, tn=1024, tt=1024)[0]
    dyc, dyg, dp5 = merge_bwd(p5, y_conv, y_gla, dmerged)

    dac = matmul_nt(dyc, wts["conv_proj"], name="conv_proj_dgrad", tm=512, out_dtype=BF16)
    g["conv_proj"] = matmul_tn(ac, dyc, name="conv_proj_wgrad", t=t, tn=1024, tt=1024)[0]
    daconv, dp2, g["conv_ln_g"], g["conv_ln_b"] = ln_gate_bwd(aconv, p2, dac, small["conv_ln_g"], small["conv_ln_b"])
    dp1, dconv_w, dconv_b = conv_bwd(p1, daconv, small["conv_w"], n)
    g["conv_w"], g["conv_b"] = dconv_w, dconv_b

    dog = matmul_nt(dyg, wts["gla_proj"], name="gla_proj_dgrad", tm=512, out_dtype=BF16)
    g["gla_proj"] = matmul_tn(og, dyg, name="gla_proj_wgrad", t=t, tn=1024, tt=1024)[0]
    do, dp4, g["gla_norm_g"] = gla_out_bwd(o_f, o_b, p4, dog, small["gla_norm_g"])
    dq_f, dk_f, dv_f, dc_f = gla_scan_bwd(qs, ks, vs, cum_f, s_f, sfin_f, do, rev=False, name="gla_scan_bwd_f")
    dq_b, dk_b, dv_b, dc_b = gla_scan_bwd(qs, ks, vs, cum_b, s_b, sfin_b, do, rev=True, name="gla_scan_bwd_b")
    dp3, g["upf"], g["upb"], g["bias_f"], g["bias_b"] = gla_prep_bwd(
        p3, dq_f, dq_b, dk_f, dk_b, dv_f, dv_b, dc_f, dc_b,
        small["upf"], small["upb"], small["bias_f"], small["bias_b"], n)

    dps = [dp1, dp2, dp3, dp4, dp5]
    for i, dp in enumerate(dps):
        rows = dp.shape[0]
        tn = W3 if dp.shape[1] == W3 else 1024
        g["w%d" % (i + 1)], g["b%d" % (i + 1)] = matmul_tn(
            u, dp, name="w_in_wgrad_%d" % (i + 1), t=rows, tn=tn, tt=1024 if rows % 1024 == 0 else 768, colsum=True)
    g["gate"] = dgate
    return loss, dh, dps, g


def _group_cols(w):
    gv, gg, z = w[..., 0:1024], w[..., 1024:2048], w[..., 2048:3072]
    q, k, v = w[..., 3072:3584], w[..., 3584:4096], w[..., 4096:5120]
    ab = w[..., 5120:5152]
    r, mc, mg = w[..., 5152:6176], w[..., 6176:7200], w[..., 7200:8224]
    g1 = jnp.concatenate([p for j in range(CONV_NCB)
                          for p in (gv[..., CONV_CB * j:CONV_CB * (j + 1)], gg[..., CONV_CB * j:CONV_CB * (j + 1)])], -1)
    pad = jnp.zeros(w.shape[:-1] + (W3 - 2080,), w.dtype)
    g3 = jnp.concatenate([v, q, k, ab, pad], -1)
    return g1, z, g3, r, jnp.concatenate([mc, mg], -1)


def _ungroup_cols(g1, g2, g3, g4, g5):
    gv = jnp.concatenate([g1[..., 2 * CONV_CB * j:2 * CONV_CB * j + CONV_CB] for j in range(CONV_NCB)], -1)
    gg = jnp.concatenate([g1[..., 2 * CONV_CB * j + CONV_CB:2 * CONV_CB * (j + 1)] for j in range(CONV_NCB)], -1)
    v, q, k, ab = g3[..., 0:1024], g3[..., 1024:1536], g3[..., 1536:2048], g3[..., 2048:2080]
    return jnp.concatenate([gv, gg, g2, q, k, v, ab, g4, g5[..., 0:1024], g5[..., 1024:2048]], -1)


def _natural_pieces():
    pieces = [(CONV_CB * j, CONV_CB, 0, 2 * CONV_CB * j) for j in range(CONV_NCB)]
    pieces += [(1024 + CONV_CB * j, CONV_CB, 0, 2 * CONV_CB * j + CONV_CB) for j in range(CONV_NCB)]
    pieces += [(2048, 1024, 1, 0), (3072, 512, 2, O3_Q), (3584, 512, 2, O3_K), (4096, 1024, 2, O3_V), (5120, 32, 2, O3_AB),
               (5152, 1024, 3, 0), (6176, 1024, 4, 0), (7200, 1024, 4, 1024)]
    return sorted(pieces)


def _ungroup_to_shards(groups):
    shards = []
    for i in range(N_CHIPS):
        lo, hi = i * W_IN_SHARD, (i + 1) * W_IN_SHARD
        parts = []
        for nat, width, g, gcol in _natural_pieces():
            a, b = max(nat, lo), min(nat + width, hi)
            if a < b:
                parts.append(groups[g][:, gcol + a - nat:gcol + b - nat])
        shards.append(jnp.concatenate(parts, 1))
    return jnp.stack(shards)


def _pad_up(up, row0):
    return jnp.zeros((128, GLA_DK), F32).at[row0:row0 + up.shape[0]].set(up)


def _adamw_math(w, g, m, v):
    m = ADAM_B1 * m + (1.0 - ADAM_B1) * g
    v = ADAM_B2 * v + (1.0 - ADAM_B2) * (g * g)
    m_hat = m / (1.0 - ADAM_B1 ** ADAM_STEP)
    v_hat = v / (1.0 - ADAM_B2 ** ADAM_STEP)
    delta = -ADAM_LR * (m_hat / (jnp.sqrt(v_hat) + ADAM_EPS) + ADAM_WD * w)
    return delta, m, v


def adamw2d(w, g, m, v, *, name, tr, tcols=None):
    rows, cols = w.shape[-2:]

    def body(w_ref, g_ref, m_ref, v_ref, d_ref, nm_ref, nv_ref):
        d_ref[...], nm_ref[...], nv_ref[...] = _adamw_math(w_ref[...], g_ref[...], m_ref[...], v_ref[...])

    tcols = cols if tcols is None else tcols
    if w.ndim == 3:
        spec = pl.BlockSpec((1, tr, tcols), lambda i, j: (0, i, j))
    else:
        spec = pl.BlockSpec((tr, tcols), lambda i, j: (i, j))
    return _pallas(
        body, name=name, grid=(rows // tr, cols // tcols), in_specs=[spec] * 4, out_specs=[spec] * 3,
        out_shape=[jax.ShapeDtypeStruct(w.shape, F32)] * 3, compiler_params=_params(("parallel", "parallel")),
    )(w, g, m, v)


def sum_devices(sall, *, name):
    rows = sall.shape[1]

    def body(s_ref, o_ref):
        acc = s_ref[0]
        for d in range(1, N_DEV):
            acc = acc + s_ref[d]
        o_ref[...] = acc

    return _pallas(body, name=name, out_shape=jax.ShapeDtypeStruct((rows, D), F32),
                   compiler_params=_params())(sall)


def pair_add(core, g, got, *, name, tr):
    n, rows, cols = got.shape
    g4 = g.reshape(n, 2, rows, cols)

    def body(core_ref, g_ref, got_ref, o_ref, ob_ref):
        del core_ref
        s = g_ref[0, 0] + got_ref[0]
        o_ref[0] = s
        ob_ref[0] = s.astype(BF16)

    spec = pl.BlockSpec((1, tr, cols), lambda i, t, core_ref: (i, t, 0))
    return _pallas(
        body, name=name,
        grid_spec=pltpu.PrefetchScalarGridSpec(
            num_scalar_prefetch=1, grid=(n, rows // tr),
            in_specs=[pl.BlockSpec((1, 1, tr, cols), lambda i, t, core_ref: (i, core_ref[0], t, 0)), spec],
            out_specs=[spec, spec]),
        out_shape=[jax.ShapeDtypeStruct(got.shape, F32), jax.ShapeDtypeStruct(got.shape, BF16)],
        compiler_params=_params(("parallel", "parallel")))(core, g4, got)


def chip_add(place, pa, rb, *, name, tr):
    _, rows, cols = pa.shape

    def body(place_ref, m_ref, r_ref, o_ref):
        del place_ref
        o_ref[0] = ((m_ref[0] + r_ref[0].astype(F32)) + r_ref[1].astype(F32)) + r_ref[2].astype(F32)

    return _pallas(
        body, name=name,
        grid_spec=pltpu.PrefetchScalarGridSpec(
            num_scalar_prefetch=1, grid=(rows // tr,),
            in_specs=[pl.BlockSpec((1, tr, cols), lambda t, place_ref: (place_ref[0], t, 0)),
                      pl.BlockSpec((3, tr, cols), lambda t, place_ref: (0, t, 0))],
            out_specs=pl.BlockSpec((1, tr, cols), lambda t, place_ref: (place_ref[1], t, 0))),
        out_shape=jax.ShapeDtypeStruct((2, rows, cols), F32),
        compiler_params=_params(("parallel",)))(place, pa, rb)


def ada_bwd(call, cctx_rows, dm_shard, dm_full, adaw):
    nsh = adaw.shape[1]

    def body(c_ref, cc_ref, dms_ref, dmf_ref, w_ref, gw_ref, gb_ref, pq_ref):
        a_lat = _silu(c_ref[...])
        a_ctx = _silu(cc_ref[...])
        dms = dms_ref[...]
        gw_ref[...] = _tn(a_lat, dms[0:64], HI) + _tn(a_ctx, dms[64:72], HI)
        gb_ref[...] = jnp.sum(dmf_ref[...], axis=0, keepdims=True)
        part = _nt(dms[64:72], w_ref[...], HI)
        pq_ref[...] = jnp.zeros_like(pq_ref) + jnp.sum(part, axis=0, keepdims=True)

    return _pallas(body, name="ada_bwd",
                   out_shape=[jax.ShapeDtypeStruct((D, nsh), F32), jax.ShapeDtypeStruct((1, 3 * D), F32),
                              jax.ShapeDtypeStruct((8, D), F32)],
                   compiler_params=_params())(call, cctx_rows, dm_shard, dm_full, adaw)


def cctx_grad(pq_all, cctx_rows):
    def body(p_ref, c_ref, o_ref):
        acc = p_ref[0]
        for qi in range(1, N_CHIPS):
            acc = acc + p_ref[qi]
        o_ref[...] = acc * _dsilu(c_ref[...])

    return _pallas(body, name="cctx_grad", out_shape=jax.ShapeDtypeStruct((8, D), F32),
                   compiler_params=_params())(pq_all, cctx_rows)


def _place():
    x, y, c = lax.axis_index("x"), lax.axis_index("y"), lax.axis_index("c")
    chips = [(1 - x, y), (x, 1 - y), (1 - x, 1 - y)]
    return x, y, c, chips


def _all_peers(x, y, c):
    return [((1 - x) if r & 4 else x, (1 - y) if r & 2 else y, (1 - c) if r & 1 else c) for r in range(1, N_DEV)]


def _remote(src, dst, send_sem, recv_sem, dev):
    return pltpu.make_async_remote_copy(src_ref=src, dst_ref=dst, send_sem=send_sem, recv_sem=recv_sem,
                                        device_id=dev, device_id_type=MESH)


ANY = pl.BlockSpec(memory_space=pl.ANY)
VMEM = pl.BlockSpec(memory_space=pltpu.VMEM)
F_ROWS = 16


W_ROW_CHUNKS = 4
P_ROW_CHUNKS = 2
N_BULK = W_ROW_CHUNKS + P_ROW_CHUNKS


def _half_chunks(core, n_rows, align):
    out = []
    for a, k in ((0, W_ROW_CHUNKS), (1, P_ROW_CHUNKS)):
        half = n_rows[a] // 2
        size = half // k
        for i in range(k):
            start = core * half + i * size
            out.append((a, pl.ds(start if isinstance(start, int) else pl.multiple_of(start, align), size)))
    return out


def gather_weights(c8, cctx8, adaw, adab, w_sh, p_sh, fp):
    nsh = adaw.shape[1]

    def body(c_ref, cctx_ref, adaw_ref, adab_ref, w_ref, p_ref, fp_ref, wall_ref, pall_ref, fall_ref, call_ref, mall_ref,
             abuf, w_send, w_recv, h_send, h_recv, c_send, c_recv, m_send, m_recv, f_send, f_recv):
        x, y, c, chips = _place()
        q = 2 * x + y
        dev = 4 * x + 2 * y + c
        qs = [2 * cx + cy for cx, cy in chips]
        sib = (x, y, 1 - c)
        srcs, dsts = (w_ref, p_ref), (wall_ref, pall_ref)
        n_rows = (w_ref.shape[0], p_ref.shape[0])
        mine = _half_chunks(c, n_rows, 16)
        other = _half_chunks(1 - c, n_rows, 16)

        bulk = [[_remote(srcs[a].at[rows], dsts[a].at[q, rows], w_send.at[j * N_BULK + i], w_recv.at[j * N_BULK + i],
                         (*chips[j], c)) for i, (a, rows) in enumerate(mine)] for j in range(3)]
        fall_ref[q] = fp_ref[...]
        small = [_remote(fp_ref, fall_ref.at[q], f_send.at[j], f_recv.at[j], (*chips[j], c)) for j in range(3)]
        my_rows = pl.ds(pl.multiple_of(8 * dev, 8), 8)
        call_ref[my_rows, :] = c_ref[...]
        cond = [_remote(c_ref, call_ref.at[my_rows, :], c_send.at[r], c_recv.at[r], peer)
                for r, peer in enumerate(_all_peers(x, y, c))]
        for cp in sum(bulk, []) + small + cond:
            cp.start()
        for cp in cond:
            cp.wait_recv()

        abuf[pl.ds(0, 64), :] = _silu(call_ref[...])
        abuf[pl.ds(64, 8), :] = _silu(cctx_ref[...])
        mall_ref[q] = _nn(abuf[...], adaw_ref[...], HI) + adab_ref[...]
        mod = [_remote(mall_ref.at[q], mall_ref.at[q], m_send.at[j], m_recv.at[j], (*chips[j], c)) for j in range(3)]
        for cp in mod:
            cp.start()

        handed = []
        for j in range(3):
            for i, (a, rows) in enumerate(mine):
                bulk[j][i].wait_recv()
                cp = _remote(dsts[a].at[qs[j], rows], dsts[a].at[qs[j], rows],
                             h_send.at[j * N_BULK + i], h_recv.at[j * N_BULK + i], sib)
                cp.start()
                handed.append(cp)
        for j in range(3):
            for i, (a, rows) in enumerate(other):
                _remote(dsts[a].at[qs[j], rows], dsts[a].at[qs[j], rows],
                        h_send.at[j * N_BULK + i], h_recv.at[j * N_BULK + i], sib).wait_recv()
        for cp in mod + small:
            cp.wait_recv()
        for cp in sum(bulk, []) + small + cond + mod + handed:
            cp.wait_send()

    def dma(n):
        return pltpu.SemaphoreType.DMA((n,))

    return _pallas(
        body, name="gather_weights",
        in_specs=[VMEM, VMEM, VMEM, VMEM, ANY, ANY, VMEM],
        out_specs=[ANY, ANY, VMEM, VMEM, VMEM],
        out_shape=[jax.ShapeDtypeStruct((N_CHIPS,) + w_sh.shape, BF16), jax.ShapeDtypeStruct((N_CHIPS,) + p_sh.shape, BF16),
                   jax.ShapeDtypeStruct((N_CHIPS, F_ROWS, D), F32),
                   jax.ShapeDtypeStruct((8 * N_DEV, D), F32), jax.ShapeDtypeStruct((N_CHIPS, MOD_ROWS, nsh), F32)],
        scratch_shapes=[pltpu.VMEM((MOD_ROWS, D), F32), dma(3 * N_BULK), dma(3 * N_BULK), dma(3 * N_BULK), dma(3 * N_BULK),
                        dma(7), dma(7), dma(3), dma(3), dma(3), dma(3)],
        compiler_params=_params(),
    )(c8, cctx8, adaw, adab, w_sh, p_sh, fp)


def pair_swap(gw, gp, sm):
    n_pair = N_CHIPS * N_BULK

    def body(gw_ref, gp_ref, sm_ref, gotw_ref, gotp_ref, sall_ref, a_send, a_recv, s_send, s_recv):
        x, y, c, _ = _place()
        dev = 4 * x + 2 * y + c
        srcs, dsts = (gw_ref, gp_ref), (gotw_ref, gotp_ref)
        n_rows = (gw_ref.shape[1], gp_ref.shape[1])
        pair = []
        for i, ((a, rows_o), (_, rows_0)) in enumerate(zip(_half_chunks(1 - c, n_rows, 8), _half_chunks(0, n_rows, 8))):
            for s in range(N_CHIPS):
                k = s * N_BULK + i
                pair.append(_remote(srcs[a].at[s, rows_o], dsts[a].at[s, rows_0], a_send.at[k], a_recv.at[k], (x, y, 1 - c)))
        sall_ref[dev] = sm_ref[...]
        small = [_remote(sm_ref, sall_ref.at[dev], s_send.at[r], s_recv.at[r], peer)
                 for r, peer in enumerate(_all_peers(x, y, c))]
        for cp in pair + small:
            cp.start()
        for cp in small + pair:
            cp.wait_recv()
        for cp in small + pair:
            cp.wait_send()

    return _pallas(
        body, name="pair_swap", in_specs=[ANY, ANY, VMEM], out_specs=[ANY, ANY, VMEM],
        out_shape=[jax.ShapeDtypeStruct((N_CHIPS, gw.shape[1] // 2, gw.shape[2]), F32),
                   jax.ShapeDtypeStruct((N_CHIPS, gp.shape[1] // 2, gp.shape[2]), F32),
                   jax.ShapeDtypeStruct((N_DEV,) + sm.shape, F32)],
        scratch_shapes=[pltpu.SemaphoreType.DMA((n_pair,)), pltpu.SemaphoreType.DMA((n_pair,)),
                        pltpu.SemaphoreType.DMA((N_DEV - 1,)), pltpu.SemaphoreType.DMA((N_DEV - 1,))],
        compiler_params=_params(),
    )(gw, gp, sm)


def gather_small(sm):
    rows = sm.shape[0]

    def body(sm_ref, sall_ref, s_send, s_recv):
        x, y, c, _ = _place()
        dev = 4 * x + 2 * y + c
        sall_ref[dev] = sm_ref[...]
        small = [_remote(sm_ref, sall_ref.at[dev], s_send.at[r], s_recv.at[r], peer)
                 for r, peer in enumerate(_all_peers(x, y, c))]
        for cp in small:
            cp.start()
        for cp in small:
            cp.wait_recv()
        for cp in small:
            cp.wait_send()

    return _pallas(
        body, name="gather_small", in_specs=[VMEM], out_specs=VMEM,
        out_shape=jax.ShapeDtypeStruct((N_DEV, rows, D), F32),
        scratch_shapes=[pltpu.SemaphoreType.DMA((7,)), pltpu.SemaphoreType.DMA((7,))],
        compiler_params=_params(),
    )(sm)


def pair_share(ghw, ghp, pq):
    def body(ghw_ref, ghp_ref, pq_ref, outw_ref, outp_ref, pqa_ref, send, recv, p_send, p_recv):
        del ghw_ref, ghp_ref
        x, y, c, chips = _place()
        q = 2 * x + y
        refs = (outw_ref, outp_ref)
        n_rows = (2 * outw_ref.shape[1], 2 * outp_ref.shape[1])
        pair = [_remote(refs[a].at[c, rows], refs[a].at[c, rows], send.at[i], recv.at[i], (x, y, 1 - c))
                for i, (a, rows) in enumerate(_half_chunks(0, n_rows, 8))]
        pqa_ref[q] = pq_ref[...]
        small = [_remote(pq_ref, pqa_ref.at[q], p_send.at[j], p_recv.at[j], (*chips[j], c)) for j in range(3)]
        for cp in pair + small:
            cp.start()
        for i, (a, rows) in enumerate(_half_chunks(0, n_rows, 8)):
            _remote(refs[a].at[1 - c, rows], refs[a].at[1 - c, rows], send.at[i], recv.at[i], (x, y, 1 - c)).wait_recv()
        for cp in small:
            cp.wait_recv()
        for cp in pair + small:
            cp.wait_send()

    return _pallas(
        body, name="pair_share", in_specs=[ANY, ANY, VMEM], out_specs=[ANY, ANY, VMEM],
        out_shape=[jax.ShapeDtypeStruct(ghw.shape, F32), jax.ShapeDtypeStruct(ghp.shape, F32),
                   jax.ShapeDtypeStruct((N_CHIPS, 8, D), F32)],
        scratch_shapes=[pltpu.SemaphoreType.DMA((N_BULK,)), pltpu.SemaphoreType.DMA((N_BULK,)),
                        pltpu.SemaphoreType.DMA((3,)), pltpu.SemaphoreType.DMA((3,))],
        input_output_aliases={0: 0, 1: 1},
        compiler_params=_params(),
    )(ghw, ghp, pq)


def _rows_of(shape):
    size = 1
    for s in shape:
        size *= s
    return -(-size // D)


def _pack(arrs, rows_multiple=8):
    parts = []
    total = 0
    for a in arrs:
        f = a.reshape(-1).astype(F32)
        r = _rows_of(a.shape)
        parts.append(jnp.pad(f, (0, r * D - f.shape[0])))
        total += r
    pad_rows = (-total) % rows_multiple
    if pad_rows:
        parts.append(jnp.zeros((pad_rows * D,), F32))
    return jnp.concatenate(parts).reshape(-1, D)


def _unpack(p, shapes):
    out = []
    r0 = 0
    for shp in shapes:
        r = _rows_of(shp)
        size = 1
        for s in shp:
            size *= s
        out.append(p[r0:r0 + r].reshape(-1)[:size].reshape(shp))
        r0 += r
    return out


WEIGHT_NAMES = ['c_ctx', 'ada_w', 'ada_b', 'norm_g', 'w_in', 'b_in', 'conv_w', 'conv_b', 'conv_ln_g', 'conv_ln_b',
                'conv_proj', 'decay_up_fwd', 'decay_bias_fwd', 'decay_up_bwd', 'decay_bias_bwd', 'gla_norm_g', 'gla_proj',
                'w_out', 'final_norm_g']
SMALL_NAMES = ['c_ctx', 'ada_b', 'norm_g', 'b_in', 'conv_w', 'conv_b', 'conv_ln_g', 'conv_ln_b', 'decay_up_fwd',
               'decay_bias_fwd', 'decay_up_bwd', 'decay_bias_bwd', 'gla_norm_g', 'final_norm_g']


def kernel(x, c, ctx, c_ctx, ada_w, ada_b, norm_g, w_in, b_in, conv_w, conv_b, conv_ln_g, conv_ln_b, conv_proj, decay_up_fwd, decay_bias_fwd, decay_up_bwd, decay_bias_bwd, gla_norm_g, gla_proj, w_out, final_norm_g, loss_target, m_c_ctx, m_ada_w, m_ada_b, m_norm_g, m_w_in, m_b_in, m_conv_w, m_conv_b, m_conv_ln_g, m_conv_ln_b, m_conv_proj, m_decay_up_fwd, m_decay_bias_fwd, m_decay_up_bwd, m_decay_bias_bwd, m_gla_norm_g, m_gla_proj, m_w_out, m_final_norm_g, v_c_ctx, v_ada_w, v_ada_b, v_norm_g, v_w_in, v_b_in, v_conv_w, v_conv_b, v_conv_ln_g, v_conv_ln_b, v_conv_proj, v_decay_up_fwd, v_decay_bias_fwd, v_decay_up_bwd, v_decay_bias_bwd, v_gla_norm_g, v_gla_proj, v_w_out, v_final_norm_g):
    w = dict(c_ctx=c_ctx, ada_w=ada_w, ada_b=ada_b, norm_g=norm_g, w_in=w_in, b_in=b_in, conv_w=conv_w, conv_b=conv_b,
             conv_ln_g=conv_ln_g, conv_ln_b=conv_ln_b, conv_proj=conv_proj, decay_up_fwd=decay_up_fwd,
             decay_bias_fwd=decay_bias_fwd, decay_up_bwd=decay_up_bwd, decay_bias_bwd=decay_bias_bwd,
             gla_norm_g=gla_norm_g, gla_proj=gla_proj, w_out=w_out, final_norm_g=final_norm_g)
    m = dict(c_ctx=m_c_ctx, ada_w=m_ada_w, ada_b=m_ada_b, norm_g=m_norm_g, w_in=m_w_in, b_in=m_b_in, conv_w=m_conv_w,
             conv_b=m_conv_b, conv_ln_g=m_conv_ln_g, conv_ln_b=m_conv_ln_b, conv_proj=m_conv_proj,
             decay_up_fwd=m_decay_up_fwd, decay_bias_fwd=m_decay_bias_fwd, decay_up_bwd=m_decay_up_bwd,
             decay_bias_bwd=m_decay_bias_bwd, gla_norm_g=m_gla_norm_g, gla_proj=m_gla_proj, w_out=m_w_out,
             final_norm_g=m_final_norm_g)
    v = dict(c_ctx=v_c_ctx, ada_w=v_ada_w, ada_b=v_ada_b, norm_g=v_norm_g, w_in=v_w_in, b_in=v_b_in, conv_w=v_conv_w,
             conv_b=v_conv_b, conv_ln_g=v_conv_ln_g, conv_ln_b=v_conv_ln_b, conv_proj=v_conv_proj,
             decay_up_fwd=v_decay_up_fwd, decay_bias_fwd=v_decay_bias_fwd, decay_up_bwd=v_decay_up_bwd,
             decay_bias_bwd=v_decay_bias_bwd, gla_norm_g=v_gla_norm_g, gla_proj=v_gla_proj, w_out=v_w_out,
             final_norm_g=v_final_norm_g)
    n = x.shape[0]
    ax, ay, ac = lax.axis_index("x"), lax.axis_index("y"), lax.axis_index("c")
    q = 2 * ax + ay
    dev = 4 * ax + 2 * ay + ac
    nsh = ada_w.shape[2]

    w_sh = w_in[0].astype(BF16)
    p_sh = jnp.concatenate([conv_proj[0], gla_proj[0], w_out[0]], 0).astype(BF16)
    fp = _pack([conv_w[0], decay_up_fwd[0], decay_up_bwd[0]], F_ROWS)
    c8 = jnp.pad(c, ((0, 8 - n), (0, 0)))
    cctx8 = jnp.pad(c_ctx[None], ((0, 7), (0, 0)))
    adab_sh = lax.dynamic_slice(ada_b, (0, q * nsh), (1, nsh))
    w_all, p_all, fall, call, mall = gather_weights(c8, cctx8, ada_w[0], adab_sh, w_sh, p_sh, fp)

    mod_all = jnp.transpose(mall, (1, 0, 2)).reshape(MOD_ROWS, 3 * D)
    mod_mine = lax.dynamic_slice(mod_all, (8 * dev, 0), (n, 3 * D))
    mod_ctx = mod_all[64:65]
    shift = jnp.concatenate([mod_mine[:, 0:D], mod_ctx[:, 0:D]], 0)[:, None, :]
    scale1 = 1.0 + jnp.concatenate([mod_mine[:, D:2 * D], mod_ctx[:, D:2 * D]], 0)[:, None, :]
    gate = mod_mine[:, 2 * D:3 * D][:, None, :]

    own = lambda i, mine, got: jnp.where(q == i, mine, got)
    g1, g2, g3, g4, g5 = _group_cols(jnp.concatenate([own(i, w_sh, w_all[i]) for i in range(N_CHIPS)], 1))
    p_full = jnp.stack([own(i, p_sh, p_all[i]) for i in range(N_CHIPS)])
    wts = dict(w1=g1, w2=g2, w3=g3, w4=g4, w5=g5,
               conv_proj=p_full[:, 0:256].reshape(D, D), gla_proj=p_full[:, 256:512].reshape(D, D),
               w_out=p_full[:, 512:768].reshape(D, D))
    f_parts = [_unpack(fall[i], [conv_w.shape[1:], decay_up_fwd.shape[1:], decay_up_bwd.shape[1:]]) for i in range(N_CHIPS)]
    conv_w_full = jnp.concatenate([p[0] for p in f_parts], 1)
    upf_full = jnp.concatenate([p[1] for p in f_parts], 1)
    upb_full = jnp.concatenate([p[2] for p in f_parts], 1)
    b1, b2, b3, b4, b5 = _group_cols(b_in)
    small = dict(b1=b1, b2=b2, b3=b3, b4=b4, b5=b5, norm_g=norm_g,
                 conv_w=jnp.pad(conv_w_full, ((0, 1), (0, 0))), conv_b=conv_b, conv_ln_g=conv_ln_g, conv_ln_b=conv_ln_b,
                 upf=_pad_up(upf_full, 0), upb=_pad_up(upb_full, 16), bias_f=decay_bias_fwd, bias_b=decay_bias_bwd,
                 gla_norm_g=gla_norm_g, final_norm_g=final_norm_g[None])

    loss_part, dh, dps, g = local_step(x, ctx, loss_target, (scale1, shift, gate), wts, small)
    loss = lax.psum(loss_part[0, 0], ("x", "y", "c"))

    gw = _ungroup_to_shards([g["w%d" % i] for i in range(1, 6)])
    gp = jnp.concatenate([g["conv_proj"].reshape(N_CHIPS, 256, D), g["gla_proj"].reshape(N_CHIPS, 256, D),
                          g["w_out"].reshape(N_CHIPS, 256, D)], 1)
    d_b_in = _ungroup_cols(*[g["b%d" % i] for i in range(1, 6)])
    early = [d_b_in, g["conv_b"].sum(0), g["conv_ln_g"], g["conv_ln_b"], g["bias_f"], g["bias_b"],
             g["gla_norm_g"], g["final_norm_g"], g["conv_w"].sum(0)[:CONV_K], g["upf"][0:16], g["upb"][16:32]]
    early_shapes = [a.shape for a in early]
    gotw, gotp, sall1 = pair_swap(gw, gp, _pack(early))
    core = ac.astype(jnp.int32).reshape(1)
    chip = q.astype(jnp.int32).reshape(1)
    paw, paw16 = pair_add(core, gw, gotw, name="pair_add_w", tr=128)
    pap, pap16 = pair_add(core, gp, gotp, name="pair_add_p", tr=384)
    grad_x2, dshift, dscale, g["norm_g"], rbw, rbp = dgrad_norm_bwd(
        dps, [wts["w%d" % i] for i in range(1, 6)], paw16, pap16, x.reshape(n * SEQ, D), ctx.reshape(n * NCTX, D), dh,
        scale1, norm_g, tm=256)

    dm_mine = jnp.concatenate([dshift[:n, 0], dscale[:n, 0], g["gate"][:, 0]], -1)
    dm_ctx = jnp.concatenate([dshift[n, 0], dscale[n, 0], jnp.zeros((D,), F32)], -1)
    late = [g["norm_g"], dm_mine, dm_ctx]
    late_shapes = [a.shape for a in late]
    sall2 = gather_small(_pack(late))
    (s_b_in, s_conv_b, s_ln_g, s_ln_b, s_bias_f, s_bias_b, s_gla_g, s_final_g, s_conv_w, s_upf,
     s_upb) = _unpack(sum_devices(sall1, name="sum_devices_early"), early_shapes)
    s_norm_g = _unpack(sum_devices(sall2, name="sum_devices_late"), late_shapes)[0]
    dm_rows = [_unpack(sall2[i], late_shapes)[-2:] for i in range(N_DEV)]
    dm_full = jnp.concatenate(
        [jnp.pad(jnp.stack([r[0] for r in dm_rows]), ((0, 0), (0, 8 - n), (0, 0))).reshape(8 * N_DEV, 3 * D),
         jnp.stack([r[1] for r in dm_rows])], 0)
    dm_shard = lax.dynamic_slice(dm_full, (0, q * nsh), (MOD_ROWS, nsh))
    cctx_rows = jnp.broadcast_to(c_ctx[None], (8, D))
    g_ada_w, g_ada_b, pq = ada_bwd(call, cctx_rows, dm_shard, dm_full, ada_w[0])

    place = jnp.concatenate([chip, core])
    ghw = chip_add(place, paw, rbw, name="chip_add_w", tr=128)
    ghp = chip_add(place, pap, rbp, name="chip_add_p", tr=384)
    gw_mine, gp_mine, pq_all = pair_share(ghw, ghp, pq)
    gp_mine = gp_mine.reshape(768, D)
    g_c_ctx = cctx_grad(pq_all, cctx_rows)[0]

    grads = dict(
        c_ctx=g_c_ctx, ada_w=g_ada_w[None], ada_b=g_ada_b, norm_g=s_norm_g,
        w_in=gw_mine.reshape(1, D, W_IN_SHARD), b_in=s_b_in,
        conv_w=lax.dynamic_slice(s_conv_w, (0, q * 256), (CONV_K, 256))[None], conv_b=s_conv_b,
        conv_ln_g=s_ln_g, conv_ln_b=s_ln_b, conv_proj=gp_mine[0:256][None],
        decay_up_fwd=lax.dynamic_slice(s_upf, (0, q * 128), (16, 128))[None], decay_bias_fwd=s_bias_f,
        decay_up_bwd=lax.dynamic_slice(s_upb, (0, q * 128), (16, 128))[None], decay_bias_bwd=s_bias_b,
        gla_norm_g=s_gla_g, gla_proj=gp_mine[256:512][None], w_out=gp_mine[512:768][None],
        final_norm_g=s_final_g[0])

    delta, new_m, new_v = {}, {}, {}
    for name in ["ada_w", "conv_proj", "gla_proj", "w_out"]:
        delta[name], new_m[name], new_v[name] = adamw2d(w[name], grads[name].reshape(w[name].shape), m[name], v[name],
                                                        name="adamw_" + name, tr=128)
    tr_ = lambda a: jnp.swapaxes(a, 1, 2)
    g_w_in_t = tr_(grads["w_in"])
    grads["w_in"] = tr_(g_w_in_t)
    d_, m_, v_ = adamw2d(tr_(w_in), g_w_in_t, tr_(m_w_in), tr_(v_w_in), name="adamw_w_in", tr=W_IN_SHARD, tcols=128)
    delta["w_in"], new_m["w_in"], new_v["w_in"] = tr_(d_), tr_(m_), tr_(v_)
    shapes = [w[nm].shape for nm in SMALL_NAMES]
    packs = [_pack([src[nm] for nm in SMALL_NAMES]) for src in (w, grads, m, v)]
    d_, m_, v_ = adamw2d(*packs, name="adamw_small", tr=packs[0].shape[0])
    for nm, a, b, cc in zip(SMALL_NAMES, _unpack(d_, shapes), _unpack(m_, shapes), _unpack(v_, shapes)):
        delta[nm], new_m[nm], new_v[nm] = a, b, cc

    grad_x = grad_x2.reshape(x.shape)
    return (loss, grad_x, *[grads[nm].reshape(w[nm].shape) for nm in WEIGHT_NAMES], *[delta[nm] for nm in WEIGHT_NAMES],
            *[new_m[nm] for nm in WEIGHT_NAMES], *[new_v[nm] for nm in WEIGHT_NAMES])
```

```python
import jax
import jax.numpy as jnp
from jax import lax
from jax.experimental import pallas as pl
from jax.experimental.pallas import tpu as pltpu

F32 = jnp.float32
BF16 = jnp.bfloat16
MESH = pl.DeviceIdType.MESH
HI = lax.Precision.HIGHEST

D = 1024
SEQ = 2048
GRID_W = 64
GRID_H = SEQ // GRID_W
NCTX = 256
SEQ_ALL = SEQ + NCTX
EPS = 1e-6
CONV_K = 31
CONV_PAD = CONV_K // 2
HEADS = 4
HEAD_K = 128
HEAD_V = 256
GLA_DK = HEADS * HEAD_K
GATE_TAU = 16.0
Q_SCALE = HEAD_K ** -0.5
CHUNK = 64
NCHUNK = SEQ_ALL // CHUNK
NCHUNK_LAT = SEQ // CHUNK
NCHUNK_CTX = NCHUNK - NCHUNK_LAT
SUB = 64
NSUB = CHUNK // SUB
N_IN = 8224
W3 = 2176
O3_V, O3_Q, O3_K, O3_AB = 0, 1024, 1536, 2048

ADAM_LR, ADAM_B1, ADAM_B2, ADAM_EPS, ADAM_WD, ADAM_STEP = 0.001, 0.9, 0.999, 1e-08, 0.01, 10
VMEM_LIMIT = 56 * 1024 * 1024

N_CHIPS = 4
N_DEV = 8
W_IN_SHARD = N_IN // N_CHIPS
MOD_ROWS = 72


def _pallas(body, **kw):
    return pl.pallas_call(body, **kw)


def _params(sem=None, **kw):
    if sem is not None:
        kw["dimension_semantics"] = sem
    return pltpu.CompilerParams(vmem_limit_bytes=VMEM_LIMIT, **kw)


def _sigmoid(v):
    return 1.0 / (1.0 + jnp.exp(-v))


def _silu(v):
    return v * _sigmoid(v)


def _dsilu(v):
    s = _sigmoid(v)
    return s * (1.0 + v * (1.0 - s))


def _log_sigmoid(v):
    return jnp.minimum(v, 0.0) - jnp.log(1.0 + jnp.exp(-jnp.abs(v)))


def _dot(a, b, dims, precision=None):
    return lax.dot_general(a, b, (dims, ((), ())), preferred_element_type=F32, precision=precision)


def _nn(a, b, precision=None):
    return _dot(a, b, ((1,), (0,)), precision)


def _nt(a, b, precision=None):
    return _dot(a, b, ((1,), (1,)), precision)


def _tn(a, b, precision=None):
    return _dot(a, b, ((0,), (0,)), precision)


def _b16(v):
    return v.astype(BF16)


def matmul_nn(a, b, bias, *, name, m, tm, tn, out_dtype):
    k = a.shape[1]
    n = b.shape[1]
    has_bias = bias is not None

    def body(*refs):
        if has_bias:
            a_ref, b_ref, bias_ref, o_ref = refs
            acc = _nn(a_ref[...], b_ref[...]) + bias_ref[...]
        else:
            a_ref, b_ref, o_ref = refs
            acc = _nn(a_ref[...], b_ref[...])
        o_ref[...] = acc.astype(o_ref.dtype)

    in_specs = [pl.BlockSpec((tm, k), lambda j, i: (i, 0)), pl.BlockSpec((k, tn), lambda j, i: (0, j))]
    args = [a, b]
    if has_bias:
        in_specs.append(pl.BlockSpec((1, tn), lambda j, i: (0, j)))
        args.append(bias)
    return _pallas(
        body, name=name, grid=(n // tn, m // tm), in_specs=in_specs,
        out_specs=pl.BlockSpec((tm, tn), lambda j, i: (i, j)),
        out_shape=jax.ShapeDtypeStruct((m, n), out_dtype),
        compiler_params=_params(("parallel", "parallel")),
    )(*args)


def matmul_nt(a, b, *, name, tm, out_dtype):
    m, k = a.shape
    n = b.shape[0]

    def body(a_ref, b_ref, o_ref):
        o_ref[...] = _nt(a_ref[...], b_ref[...]).astype(o_ref.dtype)

    return _pallas(
        body, name=name, grid=(m // tm,),
        in_specs=[pl.BlockSpec((tm, k), lambda i: (i, 0)), pl.BlockSpec((n, k), lambda i: (0, 0))],
        out_specs=pl.BlockSpec((tm, n), lambda i: (i, 0)),
        out_shape=jax.ShapeDtypeStruct((m, n), out_dtype),
        compiler_params=_params(("parallel",)),
    )(a, b)


def matmul_tn(a, b, *, name, t, tn, tt, colsum=False):
    m = a.shape[1]
    n = b.shape[1]

    def body(a_ref, b_ref, o_ref, *rest):
        @pl.when(pl.program_id(1) == 0)
        def _():
            o_ref[...] = jnp.zeros_like(o_ref)
            if colsum:
                rest[0][...] = jnp.zeros_like(rest[0])
        o_ref[...] += _tn(a_ref[...], b_ref[...])
        if colsum:
            rest[0][...] += jnp.sum(b_ref[...].astype(F32), axis=0, keepdims=True)

    out_specs = [pl.BlockSpec((m, tn), lambda j, s: (0, j))]
    out_shape = [jax.ShapeDtypeStruct((m, n), F32)]
    if colsum:
        out_specs.append(pl.BlockSpec((1, tn), lambda j, s: (0, j)))
        out_shape.append(jax.ShapeDtypeStruct((1, n), F32))
    return _pallas(
        body, name=name, grid=(n // tn, t // tt),
        in_specs=[pl.BlockSpec((tt, m), lambda j, s: (s, 0)), pl.BlockSpec((tt, tn), lambda j, s: (s, j))],
        out_specs=out_specs, out_shape=out_shape,
        compiler_params=_params(("parallel", "arbitrary")),
    )(a, b)


def dgrad_norm_bwd(dps, wts, paw, pap, x2, ctx2, dh, scale1, norm_g, *, tm):
    t, tc = x2.shape[0], ctx2.shape[0]
    t_all = t + tc
    n_lat, n_ctx = t // tm, tc // tm
    n_tiles = n_lat + n_ctx
    n_samples = scale1.shape[0] - 1
    tps = n_lat // n_samples
    n_grp = n_samples + 1
    n_g = len(dps)
    whole = [g for g in range(n_g) if dps[g].shape[0] == t_all]
    latent = [g for g in range(n_g) if dps[g].shape[0] != t_all]

    def body(*refs):
        dp_refs, w_refs = refs[:n_g], refs[n_g:2 * n_g]
        (paw_ref, pap_ref, x_ref, c_ref, dh_ref, sc_ref, g_ref, dx_ref, dsh_ref, dsc_ref, dg_ref, rbw_ref, rbp_ref,
         du_buf, b_send, b_recv) = refs[2 * n_g:]
        i = pl.program_id(0)

        def exchange():
            x, y, c, chips = _place()
            srcs, dsts = (paw_ref, pap_ref), (rbw_ref, rbp_ref)
            n_rows = (2 * paw_ref.shape[1], 2 * pap_ref.shape[1])
            return [_remote(srcs[a].at[2 * cx + cy, rows], dsts[a].at[j, rows],
                            b_send.at[j * N_BULK + k], b_recv.at[j * N_BULK + k], (cx, cy, c))
                    for j, (cx, cy) in enumerate(chips) for k, (a, rows) in enumerate(_half_chunks(0, n_rows, 16))]

        @pl.when(i == 0)
        def _():
            for cp in exchange():
                cp.start()

        acc = None
        for g in whole:
            part = _nt(dp_refs[g][...], w_refs[g][...])
            acc = part if acc is None else acc + part
        du_buf[...] = acc

        @pl.when(i < n_lat)
        def _():
            lat = None
            for g in latent:
                part = _nt(dp_refs[g][...], w_refs[g][...])
                lat = part if lat is None else lat + part
            du_buf[...] += lat

        duv = du_buf[...]
        xv = jnp.where(i < n_lat, x_ref[...], c_ref[...])
        rs = lax.rsqrt(jnp.mean(xv * xv, axis=-1, keepdims=True) + EPS)
        xh = xv * rs
        n = xh * g_ref[...]
        dn = duv * sc_ref[0]
        dxh = dn * g_ref[...]
        dx = rs * (dxh - xh * jnp.mean(dxh * xh, axis=-1, keepdims=True))

        @pl.when(i < n_lat)
        def _():
            dx_ref[...] = dx + dh_ref[...]

        @pl.when((i % tps == 0) & (i <= n_lat))
        def _():
            dsh_ref[...] = jnp.zeros_like(dsh_ref)
            dsc_ref[...] = jnp.zeros_like(dsc_ref)

        @pl.when(i == 0)
        def _():
            dg_ref[...] = jnp.zeros_like(dg_ref)

        dsh_ref[0] += jnp.sum(duv, axis=0, keepdims=True)
        dsc_ref[0] += jnp.sum(duv * n, axis=0, keepdims=True)
        dg_ref[...] += jnp.sum(dn * xh, axis=0, keepdims=True)

        @pl.when(i == n_tiles - 1)
        def _():
            for cp in exchange():
                cp.wait_recv()
            for cp in exchange():
                cp.wait_send()

    lat = lambda i: (jnp.minimum(i, n_lat - 1), 0)
    grp = lambda i: (jnp.minimum(i // tps, n_samples), 0, 0)
    in_specs = []
    for g, dp in enumerate(dps):
        nrow = dp.shape[0] // tm
        in_specs.append(pl.BlockSpec((tm, dp.shape[1]), lambda i, nrow=nrow: (jnp.minimum(i, nrow - 1), 0)))
    for w in wts:
        in_specs.append(pl.BlockSpec(w.shape, lambda i: (0, 0), pipeline_mode=pl.Buffered(1)))
    any_spec = pl.BlockSpec(memory_space=pl.ANY)
    in_specs += [any_spec, any_spec,
                 pl.BlockSpec((tm, D), lat), pl.BlockSpec((tm, D), lambda i: (jnp.maximum(i - n_lat, 0), 0)),
                 pl.BlockSpec((tm, D), lat), pl.BlockSpec((1, 1, D), grp), pl.BlockSpec((1, D), lambda i: (0, 0))]
    return _pallas(
        body, name="dgrad_norm_bwd", grid=(n_tiles,), in_specs=in_specs,
        out_specs=[pl.BlockSpec((tm, D), lat), pl.BlockSpec((1, 1, D), grp), pl.BlockSpec((1, 1, D), grp),
                   pl.BlockSpec((1, D), lambda i: (0, 0)), any_spec, any_spec],
        out_shape=[jax.ShapeDtypeStruct((t, D), F32), jax.ShapeDtypeStruct((n_grp, 1, D), F32),
                   jax.ShapeDtypeStruct((n_grp, 1, D), F32), jax.ShapeDtypeStruct((1, D), F32),
                   jax.ShapeDtypeStruct((3,) + paw.shape[1:], paw.dtype),
                   jax.ShapeDtypeStruct((3,) + pap.shape[1:], pap.dtype)],
        scratch_shapes=[pltpu.VMEM((tm, D), F32), pltpu.SemaphoreType.DMA((3 * N_BULK,)),
                        pltpu.SemaphoreType.DMA((3 * N_BULK,))],
        compiler_params=_params(("arbitrary",)),
    )(*dps, *wts, paw, pap, x2, ctx2, dh, scale1, norm_g)


TM_NORM = 512


def norm_mod_fwd(x2, ctx2, scale1, shift, norm_g):
    t = x2.shape[0]
    n_lat = t // TM_NORM
    assert ctx2.shape[0] == TM_NORM
    n_samples = scale1.shape[0] - 1
    tps = n_lat // n_samples

    def body(x_ref, c_ref, sc_ref, sh_ref, g_ref, u_ref):
        i = pl.program_id(0)
        xv = jnp.where(i < n_lat, x_ref[...], c_ref[...])
        rs = lax.rsqrt(jnp.mean(xv * xv, axis=-1, keepdims=True) + EPS)
        u = xv * rs * g_ref[...] * sc_ref[0] + sh_ref[0]
        u_ref[...] = u.astype(u_ref.dtype)

    grp = lambda i: (jnp.minimum(i // tps, n_samples), 0, 0)
    return _pallas(
        body, name="norm_mod_fwd", grid=(n_lat + 1,),
        in_specs=[pl.BlockSpec((TM_NORM, D), lambda i: (jnp.minimum(i, n_lat - 1), 0)),
                  pl.BlockSpec((TM_NORM, D), lambda i: (0, 0)),
                  pl.BlockSpec((1, 1, D), grp), pl.BlockSpec((1, 1, D), grp),
                  pl.BlockSpec((1, D), lambda i: (0, 0))],
        out_specs=pl.BlockSpec((TM_NORM, D), lambda i: (i, 0)),
        out_shape=jax.ShapeDtypeStruct((t + TM_NORM, D), BF16),
        compiler_params=_params(("parallel",)),
    )(x2, ctx2, scale1, shift, norm_g)


CONV_CB = 256
CONV_NCB = D // CONV_CB
H_OFF = 16


def _conv_pad_shape(vertical):
    if vertical:
        return (GRID_H + 2 * CONV_PAD, GRID_W, CONV_CB)
    return (GRID_H, GRID_W + 2 * H_OFF, CONV_CB)


def _conv_store(pad_ref, img, vertical):
    if vertical:
        pad_ref[pl.ds(CONV_PAD, GRID_H)] = img
    else:
        pad_ref[:, pl.ds(H_OFF, GRID_W), :] = img


def _conv_window(pad_ref, k, vertical, r):
    if vertical:
        return pad_ref[r + k]
    return pad_ref[r, pl.ds(H_OFF - CONV_PAD + k, GRID_W), :]


def _rows(r):
    return pl.ds(pl.multiple_of(r * GRID_W, GRID_W), GRID_W)


def conv_fwd(p1, conv_w, conv_b, n_samples):
    t = n_samples * SEQ

    def make(vertical, prev):
        def body(gv_ref, gg_ref, w_ref, b_ref, *rest):
            o_ref, pad_ref = rest[-2], rest[-1]
            pad_ref[...] = jnp.zeros_like(pad_ref)
            a = gv_ref[...].astype(F32) * _sigmoid(gg_ref[...].astype(F32))
            _conv_store(pad_ref, a.reshape(GRID_H, GRID_W, CONV_CB), vertical)

            def row(r, carry):
                acc = jnp.zeros((GRID_W, CONV_CB), F32) + b_ref[...]
                for k in range(CONV_K):
                    acc = acc + _conv_window(pad_ref, k, vertical, r) * w_ref[pl.ds(k, 1), :]
                o_ref[_rows(r), :] = acc
                return carry

            lax.fori_loop(0, GRID_H, row, 0)

        cb0 = CONV_NCB // 2 if vertical else 0
        in_specs = [pl.BlockSpec((SEQ, CONV_CB), lambda b, j: (b, 2 * (cb0 + j))),
                    pl.BlockSpec((SEQ, CONV_CB), lambda b, j: (b, 2 * (cb0 + j) + 1)),
                    pl.BlockSpec((CONV_K + 1, CONV_CB), lambda b, j: (0, cb0 + j)),
                    pl.BlockSpec((1, CONV_CB), lambda b, j: (0, cb0 + j))]
        args = [p1, p1, conv_w, conv_b]
        aliases = {}
        if prev is not None:
            in_specs.append(pl.BlockSpec(memory_space=pl.ANY))
            args.append(prev)
            aliases = {4: 0}
        return _pallas(
            body, name="conv_fwd_v" if vertical else "conv_fwd_h", grid=(n_samples, CONV_NCB // 2),
            in_specs=in_specs,
            out_specs=pl.BlockSpec((SEQ, CONV_CB), lambda b, j: (b, cb0 + j)),
            out_shape=jax.ShapeDtypeStruct((t, D), F32),
            scratch_shapes=[pltpu.VMEM(_conv_pad_shape(vertical), F32)],
            input_output_aliases=aliases,
            compiler_params=_params(("parallel", "parallel")),
        )(*args)

    return make(True, make(False, None))


def conv_bwd(p1, daconv, conv_w, n_samples):
    t = n_samples * SEQ

    def make(vertical, prev):
        def body(gv_ref, gg_ref, dy_ref, w_ref, *rest):
            dp_ref, dw_ref, db_ref, pad_ref, dpad_ref, da_ref = rest[-6:]
            pad_ref[...] = jnp.zeros_like(pad_ref)
            dpad_ref[...] = jnp.zeros_like(dpad_ref)
            gv = gv_ref[...].astype(F32)
            sg = _sigmoid(gg_ref[...].astype(F32))
            _conv_store(pad_ref, (gv * sg).reshape(GRID_H, GRID_W, CONV_CB), vertical)
            _conv_store(dpad_ref, dy_ref[...].reshape(GRID_H, GRID_W, CONV_CB), vertical)

            def row(r, carry):
                acc = jnp.zeros((GRID_W, CONV_CB), F32)
                for k in range(CONV_K):
                    acc = acc + _conv_window(dpad_ref, CONV_K - 1 - k, vertical, r) * w_ref[pl.ds(k, 1), :]
                da_ref[_rows(r), :] = acc
                return carry

            lax.fori_loop(0, GRID_H, row, 0)
            da = da_ref[...]
            dp_ref[:, pl.ds(0, CONV_CB)] = (da * sg).astype(dp_ref.dtype)
            dp_ref[:, pl.ds(CONV_CB, CONV_CB)] = (da * gv * sg * (1.0 - sg)).astype(dp_ref.dtype)

            for k in range(CONV_K):
                def wrow(r, acc, k=k):
                    return acc + _conv_window(pad_ref, k, vertical, r) * dy_ref[_rows(r), :]
                acc = lax.fori_loop(0, GRID_H, wrow, jnp.zeros((GRID_W, CONV_CB), F32))
                dw_ref[0, pl.ds(k, 1), :] = jnp.sum(acc, axis=0, keepdims=True)
            dw_ref[0, pl.ds(CONV_K, 1), :] = jnp.zeros((1, CONV_CB), F32)
            db_ref[0] = jnp.sum(dy_ref[...], axis=0, keepdims=True)

        cb0 = CONV_NCB // 2 if vertical else 0
        in_specs = [pl.BlockSpec((SEQ, CONV_CB), lambda b, j: (b, 2 * (cb0 + j))),
                    pl.BlockSpec((SEQ, CONV_CB), lambda b, j: (b, 2 * (cb0 + j) + 1)),
                    pl.BlockSpec((SEQ, CONV_CB), lambda b, j: (b, cb0 + j)),
                    pl.BlockSpec((CONV_K + 1, CONV_CB), lambda b, j: (0, cb0 + j))]
        args = [p1, p1, daconv, conv_w]
        aliases = {}
        if prev is not None:
            in_specs += [pl.BlockSpec(memory_space=pl.ANY)] * 3
            args += list(prev)
            aliases = {4: 0, 5: 1, 6: 2}
        return _pallas(
            body, name="conv_bwd_v" if vertical else "conv_bwd_h", grid=(n_samples, CONV_NCB // 2),
            in_specs=in_specs,
            out_specs=[pl.BlockSpec((SEQ, 2 * CONV_CB), lambda b, j: (b, cb0 + j)),
                       pl.BlockSpec((1, CONV_K + 1, CONV_CB), lambda b, j: (b, 0, cb0 + j)),
                       pl.BlockSpec((1, 1, CONV_CB), lambda b, j: (b, 0, cb0 + j))],
            out_shape=[jax.ShapeDtypeStruct((t, 2 * D), BF16),
                       jax.ShapeDtypeStruct((n_samples, CONV_K + 1, D), F32),
                       jax.ShapeDtypeStruct((n_samples, 1, D), F32)],
            scratch_shapes=[pltpu.VMEM(_conv_pad_shape(vertical), F32), pltpu.VMEM(_conv_pad_shape(vertical), F32),
                            pltpu.VMEM((SEQ, CONV_CB), F32)],
            input_output_aliases=aliases,
            compiler_params=_params(("parallel", "parallel")),
        )(*args)

    return make(True, make(False, None))


TM_EW = 256


def ln_gate_fwd(aconv, z, ln_g, ln_b):
    t = aconv.shape[0]

    def body(a_ref, z_ref, g_ref, b_ref, o_ref):
        a = a_ref[...]
        mu = jnp.mean(a, axis=-1, keepdims=True)
        xc = a - mu
        rstd = lax.rsqrt(jnp.mean(xc * xc, axis=-1, keepdims=True) + EPS)
        l = xc * rstd * g_ref[...] + b_ref[...]
        o_ref[...] = (_silu(l) * _silu(z_ref[...].astype(F32))).astype(o_ref.dtype)

    row = pl.BlockSpec((TM_EW, D), lambda i: (i, 0))
    vec = pl.BlockSpec((1, D), lambda i: (0, 0))
    return _pallas(
        body, name="ln_gate_fwd", grid=(t // TM_EW,), in_specs=[row, row, vec, vec], out_specs=row,
        out_shape=jax.ShapeDtypeStruct((t, D), BF16), compiler_params=_params(("parallel",)),
    )(aconv, z, ln_g, ln_b)


def ln_gate_bwd(aconv, z, dac, ln_g, ln_b):
    t = aconv.shape[0]

    def body(a_ref, z_ref, d_ref, g_ref, b_ref, da_ref, dz_ref, dg_ref, db_ref):
        a = a_ref[...]
        zv = z_ref[...].astype(F32)
        dac_v = d_ref[...].astype(F32)
        mu = jnp.mean(a, axis=-1, keepdims=True)
        xc = a - mu
        rstd = lax.rsqrt(jnp.mean(xc * xc, axis=-1, keepdims=True) + EPS)
        xh = xc * rstd
        l = xh * g_ref[...] + b_ref[...]
        dz_ref[...] = (dac_v * _silu(l) * _dsilu(zv)).astype(dz_ref.dtype)
        dl = dac_v * _silu(zv) * _dsilu(l)
        dxh = dl * g_ref[...]
        da_ref[...] = rstd * (dxh - jnp.mean(dxh, axis=-1, keepdims=True)
                              - xh * jnp.mean(dxh * xh, axis=-1, keepdims=True))

        @pl.when(pl.program_id(0) == 0)
        def _():
            dg_ref[...] = jnp.zeros_like(dg_ref)
            db_ref[...] = jnp.zeros_like(db_ref)

        dg_ref[...] += jnp.sum(dl * xh, axis=0, keepdims=True)
        db_ref[...] += jnp.sum(dl, axis=0, keepdims=True)

    row = pl.BlockSpec((TM_EW, D), lambda i: (i, 0))
    vec = pl.BlockSpec((1, D), lambda i: (0, 0))
    return _pallas(
        body, name="ln_gate_bwd", grid=(t // TM_EW,), in_specs=[row, row, row, vec, vec],
        out_specs=[row, row, vec, vec],
        out_shape=[jax.ShapeDtypeStruct((t, D), F32), jax.ShapeDtypeStruct((t, D), BF16),
                   jax.ShapeDtypeStruct((1, D), F32), jax.ShapeDtypeStruct((1, D), F32)],
        compiler_params=_params(("arbitrary",)),
    )(aconv, z, dac, ln_g, ln_b)


TM_PREP = 256
PREP_LAT = SEQ // TM_PREP
PREP_ALL = SEQ_ALL // TM_PREP


def _chunk_tri(n, upper):
    r = lax.broadcasted_iota(jnp.int32, (n, n), 0)
    c = lax.broadcasted_iota(jnp.int32, (n, n), 1)
    same = (r // CHUNK) == (c // CHUNK)
    keep = (c >= r) if upper else (c <= r)
    return jnp.where(same & keep, 1.0, 0.0).astype(F32)


def _split3(v):
    hi = v.astype(BF16)
    r1 = v - hi.astype(F32)
    mid = r1.astype(BF16)
    lo = (r1 - mid.astype(F32)).astype(BF16)
    return jnp.stack([hi, mid, lo])


def _chunk_sums(v, upper):
    tri = _chunk_tri(v.shape[0], upper).astype(BF16)
    pieces = _split3(v)
    return (_nn(tri, pieces[0]) + _nn(tri, pieces[1])) + _nn(tri, pieces[2])


def _gate_logits(ab, up3_ref, bias_ref):
    assert ab.dtype == BF16
    return ((_nn(ab, up3_ref[0]) + _nn(ab, up3_ref[1])) + _nn(ab, up3_ref[2])) + bias_ref[...]


def _prep_tile_maps(n_samples):
    n_lat = n_samples * PREP_LAT

    def seq_map(i):
        return jnp.where(i < n_lat, i // PREP_LAT, i - n_lat), jnp.where(i < n_lat, i % PREP_LAT, PREP_LAT)

    return n_lat, seq_map


def gla_prep_fwd(p3, upf, upb, bias_f, bias_b, n_samples):
    n_lat, seq_map = _prep_tile_maps(n_samples)
    n_tiles = n_lat + n_samples

    def body(v_ref, q_ref, k_ref, ab_ref, upf_ref, upb_ref, bf_ref, bb_ref, qo, ko, vo, cf, cb):
        i = pl.program_id(0)
        qo[0] = jnp.where(i < n_lat, q_ref[...].astype(F32) * Q_SCALE, 0.0)
        ko[0] = k_ref[...].astype(F32)
        vo[0] = v_ref[...].astype(F32)
        ab = ab_ref[...]
        gf = _log_sigmoid(_gate_logits(ab, upf_ref, bf_ref)) * (1.0 / GATE_TAU)
        gb = _log_sigmoid(_gate_logits(ab, upb_ref, bb_ref)) * (1.0 / GATE_TAU)
        cf[0] = _chunk_sums(gf, False)
        cb[0] = _chunk_sums(gb, True)

    def o_spec(w):
        return pl.BlockSpec((1, TM_PREP, w), lambda i: (*seq_map(i), 0))

    full = lambda shape: pl.BlockSpec(shape, lambda i: (0,) * len(shape))
    return _pallas(
        body, name="gla_prep_fwd", grid=(n_tiles,),
        in_specs=[pl.BlockSpec((TM_PREP, 1024), lambda i: (i, O3_V // 1024)),
                  pl.BlockSpec((TM_PREP, 512), lambda i: (i, O3_Q // 512)),
                  pl.BlockSpec((TM_PREP, 512), lambda i: (i, O3_K // 512)),
                  pl.BlockSpec((TM_PREP, 128), lambda i: (i, O3_AB // 128)),
                  full((3, 128, GLA_DK)), full((3, 128, GLA_DK)), full((1, GLA_DK)), full((1, GLA_DK))],
        out_specs=[o_spec(GLA_DK), o_spec(GLA_DK), o_spec(D), o_spec(GLA_DK), o_spec(GLA_DK)],
        out_shape=[jax.ShapeDtypeStruct((n_samples, SEQ_ALL, GLA_DK), F32),
                   jax.ShapeDtypeStruct((n_samples, SEQ_ALL, GLA_DK), F32),
                   jax.ShapeDtypeStruct((n_samples, SEQ_ALL, D), F32),
                   jax.ShapeDtypeStruct((n_samples, SEQ_ALL, GLA_DK), F32),
                   jax.ShapeDtypeStruct((n_samples, SEQ_ALL, GLA_DK), F32)],
        compiler_params=_params(("parallel",)),
    )(p3, p3, p3, p3, upf, upb, bias_f, bias_b)


def gla_prep_bwd(p3, dq_f, dq_b, dk_f, dk_b, dv_f, dv_b, dc_f, dc_b, upf, upb, bias_f, bias_b, n_samples):
    n_lat, seq_map = _prep_tile_maps(n_samples)
    n_tiles = n_lat + n_samples

    def body(ab_ref, dqf, dqb, dkf, dkb, dvf, dvb, dcf, dcb, upf_ref, upb_ref, bf_ref, bb_ref,
             dp_ref, duf_ref, dub_ref, dbf_ref, dbb_ref):
        i = pl.program_id(0)
        dp_ref[:, pl.ds(O3_V, D)] = (dvf[0] + dvb[0]).astype(dp_ref.dtype)
        dq = jnp.where(i < n_lat, (dqf[0] + dqb[0]) * Q_SCALE, 0.0)
        dp_ref[:, pl.ds(O3_Q, GLA_DK)] = dq.astype(dp_ref.dtype)
        dp_ref[:, pl.ds(O3_K, GLA_DK)] = (dkf[0] + dkb[0]).astype(dp_ref.dtype)
        ab = ab_ref[...]
        zf = _gate_logits(ab, upf_ref, bf_ref)
        zb = _gate_logits(ab, upb_ref, bb_ref)
        dgf = _chunk_sums(dcf[0], True)
        dgb = _chunk_sums(dcb[0], False)
        dzf = _b16(dgf * (1.0 / GATE_TAU) * _sigmoid(-zf))
        dzb = _b16(dgb * (1.0 / GATE_TAU) * _sigmoid(-zb))
        dab = _nt(dzf, upf_ref[0]) + _nt(dzb, upb_ref[0])
        dp_ref[:, pl.ds(O3_AB, 128)] = dab.astype(dp_ref.dtype)

        @pl.when(i == 0)
        def _():
            duf_ref[...] = jnp.zeros_like(duf_ref)
            dub_ref[...] = jnp.zeros_like(dub_ref)
            dbf_ref[...] = jnp.zeros_like(dbf_ref)
            dbb_ref[...] = jnp.zeros_like(dbb_ref)

        duf_ref[...] += _tn(ab, dzf)
        dub_ref[...] += _tn(ab, dzb)
        dbf_ref[...] += jnp.sum(dzf.astype(F32), axis=0, keepdims=True)
        dbb_ref[...] += jnp.sum(dzb.astype(F32), axis=0, keepdims=True)

    def s_spec(w):
        return pl.BlockSpec((1, TM_PREP, w), lambda i: (*seq_map(i), 0))

    full = lambda shape: pl.BlockSpec(shape, lambda i: (0,) * len(shape))
    return _pallas(
        body, name="gla_prep_bwd", grid=(n_tiles,),
        in_specs=[pl.BlockSpec((TM_PREP, 128), lambda i: (i, O3_AB // 128)),
                  s_spec(GLA_DK), s_spec(GLA_DK), s_spec(GLA_DK), s_spec(GLA_DK), s_spec(D), s_spec(D),
                  s_spec(GLA_DK), s_spec(GLA_DK),
                  full((3, 128, GLA_DK)), full((3, 128, GLA_DK)), full((1, GLA_DK)), full((1, GLA_DK))],
        out_specs=[pl.BlockSpec((TM_PREP, W3), lambda i: (i, 0)),
                   full((128, GLA_DK)), full((128, GLA_DK)), full((1, GLA_DK)), full((1, GLA_DK))],
        out_shape=[jax.ShapeDtypeStruct((n_tiles * TM_PREP, W3), BF16),
                   jax.ShapeDtypeStruct((128, GLA_DK), F32), jax.ShapeDtypeStruct((128, GLA_DK), F32),
                   jax.ShapeDtypeStruct((1, GLA_DK), F32), jax.ShapeDtypeStruct((1, GLA_DK), F32)],
        compiler_params=_params(("arbitrary",)),
    )(p3, dq_f, dq_b, dk_f, dk_b, dv_f, dv_b, dc_f, dc_b, upf, upb, bias_f, bias_b)


def _sub_blocks(rev):
    if NSUB == 1:
        return [((0, CHUNK), CHUNK // 2, (0, CHUNK))]
    out = []
    for s in range(NSUB):
        rows = (s * SUB, SUB)
        if rev:
            ref = (s + 1) * SUB if s < NSUB - 1 else None
            cols = (s * SUB, CHUNK - s * SUB)
        else:
            ref = s * SUB - 1 if s > 0 else None
            cols = (0, (s + 1) * SUB)
        out.append((rows, ref, cols))
    return out


def _sub_mask(rows, cols, rev):
    r = rows[0] + lax.broadcasted_iota(jnp.int32, (rows[1], cols[1]), 0)
    c = cols[0] + lax.broadcasted_iota(jnp.int32, (rows[1], cols[1]), 1)
    return (c >= r) if rev else (c <= r)


def _sub_operands(qc, kc, cc, rows, ref, cols):
    cref = jnp.zeros((1, HEAD_K), F32) if ref is None else cc[ref:ref + 1]
    eq = jnp.exp(cc[rows[0]:rows[0] + rows[1]] - cref)
    ek = jnp.exp(cref - cc[cols[0]:cols[0] + cols[1]])
    qs = qc[rows[0]:rows[0] + rows[1]] * eq
    kk = kc[cols[0]:cols[0] + cols[1]] * ek
    return qs, kk, eq, ek


SCAN_ROWS = 256
SCAN_CHUNKS = SCAN_ROWS // CHUNK
SCAN_STEPS = SEQ_ALL // SCAN_ROWS
LAT_BLOCKS = SEQ // SCAN_ROWS


def _scan_block(t, rev):
    if rev:
        return SCAN_STEPS - 1 - t
    return jnp.where(t == 0, SCAN_STEPS - 1, t - 1)


def _scan_lat_block(t, rev):
    if rev:
        return jnp.minimum(SCAN_STEPS - 1 - t, LAT_BLOCKS - 1)
    return jnp.maximum(t - 1, 0)


def _head_cols(h):
    return pl.ds(h * HEAD_K, HEAD_K), pl.ds(h * HEAD_V, HEAD_V)


def gla_scan_fwd(q, k, v, cum, *, rev, name):
    n = q.shape[0]

    def body(q_ref, k_ref, v_ref, c_ref, o_ref, s_ref, sfin_ref, st):
        t = pl.program_id(1)

        @pl.when(t == 0)
        def _():
            st[...] = jnp.zeros_like(st)

        def chunk(j, carry):
            lj = SCAN_CHUNKS - 1 - j if rev else j
            r0 = pl.multiple_of(lj * CHUNK, CHUNK)
            rws = pl.ds(r0, CHUNK)
            for h in range(HEADS):
                kcols, vcols = _head_cols(h)
                qc, kc, cc = q_ref[0, rws, kcols], k_ref[0, rws, kcols], c_ref[0, rws, kcols]
                vc = v_ref[0, rws, vcols]
                s_in = st[h]
                s_ref[0, h, j] = s_in
                edge = cc[0:1] if rev else cc[CHUNK - 1:CHUNK]
                ke = kc * jnp.exp(edge - cc)
                st[h] = s_in * jnp.exp(edge) + _tn(_b16(vc), _b16(ke))
                o_inter = _nt(_b16(qc * jnp.exp(cc)), _b16(s_in))
                vb = _b16(vc)
                for rows, ref, cols in _sub_blocks(rev):
                    qs, kk, _, _ = _sub_operands(qc, kc, cc, rows, ref, cols)
                    a = jnp.where(_sub_mask(rows, cols, rev), _nt(_b16(qs), _b16(kk)), 0.0)
                    o_s = _nn(_b16(a), vb[cols[0]:cols[0] + cols[1]])
                    o_ref[0, pl.ds(r0 + rows[0], rows[1]), vcols] = o_inter[rows[0]:rows[0] + rows[1]] + o_s
            return carry

        lax.fori_loop(0, SCAN_CHUNKS, chunk, 0)

        @pl.when(t == SCAN_STEPS - 1)
        def _():
            sfin_ref[0] = st[...]

    def spec(w):
        return pl.BlockSpec((1, SCAN_ROWS, w), lambda b, t: (b, _scan_block(t, rev), 0))

    return _pallas(
        body, name=name, grid=(n, SCAN_STEPS),
        in_specs=[spec(GLA_DK), spec(GLA_DK), spec(D), spec(GLA_DK)],
        out_specs=[pl.BlockSpec((1, SCAN_ROWS, D), lambda b, t: (b, _scan_lat_block(t, rev), 0)),
                   pl.BlockSpec((1, HEADS, SCAN_CHUNKS, HEAD_V, HEAD_K), lambda b, t: (b, 0, t, 0, 0)),
                   pl.BlockSpec((1, HEADS, HEAD_V, HEAD_K), lambda b, t: (b, 0, 0, 0))],
        out_shape=[jax.ShapeDtypeStruct((n, SEQ, D), F32),
                   jax.ShapeDtypeStruct((n, HEADS, NCHUNK, HEAD_V, HEAD_K), F32),
                   jax.ShapeDtypeStruct((n, HEADS, HEAD_V, HEAD_K), F32)],
        scratch_shapes=[pltpu.VMEM((HEADS, HEAD_V, HEAD_K), F32)],
        compiler_params=_params(("parallel", "arbitrary")),
    )(q, k, v, cum)


def gla_scan_bwd(q, k, v, cum, s_all, s_fin, do, *, rev, name):
    n = q.shape[0]

    def body(q_ref, k_ref, v_ref, c_ref, s_ref, sfin_ref, do_ref, dq_ref, dk_ref, dv_ref, dc_ref,
             dst, s_next, dq_acc, dk_acc, dv_acc):
        t = SCAN_STEPS - 1 - pl.program_id(1)

        @pl.when(pl.program_id(1) == 0)
        def _():
            dst[...] = jnp.zeros_like(dst)
            s_next[...] = sfin_ref[0]

        def chunk(jj, carry):
            j = SCAN_CHUNKS - 1 - jj
            lj = SCAN_CHUNKS - 1 - j if rev else j
            rws = pl.ds(pl.multiple_of(lj * CHUNK, CHUNK), CHUNK)
            for h in range(HEADS):
                kcols, vcols = _head_cols(h)
                qc, kc, cc = q_ref[0, rws, kcols], k_ref[0, rws, kcols], c_ref[0, rws, kcols]
                vc = v_ref[0, rws, vcols]
                doc = jnp.where(t > 0, do_ref[0, rws, vcols], 0.0)
                s_in = s_ref[0, h, j]
                s_out = s_next[h]
                ds_out = dst[h]
                edge = cc[0:1] if rev else cc[CHUNK - 1:CHUNK]
                e_q = jnp.exp(cc)
                e_k = jnp.exp(edge - cc)
                dob = _b16(doc)
                dsb = _b16(ds_out)
                dst[h] = ds_out * jnp.exp(edge) + _tn(dob, _b16(qc * e_q))
                s_next[h] = s_in
                dq_acc[h] = e_q * _nn(dob, _b16(s_in))
                dk_acc[h] = e_k * _nn(_b16(vc), dsb)
                dv_acc[h] = _nt(_b16(kc * e_k), dsb)
                vb = _b16(vc)
                for rows, ref, cols in _sub_blocks(rev):
                    qs, kk, eq, ek = _sub_operands(qc, kc, cc, rows, ref, cols)
                    mask = _sub_mask(rows, cols, rev)
                    rsl = slice(rows[0], rows[0] + rows[1])
                    csl = pl.ds(cols[0], cols[1])
                    qsb, kkb = _b16(qs), _b16(kk)
                    a = jnp.where(mask, _nt(qsb, kkb), 0.0)
                    da = _b16(jnp.where(mask, _nt(dob[rsl], vb[cols[0]:cols[0] + cols[1]]), 0.0))
                    dq_acc[h, pl.ds(rows[0], rows[1]), :] += _nn(da, kkb) * eq
                    dk_acc[h, csl, :] += _tn(da, qsb) * ek
                    dv_acc[h, csl, :] += _tn(_b16(a), dob[rsl])
                dq = dq_acc[h]
                dk = dk_acc[h]
                dc = qc * dq - kc * dk
                bnd = jnp.sum(ds_out * s_out, axis=0, keepdims=True)
                edge_row = 0 if rev else CHUNK - 1
                is_edge = lax.broadcasted_iota(jnp.int32, (CHUNK, HEAD_K), 0) == edge_row
                dq_ref[0, rws, kcols] = dq
                dk_ref[0, rws, kcols] = dk
                dv_ref[0, rws, vcols] = dv_acc[h]
                dc_ref[0, rws, kcols] = dc + jnp.where(is_edge, bnd, 0.0)
            return carry

        lax.fori_loop(0, SCAN_CHUNKS, chunk, 0)

    def step_of(u):
        return SCAN_STEPS - 1 - u

    def spec(w):
        return pl.BlockSpec((1, SCAN_ROWS, w), lambda b, u: (b, _scan_block(step_of(u), rev), 0))

    return _pallas(
        body, name=name, grid=(n, SCAN_STEPS),
        in_specs=[spec(GLA_DK), spec(GLA_DK), spec(D), spec(GLA_DK),
                  pl.BlockSpec((1, HEADS, SCAN_CHUNKS, HEAD_V, HEAD_K), lambda b, u: (b, 0, step_of(u), 0, 0)),
                  pl.BlockSpec((1, HEADS, HEAD_V, HEAD_K), lambda b, u: (b, 0, 0, 0)),
                  pl.BlockSpec((1, SCAN_ROWS, D), lambda b, u: (b, _scan_lat_block(step_of(u), rev), 0))],
        out_specs=[spec(GLA_DK), spec(GLA_DK), spec(D), spec(GLA_DK)],
        out_shape=[jax.ShapeDtypeStruct((n, SEQ_ALL, GLA_DK), F32), jax.ShapeDtypeStruct((n, SEQ_ALL, GLA_DK), F32),
                   jax.ShapeDtypeStruct((n, SEQ_ALL, D), F32), jax.ShapeDtypeStruct((n, SEQ_ALL, GLA_DK), F32)],
        scratch_shapes=[pltpu.VMEM((HEADS, HEAD_V, HEAD_K), F32), pltpu.VMEM((HEADS, HEAD_V, HEAD_K), F32),
                        pltpu.VMEM((HEADS, CHUNK, HEAD_K), F32), pltpu.VMEM((HEADS, CHUNK, HEAD_K), F32),
                        pltpu.VMEM((HEADS, CHUNK, HEAD_V), F32)],
        compiler_params=_params(("parallel", "arbitrary")),
    )(q, k, v, cum, s_all, s_fin, do)


def gla_out_fwd(o_f, o_b, r, gnorm):
    n = o_f.shape[0]
    tiles = SEQ // TM_EW

    def body(of_ref, ob_ref, r_ref, g_ref, og_ref):
        for h in range(HEADS):
            cols = pl.ds(h * HEAD_V, HEAD_V)
            o = of_ref[0, :, cols] + ob_ref[0, :, cols]
            rs = lax.rsqrt(jnp.mean(o * o, axis=-1, keepdims=True) + EPS)
            og_ref[:, cols] = (o * rs * g_ref[...] * _silu(r_ref[:, cols].astype(F32))).astype(og_ref.dtype)

    ospec = pl.BlockSpec((1, TM_EW, D), lambda b, j: (b, j, 0))
    row = pl.BlockSpec((TM_EW, D), lambda b, j: (b * tiles + j, 0))
    return _pallas(
        body, name="gla_out_fwd", grid=(n, tiles),
        in_specs=[ospec, ospec, row, pl.BlockSpec((1, HEAD_V), lambda b, j: (0, 0))],
        out_specs=row, out_shape=jax.ShapeDtypeStruct((n * SEQ, D), BF16),
        compiler_params=_params(("parallel", "parallel")),
    )(o_f, o_b, r, gnorm)


def gla_out_bwd(o_f, o_b, r, dog, gnorm):
    n = o_f.shape[0]
    tiles = SEQ // TM_EW

    def body(of_ref, ob_ref, r_ref, d_ref, g_ref, do_ref, dr_ref, dg_ref):
        @pl.when((pl.program_id(0) == 0) & (pl.program_id(1) == 0))
        def _():
            dg_ref[...] = jnp.zeros_like(dg_ref)

        for h in range(HEADS):
            cols = pl.ds(h * HEAD_V, HEAD_V)
            o = of_ref[0, :, cols] + ob_ref[0, :, cols]
            rv = r_ref[:, cols].astype(F32)
            dv = d_ref[:, cols].astype(F32)
            rs = lax.rsqrt(jnp.mean(o * o, axis=-1, keepdims=True) + EPS)
            oh = o * rs
            dr_ref[:, cols] = (dv * oh * g_ref[...] * _dsilu(rv)).astype(dr_ref.dtype)
            dn = dv * _silu(rv)
            dg_ref[...] += jnp.sum(dn * oh, axis=0, keepdims=True)
            doh = dn * g_ref[...]
            do_ref[0, :, cols] = rs * (doh - oh * jnp.mean(doh * oh, axis=-1, keepdims=True))

    ospec = pl.BlockSpec((1, TM_EW, D), lambda b, j: (b, j, 0))
    row = pl.BlockSpec((TM_EW, D), lambda b, j: (b * tiles + j, 0))
    vec = pl.BlockSpec((1, HEAD_V), lambda b, j: (0, 0))
    return _pallas(
        body, name="gla_out_bwd", grid=(n, tiles),
        in_specs=[ospec, ospec, row, row, vec],
        out_specs=[ospec, row, vec],
        out_shape=[jax.ShapeDtypeStruct((n, SEQ, D), F32), jax.ShapeDtypeStruct((n * SEQ, D), BF16),
                   jax.ShapeDtypeStruct((1, HEAD_V), F32)],
        compiler_params=_params(("arbitrary", "arbitrary")),
    )(o_f, o_b, r, dog, gnorm)


def merge_fwd(p5, y_conv, y_gla):
    t = y_conv.shape[0]

    def body(mc_ref, mg_ref, yc_ref, yg_ref, o_ref):
        f = lambda ref: ref[...].astype(F32)
        o_ref[...] = (_sigmoid(f(mc_ref)) * f(yc_ref) + _sigmoid(f(mg_ref)) * f(yg_ref)).astype(o_ref.dtype)

    row = pl.BlockSpec((TM_EW, D), lambda i: (i, 0))
    return _pallas(
        body, name="merge_fwd", grid=(t // TM_EW,),
        in_specs=[row, pl.BlockSpec((TM_EW, D), lambda i: (i, 1)), row, row], out_specs=row,
        out_shape=jax.ShapeDtypeStruct((t, D), BF16), compiler_params=_params(("parallel",)),
    )(p5, p5, y_conv, y_gla)


def merge_bwd(p5, y_conv, y_gla, dmerged):
    t = y_conv.shape[0]

    def body(mc_ref, mg_ref, yc_ref, yg_ref, d_ref, dyc_ref, dyg_ref, dp_ref):
        f = lambda ref: ref[...].astype(F32)
        d = f(d_ref)
        sc = _sigmoid(f(mc_ref))
        sg = _sigmoid(f(mg_ref))
        dyc_ref[...] = (d * sc).astype(dyc_ref.dtype)
        dyg_ref[...] = (d * sg).astype(dyg_ref.dtype)
        dp_ref[:, pl.ds(0, D)] = (d * f(yc_ref) * sc * (1.0 - sc)).astype(dp_ref.dtype)
        dp_ref[:, pl.ds(D, D)] = (d * f(yg_ref) * sg * (1.0 - sg)).astype(dp_ref.dtype)

    row = pl.BlockSpec((TM_EW, D), lambda i: (i, 0))
    return _pallas(
        body, name="merge_bwd", grid=(t // TM_EW,),
        in_specs=[row, pl.BlockSpec((TM_EW, D), lambda i: (i, 1)), row, row, row],
        out_specs=[row, row, pl.BlockSpec((TM_EW, 2 * D), lambda i: (i, 0))],
        out_shape=[jax.ShapeDtypeStruct((t, D), BF16), jax.ShapeDtypeStruct((t, D), BF16),
                   jax.ShapeDtypeStruct((t, 2 * D), BF16)],
        compiler_params=_params(("parallel",)),
    )(p5, p5, y_conv, y_gla, dmerged)


def final_fwd_bwd(x2, mo, gate, final_g, target, n_samples):
    t = x2.shape[0]
    tiles = SEQ // TM_EW

    def body(x_ref, mo_ref, gate_ref, g_ref, t_ref, dh_ref, dmo_ref, dgate_ref, dg_ref, loss_ref):
        b, j = pl.program_id(0), pl.program_id(1)
        mo_v = mo_ref[...]
        h = x_ref[...] + gate_ref[0] * mo_v
        rs = lax.rsqrt(jnp.mean(h * h, axis=-1, keepdims=True) + EPS)
        nh = h * rs
        err = nh * g_ref[...] - t_ref[...]
        dy = err * (1.0 / D)
        dn = dy * g_ref[...]
        dh = rs * (dn - nh * jnp.mean(dn * nh, axis=-1, keepdims=True))
        dh_ref[...] = dh
        dmo_ref[...] = (dh * gate_ref[0]).astype(dmo_ref.dtype)

        @pl.when(j == 0)
        def _():
            dgate_ref[...] = jnp.zeros_like(dgate_ref)

        @pl.when((b == 0) & (j == 0))
        def _():
            dg_ref[...] = jnp.zeros_like(dg_ref)
            loss_ref[...] = jnp.zeros_like(loss_ref)

        dgate_ref[0] += jnp.sum(dh * mo_v, axis=0, keepdims=True)
        dg_ref[...] += jnp.sum(dy * nh, axis=0, keepdims=True)
        loss_ref[...] += (0.5 / D) * jnp.sum(err * err)

    row = pl.BlockSpec((TM_EW, D), lambda b, j: (b * tiles + j, 0))
    per = pl.BlockSpec((1, 1, D), lambda b, j: (b, 0, 0))
    vec = pl.BlockSpec((1, D), lambda b, j: (0, 0))
    return _pallas(
        body, name="final_fwd_bwd", grid=(n_samples, tiles),
        in_specs=[row, row, per, vec, row],
        out_specs=[row, row, per, vec, pl.BlockSpec((8, 128), lambda b, j: (0, 0))],
        out_shape=[jax.ShapeDtypeStruct((t, D), F32), jax.ShapeDtypeStruct((t, D), BF16),
                   jax.ShapeDtypeStruct((n_samples, 1, D), F32), jax.ShapeDtypeStruct((1, D), F32),
                   jax.ShapeDtypeStruct((8, 128), F32)],
        compiler_params=_params(("arbitrary", "arbitrary")),
    )(x2, mo, gate, final_g, target)


def local_step(x, ctx, target, mod, wts, small):
    n = x.shape[0]
    t = n * SEQ
    t_all = t + n * NCTX
    x2 = x.reshape(t, D)
    ctx2 = ctx.reshape(n * NCTX, D)
    tgt2 = target.reshape(t, D)
    scale1, shift, gate = mod

    u = norm_mod_fwd(x2, ctx2, scale1, shift, small["norm_g"])
    p1 = matmul_nn(u, wts["w1"], small["b1"], name="proj_conv", m=t, tm=1024, tn=1024, out_dtype=BF16)
    p2 = matmul_nn(u, wts["w2"], small["b2"], name="proj_z", m=t, tm=1024, tn=1024, out_dtype=BF16)
    p3 = matmul_nn(u, wts["w3"], small["b3"], name="proj_gla", m=t_all, tm=512, tn=W3, out_dtype=BF16)
    p4 = matmul_nn(u, wts["w4"], small["b4"], name="proj_r", m=t, tm=1024, tn=1024, out_dtype=BF16)
    p5 = matmul_nn(u, wts["w5"], small["b5"], name="proj_merge", m=t, tm=1024, tn=1024, out_dtype=BF16)

    aconv = conv_fwd(p1, small["conv_w"], small["conv_b"], n)
    ac = ln_gate_fwd(aconv, p2, small["conv_ln_g"], small["conv_ln_b"])
    y_conv = matmul_nn(ac, wts["conv_proj"], None, name="conv_proj_fwd", m=t, tm=1024, tn=1024, out_dtype=BF16)

    qs, ks, vs, cum_f, cum_b = gla_prep_fwd(p3, small["upf"], small["upb"], small["bias_f"], small["bias_b"], n)
    o_f, s_f, sfin_f = gla_scan_fwd(qs, ks, vs, cum_f, rev=False, name="gla_scan_fwd_f")
    o_b, s_b, sfin_b = gla_scan_fwd(qs, ks, vs, cum_b, rev=True, name="gla_scan_fwd_b")
    og = gla_out_fwd(o_f, o_b, p4, small["gla_norm_g"])
    y_gla = matmul_nn(og, wts["gla_proj"], None, name="gla_proj_fwd", m=t, tm=1024, tn=1024, out_dtype=BF16)

    merged = merge_fwd(p5, y_conv, y_gla)
    mo = matmul_nn(merged, wts["w_out"], None, name="w_out_fwd", m=t, tm=1024, tn=1024, out_dtype=F32)
    dh, dmo, dgate, d_final_g, loss = final_fwd_bwd(x2, mo, gate, small["final_norm_g"], tgt2, n)

    g = {"final_norm_g": d_final_g}
    dmerged = matmul_nt(dmo, wts["w_out"], name="w_out_dgrad", tm=512, out_dtype=BF16)
    g["w_out"] = matmul_tn(merged, dmo, name="w_out_wgrad", t=t, tn=1024, tt=1024)[0]
    dyc, dyg, dp5 = merge_bwd(p5, y_conv, y_gla, dmerged)

    dac = matmul_nt(dyc, wts["conv_proj"], name="conv_proj_dgrad", tm=512, out_dtype=BF16)
    g["conv_proj"] = matmul_tn(ac, dyc, name="conv_proj_wgrad", t=t, tn=1024, tt=1024)[0]
    daconv, dp2, g["conv_ln_g"], g["conv_ln_b"] = ln_gate_bwd(aconv, p2, dac, small["conv_ln_g"], small["conv_ln_b"])
    dp1, dconv_w, dconv_b = conv_bwd(p1, daconv, small["conv_w"], n)
    g["conv_w"], g["conv_b"] = dconv_w, dconv_b

    dog = matmul_nt(dyg, wts["gla_proj"], name="gla_proj_dgrad", tm=512, out_dtype=BF16)
    g["gla_proj"] = matmul_tn(og, dyg, name="gla_proj_wgrad", t=t, tn=1024, tt=1024)[0]
    do, dp4, g["gla_norm_g"] = gla_out_bwd(o_f, o_b, p4, dog, small["gla_norm_g"])
    dq_f, dk_f, dv_f, dc_f = gla_scan_bwd(qs, ks, vs, cum_f, s_f, sfin_f, do, rev=False, name="gla_scan_bwd_f")
    dq_b, dk_b, dv_b, dc_b = gla_scan_bwd(qs, ks, vs, cum_b, s_b, sfin_b, do, rev=True, name="gla_scan_bwd_b")
    dp3, g["upf"], g["upb"], g["bias_f"], g["bias_b"] = gla_prep_bwd(
        p3, dq_f, dq_b, dk_f, dk_b, dv_f, dv_b, dc_f, dc_b,
        small["upf"], small["upb"], small["bias_f"], small["bias_b"], n)

    dps = [dp1, dp2, dp3, dp4, dp5]
    for i, dp in enumerate(dps):
        rows = dp.shape[0]
        tn = W3 if dp.shape[1] == W3 else 1024
        g["w%d" % (i + 1)], g["b%d" % (i + 1)] = matmul_tn(
            u, dp, name="w_in_wgrad_%d" % (i + 1), t=rows, tn=tn, tt=1024 if rows % 1024 == 0 else 768, colsum=True)
    g["gate"] = dgate
    return loss, dh, dps, g


def _group_cols(w):
    gv, gg, z = w[..., 0:1024], w[..., 1024:2048], w[..., 2048:3072]
    q, k, v = w[..., 3072:3584], w[..., 3584:4096], w[..., 4096:5120]
    ab = w[..., 5120:5152]
    r, mc, mg = w[..., 5152:6176], w[..., 6176:7200], w[..., 7200:8224]
    g1 = jnp.concatenate([p for j in range(CONV_NCB)
                          for p in (gv[..., CONV_CB * j:CONV_CB * (j + 1)], gg[..., CONV_CB * j:CONV_CB * (j + 1)])], -1)
    pad = jnp.zeros(w.shape[:-1] + (W3 - 2080,), w.dtype)
    g3 = jnp.concatenate([v, q, k, ab, pad], -1)
    return g1, z, g3, r, jnp.concatenate([mc, mg], -1)


def _ungroup_cols(g1, g2, g3, g4, g5):
    gv = jnp.concatenate([g1[..., 2 * CONV_CB * j:2 * CONV_CB * j + CONV_CB] for j in range(CONV_NCB)], -1)
    gg = jnp.concatenate([g1[..., 2 * CONV_CB * j + CONV_CB:2 * CONV_CB * (j + 1)] for j in range(CONV_NCB)], -1)
    v, q, k, ab = g3[..., 0:1024], g3[..., 1024:1536], g3[..., 1536:2048], g3[..., 2048:2080]
    return jnp.concatenate([gv, gg, g2, q, k, v, ab, g4, g5[..., 0:1024], g5[..., 1024:2048]], -1)


def _natural_pieces():
    pieces = [(CONV_CB * j, CONV_CB, 0, 2 * CONV_CB * j) for j in range(CONV_NCB)]
    pieces += [(1024 + CONV_CB * j, CONV_CB, 0, 2 * CONV_CB * j + CONV_CB) for j in range(CONV_NCB)]
    pieces += [(2048, 1024, 1, 0), (3072, 512, 2, O3_Q), (3584, 512, 2, O3_K), (4096, 1024, 2, O3_V), (5120, 32, 2, O3_AB),
               (5152, 1024, 3, 0), (6176, 1024, 4, 0), (7200, 1024, 4, 1024)]
    return sorted(pieces)


def _ungroup_to_shards(groups):
    shards = []
    for i in range(N_CHIPS):
        lo, hi = i * W_IN_SHARD, (i + 1) * W_IN_SHARD
        parts = []
        for nat, width, g, gcol in _natural_pieces():
            a, b = max(nat, lo), min(nat + width, hi)
            if a < b:
                parts.append(groups[g][:, gcol + a - nat:gcol + b - nat])
        shards.append(jnp.concatenate(parts, 1))
    return jnp.stack(shards)


def _pad_up(up, row0):
    return jnp.zeros((128, GLA_DK), F32).at[row0:row0 + up.shape[0]].set(up)


def _adamw_math(w, g, m, v):
    m = ADAM_B1 * m + (1.0 - ADAM_B1) * g
    v = ADAM_B2 * v + (1.0 - ADAM_B2) * (g * g)
    m_hat = m / (1.0 - ADAM_B1 ** ADAM_STEP)
    v_hat = v / (1.0 - ADAM_B2 ** ADAM_STEP)
    delta = -ADAM_LR * (m_hat / (jnp.sqrt(v_hat) + ADAM_EPS) + ADAM_WD * w)
    return delta, m, v


def adamw2d(w, g, m, v, *, name, tr, tcols=None):
    rows, cols = w.shape[-2:]

    def body(w_ref, g_ref, m_ref, v_ref, d_ref, nm_ref, nv_ref):
        d_ref[...], nm_ref[...], nv_ref[...] = _adamw_math(w_ref[...], g_ref[...], m_ref[...], v_ref[...])

    tcols = cols if tcols is None else tcols
    if w.ndim == 3:
        spec = pl.BlockSpec((1, tr, tcols), lambda i, j: (0, i, j))
    else:
        spec = pl.BlockSpec((tr, tcols), lambda i, j: (i, j))
    return _pallas(
        body, name=name, grid=(rows // tr, cols // tcols), in_specs=[spec] * 4, out_specs=[spec] * 3,
        out_shape=[jax.ShapeDtypeStruct(w.shape, F32)] * 3, compiler_params=_params(("parallel", "parallel")),
    )(w, g, m, v)


def sum_devices(sall, *, name):
    rows = sall.shape[1]

    def body(s_ref, o_ref):
        acc = s_ref[0]
        for d in range(1, N_DEV):
            acc = acc + s_ref[d]
        o_ref[...] = acc

    return _pallas(body, name=name, out_shape=jax.ShapeDtypeStruct((rows, D), F32),
                   compiler_params=_params())(sall)


def pair_add(core, g, got, *, name, tr):
    n, rows, cols = got.shape
    g4 = g.reshape(n, 2, rows, cols)

    def body(core_ref, g_ref, got_ref, ob_ref):
        del core_ref
        ob_ref[0] = (g_ref[0, 0] + got_ref[0]).astype(BF16)

    spec = pl.BlockSpec((1, tr, cols), lambda i, t, core_ref: (i, t, 0))
    return _pallas(
        body, name=name,
        grid_spec=pltpu.PrefetchScalarGridSpec(
            num_scalar_prefetch=1, grid=(n, rows // tr),
            in_specs=[pl.BlockSpec((1, 1, tr, cols), lambda i, t, core_ref: (i, core_ref[0], t, 0)), spec],
            out_specs=spec),
        out_shape=jax.ShapeDtypeStruct(got.shape, BF16),
        compiler_params=_params(("parallel", "parallel")))(core, g4, got)


def chip_add(place, pa, rb, *, name, tr):
    _, rows, cols = pa.shape

    def body(place_ref, m_ref, r_ref, o_ref):
        del place_ref
        o_ref[0] = ((m_ref[0].astype(F32) + r_ref[0].astype(F32)) + r_ref[1].astype(F32)) + r_ref[2].astype(F32)

    return _pallas(
        body, name=name,
        grid_spec=pltpu.PrefetchScalarGridSpec(
            num_scalar_prefetch=1, grid=(rows // tr,),
            in_specs=[pl.BlockSpec((1, tr, cols), lambda t, place_ref: (place_ref[0], t, 0)),
                      pl.BlockSpec((3, tr, cols), lambda t, place_ref: (0, t, 0))],
            out_specs=pl.BlockSpec((1, tr, cols), lambda t, place_ref: (place_ref[1], t, 0))),
        out_shape=jax.ShapeDtypeStruct((2, rows, cols), F32),
        compiler_params=_params(("parallel",)))(place, pa, rb)


def ada_bwd(call, cctx_rows, dm_shard, dm_full, adaw):
    nsh = adaw.shape[1]

    def body(c_ref, cc_ref, dms_ref, dmf_ref, w_ref, gw_ref, gb_ref, pq_ref):
        a_lat = _silu(c_ref[...])
        a_ctx = _silu(cc_ref[...])
        dms = dms_ref[...]
        gw_ref[...] = _tn(a_lat, dms[0:64], HI) + _tn(a_ctx, dms[64:72], HI)
        gb_ref[...] = jnp.sum(dmf_ref[...], axis=0, keepdims=True)
        part = _nt(dms[64:72], w_ref[...], HI)
        pq_ref[...] = jnp.zeros_like(pq_ref) + jnp.sum(part, axis=0, keepdims=True)

    return _pallas(body, name="ada_bwd",
                   out_shape=[jax.ShapeDtypeStruct((D, nsh), F32), jax.ShapeDtypeStruct((1, 3 * D), F32),
                              jax.ShapeDtypeStruct((8, D), F32)],
                   compiler_params=_params())(call, cctx_rows, dm_shard, dm_full, adaw)


def cctx_grad(pq_all, cctx_rows):
    def body(p_ref, c_ref, o_ref):
        acc = p_ref[0]
        for qi in range(1, N_CHIPS):
            acc = acc + p_ref[qi]
        o_ref[...] = acc * _dsilu(c_ref[...])

    return _pallas(body, name="cctx_grad", out_shape=jax.ShapeDtypeStruct((8, D), F32),
                   compiler_params=_params())(pq_all, cctx_rows)


def _place():
    x, y, c = lax.axis_index("x"), lax.axis_index("y"), lax.axis_index("c")
    chips = [(1 - x, y), (x, 1 - y), (1 - x, 1 - y)]
    return x, y, c, chips


def _all_peers(x, y, c):
    return [((1 - x) if r & 4 else x, (1 - y) if r & 2 else y, (1 - c) if r & 1 else c) for r in range(1, N_DEV)]


def _remote(src, dst, send_sem, recv_sem, dev):
    return pltpu.make_async_remote_copy(src_ref=src, dst_ref=dst, send_sem=send_sem, recv_sem=recv_sem,
                                        device_id=dev, device_id_type=MESH)


ANY = pl.BlockSpec(memory_space=pl.ANY)
VMEM = pl.BlockSpec(memory_space=pltpu.VMEM)
F_ROWS = 16


W_ROW_CHUNKS = 4
P_ROW_CHUNKS = 2
N_BULK = W_ROW_CHUNKS + P_ROW_CHUNKS


def _half_chunks(core, n_rows, align):
    out = []
    for a, k in ((0, W_ROW_CHUNKS), (1, P_ROW_CHUNKS)):
        half = n_rows[a] // 2
        size = half // k
        for i in range(k):
            start = core * half + i * size
            out.append((a, pl.ds(start if isinstance(start, int) else pl.multiple_of(start, align), size)))
    return out


def gather_weights(c8, cctx8, adaw, adab, w_sh, p_sh, fp):
    nsh = adaw.shape[1]

    def body(c_ref, cctx_ref, adaw_ref, adab_ref, w_ref, p_ref, fp_ref, wall_ref, pall_ref, fall_ref, call_ref, mall_ref,
             abuf, w_send, w_recv, h_send, h_recv, c_send, c_recv, m_send, m_recv, f_send, f_recv):
        x, y, c, chips = _place()
        q = 2 * x + y
        dev = 4 * x + 2 * y + c
        qs = [2 * cx + cy for cx, cy in chips]
        sib = (x, y, 1 - c)
        srcs, dsts = (w_ref, p_ref), (wall_ref, pall_ref)
        n_rows = (w_ref.shape[0], p_ref.shape[0])
        mine = _half_chunks(c, n_rows, 16)
        other = _half_chunks(1 - c, n_rows, 16)

        bulk = [[_remote(srcs[a].at[rows], dsts[a].at[q, rows], w_send.at[j * N_BULK + i], w_recv.at[j * N_BULK + i],
                         (*chips[j], c)) for i, (a, rows) in enumerate(mine)] for j in range(3)]
        fall_ref[q] = fp_ref[...]
        small = [_remote(fp_ref, fall_ref.at[q], f_send.at[j], f_recv.at[j], (*chips[j], c)) for j in range(3)]
        my_rows = pl.ds(pl.multiple_of(8 * dev, 8), 8)
        call_ref[my_rows, :] = c_ref[...]
        cond = [_remote(c_ref, call_ref.at[my_rows, :], c_send.at[r], c_recv.at[r], peer)
                for r, peer in enumerate(_all_peers(x, y, c))]
        for cp in sum(bulk, []) + small + cond:
            cp.start()
        for cp in cond:
            cp.wait_recv()

        abuf[pl.ds(0, 64), :] = _silu(call_ref[...])
        abuf[pl.ds(64, 8), :] = _silu(cctx_ref[...])
        mall_ref[q] = _nn(abuf[...], adaw_ref[...], HI) + adab_ref[...]
        mod = [_remote(mall_ref.at[q], mall_ref.at[q], m_send.at[j], m_recv.at[j], (*chips[j], c)) for j in range(3)]
        for cp in mod:
            cp.start()

        handed = []
        for j in range(3):
            for i, (a, rows) in enumerate(mine):
                bulk[j][i].wait_recv()
                cp = _remote(dsts[a].at[qs[j], rows], dsts[a].at[qs[j], rows],
                             h_send.at[j * N_BULK + i], h_recv.at[j * N_BULK + i], sib)
                cp.start()
                handed.append(cp)
        for j in range(3):
            for i, (a, rows) in enumerate(other):
                _remote(dsts[a].at[qs[j], rows], dsts[a].at[qs[j], rows],
                        h_send.at[j * N_BULK + i], h_recv.at[j * N_BULK + i], sib).wait_recv()
        for cp in mod + small:
            cp.wait_recv()
        for cp in sum(bulk, []) + small + cond + mod + handed:
            cp.wait_send()

    def dma(n):
        return pltpu.SemaphoreType.DMA((n,))

    return _pallas(
        body, name="gather_weights",
        in_specs=[VMEM, VMEM, VMEM, VMEM, ANY, ANY, VMEM],
        out_specs=[ANY, ANY, VMEM, VMEM, VMEM],
        out_shape=[jax.ShapeDtypeStruct((N_CHIPS,) + w_sh.shape, BF16), jax.ShapeDtypeStruct((N_CHIPS,) + p_sh.shape, BF16),
                   jax.ShapeDtypeStruct((N_CHIPS, F_ROWS, D), F32),
                   jax.ShapeDtypeStruct((8 * N_DEV, D), F32), jax.ShapeDtypeStruct((N_CHIPS, MOD_ROWS, nsh), F32)],
        scratch_shapes=[pltpu.VMEM((MOD_ROWS, D), F32), dma(3 * N_BULK), dma(3 * N_BULK), dma(3 * N_BULK), dma(3 * N_BULK),
                        dma(7), dma(7), dma(3), dma(3), dma(3), dma(3)],
        compiler_params=_params(),
    )(c8, cctx8, adaw, adab, w_sh, p_sh, fp)


def pair_swap(gs, gp, sm):
    n_gs = len(gs)
    n_pair = n_gs * W_ROW_CHUNKS + N_CHIPS * P_ROW_CHUNKS

    def body(*refs):
        g_refs, gp_ref, sm_ref = refs[:n_gs], refs[n_gs], refs[n_gs + 1]
        got_refs, gotp_ref, sall_ref = refs[n_gs + 2:2 * n_gs + 2], refs[2 * n_gs + 2], refs[2 * n_gs + 3]
        a_send, a_recv, s_send, s_recv = refs[2 * n_gs + 4:]
        x, y, c, _ = _place()
        dev = 4 * x + 2 * y + c
        sib = (x, y, 1 - c)
        pair = []
        half, size = D // 2, D // 2 // W_ROW_CHUNKS
        for gi in range(n_gs):
            for i in range(W_ROW_CHUNKS):
                k = len(pair)
                rows_o = pl.ds(pl.multiple_of((1 - c) * half + i * size, 8), size)
                pair.append(_remote(g_refs[gi].at[rows_o], got_refs[gi].at[pl.ds(i * size, size)],
                                    a_send.at[k], a_recv.at[k], sib))
        half, size = gp_ref.shape[1] // 2, gp_ref.shape[1] // 2 // P_ROW_CHUNKS
        for s in range(N_CHIPS):
            for i in range(P_ROW_CHUNKS):
                k = len(pair)
                rows_o = pl.ds(pl.multiple_of((1 - c) * half + i * size, 8), size)
                pair.append(_remote(gp_ref.at[s, rows_o], gotp_ref.at[s, pl.ds(i * size, size)],
                                    a_send.at[k], a_recv.at[k], sib))
        sall_ref[dev] = sm_ref[...]
        small = [_remote(sm_ref, sall_ref.at[dev], s_send.at[r], s_recv.at[r], peer)
                 for r, peer in enumerate(_all_peers(x, y, c))]
        for cp in pair + small:
            cp.start()
        for cp in small + pair:
            cp.wait_recv()
        for cp in small + pair:
            cp.wait_send()

    return _pallas(
        body, name="pair_swap", in_specs=[ANY] * (n_gs + 1) + [VMEM], out_specs=[ANY] * (n_gs + 1) + [VMEM],
        out_shape=[jax.ShapeDtypeStruct((D // 2, a.shape[1]), F32) for a in gs]
        + [jax.ShapeDtypeStruct((N_CHIPS, gp.shape[1] // 2, gp.shape[2]), F32),
           jax.ShapeDtypeStruct((N_DEV,) + sm.shape, F32)],
        scratch_shapes=[pltpu.SemaphoreType.DMA((n_pair,)), pltpu.SemaphoreType.DMA((n_pair,)),
                        pltpu.SemaphoreType.DMA((N_DEV - 1,)), pltpu.SemaphoreType.DMA((N_DEV - 1,))],
        compiler_params=_params(),
    )(*gs, gp, sm)


def gather_small(sm):
    rows = sm.shape[0]

    def body(sm_ref, sall_ref, s_send, s_recv):
        x, y, c, _ = _place()
        dev = 4 * x + 2 * y + c
        sall_ref[dev] = sm_ref[...]
        small = [_remote(sm_ref, sall_ref.at[dev], s_send.at[r], s_recv.at[r], peer)
                 for r, peer in enumerate(_all_peers(x, y, c))]
        for cp in small:
            cp.start()
        for cp in small:
            cp.wait_recv()
        for cp in small:
            cp.wait_send()

    return _pallas(
        body, name="gather_small", in_specs=[VMEM], out_specs=VMEM,
        out_shape=jax.ShapeDtypeStruct((N_DEV, rows, D), F32),
        scratch_shapes=[pltpu.SemaphoreType.DMA((7,)), pltpu.SemaphoreType.DMA((7,))],
        compiler_params=_params(),
    )(sm)


def pair_share(ghw, ghp, pq):
    def body(ghw_ref, ghp_ref, pq_ref, outw_ref, outp_ref, pqa_ref, send, recv, p_send, p_recv):
        del ghw_ref, ghp_ref
        x, y, c, chips = _place()
        q = 2 * x + y
        refs = (outw_ref, outp_ref)
        n_rows = (2 * outw_ref.shape[1], 2 * outp_ref.shape[1])
        pair = [_remote(refs[a].at[c, rows], refs[a].at[c, rows], send.at[i], recv.at[i], (x, y, 1 - c))
                for i, (a, rows) in enumerate(_half_chunks(0, n_rows, 8))]
        pqa_ref[q] = pq_ref[...]
        small = [_remote(pq_ref, pqa_ref.at[q], p_send.at[j], p_recv.at[j], (*chips[j], c)) for j in range(3)]
        for cp in pair + small:
            cp.start()
        for i, (a, rows) in enumerate(_half_chunks(0, n_rows, 8)):
            _remote(refs[a].at[1 - c, rows], refs[a].at[1 - c, rows], send.at[i], recv.at[i], (x, y, 1 - c)).wait_recv()
        for cp in small:
            cp.wait_recv()
        for cp in pair + small:
            cp.wait_send()

    return _pallas(
        body, name="pair_share", in_specs=[ANY, ANY, VMEM], out_specs=[ANY, ANY, VMEM],
        out_shape=[jax.ShapeDtypeStruct(ghw.shape, F32), jax.ShapeDtypeStruct(ghp.shape, F32),
                   jax.ShapeDtypeStruct((N_CHIPS, 8, D), F32)],
        scratch_shapes=[pltpu.SemaphoreType.DMA((N_BULK,)), pltpu.SemaphoreType.DMA((N_BULK,)),
                        pltpu.SemaphoreType.DMA((3,)), pltpu.SemaphoreType.DMA((3,))],
        input_output_aliases={0: 0, 1: 1},
        compiler_params=_params(),
    )(ghw, ghp, pq)


def _rows_of(shape):
    size = 1
    for s in shape:
        size *= s
    return -(-size // D)


def _pack(arrs, rows_multiple=8):
    parts = []
    total = 0
    for a in arrs:
        f = a.reshape(-1).astype(F32)
        r = _rows_of(a.shape)
        parts.append(jnp.pad(f, (0, r * D - f.shape[0])))
        total += r
    pad_rows = (-total) % rows_multiple
    if pad_rows:
        parts.append(jnp.zeros((pad_rows * D,), F32))
    return jnp.concatenate(parts).reshape(-1, D)


def _unpack(p, shapes):
    out = []
    r0 = 0
    for shp in shapes:
        r = _rows_of(shp)
        size = 1
        for s in shp:
            size *= s
        out.append(p[r0:r0 + r].reshape(-1)[:size].reshape(shp))
        r0 += r
    return out


WEIGHT_NAMES = ['c_ctx', 'ada_w', 'ada_b', 'norm_g', 'w_in', 'b_in', 'conv_w', 'conv_b', 'conv_ln_g', 'conv_ln_b',
                'conv_proj', 'decay_up_fwd', 'decay_bias_fwd', 'decay_up_bwd', 'decay_bias_bwd', 'gla_norm_g', 'gla_proj',
                'w_out', 'final_norm_g']
SMALL_NAMES = ['c_ctx', 'ada_b', 'norm_g', 'b_in', 'conv_w', 'conv_b', 'conv_ln_g', 'conv_ln_b', 'decay_up_fwd',
               'decay_bias_fwd', 'decay_up_bwd', 'decay_bias_bwd', 'gla_norm_g', 'final_norm_g']


def kernel(x, c, ctx, c_ctx, ada_w, ada_b, norm_g, w_in, b_in, conv_w, conv_b, conv_ln_g, conv_ln_b, conv_proj, decay_up_fwd, decay_bias_fwd, decay_up_bwd, decay_bias_bwd, gla_norm_g, gla_proj, w_out, final_norm_g, loss_target, m_c_ctx, m_ada_w, m_ada_b, m_norm_g, m_w_in, m_b_in, m_conv_w, m_conv_b, m_conv_ln_g, m_conv_ln_b, m_conv_proj, m_decay_up_fwd, m_decay_bias_fwd, m_decay_up_bwd, m_decay_bias_bwd, m_gla_norm_g, m_gla_proj, m_w_out, m_final_norm_g, v_c_ctx, v_ada_w, v_ada_b, v_norm_g, v_w_in, v_b_in, v_conv_w, v_conv_b, v_conv_ln_g, v_conv_ln_b, v_conv_proj, v_decay_up_fwd, v_decay_bias_fwd, v_decay_up_bwd, v_decay_bias_bwd, v_gla_norm_g, v_gla_proj, v_w_out, v_final_norm_g):
    w = dict(c_ctx=c_ctx, ada_w=ada_w, ada_b=ada_b, norm_g=norm_g, w_in=w_in, b_in=b_in, conv_w=conv_w, conv_b=conv_b,
             conv_ln_g=conv_ln_g, conv_ln_b=conv_ln_b, conv_proj=conv_proj, decay_up_fwd=decay_up_fwd,
             decay_bias_fwd=decay_bias_fwd, decay_up_bwd=decay_up_bwd, decay_bias_bwd=decay_bias_bwd,
             gla_norm_g=gla_norm_g, gla_proj=gla_proj, w_out=w_out, final_norm_g=final_norm_g)
    m = dict(c_ctx=m_c_ctx, ada_w=m_ada_w, ada_b=m_ada_b, norm_g=m_norm_g, w_in=m_w_in, b_in=m_b_in, conv_w=m_conv_w,
             conv_b=m_conv_b, conv_ln_g=m_conv_ln_g, conv_ln_b=m_conv_ln_b, conv_proj=m_conv_proj,
             decay_up_fwd=m_decay_up_fwd, decay_bias_fwd=m_decay_bias_fwd, decay_up_bwd=m_decay_up_bwd,
             decay_bias_bwd=m_decay_bias_bwd, gla_norm_g=m_gla_norm_g, gla_proj=m_gla_proj, w_out=m_w_out,
             final_norm_g=m_final_norm_g)
    v = dict(c_ctx=v_c_ctx, ada_w=v_ada_w, ada_b=v_ada_b, norm_g=v_norm_g, w_in=v_w_in, b_in=v_b_in, conv_w=v_conv_w,
             conv_b=v_conv_b, conv_ln_g=v_conv_ln_g, conv_ln_b=v_conv_ln_b, conv_proj=v_conv_proj,
             decay_up_fwd=v_decay_up_fwd, decay_bias_fwd=v_decay_bias_fwd, decay_up_bwd=v_decay_up_bwd,
             decay_bias_bwd=v_decay_bias_bwd, gla_norm_g=v_gla_norm_g, gla_proj=v_gla_proj, w_out=v_w_out,
             final_norm_g=v_final_norm_g)
    n = x.shape[0]
    ax, ay, ac = lax.axis_index("x"), lax.axis_index("y"), lax.axis_index("c")
    q = 2 * ax + ay
    dev = 4 * ax + 2 * ay + ac
    nsh = ada_w.shape[2]

    w_sh = w_in[0].astype(BF16)
    p_sh = jnp.concatenate([conv_proj[0], gla_proj[0], w_out[0]], 0).astype(BF16)
    fp = _pack([conv_w[0], decay_up_fwd[0], decay_up_bwd[0]], F_ROWS)
    c8 = jnp.pad(c, ((0, 8 - n), (0, 0)))
    cctx8 = jnp.pad(c_ctx[None], ((0, 7), (0, 0)))
    adab_sh = lax.dynamic_slice(ada_b, (0, q * nsh), (1, nsh))
    w_all, p_all, fall, call, mall = gather_weights(c8, cctx8, ada_w[0], adab_sh, w_sh, p_sh, fp)

    mod_all = jnp.transpose(mall, (1, 0, 2)).reshape(MOD_ROWS, 3 * D)
    mod_mine = lax.dynamic_slice(mod_all, (8 * dev, 0), (n, 3 * D))
    mod_ctx = mod_all[64:65]
    shift = jnp.concatenate([mod_mine[:, 0:D], mod_ctx[:, 0:D]], 0)[:, None, :]
    scale1 = 1.0 + jnp.concatenate([mod_mine[:, D:2 * D], mod_ctx[:, D:2 * D]], 0)[:, None, :]
    gate = mod_mine[:, 2 * D:3 * D][:, None, :]

    own = lambda i, mine, got: jnp.where(q == i, mine, got)
    g1, g2, g3, g4, g5 = _group_cols(jnp.concatenate([own(i, w_sh, w_all[i]) for i in range(N_CHIPS)], 1))
    p_full = jnp.stack([own(i, p_sh, p_all[i]) for i in range(N_CHIPS)])
    wts = dict(w1=g1, w2=g2, w3=g3, w4=g4, w5=g5,
               conv_proj=p_full[:, 0:256].reshape(D, D), gla_proj=p_full[:, 256:512].reshape(D, D),
               w_out=p_full[:, 512:768].reshape(D, D))
    f_parts = [_unpack(fall[i], [conv_w.shape[1:], decay_up_fwd.shape[1:], decay_up_bwd.shape[1:]]) for i in range(N_CHIPS)]
    conv_w_full = jnp.concatenate([p[0] for p in f_parts], 1)
    upf_full = jnp.concatenate([p[1] for p in f_parts], 1)
    upb_full = jnp.concatenate([p[2] for p in f_parts], 1)
    b1, b2, b3, b4, b5 = _group_cols(b_in)
    small = dict(b1=b1, b2=b2, b3=b3, b4=b4, b5=b5, norm_g=norm_g,
                 conv_w=jnp.pad(conv_w_full, ((0, 1), (0, 0))), conv_b=conv_b, conv_ln_g=conv_ln_g, conv_ln_b=conv_ln_b,
                 upf=_split3(_pad_up(upf_full, 0)), upb=_split3(_pad_up(upb_full, 16)),
                 bias_f=decay_bias_fwd, bias_b=decay_bias_bwd,
                 gla_norm_g=gla_norm_g, final_norm_g=final_norm_g[None])

    loss_part, dh, dps, g = local_step(x, ctx, loss_target, (scale1, shift, gate), wts, small)
    loss = lax.psum(loss_part[0, 0], ("x", "y", "c"))

    gs = [g["w%d" % i] for i in range(1, 6)]
    gp = jnp.concatenate([g["conv_proj"].reshape(N_CHIPS, 256, D), g["gla_proj"].reshape(N_CHIPS, 256, D),
                          g["w_out"].reshape(N_CHIPS, 256, D)], 1)
    d_b_in = _ungroup_cols(*[g["b%d" % i] for i in range(1, 6)])
    early = [d_b_in, g["conv_b"].sum(0), g["conv_ln_g"], g["conv_ln_b"], g["bias_f"], g["bias_b"],
             g["gla_norm_g"], g["final_norm_g"], g["conv_w"].sum(0)[:CONV_K], g["upf"][0:16], g["upb"][16:32]]
    early_shapes = [a.shape for a in early]
    *gots, gotp, sall1 = pair_swap(gs, gp, _pack(early))
    core = ac.astype(jnp.int32).reshape(1)
    chip = q.astype(jnp.int32).reshape(1)
    halves = [pair_add(core, a[None], got[None], name="pair_add_w%d" % (i + 1), tr=128)[0]
              for i, (a, got) in enumerate(zip(gs, gots))]
    paw16 = _ungroup_to_shards(halves)
    pap16 = pair_add(core, gp, gotp, name="pair_add_p", tr=384)
    grad_x2, dshift, dscale, g["norm_g"], rbw, rbp = dgrad_norm_bwd(
        dps, [wts["w%d" % i] for i in range(1, 6)], paw16, pap16, x.reshape(n * SEQ, D), ctx.reshape(n * NCTX, D), dh,
        scale1, norm_g, tm=256)

    dm_mine = jnp.concatenate([dshift[:n, 0], dscale[:n, 0], g["gate"][:, 0]], -1)
    dm_ctx = jnp.concatenate([dshift[n, 0], dscale[n, 0], jnp.zeros((D,), F32)], -1)
    late = [g["norm_g"], dm_mine, dm_ctx]
    late_shapes = [a.shape for a in late]
    sall2 = gather_small(_pack(late))
    (s_b_in, s_conv_b, s_ln_g, s_ln_b, s_bias_f, s_bias_b, s_gla_g, s_final_g, s_conv_w, s_upf,
     s_upb) = _unpack(sum_devices(sall1, name="sum_devices_early"), early_shapes)
    s_norm_g = _unpack(sum_devices(sall2, name="sum_devices_late"), late_shapes)[0]
    dm_rows = [_unpack(sall2[i], late_shapes)[-2:] for i in range(N_DEV)]
    dm_full = jnp.concatenate(
        [jnp.pad(jnp.stack([r[0] for r in dm_rows]), ((0, 0), (0, 8 - n), (0, 0))).reshape(8 * N_DEV, 3 * D),
         jnp.stack([r[1] for r in dm_rows])], 0)
    dm_shard = lax.dynamic_slice(dm_full, (0, q * nsh), (MOD_ROWS, nsh))
    cctx_rows = jnp.broadcast_to(c_ctx[None], (8, D))
    g_ada_w, g_ada_b, pq = ada_bwd(call, cctx_rows, dm_shard, dm_full, ada_w[0])

    place = jnp.concatenate([chip, core])
    ghw = chip_add(place, paw16, rbw, name="chip_add_w", tr=128)
    ghp = chip_add(place, pap16, rbp, name="chip_add_p", tr=384)
    gw_mine, gp_mine, pq_all = pair_share(ghw, ghp, pq)
    gp_mine = gp_mine.reshape(768, D)
    g_c_ctx = cctx_grad(pq_all, cctx_rows)[0]

    grads = dict(
        c_ctx=g_c_ctx, ada_w=g_ada_w[None], ada_b=g_ada_b, norm_g=s_norm_g,
        w_in=gw_mine.reshape(1, D, W_IN_SHARD), b_in=s_b_in,
        conv_w=lax.dynamic_slice(s_conv_w, (0, q * 256), (CONV_K, 256))[None], conv_b=s_conv_b,
        conv_ln_g=s_ln_g, conv_ln_b=s_ln_b, conv_proj=gp_mine[0:256][None],
        decay_up_fwd=lax.dynamic_slice(s_upf, (0, q * 128), (16, 128))[None], decay_bias_fwd=s_bias_f,
        decay_up_bwd=lax.dynamic_slice(s_upb, (0, q * 128), (16, 128))[None], decay_bias_bwd=s_bias_b,
        gla_norm_g=s_gla_g, gla_proj=gp_mine[256:512][None], w_out=gp_mine[512:768][None],
        final_norm_g=s_final_g[0])

    delta, new_m, new_v = {}, {}, {}
    for name in ["ada_w", "conv_proj", "gla_proj", "w_out"]:
        delta[name], new_m[name], new_v[name] = adamw2d(w[name], grads[name].reshape(w[name].shape), m[name], v[name],
                                                        name="adamw_" + name, tr=128)
    tr_ = lambda a: jnp.swapaxes(a, 1, 2)
    g_w_in_t = tr_(grads["w_in"])
    grads["w_in"] = tr_(g_w_in_t)
    d_, m_, v_ = adamw2d(tr_(w_in), g_w_in_t, tr_(m_w_in), tr_(v_w_in), name="adamw_w_in", tr=W_IN_SHARD, tcols=128)
    delta["w_in"], new_m["w_in"], new_v["w_in"] = tr_(d_), tr_(m_), tr_(v_)
    shapes = [w[nm].shape for nm in SMALL_NAMES]
    packs = [_pack([src[nm] for nm in SMALL_NAMES]) for src in (w, grads, m, v)]
    d_, m_, v_ = adamw2d(*packs, name="adamw_small", tr=packs[0].shape[0])
    for nm, a, b, cc in zip(SMALL_NAMES, _unpack(d_, shapes), _unpack(m_, shapes), _unpack(v_, shapes)):
        delta[nm], new_m[nm], new_v[nm] = a, b, cc

    grad_x = grad_x2.reshape(x.shape)
    return (loss, grad_x, *[grads[nm].reshape(w[nm].shape) for nm in WEIGHT_NAMES], *[delta[nm] for nm in WEIGHT_NAMES],
            *[new_m[nm] for nm in WEIGHT_NAMES], *[new_v[nm] for nm in WEIGHT_NAMES])
```

```python
import jax
import jax.numpy as jnp
from jax import lax
from jax.experimental import pallas as pl
from jax.experimental.pallas import tpu as pltpu

F32 = jnp.float32
BF16 = jnp.bfloat16
MESH = pl.DeviceIdType.MESH
HI = lax.Precision.HIGHEST

D = 1024
SEQ = 2048
GRID_W = 64
GRID_H = SEQ // GRID_W
NCTX = 256
SEQ_ALL = SEQ + NCTX
EPS = 1e-6
CONV_K = 31
CONV_PAD = CONV_K // 2
HEADS = 4
HEAD_K = 128
HEAD_V = 256
GLA_DK = HEADS * HEAD_K
GATE_TAU = 16.0
Q_SCALE = HEAD_K ** -0.5
CHUNK = 64
NCHUNK = SEQ_ALL // CHUNK
NCHUNK_LAT = SEQ // CHUNK
NCHUNK_CTX = NCHUNK - NCHUNK_LAT
SUB = 64
NSUB = CHUNK // SUB
N_IN = 8224
W3 = 2176
O3_V, O3_Q, O3_K, O3_AB = 0, 1024, 1536, 2048

ADAM_LR, ADAM_B1, ADAM_B2, ADAM_EPS, ADAM_WD, ADAM_STEP = 0.001, 0.9, 0.999, 1e-08, 0.01, 10
VMEM_LIMIT = 56 * 1024 * 1024

N_CHIPS = 4
N_DEV = 8
W_IN_SHARD = N_IN // N_CHIPS
MOD_ROWS = 72


def _pallas(body, **kw):
    return pl.pallas_call(body, **kw)


def _params(sem=None, **kw):
    if sem is not None:
        kw["dimension_semantics"] = sem
    return pltpu.CompilerParams(vmem_limit_bytes=VMEM_LIMIT, **kw)


def _sigmoid(v):
    return 1.0 / (1.0 + jnp.exp(-v))


def _silu(v):
    return v * _sigmoid(v)


def _dsilu(v):
    s = _sigmoid(v)
    return s * (1.0 + v * (1.0 - s))


def _log_sigmoid(v):
    return jnp.minimum(v, 0.0) - jnp.log(1.0 + jnp.exp(-jnp.abs(v)))


def _dot(a, b, dims, precision=None):
    return lax.dot_general(a, b, (dims, ((), ())), preferred_element_type=F32, precision=precision)


def _nn(a, b, precision=None):
    return _dot(a, b, ((1,), (0,)), precision)


def _nt(a, b, precision=None):
    return _dot(a, b, ((1,), (1,)), precision)


def _tn(a, b, precision=None):
    return _dot(a, b, ((0,), (0,)), precision)


def _b16(v):
    return v.astype(BF16)


def matmul_nn(a, b, bias, *, name, m, tm, tn, out_dtype):
    k = a.shape[1]
    n = b.shape[1]
    has_bias = bias is not None

    def body(*refs):
        if has_bias:
            a_ref, b_ref, bias_ref, o_ref = refs
            acc = _nn(a_ref[...], b_ref[...]) + bias_ref[...]
        else:
            a_ref, b_ref, o_ref = refs
            acc = _nn(a_ref[...], b_ref[...])
        o_ref[...] = acc.astype(o_ref.dtype)

    in_specs = [pl.BlockSpec((tm, k), lambda j, i: (i, 0)), pl.BlockSpec((k, tn), lambda j, i: (0, j))]
    args = [a, b]
    if has_bias:
        in_specs.append(pl.BlockSpec((1, tn), lambda j, i: (0, j)))
        args.append(bias)
    return _pallas(
        body, name=name, grid=(n // tn, m // tm), in_specs=in_specs,
        out_specs=pl.BlockSpec((tm, tn), lambda j, i: (i, j)),
        out_shape=jax.ShapeDtypeStruct((m, n), out_dtype),
        compiler_params=_params(("parallel", "parallel")),
    )(*args)


def matmul_nt(a, b, *, name, tm, out_dtype):
    m, k = a.shape
    n = b.shape[0]

    def body(a_ref, b_ref, o_ref):
        o_ref[...] = _nt(a_ref[...], b_ref[...]).astype(o_ref.dtype)

    return _pallas(
        body, name=name, grid=(m // tm,),
        in_specs=[pl.BlockSpec((tm, k), lambda i: (i, 0)), pl.BlockSpec((n, k), lambda i: (0, 0))],
        out_specs=pl.BlockSpec((tm, n), lambda i: (i, 0)),
        out_shape=jax.ShapeDtypeStruct((m, n), out_dtype),
        compiler_params=_params(("parallel",)),
    )(a, b)


def matmul_tn(a, b, *, name, t, tn, tt, colsum=False):
    m = a.shape[1]
    n = b.shape[1]

    def body(a_ref, b_ref, o_ref, *rest):
        @pl.when(pl.program_id(1) == 0)
        def _():
            o_ref[...] = jnp.zeros_like(o_ref)
            if colsum:
                rest[0][...] = jnp.zeros_like(rest[0])
        o_ref[...] += _tn(a_ref[...], b_ref[...])
        if colsum:
            rest[0][...] += jnp.sum(b_ref[...].astype(F32), axis=0, keepdims=True)

    out_specs = [pl.BlockSpec((m, tn), lambda j, s: (0, j))]
    out_shape = [jax.ShapeDtypeStruct((m, n), F32)]
    if colsum:
        out_specs.append(pl.BlockSpec((1, tn), lambda j, s: (0, j)))
        out_shape.append(jax.ShapeDtypeStruct((1, n), F32))
    return _pallas(
        body, name=name, grid=(n // tn, t // tt),
        in_specs=[pl.BlockSpec((tt, m), lambda j, s: (s, 0)), pl.BlockSpec((tt, tn), lambda j, s: (s, j))],
        out_specs=out_specs, out_shape=out_shape,
        compiler_params=_params(("parallel", "arbitrary")),
    )(a, b)


def dgrad_norm_bwd(dps, wts, paw, pap, x2, ctx2, dh, scale1, norm_g, *, tm):
    t, tc = x2.shape[0], ctx2.shape[0]
    t_all = t + tc
    n_lat, n_ctx = t // tm, tc // tm
    n_tiles = n_lat + n_ctx
    n_samples = scale1.shape[0] - 1
    tps = n_lat // n_samples
    n_grp = n_samples + 1
    n_g = len(dps)
    whole = [g for g in range(n_g) if dps[g].shape[0] == t_all]
    latent = [g for g in range(n_g) if dps[g].shape[0] != t_all]

    def body(*refs):
        dp_refs, w_refs = refs[:n_g], refs[n_g:2 * n_g]
        (paw_ref, pap_ref, x_ref, c_ref, dh_ref, sc_ref, g_ref, dx_ref, dsh_ref, dsc_ref, dg_ref, rbw_ref, rbp_ref,
         du_buf, b_send, b_recv) = refs[2 * n_g:]
        i = pl.program_id(0)

        def exchange():
            x, y, c, chips = _place()
            srcs, dsts = (paw_ref, pap_ref), (rbw_ref, rbp_ref)
            n_rows = (2 * paw_ref.shape[1], 2 * pap_ref.shape[1])
            return [_remote(srcs[a].at[2 * cx + cy, rows], dsts[a].at[j, rows],
                            b_send.at[j * N_BULK + k], b_recv.at[j * N_BULK + k], (cx, cy, c))
                    for j, (cx, cy) in enumerate(chips) for k, (a, rows) in enumerate(_half_chunks(0, n_rows, 16))]

        @pl.when(i == 0)
        def _():
            for cp in exchange():
                cp.start()

        acc = None
        for g in whole:
            part = _nt(dp_refs[g][...], w_refs[g][...])
            acc = part if acc is None else acc + part
        du_buf[...] = acc

        @pl.when(i < n_lat)
        def _():
            lat = None
            for g in latent:
                part = _nt(dp_refs[g][...], w_refs[g][...])
                lat = part if lat is None else lat + part
            du_buf[...] += lat

        duv = du_buf[...]
        xv = jnp.where(i < n_lat, x_ref[...], c_ref[...])
        rs = lax.rsqrt(jnp.mean(xv * xv, axis=-1, keepdims=True) + EPS)
        xh = xv * rs
        n = xh * g_ref[...]
        dn = duv * sc_ref[0]
        dxh = dn * g_ref[...]
        dx = rs * (dxh - xh * jnp.mean(dxh * xh, axis=-1, keepdims=True))

        @pl.when(i < n_lat)
        def _():
            dx_ref[...] = dx + dh_ref[...]

        @pl.when((i % tps == 0) & (i <= n_lat))
        def _():
            dsh_ref[...] = jnp.zeros_like(dsh_ref)
            dsc_ref[...] = jnp.zeros_like(dsc_ref)

        @pl.when(i == 0)
        def _():
            dg_ref[...] = jnp.zeros_like(dg_ref)

        dsh_ref[0] += jnp.sum(duv, axis=0, keepdims=True)
        dsc_ref[0] += jnp.sum(duv * n, axis=0, keepdims=True)
        dg_ref[...] += jnp.sum(dn * xh, axis=0, keepdims=True)

        @pl.when(i == n_tiles - 1)
        def _():
            for cp in exchange():
                cp.wait_recv()
            for cp in exchange():
                cp.wait_send()

    lat = lambda i: (jnp.minimum(i, n_lat - 1), 0)
    grp = lambda i: (jnp.minimum(i // tps, n_samples), 0, 0)
    in_specs = []
    for g, dp in enumerate(dps):
        nrow = dp.shape[0] // tm
        in_specs.append(pl.BlockSpec((tm, dp.shape[1]), lambda i, nrow=nrow: (jnp.minimum(i, nrow - 1), 0)))
    for w in wts:
        in_specs.append(pl.BlockSpec(w.shape, lambda i: (0, 0), pipeline_mode=pl.Buffered(1)))
    any_spec = pl.BlockSpec(memory_space=pl.ANY)
    in_specs += [any_spec, any_spec,
                 pl.BlockSpec((tm, D), lat), pl.BlockSpec((tm, D), lambda i: (jnp.maximum(i - n_lat, 0), 0)),
                 pl.BlockSpec((tm, D), lat), pl.BlockSpec((1, 1, D), grp), pl.BlockSpec((1, D), lambda i: (0, 0))]
    return _pallas(
        body, name="dgrad_norm_bwd", grid=(n_tiles,), in_specs=in_specs,
        out_specs=[pl.BlockSpec((tm, D), lat), pl.BlockSpec((1, 1, D), grp), pl.BlockSpec((1, 1, D), grp),
                   pl.BlockSpec((1, D), lambda i: (0, 0)), any_spec, any_spec],
        out_shape=[jax.ShapeDtypeStruct((t, D), F32), jax.ShapeDtypeStruct((n_grp, 1, D), F32),
                   jax.ShapeDtypeStruct((n_grp, 1, D), F32), jax.ShapeDtypeStruct((1, D), F32),
                   jax.ShapeDtypeStruct((3,) + paw.shape[1:], paw.dtype),
                   jax.ShapeDtypeStruct((3,) + pap.shape[1:], pap.dtype)],
        scratch_shapes=[pltpu.VMEM((tm, D), F32), pltpu.SemaphoreType.DMA((3 * N_BULK,)),
                        pltpu.SemaphoreType.DMA((3 * N_BULK,))],
        compiler_params=_params(("arbitrary",)),
    )(*dps, *wts, paw, pap, x2, ctx2, dh, scale1, norm_g)


TM_NORM = 512


def norm_mod_fwd(x2, ctx2, scale1, shift, norm_g):
    t = x2.shape[0]
    n_lat = t // TM_NORM
    assert ctx2.shape[0] == TM_NORM
    n_samples = scale1.shape[0] - 1
    tps = n_lat // n_samples

    def body(x_ref, c_ref, sc_ref, sh_ref, g_ref, u_ref):
        i = pl.program_id(0)
        xv = jnp.where(i < n_lat, x_ref[...], c_ref[...])
        rs = lax.rsqrt(jnp.mean(xv * xv, axis=-1, keepdims=True) + EPS)
        u = xv * rs * g_ref[...] * sc_ref[0] + sh_ref[0]
        u_ref[...] = u.astype(u_ref.dtype)

    grp = lambda i: (jnp.minimum(i // tps, n_samples), 0, 0)
    return _pallas(
        body, name="norm_mod_fwd", grid=(n_lat + 1,),
        in_specs=[pl.BlockSpec((TM_NORM, D), lambda i: (jnp.minimum(i, n_lat - 1), 0)),
                  pl.BlockSpec((TM_NORM, D), lambda i: (0, 0)),
                  pl.BlockSpec((1, 1, D), grp), pl.BlockSpec((1, 1, D), grp),
                  pl.BlockSpec((1, D), lambda i: (0, 0))],
        out_specs=pl.BlockSpec((TM_NORM, D), lambda i: (i, 0)),
        out_shape=jax.ShapeDtypeStruct((t + TM_NORM, D), BF16),
        compiler_params=_params(("parallel",)),
    )(x2, ctx2, scale1, shift, norm_g)


CONV_CB = 256
CONV_NCB = D // CONV_CB
H_OFF = 16


H_CB = 128
H_SPAN = GRID_W + 2 * H_OFF - 8


def _conv_scratch(vertical):
    if vertical:
        return [pltpu.VMEM((GRID_H + 2 * CONV_PAD, GRID_W, CONV_CB), F32)]
    return [pltpu.VMEM((GRID_H, GRID_W + 2 * H_OFF, H_CB), F32), pltpu.VMEM((7, GRID_H, H_SPAN, H_CB), F32)]


def _conv_fill(bufs, img, vertical):
    pad_ref = bufs[0]
    pad_ref[...] = jnp.zeros_like(pad_ref)
    if vertical:
        pad_ref[pl.ds(CONV_PAD, GRID_H)] = img
        return
    pad_ref[:, pl.ds(H_OFF, GRID_W), :] = img

    def shift(r, carry):
        for s in range(1, 8):
            bufs[1][s - 1, r] = pad_ref[r, pl.ds(s, H_SPAN), :]
        return carry

    lax.fori_loop(0, GRID_H, shift, 0)


def _conv_window(bufs, k, vertical, r):
    if vertical:
        return bufs[0][r + k]
    off = H_OFF - CONV_PAD + k
    if off % 8 == 0:
        return bufs[0][r, pl.ds(off, GRID_W), :]
    return bufs[1][off % 8 - 1, r, pl.ds(off - off % 8, GRID_W), :]


def _conv_col_blocks(vertical):
    if vertical:
        return [pl.ds(0, CONV_CB)]
    return [pl.ds(i * H_CB, H_CB) for i in range(CONV_CB // H_CB)]


def _rows(r):
    return pl.ds(pl.multiple_of(r * GRID_W, GRID_W), GRID_W)


def conv_fwd(p1, conv_w, conv_b, n_samples):
    t = n_samples * SEQ

    def make(vertical, prev):
        n_buf = len(_conv_scratch(vertical))

        def body(gv_ref, gg_ref, w_ref, b_ref, *rest):
            o_ref, bufs = rest[-1 - n_buf], rest[-n_buf:]
            for cols in _conv_col_blocks(vertical):
                a = gv_ref[:, cols].astype(F32) * _sigmoid(gg_ref[:, cols].astype(F32))
                _conv_fill(bufs, a.reshape(GRID_H, GRID_W, a.shape[-1]), vertical)

                def row(r, carry, cols=cols):
                    acc = jnp.zeros((GRID_W, cols.size), F32) + b_ref[:, cols]
                    for k in range(CONV_K):
                        acc = acc + _conv_window(bufs, k, vertical, r) * w_ref[pl.ds(k, 1), cols]
                    o_ref[_rows(r), cols] = acc
                    return carry

                lax.fori_loop(0, GRID_H, row, 0)

        cb0 = CONV_NCB // 2 if vertical else 0
        in_specs = [pl.BlockSpec((SEQ, CONV_CB), lambda b, j: (b, 2 * (cb0 + j))),
                    pl.BlockSpec((SEQ, CONV_CB), lambda b, j: (b, 2 * (cb0 + j) + 1)),
                    pl.BlockSpec((CONV_K + 1, CONV_CB), lambda b, j: (0, cb0 + j)),
                    pl.BlockSpec((1, CONV_CB), lambda b, j: (0, cb0 + j))]
        args = [p1, p1, conv_w, conv_b]
        aliases = {}
        if prev is not None:
            in_specs.append(pl.BlockSpec(memory_space=pl.ANY))
            args.append(prev)
            aliases = {4: 0}
        return _pallas(
            body, name="conv_fwd_v" if vertical else "conv_fwd_h", grid=(n_samples, CONV_NCB // 2),
            in_specs=in_specs,
            out_specs=pl.BlockSpec((SEQ, CONV_CB), lambda b, j: (b, cb0 + j)),
            out_shape=jax.ShapeDtypeStruct((t, D), F32),
            scratch_shapes=_conv_scratch(vertical),
            input_output_aliases=aliases,
            compiler_params=_params(("parallel", "parallel")),
        )(*args)

    return make(True, make(False, None))


def conv_bwd(p1, daconv, conv_w, n_samples):
    t = n_samples * SEQ

    def make(vertical, prev):
        n_buf = len(_conv_scratch(vertical))

        def body(gv_ref, gg_ref, dy_ref, w_ref, *rest):
            dp_ref, dw_ref, db_ref = rest[-3 - 2 * n_buf - 1:-2 * n_buf - 1]
            a_bufs, d_bufs, da_ref = rest[-2 * n_buf - 1:-n_buf - 1], rest[-n_buf - 1:-1], rest[-1]
            for cols in _conv_col_blocks(vertical):
                width = cols.size
                gv = gv_ref[:, cols].astype(F32)
                sg = _sigmoid(gg_ref[:, cols].astype(F32))
                _conv_fill(a_bufs, (gv * sg).reshape(GRID_H, GRID_W, width), vertical)
                _conv_fill(d_bufs, dy_ref[:, cols].reshape(GRID_H, GRID_W, width), vertical)

                def row(r, carry, cols=cols, width=width):
                    acc = jnp.zeros((GRID_W, width), F32)
                    for k in range(CONV_K):
                        acc = acc + _conv_window(d_bufs, CONV_K - 1 - k, vertical, r) * w_ref[pl.ds(k, 1), cols]
                    da_ref[_rows(r), cols] = acc
                    return carry

                lax.fori_loop(0, GRID_H, row, 0)
                da = da_ref[:, cols]
                dp_ref[:, pl.ds(cols.start, width)] = (da * sg).astype(dp_ref.dtype)
                dp_ref[:, pl.ds(CONV_CB + cols.start, width)] = (da * gv * sg * (1.0 - sg)).astype(dp_ref.dtype)

                for k in range(CONV_K):
                    def wrow(r, acc, k=k, cols=cols):
                        return acc + _conv_window(a_bufs, k, vertical, r) * dy_ref[_rows(r), cols]
                    acc = lax.fori_loop(0, GRID_H, wrow, jnp.zeros((GRID_W, width), F32))
                    dw_ref[0, pl.ds(k, 1), cols] = jnp.sum(acc, axis=0, keepdims=True)
            dw_ref[0, pl.ds(CONV_K, 1), :] = jnp.zeros((1, CONV_CB), F32)
            db_ref[0] = jnp.sum(dy_ref[...], axis=0, keepdims=True)

        cb0 = CONV_NCB // 2 if vertical else 0
        in_specs = [pl.BlockSpec((SEQ, CONV_CB), lambda b, j: (b, 2 * (cb0 + j))),
                    pl.BlockSpec((SEQ, CONV_CB), lambda b, j: (b, 2 * (cb0 + j) + 1)),
                    pl.BlockSpec((SEQ, CONV_CB), lambda b, j: (b, cb0 + j)),
                    pl.BlockSpec((CONV_K + 1, CONV_CB), lambda b, j: (0, cb0 + j))]
        args = [p1, p1, daconv, conv_w]
        aliases = {}
        if prev is not None:
            in_specs += [pl.BlockSpec(memory_space=pl.ANY)] * 3
            args += list(prev)
            aliases = {4: 0, 5: 1, 6: 2}
        return _pallas(
            body, name="conv_bwd_v" if vertical else "conv_bwd_h", grid=(n_samples, CONV_NCB // 2),
            in_specs=in_specs,
            out_specs=[pl.BlockSpec((SEQ, 2 * CONV_CB), lambda b, j: (b, cb0 + j)),
                       pl.BlockSpec((1, CONV_K + 1, CONV_CB), lambda b, j: (b, 0, cb0 + j)),
                       pl.BlockSpec((1, 1, CONV_CB), lambda b, j: (b, 0, cb0 + j))],
            out_shape=[jax.ShapeDtypeStruct((t, 2 * D), BF16),
                       jax.ShapeDtypeStruct((n_samples, CONV_K + 1, D), F32),
                       jax.ShapeDtypeStruct((n_samples, 1, D), F32)],
            scratch_shapes=_conv_scratch(vertical) + _conv_scratch(vertical) + [pltpu.VMEM((SEQ, CONV_CB), F32)],
            input_output_aliases=aliases,
            compiler_params=_params(("parallel", "parallel")),
        )(*args)

    return make(True, make(False, None))


TM_EW = 256


def ln_gate_fwd(aconv, z, ln_g, ln_b):
    t = aconv.shape[0]

    def body(a_ref, z_ref, g_ref, b_ref, o_ref):
        a = a_ref[...]
        mu = jnp.mean(a, axis=-1, keepdims=True)
        xc = a - mu
        rstd = lax.rsqrt(jnp.mean(xc * xc, axis=-1, keepdims=True) + EPS)
        l = xc * rstd * g_ref[...] + b_ref[...]
        o_ref[...] = (_silu(l) * _silu(z_ref[...].astype(F32))).astype(o_ref.dtype)

    row = pl.BlockSpec((TM_EW, D), lambda i: (i, 0))
    vec = pl.BlockSpec((1, D), lambda i: (0, 0))
    return _pallas(
        body, name="ln_gate_fwd", grid=(t // TM_EW,), in_specs=[row, row, vec, vec], out_specs=row,
        out_shape=jax.ShapeDtypeStruct((t, D), BF16), compiler_params=_params(("parallel",)),
    )(aconv, z, ln_g, ln_b)


def ln_gate_bwd(aconv, z, dac, ln_g, ln_b):
    t = aconv.shape[0]

    def body(a_ref, z_ref, d_ref, g_ref, b_ref, da_ref, dz_ref, dg_ref, db_ref):
        a = a_ref[...]
        zv = z_ref[...].astype(F32)
        dac_v = d_ref[...].astype(F32)
        mu = jnp.mean(a, axis=-1, keepdims=True)
        xc = a - mu
        rstd = lax.rsqrt(jnp.mean(xc * xc, axis=-1, keepdims=True) + EPS)
        xh = xc * rstd
        l = xh * g_ref[...] + b_ref[...]
        dz_ref[...] = (dac_v * _silu(l) * _dsilu(zv)).astype(dz_ref.dtype)
        dl = dac_v * _silu(zv) * _dsilu(l)
        dxh = dl * g_ref[...]
        da_ref[...] = rstd * (dxh - jnp.mean(dxh, axis=-1, keepdims=True)
                              - xh * jnp.mean(dxh * xh, axis=-1, keepdims=True))

        @pl.when(pl.program_id(0) == 0)
        def _():
            dg_ref[...] = jnp.zeros_like(dg_ref)
            db_ref[...] = jnp.zeros_like(db_ref)

        dg_ref[...] += jnp.sum(dl * xh, axis=0, keepdims=True)
        db_ref[...] += jnp.sum(dl, axis=0, keepdims=True)

    row = pl.BlockSpec((TM_EW, D), lambda i: (i, 0))
    vec = pl.BlockSpec((1, D), lambda i: (0, 0))
    return _pallas(
        body, name="ln_gate_bwd", grid=(t // TM_EW,), in_specs=[row, row, row, vec, vec],
        out_specs=[row, row, vec, vec],
        out_shape=[jax.ShapeDtypeStruct((t, D), F32), jax.ShapeDtypeStruct((t, D), BF16),
                   jax.ShapeDtypeStruct((1, D), F32), jax.ShapeDtypeStruct((1, D), F32)],
        compiler_params=_params(("arbitrary",)),
    )(aconv, z, dac, ln_g, ln_b)


TM_PREP = 256
PREP_LAT = SEQ // TM_PREP
PREP_ALL = SEQ_ALL // TM_PREP


def _chunk_tri(n, upper):
    r = lax.broadcasted_iota(jnp.int32, (n, n), 0)
    c = lax.broadcasted_iota(jnp.int32, (n, n), 1)
    same = (r // CHUNK) == (c // CHUNK)
    keep = (c >= r) if upper else (c <= r)
    return jnp.where(same & keep, 1.0, 0.0).astype(F32)


def _split3(v):
    hi = v.astype(BF16)
    r1 = v - hi.astype(F32)
    mid = r1.astype(BF16)
    lo = (r1 - mid.astype(F32)).astype(BF16)
    return jnp.stack([hi, mid, lo])


def _chunk_sums(v, upper):
    tri = _chunk_tri(v.shape[0], upper).astype(BF16)
    pieces = _split3(v)
    return (_nn(tri, pieces[0]) + _nn(tri, pieces[1])) + _nn(tri, pieces[2])


def _gate_logits(ab, up3_ref, bias_ref):
    assert ab.dtype == BF16
    return ((_nn(ab, up3_ref[0]) + _nn(ab, up3_ref[1])) + _nn(ab, up3_ref[2])) + bias_ref[...]


def _prep_tile_maps(n_samples):
    n_lat = n_samples * PREP_LAT

    def seq_map(i):
        return jnp.where(i < n_lat, i // PREP_LAT, i - n_lat), jnp.where(i < n_lat, i % PREP_LAT, PREP_LAT)

    return n_lat, seq_map


def gla_prep_fwd(p3, upf, upb, bias_f, bias_b, n_samples):
    n_lat, seq_map = _prep_tile_maps(n_samples)
    n_tiles = n_lat + n_samples

    def body(v_ref, q_ref, k_ref, ab_ref, upf_ref, upb_ref, bf_ref, bb_ref, qo, ko, vo, cf, cb):
        i = pl.program_id(0)
        qo[0] = jnp.where(i < n_lat, q_ref[...].astype(F32) * Q_SCALE, 0.0)
        ko[0] = k_ref[...].astype(F32)
        vo[0] = v_ref[...].astype(F32)
        ab = ab_ref[...]
        gf = _log_sigmoid(_gate_logits(ab, upf_ref, bf_ref)) * (1.0 / GATE_TAU)
        gb = _log_sigmoid(_gate_logits(ab, upb_ref, bb_ref)) * (1.0 / GATE_TAU)
        cf[0] = _chunk_sums(gf, False)
        cb[0] = _chunk_sums(gb, True)

    def o_spec(w):
        return pl.BlockSpec((1, TM_PREP, w), lambda i: (*seq_map(i), 0))

    full = lambda shape: pl.BlockSpec(shape, lambda i: (0,) * len(shape))
    return _pallas(
        body, name="gla_prep_fwd", grid=(n_tiles,),
        in_specs=[pl.BlockSpec((TM_PREP, 1024), lambda i: (i, O3_V // 1024)),
                  pl.BlockSpec((TM_PREP, 512), lambda i: (i, O3_Q // 512)),
                  pl.BlockSpec((TM_PREP, 512), lambda i: (i, O3_K // 512)),
                  pl.BlockSpec((TM_PREP, 128), lambda i: (i, O3_AB // 128)),
                  full((3, 128, GLA_DK)), full((3, 128, GLA_DK)), full((1, GLA_DK)), full((1, GLA_DK))],
        out_specs=[o_spec(GLA_DK), o_spec(GLA_DK), o_spec(D), o_spec(GLA_DK), o_spec(GLA_DK)],
        out_shape=[jax.ShapeDtypeStruct((n_samples, SEQ_ALL, GLA_DK), F32),
                   jax.ShapeDtypeStruct((n_samples, SEQ_ALL, GLA_DK), F32),
                   jax.ShapeDtypeStruct((n_samples, SEQ_ALL, D), F32),
                   jax.ShapeDtypeStruct((n_samples, SEQ_ALL, GLA_DK), F32),
                   jax.ShapeDtypeStruct((n_samples, SEQ_ALL, GLA_DK), F32)],
        compiler_params=_params(("parallel",)),
    )(p3, p3, p3, p3, upf, upb, bias_f, bias_b)


def gla_prep_bwd(p3, dq_f, dq_b, dk_f, dk_b, dv_f, dv_b, dc_f, dc_b, upf, upb, bias_f, bias_b, n_samples):
    n_lat, seq_map = _prep_tile_maps(n_samples)
    n_tiles = n_lat + n_samples

    def body(ab_ref, dqf, dqb, dkf, dkb, dvf, dvb, dcf, dcb, upf_ref, upb_ref, bf_ref, bb_ref,
             dp_ref, duf_ref, dub_ref, dbf_ref, dbb_ref):
        i = pl.program_id(0)
        dp_ref[:, pl.ds(O3_V, D)] = (dvf[0] + dvb[0]).astype(dp_ref.dtype)
        dq = jnp.where(i < n_lat, (dqf[0] + dqb[0]) * Q_SCALE, 0.0)
        dp_ref[:, pl.ds(O3_Q, GLA_DK)] = dq.astype(dp_ref.dtype)
        dp_ref[:, pl.ds(O3_K, GLA_DK)] = (dkf[0] + dkb[0]).astype(dp_ref.dtype)
        ab = ab_ref[...]
        zf = _gate_logits(ab, upf_ref, bf_ref)
        zb = _gate_logits(ab, upb_ref, bb_ref)
        dgf = _chunk_sums(dcf[0], True)
        dgb = _chunk_sums(dcb[0], False)
        dzf = _b16(dgf * (1.0 / GATE_TAU) * _sigmoid(-zf))
        dzb = _b16(dgb * (1.0 / GATE_TAU) * _sigmoid(-zb))
        dab = _nt(dzf, upf_ref[0]) + _nt(dzb, upb_ref[0])
        dp_ref[:, pl.ds(O3_AB, 128)] = dab.astype(dp_ref.dtype)

        @pl.when(i == 0)
        def _():
            duf_ref[...] = jnp.zeros_like(duf_ref)
            dub_ref[...] = jnp.zeros_like(dub_ref)
            dbf_ref[...] = jnp.zeros_like(dbf_ref)
            dbb_ref[...] = jnp.zeros_like(dbb_ref)

        duf_ref[...] += _tn(ab, dzf)
        dub_ref[...] += _tn(ab, dzb)
        dbf_ref[...] += jnp.sum(dzf.astype(F32), axis=0, keepdims=True)
        dbb_ref[...] += jnp.sum(dzb.astype(F32), axis=0, keepdims=True)

    def s_spec(w):
        return pl.BlockSpec((1, TM_PREP, w), lambda i: (*seq_map(i), 0))

    full = lambda shape: pl.BlockSpec(shape, lambda i: (0,) * len(shape))
    return _pallas(
        body, name="gla_prep_bwd", grid=(n_tiles,),
        in_specs=[pl.BlockSpec((TM_PREP, 128), lambda i: (i, O3_AB // 128)),
                  s_spec(GLA_DK), s_spec(GLA_DK), s_spec(GLA_DK), s_spec(GLA_DK), s_spec(D), s_spec(D),
                  s_spec(GLA_DK), s_spec(GLA_DK),
                  full((3, 128, GLA_DK)), full((3, 128, GLA_DK)), full((1, GLA_DK)), full((1, GLA_DK))],
        out_specs=[pl.BlockSpec((TM_PREP, W3), lambda i: (i, 0)),
                   full((128, GLA_DK)), full((128, GLA_DK)), full((1, GLA_DK)), full((1, GLA_DK))],
        out_shape=[jax.ShapeDtypeStruct((n_tiles * TM_PREP, W3), BF16),
                   jax.ShapeDtypeStruct((128, GLA_DK), F32), jax.ShapeDtypeStruct((128, GLA_DK), F32),
                   jax.ShapeDtypeStruct((1, GLA_DK), F32), jax.ShapeDtypeStruct((1, GLA_DK), F32)],
        compiler_params=_params(("arbitrary",)),
    )(p3, dq_f, dq_b, dk_f, dk_b, dv_f, dv_b, dc_f, dc_b, upf, upb, bias_f, bias_b)


def _sub_blocks(rev):
    if NSUB == 1:
        return [((0, CHUNK), CHUNK // 2, (0, CHUNK))]
    out = []
    for s in range(NSUB):
        rows = (s * SUB, SUB)
        if rev:
            ref = (s + 1) * SUB if s < NSUB - 1 else None
            cols = (s * SUB, CHUNK - s * SUB)
        else:
            ref = s * SUB - 1 if s > 0 else None
            cols = (0, (s + 1) * SUB)
        out.append((rows, ref, cols))
    return out


def _sub_mask(rows, cols, rev):
    r = rows[0] + lax.broadcasted_iota(jnp.int32, (rows[1], cols[1]), 0)
    c = cols[0] + lax.broadcasted_iota(jnp.int32, (rows[1], cols[1]), 1)
    return (c >= r) if rev else (c <= r)


def _sub_operands(qc, kc, cc, rows, ref, cols):
    cref = jnp.zeros((1, HEAD_K), F32) if ref is None else cc[ref:ref + 1]
    eq = jnp.exp(cc[rows[0]:rows[0] + rows[1]] - cref)
    ek = jnp.exp(cref - cc[cols[0]:cols[0] + cols[1]])
    qs = qc[rows[0]:rows[0] + rows[1]] * eq
    kk = kc[cols[0]:cols[0] + cols[1]] * ek
    return qs, kk, eq, ek


SCAN_ROWS = 256
SCAN_CHUNKS = SCAN_ROWS // CHUNK
SCAN_STEPS = SEQ_ALL // SCAN_ROWS
LAT_BLOCKS = SEQ // SCAN_ROWS


def _scan_block(t, rev):
    if rev:
        return SCAN_STEPS - 1 - t
    return jnp.where(t == 0, SCAN_STEPS - 1, t - 1)


def _scan_lat_block(t, rev):
    if rev:
        return jnp.minimum(SCAN_STEPS - 1 - t, LAT_BLOCKS - 1)
    return jnp.maximum(t - 1, 0)


def _head_cols(h):
    return pl.ds(h * HEAD_K, HEAD_K), pl.ds(h * HEAD_V, HEAD_V)


def gla_scan_fwd(q, k, v, cum, *, rev, name):
    n = q.shape[0]

    def body(q_ref, k_ref, v_ref, c_ref, o_ref, s_ref, sfin_ref, st):
        t = pl.program_id(1)

        @pl.when(t == 0)
        def _():
            st[...] = jnp.zeros_like(st)

        def chunk(j, carry):
            lj = SCAN_CHUNKS - 1 - j if rev else j
            r0 = pl.multiple_of(lj * CHUNK, CHUNK)
            rws = pl.ds(r0, CHUNK)
            for h in range(HEADS):
                kcols, vcols = _head_cols(h)
                qc, kc, cc = q_ref[0, rws, kcols], k_ref[0, rws, kcols], c_ref[0, rws, kcols]
                vc = v_ref[0, rws, vcols]
                s_in = st[h]
                s_ref[0, h, j] = s_in
                edge = cc[0:1] if rev else cc[CHUNK - 1:CHUNK]
                ke = kc * jnp.exp(edge - cc)
                st[h] = s_in * jnp.exp(edge) + _tn(_b16(vc), _b16(ke))
                o_inter = _nt(_b16(qc * jnp.exp(cc)), _b16(s_in))
                vb = _b16(vc)
                for rows, ref, cols in _sub_blocks(rev):
                    qs, kk, _, _ = _sub_operands(qc, kc, cc, rows, ref, cols)
                    a = jnp.where(_sub_mask(rows, cols, rev), _nt(_b16(qs), _b16(kk)), 0.0)
                    o_s = _nn(_b16(a), vb[cols[0]:cols[0] + cols[1]])
                    o_ref[0, pl.ds(r0 + rows[0], rows[1]), vcols] = o_inter[rows[0]:rows[0] + rows[1]] + o_s
            return carry

        lax.fori_loop(0, SCAN_CHUNKS, chunk, 0)

        @pl.when(t == SCAN_STEPS - 1)
        def _():
            sfin_ref[0] = st[...]

    def spec(w):
        return pl.BlockSpec((1, SCAN_ROWS, w), lambda b, t: (b, _scan_block(t, rev), 0))

    return _pallas(
        body, name=name, grid=(n, SCAN_STEPS),
        in_specs=[spec(GLA_DK), spec(GLA_DK), spec(D), spec(GLA_DK)],
        out_specs=[pl.BlockSpec((1, SCAN_ROWS, D), lambda b, t: (b, _scan_lat_block(t, rev), 0)),
                   pl.BlockSpec((1, HEADS, SCAN_CHUNKS, HEAD_V, HEAD_K), lambda b, t: (b, 0, t, 0, 0)),
                   pl.BlockSpec((1, HEADS, HEAD_V, HEAD_K), lambda b, t: (b, 0, 0, 0))],
        out_shape=[jax.ShapeDtypeStruct((n, SEQ, D), F32),
                   jax.ShapeDtypeStruct((n, HEADS, NCHUNK, HEAD_V, HEAD_K), F32),
                   jax.ShapeDtypeStruct((n, HEADS, HEAD_V, HEAD_K), F32)],
        scratch_shapes=[pltpu.VMEM((HEADS, HEAD_V, HEAD_K), F32)],
        compiler_params=_params(("parallel", "arbitrary")),
    )(q, k, v, cum)


def gla_scan_bwd(q, k, v, cum, s_all, s_fin, do, *, rev, name):
    n = q.shape[0]

    def body(q_ref, k_ref, v_ref, c_ref, s_ref, sfin_ref, do_ref, dq_ref, dk_ref, dv_ref, dc_ref,
             dst, s_next, dq_acc, dk_acc, dv_acc):
        t = SCAN_STEPS - 1 - pl.program_id(1)

        @pl.when(pl.program_id(1) == 0)
        def _():
            dst[...] = jnp.zeros_like(dst)
            s_next[...] = sfin_ref[0]

        def chunk(jj, carry):
            j = SCAN_CHUNKS - 1 - jj
            lj = SCAN_CHUNKS - 1 - j if rev else j
            rws = pl.ds(pl.multiple_of(lj * CHUNK, CHUNK), CHUNK)
            for h in range(HEADS):
                kcols, vcols = _head_cols(h)
                qc, kc, cc = q_ref[0, rws, kcols], k_ref[0, rws, kcols], c_ref[0, rws, kcols]
                vc = v_ref[0, rws, vcols]
                doc = jnp.where(t > 0, do_ref[0, rws, vcols], 0.0)
                s_in = s_ref[0, h, j]
                s_out = s_next[h]
                ds_out = dst[h]
                edge = cc[0:1] if rev else cc[CHUNK - 1:CHUNK]
                e_q = jnp.exp(cc)
                e_k = jnp.exp(edge - cc)
                dob = _b16(doc)
                dsb = _b16(ds_out)
                dst[h] = ds_out * jnp.exp(edge) + _tn(dob, _b16(qc * e_q))
                s_next[h] = s_in
                dq_acc[h] = e_q * _nn(dob, _b16(s_in))
                dk_acc[h] = e_k * _nn(_b16(vc), dsb)
                dv_acc[h] = _nt(_b16(kc * e_k), dsb)
                vb = _b16(vc)
                for rows, ref, cols in _sub_blocks(rev):
                    qs, kk, eq, ek = _sub_operands(qc, kc, cc, rows, ref, cols)
                    mask = _sub_mask(rows, cols, rev)
                    rsl = slice(rows[0], rows[0] + rows[1])
                    csl = pl.ds(cols[0], cols[1])
                    qsb, kkb = _b16(qs), _b16(kk)
                    a = jnp.where(mask, _nt(qsb, kkb), 0.0)
                    da = _b16(jnp.where(mask, _nt(dob[rsl], vb[cols[0]:cols[0] + cols[1]]), 0.0))
                    dq_acc[h, pl.ds(rows[0], rows[1]), :] += _nn(da, kkb) * eq
                    dk_acc[h, csl, :] += _tn(da, qsb) * ek
                    dv_acc[h, csl, :] += _tn(_b16(a), dob[rsl])
                dq = dq_acc[h]
                dk = dk_acc[h]
                dc = qc * dq - kc * dk
                bnd = jnp.sum(ds_out * s_out, axis=0, keepdims=True)
                edge_row = 0 if rev else CHUNK - 1
                is_edge = lax.broadcasted_iota(jnp.int32, (CHUNK, HEAD_K), 0) == edge_row
                dq_ref[0, rws, kcols] = dq
                dk_ref[0, rws, kcols] = dk
                dv_ref[0, rws, vcols] = dv_acc[h]
                dc_ref[0, rws, kcols] = dc + jnp.where(is_edge, bnd, 0.0)
            return carry

        lax.fori_loop(0, SCAN_CHUNKS, chunk, 0)

    def step_of(u):
        return SCAN_STEPS - 1 - u

    def spec(w):
        return pl.BlockSpec((1, SCAN_ROWS, w), lambda b, u: (b, _scan_block(step_of(u), rev), 0))

    return _pallas(
        body, name=name, grid=(n, SCAN_STEPS),
        in_specs=[spec(GLA_DK), spec(GLA_DK), spec(D), spec(GLA_DK),
                  pl.BlockSpec((1, HEADS, SCAN_CHUNKS, HEAD_V, HEAD_K), lambda b, u: (b, 0, step_of(u), 0, 0)),
                  pl.BlockSpec((1, HEADS, HEAD_V, HEAD_K), lambda b, u: (b, 0, 0, 0)),
                  pl.BlockSpec((1, SCAN_ROWS, D), lambda b, u: (b, _scan_lat_block(step_of(u), rev), 0))],
        out_specs=[spec(GLA_DK), spec(GLA_DK), spec(D), spec(GLA_DK)],
        out_shape=[jax.ShapeDtypeStruct((n, SEQ_ALL, GLA_DK), F32), jax.ShapeDtypeStruct((n, SEQ_ALL, GLA_DK), F32),
                   jax.ShapeDtypeStruct((n, SEQ_ALL, D), F32), jax.ShapeDtypeStruct((n, SEQ_ALL, GLA_DK), F32)],
        scratch_shapes=[pltpu.VMEM((HEADS, HEAD_V, HEAD_K), F32), pltpu.VMEM((HEADS, HEAD_V, HEAD_K), F32),
                        pltpu.VMEM((HEADS, CHUNK, HEAD_K), F32), pltpu.VMEM((HEADS, CHUNK, HEAD_K), F32),
                        pltpu.VMEM((HEADS, CHUNK, HEAD_V), F32)],
        compiler_params=_params(("parallel", "arbitrary")),
    )(q, k, v, cum, s_all, s_fin, do)


def gla_out_fwd(o_f, o_b, r, gnorm):
    n = o_f.shape[0]
    tiles = SEQ // TM_EW

    def body(of_ref, ob_ref, r_ref, g_ref, og_ref):
        for h in range(HEADS):
            cols = pl.ds(h * HEAD_V, HEAD_V)
            o = of_ref[0, :, cols] + ob_ref[0, :, cols]
            rs = lax.rsqrt(jnp.mean(o * o, axis=-1, keepdims=True) + EPS)
            og_ref[:, cols] = (o * rs * g_ref[...] * _silu(r_ref[:, cols].astype(F32))).astype(og_ref.dtype)

    ospec = pl.BlockSpec((1, TM_EW, D), lambda b, j: (b, j, 0))
    row = pl.BlockSpec((TM_EW, D), lambda b, j: (b * tiles + j, 0))
    return _pallas(
        body, name="gla_out_fwd", grid=(n, tiles),
        in_specs=[ospec, ospec, row, pl.BlockSpec((1, HEAD_V), lambda b, j: (0, 0))],
        out_specs=row, out_shape=jax.ShapeDtypeStruct((n * SEQ, D), BF16),
        compiler_params=_params(("parallel", "parallel")),
    )(o_f, o_b, r, gnorm)


def gla_out_bwd(o_f, o_b, r, dog, gnorm):
    n = o_f.shape[0]
    tiles = SEQ // TM_EW

    def body(of_ref, ob_ref, r_ref, d_ref, g_ref, do_ref, dr_ref, dg_ref):
        @pl.when((pl.program_id(0) == 0) & (pl.program_id(1) == 0))
        def _():
            dg_ref[...] = jnp.zeros_like(dg_ref)

        for h in range(HEADS):
            cols = pl.ds(h * HEAD_V, HEAD_V)
            o = of_ref[0, :, cols] + ob_ref[0, :, cols]
            rv = r_ref[:, cols].astype(F32)
            dv = d_ref[:, cols].astype(F32)
            rs = lax.rsqrt(jnp.mean(o * o, axis=-1, keepdims=True) + EPS)
            oh = o * rs
            dr_ref[:, cols] = (dv * oh * g_ref[...] * _dsilu(rv)).astype(dr_ref.dtype)
            dn = dv * _silu(rv)
            dg_ref[...] += jnp.sum(dn * oh, axis=0, keepdims=True)
            doh = dn * g_ref[...]
            do_ref[0, :, cols] = rs * (doh - oh * jnp.mean(doh * oh, axis=-1, keepdims=True))

    ospec = pl.BlockSpec((1, TM_EW, D), lambda b, j: (b, j, 0))
    row = pl.BlockSpec((TM_EW, D), lambda b, j: (b * tiles + j, 0))
    vec = pl.BlockSpec((1, HEAD_V), lambda b, j: (0, 0))
    return _pallas(
        body, name="gla_out_bwd", grid=(n, tiles),
        in_specs=[ospec, ospec, row, row, vec],
        out_specs=[ospec, row, vec],
        out_shape=[jax.ShapeDtypeStruct((n, SEQ, D), F32), jax.ShapeDtypeStruct((n * SEQ, D), BF16),
                   jax.ShapeDtypeStruct((1, HEAD_V), F32)],
        compiler_params=_params(("arbitrary", "arbitrary")),
    )(o_f, o_b, r, dog, gnorm)


def merge_fwd(p5, y_conv, y_gla):
    t = y_conv.shape[0]

    def body(mc_ref, mg_ref, yc_ref, yg_ref, o_ref):
        f = lambda ref: ref[...].astype(F32)
        o_ref[...] = (_sigmoid(f(mc_ref)) * f(yc_ref) + _sigmoid(f(mg_ref)) * f(yg_ref)).astype(o_ref.dtype)

    row = pl.BlockSpec((TM_EW, D), lambda i: (i, 0))
    return _pallas(
        body, name="merge_fwd", grid=(t // TM_EW,),
        in_specs=[row, pl.BlockSpec((TM_EW, D), lambda i: (i, 1)), row, row], out_specs=row,
        out_shape=jax.ShapeDtypeStruct((t, D), BF16), compiler_params=_params(("parallel",)),
    )(p5, p5, y_conv, y_gla)


def merge_bwd(p5, y_conv, y_gla, dmerged):
    t = y_conv.shape[0]

    def body(mc_ref, mg_ref, yc_ref, yg_ref, d_ref, dyc_ref, dyg_ref, dp_ref):
        f = lambda ref: ref[...].astype(F32)
        d = f(d_ref)
        sc = _sigmoid(f(mc_ref))
        sg = _sigmoid(f(mg_ref))
        dyc_ref[...] = (d * sc).astype(dyc_ref.dtype)
        dyg_ref[...] = (d * sg).astype(dyg_ref.dtype)
        dp_ref[:, pl.ds(0, D)] = (d * f(yc_ref) * sc * (1.0 - sc)).astype(dp_ref.dtype)
        dp_ref[:, pl.ds(D, D)] = (d * f(yg_ref) * sg * (1.0 - sg)).astype(dp_ref.dtype)

    row = pl.BlockSpec((TM_EW, D), lambda i: (i, 0))
    return _pallas(
        body, name="merge_bwd", grid=(t // TM_EW,),
        in_specs=[row, pl.BlockSpec((TM_EW, D), lambda i: (i, 1)), row, row, row],
        out_specs=[row, row, pl.BlockSpec((TM_EW, 2 * D), lambda i: (i, 0))],
        out_shape=[jax.ShapeDtypeStruct((t, D), BF16), jax.ShapeDtypeStruct((t, D), BF16),
                   jax.ShapeDtypeStruct((t, 2 * D), BF16)],
        compiler_params=_params(("parallel",)),
    )(p5, p5, y_conv, y_gla, dmerged)


def final_fwd_bwd(x2, mo, gate, final_g, target, n_samples):
    t = x2.shape[0]
    tiles = SEQ // TM_EW

    def body(x_ref, mo_ref, gate_ref, g_ref, t_ref, dh_ref, dmo_ref, dgate_ref, dg_ref, loss_ref):
        b, j = pl.program_id(0), pl.program_id(1)
        mo_v = mo_ref[...]
        h = x_ref[...] + gate_ref[0] * mo_v
        rs = lax.rsqrt(jnp.mean(h * h, axis=-1, keepdims=True) + EPS)
        nh = h * rs
        err = nh * g_ref[...] - t_ref[...]
        dy = err * (1.0 / D)
        dn = dy * g_ref[...]
        dh = rs * (dn - nh * jnp.mean(dn * nh, axis=-1, keepdims=True))
        dh_ref[...] = dh
        dmo_ref[...] = (dh * gate_ref[0]).astype(dmo_ref.dtype)

        @pl.when(j == 0)
        def _():
            dgate_ref[...] = jnp.zeros_like(dgate_ref)

        @pl.when((b == 0) & (j == 0))
        def _():
            dg_ref[...] = jnp.zeros_like(dg_ref)
            loss_ref[...] = jnp.zeros_like(loss_ref)

        dgate_ref[0] += jnp.sum(dh * mo_v, axis=0, keepdims=True)
        dg_ref[...] += jnp.sum(dy * nh, axis=0, keepdims=True)
        loss_ref[...] += (0.5 / D) * jnp.sum(err * err)

    row = pl.BlockSpec((TM_EW, D), lambda b, j: (b * tiles + j, 0))
    per = pl.BlockSpec((1, 1, D), lambda b, j: (b, 0, 0))
    vec = pl.BlockSpec((1, D), lambda b, j: (0, 0))
    return _pallas(
        body, name="final_fwd_bwd", grid=(n_samples, tiles),
        in_specs=[row, row, per, vec, row],
        out_specs=[row, row, per, vec, pl.BlockSpec((8, 128), lambda b, j: (0, 0))],
        out_shape=[jax.ShapeDtypeStruct((t, D), F32), jax.ShapeDtypeStruct((t, D), BF16),
                   jax.ShapeDtypeStruct((n_samples, 1, D), F32), jax.ShapeDtypeStruct((1, D), F32),
                   jax.ShapeDtypeStruct((8, 128), F32)],
        compiler_params=_params(("arbitrary", "arbitrary")),
    )(x2, mo, gate, final_g, target)


def local_step(x, ctx, target, mod, wts, small):
    n = x.shape[0]
    t = n * SEQ
    t_all = t + n * NCTX
    x2 = x.reshape(t, D)
    ctx2 = ctx.reshape(n * NCTX, D)
    tgt2 = target.reshape(t, D)
    scale1, shift, gate = mod

    u = norm_mod_fwd(x2, ctx2, scale1, shift, small["norm_g"])
    p1 = matmul_nn(u, wts["w1"], small["b1"], name="proj_conv", m=t, tm=1024, tn=1024, out_dtype=BF16)
    p2 = matmul_nn(u, wts["w2"], small["b2"], name="proj_z", m=t, tm=1024, tn=1024, out_dtype=BF16)
    p3 = matmul_nn(u, wts["w3"], small["b3"], name="proj_gla", m=t_all, tm=512, tn=W3, out_dtype=BF16)
    p4 = matmul_nn(u, wts["w4"], small["b4"], name="proj_r", m=t, tm=1024, tn=1024, out_dtype=BF16)
    p5 = matmul_nn(u, wts["w5"], small["b5"], name="proj_merge", m=t, tm=1024, tn=1024, out_dtype=BF16)

    aconv = conv_fwd(p1, small["conv_w"], small["conv_b"], n)
    ac = ln_gate_fwd(aconv, p2, small["conv_ln_g"], small["conv_ln_b"])
    y_conv = matmul_nn(ac, wts["conv_proj"], None, name="conv_proj_fwd", m=t, tm=1024, tn=1024, out_dtype=BF16)

    qs, ks, vs, cum_f, cum_b = gla_prep_fwd(p3, small["upf"], small["upb"], small["bias_f"], small["bias_b"], n)
    o_f, s_f, sfin_f = gla_scan_fwd(qs, ks, vs, cum_f, rev=False, name="gla_scan_fwd_f")
    o_b, s_b, sfin_b = gla_scan_fwd(qs, ks, vs, cum_b, rev=True, name="gla_scan_fwd_b")
    og = gla_out_fwd(o_f, o_b, p4, small["gla_norm_g"])
    y_gla = matmul_nn(og, wts["gla_proj"], None, name="gla_proj_fwd", m=t, tm=1024, tn=1024, out_dtype=BF16)

    merged = merge_fwd(p5, y_conv, y_gla)
    mo = matmul_nn(merged, wts["w_out"], None, name="w_out_fwd", m=t, tm=1024, tn=1024, out_dtype=F32)
    dh, dmo, dgate, d_final_g, loss = final_fwd_bwd(x2, mo, gate, small["final_norm_g"], tgt2, n)

    g = {"final_norm_g": d_final_g}
    dmerged = matmul_nt(dmo, wts["w_out"], name="w_out_dgrad", tm=512, out_dtype=BF16)
    g["w_out"] = matmul_tn(merged, dmo, name="w_out_wgrad", t=t, tn=1024, tt=1024)[0]
    dyc, dyg, dp5 = merge_bwd(p5, y_conv, y_gla, dmerged)

    dac = matmul_nt(dyc, wts["conv_proj"], name="conv_proj_dgrad", tm=512, out_dtype=BF16)
    g["conv_proj"] = matmul_tn(ac, dyc, name="conv_proj_wgrad", t=t, tn=1024, tt=1024)[0]
    daconv, dp2, g["conv_ln_g"], g["conv_ln_b"] = ln_gate_bwd(aconv, p2, dac, small["conv_ln_g"], small["conv_ln_b"])
    dp1, dconv_w, dconv_b = conv_bwd(p1, daconv, small["conv_w"], n)
    g["conv_w"], g["conv_b"] = dconv_w, dconv_b

    dog = matmul_nt(dyg, wts["gla_proj"], name="gla_proj_dgrad", tm=512, out_dtype=BF16)
    g["gla_proj"] = matmul_tn(og, dyg, name="gla_proj_wgrad", t=t, tn=1024, tt=1024)[0]
    do, dp4, g["gla_norm_g"] = gla_out_bwd(o_f, o_b, p4, dog, small["gla_norm_g"])
    dq_f, dk_f, dv_f, dc_f = gla_scan_bwd(qs, ks, vs, cum_f, s_f, sfin_f, do, rev=False, name="gla_scan_bwd_f")
    dq_b, dk_b, dv_b, dc_b = gla_scan_bwd(qs, ks, vs, cum_b, s_b, sfin_b, do, rev=True, name="gla_scan_bwd_b")
    dp3, g["upf"], g["upb"], g["bias_f"], g["bias_b"] = gla_prep_bwd(
        p3, dq_f, dq_b, dk_f, dk_b, dv_f, dv_b, dc_f, dc_b,
        small["upf"], small["upb"], small["bias_f"], small["bias_b"], n)

    dps = [dp1, dp2, dp3, dp4, dp5]
    for i, dp in enumerate(dps):
        rows = dp.shape[0]
        tn = W3 if dp.shape[1] == W3 else 1024
        g["w%d" % (i + 1)], g["b%d" % (i + 1)] = matmul_tn(
            u, dp, name="w_in_wgrad_%d" % (i + 1), t=rows, tn=tn, tt=1024 if rows % 1024 == 0 else 768, colsum=True)
    g["gate"] = dgate
    return loss, dh, dps, g


def _group_cols(w):
    gv, gg, z = w[..., 0:1024], w[..., 1024:2048], w[..., 2048:3072]
    q, k, v = w[..., 3072:3584], w[..., 3584:4096], w[..., 4096:5120]
    ab = w[..., 5120:5152]
    r, mc, mg = w[..., 5152:6176], w[..., 6176:7200], w[..., 7200:8224]
    g1 = jnp.concatenate([p for j in range(CONV_NCB)
                          for p in (gv[..., CONV_CB * j:CONV_CB * (j + 1)], gg[..., CONV_CB * j:CONV_CB * (j + 1)])], -1)
    pad = jnp.zeros(w.shape[:-1] + (W3 - 2080,), w.dtype)
    g3 = jnp.concatenate([v, q, k, ab, pad], -1)
    return g1, z, g3, r, jnp.concatenate([mc, mg], -1)


def _ungroup_cols(g1, g2, g3, g4, g5):
    gv = jnp.concatenate([g1[..., 2 * CONV_CB * j:2 * CONV_CB * j + CONV_CB] for j in range(CONV_NCB)], -1)
    gg = jnp.concatenate([g1[..., 2 * CONV_CB * j + CONV_CB:2 * CONV_CB * (j + 1)] for j in range(CONV_NCB)], -1)
    v, q, k, ab = g3[..., 0:1024], g3[..., 1024:1536], g3[..., 1536:2048], g3[..., 2048:2080]
    return jnp.concatenate([gv, gg, g2, q, k, v, ab, g4, g5[..., 0:1024], g5[..., 1024:2048]], -1)


def _natural_pieces():
    pieces = [(CONV_CB * j, CONV_CB, 0, 2 * CONV_CB * j) for j in range(CONV_NCB)]
    pieces += [(1024 + CONV_CB * j, CONV_CB, 0, 2 * CONV_CB * j + CONV_CB) for j in range(CONV_NCB)]
    pieces += [(2048, 1024, 1, 0), (3072, 512, 2, O3_Q), (3584, 512, 2, O3_K), (4096, 1024, 2, O3_V), (5120, 32, 2, O3_AB),
               (5152, 1024, 3, 0), (6176, 1024, 4, 0), (7200, 1024, 4, 1024)]
    return sorted(pieces)


def _ungroup_to_shards(groups):
    shards = []
    for i in range(N_CHIPS):
        lo, hi = i * W_IN_SHARD, (i + 1) * W_IN_SHARD
        parts = []
        for nat, width, g, gcol in _natural_pieces():
            a, b = max(nat, lo), min(nat + width, hi)
            if a < b:
                parts.append(groups[g][:, gcol + a - nat:gcol + b - nat])
        shards.append(jnp.concatenate(parts, 1))
    return jnp.stack(shards)


def _pad_up(up, row0):
    return jnp.zeros((128, GLA_DK), F32).at[row0:row0 + up.shape[0]].set(up)


def _adamw_math(w, g, m, v):
    m = ADAM_B1 * m + (1.0 - ADAM_B1) * g
    v = ADAM_B2 * v + (1.0 - ADAM_B2) * (g * g)
    m_hat = m / (1.0 - ADAM_B1 ** ADAM_STEP)
    v_hat = v / (1.0 - ADAM_B2 ** ADAM_STEP)
    delta = -ADAM_LR * (m_hat / (jnp.sqrt(v_hat) + ADAM_EPS) + ADAM_WD * w)
    return delta, m, v


def adamw2d(w, g, m, v, *, name, tr, tcols=None):
    rows, cols = w.shape[-2:]

    def body(w_ref, g_ref, m_ref, v_ref, d_ref, nm_ref, nv_ref):
        d_ref[...], nm_ref[...], nv_ref[...] = _adamw_math(w_ref[...], g_ref[...], m_ref[...], v_ref[...])

    tcols = cols if tcols is None else tcols
    if w.ndim == 3:
        spec = pl.BlockSpec((1, tr, tcols), lambda i, j: (0, i, j))
    else:
        spec = pl.BlockSpec((tr, tcols), lambda i, j: (i, j))
    return _pallas(
        body, name=name, grid=(rows // tr, cols // tcols), in_specs=[spec] * 4, out_specs=[spec] * 3,
        out_shape=[jax.ShapeDtypeStruct(w.shape, F32)] * 3, compiler_params=_params(("parallel", "parallel")),
    )(w, g, m, v)


def sum_devices(sall, *, name):
    rows = sall.shape[1]

    def body(s_ref, o_ref):
        acc = s_ref[0]
        for d in range(1, N_DEV):
            acc = acc + s_ref[d]
        o_ref[...] = acc

    return _pallas(body, name=name, out_shape=jax.ShapeDtypeStruct((rows, D), F32),
                   compiler_params=_params())(sall)


def pair_add(core, g, got, *, name, tr):
    n, rows, cols = got.shape
    g4 = g.reshape(n, 2, rows, cols)

    def body(core_ref, g_ref, got_ref, ob_ref):
        del core_ref
        ob_ref[0] = (g_ref[0, 0] + got_ref[0]).astype(BF16)

    spec = pl.BlockSpec((1, tr, cols), lambda i, t, core_ref: (i, t, 0))
    return _pallas(
        body, name=name,
        grid_spec=pltpu.PrefetchScalarGridSpec(
            num_scalar_prefetch=1, grid=(n, rows // tr),
            in_specs=[pl.BlockSpec((1, 1, tr, cols), lambda i, t, core_ref: (i, core_ref[0], t, 0)), spec],
            out_specs=spec),
        out_shape=jax.ShapeDtypeStruct(got.shape, BF16),
        compiler_params=_params(("parallel", "parallel")))(core, g4, got)


def chip_add(place, pa, rb, *, name, tr):
    _, rows, cols = pa.shape

    def body(place_ref, m_ref, r_ref, o_ref):
        del place_ref
        o_ref[0] = ((m_ref[0].astype(F32) + r_ref[0].astype(F32)) + r_ref[1].astype(F32)) + r_ref[2].astype(F32)

    return _pallas(
        body, name=name,
        grid_spec=pltpu.PrefetchScalarGridSpec(
            num_scalar_prefetch=1, grid=(rows // tr,),
            in_specs=[pl.BlockSpec((1, tr, cols), lambda t, place_ref: (place_ref[0], t, 0)),
                      pl.BlockSpec((3, tr, cols), lambda t, place_ref: (0, t, 0))],
            out_specs=pl.BlockSpec((1, tr, cols), lambda t, place_ref: (place_ref[1], t, 0))),
        out_shape=jax.ShapeDtypeStruct((2, rows, cols), F32),
        compiler_params=_params(("parallel",)))(place, pa, rb)


def ada_bwd(call, cctx_rows, dm_shard, dm_full, adaw):
    nsh = adaw.shape[1]

    def body(c_ref, cc_ref, dms_ref, dmf_ref, w_ref, gw_ref, gb_ref, pq_ref):
        a_lat = _silu(c_ref[...])
        a_ctx = _silu(cc_ref[...])
        dms = dms_ref[...]
        gw_ref[...] = _tn(a_lat, dms[0:64], HI) + _tn(a_ctx, dms[64:72], HI)
        gb_ref[...] = jnp.sum(dmf_ref[...], axis=0, keepdims=True)
        part = _nt(dms[64:72], w_ref[...], HI)
        pq_ref[...] = jnp.zeros_like(pq_ref) + jnp.sum(part, axis=0, keepdims=True)

    return _pallas(body, name="ada_bwd",
                   out_shape=[jax.ShapeDtypeStruct((D, nsh), F32), jax.ShapeDtypeStruct((1, 3 * D), F32),
                              jax.ShapeDtypeStruct((8, D), F32)],
                   compiler_params=_params())(call, cctx_rows, dm_shard, dm_full, adaw)


def cctx_grad(pq_all, cctx_rows):
    def body(p_ref, c_ref, o_ref):
        acc = p_ref[0]
        for qi in range(1, N_CHIPS):
            acc = acc + p_ref[qi]
        o_ref[...] = acc * _dsilu(c_ref[...])

    return _pallas(body, name="cctx_grad", out_shape=jax.ShapeDtypeStruct((8, D), F32),
                   compiler_params=_params())(pq_all, cctx_rows)


def _place():
    x, y, c = lax.axis_index("x"), lax.axis_index("y"), lax.axis_index("c")
    chips = [(1 - x, y), (x, 1 - y), (1 - x, 1 - y)]
    return x, y, c, chips


def _all_peers(x, y, c):
    return [((1 - x) if r & 4 else x, (1 - y) if r & 2 else y, (1 - c) if r & 1 else c) for r in range(1, N_DEV)]


def _remote(src, dst, send_sem, recv_sem, dev):
    return pltpu.make_async_remote_copy(src_ref=src, dst_ref=dst, send_sem=send_sem, recv_sem=recv_sem,
                                        device_id=dev, device_id_type=MESH)


ANY = pl.BlockSpec(memory_space=pl.ANY)
VMEM = pl.BlockSpec(memory_space=pltpu.VMEM)
F_ROWS = 16


W_ROW_CHUNKS = 4
P_ROW_CHUNKS = 2
N_BULK = W_ROW_CHUNKS + P_ROW_CHUNKS


def _half_chunks(core, n_rows, align):
    out = []
    for a, k in ((0, W_ROW_CHUNKS), (1, P_ROW_CHUNKS)):
        half = n_rows[a] // 2
        size = half // k
        for i in range(k):
            start = core * half + i * size
            out.append((a, pl.ds(start if isinstance(start, int) else pl.multiple_of(start, align), size)))
    return out


def gather_weights(c8, cctx8, adaw, adab, w_sh, p_sh, fp):
    nsh = adaw.shape[1]

    def body(c_ref, cctx_ref, adaw_ref, adab_ref, w_ref, p_ref, fp_ref, wall_ref, pall_ref, fall_ref, call_ref, mall_ref,
             abuf, w_send, w_recv, h_send, h_recv, c_send, c_recv, m_send, m_recv, f_send, f_recv):
        x, y, c, chips = _place()
        q = 2 * x + y
        dev = 4 * x + 2 * y + c
        qs = [2 * cx + cy for cx, cy in chips]
        sib = (x, y, 1 - c)
        srcs, dsts = (w_ref, p_ref), (wall_ref, pall_ref)
        n_rows = (w_ref.shape[0], p_ref.shape[0])
        mine = _half_chunks(c, n_rows, 16)
        other = _half_chunks(1 - c, n_rows, 16)

        bulk = [[_remote(srcs[a].at[rows], dsts[a].at[q, rows], w_send.at[j * N_BULK + i], w_recv.at[j * N_BULK + i],
                         (*chips[j], c)) for i, (a, rows) in enumerate(mine)] for j in range(3)]
        fall_ref[q] = fp_ref[...]
        small = [_remote(fp_ref, fall_ref.at[q], f_send.at[j], f_recv.at[j], (*chips[j], c)) for j in range(3)]
        my_rows = pl.ds(pl.multiple_of(8 * dev, 8), 8)
        call_ref[my_rows, :] = c_ref[...]
        cond = [_remote(c_ref, call_ref.at[my_rows, :], c_send.at[r], c_recv.at[r], peer)
                for r, peer in enumerate(_all_peers(x, y, c))]
        for cp in sum(bulk, []) + small + cond:
            cp.start()
        for cp in cond:
            cp.wait_recv()

        abuf[pl.ds(0, 64), :] = _silu(call_ref[...])
        abuf[pl.ds(64, 8), :] = _silu(cctx_ref[...])
        mall_ref[q] = _nn(abuf[...], adaw_ref[...], HI) + adab_ref[...]
        mod = [_remote(mall_ref.at[q], mall_ref.at[q], m_send.at[j], m_recv.at[j], (*chips[j], c)) for j in range(3)]
        for cp in mod:
            cp.start()

        handed = []
        for j in range(3):
            for i, (a, rows) in enumerate(mine):
                bulk[j][i].wait_recv()
                cp = _remote(dsts[a].at[qs[j], rows], dsts[a].at[qs[j], rows],
                             h_send.at[j * N_BULK + i], h_recv.at[j * N_BULK + i], sib)
                cp.start()
                handed.append(cp)
        for j in range(3):
            for i, (a, rows) in enumerate(other):
                _remote(dsts[a].at[qs[j], rows], dsts[a].at[qs[j], rows],
                        h_send.at[j * N_BULK + i], h_recv.at[j * N_BULK + i], sib).wait_recv()
        for cp in mod + small:
            cp.wait_recv()
        for cp in sum(bulk, []) + small + cond + mod + handed:
            cp.wait_send()

    def dma(n):
        return pltpu.SemaphoreType.DMA((n,))

    return _pallas(
        body, name="gather_weights",
        in_specs=[VMEM, VMEM, VMEM, VMEM, ANY, ANY, VMEM],
        out_specs=[ANY, ANY, VMEM, VMEM, VMEM],
        out_shape=[jax.ShapeDtypeStruct((N_CHIPS,) + w_sh.shape, BF16), jax.ShapeDtypeStruct((N_CHIPS,) + p_sh.shape, BF16),
                   jax.ShapeDtypeStruct((N_CHIPS, F_ROWS, D), F32),
                   jax.ShapeDtypeStruct((8 * N_DEV, D), F32), jax.ShapeDtypeStruct((N_CHIPS, MOD_ROWS, nsh), F32)],
        scratch_shapes=[pltpu.VMEM((MOD_ROWS, D), F32), dma(3 * N_BULK), dma(3 * N_BULK), dma(3 * N_BULK), dma(3 * N_BULK),
                        dma(7), dma(7), dma(3), dma(3), dma(3), dma(3)],
        compiler_params=_params(),
    )(c8, cctx8, adaw, adab, w_sh, p_sh, fp)


def pair_swap(gs, gp, sm):
    n_gs = len(gs)
    n_pair = n_gs * W_ROW_CHUNKS + N_CHIPS * P_ROW_CHUNKS

    def body(*refs):
        g_refs, gp_ref, sm_ref = refs[:n_gs], refs[n_gs], refs[n_gs + 1]
        got_refs, gotp_ref, sall_ref = refs[n_gs + 2:2 * n_gs + 2], refs[2 * n_gs + 2], refs[2 * n_gs + 3]
        a_send, a_recv, s_send, s_recv = refs[2 * n_gs + 4:]
        x, y, c, _ = _place()
        dev = 4 * x + 2 * y + c
        sib = (x, y, 1 - c)
        pair = []
        half, size = D // 2, D // 2 // W_ROW_CHUNKS
        for gi in range(n_gs):
            for i in range(W_ROW_CHUNKS):
                k = len(pair)
                rows_o = pl.ds(pl.multiple_of((1 - c) * half + i * size, 8), size)
                pair.append(_remote(g_refs[gi].at[rows_o], got_refs[gi].at[pl.ds(i * size, size)],
                                    a_send.at[k], a_recv.at[k], sib))
        half, size = gp_ref.shape[1] // 2, gp_ref.shape[1] // 2 // P_ROW_CHUNKS
        for s in range(N_CHIPS):
            for i in range(P_ROW_CHUNKS):
                k = len(pair)
                rows_o = pl.ds(pl.multiple_of((1 - c) * half + i * size, 8), size)
                pair.append(_remote(gp_ref.at[s, rows_o], gotp_ref.at[s, pl.ds(i * size, size)],
                                    a_send.at[k], a_recv.at[k], sib))
        sall_ref[dev] = sm_ref[...]
        small = [_remote(sm_ref, sall_ref.at[dev], s_send.at[r], s_recv.at[r], peer)
                 for r, peer in enumerate(_all_peers(x, y, c))]
        for cp in pair + small:
            cp.start()
        for cp in small + pair:
            cp.wait_recv()
        for cp in small + pair:
            cp.wait_send()

    return _pallas(
        body, name="pair_swap", in_specs=[ANY] * (n_gs + 1) + [VMEM], out_specs=[ANY] * (n_gs + 1) + [VMEM],
        out_shape=[jax.ShapeDtypeStruct((D // 2, a.shape[1]), F32) for a in gs]
        + [jax.ShapeDtypeStruct((N_CHIPS, gp.shape[1] // 2, gp.shape[2]), F32),
           jax.ShapeDtypeStruct((N_DEV,) + sm.shape, F32)],
        scratch_shapes=[pltpu.SemaphoreType.DMA((n_pair,)), pltpu.SemaphoreType.DMA((n_pair,)),
                        pltpu.SemaphoreType.DMA((N_DEV - 1,)), pltpu.SemaphoreType.DMA((N_DEV - 1,))],
        compiler_params=_params(),
    )(*gs, gp, sm)


def gather_small(sm):
    rows = sm.shape[0]

    def body(sm_ref, sall_ref, s_send, s_recv):
        x, y, c, _ = _place()
        dev = 4 * x + 2 * y + c
        sall_ref[dev] = sm_ref[...]
        small = [_remote(sm_ref, sall_ref.at[dev], s_send.at[r], s_recv.at[r], peer)
                 for r, peer in enumerate(_all_peers(x, y, c))]
        for cp in small:
            cp.start()
        for cp in small:
            cp.wait_recv()
        for cp in small:
            cp.wait_send()

    return _pallas(
        body, name="gather_small", in_specs=[VMEM], out_specs=VMEM,
        out_shape=jax.ShapeDtypeStruct((N_DEV, rows, D), F32),
        scratch_shapes=[pltpu.SemaphoreType.DMA((7,)), pltpu.SemaphoreType.DMA((7,))],
        compiler_params=_params(),
    )(sm)


def pair_share(ghw, ghp, pq):
    def body(ghw_ref, ghp_ref, pq_ref, outw_ref, outp_ref, pqa_ref, send, recv, p_send, p_recv):
        del ghw_ref, ghp_ref
        x, y, c, chips = _place()
        q = 2 * x + y
        refs = (outw_ref, outp_ref)
        n_rows = (2 * outw_ref.shape[1], 2 * outp_ref.shape[1])
        pair = [_remote(refs[a].at[c, rows], refs[a].at[c, rows], send.at[i], recv.at[i], (x, y, 1 - c))
                for i, (a, rows) in enumerate(_half_chunks(0, n_rows, 8))]
        pqa_ref[q] = pq_ref[...]
        small = [_remote(pq_ref, pqa_ref.at[q], p_send.at[j], p_recv.at[j], (*chips[j], c)) for j in range(3)]
        for cp in pair + small:
            cp.start()
        for i, (a, rows) in enumerate(_half_chunks(0, n_rows, 8)):
            _remote(refs[a].at[1 - c, rows], refs[a].at[1 - c, rows], send.at[i], recv.at[i], (x, y, 1 - c)).wait_recv()
        for cp in small:
            cp.wait_recv()
        for cp in pair + small:
            cp.wait_send()

    return _pallas(
        body, name="pair_share", in_specs=[ANY, ANY, VMEM], out_specs=[ANY, ANY, VMEM],
        out_shape=[jax.ShapeDtypeStruct(ghw.shape, F32), jax.ShapeDtypeStruct(ghp.shape, F32),
                   jax.ShapeDtypeStruct((N_CHIPS, 8, D), F32)],
        scratch_shapes=[pltpu.SemaphoreType.DMA((N_BULK,)), pltpu.SemaphoreType.DMA((N_BULK,)),
                        pltpu.SemaphoreType.DMA((3,)), pltpu.SemaphoreType.DMA((3,))],
        input_output_aliases={0: 0, 1: 1},
        compiler_params=_params(),
    )(ghw, ghp, pq)


def _rows_of(shape):
    size = 1
    for s in shape:
        size *= s
    return -(-size // D)


def _pack(arrs, rows_multiple=8):
    parts = []
    total = 0
    for a in arrs:
        f = a.reshape(-1).astype(F32)
        r = _rows_of(a.shape)
        parts.append(jnp.pad(f, (0, r * D - f.shape[0])))
        total += r
    pad_rows = (-total) % rows_multiple
    if pad_rows:
        parts.append(jnp.zeros((pad_rows * D,), F32))
    return jnp.concatenate(parts).reshape(-1, D)


def _unpack(p, shapes):
    out = []
    r0 = 0
    for shp in shapes:
        r = _rows_of(shp)
        size = 1
        for s in shp:
            size *= s
        out.append(p[r0:r0 + r].reshape(-1)[:size].reshape(shp))
        r0 += r
    return out


WEIGHT_NAMES = ['c_ctx', 'ada_w', 'ada_b', 'norm_g', 'w_in', 'b_in', 'conv_w', 'conv_b', 'conv_ln_g', 'conv_ln_b',
                'conv_proj', 'decay_up_fwd', 'decay_bias_fwd', 'decay_up_bwd', 'decay_bias_bwd', 'gla_norm_g', 'gla_proj',
                'w_out', 'final_norm_g']
SMALL_NAMES = ['c_ctx', 'ada_b', 'norm_g', 'b_in', 'conv_w', 'conv_b', 'conv_ln_g', 'conv_ln_b', 'decay_up_fwd',
               'decay_bias_fwd', 'decay_up_bwd', 'decay_bias_bwd', 'gla_norm_g', 'final_norm_g']


def kernel(x, c, ctx, c_ctx, ada_w, ada_b, norm_g, w_in, b_in, conv_w, conv_b, conv_ln_g, conv_ln_b, conv_proj, decay_up_fwd, decay_bias_fwd, decay_up_bwd, decay_bias_bwd, gla_norm_g, gla_proj, w_out, final_norm_g, loss_target, m_c_ctx, m_ada_w, m_ada_b, m_norm_g, m_w_in, m_b_in, m_conv_w, m_conv_b, m_conv_ln_g, m_conv_ln_b, m_conv_proj, m_decay_up_fwd, m_decay_bias_fwd, m_decay_up_bwd, m_decay_bias_bwd, m_gla_norm_g, m_gla_proj, m_w_out, m_final_norm_g, v_c_ctx, v_ada_w, v_ada_b, v_norm_g, v_w_in, v_b_in, v_conv_w, v_conv_b, v_conv_ln_g, v_conv_ln_b, v_conv_proj, v_decay_up_fwd, v_decay_bias_fwd, v_decay_up_bwd, v_decay_bias_bwd, v_gla_norm_g, v_gla_proj, v_w_out, v_final_norm_g):
    w = dict(c_ctx=c_ctx, ada_w=ada_w, ada_b=ada_b, norm_g=norm_g, w_in=w_in, b_in=b_in, conv_w=conv_w, conv_b=conv_b,
             conv_ln_g=conv_ln_g, conv_ln_b=conv_ln_b, conv_proj=conv_proj, decay_up_fwd=decay_up_fwd,
             decay_bias_fwd=decay_bias_fwd, decay_up_bwd=decay_up_bwd, decay_bias_bwd=decay_bias_bwd,
             gla_norm_g=gla_norm_g, gla_proj=gla_proj, w_out=w_out, final_norm_g=final_norm_g)
    m = dict(c_ctx=m_c_ctx, ada_w=m_ada_w, ada_b=m_ada_b, norm_g=m_norm_g, w_in=m_w_in, b_in=m_b_in, conv_w=m_conv_w,
             conv_b=m_conv_b, conv_ln_g=m_conv_ln_g, conv_ln_b=m_conv_ln_b, conv_proj=m_conv_proj,
             decay_up_fwd=m_decay_up_fwd, decay_bias_fwd=m_decay_bias_fwd, decay_up_bwd=m_decay_up_bwd,
             decay_bias_bwd=m_decay_bias_bwd, gla_norm_g=m_gla_norm_g, gla_proj=m_gla_proj, w_out=m_w_out,
             final_norm_g=m_final_norm_g)
    v = dict(c_ctx=v_c_ctx, ada_w=v_ada_w, ada_b=v_ada_b, norm_g=v_norm_g, w_in=v_w_in, b_in=v_b_in, conv_w=v_conv_w,
             conv_b=v_conv_b, conv_ln_g=v_conv_ln_g, conv_ln_b=v_conv_ln_b, conv_proj=v_conv_proj,
             decay_up_fwd=v_decay_up_fwd, decay_bias_fwd=v_decay_bias_fwd, decay_up_bwd=v_decay_up_bwd,
             decay_bias_bwd=v_decay_bias_bwd, gla_norm_g=v_gla_norm_g, gla_proj=v_gla_proj, w_out=v_w_out,
             final_norm_g=v_final_norm_g)
    n = x.shape[0]
    ax, ay, ac = lax.axis_index("x"), lax.axis_index("y"), lax.axis_index("c")
    q = 2 * ax + ay
    dev = 4 * ax + 2 * ay + ac
    nsh = ada_w.shape[2]

    w_sh = w_in[0].astype(BF16)
    p_sh = jnp.concatenate([conv_proj[0], gla_proj[0], w_out[0]], 0).astype(BF16)
    fp = _pack([conv_w[0], decay_up_fwd[0], decay_up_bwd[0]], F_ROWS)
    c8 = jnp.pad(c, ((0, 8 - n), (0, 0)))
    cctx8 = jnp.pad(c_ctx[None], ((0, 7), (0, 0)))
    adab_sh = lax.dynamic_slice(ada_b, (0, q * nsh), (1, nsh))
    w_all, p_all, fall, call, mall = gather_weights(c8, cctx8, ada_w[0], adab_sh, w_sh, p_sh, fp)

    mod_all = jnp.transpose(mall, (1, 0, 2)).reshape(MOD_ROWS, 3 * D)
    mod_mine = lax.dynamic_slice(mod_all, (8 * dev, 0), (n, 3 * D))
    mod_ctx = mod_all[64:65]
    shift = jnp.concatenate([mod_mine[:, 0:D], mod_ctx[:, 0:D]], 0)[:, None, :]
    scale1 = 1.0 + jnp.concatenate([mod_mine[:, D:2 * D], mod_ctx[:, D:2 * D]], 0)[:, None, :]
    gate = mod_mine[:, 2 * D:3 * D][:, None, :]

    own = lambda i, mine, got: jnp.where(q == i, mine, got)
    g1, g2, g3, g4, g5 = _group_cols(jnp.concatenate([own(i, w_sh, w_all[i]) for i in range(N_CHIPS)], 1))
    p_full = jnp.stack([own(i, p_sh, p_all[i]) for i in range(N_CHIPS)])
    wts = dict(w1=g1, w2=g2, w3=g3, w4=g4, w5=g5,
               conv_proj=p_full[:, 0:256].reshape(D, D), gla_proj=p_full[:, 256:512].reshape(D, D),
               w_out=p_full[:, 512:768].reshape(D, D))
    f_parts = [_unpack(fall[i], [conv_w.shape[1:], decay_up_fwd.shape[1:], decay_up_bwd.shape[1:]]) for i in range(N_CHIPS)]
    conv_w_full = jnp.concatenate([p[0] for p in f_parts], 1)
    upf_full = jnp.concatenate([p[1] for p in f_parts], 1)
    upb_full = jnp.concatenate([p[2] for p in f_parts], 1)
    b1, b2, b3, b4, b5 = _group_cols(b_in)
    small = dict(b1=b1, b2=b2, b3=b3, b4=b4, b5=b5, norm_g=norm_g,
                 conv_w=jnp.pad(conv_w_full, ((0, 1), (0, 0))), conv_b=conv_b, conv_ln_g=conv_ln_g, conv_ln_b=conv_ln_b,
                 upf=_split3(_pad_up(upf_full, 0)), upb=_split3(_pad_up(upb_full, 16)),
                 bias_f=decay_bias_fwd, bias_b=decay_bias_bwd,
                 gla_norm_g=gla_norm_g, final_norm_g=final_norm_g[None])

    loss_part, dh, dps, g = local_step(x, ctx, loss_target, (scale1, shift, gate), wts, small)
    loss = lax.psum(loss_part[0, 0], ("x", "y", "c"))

    gs = [g["w%d" % i] for i in range(1, 6)]
    gp = jnp.concatenate([g["conv_proj"].reshape(N_CHIPS, 256, D), g["gla_proj"].reshape(N_CHIPS, 256, D),
                          g["w_out"].reshape(N_CHIPS, 256, D)], 1)
    d_b_in = _ungroup_cols(*[g["b%d" % i] for i in range(1, 6)])
    early = [d_b_in, g["conv_b"].sum(0), g["conv_ln_g"], g["conv_ln_b"], g["bias_f"], g["bias_b"],
             g["gla_norm_g"], g["final_norm_g"], g["conv_w"].sum(0)[:CONV_K], g["upf"][0:16], g["upb"][16:32]]
    early_shapes = [a.shape for a in early]
    *gots, gotp, sall1 = pair_swap(gs, gp, _pack(early))
    core = ac.astype(jnp.int32).reshape(1)
    chip = q.astype(jnp.int32).reshape(1)
    halves = [pair_add(core, a[None], got[None], name="pair_add_w%d" % (i + 1), tr=128)[0]
              for i, (a, got) in enumerate(zip(gs, gots))]
    paw16 = _ungroup_to_shards(halves)
    pap16 = pair_add(core, gp, gotp, name="pair_add_p", tr=384)
    grad_x2, dshift, dscale, g["norm_g"], rbw, rbp = dgrad_norm_bwd(
        dps, [wts["w%d" % i] for i in range(1, 6)], paw16, pap16, x.reshape(n * SEQ, D), ctx.reshape(n * NCTX, D), dh,
        scale1, norm_g, tm=256)

    dm_mine = jnp.concatenate([dshift[:n, 0], dscale[:n, 0], g["gate"][:, 0]], -1)
    dm_ctx = jnp.concatenate([dshift[n, 0], dscale[n, 0], jnp.zeros((D,), F32)], -1)
    late = [g["norm_g"], dm_mine, dm_ctx]
    late_shapes = [a.shape for a in late]
    sall2 = gather_small(_pack(late))
    (s_b_in, s_conv_b, s_ln_g, s_ln_b, s_bias_f, s_bias_b, s_gla_g, s_final_g, s_conv_w, s_upf,
     s_upb) = _unpack(sum_devices(sall1, name="sum_devices_early"), early_shapes)
    s_norm_g = _unpack(sum_devices(sall2, name="sum_devices_late"), late_shapes)[0]
    dm_rows = [_unpack(sall2[i], late_shapes)[-2:] for i in range(N_DEV)]
    dm_full = jnp.concatenate(
        [jnp.pad(jnp.stack([r[0] for r in dm_rows]), ((0, 0), (0, 8 - n), (0, 0))).reshape(8 * N_DEV, 3 * D),
         jnp.stack([r[1] for r in dm_rows])], 0)
    dm_shard = lax.dynamic_slice(dm_full, (0, q * nsh), (MOD_ROWS, nsh))
    cctx_rows = jnp.broadcast_to(c_ctx[None], (8, D))
    g_ada_w, g_ada_b, pq = ada_bwd(call, cctx_rows, dm_shard, dm_full, ada_w[0])

    place = jnp.concatenate([chip, core])
    ghw = chip_add(place, paw16, rbw, name="chip_add_w", tr=128)
    ghp = chip_add(place, pap16, rbp, name="chip_add_p", tr=384)
    gw_mine, gp_mine, pq_all = pair_share(ghw, ghp, pq)
    gp_mine = gp_mine.reshape(768, D)
    g_c_ctx = cctx_grad(pq_all, cctx_rows)[0]

    grads = dict(
        c_ctx=g_c_ctx, ada_w=g_ada_w[None], ada_b=g_ada_b, norm_g=s_norm_g,
        w_in=gw_mine.reshape(1, D, W_IN_SHARD), b_in=s_b_in,
        conv_w=lax.dynamic_slice(s_conv_w, (0, q * 256), (CONV_K, 256))[None], conv_b=s_conv_b,
        conv_ln_g=s_ln_g, conv_ln_b=s_ln_b, conv_proj=gp_mine[0:256][None],
        decay_up_fwd=lax.dynamic_slice(s_upf, (0, q * 128), (16, 128))[None], decay_bias_fwd=s_bias_f,
        decay_up_bwd=lax.dynamic_slice(s_upb, (0, q * 128), (16, 128))[None], decay_bias_bwd=s_bias_b,
        gla_norm_g=s_gla_g, gla_proj=gp_mine[256:512][None], w_out=gp_mine[512:768][None],
        final_norm_g=s_final_g[0])

    delta, new_m, new_v = {}, {}, {}
    for name in ["ada_w", "conv_proj", "gla_proj", "w_out"]:
        delta[name], new_m[name], new_v[name] = adamw2d(w[name], grads[name].reshape(w[name].shape), m[name], v[name],
                                                        name="adamw_" + name, tr=128)
    tr_ = lambda a: jnp.swapaxes(a, 1, 2)
    g_w_in_t = tr_(grads["w_in"])
    grads["w_in"] = tr_(g_w_in_t)
    d_, m_, v_ = adamw2d(tr_(w_in), g_w_in_t, tr_(m_w_in), tr_(v_w_in), name="adamw_w_in", tr=W_IN_SHARD, tcols=128)
    delta["w_in"], new_m["w_in"], new_v["w_in"] = tr_(d_), tr_(m_), tr_(v_)
    shapes = [w[nm].shape for nm in SMALL_NAMES]
    packs = [_pack([src[nm] for nm in SMALL_NAMES]) for src in (w, grads, m, v)]
    d_, m_, v_ = adamw2d(*packs, name="adamw_small", tr=packs[0].shape[0])
    for nm, a, b, cc in zip(SMALL_NAMES, _unpack(d_, shapes), _unpack(m_, shapes), _unpack(v_, shapes)):
        delta[nm], new_m[nm], new_v[nm] = a, b, cc

    grad_x = grad_x2.reshape(x.shape)
    return (loss, grad_x, *[grads[nm].reshape(w[nm].shape) for nm in WEIGHT_NAMES], *[delta[nm] for nm in WEIGHT_NAMES],
            *[new_m[nm] for nm in WEIGHT_NAMES], *[new_v[nm] for nm in WEIGHT_NAMES])
```

```python
import jax
import jax.numpy as jnp
from jax import lax
from jax.experimental import pallas as pl
from jax.experimental.pallas import tpu as pltpu

F32 = jnp.float32
BF16 = jnp.bfloat16
MESH = pl.DeviceIdType.MESH
HI = lax.Precision.HIGHEST

D = 1024
SEQ = 2048
GRID_W = 64
GRID_H = SEQ // GRID_W
NCTX = 256
SEQ_ALL = SEQ + NCTX
EPS = 1e-6
CONV_K = 31
CONV_PAD = CONV_K // 2
HEADS = 4
HEAD_K = 128
HEAD_V = 256
GLA_DK = HEADS * HEAD_K
GATE_TAU = 16.0
Q_SCALE = HEAD_K ** -0.5
CHUNK = 64
NCHUNK = SEQ_ALL // CHUNK
NCHUNK_LAT = SEQ // CHUNK
NCHUNK_CTX = NCHUNK - NCHUNK_LAT
SUB = 64
NSUB = CHUNK // SUB
N_IN = 8224
W3 = 2176
O3_V, O3_Q, O3_K, O3_AB = 0, 1024, 1536, 2048

ADAM_LR, ADAM_B1, ADAM_B2, ADAM_EPS, ADAM_WD, ADAM_STEP = 0.001, 0.9, 0.999, 1e-08, 0.01, 10
VMEM_LIMIT = 56 * 1024 * 1024

N_CHIPS = 4
N_DEV = 8
W_IN_SHARD = N_IN // N_CHIPS
MOD_ROWS = 72


def _pallas(body, **kw):
    return pl.pallas_call(body, **kw)


def _params(sem=None, **kw):
    if sem is not None:
        kw["dimension_semantics"] = sem
    return pltpu.CompilerParams(vmem_limit_bytes=VMEM_LIMIT, **kw)


def _sigmoid(v):
    return 1.0 / (1.0 + jnp.exp(-v))


def _silu(v):
    return v * _sigmoid(v)


def _dsilu(v):
    s = _sigmoid(v)
    return s * (1.0 + v * (1.0 - s))


def _log_sigmoid(v):
    return jnp.minimum(v, 0.0) - jnp.log(1.0 + jnp.exp(-jnp.abs(v)))


def _dot(a, b, dims, precision=None):
    return lax.dot_general(a, b, (dims, ((), ())), preferred_element_type=F32, precision=precision)


def _nn(a, b, precision=None):
    return _dot(a, b, ((1,), (0,)), precision)


def _nt(a, b, precision=None):
    return _dot(a, b, ((1,), (1,)), precision)


def _tn(a, b, precision=None):
    return _dot(a, b, ((0,), (0,)), precision)


def _b16(v):
    return v.astype(BF16)


def matmul_nn(a, b, bias, *, name, m, tm, tn, out_dtype):
    k = a.shape[1]
    n = b.shape[1]
    has_bias = bias is not None

    def body(*refs):
        if has_bias:
            a_ref, b_ref, bias_ref, o_ref = refs
            acc = _nn(a_ref[...], b_ref[...]) + bias_ref[...]
        else:
            a_ref, b_ref, o_ref = refs
            acc = _nn(a_ref[...], b_ref[...])
        o_ref[...] = acc.astype(o_ref.dtype)

    in_specs = [pl.BlockSpec((tm, k), lambda j, i: (i, 0)), pl.BlockSpec((k, tn), lambda j, i: (0, j))]
    args = [a, b]
    if has_bias:
        in_specs.append(pl.BlockSpec((1, tn), lambda j, i: (0, j)))
        args.append(bias)
    return _pallas(
        body, name=name, grid=(n // tn, m // tm), in_specs=in_specs,
        out_specs=pl.BlockSpec((tm, tn), lambda j, i: (i, j)),
        out_shape=jax.ShapeDtypeStruct((m, n), out_dtype),
        compiler_params=_params(("parallel", "parallel")),
    )(*args)


def matmul_tn(a, b, *, name, t, tn, tt, colsum=False):
    m = a.shape[1]
    n = b.shape[1]

    def body(a_ref, b_ref, o_ref, *rest):
        @pl.when(pl.program_id(1) == 0)
        def _():
            o_ref[...] = jnp.zeros_like(o_ref)
            if colsum:
                rest[0][...] = jnp.zeros_like(rest[0])
        o_ref[...] += _tn(a_ref[...], b_ref[...])
        if colsum:
            rest[0][...] += jnp.sum(b_ref[...].astype(F32), axis=0, keepdims=True)

    out_specs = [pl.BlockSpec((m, tn), lambda j, s: (0, j))]
    out_shape = [jax.ShapeDtypeStruct((m, n), F32)]
    if colsum:
        out_specs.append(pl.BlockSpec((1, tn), lambda j, s: (0, j)))
        out_shape.append(jax.ShapeDtypeStruct((1, n), F32))
    return _pallas(
        body, name=name, grid=(n // tn, t // tt),
        in_specs=[pl.BlockSpec((tt, m), lambda j, s: (s, 0)), pl.BlockSpec((tt, tn), lambda j, s: (s, j))],
        out_specs=out_specs, out_shape=out_shape,
        compiler_params=_params(("parallel", "arbitrary")),
    )(a, b)


def dgrad_norm_bwd(dps, wts, paw, pap, x2, ctx2, dh, scale1, norm_g, *, tm):
    t, tc = x2.shape[0], ctx2.shape[0]
    t_all = t + tc
    n_lat, n_ctx = t // tm, tc // tm
    n_tiles = n_lat + n_ctx
    n_samples = scale1.shape[0] - 1
    tps = n_lat // n_samples
    n_grp = n_samples + 1
    n_g = len(dps)
    whole = [g for g in range(n_g) if dps[g].shape[0] == t_all]
    latent = [g for g in range(n_g) if dps[g].shape[0] != t_all]

    def body(*refs):
        dp_refs, w_refs = refs[:n_g], refs[n_g:2 * n_g]
        (paw_ref, pap_ref, x_ref, c_ref, dh_ref, sc_ref, g_ref, dx_ref, dsh_ref, dsc_ref, dg_ref, rbw_ref, rbp_ref,
         du_buf, b_send, b_recv) = refs[2 * n_g:]
        i = pl.program_id(0)

        def exchange():
            x, y, c, chips = _place()
            srcs, dsts = (paw_ref, pap_ref), (rbw_ref, rbp_ref)
            n_rows = (2 * paw_ref.shape[1], 2 * pap_ref.shape[1])
            return [_remote(srcs[a].at[2 * cx + cy, rows], dsts[a].at[j, rows],
                            b_send.at[j * N_BULK + k], b_recv.at[j * N_BULK + k], (cx, cy, c))
                    for j, (cx, cy) in enumerate(chips) for k, (a, rows) in enumerate(_half_chunks(0, n_rows, 16))]

        @pl.when(i == 0)
        def _():
            for cp in exchange():
                cp.start()

        acc = None
        for g in whole:
            part = _nt(dp_refs[g][...], w_refs[g][...])
            acc = part if acc is None else acc + part
        du_buf[...] = acc

        @pl.when(i < n_lat)
        def _():
            lat = None
            for g in latent:
                part = _nt(dp_refs[g][...], w_refs[g][...])
                lat = part if lat is None else lat + part
            du_buf[...] += lat

        duv = du_buf[...]
        xv = jnp.where(i < n_lat, x_ref[...], c_ref[...])
        rs = lax.rsqrt(jnp.mean(xv * xv, axis=-1, keepdims=True) + EPS)
        xh = xv * rs
        n = xh * g_ref[...]
        dn = duv * sc_ref[0]
        dxh = dn * g_ref[...]
        dx = rs * (dxh - xh * jnp.mean(dxh * xh, axis=-1, keepdims=True))

        @pl.when(i < n_lat)
        def _():
            dx_ref[...] = dx + dh_ref[...]

        @pl.when((i % tps == 0) & (i <= n_lat))
        def _():
            dsh_ref[...] = jnp.zeros_like(dsh_ref)
            dsc_ref[...] = jnp.zeros_like(dsc_ref)

        @pl.when(i == 0)
        def _():
            dg_ref[...] = jnp.zeros_like(dg_ref)

        dsh_ref[0] += jnp.sum(duv, axis=0, keepdims=True)
        dsc_ref[0] += jnp.sum(duv * n, axis=0, keepdims=True)
        dg_ref[...] += jnp.sum(dn * xh, axis=0, keepdims=True)

        @pl.when(i == n_tiles - 1)
        def _():
            for cp in exchange():
                cp.wait_recv()
            for cp in exchange():
                cp.wait_send()

    lat = lambda i: (jnp.minimum(i, n_lat - 1), 0)
    grp = lambda i: (jnp.minimum(i // tps, n_samples), 0, 0)
    in_specs = []
    for g, dp in enumerate(dps):
        nrow = dp.shape[0] // tm
        in_specs.append(pl.BlockSpec((tm, dp.shape[1]), lambda i, nrow=nrow: (jnp.minimum(i, nrow - 1), 0)))
    for w in wts:
        in_specs.append(pl.BlockSpec(w.shape, lambda i: (0, 0), pipeline_mode=pl.Buffered(1)))
    any_spec = pl.BlockSpec(memory_space=pl.ANY)
    in_specs += [any_spec, any_spec,
                 pl.BlockSpec((tm, D), lat), pl.BlockSpec((tm, D), lambda i: (jnp.maximum(i - n_lat, 0), 0)),
                 pl.BlockSpec((tm, D), lat), pl.BlockSpec((1, 1, D), grp), pl.BlockSpec((1, D), lambda i: (0, 0))]
    return _pallas(
        body, name="dgrad_norm_bwd", grid=(n_tiles,), in_specs=in_specs,
        out_specs=[pl.BlockSpec((tm, D), lat), pl.BlockSpec((1, 1, D), grp), pl.BlockSpec((1, 1, D), grp),
                   pl.BlockSpec((1, D), lambda i: (0, 0)), any_spec, any_spec],
        out_shape=[jax.ShapeDtypeStruct((t, D), F32), jax.ShapeDtypeStruct((n_grp, 1, D), F32),
                   jax.ShapeDtypeStruct((n_grp, 1, D), F32), jax.ShapeDtypeStruct((1, D), F32),
                   jax.ShapeDtypeStruct((3,) + paw.shape[1:], paw.dtype),
                   jax.ShapeDtypeStruct((3,) + pap.shape[1:], pap.dtype)],
        scratch_shapes=[pltpu.VMEM((tm, D), F32), pltpu.SemaphoreType.DMA((3 * N_BULK,)),
                        pltpu.SemaphoreType.DMA((3 * N_BULK,))],
        compiler_params=_params(("arbitrary",)),
    )(*dps, *wts, paw, pap, x2, ctx2, dh, scale1, norm_g)


TM_NORM = 512


def norm_mod_fwd(x2, ctx2, scale1, shift, norm_g):
    t = x2.shape[0]
    n_lat = t // TM_NORM
    assert ctx2.shape[0] == TM_NORM
    n_samples = scale1.shape[0] - 1
    tps = n_lat // n_samples

    def body(x_ref, c_ref, sc_ref, sh_ref, g_ref, u_ref):
        i = pl.program_id(0)
        xv = jnp.where(i < n_lat, x_ref[...], c_ref[...])
        rs = lax.rsqrt(jnp.mean(xv * xv, axis=-1, keepdims=True) + EPS)
        u = xv * rs * g_ref[...] * sc_ref[0] + sh_ref[0]
        u_ref[...] = u.astype(u_ref.dtype)

    grp = lambda i: (jnp.minimum(i // tps, n_samples), 0, 0)
    return _pallas(
        body, name="norm_mod_fwd", grid=(n_lat + 1,),
        in_specs=[pl.BlockSpec((TM_NORM, D), lambda i: (jnp.minimum(i, n_lat - 1), 0)),
                  pl.BlockSpec((TM_NORM, D), lambda i: (0, 0)),
                  pl.BlockSpec((1, 1, D), grp), pl.BlockSpec((1, 1, D), grp),
                  pl.BlockSpec((1, D), lambda i: (0, 0))],
        out_specs=pl.BlockSpec((TM_NORM, D), lambda i: (i, 0)),
        out_shape=jax.ShapeDtypeStruct((t + TM_NORM, D), BF16),
        compiler_params=_params(("parallel",)),
    )(x2, ctx2, scale1, shift, norm_g)


CONV_CB = 256
CONV_NCB = D // CONV_CB
H_OFF = 16


H_CB = 128
H_SPAN = GRID_W + 2 * H_OFF - 8


def _conv_scratch(vertical):
    if vertical:
        return [pltpu.VMEM((GRID_H + 2 * CONV_PAD, GRID_W, CONV_CB), F32)]
    return [pltpu.VMEM((GRID_H, GRID_W + 2 * H_OFF, H_CB), F32), pltpu.VMEM((7, GRID_H, H_SPAN, H_CB), F32)]


def _conv_fill(bufs, img, vertical):
    pad_ref = bufs[0]
    pad_ref[...] = jnp.zeros_like(pad_ref)
    if vertical:
        pad_ref[pl.ds(CONV_PAD, GRID_H)] = img
        return
    pad_ref[:, pl.ds(H_OFF, GRID_W), :] = img

    def shift(r, carry):
        for s in range(1, 8):
            bufs[1][s - 1, r] = pad_ref[r, pl.ds(s, H_SPAN), :]
        return carry

    lax.fori_loop(0, GRID_H, shift, 0)


def _conv_window(bufs, k, vertical, r, w0=0, nw=GRID_W, lanes=slice(None)):
    if vertical:
        return bufs[0][r + k, pl.ds(w0, nw), lanes]
    off = H_OFF - CONV_PAD + k
    if off % 8 == 0:
        return bufs[0][r, pl.ds(off + w0, nw), lanes]
    return bufs[1][off % 8 - 1, r, pl.ds(off - off % 8 + w0, nw), lanes]


def _conv_col_blocks(vertical):
    if vertical:
        return [pl.ds(0, CONV_CB)]
    return [pl.ds(i * H_CB, H_CB) for i in range(CONV_CB // H_CB)]


def _rows(r):
    return pl.ds(pl.multiple_of(r * GRID_W, GRID_W), GRID_W)


def conv_fwd(p1, conv_w, conv_b, n_samples):
    t = n_samples * SEQ

    def make(vertical, prev):
        n_buf = len(_conv_scratch(vertical))

        def body(gv_ref, gg_ref, w_ref, b_ref, *rest):
            o_ref, bufs = rest[-1 - n_buf], rest[-n_buf:]
            for cols in _conv_col_blocks(vertical):
                a = gv_ref[:, cols].astype(F32) * _sigmoid(gg_ref[:, cols].astype(F32))
                _conv_fill(bufs, a.reshape(GRID_H, GRID_W, a.shape[-1]), vertical)

                def row(r, carry, cols=cols):
                    acc = jnp.zeros((GRID_W, cols.size), F32) + b_ref[:, cols]
                    for k in range(CONV_K):
                        acc = acc + _conv_window(bufs, k, vertical, r) * w_ref[pl.ds(k, 1), cols]
                    o_ref[_rows(r), cols] = acc
                    return carry

                lax.fori_loop(0, GRID_H, row, 0)

        cb0 = CONV_NCB // 2 if vertical else 0
        in_specs = [pl.BlockSpec((SEQ, CONV_CB), lambda b, j: (b, 2 * (cb0 + j))),
                    pl.BlockSpec((SEQ, CONV_CB), lambda b, j: (b, 2 * (cb0 + j) + 1)),
                    pl.BlockSpec((CONV_K + 1, CONV_CB), lambda b, j: (0, cb0 + j)),
                    pl.BlockSpec((1, CONV_CB), lambda b, j: (0, cb0 + j))]
        args = [p1, p1, conv_w, conv_b]
        aliases = {}
        if prev is not None:
            in_specs.append(pl.BlockSpec(memory_space=pl.ANY))
            args.append(prev)
            aliases = {4: 0}
        return _pallas(
            body, name="conv_fwd_v" if vertical else "conv_fwd_h", grid=(n_samples, CONV_NCB // 2),
            in_specs=in_specs,
            out_specs=pl.BlockSpec((SEQ, CONV_CB), lambda b, j: (b, cb0 + j)),
            out_shape=jax.ShapeDtypeStruct((t, D), F32),
            scratch_shapes=_conv_scratch(vertical),
            input_output_aliases=aliases,
            compiler_params=_params(("parallel", "parallel")),
        )(*args)

    return make(True, make(False, None))


def conv_bwd(p1, daconv, conv_w, n_samples):
    t = n_samples * SEQ

    def make(vertical, prev):
        n_buf = len(_conv_scratch(vertical))

        def body(gv_ref, gg_ref, dy_ref, w_ref, *rest):
            dp_ref, dw_ref, db_ref = rest[-3 - 2 * n_buf - 1:-2 * n_buf - 1]
            a_bufs, d_bufs, da_ref = rest[-2 * n_buf - 1:-n_buf - 1], rest[-n_buf - 1:-1], rest[-1]
            for cols in _conv_col_blocks(vertical):
                width = cols.size
                gv = gv_ref[:, cols].astype(F32)
                sg = _sigmoid(gg_ref[:, cols].astype(F32))
                _conv_fill(a_bufs, (gv * sg).reshape(GRID_H, GRID_W, width), vertical)
                _conv_fill(d_bufs, dy_ref[:, cols].reshape(GRID_H, GRID_W, width), vertical)

                def row(r, carry, cols=cols, width=width):
                    acc = jnp.zeros((GRID_W, width), F32)
                    for k in range(CONV_K):
                        acc = acc + _conv_window(d_bufs, CONV_K - 1 - k, vertical, r) * w_ref[pl.ds(k, 1), cols]
                    da_ref[_rows(r), cols] = acc
                    return carry

                lax.fori_loop(0, GRID_H, row, 0)
                da = da_ref[:, cols]
                dp_ref[:, pl.ds(cols.start, width)] = (da * sg).astype(dp_ref.dtype)
                dp_ref[:, pl.ds(CONV_CB + cols.start, width)] = (da * gv * sg * (1.0 - sg)).astype(dp_ref.dtype)

                for lb in range(width // 128):
                    lanes = pl.ds(lb * 128, 128)
                    dy_lanes = pl.ds(cols.start + lb * 128, 128)

                    def wrow(r, accs, lanes=lanes, dy_lanes=dy_lanes):
                        for w0 in range(0, GRID_W, 8):
                            dyv = dy_ref[pl.ds(pl.multiple_of(r * GRID_W, GRID_W) + w0, 8), dy_lanes]
                            accs = tuple(accs[k] + _conv_window(a_bufs, k, vertical, r, w0, 8, lanes) * dyv
                                         for k in range(CONV_K))
                        return accs

                    accs = lax.fori_loop(0, GRID_H, wrow, tuple(jnp.zeros((8, 128), F32) for _ in range(CONV_K)))
                    for k in range(CONV_K):
                        dw_ref[0, pl.ds(k, 1), dy_lanes] = jnp.sum(accs[k], axis=0, keepdims=True)
            dw_ref[0, pl.ds(CONV_K, 1), :] = jnp.zeros((1, CONV_CB), F32)
            db_ref[0] = jnp.sum(dy_ref[...], axis=0, keepdims=True)

        cb0 = CONV_NCB // 2 if vertical else 0
        in_specs = [pl.BlockSpec((SEQ, CONV_CB), lambda b, j: (b, 2 * (cb0 + j))),
                    pl.BlockSpec((SEQ, CONV_CB), lambda b, j: (b, 2 * (cb0 + j) + 1)),
                    pl.BlockSpec((SEQ, CONV_CB), lambda b, j: (b, cb0 + j)),
                    pl.BlockSpec((CONV_K + 1, CONV_CB), lambda b, j: (0, cb0 + j))]
        args = [p1, p1, daconv, conv_w]
        aliases = {}
        if prev is not None:
            in_specs += [pl.BlockSpec(memory_space=pl.ANY)] * 3
            args += list(prev)
            aliases = {4: 0, 5: 1, 6: 2}
        return _pallas(
            body, name="conv_bwd_v" if vertical else "conv_bwd_h", grid=(n_samples, CONV_NCB // 2),
            in_specs=in_specs,
            out_specs=[pl.BlockSpec((SEQ, 2 * CONV_CB), lambda b, j: (b, cb0 + j)),
                       pl.BlockSpec((1, CONV_K + 1, CONV_CB), lambda b, j: (b, 0, cb0 + j)),
                       pl.BlockSpec((1, 1, CONV_CB), lambda b, j: (b, 0, cb0 + j))],
            out_shape=[jax.ShapeDtypeStruct((t, 2 * D), BF16),
                       jax.ShapeDtypeStruct((n_samples, CONV_K + 1, D), F32),
                       jax.ShapeDtypeStruct((n_samples, 1, D), F32)],
            scratch_shapes=_conv_scratch(vertical) + _conv_scratch(vertical) + [pltpu.VMEM((SEQ, CONV_CB), F32)],
            input_output_aliases=aliases,
            compiler_params=_params(("parallel", "parallel")),
        )(*args)

    return make(True, make(False, None))


TM_EW = 256


def ln_gate_fwd(aconv, z, ln_g, ln_b):
    t = aconv.shape[0]

    def body(a_ref, z_ref, g_ref, b_ref, o_ref):
        a = a_ref[...]
        mu = jnp.mean(a, axis=-1, keepdims=True)
        xc = a - mu
        rstd = lax.rsqrt(jnp.mean(xc * xc, axis=-1, keepdims=True) + EPS)
        l = xc * rstd * g_ref[...] + b_ref[...]
        o_ref[...] = (_silu(l) * _silu(z_ref[...].astype(F32))).astype(o_ref.dtype)

    row = pl.BlockSpec((TM_EW, D), lambda i: (i, 0))
    vec = pl.BlockSpec((1, D), lambda i: (0, 0))
    return _pallas(
        body, name="ln_gate_fwd", grid=(t // TM_EW,), in_specs=[row, row, vec, vec], out_specs=row,
        out_shape=jax.ShapeDtypeStruct((t, D), BF16), compiler_params=_params(("parallel",)),
    )(aconv, z, ln_g, ln_b)


def ln_gate_bwd(aconv, z, dyc, conv_proj, ln_g, ln_b):
    t = aconv.shape[0]

    def body(a_ref, z_ref, d_ref, w_ref, g_ref, b_ref, da_ref, dz_ref, dg_ref, db_ref):
        a = a_ref[...]
        zv = z_ref[...].astype(F32)
        dac_v = _nt(d_ref[...], w_ref[...])
        mu = jnp.mean(a, axis=-1, keepdims=True)
        xc = a - mu
        rstd = lax.rsqrt(jnp.mean(xc * xc, axis=-1, keepdims=True) + EPS)
        xh = xc * rstd
        l = xh * g_ref[...] + b_ref[...]
        dz_ref[...] = (dac_v * _silu(l) * _dsilu(zv)).astype(dz_ref.dtype)
        dl = dac_v * _silu(zv) * _dsilu(l)
        dxh = dl * g_ref[...]
        da_ref[...] = rstd * (dxh - jnp.mean(dxh, axis=-1, keepdims=True)
                              - xh * jnp.mean(dxh * xh, axis=-1, keepdims=True))

        @pl.when(pl.program_id(0) == 0)
        def _():
            dg_ref[...] = jnp.zeros_like(dg_ref)
            db_ref[...] = jnp.zeros_like(db_ref)

        dg_ref[...] += jnp.sum(dl * xh, axis=0, keepdims=True)
        db_ref[...] += jnp.sum(dl, axis=0, keepdims=True)

    row = pl.BlockSpec((TM_EW, D), lambda i: (i, 0))
    vec = pl.BlockSpec((1, D), lambda i: (0, 0))
    return _pallas(
        body, name="ln_gate_bwd", grid=(t // TM_EW,),
        in_specs=[row, row, row, pl.BlockSpec((D, D), lambda i: (0, 0)), vec, vec],
        out_specs=[row, row, vec, vec],
        out_shape=[jax.ShapeDtypeStruct((t, D), F32), jax.ShapeDtypeStruct((t, D), BF16),
                   jax.ShapeDtypeStruct((1, D), F32), jax.ShapeDtypeStruct((1, D), F32)],
        compiler_params=_params(("arbitrary",)),
    )(aconv, z, dyc, conv_proj, ln_g, ln_b)


TM_PREP = 256
PREP_LAT = SEQ // TM_PREP
PREP_ALL = SEQ_ALL // TM_PREP


def _chunk_tri(n, upper):
    r = lax.broadcasted_iota(jnp.int32, (n, n), 0)
    c = lax.broadcasted_iota(jnp.int32, (n, n), 1)
    same = (r // CHUNK) == (c // CHUNK)
    keep = (c >= r) if upper else (c <= r)
    return jnp.where(same & keep, 1.0, 0.0).astype(F32)


def _split3(v):
    hi = v.astype(BF16)
    r1 = v - hi.astype(F32)
    mid = r1.astype(BF16)
    lo = (r1 - mid.astype(F32)).astype(BF16)
    return jnp.stack([hi, mid, lo])


def _chunk_sums(v, upper):
    tri = _chunk_tri(v.shape[0], upper).astype(BF16)
    pieces = _split3(v)
    return (_nn(tri, pieces[0]) + _nn(tri, pieces[1])) + _nn(tri, pieces[2])


def _gate_logits(ab, up3_ref, bias_ref):
    assert ab.dtype == BF16
    return ((_nn(ab, up3_ref[0]) + _nn(ab, up3_ref[1])) + _nn(ab, up3_ref[2])) + bias_ref[...]


def _prep_tile_maps(n_samples):
    n_lat = n_samples * PREP_LAT

    def seq_map(i):
        return jnp.where(i < n_lat, i // PREP_LAT, i - n_lat), jnp.where(i < n_lat, i % PREP_LAT, PREP_LAT)

    return n_lat, seq_map


def gla_prep_fwd(p3, upf, upb, bias_f, bias_b, n_samples):
    n_lat, seq_map = _prep_tile_maps(n_samples)
    n_tiles = n_lat + n_samples

    def body(v_ref, q_ref, k_ref, ab_ref, upf_ref, upb_ref, bf_ref, bb_ref, qo, ko, vo, cf, cb):
        i = pl.program_id(0)
        qo[0] = jnp.where(i < n_lat, q_ref[...].astype(F32) * Q_SCALE, 0.0)
        ko[0] = k_ref[...].astype(F32)
        vo[0] = v_ref[...].astype(F32)
        ab = ab_ref[...]
        gf = _log_sigmoid(_gate_logits(ab, upf_ref, bf_ref)) * (1.0 / GATE_TAU)
        gb = _log_sigmoid(_gate_logits(ab, upb_ref, bb_ref)) * (1.0 / GATE_TAU)
        cf[0] = _chunk_sums(gf, False)
        cb[0] = _chunk_sums(gb, True)

    def o_spec(w):
        return pl.BlockSpec((1, TM_PREP, w), lambda i: (*seq_map(i), 0))

    full = lambda shape: pl.BlockSpec(shape, lambda i: (0,) * len(shape))
    return _pallas(
        body, name="gla_prep_fwd", grid=(n_tiles,),
        in_specs=[pl.BlockSpec((TM_PREP, 1024), lambda i: (i, O3_V // 1024)),
                  pl.BlockSpec((TM_PREP, 512), lambda i: (i, O3_Q // 512)),
                  pl.BlockSpec((TM_PREP, 512), lambda i: (i, O3_K // 512)),
                  pl.BlockSpec((TM_PREP, 128), lambda i: (i, O3_AB // 128)),
                  full((3, 128, GLA_DK)), full((3, 128, GLA_DK)), full((1, GLA_DK)), full((1, GLA_DK))],
        out_specs=[o_spec(GLA_DK), o_spec(GLA_DK), o_spec(D), o_spec(GLA_DK), o_spec(GLA_DK)],
        out_shape=[jax.ShapeDtypeStruct((n_samples, SEQ_ALL, GLA_DK), F32),
                   jax.ShapeDtypeStruct((n_samples, SEQ_ALL, GLA_DK), F32),
                   jax.ShapeDtypeStruct((n_samples, SEQ_ALL, D), F32),
                   jax.ShapeDtypeStruct((n_samples, SEQ_ALL, GLA_DK), F32),
                   jax.ShapeDtypeStruct((n_samples, SEQ_ALL, GLA_DK), F32)],
        compiler_params=_params(("parallel",)),
    )(p3, p3, p3, p3, upf, upb, bias_f, bias_b)


def gla_prep_bwd(p3, dq_f, dq_b, dk_f, dk_b, dv_f, dv_b, dc_f, dc_b, upf, upb, bias_f, bias_b, n_samples):
    n_lat, seq_map = _prep_tile_maps(n_samples)
    n_tiles = n_lat + n_samples

    def body(ab_ref, dqf, dqb, dkf, dkb, dvf, dvb, dcf, dcb, upf_ref, upb_ref, bf_ref, bb_ref,
             dp_ref, duf_ref, dub_ref, dbf_ref, dbb_ref):
        i = pl.program_id(0)
        dp_ref[:, pl.ds(O3_V, D)] = (dvf[0] + dvb[0]).astype(dp_ref.dtype)
        dq = jnp.where(i < n_lat, (dqf[0] + dqb[0]) * Q_SCALE, 0.0)
        dp_ref[:, pl.ds(O3_Q, GLA_DK)] = dq.astype(dp_ref.dtype)
        dp_ref[:, pl.ds(O3_K, GLA_DK)] = (dkf[0] + dkb[0]).astype(dp_ref.dtype)
        ab = ab_ref[...]
        zf = _gate_logits(ab, upf_ref, bf_ref)
        zb = _gate_logits(ab, upb_ref, bb_ref)
        dgf = _chunk_sums(dcf[0], True)
        dgb = _chunk_sums(dcb[0], False)
        dzf = _b16(dgf * (1.0 / GATE_TAU) * _sigmoid(-zf))
        dzb = _b16(dgb * (1.0 / GATE_TAU) * _sigmoid(-zb))
        dab = _nt(dzf, upf_ref[0]) + _nt(dzb, upb_ref[0])
        dp_ref[:, pl.ds(O3_AB, 128)] = dab.astype(dp_ref.dtype)

        @pl.when(i == 0)
        def _():
            duf_ref[...] = jnp.zeros_like(duf_ref)
            dub_ref[...] = jnp.zeros_like(dub_ref)
            dbf_ref[...] = jnp.zeros_like(dbf_ref)
            dbb_ref[...] = jnp.zeros_like(dbb_ref)

        duf_ref[...] += _tn(ab, dzf)
        dub_ref[...] += _tn(ab, dzb)
        dbf_ref[...] += jnp.sum(dzf.astype(F32), axis=0, keepdims=True)
        dbb_ref[...] += jnp.sum(dzb.astype(F32), axis=0, keepdims=True)

    def s_spec(w):
        return pl.BlockSpec((1, TM_PREP, w), lambda i: (*seq_map(i), 0))

    full = lambda shape: pl.BlockSpec(shape, lambda i: (0,) * len(shape))
    return _pallas(
        body, name="gla_prep_bwd", grid=(n_tiles,),
        in_specs=[pl.BlockSpec((TM_PREP, 128), lambda i: (i, O3_AB // 128)),
                  s_spec(GLA_DK), s_spec(GLA_DK), s_spec(GLA_DK), s_spec(GLA_DK), s_spec(D), s_spec(D),
                  s_spec(GLA_DK), s_spec(GLA_DK),
                  full((3, 128, GLA_DK)), full((3, 128, GLA_DK)), full((1, GLA_DK)), full((1, GLA_DK))],
        out_specs=[pl.BlockSpec((TM_PREP, W3), lambda i: (i, 0)),
                   full((128, GLA_DK)), full((128, GLA_DK)), full((1, GLA_DK)), full((1, GLA_DK))],
        out_shape=[jax.ShapeDtypeStruct((n_tiles * TM_PREP, W3), BF16),
                   jax.ShapeDtypeStruct((128, GLA_DK), F32), jax.ShapeDtypeStruct((128, GLA_DK), F32),
                   jax.ShapeDtypeStruct((1, GLA_DK), F32), jax.ShapeDtypeStruct((1, GLA_DK), F32)],
        compiler_params=_params(("arbitrary",)),
    )(p3, dq_f, dq_b, dk_f, dk_b, dv_f, dv_b, dc_f, dc_b, upf, upb, bias_f, bias_b)


def _sub_blocks(rev):
    if NSUB == 1:
        return [((0, CHUNK), CHUNK // 2, (0, CHUNK))]
    out = []
    for s in range(NSUB):
        rows = (s * SUB, SUB)
        if rev:
            ref = (s + 1) * SUB if s < NSUB - 1 else None
            cols = (s * SUB, CHUNK - s * SUB)
        else:
            ref = s * SUB - 1 if s > 0 else None
            cols = (0, (s + 1) * SUB)
        out.append((rows, ref, cols))
    return out


def _sub_mask(rows, cols, rev):
    r = rows[0] + lax.broadcasted_iota(jnp.int32, (rows[1], cols[1]), 0)
    c = cols[0] + lax.broadcasted_iota(jnp.int32, (rows[1], cols[1]), 1)
    return (c >= r) if rev else (c <= r)


def _sub_operands(qc, kc, cc, rows, ref, cols):
    cref = jnp.zeros((1, HEAD_K), F32) if ref is None else cc[ref:ref + 1]
    eq = jnp.exp(cc[rows[0]:rows[0] + rows[1]] - cref)
    ek = jnp.exp(cref - cc[cols[0]:cols[0] + cols[1]])
    qs = qc[rows[0]:rows[0] + rows[1]] * eq
    kk = kc[cols[0]:cols[0] + cols[1]] * ek
    return qs, kk, eq, ek


SCAN_ROWS = 256
SCAN_CHUNKS = SCAN_ROWS // CHUNK
SCAN_STEPS = SEQ_ALL // SCAN_ROWS
LAT_BLOCKS = SEQ // SCAN_ROWS


def _scan_block(t, rev):
    if rev:
        return SCAN_STEPS - 1 - t
    return jnp.where(t == 0, SCAN_STEPS - 1, t - 1)


def _scan_lat_block(t, rev):
    if rev:
        return jnp.minimum(SCAN_STEPS - 1 - t, LAT_BLOCKS - 1)
    return jnp.maximum(t - 1, 0)


def _head_cols(h):
    return pl.ds(h * HEAD_K, HEAD_K), pl.ds(h * HEAD_V, HEAD_V)


def gla_scan_fwd(q, k, v, cum, *, rev, name):
    n = q.shape[0]

    def body(q_ref, k_ref, v_ref, c_ref, o_ref, s_ref, sfin_ref, st):
        t = pl.program_id(1)

        @pl.when(t == 0)
        def _():
            st[...] = jnp.zeros_like(st)

        def chunk(j, carry):
            lj = SCAN_CHUNKS - 1 - j if rev else j
            r0 = pl.multiple_of(lj * CHUNK, CHUNK)
            rws = pl.ds(r0, CHUNK)
            for h in range(HEADS):
                kcols, vcols = _head_cols(h)
                qc, kc, cc = q_ref[0, rws, kcols], k_ref[0, rws, kcols], c_ref[0, rws, kcols]
                vc = v_ref[0, rws, vcols]
                s_in = st[h]
                s_ref[0, h, j] = s_in
                edge = cc[0:1] if rev else cc[CHUNK - 1:CHUNK]
                ke = kc * jnp.exp(edge - cc)
                st[h] = s_in * jnp.exp(edge) + _tn(_b16(vc), _b16(ke))
                o_inter = _nt(_b16(qc * jnp.exp(cc)), _b16(s_in))
                vb = _b16(vc)
                for rows, ref, cols in _sub_blocks(rev):
                    qs, kk, _, _ = _sub_operands(qc, kc, cc, rows, ref, cols)
                    a = jnp.where(_sub_mask(rows, cols, rev), _nt(_b16(qs), _b16(kk)), 0.0)
                    o_s = _nn(_b16(a), vb[cols[0]:cols[0] + cols[1]])
                    o_ref[0, pl.ds(r0 + rows[0], rows[1]), vcols] = o_inter[rows[0]:rows[0] + rows[1]] + o_s
            return carry

        lax.fori_loop(0, SCAN_CHUNKS, chunk, 0)

        @pl.when(t == SCAN_STEPS - 1)
        def _():
            sfin_ref[0] = st[...]

    def spec(w):
        return pl.BlockSpec((1, SCAN_ROWS, w), lambda b, t: (b, _scan_block(t, rev), 0))

    return _pallas(
        body, name=name, grid=(n, SCAN_STEPS),
        in_specs=[spec(GLA_DK), spec(GLA_DK), spec(D), spec(GLA_DK)],
        out_specs=[pl.BlockSpec((1, SCAN_ROWS, D), lambda b, t: (b, _scan_lat_block(t, rev), 0)),
                   pl.BlockSpec((1, HEADS, SCAN_CHUNKS, HEAD_V, HEAD_K), lambda b, t: (b, 0, t, 0, 0)),
                   pl.BlockSpec((1, HEADS, HEAD_V, HEAD_K), lambda b, t: (b, 0, 0, 0))],
        out_shape=[jax.ShapeDtypeStruct((n, SEQ, D), F32),
                   jax.ShapeDtypeStruct((n, HEADS, NCHUNK, HEAD_V, HEAD_K), F32),
                   jax.ShapeDtypeStruct((n, HEADS, HEAD_V, HEAD_K), F32)],
        scratch_shapes=[pltpu.VMEM((HEADS, HEAD_V, HEAD_K), F32)],
        compiler_params=_params(("parallel", "arbitrary")),
    )(q, k, v, cum)


def gla_scan_bwd(q, k, v, cum, s_all, s_fin, do, *, rev, name):
    n = q.shape[0]

    def body(q_ref, k_ref, v_ref, c_ref, s_ref, sfin_ref, do_ref, dq_ref, dk_ref, dv_ref, dc_ref,
             dst, s_next, dq_acc, dk_acc, dv_acc):
        t = SCAN_STEPS - 1 - pl.program_id(1)

        @pl.when(pl.program_id(1) == 0)
        def _():
            dst[...] = jnp.zeros_like(dst)
            s_next[...] = sfin_ref[0]

        def chunk(jj, carry):
            j = SCAN_CHUNKS - 1 - jj
            lj = SCAN_CHUNKS - 1 - j if rev else j
            rws = pl.ds(pl.multiple_of(lj * CHUNK, CHUNK), CHUNK)
            for h in range(HEADS):
                kcols, vcols = _head_cols(h)
                qc, kc, cc = q_ref[0, rws, kcols], k_ref[0, rws, kcols], c_ref[0, rws, kcols]
                vc = v_ref[0, rws, vcols]
                doc = jnp.where(t > 0, do_ref[0, rws, vcols], 0.0)
                s_in = s_ref[0, h, j]
                s_out = s_next[h]
                ds_out = dst[h]
                edge = cc[0:1] if rev else cc[CHUNK - 1:CHUNK]
                e_q = jnp.exp(cc)
                e_k = jnp.exp(edge - cc)
                dob = _b16(doc)
                dsb = _b16(ds_out)
                dst[h] = ds_out * jnp.exp(edge) + _tn(dob, _b16(qc * e_q))
                s_next[h] = s_in
                dq_acc[h] = e_q * _nn(dob, _b16(s_in))
                dk_acc[h] = e_k * _nn(_b16(vc), dsb)
                dv_acc[h] = _nt(_b16(kc * e_k), dsb)
                vb = _b16(vc)
                for rows, ref, cols in _sub_blocks(rev):
                    qs, kk, eq, ek = _sub_operands(qc, kc, cc, rows, ref, cols)
                    mask = _sub_mask(rows, cols, rev)
                    rsl = slice(rows[0], rows[0] + rows[1])
                    csl = pl.ds(cols[0], cols[1])
                    qsb, kkb = _b16(qs), _b16(kk)
                    a = jnp.where(mask, _nt(qsb, kkb), 0.0)
                    da = _b16(jnp.where(mask, _nt(dob[rsl], vb[cols[0]:cols[0] + cols[1]]), 0.0))
                    dq_acc[h, pl.ds(rows[0], rows[1]), :] += _nn(da, kkb) * eq
                    dk_acc[h, csl, :] += _tn(da, qsb) * ek
                    dv_acc[h, csl, :] += _tn(_b16(a), dob[rsl])
                dq = dq_acc[h]
                dk = dk_acc[h]
                dc = qc * dq - kc * dk
                bnd = jnp.sum(ds_out * s_out, axis=0, keepdims=True)
                edge_row = 0 if rev else CHUNK - 1
                is_edge = lax.broadcasted_iota(jnp.int32, (CHUNK, HEAD_K), 0) == edge_row
                dq_ref[0, rws, kcols] = dq
                dk_ref[0, rws, kcols] = dk
                dv_ref[0, rws, vcols] = dv_acc[h]
                dc_ref[0, rws, kcols] = dc + jnp.where(is_edge, bnd, 0.0)
            return carry

        lax.fori_loop(0, SCAN_CHUNKS, chunk, 0)

    def step_of(u):
        return SCAN_STEPS - 1 - u

    def spec(w):
        return pl.BlockSpec((1, SCAN_ROWS, w), lambda b, u: (b, _scan_block(step_of(u), rev), 0))

    return _pallas(
        body, name=name, grid=(n, SCAN_STEPS),
        in_specs=[spec(GLA_DK), spec(GLA_DK), spec(D), spec(GLA_DK),
                  pl.BlockSpec((1, HEADS, SCAN_CHUNKS, HEAD_V, HEAD_K), lambda b, u: (b, 0, step_of(u), 0, 0)),
                  pl.BlockSpec((1, HEADS, HEAD_V, HEAD_K), lambda b, u: (b, 0, 0, 0)),
                  pl.BlockSpec((1, SCAN_ROWS, D), lambda b, u: (b, _scan_lat_block(step_of(u), rev), 0))],
        out_specs=[spec(GLA_DK), spec(GLA_DK), spec(D), spec(GLA_DK)],
        out_shape=[jax.ShapeDtypeStruct((n, SEQ_ALL, GLA_DK), F32), jax.ShapeDtypeStruct((n, SEQ_ALL, GLA_DK), F32),
                   jax.ShapeDtypeStruct((n, SEQ_ALL, D), F32), jax.ShapeDtypeStruct((n, SEQ_ALL, GLA_DK), F32)],
        scratch_shapes=[pltpu.VMEM((HEADS, HEAD_V, HEAD_K), F32), pltpu.VMEM((HEADS, HEAD_V, HEAD_K), F32),
                        pltpu.VMEM((HEADS, CHUNK, HEAD_K), F32), pltpu.VMEM((HEADS, CHUNK, HEAD_K), F32),
                        pltpu.VMEM((HEADS, CHUNK, HEAD_V), F32)],
        compiler_params=_params(("parallel", "arbitrary")),
    )(q, k, v, cum, s_all, s_fin, do)


def gla_out_fwd(o_f, o_b, r, gnorm):
    n = o_f.shape[0]
    tiles = SEQ // TM_EW

    def body(of_ref, ob_ref, r_ref, g_ref, og_ref):
        for h in range(HEADS):
            cols = pl.ds(h * HEAD_V, HEAD_V)
            o = of_ref[0, :, cols] + ob_ref[0, :, cols]
            rs = lax.rsqrt(jnp.mean(o * o, axis=-1, keepdims=True) + EPS)
            og_ref[:, cols] = (o * rs * g_ref[...] * _silu(r_ref[:, cols].astype(F32))).astype(og_ref.dtype)

    ospec = pl.BlockSpec((1, TM_EW, D), lambda b, j: (b, j, 0))
    row = pl.BlockSpec((TM_EW, D), lambda b, j: (b * tiles + j, 0))
    return _pallas(
        body, name="gla_out_fwd", grid=(n, tiles),
        in_specs=[ospec, ospec, row, pl.BlockSpec((1, HEAD_V), lambda b, j: (0, 0))],
        out_specs=row, out_shape=jax.ShapeDtypeStruct((n * SEQ, D), BF16),
        compiler_params=_params(("parallel", "parallel")),
    )(o_f, o_b, r, gnorm)


def gla_out_bwd(o_f, o_b, r, dyg, gla_proj, gnorm):
    n = o_f.shape[0]
    tiles = SEQ // TM_EW

    def body(of_ref, ob_ref, r_ref, d_ref, w_ref, g_ref, do_ref, dr_ref, dg_ref, dog_buf):
        @pl.when((pl.program_id(0) == 0) & (pl.program_id(1) == 0))
        def _():
            dg_ref[...] = jnp.zeros_like(dg_ref)

        dog_buf[...] = _nt(d_ref[...], w_ref[...])
        for h in range(HEADS):
            cols = pl.ds(h * HEAD_V, HEAD_V)
            o = of_ref[0, :, cols] + ob_ref[0, :, cols]
            rv = r_ref[:, cols].astype(F32)
            dv = dog_buf[:, cols]
            rs = lax.rsqrt(jnp.mean(o * o, axis=-1, keepdims=True) + EPS)
            oh = o * rs
            dr_ref[:, cols] = (dv * oh * g_ref[...] * _dsilu(rv)).astype(dr_ref.dtype)
            dn = dv * _silu(rv)
            dg_ref[...] += jnp.sum(dn * oh, axis=0, keepdims=True)
            doh = dn * g_ref[...]
            do_ref[0, :, cols] = rs * (doh - oh * jnp.mean(doh * oh, axis=-1, keepdims=True))

    ospec = pl.BlockSpec((1, TM_EW, D), lambda b, j: (b, j, 0))
    row = pl.BlockSpec((TM_EW, D), lambda b, j: (b * tiles + j, 0))
    vec = pl.BlockSpec((1, HEAD_V), lambda b, j: (0, 0))
    return _pallas(
        body, name="gla_out_bwd", grid=(n, tiles),
        in_specs=[ospec, ospec, row, row, pl.BlockSpec((D, D), lambda b, j: (0, 0)), vec],
        out_specs=[ospec, row, vec],
        out_shape=[jax.ShapeDtypeStruct((n, SEQ, D), F32), jax.ShapeDtypeStruct((n * SEQ, D), BF16),
                   jax.ShapeDtypeStruct((1, HEAD_V), F32)],
        scratch_shapes=[pltpu.VMEM((TM_EW, D), F32)],
        compiler_params=_params(("arbitrary", "arbitrary")),
    )(o_f, o_b, r, dyg, gla_proj, gnorm)


TM_OUT = 512


def merge_out_final(p5, y_conv, y_gla, w_out, x2, gate, final_g, target, n_samples):
    t = x2.shape[0]
    tiles = SEQ // TM_OUT

    def body(mc_ref, mg_ref, yc_ref, yg_ref, w_ref, x_ref, gate_ref, g_ref, t_ref,
             mrg_ref, dh_ref, dmo_ref, dgate_ref, dg_ref, loss_ref):
        b, j = pl.program_id(0), pl.program_id(1)
        f = lambda ref: ref[...].astype(F32)
        merged = _b16(_sigmoid(f(mc_ref)) * f(yc_ref) + _sigmoid(f(mg_ref)) * f(yg_ref))
        mrg_ref[...] = merged
        mo_v = _nn(merged, w_ref[...])
        h = x_ref[...] + gate_ref[0] * mo_v
        rs = lax.rsqrt(jnp.mean(h * h, axis=-1, keepdims=True) + EPS)
        nh = h * rs
        err = nh * g_ref[...] - t_ref[...]
        dy = err * (1.0 / D)
        dn = dy * g_ref[...]
        dh = rs * (dn - nh * jnp.mean(dn * nh, axis=-1, keepdims=True))
        dh_ref[...] = dh
        dmo_ref[...] = (dh * gate_ref[0]).astype(dmo_ref.dtype)

        @pl.when(j == 0)
        def _():
            dgate_ref[...] = jnp.zeros_like(dgate_ref)

        @pl.when((b == 0) & (j == 0))
        def _():
            dg_ref[...] = jnp.zeros_like(dg_ref)
            loss_ref[...] = jnp.zeros_like(loss_ref)

        dgate_ref[0] += jnp.sum(dh * mo_v, axis=0, keepdims=True)
        dg_ref[...] += jnp.sum(dy * nh, axis=0, keepdims=True)
        loss_ref[...] += (0.5 / D) * jnp.sum(err * err)

    row = pl.BlockSpec((TM_OUT, D), lambda b, j: (b * tiles + j, 0))
    per = pl.BlockSpec((1, 1, D), lambda b, j: (b, 0, 0))
    vec = pl.BlockSpec((1, D), lambda b, j: (0, 0))
    return _pallas(
        body, name="merge_out_final", grid=(n_samples, tiles),
        in_specs=[row, pl.BlockSpec((TM_OUT, D), lambda b, j: (b * tiles + j, 1)), row, row,
                  pl.BlockSpec((D, D), lambda b, j: (0, 0)), row, per, vec, row],
        out_specs=[row, row, row, per, vec, pl.BlockSpec((8, 128), lambda b, j: (0, 0))],
        out_shape=[jax.ShapeDtypeStruct((t, D), BF16), jax.ShapeDtypeStruct((t, D), F32), jax.ShapeDtypeStruct((t, D), BF16),
                   jax.ShapeDtypeStruct((n_samples, 1, D), F32), jax.ShapeDtypeStruct((1, D), F32),
                   jax.ShapeDtypeStruct((8, 128), F32)],
        compiler_params=_params(("arbitrary", "arbitrary")),
    )(p5, p5, y_conv, y_gla, w_out, x2, gate, final_g, target)


def out_dgrad_merge_bwd(p5, y_conv, y_gla, dmo, w_out):
    t = y_conv.shape[0]

    def body(mc_ref, mg_ref, yc_ref, yg_ref, d_ref, w_ref, dyc_ref, dyg_ref, dp_ref):
        f = lambda ref: ref[...].astype(F32)
        d = _nt(d_ref[...], w_ref[...])
        sc = _sigmoid(f(mc_ref))
        sg = _sigmoid(f(mg_ref))
        dyc_ref[...] = (d * sc).astype(dyc_ref.dtype)
        dyg_ref[...] = (d * sg).astype(dyg_ref.dtype)
        dp_ref[:, pl.ds(0, D)] = (d * f(yc_ref) * sc * (1.0 - sc)).astype(dp_ref.dtype)
        dp_ref[:, pl.ds(D, D)] = (d * f(yg_ref) * sg * (1.0 - sg)).astype(dp_ref.dtype)

    row = pl.BlockSpec((TM_OUT, D), lambda i: (i, 0))
    return _pallas(
        body, name="out_dgrad_merge_bwd", grid=(t // TM_OUT,),
        in_specs=[row, pl.BlockSpec((TM_OUT, D), lambda i: (i, 1)), row, row, row, pl.BlockSpec((D, D), lambda i: (0, 0))],
        out_specs=[row, row, pl.BlockSpec((TM_OUT, 2 * D), lambda i: (i, 0))],
        out_shape=[jax.ShapeDtypeStruct((t, D), BF16), jax.ShapeDtypeStruct((t, D), BF16),
                   jax.ShapeDtypeStruct((t, 2 * D), BF16)],
        compiler_params=_params(("parallel",)),
    )(p5, p5, y_conv, y_gla, dmo, w_out)


def local_step(x, ctx, target, mod, wts, small):
    n = x.shape[0]
    t = n * SEQ
    t_all = t + n * NCTX
    x2 = x.reshape(t, D)
    ctx2 = ctx.reshape(n * NCTX, D)
    tgt2 = target.reshape(t, D)
    scale1, shift, gate = mod

    u = norm_mod_fwd(x2, ctx2, scale1, shift, small["norm_g"])
    p1 = matmul_nn(u, wts["w1"], small["b1"], name="proj_conv", m=t, tm=1024, tn=1024, out_dtype=BF16)
    p2 = matmul_nn(u, wts["w2"], small["b2"], name="proj_z", m=t, tm=1024, tn=1024, out_dtype=BF16)
    p3 = matmul_nn(u, wts["w3"], small["b3"], name="proj_gla", m=t_all, tm=512, tn=W3, out_dtype=BF16)
    p4 = matmul_nn(u, wts["w4"], small["b4"], name="proj_r", m=t, tm=1024, tn=1024, out_dtype=BF16)
    p5 = matmul_nn(u, wts["w5"], small["b5"], name="proj_merge", m=t, tm=1024, tn=1024, out_dtype=BF16)

    aconv = conv_fwd(p1, small["conv_w"], small["conv_b"], n)
    ac = ln_gate_fwd(aconv, p2, small["conv_ln_g"], small["conv_ln_b"])
    y_conv = matmul_nn(ac, wts["conv_proj"], None, name="conv_proj_fwd", m=t, tm=1024, tn=1024, out_dtype=BF16)

    qs, ks, vs, cum_f, cum_b = gla_prep_fwd(p3, small["upf"], small["upb"], small["bias_f"], small["bias_b"], n)
    o_f, s_f, sfin_f = gla_scan_fwd(qs, ks, vs, cum_f, rev=False, name="gla_scan_fwd_f")
    o_b, s_b, sfin_b = gla_scan_fwd(qs, ks, vs, cum_b, rev=True, name="gla_scan_fwd_b")
    og = gla_out_fwd(o_f, o_b, p4, small["gla_norm_g"])
    y_gla = matmul_nn(og, wts["gla_proj"], None, name="gla_proj_fwd", m=t, tm=1024, tn=1024, out_dtype=BF16)

    merged, dh, dmo, dgate, d_final_g, loss = merge_out_final(p5, y_conv, y_gla, wts["w_out"], x2, gate,
                                                              small["final_norm_g"], tgt2, n)

    g = {"final_norm_g": d_final_g}
    g["w_out"] = matmul_tn(merged, dmo, name="w_out_wgrad", t=t, tn=1024, tt=1024)[0]
    dyc, dyg, dp5 = out_dgrad_merge_bwd(p5, y_conv, y_gla, dmo, wts["w_out"])

    g["conv_proj"] = matmul_tn(ac, dyc, name="conv_proj_wgrad", t=t, tn=1024, tt=1024)[0]
    daconv, dp2, g["conv_ln_g"], g["conv_ln_b"] = ln_gate_bwd(aconv, p2, dyc, wts["conv_proj"], small["conv_ln_g"],
                                                               small["conv_ln_b"])
    dp1, dconv_w, dconv_b = conv_bwd(p1, daconv, small["conv_w"], n)
    g["conv_w"], g["conv_b"] = dconv_w, dconv_b

    g["gla_proj"] = matmul_tn(og, dyg, name="gla_proj_wgrad", t=t, tn=1024, tt=1024)[0]
    do, dp4, g["gla_norm_g"] = gla_out_bwd(o_f, o_b, p4, dyg, wts["gla_proj"], small["gla_norm_g"])
    dq_f, dk_f, dv_f, dc_f = gla_scan_bwd(qs, ks, vs, cum_f, s_f, sfin_f, do, rev=False, name="gla_scan_bwd_f")
    dq_b, dk_b, dv_b, dc_b = gla_scan_bwd(qs, ks, vs, cum_b, s_b, sfin_b, do, rev=True, name="gla_scan_bwd_b")
    dp3, g["upf"], g["upb"], g["bias_f"], g["bias_b"] = gla_prep_bwd(
        p3, dq_f, dq_b, dk_f, dk_b, dv_f, dv_b, dc_f, dc_b,
        small["upf"], small["upb"], small["bias_f"], small["bias_b"], n)

    dps = [dp1, dp2, dp3, dp4, dp5]
    for i, dp in enumerate(dps):
        rows = dp.shape[0]
        tn = W3 if dp.shape[1] == W3 else 1024
        g["w%d" % (i + 1)], g["b%d" % (i + 1)] = matmul_tn(
            u, dp, name="w_in_wgrad_%d" % (i + 1), t=rows, tn=tn, tt=1024 if rows % 1024 == 0 else 768, colsum=True)
    g["gate"] = dgate
    return loss, dh, dps, g


def _group_cols(w):
    gv, gg, z = w[..., 0:1024], w[..., 1024:2048], w[..., 2048:3072]
    q, k, v = w[..., 3072:3584], w[..., 3584:4096], w[..., 4096:5120]
    ab = w[..., 5120:5152]
    r, mc, mg = w[..., 5152:6176], w[..., 6176:7200], w[..., 7200:8224]
    g1 = jnp.concatenate([p for j in range(CONV_NCB)
                          for p in (gv[..., CONV_CB * j:CONV_CB * (j + 1)], gg[..., CONV_CB * j:CONV_CB * (j + 1)])], -1)
    pad = jnp.zeros(w.shape[:-1] + (W3 - 2080,), w.dtype)
    g3 = jnp.concatenate([v, q, k, ab, pad], -1)
    return g1, z, g3, r, jnp.concatenate([mc, mg], -1)


def _ungroup_cols(g1, g2, g3, g4, g5):
    gv = jnp.concatenate([g1[..., 2 * CONV_CB * j:2 * CONV_CB * j + CONV_CB] for j in range(CONV_NCB)], -1)
    gg = jnp.concatenate([g1[..., 2 * CONV_CB * j + CONV_CB:2 * CONV_CB * (j + 1)] for j in range(CONV_NCB)], -1)
    v, q, k, ab = g3[..., 0:1024], g3[..., 1024:1536], g3[..., 1536:2048], g3[..., 2048:2080]
    return jnp.concatenate([gv, gg, g2, q, k, v, ab, g4, g5[..., 0:1024], g5[..., 1024:2048]], -1)


def _natural_pieces():
    pieces = [(CONV_CB * j, CONV_CB, 0, 2 * CONV_CB * j) for j in range(CONV_NCB)]
    pieces += [(1024 + CONV_CB * j, CONV_CB, 0, 2 * CONV_CB * j + CONV_CB) for j in range(CONV_NCB)]
    pieces += [(2048, 1024, 1, 0), (3072, 512, 2, O3_Q), (3584, 512, 2, O3_K), (4096, 1024, 2, O3_V), (5120, 32, 2, O3_AB),
               (5152, 1024, 3, 0), (6176, 1024, 4, 0), (7200, 1024, 4, 1024)]
    return sorted(pieces)


def _ungroup_to_shards(groups):
    shards = []
    for i in range(N_CHIPS):
        lo, hi = i * W_IN_SHARD, (i + 1) * W_IN_SHARD
        parts = []
        for nat, width, g, gcol in _natural_pieces():
            a, b = max(nat, lo), min(nat + width, hi)
            if a < b:
                parts.append(groups[g][:, gcol + a - nat:gcol + b - nat])
        shards.append(jnp.concatenate(parts, 1))
    return jnp.stack(shards)


def _pad_up(up, row0):
    return jnp.zeros((128, GLA_DK), F32).at[row0:row0 + up.shape[0]].set(up)


def _adamw_math(w, g, m, v):
    m = ADAM_B1 * m + (1.0 - ADAM_B1) * g
    v = ADAM_B2 * v + (1.0 - ADAM_B2) * (g * g)
    m_hat = m / (1.0 - ADAM_B1 ** ADAM_STEP)
    v_hat = v / (1.0 - ADAM_B2 ** ADAM_STEP)
    delta = -ADAM_LR * (m_hat / (jnp.sqrt(v_hat) + ADAM_EPS) + ADAM_WD * w)
    return delta, m, v


def adamw2d(w, g, m, v, *, name, tr, tcols=None):
    rows, cols = w.shape[-2:]

    def body(w_ref, g_ref, m_ref, v_ref, d_ref, nm_ref, nv_ref):
        d_ref[...], nm_ref[...], nv_ref[...] = _adamw_math(w_ref[...], g_ref[...], m_ref[...], v_ref[...])

    tcols = cols if tcols is None else tcols
    if w.ndim == 3:
        spec = pl.BlockSpec((1, tr, tcols), lambda i, j: (0, i, j))
    else:
        spec = pl.BlockSpec((tr, tcols), lambda i, j: (i, j))
    return _pallas(
        body, name=name, grid=(rows // tr, cols // tcols), in_specs=[spec] * 4, out_specs=[spec] * 3,
        out_shape=[jax.ShapeDtypeStruct(w.shape, F32)] * 3, compiler_params=_params(("parallel", "parallel")),
    )(w, g, m, v)


def adamw_many(ws, gs, ms, vs):
    k = len(ws)
    two = lambda a: a.reshape((-1, a.shape[-1]))

    def body(*refs):
        w_refs, g_refs, m_refs, v_refs = refs[:k], refs[k:2 * k], refs[2 * k:3 * k], refs[3 * k:4 * k]
        d_refs, nm_refs, nv_refs = refs[4 * k:5 * k], refs[5 * k:6 * k], refs[6 * k:7 * k]
        for i in range(k):
            d_refs[i][...], nm_refs[i][...], nv_refs[i][...] = _adamw_math(
                w_refs[i][...], g_refs[i][...], m_refs[i][...], v_refs[i][...])

    shapes = [jax.ShapeDtypeStruct(two(a).shape, F32) for a in ws]
    outs = _pallas(body, name="adamw_small", out_shape=shapes * 3, compiler_params=_params())(
        *[two(a) for a in ws], *[two(a) for a in gs], *[two(a) for a in ms], *[two(a) for a in vs])
    back = lambda lst: [o.reshape(a.shape) for o, a in zip(lst, ws)]
    return back(outs[:k]), back(outs[k:2 * k]), back(outs[2 * k:])


def sum_devices(sall, *, name):
    rows = sall.shape[1]

    def body(s_ref, o_ref):
        acc = s_ref[0]
        for d in range(1, N_DEV):
            acc = acc + s_ref[d]
        o_ref[...] = acc

    return _pallas(body, name=name, out_shape=jax.ShapeDtypeStruct((rows, D), F32),
                   compiler_params=_params())(sall)


def pair_add(core, g, got, *, name, tr):
    n, rows, cols = got.shape
    g4 = g.reshape(n, 2, rows, cols)

    def body(core_ref, g_ref, got_ref, ob_ref):
        del core_ref
        ob_ref[0] = (g_ref[0, 0] + got_ref[0]).astype(BF16)

    spec = pl.BlockSpec((1, tr, cols), lambda i, t, core_ref: (i, t, 0))
    return _pallas(
        body, name=name,
        grid_spec=pltpu.PrefetchScalarGridSpec(
            num_scalar_prefetch=1, grid=(n, rows // tr),
            in_specs=[pl.BlockSpec((1, 1, tr, cols), lambda i, t, core_ref: (i, core_ref[0], t, 0)), spec],
            out_specs=spec),
        out_shape=jax.ShapeDtypeStruct(got.shape, BF16),
        compiler_params=_params(("parallel", "parallel")))(core, g4, got)


def chip_add(place, pa, rb, *, name, tr):
    _, rows, cols = pa.shape

    def body(place_ref, m_ref, r_ref, o_ref):
        del place_ref
        o_ref[0] = ((m_ref[0].astype(F32) + r_ref[0].astype(F32)) + r_ref[1].astype(F32)) + r_ref[2].astype(F32)

    return _pallas(
        body, name=name,
        grid_spec=pltpu.PrefetchScalarGridSpec(
            num_scalar_prefetch=1, grid=(rows // tr,),
            in_specs=[pl.BlockSpec((1, tr, cols), lambda t, place_ref: (place_ref[0], t, 0)),
                      pl.BlockSpec((3, tr, cols), lambda t, place_ref: (0, t, 0))],
            out_specs=pl.BlockSpec((1, tr, cols), lambda t, place_ref: (place_ref[1], t, 0))),
        out_shape=jax.ShapeDtypeStruct((2, rows, cols), F32),
        compiler_params=_params(("parallel",)))(place, pa, rb)


def ada_bwd(call, cctx_rows, dm_shard, dm_full, adaw):
    nsh = adaw.shape[1]

    def body(c_ref, cc_ref, dms_ref, dmf_ref, w_ref, gw_ref, gb_ref, pq_ref):
        a_lat = _silu(c_ref[...])
        a_ctx = _silu(cc_ref[...])
        dms = dms_ref[...]
        gw_ref[...] = _tn(a_lat, dms[0:64], HI) + _tn(a_ctx, dms[64:72], HI)
        gb_ref[...] = jnp.sum(dmf_ref[...], axis=0, keepdims=True)
        part = _nt(dms[64:72], w_ref[...], HI)
        pq_ref[...] = jnp.zeros_like(pq_ref) + jnp.sum(part, axis=0, keepdims=True)

    return _pallas(body, name="ada_bwd",
                   out_shape=[jax.ShapeDtypeStruct((D, nsh), F32), jax.ShapeDtypeStruct((1, 3 * D), F32),
                              jax.ShapeDtypeStruct((8, D), F32)],
                   compiler_params=_params())(call, cctx_rows, dm_shard, dm_full, adaw)


def cctx_grad(pq_all, cctx_rows):
    def body(p_ref, c_ref, o_ref):
        acc = p_ref[0]
        for qi in range(1, N_CHIPS):
            acc = acc + p_ref[qi]
        o_ref[...] = acc * _dsilu(c_ref[...])

    return _pallas(body, name="cctx_grad", out_shape=jax.ShapeDtypeStruct((8, D), F32),
                   compiler_params=_params())(pq_all, cctx_rows)


def _place():
    x, y, c = lax.axis_index("x"), lax.axis_index("y"), lax.axis_index("c")
    chips = [(1 - x, y), (x, 1 - y), (1 - x, 1 - y)]
    return x, y, c, chips


def _all_peers(x, y, c):
    return [((1 - x) if r & 4 else x, (1 - y) if r & 2 else y, (1 - c) if r & 1 else c) for r in range(1, N_DEV)]


def _remote(src, dst, send_sem, recv_sem, dev):
    return pltpu.make_async_remote_copy(src_ref=src, dst_ref=dst, send_sem=send_sem, recv_sem=recv_sem,
                                        device_id=dev, device_id_type=MESH)


ANY = pl.BlockSpec(memory_space=pl.ANY)
VMEM = pl.BlockSpec(memory_space=pltpu.VMEM)
F_ROWS = 16


W_ROW_CHUNKS = 4
P_ROW_CHUNKS = 2
N_BULK = W_ROW_CHUNKS + P_ROW_CHUNKS


def _half_chunks(core, n_rows, align):
    out = []
    for a, k in ((0, W_ROW_CHUNKS), (1, P_ROW_CHUNKS)):
        half = n_rows[a] // 2
        size = half // k
        for i in range(k):
            start = core * half + i * size
            out.append((a, pl.ds(start if isinstance(start, int) else pl.multiple_of(start, align), size)))
    return out


def gather_weights(c8, cctx8, adaw, adab, w_sh, p_sh, fp):
    nsh = adaw.shape[1]

    def body(c_ref, cctx_ref, adaw_ref, adab_ref, w_ref, p_ref, fp_ref, wall_ref, pall_ref, fall_ref, call_ref, mall_ref,
             abuf, w_send, w_recv, h_send, h_recv, c_send, c_recv, m_send, m_recv, f_send, f_recv):
        x, y, c, chips = _place()
        q = 2 * x + y
        dev = 4 * x + 2 * y + c
        qs = [2 * cx + cy for cx, cy in chips]
        sib = (x, y, 1 - c)
        srcs, dsts = (w_ref, p_ref), (wall_ref, pall_ref)
        n_rows = (w_ref.shape[0], p_ref.shape[0])
        mine = _half_chunks(c, n_rows, 16)
        other = _half_chunks(1 - c, n_rows, 16)

        bulk = [[_remote(srcs[a].at[rows], dsts[a].at[q, rows], w_send.at[j * N_BULK + i], w_recv.at[j * N_BULK + i],
                         (*chips[j], c)) for i, (a, rows) in enumerate(mine)] for j in range(3)]
        fall_ref[q] = fp_ref[...]
        small = [_remote(fp_ref, fall_ref.at[q], f_send.at[j], f_recv.at[j], (*chips[j], c)) for j in range(3)]
        my_rows = pl.ds(pl.multiple_of(8 * dev, 8), 8)
        call_ref[my_rows, :] = c_ref[...]
        cond = [_remote(c_ref, call_ref.at[my_rows, :], c_send.at[r], c_recv.at[r], peer)
                for r, peer in enumerate(_all_peers(x, y, c))]
        for cp in sum(bulk, []) + small + cond:
            cp.start()
        for cp in cond:
            cp.wait_recv()

        abuf[pl.ds(0, 64), :] = _silu(call_ref[...])
        abuf[pl.ds(64, 8), :] = _silu(cctx_ref[...])
        mall_ref[q] = _nn(abuf[...], adaw_ref[...], HI) + adab_ref[...]
        mod = [_remote(mall_ref.at[q], mall_ref.at[q], m_send.at[j], m_recv.at[j], (*chips[j], c)) for j in range(3)]
        for cp in mod:
            cp.start()

        handed = []
        for j in range(3):
            for i, (a, rows) in enumerate(mine):
                bulk[j][i].wait_recv()
                cp = _remote(dsts[a].at[qs[j], rows], dsts[a].at[qs[j], rows],
                             h_send.at[j * N_BULK + i], h_recv.at[j * N_BULK + i], sib)
                cp.start()
                handed.append(cp)
        for j in range(3):
            for i, (a, rows) in enumerate(other):
                _remote(dsts[a].at[qs[j], rows], dsts[a].at[qs[j], rows],
                        h_send.at[j * N_BULK + i], h_recv.at[j * N_BULK + i], sib).wait_recv()
        for cp in mod + small:
            cp.wait_recv()
        for cp in sum(bulk, []) + small + cond + mod + handed:
            cp.wait_send()

    def dma(n):
        return pltpu.SemaphoreType.DMA((n,))

    return _pallas(
        body, name="gather_weights",
        in_specs=[VMEM, VMEM, VMEM, VMEM, ANY, ANY, VMEM],
        out_specs=[ANY, ANY, VMEM, VMEM, VMEM],
        out_shape=[jax.ShapeDtypeStruct((N_CHIPS,) + w_sh.shape, BF16), jax.ShapeDtypeStruct((N_CHIPS,) + p_sh.shape, BF16),
                   jax.ShapeDtypeStruct((N_CHIPS, F_ROWS, D), F32),
                   jax.ShapeDtypeStruct((8 * N_DEV, D), F32), jax.ShapeDtypeStruct((N_CHIPS, MOD_ROWS, nsh), F32)],
        scratch_shapes=[pltpu.VMEM((MOD_ROWS, D), F32), dma(3 * N_BULK), dma(3 * N_BULK), dma(3 * N_BULK), dma(3 * N_BULK),
                        dma(7), dma(7), dma(3), dma(3), dma(3), dma(3)],
        compiler_params=_params(),
    )(c8, cctx8, adaw, adab, w_sh, p_sh, fp)


def pair_swap(gs, gp, sm):
    n_gs = len(gs)
    n_pair = n_gs * W_ROW_CHUNKS + N_CHIPS * P_ROW_CHUNKS

    def body(*refs):
        g_refs, gp_ref, sm_ref = refs[:n_gs], refs[n_gs], refs[n_gs + 1]
        got_refs, gotp_ref, sall_ref = refs[n_gs + 2:2 * n_gs + 2], refs[2 * n_gs + 2], refs[2 * n_gs + 3]
        a_send, a_recv, s_send, s_recv = refs[2 * n_gs + 4:]
        x, y, c, _ = _place()
        dev = 4 * x + 2 * y + c
        sib = (x, y, 1 - c)
        pair = []
        half, size = D // 2, D // 2 // W_ROW_CHUNKS
        for gi in range(n_gs):
            for i in range(W_ROW_CHUNKS):
                k = len(pair)
                rows_o = pl.ds(pl.multiple_of((1 - c) * half + i * size, 8), size)
                pair.append(_remote(g_refs[gi].at[rows_o], got_refs[gi].at[pl.ds(i * size, size)],
                                    a_send.at[k], a_recv.at[k], sib))
        half, size = gp_ref.shape[1] // 2, gp_ref.shape[1] // 2 // P_ROW_CHUNKS
        for s in range(N_CHIPS):
            for i in range(P_ROW_CHUNKS):
                k = len(pair)
                rows_o = pl.ds(pl.multiple_of((1 - c) * half + i * size, 8), size)
                pair.append(_remote(gp_ref.at[s, rows_o], gotp_ref.at[s, pl.ds(i * size, size)],
                                    a_send.at[k], a_recv.at[k], sib))
        sall_ref[dev] = sm_ref[...]
        small = [_remote(sm_ref, sall_ref.at[dev], s_send.at[r], s_recv.at[r], peer)
                 for r, peer in enumerate(_all_peers(x, y, c))]
        for cp in pair + small:
            cp.start()
        for cp in small + pair:
            cp.wait_recv()
        for cp in small + pair:
            cp.wait_send()

    return _pallas(
        body, name="pair_swap", in_specs=[ANY] * (n_gs + 1) + [VMEM], out_specs=[ANY] * (n_gs + 1) + [VMEM],
        out_shape=[jax.ShapeDtypeStruct((D // 2, a.shape[1]), F32) for a in gs]
        + [jax.ShapeDtypeStruct((N_CHIPS, gp.shape[1] // 2, gp.shape[2]), F32),
           jax.ShapeDtypeStruct((N_DEV,) + sm.shape, F32)],
        scratch_shapes=[pltpu.SemaphoreType.DMA((n_pair,)), pltpu.SemaphoreType.DMA((n_pair,)),
                        pltpu.SemaphoreType.DMA((N_DEV - 1,)), pltpu.SemaphoreType.DMA((N_DEV - 1,))],
        compiler_params=_params(),
    )(*gs, gp, sm)


def gather_small(sm):
    rows = sm.shape[0]

    def body(sm_ref, sall_ref, s_send, s_recv):
        x, y, c, _ = _place()
        dev = 4 * x + 2 * y + c
        sall_ref[dev] = sm_ref[...]
        small = [_remote(sm_ref, sall_ref.at[dev], s_send.at[r], s_recv.at[r], peer)
                 for r, peer in enumerate(_all_peers(x, y, c))]
        for cp in small:
            cp.start()
        for cp in small:
            cp.wait_recv()
        for cp in small:
            cp.wait_send()

    return _pallas(
        body, name="gather_small", in_specs=[VMEM], out_specs=VMEM,
        out_shape=jax.ShapeDtypeStruct((N_DEV, rows, D), F32),
        scratch_shapes=[pltpu.SemaphoreType.DMA((7,)), pltpu.SemaphoreType.DMA((7,))],
        compiler_params=_params(),
    )(sm)


def pair_share(ghw, ghp, pq):
    def body(ghw_ref, ghp_ref, pq_ref, outw_ref, outp_ref, pqa_ref, send, recv, p_send, p_recv):
        del ghw_ref, ghp_ref
        x, y, c, chips = _place()
        q = 2 * x + y
        refs = (outw_ref, outp_ref)
        n_rows = (2 * outw_ref.shape[1], 2 * outp_ref.shape[1])
        pair = [_remote(refs[a].at[c, rows], refs[a].at[c, rows], send.at[i], recv.at[i], (x, y, 1 - c))
                for i, (a, rows) in enumerate(_half_chunks(0, n_rows, 8))]
        pqa_ref[q] = pq_ref[...]
        small = [_remote(pq_ref, pqa_ref.at[q], p_send.at[j], p_recv.at[j], (*chips[j], c)) for j in range(3)]
        for cp in pair + small:
            cp.start()
        for i, (a, rows) in enumerate(_half_chunks(0, n_rows, 8)):
            _remote(refs[a].at[1 - c, rows], refs[a].at[1 - c, rows], send.at[i], recv.at[i], (x, y, 1 - c)).wait_recv()
        for cp in small:
            cp.wait_recv()
        for cp in pair + small:
            cp.wait_send()

    return _pallas(
        body, name="pair_share", in_specs=[ANY, ANY, VMEM], out_specs=[ANY, ANY, VMEM],
        out_shape=[jax.ShapeDtypeStruct(ghw.shape, F32), jax.ShapeDtypeStruct(ghp.shape, F32),
                   jax.ShapeDtypeStruct((N_CHIPS, 8, D), F32)],
        scratch_shapes=[pltpu.SemaphoreType.DMA((N_BULK,)), pltpu.SemaphoreType.DMA((N_BULK,)),
                        pltpu.SemaphoreType.DMA((3,)), pltpu.SemaphoreType.DMA((3,))],
        input_output_aliases={0: 0, 1: 1},
        compiler_params=_params(),
    )(ghw, ghp, pq)


def _rows_of(shape):
    size = 1
    for s in shape:
        size *= s
    return -(-size // D)


def _pack(arrs, rows_multiple=8):
    parts = []
    total = 0
    for a in arrs:
        f = a.reshape(-1).astype(F32)
        r = _rows_of(a.shape)
        parts.append(jnp.pad(f, (0, r * D - f.shape[0])))
        total += r
    pad_rows = (-total) % rows_multiple
    if pad_rows:
        parts.append(jnp.zeros((pad_rows * D,), F32))
    return jnp.concatenate(parts).reshape(-1, D)


def _unpack(p, shapes):
    out = []
    r0 = 0
    for shp in shapes:
        r = _rows_of(shp)
        size = 1
        for s in shp:
            size *= s
        out.append(p[r0:r0 + r].reshape(-1)[:size].reshape(shp))
        r0 += r
    return out


WEIGHT_NAMES = ['c_ctx', 'ada_w', 'ada_b', 'norm_g', 'w_in', 'b_in', 'conv_w', 'conv_b', 'conv_ln_g', 'conv_ln_b',
                'conv_proj', 'decay_up_fwd', 'decay_bias_fwd', 'decay_up_bwd', 'decay_bias_bwd', 'gla_norm_g', 'gla_proj',
                'w_out', 'final_norm_g']
SMALL_NAMES = ['c_ctx', 'ada_b', 'norm_g', 'b_in', 'conv_w', 'conv_b', 'conv_ln_g', 'conv_ln_b', 'decay_up_fwd',
               'decay_bias_fwd', 'decay_up_bwd', 'decay_bias_bwd', 'gla_norm_g', 'final_norm_g']


def kernel(x, c, ctx, c_ctx, ada_w, ada_b, norm_g, w_in, b_in, conv_w, conv_b, conv_ln_g, conv_ln_b, conv_proj, decay_up_fwd, decay_bias_fwd, decay_up_bwd, decay_bias_bwd, gla_norm_g, gla_proj, w_out, final_norm_g, loss_target, m_c_ctx, m_ada_w, m_ada_b, m_norm_g, m_w_in, m_b_in, m_conv_w, m_conv_b, m_conv_ln_g, m_conv_ln_b, m_conv_proj, m_decay_up_fwd, m_decay_bias_fwd, m_decay_up_bwd, m_decay_bias_bwd, m_gla_norm_g, m_gla_proj, m_w_out, m_final_norm_g, v_c_ctx, v_ada_w, v_ada_b, v_norm_g, v_w_in, v_b_in, v_conv_w, v_conv_b, v_conv_ln_g, v_conv_ln_b, v_conv_proj, v_decay_up_fwd, v_decay_bias_fwd, v_decay_up_bwd, v_decay_bias_bwd, v_gla_norm_g, v_gla_proj, v_w_out, v_final_norm_g):
    w = dict(c_ctx=c_ctx, ada_w=ada_w, ada_b=ada_b, norm_g=norm_g, w_in=w_in, b_in=b_in, conv_w=conv_w, conv_b=conv_b,
             conv_ln_g=conv_ln_g, conv_ln_b=conv_ln_b, conv_proj=conv_proj, decay_up_fwd=decay_up_fwd,
             decay_bias_fwd=decay_bias_fwd, decay_up_bwd=decay_up_bwd, decay_bias_bwd=decay_bias_bwd,
             gla_norm_g=gla_norm_g, gla_proj=gla_proj, w_out=w_out, final_norm_g=final_norm_g)
    m = dict(c_ctx=m_c_ctx, ada_w=m_ada_w, ada_b=m_ada_b, norm_g=m_norm_g, w_in=m_w_in, b_in=m_b_in, conv_w=m_conv_w,
             conv_b=m_conv_b, conv_ln_g=m_conv_ln_g, conv_ln_b=m_conv_ln_b, conv_proj=m_conv_proj,
             decay_up_fwd=m_decay_up_fwd, decay_bias_fwd=m_decay_bias_fwd, decay_up_bwd=m_decay_up_bwd,
             decay_bias_bwd=m_decay_bias_bwd, gla_norm_g=m_gla_norm_g, gla_proj=m_gla_proj, w_out=m_w_out,
             final_norm_g=m_final_norm_g)
    v = dict(c_ctx=v_c_ctx, ada_w=v_ada_w, ada_b=v_ada_b, norm_g=v_norm_g, w_in=v_w_in, b_in=v_b_in, conv_w=v_conv_w,
             conv_b=v_conv_b, conv_ln_g=v_conv_ln_g, conv_ln_b=v_conv_ln_b, conv_proj=v_conv_proj,
             decay_up_fwd=v_decay_up_fwd, decay_bias_fwd=v_decay_bias_fwd, decay_up_bwd=v_decay_up_bwd,
             decay_bias_bwd=v_decay_bias_bwd, gla_norm_g=v_gla_norm_g, gla_proj=v_gla_proj, w_out=v_w_out,
             final_norm_g=v_final_norm_g)
    n = x.shape[0]
    ax, ay, ac = lax.axis_index("x"), lax.axis_index("y"), lax.axis_index("c")
    q = 2 * ax + ay
    dev = 4 * ax + 2 * ay + ac
    nsh = ada_w.shape[2]

    w_sh = w_in[0].astype(BF16)
    p_sh = jnp.concatenate([conv_proj[0], gla_proj[0], w_out[0]], 0).astype(BF16)
    fp = _pack([conv_w[0], decay_up_fwd[0], decay_up_bwd[0]], F_ROWS)
    c8 = jnp.pad(c, ((0, 8 - n), (0, 0)))
    cctx8 = jnp.pad(c_ctx[None], ((0, 7), (0, 0)))
    adab_sh = lax.dynamic_slice(ada_b, (0, q * nsh), (1, nsh))
    w_all, p_all, fall, call, mall = gather_weights(c8, cctx8, ada_w[0], adab_sh, w_sh, p_sh, fp)

    mod_all = jnp.transpose(mall, (1, 0, 2)).reshape(MOD_ROWS, 3 * D)
    mod_mine = lax.dynamic_slice(mod_all, (8 * dev, 0), (n, 3 * D))
    mod_ctx = mod_all[64:65]
    shift = jnp.concatenate([mod_mine[:, 0:D], mod_ctx[:, 0:D]], 0)[:, None, :]
    scale1 = 1.0 + jnp.concatenate([mod_mine[:, D:2 * D], mod_ctx[:, D:2 * D]], 0)[:, None, :]
    gate = mod_mine[:, 2 * D:3 * D][:, None, :]

    own = lambda i, mine, got: jnp.where(q == i, mine, got)
    g1, g2, g3, g4, g5 = _group_cols(jnp.concatenate([own(i, w_sh, w_all[i]) for i in range(N_CHIPS)], 1))
    p_full = jnp.stack([own(i, p_sh, p_all[i]) for i in range(N_CHIPS)])
    wts = dict(w1=g1, w2=g2, w3=g3, w4=g4, w5=g5,
               conv_proj=p_full[:, 0:256].reshape(D, D), gla_proj=p_full[:, 256:512].reshape(D, D),
               w_out=p_full[:, 512:768].reshape(D, D))
    f_parts = [_unpack(fall[i], [conv_w.shape[1:], decay_up_fwd.shape[1:], decay_up_bwd.shape[1:]]) for i in range(N_CHIPS)]
    conv_w_full = jnp.concatenate([p[0] for p in f_parts], 1)
    upf_full = jnp.concatenate([p[1] for p in f_parts], 1)
    upb_full = jnp.concatenate([p[2] for p in f_parts], 1)
    b1, b2, b3, b4, b5 = _group_cols(b_in)
    small = dict(b1=b1, b2=b2, b3=b3, b4=b4, b5=b5, norm_g=norm_g,
                 conv_w=jnp.pad(conv_w_full, ((0, 1), (0, 0))), conv_b=conv_b, conv_ln_g=conv_ln_g, conv_ln_b=conv_ln_b,
                 upf=_split3(_pad_up(upf_full, 0)), upb=_split3(_pad_up(upb_full, 16)),
                 bias_f=decay_bias_fwd, bias_b=decay_bias_bwd,
                 gla_norm_g=gla_norm_g, final_norm_g=final_norm_g[None])

    loss_part, dh, dps, g = local_step(x, ctx, loss_target, (scale1, shift, gate), wts, small)
    loss = lax.psum(loss_part[0, 0], ("x", "y", "c"))

    gs = [g["w%d" % i] for i in range(1, 6)]
    gp = jnp.concatenate([g["conv_proj"].reshape(N_CHIPS, 256, D), g["gla_proj"].reshape(N_CHIPS, 256, D),
                          g["w_out"].reshape(N_CHIPS, 256, D)], 1)
    d_b_in = _ungroup_cols(*[g["b%d" % i] for i in range(1, 6)])
    early = [d_b_in, g["conv_b"].sum(0), g["conv_ln_g"], g["conv_ln_b"], g["bias_f"], g["bias_b"],
             g["gla_norm_g"], g["final_norm_g"], g["conv_w"].sum(0)[:CONV_K], g["upf"][0:16], g["upb"][16:32]]
    early_shapes = [a.shape for a in early]
    *gots, gotp, sall1 = pair_swap(gs, gp, _pack(early))
    core = ac.astype(jnp.int32).reshape(1)
    chip = q.astype(jnp.int32).reshape(1)
    halves = [pair_add(core, a[None], got[None], name="pair_add_w%d" % (i + 1), tr=128)[0]
              for i, (a, got) in enumerate(zip(gs, gots))]
    paw16 = _ungroup_to_shards(halves)
    pap16 = pair_add(core, gp, gotp, name="pair_add_p", tr=384)
    grad_x2, dshift, dscale, g["norm_g"], rbw, rbp = dgrad_norm_bwd(
        dps, [wts["w%d" % i] for i in range(1, 6)], paw16, pap16, x.reshape(n * SEQ, D), ctx.reshape(n * NCTX, D), dh,
        scale1, norm_g, tm=256)

    dm_mine = jnp.concatenate([dshift[:n, 0], dscale[:n, 0], g["gate"][:, 0]], -1)
    dm_ctx = jnp.concatenate([dshift[n, 0], dscale[n, 0], jnp.zeros((D,), F32)], -1)
    late = [g["norm_g"], dm_mine, dm_ctx]
    late_shapes = [a.shape for a in late]
    sall2 = gather_small(_pack(late))
    (s_b_in, s_conv_b, s_ln_g, s_ln_b, s_bias_f, s_bias_b, s_gla_g, s_final_g, s_conv_w, s_upf,
     s_upb) = _unpack(sum_devices(sall1, name="sum_devices_early"), early_shapes)
    s_norm_g = _unpack(sum_devices(sall2, name="sum_devices_late"), late_shapes)[0]
    r_mine, r_ctx = 1, 1 + 3 * n
    dm_all = sall2[:, r_mine:r_ctx].reshape(N_DEV, n, 3 * D)
    dm_full = jnp.concatenate([jnp.pad(dm_all, ((0, 0), (0, 8 - n), (0, 0))).reshape(8 * N_DEV, 3 * D),
                               sall2[:, r_ctx:r_ctx + 3].reshape(N_DEV, 3 * D)], 0)
    dm_shard = lax.dynamic_slice(dm_full, (0, q * nsh), (MOD_ROWS, nsh))
    cctx_rows = jnp.broadcast_to(c_ctx[None], (8, D))
    g_ada_w, g_ada_b, pq = ada_bwd(call, cctx_rows, dm_shard, dm_full, ada_w[0])

    place = jnp.concatenate([chip, core])
    ghw = chip_add(place, paw16, rbw, name="chip_add_w", tr=128)
    ghp = chip_add(place, pap16, rbp, name="chip_add_p", tr=384)
    gw_mine, gp_mine, pq_all = pair_share(ghw, ghp, pq)
    gp_mine = gp_mine.reshape(768, D)
    g_c_ctx = cctx_grad(pq_all, cctx_rows)[0]

    grads = dict(
        c_ctx=g_c_ctx, ada_w=g_ada_w[None], ada_b=g_ada_b, norm_g=s_norm_g,
        w_in=gw_mine.reshape(1, D, W_IN_SHARD), b_in=s_b_in,
        conv_w=lax.dynamic_slice(s_conv_w, (0, q * 256), (CONV_K, 256))[None], conv_b=s_conv_b,
        conv_ln_g=s_ln_g, conv_ln_b=s_ln_b, conv_proj=gp_mine[0:256][None],
        decay_up_fwd=lax.dynamic_slice(s_upf, (0, q * 128), (16, 128))[None], decay_bias_fwd=s_bias_f,
        decay_up_bwd=lax.dynamic_slice(s_upb, (0, q * 128), (16, 128))[None], decay_bias_bwd=s_bias_b,
        gla_norm_g=s_gla_g, gla_proj=gp_mine[256:512][None], w_out=gp_mine[512:768][None],
        final_norm_g=s_final_g[0])

    delta, new_m, new_v = {}, {}, {}
    for name in ["ada_w", "conv_proj", "gla_proj", "w_out"]:
        delta[name], new_m[name], new_v[name] = adamw2d(w[name], grads[name].reshape(w[name].shape), m[name], v[name],
                                                        name="adamw_" + name, tr=128)
    tr_ = lambda a: jnp.swapaxes(a, 1, 2)
    g_w_in_t = tr_(grads["w_in"])
    grads["w_in"] = tr_(g_w_in_t)
    d_, m_, v_ = adamw2d(tr_(w_in), g_w_in_t, tr_(m_w_in), tr_(v_w_in), name="adamw_w_in", tr=W_IN_SHARD, tcols=128)
    delta["w_in"], new_m["w_in"], new_v["w_in"] = tr_(d_), tr_(m_), tr_(v_)
    d_, m_, v_ = adamw_many([w[nm] for nm in SMALL_NAMES], [grads[nm].reshape(w[nm].shape) for nm in SMALL_NAMES],
                            [m[nm] for nm in SMALL_NAMES], [v[nm] for nm in SMALL_NAMES])
    for nm, a, b, cc in zip(SMALL_NAMES, d_, m_, v_):
        delta[nm], new_m[nm], new_v[nm] = a, b, cc

    grad_x = grad_x2.reshape(x.shape)
    return (loss, grad_x, *[grads[nm].reshape(w[nm].shape) for nm in WEIGHT_NAMES], *[delta[nm] for nm in WEIGHT_NAMES],
            *[new_m[nm] for nm in WEIGHT_NAMES], *[new_v[nm] for nm in WEIGHT_NAMES])
```

```python
import jax
import jax.numpy as jnp
from jax import lax
from jax.experimental import pallas as pl
from jax.experimental.pallas import tpu as pltpu

F32 = jnp.float32
BF16 = jnp.bfloat16
MESH = pl.DeviceIdType.MESH
HI = lax.Precision.HIGHEST

D = 1024
SEQ = 2048
GRID_W = 64
GRID_H = SEQ // GRID_W
NCTX = 256
SEQ_ALL = SEQ + NCTX
EPS = 1e-6
CONV_K = 31
CONV_PAD = CONV_K // 2
HEADS = 4
HEAD_K = 128
HEAD_V = 256
GLA_DK = HEADS * HEAD_K
GATE_TAU = 16.0
Q_SCALE = HEAD_K ** -0.5
CHUNK = 64
NCHUNK = SEQ_ALL // CHUNK
NCHUNK_LAT = SEQ // CHUNK
NCHUNK_CTX = NCHUNK - NCHUNK_LAT
SUB = 64
NSUB = CHUNK // SUB
N_IN = 8224
W3 = 2176
O3_V, O3_Q, O3_K, O3_AB = 0, 1024, 1536, 2048

ADAM_LR, ADAM_B1, ADAM_B2, ADAM_EPS, ADAM_WD, ADAM_STEP = 0.001, 0.9, 0.999, 1e-08, 0.01, 10
VMEM_LIMIT = 56 * 1024 * 1024

N_CHIPS = 4
N_DEV = 8
W_IN_SHARD = N_IN // N_CHIPS
MOD_ROWS = 72


def _pallas(body, **kw):
    return pl.pallas_call(body, **kw)


def _params(sem=None, **kw):
    if sem is not None:
        kw["dimension_semantics"] = sem
    return pltpu.CompilerParams(vmem_limit_bytes=VMEM_LIMIT, **kw)


def _sigmoid(v):
    return 1.0 / (1.0 + jnp.exp(-v))


def _silu(v):
    return v * _sigmoid(v)


def _dsilu(v):
    s = _sigmoid(v)
    return s * (1.0 + v * (1.0 - s))


def _log_sigmoid(v):
    return jnp.minimum(v, 0.0) - jnp.log(1.0 + jnp.exp(-jnp.abs(v)))


def _dot(a, b, dims, precision=None):
    return lax.dot_general(a, b, (dims, ((), ())), preferred_element_type=F32, precision=precision)


def _nn(a, b, precision=None):
    return _dot(a, b, ((1,), (0,)), precision)


def _nt(a, b, precision=None):
    return _dot(a, b, ((1,), (1,)), precision)


def _tn(a, b, precision=None):
    return _dot(a, b, ((0,), (0,)), precision)


def _b16(v):
    return v.astype(BF16)


def matmul_nn(a, b, bias, *, name, m, tm, tn, out_dtype):
    k = a.shape[1]
    n = b.shape[1]
    has_bias = bias is not None

    def body(*refs):
        if has_bias:
            a_ref, b_ref, bias_ref, o_ref = refs
            acc = _nn(a_ref[...], b_ref[...]) + bias_ref[...]
        else:
            a_ref, b_ref, o_ref = refs
            acc = _nn(a_ref[...], b_ref[...])
        o_ref[...] = acc.astype(o_ref.dtype)

    in_specs = [pl.BlockSpec((tm, k), lambda j, i: (i, 0)), pl.BlockSpec((k, tn), lambda j, i: (0, j))]
    args = [a, b]
    if has_bias:
        in_specs.append(pl.BlockSpec((1, tn), lambda j, i: (0, j)))
        args.append(bias)
    return _pallas(
        body, name=name, grid=(n // tn, m // tm), in_specs=in_specs,
        out_specs=pl.BlockSpec((tm, tn), lambda j, i: (i, j)),
        out_shape=jax.ShapeDtypeStruct((m, n), out_dtype),
        compiler_params=_params(("parallel", "parallel")),
    )(*args)


def matmul_tn(a, b, *, name, t, tn, tt, colsum=False, swap=None):
    m = a.shape[1]
    n = b.shape[1]
    nj, ns = n // tn, t // tt
    n_out = 2 if colsum else 1
    n_sw = 0 if swap is None else len(swap[0])

    def body(a_ref, b_ref, *rest):
        o_ref = rest[n_sw + 1] if swap is not None else rest[0]
        cs_ref = (rest[n_sw + 2] if swap is not None else rest[1]) if colsum else None
        j, s = pl.program_id(0), pl.program_id(1)

        if swap is not None:
            g_refs, gp_ref = rest[:n_sw], rest[n_sw]
            got_refs, gotp_ref = rest[n_sw + 1 + n_out:2 * n_sw + 1 + n_out], rest[2 * n_sw + 1 + n_out]
            sems = rest[2 * n_sw + 2 + n_out:]

            @pl.when((j == 0) & (s == 0))
            def _():
                for cp in _pair_copies(g_refs, gp_ref, got_refs, gotp_ref, *sems):
                    cp.start()

            @pl.when((j == nj - 1) & (s == ns - 1))
            def _():
                for cp in _pair_copies(g_refs, gp_ref, got_refs, gotp_ref, *sems):
                    cp.wait_recv()
                for cp in _pair_copies(g_refs, gp_ref, got_refs, gotp_ref, *sems):
                    cp.wait_send()

        @pl.when(s == 0)
        def _():
            o_ref[...] = jnp.zeros_like(o_ref)
            if colsum:
                cs_ref[...] = jnp.zeros_like(cs_ref)
        o_ref[...] += _tn(a_ref[...], b_ref[...])
        if colsum:
            cs_ref[...] += jnp.sum(b_ref[...].astype(F32), axis=0, keepdims=True)

    in_specs = [pl.BlockSpec((tt, m), lambda j, s: (s, 0)), pl.BlockSpec((tt, tn), lambda j, s: (s, j))]
    out_specs = [pl.BlockSpec((m, tn), lambda j, s: (0, j))]
    out_shape = [jax.ShapeDtypeStruct((m, n), F32)]
    if colsum:
        out_specs.append(pl.BlockSpec((1, tn), lambda j, s: (0, j)))
        out_shape.append(jax.ShapeDtypeStruct((1, n), F32))
    args, scratch = [a, b], []
    if swap is not None:
        gs, gp = swap
        any_spec = pl.BlockSpec(memory_space=pl.ANY)
        in_specs += [any_spec] * (n_sw + 1)
        out_specs += [any_spec] * (n_sw + 1)
        out_shape += _pair_got_shapes(gs, gp)
        args += [*gs, gp]
        scratch = [pltpu.SemaphoreType.DMA((_pair_count(gs, gp),)), pltpu.SemaphoreType.DMA((_pair_count(gs, gp),))]
    return _pallas(
        body, name=name, grid=(nj, ns), in_specs=in_specs, out_specs=out_specs, out_shape=out_shape,
        scratch_shapes=scratch,
        compiler_params=_params(("parallel" if swap is None else "arbitrary", "arbitrary")),
    )(*args)


def dgrad_norm_bwd(dps, wts, paw, pap, x2, ctx2, dh, scale1, norm_g, *, tm):
    t, tc = x2.shape[0], ctx2.shape[0]
    t_all = t + tc
    n_lat, n_ctx = t // tm, tc // tm
    n_tiles = n_lat + n_ctx
    n_samples = scale1.shape[0] - 1
    tps = n_lat // n_samples
    n_grp = n_samples + 1
    n_g = len(dps)
    whole = [g for g in range(n_g) if dps[g].shape[0] == t_all]
    latent = [g for g in range(n_g) if dps[g].shape[0] != t_all]

    def body(*refs):
        dp_refs, w_refs = refs[:n_g], refs[n_g:2 * n_g]
        (paw_ref, pap_ref, x_ref, c_ref, dh_ref, sc_ref, g_ref, dx_ref, dsh_ref, dsc_ref, dg_ref, rbw_ref, rbp_ref,
         du_buf, b_send, b_recv) = refs[2 * n_g:]
        i = pl.program_id(0)

        def exchange():
            x, y, c, chips = _place()
            srcs, dsts = (paw_ref, pap_ref), (rbw_ref, rbp_ref)
            n_rows = (2 * paw_ref.shape[1], 2 * pap_ref.shape[1])
            return [_remote(srcs[a].at[2 * cx + cy, rows], dsts[a].at[j, rows],
                            b_send.at[j * N_BULK + k], b_recv.at[j * N_BULK + k], (cx, cy, c))
                    for j, (cx, cy) in enumerate(chips) for k, (a, rows) in enumerate(_half_chunks(0, n_rows, 16))]

        @pl.when(i == 0)
        def _():
            for cp in exchange():
                cp.start()

        acc = None
        for g in whole:
            part = _nt(dp_refs[g][...], w_refs[g][...])
            acc = part if acc is None else acc + part
        du_buf[...] = acc

        @pl.when(i < n_lat)
        def _():
            lat = None
            for g in latent:
                part = _nt(dp_refs[g][...], w_refs[g][...])
                lat = part if lat is None else lat + part
            du_buf[...] += lat

        duv = du_buf[...]
        xv = jnp.where(i < n_lat, x_ref[...], c_ref[...])
        rs = lax.rsqrt(jnp.mean(xv * xv, axis=-1, keepdims=True) + EPS)
        xh = xv * rs
        n = xh * g_ref[...]
        dn = duv * sc_ref[0]
        dxh = dn * g_ref[...]
        dx = rs * (dxh - xh * jnp.mean(dxh * xh, axis=-1, keepdims=True))

        @pl.when(i < n_lat)
        def _():
            dx_ref[...] = dx + dh_ref[...]

        @pl.when((i % tps == 0) & (i <= n_lat))
        def _():
            dsh_ref[...] = jnp.zeros_like(dsh_ref)
            dsc_ref[...] = jnp.zeros_like(dsc_ref)

        @pl.when(i == 0)
        def _():
            dg_ref[...] = jnp.zeros_like(dg_ref)

        dsh_ref[0] += jnp.sum(duv, axis=0, keepdims=True)
        dsc_ref[0] += jnp.sum(duv * n, axis=0, keepdims=True)
        dg_ref[...] += jnp.sum(dn * xh, axis=0, keepdims=True)

        @pl.when(i == n_tiles - 1)
        def _():
            for cp in exchange():
                cp.wait_recv()
            for cp in exchange():
                cp.wait_send()

    lat = lambda i: (jnp.minimum(i, n_lat - 1), 0)
    grp = lambda i: (jnp.minimum(i // tps, n_samples), 0, 0)
    in_specs = []
    for g, dp in enumerate(dps):
        nrow = dp.shape[0] // tm
        in_specs.append(pl.BlockSpec((tm, dp.shape[1]), lambda i, nrow=nrow: (jnp.minimum(i, nrow - 1), 0)))
    for w in wts:
        in_specs.append(pl.BlockSpec(w.shape, lambda i: (0, 0), pipeline_mode=pl.Buffered(1)))
    any_spec = pl.BlockSpec(memory_space=pl.ANY)
    in_specs += [any_spec, any_spec,
                 pl.BlockSpec((tm, D), lat), pl.BlockSpec((tm, D), lambda i: (jnp.maximum(i - n_lat, 0), 0)),
                 pl.BlockSpec((tm, D), lat), pl.BlockSpec((1, 1, D), grp), pl.BlockSpec((1, D), lambda i: (0, 0))]
    return _pallas(
        body, name="dgrad_norm_bwd", grid=(n_tiles,), in_specs=in_specs,
        out_specs=[pl.BlockSpec((tm, D), lat), pl.BlockSpec((1, 1, D), grp), pl.BlockSpec((1, 1, D), grp),
                   pl.BlockSpec((1, D), lambda i: (0, 0)), any_spec, any_spec],
        out_shape=[jax.ShapeDtypeStruct((t, D), F32), jax.ShapeDtypeStruct((n_grp, 1, D), F32),
                   jax.ShapeDtypeStruct((n_grp, 1, D), F32), jax.ShapeDtypeStruct((1, D), F32),
                   jax.ShapeDtypeStruct((3,) + paw.shape[1:], paw.dtype),
                   jax.ShapeDtypeStruct((3,) + pap.shape[1:], pap.dtype)],
        scratch_shapes=[pltpu.VMEM((tm, D), F32), pltpu.SemaphoreType.DMA((3 * N_BULK,)),
                        pltpu.SemaphoreType.DMA((3 * N_BULK,))],
        compiler_params=_params(("arbitrary",)),
    )(*dps, *wts, paw, pap, x2, ctx2, dh, scale1, norm_g)


TM_NORM = 512


def norm_mod_fwd(x2, ctx2, scale1, shift, norm_g):
    t = x2.shape[0]
    n_lat = t // TM_NORM
    assert ctx2.shape[0] == TM_NORM
    n_samples = scale1.shape[0] - 1
    tps = n_lat // n_samples

    def body(x_ref, c_ref, sc_ref, sh_ref, g_ref, u_ref):
        i = pl.program_id(0)
        xv = jnp.where(i < n_lat, x_ref[...], c_ref[...])
        rs = lax.rsqrt(jnp.mean(xv * xv, axis=-1, keepdims=True) + EPS)
        u = xv * rs * g_ref[...] * sc_ref[0] + sh_ref[0]
        u_ref[...] = u.astype(u_ref.dtype)

    grp = lambda i: (jnp.minimum(i // tps, n_samples), 0, 0)
    return _pallas(
        body, name="norm_mod_fwd", grid=(n_lat + 1,),
        in_specs=[pl.BlockSpec((TM_NORM, D), lambda i: (jnp.minimum(i, n_lat - 1), 0)),
                  pl.BlockSpec((TM_NORM, D), lambda i: (0, 0)),
                  pl.BlockSpec((1, 1, D), grp), pl.BlockSpec((1, 1, D), grp),
                  pl.BlockSpec((1, D), lambda i: (0, 0))],
        out_specs=pl.BlockSpec((TM_NORM, D), lambda i: (i, 0)),
        out_shape=jax.ShapeDtypeStruct((t + TM_NORM, D), BF16),
        compiler_params=_params(("parallel",)),
    )(x2, ctx2, scale1, shift, norm_g)


CONV_CB = 256
CONV_NCB = D // CONV_CB
H_OFF = 16


H_CB = 128
H_SPAN = GRID_W + 2 * H_OFF - 8


def _conv_scratch(vertical):
    if vertical:
        return [pltpu.VMEM((GRID_H + 2 * CONV_PAD, GRID_W, CONV_CB), F32)]
    return [pltpu.VMEM((GRID_H, GRID_W + 2 * H_OFF, H_CB), F32), pltpu.VMEM((7, GRID_H, H_SPAN, H_CB), F32)]


def _conv_fill(bufs, img, vertical):
    pad_ref = bufs[0]
    pad_ref[...] = jnp.zeros_like(pad_ref)
    if vertical:
        pad_ref[pl.ds(CONV_PAD, GRID_H)] = img
        return
    pad_ref[:, pl.ds(H_OFF, GRID_W), :] = img

    def shift(r, carry):
        for s in range(1, 8):
            bufs[1][s - 1, r] = pad_ref[r, pl.ds(s, H_SPAN), :]
        return carry

    lax.fori_loop(0, GRID_H, shift, 0)


def _conv_window(bufs, k, vertical, r, w0=0, nw=GRID_W, lanes=slice(None)):
    if vertical:
        return bufs[0][r + k, pl.ds(w0, nw), lanes]
    off = H_OFF - CONV_PAD + k
    if off % 8 == 0:
        return bufs[0][r, pl.ds(off + w0, nw), lanes]
    return bufs[1][off % 8 - 1, r, pl.ds(off - off % 8 + w0, nw), lanes]


def _conv_col_blocks(vertical):
    if vertical:
        return [pl.ds(0, CONV_CB)]
    return [pl.ds(i * H_CB, H_CB) for i in range(CONV_CB // H_CB)]


def _rows(r):
    return pl.ds(pl.multiple_of(r * GRID_W, GRID_W), GRID_W)


def conv_fwd(p1, conv_w, conv_b, n_samples):
    t = n_samples * SEQ

    def make(vertical, prev):
        n_buf = len(_conv_scratch(vertical))

        def body(gv_ref, gg_ref, w_ref, b_ref, *rest):
            o_ref, bufs = rest[-1 - n_buf], rest[-n_buf:]
            for cols in _conv_col_blocks(vertical):
                a = gv_ref[:, cols].astype(F32) * _sigmoid(gg_ref[:, cols].astype(F32))
                _conv_fill(bufs, a.reshape(GRID_H, GRID_W, a.shape[-1]), vertical)

                def row(r, carry, cols=cols):
                    acc = jnp.zeros((GRID_W, cols.size), F32) + b_ref[:, cols]
                    for k in range(CONV_K):
                        acc = acc + _conv_window(bufs, k, vertical, r) * w_ref[pl.ds(k, 1), cols]
                    o_ref[_rows(r), cols] = acc
                    return carry

                lax.fori_loop(0, GRID_H, row, 0)

        cb0 = CONV_NCB // 2 if vertical else 0
        in_specs = [pl.BlockSpec((SEQ, CONV_CB), lambda b, j: (b, 2 * (cb0 + j))),
                    pl.BlockSpec((SEQ, CONV_CB), lambda b, j: (b, 2 * (cb0 + j) + 1)),
                    pl.BlockSpec((CONV_K + 1, CONV_CB), lambda b, j: (0, cb0 + j)),
                    pl.BlockSpec((1, CONV_CB), lambda b, j: (0, cb0 + j))]
        args = [p1, p1, conv_w, conv_b]
        aliases = {}
        if prev is not None:
            in_specs.append(pl.BlockSpec(memory_space=pl.ANY))
            args.append(prev)
            aliases = {4: 0}
        return _pallas(
            body, name="conv_fwd_v" if vertical else "conv_fwd_h", grid=(n_samples, CONV_NCB // 2),
            in_specs=in_specs,
            out_specs=pl.BlockSpec((SEQ, CONV_CB), lambda b, j: (b, cb0 + j)),
            out_shape=jax.ShapeDtypeStruct((t, D), F32),
            scratch_shapes=_conv_scratch(vertical),
            input_output_aliases=aliases,
            compiler_params=_params(("parallel", "parallel")),
        )(*args)

    return make(True, make(False, None))


def conv_bwd(p1, daconv, conv_w, n_samples):
    t = n_samples * SEQ

    def make(vertical, prev):
        n_buf = len(_conv_scratch(vertical))

        def body(gv_ref, gg_ref, dy_ref, w_ref, *rest):
            dp_ref, dw_ref, db_ref = rest[-3 - 2 * n_buf - 1:-2 * n_buf - 1]
            a_bufs, d_bufs, da_ref = rest[-2 * n_buf - 1:-n_buf - 1], rest[-n_buf - 1:-1], rest[-1]
            for cols in _conv_col_blocks(vertical):
                width = cols.size
                gv = gv_ref[:, cols].astype(F32)
                sg = _sigmoid(gg_ref[:, cols].astype(F32))
                _conv_fill(a_bufs, (gv * sg).reshape(GRID_H, GRID_W, width), vertical)
                _conv_fill(d_bufs, dy_ref[:, cols].reshape(GRID_H, GRID_W, width), vertical)

                def row(r, carry, cols=cols, width=width):
                    acc = jnp.zeros((GRID_W, width), F32)
                    for k in range(CONV_K):
                        acc = acc + _conv_window(d_bufs, CONV_K - 1 - k, vertical, r) * w_ref[pl.ds(k, 1), cols]
                    da_ref[_rows(r), cols] = acc
                    return carry

                lax.fori_loop(0, GRID_H, row, 0)
                da = da_ref[:, cols]
                dp_ref[:, pl.ds(cols.start, width)] = (da * sg).astype(dp_ref.dtype)
                dp_ref[:, pl.ds(CONV_CB + cols.start, width)] = (da * gv * sg * (1.0 - sg)).astype(dp_ref.dtype)

                for lb in range(width // 128):
                    lanes = pl.ds(lb * 128, 128)
                    dy_lanes = pl.ds(cols.start + lb * 128, 128)

                    def wrow(r, accs, lanes=lanes, dy_lanes=dy_lanes):
                        for w0 in range(0, GRID_W, 8):
                            dyv = dy_ref[pl.ds(pl.multiple_of(r * GRID_W, GRID_W) + w0, 8), dy_lanes]
                            accs = tuple(accs[k] + _conv_window(a_bufs, k, vertical, r, w0, 8, lanes) * dyv
                                         for k in range(CONV_K))
                        return accs

                    accs = lax.fori_loop(0, GRID_H, wrow, tuple(jnp.zeros((8, 128), F32) for _ in range(CONV_K)))
                    for k in range(CONV_K):
                        dw_ref[0, pl.ds(k, 1), dy_lanes] = jnp.sum(accs[k], axis=0, keepdims=True)
            dw_ref[0, pl.ds(CONV_K, 1), :] = jnp.zeros((1, CONV_CB), F32)
            db_ref[0] = jnp.sum(dy_ref[...], axis=0, keepdims=True)

        cb0 = CONV_NCB // 2 if vertical else 0
        in_specs = [pl.BlockSpec((SEQ, CONV_CB), lambda b, j: (b, 2 * (cb0 + j))),
                    pl.BlockSpec((SEQ, CONV_CB), lambda b, j: (b, 2 * (cb0 + j) + 1)),
                    pl.BlockSpec((SEQ, CONV_CB), lambda b, j: (b, cb0 + j)),
                    pl.BlockSpec((CONV_K + 1, CONV_CB), lambda b, j: (0, cb0 + j))]
        args = [p1, p1, daconv, conv_w]
        aliases = {}
        if prev is not None:
            in_specs += [pl.BlockSpec(memory_space=pl.ANY)] * 3
            args += list(prev)
            aliases = {4: 0, 5: 1, 6: 2}
        return _pallas(
            body, name="conv_bwd_v" if vertical else "conv_bwd_h", grid=(n_samples, CONV_NCB // 2),
            in_specs=in_specs,
            out_specs=[pl.BlockSpec((SEQ, 2 * CONV_CB), lambda b, j: (b, cb0 + j)),
                       pl.BlockSpec((1, CONV_K + 1, CONV_CB), lambda b, j: (b, 0, cb0 + j)),
                       pl.BlockSpec((1, 1, CONV_CB), lambda b, j: (b, 0, cb0 + j))],
            out_shape=[jax.ShapeDtypeStruct((t, 2 * D), BF16),
                       jax.ShapeDtypeStruct((n_samples, CONV_K + 1, D), F32),
                       jax.ShapeDtypeStruct((n_samples, 1, D), F32)],
            scratch_shapes=_conv_scratch(vertical) + _conv_scratch(vertical) + [pltpu.VMEM((SEQ, CONV_CB), F32)],
            input_output_aliases=aliases,
            compiler_params=_params(("parallel", "parallel")),
        )(*args)

    return make(True, make(False, None))


TM_EW = 256


def ln_gate_fwd(aconv, z, ln_g, ln_b):
    t = aconv.shape[0]

    def body(a_ref, z_ref, g_ref, b_ref, o_ref):
        a = a_ref[...]
        mu = jnp.mean(a, axis=-1, keepdims=True)
        xc = a - mu
        rstd = lax.rsqrt(jnp.mean(xc * xc, axis=-1, keepdims=True) + EPS)
        l = xc * rstd * g_ref[...] + b_ref[...]
        o_ref[...] = (_silu(l) * _silu(z_ref[...].astype(F32))).astype(o_ref.dtype)

    row = pl.BlockSpec((TM_EW, D), lambda i: (i, 0))
    vec = pl.BlockSpec((1, D), lambda i: (0, 0))
    return _pallas(
        body, name="ln_gate_fwd", grid=(t // TM_EW,), in_specs=[row, row, vec, vec], out_specs=row,
        out_shape=jax.ShapeDtypeStruct((t, D), BF16), compiler_params=_params(("parallel",)),
    )(aconv, z, ln_g, ln_b)


def ln_gate_bwd(aconv, z, dyc, conv_proj, ln_g, ln_b):
    t = aconv.shape[0]

    def body(a_ref, z_ref, d_ref, w_ref, g_ref, b_ref, da_ref, dz_ref, dg_ref, db_ref):
        a = a_ref[...]
        zv = z_ref[...].astype(F32)
        dac_v = _nt(d_ref[...], w_ref[...])
        mu = jnp.mean(a, axis=-1, keepdims=True)
        xc = a - mu
        rstd = lax.rsqrt(jnp.mean(xc * xc, axis=-1, keepdims=True) + EPS)
        xh = xc * rstd
        l = xh * g_ref[...] + b_ref[...]
        dz_ref[...] = (dac_v * _silu(l) * _dsilu(zv)).astype(dz_ref.dtype)
        dl = dac_v * _silu(zv) * _dsilu(l)
        dxh = dl * g_ref[...]
        da_ref[...] = rstd * (dxh - jnp.mean(dxh, axis=-1, keepdims=True)
                              - xh * jnp.mean(dxh * xh, axis=-1, keepdims=True))

        @pl.when(pl.program_id(0) == 0)
        def _():
            dg_ref[...] = jnp.zeros_like(dg_ref)
            db_ref[...] = jnp.zeros_like(db_ref)

        dg_ref[...] += jnp.sum(dl * xh, axis=0, keepdims=True)
        db_ref[...] += jnp.sum(dl, axis=0, keepdims=True)

    row = pl.BlockSpec((TM_EW, D), lambda i: (i, 0))
    vec = pl.BlockSpec((1, D), lambda i: (0, 0))
    return _pallas(
        body, name="ln_gate_bwd", grid=(t // TM_EW,),
        in_specs=[row, row, row, pl.BlockSpec((D, D), lambda i: (0, 0)), vec, vec],
        out_specs=[row, row, vec, vec],
        out_shape=[jax.ShapeDtypeStruct((t, D), F32), jax.ShapeDtypeStruct((t, D), BF16),
                   jax.ShapeDtypeStruct((1, D), F32), jax.ShapeDtypeStruct((1, D), F32)],
        compiler_params=_params(("arbitrary",)),
    )(aconv, z, dyc, conv_proj, ln_g, ln_b)


TM_PREP = 256
PREP_LAT = SEQ // TM_PREP
PREP_ALL = SEQ_ALL // TM_PREP


def _chunk_tri(n, upper):
    r = lax.broadcasted_iota(jnp.int32, (n, n), 0)
    c = lax.broadcasted_iota(jnp.int32, (n, n), 1)
    same = (r // CHUNK) == (c // CHUNK)
    keep = (c >= r) if upper else (c <= r)
    return jnp.where(same & keep, 1.0, 0.0).astype(F32)


def _split3(v):
    hi = v.astype(BF16)
    r1 = v - hi.astype(F32)
    mid = r1.astype(BF16)
    lo = (r1 - mid.astype(F32)).astype(BF16)
    return jnp.stack([hi, mid, lo])


def _chunk_sums(v, upper):
    tri = _chunk_tri(v.shape[0], upper).astype(BF16)
    pieces = _split3(v)
    return (_nn(tri, pieces[0]) + _nn(tri, pieces[1])) + _nn(tri, pieces[2])


def _gate_logits(ab, up3_ref, bias_ref):
    assert ab.dtype == BF16
    return ((_nn(ab, up3_ref[0]) + _nn(ab, up3_ref[1])) + _nn(ab, up3_ref[2])) + bias_ref[...]


def _prep_tile_maps(n_samples):
    n_lat = n_samples * PREP_LAT

    def seq_map(i):
        return jnp.where(i < n_lat, i // PREP_LAT, i - n_lat), jnp.where(i < n_lat, i % PREP_LAT, PREP_LAT)

    return n_lat, seq_map


def gla_prep_fwd(p3, upf, upb, bias_f, bias_b, n_samples):
    n_lat, seq_map = _prep_tile_maps(n_samples)
    n_tiles = n_lat + n_samples

    def body(v_ref, q_ref, k_ref, ab_ref, upf_ref, upb_ref, bf_ref, bb_ref, qo, ko, vo, cf, cb):
        i = pl.program_id(0)
        qo[0] = jnp.where(i < n_lat, q_ref[...].astype(F32) * Q_SCALE, 0.0)
        ko[0] = k_ref[...].astype(F32)
        vo[0] = v_ref[...].astype(F32)
        ab = ab_ref[...]
        gf = _log_sigmoid(_gate_logits(ab, upf_ref, bf_ref)) * (1.0 / GATE_TAU)
        gb = _log_sigmoid(_gate_logits(ab, upb_ref, bb_ref)) * (1.0 / GATE_TAU)
        cf[0] = _chunk_sums(gf, False)
        cb[0] = _chunk_sums(gb, True)

    def o_spec(w):
        return pl.BlockSpec((1, TM_PREP, w), lambda i: (*seq_map(i), 0))

    full = lambda shape: pl.BlockSpec(shape, lambda i: (0,) * len(shape))
    return _pallas(
        body, name="gla_prep_fwd", grid=(n_tiles,),
        in_specs=[pl.BlockSpec((TM_PREP, 1024), lambda i: (i, O3_V // 1024)),
                  pl.BlockSpec((TM_PREP, 512), lambda i: (i, O3_Q // 512)),
                  pl.BlockSpec((TM_PREP, 512), lambda i: (i, O3_K // 512)),
                  pl.BlockSpec((TM_PREP, 128), lambda i: (i, O3_AB // 128)),
                  full((3, 128, GLA_DK)), full((3, 128, GLA_DK)), full((1, GLA_DK)), full((1, GLA_DK))],
        out_specs=[o_spec(GLA_DK), o_spec(GLA_DK), o_spec(D), o_spec(GLA_DK), o_spec(GLA_DK)],
        out_shape=[jax.ShapeDtypeStruct((n_samples, SEQ_ALL, GLA_DK), F32),
                   jax.ShapeDtypeStruct((n_samples, SEQ_ALL, GLA_DK), F32),
                   jax.ShapeDtypeStruct((n_samples, SEQ_ALL, D), F32),
                   jax.ShapeDtypeStruct((n_samples, SEQ_ALL, GLA_DK), F32),
                   jax.ShapeDtypeStruct((n_samples, SEQ_ALL, GLA_DK), F32)],
        compiler_params=_params(("parallel",)),
    )(p3, p3, p3, p3, upf, upb, bias_f, bias_b)


def gla_prep_bwd(p3, dq_f, dq_b, dk_f, dk_b, dv_f, dv_b, dc_f, dc_b, upf, upb, bias_f, bias_b, n_samples):
    n_lat, seq_map = _prep_tile_maps(n_samples)
    n_tiles = n_lat + n_samples

    def body(ab_ref, dqf, dqb, dkf, dkb, dvf, dvb, dcf, dcb, upf_ref, upb_ref, bf_ref, bb_ref,
             dp_ref, duf_ref, dub_ref, dbf_ref, dbb_ref):
        i = pl.program_id(0)
        dp_ref[:, pl.ds(O3_V, D)] = (dvf[0] + dvb[0]).astype(dp_ref.dtype)
        dq = jnp.where(i < n_lat, (dqf[0] + dqb[0]) * Q_SCALE, 0.0)
        dp_ref[:, pl.ds(O3_Q, GLA_DK)] = dq.astype(dp_ref.dtype)
        dp_ref[:, pl.ds(O3_K, GLA_DK)] = (dkf[0] + dkb[0]).astype(dp_ref.dtype)
        ab = ab_ref[...]
        zf = _gate_logits(ab, upf_ref, bf_ref)
        zb = _gate_logits(ab, upb_ref, bb_ref)
        dgf = _chunk_sums(dcf[0], True)
        dgb = _chunk_sums(dcb[0], False)
        dzf = _b16(dgf * (1.0 / GATE_TAU) * _sigmoid(-zf))
        dzb = _b16(dgb * (1.0 / GATE_TAU) * _sigmoid(-zb))
        dab = _nt(dzf, upf_ref[0]) + _nt(dzb, upb_ref[0])
        dp_ref[:, pl.ds(O3_AB, 128)] = dab.astype(dp_ref.dtype)

        @pl.when(i == 0)
        def _():
            duf_ref[...] = jnp.zeros_like(duf_ref)
            dub_ref[...] = jnp.zeros_like(dub_ref)
            dbf_ref[...] = jnp.zeros_like(dbf_ref)
            dbb_ref[...] = jnp.zeros_like(dbb_ref)

        duf_ref[...] += _tn(ab, dzf)
        dub_ref[...] += _tn(ab, dzb)
        dbf_ref[...] += jnp.sum(dzf.astype(F32), axis=0, keepdims=True)
        dbb_ref[...] += jnp.sum(dzb.astype(F32), axis=0, keepdims=True)

    def s_spec(w):
        return pl.BlockSpec((1, TM_PREP, w), lambda i: (*seq_map(i), 0))

    full = lambda shape: pl.BlockSpec(shape, lambda i: (0,) * len(shape))
    return _pallas(
        body, name="gla_prep_bwd", grid=(n_tiles,),
        in_specs=[pl.BlockSpec((TM_PREP, 128), lambda i: (i, O3_AB // 128)),
                  s_spec(GLA_DK), s_spec(GLA_DK), s_spec(GLA_DK), s_spec(GLA_DK), s_spec(D), s_spec(D),
                  s_spec(GLA_DK), s_spec(GLA_DK),
                  full((3, 128, GLA_DK)), full((3, 128, GLA_DK)), full((1, GLA_DK)), full((1, GLA_DK))],
        out_specs=[pl.BlockSpec((TM_PREP, W3), lambda i: (i, 0)),
                   full((128, GLA_DK)), full((128, GLA_DK)), full((1, GLA_DK)), full((1, GLA_DK))],
        out_shape=[jax.ShapeDtypeStruct((n_tiles * TM_PREP, W3), BF16),
                   jax.ShapeDtypeStruct((128, GLA_DK), F32), jax.ShapeDtypeStruct((128, GLA_DK), F32),
                   jax.ShapeDtypeStruct((1, GLA_DK), F32), jax.ShapeDtypeStruct((1, GLA_DK), F32)],
        compiler_params=_params(("arbitrary",)),
    )(p3, dq_f, dq_b, dk_f, dk_b, dv_f, dv_b, dc_f, dc_b, upf, upb, bias_f, bias_b)


def _sub_blocks(rev):
    if NSUB == 1:
        return [((0, CHUNK), CHUNK // 2, (0, CHUNK))]
    out = []
    for s in range(NSUB):
        rows = (s * SUB, SUB)
        if rev:
            ref = (s + 1) * SUB if s < NSUB - 1 else None
            cols = (s * SUB, CHUNK - s * SUB)
        else:
            ref = s * SUB - 1 if s > 0 else None
            cols = (0, (s + 1) * SUB)
        out.append((rows, ref, cols))
    return out


def _sub_mask(rows, cols, rev):
    r = rows[0] + lax.broadcasted_iota(jnp.int32, (rows[1], cols[1]), 0)
    c = cols[0] + lax.broadcasted_iota(jnp.int32, (rows[1], cols[1]), 1)
    return (c >= r) if rev else (c <= r)


def _sub_operands(qc, kc, cc, rows, ref, cols):
    cref = jnp.zeros((1, HEAD_K), F32) if ref is None else cc[ref:ref + 1]
    eq = jnp.exp(cc[rows[0]:rows[0] + rows[1]] - cref)
    ek = jnp.exp(cref - cc[cols[0]:cols[0] + cols[1]])
    qs = qc[rows[0]:rows[0] + rows[1]] * eq
    kk = kc[cols[0]:cols[0] + cols[1]] * ek
    return qs, kk, eq, ek


SCAN_ROWS = 256
SCAN_CHUNKS = SCAN_ROWS // CHUNK
SCAN_STEPS = SEQ_ALL // SCAN_ROWS
LAT_BLOCKS = SEQ // SCAN_ROWS


def _scan_block(t, rev):
    if rev:
        return SCAN_STEPS - 1 - t
    return jnp.where(t == 0, SCAN_STEPS - 1, t - 1)


def _scan_lat_block(t, rev):
    if rev:
        return jnp.minimum(SCAN_STEPS - 1 - t, LAT_BLOCKS - 1)
    return jnp.maximum(t - 1, 0)


def _head_cols(h):
    return pl.ds(h * HEAD_K, HEAD_K), pl.ds(h * HEAD_V, HEAD_V)


def gla_scan_fwd(q, k, v, cum, *, rev, name):
    n = q.shape[0]

    def body(q_ref, k_ref, v_ref, c_ref, o_ref, s_ref, sfin_ref, st):
        t = pl.program_id(1)

        @pl.when(t == 0)
        def _():
            st[...] = jnp.zeros_like(st)

        def chunk(j, carry):
            lj = SCAN_CHUNKS - 1 - j if rev else j
            r0 = lj * CHUNK
            rws = pl.ds(r0, CHUNK)
            for h in range(HEADS):
                kcols, vcols = _head_cols(h)
                qc, kc, cc = q_ref[0, rws, kcols], k_ref[0, rws, kcols], c_ref[0, rws, kcols]
                vc = v_ref[0, rws, vcols]
                s_in = st[h]
                s_ref[0, h, j] = s_in
                edge = cc[0:1] if rev else cc[CHUNK - 1:CHUNK]
                ke = kc * jnp.exp(edge - cc)
                st[h] = s_in * jnp.exp(edge) + _tn(_b16(vc), _b16(ke))
                o_inter = _nt(_b16(qc * jnp.exp(cc)), _b16(s_in))
                vb = _b16(vc)
                for rows, ref, cols in _sub_blocks(rev):
                    qs, kk, _, _ = _sub_operands(qc, kc, cc, rows, ref, cols)
                    a = jnp.where(_sub_mask(rows, cols, rev), _nt(_b16(qs), _b16(kk)), 0.0)
                    o_s = _nn(_b16(a), vb[cols[0]:cols[0] + cols[1]])
                    o_ref[0, pl.ds(r0 + rows[0], rows[1]), vcols] = o_inter[rows[0]:rows[0] + rows[1]] + o_s
            return carry

        for j in range(SCAN_CHUNKS):
            chunk(j, 0)

        @pl.when(t == SCAN_STEPS - 1)
        def _():
            sfin_ref[0] = st[...]

    def spec(w):
        return pl.BlockSpec((1, SCAN_ROWS, w), lambda b, t: (b, _scan_block(t, rev), 0))

    return _pallas(
        body, name=name, grid=(n, SCAN_STEPS),
        in_specs=[spec(GLA_DK), spec(GLA_DK), spec(D), spec(GLA_DK)],
        out_specs=[pl.BlockSpec((1, SCAN_ROWS, D), lambda b, t: (b, _scan_lat_block(t, rev), 0)),
                   pl.BlockSpec((1, HEADS, SCAN_CHUNKS, HEAD_V, HEAD_K), lambda b, t: (b, 0, t, 0, 0)),
                   pl.BlockSpec((1, HEADS, HEAD_V, HEAD_K), lambda b, t: (b, 0, 0, 0))],
        out_shape=[jax.ShapeDtypeStruct((n, SEQ, D), F32),
                   jax.ShapeDtypeStruct((n, HEADS, NCHUNK, HEAD_V, HEAD_K), F32),
                   jax.ShapeDtypeStruct((n, HEADS, HEAD_V, HEAD_K), F32)],
        scratch_shapes=[pltpu.VMEM((HEADS, HEAD_V, HEAD_K), F32)],
        compiler_params=_params(("parallel", "arbitrary")),
    )(q, k, v, cum)


def gla_scan_bwd(q, k, v, cum, s_all, s_fin, do, *, rev, name):
    n = q.shape[0]

    def body(q_ref, k_ref, v_ref, c_ref, s_ref, sfin_ref, do_ref, dq_ref, dk_ref, dv_ref, dc_ref,
             dst, s_next, dq_acc, dk_acc, dv_acc):
        t = SCAN_STEPS - 1 - pl.program_id(1)

        @pl.when(pl.program_id(1) == 0)
        def _():
            dst[...] = jnp.zeros_like(dst)
            s_next[...] = sfin_ref[0]

        def chunk(jj, carry):
            j = SCAN_CHUNKS - 1 - jj
            lj = SCAN_CHUNKS - 1 - j if rev else j
            rws = pl.ds(lj * CHUNK, CHUNK)
            for h in range(HEADS):
                kcols, vcols = _head_cols(h)
                qc, kc, cc = q_ref[0, rws, kcols], k_ref[0, rws, kcols], c_ref[0, rws, kcols]
                vc = v_ref[0, rws, vcols]
                doc = jnp.where(t > 0, do_ref[0, rws, vcols], 0.0)
                s_in = s_ref[0, h, j]
                s_out = s_next[h]
                ds_out = dst[h]
                edge = cc[0:1] if rev else cc[CHUNK - 1:CHUNK]
                e_q = jnp.exp(cc)
                e_k = jnp.exp(edge - cc)
                dob = _b16(doc)
                dsb = _b16(ds_out)
                dst[h] = ds_out * jnp.exp(edge) + _tn(dob, _b16(qc * e_q))
                s_next[h] = s_in
                dq_acc[h] = e_q * _nn(dob, _b16(s_in))
                dk_acc[h] = e_k * _nn(_b16(vc), dsb)
                dv_acc[h] = _nt(_b16(kc * e_k), dsb)
                vb = _b16(vc)
                for rows, ref, cols in _sub_blocks(rev):
                    qs, kk, eq, ek = _sub_operands(qc, kc, cc, rows, ref, cols)
                    mask = _sub_mask(rows, cols, rev)
                    rsl = slice(rows[0], rows[0] + rows[1])
                    csl = pl.ds(cols[0], cols[1])
                    qsb, kkb = _b16(qs), _b16(kk)
                    a = jnp.where(mask, _nt(qsb, kkb), 0.0)
                    da = _b16(jnp.where(mask, _nt(dob[rsl], vb[cols[0]:cols[0] + cols[1]]), 0.0))
                    dq_acc[h, pl.ds(rows[0], rows[1]), :] += _nn(da, kkb) * eq
                    dk_acc[h, csl, :] += _tn(da, qsb) * ek
                    dv_acc[h, csl, :] += _tn(_b16(a), dob[rsl])
                dq = dq_acc[h]
                dk = dk_acc[h]
                dc = qc * dq - kc * dk
                bnd = jnp.sum(ds_out * s_out, axis=0, keepdims=True)
                edge_row = 0 if rev else CHUNK - 1
                is_edge = lax.broadcasted_iota(jnp.int32, (CHUNK, HEAD_K), 0) == edge_row
                dq_ref[0, rws, kcols] = dq
                dk_ref[0, rws, kcols] = dk
                dv_ref[0, rws, vcols] = dv_acc[h]
                dc_ref[0, rws, kcols] = dc + jnp.where(is_edge, bnd, 0.0)
            return carry

        for jj in range(SCAN_CHUNKS):
            chunk(jj, 0)

    def step_of(u):
        return SCAN_STEPS - 1 - u

    def spec(w):
        return pl.BlockSpec((1, SCAN_ROWS, w), lambda b, u: (b, _scan_block(step_of(u), rev), 0))

    return _pallas(
        body, name=name, grid=(n, SCAN_STEPS),
        in_specs=[spec(GLA_DK), spec(GLA_DK), spec(D), spec(GLA_DK),
                  pl.BlockSpec((1, HEADS, SCAN_CHUNKS, HEAD_V, HEAD_K), lambda b, u: (b, 0, step_of(u), 0, 0)),
                  pl.BlockSpec((1, HEADS, HEAD_V, HEAD_K), lambda b, u: (b, 0, 0, 0)),
                  pl.BlockSpec((1, SCAN_ROWS, D), lambda b, u: (b, _scan_lat_block(step_of(u), rev), 0))],
        out_specs=[spec(GLA_DK), spec(GLA_DK), spec(D), spec(GLA_DK)],
        out_shape=[jax.ShapeDtypeStruct((n, SEQ_ALL, GLA_DK), F32), jax.ShapeDtypeStruct((n, SEQ_ALL, GLA_DK), F32),
                   jax.ShapeDtypeStruct((n, SEQ_ALL, D), F32), jax.ShapeDtypeStruct((n, SEQ_ALL, GLA_DK), F32)],
        scratch_shapes=[pltpu.VMEM((HEADS, HEAD_V, HEAD_K), F32), pltpu.VMEM((HEADS, HEAD_V, HEAD_K), F32),
                        pltpu.VMEM((HEADS, CHUNK, HEAD_K), F32), pltpu.VMEM((HEADS, CHUNK, HEAD_K), F32),
                        pltpu.VMEM((HEADS, CHUNK, HEAD_V), F32)],
        compiler_params=_params(("parallel", "arbitrary")),
    )(q, k, v, cum, s_all, s_fin, do)


def gla_out_fwd(o_f, o_b, r, gnorm):
    n = o_f.shape[0]
    tiles = SEQ // TM_EW

    def body(of_ref, ob_ref, r_ref, g_ref, og_ref):
        for h in range(HEADS):
            cols = pl.ds(h * HEAD_V, HEAD_V)
            o = of_ref[0, :, cols] + ob_ref[0, :, cols]
            rs = lax.rsqrt(jnp.mean(o * o, axis=-1, keepdims=True) + EPS)
            og_ref[:, cols] = (o * rs * g_ref[...] * _silu(r_ref[:, cols].astype(F32))).astype(og_ref.dtype)

    ospec = pl.BlockSpec((1, TM_EW, D), lambda b, j: (b, j, 0))
    row = pl.BlockSpec((TM_EW, D), lambda b, j: (b * tiles + j, 0))
    return _pallas(
        body, name="gla_out_fwd", grid=(n, tiles),
        in_specs=[ospec, ospec, row, pl.BlockSpec((1, HEAD_V), lambda b, j: (0, 0))],
        out_specs=row, out_shape=jax.ShapeDtypeStruct((n * SEQ, D), BF16),
        compiler_params=_params(("parallel", "parallel")),
    )(o_f, o_b, r, gnorm)


def gla_out_bwd(o_f, o_b, r, dyg, gla_proj, gnorm):
    n = o_f.shape[0]
    tiles = SEQ // TM_EW

    def body(of_ref, ob_ref, r_ref, d_ref, w_ref, g_ref, do_ref, dr_ref, dg_ref, dog_buf):
        @pl.when((pl.program_id(0) == 0) & (pl.program_id(1) == 0))
        def _():
            dg_ref[...] = jnp.zeros_like(dg_ref)

        dog_buf[...] = _nt(d_ref[...], w_ref[...])
        for h in range(HEADS):
            cols = pl.ds(h * HEAD_V, HEAD_V)
            o = of_ref[0, :, cols] + ob_ref[0, :, cols]
            rv = r_ref[:, cols].astype(F32)
            dv = dog_buf[:, cols]
            rs = lax.rsqrt(jnp.mean(o * o, axis=-1, keepdims=True) + EPS)
            oh = o * rs
            dr_ref[:, cols] = (dv * oh * g_ref[...] * _dsilu(rv)).astype(dr_ref.dtype)
            dn = dv * _silu(rv)
            dg_ref[...] += jnp.sum(dn * oh, axis=0, keepdims=True)
            doh = dn * g_ref[...]
            do_ref[0, :, cols] = rs * (doh - oh * jnp.mean(doh * oh, axis=-1, keepdims=True))

    ospec = pl.BlockSpec((1, TM_EW, D), lambda b, j: (b, j, 0))
    row = pl.BlockSpec((TM_EW, D), lambda b, j: (b * tiles + j, 0))
    vec = pl.BlockSpec((1, HEAD_V), lambda b, j: (0, 0))
    return _pallas(
        body, name="gla_out_bwd", grid=(n, tiles),
        in_specs=[ospec, ospec, row, row, pl.BlockSpec((D, D), lambda b, j: (0, 0)), vec],
        out_specs=[ospec, row, vec],
        out_shape=[jax.ShapeDtypeStruct((n, SEQ, D), F32), jax.ShapeDtypeStruct((n * SEQ, D), BF16),
                   jax.ShapeDtypeStruct((1, HEAD_V), F32)],
        scratch_shapes=[pltpu.VMEM((TM_EW, D), F32)],
        compiler_params=_params(("arbitrary", "arbitrary")),
    )(o_f, o_b, r, dyg, gla_proj, gnorm)


TM_OUT = 512


def merge_out_final(p5, y_conv, y_gla, w_out, x2, gate, final_g, target, n_samples):
    t = x2.shape[0]
    tiles = SEQ // TM_OUT

    def body(mc_ref, mg_ref, yc_ref, yg_ref, w_ref, x_ref, gate_ref, g_ref, t_ref,
             mrg_ref, dh_ref, dmo_ref, dgate_ref, dg_ref, loss_ref):
        b, j = pl.program_id(0), pl.program_id(1)
        f = lambda ref: ref[...].astype(F32)
        merged = _b16(_sigmoid(f(mc_ref)) * f(yc_ref) + _sigmoid(f(mg_ref)) * f(yg_ref))
        mrg_ref[...] = merged
        mo_v = _nn(merged, w_ref[...])
        h = x_ref[...] + gate_ref[0] * mo_v
        rs = lax.rsqrt(jnp.mean(h * h, axis=-1, keepdims=True) + EPS)
        nh = h * rs
        err = nh * g_ref[...] - t_ref[...]
        dy = err * (1.0 / D)
        dn = dy * g_ref[...]
        dh = rs * (dn - nh * jnp.mean(dn * nh, axis=-1, keepdims=True))
        dh_ref[...] = dh
        dmo_ref[...] = (dh * gate_ref[0]).astype(dmo_ref.dtype)

        @pl.when(j == 0)
        def _():
            dgate_ref[...] = jnp.zeros_like(dgate_ref)

        @pl.when((b == 0) & (j == 0))
        def _():
            dg_ref[...] = jnp.zeros_like(dg_ref)
            loss_ref[...] = jnp.zeros_like(loss_ref)

        dgate_ref[0] += jnp.sum(dh * mo_v, axis=0, keepdims=True)
        dg_ref[...] += jnp.sum(dy * nh, axis=0, keepdims=True)
        loss_ref[...] += (0.5 / D) * jnp.sum(err * err)

    row = pl.BlockSpec((TM_OUT, D), lambda b, j: (b * tiles + j, 0))
    per = pl.BlockSpec((1, 1, D), lambda b, j: (b, 0, 0))
    vec = pl.BlockSpec((1, D), lambda b, j: (0, 0))
    return _pallas(
        body, name="merge_out_final", grid=(n_samples, tiles),
        in_specs=[row, pl.BlockSpec((TM_OUT, D), lambda b, j: (b * tiles + j, 1)), row, row,
                  pl.BlockSpec((D, D), lambda b, j: (0, 0)), row, per, vec, row],
        out_specs=[row, row, row, per, vec, pl.BlockSpec((8, 128), lambda b, j: (0, 0))],
        out_shape=[jax.ShapeDtypeStruct((t, D), BF16), jax.ShapeDtypeStruct((t, D), F32), jax.ShapeDtypeStruct((t, D), BF16),
                   jax.ShapeDtypeStruct((n_samples, 1, D), F32), jax.ShapeDtypeStruct((1, D), F32),
                   jax.ShapeDtypeStruct((8, 128), F32)],
        compiler_params=_params(("arbitrary", "arbitrary")),
    )(p5, p5, y_conv, y_gla, w_out, x2, gate, final_g, target)


def out_dgrad_merge_bwd(p5, y_conv, y_gla, dmo, w_out):
    t = y_conv.shape[0]

    def body(mc_ref, mg_ref, yc_ref, yg_ref, d_ref, w_ref, dyc_ref, dyg_ref, dp_ref):
        f = lambda ref: ref[...].astype(F32)
        d = _nt(d_ref[...], w_ref[...])
        sc = _sigmoid(f(mc_ref))
        sg = _sigmoid(f(mg_ref))
        dyc_ref[...] = (d * sc).astype(dyc_ref.dtype)
        dyg_ref[...] = (d * sg).astype(dyg_ref.dtype)
        dp_ref[:, pl.ds(0, D)] = (d * f(yc_ref) * sc * (1.0 - sc)).astype(dp_ref.dtype)
        dp_ref[:, pl.ds(D, D)] = (d * f(yg_ref) * sg * (1.0 - sg)).astype(dp_ref.dtype)

    row = pl.BlockSpec((TM_OUT, D), lambda i: (i, 0))
    return _pallas(
        body, name="out_dgrad_merge_bwd", grid=(t // TM_OUT,),
        in_specs=[row, pl.BlockSpec((TM_OUT, D), lambda i: (i, 1)), row, row, row, pl.BlockSpec((D, D), lambda i: (0, 0))],
        out_specs=[row, row, pl.BlockSpec((TM_OUT, 2 * D), lambda i: (i, 0))],
        out_shape=[jax.ShapeDtypeStruct((t, D), BF16), jax.ShapeDtypeStruct((t, D), BF16),
                   jax.ShapeDtypeStruct((t, 2 * D), BF16)],
        compiler_params=_params(("parallel",)),
    )(p5, p5, y_conv, y_gla, dmo, w_out)


def local_step(x, ctx, target, mod, wts, small):
    n = x.shape[0]
    t = n * SEQ
    t_all = t + n * NCTX
    x2 = x.reshape(t, D)
    ctx2 = ctx.reshape(n * NCTX, D)
    tgt2 = target.reshape(t, D)
    scale1, shift, gate = mod

    u = norm_mod_fwd(x2, ctx2, scale1, shift, small["norm_g"])
    p1 = matmul_nn(u, wts["w1"], small["b1"], name="proj_conv", m=t, tm=1024, tn=1024, out_dtype=BF16)
    p2 = matmul_nn(u, wts["w2"], small["b2"], name="proj_z", m=t, tm=1024, tn=1024, out_dtype=BF16)
    p3 = matmul_nn(u, wts["w3"], small["b3"], name="proj_gla", m=t_all, tm=512, tn=W3, out_dtype=BF16)
    p4 = matmul_nn(u, wts["w4"], small["b4"], name="proj_r", m=t, tm=1024, tn=1024, out_dtype=BF16)
    p5 = matmul_nn(u, wts["w5"], small["b5"], name="proj_merge", m=t, tm=1024, tn=1024, out_dtype=BF16)

    aconv = conv_fwd(p1, small["conv_w"], small["conv_b"], n)
    ac = ln_gate_fwd(aconv, p2, small["conv_ln_g"], small["conv_ln_b"])
    y_conv = matmul_nn(ac, wts["conv_proj"], None, name="conv_proj_fwd", m=t, tm=1024, tn=1024, out_dtype=BF16)

    qs, ks, vs, cum_f, cum_b = gla_prep_fwd(p3, small["upf"], small["upb"], small["bias_f"], small["bias_b"], n)
    o_f, s_f, sfin_f = gla_scan_fwd(qs, ks, vs, cum_f, rev=False, name="gla_scan_fwd_f")
    o_b, s_b, sfin_b = gla_scan_fwd(qs, ks, vs, cum_b, rev=True, name="gla_scan_fwd_b")
    og = gla_out_fwd(o_f, o_b, p4, small["gla_norm_g"])
    y_gla = matmul_nn(og, wts["gla_proj"], None, name="gla_proj_fwd", m=t, tm=1024, tn=1024, out_dtype=BF16)

    merged, dh, dmo, dgate, d_final_g, loss = merge_out_final(p5, y_conv, y_gla, wts["w_out"], x2, gate,
                                                              small["final_norm_g"], tgt2, n)

    g = {"final_norm_g": d_final_g}
    g["w_out"] = matmul_tn(merged, dmo, name="w_out_wgrad", t=t, tn=1024, tt=1024)[0]
    dyc, dyg, dp5 = out_dgrad_merge_bwd(p5, y_conv, y_gla, dmo, wts["w_out"])

    g["conv_proj"] = matmul_tn(ac, dyc, name="conv_proj_wgrad", t=t, tn=1024, tt=1024)[0]
    daconv, dp2, g["conv_ln_g"], g["conv_ln_b"] = ln_gate_bwd(aconv, p2, dyc, wts["conv_proj"], small["conv_ln_g"],
                                                               small["conv_ln_b"])
    dp1, dconv_w, dconv_b = conv_bwd(p1, daconv, small["conv_w"], n)
    g["conv_w"], g["conv_b"] = dconv_w, dconv_b

    g["gla_proj"] = matmul_tn(og, dyg, name="gla_proj_wgrad", t=t, tn=1024, tt=1024)[0]
    do, dp4, g["gla_norm_g"] = gla_out_bwd(o_f, o_b, p4, dyg, wts["gla_proj"], small["gla_norm_g"])
    dq_f, dk_f, dv_f, dc_f = gla_scan_bwd(qs, ks, vs, cum_f, s_f, sfin_f, do, rev=False, name="gla_scan_bwd_f")
    dq_b, dk_b, dv_b, dc_b = gla_scan_bwd(qs, ks, vs, cum_b, s_b, sfin_b, do, rev=True, name="gla_scan_bwd_b")
    dp3, g["upf"], g["upb"], g["bias_f"], g["bias_b"] = gla_prep_bwd(
        p3, dq_f, dq_b, dk_f, dk_b, dv_f, dv_b, dc_f, dc_b,
        small["upf"], small["upb"], small["bias_f"], small["bias_b"], n)

    dps = [dp1, dp2, dp3, dp4, dp5]
    g["proj"] = jnp.concatenate([g["conv_proj"].reshape(N_CHIPS, 256, D), g["gla_proj"].reshape(N_CHIPS, 256, D),
                                 g["w_out"].reshape(N_CHIPS, 256, D)], 1)
    got = {}
    for i in [0, 1, 3, 4, 2]:
        dp = dps[i]
        rows = dp.shape[0]
        tn = W3 if dp.shape[1] == W3 else 1024
        others = [j for j in range(5) if j != i]
        swap = ([g["w%d" % (j + 1)] for j in others], g["proj"]) if i == 2 else None
        outs = matmul_tn(u, dp, name="w_in_wgrad_%d" % (i + 1), t=rows, tn=tn, tt=1024 if rows % 1024 == 0 else 768,
                         colsum=True, swap=swap)
        g["w%d" % (i + 1)], g["b%d" % (i + 1)] = outs[0], outs[1]
        if swap is not None:
            got = dict(zip(others, outs[2:2 + len(others)]), proj=outs[2 + len(others)])
    g["gate"] = dgate
    return loss, dh, dps, g, got


def _group_cols(w):
    gv, gg, z = w[..., 0:1024], w[..., 1024:2048], w[..., 2048:3072]
    q, k, v = w[..., 3072:3584], w[..., 3584:4096], w[..., 4096:5120]
    ab = w[..., 5120:5152]
    r, mc, mg = w[..., 5152:6176], w[..., 6176:7200], w[..., 7200:8224]
    g1 = jnp.concatenate([p for j in range(CONV_NCB)
                          for p in (gv[..., CONV_CB * j:CONV_CB * (j + 1)], gg[..., CONV_CB * j:CONV_CB * (j + 1)])], -1)
    pad = jnp.zeros(w.shape[:-1] + (W3 - 2080,), w.dtype)
    g3 = jnp.concatenate([v, q, k, ab, pad], -1)
    return g1, z, g3, r, jnp.concatenate([mc, mg], -1)


def _ungroup_cols(g1, g2, g3, g4, g5):
    gv = jnp.concatenate([g1[..., 2 * CONV_CB * j:2 * CONV_CB * j + CONV_CB] for j in range(CONV_NCB)], -1)
    gg = jnp.concatenate([g1[..., 2 * CONV_CB * j + CONV_CB:2 * CONV_CB * (j + 1)] for j in range(CONV_NCB)], -1)
    v, q, k, ab = g3[..., 0:1024], g3[..., 1024:1536], g3[..., 1536:2048], g3[..., 2048:2080]
    return jnp.concatenate([gv, gg, g2, q, k, v, ab, g4, g5[..., 0:1024], g5[..., 1024:2048]], -1)


def _natural_pieces():
    pieces = [(CONV_CB * j, CONV_CB, 0, 2 * CONV_CB * j) for j in range(CONV_NCB)]
    pieces += [(1024 + CONV_CB * j, CONV_CB, 0, 2 * CONV_CB * j + CONV_CB) for j in range(CONV_NCB)]
    pieces += [(2048, 1024, 1, 0), (3072, 512, 2, O3_Q), (3584, 512, 2, O3_K), (4096, 1024, 2, O3_V), (5120, 32, 2, O3_AB),
               (5152, 1024, 3, 0), (6176, 1024, 4, 0), (7200, 1024, 4, 1024)]
    return sorted(pieces)


def _ungroup_to_shards(groups):
    shards = []
    for i in range(N_CHIPS):
        lo, hi = i * W_IN_SHARD, (i + 1) * W_IN_SHARD
        parts = []
        for nat, width, g, gcol in _natural_pieces():
            a, b = max(nat, lo), min(nat + width, hi)
            if a < b:
                parts.append(groups[g][:, gcol + a - nat:gcol + b - nat])
        shards.append(jnp.concatenate(parts, 1))
    return jnp.stack(shards)


def _pad_up(up, row0):
    return jnp.zeros((128, GLA_DK), F32).at[row0:row0 + up.shape[0]].set(up)


def _adamw_math(w, g, m, v):
    m = ADAM_B1 * m + (1.0 - ADAM_B1) * g
    v = ADAM_B2 * v + (1.0 - ADAM_B2) * (g * g)
    m_hat = m / (1.0 - ADAM_B1 ** ADAM_STEP)
    v_hat = v / (1.0 - ADAM_B2 ** ADAM_STEP)
    delta = -ADAM_LR * (m_hat / (jnp.sqrt(v_hat) + ADAM_EPS) + ADAM_WD * w)
    return delta, m, v


def adamw2d(w, g, m, v, *, name, tr, tcols=None):
    rows, cols = w.shape[-2:]

    def body(w_ref, g_ref, m_ref, v_ref, d_ref, nm_ref, nv_ref):
        d_ref[...], nm_ref[...], nv_ref[...] = _adamw_math(w_ref[...], g_ref[...], m_ref[...], v_ref[...])

    tcols = cols if tcols is None else tcols
    if w.ndim == 3:
        spec = pl.BlockSpec((1, tr, tcols), lambda i, j: (0, i, j))
    else:
        spec = pl.BlockSpec((tr, tcols), lambda i, j: (i, j))
    return _pallas(
        body, name=name, grid=(rows // tr, cols // tcols), in_specs=[spec] * 4, out_specs=[spec] * 3,
        out_shape=[jax.ShapeDtypeStruct(w.shape, F32)] * 3, compiler_params=_params(("parallel", "parallel")),
    )(w, g, m, v)


def adamw_many(ws, gs, ms, vs):
    k = len(ws)
    two = lambda a: a.reshape((-1, a.shape[-1]))

    def body(*refs):
        w_refs, g_refs, m_refs, v_refs = refs[:k], refs[k:2 * k], refs[2 * k:3 * k], refs[3 * k:4 * k]
        d_refs, nm_refs, nv_refs = refs[4 * k:5 * k], refs[5 * k:6 * k], refs[6 * k:7 * k]
        for i in range(k):
            d_refs[i][...], nm_refs[i][...], nv_refs[i][...] = _adamw_math(
                w_refs[i][...], g_refs[i][...], m_refs[i][...], v_refs[i][...])

    shapes = [jax.ShapeDtypeStruct(two(a).shape, F32) for a in ws]
    outs = _pallas(body, name="adamw_small", out_shape=shapes * 3, compiler_params=_params())(
        *[two(a) for a in ws], *[two(a) for a in gs], *[two(a) for a in ms], *[two(a) for a in vs])
    back = lambda lst: [o.reshape(a.shape) for o, a in zip(lst, ws)]
    return back(outs[:k]), back(outs[k:2 * k]), back(outs[2 * k:])


def sum_devices(sall, *, name):
    rows = sall.shape[1]

    def body(s_ref, o_ref):
        acc = s_ref[0]
        for d in range(1, N_DEV):
            acc = acc + s_ref[d]
        o_ref[...] = acc

    return _pallas(body, name=name, out_shape=jax.ShapeDtypeStruct((rows, D), F32),
                   compiler_params=_params())(sall)


def pair_add(core, g, got, *, name, tr):
    n, rows, cols = got.shape
    g4 = g.reshape(n, 2, rows, cols)

    def body(core_ref, g_ref, got_ref, ob_ref):
        del core_ref
        ob_ref[0] = (g_ref[0, 0] + got_ref[0]).astype(BF16)

    spec = pl.BlockSpec((1, tr, cols), lambda i, t, core_ref: (i, t, 0))
    return _pallas(
        body, name=name,
        grid_spec=pltpu.PrefetchScalarGridSpec(
            num_scalar_prefetch=1, grid=(n, rows // tr),
            in_specs=[pl.BlockSpec((1, 1, tr, cols), lambda i, t, core_ref: (i, core_ref[0], t, 0)), spec],
            out_specs=spec),
        out_shape=jax.ShapeDtypeStruct(got.shape, BF16),
        compiler_params=_params(("parallel", "parallel")))(core, g4, got)


def chip_add(place, pa, rb, *, name, tr):
    _, rows, cols = pa.shape

    def body(place_ref, m_ref, r_ref, o_ref):
        del place_ref
        o_ref[0] = ((m_ref[0].astype(F32) + r_ref[0].astype(F32)) + r_ref[1].astype(F32)) + r_ref[2].astype(F32)

    return _pallas(
        body, name=name,
        grid_spec=pltpu.PrefetchScalarGridSpec(
            num_scalar_prefetch=1, grid=(rows // tr,),
            in_specs=[pl.BlockSpec((1, tr, cols), lambda t, place_ref: (place_ref[0], t, 0)),
                      pl.BlockSpec((3, tr, cols), lambda t, place_ref: (0, t, 0))],
            out_specs=pl.BlockSpec((1, tr, cols), lambda t, place_ref: (place_ref[1], t, 0))),
        out_shape=jax.ShapeDtypeStruct((2, rows, cols), F32),
        compiler_params=_params(("parallel",)))(place, pa, rb)


def ada_bwd(call, cctx_rows, dm_shard, dm_full, adaw):
    nsh = adaw.shape[1]

    def body(c_ref, cc_ref, dms_ref, dmf_ref, w_ref, gw_ref, gb_ref, pq_ref):
        a_lat = _silu(c_ref[...])
        a_ctx = _silu(cc_ref[...])
        dms = dms_ref[...]
        gw_ref[...] = _tn(a_lat, dms[0:64], HI) + _tn(a_ctx, dms[64:72], HI)
        gb_ref[...] = jnp.sum(dmf_ref[...], axis=0, keepdims=True)
        part = _nt(dms[64:72], w_ref[...], HI)
        pq_ref[...] = jnp.zeros_like(pq_ref) + jnp.sum(part, axis=0, keepdims=True)

    return _pallas(body, name="ada_bwd",
                   out_shape=[jax.ShapeDtypeStruct((D, nsh), F32), jax.ShapeDtypeStruct((1, 3 * D), F32),
                              jax.ShapeDtypeStruct((8, D), F32)],
                   compiler_params=_params())(call, cctx_rows, dm_shard, dm_full, adaw)


def cctx_grad(pq_all, cctx_rows):
    def body(p_ref, c_ref, o_ref):
        acc = p_ref[0]
        for qi in range(1, N_CHIPS):
            acc = acc + p_ref[qi]
        o_ref[...] = acc * _dsilu(c_ref[...])

    return _pallas(body, name="cctx_grad", out_shape=jax.ShapeDtypeStruct((8, D), F32),
                   compiler_params=_params())(pq_all, cctx_rows)


def _place():
    x, y, c = lax.axis_index("x"), lax.axis_index("y"), lax.axis_index("c")
    chips = [(1 - x, y), (x, 1 - y), (1 - x, 1 - y)]
    return x, y, c, chips


def _all_peers(x, y, c):
    return [((1 - x) if r & 4 else x, (1 - y) if r & 2 else y, (1 - c) if r & 1 else c) for r in range(1, N_DEV)]


def _remote(src, dst, send_sem, recv_sem, dev):
    return pltpu.make_async_remote_copy(src_ref=src, dst_ref=dst, send_sem=send_sem, recv_sem=recv_sem,
                                        device_id=dev, device_id_type=MESH)


ANY = pl.BlockSpec(memory_space=pl.ANY)
VMEM = pl.BlockSpec(memory_space=pltpu.VMEM)
F_ROWS = 16


W_ROW_CHUNKS = 4
P_ROW_CHUNKS = 2
N_BULK = W_ROW_CHUNKS + P_ROW_CHUNKS


def _half_chunks(core, n_rows, align):
    out = []
    for a, k in ((0, W_ROW_CHUNKS), (1, P_ROW_CHUNKS)):
        half = n_rows[a] // 2
        size = half // k
        for i in range(k):
            start = core * half + i * size
            out.append((a, pl.ds(start if isinstance(start, int) else pl.multiple_of(start, align), size)))
    return out


def gather_weights(c8, cctx8, adaw, adab, w_sh, p_sh, fp):
    nsh = adaw.shape[1]

    def body(c_ref, cctx_ref, adaw_ref, adab_ref, w_ref, p_ref, fp_ref, wall_ref, pall_ref, fall_ref, call_ref, mall_ref,
             abuf, w_send, w_recv, h_send, h_recv, c_send, c_recv, m_send, m_recv, f_send, f_recv):
        x, y, c, chips = _place()
        q = 2 * x + y
        dev = 4 * x + 2 * y + c
        qs = [2 * cx + cy for cx, cy in chips]
        sib = (x, y, 1 - c)
        srcs, dsts = (w_ref, p_ref), (wall_ref, pall_ref)
        n_rows = (w_ref.shape[0], p_ref.shape[0])
        mine = _half_chunks(c, n_rows, 16)
        other = _half_chunks(1 - c, n_rows, 16)

        bulk = [[_remote(srcs[a].at[rows], dsts[a].at[q, rows], w_send.at[j * N_BULK + i], w_recv.at[j * N_BULK + i],
                         (*chips[j], c)) for i, (a, rows) in enumerate(mine)] for j in range(3)]
        fall_ref[q] = fp_ref[...]
        small = [_remote(fp_ref, fall_ref.at[q], f_send.at[j], f_recv.at[j], (*chips[j], c)) for j in range(3)]
        my_rows = pl.ds(pl.multiple_of(8 * dev, 8), 8)
        call_ref[my_rows, :] = c_ref[...]
        cond = [_remote(c_ref, call_ref.at[my_rows, :], c_send.at[r], c_recv.at[r], peer)
                for r, peer in enumerate(_all_peers(x, y, c))]
        for cp in sum(bulk, []) + small + cond:
            cp.start()
        for cp in cond:
            cp.wait_recv()

        abuf[pl.ds(0, 64), :] = _silu(call_ref[...])
        abuf[pl.ds(64, 8), :] = _silu(cctx_ref[...])
        mall_ref[q] = _nn(abuf[...], adaw_ref[...], HI) + adab_ref[...]
        mod = [_remote(mall_ref.at[q], mall_ref.at[q], m_send.at[j], m_recv.at[j], (*chips[j], c)) for j in range(3)]
        for cp in mod:
            cp.start()

        handed = []
        for j in range(3):
            for i, (a, rows) in enumerate(mine):
                bulk[j][i].wait_recv()
                cp = _remote(dsts[a].at[qs[j], rows], dsts[a].at[qs[j], rows],
                             h_send.at[j * N_BULK + i], h_recv.at[j * N_BULK + i], sib)
                cp.start()
                handed.append(cp)
        for j in range(3):
            for i, (a, rows) in enumerate(other):
                _remote(dsts[a].at[qs[j], rows], dsts[a].at[qs[j], rows],
                        h_send.at[j * N_BULK + i], h_recv.at[j * N_BULK + i], sib).wait_recv()
        for cp in mod + small:
            cp.wait_recv()
        for cp in sum(bulk, []) + small + cond + mod + handed:
            cp.wait_send()

    def dma(n):
        return pltpu.SemaphoreType.DMA((n,))

    return _pallas(
        body, name="gather_weights",
        in_specs=[VMEM, VMEM, VMEM, VMEM, ANY, ANY, VMEM],
        out_specs=[ANY, ANY, VMEM, VMEM, VMEM],
        out_shape=[jax.ShapeDtypeStruct((N_CHIPS,) + w_sh.shape, BF16), jax.ShapeDtypeStruct((N_CHIPS,) + p_sh.shape, BF16),
                   jax.ShapeDtypeStruct((N_CHIPS, F_ROWS, D), F32),
                   jax.ShapeDtypeStruct((8 * N_DEV, D), F32), jax.ShapeDtypeStruct((N_CHIPS, MOD_ROWS, nsh), F32)],
        scratch_shapes=[pltpu.VMEM((MOD_ROWS, D), F32), dma(3 * N_BULK), dma(3 * N_BULK), dma(3 * N_BULK), dma(3 * N_BULK),
                        dma(7), dma(7), dma(3), dma(3), dma(3), dma(3)],
        compiler_params=_params(),
    )(c8, cctx8, adaw, adab, w_sh, p_sh, fp)


def _pair_count(gs, gp):
    return len(gs) * W_ROW_CHUNKS + (0 if gp is None else N_CHIPS * P_ROW_CHUNKS)


def _pair_got_shapes(gs, gp):
    shapes = [jax.ShapeDtypeStruct((D // 2, a.shape[1]), F32) for a in gs]
    if gp is not None:
        shapes.append(jax.ShapeDtypeStruct((N_CHIPS, gp.shape[1] // 2, gp.shape[2]), F32))
    return shapes


def _pair_copies(g_refs, gp_ref, got_refs, gotp_ref, a_send, a_recv):
    x, y, c, _ = _place()
    sib = (x, y, 1 - c)
    pair = []
    half, size = D // 2, D // 2 // W_ROW_CHUNKS
    for gi in range(len(g_refs)):
        for i in range(W_ROW_CHUNKS):
            k = len(pair)
            rows_o = pl.ds(pl.multiple_of((1 - c) * half + i * size, 8), size)
            pair.append(_remote(g_refs[gi].at[rows_o], got_refs[gi].at[pl.ds(i * size, size)],
                                a_send.at[k], a_recv.at[k], sib))
    if gp_ref is not None:
        half, size = gp_ref.shape[1] // 2, gp_ref.shape[1] // 2 // P_ROW_CHUNKS
        for s in range(N_CHIPS):
            for i in range(P_ROW_CHUNKS):
                k = len(pair)
                rows_o = pl.ds(pl.multiple_of((1 - c) * half + i * size, 8), size)
                pair.append(_remote(gp_ref.at[s, rows_o], gotp_ref.at[s, pl.ds(i * size, size)],
                                    a_send.at[k], a_recv.at[k], sib))
    return pair


def pair_swap(gs, sm):
    n_gs = len(gs)

    def body(*refs):
        g_refs, sm_ref = refs[:n_gs], refs[n_gs]
        got_refs, sall_ref = refs[n_gs + 1:2 * n_gs + 1], refs[2 * n_gs + 1]
        a_send, a_recv, s_send, s_recv = refs[2 * n_gs + 2:]
        x, y, c, _ = _place()
        dev = 4 * x + 2 * y + c
        pair = _pair_copies(g_refs, None, got_refs, None, a_send, a_recv)
        sall_ref[dev] = sm_ref[...]
        small = [_remote(sm_ref, sall_ref.at[dev], s_send.at[r], s_recv.at[r], peer)
                 for r, peer in enumerate(_all_peers(x, y, c))]
        for cp in pair + small:
            cp.start()
        for cp in small + pair:
            cp.wait_recv()
        for cp in small + pair:
            cp.wait_send()

    return _pallas(
        body, name="pair_swap", in_specs=[ANY] * n_gs + [VMEM], out_specs=[ANY] * n_gs + [VMEM],
        out_shape=_pair_got_shapes(gs, None) + [jax.ShapeDtypeStruct((N_DEV,) + sm.shape, F32)],
        scratch_shapes=[pltpu.SemaphoreType.DMA((_pair_count(gs, None),)), pltpu.SemaphoreType.DMA((_pair_count(gs, None),)),
                        pltpu.SemaphoreType.DMA((N_DEV - 1,)), pltpu.SemaphoreType.DMA((N_DEV - 1,))],
        compiler_params=_params(),
    )(*gs, sm)


def gather_small(sm):
    rows = sm.shape[0]

    def body(sm_ref, sall_ref, s_send, s_recv):
        x, y, c, _ = _place()
        dev = 4 * x + 2 * y + c
        sall_ref[dev] = sm_ref[...]
        small = [_remote(sm_ref, sall_ref.at[dev], s_send.at[r], s_recv.at[r], peer)
                 for r, peer in enumerate(_all_peers(x, y, c))]
        for cp in small:
            cp.start()
        for cp in small:
            cp.wait_recv()
        for cp in small:
            cp.wait_send()

    return _pallas(
        body, name="gather_small", in_specs=[VMEM], out_specs=VMEM,
        out_shape=jax.ShapeDtypeStruct((N_DEV, rows, D), F32),
        scratch_shapes=[pltpu.SemaphoreType.DMA((7,)), pltpu.SemaphoreType.DMA((7,))],
        compiler_params=_params(),
    )(sm)


def pair_share(ghw, ghp, pq):
    def body(ghw_ref, ghp_ref, pq_ref, outw_ref, outp_ref, pqa_ref, send, recv, p_send, p_recv):
        del ghw_ref, ghp_ref
        x, y, c, chips = _place()
        q = 2 * x + y
        refs = (outw_ref, outp_ref)
        n_rows = (2 * outw_ref.shape[1], 2 * outp_ref.shape[1])
        pair = [_remote(refs[a].at[c, rows], refs[a].at[c, rows], send.at[i], recv.at[i], (x, y, 1 - c))
                for i, (a, rows) in enumerate(_half_chunks(0, n_rows, 8))]
        pqa_ref[q] = pq_ref[...]
        small = [_remote(pq_ref, pqa_ref.at[q], p_send.at[j], p_recv.at[j], (*chips[j], c)) for j in range(3)]
        for cp in pair + small:
            cp.start()
        for i, (a, rows) in enumerate(_half_chunks(0, n_rows, 8)):
            _remote(refs[a].at[1 - c, rows], refs[a].at[1 - c, rows], send.at[i], recv.at[i], (x, y, 1 - c)).wait_recv()
        for cp in small:
            cp.wait_recv()
        for cp in pair + small:
            cp.wait_send()

    return _pallas(
        body, name="pair_share", in_specs=[ANY, ANY, VMEM], out_specs=[ANY, ANY, VMEM],
        out_shape=[jax.ShapeDtypeStruct(ghw.shape, F32), jax.ShapeDtypeStruct(ghp.shape, F32),
                   jax.ShapeDtypeStruct((N_CHIPS, 8, D), F32)],
        scratch_shapes=[pltpu.SemaphoreType.DMA((N_BULK,)), pltpu.SemaphoreType.DMA((N_BULK,)),
                        pltpu.SemaphoreType.DMA((3,)), pltpu.SemaphoreType.DMA((3,))],
        input_output_aliases={0: 0, 1: 1},
        compiler_params=_params(),
    )(ghw, ghp, pq)


def _rows_of(shape):
    size = 1
    for s in shape:
        size *= s
    return -(-size // D)


def _pack(arrs, rows_multiple=8):
    parts = []
    total = 0
    for a in arrs:
        f = a.reshape(-1).astype(F32)
        r = _rows_of(a.shape)
        parts.append(jnp.pad(f, (0, r * D - f.shape[0])))
        total += r
    pad_rows = (-total) % rows_multiple
    if pad_rows:
        parts.append(jnp.zeros((pad_rows * D,), F32))
    return jnp.concatenate(parts).reshape(-1, D)


def _unpack(p, shapes):
    out = []
    r0 = 0
    for shp in shapes:
        r = _rows_of(shp)
        size = 1
        for s in shp:
            size *= s
        out.append(p[r0:r0 + r].reshape(-1)[:size].reshape(shp))
        r0 += r
    return out


WEIGHT_NAMES = ['c_ctx', 'ada_w', 'ada_b', 'norm_g', 'w_in', 'b_in', 'conv_w', 'conv_b', 'conv_ln_g', 'conv_ln_b',
                'conv_proj', 'decay_up_fwd', 'decay_bias_fwd', 'decay_up_bwd', 'decay_bias_bwd', 'gla_norm_g', 'gla_proj',
                'w_out', 'final_norm_g']
SMALL_NAMES = ['c_ctx', 'ada_b', 'norm_g', 'b_in', 'conv_w', 'conv_b', 'conv_ln_g', 'conv_ln_b', 'decay_up_fwd',
               'decay_bias_fwd', 'decay_up_bwd', 'decay_bias_bwd', 'gla_norm_g', 'final_norm_g']


def kernel(x, c, ctx, c_ctx, ada_w, ada_b, norm_g, w_in, b_in, conv_w, conv_b, conv_ln_g, conv_ln_b, conv_proj, decay_up_fwd, decay_bias_fwd, decay_up_bwd, decay_bias_bwd, gla_norm_g, gla_proj, w_out, final_norm_g, loss_target, m_c_ctx, m_ada_w, m_ada_b, m_norm_g, m_w_in, m_b_in, m_conv_w, m_conv_b, m_conv_ln_g, m_conv_ln_b, m_conv_proj, m_decay_up_fwd, m_decay_bias_fwd, m_decay_up_bwd, m_decay_bias_bwd, m_gla_norm_g, m_gla_proj, m_w_out, m_final_norm_g, v_c_ctx, v_ada_w, v_ada_b, v_norm_g, v_w_in, v_b_in, v_conv_w, v_conv_b, v_conv_ln_g, v_conv_ln_b, v_conv_proj, v_decay_up_fwd, v_decay_bias_fwd, v_decay_up_bwd, v_decay_bias_bwd, v_gla_norm_g, v_gla_proj, v_w_out, v_final_norm_g):
    w = dict(c_ctx=c_ctx, ada_w=ada_w, ada_b=ada_b, norm_g=norm_g, w_in=w_in, b_in=b_in, conv_w=conv_w, conv_b=conv_b,
             conv_ln_g=conv_ln_g, conv_ln_b=conv_ln_b, conv_proj=conv_proj, decay_up_fwd=decay_up_fwd,
             decay_bias_fwd=decay_bias_fwd, decay_up_bwd=decay_up_bwd, decay_bias_bwd=decay_bias_bwd,
             gla_norm_g=gla_norm_g, gla_proj=gla_proj, w_out=w_out, final_norm_g=final_norm_g)
    m = dict(c_ctx=m_c_ctx, ada_w=m_ada_w, ada_b=m_ada_b, norm_g=m_norm_g, w_in=m_w_in, b_in=m_b_in, conv_w=m_conv_w,
             conv_b=m_conv_b, conv_ln_g=m_conv_ln_g, conv_ln_b=m_conv_ln_b, conv_proj=m_conv_proj,
             decay_up_fwd=m_decay_up_fwd, decay_bias_fwd=m_decay_bias_fwd, decay_up_bwd=m_decay_up_bwd,
             decay_bias_bwd=m_decay_bias_bwd, gla_norm_g=m_gla_norm_g, gla_proj=m_gla_proj, w_out=m_w_out,
             final_norm_g=m_final_norm_g)
    v = dict(c_ctx=v_c_ctx, ada_w=v_ada_w, ada_b=v_ada_b, norm_g=v_norm_g, w_in=v_w_in, b_in=v_b_in, conv_w=v_conv_w,
             conv_b=v_conv_b, conv_ln_g=v_conv_ln_g, conv_ln_b=v_conv_ln_b, conv_proj=v_conv_proj,
             decay_up_fwd=v_decay_up_fwd, decay_bias_fwd=v_decay_bias_fwd, decay_up_bwd=v_decay_up_bwd,
             decay_bias_bwd=v_decay_bias_bwd, gla_norm_g=v_gla_norm_g, gla_proj=v_gla_proj, w_out=v_w_out,
             final_norm_g=v_final_norm_g)
    n = x.shape[0]
    ax, ay, ac = lax.axis_index("x"), lax.axis_index("y"), lax.axis_index("c")
    q = 2 * ax + ay
    dev = 4 * ax + 2 * ay + ac
    nsh = ada_w.shape[2]

    w_sh = w_in[0].astype(BF16)
    p_sh = jnp.concatenate([conv_proj[0], gla_proj[0], w_out[0]], 0).astype(BF16)
    fp = _pack([conv_w[0], decay_up_fwd[0], decay_up_bwd[0]], F_ROWS)
    c8 = jnp.pad(c, ((0, 8 - n), (0, 0)))
    cctx8 = jnp.pad(c_ctx[None], ((0, 7), (0, 0)))
    adab_sh = lax.dynamic_slice(ada_b, (0, q * nsh), (1, nsh))
    w_all, p_all, fall, call, mall = gather_weights(c8, cctx8, ada_w[0], adab_sh, w_sh, p_sh, fp)

    mod_all = jnp.transpose(mall, (1, 0, 2)).reshape(MOD_ROWS, 3 * D)
    mod_mine = lax.dynamic_slice(mod_all, (8 * dev, 0), (n, 3 * D))
    mod_ctx = mod_all[64:65]
    shift = jnp.concatenate([mod_mine[:, 0:D], mod_ctx[:, 0:D]], 0)[:, None, :]
    scale1 = 1.0 + jnp.concatenate([mod_mine[:, D:2 * D], mod_ctx[:, D:2 * D]], 0)[:, None, :]
    gate = mod_mine[:, 2 * D:3 * D][:, None, :]

    own = lambda i, mine, got: jnp.where(q == i, mine, got)
    g1, g2, g3, g4, g5 = _group_cols(jnp.concatenate([own(i, w_sh, w_all[i]) for i in range(N_CHIPS)], 1))
    p_full = jnp.stack([own(i, p_sh, p_all[i]) for i in range(N_CHIPS)])
    wts = dict(w1=g1, w2=g2, w3=g3, w4=g4, w5=g5,
               conv_proj=p_full[:, 0:256].reshape(D, D), gla_proj=p_full[:, 256:512].reshape(D, D),
               w_out=p_full[:, 512:768].reshape(D, D))
    f_parts = [_unpack(fall[i], [conv_w.shape[1:], decay_up_fwd.shape[1:], decay_up_bwd.shape[1:]]) for i in range(N_CHIPS)]
    conv_w_full = jnp.concatenate([p[0] for p in f_parts], 1)
    upf_full = jnp.concatenate([p[1] for p in f_parts], 1)
    upb_full = jnp.concatenate([p[2] for p in f_parts], 1)
    b1, b2, b3, b4, b5 = _group_cols(b_in)
    small = dict(b1=b1, b2=b2, b3=b3, b4=b4, b5=b5, norm_g=norm_g,
                 conv_w=jnp.pad(conv_w_full, ((0, 1), (0, 0))), conv_b=conv_b, conv_ln_g=conv_ln_g, conv_ln_b=conv_ln_b,
                 upf=_split3(_pad_up(upf_full, 0)), upb=_split3(_pad_up(upb_full, 16)),
                 bias_f=decay_bias_fwd, bias_b=decay_bias_bwd,
                 gla_norm_g=gla_norm_g, final_norm_g=final_norm_g[None])

    loss_part, dh, dps, g, got = local_step(x, ctx, loss_target, (scale1, shift, gate), wts, small)
    loss = lax.psum(loss_part[0, 0], ("x", "y", "c"))

    gs = [g["w%d" % i] for i in range(1, 6)]
    d_b_in = _ungroup_cols(*[g["b%d" % i] for i in range(1, 6)])
    early = [d_b_in, g["conv_b"].sum(0), g["conv_ln_g"], g["conv_ln_b"], g["bias_f"], g["bias_b"],
             g["gla_norm_g"], g["final_norm_g"], g["conv_w"].sum(0)[:CONV_K], g["upf"][0:16], g["upb"][16:32]]
    early_shapes = [a.shape for a in early]
    got[2], sall1 = pair_swap([gs[2]], _pack(early))
    core = ac.astype(jnp.int32).reshape(1)
    chip = q.astype(jnp.int32).reshape(1)
    halves = [pair_add(core, gs[i][None], got[i][None], name="pair_add_w%d" % (i + 1), tr=128)[0]
              for i in range(5)]
    paw16 = _ungroup_to_shards(halves)
    pap16 = pair_add(core, g["proj"], got["proj"], name="pair_add_p", tr=384)
    grad_x2, dshift, dscale, g["norm_g"], rbw, rbp = dgrad_norm_bwd(
        dps, [wts["w%d" % i] for i in range(1, 6)], paw16, pap16, x.reshape(n * SEQ, D), ctx.reshape(n * NCTX, D), dh,
        scale1, norm_g, tm=256)

    dm_mine = jnp.concatenate([dshift[:n, 0], dscale[:n, 0], g["gate"][:, 0]], -1)
    dm_ctx = jnp.concatenate([dshift[n, 0], dscale[n, 0], jnp.zeros((D,), F32)], -1)
    late = [g["norm_g"], dm_mine, dm_ctx]
    late_shapes = [a.shape for a in late]
    sall2 = gather_small(_pack(late))
    (s_b_in, s_conv_b, s_ln_g, s_ln_b, s_bias_f, s_bias_b, s_gla_g, s_final_g, s_conv_w, s_upf,
     s_upb) = _unpack(sum_devices(sall1, name="sum_devices_early"), early_shapes)
    s_norm_g = _unpack(sum_devices(sall2, name="sum_devices_late"), late_shapes)[0]
    r_mine, r_ctx = 1, 1 + 3 * n
    dm_all = sall2[:, r_mine:r_ctx].reshape(N_DEV, n, 3 * D)
    dm_full = jnp.concatenate([jnp.pad(dm_all, ((0, 0), (0, 8 - n), (0, 0))).reshape(8 * N_DEV, 3 * D),
                               sall2[:, r_ctx:r_ctx + 3].reshape(N_DEV, 3 * D)], 0)
    dm_shard = lax.dynamic_slice(dm_full, (0, q * nsh), (MOD_ROWS, nsh))
    cctx_rows = jnp.broadcast_to(c_ctx[None], (8, D))
    g_ada_w, g_ada_b, pq = ada_bwd(call, cctx_rows, dm_shard, dm_full, ada_w[0])

    place = jnp.concatenate([chip, core])
    ghw = chip_add(place, paw16, rbw, name="chip_add_w", tr=128)
    ghp = chip_add(place, pap16, rbp, name="chip_add_p", tr=384)
    gw_mine, gp_mine, pq_all = pair_share(ghw, ghp, pq)
    gp_mine = gp_mine.reshape(768, D)
    g_c_ctx = cctx_grad(pq_all, cctx_rows)[0]

    grads = dict(
        c_ctx=g_c_ctx, ada_w=g_ada_w[None], ada_b=g_ada_b, norm_g=s_norm_g,
        w_in=gw_mine.reshape(1, D, W_IN_SHARD), b_in=s_b_in,
        conv_w=lax.dynamic_slice(s_conv_w, (0, q * 256), (CONV_K, 256))[None], conv_b=s_conv_b,
        conv_ln_g=s_ln_g, conv_ln_b=s_ln_b, conv_proj=gp_mine[0:256][None],
        decay_up_fwd=lax.dynamic_slice(s_upf, (0, q * 128), (16, 128))[None], decay_bias_fwd=s_bias_f,
        decay_up_bwd=lax.dynamic_slice(s_upb, (0, q * 128), (16, 128))[None], decay_bias_bwd=s_bias_b,
        gla_norm_g=s_gla_g, gla_proj=gp_mine[256:512][None], w_out=gp_mine[512:768][None],
        final_norm_g=s_final_g[0])

    delta, new_m, new_v = {}, {}, {}
    for name in ["ada_w", "conv_proj", "gla_proj", "w_out"]:
        delta[name], new_m[name], new_v[name] = adamw2d(w[name], grads[name].reshape(w[name].shape), m[name], v[name],
                                                        name="adamw_" + name, tr=128)
    tr_ = lambda a: jnp.swapaxes(a, 1, 2)
    g_w_in_t = tr_(grads["w_in"])
    grads["w_in"] = tr_(g_w_in_t)
    d_, m_, v_ = adamw2d(tr_(w_in), g_w_in_t, tr_(m_w_in), tr_(v_w_in), name="adamw_w_in", tr=W_IN_SHARD, tcols=128)
    delta["w_in"], new_m["w_in"], new_v["w_in"] = tr_(d_), tr_(m_), tr_(v_)
    d_, m_, v_ = adamw_many([w[nm] for nm in SMALL_NAMES], [grads[nm].reshape(w[nm].shape) for nm in SMALL_NAMES],
                            [m[nm] for nm in SMALL_NAMES], [v[nm] for nm in SMALL_NAMES])
    for nm, a, b, cc in zip(SMALL_NAMES, d_, m_, v_):
        delta[nm], new_m[nm], new_v[nm] = a, b, cc

    grad_x = grad_x2.reshape(x.shape)
    return (loss, grad_x, *[grads[nm].reshape(w[nm].shape) for nm in WEIGHT_NAMES], *[delta[nm] for nm in WEIGHT_NAMES],
            *[new_m[nm] for nm in WEIGHT_NAMES], *[new_v[nm] for nm in WEIGHT_NAMES])
```

```python
import jax
import jax.numpy as jnp
from jax import lax
from jax.experimental import pallas as pl
from jax.experimental.pallas import tpu as pltpu

F32 = jnp.float32
BF16 = jnp.bfloat16
MESH = pl.DeviceIdType.MESH
HI = lax.Precision.HIGHEST

D = 1024
SEQ = 2048
GRID_W = 64
GRID_H = SEQ // GRID_W
NCTX = 256
SEQ_ALL = SEQ + NCTX
EPS = 1e-6
CONV_K = 31
CONV_PAD = CONV_K // 2
HEADS = 4
HEAD_K = 128
HEAD_V = 256
GLA_DK = HEADS * HEAD_K
GATE_TAU = 16.0
Q_SCALE = HEAD_K ** -0.5
CHUNK = 64
NCHUNK = SEQ_ALL // CHUNK
NCHUNK_LAT = SEQ // CHUNK
NCHUNK_CTX = NCHUNK - NCHUNK_LAT
SUB = 64
NSUB = CHUNK // SUB
N_IN = 8224
W3 = 2176
O3_V, O3_Q, O3_K, O3_AB = 0, 1024, 1536, 2048

ADAM_LR, ADAM_B1, ADAM_B2, ADAM_EPS, ADAM_WD, ADAM_STEP = 0.001, 0.9, 0.999, 1e-08, 0.01, 10
VMEM_LIMIT = 56 * 1024 * 1024

N_CHIPS = 4
N_DEV = 8
W_IN_SHARD = N_IN // N_CHIPS
MOD_ROWS = 72


def _pallas(body, **kw):
    return pl.pallas_call(body, **kw)


def _params(sem=None, **kw):
    if sem is not None:
        kw["dimension_semantics"] = sem
    return pltpu.CompilerParams(vmem_limit_bytes=VMEM_LIMIT, **kw)


def _sigmoid(v):
    return 1.0 / (1.0 + jnp.exp(-v))


def _silu(v):
    return v * _sigmoid(v)


def _dsilu(v):
    s = _sigmoid(v)
    return s * (1.0 + v * (1.0 - s))


def _log_sigmoid(v):
    return jnp.minimum(v, 0.0) - jnp.log(1.0 + jnp.exp(-jnp.abs(v)))


def _dot(a, b, dims, precision=None):
    return lax.dot_general(a, b, (dims, ((), ())), preferred_element_type=F32, precision=precision)


def _nn(a, b, precision=None):
    return _dot(a, b, ((1,), (0,)), precision)


def _nt(a, b, precision=None):
    return _dot(a, b, ((1,), (1,)), precision)


def _tn(a, b, precision=None):
    return _dot(a, b, ((0,), (0,)), precision)


def _b16(v):
    return v.astype(BF16)


def matmul_nn(a, b, bias, *, name, m, tm, tn, out_dtype, gather=None):
    k = a.shape[1]
    n = b.shape[1]
    has_bias = bias is not None
    nj, ni = n // tn, m // tm
    n_in = 2 + int(has_bias) + int(gather is not None)

    def body(*refs):
        a_ref, b_ref, o_ref = refs[0], refs[1], refs[n_in]
        acc = _nn(a_ref[...], b_ref[...])
        if has_bias:
            acc = acc + refs[2][...]
        o_ref[...] = acc.astype(o_ref.dtype)

        if gather is not None:
            p_ref, pall_ref = refs[n_in - 1], refs[n_in + 1]
            w_send, w_recv, h_send, h_recv = refs[n_in + 2:]
            j, i = pl.program_id(0), pl.program_id(1)
            x, y, c, chips = _place()
            q = 2 * x + y
            mine = _half_chunks(c, (0, p_ref.shape[0]), 16, which=(1,))
            other = _half_chunks(1 - c, (0, p_ref.shape[0]), 16, which=(1,))
            nb = len(mine)

            def bulk():
                return [[_remote(p_ref.at[rows], pall_ref.at[q, rows], w_send.at[pj * nb + pi], w_recv.at[pj * nb + pi],
                                 (*chips[pj], c)) for pi, (_, rows) in enumerate(mine)] for pj in range(3)]

            @pl.when((j == 0) & (i == 0))
            def _():
                for cp in sum(bulk(), []):
                    cp.start()

            @pl.when((j == nj - 1) & (i == ni - 1))
            def _():
                handed = []
                for pj, (cx, cy) in enumerate(chips):
                    for pi, (_, rows) in enumerate(mine):
                        bulk()[pj][pi].wait_recv()
                        cp = _remote(pall_ref.at[2 * cx + cy, rows], pall_ref.at[2 * cx + cy, rows],
                                     h_send.at[pj * nb + pi], h_recv.at[pj * nb + pi], (x, y, 1 - c))
                        cp.start()
                        handed.append(cp)
                for pj, (cx, cy) in enumerate(chips):
                    for pi, (_, rows) in enumerate(other):
                        _remote(pall_ref.at[2 * cx + cy, rows], pall_ref.at[2 * cx + cy, rows],
                                h_send.at[pj * nb + pi], h_recv.at[pj * nb + pi], (x, y, 1 - c)).wait_recv()
                for cp in sum(bulk(), []) + handed:
                    cp.wait_send()

    in_specs = [pl.BlockSpec((tm, k), lambda j, i: (i, 0)), pl.BlockSpec((k, tn), lambda j, i: (0, j))]
    args = [a, b]
    if has_bias:
        in_specs.append(pl.BlockSpec((1, tn), lambda j, i: (0, j)))
        args.append(bias)
    out_specs = [pl.BlockSpec((tm, tn), lambda j, i: (i, j))]
    out_shape = [jax.ShapeDtypeStruct((m, n), out_dtype)]
    scratch = []
    if gather is not None:
        any_spec = pl.BlockSpec(memory_space=pl.ANY)
        in_specs.append(any_spec)
        args.append(gather)
        out_specs.append(any_spec)
        out_shape.append(jax.ShapeDtypeStruct((N_CHIPS,) + gather.shape, gather.dtype))
        scratch = [pltpu.SemaphoreType.DMA((3 * P_ROW_CHUNKS,)) for _ in range(4)]
    outs = _pallas(
        body, name=name, grid=(nj, ni), in_specs=in_specs, out_specs=out_specs, out_shape=out_shape,
        scratch_shapes=scratch,
        compiler_params=_params(("parallel", "parallel") if gather is None else ("arbitrary", "arbitrary")),
    )(*args)
    return outs[0] if gather is None else outs


def matmul_tn(a, b, *, name, t, tn, tt, colsum=False, swap=None):
    m = a.shape[1]
    n = b.shape[1]
    nj, ns = n // tn, t // tt
    n_out = 2 if colsum else 1
    n_sw = 0 if swap is None else len(swap[0])

    def body(a_ref, b_ref, *rest):
        o_ref = rest[n_sw + 1] if swap is not None else rest[0]
        cs_ref = (rest[n_sw + 2] if swap is not None else rest[1]) if colsum else None
        j, s = pl.program_id(0), pl.program_id(1)

        if swap is not None:
            g_refs, gp_ref = rest[:n_sw], rest[n_sw]
            got_refs, gotp_ref = rest[n_sw + 1 + n_out:2 * n_sw + 1 + n_out], rest[2 * n_sw + 1 + n_out]
            sems = rest[2 * n_sw + 2 + n_out:]

            @pl.when((j == 0) & (s == 0))
            def _():
                for cp in _pair_copies(g_refs, gp_ref, got_refs, gotp_ref, *sems):
                    cp.start()

            @pl.when((j == nj - 1) & (s == ns - 1))
            def _():
                for cp in _pair_copies(g_refs, gp_ref, got_refs, gotp_ref, *sems):
                    cp.wait_recv()
                for cp in _pair_copies(g_refs, gp_ref, got_refs, gotp_ref, *sems):
                    cp.wait_send()

        @pl.when(s == 0)
        def _():
            o_ref[...] = jnp.zeros_like(o_ref)
            if colsum:
                cs_ref[...] = jnp.zeros_like(cs_ref)
        o_ref[...] += _tn(a_ref[...], b_ref[...])
        if colsum:
            cs_ref[...] += jnp.sum(b_ref[...].astype(F32), axis=0, keepdims=True)

    in_specs = [pl.BlockSpec((tt, m), lambda j, s: (s, 0)), pl.BlockSpec((tt, tn), lambda j, s: (s, j))]
    out_specs = [pl.BlockSpec((m, tn), lambda j, s: (0, j))]
    out_shape = [jax.ShapeDtypeStruct((m, n), F32)]
    if colsum:
        out_specs.append(pl.BlockSpec((1, tn), lambda j, s: (0, j)))
        out_shape.append(jax.ShapeDtypeStruct((1, n), F32))
    args, scratch = [a, b], []
    if swap is not None:
        gs, gp = swap
        any_spec = pl.BlockSpec(memory_space=pl.ANY)
        in_specs += [any_spec] * (n_sw + 1)
        out_specs += [any_spec] * (n_sw + 1)
        out_shape += _pair_got_shapes(gs, gp)
        args += [*gs, gp]
        scratch = [pltpu.SemaphoreType.DMA((_pair_count(gs, gp),)), pltpu.SemaphoreType.DMA((_pair_count(gs, gp),))]
    return _pallas(
        body, name=name, grid=(nj, ns), in_specs=in_specs, out_specs=out_specs, out_shape=out_shape,
        scratch_shapes=scratch,
        compiler_params=_params(("parallel" if swap is None else "arbitrary", "arbitrary")),
    )(*args)


def dgrad_norm_bwd(dps, wts, paw, pap, x2, ctx2, dh, scale1, norm_g, *, tm):
    t, tc = x2.shape[0], ctx2.shape[0]
    t_all = t + tc
    n_lat, n_ctx = t // tm, tc // tm
    n_tiles = n_lat + n_ctx
    n_samples = scale1.shape[0] - 1
    tps = n_lat // n_samples
    n_grp = n_samples + 1
    n_g = len(dps)
    whole = [g for g in range(n_g) if dps[g].shape[0] == t_all]
    latent = [g for g in range(n_g) if dps[g].shape[0] != t_all]

    def body(*refs):
        dp_refs, w_refs = refs[:n_g], refs[n_g:2 * n_g]
        (paw_ref, pap_ref, x_ref, c_ref, dh_ref, sc_ref, g_ref, dx_ref, dsh_ref, dsc_ref, dg_ref, rbw_ref, rbp_ref,
         du_buf, b_send, b_recv) = refs[2 * n_g:]
        i = pl.program_id(0)

        def exchange():
            x, y, c, chips = _place()
            srcs, dsts = (paw_ref, pap_ref), (rbw_ref, rbp_ref)
            n_rows = (2 * paw_ref.shape[1], 2 * pap_ref.shape[1])
            return [_remote(srcs[a].at[2 * cx + cy, rows], dsts[a].at[j, rows],
                            b_send.at[j * N_BULK + k], b_recv.at[j * N_BULK + k], (cx, cy, c))
                    for j, (cx, cy) in enumerate(chips) for k, (a, rows) in enumerate(_half_chunks(0, n_rows, 16))]

        @pl.when(i == 0)
        def _():
            for cp in exchange():
                cp.start()

        acc = None
        for g in whole:
            part = _nt(dp_refs[g][...], w_refs[g][...])
            acc = part if acc is None else acc + part
        du_buf[...] = acc

        @pl.when(i < n_lat)
        def _():
            lat = None
            for g in latent:
                part = _nt(dp_refs[g][...], w_refs[g][...])
                lat = part if lat is None else lat + part
            du_buf[...] += lat

        duv = du_buf[...]
        xv = jnp.where(i < n_lat, x_ref[...], c_ref[...])
        rs = lax.rsqrt(jnp.mean(xv * xv, axis=-1, keepdims=True) + EPS)
        xh = xv * rs
        n = xh * g_ref[...]
        dn = duv * sc_ref[0]
        dxh = dn * g_ref[...]
        dx = rs * (dxh - xh * jnp.mean(dxh * xh, axis=-1, keepdims=True))

        @pl.when(i < n_lat)
        def _():
            dx_ref[...] = dx + dh_ref[...]

        @pl.when((i % tps == 0) & (i <= n_lat))
        def _():
            dsh_ref[...] = jnp.zeros_like(dsh_ref)
            dsc_ref[...] = jnp.zeros_like(dsc_ref)

        @pl.when(i == 0)
        def _():
            dg_ref[...] = jnp.zeros_like(dg_ref)

        dsh_ref[0] += jnp.sum(duv, axis=0, keepdims=True)
        dsc_ref[0] += jnp.sum(duv * n, axis=0, keepdims=True)
        dg_ref[...] += jnp.sum(dn * xh, axis=0, keepdims=True)

        @pl.when(i == n_tiles - 1)
        def _():
            for cp in exchange():
                cp.wait_recv()
            for cp in exchange():
                cp.wait_send()

    lat = lambda i: (jnp.minimum(i, n_lat - 1), 0)
    grp = lambda i: (jnp.minimum(i // tps, n_samples), 0, 0)
    in_specs = []
    for g, dp in enumerate(dps):
        nrow = dp.shape[0] // tm
        in_specs.append(pl.BlockSpec((tm, dp.shape[1]), lambda i, nrow=nrow: (jnp.minimum(i, nrow - 1), 0)))
    for w in wts:
        in_specs.append(pl.BlockSpec(w.shape, lambda i: (0, 0), pipeline_mode=pl.Buffered(1)))
    any_spec = pl.BlockSpec(memory_space=pl.ANY)
    in_specs += [any_spec, any_spec,
                 pl.BlockSpec((tm, D), lat), pl.BlockSpec((tm, D), lambda i: (jnp.maximum(i - n_lat, 0), 0)),
                 pl.BlockSpec((tm, D), lat), pl.BlockSpec((1, 1, D), grp), pl.BlockSpec((1, D), lambda i: (0, 0))]
    return _pallas(
        body, name="dgrad_norm_bwd", grid=(n_tiles,), in_specs=in_specs,
        out_specs=[pl.BlockSpec((tm, D), lat), pl.BlockSpec((1, 1, D), grp), pl.BlockSpec((1, 1, D), grp),
                   pl.BlockSpec((1, D), lambda i: (0, 0)), any_spec, any_spec],
        out_shape=[jax.ShapeDtypeStruct((t, D), F32), jax.ShapeDtypeStruct((n_grp, 1, D), F32),
                   jax.ShapeDtypeStruct((n_grp, 1, D), F32), jax.ShapeDtypeStruct((1, D), F32),
                   jax.ShapeDtypeStruct((3,) + paw.shape[1:], paw.dtype),
                   jax.ShapeDtypeStruct((3,) + pap.shape[1:], pap.dtype)],
        scratch_shapes=[pltpu.VMEM((tm, D), F32), pltpu.SemaphoreType.DMA((3 * N_BULK,)),
                        pltpu.SemaphoreType.DMA((3 * N_BULK,))],
        compiler_params=_params(("arbitrary",)),
    )(*dps, *wts, paw, pap, x2, ctx2, dh, scale1, norm_g)


TM_NORM = 512


def norm_mod_fwd(x2, ctx2, scale1, shift, norm_g):
    t = x2.shape[0]
    n_lat = t // TM_NORM
    assert ctx2.shape[0] == TM_NORM
    n_samples = scale1.shape[0] - 1
    tps = n_lat // n_samples

    def body(x_ref, c_ref, sc_ref, sh_ref, g_ref, u_ref):
        i = pl.program_id(0)
        xv = jnp.where(i < n_lat, x_ref[...], c_ref[...])
        rs = lax.rsqrt(jnp.mean(xv * xv, axis=-1, keepdims=True) + EPS)
        u = xv * rs * g_ref[...] * sc_ref[0] + sh_ref[0]
        u_ref[...] = u.astype(u_ref.dtype)

    grp = lambda i: (jnp.minimum(i // tps, n_samples), 0, 0)
    return _pallas(
        body, name="norm_mod_fwd", grid=(n_lat + 1,),
        in_specs=[pl.BlockSpec((TM_NORM, D), lambda i: (jnp.minimum(i, n_lat - 1), 0)),
                  pl.BlockSpec((TM_NORM, D), lambda i: (0, 0)),
                  pl.BlockSpec((1, 1, D), grp), pl.BlockSpec((1, 1, D), grp),
                  pl.BlockSpec((1, D), lambda i: (0, 0))],
        out_specs=pl.BlockSpec((TM_NORM, D), lambda i: (i, 0)),
        out_shape=jax.ShapeDtypeStruct((t + TM_NORM, D), BF16),
        compiler_params=_params(("parallel",)),
    )(x2, ctx2, scale1, shift, norm_g)


CONV_CB = 256
CONV_NCB = D // CONV_CB
H_OFF = 16


H_CB = 128
H_SPAN = GRID_W + 2 * H_OFF - 8


def _conv_scratch(vertical):
    if vertical:
        return [pltpu.VMEM((GRID_H + 2 * CONV_PAD, GRID_W, CONV_CB), F32)]
    return [pltpu.VMEM((GRID_H, GRID_W + 2 * H_OFF, H_CB), F32), pltpu.VMEM((7, GRID_H, H_SPAN, H_CB), F32)]


def _conv_fill(bufs, img, vertical):
    pad_ref = bufs[0]
    pad_ref[...] = jnp.zeros_like(pad_ref)
    if vertical:
        pad_ref[pl.ds(CONV_PAD, GRID_H)] = img
        return
    pad_ref[:, pl.ds(H_OFF, GRID_W), :] = img

    def shift(r, carry):
        for s in range(1, 8):
            bufs[1][s - 1, r] = pad_ref[r, pl.ds(s, H_SPAN), :]
        return carry

    lax.fori_loop(0, GRID_H, shift, 0)


def _conv_window(bufs, k, vertical, r, w0=0, nw=GRID_W, lanes=slice(None)):
    if vertical:
        return bufs[0][r + k, pl.ds(w0, nw), lanes]
    off = H_OFF - CONV_PAD + k
    if off % 8 == 0:
        return bufs[0][r, pl.ds(off + w0, nw), lanes]
    return bufs[1][off % 8 - 1, r, pl.ds(off - off % 8 + w0, nw), lanes]


def _conv_col_blocks(vertical):
    if vertical:
        return [pl.ds(0, CONV_CB)]
    return [pl.ds(i * H_CB, H_CB) for i in range(CONV_CB // H_CB)]


def _rows(r):
    return pl.ds(pl.multiple_of(r * GRID_W, GRID_W), GRID_W)


def conv_fwd(p1, conv_w, conv_b, n_samples):
    t = n_samples * SEQ

    def make(vertical, prev):
        n_buf = len(_conv_scratch(vertical))

        def body(gv_ref, gg_ref, w_ref, b_ref, *rest):
            o_ref, bufs = rest[-1 - n_buf], rest[-n_buf:]
            for cols in _conv_col_blocks(vertical):
                a = gv_ref[:, cols].astype(F32) * _sigmoid(gg_ref[:, cols].astype(F32))
                _conv_fill(bufs, a.reshape(GRID_H, GRID_W, a.shape[-1]), vertical)

                def row(r, carry, cols=cols):
                    acc = jnp.zeros((GRID_W, cols.size), F32) + b_ref[:, cols]
                    for k in range(CONV_K):
                        acc = acc + _conv_window(bufs, k, vertical, r) * w_ref[pl.ds(k, 1), cols]
                    o_ref[_rows(r), cols] = acc
                    return carry

                lax.fori_loop(0, GRID_H, row, 0)

        cb0 = CONV_NCB // 2 if vertical else 0
        in_specs = [pl.BlockSpec((SEQ, CONV_CB), lambda b, j: (b, 2 * (cb0 + j))),
                    pl.BlockSpec((SEQ, CONV_CB), lambda b, j: (b, 2 * (cb0 + j) + 1)),
                    pl.BlockSpec((CONV_K + 1, CONV_CB), lambda b, j: (0, cb0 + j)),
                    pl.BlockSpec((1, CONV_CB), lambda b, j: (0, cb0 + j))]
        args = [p1, p1, conv_w, conv_b]
        aliases = {}
        if prev is not None:
            in_specs.append(pl.BlockSpec(memory_space=pl.ANY))
            args.append(prev)
            aliases = {4: 0}
        return _pallas(
            body, name="conv_fwd_v" if vertical else "conv_fwd_h", grid=(n_samples, CONV_NCB // 2),
            in_specs=in_specs,
            out_specs=pl.BlockSpec((SEQ, CONV_CB), lambda b, j: (b, cb0 + j)),
            out_shape=jax.ShapeDtypeStruct((t, D), F32),
            scratch_shapes=_conv_scratch(vertical),
            input_output_aliases=aliases,
            compiler_params=_params(("parallel", "parallel")),
        )(*args)

    return make(True, make(False, None))


def conv_bwd(p1, daconv, conv_w, n_samples):
    t = n_samples * SEQ

    def make(vertical, prev):
        n_buf = len(_conv_scratch(vertical))

        def body(gv_ref, gg_ref, dy_ref, w_ref, *rest):
            dp_ref, dw_ref, db_ref = rest[-3 - 2 * n_buf - 1:-2 * n_buf - 1]
            a_bufs, d_bufs, da_ref = rest[-2 * n_buf - 1:-n_buf - 1], rest[-n_buf - 1:-1], rest[-1]
            for cols in _conv_col_blocks(vertical):
                width = cols.size
                gv = gv_ref[:, cols].astype(F32)
                sg = _sigmoid(gg_ref[:, cols].astype(F32))
                _conv_fill(a_bufs, (gv * sg).reshape(GRID_H, GRID_W, width), vertical)
                _conv_fill(d_bufs, dy_ref[:, cols].reshape(GRID_H, GRID_W, width), vertical)

                def row(r, carry, cols=cols, width=width):
                    acc = jnp.zeros((GRID_W, width), F32)
                    for k in range(CONV_K):
                        acc = acc + _conv_window(d_bufs, CONV_K - 1 - k, vertical, r) * w_ref[pl.ds(k, 1), cols]
                    da_ref[_rows(r), cols] = acc
                    return carry

                lax.fori_loop(0, GRID_H, row, 0)
                da = da_ref[:, cols]
                dp_ref[:, pl.ds(cols.start, width)] = (da * sg).astype(dp_ref.dtype)
                dp_ref[:, pl.ds(CONV_CB + cols.start, width)] = (da * gv * sg * (1.0 - sg)).astype(dp_ref.dtype)

                for lb in range(width // 128):
                    lanes = pl.ds(lb * 128, 128)
                    dy_lanes = pl.ds(cols.start + lb * 128, 128)

                    def wrow(r, accs, lanes=lanes, dy_lanes=dy_lanes):
                        for w0 in range(0, GRID_W, 8):
                            dyv = dy_ref[pl.ds(pl.multiple_of(r * GRID_W, GRID_W) + w0, 8), dy_lanes]
                            accs = tuple(accs[k] + _conv_window(a_bufs, k, vertical, r, w0, 8, lanes) * dyv
                                         for k in range(CONV_K))
                        return accs

                    accs = lax.fori_loop(0, GRID_H, wrow, tuple(jnp.zeros((8, 128), F32) for _ in range(CONV_K)))
                    for k in range(CONV_K):
                        dw_ref[0, pl.ds(k, 1), dy_lanes] = jnp.sum(accs[k], axis=0, keepdims=True)
            dw_ref[0, pl.ds(CONV_K, 1), :] = jnp.zeros((1, CONV_CB), F32)
            db_ref[0] = jnp.sum(dy_ref[...], axis=0, keepdims=True)

        cb0 = CONV_NCB // 2 if vertical else 0
        in_specs = [pl.BlockSpec((SEQ, CONV_CB), lambda b, j: (b, 2 * (cb0 + j))),
                    pl.BlockSpec((SEQ, CONV_CB), lambda b, j: (b, 2 * (cb0 + j) + 1)),
                    pl.BlockSpec((SEQ, CONV_CB), lambda b, j: (b, cb0 + j)),
                    pl.BlockSpec((CONV_K + 1, CONV_CB), lambda b, j: (0, cb0 + j))]
        args = [p1, p1, daconv, conv_w]
        aliases = {}
        if prev is not None:
            in_specs += [pl.BlockSpec(memory_space=pl.ANY)] * 3
            args += list(prev)
            aliases = {4: 0, 5: 1, 6: 2}
        return _pallas(
            body, name="conv_bwd_v" if vertical else "conv_bwd_h", grid=(n_samples, CONV_NCB // 2),
            in_specs=in_specs,
            out_specs=[pl.BlockSpec((SEQ, 2 * CONV_CB), lambda b, j: (b, cb0 + j)),
                       pl.BlockSpec((1, CONV_K + 1, CONV_CB), lambda b, j: (b, 0, cb0 + j)),
                       pl.BlockSpec((1, 1, CONV_CB), lambda b, j: (b, 0, cb0 + j))],
            out_shape=[jax.ShapeDtypeStruct((t, 2 * D), BF16),
                       jax.ShapeDtypeStruct((n_samples, CONV_K + 1, D), F32),
                       jax.ShapeDtypeStruct((n_samples, 1, D), F32)],
            scratch_shapes=_conv_scratch(vertical) + _conv_scratch(vertical) + [pltpu.VMEM((SEQ, CONV_CB), F32)],
            input_output_aliases=aliases,
            compiler_params=_params(("parallel", "parallel")),
        )(*args)

    return make(True, make(False, None))


TM_EW = 256


def ln_gate_fwd(aconv, z, ln_g, ln_b):
    t = aconv.shape[0]

    def body(a_ref, z_ref, g_ref, b_ref, o_ref):
        a = a_ref[...]
        mu = jnp.mean(a, axis=-1, keepdims=True)
        xc = a - mu
        rstd = lax.rsqrt(jnp.mean(xc * xc, axis=-1, keepdims=True) + EPS)
        l = xc * rstd * g_ref[...] + b_ref[...]
        o_ref[...] = (_silu(l) * _silu(z_ref[...].astype(F32))).astype(o_ref.dtype)

    row = pl.BlockSpec((TM_EW, D), lambda i: (i, 0))
    vec = pl.BlockSpec((1, D), lambda i: (0, 0))
    return _pallas(
        body, name="ln_gate_fwd", grid=(t // TM_EW,), in_specs=[row, row, vec, vec], out_specs=row,
        out_shape=jax.ShapeDtypeStruct((t, D), BF16), compiler_params=_params(("parallel",)),
    )(aconv, z, ln_g, ln_b)


def ln_gate_bwd(aconv, z, dyc, conv_proj, ln_g, ln_b):
    t = aconv.shape[0]

    def body(a_ref, z_ref, d_ref, w_ref, g_ref, b_ref, da_ref, dz_ref, dg_ref, db_ref):
        a = a_ref[...]
        zv = z_ref[...].astype(F32)
        dac_v = _nt(d_ref[...], w_ref[...])
        mu = jnp.mean(a, axis=-1, keepdims=True)
        xc = a - mu
        rstd = lax.rsqrt(jnp.mean(xc * xc, axis=-1, keepdims=True) + EPS)
        xh = xc * rstd
        l = xh * g_ref[...] + b_ref[...]
        dz_ref[...] = (dac_v * _silu(l) * _dsilu(zv)).astype(dz_ref.dtype)
        dl = dac_v * _silu(zv) * _dsilu(l)
        dxh = dl * g_ref[...]
        da_ref[...] = rstd * (dxh - jnp.mean(dxh, axis=-1, keepdims=True)
                              - xh * jnp.mean(dxh * xh, axis=-1, keepdims=True))

        @pl.when(pl.program_id(0) == 0)
        def _():
            dg_ref[...] = jnp.zeros_like(dg_ref)
            db_ref[...] = jnp.zeros_like(db_ref)

        dg_ref[...] += jnp.sum(dl * xh, axis=0, keepdims=True)
        db_ref[...] += jnp.sum(dl, axis=0, keepdims=True)

    row = pl.BlockSpec((TM_EW, D), lambda i: (i, 0))
    vec = pl.BlockSpec((1, D), lambda i: (0, 0))
    return _pallas(
        body, name="ln_gate_bwd", grid=(t // TM_EW,),
        in_specs=[row, row, row, pl.BlockSpec((D, D), lambda i: (0, 0)), vec, vec],
        out_specs=[row, row, vec, vec],
        out_shape=[jax.ShapeDtypeStruct((t, D), F32), jax.ShapeDtypeStruct((t, D), BF16),
                   jax.ShapeDtypeStruct((1, D), F32), jax.ShapeDtypeStruct((1, D), F32)],
        compiler_params=_params(("arbitrary",)),
    )(aconv, z, dyc, conv_proj, ln_g, ln_b)


TM_PREP = 256
PREP_LAT = SEQ // TM_PREP
PREP_ALL = SEQ_ALL // TM_PREP


def _chunk_tri(n, upper):
    r = lax.broadcasted_iota(jnp.int32, (n, n), 0)
    c = lax.broadcasted_iota(jnp.int32, (n, n), 1)
    same = (r // CHUNK) == (c // CHUNK)
    keep = (c >= r) if upper else (c <= r)
    return jnp.where(same & keep, 1.0, 0.0).astype(F32)


def _split3(v):
    hi = v.astype(BF16)
    r1 = v - hi.astype(F32)
    mid = r1.astype(BF16)
    lo = (r1 - mid.astype(F32)).astype(BF16)
    return jnp.stack([hi, mid, lo])


def _chunk_sums(v, upper):
    tri = _chunk_tri(v.shape[0], upper).astype(BF16)
    pieces = _split3(v)
    return (_nn(tri, pieces[0]) + _nn(tri, pieces[1])) + _nn(tri, pieces[2])


def _gate_logits(ab, up3_ref, bias_ref):
    assert ab.dtype == BF16
    return ((_nn(ab, up3_ref[0]) + _nn(ab, up3_ref[1])) + _nn(ab, up3_ref[2])) + bias_ref[...]


def _prep_tile_maps(n_samples):
    n_lat = n_samples * PREP_LAT

    def seq_map(i):
        return jnp.where(i < n_lat, i // PREP_LAT, i - n_lat), jnp.where(i < n_lat, i % PREP_LAT, PREP_LAT)

    return n_lat, seq_map


def gla_prep_fwd(p3, upf, upb, bias_f, bias_b, n_samples):
    n_lat, seq_map = _prep_tile_maps(n_samples)
    n_tiles = n_lat + n_samples

    def body(v_ref, q_ref, k_ref, ab_ref, upf_ref, upb_ref, bf_ref, bb_ref, qo, ko, vo, cf, cb):
        i = pl.program_id(0)
        qo[0] = jnp.where(i < n_lat, q_ref[...].astype(F32) * Q_SCALE, 0.0)
        ko[0] = k_ref[...]
        vo[0] = v_ref[...]
        ab = ab_ref[...]
        gf = _log_sigmoid(_gate_logits(ab, upf_ref, bf_ref)) * (1.0 / GATE_TAU)
        gb = _log_sigmoid(_gate_logits(ab, upb_ref, bb_ref)) * (1.0 / GATE_TAU)
        cf[0] = _chunk_sums(gf, False)
        cb[0] = _chunk_sums(gb, True)

    def o_spec(w):
        return pl.BlockSpec((1, TM_PREP, w), lambda i: (*seq_map(i), 0))

    full = lambda shape: pl.BlockSpec(shape, lambda i: (0,) * len(shape))
    return _pallas(
        body, name="gla_prep_fwd", grid=(n_tiles,),
        in_specs=[pl.BlockSpec((TM_PREP, 1024), lambda i: (i, O3_V // 1024)),
                  pl.BlockSpec((TM_PREP, 512), lambda i: (i, O3_Q // 512)),
                  pl.BlockSpec((TM_PREP, 512), lambda i: (i, O3_K // 512)),
                  pl.BlockSpec((TM_PREP, 128), lambda i: (i, O3_AB // 128)),
                  full((3, 128, GLA_DK)), full((3, 128, GLA_DK)), full((1, GLA_DK)), full((1, GLA_DK))],
        out_specs=[o_spec(GLA_DK), o_spec(GLA_DK), o_spec(D), o_spec(GLA_DK), o_spec(GLA_DK)],
        out_shape=[jax.ShapeDtypeStruct((n_samples, SEQ_ALL, GLA_DK), F32),
                   jax.ShapeDtypeStruct((n_samples, SEQ_ALL, GLA_DK), p3.dtype),
                   jax.ShapeDtypeStruct((n_samples, SEQ_ALL, D), p3.dtype),
                   jax.ShapeDtypeStruct((n_samples, SEQ_ALL, GLA_DK), F32),
                   jax.ShapeDtypeStruct((n_samples, SEQ_ALL, GLA_DK), F32)],
        compiler_params=_params(("parallel",)),
    )(p3, p3, p3, p3, upf, upb, bias_f, bias_b)


def gla_prep_bwd(p3, dq_f, dq_b, dk_f, dk_b, dv_f, dv_b, dc_f, dc_b, upf, upb, bias_f, bias_b, n_samples):
    n_lat, seq_map = _prep_tile_maps(n_samples)
    n_tiles = n_lat + n_samples

    def body(ab_ref, dqf, dqb, dkf, dkb, dvf, dvb, dcf, dcb, upf_ref, upb_ref, bf_ref, bb_ref,
             dp_ref, duf_ref, dub_ref, dbf_ref, dbb_ref):
        i = pl.program_id(0)
        dp_ref[:, pl.ds(O3_V, D)] = (dvf[0] + dvb[0]).astype(dp_ref.dtype)
        dq = jnp.where(i < n_lat, (dqf[0] + dqb[0]) * Q_SCALE, 0.0)
        dp_ref[:, pl.ds(O3_Q, GLA_DK)] = dq.astype(dp_ref.dtype)
        dp_ref[:, pl.ds(O3_K, GLA_DK)] = (dkf[0] + dkb[0]).astype(dp_ref.dtype)
        ab = ab_ref[...]
        zf = _gate_logits(ab, upf_ref, bf_ref)
        zb = _gate_logits(ab, upb_ref, bb_ref)
        dgf = _chunk_sums(dcf[0], True)
        dgb = _chunk_sums(dcb[0], False)
        dzf = _b16(dgf * (1.0 / GATE_TAU) * _sigmoid(-zf))
        dzb = _b16(dgb * (1.0 / GATE_TAU) * _sigmoid(-zb))
        dab = _nt(dzf, upf_ref[0]) + _nt(dzb, upb_ref[0])
        dp_ref[:, pl.ds(O3_AB, 128)] = dab.astype(dp_ref.dtype)

        @pl.when(i == 0)
        def _():
            duf_ref[...] = jnp.zeros_like(duf_ref)
            dub_ref[...] = jnp.zeros_like(dub_ref)
            dbf_ref[...] = jnp.zeros_like(dbf_ref)
            dbb_ref[...] = jnp.zeros_like(dbb_ref)

        duf_ref[...] += _tn(ab, dzf)
        dub_ref[...] += _tn(ab, dzb)
        dbf_ref[...] += jnp.sum(dzf.astype(F32), axis=0, keepdims=True)
        dbb_ref[...] += jnp.sum(dzb.astype(F32), axis=0, keepdims=True)

    def s_spec(w):
        return pl.BlockSpec((1, TM_PREP, w), lambda i: (*seq_map(i), 0))

    full = lambda shape: pl.BlockSpec(shape, lambda i: (0,) * len(shape))
    return _pallas(
        body, name="gla_prep_bwd", grid=(n_tiles,),
        in_specs=[pl.BlockSpec((TM_PREP, 128), lambda i: (i, O3_AB // 128)),
                  s_spec(GLA_DK), s_spec(GLA_DK), s_spec(GLA_DK), s_spec(GLA_DK), s_spec(D), s_spec(D),
                  s_spec(GLA_DK), s_spec(GLA_DK),
                  full((3, 128, GLA_DK)), full((3, 128, GLA_DK)), full((1, GLA_DK)), full((1, GLA_DK))],
        out_specs=[pl.BlockSpec((TM_PREP, W3), lambda i: (i, 0)),
                   full((128, GLA_DK)), full((128, GLA_DK)), full((1, GLA_DK)), full((1, GLA_DK))],
        out_shape=[jax.ShapeDtypeStruct((n_tiles * TM_PREP, W3), BF16),
                   jax.ShapeDtypeStruct((128, GLA_DK), F32), jax.ShapeDtypeStruct((128, GLA_DK), F32),
                   jax.ShapeDtypeStruct((1, GLA_DK), F32), jax.ShapeDtypeStruct((1, GLA_DK), F32)],
        compiler_params=_params(("arbitrary",)),
    )(p3, dq_f, dq_b, dk_f, dk_b, dv_f, dv_b, dc_f, dc_b, upf, upb, bias_f, bias_b)


def _sub_blocks(rev):
    if NSUB == 1:
        return [((0, CHUNK), CHUNK // 2, (0, CHUNK))]
    out = []
    for s in range(NSUB):
        rows = (s * SUB, SUB)
        if rev:
            ref = (s + 1) * SUB if s < NSUB - 1 else None
            cols = (s * SUB, CHUNK - s * SUB)
        else:
            ref = s * SUB - 1 if s > 0 else None
            cols = (0, (s + 1) * SUB)
        out.append((rows, ref, cols))
    return out


def _sub_mask(rows, cols, rev):
    r = rows[0] + lax.broadcasted_iota(jnp.int32, (rows[1], cols[1]), 0)
    c = cols[0] + lax.broadcasted_iota(jnp.int32, (rows[1], cols[1]), 1)
    return (c >= r) if rev else (c <= r)


def _sub_operands(qc, kc, cc, rows, ref, cols):
    cref = jnp.zeros((1, HEAD_K), F32) if ref is None else cc[ref:ref + 1]
    eq = jnp.exp(cc[rows[0]:rows[0] + rows[1]] - cref)
    ek = jnp.exp(cref - cc[cols[0]:cols[0] + cols[1]])
    qs = qc[rows[0]:rows[0] + rows[1]] * eq
    kk = kc[cols[0]:cols[0] + cols[1]] * ek
    return qs, kk, eq, ek


SCAN_ROWS = 256
SCAN_CHUNKS = SCAN_ROWS // CHUNK
SCAN_STEPS = SEQ_ALL // SCAN_ROWS
LAT_BLOCKS = SEQ // SCAN_ROWS


def _scan_block(t, rev):
    if rev:
        return SCAN_STEPS - 1 - t
    return jnp.where(t == 0, SCAN_STEPS - 1, t - 1)


def _scan_lat_block(t, rev):
    if rev:
        return jnp.minimum(SCAN_STEPS - 1 - t, LAT_BLOCKS - 1)
    return jnp.maximum(t - 1, 0)


def _head_cols(h):
    return pl.ds(h * HEAD_K, HEAD_K), pl.ds(h * HEAD_V, HEAD_V)


def gla_scan_fwd(q, k, v, cum, *, rev, name):
    n = q.shape[0]

    def body(q_ref, k_ref, v_ref, c_ref, o_ref, s_ref, sfin_ref, st):
        t = pl.program_id(1)

        @pl.when(t == 0)
        def _():
            st[...] = jnp.zeros_like(st)

        def chunk(j, carry):
            lj = SCAN_CHUNKS - 1 - j if rev else j
            r0 = lj * CHUNK
            rws = pl.ds(r0, CHUNK)
            for h in range(HEADS):
                kcols, vcols = _head_cols(h)
                qc, kc, cc = q_ref[0, rws, kcols], k_ref[0, rws, kcols], c_ref[0, rws, kcols]
                vc = v_ref[0, rws, vcols]
                s_in = st[h]
                s_ref[0, h, j] = _b16(s_in)
                edge = cc[0:1] if rev else cc[CHUNK - 1:CHUNK]
                ke = kc * jnp.exp(edge - cc)
                st[h] = s_in * jnp.exp(edge) + _tn(_b16(vc), _b16(ke))
                o_inter = _nt(_b16(qc * jnp.exp(cc)), _b16(s_in))
                vb = _b16(vc)
                for rows, ref, cols in _sub_blocks(rev):
                    qs, kk, _, _ = _sub_operands(qc, kc, cc, rows, ref, cols)
                    a = jnp.where(_sub_mask(rows, cols, rev), _nt(_b16(qs), _b16(kk)), 0.0)
                    o_s = _nn(_b16(a), vb[cols[0]:cols[0] + cols[1]])
                    o_ref[0, pl.ds(r0 + rows[0], rows[1]), vcols] = o_inter[rows[0]:rows[0] + rows[1]] + o_s
            return carry

        for j in range(SCAN_CHUNKS):
            chunk(j, 0)

        @pl.when(t == SCAN_STEPS - 1)
        def _():
            sfin_ref[0] = st[...]

    def spec(w):
        return pl.BlockSpec((1, SCAN_ROWS, w), lambda b, t: (b, _scan_block(t, rev), 0))

    return _pallas(
        body, name=name, grid=(n, SCAN_STEPS),
        in_specs=[spec(GLA_DK), spec(GLA_DK), spec(D), spec(GLA_DK)],
        out_specs=[pl.BlockSpec((1, SCAN_ROWS, D), lambda b, t: (b, _scan_lat_block(t, rev), 0)),
                   pl.BlockSpec((1, HEADS, SCAN_CHUNKS, HEAD_V, HEAD_K), lambda b, t: (b, 0, t, 0, 0)),
                   pl.BlockSpec((1, HEADS, HEAD_V, HEAD_K), lambda b, t: (b, 0, 0, 0))],
        out_shape=[jax.ShapeDtypeStruct((n, SEQ, D), F32),
                   jax.ShapeDtypeStruct((n, HEADS, NCHUNK, HEAD_V, HEAD_K), BF16),
                   jax.ShapeDtypeStruct((n, HEADS, HEAD_V, HEAD_K), F32)],
        scratch_shapes=[pltpu.VMEM((HEADS, HEAD_V, HEAD_K), F32)],
        compiler_params=_params(("parallel", "arbitrary")),
    )(q, k, v, cum)


def gla_scan_bwd(q, k, v, cum, s_all, s_fin, do, *, rev, name):
    n = q.shape[0]

    def body(q_ref, k_ref, v_ref, c_ref, s_ref, sfin_ref, do_ref, dq_ref, dk_ref, dv_ref, dc_ref,
             dst, s_next, dq_acc, dk_acc, dv_acc):
        t = SCAN_STEPS - 1 - pl.program_id(1)

        @pl.when(pl.program_id(1) == 0)
        def _():
            dst[...] = jnp.zeros_like(dst)
            s_next[...] = sfin_ref[0]

        def chunk(jj, carry):
            j = SCAN_CHUNKS - 1 - jj
            lj = SCAN_CHUNKS - 1 - j if rev else j
            rws = pl.ds(lj * CHUNK, CHUNK)
            for h in range(HEADS):
                kcols, vcols = _head_cols(h)
                qc, kc, cc = q_ref[0, rws, kcols], k_ref[0, rws, kcols], c_ref[0, rws, kcols]
                vc = v_ref[0, rws, vcols]
                doc = jnp.where(t > 0, do_ref[0, rws, vcols], 0.0)
                s_in = s_ref[0, h, j]
                s_out = s_next[h]
                ds_out = dst[h]
                edge = cc[0:1] if rev else cc[CHUNK - 1:CHUNK]
                e_q = jnp.exp(cc)
                e_k = jnp.exp(edge - cc)
                dob = _b16(doc)
                dsb = _b16(ds_out)
                dst[h] = ds_out * jnp.exp(edge) + _tn(dob, _b16(qc * e_q))
                s_next[h] = s_in.astype(F32)
                dq_acc[h] = e_q * _nn(dob, s_in)
                dk_acc[h] = e_k * _nn(_b16(vc), dsb)
                dv_acc[h] = _nt(_b16(kc * e_k), dsb)
                vb = _b16(vc)
                for rows, ref, cols in _sub_blocks(rev):
                    qs, kk, eq, ek = _sub_operands(qc, kc, cc, rows, ref, cols)
                    mask = _sub_mask(rows, cols, rev)
                    rsl = slice(rows[0], rows[0] + rows[1])
                    csl = pl.ds(cols[0], cols[1])
                    qsb, kkb = _b16(qs), _b16(kk)
                    a = jnp.where(mask, _nt(qsb, kkb), 0.0)
                    da = _b16(jnp.where(mask, _nt(dob[rsl], vb[cols[0]:cols[0] + cols[1]]), 0.0))
                    dq_acc[h, pl.ds(rows[0], rows[1]), :] += _nn(da, kkb) * eq
                    dk_acc[h, csl, :] += _tn(da, qsb) * ek
                    dv_acc[h, csl, :] += _tn(_b16(a), dob[rsl])
                dq = dq_acc[h]
                dk = dk_acc[h]
                dc = qc * dq - kc * dk
                bnd = jnp.sum(ds_out * s_out, axis=0, keepdims=True)
                edge_row = 0 if rev else CHUNK - 1
                is_edge = lax.broadcasted_iota(jnp.int32, (CHUNK, HEAD_K), 0) == edge_row
                dq_ref[0, rws, kcols] = dq
                dk_ref[0, rws, kcols] = dk
                dv_ref[0, rws, vcols] = dv_acc[h]
                dc_ref[0, rws, kcols] = dc + jnp.where(is_edge, bnd, 0.0)
            return carry

        for jj in range(SCAN_CHUNKS):
            chunk(jj, 0)

    def step_of(u):
        return SCAN_STEPS - 1 - u

    def spec(w):
        return pl.BlockSpec((1, SCAN_ROWS, w), lambda b, u: (b, _scan_block(step_of(u), rev), 0))

    return _pallas(
        body, name=name, grid=(n, SCAN_STEPS),
        in_specs=[spec(GLA_DK), spec(GLA_DK), spec(D), spec(GLA_DK),
                  pl.BlockSpec((1, HEADS, SCAN_CHUNKS, HEAD_V, HEAD_K), lambda b, u: (b, 0, step_of(u), 0, 0)),
                  pl.BlockSpec((1, HEADS, HEAD_V, HEAD_K), lambda b, u: (b, 0, 0, 0)),
                  pl.BlockSpec((1, SCAN_ROWS, D), lambda b, u: (b, _scan_lat_block(step_of(u), rev), 0))],
        out_specs=[spec(GLA_DK), spec(GLA_DK), spec(D), spec(GLA_DK)],
        out_shape=[jax.ShapeDtypeStruct((n, SEQ_ALL, GLA_DK), F32), jax.ShapeDtypeStruct((n, SEQ_ALL, GLA_DK), F32),
                   jax.ShapeDtypeStruct((n, SEQ_ALL, D), F32), jax.ShapeDtypeStruct((n, SEQ_ALL, GLA_DK), F32)],
        scratch_shapes=[pltpu.VMEM((HEADS, HEAD_V, HEAD_K), F32), pltpu.VMEM((HEADS, HEAD_V, HEAD_K), F32),
                        pltpu.VMEM((HEADS, CHUNK, HEAD_K), F32), pltpu.VMEM((HEADS, CHUNK, HEAD_K), F32),
                        pltpu.VMEM((HEADS, CHUNK, HEAD_V), F32)],
        compiler_params=_params(("parallel", "arbitrary")),
    )(q, k, v, cum, s_all, s_fin, do)


def gla_out_fwd(o_f, o_b, r, gnorm):
    n = o_f.shape[0]
    tiles = SEQ // TM_EW

    def body(of_ref, ob_ref, r_ref, g_ref, og_ref):
        for h in range(HEADS):
            cols = pl.ds(h * HEAD_V, HEAD_V)
            o = of_ref[0, :, cols] + ob_ref[0, :, cols]
            rs = lax.rsqrt(jnp.mean(o * o, axis=-1, keepdims=True) + EPS)
            og_ref[:, cols] = (o * rs * g_ref[...] * _silu(r_ref[:, cols].astype(F32))).astype(og_ref.dtype)

    ospec = pl.BlockSpec((1, TM_EW, D), lambda b, j: (b, j, 0))
    row = pl.BlockSpec((TM_EW, D), lambda b, j: (b * tiles + j, 0))
    return _pallas(
        body, name="gla_out_fwd", grid=(n, tiles),
        in_specs=[ospec, ospec, row, pl.BlockSpec((1, HEAD_V), lambda b, j: (0, 0))],
        out_specs=row, out_shape=jax.ShapeDtypeStruct((n * SEQ, D), BF16),
        compiler_params=_params(("parallel", "parallel")),
    )(o_f, o_b, r, gnorm)


def gla_out_bwd(o_f, o_b, r, dyg, gla_proj, gnorm):
    n = o_f.shape[0]
    tiles = SEQ // TM_EW

    def body(of_ref, ob_ref, r_ref, d_ref, w_ref, g_ref, do_ref, dr_ref, dg_ref, dog_buf):
        @pl.when((pl.program_id(0) == 0) & (pl.program_id(1) == 0))
        def _():
            dg_ref[...] = jnp.zeros_like(dg_ref)

        dog_buf[...] = _nt(d_ref[...], w_ref[...])
        for h in range(HEADS):
            cols = pl.ds(h * HEAD_V, HEAD_V)
            o = of_ref[0, :, cols] + ob_ref[0, :, cols]
            rv = r_ref[:, cols].astype(F32)
            dv = dog_buf[:, cols]
            rs = lax.rsqrt(jnp.mean(o * o, axis=-1, keepdims=True) + EPS)
            oh = o * rs
            dr_ref[:, cols] = (dv * oh * g_ref[...] * _dsilu(rv)).astype(dr_ref.dtype)
            dn = dv * _silu(rv)
            dg_ref[...] += jnp.sum(dn * oh, axis=0, keepdims=True)
            doh = dn * g_ref[...]
            do_ref[0, :, cols] = rs * (doh - oh * jnp.mean(doh * oh, axis=-1, keepdims=True))

    ospec = pl.BlockSpec((1, TM_EW, D), lambda b, j: (b, j, 0))
    row = pl.BlockSpec((TM_EW, D), lambda b, j: (b * tiles + j, 0))
    vec = pl.BlockSpec((1, HEAD_V), lambda b, j: (0, 0))
    return _pallas(
        body, name="gla_out_bwd", grid=(n, tiles),
        in_specs=[ospec, ospec, row, row, pl.BlockSpec((D, D), lambda b, j: (0, 0)), vec],
        out_specs=[ospec, row, vec],
        out_shape=[jax.ShapeDtypeStruct((n, SEQ, D), F32), jax.ShapeDtypeStruct((n * SEQ, D), BF16),
                   jax.ShapeDtypeStruct((1, HEAD_V), F32)],
        scratch_shapes=[pltpu.VMEM((TM_EW, D), F32)],
        compiler_params=_params(("arbitrary", "arbitrary")),
    )(o_f, o_b, r, dyg, gla_proj, gnorm)


TM_OUT = 512


def merge_out_final(p5, y_conv, y_gla, w_out, x2, gate, final_g, target, n_samples):
    t = x2.shape[0]
    tiles = SEQ // TM_OUT

    def body(mc_ref, mg_ref, yc_ref, yg_ref, w_ref, x_ref, gate_ref, g_ref, t_ref,
             mrg_ref, dh_ref, dmo_ref, dgate_ref, dg_ref, loss_ref):
        b, j = pl.program_id(0), pl.program_id(1)
        f = lambda ref: ref[...].astype(F32)
        merged = _b16(_sigmoid(f(mc_ref)) * f(yc_ref) + _sigmoid(f(mg_ref)) * f(yg_ref))
        mrg_ref[...] = merged
        mo_v = _nn(merged, w_ref[...])
        h = x_ref[...] + gate_ref[0] * mo_v
        rs = lax.rsqrt(jnp.mean(h * h, axis=-1, keepdims=True) + EPS)
        nh = h * rs
        err = nh * g_ref[...] - t_ref[...]
        dy = err * (1.0 / D)
        dn = dy * g_ref[...]
        dh = rs * (dn - nh * jnp.mean(dn * nh, axis=-1, keepdims=True))
        dh_ref[...] = dh
        dmo_ref[...] = (dh * gate_ref[0]).astype(dmo_ref.dtype)

        @pl.when(j == 0)
        def _():
            dgate_ref[...] = jnp.zeros_like(dgate_ref)

        @pl.when((b == 0) & (j == 0))
        def _():
            dg_ref[...] = jnp.zeros_like(dg_ref)
            loss_ref[...] = jnp.zeros_like(loss_ref)

        dgate_ref[0] += jnp.sum(dh * mo_v, axis=0, keepdims=True)
        dg_ref[...] += jnp.sum(dy * nh, axis=0, keepdims=True)
        loss_ref[...] += (0.5 / D) * jnp.sum(err * err)

    row = pl.BlockSpec((TM_OUT, D), lambda b, j: (b * tiles + j, 0))
    per = pl.BlockSpec((1, 1, D), lambda b, j: (b, 0, 0))
    vec = pl.BlockSpec((1, D), lambda b, j: (0, 0))
    return _pallas(
        body, name="merge_out_final", grid=(n_samples, tiles),
        in_specs=[row, pl.BlockSpec((TM_OUT, D), lambda b, j: (b * tiles + j, 1)), row, row,
                  pl.BlockSpec((D, D), lambda b, j: (0, 0)), row, per, vec, row],
        out_specs=[row, row, row, per, vec, pl.BlockSpec((8, 128), lambda b, j: (0, 0))],
        out_shape=[jax.ShapeDtypeStruct((t, D), BF16), jax.ShapeDtypeStruct((t, D), F32), jax.ShapeDtypeStruct((t, D), BF16),
                   jax.ShapeDtypeStruct((n_samples, 1, D), F32), jax.ShapeDtypeStruct((1, D), F32),
                   jax.ShapeDtypeStruct((8, 128), F32)],
        compiler_params=_params(("arbitrary", "arbitrary")),
    )(p5, p5, y_conv, y_gla, w_out, x2, gate, final_g, target)


def out_dgrad_merge_bwd(p5, y_conv, y_gla, dmo, w_out):
    t = y_conv.shape[0]

    def body(mc_ref, mg_ref, yc_ref, yg_ref, d_ref, w_ref, dyc_ref, dyg_ref, dp_ref):
        f = lambda ref: ref[...].astype(F32)
        d = _nt(d_ref[...], w_ref[...])
        sc = _sigmoid(f(mc_ref))
        sg = _sigmoid(f(mg_ref))
        dyc_ref[...] = (d * sc).astype(dyc_ref.dtype)
        dyg_ref[...] = (d * sg).astype(dyg_ref.dtype)
        dp_ref[:, pl.ds(0, D)] = (d * f(yc_ref) * sc * (1.0 - sc)).astype(dp_ref.dtype)
        dp_ref[:, pl.ds(D, D)] = (d * f(yg_ref) * sg * (1.0 - sg)).astype(dp_ref.dtype)

    row = pl.BlockSpec((TM_OUT, D), lambda i: (i, 0))
    return _pallas(
        body, name="out_dgrad_merge_bwd", grid=(t // TM_OUT,),
        in_specs=[row, pl.BlockSpec((TM_OUT, D), lambda i: (i, 1)), row, row, row, pl.BlockSpec((D, D), lambda i: (0, 0))],
        out_specs=[row, row, pl.BlockSpec((TM_OUT, 2 * D), lambda i: (i, 0))],
        out_shape=[jax.ShapeDtypeStruct((t, D), BF16), jax.ShapeDtypeStruct((t, D), BF16),
                   jax.ShapeDtypeStruct((t, 2 * D), BF16)],
        compiler_params=_params(("parallel",)),
    )(p5, p5, y_conv, y_gla, dmo, w_out)


def local_step(x, ctx, target, mod, wts, small, p_sh, chip):
    n = x.shape[0]
    t = n * SEQ
    t_all = t + n * NCTX
    x2 = x.reshape(t, D)
    ctx2 = ctx.reshape(n * NCTX, D)
    tgt2 = target.reshape(t, D)
    scale1, shift, gate = mod

    u = norm_mod_fwd(x2, ctx2, scale1, shift, small["norm_g"])
    p1 = matmul_nn(u, wts["w1"], small["b1"], name="proj_conv", m=t, tm=1024, tn=1024, out_dtype=BF16)
    p2 = matmul_nn(u, wts["w2"], small["b2"], name="proj_z", m=t, tm=1024, tn=1024, out_dtype=BF16)
    p3, p_all = matmul_nn(u, wts["w3"], small["b3"], name="proj_gla", m=t_all, tm=512, tn=W3, out_dtype=BF16,
                          gather=p_sh)
    p4 = matmul_nn(u, wts["w4"], small["b4"], name="proj_r", m=t, tm=1024, tn=1024, out_dtype=BF16)
    p5 = matmul_nn(u, wts["w5"], small["b5"], name="proj_merge", m=t, tm=1024, tn=1024, out_dtype=BF16)
    p_full = jnp.stack([jnp.where(chip == i, p_sh, p_all[i]) for i in range(N_CHIPS)])
    wts = dict(wts, conv_proj=p_full[:, 0:256].reshape(D, D), gla_proj=p_full[:, 256:512].reshape(D, D),
               w_out=p_full[:, 512:768].reshape(D, D))

    aconv = conv_fwd(p1, small["conv_w"], small["conv_b"], n)
    ac = ln_gate_fwd(aconv, p2, small["conv_ln_g"], small["conv_ln_b"])
    y_conv = matmul_nn(ac, wts["conv_proj"], None, name="conv_proj_fwd", m=t, tm=1024, tn=1024, out_dtype=BF16)

    qs, ks, vs, cum_f, cum_b = gla_prep_fwd(p3, small["upf"], small["upb"], small["bias_f"], small["bias_b"], n)
    o_f, s_f, sfin_f = gla_scan_fwd(qs, ks, vs, cum_f, rev=False, name="gla_scan_fwd_f")
    o_b, s_b, sfin_b = gla_scan_fwd(qs, ks, vs, cum_b, rev=True, name="gla_scan_fwd_b")
    og = gla_out_fwd(o_f, o_b, p4, small["gla_norm_g"])
    y_gla = matmul_nn(og, wts["gla_proj"], None, name="gla_proj_fwd", m=t, tm=1024, tn=1024, out_dtype=BF16)

    merged, dh, dmo, dgate, d_final_g, loss = merge_out_final(p5, y_conv, y_gla, wts["w_out"], x2, gate,
                                                              small["final_norm_g"], tgt2, n)

    g = {"final_norm_g": d_final_g}
    g["w_out"] = matmul_tn(merged, dmo, name="w_out_wgrad", t=t, tn=1024, tt=1024)[0]
    dyc, dyg, dp5 = out_dgrad_merge_bwd(p5, y_conv, y_gla, dmo, wts["w_out"])

    g["conv_proj"] = matmul_tn(ac, dyc, name="conv_proj_wgrad", t=t, tn=1024, tt=1024)[0]
    daconv, dp2, g["conv_ln_g"], g["conv_ln_b"] = ln_gate_bwd(aconv, p2, dyc, wts["conv_proj"], small["conv_ln_g"],
                                                               small["conv_ln_b"])
    dp1, dconv_w, dconv_b = conv_bwd(p1, daconv, small["conv_w"], n)
    g["conv_w"], g["conv_b"] = dconv_w, dconv_b

    g["gla_proj"] = matmul_tn(og, dyg, name="gla_proj_wgrad", t=t, tn=1024, tt=1024)[0]
    do, dp4, g["gla_norm_g"] = gla_out_bwd(o_f, o_b, p4, dyg, wts["gla_proj"], small["gla_norm_g"])
    dq_f, dk_f, dv_f, dc_f = gla_scan_bwd(qs, ks, vs, cum_f, s_f, sfin_f, do, rev=False, name="gla_scan_bwd_f")
    dq_b, dk_b, dv_b, dc_b = gla_scan_bwd(qs, ks, vs, cum_b, s_b, sfin_b, do, rev=True, name="gla_scan_bwd_b")
    dp3, g["upf"], g["upb"], g["bias_f"], g["bias_b"] = gla_prep_bwd(
        p3, dq_f, dq_b, dk_f, dk_b, dv_f, dv_b, dc_f, dc_b,
        small["upf"], small["upb"], small["bias_f"], small["bias_b"], n)

    dps = [dp1, dp2, dp3, dp4, dp5]
    g["proj"] = jnp.concatenate([g["conv_proj"].reshape(N_CHIPS, 256, D), g["gla_proj"].reshape(N_CHIPS, 256, D),
                                 g["w_out"].reshape(N_CHIPS, 256, D)], 1)
    got = {}
    for i in [0, 1, 3, 4, 2]:
        dp = dps[i]
        rows = dp.shape[0]
        tn = W3 if dp.shape[1] == W3 else 1024
        others = [j for j in range(5) if j != i]
        swap = ([g["w%d" % (j + 1)] for j in others], g["proj"]) if i == 2 else None
        outs = matmul_tn(u, dp, name="w_in_wgrad_%d" % (i + 1), t=rows, tn=tn, tt=1024 if rows % 1024 == 0 else 768,
                         colsum=True, swap=swap)
        g["w%d" % (i + 1)], g["b%d" % (i + 1)] = outs[0], outs[1]
        if swap is not None:
            got = dict(zip(others, outs[2:2 + len(others)]), proj=outs[2 + len(others)])
    g["gate"] = dgate
    return loss, dh, dps, g, got


def _group_cols(w):
    gv, gg, z = w[..., 0:1024], w[..., 1024:2048], w[..., 2048:3072]
    q, k, v = w[..., 3072:3584], w[..., 3584:4096], w[..., 4096:5120]
    ab = w[..., 5120:5152]
    r, mc, mg = w[..., 5152:6176], w[..., 6176:7200], w[..., 7200:8224]
    g1 = jnp.concatenate([p for j in range(CONV_NCB)
                          for p in (gv[..., CONV_CB * j:CONV_CB * (j + 1)], gg[..., CONV_CB * j:CONV_CB * (j + 1)])], -1)
    pad = jnp.zeros(w.shape[:-1] + (W3 - 2080,), w.dtype)
    g3 = jnp.concatenate([v, q, k, ab, pad], -1)
    return g1, z, g3, r, jnp.concatenate([mc, mg], -1)


def _ungroup_cols(g1, g2, g3, g4, g5):
    gv = jnp.concatenate([g1[..., 2 * CONV_CB * j:2 * CONV_CB * j + CONV_CB] for j in range(CONV_NCB)], -1)
    gg = jnp.concatenate([g1[..., 2 * CONV_CB * j + CONV_CB:2 * CONV_CB * (j + 1)] for j in range(CONV_NCB)], -1)
    v, q, k, ab = g3[..., 0:1024], g3[..., 1024:1536], g3[..., 1536:2048], g3[..., 2048:2080]
    return jnp.concatenate([gv, gg, g2, q, k, v, ab, g4, g5[..., 0:1024], g5[..., 1024:2048]], -1)


def _natural_pieces():
    pieces = [(CONV_CB * j, CONV_CB, 0, 2 * CONV_CB * j) for j in range(CONV_NCB)]
    pieces += [(1024 + CONV_CB * j, CONV_CB, 0, 2 * CONV_CB * j + CONV_CB) for j in range(CONV_NCB)]
    pieces += [(2048, 1024, 1, 0), (3072, 512, 2, O3_Q), (3584, 512, 2, O3_K), (4096, 1024, 2, O3_V), (5120, 32, 2, O3_AB),
               (5152, 1024, 3, 0), (6176, 1024, 4, 0), (7200, 1024, 4, 1024)]
    return sorted(pieces)


def _ungroup_to_shards(groups):
    shards = []
    for i in range(N_CHIPS):
        lo, hi = i * W_IN_SHARD, (i + 1) * W_IN_SHARD
        parts = []
        for nat, width, g, gcol in _natural_pieces():
            a, b = max(nat, lo), min(nat + width, hi)
            if a < b:
                parts.append(groups[g][:, gcol + a - nat:gcol + b - nat])
        shards.append(jnp.concatenate(parts, 1))
    return jnp.stack(shards)


def _pad_up(up, row0):
    return jnp.zeros((128, GLA_DK), F32).at[row0:row0 + up.shape[0]].set(up)


def _adamw_math(w, g, m, v):
    m = ADAM_B1 * m + (1.0 - ADAM_B1) * g
    v = ADAM_B2 * v + (1.0 - ADAM_B2) * (g * g)
    m_hat = m / (1.0 - ADAM_B1 ** ADAM_STEP)
    v_hat = v / (1.0 - ADAM_B2 ** ADAM_STEP)
    delta = -ADAM_LR * (m_hat / (jnp.sqrt(v_hat) + ADAM_EPS) + ADAM_WD * w)
    return delta, m, v


def adamw2d(w, g, m, v, *, name, tr, tcols=None):
    rows, cols = w.shape[-2:]

    def body(w_ref, g_ref, m_ref, v_ref, d_ref, nm_ref, nv_ref):
        d_ref[...], nm_ref[...], nv_ref[...] = _adamw_math(w_ref[...], g_ref[...], m_ref[...], v_ref[...])

    tcols = cols if tcols is None else tcols
    if w.ndim == 3:
        spec = pl.BlockSpec((1, tr, tcols), lambda i, j: (0, i, j))
    else:
        spec = pl.BlockSpec((tr, tcols), lambda i, j: (i, j))
    return _pallas(
        body, name=name, grid=(rows // tr, cols // tcols), in_specs=[spec] * 4, out_specs=[spec] * 3,
        out_shape=[jax.ShapeDtypeStruct(w.shape, F32)] * 3, compiler_params=_params(("parallel", "parallel")),
    )(w, g, m, v)


def adamw_many(ws, gs, ms, vs):
    k = len(ws)
    two = lambda a: a.reshape((-1, a.shape[-1]))

    def body(*refs):
        w_refs, g_refs, m_refs, v_refs = refs[:k], refs[k:2 * k], refs[2 * k:3 * k], refs[3 * k:4 * k]
        d_refs, nm_refs, nv_refs = refs[4 * k:5 * k], refs[5 * k:6 * k], refs[6 * k:7 * k]
        for i in range(k):
            d_refs[i][...], nm_refs[i][...], nv_refs[i][...] = _adamw_math(
                w_refs[i][...], g_refs[i][...], m_refs[i][...], v_refs[i][...])

    shapes = [jax.ShapeDtypeStruct(two(a).shape, F32) for a in ws]
    outs = _pallas(body, name="adamw_small", out_shape=shapes * 3, compiler_params=_params())(
        *[two(a) for a in ws], *[two(a) for a in gs], *[two(a) for a in ms], *[two(a) for a in vs])
    back = lambda lst: [o.reshape(a.shape) for o, a in zip(lst, ws)]
    return back(outs[:k]), back(outs[k:2 * k]), back(outs[2 * k:])


def sum_devices(sall, *, name):
    rows = sall.shape[1]

    def body(s_ref, o_ref):
        acc = s_ref[0]
        for d in range(1, N_DEV):
            acc = acc + s_ref[d]
        o_ref[...] = acc

    return _pallas(body, name=name, out_shape=jax.ShapeDtypeStruct((rows, D), F32),
                   compiler_params=_params())(sall)


def pair_add(core, g, got, *, name, tr):
    n, rows, cols = got.shape
    g4 = g.reshape(n, 2, rows, cols)

    def body(core_ref, g_ref, got_ref, ob_ref):
        del core_ref
        ob_ref[0] = (g_ref[0, 0] + got_ref[0]).astype(BF16)

    spec = pl.BlockSpec((1, tr, cols), lambda i, t, core_ref: (i, t, 0))
    return _pallas(
        body, name=name,
        grid_spec=pltpu.PrefetchScalarGridSpec(
            num_scalar_prefetch=1, grid=(n, rows // tr),
            in_specs=[pl.BlockSpec((1, 1, tr, cols), lambda i, t, core_ref: (i, core_ref[0], t, 0)), spec],
            out_specs=spec),
        out_shape=jax.ShapeDtypeStruct(got.shape, BF16),
        compiler_params=_params(("parallel", "parallel")))(core, g4, got)


def chip_add(place, pa, rb, *, name, tr):
    _, rows, cols = pa.shape

    def body(place_ref, m_ref, r_ref, o_ref):
        del place_ref
        o_ref[0] = ((m_ref[0].astype(F32) + r_ref[0].astype(F32)) + r_ref[1].astype(F32)) + r_ref[2].astype(F32)

    return _pallas(
        body, name=name,
        grid_spec=pltpu.PrefetchScalarGridSpec(
            num_scalar_prefetch=1, grid=(rows // tr,),
            in_specs=[pl.BlockSpec((1, tr, cols), lambda t, place_ref: (place_ref[0], t, 0)),
                      pl.BlockSpec((3, tr, cols), lambda t, place_ref: (0, t, 0))],
            out_specs=pl.BlockSpec((1, tr, cols), lambda t, place_ref: (place_ref[1], t, 0))),
        out_shape=jax.ShapeDtypeStruct((2, rows, cols), F32),
        compiler_params=_params(("parallel",)))(place, pa, rb)


def ada_bwd(call, cctx_rows, dm_shard, dm_full, adaw):
    nsh = adaw.shape[1]

    def body(c_ref, cc_ref, dms_ref, dmf_ref, w_ref, gw_ref, gb_ref, pq_ref):
        a_lat = _silu(c_ref[...])
        a_ctx = _silu(cc_ref[...])
        dms = dms_ref[...]
        gw_ref[...] = _tn(a_lat, dms[0:64], HI) + _tn(a_ctx, dms[64:72], HI)
        gb_ref[...] = jnp.sum(dmf_ref[...], axis=0, keepdims=True)
        part = _nt(dms[64:72], w_ref[...], HI)
        pq_ref[...] = jnp.zeros_like(pq_ref) + jnp.sum(part, axis=0, keepdims=True)

    return _pallas(body, name="ada_bwd",
                   out_shape=[jax.ShapeDtypeStruct((D, nsh), F32), jax.ShapeDtypeStruct((1, 3 * D), F32),
                              jax.ShapeDtypeStruct((8, D), F32)],
                   compiler_params=_params())(call, cctx_rows, dm_shard, dm_full, adaw)


def cctx_grad(pq_all, cctx_rows):
    def body(p_ref, c_ref, o_ref):
        acc = p_ref[0]
        for qi in range(1, N_CHIPS):
            acc = acc + p_ref[qi]
        o_ref[...] = acc * _dsilu(c_ref[...])

    return _pallas(body, name="cctx_grad", out_shape=jax.ShapeDtypeStruct((8, D), F32),
                   compiler_params=_params())(pq_all, cctx_rows)


def _place():
    x, y, c = lax.axis_index("x"), lax.axis_index("y"), lax.axis_index("c")
    chips = [(1 - x, y), (x, 1 - y), (1 - x, 1 - y)]
    return x, y, c, chips


def _all_peers(x, y, c):
    return [((1 - x) if r & 4 else x, (1 - y) if r & 2 else y, (1 - c) if r & 1 else c) for r in range(1, N_DEV)]


def _remote(src, dst, send_sem, recv_sem, dev):
    return pltpu.make_async_remote_copy(src_ref=src, dst_ref=dst, send_sem=send_sem, recv_sem=recv_sem,
                                        device_id=dev, device_id_type=MESH)


ANY = pl.BlockSpec(memory_space=pl.ANY)
VMEM = pl.BlockSpec(memory_space=pltpu.VMEM)
F_ROWS = 16


W_ROW_CHUNKS = 4
P_ROW_CHUNKS = 2
N_BULK = W_ROW_CHUNKS + P_ROW_CHUNKS


def _half_chunks(core, n_rows, align, which=(0, 1)):
    out = []
    for a, k in ((0, W_ROW_CHUNKS), (1, P_ROW_CHUNKS)):
        if a not in which:
            continue
        half = n_rows[a] // 2
        size = half // k
        for i in range(k):
            start = core * half + i * size
            out.append((a, pl.ds(start if isinstance(start, int) else pl.multiple_of(start, align), size)))
    return out


def gather_weights(c8, cctx8, adaw, adab, w_sh, fp):
    nsh = adaw.shape[1]

    def body(c_ref, cctx_ref, adaw_ref, adab_ref, w_ref, fp_ref, wall_ref, fall_ref, call_ref, mall_ref,
             abuf, w_send, w_recv, h_send, h_recv, c_send, c_recv, m_send, m_recv, f_send, f_recv):
        x, y, c, chips = _place()
        q = 2 * x + y
        dev = 4 * x + 2 * y + c
        qs = [2 * cx + cy for cx, cy in chips]
        sib = (x, y, 1 - c)
        srcs, dsts = (w_ref,), (wall_ref,)
        n_rows = (w_ref.shape[0],)
        mine = _half_chunks(c, n_rows, 16, which=(0,))
        other = _half_chunks(1 - c, n_rows, 16, which=(0,))

        bulk = [[_remote(srcs[a].at[rows], dsts[a].at[q, rows], w_send.at[j * N_BULK + i], w_recv.at[j * N_BULK + i],
                         (*chips[j], c)) for i, (a, rows) in enumerate(mine)] for j in range(3)]
        fall_ref[q] = fp_ref[...]
        small = [_remote(fp_ref, fall_ref.at[q], f_send.at[j], f_recv.at[j], (*chips[j], c)) for j in range(3)]
        my_rows = pl.ds(pl.multiple_of(8 * dev, 8), 8)
        call_ref[my_rows, :] = c_ref[...]
        cond = [_remote(c_ref, call_ref.at[my_rows, :], c_send.at[r], c_recv.at[r], peer)
                for r, peer in enumerate(_all_peers(x, y, c))]
        for cp in sum(bulk, []) + small + cond:
            cp.start()
        for cp in cond:
            cp.wait_recv()

        abuf[pl.ds(0, 64), :] = _silu(call_ref[...])
        abuf[pl.ds(64, 8), :] = _silu(cctx_ref[...])
        mall_ref[q] = _nn(abuf[...], adaw_ref[...], HI) + adab_ref[...]
        mod = [_remote(mall_ref.at[q], mall_ref.at[q], m_send.at[j], m_recv.at[j], (*chips[j], c)) for j in range(3)]
        for cp in mod:
            cp.start()

        handed = []
        for j in range(3):
            for i, (a, rows) in enumerate(mine):
                bulk[j][i].wait_recv()
                cp = _remote(dsts[a].at[qs[j], rows], dsts[a].at[qs[j], rows],
                             h_send.at[j * N_BULK + i], h_recv.at[j * N_BULK + i], sib)
                cp.start()
                handed.append(cp)
        for j in range(3):
            for i, (a, rows) in enumerate(other):
                _remote(dsts[a].at[qs[j], rows], dsts[a].at[qs[j], rows],
                        h_send.at[j * N_BULK + i], h_recv.at[j * N_BULK + i], sib).wait_recv()
        for cp in mod + small:
            cp.wait_recv()
        for cp in sum(bulk, []) + small + cond + mod + handed:
            cp.wait_send()

    def dma(n):
        return pltpu.SemaphoreType.DMA((n,))

    return _pallas(
        body, name="gather_weights",
        in_specs=[VMEM, VMEM, VMEM, VMEM, ANY, VMEM],
        out_specs=[ANY, VMEM, VMEM, VMEM],
        out_shape=[jax.ShapeDtypeStruct((N_CHIPS,) + w_sh.shape, BF16),
                   jax.ShapeDtypeStruct((N_CHIPS, F_ROWS, D), F32),
                   jax.ShapeDtypeStruct((8 * N_DEV, D), F32), jax.ShapeDtypeStruct((N_CHIPS, MOD_ROWS, nsh), F32)],
        scratch_shapes=[pltpu.VMEM((MOD_ROWS, D), F32), dma(3 * N_BULK), dma(3 * N_BULK), dma(3 * N_BULK), dma(3 * N_BULK),
                        dma(7), dma(7), dma(3), dma(3), dma(3), dma(3)],
        compiler_params=_params(),
    )(c8, cctx8, adaw, adab, w_sh, fp)


def _pair_count(gs, gp):
    return len(gs) * W_ROW_CHUNKS + (0 if gp is None else N_CHIPS * P_ROW_CHUNKS)


def _pair_got_shapes(gs, gp):
    shapes = [jax.ShapeDtypeStruct((D // 2, a.shape[1]), F32) for a in gs]
    if gp is not None:
        shapes.append(jax.ShapeDtypeStruct((N_CHIPS, gp.shape[1] // 2, gp.shape[2]), F32))
    return shapes


def _pair_copies(g_refs, gp_ref, got_refs, gotp_ref, a_send, a_recv):
    x, y, c, _ = _place()
    sib = (x, y, 1 - c)
    pair = []
    half, size = D // 2, D // 2 // W_ROW_CHUNKS
    for gi in range(len(g_refs)):
        for i in range(W_ROW_CHUNKS):
            k = len(pair)
            rows_o = pl.ds(pl.multiple_of((1 - c) * half + i * size, 8), size)
            pair.append(_remote(g_refs[gi].at[rows_o], got_refs[gi].at[pl.ds(i * size, size)],
                                a_send.at[k], a_recv.at[k], sib))
    if gp_ref is not None:
        half, size = gp_ref.shape[1] // 2, gp_ref.shape[1] // 2 // P_ROW_CHUNKS
        for s in range(N_CHIPS):
            for i in range(P_ROW_CHUNKS):
                k = len(pair)
                rows_o = pl.ds(pl.multiple_of((1 - c) * half + i * size, 8), size)
                pair.append(_remote(gp_ref.at[s, rows_o], gotp_ref.at[s, pl.ds(i * size, size)],
                                    a_send.at[k], a_recv.at[k], sib))
    return pair


def pair_swap(gs, sm):
    n_gs = len(gs)

    def body(*refs):
        g_refs, sm_ref = refs[:n_gs], refs[n_gs]
        got_refs, sall_ref = refs[n_gs + 1:2 * n_gs + 1], refs[2 * n_gs + 1]
        a_send, a_recv, s_send, s_recv = refs[2 * n_gs + 2:]
        x, y, c, _ = _place()
        dev = 4 * x + 2 * y + c
        pair = _pair_copies(g_refs, None, got_refs, None, a_send, a_recv)
        sall_ref[dev] = sm_ref[...]
        small = [_remote(sm_ref, sall_ref.at[dev], s_send.at[r], s_recv.at[r], peer)
                 for r, peer in enumerate(_all_peers(x, y, c))]
        for cp in pair + small:
            cp.start()
        for cp in small + pair:
            cp.wait_recv()
        for cp in small + pair:
            cp.wait_send()

    return _pallas(
        body, name="pair_swap", in_specs=[ANY] * n_gs + [VMEM], out_specs=[ANY] * n_gs + [VMEM],
        out_shape=_pair_got_shapes(gs, None) + [jax.ShapeDtypeStruct((N_DEV,) + sm.shape, F32)],
        scratch_shapes=[pltpu.SemaphoreType.DMA((_pair_count(gs, None),)), pltpu.SemaphoreType.DMA((_pair_count(gs, None),)),
                        pltpu.SemaphoreType.DMA((N_DEV - 1,)), pltpu.SemaphoreType.DMA((N_DEV - 1,))],
        compiler_params=_params(),
    )(*gs, sm)


def gather_small(sm):
    rows = sm.shape[0]

    def body(sm_ref, sall_ref, s_send, s_recv):
        x, y, c, _ = _place()
        dev = 4 * x + 2 * y + c
        sall_ref[dev] = sm_ref[...]
        small = [_remote(sm_ref, sall_ref.at[dev], s_send.at[r], s_recv.at[r], peer)
                 for r, peer in enumerate(_all_peers(x, y, c))]
        for cp in small:
            cp.start()
        for cp in small:
            cp.wait_recv()
        for cp in small:
            cp.wait_send()

    return _pallas(
        body, name="gather_small", in_specs=[VMEM], out_specs=VMEM,
        out_shape=jax.ShapeDtypeStruct((N_DEV, rows, D), F32),
        scratch_shapes=[pltpu.SemaphoreType.DMA((7,)), pltpu.SemaphoreType.DMA((7,))],
        compiler_params=_params(),
    )(sm)


def pair_share(ghw, ghp, pq):
    def body(ghw_ref, ghp_ref, pq_ref, outw_ref, outp_ref, pqa_ref, send, recv, p_send, p_recv):
        del ghw_ref, ghp_ref
        x, y, c, chips = _place()
        q = 2 * x + y
        refs = (outw_ref, outp_ref)
        n_rows = (2 * outw_ref.shape[1], 2 * outp_ref.shape[1])
        pair = [_remote(refs[a].at[c, rows], refs[a].at[c, rows], send.at[i], recv.at[i], (x, y, 1 - c))
                for i, (a, rows) in enumerate(_half_chunks(0, n_rows, 8))]
        pqa_ref[q] = pq_ref[...]
        small = [_remote(pq_ref, pqa_ref.at[q], p_send.at[j], p_recv.at[j], (*chips[j], c)) for j in range(3)]
        for cp in pair + small:
            cp.start()
        for i, (a, rows) in enumerate(_half_chunks(0, n_rows, 8)):
            _remote(refs[a].at[1 - c, rows], refs[a].at[1 - c, rows], send.at[i], recv.at[i], (x, y, 1 - c)).wait_recv()
        for cp in small:
            cp.wait_recv()
        for cp in pair + small:
            cp.wait_send()

    return _pallas(
        body, name="pair_share", in_specs=[ANY, ANY, VMEM], out_specs=[ANY, ANY, VMEM],
        out_shape=[jax.ShapeDtypeStruct(ghw.shape, F32), jax.ShapeDtypeStruct(ghp.shape, F32),
                   jax.ShapeDtypeStruct((N_CHIPS, 8, D), F32)],
        scratch_shapes=[pltpu.SemaphoreType.DMA((N_BULK,)), pltpu.SemaphoreType.DMA((N_BULK,)),
                        pltpu.SemaphoreType.DMA((3,)), pltpu.SemaphoreType.DMA((3,))],
        input_output_aliases={0: 0, 1: 1},
        compiler_params=_params(),
    )(ghw, ghp, pq)


def _rows_of(shape):
    size = 1
    for s in shape:
        size *= s
    return -(-size // D)


def _pack(arrs, rows_multiple=8):
    parts = []
    total = 0
    for a in arrs:
        f = a.reshape(-1).astype(F32)
        r = _rows_of(a.shape)
        parts.append(jnp.pad(f, (0, r * D - f.shape[0])))
        total += r
    pad_rows = (-total) % rows_multiple
    if pad_rows:
        parts.append(jnp.zeros((pad_rows * D,), F32))
    return jnp.concatenate(parts).reshape(-1, D)


def _unpack(p, shapes):
    out = []
    r0 = 0
    for shp in shapes:
        r = _rows_of(shp)
        size = 1
        for s in shp:
            size *= s
        out.append(p[r0:r0 + r].reshape(-1)[:size].reshape(shp))
        r0 += r
    return out


WEIGHT_NAMES = ['c_ctx', 'ada_w', 'ada_b', 'norm_g', 'w_in', 'b_in', 'conv_w', 'conv_b', 'conv_ln_g', 'conv_ln_b',
                'conv_proj', 'decay_up_fwd', 'decay_bias_fwd', 'decay_up_bwd', 'decay_bias_bwd', 'gla_norm_g', 'gla_proj',
                'w_out', 'final_norm_g']
SMALL_NAMES = ['c_ctx', 'ada_b', 'norm_g', 'b_in', 'conv_w', 'conv_b', 'conv_ln_g', 'conv_ln_b', 'decay_up_fwd',
               'decay_bias_fwd', 'decay_up_bwd', 'decay_bias_bwd', 'gla_norm_g', 'final_norm_g']


def kernel(x, c, ctx, c_ctx, ada_w, ada_b, norm_g, w_in, b_in, conv_w, conv_b, conv_ln_g, conv_ln_b, conv_proj, decay_up_fwd, decay_bias_fwd, decay_up_bwd, decay_bias_bwd, gla_norm_g, gla_proj, w_out, final_norm_g, loss_target, m_c_ctx, m_ada_w, m_ada_b, m_norm_g, m_w_in, m_b_in, m_conv_w, m_conv_b, m_conv_ln_g, m_conv_ln_b, m_conv_proj, m_decay_up_fwd, m_decay_bias_fwd, m_decay_up_bwd, m_decay_bias_bwd, m_gla_norm_g, m_gla_proj, m_w_out, m_final_norm_g, v_c_ctx, v_ada_w, v_ada_b, v_norm_g, v_w_in, v_b_in, v_conv_w, v_conv_b, v_conv_ln_g, v_conv_ln_b, v_conv_proj, v_decay_up_fwd, v_decay_bias_fwd, v_decay_up_bwd, v_decay_bias_bwd, v_gla_norm_g, v_gla_proj, v_w_out, v_final_norm_g):
    w = dict(c_ctx=c_ctx, ada_w=ada_w, ada_b=ada_b, norm_g=norm_g, w_in=w_in, b_in=b_in, conv_w=conv_w, conv_b=conv_b,
             conv_ln_g=conv_ln_g, conv_ln_b=conv_ln_b, conv_proj=conv_proj, decay_up_fwd=decay_up_fwd,
             decay_bias_fwd=decay_bias_fwd, decay_up_bwd=decay_up_bwd, decay_bias_bwd=decay_bias_bwd,
             gla_norm_g=gla_norm_g, gla_proj=gla_proj, w_out=w_out, final_norm_g=final_norm_g)
    m = dict(c_ctx=m_c_ctx, ada_w=m_ada_w, ada_b=m_ada_b, norm_g=m_norm_g, w_in=m_w_in, b_in=m_b_in, conv_w=m_conv_w,
             conv_b=m_conv_b, conv_ln_g=m_conv_ln_g, conv_ln_b=m_conv_ln_b, conv_proj=m_conv_proj,
             decay_up_fwd=m_decay_up_fwd, decay_bias_fwd=m_decay_bias_fwd, decay_up_bwd=m_decay_up_bwd,
             decay_bias_bwd=m_decay_bias_bwd, gla_norm_g=m_gla_norm_g, gla_proj=m_gla_proj, w_out=m_w_out,
             final_norm_g=m_final_norm_g)
    v = dict(c_ctx=v_c_ctx, ada_w=v_ada_w, ada_b=v_ada_b, norm_g=v_norm_g, w_in=v_w_in, b_in=v_b_in, conv_w=v_conv_w,
             conv_b=v_conv_b, conv_ln_g=v_conv_ln_g, conv_ln_b=v_conv_ln_b, conv_proj=v_conv_proj,
             decay_up_fwd=v_decay_up_fwd, decay_bias_fwd=v_decay_bias_fwd, decay_up_bwd=v_decay_up_bwd,
             decay_bias_bwd=v_decay_bias_bwd, gla_norm_g=v_gla_norm_g, gla_proj=v_gla_proj, w_out=v_w_out,
             final_norm_g=v_final_norm_g)
    n = x.shape[0]
    ax, ay, ac = lax.axis_index("x"), lax.axis_index("y"), lax.axis_index("c")
    q = 2 * ax + ay
    dev = 4 * ax + 2 * ay + ac
    nsh = ada_w.shape[2]

    w_sh = w_in[0].astype(BF16)
    p_sh = jnp.concatenate([conv_proj[0], gla_proj[0], w_out[0]], 0).astype(BF16)
    fp = _pack([conv_w[0], decay_up_fwd[0], decay_up_bwd[0]], F_ROWS)
    c8 = jnp.pad(c, ((0, 8 - n), (0, 0)))
    cctx8 = jnp.pad(c_ctx[None], ((0, 7), (0, 0)))
    adab_sh = lax.dynamic_slice(ada_b, (0, q * nsh), (1, nsh))
    w_all, fall, call, mall = gather_weights(c8, cctx8, ada_w[0], adab_sh, w_sh, fp)

    mod_all = jnp.transpose(mall, (1, 0, 2)).reshape(MOD_ROWS, 3 * D)
    mod_mine = lax.dynamic_slice(mod_all, (8 * dev, 0), (n, 3 * D))
    mod_ctx = mod_all[64:65]
    shift = jnp.concatenate([mod_mine[:, 0:D], mod_ctx[:, 0:D]], 0)[:, None, :]
    scale1 = 1.0 + jnp.concatenate([mod_mine[:, D:2 * D], mod_ctx[:, D:2 * D]], 0)[:, None, :]
    gate = mod_mine[:, 2 * D:3 * D][:, None, :]

    own = lambda i, mine, got: jnp.where(q == i, mine, got)
    g1, g2, g3, g4, g5 = _group_cols(jnp.concatenate([own(i, w_sh, w_all[i]) for i in range(N_CHIPS)], 1))
    wts = dict(w1=g1, w2=g2, w3=g3, w4=g4, w5=g5)
    f_parts = [_unpack(fall[i], [conv_w.shape[1:], decay_up_fwd.shape[1:], decay_up_bwd.shape[1:]]) for i in range(N_CHIPS)]
    conv_w_full = jnp.concatenate([p[0] for p in f_parts], 1)
    upf_full = jnp.concatenate([p[1] for p in f_parts], 1)
    upb_full = jnp.concatenate([p[2] for p in f_parts], 1)
    b1, b2, b3, b4, b5 = _group_cols(b_in)
    small = dict(b1=b1, b2=b2, b3=b3, b4=b4, b5=b5, norm_g=norm_g,
                 conv_w=jnp.pad(conv_w_full, ((0, 1), (0, 0))), conv_b=conv_b, conv_ln_g=conv_ln_g, conv_ln_b=conv_ln_b,
                 upf=_split3(_pad_up(upf_full, 0)), upb=_split3(_pad_up(upb_full, 16)),
                 bias_f=decay_bias_fwd, bias_b=decay_bias_bwd,
                 gla_norm_g=gla_norm_g, final_norm_g=final_norm_g[None])

    loss_part, dh, dps, g, got = local_step(x, ctx, loss_target, (scale1, shift, gate), wts, small, p_sh, q)
    loss = lax.psum(loss_part[0, 0], ("x", "y", "c"))

    gs = [g["w%d" % i] for i in range(1, 6)]
    d_b_in = _ungroup_cols(*[g["b%d" % i] for i in range(1, 6)])
    early = [d_b_in, g["conv_b"].sum(0), g["conv_ln_g"], g["conv_ln_b"], g["bias_f"], g["bias_b"],
             g["gla_norm_g"], g["final_norm_g"], g["conv_w"].sum(0)[:CONV_K], g["upf"][0:16], g["upb"][16:32]]
    early_shapes = [a.shape for a in early]
    got[2], sall1 = pair_swap([gs[2]], _pack(early))
    core = ac.astype(jnp.int32).reshape(1)
    chip = q.astype(jnp.int32).reshape(1)
    halves = [pair_add(core, gs[i][None], got[i][None], name="pair_add_w%d" % (i + 1), tr=128)[0]
              for i in range(5)]
    paw16 = _ungroup_to_shards(halves)
    pap16 = pair_add(core, g["proj"], got["proj"], name="pair_add_p", tr=384)
    grad_x2, dshift, dscale, g["norm_g"], rbw, rbp = dgrad_norm_bwd(
        dps, [wts["w%d" % i] for i in range(1, 6)], paw16, pap16, x.reshape(n * SEQ, D), ctx.reshape(n * NCTX, D), dh,
        scale1, norm_g, tm=256)

    dm_mine = jnp.concatenate([dshift[:n, 0], dscale[:n, 0], g["gate"][:, 0]], -1)
    dm_ctx = jnp.concatenate([dshift[n, 0], dscale[n, 0], jnp.zeros((D,), F32)], -1)
    late = [g["norm_g"], dm_mine, dm_ctx]
    late_shapes = [a.shape for a in late]
    sall2 = gather_small(_pack(late))
    (s_b_in, s_conv_b, s_ln_g, s_ln_b, s_bias_f, s_bias_b, s_gla_g, s_final_g, s_conv_w, s_upf,
     s_upb) = _unpack(sum_devices(sall1, name="sum_devices_early"), early_shapes)
    s_norm_g = _unpack(sum_devices(sall2, name="sum_devices_late"), late_shapes)[0]
    r_mine, r_ctx = 1, 1 + 3 * n
    dm_all = sall2[:, r_mine:r_ctx].reshape(N_DEV, n, 3 * D)
    dm_full = jnp.concatenate([jnp.pad(dm_all, ((0, 0), (0, 8 - n), (0, 0))).reshape(8 * N_DEV, 3 * D),
                               sall2[:, r_ctx:r_ctx + 3].reshape(N_DEV, 3 * D)], 0)
    dm_shard = lax.dynamic_slice(dm_full, (0, q * nsh), (MOD_ROWS, nsh))
    cctx_rows = jnp.broadcast_to(c_ctx[None], (8, D))
    g_ada_w, g_ada_b, pq = ada_bwd(call, cctx_rows, dm_shard, dm_full, ada_w[0])

    place = jnp.concatenate([chip, core])
    ghw = chip_add(place, paw16, rbw, name="chip_add_w", tr=128)
    ghp = chip_add(place, pap16, rbp, name="chip_add_p", tr=384)
    gw_mine, gp_mine, pq_all = pair_share(ghw, ghp, pq)
    gp_mine = gp_mine.reshape(768, D)
    g_c_ctx = cctx_grad(pq_all, cctx_rows)[0]

    grads = dict(
        c_ctx=g_c_ctx, ada_w=g_ada_w[None], ada_b=g_ada_b, norm_g=s_norm_g,
        w_in=gw_mine.reshape(1, D, W_IN_SHARD), b_in=s_b_in,
        conv_w=lax.dynamic_slice(s_conv_w, (0, q * 256), (CONV_K, 256))[None], conv_b=s_conv_b,
        conv_ln_g=s_ln_g, conv_ln_b=s_ln_b, conv_proj=gp_mine[0:256][None],
        decay_up_fwd=lax.dynamic_slice(s_upf, (0, q * 128), (16, 128))[None], decay_bias_fwd=s_bias_f,
        decay_up_bwd=lax.dynamic_slice(s_upb, (0, q * 128), (16, 128))[None], decay_bias_bwd=s_bias_b,
        gla_norm_g=s_gla_g, gla_proj=gp_mine[256:512][None], w_out=gp_mine[512:768][None],
        final_norm_g=s_final_g[0])

    delta, new_m, new_v = {}, {}, {}
    for name in ["ada_w", "conv_proj", "gla_proj", "w_out"]:
        delta[name], new_m[name], new_v[name] = adamw2d(w[name], grads[name].reshape(w[name].shape), m[name], v[name],
                                                        name="adamw_" + name, tr=128)
    tr_ = lambda a: jnp.swapaxes(a, 1, 2)
    g_w_in_t = tr_(grads["w_in"])
    grads["w_in"] = tr_(g_w_in_t)
    d_, m_, v_ = adamw2d(tr_(w_in), g_w_in_t, tr_(m_w_in), tr_(v_w_in), name="adamw_w_in", tr=W_IN_SHARD, tcols=128)
    delta["w_in"], new_m["w_in"], new_v["w_in"] = tr_(d_), tr_(m_), tr_(v_)
    d_, m_, v_ = adamw_many([w[nm] for nm in SMALL_NAMES], [grads[nm].reshape(w[nm].shape) for nm in SMALL_NAMES],
                            [m[nm] for nm in SMALL_NAMES], [v[nm] for nm in SMALL_NAMES])
    for nm, a, b, cc in zip(SMALL_NAMES, d_, m_, v_):
        delta[nm], new_m[nm], new_v[nm] = a, b, cc

    grad_x = grad_x2.reshape(x.shape)
    return (loss, grad_x, *[grads[nm].reshape(w[nm].shape) for nm in WEIGHT_NAMES], *[delta[nm] for nm in WEIGHT_NAMES],
            *[new_m[nm] for nm in WEIGHT_NAMES], *[new_v[nm] for nm in WEIGHT_NAMES])
```

```python
import jax
import jax.numpy as jnp
from jax import lax
from jax.experimental import pallas as pl
from jax.experimental.pallas import tpu as pltpu

F32 = jnp.float32
BF16 = jnp.bfloat16
MESH = pl.DeviceIdType.MESH
HI = lax.Precision.HIGHEST

D = 1024
SEQ = 2048
GRID_W = 64
GRID_H = SEQ // GRID_W
NCTX = 256
SEQ_ALL = SEQ + NCTX
EPS = 1e-6
CONV_K = 31
CONV_PAD = CONV_K // 2
HEADS = 4
HEAD_K = 128
HEAD_V = 256
GLA_DK = HEADS * HEAD_K
GATE_TAU = 16.0
Q_SCALE = HEAD_K ** -0.5
CHUNK = 64
NCHUNK = SEQ_ALL // CHUNK
NCHUNK_LAT = SEQ // CHUNK
NCHUNK_CTX = NCHUNK - NCHUNK_LAT
SUB = 64
NSUB = CHUNK // SUB
N_IN = 8224
W3 = 2176
O3_V, O3_Q, O3_K, O3_AB = 0, 1024, 1536, 2048

ADAM_LR, ADAM_B1, ADAM_B2, ADAM_EPS, ADAM_WD, ADAM_STEP = 0.001, 0.9, 0.999, 1e-08, 0.01, 10
VMEM_LIMIT = 56 * 1024 * 1024

N_CHIPS = 4
N_DEV = 8
W_IN_SHARD = N_IN // N_CHIPS
MOD_ROWS = 72


def _pallas(body, **kw):
    return pl.pallas_call(body, **kw)


def _params(sem=None, **kw):
    if sem is not None:
        kw["dimension_semantics"] = sem
    return pltpu.CompilerParams(vmem_limit_bytes=VMEM_LIMIT, **kw)


def _sigmoid(v):
    return 1.0 / (1.0 + jnp.exp(-v))


def _silu(v):
    return v * _sigmoid(v)


def _dsilu(v):
    s = _sigmoid(v)
    return s * (1.0 + v * (1.0 - s))


def _log_sigmoid(v):
    return jnp.minimum(v, 0.0) - jnp.log(1.0 + jnp.exp(-jnp.abs(v)))


def _dot(a, b, dims, precision=None):
    return lax.dot_general(a, b, (dims, ((), ())), preferred_element_type=F32, precision=precision)


def _nn(a, b, precision=None):
    return _dot(a, b, ((1,), (0,)), precision)


def _nt(a, b, precision=None):
    return _dot(a, b, ((1,), (1,)), precision)


def _tn(a, b, precision=None):
    return _dot(a, b, ((0,), (0,)), precision)


def _b16(v):
    return v.astype(BF16)


def matmul_nn(a, b, bias, *, name, m, tm, tn, out_dtype, gather=None):
    k = a.shape[1]
    n = b.shape[1]
    has_bias = bias is not None
    nj, ni = n // tn, m // tm
    n_in = 2 + int(has_bias) + int(gather is not None)

    def body(*refs):
        a_ref, b_ref, o_ref = refs[0], refs[1], refs[n_in]
        acc = _nn(a_ref[...], b_ref[...])
        if has_bias:
            acc = acc + refs[2][...]
        o_ref[...] = acc.astype(o_ref.dtype)

        if gather is not None:
            p_ref, pall_ref = refs[n_in - 1], refs[n_in + 1]
            w_send, w_recv, h_send, h_recv = refs[n_in + 2:]
            j, i = pl.program_id(0), pl.program_id(1)
            x, y, c, chips = _place()
            q = 2 * x + y
            mine = _half_chunks(c, (0, p_ref.shape[0]), 16, which=(1,))
            other = _half_chunks(1 - c, (0, p_ref.shape[0]), 16, which=(1,))
            nb = len(mine)

            def bulk():
                return [[_remote(p_ref.at[rows], pall_ref.at[q, rows], w_send.at[pj * nb + pi], w_recv.at[pj * nb + pi],
                                 (*chips[pj], c)) for pi, (_, rows) in enumerate(mine)] for pj in range(3)]

            @pl.when((j == 0) & (i == 0))
            def _():
                for cp in sum(bulk(), []):
                    cp.start()

            @pl.when((j == nj - 1) & (i == ni - 1))
            def _():
                handed = []
                for pj, (cx, cy) in enumerate(chips):
                    for pi, (_, rows) in enumerate(mine):
                        bulk()[pj][pi].wait_recv()
                        cp = _remote(pall_ref.at[2 * cx + cy, rows], pall_ref.at[2 * cx + cy, rows],
                                     h_send.at[pj * nb + pi], h_recv.at[pj * nb + pi], (x, y, 1 - c))
                        cp.start()
                        handed.append(cp)
                for pj, (cx, cy) in enumerate(chips):
                    for pi, (_, rows) in enumerate(other):
                        _remote(pall_ref.at[2 * cx + cy, rows], pall_ref.at[2 * cx + cy, rows],
                                h_send.at[pj * nb + pi], h_recv.at[pj * nb + pi], (x, y, 1 - c)).wait_recv()
                for cp in sum(bulk(), []) + handed:
                    cp.wait_send()

    in_specs = [pl.BlockSpec((tm, k), lambda j, i: (i, 0)), pl.BlockSpec((k, tn), lambda j, i: (0, j))]
    args = [a, b]
    if has_bias:
        in_specs.append(pl.BlockSpec((1, tn), lambda j, i: (0, j)))
        args.append(bias)
    out_specs = [pl.BlockSpec((tm, tn), lambda j, i: (i, j))]
    out_shape = [jax.ShapeDtypeStruct((m, n), out_dtype)]
    scratch = []
    if gather is not None:
        any_spec = pl.BlockSpec(memory_space=pl.ANY)
        in_specs.append(any_spec)
        args.append(gather)
        out_specs.append(any_spec)
        out_shape.append(jax.ShapeDtypeStruct((N_CHIPS,) + gather.shape, gather.dtype))
        scratch = [pltpu.SemaphoreType.DMA((3 * P_ROW_CHUNKS,)) for _ in range(4)]
    outs = _pallas(
        body, name=name, grid=(nj, ni), in_specs=in_specs, out_specs=out_specs, out_shape=out_shape,
        scratch_shapes=scratch,
        compiler_params=_params(("parallel", "parallel") if gather is None else ("arbitrary", "arbitrary")),
    )(*args)
    return outs[0] if gather is None else outs


def matmul_tn(a, b, *, name, t, tn, tt, colsum=False, swap=None):
    m = a.shape[1]
    n = b.shape[1]
    nj, ns = n // tn, t // tt
    n_out = 2 if colsum else 1
    n_sw = 0 if swap is None else len(swap)

    def body(a_ref, b_ref, *rest):
        o_ref = rest[n_sw]
        cs_ref = rest[n_sw + 1] if colsum else None
        j, s = pl.program_id(0), pl.program_id(1)

        if swap is not None:
            g_refs = rest[:n_sw]
            got_refs = rest[n_sw + n_out:2 * n_sw + n_out]
            sems = rest[2 * n_sw + n_out:]

            @pl.when((j == 0) & (s == 0))
            def _():
                for cp in _pair_copies(g_refs, None, got_refs, None, *sems):
                    cp.start()

            @pl.when((j == nj - 1) & (s == ns - 1))
            def _():
                for cp in _pair_copies(g_refs, None, got_refs, None, *sems):
                    cp.wait_recv()
                for cp in _pair_copies(g_refs, None, got_refs, None, *sems):
                    cp.wait_send()

        @pl.when(s == 0)
        def _():
            o_ref[...] = jnp.zeros_like(o_ref)
            if colsum:
                cs_ref[...] = jnp.zeros_like(cs_ref)
        o_ref[...] += _tn(a_ref[...], b_ref[...])
        if colsum:
            cs_ref[...] += jnp.sum(b_ref[...].astype(F32), axis=0, keepdims=True)

    in_specs = [pl.BlockSpec((tt, m), lambda j, s: (s, 0)), pl.BlockSpec((tt, tn), lambda j, s: (s, j))]
    out_specs = [pl.BlockSpec((m, tn), lambda j, s: (0, j))]
    out_shape = [jax.ShapeDtypeStruct((m, n), F32)]
    if colsum:
        out_specs.append(pl.BlockSpec((1, tn), lambda j, s: (0, j)))
        out_shape.append(jax.ShapeDtypeStruct((1, n), F32))
    args, scratch = [a, b], []
    if swap is not None:
        any_spec = pl.BlockSpec(memory_space=pl.ANY)
        in_specs += [any_spec] * n_sw
        out_specs += [any_spec] * n_sw
        out_shape += _pair_got_shapes(swap, None)
        args += list(swap)
        scratch = [pltpu.SemaphoreType.DMA((_pair_count(swap, None),)), pltpu.SemaphoreType.DMA((_pair_count(swap, None),))]
    return _pallas(
        body, name=name, grid=(nj, ns), in_specs=in_specs, out_specs=out_specs, out_shape=out_shape,
        scratch_shapes=scratch,
        compiler_params=_params(("parallel" if swap is None else "arbitrary", "arbitrary")),
    )(*args)


def dgrad_norm_bwd(dps, wts, paw, x2, ctx2, dh, scale1, norm_g, *, tm):
    t, tc = x2.shape[0], ctx2.shape[0]
    t_all = t + tc
    n_lat, n_ctx = t // tm, tc // tm
    n_tiles = n_lat + n_ctx
    n_samples = scale1.shape[0] - 1
    tps = n_lat // n_samples
    n_grp = n_samples + 1
    n_g = len(dps)
    whole = [g for g in range(n_g) if dps[g].shape[0] == t_all]
    latent = [g for g in range(n_g) if dps[g].shape[0] != t_all]

    def body(*refs):
        dp_refs, w_refs = refs[:n_g], refs[n_g:2 * n_g]
        (paw_ref, x_ref, c_ref, dh_ref, sc_ref, g_ref, dx_ref, dsh_ref, dsc_ref, dg_ref, rbw_ref,
         du_buf, b_send, b_recv) = refs[2 * n_g:]
        i = pl.program_id(0)

        def exchange():
            x, y, c, chips = _place()
            chunks = _half_chunks(0, (2 * paw_ref.shape[1],), 16, which=(0,))
            return [_remote(paw_ref.at[2 * cx + cy, rows], rbw_ref.at[j, rows],
                            b_send.at[j * N_BULK + k], b_recv.at[j * N_BULK + k], (cx, cy, c))
                    for j, (cx, cy) in enumerate(chips) for k, (_, rows) in enumerate(chunks)]

        @pl.when(i == 0)
        def _():
            for cp in exchange():
                cp.start()

        acc = None
        for g in whole:
            part = _nt(dp_refs[g][...], w_refs[g][...])
            acc = part if acc is None else acc + part
        du_buf[...] = acc

        @pl.when(i < n_lat)
        def _():
            lat = None
            for g in latent:
                part = _nt(dp_refs[g][...], w_refs[g][...])
                lat = part if lat is None else lat + part
            du_buf[...] += lat

        duv = du_buf[...]
        xv = jnp.where(i < n_lat, x_ref[...], c_ref[...])
        rs = lax.rsqrt(jnp.mean(xv * xv, axis=-1, keepdims=True) + EPS)
        xh = xv * rs
        n = xh * g_ref[...]
        dn = duv * sc_ref[0]
        dxh = dn * g_ref[...]
        dx = rs * (dxh - xh * jnp.mean(dxh * xh, axis=-1, keepdims=True))

        @pl.when(i < n_lat)
        def _():
            dx_ref[...] = dx + dh_ref[...]

        @pl.when((i % tps == 0) & (i <= n_lat))
        def _():
            dsh_ref[...] = jnp.zeros_like(dsh_ref)
            dsc_ref[...] = jnp.zeros_like(dsc_ref)

        @pl.when(i == 0)
        def _():
            dg_ref[...] = jnp.zeros_like(dg_ref)

        dsh_ref[0] += jnp.sum(duv, axis=0, keepdims=True)
        dsc_ref[0] += jnp.sum(duv * n, axis=0, keepdims=True)
        dg_ref[...] += jnp.sum(dn * xh, axis=0, keepdims=True)

        @pl.when(i == n_tiles - 1)
        def _():
            for cp in exchange():
                cp.wait_recv()
            for cp in exchange():
                cp.wait_send()

    lat = lambda i: (jnp.minimum(i, n_lat - 1), 0)
    grp = lambda i: (jnp.minimum(i // tps, n_samples), 0, 0)
    in_specs = []
    for g, dp in enumerate(dps):
        nrow = dp.shape[0] // tm
        in_specs.append(pl.BlockSpec((tm, dp.shape[1]), lambda i, nrow=nrow: (jnp.minimum(i, nrow - 1), 0)))
    for w in wts:
        in_specs.append(pl.BlockSpec(w.shape, lambda i: (0, 0), pipeline_mode=pl.Buffered(1)))
    any_spec = pl.BlockSpec(memory_space=pl.ANY)
    in_specs += [any_spec,
                 pl.BlockSpec((tm, D), lat), pl.BlockSpec((tm, D), lambda i: (jnp.maximum(i - n_lat, 0), 0)),
                 pl.BlockSpec((tm, D), lat), pl.BlockSpec((1, 1, D), grp), pl.BlockSpec((1, D), lambda i: (0, 0))]
    return _pallas(
        body, name="dgrad_norm_bwd", grid=(n_tiles,), in_specs=in_specs,
        out_specs=[pl.BlockSpec((tm, D), lat), pl.BlockSpec((1, 1, D), grp), pl.BlockSpec((1, 1, D), grp),
                   pl.BlockSpec((1, D), lambda i: (0, 0)), any_spec],
        out_shape=[jax.ShapeDtypeStruct((t, D), F32), jax.ShapeDtypeStruct((n_grp, 1, D), F32),
                   jax.ShapeDtypeStruct((n_grp, 1, D), F32), jax.ShapeDtypeStruct((1, D), F32),
                   jax.ShapeDtypeStruct((3,) + paw.shape[1:], paw.dtype)],
        scratch_shapes=[pltpu.VMEM((tm, D), F32), pltpu.SemaphoreType.DMA((3 * N_BULK,)),
                        pltpu.SemaphoreType.DMA((3 * N_BULK,))],
        compiler_params=_params(("arbitrary",)),
    )(*dps, *wts, paw, x2, ctx2, dh, scale1, norm_g)


TM_NORM = 512


def norm_mod_fwd(x2, ctx2, scale1, shift, norm_g):
    t = x2.shape[0]
    n_lat = t // TM_NORM
    assert ctx2.shape[0] == TM_NORM
    n_samples = scale1.shape[0] - 1
    tps = n_lat // n_samples

    def body(x_ref, c_ref, sc_ref, sh_ref, g_ref, u_ref):
        i = pl.program_id(0)
        xv = jnp.where(i < n_lat, x_ref[...], c_ref[...])
        rs = lax.rsqrt(jnp.mean(xv * xv, axis=-1, keepdims=True) + EPS)
        u = xv * rs * g_ref[...] * sc_ref[0] + sh_ref[0]
        u_ref[...] = u.astype(u_ref.dtype)

    grp = lambda i: (jnp.minimum(i // tps, n_samples), 0, 0)
    return _pallas(
        body, name="norm_mod_fwd", grid=(n_lat + 1,),
        in_specs=[pl.BlockSpec((TM_NORM, D), lambda i: (jnp.minimum(i, n_lat - 1), 0)),
                  pl.BlockSpec((TM_NORM, D), lambda i: (0, 0)),
                  pl.BlockSpec((1, 1, D), grp), pl.BlockSpec((1, 1, D), grp),
                  pl.BlockSpec((1, D), lambda i: (0, 0))],
        out_specs=pl.BlockSpec((TM_NORM, D), lambda i: (i, 0)),
        out_shape=jax.ShapeDtypeStruct((t + TM_NORM, D), BF16),
        compiler_params=_params(("parallel",)),
    )(x2, ctx2, scale1, shift, norm_g)


CONV_CB = 256
CONV_NCB = D // CONV_CB
H_OFF = 16


H_CB = 128
H_SPAN = GRID_W + 2 * H_OFF - 8


def _conv_scratch(vertical):
    if vertical:
        return [pltpu.VMEM((GRID_H + 2 * CONV_PAD, GRID_W, CONV_CB), F32)]
    return [pltpu.VMEM((GRID_H, GRID_W + 2 * H_OFF, H_CB), F32), pltpu.VMEM((7, GRID_H, H_SPAN, H_CB), F32)]


def _conv_fill(bufs, img, vertical):
    pad_ref = bufs[0]
    pad_ref[...] = jnp.zeros_like(pad_ref)
    if vertical:
        pad_ref[pl.ds(CONV_PAD, GRID_H)] = img
        return
    pad_ref[:, pl.ds(H_OFF, GRID_W), :] = img

    def shift(r, carry):
        for s in range(1, 8):
            bufs[1][s - 1, r] = pad_ref[r, pl.ds(s, H_SPAN), :]
        return carry

    lax.fori_loop(0, GRID_H, shift, 0)


def _conv_window(bufs, k, vertical, r, w0=0, nw=GRID_W, lanes=slice(None)):
    if vertical:
        return bufs[0][r + k, pl.ds(w0, nw), lanes]
    off = H_OFF - CONV_PAD + k
    if off % 8 == 0:
        return bufs[0][r, pl.ds(off + w0, nw), lanes]
    return bufs[1][off % 8 - 1, r, pl.ds(off - off % 8 + w0, nw), lanes]


def _conv_col_blocks(vertical):
    if vertical:
        return [pl.ds(0, CONV_CB)]
    return [pl.ds(i * H_CB, H_CB) for i in range(CONV_CB // H_CB)]


def _rows(r):
    return pl.ds(pl.multiple_of(r * GRID_W, GRID_W), GRID_W)


def conv_fwd(p1, conv_w, conv_b, n_samples):
    t = n_samples * SEQ

    def make(vertical, prev):
        n_buf = len(_conv_scratch(vertical))

        def body(gv_ref, gg_ref, w_ref, b_ref, *rest):
            o_ref, bufs = rest[-1 - n_buf], rest[-n_buf:]
            for cols in _conv_col_blocks(vertical):
                a = gv_ref[:, cols].astype(F32) * _sigmoid(gg_ref[:, cols].astype(F32))
                _conv_fill(bufs, a.reshape(GRID_H, GRID_W, a.shape[-1]), vertical)

                def row(r, carry, cols=cols):
                    acc = jnp.zeros((GRID_W, cols.size), F32) + b_ref[:, cols]
                    for k in range(CONV_K):
                        acc = acc + _conv_window(bufs, k, vertical, r) * w_ref[pl.ds(k, 1), cols]
                    o_ref[_rows(r), cols] = acc
                    return carry

                lax.fori_loop(0, GRID_H, row, 0)

        cb0 = CONV_NCB // 2 if vertical else 0
        in_specs = [pl.BlockSpec((SEQ, CONV_CB), lambda b, j: (b, 2 * (cb0 + j))),
                    pl.BlockSpec((SEQ, CONV_CB), lambda b, j: (b, 2 * (cb0 + j) + 1)),
                    pl.BlockSpec((CONV_K + 1, CONV_CB), lambda b, j: (0, cb0 + j)),
                    pl.BlockSpec((1, CONV_CB), lambda b, j: (0, cb0 + j))]
        args = [p1, p1, conv_w, conv_b]
        aliases = {}
        if prev is not None:
            in_specs.append(pl.BlockSpec(memory_space=pl.ANY))
            args.append(prev)
            aliases = {4: 0}
        return _pallas(
            body, name="conv_fwd_v" if vertical else "conv_fwd_h", grid=(n_samples, CONV_NCB // 2),
            in_specs=in_specs,
            out_specs=pl.BlockSpec((SEQ, CONV_CB), lambda b, j: (b, cb0 + j)),
            out_shape=jax.ShapeDtypeStruct((t, D), F32),
            scratch_shapes=_conv_scratch(vertical),
            input_output_aliases=aliases,
            compiler_params=_params(("parallel", "parallel")),
        )(*args)

    return make(True, make(False, None))


def conv_bwd(p1, daconv, conv_w, n_samples):
    t = n_samples * SEQ

    def make(vertical, prev):
        n_buf = len(_conv_scratch(vertical))

        def body(gv_ref, gg_ref, dy_ref, w_ref, *rest):
            dp_ref, dw_ref, db_ref = rest[-3 - 2 * n_buf - 1:-2 * n_buf - 1]
            a_bufs, d_bufs, da_ref = rest[-2 * n_buf - 1:-n_buf - 1], rest[-n_buf - 1:-1], rest[-1]
            for cols in _conv_col_blocks(vertical):
                width = cols.size
                gv = gv_ref[:, cols].astype(F32)
                sg = _sigmoid(gg_ref[:, cols].astype(F32))
                _conv_fill(a_bufs, (gv * sg).reshape(GRID_H, GRID_W, width), vertical)
                _conv_fill(d_bufs, dy_ref[:, cols].reshape(GRID_H, GRID_W, width), vertical)

                def row(r, carry, cols=cols, width=width):
                    acc = jnp.zeros((GRID_W, width), F32)
                    for k in range(CONV_K):
                        acc = acc + _conv_window(d_bufs, CONV_K - 1 - k, vertical, r) * w_ref[pl.ds(k, 1), cols]
                    da_ref[_rows(r), cols] = acc
                    return carry

                lax.fori_loop(0, GRID_H, row, 0)
                da = da_ref[:, cols]
                dp_ref[:, pl.ds(cols.start, width)] = (da * sg).astype(dp_ref.dtype)
                dp_ref[:, pl.ds(CONV_CB + cols.start, width)] = (da * gv * sg * (1.0 - sg)).astype(dp_ref.dtype)

                for lb in range(width // 128):
                    lanes = pl.ds(lb * 128, 128)
                    dy_lanes = pl.ds(cols.start + lb * 128, 128)

                    def wrow(r, accs, lanes=lanes, dy_lanes=dy_lanes):
                        for w0 in range(0, GRID_W, 8):
                            dyv = dy_ref[pl.ds(pl.multiple_of(r * GRID_W, GRID_W) + w0, 8), dy_lanes]
                            accs = tuple(accs[k] + _conv_window(a_bufs, k, vertical, r, w0, 8, lanes) * dyv
                                         for k in range(CONV_K))
                        return accs

                    accs = lax.fori_loop(0, GRID_H, wrow, tuple(jnp.zeros((8, 128), F32) for _ in range(CONV_K)))
                    for k in range(CONV_K):
                        dw_ref[0, pl.ds(k, 1), dy_lanes] = jnp.sum(accs[k], axis=0, keepdims=True)
            dw_ref[0, pl.ds(CONV_K, 1), :] = jnp.zeros((1, CONV_CB), F32)
            db_ref[0] = jnp.sum(dy_ref[...], axis=0, keepdims=True)

        cb0 = CONV_NCB // 2 if vertical else 0
        in_specs = [pl.BlockSpec((SEQ, CONV_CB), lambda b, j: (b, 2 * (cb0 + j))),
                    pl.BlockSpec((SEQ, CONV_CB), lambda b, j: (b, 2 * (cb0 + j) + 1)),
                    pl.BlockSpec((SEQ, CONV_CB), lambda b, j: (b, cb0 + j)),
                    pl.BlockSpec((CONV_K + 1, CONV_CB), lambda b, j: (0, cb0 + j))]
        args = [p1, p1, daconv, conv_w]
        aliases = {}
        if prev is not None:
            in_specs += [pl.BlockSpec(memory_space=pl.ANY)] * 3
            args += list(prev)
            aliases = {4: 0, 5: 1, 6: 2}
        return _pallas(
            body, name="conv_bwd_v" if vertical else "conv_bwd_h", grid=(n_samples, CONV_NCB // 2),
            in_specs=in_specs,
            out_specs=[pl.BlockSpec((SEQ, 2 * CONV_CB), lambda b, j: (b, cb0 + j)),
                       pl.BlockSpec((1, CONV_K + 1, CONV_CB), lambda b, j: (b, 0, cb0 + j)),
                       pl.BlockSpec((1, 1, CONV_CB), lambda b, j: (b, 0, cb0 + j))],
            out_shape=[jax.ShapeDtypeStruct((t, 2 * D), BF16),
                       jax.ShapeDtypeStruct((n_samples, CONV_K + 1, D), F32),
                       jax.ShapeDtypeStruct((n_samples, 1, D), F32)],
            scratch_shapes=_conv_scratch(vertical) + _conv_scratch(vertical) + [pltpu.VMEM((SEQ, CONV_CB), F32)],
            input_output_aliases=aliases,
            compiler_params=_params(("parallel", "parallel")),
        )(*args)

    return make(True, make(False, None))


TM_EW = 256


def ln_gate_fwd(aconv, z, ln_g, ln_b):
    t = aconv.shape[0]

    def body(a_ref, z_ref, g_ref, b_ref, o_ref):
        a = a_ref[...]
        mu = jnp.mean(a, axis=-1, keepdims=True)
        xc = a - mu
        rstd = lax.rsqrt(jnp.mean(xc * xc, axis=-1, keepdims=True) + EPS)
        l = xc * rstd * g_ref[...] + b_ref[...]
        o_ref[...] = (_silu(l) * _silu(z_ref[...].astype(F32))).astype(o_ref.dtype)

    row = pl.BlockSpec((TM_EW, D), lambda i: (i, 0))
    vec = pl.BlockSpec((1, D), lambda i: (0, 0))
    return _pallas(
        body, name="ln_gate_fwd", grid=(t // TM_EW,), in_specs=[row, row, vec, vec], out_specs=row,
        out_shape=jax.ShapeDtypeStruct((t, D), BF16), compiler_params=_params(("parallel",)),
    )(aconv, z, ln_g, ln_b)


def ln_gate_bwd(aconv, z, dyc, conv_proj, ln_g, ln_b):
    t = aconv.shape[0]

    def body(a_ref, z_ref, d_ref, w_ref, g_ref, b_ref, da_ref, dz_ref, dg_ref, db_ref):
        a = a_ref[...]
        zv = z_ref[...].astype(F32)
        dac_v = _nt(d_ref[...], w_ref[...])
        mu = jnp.mean(a, axis=-1, keepdims=True)
        xc = a - mu
        rstd = lax.rsqrt(jnp.mean(xc * xc, axis=-1, keepdims=True) + EPS)
        xh = xc * rstd
        l = xh * g_ref[...] + b_ref[...]
        dz_ref[...] = (dac_v * _silu(l) * _dsilu(zv)).astype(dz_ref.dtype)
        dl = dac_v * _silu(zv) * _dsilu(l)
        dxh = dl * g_ref[...]
        da_ref[...] = rstd * (dxh - jnp.mean(dxh, axis=-1, keepdims=True)
                              - xh * jnp.mean(dxh * xh, axis=-1, keepdims=True))

        @pl.when(pl.program_id(0) == 0)
        def _():
            dg_ref[...] = jnp.zeros_like(dg_ref)
            db_ref[...] = jnp.zeros_like(db_ref)

        dg_ref[...] += jnp.sum(dl * xh, axis=0, keepdims=True)
        db_ref[...] += jnp.sum(dl, axis=0, keepdims=True)

    row = pl.BlockSpec((TM_EW, D), lambda i: (i, 0))
    vec = pl.BlockSpec((1, D), lambda i: (0, 0))
    return _pallas(
        body, name="ln_gate_bwd", grid=(t // TM_EW,),
        in_specs=[row, row, row, pl.BlockSpec((D, D), lambda i: (0, 0)), vec, vec],
        out_specs=[row, row, vec, vec],
        out_shape=[jax.ShapeDtypeStruct((t, D), F32), jax.ShapeDtypeStruct((t, D), BF16),
                   jax.ShapeDtypeStruct((1, D), F32), jax.ShapeDtypeStruct((1, D), F32)],
        compiler_params=_params(("arbitrary",)),
    )(aconv, z, dyc, conv_proj, ln_g, ln_b)


TM_PREP = 256
PREP_LAT = SEQ // TM_PREP
PREP_ALL = SEQ_ALL // TM_PREP


def _chunk_tri(n, upper):
    r = lax.broadcasted_iota(jnp.int32, (n, n), 0)
    c = lax.broadcasted_iota(jnp.int32, (n, n), 1)
    same = (r // CHUNK) == (c // CHUNK)
    keep = (c >= r) if upper else (c <= r)
    return jnp.where(same & keep, 1.0, 0.0).astype(F32)


def _split3(v):
    hi = v.astype(BF16)
    r1 = v - hi.astype(F32)
    mid = r1.astype(BF16)
    lo = (r1 - mid.astype(F32)).astype(BF16)
    return jnp.stack([hi, mid, lo])


def _chunk_sums(v, upper):
    tri = _chunk_tri(v.shape[0], upper).astype(BF16)
    pieces = _split3(v)
    return (_nn(tri, pieces[0]) + _nn(tri, pieces[1])) + _nn(tri, pieces[2])


def _gate_logits(ab, up3_ref, bias_ref):
    assert ab.dtype == BF16
    return ((_nn(ab, up3_ref[0]) + _nn(ab, up3_ref[1])) + _nn(ab, up3_ref[2])) + bias_ref[...]


def _prep_tile_maps(n_samples):
    n_lat = n_samples * PREP_LAT

    def seq_map(i):
        return jnp.where(i < n_lat, i // PREP_LAT, i - n_lat), jnp.where(i < n_lat, i % PREP_LAT, PREP_LAT)

    return n_lat, seq_map


def gla_prep_fwd(p3, upf, upb, bias_f, bias_b, n_samples):
    n_lat, seq_map = _prep_tile_maps(n_samples)
    n_tiles = n_lat + n_samples

    def body(v_ref, q_ref, k_ref, ab_ref, upf_ref, upb_ref, bf_ref, bb_ref, qo, ko, vo, cf, cb):
        i = pl.program_id(0)
        qo[0] = jnp.where(i < n_lat, q_ref[...].astype(F32) * Q_SCALE, 0.0)
        ko[0] = k_ref[...]
        vo[0] = v_ref[...]
        ab = ab_ref[...]
        gf = _log_sigmoid(_gate_logits(ab, upf_ref, bf_ref)) * (1.0 / GATE_TAU)
        gb = _log_sigmoid(_gate_logits(ab, upb_ref, bb_ref)) * (1.0 / GATE_TAU)
        cf[0] = _chunk_sums(gf, False)
        cb[0] = _chunk_sums(gb, True)

    def o_spec(w):
        return pl.BlockSpec((1, TM_PREP, w), lambda i: (*seq_map(i), 0))

    full = lambda shape: pl.BlockSpec(shape, lambda i: (0,) * len(shape))
    return _pallas(
        body, name="gla_prep_fwd", grid=(n_tiles,),
        in_specs=[pl.BlockSpec((TM_PREP, 1024), lambda i: (i, O3_V // 1024)),
                  pl.BlockSpec((TM_PREP, 512), lambda i: (i, O3_Q // 512)),
                  pl.BlockSpec((TM_PREP, 512), lambda i: (i, O3_K // 512)),
                  pl.BlockSpec((TM_PREP, 128), lambda i: (i, O3_AB // 128)),
                  full((3, 128, GLA_DK)), full((3, 128, GLA_DK)), full((1, GLA_DK)), full((1, GLA_DK))],
        out_specs=[o_spec(GLA_DK), o_spec(GLA_DK), o_spec(D), o_spec(GLA_DK), o_spec(GLA_DK)],
        out_shape=[jax.ShapeDtypeStruct((n_samples, SEQ_ALL, GLA_DK), F32),
                   jax.ShapeDtypeStruct((n_samples, SEQ_ALL, GLA_DK), p3.dtype),
                   jax.ShapeDtypeStruct((n_samples, SEQ_ALL, D), p3.dtype),
                   jax.ShapeDtypeStruct((n_samples, SEQ_ALL, GLA_DK), F32),
                   jax.ShapeDtypeStruct((n_samples, SEQ_ALL, GLA_DK), F32)],
        compiler_params=_params(("parallel",)),
    )(p3, p3, p3, p3, upf, upb, bias_f, bias_b)


def gla_prep_bwd(p3, dq_f, dq_b, dk_f, dk_b, dv_f, dv_b, dc_f, dc_b, upf, upb, bias_f, bias_b, n_samples):
    n_lat, seq_map = _prep_tile_maps(n_samples)
    n_tiles = n_lat + n_samples

    def body(ab_ref, dqf, dqb, dkf, dkb, dvf, dvb, dcf, dcb, upf_ref, upb_ref, bf_ref, bb_ref,
             dp_ref, duf_ref, dub_ref, dbf_ref, dbb_ref):
        i = pl.program_id(0)
        dp_ref[:, pl.ds(O3_V, D)] = (dvf[0] + dvb[0]).astype(dp_ref.dtype)
        dq = jnp.where(i < n_lat, (dqf[0] + dqb[0]) * Q_SCALE, 0.0)
        dp_ref[:, pl.ds(O3_Q, GLA_DK)] = dq.astype(dp_ref.dtype)
        dp_ref[:, pl.ds(O3_K, GLA_DK)] = (dkf[0] + dkb[0]).astype(dp_ref.dtype)
        ab = ab_ref[...]
        zf = _gate_logits(ab, upf_ref, bf_ref)
        zb = _gate_logits(ab, upb_ref, bb_ref)
        dgf = _chunk_sums(dcf[0], True)
        dgb = _chunk_sums(dcb[0], False)
        dzf = _b16(dgf * (1.0 / GATE_TAU) * _sigmoid(-zf))
        dzb = _b16(dgb * (1.0 / GATE_TAU) * _sigmoid(-zb))
        dab = _nt(dzf, upf_ref[0]) + _nt(dzb, upb_ref[0])
        dp_ref[:, pl.ds(O3_AB, 128)] = dab.astype(dp_ref.dtype)

        @pl.when(i == 0)
        def _():
            duf_ref[...] = jnp.zeros_like(duf_ref)
            dub_ref[...] = jnp.zeros_like(dub_ref)
            dbf_ref[...] = jnp.zeros_like(dbf_ref)
            dbb_ref[...] = jnp.zeros_like(dbb_ref)

        duf_ref[...] += _tn(ab, dzf)
        dub_ref[...] += _tn(ab, dzb)
        dbf_ref[...] += jnp.sum(dzf.astype(F32), axis=0, keepdims=True)
        dbb_ref[...] += jnp.sum(dzb.astype(F32), axis=0, keepdims=True)

    def s_spec(w):
        return pl.BlockSpec((1, TM_PREP, w), lambda i: (*seq_map(i), 0))

    full = lambda shape: pl.BlockSpec(shape, lambda i: (0,) * len(shape))
    return _pallas(
        body, name="gla_prep_bwd", grid=(n_tiles,),
        in_specs=[pl.BlockSpec((TM_PREP, 128), lambda i: (i, O3_AB // 128)),
                  s_spec(GLA_DK), s_spec(GLA_DK), s_spec(GLA_DK), s_spec(GLA_DK), s_spec(D), s_spec(D),
                  s_spec(GLA_DK), s_spec(GLA_DK),
                  full((3, 128, GLA_DK)), full((3, 128, GLA_DK)), full((1, GLA_DK)), full((1, GLA_DK))],
        out_specs=[pl.BlockSpec((TM_PREP, W3), lambda i: (i, 0)),
                   full((128, GLA_DK)), full((128, GLA_DK)), full((1, GLA_DK)), full((1, GLA_DK))],
        out_shape=[jax.ShapeDtypeStruct((n_tiles * TM_PREP, W3), BF16),
                   jax.ShapeDtypeStruct((128, GLA_DK), F32), jax.ShapeDtypeStruct((128, GLA_DK), F32),
                   jax.ShapeDtypeStruct((1, GLA_DK), F32), jax.ShapeDtypeStruct((1, GLA_DK), F32)],
        compiler_params=_params(("arbitrary",)),
    )(p3, dq_f, dq_b, dk_f, dk_b, dv_f, dv_b, dc_f, dc_b, upf, upb, bias_f, bias_b)


def _sub_blocks(rev):
    if NSUB == 1:
        return [((0, CHUNK), CHUNK // 2, (0, CHUNK))]
    out = []
    for s in range(NSUB):
        rows = (s * SUB, SUB)
        if rev:
            ref = (s + 1) * SUB if s < NSUB - 1 else None
            cols = (s * SUB, CHUNK - s * SUB)
        else:
            ref = s * SUB - 1 if s > 0 else None
            cols = (0, (s + 1) * SUB)
        out.append((rows, ref, cols))
    return out


def _sub_mask(rows, cols, rev):
    r = rows[0] + lax.broadcasted_iota(jnp.int32, (rows[1], cols[1]), 0)
    c = cols[0] + lax.broadcasted_iota(jnp.int32, (rows[1], cols[1]), 1)
    return (c >= r) if rev else (c <= r)


def _sub_operands(qc, kc, cc, rows, ref, cols):
    cref = jnp.zeros((1, HEAD_K), F32) if ref is None else cc[ref:ref + 1]
    eq = jnp.exp(cc[rows[0]:rows[0] + rows[1]] - cref)
    ek = jnp.exp(cref - cc[cols[0]:cols[0] + cols[1]])
    qs = qc[rows[0]:rows[0] + rows[1]] * eq
    kk = kc[cols[0]:cols[0] + cols[1]] * ek
    return qs, kk, eq, ek


SCAN_ROWS = 256
SCAN_CHUNKS = SCAN_ROWS // CHUNK
SCAN_STEPS = SEQ_ALL // SCAN_ROWS
LAT_BLOCKS = SEQ // SCAN_ROWS


def _scan_block(t, rev):
    if rev:
        return SCAN_STEPS - 1 - t
    return jnp.where(t == 0, SCAN_STEPS - 1, t - 1)


def _scan_lat_block(t, rev):
    if rev:
        return jnp.minimum(SCAN_STEPS - 1 - t, LAT_BLOCKS - 1)
    return jnp.maximum(t - 1, 0)


def _head_cols(h):
    return pl.ds(h * HEAD_K, HEAD_K), pl.ds(h * HEAD_V, HEAD_V)


def gla_scan_fwd(q, k, v, cum, *, rev, name):
    n = q.shape[0]

    def body(q_ref, k_ref, v_ref, c_ref, o_ref, s_ref, sfin_ref, st):
        t = pl.program_id(1)

        @pl.when(t == 0)
        def _():
            st[...] = jnp.zeros_like(st)

        def chunk(j, carry):
            lj = SCAN_CHUNKS - 1 - j if rev else j
            r0 = lj * CHUNK
            rws = pl.ds(r0, CHUNK)
            for h in range(HEADS):
                kcols, vcols = _head_cols(h)
                qc, kc, cc = q_ref[0, rws, kcols], k_ref[0, rws, kcols], c_ref[0, rws, kcols]
                vc = v_ref[0, rws, vcols]
                s_in = st[h]
                s_ref[0, h, j] = _b16(s_in)
                edge = cc[0:1] if rev else cc[CHUNK - 1:CHUNK]
                ke = kc * jnp.exp(edge - cc)
                st[h] = s_in * jnp.exp(edge) + _tn(_b16(vc), _b16(ke))
                o_inter = _nt(_b16(qc * jnp.exp(cc)), _b16(s_in))
                vb = _b16(vc)
                for rows, ref, cols in _sub_blocks(rev):
                    qs, kk, _, _ = _sub_operands(qc, kc, cc, rows, ref, cols)
                    a = jnp.where(_sub_mask(rows, cols, rev), _nt(_b16(qs), _b16(kk)), 0.0)
                    o_s = _nn(_b16(a), vb[cols[0]:cols[0] + cols[1]])
                    o_ref[0, pl.ds(r0 + rows[0], rows[1]), vcols] = o_inter[rows[0]:rows[0] + rows[1]] + o_s
            return carry

        for j in range(SCAN_CHUNKS):
            chunk(j, 0)

        @pl.when(t == SCAN_STEPS - 1)
        def _():
            sfin_ref[0] = st[...]

    def spec(w):
        return pl.BlockSpec((1, SCAN_ROWS, w), lambda b, t: (b, _scan_block(t, rev), 0))

    return _pallas(
        body, name=name, grid=(n, SCAN_STEPS),
        in_specs=[spec(GLA_DK), spec(GLA_DK), spec(D), spec(GLA_DK)],
        out_specs=[pl.BlockSpec((1, SCAN_ROWS, D), lambda b, t: (b, _scan_lat_block(t, rev), 0)),
                   pl.BlockSpec((1, HEADS, SCAN_CHUNKS, HEAD_V, HEAD_K), lambda b, t: (b, 0, t, 0, 0)),
                   pl.BlockSpec((1, HEADS, HEAD_V, HEAD_K), lambda b, t: (b, 0, 0, 0))],
        out_shape=[jax.ShapeDtypeStruct((n, SEQ, D), F32),
                   jax.ShapeDtypeStruct((n, HEADS, NCHUNK, HEAD_V, HEAD_K), BF16),
                   jax.ShapeDtypeStruct((n, HEADS, HEAD_V, HEAD_K), F32)],
        scratch_shapes=[pltpu.VMEM((HEADS, HEAD_V, HEAD_K), F32)],
        compiler_params=_params(("parallel", "arbitrary")),
    )(q, k, v, cum)


def gla_scan_bwd(q, k, v, cum, s_all, s_fin, do, *, rev, name, rider=None):
    n = q.shape[0]

    def body(q_ref, k_ref, v_ref, c_ref, s_ref, sfin_ref, do_ref, *rest):
        if rider is not None:
            ride_in, rest = rest[0], rest[1:]
        dq_ref, dk_ref, dv_ref, dc_ref = rest[:4]
        if rider is not None:
            ride_out, rest = rest[4], rest[:4] + rest[5:]
        dst, s_next, dq_acc, dk_acc, dv_acc = rest[4:9]
        t = SCAN_STEPS - 1 - pl.program_id(1)

        if rider is not None:
            def copies():
                send, recv = rest[9], rest[10]
                if rider[0] == "swap":
                    return _pair_copies([], ride_in, [], ride_out, send, recv)
                x, y, c, chips = _place()
                return [_remote(ride_in.at[2 * cx + cy, rows], ride_out.at[pj, rows], send.at[pj * P_ROW_CHUNKS + pi],
                                recv.at[pj * P_ROW_CHUNKS + pi], (cx, cy, c))
                        for pj, (cx, cy) in enumerate(chips)
                        for pi, (_, rows) in enumerate(_half_chunks(0, (0, 2 * ride_in.shape[1]), 16, which=(1,)))]

            @pl.when((pl.program_id(0) == 0) & (pl.program_id(1) == 0))
            def _():
                for cp in copies():
                    cp.start()

            @pl.when((pl.program_id(0) == n - 1) & (pl.program_id(1) == SCAN_STEPS - 1))
            def _():
                for cp in copies():
                    cp.wait_recv()
                for cp in copies():
                    cp.wait_send()

        @pl.when(pl.program_id(1) == 0)
        def _():
            dst[...] = jnp.zeros_like(dst)
            s_next[...] = sfin_ref[0]

        def chunk(jj, carry):
            j = SCAN_CHUNKS - 1 - jj
            lj = SCAN_CHUNKS - 1 - j if rev else j
            rws = pl.ds(lj * CHUNK, CHUNK)
            for h in range(HEADS):
                kcols, vcols = _head_cols(h)
                qc, kc, cc = q_ref[0, rws, kcols], k_ref[0, rws, kcols], c_ref[0, rws, kcols]
                vc = v_ref[0, rws, vcols]
                doc = jnp.where(t > 0, do_ref[0, rws, vcols], 0.0)
                s_in = s_ref[0, h, j]
                s_out = s_next[h]
                ds_out = dst[h]
                edge = cc[0:1] if rev else cc[CHUNK - 1:CHUNK]
                e_q = jnp.exp(cc)
                e_k = jnp.exp(edge - cc)
                dob = _b16(doc)
                dsb = _b16(ds_out)
                dst[h] = ds_out * jnp.exp(edge) + _tn(dob, _b16(qc * e_q))
                s_next[h] = s_in.astype(F32)
                dq_acc[h] = e_q * _nn(dob, s_in)
                dk_acc[h] = e_k * _nn(_b16(vc), dsb)
                dv_acc[h] = _nt(_b16(kc * e_k), dsb)
                vb = _b16(vc)
                for rows, ref, cols in _sub_blocks(rev):
                    qs, kk, eq, ek = _sub_operands(qc, kc, cc, rows, ref, cols)
                    mask = _sub_mask(rows, cols, rev)
                    rsl = slice(rows[0], rows[0] + rows[1])
                    csl = pl.ds(cols[0], cols[1])
                    qsb, kkb = _b16(qs), _b16(kk)
                    a = jnp.where(mask, _nt(qsb, kkb), 0.0)
                    da = _b16(jnp.where(mask, _nt(dob[rsl], vb[cols[0]:cols[0] + cols[1]]), 0.0))
                    dq_acc[h, pl.ds(rows[0], rows[1]), :] += _nn(da, kkb) * eq
                    dk_acc[h, csl, :] += _tn(da, qsb) * ek
                    dv_acc[h, csl, :] += _tn(_b16(a), dob[rsl])
                dq = dq_acc[h]
                dk = dk_acc[h]
                dc = qc * dq - kc * dk
                bnd = jnp.sum(ds_out * s_out, axis=0, keepdims=True)
                edge_row = 0 if rev else CHUNK - 1
                is_edge = lax.broadcasted_iota(jnp.int32, (CHUNK, HEAD_K), 0) == edge_row
                dq_ref[0, rws, kcols] = dq
                dk_ref[0, rws, kcols] = dk
                dv_ref[0, rws, vcols] = dv_acc[h]
                dc_ref[0, rws, kcols] = dc + jnp.where(is_edge, bnd, 0.0)
            return carry

        for jj in range(SCAN_CHUNKS):
            chunk(jj, 0)

    def step_of(u):
        return SCAN_STEPS - 1 - u

    def spec(w):
        return pl.BlockSpec((1, SCAN_ROWS, w), lambda b, u: (b, _scan_block(step_of(u), rev), 0))

    in_specs = [spec(GLA_DK), spec(GLA_DK), spec(D), spec(GLA_DK),
                pl.BlockSpec((1, HEADS, SCAN_CHUNKS, HEAD_V, HEAD_K), lambda b, u: (b, 0, step_of(u), 0, 0)),
                pl.BlockSpec((1, HEADS, HEAD_V, HEAD_K), lambda b, u: (b, 0, 0, 0)),
                pl.BlockSpec((1, SCAN_ROWS, D), lambda b, u: (b, _scan_lat_block(step_of(u), rev), 0))]
    out_specs = [spec(GLA_DK), spec(GLA_DK), spec(D), spec(GLA_DK)]
    out_shape = [jax.ShapeDtypeStruct((n, SEQ_ALL, GLA_DK), F32), jax.ShapeDtypeStruct((n, SEQ_ALL, GLA_DK), F32),
                 jax.ShapeDtypeStruct((n, SEQ_ALL, D), F32), jax.ShapeDtypeStruct((n, SEQ_ALL, GLA_DK), F32)]
    scratch = [pltpu.VMEM((HEADS, HEAD_V, HEAD_K), F32), pltpu.VMEM((HEADS, HEAD_V, HEAD_K), F32),
               pltpu.VMEM((HEADS, CHUNK, HEAD_K), F32), pltpu.VMEM((HEADS, CHUNK, HEAD_K), F32),
               pltpu.VMEM((HEADS, CHUNK, HEAD_V), F32)]
    args = [q, k, v, cum, s_all, s_fin, do]
    if rider is not None:
        kind, arr = rider
        any_spec = pl.BlockSpec(memory_space=pl.ANY)
        in_specs.append(any_spec)
        out_specs.append(any_spec)
        args.append(arr)
        if kind == "swap":
            out_shape += _pair_got_shapes([], arr)
            n_cp = _pair_count([], arr)
        else:
            out_shape.append(jax.ShapeDtypeStruct((3,) + arr.shape[1:], arr.dtype))
            n_cp = 3 * P_ROW_CHUNKS
        scratch += [pltpu.SemaphoreType.DMA((n_cp,)), pltpu.SemaphoreType.DMA((n_cp,))]
    return _pallas(
        body, name=name, grid=(n, SCAN_STEPS), in_specs=in_specs, out_specs=out_specs, out_shape=out_shape,
        scratch_shapes=scratch,
        compiler_params=_params(("parallel" if rider is None else "arbitrary", "arbitrary")),
    )(*args)


def gla_out_fwd(o_f, o_b, r, gnorm):
    n = o_f.shape[0]
    tiles = SEQ // TM_EW

    def body(of_ref, ob_ref, r_ref, g_ref, og_ref):
        for h in range(HEADS):
            cols = pl.ds(h * HEAD_V, HEAD_V)
            o = of_ref[0, :, cols] + ob_ref[0, :, cols]
            rs = lax.rsqrt(jnp.mean(o * o, axis=-1, keepdims=True) + EPS)
            og_ref[:, cols] = (o * rs * g_ref[...] * _silu(r_ref[:, cols].astype(F32))).astype(og_ref.dtype)

    ospec = pl.BlockSpec((1, TM_EW, D), lambda b, j: (b, j, 0))
    row = pl.BlockSpec((TM_EW, D), lambda b, j: (b * tiles + j, 0))
    return _pallas(
        body, name="gla_out_fwd", grid=(n, tiles),
        in_specs=[ospec, ospec, row, pl.BlockSpec((1, HEAD_V), lambda b, j: (0, 0))],
        out_specs=row, out_shape=jax.ShapeDtypeStruct((n * SEQ, D), BF16),
        compiler_params=_params(("parallel", "parallel")),
    )(o_f, o_b, r, gnorm)


def gla_out_bwd(o_f, o_b, r, dyg, gla_proj, gnorm):
    n = o_f.shape[0]
    tiles = SEQ // TM_EW

    def body(of_ref, ob_ref, r_ref, d_ref, w_ref, g_ref, do_ref, dr_ref, dg_ref, dog_buf):
        @pl.when((pl.program_id(0) == 0) & (pl.program_id(1) == 0))
        def _():
            dg_ref[...] = jnp.zeros_like(dg_ref)

        dog_buf[...] = _nt(d_ref[...], w_ref[...])
        for h in range(HEADS):
            cols = pl.ds(h * HEAD_V, HEAD_V)
            o = of_ref[0, :, cols] + ob_ref[0, :, cols]
            rv = r_ref[:, cols].astype(F32)
            dv = dog_buf[:, cols]
            rs = lax.rsqrt(jnp.mean(o * o, axis=-1, keepdims=True) + EPS)
            oh = o * rs
            dr_ref[:, cols] = (dv * oh * g_ref[...] * _dsilu(rv)).astype(dr_ref.dtype)
            dn = dv * _silu(rv)
            dg_ref[...] += jnp.sum(dn * oh, axis=0, keepdims=True)
            doh = dn * g_ref[...]
            do_ref[0, :, cols] = rs * (doh - oh * jnp.mean(doh * oh, axis=-1, keepdims=True))

    ospec = pl.BlockSpec((1, TM_EW, D), lambda b, j: (b, j, 0))
    row = pl.BlockSpec((TM_EW, D), lambda b, j: (b * tiles + j, 0))
    vec = pl.BlockSpec((1, HEAD_V), lambda b, j: (0, 0))
    return _pallas(
        body, name="gla_out_bwd", grid=(n, tiles),
        in_specs=[ospec, ospec, row, row, pl.BlockSpec((D, D), lambda b, j: (0, 0)), vec],
        out_specs=[ospec, row, vec],
        out_shape=[jax.ShapeDtypeStruct((n, SEQ, D), F32), jax.ShapeDtypeStruct((n * SEQ, D), BF16),
                   jax.ShapeDtypeStruct((1, HEAD_V), F32)],
        scratch_shapes=[pltpu.VMEM((TM_EW, D), F32)],
        compiler_params=_params(("arbitrary", "arbitrary")),
    )(o_f, o_b, r, dyg, gla_proj, gnorm)


TM_OUT = 512


def merge_out_final(p5, y_conv, y_gla, w_out, x2, gate, final_g, target, n_samples):
    t = x2.shape[0]
    tiles = SEQ // TM_OUT

    def body(mc_ref, mg_ref, yc_ref, yg_ref, w_ref, x_ref, gate_ref, g_ref, t_ref,
             mrg_ref, dh_ref, dmo_ref, dgate_ref, dg_ref, loss_ref):
        b, j = pl.program_id(0), pl.program_id(1)
        f = lambda ref: ref[...].astype(F32)
        merged = _b16(_sigmoid(f(mc_ref)) * f(yc_ref) + _sigmoid(f(mg_ref)) * f(yg_ref))
        mrg_ref[...] = merged
        mo_v = _nn(merged, w_ref[...])
        h = x_ref[...] + gate_ref[0] * mo_v
        rs = lax.rsqrt(jnp.mean(h * h, axis=-1, keepdims=True) + EPS)
        nh = h * rs
        err = nh * g_ref[...] - t_ref[...]
        dy = err * (1.0 / D)
        dn = dy * g_ref[...]
        dh = rs * (dn - nh * jnp.mean(dn * nh, axis=-1, keepdims=True))
        dh_ref[...] = dh
        dmo_ref[...] = (dh * gate_ref[0]).astype(dmo_ref.dtype)

        @pl.when(j == 0)
        def _():
            dgate_ref[...] = jnp.zeros_like(dgate_ref)

        @pl.when((b == 0) & (j == 0))
        def _():
            dg_ref[...] = jnp.zeros_like(dg_ref)
            loss_ref[...] = jnp.zeros_like(loss_ref)

        dgate_ref[0] += jnp.sum(dh * mo_v, axis=0, keepdims=True)
        dg_ref[...] += jnp.sum(dy * nh, axis=0, keepdims=True)
        loss_ref[...] += (0.5 / D) * jnp.sum(err * err)

    row = pl.BlockSpec((TM_OUT, D), lambda b, j: (b * tiles + j, 0))
    per = pl.BlockSpec((1, 1, D), lambda b, j: (b, 0, 0))
    vec = pl.BlockSpec((1, D), lambda b, j: (0, 0))
    return _pallas(
        body, name="merge_out_final", grid=(n_samples, tiles),
        in_specs=[row, pl.BlockSpec((TM_OUT, D), lambda b, j: (b * tiles + j, 1)), row, row,
                  pl.BlockSpec((D, D), lambda b, j: (0, 0)), row, per, vec, row],
        out_specs=[row, row, row, per, vec, pl.BlockSpec((8, 128), lambda b, j: (0, 0))],
        out_shape=[jax.ShapeDtypeStruct((t, D), BF16), jax.ShapeDtypeStruct((t, D), F32), jax.ShapeDtypeStruct((t, D), BF16),
                   jax.ShapeDtypeStruct((n_samples, 1, D), F32), jax.ShapeDtypeStruct((1, D), F32),
                   jax.ShapeDtypeStruct((8, 128), F32)],
        compiler_params=_params(("arbitrary", "arbitrary")),
    )(p5, p5, y_conv, y_gla, w_out, x2, gate, final_g, target)


def out_dgrad_merge_bwd(p5, y_conv, y_gla, dmo, w_out):
    t = y_conv.shape[0]

    def body(mc_ref, mg_ref, yc_ref, yg_ref, d_ref, w_ref, dyc_ref, dyg_ref, dp_ref):
        f = lambda ref: ref[...].astype(F32)
        d = _nt(d_ref[...], w_ref[...])
        sc = _sigmoid(f(mc_ref))
        sg = _sigmoid(f(mg_ref))
        dyc_ref[...] = (d * sc).astype(dyc_ref.dtype)
        dyg_ref[...] = (d * sg).astype(dyg_ref.dtype)
        dp_ref[:, pl.ds(0, D)] = (d * f(yc_ref) * sc * (1.0 - sc)).astype(dp_ref.dtype)
        dp_ref[:, pl.ds(D, D)] = (d * f(yg_ref) * sg * (1.0 - sg)).astype(dp_ref.dtype)

    row = pl.BlockSpec((TM_OUT, D), lambda i: (i, 0))
    return _pallas(
        body, name="out_dgrad_merge_bwd", grid=(t // TM_OUT,),
        in_specs=[row, pl.BlockSpec((TM_OUT, D), lambda i: (i, 1)), row, row, row, pl.BlockSpec((D, D), lambda i: (0, 0))],
        out_specs=[row, row, pl.BlockSpec((TM_OUT, 2 * D), lambda i: (i, 0))],
        out_shape=[jax.ShapeDtypeStruct((t, D), BF16), jax.ShapeDtypeStruct((t, D), BF16),
                   jax.ShapeDtypeStruct((t, 2 * D), BF16)],
        compiler_params=_params(("parallel",)),
    )(p5, p5, y_conv, y_gla, dmo, w_out)


def local_step(x, ctx, target, mod, wts, small, p_sh, chip, core):
    n = x.shape[0]
    t = n * SEQ
    t_all = t + n * NCTX
    x2 = x.reshape(t, D)
    ctx2 = ctx.reshape(n * NCTX, D)
    tgt2 = target.reshape(t, D)
    scale1, shift, gate = mod

    u = norm_mod_fwd(x2, ctx2, scale1, shift, small["norm_g"])
    p1 = matmul_nn(u, wts["w1"], small["b1"], name="proj_conv", m=t, tm=1024, tn=1024, out_dtype=BF16)
    p2 = matmul_nn(u, wts["w2"], small["b2"], name="proj_z", m=t, tm=1024, tn=1024, out_dtype=BF16)
    p3, p_all = matmul_nn(u, wts["w3"], small["b3"], name="proj_gla", m=t_all, tm=512, tn=W3, out_dtype=BF16,
                          gather=p_sh)
    p4 = matmul_nn(u, wts["w4"], small["b4"], name="proj_r", m=t, tm=1024, tn=1024, out_dtype=BF16)
    p5 = matmul_nn(u, wts["w5"], small["b5"], name="proj_merge", m=t, tm=1024, tn=1024, out_dtype=BF16)
    p_full = jnp.stack([jnp.where(chip == i, p_sh, p_all[i]) for i in range(N_CHIPS)])
    wts = dict(wts, conv_proj=p_full[:, 0:256].reshape(D, D), gla_proj=p_full[:, 256:512].reshape(D, D),
               w_out=p_full[:, 512:768].reshape(D, D))

    aconv = conv_fwd(p1, small["conv_w"], small["conv_b"], n)
    ac = ln_gate_fwd(aconv, p2, small["conv_ln_g"], small["conv_ln_b"])
    y_conv = matmul_nn(ac, wts["conv_proj"], None, name="conv_proj_fwd", m=t, tm=1024, tn=1024, out_dtype=BF16)

    qs, ks, vs, cum_f, cum_b = gla_prep_fwd(p3, small["upf"], small["upb"], small["bias_f"], small["bias_b"], n)
    o_f, s_f, sfin_f = gla_scan_fwd(qs, ks, vs, cum_f, rev=False, name="gla_scan_fwd_f")
    o_b, s_b, sfin_b = gla_scan_fwd(qs, ks, vs, cum_b, rev=True, name="gla_scan_fwd_b")
    og = gla_out_fwd(o_f, o_b, p4, small["gla_norm_g"])
    y_gla = matmul_nn(og, wts["gla_proj"], None, name="gla_proj_fwd", m=t, tm=1024, tn=1024, out_dtype=BF16)

    merged, dh, dmo, dgate, d_final_g, loss = merge_out_final(p5, y_conv, y_gla, wts["w_out"], x2, gate,
                                                              small["final_norm_g"], tgt2, n)

    g = {"final_norm_g": d_final_g}
    g["w_out"] = matmul_tn(merged, dmo, name="w_out_wgrad", t=t, tn=1024, tt=1024)[0]
    dyc, dyg, dp5 = out_dgrad_merge_bwd(p5, y_conv, y_gla, dmo, wts["w_out"])

    g["conv_proj"] = matmul_tn(ac, dyc, name="conv_proj_wgrad", t=t, tn=1024, tt=1024)[0]
    daconv, dp2, g["conv_ln_g"], g["conv_ln_b"] = ln_gate_bwd(aconv, p2, dyc, wts["conv_proj"], small["conv_ln_g"],
                                                               small["conv_ln_b"])
    dp1, dconv_w, dconv_b = conv_bwd(p1, daconv, small["conv_w"], n)
    g["conv_w"], g["conv_b"] = dconv_w, dconv_b

    g["gla_proj"] = matmul_tn(og, dyg, name="gla_proj_wgrad", t=t, tn=1024, tt=1024)[0]
    do, dp4, g["gla_norm_g"] = gla_out_bwd(o_f, o_b, p4, dyg, wts["gla_proj"], small["gla_norm_g"])
    g["proj"] = jnp.concatenate([g["conv_proj"].reshape(N_CHIPS, 256, D), g["gla_proj"].reshape(N_CHIPS, 256, D),
                                 g["w_out"].reshape(N_CHIPS, 256, D)], 1)
    dq_f, dk_f, dv_f, dc_f, gotp = gla_scan_bwd(qs, ks, vs, cum_f, s_f, sfin_f, do, rev=False, name="gla_scan_bwd_f",
                                                rider=("swap", g["proj"]))
    pap16 = pair_add(core, g["proj"], gotp, name="pair_add_p", tr=384)
    dq_b, dk_b, dv_b, dc_b, rbp = gla_scan_bwd(qs, ks, vs, cum_b, s_b, sfin_b, do, rev=True, name="gla_scan_bwd_b",
                                               rider=("exchange", pap16))
    dp3, g["upf"], g["upb"], g["bias_f"], g["bias_b"] = gla_prep_bwd(
        p3, dq_f, dq_b, dk_f, dk_b, dv_f, dv_b, dc_f, dc_b,
        small["upf"], small["upb"], small["bias_f"], small["bias_b"], n)

    dps = [dp1, dp2, dp3, dp4, dp5]
    got = {}
    for i in [0, 1, 3, 4, 2]:
        dp = dps[i]
        rows = dp.shape[0]
        tn = W3 if dp.shape[1] == W3 else 1024
        others = [j for j in range(5) if j != i]
        swap = [g["w%d" % (j + 1)] for j in others] if i == 2 else None
        outs = matmul_tn(u, dp, name="w_in_wgrad_%d" % (i + 1), t=rows, tn=tn, tt=1024 if rows % 1024 == 0 else 768,
                         colsum=True, swap=swap)
        g["w%d" % (i + 1)], g["b%d" % (i + 1)] = outs[0], outs[1]
        if swap is not None:
            got = dict(zip(others, outs[2:2 + len(others)]))
    g["gate"] = dgate
    return loss, dh, dps, g, got, (pap16, rbp)


def _group_cols(w):
    gv, gg, z = w[..., 0:1024], w[..., 1024:2048], w[..., 2048:3072]
    q, k, v = w[..., 3072:3584], w[..., 3584:4096], w[..., 4096:5120]
    ab = w[..., 5120:5152]
    r, mc, mg = w[..., 5152:6176], w[..., 6176:7200], w[..., 7200:8224]
    g1 = jnp.concatenate([p for j in range(CONV_NCB)
                          for p in (gv[..., CONV_CB * j:CONV_CB * (j + 1)], gg[..., CONV_CB * j:CONV_CB * (j + 1)])], -1)
    pad = jnp.zeros(w.shape[:-1] + (W3 - 2080,), w.dtype)
    g3 = jnp.concatenate([v, q, k, ab, pad], -1)
    return g1, z, g3, r, jnp.concatenate([mc, mg], -1)


def _ungroup_cols(g1, g2, g3, g4, g5):
    gv = jnp.concatenate([g1[..., 2 * CONV_CB * j:2 * CONV_CB * j + CONV_CB] for j in range(CONV_NCB)], -1)
    gg = jnp.concatenate([g1[..., 2 * CONV_CB * j + CONV_CB:2 * CONV_CB * (j + 1)] for j in range(CONV_NCB)], -1)
    v, q, k, ab = g3[..., 0:1024], g3[..., 1024:1536], g3[..., 1536:2048], g3[..., 2048:2080]
    return jnp.concatenate([gv, gg, g2, q, k, v, ab, g4, g5[..., 0:1024], g5[..., 1024:2048]], -1)


def _natural_pieces():
    pieces = [(CONV_CB * j, CONV_CB, 0, 2 * CONV_CB * j) for j in range(CONV_NCB)]
    pieces += [(1024 + CONV_CB * j, CONV_CB, 0, 2 * CONV_CB * j + CONV_CB) for j in range(CONV_NCB)]
    pieces += [(2048, 1024, 1, 0), (3072, 512, 2, O3_Q), (3584, 512, 2, O3_K), (4096, 1024, 2, O3_V), (5120, 32, 2, O3_AB),
               (5152, 1024, 3, 0), (6176, 1024, 4, 0), (7200, 1024, 4, 1024)]
    return sorted(pieces)


def _ungroup_to_shards(groups):
    shards = []
    for i in range(N_CHIPS):
        lo, hi = i * W_IN_SHARD, (i + 1) * W_IN_SHARD
        parts = []
        for nat, width, g, gcol in _natural_pieces():
            a, b = max(nat, lo), min(nat + width, hi)
            if a < b:
                parts.append(groups[g][:, gcol + a - nat:gcol + b - nat])
        shards.append(jnp.concatenate(parts, 1))
    return jnp.stack(shards)


def _pad_up(up, row0):
    return jnp.zeros((128, GLA_DK), F32).at[row0:row0 + up.shape[0]].set(up)


def _adamw_math(w, g, m, v):
    m = ADAM_B1 * m + (1.0 - ADAM_B1) * g
    v = ADAM_B2 * v + (1.0 - ADAM_B2) * (g * g)
    m_hat = m / (1.0 - ADAM_B1 ** ADAM_STEP)
    v_hat = v / (1.0 - ADAM_B2 ** ADAM_STEP)
    delta = -ADAM_LR * (m_hat / (jnp.sqrt(v_hat) + ADAM_EPS) + ADAM_WD * w)
    return delta, m, v


def adamw2d(w, g, m, v, *, name, tr, tcols=None):
    rows, cols = w.shape[-2:]

    def body(w_ref, g_ref, m_ref, v_ref, d_ref, nm_ref, nv_ref):
        d_ref[...], nm_ref[...], nv_ref[...] = _adamw_math(w_ref[...], g_ref[...], m_ref[...], v_ref[...])

    tcols = cols if tcols is None else tcols
    if w.ndim == 3:
        spec = pl.BlockSpec((1, tr, tcols), lambda i, j: (0, i, j))
    else:
        spec = pl.BlockSpec((tr, tcols), lambda i, j: (i, j))
    return _pallas(
        body, name=name, grid=(rows // tr, cols // tcols), in_specs=[spec] * 4, out_specs=[spec] * 3,
        out_shape=[jax.ShapeDtypeStruct(w.shape, F32)] * 3, compiler_params=_params(("parallel", "parallel")),
    )(w, g, m, v)


def adamw_many(ws, gs, ms, vs):
    k = len(ws)
    two = lambda a: a.reshape((-1, a.shape[-1]))

    def body(*refs):
        w_refs, g_refs, m_refs, v_refs = refs[:k], refs[k:2 * k], refs[2 * k:3 * k], refs[3 * k:4 * k]
        d_refs, nm_refs, nv_refs = refs[4 * k:5 * k], refs[5 * k:6 * k], refs[6 * k:7 * k]
        for i in range(k):
            d_refs[i][...], nm_refs[i][...], nv_refs[i][...] = _adamw_math(
                w_refs[i][...], g_refs[i][...], m_refs[i][...], v_refs[i][...])

    shapes = [jax.ShapeDtypeStruct(two(a).shape, F32) for a in ws]
    outs = _pallas(body, name="adamw_small", out_shape=shapes * 3, compiler_params=_params())(
        *[two(a) for a in ws], *[two(a) for a in gs], *[two(a) for a in ms], *[two(a) for a in vs])
    back = lambda lst: [o.reshape(a.shape) for o, a in zip(lst, ws)]
    return back(outs[:k]), back(outs[k:2 * k]), back(outs[2 * k:])


def sum_devices(sall, *, name):
    rows = sall.shape[1]

    def body(s_ref, o_ref):
        acc = s_ref[0]
        for d in range(1, N_DEV):
            acc = acc + s_ref[d]
        o_ref[...] = acc

    return _pallas(body, name=name, out_shape=jax.ShapeDtypeStruct((rows, D), F32),
                   compiler_params=_params())(sall)


def pair_add(core, g, got, *, name, tr):
    n, rows, cols = got.shape
    g4 = g.reshape(n, 2, rows, cols)

    def body(core_ref, g_ref, got_ref, ob_ref):
        del core_ref
        ob_ref[0] = (g_ref[0, 0] + got_ref[0]).astype(BF16)

    spec = pl.BlockSpec((1, tr, cols), lambda i, t, core_ref: (i, t, 0))
    return _pallas(
        body, name=name,
        grid_spec=pltpu.PrefetchScalarGridSpec(
            num_scalar_prefetch=1, grid=(n, rows // tr),
            in_specs=[pl.BlockSpec((1, 1, tr, cols), lambda i, t, core_ref: (i, core_ref[0], t, 0)), spec],
            out_specs=spec),
        out_shape=jax.ShapeDtypeStruct(got.shape, BF16),
        compiler_params=_params(("parallel", "parallel")))(core, g4, got)


def chip_add(place, pa, rb, *, name, tr):
    _, rows, cols = pa.shape

    def body(place_ref, m_ref, r_ref, o_ref):
        del place_ref
        o_ref[0] = ((m_ref[0].astype(F32) + r_ref[0].astype(F32)) + r_ref[1].astype(F32)) + r_ref[2].astype(F32)

    return _pallas(
        body, name=name,
        grid_spec=pltpu.PrefetchScalarGridSpec(
            num_scalar_prefetch=1, grid=(rows // tr,),
            in_specs=[pl.BlockSpec((1, tr, cols), lambda t, place_ref: (place_ref[0], t, 0)),
                      pl.BlockSpec((3, tr, cols), lambda t, place_ref: (0, t, 0))],
            out_specs=pl.BlockSpec((1, tr, cols), lambda t, place_ref: (place_ref[1], t, 0))),
        out_shape=jax.ShapeDtypeStruct((2, rows, cols), F32),
        compiler_params=_params(("parallel",)))(place, pa, rb)


def ada_bwd(call, cctx_rows, dm_shard, dm_full, adaw):
    nsh = adaw.shape[1]

    def body(c_ref, cc_ref, dms_ref, dmf_ref, w_ref, gw_ref, gb_ref, pq_ref):
        a_lat = _silu(c_ref[...])
        a_ctx = _silu(cc_ref[...])
        dms = dms_ref[...]
        gw_ref[...] = _tn(a_lat, dms[0:64], HI) + _tn(a_ctx, dms[64:72], HI)
        gb_ref[...] = jnp.sum(dmf_ref[...], axis=0, keepdims=True)
        part = _nt(dms[64:72], w_ref[...], HI)
        pq_ref[...] = jnp.zeros_like(pq_ref) + jnp.sum(part, axis=0, keepdims=True)

    return _pallas(body, name="ada_bwd",
                   out_shape=[jax.ShapeDtypeStruct((D, nsh), F32), jax.ShapeDtypeStruct((1, 3 * D), F32),
                              jax.ShapeDtypeStruct((8, D), F32)],
                   compiler_params=_params())(call, cctx_rows, dm_shard, dm_full, adaw)


def cctx_grad(pq_all, cctx_rows):
    def body(p_ref, c_ref, o_ref):
        acc = p_ref[0]
        for qi in range(1, N_CHIPS):
            acc = acc + p_ref[qi]
        o_ref[...] = acc * _dsilu(c_ref[...])

    return _pallas(body, name="cctx_grad", out_shape=jax.ShapeDtypeStruct((8, D), F32),
                   compiler_params=_params())(pq_all, cctx_rows)


def _place():
    x, y, c = lax.axis_index("x"), lax.axis_index("y"), lax.axis_index("c")
    chips = [(1 - x, y), (x, 1 - y), (1 - x, 1 - y)]
    return x, y, c, chips


def _all_peers(x, y, c):
    return [((1 - x) if r & 4 else x, (1 - y) if r & 2 else y, (1 - c) if r & 1 else c) for r in range(1, N_DEV)]


def _remote(src, dst, send_sem, recv_sem, dev):
    return pltpu.make_async_remote_copy(src_ref=src, dst_ref=dst, send_sem=send_sem, recv_sem=recv_sem,
                                        device_id=dev, device_id_type=MESH)


ANY = pl.BlockSpec(memory_space=pl.ANY)
VMEM = pl.BlockSpec(memory_space=pltpu.VMEM)
F_ROWS = 16


W_ROW_CHUNKS = 4
P_ROW_CHUNKS = 2
N_BULK = W_ROW_CHUNKS + P_ROW_CHUNKS


def _half_chunks(core, n_rows, align, which=(0, 1)):
    out = []
    for a, k in ((0, W_ROW_CHUNKS), (1, P_ROW_CHUNKS)):
        if a not in which:
            continue
        half = n_rows[a] // 2
        size = half // k
        for i in range(k):
            start = core * half + i * size
            out.append((a, pl.ds(start if isinstance(start, int) else pl.multiple_of(start, align), size)))
    return out


def gather_weights(c8, cctx8, adaw, adab, w_sh, fp):
    nsh = adaw.shape[1]

    def body(c_ref, cctx_ref, adaw_ref, adab_ref, w_ref, fp_ref, wall_ref, fall_ref, call_ref, mall_ref,
             abuf, w_send, w_recv, h_send, h_recv, c_send, c_recv, m_send, m_recv, f_send, f_recv):
        x, y, c, chips = _place()
        q = 2 * x + y
        dev = 4 * x + 2 * y + c
        qs = [2 * cx + cy for cx, cy in chips]
        sib = (x, y, 1 - c)
        srcs, dsts = (w_ref,), (wall_ref,)
        n_rows = (w_ref.shape[0],)
        mine = _half_chunks(c, n_rows, 16, which=(0,))
        other = _half_chunks(1 - c, n_rows, 16, which=(0,))

        bulk = [[_remote(srcs[a].at[rows], dsts[a].at[q, rows], w_send.at[j * N_BULK + i], w_recv.at[j * N_BULK + i],
                         (*chips[j], c)) for i, (a, rows) in enumerate(mine)] for j in range(3)]
        fall_ref[q] = fp_ref[...]
        small = [_remote(fp_ref, fall_ref.at[q], f_send.at[j], f_recv.at[j], (*chips[j], c)) for j in range(3)]
        my_rows = pl.ds(pl.multiple_of(8 * dev, 8), 8)
        call_ref[my_rows, :] = c_ref[...]
        cond = [_remote(c_ref, call_ref.at[my_rows, :], c_send.at[r], c_recv.at[r], peer)
                for r, peer in enumerate(_all_peers(x, y, c))]
        for cp in sum(bulk, []) + small + cond:
            cp.start()
        for cp in cond:
            cp.wait_recv()

        abuf[pl.ds(0, 64), :] = _silu(call_ref[...])
        abuf[pl.ds(64, 8), :] = _silu(cctx_ref[...])
        mall_ref[q] = _nn(abuf[...], adaw_ref[...], HI) + adab_ref[...]
        mod = [_remote(mall_ref.at[q], mall_ref.at[q], m_send.at[j], m_recv.at[j], (*chips[j], c)) for j in range(3)]
        for cp in mod:
            cp.start()

        handed = []
        for j in range(3):
            for i, (a, rows) in enumerate(mine):
                bulk[j][i].wait_recv()
                cp = _remote(dsts[a].at[qs[j], rows], dsts[a].at[qs[j], rows],
                             h_send.at[j * N_BULK + i], h_recv.at[j * N_BULK + i], sib)
                cp.start()
                handed.append(cp)
        for j in range(3):
            for i, (a, rows) in enumerate(other):
                _remote(dsts[a].at[qs[j], rows], dsts[a].at[qs[j], rows],
                        h_send.at[j * N_BULK + i], h_recv.at[j * N_BULK + i], sib).wait_recv()
        for cp in mod + small:
            cp.wait_recv()
        for cp in sum(bulk, []) + small + cond + mod + handed:
            cp.wait_send()

    def dma(n):
        return pltpu.SemaphoreType.DMA((n,))

    return _pallas(
        body, name="gather_weights",
        in_specs=[VMEM, VMEM, VMEM, VMEM, ANY, VMEM],
        out_specs=[ANY, VMEM, VMEM, VMEM],
        out_shape=[jax.ShapeDtypeStruct((N_CHIPS,) + w_sh.shape, BF16),
                   jax.ShapeDtypeStruct((N_CHIPS, F_ROWS, D), F32),
                   jax.ShapeDtypeStruct((8 * N_DEV, D), F32), jax.ShapeDtypeStruct((N_CHIPS, MOD_ROWS, nsh), F32)],
        scratch_shapes=[pltpu.VMEM((MOD_ROWS, D), F32), dma(3 * N_BULK), dma(3 * N_BULK), dma(3 * N_BULK), dma(3 * N_BULK),
                        dma(7), dma(7), dma(3), dma(3), dma(3), dma(3)],
        compiler_params=_params(),
    )(c8, cctx8, adaw, adab, w_sh, fp)


def _pair_count(gs, gp):
    return len(gs) * W_ROW_CHUNKS + (0 if gp is None else N_CHIPS * P_ROW_CHUNKS)


def _pair_got_shapes(gs, gp):
    shapes = [jax.ShapeDtypeStruct((D // 2, a.shape[1]), F32) for a in gs]
    if gp is not None:
        shapes.append(jax.ShapeDtypeStruct((N_CHIPS, gp.shape[1] // 2, gp.shape[2]), F32))
    return shapes


def _pair_copies(g_refs, gp_ref, got_refs, gotp_ref, a_send, a_recv):
    x, y, c, _ = _place()
    sib = (x, y, 1 - c)
    pair = []
    half, size = D // 2, D // 2 // W_ROW_CHUNKS
    for gi in range(len(g_refs)):
        for i in range(W_ROW_CHUNKS):
            k = len(pair)
            rows_o = pl.ds(pl.multiple_of((1 - c) * half + i * size, 8), size)
            pair.append(_remote(g_refs[gi].at[rows_o], got_refs[gi].at[pl.ds(i * size, size)],
                                a_send.at[k], a_recv.at[k], sib))
    if gp_ref is not None:
        half, size = gp_ref.shape[1] // 2, gp_ref.shape[1] // 2 // P_ROW_CHUNKS
        for s in range(N_CHIPS):
            for i in range(P_ROW_CHUNKS):
                k = len(pair)
                rows_o = pl.ds(pl.multiple_of((1 - c) * half + i * size, 8), size)
                pair.append(_remote(gp_ref.at[s, rows_o], gotp_ref.at[s, pl.ds(i * size, size)],
                                    a_send.at[k], a_recv.at[k], sib))
    return pair


def pair_swap(gs, sm):
    n_gs = len(gs)

    def body(*refs):
        g_refs, sm_ref = refs[:n_gs], refs[n_gs]
        got_refs, sall_ref = refs[n_gs + 1:2 * n_gs + 1], refs[2 * n_gs + 1]
        a_send, a_recv, s_send, s_recv = refs[2 * n_gs + 2:]
        x, y, c, _ = _place()
        dev = 4 * x + 2 * y + c
        pair = _pair_copies(g_refs, None, got_refs, None, a_send, a_recv)
        sall_ref[dev] = sm_ref[...]
        small = [_remote(sm_ref, sall_ref.at[dev], s_send.at[r], s_recv.at[r], peer)
                 for r, peer in enumerate(_all_peers(x, y, c))]
        for cp in pair + small:
            cp.start()
        for cp in small + pair:
            cp.wait_recv()
        for cp in small + pair:
            cp.wait_send()

    return _pallas(
        body, name="pair_swap", in_specs=[ANY] * n_gs + [VMEM], out_specs=[ANY] * n_gs + [VMEM],
        out_shape=_pair_got_shapes(gs, None) + [jax.ShapeDtypeStruct((N_DEV,) + sm.shape, F32)],
        scratch_shapes=[pltpu.SemaphoreType.DMA((_pair_count(gs, None),)), pltpu.SemaphoreType.DMA((_pair_count(gs, None),)),
                        pltpu.SemaphoreType.DMA((N_DEV - 1,)), pltpu.SemaphoreType.DMA((N_DEV - 1,))],
        compiler_params=_params(),
    )(*gs, sm)


def gather_small(sm):
    rows = sm.shape[0]

    def body(sm_ref, sall_ref, s_send, s_recv):
        x, y, c, _ = _place()
        dev = 4 * x + 2 * y + c
        sall_ref[dev] = sm_ref[...]
        small = [_remote(sm_ref, sall_ref.at[dev], s_send.at[r], s_recv.at[r], peer)
                 for r, peer in enumerate(_all_peers(x, y, c))]
        for cp in small:
            cp.start()
        for cp in small:
            cp.wait_recv()
        for cp in small:
            cp.wait_send()

    return _pallas(
        body, name="gather_small", in_specs=[VMEM], out_specs=VMEM,
        out_shape=jax.ShapeDtypeStruct((N_DEV, rows, D), F32),
        scratch_shapes=[pltpu.SemaphoreType.DMA((7,)), pltpu.SemaphoreType.DMA((7,))],
        compiler_params=_params(),
    )(sm)


def pair_share(ghw, ghp, pq):
    def body(ghw_ref, ghp_ref, pq_ref, outw_ref, outp_ref, pqa_ref, send, recv, p_send, p_recv):
        del ghw_ref, ghp_ref
        x, y, c, chips = _place()
        q = 2 * x + y
        refs = (outw_ref, outp_ref)
        n_rows = (2 * outw_ref.shape[1], 2 * outp_ref.shape[1])
        pair = [_remote(refs[a].at[c, rows], refs[a].at[c, rows], send.at[i], recv.at[i], (x, y, 1 - c))
                for i, (a, rows) in enumerate(_half_chunks(0, n_rows, 8))]
        pqa_ref[q] = pq_ref[...]
        small = [_remote(pq_ref, pqa_ref.at[q], p_send.at[j], p_recv.at[j], (*chips[j], c)) for j in range(3)]
        for cp in pair + small:
            cp.start()
        for i, (a, rows) in enumerate(_half_chunks(0, n_rows, 8)):
            _remote(refs[a].at[1 - c, rows], refs[a].at[1 - c, rows], send.at[i], recv.at[i], (x, y, 1 - c)).wait_recv()
        for cp in small:
            cp.wait_recv()
        for cp in pair + small:
            cp.wait_send()

    return _pallas(
        body, name="pair_share", in_specs=[ANY, ANY, VMEM], out_specs=[ANY, ANY, VMEM],
        out_shape=[jax.ShapeDtypeStruct(ghw.shape, F32), jax.ShapeDtypeStruct(ghp.shape, F32),
                   jax.ShapeDtypeStruct((N_CHIPS, 8, D), F32)],
        scratch_shapes=[pltpu.SemaphoreType.DMA((N_BULK,)), pltpu.SemaphoreType.DMA((N_BULK,)),
                        pltpu.SemaphoreType.DMA((3,)), pltpu.SemaphoreType.DMA((3,))],
        input_output_aliases={0: 0, 1: 1},
        compiler_params=_params(),
    )(ghw, ghp, pq)


def _rows_of(shape):
    size = 1
    for s in shape:
        size *= s
    return -(-size // D)


def _pack(arrs, rows_multiple=8):
    parts = []
    total = 0
    for a in arrs:
        f = a.reshape(-1).astype(F32)
        r = _rows_of(a.shape)
        parts.append(jnp.pad(f, (0, r * D - f.shape[0])))
        total += r
    pad_rows = (-total) % rows_multiple
    if pad_rows:
        parts.append(jnp.zeros((pad_rows * D,), F32))
    return jnp.concatenate(parts).reshape(-1, D)


def _unpack(p, shapes):
    out = []
    r0 = 0
    for shp in shapes:
        r = _rows_of(shp)
        size = 1
        for s in shp:
            size *= s
        out.append(p[r0:r0 + r].reshape(-1)[:size].reshape(shp))
        r0 += r
    return out


WEIGHT_NAMES = ['c_ctx', 'ada_w', 'ada_b', 'norm_g', 'w_in', 'b_in', 'conv_w', 'conv_b', 'conv_ln_g', 'conv_ln_b',
                'conv_proj', 'decay_up_fwd', 'decay_bias_fwd', 'decay_up_bwd', 'decay_bias_bwd', 'gla_norm_g', 'gla_proj',
                'w_out', 'final_norm_g']
SMALL_NAMES = ['c_ctx', 'ada_b', 'norm_g', 'b_in', 'conv_w', 'conv_b', 'conv_ln_g', 'conv_ln_b', 'decay_up_fwd',
               'decay_bias_fwd', 'decay_up_bwd', 'decay_bias_bwd', 'gla_norm_g', 'final_norm_g']


def kernel(x, c, ctx, c_ctx, ada_w, ada_b, norm_g, w_in, b_in, conv_w, conv_b, conv_ln_g, conv_ln_b, conv_proj, decay_up_fwd, decay_bias_fwd, decay_up_bwd, decay_bias_bwd, gla_norm_g, gla_proj, w_out, final_norm_g, loss_target, m_c_ctx, m_ada_w, m_ada_b, m_norm_g, m_w_in, m_b_in, m_conv_w, m_conv_b, m_conv_ln_g, m_conv_ln_b, m_conv_proj, m_decay_up_fwd, m_decay_bias_fwd, m_decay_up_bwd, m_decay_bias_bwd, m_gla_norm_g, m_gla_proj, m_w_out, m_final_norm_g, v_c_ctx, v_ada_w, v_ada_b, v_norm_g, v_w_in, v_b_in, v_conv_w, v_conv_b, v_conv_ln_g, v_conv_ln_b, v_conv_proj, v_decay_up_fwd, v_decay_bias_fwd, v_decay_up_bwd, v_decay_bias_bwd, v_gla_norm_g, v_gla_proj, v_w_out, v_final_norm_g):
    w = dict(c_ctx=c_ctx, ada_w=ada_w, ada_b=ada_b, norm_g=norm_g, w_in=w_in, b_in=b_in, conv_w=conv_w, conv_b=conv_b,
             conv_ln_g=conv_ln_g, conv_ln_b=conv_ln_b, conv_proj=conv_proj, decay_up_fwd=decay_up_fwd,
             decay_bias_fwd=decay_bias_fwd, decay_up_bwd=decay_up_bwd, decay_bias_bwd=decay_bias_bwd,
             gla_norm_g=gla_norm_g, gla_proj=gla_proj, w_out=w_out, final_norm_g=final_norm_g)
    m = dict(c_ctx=m_c_ctx, ada_w=m_ada_w, ada_b=m_ada_b, norm_g=m_norm_g, w_in=m_w_in, b_in=m_b_in, conv_w=m_conv_w,
             conv_b=m_conv_b, conv_ln_g=m_conv_ln_g, conv_ln_b=m_conv_ln_b, conv_proj=m_conv_proj,
             decay_up_fwd=m_decay_up_fwd, decay_bias_fwd=m_decay_bias_fwd, decay_up_bwd=m_decay_up_bwd,
             decay_bias_bwd=m_decay_bias_bwd, gla_norm_g=m_gla_norm_g, gla_proj=m_gla_proj, w_out=m_w_out,
             final_norm_g=m_final_norm_g)
    v = dict(c_ctx=v_c_ctx, ada_w=v_ada_w, ada_b=v_ada_b, norm_g=v_norm_g, w_in=v_w_in, b_in=v_b_in, conv_w=v_conv_w,
             conv_b=v_conv_b, conv_ln_g=v_conv_ln_g, conv_ln_b=v_conv_ln_b, conv_proj=v_conv_proj,
             decay_up_fwd=v_decay_up_fwd, decay_bias_fwd=v_decay_bias_fwd, decay_up_bwd=v_decay_up_bwd,
             decay_bias_bwd=v_decay_bias_bwd, gla_norm_g=v_gla_norm_g, gla_proj=v_gla_proj, w_out=v_w_out,
             final_norm_g=v_final_norm_g)
    n = x.shape[0]
    ax, ay, ac = lax.axis_index("x"), lax.axis_index("y"), lax.axis_index("c")
    q = 2 * ax + ay
    dev = 4 * ax + 2 * ay + ac
    nsh = ada_w.shape[2]

    w_sh = w_in[0].astype(BF16)
    p_sh = jnp.concatenate([conv_proj[0], gla_proj[0], w_out[0]], 0).astype(BF16)
    fp = _pack([conv_w[0], decay_up_fwd[0], decay_up_bwd[0]], F_ROWS)
    c8 = jnp.pad(c, ((0, 8 - n), (0, 0)))
    cctx8 = jnp.pad(c_ctx[None], ((0, 7), (0, 0)))
    adab_sh = lax.dynamic_slice(ada_b, (0, q * nsh), (1, nsh))
    w_all, fall, call, mall = gather_weights(c8, cctx8, ada_w[0], adab_sh, w_sh, fp)

    mod_all = jnp.transpose(mall, (1, 0, 2)).reshape(MOD_ROWS, 3 * D)
    mod_mine = lax.dynamic_slice(mod_all, (8 * dev, 0), (n, 3 * D))
    mod_ctx = mod_all[64:65]
    shift = jnp.concatenate([mod_mine[:, 0:D], mod_ctx[:, 0:D]], 0)[:, None, :]
    scale1 = 1.0 + jnp.concatenate([mod_mine[:, D:2 * D], mod_ctx[:, D:2 * D]], 0)[:, None, :]
    gate = mod_mine[:, 2 * D:3 * D][:, None, :]

    own = lambda i, mine, got: jnp.where(q == i, mine, got)
    g1, g2, g3, g4, g5 = _group_cols(jnp.concatenate([own(i, w_sh, w_all[i]) for i in range(N_CHIPS)], 1))
    wts = dict(w1=g1, w2=g2, w3=g3, w4=g4, w5=g5)
    f_parts = [_unpack(fall[i], [conv_w.shape[1:], decay_up_fwd.shape[1:], decay_up_bwd.shape[1:]]) for i in range(N_CHIPS)]
    conv_w_full = jnp.concatenate([p[0] for p in f_parts], 1)
    upf_full = jnp.concatenate([p[1] for p in f_parts], 1)
    upb_full = jnp.concatenate([p[2] for p in f_parts], 1)
    b1, b2, b3, b4, b5 = _group_cols(b_in)
    small = dict(b1=b1, b2=b2, b3=b3, b4=b4, b5=b5, norm_g=norm_g,
                 conv_w=jnp.pad(conv_w_full, ((0, 1), (0, 0))), conv_b=conv_b, conv_ln_g=conv_ln_g, conv_ln_b=conv_ln_b,
                 upf=_split3(_pad_up(upf_full, 0)), upb=_split3(_pad_up(upb_full, 16)),
                 bias_f=decay_bias_fwd, bias_b=decay_bias_bwd,
                 gla_norm_g=gla_norm_g, final_norm_g=final_norm_g[None])

    core = ac.astype(jnp.int32).reshape(1)
    chip = q.astype(jnp.int32).reshape(1)
    loss_part, dh, dps, g, got, (pap16, rbp) = local_step(x, ctx, loss_target, (scale1, shift, gate), wts, small,
                                                          p_sh, q, core)
    loss = lax.psum(loss_part[0, 0], ("x", "y", "c"))

    gs = [g["w%d" % i] for i in range(1, 6)]
    d_b_in = _ungroup_cols(*[g["b%d" % i] for i in range(1, 6)])
    early = [d_b_in, g["conv_b"].sum(0), g["conv_ln_g"], g["conv_ln_b"], g["bias_f"], g["bias_b"],
             g["gla_norm_g"], g["final_norm_g"], g["conv_w"].sum(0)[:CONV_K], g["upf"][0:16], g["upb"][16:32]]
    early_shapes = [a.shape for a in early]
    got[2], sall1 = pair_swap([gs[2]], _pack(early))
    halves = [pair_add(core, gs[i][None], got[i][None], name="pair_add_w%d" % (i + 1), tr=128)[0]
              for i in range(5)]
    paw16 = _ungroup_to_shards(halves)
    grad_x2, dshift, dscale, g["norm_g"], rbw = dgrad_norm_bwd(
        dps, [wts["w%d" % i] for i in range(1, 6)], paw16, x.reshape(n * SEQ, D), ctx.reshape(n * NCTX, D), dh,
        scale1, norm_g, tm=256)

    dm_mine = jnp.concatenate([dshift[:n, 0], dscale[:n, 0], g["gate"][:, 0]], -1)
    dm_ctx = jnp.concatenate([dshift[n, 0], dscale[n, 0], jnp.zeros((D,), F32)], -1)
    late = [g["norm_g"], dm_mine, dm_ctx]
    late_shapes = [a.shape for a in late]
    sall2 = gather_small(_pack(late))
    (s_b_in, s_conv_b, s_ln_g, s_ln_b, s_bias_f, s_bias_b, s_gla_g, s_final_g, s_conv_w, s_upf,
     s_upb) = _unpack(sum_devices(sall1, name="sum_devices_early"), early_shapes)
    s_norm_g = _unpack(sum_devices(sall2, name="sum_devices_late"), late_shapes)[0]
    r_mine, r_ctx = 1, 1 + 3 * n
    dm_all = sall2[:, r_mine:r_ctx].reshape(N_DEV, n, 3 * D)
    dm_full = jnp.concatenate([jnp.pad(dm_all, ((0, 0), (0, 8 - n), (0, 0))).reshape(8 * N_DEV, 3 * D),
                               sall2[:, r_ctx:r_ctx + 3].reshape(N_DEV, 3 * D)], 0)
    dm_shard = lax.dynamic_slice(dm_full, (0, q * nsh), (MOD_ROWS, nsh))
    cctx_rows = jnp.broadcast_to(c_ctx[None], (8, D))
    g_ada_w, g_ada_b, pq = ada_bwd(call, cctx_rows, dm_shard, dm_full, ada_w[0])

    place = jnp.concatenate([chip, core])
    ghw = chip_add(place, paw16, rbw, name="chip_add_w", tr=128)
    ghp = chip_add(place, pap16, rbp, name="chip_add_p", tr=384)
    gw_mine, gp_mine, pq_all = pair_share(ghw, ghp, pq)
    gp_mine = gp_mine.reshape(768, D)
    g_c_ctx = cctx_grad(pq_all, cctx_rows)[0]

    grads = dict(
        c_ctx=g_c_ctx, ada_w=g_ada_w[None], ada_b=g_ada_b, norm_g=s_norm_g,
        w_in=gw_mine.reshape(1, D, W_IN_SHARD), b_in=s_b_in,
        conv_w=lax.dynamic_slice(s_conv_w, (0, q * 256), (CONV_K, 256))[None], conv_b=s_conv_b,
        conv_ln_g=s_ln_g, conv_ln_b=s_ln_b, conv_proj=gp_mine[0:256][None],
        decay_up_fwd=lax.dynamic_slice(s_upf, (0, q * 128), (16, 128))[None], decay_bias_fwd=s_bias_f,
        decay_up_bwd=lax.dynamic_slice(s_upb, (0, q * 128), (16, 128))[None], decay_bias_bwd=s_bias_b,
        gla_norm_g=s_gla_g, gla_proj=gp_mine[256:512][None], w_out=gp_mine[512:768][None],
        final_norm_g=s_final_g[0])

    delta, new_m, new_v = {}, {}, {}
    for name in ["ada_w", "conv_proj", "gla_proj", "w_out"]:
        delta[name], new_m[name], new_v[name] = adamw2d(w[name], grads[name].reshape(w[name].shape), m[name], v[name],
                                                        name="adamw_" + name, tr=128)
    tr_ = lambda a: jnp.swapaxes(a, 1, 2)
    g_w_in_t = tr_(grads["w_in"])
    grads["w_in"] = tr_(g_w_in_t)
    d_, m_, v_ = adamw2d(tr_(w_in), g_w_in_t, tr_(m_w_in), tr_(v_w_in), name="adamw_w_in", tr=W_IN_SHARD, tcols=128)
    delta["w_in"], new_m["w_in"], new_v["w_in"] = tr_(d_), tr_(m_), tr_(v_)
    d_, m_, v_ = adamw_many([w[nm] for nm in SMALL_NAMES], [grads[nm].reshape(w[nm].shape) for nm in SMALL_NAMES],
                            [m[nm] for nm in SMALL_NAMES], [v[nm] for nm in SMALL_NAMES])
    for nm, a, b, cc in zip(SMALL_NAMES, d_, m_, v_):
        delta[nm], new_m[nm], new_v[nm] = a, b, cc

    grad_x = grad_x2.reshape(x.shape)
    return (loss, grad_x, *[grads[nm].reshape(w[nm].shape) for nm in WEIGHT_NAMES], *[delta[nm] for nm in WEIGHT_NAMES],
            *[new_m[nm] for nm in WEIGHT_NAMES], *[new_v[nm] for nm in WEIGHT_NAMES])
```

```python
import jax
import jax.numpy as jnp
from jax import lax
from jax.experimental import pallas as pl
from jax.experimental.pallas import tpu as pltpu

F32 = jnp.float32
BF16 = jnp.bfloat16
MESH = pl.DeviceIdType.MESH
HI = lax.Precision.HIGHEST

D = 1024
SEQ = 2048
GRID_W = 64
GRID_H = SEQ // GRID_W
NCTX = 256
SEQ_ALL = SEQ + NCTX
EPS = 1e-6
CONV_K = 31
CONV_PAD = CONV_K // 2
HEADS = 4
HEAD_K = 128
HEAD_V = 256
GLA_DK = HEADS * HEAD_K
GATE_TAU = 16.0
Q_SCALE = HEAD_K ** -0.5
CHUNK = 64
NCHUNK = SEQ_ALL // CHUNK
NCHUNK_LAT = SEQ // CHUNK
NCHUNK_CTX = NCHUNK - NCHUNK_LAT
SUB = 64
NSUB = CHUNK // SUB
N_IN = 8224
W3 = 2176
O3_V, O3_Q, O3_K, O3_AB = 0, 1024, 1536, 2048

ADAM_LR, ADAM_B1, ADAM_B2, ADAM_EPS, ADAM_WD, ADAM_STEP = 0.001, 0.9, 0.999, 1e-08, 0.01, 10
VMEM_LIMIT = 56 * 1024 * 1024

N_CHIPS = 4
N_DEV = 8
W_IN_SHARD = N_IN // N_CHIPS
MOD_ROWS = 72


def _pallas(body, **kw):
    return pl.pallas_call(body, **kw)


def _params(sem=None, **kw):
    if sem is not None:
        kw["dimension_semantics"] = sem
    return pltpu.CompilerParams(vmem_limit_bytes=VMEM_LIMIT, **kw)


def _sigmoid(v):
    return 1.0 / (1.0 + jnp.exp(-v))


def _silu(v):
    return v * _sigmoid(v)


def _dsilu(v):
    s = _sigmoid(v)
    return s * (1.0 + v * (1.0 - s))


def _log_sigmoid(v):
    return jnp.minimum(v, 0.0) - jnp.log(1.0 + jnp.exp(-jnp.abs(v)))


def _dot(a, b, dims, precision=None):
    return lax.dot_general(a, b, (dims, ((), ())), preferred_element_type=F32, precision=precision)


def _nn(a, b, precision=None):
    return _dot(a, b, ((1,), (0,)), precision)


def _nt(a, b, precision=None):
    return _dot(a, b, ((1,), (1,)), precision)


def _tn(a, b, precision=None):
    return _dot(a, b, ((0,), (0,)), precision)


def _b16(v):
    return v.astype(BF16)


def matmul_nn(a, b, *, name, m, tm, tn, out_dtype):
    k = a.shape[1]
    n = b.shape[1]

    def body(a_ref, b_ref, o_ref):
        o_ref[...] = _nn(a_ref[...], b_ref[...]).astype(o_ref.dtype)

    return _pallas(
        body, name=name, grid=(n // tn, m // tm),
        in_specs=[pl.BlockSpec((tm, k), lambda j, i: (i, 0)), pl.BlockSpec((k, tn), lambda j, i: (0, j))],
        out_specs=pl.BlockSpec((tm, tn), lambda j, i: (i, j)),
        out_shape=jax.ShapeDtypeStruct((m, n), out_dtype),
        compiler_params=_params(("parallel", "parallel")),
    )(a, b)


def proj_all(u, ws, bs, rows, p_sh, *, tm):
    k = u.shape[1]
    n_g = len(ws)
    tns = [w.shape[1] if w.shape[1] % 1024 else 1024 for w in ws]
    mts = [r // tm for r in rows]
    cnts = [(w.shape[1] // tn) * mt for w, tn, mt in zip(ws, tns, mts)]
    los = [sum(cnts[:g]) for g in range(n_g)]
    n_steps = sum(cnts)

    def rel(s, g):
        return jnp.clip(s - los[g], 0, cnts[g] - 1)

    def active(s, g):
        return (s >= los[g]) & (s < los[g] + cnts[g])

    def u_row(s):
        r = 0
        for g in range(n_g):
            r = r + jnp.where(active(s, g), rel(s, g) % mts[g], 0)
        return r

    def body(*refs):
        u_ref = refs[0]
        w_refs, b_refs = refs[1:1 + n_g], refs[1 + n_g:1 + 2 * n_g]
        p_ref = refs[1 + 2 * n_g]
        o_refs = refs[2 + 2 * n_g:2 + 3 * n_g]
        pall_ref = refs[2 + 3 * n_g]
        w_send, w_recv, h_send, h_recv = refs[3 + 3 * n_g:]
        s = pl.program_id(0)
        for g in range(n_g):
            @pl.when(active(s, g))
            def _(g=g):
                o_refs[g][...] = (_nn(u_ref[...], w_refs[g][...]) + b_refs[g][...]).astype(o_refs[g].dtype)

        x, y, c, chips = _place()
        q = 2 * x + y
        mine = _half_chunks(c, (0, p_ref.shape[0]), 16, which=(1,))
        other = _half_chunks(1 - c, (0, p_ref.shape[0]), 16, which=(1,))
        nb = len(mine)

        def bulk():
            return [[_remote(p_ref.at[rws], pall_ref.at[q, rws], w_send.at[pj * nb + pi], w_recv.at[pj * nb + pi],
                             (*chips[pj], c)) for pi, (_, rws) in enumerate(mine)] for pj in range(3)]

        @pl.when(s == 0)
        def _():
            for cp in sum(bulk(), []):
                cp.start()

        @pl.when(s == n_steps - 1)
        def _():
            handed = []
            for pj, (cx, cy) in enumerate(chips):
                for pi, (_, rws) in enumerate(mine):
                    bulk()[pj][pi].wait_recv()
                    cp = _remote(pall_ref.at[2 * cx + cy, rws], pall_ref.at[2 * cx + cy, rws],
                                 h_send.at[pj * nb + pi], h_recv.at[pj * nb + pi], (x, y, 1 - c))
                    cp.start()
                    handed.append(cp)
            for pj, (cx, cy) in enumerate(chips):
                for pi, (_, rws) in enumerate(other):
                    _remote(pall_ref.at[2 * cx + cy, rws], pall_ref.at[2 * cx + cy, rws],
                            h_send.at[pj * nb + pi], h_recv.at[pj * nb + pi], (x, y, 1 - c)).wait_recv()
            for cp in sum(bulk(), []) + handed:
                cp.wait_send()

    any_spec = pl.BlockSpec(memory_space=pl.ANY)
    in_specs = [pl.BlockSpec((tm, k), lambda s: (u_row(s), 0))]
    in_specs += [pl.BlockSpec((k, tns[g]), lambda s, g=g: (0, rel(s, g) // mts[g])) for g in range(n_g)]
    in_specs += [pl.BlockSpec((1, tns[g]), lambda s, g=g: (0, rel(s, g) // mts[g])) for g in range(n_g)]
    in_specs.append(any_spec)
    out_specs = [pl.BlockSpec((tm, tns[g]), lambda s, g=g: (rel(s, g) % mts[g], rel(s, g) // mts[g])) for g in range(n_g)]
    out_specs.append(any_spec)
    out_shape = [jax.ShapeDtypeStruct((rows[g], ws[g].shape[1]), BF16) for g in range(n_g)]
    out_shape.append(jax.ShapeDtypeStruct((N_CHIPS,) + p_sh.shape, p_sh.dtype))
    return _pallas(
        body, name="proj_all", grid=(n_steps,), in_specs=in_specs, out_specs=out_specs, out_shape=out_shape,
        scratch_shapes=[pltpu.SemaphoreType.DMA((3 * P_ROW_CHUNKS,)) for _ in range(4)],
        compiler_params=_params(("arbitrary",)),
    )(u, *ws, *bs, p_sh)


def matmul_tn(a, b, *, name, t, tn, tt, colsum=False, swap=None):
    m = a.shape[1]
    n = b.shape[1]
    nj, ns = n // tn, t // tt
    n_out = 2 if colsum else 1
    n_sw = 0 if swap is None else len(swap)

    def body(a_ref, b_ref, *rest):
        o_ref = rest[n_sw]
        cs_ref = rest[n_sw + 1] if colsum else None
        j, s = pl.program_id(0), pl.program_id(1)

        if swap is not None:
            g_refs = rest[:n_sw]
            got_refs = rest[n_sw + n_out:2 * n_sw + n_out]
            sems = rest[2 * n_sw + n_out:]

            @pl.when((j == 0) & (s == 0))
            def _():
                for cp in _pair_copies(g_refs, None, got_refs, None, *sems):
                    cp.start()

            @pl.when((j == nj - 1) & (s == ns - 1))
            def _():
                for cp in _pair_copies(g_refs, None, got_refs, None, *sems):
                    cp.wait_recv()
                for cp in _pair_copies(g_refs, None, got_refs, None, *sems):
                    cp.wait_send()

        @pl.when(s == 0)
        def _():
            o_ref[...] = jnp.zeros_like(o_ref)
            if colsum:
                cs_ref[...] = jnp.zeros_like(cs_ref)
        o_ref[...] += _tn(a_ref[...], b_ref[...])
        if colsum:
            cs_ref[...] += jnp.sum(b_ref[...].astype(F32), axis=0, keepdims=True)

    in_specs = [pl.BlockSpec((tt, m), lambda j, s: (s, 0)), pl.BlockSpec((tt, tn), lambda j, s: (s, j))]
    out_specs = [pl.BlockSpec((m, tn), lambda j, s: (0, j))]
    out_shape = [jax.ShapeDtypeStruct((m, n), F32)]
    if colsum:
        out_specs.append(pl.BlockSpec((1, tn), lambda j, s: (0, j)))
        out_shape.append(jax.ShapeDtypeStruct((1, n), F32))
    args, scratch = [a, b], []
    if swap is not None:
        any_spec = pl.BlockSpec(memory_space=pl.ANY)
        in_specs += [any_spec] * n_sw
        out_specs += [any_spec] * n_sw
        out_shape += _pair_got_shapes(swap, None)
        args += list(swap)
        scratch = [pltpu.SemaphoreType.DMA((_pair_count(swap, None),)), pltpu.SemaphoreType.DMA((_pair_count(swap, None),))]
    return _pallas(
        body, name=name, grid=(nj, ns), in_specs=in_specs, out_specs=out_specs, out_shape=out_shape,
        scratch_shapes=scratch,
        compiler_params=_params(("parallel" if swap is None else "arbitrary", "arbitrary")),
    )(*args)


def dgrad_norm_bwd(dps, wts, paw, x2, ctx2, dh, scale1, norm_g, *, tm):
    t, tc = x2.shape[0], ctx2.shape[0]
    t_all = t + tc
    n_lat, n_ctx = t // tm, tc // tm
    n_tiles = n_lat + n_ctx
    n_samples = scale1.shape[0] - 1
    tps = n_lat // n_samples
    n_grp = n_samples + 1
    n_g = len(dps)
    whole = [g for g in range(n_g) if dps[g].shape[0] == t_all]
    latent = [g for g in range(n_g) if dps[g].shape[0] != t_all]

    def body(*refs):
        dp_refs, w_refs = refs[:n_g], refs[n_g:2 * n_g]
        (paw_ref, x_ref, c_ref, dh_ref, sc_ref, g_ref, dx_ref, dsh_ref, dsc_ref, dg_ref, rbw_ref,
         du_buf, b_send, b_recv) = refs[2 * n_g:]
        i = pl.program_id(0)

        def exchange():
            x, y, c, chips = _place()
            chunks = _half_chunks(0, (2 * paw_ref.shape[1],), 16, which=(0,))
            return [_remote(paw_ref.at[2 * cx + cy, rows], rbw_ref.at[j, rows],
                            b_send.at[j * N_BULK + k], b_recv.at[j * N_BULK + k], (cx, cy, c))
                    for j, (cx, cy) in enumerate(chips) for k, (_, rows) in enumerate(chunks)]

        @pl.when(i == 0)
        def _():
            for cp in exchange():
                cp.start()

        acc = None
        for g in whole:
            part = _nt(dp_refs[g][...], w_refs[g][...])
            acc = part if acc is None else acc + part
        du_buf[...] = acc

        @pl.when(i < n_lat)
        def _():
            lat = None
            for g in latent:
                part = _nt(dp_refs[g][...], w_refs[g][...])
                lat = part if lat is None else lat + part
            du_buf[...] += lat

        duv = du_buf[...]
        xv = jnp.where(i < n_lat, x_ref[...], c_ref[...])
        rs = lax.rsqrt(jnp.mean(xv * xv, axis=-1, keepdims=True) + EPS)
        xh = xv * rs
        n = xh * g_ref[...]
        dn = duv * sc_ref[0]
        dxh = dn * g_ref[...]
        dx = rs * (dxh - xh * jnp.mean(dxh * xh, axis=-1, keepdims=True))

        @pl.when(i < n_lat)
        def _():
            dx_ref[...] = dx + dh_ref[...]

        @pl.when((i % tps == 0) & (i <= n_lat))
        def _():
            dsh_ref[...] = jnp.zeros_like(dsh_ref)
            dsc_ref[...] = jnp.zeros_like(dsc_ref)

        @pl.when(i == 0)
        def _():
            dg_ref[...] = jnp.zeros_like(dg_ref)

        dsh_ref[0] += jnp.sum(duv, axis=0, keepdims=True)
        dsc_ref[0] += jnp.sum(duv * n, axis=0, keepdims=True)
        dg_ref[...] += jnp.sum(dn * xh, axis=0, keepdims=True)

        @pl.when(i == n_tiles - 1)
        def _():
            for cp in exchange():
                cp.wait_recv()
            for cp in exchange():
                cp.wait_send()

    lat = lambda i: (jnp.minimum(i, n_lat - 1), 0)
    grp = lambda i: (jnp.minimum(i // tps, n_samples), 0, 0)
    in_specs = []
    for g, dp in enumerate(dps):
        nrow = dp.shape[0] // tm
        in_specs.append(pl.BlockSpec((tm, dp.shape[1]), lambda i, nrow=nrow: (jnp.minimum(i, nrow - 1), 0)))
    for w in wts:
        in_specs.append(pl.BlockSpec(w.shape, lambda i: (0, 0), pipeline_mode=pl.Buffered(1)))
    any_spec = pl.BlockSpec(memory_space=pl.ANY)
    in_specs += [any_spec,
                 pl.BlockSpec((tm, D), lat), pl.BlockSpec((tm, D), lambda i: (jnp.maximum(i - n_lat, 0), 0)),
                 pl.BlockSpec((tm, D), lat), pl.BlockSpec((1, 1, D), grp), pl.BlockSpec((1, D), lambda i: (0, 0))]
    return _pallas(
        body, name="dgrad_norm_bwd", grid=(n_tiles,), in_specs=in_specs,
        out_specs=[pl.BlockSpec((tm, D), lat), pl.BlockSpec((1, 1, D), grp), pl.BlockSpec((1, 1, D), grp),
                   pl.BlockSpec((1, D), lambda i: (0, 0)), any_spec],
        out_shape=[jax.ShapeDtypeStruct((t, D), F32), jax.ShapeDtypeStruct((n_grp, 1, D), F32),
                   jax.ShapeDtypeStruct((n_grp, 1, D), F32), jax.ShapeDtypeStruct((1, D), F32),
                   jax.ShapeDtypeStruct((3,) + paw.shape[1:], paw.dtype)],
        scratch_shapes=[pltpu.VMEM((tm, D), F32), pltpu.SemaphoreType.DMA((3 * N_BULK,)),
                        pltpu.SemaphoreType.DMA((3 * N_BULK,))],
        compiler_params=_params(("arbitrary",)),
    )(*dps, *wts, paw, x2, ctx2, dh, scale1, norm_g)


TM_NORM = 512


def norm_mod_fwd(x2, ctx2, scale1, shift, norm_g):
    t = x2.shape[0]
    n_lat = t // TM_NORM
    assert ctx2.shape[0] == TM_NORM
    n_samples = scale1.shape[0] - 1
    tps = n_lat // n_samples

    def body(x_ref, c_ref, sc_ref, sh_ref, g_ref, u_ref):
        i = pl.program_id(0)
        xv = jnp.where(i < n_lat, x_ref[...], c_ref[...])
        rs = lax.rsqrt(jnp.mean(xv * xv, axis=-1, keepdims=True) + EPS)
        u = xv * rs * g_ref[...] * sc_ref[0] + sh_ref[0]
        u_ref[...] = u.astype(u_ref.dtype)

    grp = lambda i: (jnp.minimum(i // tps, n_samples), 0, 0)
    return _pallas(
        body, name="norm_mod_fwd", grid=(n_lat + 1,),
        in_specs=[pl.BlockSpec((TM_NORM, D), lambda i: (jnp.minimum(i, n_lat - 1), 0)),
                  pl.BlockSpec((TM_NORM, D), lambda i: (0, 0)),
                  pl.BlockSpec((1, 1, D), grp), pl.BlockSpec((1, 1, D), grp),
                  pl.BlockSpec((1, D), lambda i: (0, 0))],
        out_specs=pl.BlockSpec((TM_NORM, D), lambda i: (i, 0)),
        out_shape=jax.ShapeDtypeStruct((t + TM_NORM, D), BF16),
        compiler_params=_params(("parallel",)),
    )(x2, ctx2, scale1, shift, norm_g)


CONV_CB = 256
CONV_NCB = D // CONV_CB
H_OFF = 16


H_CB = 128
H_SPAN = GRID_W + 2 * H_OFF - 8


def _conv_scratch(vertical):
    if vertical:
        return [pltpu.VMEM((GRID_H + 2 * CONV_PAD, GRID_W, CONV_CB), F32)]
    return [pltpu.VMEM((GRID_H, GRID_W + 2 * H_OFF, H_CB), F32), pltpu.VMEM((7, GRID_H, H_SPAN, H_CB), F32)]


def _conv_fill(bufs, img, vertical):
    pad_ref = bufs[0]
    pad_ref[...] = jnp.zeros_like(pad_ref)
    if vertical:
        pad_ref[pl.ds(CONV_PAD, GRID_H)] = img
        return
    pad_ref[:, pl.ds(H_OFF, GRID_W), :] = img

    def shift(r, carry):
        for s in range(1, 8):
            bufs[1][s - 1, r] = pad_ref[r, pl.ds(s, H_SPAN), :]
        return carry

    lax.fori_loop(0, GRID_H, shift, 0)


def _conv_window(bufs, k, vertical, r, w0=0, nw=GRID_W, lanes=slice(None)):
    if vertical:
        return bufs[0][r + k, pl.ds(w0, nw), lanes]
    off = H_OFF - CONV_PAD + k
    if off % 8 == 0:
        return bufs[0][r, pl.ds(off + w0, nw), lanes]
    return bufs[1][off % 8 - 1, r, pl.ds(off - off % 8 + w0, nw), lanes]


def _conv_col_blocks(vertical):
    if vertical:
        return [pl.ds(0, CONV_CB)]
    return [pl.ds(i * H_CB, H_CB) for i in range(CONV_CB // H_CB)]


def _rows(r):
    return pl.ds(pl.multiple_of(r * GRID_W, GRID_W), GRID_W)


def conv_fwd(p1, conv_w, conv_b, n_samples):
    t = n_samples * SEQ

    def make(vertical, prev):
        n_buf = len(_conv_scratch(vertical))

        def body(gv_ref, gg_ref, w_ref, b_ref, *rest):
            o_ref, bufs = rest[-1 - n_buf], rest[-n_buf:]
            for cols in _conv_col_blocks(vertical):
                a = gv_ref[:, cols].astype(F32) * _sigmoid(gg_ref[:, cols].astype(F32))
                _conv_fill(bufs, a.reshape(GRID_H, GRID_W, a.shape[-1]), vertical)

                def row(r, carry, cols=cols):
                    acc = jnp.zeros((GRID_W, cols.size), F32) + b_ref[:, cols]
                    for k in range(CONV_K):
                        acc = acc + _conv_window(bufs, k, vertical, r) * w_ref[pl.ds(k, 1), cols]
                    o_ref[_rows(r), cols] = acc
                    return carry

                lax.fori_loop(0, GRID_H, row, 0)

        cb0 = CONV_NCB // 2 if vertical else 0
        in_specs = [pl.BlockSpec((SEQ, CONV_CB), lambda b, j: (b, 2 * (cb0 + j))),
                    pl.BlockSpec((SEQ, CONV_CB), lambda b, j: (b, 2 * (cb0 + j) + 1)),
                    pl.BlockSpec((CONV_K + 1, CONV_CB), lambda b, j: (0, cb0 + j)),
                    pl.BlockSpec((1, CONV_CB), lambda b, j: (0, cb0 + j))]
        args = [p1, p1, conv_w, conv_b]
        aliases = {}
        if prev is not None:
            in_specs.append(pl.BlockSpec(memory_space=pl.ANY))
            args.append(prev)
            aliases = {4: 0}
        return _pallas(
            body, name="conv_fwd_v" if vertical else "conv_fwd_h", grid=(n_samples, CONV_NCB // 2),
            in_specs=in_specs,
            out_specs=pl.BlockSpec((SEQ, CONV_CB), lambda b, j: (b, cb0 + j)),
            out_shape=jax.ShapeDtypeStruct((t, D), F32),
            scratch_shapes=_conv_scratch(vertical),
            input_output_aliases=aliases,
            compiler_params=_params(("parallel", "parallel")),
        )(*args)

    return make(True, make(False, None))


def conv_bwd(p1, daconv, conv_w, n_samples):
    t = n_samples * SEQ

    def make(vertical, prev):
        n_buf = len(_conv_scratch(vertical))

        def body(gv_ref, gg_ref, dy_ref, w_ref, *rest):
            dp_ref, dw_ref, db_ref = rest[-3 - 2 * n_buf - 1:-2 * n_buf - 1]
            a_bufs, d_bufs, da_ref = rest[-2 * n_buf - 1:-n_buf - 1], rest[-n_buf - 1:-1], rest[-1]
            for cols in _conv_col_blocks(vertical):
                width = cols.size
                gv = gv_ref[:, cols].astype(F32)
                sg = _sigmoid(gg_ref[:, cols].astype(F32))
                _conv_fill(a_bufs, (gv * sg).reshape(GRID_H, GRID_W, width), vertical)
                _conv_fill(d_bufs, dy_ref[:, cols].reshape(GRID_H, GRID_W, width), vertical)

                def row(r, carry, cols=cols, width=width):
                    acc = jnp.zeros((GRID_W, width), F32)
                    for k in range(CONV_K):
                        acc = acc + _conv_window(d_bufs, CONV_K - 1 - k, vertical, r) * w_ref[pl.ds(k, 1), cols]
                    da_ref[_rows(r), cols] = acc
                    return carry

                lax.fori_loop(0, GRID_H, row, 0)
                da = da_ref[:, cols]
                dp_ref[:, pl.ds(cols.start, width)] = (da * sg).astype(dp_ref.dtype)
                dp_ref[:, pl.ds(CONV_CB + cols.start, width)] = (da * gv * sg * (1.0 - sg)).astype(dp_ref.dtype)

                for lb in range(width // 128):
                    lanes = pl.ds(lb * 128, 128)
                    dy_lanes = pl.ds(cols.start + lb * 128, 128)

                    def wrow(r, accs, lanes=lanes, dy_lanes=dy_lanes):
                        for w0 in range(0, GRID_W, 8):
                            dyv = dy_ref[pl.ds(pl.multiple_of(r * GRID_W, GRID_W) + w0, 8), dy_lanes]
                            accs = tuple(accs[k] + _conv_window(a_bufs, k, vertical, r, w0, 8, lanes) * dyv
                                         for k in range(CONV_K))
                        return accs

                    accs = lax.fori_loop(0, GRID_H, wrow, tuple(jnp.zeros((8, 128), F32) for _ in range(CONV_K)))
                    for k in range(CONV_K):
                        dw_ref[0, pl.ds(k, 1), dy_lanes] = jnp.sum(accs[k], axis=0, keepdims=True)
            dw_ref[0, pl.ds(CONV_K, 1), :] = jnp.zeros((1, CONV_CB), F32)
            db_ref[0] = jnp.sum(dy_ref[...], axis=0, keepdims=True)

        cb0 = CONV_NCB // 2 if vertical else 0
        in_specs = [pl.BlockSpec((SEQ, CONV_CB), lambda b, j: (b, 2 * (cb0 + j))),
                    pl.BlockSpec((SEQ, CONV_CB), lambda b, j: (b, 2 * (cb0 + j) + 1)),
                    pl.BlockSpec((SEQ, CONV_CB), lambda b, j: (b, cb0 + j)),
                    pl.BlockSpec((CONV_K + 1, CONV_CB), lambda b, j: (0, cb0 + j))]
        args = [p1, p1, daconv, conv_w]
        aliases = {}
        if prev is not None:
            in_specs += [pl.BlockSpec(memory_space=pl.ANY)] * 3
            args += list(prev)
            aliases = {4: 0, 5: 1, 6: 2}
        return _pallas(
            body, name="conv_bwd_v" if vertical else "conv_bwd_h", grid=(n_samples, CONV_NCB // 2),
            in_specs=in_specs,
            out_specs=[pl.BlockSpec((SEQ, 2 * CONV_CB), lambda b, j: (b, cb0 + j)),
                       pl.BlockSpec((1, CONV_K + 1, CONV_CB), lambda b, j: (b, 0, cb0 + j)),
                       pl.BlockSpec((1, 1, CONV_CB), lambda b, j: (b, 0, cb0 + j))],
            out_shape=[jax.ShapeDtypeStruct((t, 2 * D), BF16),
                       jax.ShapeDtypeStruct((n_samples, CONV_K + 1, D), F32),
                       jax.ShapeDtypeStruct((n_samples, 1, D), F32)],
            scratch_shapes=_conv_scratch(vertical) + _conv_scratch(vertical) + [pltpu.VMEM((SEQ, CONV_CB), F32)],
            input_output_aliases=aliases,
            compiler_params=_params(("parallel", "parallel")),
        )(*args)

    return make(True, make(False, None))


TM_EW = 256


def ln_gate_fwd(aconv, z, ln_g, ln_b):
    t = aconv.shape[0]

    def body(a_ref, z_ref, g_ref, b_ref, o_ref):
        a = a_ref[...]
        mu = jnp.mean(a, axis=-1, keepdims=True)
        xc = a - mu
        rstd = lax.rsqrt(jnp.mean(xc * xc, axis=-1, keepdims=True) + EPS)
        l = xc * rstd * g_ref[...] + b_ref[...]
        o_ref[...] = (_silu(l) * _silu(z_ref[...].astype(F32))).astype(o_ref.dtype)

    row = pl.BlockSpec((TM_EW, D), lambda i: (i, 0))
    vec = pl.BlockSpec((1, D), lambda i: (0, 0))
    return _pallas(
        body, name="ln_gate_fwd", grid=(t // TM_EW,), in_specs=[row, row, vec, vec], out_specs=row,
        out_shape=jax.ShapeDtypeStruct((t, D), BF16), compiler_params=_params(("parallel",)),
    )(aconv, z, ln_g, ln_b)


def ln_gate_bwd(aconv, z, dyc, conv_proj, ln_g, ln_b):
    t = aconv.shape[0]

    def body(a_ref, z_ref, d_ref, w_ref, g_ref, b_ref, da_ref, dz_ref, dg_ref, db_ref):
        a = a_ref[...]
        zv = z_ref[...].astype(F32)
        dac_v = _nt(d_ref[...], w_ref[...])
        mu = jnp.mean(a, axis=-1, keepdims=True)
        xc = a - mu
        rstd = lax.rsqrt(jnp.mean(xc * xc, axis=-1, keepdims=True) + EPS)
        xh = xc * rstd
        l = xh * g_ref[...] + b_ref[...]
        dz_ref[...] = (dac_v * _silu(l) * _dsilu(zv)).astype(dz_ref.dtype)
        dl = dac_v * _silu(zv) * _dsilu(l)
        dxh = dl * g_ref[...]
        da_ref[...] = rstd * (dxh - jnp.mean(dxh, axis=-1, keepdims=True)
                              - xh * jnp.mean(dxh * xh, axis=-1, keepdims=True))

        @pl.when(pl.program_id(0) == 0)
        def _():
            dg_ref[...] = jnp.zeros_like(dg_ref)
            db_ref[...] = jnp.zeros_like(db_ref)

        dg_ref[...] += jnp.sum(dl * xh, axis=0, keepdims=True)
        db_ref[...] += jnp.sum(dl, axis=0, keepdims=True)

    row = pl.BlockSpec((TM_EW, D), lambda i: (i, 0))
    vec = pl.BlockSpec((1, D), lambda i: (0, 0))
    return _pallas(
        body, name="ln_gate_bwd", grid=(t // TM_EW,),
        in_specs=[row, row, row, pl.BlockSpec((D, D), lambda i: (0, 0)), vec, vec],
        out_specs=[row, row, vec, vec],
        out_shape=[jax.ShapeDtypeStruct((t, D), F32), jax.ShapeDtypeStruct((t, D), BF16),
                   jax.ShapeDtypeStruct((1, D), F32), jax.ShapeDtypeStruct((1, D), F32)],
        compiler_params=_params(("arbitrary",)),
    )(aconv, z, dyc, conv_proj, ln_g, ln_b)


TM_PREP = 256
PREP_LAT = SEQ // TM_PREP
PREP_ALL = SEQ_ALL // TM_PREP


def _chunk_tri(n, upper):
    r = lax.broadcasted_iota(jnp.int32, (n, n), 0)
    c = lax.broadcasted_iota(jnp.int32, (n, n), 1)
    same = (r // CHUNK) == (c // CHUNK)
    keep = (c >= r) if upper else (c <= r)
    return jnp.where(same & keep, 1.0, 0.0).astype(F32)


def _split3(v):
    hi = v.astype(BF16)
    r1 = v - hi.astype(F32)
    mid = r1.astype(BF16)
    lo = (r1 - mid.astype(F32)).astype(BF16)
    return jnp.stack([hi, mid, lo])


def _chunk_sums(v, upper):
    tri = _chunk_tri(v.shape[0], upper).astype(BF16)
    pieces = _split3(v)
    return (_nn(tri, pieces[0]) + _nn(tri, pieces[1])) + _nn(tri, pieces[2])


def _gate_logits(ab, up3_ref, bias_ref):
    assert ab.dtype == BF16
    return ((_nn(ab, up3_ref[0]) + _nn(ab, up3_ref[1])) + _nn(ab, up3_ref[2])) + bias_ref[...]


def _prep_tile_maps(n_samples):
    n_lat = n_samples * PREP_LAT

    def seq_map(i):
        return jnp.where(i < n_lat, i // PREP_LAT, i - n_lat), jnp.where(i < n_lat, i % PREP_LAT, PREP_LAT)

    return n_lat, seq_map


def gla_prep_fwd(p3, upf, upb, bias_f, bias_b, n_samples):
    n_lat, seq_map = _prep_tile_maps(n_samples)
    n_tiles = n_lat + n_samples

    def body(v_ref, q_ref, k_ref, ab_ref, upf_ref, upb_ref, bf_ref, bb_ref, qo, ko, vo, cf, cb):
        i = pl.program_id(0)
        qo[0] = jnp.where(i < n_lat, q_ref[...].astype(F32) * Q_SCALE, 0.0)
        ko[0] = k_ref[...]
        vo[0] = v_ref[...]
        ab = ab_ref[...]
        gf = _log_sigmoid(_gate_logits(ab, upf_ref, bf_ref)) * (1.0 / GATE_TAU)
        gb = _log_sigmoid(_gate_logits(ab, upb_ref, bb_ref)) * (1.0 / GATE_TAU)
        cf[0] = _chunk_sums(gf, False)
        cb[0] = _chunk_sums(gb, True)

    def o_spec(w):
        return pl.BlockSpec((1, TM_PREP, w), lambda i: (*seq_map(i), 0))

    full = lambda shape: pl.BlockSpec(shape, lambda i: (0,) * len(shape))
    return _pallas(
        body, name="gla_prep_fwd", grid=(n_tiles,),
        in_specs=[pl.BlockSpec((TM_PREP, 1024), lambda i: (i, O3_V // 1024)),
                  pl.BlockSpec((TM_PREP, 512), lambda i: (i, O3_Q // 512)),
                  pl.BlockSpec((TM_PREP, 512), lambda i: (i, O3_K // 512)),
                  pl.BlockSpec((TM_PREP, 128), lambda i: (i, O3_AB // 128)),
                  full((3, 128, GLA_DK)), full((3, 128, GLA_DK)), full((1, GLA_DK)), full((1, GLA_DK))],
        out_specs=[o_spec(GLA_DK), o_spec(GLA_DK), o_spec(D), o_spec(GLA_DK), o_spec(GLA_DK)],
        out_shape=[jax.ShapeDtypeStruct((n_samples, SEQ_ALL, GLA_DK), F32),
                   jax.ShapeDtypeStruct((n_samples, SEQ_ALL, GLA_DK), p3.dtype),
                   jax.ShapeDtypeStruct((n_samples, SEQ_ALL, D), p3.dtype),
                   jax.ShapeDtypeStruct((n_samples, SEQ_ALL, GLA_DK), F32),
                   jax.ShapeDtypeStruct((n_samples, SEQ_ALL, GLA_DK), F32)],
        compiler_params=_params(("parallel",)),
    )(p3, p3, p3, p3, upf, upb, bias_f, bias_b)


def gla_prep_bwd(p3, dq_f, dq_b, dk_f, dk_b, dv_f, dv_b, dc_f, dc_b, upf, upb, bias_f, bias_b, n_samples):
    n_lat, seq_map = _prep_tile_maps(n_samples)
    n_tiles = n_lat + n_samples

    def body(ab_ref, dqf, dqb, dkf, dkb, dvf, dvb, dcf, dcb, upf_ref, upb_ref, bf_ref, bb_ref,
             dp_ref, duf_ref, dub_ref, dbf_ref, dbb_ref):
        i = pl.program_id(0)
        dp_ref[:, pl.ds(O3_V, D)] = (dvf[0] + dvb[0]).astype(dp_ref.dtype)
        dq = jnp.where(i < n_lat, (dqf[0] + dqb[0]) * Q_SCALE, 0.0)
        dp_ref[:, pl.ds(O3_Q, GLA_DK)] = dq.astype(dp_ref.dtype)
        dp_ref[:, pl.ds(O3_K, GLA_DK)] = (dkf[0] + dkb[0]).astype(dp_ref.dtype)
        ab = ab_ref[...]
        zf = _gate_logits(ab, upf_ref, bf_ref)
        zb = _gate_logits(ab, upb_ref, bb_ref)
        dgf = _chunk_sums(dcf[0], True)
        dgb = _chunk_sums(dcb[0], False)
        dzf = _b16(dgf * (1.0 / GATE_TAU) * _sigmoid(-zf))
        dzb = _b16(dgb * (1.0 / GATE_TAU) * _sigmoid(-zb))
        dab = _nt(dzf, upf_ref[0]) + _nt(dzb, upb_ref[0])
        dp_ref[:, pl.ds(O3_AB, 128)] = dab.astype(dp_ref.dtype)

        @pl.when(i == 0)
        def _():
            duf_ref[...] = jnp.zeros_like(duf_ref)
            dub_ref[...] = jnp.zeros_like(dub_ref)
            dbf_ref[...] = jnp.zeros_like(dbf_ref)
            dbb_ref[...] = jnp.zeros_like(dbb_ref)

        duf_ref[...] += _tn(ab, dzf)
        dub_ref[...] += _tn(ab, dzb)
        dbf_ref[...] += jnp.sum(dzf.astype(F32), axis=0, keepdims=True)
        dbb_ref[...] += jnp.sum(dzb.astype(F32), axis=0, keepdims=True)

    def s_spec(w):
        return pl.BlockSpec((1, TM_PREP, w), lambda i: (*seq_map(i), 0))

    full = lambda shape: pl.BlockSpec(shape, lambda i: (0,) * len(shape))
    return _pallas(
        body, name="gla_prep_bwd", grid=(n_tiles,),
        in_specs=[pl.BlockSpec((TM_PREP, 128), lambda i: (i, O3_AB // 128)),
                  s_spec(GLA_DK), s_spec(GLA_DK), s_spec(GLA_DK), s_spec(GLA_DK), s_spec(D), s_spec(D),
                  s_spec(GLA_DK), s_spec(GLA_DK),
                  full((3, 128, GLA_DK)), full((3, 128, GLA_DK)), full((1, GLA_DK)), full((1, GLA_DK))],
        out_specs=[pl.BlockSpec((TM_PREP, W3), lambda i: (i, 0)),
                   full((128, GLA_DK)), full((128, GLA_DK)), full((1, GLA_DK)), full((1, GLA_DK))],
        out_shape=[jax.ShapeDtypeStruct((n_tiles * TM_PREP, W3), BF16),
                   jax.ShapeDtypeStruct((128, GLA_DK), F32), jax.ShapeDtypeStruct((128, GLA_DK), F32),
                   jax.ShapeDtypeStruct((1, GLA_DK), F32), jax.ShapeDtypeStruct((1, GLA_DK), F32)],
        compiler_params=_params(("arbitrary",)),
    )(p3, dq_f, dq_b, dk_f, dk_b, dv_f, dv_b, dc_f, dc_b, upf, upb, bias_f, bias_b)


def _sub_blocks(rev):
    if NSUB == 1:
        return [((0, CHUNK), CHUNK // 2, (0, CHUNK))]
    out = []
    for s in range(NSUB):
        rows = (s * SUB, SUB)
        if rev:
            ref = (s + 1) * SUB if s < NSUB - 1 else None
            cols = (s * SUB, CHUNK - s * SUB)
        else:
            ref = s * SUB - 1 if s > 0 else None
            cols = (0, (s + 1) * SUB)
        out.append((rows, ref, cols))
    return out


def _sub_mask(rows, cols, rev):
    r = rows[0] + lax.broadcasted_iota(jnp.int32, (rows[1], cols[1]), 0)
    c = cols[0] + lax.broadcasted_iota(jnp.int32, (rows[1], cols[1]), 1)
    return (c >= r) if rev else (c <= r)


def _sub_operands(qc, kc, cc, rows, ref, cols):
    cref = jnp.zeros((1, HEAD_K), F32) if ref is None else cc[ref:ref + 1]
    eq = jnp.exp(cc[rows[0]:rows[0] + rows[1]] - cref)
    ek = jnp.exp(cref - cc[cols[0]:cols[0] + cols[1]])
    qs = qc[rows[0]:rows[0] + rows[1]] * eq
    kk = kc[cols[0]:cols[0] + cols[1]] * ek
    return qs, kk, eq, ek


SCAN_ROWS = 256
SCAN_CHUNKS = SCAN_ROWS // CHUNK
SCAN_STEPS = SEQ_ALL // SCAN_ROWS
LAT_BLOCKS = SEQ // SCAN_ROWS


def _scan_block(t, rev):
    if rev:
        return SCAN_STEPS - 1 - t
    return jnp.where(t == 0, SCAN_STEPS - 1, t - 1)


def _scan_lat_block(t, rev):
    if rev:
        return jnp.minimum(SCAN_STEPS - 1 - t, LAT_BLOCKS - 1)
    return jnp.maximum(t - 1, 0)


def _head_cols(h):
    return pl.ds(h * HEAD_K, HEAD_K), pl.ds(h * HEAD_V, HEAD_V)


def gla_scan_fwd(q, k, v, cum, *, rev, name):
    n = q.shape[0]

    def body(q_ref, k_ref, v_ref, c_ref, o_ref, s_ref, sfin_ref, st):
        t = pl.program_id(1)

        @pl.when(t == 0)
        def _():
            st[...] = jnp.zeros_like(st)

        def chunk(j, carry):
            lj = SCAN_CHUNKS - 1 - j if rev else j
            r0 = lj * CHUNK
            rws = pl.ds(r0, CHUNK)
            for h in range(HEADS):
                kcols, vcols = _head_cols(h)
                qc, kc, cc = q_ref[0, rws, kcols], k_ref[0, rws, kcols], c_ref[0, rws, kcols]
                vc = v_ref[0, rws, vcols]
                s_in = st[h]
                s_ref[0, h, j] = _b16(s_in)
                edge = cc[0:1] if rev else cc[CHUNK - 1:CHUNK]
                ke = kc * jnp.exp(edge - cc)
                st[h] = s_in * jnp.exp(edge) + _tn(_b16(vc), _b16(ke))
                o_inter = _nt(_b16(qc * jnp.exp(cc)), _b16(s_in))
                vb = _b16(vc)
                for rows, ref, cols in _sub_blocks(rev):
                    qs, kk, _, _ = _sub_operands(qc, kc, cc, rows, ref, cols)
                    a = jnp.where(_sub_mask(rows, cols, rev), _nt(_b16(qs), _b16(kk)), 0.0)
                    o_s = _nn(_b16(a), vb[cols[0]:cols[0] + cols[1]])
                    o_ref[0, pl.ds(r0 + rows[0], rows[1]), vcols] = o_inter[rows[0]:rows[0] + rows[1]] + o_s
            return carry

        for j in range(SCAN_CHUNKS):
            chunk(j, 0)

        @pl.when(t == SCAN_STEPS - 1)
        def _():
            sfin_ref[0] = st[...]

    def spec(w):
        return pl.BlockSpec((1, SCAN_ROWS, w), lambda b, t: (b, _scan_block(t, rev), 0))

    return _pallas(
        body, name=name, grid=(n, SCAN_STEPS),
        in_specs=[spec(GLA_DK), spec(GLA_DK), spec(D), spec(GLA_DK)],
        out_specs=[pl.BlockSpec((1, SCAN_ROWS, D), lambda b, t: (b, _scan_lat_block(t, rev), 0)),
                   pl.BlockSpec((1, HEADS, SCAN_CHUNKS, HEAD_V, HEAD_K), lambda b, t: (b, 0, t, 0, 0)),
                   pl.BlockSpec((1, HEADS, HEAD_V, HEAD_K), lambda b, t: (b, 0, 0, 0))],
        out_shape=[jax.ShapeDtypeStruct((n, SEQ, D), F32),
                   jax.ShapeDtypeStruct((n, HEADS, NCHUNK, HEAD_V, HEAD_K), BF16),
                   jax.ShapeDtypeStruct((n, HEADS, HEAD_V, HEAD_K), F32)],
        scratch_shapes=[pltpu.VMEM((HEADS, HEAD_V, HEAD_K), F32)],
        compiler_params=_params(("parallel", "arbitrary")),
    )(q, k, v, cum)


def gla_scan_bwd(q, k, v, cum, s_all, s_fin, do, *, rev, name, rider=None):
    n = q.shape[0]

    def body(q_ref, k_ref, v_ref, c_ref, s_ref, sfin_ref, do_ref, *rest):
        if rider is not None:
            ride_in, rest = rest[0], rest[1:]
        dq_ref, dk_ref, dv_ref, dc_ref = rest[:4]
        if rider is not None:
            ride_out, rest = rest[4], rest[:4] + rest[5:]
        dst, s_next, dq_acc, dk_acc, dv_acc = rest[4:9]
        t = SCAN_STEPS - 1 - pl.program_id(1)

        if rider is not None:
            def copies():
                send, recv = rest[9], rest[10]
                if rider[0] == "swap":
                    return _pair_copies([], ride_in, [], ride_out, send, recv)
                x, y, c, chips = _place()
                return [_remote(ride_in.at[2 * cx + cy, rows], ride_out.at[pj, rows], send.at[pj * P_ROW_CHUNKS + pi],
                                recv.at[pj * P_ROW_CHUNKS + pi], (cx, cy, c))
                        for pj, (cx, cy) in enumerate(chips)
                        for pi, (_, rows) in enumerate(_half_chunks(0, (0, 2 * ride_in.shape[1]), 16, which=(1,)))]

            @pl.when((pl.program_id(0) == 0) & (pl.program_id(1) == 0))
            def _():
                for cp in copies():
                    cp.start()

            @pl.when((pl.program_id(0) == n - 1) & (pl.program_id(1) == SCAN_STEPS - 1))
            def _():
                for cp in copies():
                    cp.wait_recv()
                for cp in copies():
                    cp.wait_send()

        @pl.when(pl.program_id(1) == 0)
        def _():
            dst[...] = jnp.zeros_like(dst)
            s_next[...] = sfin_ref[0]

        def chunk(jj, carry):
            j = SCAN_CHUNKS - 1 - jj
            lj = SCAN_CHUNKS - 1 - j if rev else j
            rws = pl.ds(lj * CHUNK, CHUNK)
            for h in range(HEADS):
                kcols, vcols = _head_cols(h)
                qc, kc, cc = q_ref[0, rws, kcols], k_ref[0, rws, kcols], c_ref[0, rws, kcols]
                vc = v_ref[0, rws, vcols]
                doc = jnp.where(t > 0, do_ref[0, rws, vcols], 0.0)
                s_in = s_ref[0, h, j]
                s_out = s_next[h]
                ds_out = dst[h]
                edge = cc[0:1] if rev else cc[CHUNK - 1:CHUNK]
                e_q = jnp.exp(cc)
                e_k = jnp.exp(edge - cc)
                dob = _b16(doc)
                dsb = _b16(ds_out)
                dst[h] = ds_out * jnp.exp(edge) + _tn(dob, _b16(qc * e_q))
                s_next[h] = s_in.astype(F32)
                dq_acc[h] = e_q * _nn(dob, s_in)
                dk_acc[h] = e_k * _nn(_b16(vc), dsb)
                dv_acc[h] = _nt(_b16(kc * e_k), dsb)
                vb = _b16(vc)
                for rows, ref, cols in _sub_blocks(rev):
                    qs, kk, eq, ek = _sub_operands(qc, kc, cc, rows, ref, cols)
                    mask = _sub_mask(rows, cols, rev)
                    rsl = slice(rows[0], rows[0] + rows[1])
                    csl = pl.ds(cols[0], cols[1])
                    qsb, kkb = _b16(qs), _b16(kk)
                    a = jnp.where(mask, _nt(qsb, kkb), 0.0)
                    da = _b16(jnp.where(mask, _nt(dob[rsl], vb[cols[0]:cols[0] + cols[1]]), 0.0))
                    dq_acc[h, pl.ds(rows[0], rows[1]), :] += _nn(da, kkb) * eq
                    dk_acc[h, csl, :] += _tn(da, qsb) * ek
                    dv_acc[h, csl, :] += _tn(_b16(a), dob[rsl])
                dq = dq_acc[h]
                dk = dk_acc[h]
                dc = qc * dq - kc * dk
                bnd = jnp.sum(ds_out * s_out, axis=0, keepdims=True)
                edge_row = 0 if rev else CHUNK - 1
                is_edge = lax.broadcasted_iota(jnp.int32, (CHUNK, HEAD_K), 0) == edge_row
                dq_ref[0, rws, kcols] = dq
                dk_ref[0, rws, kcols] = dk
                dv_ref[0, rws, vcols] = dv_acc[h]
                dc_ref[0, rws, kcols] = dc + jnp.where(is_edge, bnd, 0.0)
            return carry

        for jj in range(SCAN_CHUNKS):
            chunk(jj, 0)

    def step_of(u):
        return SCAN_STEPS - 1 - u

    def spec(w):
        return pl.BlockSpec((1, SCAN_ROWS, w), lambda b, u: (b, _scan_block(step_of(u), rev), 0))

    in_specs = [spec(GLA_DK), spec(GLA_DK), spec(D), spec(GLA_DK),
                pl.BlockSpec((1, HEADS, SCAN_CHUNKS, HEAD_V, HEAD_K), lambda b, u: (b, 0, step_of(u), 0, 0)),
                pl.BlockSpec((1, HEADS, HEAD_V, HEAD_K), lambda b, u: (b, 0, 0, 0)),
                pl.BlockSpec((1, SCAN_ROWS, D), lambda b, u: (b, _scan_lat_block(step_of(u), rev), 0))]
    out_specs = [spec(GLA_DK), spec(GLA_DK), spec(D), spec(GLA_DK)]
    out_shape = [jax.ShapeDtypeStruct((n, SEQ_ALL, GLA_DK), F32), jax.ShapeDtypeStruct((n, SEQ_ALL, GLA_DK), F32),
                 jax.ShapeDtypeStruct((n, SEQ_ALL, D), F32), jax.ShapeDtypeStruct((n, SEQ_ALL, GLA_DK), F32)]
    scratch = [pltpu.VMEM((HEADS, HEAD_V, HEAD_K), F32), pltpu.VMEM((HEADS, HEAD_V, HEAD_K), F32),
               pltpu.VMEM((HEADS, CHUNK, HEAD_K), F32), pltpu.VMEM((HEADS, CHUNK, HEAD_K), F32),
               pltpu.VMEM((HEADS, CHUNK, HEAD_V), F32)]
    args = [q, k, v, cum, s_all, s_fin, do]
    if rider is not None:
        kind, arr = rider
        any_spec = pl.BlockSpec(memory_space=pl.ANY)
        in_specs.append(any_spec)
        out_specs.append(any_spec)
        args.append(arr)
        if kind == "swap":
            out_shape += _pair_got_shapes([], arr)
            n_cp = _pair_count([], arr)
        else:
            out_shape.append(jax.ShapeDtypeStruct((3,) + arr.shape[1:], arr.dtype))
            n_cp = 3 * P_ROW_CHUNKS
        scratch += [pltpu.SemaphoreType.DMA((n_cp,)), pltpu.SemaphoreType.DMA((n_cp,))]
    return _pallas(
        body, name=name, grid=(n, SCAN_STEPS), in_specs=in_specs, out_specs=out_specs, out_shape=out_shape,
        scratch_shapes=scratch,
        compiler_params=_params(("parallel" if rider is None else "arbitrary", "arbitrary")),
    )(*args)


def gla_out_fwd(o_f, o_b, r, gnorm):
    n = o_f.shape[0]
    tiles = SEQ // TM_EW

    def body(of_ref, ob_ref, r_ref, g_ref, og_ref):
        for h in range(HEADS):
            cols = pl.ds(h * HEAD_V, HEAD_V)
            o = of_ref[0, :, cols] + ob_ref[0, :, cols]
            rs = lax.rsqrt(jnp.mean(o * o, axis=-1, keepdims=True) + EPS)
            og_ref[:, cols] = (o * rs * g_ref[...] * _silu(r_ref[:, cols].astype(F32))).astype(og_ref.dtype)

    ospec = pl.BlockSpec((1, TM_EW, D), lambda b, j: (b, j, 0))
    row = pl.BlockSpec((TM_EW, D), lambda b, j: (b * tiles + j, 0))
    return _pallas(
        body, name="gla_out_fwd", grid=(n, tiles),
        in_specs=[ospec, ospec, row, pl.BlockSpec((1, HEAD_V), lambda b, j: (0, 0))],
        out_specs=row, out_shape=jax.ShapeDtypeStruct((n * SEQ, D), BF16),
        compiler_params=_params(("parallel", "parallel")),
    )(o_f, o_b, r, gnorm)


def gla_out_bwd(o_f, o_b, r, dyg, gla_proj, gnorm):
    n = o_f.shape[0]
    tiles = SEQ // TM_EW

    def body(of_ref, ob_ref, r_ref, d_ref, w_ref, g_ref, do_ref, dr_ref, dg_ref, dog_buf):
        @pl.when((pl.program_id(0) == 0) & (pl.program_id(1) == 0))
        def _():
            dg_ref[...] = jnp.zeros_like(dg_ref)

        dog_buf[...] = _nt(d_ref[...], w_ref[...])
        for h in range(HEADS):
            cols = pl.ds(h * HEAD_V, HEAD_V)
            o = of_ref[0, :, cols] + ob_ref[0, :, cols]
            rv = r_ref[:, cols].astype(F32)
            dv = dog_buf[:, cols]
            rs = lax.rsqrt(jnp.mean(o * o, axis=-1, keepdims=True) + EPS)
            oh = o * rs
            dr_ref[:, cols] = (dv * oh * g_ref[...] * _dsilu(rv)).astype(dr_ref.dtype)
            dn = dv * _silu(rv)
            dg_ref[...] += jnp.sum(dn * oh, axis=0, keepdims=True)
            doh = dn * g_ref[...]
            do_ref[0, :, cols] = rs * (doh - oh * jnp.mean(doh * oh, axis=-1, keepdims=True))

    ospec = pl.BlockSpec((1, TM_EW, D), lambda b, j: (b, j, 0))
    row = pl.BlockSpec((TM_EW, D), lambda b, j: (b * tiles + j, 0))
    vec = pl.BlockSpec((1, HEAD_V), lambda b, j: (0, 0))
    return _pallas(
        body, name="gla_out_bwd", grid=(n, tiles),
        in_specs=[ospec, ospec, row, row, pl.BlockSpec((D, D), lambda b, j: (0, 0)), vec],
        out_specs=[ospec, row, vec],
        out_shape=[jax.ShapeDtypeStruct((n, SEQ, D), F32), jax.ShapeDtypeStruct((n * SEQ, D), BF16),
                   jax.ShapeDtypeStruct((1, HEAD_V), F32)],
        scratch_shapes=[pltpu.VMEM((TM_EW, D), F32)],
        compiler_params=_params(("arbitrary", "arbitrary")),
    )(o_f, o_b, r, dyg, gla_proj, gnorm)


TM_OUT = 512


def merge_out_final(p5, y_conv, y_gla, w_out, x2, gate, final_g, target, n_samples):
    t = x2.shape[0]
    tiles = SEQ // TM_OUT

    def body(mc_ref, mg_ref, yc_ref, yg_ref, w_ref, x_ref, gate_ref, g_ref, t_ref,
             mrg_ref, dh_ref, dmo_ref, dgate_ref, dg_ref, loss_ref):
        b, j = pl.program_id(0), pl.program_id(1)
        f = lambda ref: ref[...].astype(F32)
        merged = _b16(_sigmoid(f(mc_ref)) * f(yc_ref) + _sigmoid(f(mg_ref)) * f(yg_ref))
        mrg_ref[...] = merged
        mo_v = _nn(merged, w_ref[...])
        h = x_ref[...] + gate_ref[0] * mo_v
        rs = lax.rsqrt(jnp.mean(h * h, axis=-1, keepdims=True) + EPS)
        nh = h * rs
        err = nh * g_ref[...] - t_ref[...]
        dy = err * (1.0 / D)
        dn = dy * g_ref[...]
        dh = rs * (dn - nh * jnp.mean(dn * nh, axis=-1, keepdims=True))
        dh_ref[...] = dh
        dmo_ref[...] = (dh * gate_ref[0]).astype(dmo_ref.dtype)

        @pl.when(j == 0)
        def _():
            dgate_ref[...] = jnp.zeros_like(dgate_ref)

        @pl.when((b == 0) & (j == 0))
        def _():
            dg_ref[...] = jnp.zeros_like(dg_ref)
            loss_ref[...] = jnp.zeros_like(loss_ref)

        dgate_ref[0] += jnp.sum(dh * mo_v, axis=0, keepdims=True)
        dg_ref[...] += jnp.sum(dy * nh, axis=0, keepdims=True)
        loss_ref[...] += (0.5 / D) * jnp.sum(err * err)

    row = pl.BlockSpec((TM_OUT, D), lambda b, j: (b * tiles + j, 0))
    per = pl.BlockSpec((1, 1, D), lambda b, j: (b, 0, 0))
    vec = pl.BlockSpec((1, D), lambda b, j: (0, 0))
    return _pallas(
        body, name="merge_out_final", grid=(n_samples, tiles),
        in_specs=[row, pl.BlockSpec((TM_OUT, D), lambda b, j: (b * tiles + j, 1)), row, row,
                  pl.BlockSpec((D, D), lambda b, j: (0, 0)), row, per, vec, row],
        out_specs=[row, row, row, per, vec, pl.BlockSpec((8, 128), lambda b, j: (0, 0))],
        out_shape=[jax.ShapeDtypeStruct((t, D), BF16), jax.ShapeDtypeStruct((t, D), F32), jax.ShapeDtypeStruct((t, D), BF16),
                   jax.ShapeDtypeStruct((n_samples, 1, D), F32), jax.ShapeDtypeStruct((1, D), F32),
                   jax.ShapeDtypeStruct((8, 128), F32)],
        compiler_params=_params(("arbitrary", "arbitrary")),
    )(p5, p5, y_conv, y_gla, w_out, x2, gate, final_g, target)


def out_dgrad_merge_bwd(p5, y_conv, y_gla, dmo, w_out):
    t = y_conv.shape[0]

    def body(mc_ref, mg_ref, yc_ref, yg_ref, d_ref, w_ref, dyc_ref, dyg_ref, dp_ref):
        f = lambda ref: ref[...].astype(F32)
        d = _nt(d_ref[...], w_ref[...])
        sc = _sigmoid(f(mc_ref))
        sg = _sigmoid(f(mg_ref))
        dyc_ref[...] = (d * sc).astype(dyc_ref.dtype)
        dyg_ref[...] = (d * sg).astype(dyg_ref.dtype)
        dp_ref[:, pl.ds(0, D)] = (d * f(yc_ref) * sc * (1.0 - sc)).astype(dp_ref.dtype)
        dp_ref[:, pl.ds(D, D)] = (d * f(yg_ref) * sg * (1.0 - sg)).astype(dp_ref.dtype)

    row = pl.BlockSpec((TM_OUT, D), lambda i: (i, 0))
    return _pallas(
        body, name="out_dgrad_merge_bwd", grid=(t // TM_OUT,),
        in_specs=[row, pl.BlockSpec((TM_OUT, D), lambda i: (i, 1)), row, row, row, pl.BlockSpec((D, D), lambda i: (0, 0))],
        out_specs=[row, row, pl.BlockSpec((TM_OUT, 2 * D), lambda i: (i, 0))],
        out_shape=[jax.ShapeDtypeStruct((t, D), BF16), jax.ShapeDtypeStruct((t, D), BF16),
                   jax.ShapeDtypeStruct((t, 2 * D), BF16)],
        compiler_params=_params(("parallel",)),
    )(p5, p5, y_conv, y_gla, dmo, w_out)


def local_step(x, ctx, target, mod, wts, small, p_sh, chip, core):
    n = x.shape[0]
    t = n * SEQ
    t_all = t + n * NCTX
    x2 = x.reshape(t, D)
    ctx2 = ctx.reshape(n * NCTX, D)
    tgt2 = target.reshape(t, D)
    scale1, shift, gate = mod

    u = norm_mod_fwd(x2, ctx2, scale1, shift, small["norm_g"])
    p1, p2, p3, p4, p5, p_all = proj_all(u, [wts["w%d" % i] for i in range(1, 6)], [small["b%d" % i] for i in range(1, 6)],
                                         [t, t, t_all, t, t], p_sh, tm=512)
    p_full = jnp.stack([jnp.where(chip == i, p_sh, p_all[i]) for i in range(N_CHIPS)])
    wts = dict(wts, conv_proj=p_full[:, 0:256].reshape(D, D), gla_proj=p_full[:, 256:512].reshape(D, D),
               w_out=p_full[:, 512:768].reshape(D, D))

    aconv = conv_fwd(p1, small["conv_w"], small["conv_b"], n)
    ac = ln_gate_fwd(aconv, p2, small["conv_ln_g"], small["conv_ln_b"])
    y_conv = matmul_nn(ac, wts["conv_proj"], name="conv_proj_fwd", m=t, tm=1024, tn=1024, out_dtype=BF16)

    qs, ks, vs, cum_f, cum_b = gla_prep_fwd(p3, small["upf"], small["upb"], small["bias_f"], small["bias_b"], n)
    o_f, s_f, sfin_f = gla_scan_fwd(qs, ks, vs, cum_f, rev=False, name="gla_scan_fwd_f")
    o_b, s_b, sfin_b = gla_scan_fwd(qs, ks, vs, cum_b, rev=True, name="gla_scan_fwd_b")
    og = gla_out_fwd(o_f, o_b, p4, small["gla_norm_g"])
    y_gla = matmul_nn(og, wts["gla_proj"], name="gla_proj_fwd", m=t, tm=1024, tn=1024, out_dtype=BF16)

    merged, dh, dmo, dgate, d_final_g, loss = merge_out_final(p5, y_conv, y_gla, wts["w_out"], x2, gate,
                                                              small["final_norm_g"], tgt2, n)

    g = {"final_norm_g": d_final_g}
    g["w_out"] = matmul_tn(merged, dmo, name="w_out_wgrad", t=t, tn=1024, tt=1024)[0]
    dyc, dyg, dp5 = out_dgrad_merge_bwd(p5, y_conv, y_gla, dmo, wts["w_out"])

    g["conv_proj"] = matmul_tn(ac, dyc, name="conv_proj_wgrad", t=t, tn=1024, tt=1024)[0]
    daconv, dp2, g["conv_ln_g"], g["conv_ln_b"] = ln_gate_bwd(aconv, p2, dyc, wts["conv_proj"], small["conv_ln_g"],
                                                               small["conv_ln_b"])
    dp1, dconv_w, dconv_b = conv_bwd(p1, daconv, small["conv_w"], n)
    g["conv_w"], g["conv_b"] = dconv_w, dconv_b

    g["gla_proj"] = matmul_tn(og, dyg, name="gla_proj_wgrad", t=t, tn=1024, tt=1024)[0]
    do, dp4, g["gla_norm_g"] = gla_out_bwd(o_f, o_b, p4, dyg, wts["gla_proj"], small["gla_norm_g"])
    g["proj"] = jnp.concatenate([g["conv_proj"].reshape(N_CHIPS, 256, D), g["gla_proj"].reshape(N_CHIPS, 256, D),
                                 g["w_out"].reshape(N_CHIPS, 256, D)], 1)
    dq_f, dk_f, dv_f, dc_f, gotp = gla_scan_bwd(qs, ks, vs, cum_f, s_f, sfin_f, do, rev=False, name="gla_scan_bwd_f",
                                                rider=("swap", g["proj"]))
    pap16 = pair_add(core, g["proj"], gotp, name="pair_add_p", tr=384)
    dq_b, dk_b, dv_b, dc_b, rbp = gla_scan_bwd(qs, ks, vs, cum_b, s_b, sfin_b, do, rev=True, name="gla_scan_bwd_b",
                                               rider=("exchange", pap16))
    dp3, g["upf"], g["upb"], g["bias_f"], g["bias_b"] = gla_prep_bwd(
        p3, dq_f, dq_b, dk_f, dk_b, dv_f, dv_b, dc_f, dc_b,
        small["upf"], small["upb"], small["bias_f"], small["bias_b"], n)

    dps = [dp1, dp2, dp3, dp4, dp5]
    got = {}
    for i in [0, 1, 3, 4, 2]:
        dp = dps[i]
        rows = dp.shape[0]
        tn = W3 if dp.shape[1] == W3 else 1024
        others = [j for j in range(5) if j != i]
        swap = [g["w%d" % (j + 1)] for j in others] if i == 2 else None
        outs = matmul_tn(u, dp, name="w_in_wgrad_%d" % (i + 1), t=rows, tn=tn, tt=1024 if rows % 1024 == 0 else 768,
                         colsum=True, swap=swap)
        g["w%d" % (i + 1)], g["b%d" % (i + 1)] = outs[0], outs[1]
        if swap is not None:
            got = dict(zip(others, outs[2:2 + len(others)]))
    g["gate"] = dgate
    return loss, dh, dps, g, got, (pap16, rbp)


def _group_cols(w):
    gv, gg, z = w[..., 0:1024], w[..., 1024:2048], w[..., 2048:3072]
    q, k, v = w[..., 3072:3584], w[..., 3584:4096], w[..., 4096:5120]
    ab = w[..., 5120:5152]
    r, mc, mg = w[..., 5152:6176], w[..., 6176:7200], w[..., 7200:8224]
    g1 = jnp.concatenate([p for j in range(CONV_NCB)
                          for p in (gv[..., CONV_CB * j:CONV_CB * (j + 1)], gg[..., CONV_CB * j:CONV_CB * (j + 1)])], -1)
    pad = jnp.zeros(w.shape[:-1] + (W3 - 2080,), w.dtype)
    g3 = jnp.concatenate([v, q, k, ab, pad], -1)
    return g1, z, g3, r, jnp.concatenate([mc, mg], -1)


def _ungroup_cols(g1, g2, g3, g4, g5):
    gv = jnp.concatenate([g1[..., 2 * CONV_CB * j:2 * CONV_CB * j + CONV_CB] for j in range(CONV_NCB)], -1)
    gg = jnp.concatenate([g1[..., 2 * CONV_CB * j + CONV_CB:2 * CONV_CB * (j + 1)] for j in range(CONV_NCB)], -1)
    v, q, k, ab = g3[..., 0:1024], g3[..., 1024:1536], g3[..., 1536:2048], g3[..., 2048:2080]
    return jnp.concatenate([gv, gg, g2, q, k, v, ab, g4, g5[..., 0:1024], g5[..., 1024:2048]], -1)


def _natural_pieces():
    pieces = [(CONV_CB * j, CONV_CB, 0, 2 * CONV_CB * j) for j in range(CONV_NCB)]
    pieces += [(1024 + CONV_CB * j, CONV_CB, 0, 2 * CONV_CB * j + CONV_CB) for j in range(CONV_NCB)]
    pieces += [(2048, 1024, 1, 0), (3072, 512, 2, O3_Q), (3584, 512, 2, O3_K), (4096, 1024, 2, O3_V), (5120, 32, 2, O3_AB),
               (5152, 1024, 3, 0), (6176, 1024, 4, 0), (7200, 1024, 4, 1024)]
    return sorted(pieces)


def _ungroup_to_shards(groups):
    shards = []
    for i in range(N_CHIPS):
        lo, hi = i * W_IN_SHARD, (i + 1) * W_IN_SHARD
        parts = []
        for nat, width, g, gcol in _natural_pieces():
            a, b = max(nat, lo), min(nat + width, hi)
            if a < b:
                parts.append(groups[g][:, gcol + a - nat:gcol + b - nat])
        shards.append(jnp.concatenate(parts, 1))
    return jnp.stack(shards)


def _pad_up(up, row0):
    return jnp.zeros((128, GLA_DK), F32).at[row0:row0 + up.shape[0]].set(up)


def _adamw_math(w, g, m, v):
    m = ADAM_B1 * m + (1.0 - ADAM_B1) * g
    v = ADAM_B2 * v + (1.0 - ADAM_B2) * (g * g)
    m_hat = m / (1.0 - ADAM_B1 ** ADAM_STEP)
    v_hat = v / (1.0 - ADAM_B2 ** ADAM_STEP)
    delta = -ADAM_LR * (m_hat / (jnp.sqrt(v_hat) + ADAM_EPS) + ADAM_WD * w)
    return delta, m, v


def adamw2d(w, g, m, v, *, name, tr, tcols=None):
    rows, cols = w.shape[-2:]

    def body(w_ref, g_ref, m_ref, v_ref, d_ref, nm_ref, nv_ref):
        d_ref[...], nm_ref[...], nv_ref[...] = _adamw_math(w_ref[...], g_ref[...], m_ref[...], v_ref[...])

    tcols = cols if tcols is None else tcols
    if w.ndim == 3:
        spec = pl.BlockSpec((1, tr, tcols), lambda i, j: (0, i, j))
    else:
        spec = pl.BlockSpec((tr, tcols), lambda i, j: (i, j))
    return _pallas(
        body, name=name, grid=(rows // tr, cols // tcols), in_specs=[spec] * 4, out_specs=[spec] * 3,
        out_shape=[jax.ShapeDtypeStruct(w.shape, F32)] * 3, compiler_params=_params(("parallel", "parallel")),
    )(w, g, m, v)


def adamw_many(ws, gs, ms, vs):
    k = len(ws)
    two = lambda a: a.reshape((-1, a.shape[-1]))

    def body(*refs):
        w_refs, g_refs, m_refs, v_refs = refs[:k], refs[k:2 * k], refs[2 * k:3 * k], refs[3 * k:4 * k]
        d_refs, nm_refs, nv_refs = refs[4 * k:5 * k], refs[5 * k:6 * k], refs[6 * k:7 * k]
        for i in range(k):
            d_refs[i][...], nm_refs[i][...], nv_refs[i][...] = _adamw_math(
                w_refs[i][...], g_refs[i][...], m_refs[i][...], v_refs[i][...])

    shapes = [jax.ShapeDtypeStruct(two(a).shape, F32) for a in ws]
    outs = _pallas(body, name="adamw_small", out_shape=shapes * 3, compiler_params=_params())(
        *[two(a) for a in ws], *[two(a) for a in gs], *[two(a) for a in ms], *[two(a) for a in vs])
    back = lambda lst: [o.reshape(a.shape) for o, a in zip(lst, ws)]
    return back(outs[:k]), back(outs[k:2 * k]), back(outs[2 * k:])


def sum_devices(sall, *, name):
    rows = sall.shape[1]

    def body(s_ref, o_ref):
        acc = s_ref[0]
        for d in range(1, N_DEV):
            acc = acc + s_ref[d]
        o_ref[...] = acc

    return _pallas(body, name=name, out_shape=jax.ShapeDtypeStruct((rows, D), F32),
                   compiler_params=_params())(sall)


def pair_add(core, g, got, *, name, tr):
    n, rows, cols = got.shape
    g4 = g.reshape(n, 2, rows, cols)

    def body(core_ref, g_ref, got_ref, ob_ref):
        del core_ref
        ob_ref[0] = (g_ref[0, 0] + got_ref[0]).astype(BF16)

    spec = pl.BlockSpec((1, tr, cols), lambda i, t, core_ref: (i, t, 0))
    return _pallas(
        body, name=name,
        grid_spec=pltpu.PrefetchScalarGridSpec(
            num_scalar_prefetch=1, grid=(n, rows // tr),
            in_specs=[pl.BlockSpec((1, 1, tr, cols), lambda i, t, core_ref: (i, core_ref[0], t, 0)), spec],
            out_specs=spec),
        out_shape=jax.ShapeDtypeStruct(got.shape, BF16),
        compiler_params=_params(("parallel", "parallel")))(core, g4, got)


def pair_add_groups(core, gs, gots, *, tr):
    k = len(gs)
    rows = gots[0].shape[0]

    def body(core_ref, *refs):
        del core_ref
        for i in range(k):
            refs[2 * k + i][...] = (refs[i][0] + refs[k + i][...]).astype(BF16)

    g_specs = [pl.BlockSpec((1, tr, a.shape[1]), lambda t, core_ref: (core_ref[0], t, 0)) for a in gots]
    r_specs = [pl.BlockSpec((tr, a.shape[1]), lambda t, core_ref: (t, 0)) for a in gots]
    return _pallas(
        body, name="pair_add_w",
        grid_spec=pltpu.PrefetchScalarGridSpec(num_scalar_prefetch=1, grid=(rows // tr,),
                                               in_specs=g_specs + r_specs, out_specs=r_specs),
        out_shape=[jax.ShapeDtypeStruct(a.shape, BF16) for a in gots],
        compiler_params=_params(("parallel",)))(core, *[a.reshape(2, rows, a.shape[1]) for a in gs], *gots)


def chip_add(place, pa, rb, *, name, tr):
    _, rows, cols = pa.shape

    def body(place_ref, m_ref, r_ref, o_ref):
        del place_ref
        o_ref[0] = ((m_ref[0].astype(F32) + r_ref[0].astype(F32)) + r_ref[1].astype(F32)) + r_ref[2].astype(F32)

    return _pallas(
        body, name=name,
        grid_spec=pltpu.PrefetchScalarGridSpec(
            num_scalar_prefetch=1, grid=(rows // tr,),
            in_specs=[pl.BlockSpec((1, tr, cols), lambda t, place_ref: (place_ref[0], t, 0)),
                      pl.BlockSpec((3, tr, cols), lambda t, place_ref: (0, t, 0))],
            out_specs=pl.BlockSpec((1, tr, cols), lambda t, place_ref: (place_ref[1], t, 0))),
        out_shape=jax.ShapeDtypeStruct((2, rows, cols), F32),
        compiler_params=_params(("parallel",)))(place, pa, rb)


def ada_bwd(call, cctx_rows, dm_shard, dm_full, adaw):
    nsh = adaw.shape[1]

    def body(c_ref, cc_ref, dms_ref, dmf_ref, w_ref, gw_ref, gb_ref, pq_ref):
        a_lat = _silu(c_ref[...])
        a_ctx = _silu(cc_ref[...])
        dms = dms_ref[...]
        gw_ref[...] = _tn(a_lat, dms[0:64], HI) + _tn(a_ctx, dms[64:72], HI)
        gb_ref[...] = jnp.sum(dmf_ref[...], axis=0, keepdims=True)
        part = _nt(dms[64:72], w_ref[...], HI)
        pq_ref[...] = jnp.zeros_like(pq_ref) + jnp.sum(part, axis=0, keepdims=True)

    return _pallas(body, name="ada_bwd",
                   out_shape=[jax.ShapeDtypeStruct((D, nsh), F32), jax.ShapeDtypeStruct((1, 3 * D), F32),
                              jax.ShapeDtypeStruct((8, D), F32)],
                   compiler_params=_params())(call, cctx_rows, dm_shard, dm_full, adaw)


def cctx_grad(pq_all, cctx_rows):
    def body(p_ref, c_ref, o_ref):
        acc = p_ref[0]
        for qi in range(1, N_CHIPS):
            acc = acc + p_ref[qi]
        o_ref[...] = acc * _dsilu(c_ref[...])

    return _pallas(body, name="cctx_grad", out_shape=jax.ShapeDtypeStruct((8, D), F32),
                   compiler_params=_params())(pq_all, cctx_rows)


def _place():
    x, y, c = lax.axis_index("x"), lax.axis_index("y"), lax.axis_index("c")
    chips = [(1 - x, y), (x, 1 - y), (1 - x, 1 - y)]
    return x, y, c, chips


def _all_peers(x, y, c):
    return [((1 - x) if r & 4 else x, (1 - y) if r & 2 else y, (1 - c) if r & 1 else c) for r in range(1, N_DEV)]


def _remote(src, dst, send_sem, recv_sem, dev):
    return pltpu.make_async_remote_copy(src_ref=src, dst_ref=dst, send_sem=send_sem, recv_sem=recv_sem,
                                        device_id=dev, device_id_type=MESH)


ANY = pl.BlockSpec(memory_space=pl.ANY)
VMEM = pl.BlockSpec(memory_space=pltpu.VMEM)
F_ROWS = 16


W_ROW_CHUNKS = 4
P_ROW_CHUNKS = 2
N_BULK = W_ROW_CHUNKS + P_ROW_CHUNKS


def _half_chunks(core, n_rows, align, which=(0, 1)):
    out = []
    for a, k in ((0, W_ROW_CHUNKS), (1, P_ROW_CHUNKS)):
        if a not in which:
            continue
        half = n_rows[a] // 2
        size = half // k
        for i in range(k):
            start = core * half + i * size
            out.append((a, pl.ds(start if isinstance(start, int) else pl.multiple_of(start, align), size)))
    return out


def gather_weights(c8, cctx8, adaw, adab, w_sh, fp):
    nsh = adaw.shape[1]

    def body(c_ref, cctx_ref, adaw_ref, adab_ref, w_ref, fp_ref, wall_ref, fall_ref, call_ref, mall_ref,
             abuf, w_send, w_recv, h_send, h_recv, c_send, c_recv, m_send, m_recv, f_send, f_recv):
        x, y, c, chips = _place()
        q = 2 * x + y
        dev = 4 * x + 2 * y + c
        qs = [2 * cx + cy for cx, cy in chips]
        sib = (x, y, 1 - c)
        srcs, dsts = (w_ref,), (wall_ref,)
        n_rows = (w_ref.shape[0],)
        mine = _half_chunks(c, n_rows, 16, which=(0,))
        other = _half_chunks(1 - c, n_rows, 16, which=(0,))

        bulk = [[_remote(srcs[a].at[rows], dsts[a].at[q, rows], w_send.at[j * N_BULK + i], w_recv.at[j * N_BULK + i],
                         (*chips[j], c)) for i, (a, rows) in enumerate(mine)] for j in range(3)]
        fall_ref[q] = fp_ref[...]
        small = [_remote(fp_ref, fall_ref.at[q], f_send.at[j], f_recv.at[j], (*chips[j], c)) for j in range(3)]
        my_rows = pl.ds(pl.multiple_of(8 * dev, 8), 8)
        call_ref[my_rows, :] = c_ref[...]
        cond = [_remote(c_ref, call_ref.at[my_rows, :], c_send.at[r], c_recv.at[r], peer)
                for r, peer in enumerate(_all_peers(x, y, c))]
        for cp in sum(bulk, []) + small + cond:
            cp.start()
        for cp in cond:
            cp.wait_recv()

        abuf[pl.ds(0, 64), :] = _silu(call_ref[...])
        abuf[pl.ds(64, 8), :] = _silu(cctx_ref[...])
        mall_ref[q] = _nn(abuf[...], adaw_ref[...], HI) + adab_ref[...]
        mod = [_remote(mall_ref.at[q], mall_ref.at[q], m_send.at[j], m_recv.at[j], (*chips[j], c)) for j in range(3)]
        for cp in mod:
            cp.start()

        handed = []
        for j in range(3):
            for i, (a, rows) in enumerate(mine):
                bulk[j][i].wait_recv()
                cp = _remote(dsts[a].at[qs[j], rows], dsts[a].at[qs[j], rows],
                             h_send.at[j * N_BULK + i], h_recv.at[j * N_BULK + i], sib)
                cp.start()
                handed.append(cp)
        for j in range(3):
            for i, (a, rows) in enumerate(other):
                _remote(dsts[a].at[qs[j], rows], dsts[a].at[qs[j], rows],
                        h_send.at[j * N_BULK + i], h_recv.at[j * N_BULK + i], sib).wait_recv()
        for cp in mod + small:
            cp.wait_recv()
        for cp in sum(bulk, []) + small + cond + mod + handed:
            cp.wait_send()

    def dma(n):
        return pltpu.SemaphoreType.DMA((n,))

    return _pallas(
        body, name="gather_weights",
        in_specs=[VMEM, VMEM, VMEM, VMEM, ANY, VMEM],
        out_specs=[ANY, VMEM, VMEM, VMEM],
        out_shape=[jax.ShapeDtypeStruct((N_CHIPS,) + w_sh.shape, BF16),
                   jax.ShapeDtypeStruct((N_CHIPS, F_ROWS, D), F32),
                   jax.ShapeDtypeStruct((8 * N_DEV, D), F32), jax.ShapeDtypeStruct((N_CHIPS, MOD_ROWS, nsh), F32)],
        scratch_shapes=[pltpu.VMEM((MOD_ROWS, D), F32), dma(3 * N_BULK), dma(3 * N_BULK), dma(3 * N_BULK), dma(3 * N_BULK),
                        dma(7), dma(7), dma(3), dma(3), dma(3), dma(3)],
        compiler_params=_params(),
    )(c8, cctx8, adaw, adab, w_sh, fp)


def _pair_count(gs, gp):
    return len(gs) * W_ROW_CHUNKS + (0 if gp is None else N_CHIPS * P_ROW_CHUNKS)


def _pair_got_shapes(gs, gp):
    shapes = [jax.ShapeDtypeStruct((D // 2, a.shape[1]), F32) for a in gs]
    if gp is not None:
        shapes.append(jax.ShapeDtypeStruct((N_CHIPS, gp.shape[1] // 2, gp.shape[2]), F32))
    return shapes


def _pair_copies(g_refs, gp_ref, got_refs, gotp_ref, a_send, a_recv):
    x, y, c, _ = _place()
    sib = (x, y, 1 - c)
    pair = []
    half, size = D // 2, D // 2 // W_ROW_CHUNKS
    for gi in range(len(g_refs)):
        for i in range(W_ROW_CHUNKS):
            k = len(pair)
            rows_o = pl.ds(pl.multiple_of((1 - c) * half + i * size, 8), size)
            pair.append(_remote(g_refs[gi].at[rows_o], got_refs[gi].at[pl.ds(i * size, size)],
                                a_send.at[k], a_recv.at[k], sib))
    if gp_ref is not None:
        half, size = gp_ref.shape[1] // 2, gp_ref.shape[1] // 2 // P_ROW_CHUNKS
        for s in range(N_CHIPS):
            for i in range(P_ROW_CHUNKS):
                k = len(pair)
                rows_o = pl.ds(pl.multiple_of((1 - c) * half + i * size, 8), size)
                pair.append(_remote(gp_ref.at[s, rows_o], gotp_ref.at[s, pl.ds(i * size, size)],
                                    a_send.at[k], a_recv.at[k], sib))
    return pair


def pair_swap(gs, sm):
    n_gs = len(gs)

    def body(*refs):
        g_refs, sm_ref = refs[:n_gs], refs[n_gs]
        got_refs, sall_ref = refs[n_gs + 1:2 * n_gs + 1], refs[2 * n_gs + 1]
        a_send, a_recv, s_send, s_recv = refs[2 * n_gs + 2:]
        x, y, c, _ = _place()
        dev = 4 * x + 2 * y + c
        pair = _pair_copies(g_refs, None, got_refs, None, a_send, a_recv)
        sall_ref[dev] = sm_ref[...]
        small = [_remote(sm_ref, sall_ref.at[dev], s_send.at[r], s_recv.at[r], peer)
                 for r, peer in enumerate(_all_peers(x, y, c))]
        for cp in pair + small:
            cp.start()
        for cp in small + pair:
            cp.wait_recv()
        for cp in small + pair:
            cp.wait_send()

    return _pallas(
        body, name="pair_swap", in_specs=[ANY] * n_gs + [VMEM], out_specs=[ANY] * n_gs + [VMEM],
        out_shape=_pair_got_shapes(gs, None) + [jax.ShapeDtypeStruct((N_DEV,) + sm.shape, F32)],
        scratch_shapes=[pltpu.SemaphoreType.DMA((_pair_count(gs, None),)), pltpu.SemaphoreType.DMA((_pair_count(gs, None),)),
                        pltpu.SemaphoreType.DMA((N_DEV - 1,)), pltpu.SemaphoreType.DMA((N_DEV - 1,))],
        compiler_params=_params(),
    )(*gs, sm)


def gather_small(sm):
    rows = sm.shape[0]

    def body(sm_ref, sall_ref, s_send, s_recv):
        x, y, c, _ = _place()
        dev = 4 * x + 2 * y + c
        sall_ref[dev] = sm_ref[...]
        small = [_remote(sm_ref, sall_ref.at[dev], s_send.at[r], s_recv.at[r], peer)
                 for r, peer in enumerate(_all_peers(x, y, c))]
        for cp in small:
            cp.start()
        for cp in small:
            cp.wait_recv()
        for cp in small:
            cp.wait_send()

    return _pallas(
        body, name="gather_small", in_specs=[VMEM], out_specs=VMEM,
        out_shape=jax.ShapeDtypeStruct((N_DEV, rows, D), F32),
        scratch_shapes=[pltpu.SemaphoreType.DMA((7,)), pltpu.SemaphoreType.DMA((7,))],
        compiler_params=_params(),
    )(sm)


def pair_share(ghw, ghp, pq):
    def body(ghw_ref, ghp_ref, pq_ref, outw_ref, outp_ref, pqa_ref, send, recv, p_send, p_recv):
        del ghw_ref, ghp_ref
        x, y, c, chips = _place()
        q = 2 * x + y
        refs = (outw_ref, outp_ref)
        n_rows = (2 * outw_ref.shape[1], 2 * outp_ref.shape[1])
        pair = [_remote(refs[a].at[c, rows], refs[a].at[c, rows], send.at[i], recv.at[i], (x, y, 1 - c))
                for i, (a, rows) in enumerate(_half_chunks(0, n_rows, 8))]
        pqa_ref[q] = pq_ref[...]
        small = [_remote(pq_ref, pqa_ref.at[q], p_send.at[j], p_recv.at[j], (*chips[j], c)) for j in range(3)]
        for cp in pair + small:
            cp.start()
        for i, (a, rows) in enumerate(_half_chunks(0, n_rows, 8)):
            _remote(refs[a].at[1 - c, rows], refs[a].at[1 - c, rows], send.at[i], recv.at[i], (x, y, 1 - c)).wait_recv()
        for cp in small:
            cp.wait_recv()
        for cp in pair + small:
            cp.wait_send()

    return _pallas(
        body, name="pair_share", in_specs=[ANY, ANY, VMEM], out_specs=[ANY, ANY, VMEM],
        out_shape=[jax.ShapeDtypeStruct(ghw.shape, F32), jax.ShapeDtypeStruct(ghp.shape, F32),
                   jax.ShapeDtypeStruct((N_CHIPS, 8, D), F32)],
        scratch_shapes=[pltpu.SemaphoreType.DMA((N_BULK,)), pltpu.SemaphoreType.DMA((N_BULK,)),
                        pltpu.SemaphoreType.DMA((3,)), pltpu.SemaphoreType.DMA((3,))],
        input_output_aliases={0: 0, 1: 1},
        compiler_params=_params(),
    )(ghw, ghp, pq)


def _rows_of(shape):
    size = 1
    for s in shape:
        size *= s
    return -(-size // D)


def _pack(arrs, rows_multiple=8):
    parts = []
    total = 0
    for a in arrs:
        f = a.reshape(-1).astype(F32)
        r = _rows_of(a.shape)
        parts.append(jnp.pad(f, (0, r * D - f.shape[0])))
        total += r
    pad_rows = (-total) % rows_multiple
    if pad_rows:
        parts.append(jnp.zeros((pad_rows * D,), F32))
    return jnp.concatenate(parts).reshape(-1, D)


def _unpack(p, shapes):
    out = []
    r0 = 0
    for shp in shapes:
        r = _rows_of(shp)
        size = 1
        for s in shp:
            size *= s
        out.append(p[r0:r0 + r].reshape(-1)[:size].reshape(shp))
        r0 += r
    return out


WEIGHT_NAMES = ['c_ctx', 'ada_w', 'ada_b', 'norm_g', 'w_in', 'b_in', 'conv_w', 'conv_b', 'conv_ln_g', 'conv_ln_b',
                'conv_proj', 'decay_up_fwd', 'decay_bias_fwd', 'decay_up_bwd', 'decay_bias_bwd', 'gla_norm_g', 'gla_proj',
                'w_out', 'final_norm_g']
SMALL_NAMES = ['c_ctx', 'ada_b', 'norm_g', 'b_in', 'conv_w', 'conv_b', 'conv_ln_g', 'conv_ln_b', 'decay_up_fwd',
               'decay_bias_fwd', 'decay_up_bwd', 'decay_bias_bwd', 'gla_norm_g', 'final_norm_g']


def kernel(x, c, ctx, c_ctx, ada_w, ada_b, norm_g, w_in, b_in, conv_w, conv_b, conv_ln_g, conv_ln_b, conv_proj, decay_up_fwd, decay_bias_fwd, decay_up_bwd, decay_bias_bwd, gla_norm_g, gla_proj, w_out, final_norm_g, loss_target, m_c_ctx, m_ada_w, m_ada_b, m_norm_g, m_w_in, m_b_in, m_conv_w, m_conv_b, m_conv_ln_g, m_conv_ln_b, m_conv_proj, m_decay_up_fwd, m_decay_bias_fwd, m_decay_up_bwd, m_decay_bias_bwd, m_gla_norm_g, m_gla_proj, m_w_out, m_final_norm_g, v_c_ctx, v_ada_w, v_ada_b, v_norm_g, v_w_in, v_b_in, v_conv_w, v_conv_b, v_conv_ln_g, v_conv_ln_b, v_conv_proj, v_decay_up_fwd, v_decay_bias_fwd, v_decay_up_bwd, v_decay_bias_bwd, v_gla_norm_g, v_gla_proj, v_w_out, v_final_norm_g):
    w = dict(c_ctx=c_ctx, ada_w=ada_w, ada_b=ada_b, norm_g=norm_g, w_in=w_in, b_in=b_in, conv_w=conv_w, conv_b=conv_b,
             conv_ln_g=conv_ln_g, conv_ln_b=conv_ln_b, conv_proj=conv_proj, decay_up_fwd=decay_up_fwd,
             decay_bias_fwd=decay_bias_fwd, decay_up_bwd=decay_up_bwd, decay_bias_bwd=decay_bias_bwd,
             gla_norm_g=gla_norm_g, gla_proj=gla_proj, w_out=w_out, final_norm_g=final_norm_g)
    m = dict(c_ctx=m_c_ctx, ada_w=m_ada_w, ada_b=m_ada_b, norm_g=m_norm_g, w_in=m_w_in, b_in=m_b_in, conv_w=m_conv_w,
             conv_b=m_conv_b, conv_ln_g=m_conv_ln_g, conv_ln_b=m_conv_ln_b, conv_proj=m_conv_proj,
             decay_up_fwd=m_decay_up_fwd, decay_bias_fwd=m_decay_bias_fwd, decay_up_bwd=m_decay_up_bwd,
             decay_bias_bwd=m_decay_bias_bwd, gla_norm_g=m_gla_norm_g, gla_proj=m_gla_proj, w_out=m_w_out,
             final_norm_g=m_final_norm_g)
    v = dict(c_ctx=v_c_ctx, ada_w=v_ada_w, ada_b=v_ada_b, norm_g=v_norm_g, w_in=v_w_in, b_in=v_b_in, conv_w=v_conv_w,
             conv_b=v_conv_b, conv_ln_g=v_conv_ln_g, conv_ln_b=v_conv_ln_b, conv_proj=v_conv_proj,
             decay_up_fwd=v_decay_up_fwd, decay_bias_fwd=v_decay_bias_fwd, decay_up_bwd=v_decay_up_bwd,
             decay_bias_bwd=v_decay_bias_bwd, gla_norm_g=v_gla_norm_g, gla_proj=v_gla_proj, w_out=v_w_out,
             final_norm_g=v_final_norm_g)
    n = x.shape[0]
    ax, ay, ac = lax.axis_index("x"), lax.axis_index("y"), lax.axis_index("c")
    q = 2 * ax + ay
    dev = 4 * ax + 2 * ay + ac
    nsh = ada_w.shape[2]

    w_sh = w_in[0].astype(BF16)
    p_sh = jnp.concatenate([conv_proj[0], gla_proj[0], w_out[0]], 0).astype(BF16)
    fp = _pack([conv_w[0], decay_up_fwd[0], decay_up_bwd[0]], F_ROWS)
    c8 = jnp.pad(c, ((0, 8 - n), (0, 0)))
    cctx8 = jnp.pad(c_ctx[None], ((0, 7), (0, 0)))
    adab_sh = lax.dynamic_slice(ada_b, (0, q * nsh), (1, nsh))
    w_all, fall, call, mall = gather_weights(c8, cctx8, ada_w[0], adab_sh, w_sh, fp)

    mod_all = jnp.transpose(mall, (1, 0, 2)).reshape(MOD_ROWS, 3 * D)
    mod_mine = lax.dynamic_slice(mod_all, (8 * dev, 0), (n, 3 * D))
    mod_ctx = mod_all[64:65]
    shift = jnp.concatenate([mod_mine[:, 0:D], mod_ctx[:, 0:D]], 0)[:, None, :]
    scale1 = 1.0 + jnp.concatenate([mod_mine[:, D:2 * D], mod_ctx[:, D:2 * D]], 0)[:, None, :]
    gate = mod_mine[:, 2 * D:3 * D][:, None, :]

    own = lambda i, mine, got: jnp.where(q == i, mine, got)
    g1, g2, g3, g4, g5 = _group_cols(jnp.concatenate([own(i, w_sh, w_all[i]) for i in range(N_CHIPS)], 1))
    wts = dict(w1=g1, w2=g2, w3=g3, w4=g4, w5=g5)
    f_parts = [_unpack(fall[i], [conv_w.shape[1:], decay_up_fwd.shape[1:], decay_up_bwd.shape[1:]]) for i in range(N_CHIPS)]
    conv_w_full = jnp.concatenate([p[0] for p in f_parts], 1)
    upf_full = jnp.concatenate([p[1] for p in f_parts], 1)
    upb_full = jnp.concatenate([p[2] for p in f_parts], 1)
    b1, b2, b3, b4, b5 = _group_cols(b_in)
    small = dict(b1=b1, b2=b2, b3=b3, b4=b4, b5=b5, norm_g=norm_g,
                 conv_w=jnp.pad(conv_w_full, ((0, 1), (0, 0))), conv_b=conv_b, conv_ln_g=conv_ln_g, conv_ln_b=conv_ln_b,
                 upf=_split3(_pad_up(upf_full, 0)), upb=_split3(_pad_up(upb_full, 16)),
                 bias_f=decay_bias_fwd, bias_b=decay_bias_bwd,
                 gla_norm_g=gla_norm_g, final_norm_g=final_norm_g[None])

    core = ac.astype(jnp.int32).reshape(1)
    chip = q.astype(jnp.int32).reshape(1)
    loss_part, dh, dps, g, got, (pap16, rbp) = local_step(x, ctx, loss_target, (scale1, shift, gate), wts, small,
                                                          p_sh, q, core)
    loss = lax.psum(loss_part[0, 0], ("x", "y", "c"))

    gs = [g["w%d" % i] for i in range(1, 6)]
    d_b_in = _ungroup_cols(*[g["b%d" % i] for i in range(1, 6)])
    early = [d_b_in, g["conv_b"].sum(0), g["conv_ln_g"], g["conv_ln_b"], g["bias_f"], g["bias_b"],
             g["gla_norm_g"], g["final_norm_g"], g["conv_w"].sum(0)[:CONV_K], g["upf"][0:16], g["upb"][16:32]]
    early_shapes = [a.shape for a in early]
    got[2], sall1 = pair_swap([gs[2]], _pack(early))
    halves = pair_add_groups(core, gs, [got[i] for i in range(5)], tr=128)
    paw16 = _ungroup_to_shards(halves)
    grad_x2, dshift, dscale, g["norm_g"], rbw = dgrad_norm_bwd(
        dps, [wts["w%d" % i] for i in range(1, 6)], paw16, x.reshape(n * SEQ, D), ctx.reshape(n * NCTX, D), dh,
        scale1, norm_g, tm=256)

    dm_mine = jnp.concatenate([dshift[:n, 0], dscale[:n, 0], g["gate"][:, 0]], -1)
    dm_ctx = jnp.concatenate([dshift[n, 0], dscale[n, 0], jnp.zeros((D,), F32)], -1)
    late = [g["norm_g"], dm_mine, dm_ctx]
    late_shapes = [a.shape for a in late]
    sall2 = gather_small(_pack(late))
    (s_b_in, s_conv_b, s_ln_g, s_ln_b, s_bias_f, s_bias_b, s_gla_g, s_final_g, s_conv_w, s_upf,
     s_upb) = _unpack(sum_devices(sall1, name="sum_devices_early"), early_shapes)
    s_norm_g = _unpack(sum_devices(sall2, name="sum_devices_late"), late_shapes)[0]
    r_mine, r_ctx = 1, 1 + 3 * n
    dm_all = sall2[:, r_mine:r_ctx].reshape(N_DEV, n, 3 * D)
    dm_full = jnp.concatenate([jnp.pad(dm_all, ((0, 0), (0, 8 - n), (0, 0))).reshape(8 * N_DEV, 3 * D),
                               sall2[:, r_ctx:r_ctx + 3].reshape(N_DEV, 3 * D)], 0)
    dm_shard = lax.dynamic_slice(dm_full, (0, q * nsh), (MOD_ROWS, nsh))
    cctx_rows = jnp.broadcast_to(c_ctx[None], (8, D))
    g_ada_w, g_ada_b, pq = ada_bwd(call, cctx_rows, dm_shard, dm_full, ada_w[0])

    place = jnp.concatenate([chip, core])
    ghw = chip_add(place, paw16, rbw, name="chip_add_w", tr=128)
    ghp = chip_add(place, pap16, rbp, name="chip_add_p", tr=384)
    gw_mine, gp_mine, pq_all = pair_share(ghw, ghp, pq)
    gp_mine = gp_mine.reshape(768, D)
    g_c_ctx = cctx_grad(pq_all, cctx_rows)[0]

    grads = dict(
        c_ctx=g_c_ctx, ada_w=g_ada_w[None], ada_b=g_ada_b, norm_g=s_norm_g,
        w_in=gw_mine.reshape(1, D, W_IN_SHARD), b_in=s_b_in,
        conv_w=lax.dynamic_slice(s_conv_w, (0, q * 256), (CONV_K, 256))[None], conv_b=s_conv_b,
        conv_ln_g=s_ln_g, conv_ln_b=s_ln_b, conv_proj=gp_mine[0:256][None],
        decay_up_fwd=lax.dynamic_slice(s_upf, (0, q * 128), (16, 128))[None], decay_bias_fwd=s_bias_f,
        decay_up_bwd=lax.dynamic_slice(s_upb, (0, q * 128), (16, 128))[None], decay_bias_bwd=s_bias_b,
        gla_norm_g=s_gla_g, gla_proj=gp_mine[256:512][None], w_out=gp_mine[512:768][None],
        final_norm_g=s_final_g[0])

    delta, new_m, new_v = {}, {}, {}
    for name in ["ada_w", "conv_proj", "gla_proj", "w_out"]:
        delta[name], new_m[name], new_v[name] = adamw2d(w[name], grads[name].reshape(w[name].shape), m[name], v[name],
                                                        name="adamw_" + name, tr=128)
    tr_ = lambda a: jnp.swapaxes(a, 1, 2)
    g_w_in_t = tr_(grads["w_in"])
    grads["w_in"] = tr_(g_w_in_t)
    d_, m_, v_ = adamw2d(tr_(w_in), g_w_in_t, tr_(m_w_in), tr_(v_w_in), name="adamw_w_in", tr=W_IN_SHARD, tcols=128)
    delta["w_in"], new_m["w_in"], new_v["w_in"] = tr_(d_), tr_(m_), tr_(v_)
    d_, m_, v_ = adamw_many([w[nm] for nm in SMALL_NAMES], [grads[nm].reshape(w[nm].shape) for nm in SMALL_NAMES],
                            [m[nm] for nm in SMALL_NAMES], [v[nm] for nm in SMALL_NAMES])
    for nm, a, b, cc in zip(SMALL_NAMES, d_, m_, v_):
        delta[nm], new_m[nm], new_v[nm] = a, b, cc

    grad_x = grad_x2.reshape(x.shape)
    return (loss, grad_x, *[grads[nm].reshape(w[nm].shape) for nm in WEIGHT_NAMES], *[delta[nm] for nm in WEIGHT_NAMES],
            *[new_m[nm] for nm in WEIGHT_NAMES], *[new_v[nm] for nm in WEIGHT_NAMES])
```

```python
import jax
import jax.numpy as jnp
from jax import lax
from jax.experimental import pallas as pl
from jax.experimental.pallas import tpu as pltpu

F32 = jnp.float32
BF16 = jnp.bfloat16
MESH = pl.DeviceIdType.MESH
HI = lax.Precision.HIGHEST

D = 1024
SEQ = 2048
GRID_W = 64
GRID_H = SEQ // GRID_W
NCTX = 256
SEQ_ALL = SEQ + NCTX
EPS = 1e-6
CONV_K = 31
CONV_PAD = CONV_K // 2
HEADS = 4
HEAD_K = 128
HEAD_V = 256
GLA_DK = HEADS * HEAD_K
GATE_TAU = 16.0
Q_SCALE = HEAD_K ** -0.5
CHUNK = 64
NCHUNK = SEQ_ALL // CHUNK
NCHUNK_LAT = SEQ // CHUNK
NCHUNK_CTX = NCHUNK - NCHUNK_LAT
SUB = 64
NSUB = CHUNK // SUB
N_IN = 8224
W3 = 2176
O3_V, O3_Q, O3_K, O3_AB = 0, 1024, 1536, 2048

ADAM_LR, ADAM_B1, ADAM_B2, ADAM_EPS, ADAM_WD, ADAM_STEP = 0.001, 0.9, 0.999, 1e-08, 0.01, 10
VMEM_LIMIT = 56 * 1024 * 1024

N_CHIPS = 4
N_DEV = 8
W_IN_SHARD = N_IN // N_CHIPS
MOD_ROWS = 72


def _pallas(body, **kw):
    return pl.pallas_call(body, **kw)


def _params(sem=None, **kw):
    if sem is not None:
        kw["dimension_semantics"] = sem
    return pltpu.CompilerParams(vmem_limit_bytes=VMEM_LIMIT, **kw)


def _sigmoid(v):
    return 1.0 / (1.0 + jnp.exp(-v))


def _silu(v):
    return v * _sigmoid(v)


def _dsilu(v):
    s = _sigmoid(v)
    return s * (1.0 + v * (1.0 - s))


def _log_sigmoid(v):
    return jnp.minimum(v, 0.0) - jnp.log(1.0 + jnp.exp(-jnp.abs(v)))


def _dot(a, b, dims, precision=None):
    return lax.dot_general(a, b, (dims, ((), ())), preferred_element_type=F32, precision=precision)


def _nn(a, b, precision=None):
    return _dot(a, b, ((1,), (0,)), precision)


def _nt(a, b, precision=None):
    return _dot(a, b, ((1,), (1,)), precision)


def _tn(a, b, precision=None):
    return _dot(a, b, ((0,), (0,)), precision)


def _b16(v):
    return v.astype(BF16)


def matmul_nn(a, b, *, name, m, tm, tn, out_dtype):
    k = a.shape[1]
    n = b.shape[1]

    def body(a_ref, b_ref, o_ref):
        o_ref[...] = _nn(a_ref[...], b_ref[...]).astype(o_ref.dtype)

    return _pallas(
        body, name=name, grid=(n // tn, m // tm),
        in_specs=[pl.BlockSpec((tm, k), lambda j, i: (i, 0)), pl.BlockSpec((k, tn), lambda j, i: (0, j))],
        out_specs=pl.BlockSpec((tm, tn), lambda j, i: (i, j)),
        out_shape=jax.ShapeDtypeStruct((m, n), out_dtype),
        compiler_params=_params(("parallel", "parallel")),
    )(a, b)


def proj_all(u, ws, bs, rows, p_sh, *, tm):
    k = u.shape[1]
    n_g = len(ws)
    tns = [w.shape[1] if w.shape[1] % 1024 else 1024 for w in ws]
    mts = [r // tm for r in rows]
    cnts = [(w.shape[1] // tn) * mt for w, tn, mt in zip(ws, tns, mts)]
    los = [sum(cnts[:g]) for g in range(n_g)]
    n_steps = sum(cnts)

    def rel(s, g):
        return jnp.clip(s - los[g], 0, cnts[g] - 1)

    def active(s, g):
        return (s >= los[g]) & (s < los[g] + cnts[g])

    def u_row(s):
        r = 0
        for g in range(n_g):
            r = r + jnp.where(active(s, g), rel(s, g) % mts[g], 0)
        return r

    def body(*refs):
        u_ref = refs[0]
        w_refs, b_refs = refs[1:1 + n_g], refs[1 + n_g:1 + 2 * n_g]
        p_ref = refs[1 + 2 * n_g]
        o_refs = refs[2 + 2 * n_g:2 + 3 * n_g]
        pall_ref = refs[2 + 3 * n_g]
        w_send, w_recv, h_send, h_recv = refs[3 + 3 * n_g:]
        s = pl.program_id(0)
        for g in range(n_g):
            @pl.when(active(s, g))
            def _(g=g):
                o_refs[g][...] = (_nn(u_ref[...], w_refs[g][...]) + b_refs[g][...]).astype(o_refs[g].dtype)

        x, y, c, chips = _place()
        q = 2 * x + y
        mine = _half_chunks(c, (0, p_ref.shape[0]), 16, which=(1,))
        other = _half_chunks(1 - c, (0, p_ref.shape[0]), 16, which=(1,))
        nb = len(mine)

        def bulk():
            return [[_remote(p_ref.at[rws], pall_ref.at[q, rws], w_send.at[pj * nb + pi], w_recv.at[pj * nb + pi],
                             (*chips[pj], c)) for pi, (_, rws) in enumerate(mine)] for pj in range(3)]

        @pl.when(s == 0)
        def _():
            for cp in sum(bulk(), []):
                cp.start()

        @pl.when(s == n_steps - 1)
        def _():
            handed = []
            for pj, (cx, cy) in enumerate(chips):
                for pi, (_, rws) in enumerate(mine):
                    bulk()[pj][pi].wait_recv()
                    cp = _remote(pall_ref.at[2 * cx + cy, rws], pall_ref.at[2 * cx + cy, rws],
                                 h_send.at[pj * nb + pi], h_recv.at[pj * nb + pi], (x, y, 1 - c))
                    cp.start()
                    handed.append(cp)
            for pj, (cx, cy) in enumerate(chips):
                for pi, (_, rws) in enumerate(other):
                    _remote(pall_ref.at[2 * cx + cy, rws], pall_ref.at[2 * cx + cy, rws],
                            h_send.at[pj * nb + pi], h_recv.at[pj * nb + pi], (x, y, 1 - c)).wait_recv()
            for cp in sum(bulk(), []) + handed:
                cp.wait_send()

    any_spec = pl.BlockSpec(memory_space=pl.ANY)
    in_specs = [pl.BlockSpec((tm, k), lambda s: (u_row(s), 0))]
    in_specs += [pl.BlockSpec((k, tns[g]), lambda s, g=g: (0, rel(s, g) // mts[g])) for g in range(n_g)]
    in_specs += [pl.BlockSpec((1, tns[g]), lambda s, g=g: (0, rel(s, g) // mts[g])) for g in range(n_g)]
    in_specs.append(any_spec)
    out_specs = [pl.BlockSpec((tm, tns[g]), lambda s, g=g: (rel(s, g) % mts[g], rel(s, g) // mts[g])) for g in range(n_g)]
    out_specs.append(any_spec)
    out_shape = [jax.ShapeDtypeStruct((rows[g], ws[g].shape[1]), BF16) for g in range(n_g)]
    out_shape.append(jax.ShapeDtypeStruct((N_CHIPS,) + p_sh.shape, p_sh.dtype))
    return _pallas(
        body, name="proj_all", grid=(n_steps,), in_specs=in_specs, out_specs=out_specs, out_shape=out_shape,
        scratch_shapes=[pltpu.SemaphoreType.DMA((3 * P_ROW_CHUNKS,)) for _ in range(4)],
        compiler_params=_params(("arbitrary",)),
    )(u, *ws, *bs, p_sh)


def matmul_tn(a, b, *, name, t, tn, tt, colsum=False, swap=None):
    m = a.shape[1]
    n = b.shape[1]
    nj, ns = n // tn, t // tt
    n_out = 2 if colsum else 1
    n_sw = 0 if swap is None else len(swap)

    def body(a_ref, b_ref, *rest):
        o_ref = rest[n_sw]
        cs_ref = rest[n_sw + 1] if colsum else None
        j, s = pl.program_id(0), pl.program_id(1)

        if swap is not None:
            g_refs = rest[:n_sw]
            got_refs = rest[n_sw + n_out:2 * n_sw + n_out]
            sems = rest[2 * n_sw + n_out:]

            @pl.when((j == 0) & (s == 0))
            def _():
                for cp in _pair_copies(g_refs, None, got_refs, None, *sems):
                    cp.start()

            @pl.when((j == nj - 1) & (s == ns - 1))
            def _():
                for cp in _pair_copies(g_refs, None, got_refs, None, *sems):
                    cp.wait_recv()
                for cp in _pair_copies(g_refs, None, got_refs, None, *sems):
                    cp.wait_send()

        @pl.when(s == 0)
        def _():
            o_ref[...] = jnp.zeros_like(o_ref)
            if colsum:
                cs_ref[...] = jnp.zeros_like(cs_ref)
        o_ref[...] += _tn(a_ref[...], b_ref[...])
        if colsum:
            cs_ref[...] += jnp.sum(b_ref[...].astype(F32), axis=0, keepdims=True)

    in_specs = [pl.BlockSpec((tt, m), lambda j, s: (s, 0)), pl.BlockSpec((tt, tn), lambda j, s: (s, j))]
    out_specs = [pl.BlockSpec((m, tn), lambda j, s: (0, j))]
    out_shape = [jax.ShapeDtypeStruct((m, n), F32)]
    if colsum:
        out_specs.append(pl.BlockSpec((1, tn), lambda j, s: (0, j)))
        out_shape.append(jax.ShapeDtypeStruct((1, n), F32))
    args, scratch = [a, b], []
    if swap is not None:
        any_spec = pl.BlockSpec(memory_space=pl.ANY)
        in_specs += [any_spec] * n_sw
        out_specs += [any_spec] * n_sw
        out_shape += _pair_got_shapes(swap, None)
        args += list(swap)
        scratch = [pltpu.SemaphoreType.DMA((_pair_count(swap, None),)), pltpu.SemaphoreType.DMA((_pair_count(swap, None),))]
    return _pallas(
        body, name=name, grid=(nj, ns), in_specs=in_specs, out_specs=out_specs, out_shape=out_shape,
        scratch_shapes=scratch,
        compiler_params=_params(("parallel" if swap is None else "arbitrary", "arbitrary")),
    )(*args)


def dgrad_norm_bwd(dps, wts, paw, x2, ctx2, dh, scale1, norm_g, *, tm):
    t, tc = x2.shape[0], ctx2.shape[0]
    t_all = t + tc
    n_lat, n_ctx = t // tm, tc // tm
    n_tiles = n_lat + n_ctx
    n_samples = scale1.shape[0] - 1
    tps = n_lat // n_samples
    n_grp = n_samples + 1
    n_g = len(dps)
    whole = [g for g in range(n_g) if dps[g].shape[0] == t_all]
    latent = [g for g in range(n_g) if dps[g].shape[0] != t_all]

    def body(*refs):
        dp_refs, w_refs = refs[:n_g], refs[n_g:2 * n_g]
        (paw_ref, x_ref, c_ref, dh_ref, sc_ref, g_ref, dx_ref, dsh_ref, dsc_ref, dg_ref, rbw_ref,
         du_buf, b_send, b_recv) = refs[2 * n_g:]
        i = pl.program_id(0)

        def exchange():
            x, y, c, chips = _place()
            chunks = _half_chunks(0, (2 * paw_ref.shape[1],), 16, which=(0,))
            return [_remote(paw_ref.at[2 * cx + cy, rows], rbw_ref.at[j, rows],
                            b_send.at[j * N_BULK + k], b_recv.at[j * N_BULK + k], (cx, cy, c))
                    for j, (cx, cy) in enumerate(chips) for k, (_, rows) in enumerate(chunks)]

        @pl.when(i == 0)
        def _():
            for cp in exchange():
                cp.start()

        acc = None
        for g in whole:
            part = _nt(dp_refs[g][...], w_refs[g][...])
            acc = part if acc is None else acc + part
        du_buf[...] = acc

        @pl.when(i < n_lat)
        def _():
            lat = None
            for g in latent:
                part = _nt(dp_refs[g][...], w_refs[g][...])
                lat = part if lat is None else lat + part
            du_buf[...] += lat

        duv = du_buf[...]
        xv = jnp.where(i < n_lat, x_ref[...], c_ref[...])
        rs = lax.rsqrt(jnp.mean(xv * xv, axis=-1, keepdims=True) + EPS)
        xh = xv * rs
        n = xh * g_ref[...]
        dn = duv * sc_ref[0]
        dxh = dn * g_ref[...]
        dx = rs * (dxh - xh * jnp.mean(dxh * xh, axis=-1, keepdims=True))

        @pl.when(i < n_lat)
        def _():
            dx_ref[...] = dx + dh_ref[...]

        @pl.when((i % tps == 0) & (i <= n_lat))
        def _():
            dsh_ref[...] = jnp.zeros_like(dsh_ref)
            dsc_ref[...] = jnp.zeros_like(dsc_ref)

        @pl.when(i == 0)
        def _():
            dg_ref[...] = jnp.zeros_like(dg_ref)

        dsh_ref[0] += jnp.sum(duv, axis=0, keepdims=True)
        dsc_ref[0] += jnp.sum(duv * n, axis=0, keepdims=True)
        dg_ref[...] += jnp.sum(dn * xh, axis=0, keepdims=True)

        @pl.when(i == n_tiles - 1)
        def _():
            for cp in exchange():
                cp.wait_recv()
            for cp in exchange():
                cp.wait_send()

    lat = lambda i: (jnp.minimum(i, n_lat - 1), 0)
    grp = lambda i: (jnp.minimum(i // tps, n_samples), 0, 0)
    in_specs = []
    for g, dp in enumerate(dps):
        nrow = dp.shape[0] // tm
        in_specs.append(pl.BlockSpec((tm, dp.shape[1]), lambda i, nrow=nrow: (jnp.minimum(i, nrow - 1), 0)))
    for w in wts:
        in_specs.append(pl.BlockSpec(w.shape, lambda i: (0, 0), pipeline_mode=pl.Buffered(1)))
    any_spec = pl.BlockSpec(memory_space=pl.ANY)
    in_specs += [any_spec,
                 pl.BlockSpec((tm, D), lat), pl.BlockSpec((tm, D), lambda i: (jnp.maximum(i - n_lat, 0), 0)),
                 pl.BlockSpec((tm, D), lat), pl.BlockSpec((1, 1, D), grp), pl.BlockSpec((1, D), lambda i: (0, 0))]
    return _pallas(
        body, name="dgrad_norm_bwd", grid=(n_tiles,), in_specs=in_specs,
        out_specs=[pl.BlockSpec((tm, D), lat), pl.BlockSpec((1, 1, D), grp), pl.BlockSpec((1, 1, D), grp),
                   pl.BlockSpec((1, D), lambda i: (0, 0)), any_spec],
        out_shape=[jax.ShapeDtypeStruct((t, D), F32), jax.ShapeDtypeStruct((n_grp, 1, D), F32),
                   jax.ShapeDtypeStruct((n_grp, 1, D), F32), jax.ShapeDtypeStruct((1, D), F32),
                   jax.ShapeDtypeStruct((3,) + paw.shape[1:], paw.dtype)],
        scratch_shapes=[pltpu.VMEM((tm, D), F32), pltpu.SemaphoreType.DMA((3 * N_BULK,)),
                        pltpu.SemaphoreType.DMA((3 * N_BULK,))],
        compiler_params=_params(("arbitrary",)),
    )(*dps, *wts, paw, x2, ctx2, dh, scale1, norm_g)


TM_NORM = 512


def norm_mod_fwd(x2, ctx2, scale1, shift, norm_g):
    t = x2.shape[0]
    n_lat = t // TM_NORM
    assert ctx2.shape[0] == TM_NORM
    n_samples = scale1.shape[0] - 1
    tps = n_lat // n_samples

    def body(x_ref, c_ref, sc_ref, sh_ref, g_ref, u_ref):
        i = pl.program_id(0)
        xv = jnp.where(i < n_lat, x_ref[...], c_ref[...])
        rs = lax.rsqrt(jnp.mean(xv * xv, axis=-1, keepdims=True) + EPS)
        u = xv * rs * g_ref[...] * sc_ref[0] + sh_ref[0]
        u_ref[...] = u.astype(u_ref.dtype)

    grp = lambda i: (jnp.minimum(i // tps, n_samples), 0, 0)
    return _pallas(
        body, name="norm_mod_fwd", grid=(n_lat + 1,),
        in_specs=[pl.BlockSpec((TM_NORM, D), lambda i: (jnp.minimum(i, n_lat - 1), 0)),
                  pl.BlockSpec((TM_NORM, D), lambda i: (0, 0)),
                  pl.BlockSpec((1, 1, D), grp), pl.BlockSpec((1, 1, D), grp),
                  pl.BlockSpec((1, D), lambda i: (0, 0))],
        out_specs=pl.BlockSpec((TM_NORM, D), lambda i: (i, 0)),
        out_shape=jax.ShapeDtypeStruct((t + TM_NORM, D), BF16),
        compiler_params=_params(("parallel",)),
    )(x2, ctx2, scale1, shift, norm_g)


CONV_CB = 256
CONV_NCB = D // CONV_CB
H_OFF = 16


H_CB = 128
H_SPAN = GRID_W + 2 * H_OFF - 8


def _conv_scratch(vertical):
    if vertical:
        return [pltpu.VMEM((GRID_H + 2 * CONV_PAD, GRID_W, CONV_CB), F32)]
    return [pltpu.VMEM((GRID_H, GRID_W + 2 * H_OFF, H_CB), F32), pltpu.VMEM((7, GRID_H, H_SPAN, H_CB), F32)]


def _conv_fill(bufs, img, vertical):
    pad_ref = bufs[0]
    pad_ref[...] = jnp.zeros_like(pad_ref)
    if vertical:
        pad_ref[pl.ds(CONV_PAD, GRID_H)] = img
        return
    pad_ref[:, pl.ds(H_OFF, GRID_W), :] = img

    def shift(r, carry):
        for s in range(1, 8):
            bufs[1][s - 1, r] = pad_ref[r, pl.ds(s, H_SPAN), :]
        return carry

    lax.fori_loop(0, GRID_H, shift, 0)


def _conv_window(bufs, k, vertical, r, w0=0, nw=GRID_W, lanes=slice(None)):
    if vertical:
        return bufs[0][r + k, pl.ds(w0, nw), lanes]
    off = H_OFF - CONV_PAD + k
    if off % 8 == 0:
        return bufs[0][r, pl.ds(off + w0, nw), lanes]
    return bufs[1][off % 8 - 1, r, pl.ds(off - off % 8 + w0, nw), lanes]


def _conv_col_blocks(vertical):
    if vertical:
        return [pl.ds(0, CONV_CB)]
    return [pl.ds(i * H_CB, H_CB) for i in range(CONV_CB // H_CB)]


def _rows(r):
    return pl.ds(pl.multiple_of(r * GRID_W, GRID_W), GRID_W)


def conv_fwd(p1, conv_w, conv_b, n_samples):
    t = n_samples * SEQ

    def make(vertical, prev):
        n_buf = len(_conv_scratch(vertical))

        def body(gv_ref, gg_ref, w_ref, b_ref, *rest):
            o_ref, bufs = rest[-1 - n_buf], rest[-n_buf:]
            for cols in _conv_col_blocks(vertical):
                a = gv_ref[:, cols].astype(F32) * _sigmoid(gg_ref[:, cols].astype(F32))
                _conv_fill(bufs, a.reshape(GRID_H, GRID_W, a.shape[-1]), vertical)

                def row(r, carry, cols=cols):
                    acc = jnp.zeros((GRID_W, cols.size), F32) + b_ref[:, cols]
                    for k in range(CONV_K):
                        acc = acc + _conv_window(bufs, k, vertical, r) * w_ref[pl.ds(k, 1), cols]
                    o_ref[_rows(r), cols] = acc
                    return carry

                lax.fori_loop(0, GRID_H, row, 0)

        cb0 = CONV_NCB // 2 if vertical else 0
        in_specs = [pl.BlockSpec((SEQ, CONV_CB), lambda b, j: (b, 2 * (cb0 + j))),
                    pl.BlockSpec((SEQ, CONV_CB), lambda b, j: (b, 2 * (cb0 + j) + 1)),
                    pl.BlockSpec((CONV_K + 1, CONV_CB), lambda b, j: (0, cb0 + j)),
                    pl.BlockSpec((1, CONV_CB), lambda b, j: (0, cb0 + j))]
        args = [p1, p1, conv_w, conv_b]
        aliases = {}
        if prev is not None:
            in_specs.append(pl.BlockSpec(memory_space=pl.ANY))
            args.append(prev)
            aliases = {4: 0}
        return _pallas(
            body, name="conv_fwd_v" if vertical else "conv_fwd_h", grid=(n_samples, CONV_NCB // 2),
            in_specs=in_specs,
            out_specs=pl.BlockSpec((SEQ, CONV_CB), lambda b, j: (b, cb0 + j)),
            out_shape=jax.ShapeDtypeStruct((t, D), F32),
            scratch_shapes=_conv_scratch(vertical),
            input_output_aliases=aliases,
            compiler_params=_params(("parallel", "parallel")),
        )(*args)

    return make(True, make(False, None))


def conv_bwd(p1, daconv, conv_w, n_samples):
    t = n_samples * SEQ

    def make(vertical, prev):
        n_buf = len(_conv_scratch(vertical))

        def body(gv_ref, gg_ref, dy_ref, w_ref, *rest):
            dp_ref, dw_ref, db_ref = rest[-3 - 2 * n_buf - 1:-2 * n_buf - 1]
            a_bufs, d_bufs, da_ref = rest[-2 * n_buf - 1:-n_buf - 1], rest[-n_buf - 1:-1], rest[-1]
            for cols in _conv_col_blocks(vertical):
                width = cols.size
                gv = gv_ref[:, cols].astype(F32)
                sg = _sigmoid(gg_ref[:, cols].astype(F32))
                _conv_fill(a_bufs, (gv * sg).reshape(GRID_H, GRID_W, width), vertical)
                _conv_fill(d_bufs, dy_ref[:, cols].reshape(GRID_H, GRID_W, width), vertical)

                def row(r, carry, cols=cols, width=width):
                    acc = jnp.zeros((GRID_W, width), F32)
                    for k in range(CONV_K):
                        acc = acc + _conv_window(d_bufs, CONV_K - 1 - k, vertical, r) * w_ref[pl.ds(k, 1), cols]
                    da_ref[_rows(r), cols] = acc
                    return carry

                lax.fori_loop(0, GRID_H, row, 0)
                da = da_ref[:, cols]
                dp_ref[:, pl.ds(cols.start, width)] = (da * sg).astype(dp_ref.dtype)
                dp_ref[:, pl.ds(CONV_CB + cols.start, width)] = (da * gv * sg * (1.0 - sg)).astype(dp_ref.dtype)

                for lb in range(width // 128):
                    lanes = pl.ds(lb * 128, 128)
                    dy_lanes = pl.ds(cols.start + lb * 128, 128)

                    def wrow(r, accs, lanes=lanes, dy_lanes=dy_lanes):
                        for w0 in range(0, GRID_W, 8):
                            dyv = dy_ref[pl.ds(pl.multiple_of(r * GRID_W, GRID_W) + w0, 8), dy_lanes]
                            accs = tuple(accs[k] + _conv_window(a_bufs, k, vertical, r, w0, 8, lanes) * dyv
                                         for k in range(CONV_K))
                        return accs

                    accs = lax.fori_loop(0, GRID_H, wrow, tuple(jnp.zeros((8, 128), F32) for _ in range(CONV_K)))
                    for k in range(CONV_K):
                        dw_ref[0, pl.ds(k, 1), dy_lanes] = jnp.sum(accs[k], axis=0, keepdims=True)
            dw_ref[0, pl.ds(CONV_K, 1), :] = jnp.zeros((1, CONV_CB), F32)
            db_ref[0] = jnp.sum(dy_ref[...], axis=0, keepdims=True)

        cb0 = CONV_NCB // 2 if vertical else 0
        in_specs = [pl.BlockSpec((SEQ, CONV_CB), lambda b, j: (b, 2 * (cb0 + j))),
                    pl.BlockSpec((SEQ, CONV_CB), lambda b, j: (b, 2 * (cb0 + j) + 1)),
                    pl.BlockSpec((SEQ, CONV_CB), lambda b, j: (b, cb0 + j)),
                    pl.BlockSpec((CONV_K + 1, CONV_CB), lambda b, j: (0, cb0 + j))]
        args = [p1, p1, daconv, conv_w]
        aliases = {}
        if prev is not None:
            in_specs += [pl.BlockSpec(memory_space=pl.ANY)] * 3
            args += list(prev)
            aliases = {4: 0, 5: 1, 6: 2}
        return _pallas(
            body, name="conv_bwd_v" if vertical else "conv_bwd_h", grid=(n_samples, CONV_NCB // 2),
            in_specs=in_specs,
            out_specs=[pl.BlockSpec((SEQ, 2 * CONV_CB), lambda b, j: (b, cb0 + j)),
                       pl.BlockSpec((1, CONV_K + 1, CONV_CB), lambda b, j: (b, 0, cb0 + j)),
                       pl.BlockSpec((1, 1, CONV_CB), lambda b, j: (b, 0, cb0 + j))],
            out_shape=[jax.ShapeDtypeStruct((t, 2 * D), BF16),
                       jax.ShapeDtypeStruct((n_samples, CONV_K + 1, D), F32),
                       jax.ShapeDtypeStruct((n_samples, 1, D), F32)],
            scratch_shapes=_conv_scratch(vertical) + _conv_scratch(vertical) + [pltpu.VMEM((SEQ, CONV_CB), F32)],
            input_output_aliases=aliases,
            compiler_params=_params(("parallel", "parallel")),
        )(*args)

    return make(True, make(False, None))


TM_EW = 256


def ln_gate_fwd(aconv, z, ln_g, ln_b):
    t = aconv.shape[0]

    def body(a_ref, z_ref, g_ref, b_ref, o_ref):
        a = a_ref[...]
        mu = jnp.mean(a, axis=-1, keepdims=True)
        xc = a - mu
        rstd = lax.rsqrt(jnp.mean(xc * xc, axis=-1, keepdims=True) + EPS)
        l = xc * rstd * g_ref[...] + b_ref[...]
        o_ref[...] = (_silu(l) * _silu(z_ref[...].astype(F32))).astype(o_ref.dtype)

    row = pl.BlockSpec((TM_EW, D), lambda i: (i, 0))
    vec = pl.BlockSpec((1, D), lambda i: (0, 0))
    return _pallas(
        body, name="ln_gate_fwd", grid=(t // TM_EW,), in_specs=[row, row, vec, vec], out_specs=row,
        out_shape=jax.ShapeDtypeStruct((t, D), BF16), compiler_params=_params(("parallel",)),
    )(aconv, z, ln_g, ln_b)


def ln_gate_bwd(aconv, z, dyc, conv_proj, ln_g, ln_b):
    t = aconv.shape[0]

    def body(a_ref, z_ref, d_ref, w_ref, g_ref, b_ref, da_ref, dz_ref, dg_ref, db_ref):
        a = a_ref[...]
        zv = z_ref[...].astype(F32)
        dac_v = _nt(d_ref[...], w_ref[...])
        mu = jnp.mean(a, axis=-1, keepdims=True)
        xc = a - mu
        rstd = lax.rsqrt(jnp.mean(xc * xc, axis=-1, keepdims=True) + EPS)
        xh = xc * rstd
        l = xh * g_ref[...] + b_ref[...]
        dz_ref[...] = (dac_v * _silu(l) * _dsilu(zv)).astype(dz_ref.dtype)
        dl = dac_v * _silu(zv) * _dsilu(l)
        dxh = dl * g_ref[...]
        da_ref[...] = rstd * (dxh - jnp.mean(dxh, axis=-1, keepdims=True)
                              - xh * jnp.mean(dxh * xh, axis=-1, keepdims=True))

        @pl.when(pl.program_id(0) == 0)
        def _():
            dg_ref[...] = jnp.zeros_like(dg_ref)
            db_ref[...] = jnp.zeros_like(db_ref)

        dg_ref[...] += jnp.sum(dl * xh, axis=0, keepdims=True)
        db_ref[...] += jnp.sum(dl, axis=0, keepdims=True)

    row = pl.BlockSpec((TM_EW, D), lambda i: (i, 0))
    vec = pl.BlockSpec((1, D), lambda i: (0, 0))
    return _pallas(
        body, name="ln_gate_bwd", grid=(t // TM_EW,),
        in_specs=[row, row, row, pl.BlockSpec((D, D), lambda i: (0, 0)), vec, vec],
        out_specs=[row, row, vec, vec],
        out_shape=[jax.ShapeDtypeStruct((t, D), F32), jax.ShapeDtypeStruct((t, D), BF16),
                   jax.ShapeDtypeStruct((1, D), F32), jax.ShapeDtypeStruct((1, D), F32)],
        compiler_params=_params(("arbitrary",)),
    )(aconv, z, dyc, conv_proj, ln_g, ln_b)


TM_PREP = 256
PREP_LAT = SEQ // TM_PREP
PREP_ALL = SEQ_ALL // TM_PREP


def _chunk_tri(n, upper):
    r = lax.broadcasted_iota(jnp.int32, (n, n), 0)
    c = lax.broadcasted_iota(jnp.int32, (n, n), 1)
    same = (r // CHUNK) == (c // CHUNK)
    keep = (c >= r) if upper else (c <= r)
    return jnp.where(same & keep, 1.0, 0.0).astype(F32)


def _split3(v):
    hi = v.astype(BF16)
    r1 = v - hi.astype(F32)
    mid = r1.astype(BF16)
    lo = (r1 - mid.astype(F32)).astype(BF16)
    return jnp.stack([hi, mid, lo])


def _chunk_sums(v, upper):
    tri = _chunk_tri(v.shape[0], upper).astype(BF16)
    pieces = _split3(v)
    return (_nn(tri, pieces[0]) + _nn(tri, pieces[1])) + _nn(tri, pieces[2])


def _gate_logits(ab, up3_ref, bias_ref):
    assert ab.dtype == BF16
    return ((_nn(ab, up3_ref[0]) + _nn(ab, up3_ref[1])) + _nn(ab, up3_ref[2])) + bias_ref[...]


def _prep_tile_maps(n_samples):
    n_lat = n_samples * PREP_LAT

    def seq_map(i):
        return jnp.where(i < n_lat, i // PREP_LAT, i - n_lat), jnp.where(i < n_lat, i % PREP_LAT, PREP_LAT)

    return n_lat, seq_map


def gla_prep_fwd(p3, upf, upb, bias_f, bias_b, n_samples):
    n_lat, seq_map = _prep_tile_maps(n_samples)
    n_tiles = n_lat + n_samples

    def body(v_ref, q_ref, k_ref, ab_ref, upf_ref, upb_ref, bf_ref, bb_ref, qo, ko, vo, cf, cb):
        i = pl.program_id(0)
        qo[0] = jnp.where(i < n_lat, q_ref[...].astype(F32) * Q_SCALE, 0.0)
        ko[0] = k_ref[...]
        vo[0] = v_ref[...]
        ab = ab_ref[...]
        gf = _log_sigmoid(_gate_logits(ab, upf_ref, bf_ref)) * (1.0 / GATE_TAU)
        gb = _log_sigmoid(_gate_logits(ab, upb_ref, bb_ref)) * (1.0 / GATE_TAU)
        cf[0] = _chunk_sums(gf, False)
        cb[0] = _chunk_sums(gb, True)

    def o_spec(w):
        return pl.BlockSpec((1, TM_PREP, w), lambda i: (*seq_map(i), 0))

    full = lambda shape: pl.BlockSpec(shape, lambda i: (0,) * len(shape))
    return _pallas(
        body, name="gla_prep_fwd", grid=(n_tiles,),
        in_specs=[pl.BlockSpec((TM_PREP, 1024), lambda i: (i, O3_V // 1024)),
                  pl.BlockSpec((TM_PREP, 512), lambda i: (i, O3_Q // 512)),
                  pl.BlockSpec((TM_PREP, 512), lambda i: (i, O3_K // 512)),
                  pl.BlockSpec((TM_PREP, 128), lambda i: (i, O3_AB // 128)),
                  full((3, 128, GLA_DK)), full((3, 128, GLA_DK)), full((1, GLA_DK)), full((1, GLA_DK))],
        out_specs=[o_spec(GLA_DK), o_spec(GLA_DK), o_spec(D), o_spec(GLA_DK), o_spec(GLA_DK)],
        out_shape=[jax.ShapeDtypeStruct((n_samples, SEQ_ALL, GLA_DK), F32),
                   jax.ShapeDtypeStruct((n_samples, SEQ_ALL, GLA_DK), p3.dtype),
                   jax.ShapeDtypeStruct((n_samples, SEQ_ALL, D), p3.dtype),
                   jax.ShapeDtypeStruct((n_samples, SEQ_ALL, GLA_DK), F32),
                   jax.ShapeDtypeStruct((n_samples, SEQ_ALL, GLA_DK), F32)],
        compiler_params=_params(("parallel",)),
    )(p3, p3, p3, p3, upf, upb, bias_f, bias_b)


def gla_prep_bwd(p3, dq_f, dq_b, dk_f, dk_b, dv_f, dv_b, dc_f, dc_b, upf, upb, bias_f, bias_b, n_samples):
    n_lat, seq_map = _prep_tile_maps(n_samples)
    n_tiles = n_lat + n_samples

    def body(ab_ref, dqf, dqb, dkf, dkb, dvf, dvb, dcf, dcb, upf_ref, upb_ref, bf_ref, bb_ref,
             dp_ref, duf_ref, dub_ref, dbf_ref, dbb_ref):
        i = pl.program_id(0)
        both = lambda a, b: a[0].astype(F32) + b[0].astype(F32)
        dp_ref[:, pl.ds(O3_V, D)] = both(dvf, dvb).astype(dp_ref.dtype)
        dq = jnp.where(i < n_lat, both(dqf, dqb) * Q_SCALE, 0.0)
        dp_ref[:, pl.ds(O3_Q, GLA_DK)] = dq.astype(dp_ref.dtype)
        dp_ref[:, pl.ds(O3_K, GLA_DK)] = both(dkf, dkb).astype(dp_ref.dtype)
        ab = ab_ref[...]
        zf = _gate_logits(ab, upf_ref, bf_ref)
        zb = _gate_logits(ab, upb_ref, bb_ref)
        dgf = _chunk_sums(dcf[0], True)
        dgb = _chunk_sums(dcb[0], False)
        dzf = _b16(dgf * (1.0 / GATE_TAU) * _sigmoid(-zf))
        dzb = _b16(dgb * (1.0 / GATE_TAU) * _sigmoid(-zb))
        dab = _nt(dzf, upf_ref[0]) + _nt(dzb, upb_ref[0])
        dp_ref[:, pl.ds(O3_AB, 128)] = dab.astype(dp_ref.dtype)

        @pl.when(i == 0)
        def _():
            duf_ref[...] = jnp.zeros_like(duf_ref)
            dub_ref[...] = jnp.zeros_like(dub_ref)
            dbf_ref[...] = jnp.zeros_like(dbf_ref)
            dbb_ref[...] = jnp.zeros_like(dbb_ref)

        duf_ref[...] += _tn(ab, dzf)
        dub_ref[...] += _tn(ab, dzb)
        dbf_ref[...] += jnp.sum(dzf.astype(F32), axis=0, keepdims=True)
        dbb_ref[...] += jnp.sum(dzb.astype(F32), axis=0, keepdims=True)

    def s_spec(w):
        return pl.BlockSpec((1, TM_PREP, w), lambda i: (*seq_map(i), 0))

    full = lambda shape: pl.BlockSpec(shape, lambda i: (0,) * len(shape))
    return _pallas(
        body, name="gla_prep_bwd", grid=(n_tiles,),
        in_specs=[pl.BlockSpec((TM_PREP, 128), lambda i: (i, O3_AB // 128)),
                  s_spec(GLA_DK), s_spec(GLA_DK), s_spec(GLA_DK), s_spec(GLA_DK), s_spec(D), s_spec(D),
                  s_spec(GLA_DK), s_spec(GLA_DK),
                  full((3, 128, GLA_DK)), full((3, 128, GLA_DK)), full((1, GLA_DK)), full((1, GLA_DK))],
        out_specs=[pl.BlockSpec((TM_PREP, W3), lambda i: (i, 0)),
                   full((128, GLA_DK)), full((128, GLA_DK)), full((1, GLA_DK)), full((1, GLA_DK))],
        out_shape=[jax.ShapeDtypeStruct((n_tiles * TM_PREP, W3), BF16),
                   jax.ShapeDtypeStruct((128, GLA_DK), F32), jax.ShapeDtypeStruct((128, GLA_DK), F32),
                   jax.ShapeDtypeStruct((1, GLA_DK), F32), jax.ShapeDtypeStruct((1, GLA_DK), F32)],
        compiler_params=_params(("arbitrary",)),
    )(p3, dq_f, dq_b, dk_f, dk_b, dv_f, dv_b, dc_f, dc_b, upf, upb, bias_f, bias_b)


def _sub_blocks(rev):
    if NSUB == 1:
        return [((0, CHUNK), CHUNK // 2, (0, CHUNK))]
    out = []
    for s in range(NSUB):
        rows = (s * SUB, SUB)
        if rev:
            ref = (s + 1) * SUB if s < NSUB - 1 else None
            cols = (s * SUB, CHUNK - s * SUB)
        else:
            ref = s * SUB - 1 if s > 0 else None
            cols = (0, (s + 1) * SUB)
        out.append((rows, ref, cols))
    return out


def _sub_mask(rows, cols, rev):
    r = rows[0] + lax.broadcasted_iota(jnp.int32, (rows[1], cols[1]), 0)
    c = cols[0] + lax.broadcasted_iota(jnp.int32, (rows[1], cols[1]), 1)
    return (c >= r) if rev else (c <= r)


def _sub_operands(qc, kc, cc, rows, ref, cols):
    cref = jnp.zeros((1, HEAD_K), F32) if ref is None else cc[ref:ref + 1]
    eq = jnp.exp(cc[rows[0]:rows[0] + rows[1]] - cref)
    ek = jnp.exp(cref - cc[cols[0]:cols[0] + cols[1]])
    qs = qc[rows[0]:rows[0] + rows[1]] * eq
    kk = kc[cols[0]:cols[0] + cols[1]] * ek
    return qs, kk, eq, ek


SCAN_ROWS = 256
SCAN_CHUNKS = SCAN_ROWS // CHUNK
SCAN_STEPS = SEQ_ALL // SCAN_ROWS
LAT_BLOCKS = SEQ // SCAN_ROWS


def _scan_block(t, rev):
    if rev:
        return SCAN_STEPS - 1 - t
    return jnp.where(t == 0, SCAN_STEPS - 1, t - 1)


def _scan_lat_block(t, rev):
    if rev:
        return jnp.minimum(SCAN_STEPS - 1 - t, LAT_BLOCKS - 1)
    return jnp.maximum(t - 1, 0)


def _head_cols(h):
    return pl.ds(h * HEAD_K, HEAD_K), pl.ds(h * HEAD_V, HEAD_V)


def gla_scan_fwd(q, k, v, cum, *, rev, name):
    n = q.shape[0]

    def body(q_ref, k_ref, v_ref, c_ref, o_ref, s_ref, sfin_ref, st):
        t = pl.program_id(1)

        @pl.when(t == 0)
        def _():
            st[...] = jnp.zeros_like(st)

        def chunk(j, carry):
            lj = SCAN_CHUNKS - 1 - j if rev else j
            r0 = lj * CHUNK
            rws = pl.ds(r0, CHUNK)
            for h in range(HEADS):
                kcols, vcols = _head_cols(h)
                qc, kc, cc = q_ref[0, rws, kcols], k_ref[0, rws, kcols], c_ref[0, rws, kcols]
                vc = v_ref[0, rws, vcols]
                s_in = st[h]
                s_ref[0, h, j] = _b16(s_in)
                edge = cc[0:1] if rev else cc[CHUNK - 1:CHUNK]
                ke = kc * jnp.exp(edge - cc)
                st[h] = s_in * jnp.exp(edge) + _tn(_b16(vc), _b16(ke))
                o_inter = _nt(_b16(qc * jnp.exp(cc)), _b16(s_in))
                vb = _b16(vc)
                for rows, ref, cols in _sub_blocks(rev):
                    qs, kk, _, _ = _sub_operands(qc, kc, cc, rows, ref, cols)
                    a = jnp.where(_sub_mask(rows, cols, rev), _nt(_b16(qs), _b16(kk)), 0.0)
                    o_s = _nn(_b16(a), vb[cols[0]:cols[0] + cols[1]])
                    o_ref[0, pl.ds(r0 + rows[0], rows[1]), vcols] = _b16(o_inter[rows[0]:rows[0] + rows[1]] + o_s)
            return carry

        for j in range(SCAN_CHUNKS):
            chunk(j, 0)

        @pl.when(t == SCAN_STEPS - 1)
        def _():
            sfin_ref[0] = st[...]

    def spec(w):
        return pl.BlockSpec((1, SCAN_ROWS, w), lambda b, t: (b, _scan_block(t, rev), 0))

    return _pallas(
        body, name=name, grid=(n, SCAN_STEPS),
        in_specs=[spec(GLA_DK), spec(GLA_DK), spec(D), spec(GLA_DK)],
        out_specs=[pl.BlockSpec((1, SCAN_ROWS, D), lambda b, t: (b, _scan_lat_block(t, rev), 0)),
                   pl.BlockSpec((1, HEADS, SCAN_CHUNKS, HEAD_V, HEAD_K), lambda b, t: (b, 0, t, 0, 0)),
                   pl.BlockSpec((1, HEADS, HEAD_V, HEAD_K), lambda b, t: (b, 0, 0, 0))],
        out_shape=[jax.ShapeDtypeStruct((n, SEQ, D), BF16),
                   jax.ShapeDtypeStruct((n, HEADS, NCHUNK, HEAD_V, HEAD_K), BF16),
                   jax.ShapeDtypeStruct((n, HEADS, HEAD_V, HEAD_K), F32)],
        scratch_shapes=[pltpu.VMEM((HEADS, HEAD_V, HEAD_K), F32)],
        compiler_params=_params(("parallel", "arbitrary")),
    )(q, k, v, cum)


def gla_scan_bwd(q, k, v, cum, s_all, s_fin, do, *, rev, name, rider=None):
    n = q.shape[0]

    def body(q_ref, k_ref, v_ref, c_ref, s_ref, sfin_ref, do_ref, *rest):
        if rider is not None:
            ride_in, rest = rest[0], rest[1:]
        dq_ref, dk_ref, dv_ref, dc_ref = rest[:4]
        if rider is not None:
            ride_out, rest = rest[4], rest[:4] + rest[5:]
        dst, s_next, dq_acc, dk_acc, dv_acc = rest[4:9]
        t = SCAN_STEPS - 1 - pl.program_id(1)

        if rider is not None:
            def copies():
                send, recv = rest[9], rest[10]
                if rider[0] == "swap":
                    return _pair_copies([], ride_in, [], ride_out, send, recv)
                x, y, c, chips = _place()
                return [_remote(ride_in.at[2 * cx + cy, rows], ride_out.at[pj, rows], send.at[pj * P_ROW_CHUNKS + pi],
                                recv.at[pj * P_ROW_CHUNKS + pi], (cx, cy, c))
                        for pj, (cx, cy) in enumerate(chips)
                        for pi, (_, rows) in enumerate(_half_chunks(0, (0, 2 * ride_in.shape[1]), 16, which=(1,)))]

            @pl.when((pl.program_id(0) == 0) & (pl.program_id(1) == 0))
            def _():
                for cp in copies():
                    cp.start()

            @pl.when((pl.program_id(0) == n - 1) & (pl.program_id(1) == SCAN_STEPS - 1))
            def _():
                for cp in copies():
                    cp.wait_recv()
                for cp in copies():
                    cp.wait_send()

        @pl.when(pl.program_id(1) == 0)
        def _():
            dst[...] = jnp.zeros_like(dst)
            s_next[...] = sfin_ref[0]

        def chunk(jj, carry):
            j = SCAN_CHUNKS - 1 - jj
            lj = SCAN_CHUNKS - 1 - j if rev else j
            rws = pl.ds(lj * CHUNK, CHUNK)
            for h in range(HEADS):
                kcols, vcols = _head_cols(h)
                qc, kc, cc = q_ref[0, rws, kcols], k_ref[0, rws, kcols], c_ref[0, rws, kcols]
                vc = v_ref[0, rws, vcols]
                doc = jnp.where(t > 0, do_ref[0, rws, vcols], 0.0)
                s_in = s_ref[0, h, j]
                s_out = s_next[h]
                ds_out = dst[h]
                edge = cc[0:1] if rev else cc[CHUNK - 1:CHUNK]
                e_q = jnp.exp(cc)
                e_k = jnp.exp(edge - cc)
                dob = _b16(doc)
                dsb = _b16(ds_out)
                dst[h] = ds_out * jnp.exp(edge) + _tn(dob, _b16(qc * e_q))
                s_next[h] = s_in.astype(F32)
                dq_acc[h] = e_q * _nn(dob, s_in)
                dk_acc[h] = e_k * _nn(_b16(vc), dsb)
                dv_acc[h] = _nt(_b16(kc * e_k), dsb)
                vb = _b16(vc)
                for rows, ref, cols in _sub_blocks(rev):
                    qs, kk, eq, ek = _sub_operands(qc, kc, cc, rows, ref, cols)
                    mask = _sub_mask(rows, cols, rev)
                    rsl = slice(rows[0], rows[0] + rows[1])
                    csl = pl.ds(cols[0], cols[1])
                    qsb, kkb = _b16(qs), _b16(kk)
                    a = jnp.where(mask, _nt(qsb, kkb), 0.0)
                    da = _b16(jnp.where(mask, _nt(dob[rsl], vb[cols[0]:cols[0] + cols[1]]), 0.0))
                    dq_acc[h, pl.ds(rows[0], rows[1]), :] += _nn(da, kkb) * eq
                    dk_acc[h, csl, :] += _tn(da, qsb) * ek
                    dv_acc[h, csl, :] += _tn(_b16(a), dob[rsl])
                dq = dq_acc[h]
                dk = dk_acc[h]
                dc = qc * dq - kc * dk
                bnd = jnp.sum(ds_out * s_out, axis=0, keepdims=True)
                edge_row = 0 if rev else CHUNK - 1
                is_edge = lax.broadcasted_iota(jnp.int32, (CHUNK, HEAD_K), 0) == edge_row
                dq_ref[0, rws, kcols] = _b16(dq)
                dk_ref[0, rws, kcols] = _b16(dk)
                dv_ref[0, rws, vcols] = _b16(dv_acc[h])
                dc_ref[0, rws, kcols] = dc + jnp.where(is_edge, bnd, 0.0)
            return carry

        for jj in range(SCAN_CHUNKS):
            chunk(jj, 0)

    def step_of(u):
        return SCAN_STEPS - 1 - u

    def spec(w):
        return pl.BlockSpec((1, SCAN_ROWS, w), lambda b, u: (b, _scan_block(step_of(u), rev), 0))

    in_specs = [spec(GLA_DK), spec(GLA_DK), spec(D), spec(GLA_DK),
                pl.BlockSpec((1, HEADS, SCAN_CHUNKS, HEAD_V, HEAD_K), lambda b, u: (b, 0, step_of(u), 0, 0)),
                pl.BlockSpec((1, HEADS, HEAD_V, HEAD_K), lambda b, u: (b, 0, 0, 0)),
                pl.BlockSpec((1, SCAN_ROWS, D), lambda b, u: (b, _scan_lat_block(step_of(u), rev), 0))]
    out_specs = [spec(GLA_DK), spec(GLA_DK), spec(D), spec(GLA_DK)]
    out_shape = [jax.ShapeDtypeStruct((n, SEQ_ALL, GLA_DK), BF16), jax.ShapeDtypeStruct((n, SEQ_ALL, GLA_DK), BF16),
                 jax.ShapeDtypeStruct((n, SEQ_ALL, D), BF16), jax.ShapeDtypeStruct((n, SEQ_ALL, GLA_DK), F32)]
    scratch = [pltpu.VMEM((HEADS, HEAD_V, HEAD_K), F32), pltpu.VMEM((HEADS, HEAD_V, HEAD_K), F32),
               pltpu.VMEM((HEADS, CHUNK, HEAD_K), F32), pltpu.VMEM((HEADS, CHUNK, HEAD_K), F32),
               pltpu.VMEM((HEADS, CHUNK, HEAD_V), F32)]
    args = [q, k, v, cum, s_all, s_fin, do]
    if rider is not None:
        kind, arr = rider
        any_spec = pl.BlockSpec(memory_space=pl.ANY)
        in_specs.append(any_spec)
        out_specs.append(any_spec)
        args.append(arr)
        if kind == "swap":
            out_shape += _pair_got_shapes([], arr)
            n_cp = _pair_count([], arr)
        else:
            out_shape.append(jax.ShapeDtypeStruct((3,) + arr.shape[1:], arr.dtype))
            n_cp = 3 * P_ROW_CHUNKS
        scratch += [pltpu.SemaphoreType.DMA((n_cp,)), pltpu.SemaphoreType.DMA((n_cp,))]
    return _pallas(
        body, name=name, grid=(n, SCAN_STEPS), in_specs=in_specs, out_specs=out_specs, out_shape=out_shape,
        scratch_shapes=scratch,
        compiler_params=_params(("parallel" if rider is None else "arbitrary", "arbitrary")),
    )(*args)


def gla_out_fwd(o_f, o_b, r, gnorm):
    n = o_f.shape[0]
    tiles = SEQ // TM_EW

    def body(of_ref, ob_ref, r_ref, g_ref, og_ref):
        for h in range(HEADS):
            cols = pl.ds(h * HEAD_V, HEAD_V)
            o = of_ref[0, :, cols].astype(F32) + ob_ref[0, :, cols].astype(F32)
            rs = lax.rsqrt(jnp.mean(o * o, axis=-1, keepdims=True) + EPS)
            og_ref[:, cols] = (o * rs * g_ref[...] * _silu(r_ref[:, cols].astype(F32))).astype(og_ref.dtype)

    ospec = pl.BlockSpec((1, TM_EW, D), lambda b, j: (b, j, 0))
    row = pl.BlockSpec((TM_EW, D), lambda b, j: (b * tiles + j, 0))
    return _pallas(
        body, name="gla_out_fwd", grid=(n, tiles),
        in_specs=[ospec, ospec, row, pl.BlockSpec((1, HEAD_V), lambda b, j: (0, 0))],
        out_specs=row, out_shape=jax.ShapeDtypeStruct((n * SEQ, D), BF16),
        compiler_params=_params(("parallel", "parallel")),
    )(o_f, o_b, r, gnorm)


def gla_out_bwd(o_f, o_b, r, dyg, gla_proj, gnorm):
    n = o_f.shape[0]
    tiles = SEQ // TM_EW

    def body(of_ref, ob_ref, r_ref, d_ref, w_ref, g_ref, do_ref, dr_ref, dg_ref, dog_buf):
        @pl.when((pl.program_id(0) == 0) & (pl.program_id(1) == 0))
        def _():
            dg_ref[...] = jnp.zeros_like(dg_ref)

        dog_buf[...] = _nt(d_ref[...], w_ref[...])
        for h in range(HEADS):
            cols = pl.ds(h * HEAD_V, HEAD_V)
            o = of_ref[0, :, cols].astype(F32) + ob_ref[0, :, cols].astype(F32)
            rv = r_ref[:, cols].astype(F32)
            dv = dog_buf[:, cols]
            rs = lax.rsqrt(jnp.mean(o * o, axis=-1, keepdims=True) + EPS)
            oh = o * rs
            dr_ref[:, cols] = (dv * oh * g_ref[...] * _dsilu(rv)).astype(dr_ref.dtype)
            dn = dv * _silu(rv)
            dg_ref[...] += jnp.sum(dn * oh, axis=0, keepdims=True)
            doh = dn * g_ref[...]
            do_ref[0, :, cols] = _b16(rs * (doh - oh * jnp.mean(doh * oh, axis=-1, keepdims=True)))

    ospec = pl.BlockSpec((1, TM_EW, D), lambda b, j: (b, j, 0))
    row = pl.BlockSpec((TM_EW, D), lambda b, j: (b * tiles + j, 0))
    vec = pl.BlockSpec((1, HEAD_V), lambda b, j: (0, 0))
    return _pallas(
        body, name="gla_out_bwd", grid=(n, tiles),
        in_specs=[ospec, ospec, row, row, pl.BlockSpec((D, D), lambda b, j: (0, 0)), vec],
        out_specs=[ospec, row, vec],
        out_shape=[jax.ShapeDtypeStruct((n, SEQ, D), BF16), jax.ShapeDtypeStruct((n * SEQ, D), BF16),
                   jax.ShapeDtypeStruct((1, HEAD_V), F32)],
        scratch_shapes=[pltpu.VMEM((TM_EW, D), F32)],
        compiler_params=_params(("arbitrary", "arbitrary")),
    )(o_f, o_b, r, dyg, gla_proj, gnorm)


TM_OUT = 512


def merge_out_final(p5, y_conv, y_gla, w_out, x2, gate, final_g, target, n_samples):
    t = x2.shape[0]
    tiles = SEQ // TM_OUT

    def body(mc_ref, mg_ref, yc_ref, yg_ref, w_ref, x_ref, gate_ref, g_ref, t_ref,
             mrg_ref, dh_ref, dmo_ref, dgate_ref, dg_ref, loss_ref):
        b, j = pl.program_id(0), pl.program_id(1)
        f = lambda ref: ref[...].astype(F32)
        merged = _b16(_sigmoid(f(mc_ref)) * f(yc_ref) + _sigmoid(f(mg_ref)) * f(yg_ref))
        mrg_ref[...] = merged
        mo_v = _nn(merged, w_ref[...])
        h = x_ref[...] + gate_ref[0] * mo_v
        rs = lax.rsqrt(jnp.mean(h * h, axis=-1, keepdims=True) + EPS)
        nh = h * rs
        err = nh * g_ref[...] - t_ref[...]
        dy = err * (1.0 / D)
        dn = dy * g_ref[...]
        dh = rs * (dn - nh * jnp.mean(dn * nh, axis=-1, keepdims=True))
        dh_ref[...] = dh
        dmo_ref[...] = (dh * gate_ref[0]).astype(dmo_ref.dtype)

        @pl.when(j == 0)
        def _():
            dgate_ref[...] = jnp.zeros_like(dgate_ref)

        @pl.when((b == 0) & (j == 0))
        def _():
            dg_ref[...] = jnp.zeros_like(dg_ref)
            loss_ref[...] = jnp.zeros_like(loss_ref)

        dgate_ref[0] += jnp.sum(dh * mo_v, axis=0, keepdims=True)
        dg_ref[...] += jnp.sum(dy * nh, axis=0, keepdims=True)
        loss_ref[...] += (0.5 / D) * jnp.sum(err * err)

    row = pl.BlockSpec((TM_OUT, D), lambda b, j: (b * tiles + j, 0))
    per = pl.BlockSpec((1, 1, D), lambda b, j: (b, 0, 0))
    vec = pl.BlockSpec((1, D), lambda b, j: (0, 0))
    return _pallas(
        body, name="merge_out_final", grid=(n_samples, tiles),
        in_specs=[row, pl.BlockSpec((TM_OUT, D), lambda b, j: (b * tiles + j, 1)), row, row,
                  pl.BlockSpec((D, D), lambda b, j: (0, 0)), row, per, vec, row],
        out_specs=[row, row, row, per, vec, pl.BlockSpec((8, 128), lambda b, j: (0, 0))],
        out_shape=[jax.ShapeDtypeStruct((t, D), BF16), jax.ShapeDtypeStruct((t, D), F32), jax.ShapeDtypeStruct((t, D), BF16),
                   jax.ShapeDtypeStruct((n_samples, 1, D), F32), jax.ShapeDtypeStruct((1, D), F32),
                   jax.ShapeDtypeStruct((8, 128), F32)],
        compiler_params=_params(("arbitrary", "arbitrary")),
    )(p5, p5, y_conv, y_gla, w_out, x2, gate, final_g, target)


def out_dgrad_merge_bwd(p5, y_conv, y_gla, dmo, w_out):
    t = y_conv.shape[0]

    def body(mc_ref, mg_ref, yc_ref, yg_ref, d_ref, w_ref, dyc_ref, dyg_ref, dp_ref):
        f = lambda ref: ref[...].astype(F32)
        d = _nt(d_ref[...], w_ref[...])
        sc = _sigmoid(f(mc_ref))
        sg = _sigmoid(f(mg_ref))
        dyc_ref[...] = (d * sc).astype(dyc_ref.dtype)
        dyg_ref[...] = (d * sg).astype(dyg_ref.dtype)
        dp_ref[:, pl.ds(0, D)] = (d * f(yc_ref) * sc * (1.0 - sc)).astype(dp_ref.dtype)
        dp_ref[:, pl.ds(D, D)] = (d * f(yg_ref) * sg * (1.0 - sg)).astype(dp_ref.dtype)

    row = pl.BlockSpec((TM_OUT, D), lambda i: (i, 0))
    return _pallas(
        body, name="out_dgrad_merge_bwd", grid=(t // TM_OUT,),
        in_specs=[row, pl.BlockSpec((TM_OUT, D), lambda i: (i, 1)), row, row, row, pl.BlockSpec((D, D), lambda i: (0, 0))],
        out_specs=[row, row, pl.BlockSpec((TM_OUT, 2 * D), lambda i: (i, 0))],
        out_shape=[jax.ShapeDtypeStruct((t, D), BF16), jax.ShapeDtypeStruct((t, D), BF16),
                   jax.ShapeDtypeStruct((t, 2 * D), BF16)],
        compiler_params=_params(("parallel",)),
    )(p5, p5, y_conv, y_gla, dmo, w_out)


def local_step(x, ctx, target, mod, wts, small, p_sh, chip, core):
    n = x.shape[0]
    t = n * SEQ
    t_all = t + n * NCTX
    x2 = x.reshape(t, D)
    ctx2 = ctx.reshape(n * NCTX, D)
    tgt2 = target.reshape(t, D)
    scale1, shift, gate = mod

    u = norm_mod_fwd(x2, ctx2, scale1, shift, small["norm_g"])
    p1, p2, p3, p4, p5, p_all = proj_all(u, [wts["w%d" % i] for i in range(1, 6)], [small["b%d" % i] for i in range(1, 6)],
                                         [t, t, t_all, t, t], p_sh, tm=512)
    p_full = jnp.stack([jnp.where(chip == i, p_sh, p_all[i]) for i in range(N_CHIPS)])
    wts = dict(wts, conv_proj=p_full[:, 0:256].reshape(D, D), gla_proj=p_full[:, 256:512].reshape(D, D),
               w_out=p_full[:, 512:768].reshape(D, D))

    aconv = conv_fwd(p1, small["conv_w"], small["conv_b"], n)
    ac = ln_gate_fwd(aconv, p2, small["conv_ln_g"], small["conv_ln_b"])
    y_conv = matmul_nn(ac, wts["conv_proj"], name="conv_proj_fwd", m=t, tm=1024, tn=1024, out_dtype=BF16)

    qs, ks, vs, cum_f, cum_b = gla_prep_fwd(p3, small["upf"], small["upb"], small["bias_f"], small["bias_b"], n)
    o_f, s_f, sfin_f = gla_scan_fwd(qs, ks, vs, cum_f, rev=False, name="gla_scan_fwd_f")
    o_b, s_b, sfin_b = gla_scan_fwd(qs, ks, vs, cum_b, rev=True, name="gla_scan_fwd_b")
    og = gla_out_fwd(o_f, o_b, p4, small["gla_norm_g"])
    y_gla = matmul_nn(og, wts["gla_proj"], name="gla_proj_fwd", m=t, tm=1024, tn=1024, out_dtype=BF16)

    merged, dh, dmo, dgate, d_final_g, loss = merge_out_final(p5, y_conv, y_gla, wts["w_out"], x2, gate,
                                                              small["final_norm_g"], tgt2, n)

    g = {"final_norm_g": d_final_g}
    g["w_out"] = matmul_tn(merged, dmo, name="w_out_wgrad", t=t, tn=1024, tt=1024)[0]
    dyc, dyg, dp5 = out_dgrad_merge_bwd(p5, y_conv, y_gla, dmo, wts["w_out"])

    g["conv_proj"] = matmul_tn(ac, dyc, name="conv_proj_wgrad", t=t, tn=1024, tt=1024)[0]
    daconv, dp2, g["conv_ln_g"], g["conv_ln_b"] = ln_gate_bwd(aconv, p2, dyc, wts["conv_proj"], small["conv_ln_g"],
                                                               small["conv_ln_b"])
    dp1, dconv_w, dconv_b = conv_bwd(p1, daconv, small["conv_w"], n)
    g["conv_w"], g["conv_b"] = dconv_w, dconv_b

    g["gla_proj"] = matmul_tn(og, dyg, name="gla_proj_wgrad", t=t, tn=1024, tt=1024)[0]
    do, dp4, g["gla_norm_g"] = gla_out_bwd(o_f, o_b, p4, dyg, wts["gla_proj"], small["gla_norm_g"])
    g["proj"] = jnp.concatenate([g["conv_proj"].reshape(N_CHIPS, 256, D), g["gla_proj"].reshape(N_CHIPS, 256, D),
                                 g["w_out"].reshape(N_CHIPS, 256, D)], 1)
    dq_f, dk_f, dv_f, dc_f, gotp = gla_scan_bwd(qs, ks, vs, cum_f, s_f, sfin_f, do, rev=False, name="gla_scan_bwd_f",
                                                rider=("swap", g["proj"]))
    pap16 = pair_add(core, g["proj"], gotp, name="pair_add_p", tr=384)
    dq_b, dk_b, dv_b, dc_b, rbp = gla_scan_bwd(qs, ks, vs, cum_b, s_b, sfin_b, do, rev=True, name="gla_scan_bwd_b",
                                               rider=("exchange", pap16))
    dp3, g["upf"], g["upb"], g["bias_f"], g["bias_b"] = gla_prep_bwd(
        p3, dq_f, dq_b, dk_f, dk_b, dv_f, dv_b, dc_f, dc_b,
        small["upf"], small["upb"], small["bias_f"], small["bias_b"], n)

    dps = [dp1, dp2, dp3, dp4, dp5]
    got = {}
    for i in [0, 1, 3, 4, 2]:
        dp = dps[i]
        rows = dp.shape[0]
        tn = W3 if dp.shape[1] == W3 else 1024
        others = [j for j in range(5) if j != i]
        swap = [g["w%d" % (j + 1)] for j in others] if i == 2 else None
        outs = matmul_tn(u, dp, name="w_in_wgrad_%d" % (i + 1), t=rows, tn=tn, tt=1024 if rows % 1024 == 0 else 768,
                         colsum=True, swap=swap)
        g["w%d" % (i + 1)], g["b%d" % (i + 1)] = outs[0], outs[1]
        if swap is not None:
            got = dict(zip(others, outs[2:2 + len(others)]))
    g["gate"] = dgate
    return loss, dh, dps, g, got, (pap16, rbp)


def _group_cols(w):
    gv, gg, z = w[..., 0:1024], w[..., 1024:2048], w[..., 2048:3072]
    q, k, v = w[..., 3072:3584], w[..., 3584:4096], w[..., 4096:5120]
    ab = w[..., 5120:5152]
    r, mc, mg = w[..., 5152:6176], w[..., 6176:7200], w[..., 7200:8224]
    g1 = jnp.concatenate([p for j in range(CONV_NCB)
                          for p in (gv[..., CONV_CB * j:CONV_CB * (j + 1)], gg[..., CONV_CB * j:CONV_CB * (j + 1)])], -1)
    pad = jnp.zeros(w.shape[:-1] + (W3 - 2080,), w.dtype)
    g3 = jnp.concatenate([v, q, k, ab, pad], -1)
    return g1, z, g3, r, jnp.concatenate([mc, mg], -1)


def _ungroup_cols(g1, g2, g3, g4, g5):
    gv = jnp.concatenate([g1[..., 2 * CONV_CB * j:2 * CONV_CB * j + CONV_CB] for j in range(CONV_NCB)], -1)
    gg = jnp.concatenate([g1[..., 2 * CONV_CB * j + CONV_CB:2 * CONV_CB * (j + 1)] for j in range(CONV_NCB)], -1)
    v, q, k, ab = g3[..., 0:1024], g3[..., 1024:1536], g3[..., 1536:2048], g3[..., 2048:2080]
    return jnp.concatenate([gv, gg, g2, q, k, v, ab, g4, g5[..., 0:1024], g5[..., 1024:2048]], -1)


def _natural_pieces():
    pieces = [(CONV_CB * j, CONV_CB, 0, 2 * CONV_CB * j) for j in range(CONV_NCB)]
    pieces += [(1024 + CONV_CB * j, CONV_CB, 0, 2 * CONV_CB * j + CONV_CB) for j in range(CONV_NCB)]
    pieces += [(2048, 1024, 1, 0), (3072, 512, 2, O3_Q), (3584, 512, 2, O3_K), (4096, 1024, 2, O3_V), (5120, 32, 2, O3_AB),
               (5152, 1024, 3, 0), (6176, 1024, 4, 0), (7200, 1024, 4, 1024)]
    return sorted(pieces)


def _ungroup_to_shards(groups):
    shards = []
    for i in range(N_CHIPS):
        lo, hi = i * W_IN_SHARD, (i + 1) * W_IN_SHARD
        parts = []
        for nat, width, g, gcol in _natural_pieces():
            a, b = max(nat, lo), min(nat + width, hi)
            if a < b:
                parts.append(groups[g][:, gcol + a - nat:gcol + b - nat])
        shards.append(jnp.concatenate(parts, 1))
    return jnp.stack(shards)


def _pad_up(up, row0):
    return jnp.zeros((128, GLA_DK), F32).at[row0:row0 + up.shape[0]].set(up)


def _adamw_math(w, g, m, v):
    m = ADAM_B1 * m + (1.0 - ADAM_B1) * g
    v = ADAM_B2 * v + (1.0 - ADAM_B2) * (g * g)
    m_hat = m / (1.0 - ADAM_B1 ** ADAM_STEP)
    v_hat = v / (1.0 - ADAM_B2 ** ADAM_STEP)
    delta = -ADAM_LR * (m_hat / (jnp.sqrt(v_hat) + ADAM_EPS) + ADAM_WD * w)
    return delta, m, v


def adamw2d(w, g, m, v, *, name, tr, tcols=None):
    rows, cols = w.shape[-2:]

    def body(w_ref, g_ref, m_ref, v_ref, d_ref, nm_ref, nv_ref):
        d_ref[...], nm_ref[...], nv_ref[...] = _adamw_math(w_ref[...], g_ref[...], m_ref[...], v_ref[...])

    tcols = cols if tcols is None else tcols
    if w.ndim == 3:
        spec = pl.BlockSpec((1, tr, tcols), lambda i, j: (0, i, j))
    else:
        spec = pl.BlockSpec((tr, tcols), lambda i, j: (i, j))
    return _pallas(
        body, name=name, grid=(rows // tr, cols // tcols), in_specs=[spec] * 4, out_specs=[spec] * 3,
        out_shape=[jax.ShapeDtypeStruct(w.shape, F32)] * 3, compiler_params=_params(("parallel", "parallel")),
    )(w, g, m, v)


def adamw_many(ws, gs, ms, vs):
    k = len(ws)
    two = lambda a: a.reshape((-1, a.shape[-1]))

    def body(*refs):
        w_refs, g_refs, m_refs, v_refs = refs[:k], refs[k:2 * k], refs[2 * k:3 * k], refs[3 * k:4 * k]
        d_refs, nm_refs, nv_refs = refs[4 * k:5 * k], refs[5 * k:6 * k], refs[6 * k:7 * k]
        for i in range(k):
            d_refs[i][...], nm_refs[i][...], nv_refs[i][...] = _adamw_math(
                w_refs[i][...], g_refs[i][...], m_refs[i][...], v_refs[i][...])

    shapes = [jax.ShapeDtypeStruct(two(a).shape, F32) for a in ws]
    outs = _pallas(body, name="adamw_small", out_shape=shapes * 3, compiler_params=_params())(
        *[two(a) for a in ws], *[two(a) for a in gs], *[two(a) for a in ms], *[two(a) for a in vs])
    back = lambda lst: [o.reshape(a.shape) for o, a in zip(lst, ws)]
    return back(outs[:k]), back(outs[k:2 * k]), back(outs[2 * k:])


def sum_devices(sall, *, name):
    rows = sall.shape[1]

    def body(s_ref, o_ref):
        acc = s_ref[0]
        for d in range(1, N_DEV):
            acc = acc + s_ref[d]
        o_ref[...] = acc

    return _pallas(body, name=name, out_shape=jax.ShapeDtypeStruct((rows, D), F32),
                   compiler_params=_params())(sall)


def pair_add(core, g, got, *, name, tr):
    n, rows, cols = got.shape
    g4 = g.reshape(n, 2, rows, cols)

    def body(core_ref, g_ref, got_ref, ob_ref):
        del core_ref
        ob_ref[0] = (g_ref[0, 0] + got_ref[0]).astype(BF16)

    spec = pl.BlockSpec((1, tr, cols), lambda i, t, core_ref: (i, t, 0))
    return _pallas(
        body, name=name,
        grid_spec=pltpu.PrefetchScalarGridSpec(
            num_scalar_prefetch=1, grid=(n, rows // tr),
            in_specs=[pl.BlockSpec((1, 1, tr, cols), lambda i, t, core_ref: (i, core_ref[0], t, 0)), spec],
            out_specs=spec),
        out_shape=jax.ShapeDtypeStruct(got.shape, BF16),
        compiler_params=_params(("parallel", "parallel")))(core, g4, got)


def pair_add_groups(core, gs, gots, *, tr):
    k = len(gs)
    rows = gots[0].shape[0]

    def body(core_ref, *refs):
        del core_ref
        for i in range(k):
            refs[2 * k + i][...] = (refs[i][0] + refs[k + i][...]).astype(BF16)

    g_specs = [pl.BlockSpec((1, tr, a.shape[1]), lambda t, core_ref: (core_ref[0], t, 0)) for a in gots]
    r_specs = [pl.BlockSpec((tr, a.shape[1]), lambda t, core_ref: (t, 0)) for a in gots]
    return _pallas(
        body, name="pair_add_w",
        grid_spec=pltpu.PrefetchScalarGridSpec(num_scalar_prefetch=1, grid=(rows // tr,),
                                               in_specs=g_specs + r_specs, out_specs=r_specs),
        out_shape=[jax.ShapeDtypeStruct(a.shape, BF16) for a in gots],
        compiler_params=_params(("parallel",)))(core, *[a.reshape(2, rows, a.shape[1]) for a in gs], *gots)


def chip_add(place, pa, rb, *, name, tr):
    _, rows, cols = pa.shape

    def body(place_ref, m_ref, r_ref, o_ref):
        del place_ref
        o_ref[0] = ((m_ref[0].astype(F32) + r_ref[0].astype(F32)) + r_ref[1].astype(F32)) + r_ref[2].astype(F32)

    return _pallas(
        body, name=name,
        grid_spec=pltpu.PrefetchScalarGridSpec(
            num_scalar_prefetch=1, grid=(rows // tr,),
            in_specs=[pl.BlockSpec((1, tr, cols), lambda t, place_ref: (place_ref[0], t, 0)),
                      pl.BlockSpec((3, tr, cols), lambda t, place_ref: (0, t, 0))],
            out_specs=pl.BlockSpec((1, tr, cols), lambda t, place_ref: (place_ref[1], t, 0))),
        out_shape=jax.ShapeDtypeStruct((2, rows, cols), F32),
        compiler_params=_params(("parallel",)))(place, pa, rb)


def ada_bwd(call, cctx_rows, dm_shard, dm_full, adaw):
    nsh = adaw.shape[1]

    def body(c_ref, cc_ref, dms_ref, dmf_ref, w_ref, gw_ref, gb_ref, pq_ref):
        a_lat = _silu(c_ref[...])
        a_ctx = _silu(cc_ref[...])
        dms = dms_ref[...]
        gw_ref[...] = _tn(a_lat, dms[0:64], HI) + _tn(a_ctx, dms[64:72], HI)
        gb_ref[...] = jnp.sum(dmf_ref[...], axis=0, keepdims=True)
        part = _nt(dms[64:72], w_ref[...], HI)
        pq_ref[...] = jnp.zeros_like(pq_ref) + jnp.sum(part, axis=0, keepdims=True)

    return _pallas(body, name="ada_bwd",
                   out_shape=[jax.ShapeDtypeStruct((D, nsh), F32), jax.ShapeDtypeStruct((1, 3 * D), F32),
                              jax.ShapeDtypeStruct((8, D), F32)],
                   compiler_params=_params())(call, cctx_rows, dm_shard, dm_full, adaw)


def cctx_grad(pq_all, cctx_rows):
    def body(p_ref, c_ref, o_ref):
        acc = p_ref[0]
        for qi in range(1, N_CHIPS):
            acc = acc + p_ref[qi]
        o_ref[...] = acc * _dsilu(c_ref[...])

    return _pallas(body, name="cctx_grad", out_shape=jax.ShapeDtypeStruct((8, D), F32),
                   compiler_params=_params())(pq_all, cctx_rows)


def _place():
    x, y, c = lax.axis_index("x"), lax.axis_index("y"), lax.axis_index("c")
    chips = [(1 - x, y), (x, 1 - y), (1 - x, 1 - y)]
    return x, y, c, chips


def _all_peers(x, y, c):
    return [((1 - x) if r & 4 else x, (1 - y) if r & 2 else y, (1 - c) if r & 1 else c) for r in range(1, N_DEV)]


def _remote(src, dst, send_sem, recv_sem, dev):
    return pltpu.make_async_remote_copy(src_ref=src, dst_ref=dst, send_sem=send_sem, recv_sem=recv_sem,
                                        device_id=dev, device_id_type=MESH)


ANY = pl.BlockSpec(memory_space=pl.ANY)
VMEM = pl.BlockSpec(memory_space=pltpu.VMEM)
F_ROWS = 16


W_ROW_CHUNKS = 4
P_ROW_CHUNKS = 2
N_BULK = W_ROW_CHUNKS + P_ROW_CHUNKS


def _half_chunks(core, n_rows, align, which=(0, 1)):
    out = []
    for a, k in ((0, W_ROW_CHUNKS), (1, P_ROW_CHUNKS)):
        if a not in which:
            continue
        half = n_rows[a] // 2
        size = half // k
        for i in range(k):
            start = core * half + i * size
            out.append((a, pl.ds(start if isinstance(start, int) else pl.multiple_of(start, align), size)))
    return out


def gather_weights(c8, cctx8, adaw, adab, w_sh, fp):
    nsh = adaw.shape[1]

    def body(c_ref, cctx_ref, adaw_ref, adab_ref, w_ref, fp_ref, wall_ref, fall_ref, call_ref, mall_ref,
             abuf, w_send, w_recv, h_send, h_recv, c_send, c_recv, m_send, m_recv, f_send, f_recv):
        x, y, c, chips = _place()
        q = 2 * x + y
        dev = 4 * x + 2 * y + c
        qs = [2 * cx + cy for cx, cy in chips]
        sib = (x, y, 1 - c)
        srcs, dsts = (w_ref,), (wall_ref,)
        n_rows = (w_ref.shape[0],)
        mine = _half_chunks(c, n_rows, 16, which=(0,))
        other = _half_chunks(1 - c, n_rows, 16, which=(0,))

        bulk = [[_remote(srcs[a].at[rows], dsts[a].at[q, rows], w_send.at[j * N_BULK + i], w_recv.at[j * N_BULK + i],
                         (*chips[j], c)) for i, (a, rows) in enumerate(mine)] for j in range(3)]
        fall_ref[q] = fp_ref[...]
        small = [_remote(fp_ref, fall_ref.at[q], f_send.at[j], f_recv.at[j], (*chips[j], c)) for j in range(3)]
        my_rows = pl.ds(pl.multiple_of(8 * dev, 8), 8)
        call_ref[my_rows, :] = c_ref[...]
        cond = [_remote(c_ref, call_ref.at[my_rows, :], c_send.at[r], c_recv.at[r], peer)
                for r, peer in enumerate(_all_peers(x, y, c))]
        for cp in sum(bulk, []) + small + cond:
            cp.start()
        for cp in cond:
            cp.wait_recv()

        abuf[pl.ds(0, 64), :] = _silu(call_ref[...])
        abuf[pl.ds(64, 8), :] = _silu(cctx_ref[...])
        mall_ref[q] = _nn(abuf[...], adaw_ref[...], HI) + adab_ref[...]
        mod = [_remote(mall_ref.at[q], mall_ref.at[q], m_send.at[j], m_recv.at[j], (*chips[j], c)) for j in range(3)]
        for cp in mod:
            cp.start()

        handed = []
        for j in range(3):
            for i, (a, rows) in enumerate(mine):
                bulk[j][i].wait_recv()
                cp = _remote(dsts[a].at[qs[j], rows], dsts[a].at[qs[j], rows],
                             h_send.at[j * N_BULK + i], h_recv.at[j * N_BULK + i], sib)
                cp.start()
                handed.append(cp)
        for j in range(3):
            for i, (a, rows) in enumerate(other):
                _remote(dsts[a].at[qs[j], rows], dsts[a].at[qs[j], rows],
                        h_send.at[j * N_BULK + i], h_recv.at[j * N_BULK + i], sib).wait_recv()
        for cp in mod + small:
            cp.wait_recv()
        for cp in sum(bulk, []) + small + cond + mod + handed:
            cp.wait_send()

    def dma(n):
        return pltpu.SemaphoreType.DMA((n,))

    return _pallas(
        body, name="gather_weights",
        in_specs=[VMEM, VMEM, VMEM, VMEM, ANY, VMEM],
        out_specs=[ANY, VMEM, VMEM, VMEM],
        out_shape=[jax.ShapeDtypeStruct((N_CHIPS,) + w_sh.shape, BF16),
                   jax.ShapeDtypeStruct((N_CHIPS, F_ROWS, D), F32),
                   jax.ShapeDtypeStruct((8 * N_DEV, D), F32), jax.ShapeDtypeStruct((N_CHIPS, MOD_ROWS, nsh), F32)],
        scratch_shapes=[pltpu.VMEM((MOD_ROWS, D), F32), dma(3 * N_BULK), dma(3 * N_BULK), dma(3 * N_BULK), dma(3 * N_BULK),
                        dma(7), dma(7), dma(3), dma(3), dma(3), dma(3)],
        compiler_params=_params(),
    )(c8, cctx8, adaw, adab, w_sh, fp)


def _pair_count(gs, gp):
    return len(gs) * W_ROW_CHUNKS + (0 if gp is None else N_CHIPS * P_ROW_CHUNKS)


def _pair_got_shapes(gs, gp):
    shapes = [jax.ShapeDtypeStruct((D // 2, a.shape[1]), F32) for a in gs]
    if gp is not None:
        shapes.append(jax.ShapeDtypeStruct((N_CHIPS, gp.shape[1] // 2, gp.shape[2]), F32))
    return shapes


def _pair_copies(g_refs, gp_ref, got_refs, gotp_ref, a_send, a_recv):
    x, y, c, _ = _place()
    sib = (x, y, 1 - c)
    pair = []
    half, size = D // 2, D // 2 // W_ROW_CHUNKS
    for gi in range(len(g_refs)):
        for i in range(W_ROW_CHUNKS):
            k = len(pair)
            rows_o = pl.ds(pl.multiple_of((1 - c) * half + i * size, 8), size)
            pair.append(_remote(g_refs[gi].at[rows_o], got_refs[gi].at[pl.ds(i * size, size)],
                                a_send.at[k], a_recv.at[k], sib))
    if gp_ref is not None:
        half, size = gp_ref.shape[1] // 2, gp_ref.shape[1] // 2 // P_ROW_CHUNKS
        for s in range(N_CHIPS):
            for i in range(P_ROW_CHUNKS):
                k = len(pair)
                rows_o = pl.ds(pl.multiple_of((1 - c) * half + i * size, 8), size)
                pair.append(_remote(gp_ref.at[s, rows_o], gotp_ref.at[s, pl.ds(i * size, size)],
                                    a_send.at[k], a_recv.at[k], sib))
    return pair


def pair_swap(gs, sm):
    n_gs = len(gs)

    def body(*refs):
        g_refs, sm_ref = refs[:n_gs], refs[n_gs]
        got_refs, sall_ref = refs[n_gs + 1:2 * n_gs + 1], refs[2 * n_gs + 1]
        a_send, a_recv, s_send, s_recv = refs[2 * n_gs + 2:]
        x, y, c, _ = _place()
        dev = 4 * x + 2 * y + c
        pair = _pair_copies(g_refs, None, got_refs, None, a_send, a_recv)
        sall_ref[dev] = sm_ref[...]
        small = [_remote(sm_ref, sall_ref.at[dev], s_send.at[r], s_recv.at[r], peer)
                 for r, peer in enumerate(_all_peers(x, y, c))]
        for cp in pair + small:
            cp.start()
        for cp in small + pair:
            cp.wait_recv()
        for cp in small + pair:
            cp.wait_send()

    return _pallas(
        body, name="pair_swap", in_specs=[ANY] * n_gs + [VMEM], out_specs=[ANY] * n_gs + [VMEM],
        out_shape=_pair_got_shapes(gs, None) + [jax.ShapeDtypeStruct((N_DEV,) + sm.shape, F32)],
        scratch_shapes=[pltpu.SemaphoreType.DMA((_pair_count(gs, None),)), pltpu.SemaphoreType.DMA((_pair_count(gs, None),)),
                        pltpu.SemaphoreType.DMA((N_DEV - 1,)), pltpu.SemaphoreType.DMA((N_DEV - 1,))],
        compiler_params=_params(),
    )(*gs, sm)


def gather_small(sm):
    rows = sm.shape[0]

    def body(sm_ref, sall_ref, s_send, s_recv):
        x, y, c, _ = _place()
        dev = 4 * x + 2 * y + c
        sall_ref[dev] = sm_ref[...]
        small = [_remote(sm_ref, sall_ref.at[dev], s_send.at[r], s_recv.at[r], peer)
                 for r, peer in enumerate(_all_peers(x, y, c))]
        for cp in small:
            cp.start()
        for cp in small:
            cp.wait_recv()
        for cp in small:
            cp.wait_send()

    return _pallas(
        body, name="gather_small", in_specs=[VMEM], out_specs=VMEM,
        out_shape=jax.ShapeDtypeStruct((N_DEV, rows, D), F32),
        scratch_shapes=[pltpu.SemaphoreType.DMA((7,)), pltpu.SemaphoreType.DMA((7,))],
        compiler_params=_params(),
    )(sm)


def pair_share(ghw, ghp, pq):
    def body(ghw_ref, ghp_ref, pq_ref, outw_ref, outp_ref, pqa_ref, send, recv, p_send, p_recv):
        del ghw_ref, ghp_ref
        x, y, c, chips = _place()
        q = 2 * x + y
        refs = (outw_ref, outp_ref)
        n_rows = (2 * outw_ref.shape[1], 2 * outp_ref.shape[1])
        pair = [_remote(refs[a].at[c, rows], refs[a].at[c, rows], send.at[i], recv.at[i], (x, y, 1 - c))
                for i, (a, rows) in enumerate(_half_chunks(0, n_rows, 8))]
        pqa_ref[q] = pq_ref[...]
        small = [_remote(pq_ref, pqa_ref.at[q], p_send.at[j], p_recv.at[j], (*chips[j], c)) for j in range(3)]
        for cp in pair + small:
            cp.start()
        for i, (a, rows) in enumerate(_half_chunks(0, n_rows, 8)):
            _remote(refs[a].at[1 - c, rows], refs[a].at[1 - c, rows], send.at[i], recv.at[i], (x, y, 1 - c)).wait_recv()
        for cp in small:
            cp.wait_recv()
        for cp in pair + small:
            cp.wait_send()

    return _pallas(
        body, name="pair_share", in_specs=[ANY, ANY, VMEM], out_specs=[ANY, ANY, VMEM],
        out_shape=[jax.ShapeDtypeStruct(ghw.shape, F32), jax.ShapeDtypeStruct(ghp.shape, F32),
                   jax.ShapeDtypeStruct((N_CHIPS, 8, D), F32)],
        scratch_shapes=[pltpu.SemaphoreType.DMA((N_BULK,)), pltpu.SemaphoreType.DMA((N_BULK,)),
                        pltpu.SemaphoreType.DMA((3,)), pltpu.SemaphoreType.DMA((3,))],
        input_output_aliases={0: 0, 1: 1},
        compiler_params=_params(),
    )(ghw, ghp, pq)


def _rows_of(shape):
    size = 1
    for s in shape:
        size *= s
    return -(-size // D)


def _pack(arrs, rows_multiple=8):
    parts = []
    total = 0
    for a in arrs:
        f = a.reshape(-1).astype(F32)
        r = _rows_of(a.shape)
        parts.append(jnp.pad(f, (0, r * D - f.shape[0])))
        total += r
    pad_rows = (-total) % rows_multiple
    if pad_rows:
        parts.append(jnp.zeros((pad_rows * D,), F32))
    return jnp.concatenate(parts).reshape(-1, D)


def _unpack(p, shapes):
    out = []
    r0 = 0
    for shp in shapes:
        r = _rows_of(shp)
        size = 1
        for s in shp:
            size *= s
        out.append(p[r0:r0 + r].reshape(-1)[:size].reshape(shp))
        r0 += r
    return out


WEIGHT_NAMES = ['c_ctx', 'ada_w', 'ada_b', 'norm_g', 'w_in', 'b_in', 'conv_w', 'conv_b', 'conv_ln_g', 'conv_ln_b',
                'conv_proj', 'decay_up_fwd', 'decay_bias_fwd', 'decay_up_bwd', 'decay_bias_bwd', 'gla_norm_g', 'gla_proj',
                'w_out', 'final_norm_g']
SMALL_NAMES = ['c_ctx', 'ada_b', 'norm_g', 'b_in', 'conv_w', 'conv_b', 'conv_ln_g', 'conv_ln_b', 'decay_up_fwd',
               'decay_bias_fwd', 'decay_up_bwd', 'decay_bias_bwd', 'gla_norm_g', 'final_norm_g']


def kernel(x, c, ctx, c_ctx, ada_w, ada_b, norm_g, w_in, b_in, conv_w, conv_b, conv_ln_g, conv_ln_b, conv_proj, decay_up_fwd, decay_bias_fwd, decay_up_bwd, decay_bias_bwd, gla_norm_g, gla_proj, w_out, final_norm_g, loss_target, m_c_ctx, m_ada_w, m_ada_b, m_norm_g, m_w_in, m_b_in, m_conv_w, m_conv_b, m_conv_ln_g, m_conv_ln_b, m_conv_proj, m_decay_up_fwd, m_decay_bias_fwd, m_decay_up_bwd, m_decay_bias_bwd, m_gla_norm_g, m_gla_proj, m_w_out, m_final_norm_g, v_c_ctx, v_ada_w, v_ada_b, v_norm_g, v_w_in, v_b_in, v_conv_w, v_conv_b, v_conv_ln_g, v_conv_ln_b, v_conv_proj, v_decay_up_fwd, v_decay_bias_fwd, v_decay_up_bwd, v_decay_bias_bwd, v_gla_norm_g, v_gla_proj, v_w_out, v_final_norm_g):
    w = dict(c_ctx=c_ctx, ada_w=ada_w, ada_b=ada_b, norm_g=norm_g, w_in=w_in, b_in=b_in, conv_w=conv_w, conv_b=conv_b,
             conv_ln_g=conv_ln_g, conv_ln_b=conv_ln_b, conv_proj=conv_proj, decay_up_fwd=decay_up_fwd,
             decay_bias_fwd=decay_bias_fwd, decay_up_bwd=decay_up_bwd, decay_bias_bwd=decay_bias_bwd,
             gla_norm_g=gla_norm_g, gla_proj=gla_proj, w_out=w_out, final_norm_g=final_norm_g)
    m = dict(c_ctx=m_c_ctx, ada_w=m_ada_w, ada_b=m_ada_b, norm_g=m_norm_g, w_in=m_w_in, b_in=m_b_in, conv_w=m_conv_w,
             conv_b=m_conv_b, conv_ln_g=m_conv_ln_g, conv_ln_b=m_conv_ln_b, conv_proj=m_conv_proj,
             decay_up_fwd=m_decay_up_fwd, decay_bias_fwd=m_decay_bias_fwd, decay_up_bwd=m_decay_up_bwd,
             decay_bias_bwd=m_decay_bias_bwd, gla_norm_g=m_gla_norm_g, gla_proj=m_gla_proj, w_out=m_w_out,
             final_norm_g=m_final_norm_g)
    v = dict(c_ctx=v_c_ctx, ada_w=v_ada_w, ada_b=v_ada_b, norm_g=v_norm_g, w_in=v_w_in, b_in=v_b_in, conv_w=v_conv_w,
             conv_b=v_conv_b, conv_ln_g=v_conv_ln_g, conv_ln_b=v_conv_ln_b, conv_proj=v_conv_proj,
             decay_up_fwd=v_decay_up_fwd, decay_bias_fwd=v_decay_bias_fwd, decay_up_bwd=v_decay_up_bwd,
             decay_bias_bwd=v_decay_bias_bwd, gla_norm_g=v_gla_norm_g, gla_proj=v_gla_proj, w_out=v_w_out,
             final_norm_g=v_final_norm_g)
    n = x.shape[0]
    ax, ay, ac = lax.axis_index("x"), lax.axis_index("y"), lax.axis_index("c")
    q = 2 * ax + ay
    dev = 4 * ax + 2 * ay + ac
    nsh = ada_w.shape[2]

    w_sh = w_in[0].astype(BF16)
    p_sh = jnp.concatenate([conv_proj[0], gla_proj[0], w_out[0]], 0).astype(BF16)
    fp = _pack([conv_w[0], decay_up_fwd[0], decay_up_bwd[0]], F_ROWS)
    c8 = jnp.pad(c, ((0, 8 - n), (0, 0)))
    cctx8 = jnp.pad(c_ctx[None], ((0, 7), (0, 0)))
    adab_sh = lax.dynamic_slice(ada_b, (0, q * nsh), (1, nsh))
    w_all, fall, call, mall = gather_weights(c8, cctx8, ada_w[0], adab_sh, w_sh, fp)

    mod_all = jnp.transpose(mall, (1, 0, 2)).reshape(MOD_ROWS, 3 * D)
    mod_mine = lax.dynamic_slice(mod_all, (8 * dev, 0), (n, 3 * D))
    mod_ctx = mod_all[64:65]
    shift = jnp.concatenate([mod_mine[:, 0:D], mod_ctx[:, 0:D]], 0)[:, None, :]
    scale1 = 1.0 + jnp.concatenate([mod_mine[:, D:2 * D], mod_ctx[:, D:2 * D]], 0)[:, None, :]
    gate = mod_mine[:, 2 * D:3 * D][:, None, :]

    own = lambda i, mine, got: jnp.where(q == i, mine, got)
    g1, g2, g3, g4, g5 = _group_cols(jnp.concatenate([own(i, w_sh, w_all[i]) for i in range(N_CHIPS)], 1))
    wts = dict(w1=g1, w2=g2, w3=g3, w4=g4, w5=g5)
    f_parts = [_unpack(fall[i], [conv_w.shape[1:], decay_up_fwd.shape[1:], decay_up_bwd.shape[1:]]) for i in range(N_CHIPS)]
    conv_w_full = jnp.concatenate([p[0] for p in f_parts], 1)
    upf_full = jnp.concatenate([p[1] for p in f_parts], 1)
    upb_full = jnp.concatenate([p[2] for p in f_parts], 1)
    b1, b2, b3, b4, b5 = _group_cols(b_in)
    small = dict(b1=b1, b2=b2, b3=b3, b4=b4, b5=b5, norm_g=norm_g,
                 conv_w=jnp.pad(conv_w_full, ((0, 1), (0, 0))), conv_b=conv_b, conv_ln_g=conv_ln_g, conv_ln_b=conv_ln_b,
                 upf=_split3(_pad_up(upf_full, 0)), upb=_split3(_pad_up(upb_full, 16)),
                 bias_f=decay_bias_fwd, bias_b=decay_bias_bwd,
                 gla_norm_g=gla_norm_g, final_norm_g=final_norm_g[None])

    core = ac.astype(jnp.int32).reshape(1)
    chip = q.astype(jnp.int32).reshape(1)
    loss_part, dh, dps, g, got, (pap16, rbp) = local_step(x, ctx, loss_target, (scale1, shift, gate), wts, small,
                                                          p_sh, q, core)
    loss = lax.psum(loss_part[0, 0], ("x", "y", "c"))

    gs = [g["w%d" % i] for i in range(1, 6)]
    d_b_in = _ungroup_cols(*[g["b%d" % i] for i in range(1, 6)])
    early = [d_b_in, g["conv_b"].sum(0), g["conv_ln_g"], g["conv_ln_b"], g["bias_f"], g["bias_b"],
             g["gla_norm_g"], g["final_norm_g"], g["conv_w"].sum(0)[:CONV_K], g["upf"][0:16], g["upb"][16:32]]
    early_shapes = [a.shape for a in early]
    got[2], sall1 = pair_swap([gs[2]], _pack(early))
    halves = pair_add_groups(core, gs, [got[i] for i in range(5)], tr=128)
    paw16 = _ungroup_to_shards(halves)
    grad_x2, dshift, dscale, g["norm_g"], rbw = dgrad_norm_bwd(
        dps, [wts["w%d" % i] for i in range(1, 6)], paw16, x.reshape(n * SEQ, D), ctx.reshape(n * NCTX, D), dh,
        scale1, norm_g, tm=256)

    dm_mine = jnp.concatenate([dshift[:n, 0], dscale[:n, 0], g["gate"][:, 0]], -1)
    dm_ctx = jnp.concatenate([dshift[n, 0], dscale[n, 0], jnp.zeros((D,), F32)], -1)
    late = [g["norm_g"], dm_mine, dm_ctx]
    late_shapes = [a.shape for a in late]
    sall2 = gather_small(_pack(late))
    (s_b_in, s_conv_b, s_ln_g, s_ln_b, s_bias_f, s_bias_b, s_gla_g, s_final_g, s_conv_w, s_upf,
     s_upb) = _unpack(sum_devices(sall1, name="sum_devices_early"), early_shapes)
    s_norm_g = _unpack(sum_devices(sall2, name="sum_devices_late"), late_shapes)[0]
    r_mine, r_ctx = 1, 1 + 3 * n
    dm_all = sall2[:, r_mine:r_ctx].reshape(N_DEV, n, 3 * D)
    dm_full = jnp.concatenate([jnp.pad(dm_all, ((0, 0), (0, 8 - n), (0, 0))).reshape(8 * N_DEV, 3 * D),
                               sall2[:, r_ctx:r_ctx + 3].reshape(N_DEV, 3 * D)], 0)
    dm_shard = lax.dynamic_slice(dm_full, (0, q * nsh), (MOD_ROWS, nsh))
    cctx_rows = jnp.broadcast_to(c_ctx[None], (8, D))
    g_ada_w, g_ada_b, pq = ada_bwd(call, cctx_rows, dm_shard, dm_full, ada_w[0])

    place = jnp.concatenate([chip, core])
    ghw = chip_add(place, paw16, rbw, name="chip_add_w", tr=128)
    ghp = chip_add(place, pap16, rbp, name="chip_add_p", tr=384)
    gw_mine, gp_mine, pq_all = pair_share(ghw, ghp, pq)
    gp_mine = gp_mine.reshape(768, D)
    g_c_ctx = cctx_grad(pq_all, cctx_rows)[0]

    grads = dict(
        c_ctx=g_c_ctx, ada_w=g_ada_w[None], ada_b=g_ada_b, norm_g=s_norm_g,
        w_in=gw_mine.reshape(1, D, W_IN_SHARD), b_in=s_b_in,
        conv_w=lax.dynamic_slice(s_conv_w, (0, q * 256), (CONV_K, 256))[None], conv_b=s_conv_b,
        conv_ln_g=s_ln_g, conv_ln_b=s_ln_b, conv_proj=gp_mine[0:256][None],
        decay_up_fwd=lax.dynamic_slice(s_upf, (0, q * 128), (16, 128))[None], decay_bias_fwd=s_bias_f,
        decay_up_bwd=lax.dynamic_slice(s_upb, (0, q * 128), (16, 128))[None], decay_bias_bwd=s_bias_b,
        gla_norm_g=s_gla_g, gla_proj=gp_mine[256:512][None], w_out=gp_mine[512:768][None],
        final_norm_g=s_final_g[0])

    delta, new_m, new_v = {}, {}, {}
    for name in ["ada_w", "conv_proj", "gla_proj", "w_out"]:
        delta[name], new_m[name], new_v[name] = adamw2d(w[name], grads[name].reshape(w[name].shape), m[name], v[name],
                                                        name="adamw_" + name, tr=128)
    tr_ = lambda a: jnp.swapaxes(a, 1, 2)
    g_w_in_t = tr_(grads["w_in"])
    grads["w_in"] = tr_(g_w_in_t)
    d_, m_, v_ = adamw2d(tr_(w_in), g_w_in_t, tr_(m_w_in), tr_(v_w_in), name="adamw_w_in", tr=W_IN_SHARD, tcols=128)
    delta["w_in"], new_m["w_in"], new_v["w_in"] = tr_(d_), tr_(m_), tr_(v_)
    d_, m_, v_ = adamw_many([w[nm] for nm in SMALL_NAMES], [grads[nm].reshape(w[nm].shape) for nm in SMALL_NAMES],
                            [m[nm] for nm in SMALL_NAMES], [v[nm] for nm in SMALL_NAMES])
    for nm, a, b, cc in zip(SMALL_NAMES, d_, m_, v_):
        delta[nm], new_m[nm], new_v[nm] = a, b, cc

    grad_x = grad_x2.reshape(x.shape)
    return (loss, grad_x, *[grads[nm].reshape(w[nm].shape) for nm in WEIGHT_NAMES], *[delta[nm] for nm in WEIGHT_NAMES],
            *[new_m[nm] for nm in WEIGHT_NAMES], *[new_v[nm] for nm in WEIGHT_NAMES])
```

```python
import jax
import jax.numpy as jnp
from jax import lax
from jax.experimental import pallas as pl
from jax.experimental.pallas import tpu as pltpu

F32 = jnp.float32
BF16 = jnp.bfloat16
MESH = pl.DeviceIdType.MESH
HI = lax.Precision.HIGHEST

D = 1024
SEQ = 2048
GRID_W = 64
GRID_H = SEQ // GRID_W
NCTX = 256
SEQ_ALL = SEQ + NCTX
EPS = 1e-6
CONV_K = 31
CONV_PAD = CONV_K // 2
HEADS = 4
HEAD_K = 128
HEAD_V = 256
GLA_DK = HEADS * HEAD_K
GATE_TAU = 16.0
Q_SCALE = HEAD_K ** -0.5
CHUNK = 64
NCHUNK = SEQ_ALL // CHUNK
NCHUNK_LAT = SEQ // CHUNK
NCHUNK_CTX = NCHUNK - NCHUNK_LAT
SUB = 64
NSUB = CHUNK // SUB
N_IN = 8224
W3 = 2176
O3_V, O3_Q, O3_K, O3_AB = 0, 1024, 1536, 2048

ADAM_LR, ADAM_B1, ADAM_B2, ADAM_EPS, ADAM_WD, ADAM_STEP = 0.001, 0.9, 0.999, 1e-08, 0.01, 10
VMEM_LIMIT = 56 * 1024 * 1024

N_CHIPS = 4
N_DEV = 8
W_IN_SHARD = N_IN // N_CHIPS
MOD_ROWS = 72


def _pallas(body, **kw):
    return pl.pallas_call(body, **kw)


def _params(sem=None, **kw):
    if sem is not None:
        kw["dimension_semantics"] = sem
    return pltpu.CompilerParams(vmem_limit_bytes=VMEM_LIMIT, **kw)


def _sigmoid(v):
    return 1.0 / (1.0 + jnp.exp(-v))


def _silu(v):
    return v * _sigmoid(v)


def _dsilu(v):
    s = _sigmoid(v)
    return s * (1.0 + v * (1.0 - s))


def _log_sigmoid(v):
    return jnp.minimum(v, 0.0) - jnp.log(1.0 + jnp.exp(-jnp.abs(v)))


def _dot(a, b, dims, precision=None):
    return lax.dot_general(a, b, (dims, ((), ())), preferred_element_type=F32, precision=precision)


def _nn(a, b, precision=None):
    return _dot(a, b, ((1,), (0,)), precision)


def _nt(a, b, precision=None):
    return _dot(a, b, ((1,), (1,)), precision)


def _tn(a, b, precision=None):
    return _dot(a, b, ((0,), (0,)), precision)


def _b16(v):
    return v.astype(BF16)


def proj_all(u, ws, bs, rows, p_sh, *, tm):
    k = u.shape[1]
    n_g = len(ws)
    tns = [w.shape[1] if w.shape[1] % 1024 else 1024 for w in ws]
    mts = [r // tm for r in rows]
    cnts = [(w.shape[1] // tn) * mt for w, tn, mt in zip(ws, tns, mts)]
    los = [sum(cnts[:g]) for g in range(n_g)]
    n_steps = sum(cnts)

    def rel(s, g):
        return jnp.clip(s - los[g], 0, cnts[g] - 1)

    def active(s, g):
        return (s >= los[g]) & (s < los[g] + cnts[g])

    def u_row(s):
        r = 0
        for g in range(n_g):
            r = r + jnp.where(active(s, g), rel(s, g) % mts[g], 0)
        return r

    def body(*refs):
        u_ref = refs[0]
        w_refs, b_refs = refs[1:1 + n_g], refs[1 + n_g:1 + 2 * n_g]
        p_ref = refs[1 + 2 * n_g]
        o_refs = refs[2 + 2 * n_g:2 + 3 * n_g]
        pall_ref = refs[2 + 3 * n_g]
        w_send, w_recv, h_send, h_recv = refs[3 + 3 * n_g:]
        s = pl.program_id(0)
        for g in range(n_g):
            @pl.when(active(s, g))
            def _(g=g):
                o_refs[g][...] = (_nn(u_ref[...], w_refs[g][...]) + b_refs[g][...]).astype(o_refs[g].dtype)

        x, y, c, chips = _place()
        q = 2 * x + y
        mine = _half_chunks(c, (0, p_ref.shape[0]), 16, which=(1,))
        other = _half_chunks(1 - c, (0, p_ref.shape[0]), 16, which=(1,))
        nb = len(mine)

        def bulk():
            return [[_remote(p_ref.at[rws], pall_ref.at[q, rws], w_send.at[pj * nb + pi], w_recv.at[pj * nb + pi],
                             (*chips[pj], c)) for pi, (_, rws) in enumerate(mine)] for pj in range(3)]

        @pl.when(s == 0)
        def _():
            for cp in sum(bulk(), []):
                cp.start()

        @pl.when(s == n_steps - 1)
        def _():
            handed = []
            for pj, (cx, cy) in enumerate(chips):
                for pi, (_, rws) in enumerate(mine):
                    bulk()[pj][pi].wait_recv()
                    cp = _remote(pall_ref.at[2 * cx + cy, rws], pall_ref.at[2 * cx + cy, rws],
                                 h_send.at[pj * nb + pi], h_recv.at[pj * nb + pi], (x, y, 1 - c))
                    cp.start()
                    handed.append(cp)
            for pj, (cx, cy) in enumerate(chips):
                for pi, (_, rws) in enumerate(other):
                    _remote(pall_ref.at[2 * cx + cy, rws], pall_ref.at[2 * cx + cy, rws],
                            h_send.at[pj * nb + pi], h_recv.at[pj * nb + pi], (x, y, 1 - c)).wait_recv()
            for cp in sum(bulk(), []) + handed:
                cp.wait_send()

    any_spec = pl.BlockSpec(memory_space=pl.ANY)
    in_specs = [pl.BlockSpec((tm, k), lambda s: (u_row(s), 0))]
    in_specs += [pl.BlockSpec((k, tns[g]), lambda s, g=g: (0, rel(s, g) // mts[g])) for g in range(n_g)]
    in_specs += [pl.BlockSpec((1, tns[g]), lambda s, g=g: (0, rel(s, g) // mts[g])) for g in range(n_g)]
    in_specs.append(any_spec)
    out_specs = [pl.BlockSpec((tm, tns[g]), lambda s, g=g: (rel(s, g) % mts[g], rel(s, g) // mts[g])) for g in range(n_g)]
    out_specs.append(any_spec)
    out_shape = [jax.ShapeDtypeStruct((rows[g], ws[g].shape[1]), BF16) for g in range(n_g)]
    out_shape.append(jax.ShapeDtypeStruct((N_CHIPS,) + p_sh.shape, p_sh.dtype))
    return _pallas(
        body, name="proj_all", grid=(n_steps,), in_specs=in_specs, out_specs=out_specs, out_shape=out_shape,
        scratch_shapes=[pltpu.SemaphoreType.DMA((3 * P_ROW_CHUNKS,)) for _ in range(4)],
        compiler_params=_params(("arbitrary",)),
    )(u, *ws, *bs, p_sh)


def matmul_tn(a, b, *, name, t, tn, tt, colsum=False, swap=None):
    m = a.shape[1]
    n = b.shape[1]
    nj, ns = n // tn, t // tt
    n_out = 2 if colsum else 1
    n_sw = 0 if swap is None else len(swap)

    def body(a_ref, b_ref, *rest):
        o_ref = rest[n_sw]
        cs_ref = rest[n_sw + 1] if colsum else None
        j, s = pl.program_id(0), pl.program_id(1)

        if swap is not None:
            g_refs = rest[:n_sw]
            got_refs = rest[n_sw + n_out:2 * n_sw + n_out]
            sems = rest[2 * n_sw + n_out:]

            @pl.when((j == 0) & (s == 0))
            def _():
                for cp in _pair_copies(g_refs, None, got_refs, None, *sems):
                    cp.start()

            @pl.when((j == nj - 1) & (s == ns - 1))
            def _():
                for cp in _pair_copies(g_refs, None, got_refs, None, *sems):
                    cp.wait_recv()
                for cp in _pair_copies(g_refs, None, got_refs, None, *sems):
                    cp.wait_send()

        @pl.when(s == 0)
        def _():
            o_ref[...] = jnp.zeros_like(o_ref)
            if colsum:
                cs_ref[...] = jnp.zeros_like(cs_ref)
        o_ref[...] += _tn(a_ref[...], b_ref[...])
        if colsum:
            cs_ref[...] += jnp.sum(b_ref[...].astype(F32), axis=0, keepdims=True)

    in_specs = [pl.BlockSpec((tt, m), lambda j, s: (s, 0)), pl.BlockSpec((tt, tn), lambda j, s: (s, j))]
    out_specs = [pl.BlockSpec((m, tn), lambda j, s: (0, j))]
    out_shape = [jax.ShapeDtypeStruct((m, n), F32)]
    if colsum:
        out_specs.append(pl.BlockSpec((1, tn), lambda j, s: (0, j)))
        out_shape.append(jax.ShapeDtypeStruct((1, n), F32))
    args, scratch = [a, b], []
    if swap is not None:
        any_spec = pl.BlockSpec(memory_space=pl.ANY)
        in_specs += [any_spec] * n_sw
        out_specs += [any_spec] * n_sw
        out_shape += _pair_got_shapes(swap, None)
        args += list(swap)
        scratch = [pltpu.SemaphoreType.DMA((_pair_count(swap, None),)), pltpu.SemaphoreType.DMA((_pair_count(swap, None),))]
    return _pallas(
        body, name=name, grid=(nj, ns), in_specs=in_specs, out_specs=out_specs, out_shape=out_shape,
        scratch_shapes=scratch,
        compiler_params=_params(("parallel" if swap is None else "arbitrary", "arbitrary")),
    )(*args)


def dgrad_norm_bwd(dps, wts, paw, x2, ctx2, dh, scale1, norm_g, *, tm):
    t, tc = x2.shape[0], ctx2.shape[0]
    t_all = t + tc
    n_lat, n_ctx = t // tm, tc // tm
    n_tiles = n_lat + n_ctx
    n_samples = scale1.shape[0] - 1
    tps = n_lat // n_samples
    n_grp = n_samples + 1
    n_g = len(dps)
    whole = [g for g in range(n_g) if dps[g].shape[0] == t_all]
    latent = [g for g in range(n_g) if dps[g].shape[0] != t_all]

    def body(*refs):
        dp_refs, w_refs = refs[:n_g], refs[n_g:2 * n_g]
        (paw_ref, x_ref, c_ref, dh_ref, sc_ref, g_ref, dx_ref, dsh_ref, dsc_ref, dg_ref, rbw_ref,
         du_buf, b_send, b_recv) = refs[2 * n_g:]
        i = pl.program_id(0)

        def exchange():
            x, y, c, chips = _place()
            chunks = _half_chunks(0, (2 * paw_ref.shape[1],), 16, which=(0,))
            return [_remote(paw_ref.at[2 * cx + cy, rows], rbw_ref.at[j, rows],
                            b_send.at[j * N_BULK + k], b_recv.at[j * N_BULK + k], (cx, cy, c))
                    for j, (cx, cy) in enumerate(chips) for k, (_, rows) in enumerate(chunks)]

        @pl.when(i == 0)
        def _():
            for cp in exchange():
                cp.start()

        acc = None
        for g in whole:
            part = _nt(dp_refs[g][...], w_refs[g][...])
            acc = part if acc is None else acc + part
        du_buf[...] = acc

        @pl.when(i < n_lat)
        def _():
            lat = None
            for g in latent:
                part = _nt(dp_refs[g][...], w_refs[g][...])
                lat = part if lat is None else lat + part
            du_buf[...] += lat

        duv = du_buf[...]
        xv = jnp.where(i < n_lat, x_ref[...], c_ref[...])
        rs = lax.rsqrt(jnp.mean(xv * xv, axis=-1, keepdims=True) + EPS)
        xh = xv * rs
        n = xh * g_ref[...]
        dn = duv * sc_ref[0]
        dxh = dn * g_ref[...]
        dx = rs * (dxh - xh * jnp.mean(dxh * xh, axis=-1, keepdims=True))

        @pl.when(i < n_lat)
        def _():
            dx_ref[...] = dx + dh_ref[...]

        @pl.when((i % tps == 0) & (i <= n_lat))
        def _():
            dsh_ref[...] = jnp.zeros_like(dsh_ref)
            dsc_ref[...] = jnp.zeros_like(dsc_ref)

        @pl.when(i == 0)
        def _():
            dg_ref[...] = jnp.zeros_like(dg_ref)

        dsh_ref[0] += jnp.sum(duv, axis=0, keepdims=True)
        dsc_ref[0] += jnp.sum(duv * n, axis=0, keepdims=True)
        dg_ref[...] += jnp.sum(dn * xh, axis=0, keepdims=True)

        @pl.when(i == n_tiles - 1)
        def _():
            for cp in exchange():
                cp.wait_recv()
            for cp in exchange():
                cp.wait_send()

    lat = lambda i: (jnp.minimum(i, n_lat - 1), 0)
    grp = lambda i: (jnp.minimum(i // tps, n_samples), 0, 0)
    in_specs = []
    for g, dp in enumerate(dps):
        nrow = dp.shape[0] // tm
        in_specs.append(pl.BlockSpec((tm, dp.shape[1]), lambda i, nrow=nrow: (jnp.minimum(i, nrow - 1), 0)))
    for w in wts:
        in_specs.append(pl.BlockSpec(w.shape, lambda i: (0, 0), pipeline_mode=pl.Buffered(1)))
    any_spec = pl.BlockSpec(memory_space=pl.ANY)
    in_specs += [any_spec,
                 pl.BlockSpec((tm, D), lat), pl.BlockSpec((tm, D), lambda i: (jnp.maximum(i - n_lat, 0), 0)),
                 pl.BlockSpec((tm, D), lat), pl.BlockSpec((1, 1, D), grp), pl.BlockSpec((1, D), lambda i: (0, 0))]
    return _pallas(
        body, name="dgrad_norm_bwd", grid=(n_tiles,), in_specs=in_specs,
        out_specs=[pl.BlockSpec((tm, D), lat), pl.BlockSpec((1, 1, D), grp), pl.BlockSpec((1, 1, D), grp),
                   pl.BlockSpec((1, D), lambda i: (0, 0)), any_spec],
        out_shape=[jax.ShapeDtypeStruct((t, D), F32), jax.ShapeDtypeStruct((n_grp, 1, D), F32),
                   jax.ShapeDtypeStruct((n_grp, 1, D), F32), jax.ShapeDtypeStruct((1, D), F32),
                   jax.ShapeDtypeStruct((3,) + paw.shape[1:], paw.dtype)],
        scratch_shapes=[pltpu.VMEM((tm, D), F32), pltpu.SemaphoreType.DMA((3 * N_BULK,)),
                        pltpu.SemaphoreType.DMA((3 * N_BULK,))],
        compiler_params=_params(("arbitrary",)),
    )(*dps, *wts, paw, x2, ctx2, dh, scale1, norm_g)


TM_NORM = 512


def norm_mod_fwd(x2, ctx2, scale1, shift, norm_g):
    t = x2.shape[0]
    n_lat = t // TM_NORM
    assert ctx2.shape[0] == TM_NORM
    n_samples = scale1.shape[0] - 1
    tps = n_lat // n_samples

    def body(x_ref, c_ref, sc_ref, sh_ref, g_ref, u_ref):
        i = pl.program_id(0)
        xv = jnp.where(i < n_lat, x_ref[...], c_ref[...])
        rs = lax.rsqrt(jnp.mean(xv * xv, axis=-1, keepdims=True) + EPS)
        u = xv * rs * g_ref[...] * sc_ref[0] + sh_ref[0]
        u_ref[...] = u.astype(u_ref.dtype)

    grp = lambda i: (jnp.minimum(i // tps, n_samples), 0, 0)
    return _pallas(
        body, name="norm_mod_fwd", grid=(n_lat + 1,),
        in_specs=[pl.BlockSpec((TM_NORM, D), lambda i: (jnp.minimum(i, n_lat - 1), 0)),
                  pl.BlockSpec((TM_NORM, D), lambda i: (0, 0)),
                  pl.BlockSpec((1, 1, D), grp), pl.BlockSpec((1, 1, D), grp),
                  pl.BlockSpec((1, D), lambda i: (0, 0))],
        out_specs=pl.BlockSpec((TM_NORM, D), lambda i: (i, 0)),
        out_shape=jax.ShapeDtypeStruct((t + TM_NORM, D), BF16),
        compiler_params=_params(("parallel",)),
    )(x2, ctx2, scale1, shift, norm_g)


CONV_CB = 256
CONV_NCB = D // CONV_CB
H_OFF = 16


H_CB = 128
H_SPAN = GRID_W + 2 * H_OFF - 8


def _conv_scratch(vertical):
    if vertical:
        return [pltpu.VMEM((GRID_H + 2 * CONV_PAD, GRID_W, CONV_CB), F32)]
    return [pltpu.VMEM((GRID_H, GRID_W + 2 * H_OFF, H_CB), F32), pltpu.VMEM((7, GRID_H, H_SPAN, H_CB), F32)]


def _conv_fill(bufs, img, vertical):
    pad_ref = bufs[0]
    pad_ref[...] = jnp.zeros_like(pad_ref)
    if vertical:
        pad_ref[pl.ds(CONV_PAD, GRID_H)] = img
        return
    pad_ref[:, pl.ds(H_OFF, GRID_W), :] = img

    def shift(r, carry):
        for s in range(1, 8):
            bufs[1][s - 1, r] = pad_ref[r, pl.ds(s, H_SPAN), :]
        return carry

    lax.fori_loop(0, GRID_H, shift, 0)


def _conv_window(bufs, k, vertical, r, w0=0, nw=GRID_W, lanes=slice(None)):
    if vertical:
        return bufs[0][r + k, pl.ds(w0, nw), lanes]
    off = H_OFF - CONV_PAD + k
    if off % 8 == 0:
        return bufs[0][r, pl.ds(off + w0, nw), lanes]
    return bufs[1][off % 8 - 1, r, pl.ds(off - off % 8 + w0, nw), lanes]


def _conv_col_blocks(vertical):
    if vertical:
        return [pl.ds(0, CONV_CB)]
    return [pl.ds(i * H_CB, H_CB) for i in range(CONV_CB // H_CB)]


def _rows(r):
    return pl.ds(pl.multiple_of(r * GRID_W, GRID_W), GRID_W)


def conv_fwd(p1, conv_w, conv_b, n_samples):
    t = n_samples * SEQ

    def make(vertical, prev):
        n_buf = len(_conv_scratch(vertical))

        def body(gv_ref, gg_ref, w_ref, b_ref, *rest):
            o_ref, bufs = rest[-1 - n_buf], rest[-n_buf:]
            for cols in _conv_col_blocks(vertical):
                a = gv_ref[:, cols].astype(F32) * _sigmoid(gg_ref[:, cols].astype(F32))
                _conv_fill(bufs, a.reshape(GRID_H, GRID_W, a.shape[-1]), vertical)

                def row(r, carry, cols=cols):
                    acc = jnp.zeros((GRID_W, cols.size), F32) + b_ref[:, cols]
                    for k in range(CONV_K):
                        acc = acc + _conv_window(bufs, k, vertical, r) * w_ref[pl.ds(k, 1), cols]
                    o_ref[_rows(r), cols] = acc
                    return carry

                lax.fori_loop(0, GRID_H, row, 0)

        cb0 = CONV_NCB // 2 if vertical else 0
        in_specs = [pl.BlockSpec((SEQ, CONV_CB), lambda b, j: (b, 2 * (cb0 + j))),
                    pl.BlockSpec((SEQ, CONV_CB), lambda b, j: (b, 2 * (cb0 + j) + 1)),
                    pl.BlockSpec((CONV_K + 1, CONV_CB), lambda b, j: (0, cb0 + j)),
                    pl.BlockSpec((1, CONV_CB), lambda b, j: (0, cb0 + j))]
        args = [p1, p1, conv_w, conv_b]
        aliases = {}
        if prev is not None:
            in_specs.append(pl.BlockSpec(memory_space=pl.ANY))
            args.append(prev)
            aliases = {4: 0}
        return _pallas(
            body, name="conv_fwd_v" if vertical else "conv_fwd_h", grid=(n_samples, CONV_NCB // 2),
            in_specs=in_specs,
            out_specs=pl.BlockSpec((SEQ, CONV_CB), lambda b, j: (b, cb0 + j)),
            out_shape=jax.ShapeDtypeStruct((t, D), F32),
            scratch_shapes=_conv_scratch(vertical),
            input_output_aliases=aliases,
            compiler_params=_params(("parallel", "parallel")),
        )(*args)

    return make(True, make(False, None))


def conv_bwd(p1, daconv, conv_w, n_samples):
    t = n_samples * SEQ

    def make(vertical, prev):
        n_buf = len(_conv_scratch(vertical))

        def body(gv_ref, gg_ref, dy_ref, w_ref, *rest):
            dp_ref, dw_ref, db_ref = rest[-3 - 2 * n_buf - 1:-2 * n_buf - 1]
            a_bufs, d_bufs, da_ref = rest[-2 * n_buf - 1:-n_buf - 1], rest[-n_buf - 1:-1], rest[-1]
            for cols in _conv_col_blocks(vertical):
                width = cols.size
                gv = gv_ref[:, cols].astype(F32)
                sg = _sigmoid(gg_ref[:, cols].astype(F32))
                _conv_fill(a_bufs, (gv * sg).reshape(GRID_H, GRID_W, width), vertical)
                _conv_fill(d_bufs, dy_ref[:, cols].reshape(GRID_H, GRID_W, width), vertical)

                def row(r, carry, cols=cols, width=width):
                    acc = jnp.zeros((GRID_W, width), F32)
                    for k in range(CONV_K):
                        acc = acc + _conv_window(d_bufs, CONV_K - 1 - k, vertical, r) * w_ref[pl.ds(k, 1), cols]
                    da_ref[_rows(r), cols] = acc
                    return carry

                lax.fori_loop(0, GRID_H, row, 0)
                da = da_ref[:, cols]
                dp_ref[:, pl.ds(cols.start, width)] = (da * sg).astype(dp_ref.dtype)
                dp_ref[:, pl.ds(CONV_CB + cols.start, width)] = (da * gv * sg * (1.0 - sg)).astype(dp_ref.dtype)

                for lb in range(width // 128):
                    lanes = pl.ds(lb * 128, 128)
                    dy_lanes = pl.ds(cols.start + lb * 128, 128)

                    def wrow(r, accs, lanes=lanes, dy_lanes=dy_lanes):
                        for w0 in range(0, GRID_W, 8):
                            dyv = dy_ref[pl.ds(pl.multiple_of(r * GRID_W, GRID_W) + w0, 8), dy_lanes]
                            accs = tuple(accs[k] + _conv_window(a_bufs, k, vertical, r, w0, 8, lanes) * dyv
                                         for k in range(CONV_K))
                        return accs

                    accs = lax.fori_loop(0, GRID_H, wrow, tuple(jnp.zeros((8, 128), F32) for _ in range(CONV_K)))
                    for k in range(CONV_K):
                        dw_ref[0, pl.ds(k, 1), dy_lanes] = jnp.sum(accs[k], axis=0, keepdims=True)
            dw_ref[0, pl.ds(CONV_K, 1), :] = jnp.zeros((1, CONV_CB), F32)
            db_ref[0] = jnp.sum(dy_ref[...], axis=0, keepdims=True)

        cb0 = CONV_NCB // 2 if vertical else 0
        in_specs = [pl.BlockSpec((SEQ, CONV_CB), lambda b, j: (b, 2 * (cb0 + j))),
                    pl.BlockSpec((SEQ, CONV_CB), lambda b, j: (b, 2 * (cb0 + j) + 1)),
                    pl.BlockSpec((SEQ, CONV_CB), lambda b, j: (b, cb0 + j)),
                    pl.BlockSpec((CONV_K + 1, CONV_CB), lambda b, j: (0, cb0 + j))]
        args = [p1, p1, daconv, conv_w]
        aliases = {}
        if prev is not None:
            in_specs += [pl.BlockSpec(memory_space=pl.ANY)] * 3
            args += list(prev)
            aliases = {4: 0, 5: 1, 6: 2}
        return _pallas(
            body, name="conv_bwd_v" if vertical else "conv_bwd_h", grid=(n_samples, CONV_NCB // 2),
            in_specs=in_specs,
            out_specs=[pl.BlockSpec((SEQ, 2 * CONV_CB), lambda b, j: (b, cb0 + j)),
                       pl.BlockSpec((1, CONV_K + 1, CONV_CB), lambda b, j: (b, 0, cb0 + j)),
                       pl.BlockSpec((1, 1, CONV_CB), lambda b, j: (b, 0, cb0 + j))],
            out_shape=[jax.ShapeDtypeStruct((t, 2 * D), BF16),
                       jax.ShapeDtypeStruct((n_samples, CONV_K + 1, D), F32),
                       jax.ShapeDtypeStruct((n_samples, 1, D), F32)],
            scratch_shapes=_conv_scratch(vertical) + _conv_scratch(vertical) + [pltpu.VMEM((SEQ, CONV_CB), F32)],
            input_output_aliases=aliases,
            compiler_params=_params(("parallel", "parallel")),
        )(*args)

    return make(True, make(False, None))


TM_EW = 256


def ln_gate_proj(aconv, z, ln_g, ln_b, conv_proj):
    t = aconv.shape[0]

    def body(a_ref, z_ref, g_ref, b_ref, w_ref, o_ref, y_ref):
        a = a_ref[...]
        mu = jnp.mean(a, axis=-1, keepdims=True)
        xc = a - mu
        rstd = lax.rsqrt(jnp.mean(xc * xc, axis=-1, keepdims=True) + EPS)
        l = xc * rstd * g_ref[...] + b_ref[...]
        ac = _b16(_silu(l) * _silu(z_ref[...].astype(F32)))
        o_ref[...] = ac
        y_ref[...] = _nn(ac, w_ref[...]).astype(y_ref.dtype)

    row = pl.BlockSpec((TM_OUT, D), lambda i: (i, 0))
    vec = pl.BlockSpec((1, D), lambda i: (0, 0))
    return _pallas(
        body, name="ln_gate_proj", grid=(t // TM_OUT,),
        in_specs=[row, row, vec, vec, pl.BlockSpec((D, D), lambda i: (0, 0))], out_specs=[row, row],
        out_shape=[jax.ShapeDtypeStruct((t, D), BF16), jax.ShapeDtypeStruct((t, D), BF16)],
        compiler_params=_params(("parallel",)),
    )(aconv, z, ln_g, ln_b, conv_proj)


def ln_gate_bwd(aconv, z, dyc, conv_proj, ln_g, ln_b):
    t = aconv.shape[0]

    def body(a_ref, z_ref, d_ref, w_ref, g_ref, b_ref, da_ref, dz_ref, dg_ref, db_ref):
        a = a_ref[...]
        zv = z_ref[...].astype(F32)
        dac_v = _nt(d_ref[...], w_ref[...])
        mu = jnp.mean(a, axis=-1, keepdims=True)
        xc = a - mu
        rstd = lax.rsqrt(jnp.mean(xc * xc, axis=-1, keepdims=True) + EPS)
        xh = xc * rstd
        l = xh * g_ref[...] + b_ref[...]
        dz_ref[...] = (dac_v * _silu(l) * _dsilu(zv)).astype(dz_ref.dtype)
        dl = dac_v * _silu(zv) * _dsilu(l)
        dxh = dl * g_ref[...]
        da_ref[...] = rstd * (dxh - jnp.mean(dxh, axis=-1, keepdims=True)
                              - xh * jnp.mean(dxh * xh, axis=-1, keepdims=True))

        @pl.when(pl.program_id(0) == 0)
        def _():
            dg_ref[...] = jnp.zeros_like(dg_ref)
            db_ref[...] = jnp.zeros_like(db_ref)

        dg_ref[...] += jnp.sum(dl * xh, axis=0, keepdims=True)
        db_ref[...] += jnp.sum(dl, axis=0, keepdims=True)

    row = pl.BlockSpec((TM_EW, D), lambda i: (i, 0))
    vec = pl.BlockSpec((1, D), lambda i: (0, 0))
    return _pallas(
        body, name="ln_gate_bwd", grid=(t // TM_EW,),
        in_specs=[row, row, row, pl.BlockSpec((D, D), lambda i: (0, 0)), vec, vec],
        out_specs=[row, row, vec, vec],
        out_shape=[jax.ShapeDtypeStruct((t, D), F32), jax.ShapeDtypeStruct((t, D), BF16),
                   jax.ShapeDtypeStruct((1, D), F32), jax.ShapeDtypeStruct((1, D), F32)],
        compiler_params=_params(("arbitrary",)),
    )(aconv, z, dyc, conv_proj, ln_g, ln_b)


TM_PREP = 256
PREP_LAT = SEQ // TM_PREP
PREP_ALL = SEQ_ALL // TM_PREP


def _chunk_tri(n, upper):
    r = lax.broadcasted_iota(jnp.int32, (n, n), 0)
    c = lax.broadcasted_iota(jnp.int32, (n, n), 1)
    same = (r // CHUNK) == (c // CHUNK)
    keep = (c >= r) if upper else (c <= r)
    return jnp.where(same & keep, 1.0, 0.0).astype(F32)


def _split3(v):
    hi = v.astype(BF16)
    r1 = v - hi.astype(F32)
    mid = r1.astype(BF16)
    lo = (r1 - mid.astype(F32)).astype(BF16)
    return jnp.stack([hi, mid, lo])


def _chunk_sums(v, upper):
    tri = _chunk_tri(v.shape[0], upper).astype(BF16)
    pieces = _split3(v)
    return (_nn(tri, pieces[0]) + _nn(tri, pieces[1])) + _nn(tri, pieces[2])


def _gate_logits(ab, up3_ref, bias_ref):
    assert ab.dtype == BF16
    return ((_nn(ab, up3_ref[0]) + _nn(ab, up3_ref[1])) + _nn(ab, up3_ref[2])) + bias_ref[...]


def _prep_tile_maps(n_samples):
    n_lat = n_samples * PREP_LAT

    def seq_map(i):
        return jnp.where(i < n_lat, i // PREP_LAT, i - n_lat), jnp.where(i < n_lat, i % PREP_LAT, PREP_LAT)

    return n_lat, seq_map


def gla_prep_fwd(p3, upf, upb, bias_f, bias_b, n_samples):
    n_lat, seq_map = _prep_tile_maps(n_samples)
    n_tiles = n_lat + n_samples

    def body(v_ref, q_ref, k_ref, ab_ref, upf_ref, upb_ref, bf_ref, bb_ref, qo, ko, vo, cf, cb):
        i = pl.program_id(0)
        qo[0] = jnp.where(i < n_lat, q_ref[...].astype(F32) * Q_SCALE, 0.0)
        ko[0] = k_ref[...]
        vo[0] = v_ref[...]
        ab = ab_ref[...]
        gf = _log_sigmoid(_gate_logits(ab, upf_ref, bf_ref)) * (1.0 / GATE_TAU)
        gb = _log_sigmoid(_gate_logits(ab, upb_ref, bb_ref)) * (1.0 / GATE_TAU)
        cf[0] = _chunk_sums(gf, False)
        cb[0] = _chunk_sums(gb, True)

    def o_spec(w):
        return pl.BlockSpec((1, TM_PREP, w), lambda i: (*seq_map(i), 0))

    full = lambda shape: pl.BlockSpec(shape, lambda i: (0,) * len(shape))
    return _pallas(
        body, name="gla_prep_fwd", grid=(n_tiles,),
        in_specs=[pl.BlockSpec((TM_PREP, 1024), lambda i: (i, O3_V // 1024)),
                  pl.BlockSpec((TM_PREP, 512), lambda i: (i, O3_Q // 512)),
                  pl.BlockSpec((TM_PREP, 512), lambda i: (i, O3_K // 512)),
                  pl.BlockSpec((TM_PREP, 128), lambda i: (i, O3_AB // 128)),
                  full((3, 128, GLA_DK)), full((3, 128, GLA_DK)), full((1, GLA_DK)), full((1, GLA_DK))],
        out_specs=[o_spec(GLA_DK), o_spec(GLA_DK), o_spec(D), o_spec(GLA_DK), o_spec(GLA_DK)],
        out_shape=[jax.ShapeDtypeStruct((n_samples, SEQ_ALL, GLA_DK), F32),
                   jax.ShapeDtypeStruct((n_samples, SEQ_ALL, GLA_DK), p3.dtype),
                   jax.ShapeDtypeStruct((n_samples, SEQ_ALL, D), p3.dtype),
                   jax.ShapeDtypeStruct((n_samples, SEQ_ALL, GLA_DK), F32),
                   jax.ShapeDtypeStruct((n_samples, SEQ_ALL, GLA_DK), F32)],
        compiler_params=_params(("parallel",)),
    )(p3, p3, p3, p3, upf, upb, bias_f, bias_b)


def gla_prep_bwd(p3, dq_f, dq_b, dk_f, dk_b, dv_f, dv_b, dc_f, dc_b, upf, upb, bias_f, bias_b, n_samples):
    n_lat, seq_map = _prep_tile_maps(n_samples)
    n_tiles = n_lat + n_samples

    def body(ab_ref, dqf, dqb, dkf, dkb, dvf, dvb, dcf, dcb, upf_ref, upb_ref, bf_ref, bb_ref,
             dp_ref, duf_ref, dub_ref, dbf_ref, dbb_ref):
        i = pl.program_id(0)
        both = lambda a, b: a[0].astype(F32) + b[0].astype(F32)
        dp_ref[:, pl.ds(O3_V, D)] = both(dvf, dvb).astype(dp_ref.dtype)
        dq = jnp.where(i < n_lat, both(dqf, dqb) * Q_SCALE, 0.0)
        dp_ref[:, pl.ds(O3_Q, GLA_DK)] = dq.astype(dp_ref.dtype)
        dp_ref[:, pl.ds(O3_K, GLA_DK)] = both(dkf, dkb).astype(dp_ref.dtype)
        ab = ab_ref[...]
        zf = _gate_logits(ab, upf_ref, bf_ref)
        zb = _gate_logits(ab, upb_ref, bb_ref)
        dgf = _chunk_sums(dcf[0], True)
        dgb = _chunk_sums(dcb[0], False)
        dzf = _b16(dgf * (1.0 / GATE_TAU) * _sigmoid(-zf))
        dzb = _b16(dgb * (1.0 / GATE_TAU) * _sigmoid(-zb))
        dab = _nt(dzf, upf_ref[0]) + _nt(dzb, upb_ref[0])
        dp_ref[:, pl.ds(O3_AB, 128)] = dab.astype(dp_ref.dtype)

        @pl.when(i == 0)
        def _():
            duf_ref[...] = jnp.zeros_like(duf_ref)
            dub_ref[...] = jnp.zeros_like(dub_ref)
            dbf_ref[...] = jnp.zeros_like(dbf_ref)
            dbb_ref[...] = jnp.zeros_like(dbb_ref)

        duf_ref[...] += _tn(ab, dzf)
        dub_ref[...] += _tn(ab, dzb)
        dbf_ref[...] += jnp.sum(dzf.astype(F32), axis=0, keepdims=True)
        dbb_ref[...] += jnp.sum(dzb.astype(F32), axis=0, keepdims=True)

    def s_spec(w):
        return pl.BlockSpec((1, TM_PREP, w), lambda i: (*seq_map(i), 0))

    full = lambda shape: pl.BlockSpec(shape, lambda i: (0,) * len(shape))
    return _pallas(
        body, name="gla_prep_bwd", grid=(n_tiles,),
        in_specs=[pl.BlockSpec((TM_PREP, 128), lambda i: (i, O3_AB // 128)),
                  s_spec(GLA_DK), s_spec(GLA_DK), s_spec(GLA_DK), s_spec(GLA_DK), s_spec(D), s_spec(D),
                  s_spec(GLA_DK), s_spec(GLA_DK),
                  full((3, 128, GLA_DK)), full((3, 128, GLA_DK)), full((1, GLA_DK)), full((1, GLA_DK))],
        out_specs=[pl.BlockSpec((TM_PREP, W3), lambda i: (i, 0)),
                   full((128, GLA_DK)), full((128, GLA_DK)), full((1, GLA_DK)), full((1, GLA_DK))],
        out_shape=[jax.ShapeDtypeStruct((n_tiles * TM_PREP, W3), BF16),
                   jax.ShapeDtypeStruct((128, GLA_DK), F32), jax.ShapeDtypeStruct((128, GLA_DK), F32),
                   jax.ShapeDtypeStruct((1, GLA_DK), F32), jax.ShapeDtypeStruct((1, GLA_DK), F32)],
        compiler_params=_params(("arbitrary",)),
    )(p3, dq_f, dq_b, dk_f, dk_b, dv_f, dv_b, dc_f, dc_b, upf, upb, bias_f, bias_b)


def _sub_blocks(rev):
    if NSUB == 1:
        return [((0, CHUNK), CHUNK // 2, (0, CHUNK))]
    out = []
    for s in range(NSUB):
        rows = (s * SUB, SUB)
        if rev:
            ref = (s + 1) * SUB if s < NSUB - 1 else None
            cols = (s * SUB, CHUNK - s * SUB)
        else:
            ref = s * SUB - 1 if s > 0 else None
            cols = (0, (s + 1) * SUB)
        out.append((rows, ref, cols))
    return out


def _sub_mask(rows, cols, rev):
    r = rows[0] + lax.broadcasted_iota(jnp.int32, (rows[1], cols[1]), 0)
    c = cols[0] + lax.broadcasted_iota(jnp.int32, (rows[1], cols[1]), 1)
    return (c >= r) if rev else (c <= r)


def _sub_operands(qc, kc, cc, rows, ref, cols):
    cref = jnp.zeros((1, HEAD_K), F32) if ref is None else cc[ref:ref + 1]
    eq = jnp.exp(cc[rows[0]:rows[0] + rows[1]] - cref)
    ek = jnp.exp(cref - cc[cols[0]:cols[0] + cols[1]])
    qs = qc[rows[0]:rows[0] + rows[1]] * eq
    kk = kc[cols[0]:cols[0] + cols[1]] * ek
    return qs, kk, eq, ek


SCAN_ROWS = 256
SCAN_CHUNKS = SCAN_ROWS // CHUNK
SCAN_STEPS = SEQ_ALL // SCAN_ROWS
LAT_BLOCKS = SEQ // SCAN_ROWS


def _scan_block(t, rev):
    if rev:
        return SCAN_STEPS - 1 - t
    return jnp.where(t == 0, SCAN_STEPS - 1, t - 1)


def _scan_lat_block(t, rev):
    if rev:
        return jnp.minimum(SCAN_STEPS - 1 - t, LAT_BLOCKS - 1)
    return jnp.maximum(t - 1, 0)


def _head_cols(h):
    return pl.ds(h * HEAD_K, HEAD_K), pl.ds(h * HEAD_V, HEAD_V)


def gla_scan_fwd(q, k, v, cum, *, rev, name):
    n = q.shape[0]

    def body(q_ref, k_ref, v_ref, c_ref, o_ref, s_ref, sfin_ref, st):
        t = pl.program_id(1)

        @pl.when(t == 0)
        def _():
            st[...] = jnp.zeros_like(st)

        def chunk(j, carry):
            lj = SCAN_CHUNKS - 1 - j if rev else j
            r0 = lj * CHUNK
            rws = pl.ds(r0, CHUNK)
            for h in range(HEADS):
                kcols, vcols = _head_cols(h)
                qc, kc, cc = q_ref[0, rws, kcols], k_ref[0, rws, kcols], c_ref[0, rws, kcols]
                vc = v_ref[0, rws, vcols]
                s_in = st[h]
                s_ref[0, h, j] = _b16(s_in)
                edge = cc[0:1] if rev else cc[CHUNK - 1:CHUNK]
                ke = kc * jnp.exp(edge - cc)
                st[h] = s_in * jnp.exp(edge) + _tn(_b16(vc), _b16(ke))
                o_inter = _nt(_b16(qc * jnp.exp(cc)), _b16(s_in))
                vb = _b16(vc)
                for rows, ref, cols in _sub_blocks(rev):
                    qs, kk, _, _ = _sub_operands(qc, kc, cc, rows, ref, cols)
                    a = jnp.where(_sub_mask(rows, cols, rev), _nt(_b16(qs), _b16(kk)), 0.0)
                    o_s = _nn(_b16(a), vb[cols[0]:cols[0] + cols[1]])
                    o_ref[0, pl.ds(r0 + rows[0], rows[1]), vcols] = _b16(o_inter[rows[0]:rows[0] + rows[1]] + o_s)
            return carry

        for j in range(SCAN_CHUNKS):
            chunk(j, 0)

        @pl.when(t == SCAN_STEPS - 1)
        def _():
            sfin_ref[0] = st[...]

    def spec(w):
        return pl.BlockSpec((1, SCAN_ROWS, w), lambda b, t: (b, _scan_block(t, rev), 0))

    return _pallas(
        body, name=name, grid=(n, SCAN_STEPS),
        in_specs=[spec(GLA_DK), spec(GLA_DK), spec(D), spec(GLA_DK)],
        out_specs=[pl.BlockSpec((1, SCAN_ROWS, D), lambda b, t: (b, _scan_lat_block(t, rev), 0)),
                   pl.BlockSpec((1, HEADS, SCAN_CHUNKS, HEAD_V, HEAD_K), lambda b, t: (b, 0, t, 0, 0)),
                   pl.BlockSpec((1, HEADS, HEAD_V, HEAD_K), lambda b, t: (b, 0, 0, 0))],
        out_shape=[jax.ShapeDtypeStruct((n, SEQ, D), BF16),
                   jax.ShapeDtypeStruct((n, HEADS, NCHUNK, HEAD_V, HEAD_K), BF16),
                   jax.ShapeDtypeStruct((n, HEADS, HEAD_V, HEAD_K), F32)],
        scratch_shapes=[pltpu.VMEM((HEADS, HEAD_V, HEAD_K), F32)],
        compiler_params=_params(("parallel", "arbitrary")),
    )(q, k, v, cum)


def gla_scan_bwd(q, k, v, cum, s_all, s_fin, do, *, rev, name, rider=None):
    n = q.shape[0]

    def body(q_ref, k_ref, v_ref, c_ref, s_ref, sfin_ref, do_ref, *rest):
        if rider is not None:
            ride_in, rest = rest[0], rest[1:]
        dq_ref, dk_ref, dv_ref, dc_ref = rest[:4]
        if rider is not None:
            ride_out, rest = rest[4], rest[:4] + rest[5:]
        dst, s_next, dq_acc, dk_acc, dv_acc = rest[4:9]
        t = SCAN_STEPS - 1 - pl.program_id(1)

        if rider is not None:
            def copies():
                send, recv = rest[9], rest[10]
                if rider[0] == "swap":
                    return _pair_copies([], ride_in, [], ride_out, send, recv)
                x, y, c, chips = _place()
                return [_remote(ride_in.at[2 * cx + cy, rows], ride_out.at[pj, rows], send.at[pj * P_ROW_CHUNKS + pi],
                                recv.at[pj * P_ROW_CHUNKS + pi], (cx, cy, c))
                        for pj, (cx, cy) in enumerate(chips)
                        for pi, (_, rows) in enumerate(_half_chunks(0, (0, 2 * ride_in.shape[1]), 16, which=(1,)))]

            @pl.when((pl.program_id(0) == 0) & (pl.program_id(1) == 0))
            def _():
                for cp in copies():
                    cp.start()

            @pl.when((pl.program_id(0) == n - 1) & (pl.program_id(1) == SCAN_STEPS - 1))
            def _():
                for cp in copies():
                    cp.wait_recv()
                for cp in copies():
                    cp.wait_send()

        @pl.when(pl.program_id(1) == 0)
        def _():
            dst[...] = jnp.zeros_like(dst)
            s_next[...] = sfin_ref[0]

        def chunk(jj, carry):
            j = SCAN_CHUNKS - 1 - jj
            lj = SCAN_CHUNKS - 1 - j if rev else j
            rws = pl.ds(lj * CHUNK, CHUNK)
            for h in range(HEADS):
                kcols, vcols = _head_cols(h)
                qc, kc, cc = q_ref[0, rws, kcols], k_ref[0, rws, kcols], c_ref[0, rws, kcols]
                vc = v_ref[0, rws, vcols]
                doc = jnp.where(t > 0, do_ref[0, rws, vcols], 0.0)
                s_in = s_ref[0, h, j]
                s_out = s_next[h]
                ds_out = dst[h]
                edge = cc[0:1] if rev else cc[CHUNK - 1:CHUNK]
                e_q = jnp.exp(cc)
                e_k = jnp.exp(edge - cc)
                dob = _b16(doc)
                dsb = _b16(ds_out)
                dst[h] = ds_out * jnp.exp(edge) + _tn(dob, _b16(qc * e_q))
                s_next[h] = s_in.astype(F32)
                dq_acc[h] = e_q * _nn(dob, s_in)
                dk_acc[h] = e_k * _nn(_b16(vc), dsb)
                dv_acc[h] = _nt(_b16(kc * e_k), dsb)
                vb = _b16(vc)
                for rows, ref, cols in _sub_blocks(rev):
                    qs, kk, eq, ek = _sub_operands(qc, kc, cc, rows, ref, cols)
                    mask = _sub_mask(rows, cols, rev)
                    rsl = slice(rows[0], rows[0] + rows[1])
                    csl = pl.ds(cols[0], cols[1])
                    qsb, kkb = _b16(qs), _b16(kk)
                    a = jnp.where(mask, _nt(qsb, kkb), 0.0)
                    da = _b16(jnp.where(mask, _nt(dob[rsl], vb[cols[0]:cols[0] + cols[1]]), 0.0))
                    dq_acc[h, pl.ds(rows[0], rows[1]), :] += _nn(da, kkb) * eq
                    dk_acc[h, csl, :] += _tn(da, qsb) * ek
                    dv_acc[h, csl, :] += _tn(_b16(a), dob[rsl])
                dq = dq_acc[h]
                dk = dk_acc[h]
                dc = qc * dq - kc * dk
                bnd = jnp.sum(ds_out * s_out, axis=0, keepdims=True)
                edge_row = 0 if rev else CHUNK - 1
                is_edge = lax.broadcasted_iota(jnp.int32, (CHUNK, HEAD_K), 0) == edge_row
                dq_ref[0, rws, kcols] = _b16(dq)
                dk_ref[0, rws, kcols] = _b16(dk)
                dv_ref[0, rws, vcols] = _b16(dv_acc[h])
                dc_ref[0, rws, kcols] = dc + jnp.where(is_edge, bnd, 0.0)
            return carry

        for jj in range(SCAN_CHUNKS):
            chunk(jj, 0)

    def step_of(u):
        return SCAN_STEPS - 1 - u

    def spec(w):
        return pl.BlockSpec((1, SCAN_ROWS, w), lambda b, u: (b, _scan_block(step_of(u), rev), 0))

    in_specs = [spec(GLA_DK), spec(GLA_DK), spec(D), spec(GLA_DK),
                pl.BlockSpec((1, HEADS, SCAN_CHUNKS, HEAD_V, HEAD_K), lambda b, u: (b, 0, step_of(u), 0, 0)),
                pl.BlockSpec((1, HEADS, HEAD_V, HEAD_K), lambda b, u: (b, 0, 0, 0)),
                pl.BlockSpec((1, SCAN_ROWS, D), lambda b, u: (b, _scan_lat_block(step_of(u), rev), 0))]
    out_specs = [spec(GLA_DK), spec(GLA_DK), spec(D), spec(GLA_DK)]
    out_shape = [jax.ShapeDtypeStruct((n, SEQ_ALL, GLA_DK), BF16), jax.ShapeDtypeStruct((n, SEQ_ALL, GLA_DK), BF16),
                 jax.ShapeDtypeStruct((n, SEQ_ALL, D), BF16), jax.ShapeDtypeStruct((n, SEQ_ALL, GLA_DK), F32)]
    scratch = [pltpu.VMEM((HEADS, HEAD_V, HEAD_K), F32), pltpu.VMEM((HEADS, HEAD_V, HEAD_K), F32),
               pltpu.VMEM((HEADS, CHUNK, HEAD_K), F32), pltpu.VMEM((HEADS, CHUNK, HEAD_K), F32),
               pltpu.VMEM((HEADS, CHUNK, HEAD_V), F32)]
    args = [q, k, v, cum, s_all, s_fin, do]
    if rider is not None:
        kind, arr = rider
        any_spec = pl.BlockSpec(memory_space=pl.ANY)
        in_specs.append(any_spec)
        out_specs.append(any_spec)
        args.append(arr)
        if kind == "swap":
            out_shape += _pair_got_shapes([], arr)
            n_cp = _pair_count([], arr)
        else:
            out_shape.append(jax.ShapeDtypeStruct((3,) + arr.shape[1:], arr.dtype))
            n_cp = 3 * P_ROW_CHUNKS
        scratch += [pltpu.SemaphoreType.DMA((n_cp,)), pltpu.SemaphoreType.DMA((n_cp,))]
    return _pallas(
        body, name=name, grid=(n, SCAN_STEPS), in_specs=in_specs, out_specs=out_specs, out_shape=out_shape,
        scratch_shapes=scratch,
        compiler_params=_params(("parallel" if rider is None else "arbitrary", "arbitrary")),
    )(*args)


def gla_out_proj(o_f, o_b, r, gnorm, gla_proj):
    n = o_f.shape[0]
    tiles = SEQ // TM_OUT

    def body(of_ref, ob_ref, r_ref, g_ref, w_ref, og_ref, y_ref):
        for h in range(HEADS):
            cols = pl.ds(h * HEAD_V, HEAD_V)
            o = of_ref[0, :, cols].astype(F32) + ob_ref[0, :, cols].astype(F32)
            rs = lax.rsqrt(jnp.mean(o * o, axis=-1, keepdims=True) + EPS)
            og_ref[:, cols] = (o * rs * g_ref[...] * _silu(r_ref[:, cols].astype(F32))).astype(og_ref.dtype)
        y_ref[...] = _nn(og_ref[...], w_ref[...]).astype(y_ref.dtype)

    ospec = pl.BlockSpec((1, TM_OUT, D), lambda b, j: (b, j, 0))
    row = pl.BlockSpec((TM_OUT, D), lambda b, j: (b * tiles + j, 0))
    return _pallas(
        body, name="gla_out_proj", grid=(n, tiles),
        in_specs=[ospec, ospec, row, pl.BlockSpec((1, HEAD_V), lambda b, j: (0, 0)),
                  pl.BlockSpec((D, D), lambda b, j: (0, 0))],
        out_specs=[row, row],
        out_shape=[jax.ShapeDtypeStruct((n * SEQ, D), BF16), jax.ShapeDtypeStruct((n * SEQ, D), BF16)],
        compiler_params=_params(("parallel", "parallel")),
    )(o_f, o_b, r, gnorm, gla_proj)


def gla_out_bwd(o_f, o_b, r, dyg, gla_proj, gnorm):
    n = o_f.shape[0]
    tiles = SEQ // TM_EW

    def body(of_ref, ob_ref, r_ref, d_ref, w_ref, g_ref, do_ref, dr_ref, dg_ref, dog_buf):
        @pl.when((pl.program_id(0) == 0) & (pl.program_id(1) == 0))
        def _():
            dg_ref[...] = jnp.zeros_like(dg_ref)

        dog_buf[...] = _nt(d_ref[...], w_ref[...])
        for h in range(HEADS):
            cols = pl.ds(h * HEAD_V, HEAD_V)
            o = of_ref[0, :, cols].astype(F32) + ob_ref[0, :, cols].astype(F32)
            rv = r_ref[:, cols].astype(F32)
            dv = dog_buf[:, cols]
            rs = lax.rsqrt(jnp.mean(o * o, axis=-1, keepdims=True) + EPS)
            oh = o * rs
            dr_ref[:, cols] = (dv * oh * g_ref[...] * _dsilu(rv)).astype(dr_ref.dtype)
            dn = dv * _silu(rv)
            dg_ref[...] += jnp.sum(dn * oh, axis=0, keepdims=True)
            doh = dn * g_ref[...]
            do_ref[0, :, cols] = _b16(rs * (doh - oh * jnp.mean(doh * oh, axis=-1, keepdims=True)))

    ospec = pl.BlockSpec((1, TM_EW, D), lambda b, j: (b, j, 0))
    row = pl.BlockSpec((TM_EW, D), lambda b, j: (b * tiles + j, 0))
    vec = pl.BlockSpec((1, HEAD_V), lambda b, j: (0, 0))
    return _pallas(
        body, name="gla_out_bwd", grid=(n, tiles),
        in_specs=[ospec, ospec, row, row, pl.BlockSpec((D, D), lambda b, j: (0, 0)), vec],
        out_specs=[ospec, row, vec],
        out_shape=[jax.ShapeDtypeStruct((n, SEQ, D), BF16), jax.ShapeDtypeStruct((n * SEQ, D), BF16),
                   jax.ShapeDtypeStruct((1, HEAD_V), F32)],
        scratch_shapes=[pltpu.VMEM((TM_EW, D), F32)],
        compiler_params=_params(("arbitrary", "arbitrary")),
    )(o_f, o_b, r, dyg, gla_proj, gnorm)


TM_OUT = 512


def merge_out_final(p5, y_conv, y_gla, w_out, x2, gate, final_g, target, n_samples):
    t = x2.shape[0]
    tiles = SEQ // TM_OUT

    def body(mc_ref, mg_ref, yc_ref, yg_ref, w_ref, x_ref, gate_ref, g_ref, t_ref,
             mrg_ref, dh_ref, dmo_ref, dgate_ref, dg_ref, loss_ref):
        b, j = pl.program_id(0), pl.program_id(1)
        f = lambda ref: ref[...].astype(F32)
        merged = _b16(_sigmoid(f(mc_ref)) * f(yc_ref) + _sigmoid(f(mg_ref)) * f(yg_ref))
        mrg_ref[...] = merged
        mo_v = _nn(merged, w_ref[...])
        h = x_ref[...] + gate_ref[0] * mo_v
        rs = lax.rsqrt(jnp.mean(h * h, axis=-1, keepdims=True) + EPS)
        nh = h * rs
        err = nh * g_ref[...] - t_ref[...]
        dy = err * (1.0 / D)
        dn = dy * g_ref[...]
        dh = rs * (dn - nh * jnp.mean(dn * nh, axis=-1, keepdims=True))
        dh_ref[...] = dh
        dmo_ref[...] = (dh * gate_ref[0]).astype(dmo_ref.dtype)

        @pl.when(j == 0)
        def _():
            dgate_ref[...] = jnp.zeros_like(dgate_ref)

        @pl.when((b == 0) & (j == 0))
        def _():
            dg_ref[...] = jnp.zeros_like(dg_ref)
            loss_ref[...] = jnp.zeros_like(loss_ref)

        dgate_ref[0] += jnp.sum(dh * mo_v, axis=0, keepdims=True)
        dg_ref[...] += jnp.sum(dy * nh, axis=0, keepdims=True)
        loss_ref[...] += (0.5 / D) * jnp.sum(err * err)

    row = pl.BlockSpec((TM_OUT, D), lambda b, j: (b * tiles + j, 0))
    per = pl.BlockSpec((1, 1, D), lambda b, j: (b, 0, 0))
    vec = pl.BlockSpec((1, D), lambda b, j: (0, 0))
    return _pallas(
        body, name="merge_out_final", grid=(n_samples, tiles),
        in_specs=[row, pl.BlockSpec((TM_OUT, D), lambda b, j: (b * tiles + j, 1)), row, row,
                  pl.BlockSpec((D, D), lambda b, j: (0, 0)), row, per, vec, row],
        out_specs=[row, row, row, per, vec, pl.BlockSpec((8, 128), lambda b, j: (0, 0))],
        out_shape=[jax.ShapeDtypeStruct((t, D), BF16), jax.ShapeDtypeStruct((t, D), F32), jax.ShapeDtypeStruct((t, D), BF16),
                   jax.ShapeDtypeStruct((n_samples, 1, D), F32), jax.ShapeDtypeStruct((1, D), F32),
                   jax.ShapeDtypeStruct((8, 128), F32)],
        compiler_params=_params(("arbitrary", "arbitrary")),
    )(p5, p5, y_conv, y_gla, w_out, x2, gate, final_g, target)


def out_dgrad_merge_bwd(p5, y_conv, y_gla, dmo, w_out):
    t = y_conv.shape[0]

    def body(mc_ref, mg_ref, yc_ref, yg_ref, d_ref, w_ref, dyc_ref, dyg_ref, dp_ref):
        f = lambda ref: ref[...].astype(F32)
        d = _nt(d_ref[...], w_ref[...])
        sc = _sigmoid(f(mc_ref))
        sg = _sigmoid(f(mg_ref))
        dyc_ref[...] = (d * sc).astype(dyc_ref.dtype)
        dyg_ref[...] = (d * sg).astype(dyg_ref.dtype)
        dp_ref[:, pl.ds(0, D)] = (d * f(yc_ref) * sc * (1.0 - sc)).astype(dp_ref.dtype)
        dp_ref[:, pl.ds(D, D)] = (d * f(yg_ref) * sg * (1.0 - sg)).astype(dp_ref.dtype)

    row = pl.BlockSpec((TM_OUT, D), lambda i: (i, 0))
    return _pallas(
        body, name="out_dgrad_merge_bwd", grid=(t // TM_OUT,),
        in_specs=[row, pl.BlockSpec((TM_OUT, D), lambda i: (i, 1)), row, row, row, pl.BlockSpec((D, D), lambda i: (0, 0))],
        out_specs=[row, row, pl.BlockSpec((TM_OUT, 2 * D), lambda i: (i, 0))],
        out_shape=[jax.ShapeDtypeStruct((t, D), BF16), jax.ShapeDtypeStruct((t, D), BF16),
                   jax.ShapeDtypeStruct((t, 2 * D), BF16)],
        compiler_params=_params(("parallel",)),
    )(p5, p5, y_conv, y_gla, dmo, w_out)


def local_step(x, ctx, target, mod, wts, small, p_sh, chip, core):
    n = x.shape[0]
    t = n * SEQ
    t_all = t + n * NCTX
    x2 = x.reshape(t, D)
    ctx2 = ctx.reshape(n * NCTX, D)
    tgt2 = target.reshape(t, D)
    scale1, shift, gate = mod

    u = norm_mod_fwd(x2, ctx2, scale1, shift, small["norm_g"])
    p1, p2, p3, p4, p5, p_all = proj_all(u, [wts["w%d" % i] for i in range(1, 6)], [small["b%d" % i] for i in range(1, 6)],
                                         [t, t, t_all, t, t], p_sh, tm=512)
    p_full = jnp.stack([jnp.where(chip == i, p_sh, p_all[i]) for i in range(N_CHIPS)])
    wts = dict(wts, conv_proj=p_full[:, 0:256].reshape(D, D), gla_proj=p_full[:, 256:512].reshape(D, D),
               w_out=p_full[:, 512:768].reshape(D, D))

    aconv = conv_fwd(p1, small["conv_w"], small["conv_b"], n)
    ac, y_conv = ln_gate_proj(aconv, p2, small["conv_ln_g"], small["conv_ln_b"], wts["conv_proj"])

    qs, ks, vs, cum_f, cum_b = gla_prep_fwd(p3, small["upf"], small["upb"], small["bias_f"], small["bias_b"], n)
    o_f, s_f, sfin_f = gla_scan_fwd(qs, ks, vs, cum_f, rev=False, name="gla_scan_fwd_f")
    o_b, s_b, sfin_b = gla_scan_fwd(qs, ks, vs, cum_b, rev=True, name="gla_scan_fwd_b")
    og, y_gla = gla_out_proj(o_f, o_b, p4, small["gla_norm_g"], wts["gla_proj"])

    merged, dh, dmo, dgate, d_final_g, loss = merge_out_final(p5, y_conv, y_gla, wts["w_out"], x2, gate,
                                                              small["final_norm_g"], tgt2, n)

    g = {"final_norm_g": d_final_g}
    g["w_out"] = matmul_tn(merged, dmo, name="w_out_wgrad", t=t, tn=1024, tt=1024)[0]
    dyc, dyg, dp5 = out_dgrad_merge_bwd(p5, y_conv, y_gla, dmo, wts["w_out"])

    g["conv_proj"] = matmul_tn(ac, dyc, name="conv_proj_wgrad", t=t, tn=1024, tt=1024)[0]
    daconv, dp2, g["conv_ln_g"], g["conv_ln_b"] = ln_gate_bwd(aconv, p2, dyc, wts["conv_proj"], small["conv_ln_g"],
                                                               small["conv_ln_b"])
    dp1, dconv_w, dconv_b = conv_bwd(p1, daconv, small["conv_w"], n)
    g["conv_w"], g["conv_b"] = dconv_w, dconv_b

    g["gla_proj"] = matmul_tn(og, dyg, name="gla_proj_wgrad", t=t, tn=1024, tt=1024)[0]
    do, dp4, g["gla_norm_g"] = gla_out_bwd(o_f, o_b, p4, dyg, wts["gla_proj"], small["gla_norm_g"])
    g["proj"] = jnp.concatenate([g["conv_proj"].reshape(N_CHIPS, 256, D), g["gla_proj"].reshape(N_CHIPS, 256, D),
                                 g["w_out"].reshape(N_CHIPS, 256, D)], 1)
    dq_f, dk_f, dv_f, dc_f, gotp = gla_scan_bwd(qs, ks, vs, cum_f, s_f, sfin_f, do, rev=False, name="gla_scan_bwd_f",
                                                rider=("swap", g["proj"]))
    pap16 = pair_add(core, g["proj"], gotp, name="pair_add_p", tr=384)
    dq_b, dk_b, dv_b, dc_b, rbp = gla_scan_bwd(qs, ks, vs, cum_b, s_b, sfin_b, do, rev=True, name="gla_scan_bwd_b",
                                               rider=("exchange", pap16))
    dp3, g["upf"], g["upb"], g["bias_f"], g["bias_b"] = gla_prep_bwd(
        p3, dq_f, dq_b, dk_f, dk_b, dv_f, dv_b, dc_f, dc_b,
        small["upf"], small["upb"], small["bias_f"], small["bias_b"], n)

    dps = [dp1, dp2, dp3, dp4, dp5]
    got = {}
    for i in [0, 1, 3, 4, 2]:
        dp = dps[i]
        rows = dp.shape[0]
        tn = W3 if dp.shape[1] == W3 else 1024
        others = [j for j in range(5) if j != i]
        swap = [g["w%d" % (j + 1)] for j in others] if i == 2 else None
        outs = matmul_tn(u, dp, name="w_in_wgrad_%d" % (i + 1), t=rows, tn=tn, tt=1024 if rows % 1024 == 0 else 768,
                         colsum=True, swap=swap)
        g["w%d" % (i + 1)], g["b%d" % (i + 1)] = outs[0], outs[1]
        if swap is not None:
            got = dict(zip(others, outs[2:2 + len(others)]))
    g["gate"] = dgate
    return loss, dh, dps, g, got, (pap16, rbp)


def _group_cols(w):
    gv, gg, z = w[..., 0:1024], w[..., 1024:2048], w[..., 2048:3072]
    q, k, v = w[..., 3072:3584], w[..., 3584:4096], w[..., 4096:5120]
    ab = w[..., 5120:5152]
    r, mc, mg = w[..., 5152:6176], w[..., 6176:7200], w[..., 7200:8224]
    g1 = jnp.concatenate([p for j in range(CONV_NCB)
                          for p in (gv[..., CONV_CB * j:CONV_CB * (j + 1)], gg[..., CONV_CB * j:CONV_CB * (j + 1)])], -1)
    pad = jnp.zeros(w.shape[:-1] + (W3 - 2080,), w.dtype)
    g3 = jnp.concatenate([v, q, k, ab, pad], -1)
    return g1, z, g3, r, jnp.concatenate([mc, mg], -1)


def _ungroup_cols(g1, g2, g3, g4, g5):
    gv = jnp.concatenate([g1[..., 2 * CONV_CB * j:2 * CONV_CB * j + CONV_CB] for j in range(CONV_NCB)], -1)
    gg = jnp.concatenate([g1[..., 2 * CONV_CB * j + CONV_CB:2 * CONV_CB * (j + 1)] for j in range(CONV_NCB)], -1)
    v, q, k, ab = g3[..., 0:1024], g3[..., 1024:1536], g3[..., 1536:2048], g3[..., 2048:2080]
    return jnp.concatenate([gv, gg, g2, q, k, v, ab, g4, g5[..., 0:1024], g5[..., 1024:2048]], -1)


def _natural_pieces():
    pieces = [(CONV_CB * j, CONV_CB, 0, 2 * CONV_CB * j) for j in range(CONV_NCB)]
    pieces += [(1024 + CONV_CB * j, CONV_CB, 0, 2 * CONV_CB * j + CONV_CB) for j in range(CONV_NCB)]
    pieces += [(2048, 1024, 1, 0), (3072, 512, 2, O3_Q), (3584, 512, 2, O3_K), (4096, 1024, 2, O3_V), (5120, 32, 2, O3_AB),
               (5152, 1024, 3, 0), (6176, 1024, 4, 0), (7200, 1024, 4, 1024)]
    return sorted(pieces)


def _ungroup_to_shards(groups):
    shards = []
    for i in range(N_CHIPS):
        lo, hi = i * W_IN_SHARD, (i + 1) * W_IN_SHARD
        parts = []
        for nat, width, g, gcol in _natural_pieces():
            a, b = max(nat, lo), min(nat + width, hi)
            if a < b:
                parts.append(groups[g][:, gcol + a - nat:gcol + b - nat])
        shards.append(jnp.concatenate(parts, 1))
    return jnp.stack(shards)


def _pad_up(up, row0):
    return jnp.zeros((128, GLA_DK), F32).at[row0:row0 + up.shape[0]].set(up)


def _adamw_math(w, g, m, v):
    m = ADAM_B1 * m + (1.0 - ADAM_B1) * g
    v = ADAM_B2 * v + (1.0 - ADAM_B2) * (g * g)
    m_hat = m / (1.0 - ADAM_B1 ** ADAM_STEP)
    v_hat = v / (1.0 - ADAM_B2 ** ADAM_STEP)
    delta = -ADAM_LR * (m_hat / (jnp.sqrt(v_hat) + ADAM_EPS) + ADAM_WD * w)
    return delta, m, v


def adamw2d(w, g, m, v, *, name, tr, tcols=None):
    rows, cols = w.shape[-2:]

    def body(w_ref, g_ref, m_ref, v_ref, d_ref, nm_ref, nv_ref):
        d_ref[...], nm_ref[...], nv_ref[...] = _adamw_math(w_ref[...], g_ref[...], m_ref[...], v_ref[...])

    tcols = cols if tcols is None else tcols
    if w.ndim == 3:
        spec = pl.BlockSpec((1, tr, tcols), lambda i, j: (0, i, j))
    else:
        spec = pl.BlockSpec((tr, tcols), lambda i, j: (i, j))
    return _pallas(
        body, name=name, grid=(rows // tr, cols // tcols), in_specs=[spec] * 4, out_specs=[spec] * 3,
        out_shape=[jax.ShapeDtypeStruct(w.shape, F32)] * 3, compiler_params=_params(("parallel", "parallel")),
    )(w, g, m, v)


def adamw_many(ws, gs, ms, vs):
    k = len(ws)
    two = lambda a: a.reshape((-1, a.shape[-1]))

    def body(*refs):
        w_refs, g_refs, m_refs, v_refs = refs[:k], refs[k:2 * k], refs[2 * k:3 * k], refs[3 * k:4 * k]
        d_refs, nm_refs, nv_refs = refs[4 * k:5 * k], refs[5 * k:6 * k], refs[6 * k:7 * k]
        for i in range(k):
            d_refs[i][...], nm_refs[i][...], nv_refs[i][...] = _adamw_math(
                w_refs[i][...], g_refs[i][...], m_refs[i][...], v_refs[i][...])

    shapes = [jax.ShapeDtypeStruct(two(a).shape, F32) for a in ws]
    outs = _pallas(body, name="adamw_small", out_shape=shapes * 3, compiler_params=_params())(
        *[two(a) for a in ws], *[two(a) for a in gs], *[two(a) for a in ms], *[two(a) for a in vs])
    back = lambda lst: [o.reshape(a.shape) for o, a in zip(lst, ws)]
    return back(outs[:k]), back(outs[k:2 * k]), back(outs[2 * k:])


def sum_devices(sall, *, name):
    rows = sall.shape[1]

    def body(s_ref, o_ref):
        acc = s_ref[0]
        for d in range(1, N_DEV):
            acc = acc + s_ref[d]
        o_ref[...] = acc

    return _pallas(body, name=name, out_shape=jax.ShapeDtypeStruct((rows, D), F32),
                   compiler_params=_params())(sall)


def pair_add(core, g, got, *, name, tr):
    n, rows, cols = got.shape
    g4 = g.reshape(n, 2, rows, cols)

    def body(core_ref, g_ref, got_ref, ob_ref):
        del core_ref
        ob_ref[0] = (g_ref[0, 0] + got_ref[0]).astype(BF16)

    spec = pl.BlockSpec((1, tr, cols), lambda i, t, core_ref: (i, t, 0))
    return _pallas(
        body, name=name,
        grid_spec=pltpu.PrefetchScalarGridSpec(
            num_scalar_prefetch=1, grid=(n, rows // tr),
            in_specs=[pl.BlockSpec((1, 1, tr, cols), lambda i, t, core_ref: (i, core_ref[0], t, 0)), spec],
            out_specs=spec),
        out_shape=jax.ShapeDtypeStruct(got.shape, BF16),
        compiler_params=_params(("parallel", "parallel")))(core, g4, got)


def pair_add_groups(core, gs, gots, *, tr):
    k = len(gs)
    rows = gots[0].shape[0]

    def body(core_ref, *refs):
        del core_ref
        for i in range(k):
            refs[2 * k + i][...] = (refs[i][0] + refs[k + i][...]).astype(BF16)

    g_specs = [pl.BlockSpec((1, tr, a.shape[1]), lambda t, core_ref: (core_ref[0], t, 0)) for a in gots]
    r_specs = [pl.BlockSpec((tr, a.shape[1]), lambda t, core_ref: (t, 0)) for a in gots]
    return _pallas(
        body, name="pair_add_w",
        grid_spec=pltpu.PrefetchScalarGridSpec(num_scalar_prefetch=1, grid=(rows // tr,),
                                               in_specs=g_specs + r_specs, out_specs=r_specs),
        out_shape=[jax.ShapeDtypeStruct(a.shape, BF16) for a in gots],
        compiler_params=_params(("parallel",)))(core, *[a.reshape(2, rows, a.shape[1]) for a in gs], *gots)


def chip_add(place, pa, rb, *, name, tr):
    _, rows, cols = pa.shape

    def body(place_ref, m_ref, r_ref, o_ref):
        del place_ref
        o_ref[0] = ((m_ref[0].astype(F32) + r_ref[0].astype(F32)) + r_ref[1].astype(F32)) + r_ref[2].astype(F32)

    return _pallas(
        body, name=name,
        grid_spec=pltpu.PrefetchScalarGridSpec(
            num_scalar_prefetch=1, grid=(rows // tr,),
            in_specs=[pl.BlockSpec((1, tr, cols), lambda t, place_ref: (place_ref[0], t, 0)),
                      pl.BlockSpec((3, tr, cols), lambda t, place_ref: (0, t, 0))],
            out_specs=pl.BlockSpec((1, tr, cols), lambda t, place_ref: (place_ref[1], t, 0))),
        out_shape=jax.ShapeDtypeStruct((2, rows, cols), F32),
        compiler_params=_params(("parallel",)))(place, pa, rb)


def ada_bwd(call, cctx_rows, dm_shard, dm_full, adaw):
    nsh = adaw.shape[1]

    def body(c_ref, cc_ref, dms_ref, dmf_ref, w_ref, gw_ref, gb_ref, pq_ref):
        a_lat = _silu(c_ref[...])
        a_ctx = _silu(cc_ref[...])
        dms = dms_ref[...]
        gw_ref[...] = _tn(a_lat, dms[0:64], HI) + _tn(a_ctx, dms[64:72], HI)
        gb_ref[...] = jnp.sum(dmf_ref[...], axis=0, keepdims=True)
        part = _nt(dms[64:72], w_ref[...], HI)
        pq_ref[...] = jnp.zeros_like(pq_ref) + jnp.sum(part, axis=0, keepdims=True)

    return _pallas(body, name="ada_bwd",
                   out_shape=[jax.ShapeDtypeStruct((D, nsh), F32), jax.ShapeDtypeStruct((1, 3 * D), F32),
                              jax.ShapeDtypeStruct((8, D), F32)],
                   compiler_params=_params())(call, cctx_rows, dm_shard, dm_full, adaw)


def cctx_grad(pq_all, cctx_rows):
    def body(p_ref, c_ref, o_ref):
        acc = p_ref[0]
        for qi in range(1, N_CHIPS):
            acc = acc + p_ref[qi]
        o_ref[...] = acc * _dsilu(c_ref[...])

    return _pallas(body, name="cctx_grad", out_shape=jax.ShapeDtypeStruct((8, D), F32),
                   compiler_params=_params())(pq_all, cctx_rows)


def _place():
    x, y, c = lax.axis_index("x"), lax.axis_index("y"), lax.axis_index("c")
    chips = [(1 - x, y), (x, 1 - y), (1 - x, 1 - y)]
    return x, y, c, chips


def _all_peers(x, y, c):
    return [((1 - x) if r & 4 else x, (1 - y) if r & 2 else y, (1 - c) if r & 1 else c) for r in range(1, N_DEV)]


def _remote(src, dst, send_sem, recv_sem, dev):
    return pltpu.make_async_remote_copy(src_ref=src, dst_ref=dst, send_sem=send_sem, recv_sem=recv_sem,
                                        device_id=dev, device_id_type=MESH)


ANY = pl.BlockSpec(memory_space=pl.ANY)
VMEM = pl.BlockSpec(memory_space=pltpu.VMEM)
F_ROWS = 16


W_ROW_CHUNKS = 4
P_ROW_CHUNKS = 2
N_BULK = W_ROW_CHUNKS + P_ROW_CHUNKS


def _half_chunks(core, n_rows, align, which=(0, 1)):
    out = []
    for a, k in ((0, W_ROW_CHUNKS), (1, P_ROW_CHUNKS)):
        if a not in which:
            continue
        half = n_rows[a] // 2
        size = half // k
        for i in range(k):
            start = core * half + i * size
            out.append((a, pl.ds(start if isinstance(start, int) else pl.multiple_of(start, align), size)))
    return out


def gather_weights(c8, cctx8, adaw, adab, w_sh, fp):
    nsh = adaw.shape[1]

    def body(c_ref, cctx_ref, adaw_ref, adab_ref, w_ref, fp_ref, wall_ref, fall_ref, call_ref, mall_ref,
             abuf, w_send, w_recv, h_send, h_recv, c_send, c_recv, m_send, m_recv, f_send, f_recv):
        x, y, c, chips = _place()
        q = 2 * x + y
        dev = 4 * x + 2 * y + c
        qs = [2 * cx + cy for cx, cy in chips]
        sib = (x, y, 1 - c)
        srcs, dsts = (w_ref,), (wall_ref,)
        n_rows = (w_ref.shape[0],)
        mine = _half_chunks(c, n_rows, 16, which=(0,))
        other = _half_chunks(1 - c, n_rows, 16, which=(0,))

        bulk = [[_remote(srcs[a].at[rows], dsts[a].at[q, rows], w_send.at[j * N_BULK + i], w_recv.at[j * N_BULK + i],
                         (*chips[j], c)) for i, (a, rows) in enumerate(mine)] for j in range(3)]
        fall_ref[q] = fp_ref[...]
        small = [_remote(fp_ref, fall_ref.at[q], f_send.at[j], f_recv.at[j], (*chips[j], c)) for j in range(3)]
        my_rows = pl.ds(pl.multiple_of(8 * dev, 8), 8)
        call_ref[my_rows, :] = c_ref[...]
        cond = [_remote(c_ref, call_ref.at[my_rows, :], c_send.at[r], c_recv.at[r], peer)
                for r, peer in enumerate(_all_peers(x, y, c))]
        for cp in sum(bulk, []) + small + cond:
            cp.start()
        for cp in cond:
            cp.wait_recv()

        abuf[pl.ds(0, 64), :] = _silu(call_ref[...])
        abuf[pl.ds(64, 8), :] = _silu(cctx_ref[...])
        mall_ref[q] = _nn(abuf[...], adaw_ref[...], HI) + adab_ref[...]
        mod = [_remote(mall_ref.at[q], mall_ref.at[q], m_send.at[j], m_recv.at[j], (*chips[j], c)) for j in range(3)]
        for cp in mod:
            cp.start()

        handed = []
        for j in range(3):
            for i, (a, rows) in enumerate(mine):
                bulk[j][i].wait_recv()
                cp = _remote(dsts[a].at[qs[j], rows], dsts[a].at[qs[j], rows],
                             h_send.at[j * N_BULK + i], h_recv.at[j * N_BULK + i], sib)
                cp.start()
                handed.append(cp)
        for j in range(3):
            for i, (a, rows) in enumerate(other):
                _remote(dsts[a].at[qs[j], rows], dsts[a].at[qs[j], rows],
                        h_send.at[j * N_BULK + i], h_recv.at[j * N_BULK + i], sib).wait_recv()
        for cp in mod + small:
            cp.wait_recv()
        for cp in sum(bulk, []) + small + cond + mod + handed:
            cp.wait_send()

    def dma(n):
        return pltpu.SemaphoreType.DMA((n,))

    return _pallas(
        body, name="gather_weights",
        in_specs=[VMEM, VMEM, VMEM, VMEM, ANY, VMEM],
        out_specs=[ANY, VMEM, VMEM, VMEM],
        out_shape=[jax.ShapeDtypeStruct((N_CHIPS,) + w_sh.shape, BF16),
                   jax.ShapeDtypeStruct((N_CHIPS, F_ROWS, D), F32),
                   jax.ShapeDtypeStruct((8 * N_DEV, D), F32), jax.ShapeDtypeStruct((N_CHIPS, MOD_ROWS, nsh), F32)],
        scratch_shapes=[pltpu.VMEM((MOD_ROWS, D), F32), dma(3 * N_BULK), dma(3 * N_BULK), dma(3 * N_BULK), dma(3 * N_BULK),
                        dma(7), dma(7), dma(3), dma(3), dma(3), dma(3)],
        compiler_params=_params(),
    )(c8, cctx8, adaw, adab, w_sh, fp)


def _pair_count(gs, gp):
    return len(gs) * W_ROW_CHUNKS + (0 if gp is None else N_CHIPS * P_ROW_CHUNKS)


def _pair_got_shapes(gs, gp):
    shapes = [jax.ShapeDtypeStruct((D // 2, a.shape[1]), F32) for a in gs]
    if gp is not None:
        shapes.append(jax.ShapeDtypeStruct((N_CHIPS, gp.shape[1] // 2, gp.shape[2]), F32))
    return shapes


def _pair_copies(g_refs, gp_ref, got_refs, gotp_ref, a_send, a_recv):
    x, y, c, _ = _place()
    sib = (x, y, 1 - c)
    pair = []
    half, size = D // 2, D // 2 // W_ROW_CHUNKS
    for gi in range(len(g_refs)):
        for i in range(W_ROW_CHUNKS):
            k = len(pair)
            rows_o = pl.ds(pl.multiple_of((1 - c) * half + i * size, 8), size)
            pair.append(_remote(g_refs[gi].at[rows_o], got_refs[gi].at[pl.ds(i * size, size)],
                                a_send.at[k], a_recv.at[k], sib))
    if gp_ref is not None:
        half, size = gp_ref.shape[1] // 2, gp_ref.shape[1] // 2 // P_ROW_CHUNKS
        for s in range(N_CHIPS):
            for i in range(P_ROW_CHUNKS):
                k = len(pair)
                rows_o = pl.ds(pl.multiple_of((1 - c) * half + i * size, 8), size)
                pair.append(_remote(gp_ref.at[s, rows_o], gotp_ref.at[s, pl.ds(i * size, size)],
                                    a_send.at[k], a_recv.at[k], sib))
    return pair


def pair_swap(gs, sm):
    n_gs = len(gs)

    def body(*refs):
        g_refs, sm_ref = refs[:n_gs], refs[n_gs]
        got_refs, sall_ref = refs[n_gs + 1:2 * n_gs + 1], refs[2 * n_gs + 1]
        a_send, a_recv, s_send, s_recv = refs[2 * n_gs + 2:]
        x, y, c, _ = _place()
        dev = 4 * x + 2 * y + c
        pair = _pair_copies(g_refs, None, got_refs, None, a_send, a_recv)
        sall_ref[dev] = sm_ref[...]
        small = [_remote(sm_ref, sall_ref.at[dev], s_send.at[r], s_recv.at[r], peer)
                 for r, peer in enumerate(_all_peers(x, y, c))]
        for cp in pair + small:
            cp.start()
        for cp in small + pair:
            cp.wait_recv()
        for cp in small + pair:
            cp.wait_send()

    return _pallas(
        body, name="pair_swap", in_specs=[ANY] * n_gs + [VMEM], out_specs=[ANY] * n_gs + [VMEM],
        out_shape=_pair_got_shapes(gs, None) + [jax.ShapeDtypeStruct((N_DEV,) + sm.shape, F32)],
        scratch_shapes=[pltpu.SemaphoreType.DMA((_pair_count(gs, None),)), pltpu.SemaphoreType.DMA((_pair_count(gs, None),)),
                        pltpu.SemaphoreType.DMA((N_DEV - 1,)), pltpu.SemaphoreType.DMA((N_DEV - 1,))],
        compiler_params=_params(),
    )(*gs, sm)


def gather_small(sm):
    rows = sm.shape[0]

    def body(sm_ref, sall_ref, s_send, s_recv):
        x, y, c, _ = _place()
        dev = 4 * x + 2 * y + c
        sall_ref[dev] = sm_ref[...]
        small = [_remote(sm_ref, sall_ref.at[dev], s_send.at[r], s_recv.at[r], peer)
                 for r, peer in enumerate(_all_peers(x, y, c))]
        for cp in small:
            cp.start()
        for cp in small:
            cp.wait_recv()
        for cp in small:
            cp.wait_send()

    return _pallas(
        body, name="gather_small", in_specs=[VMEM], out_specs=VMEM,
        out_shape=jax.ShapeDtypeStruct((N_DEV, rows, D), F32),
        scratch_shapes=[pltpu.SemaphoreType.DMA((7,)), pltpu.SemaphoreType.DMA((7,))],
        compiler_params=_params(),
    )(sm)


def pair_share(ghw, ghp, pq):
    def body(ghw_ref, ghp_ref, pq_ref, outw_ref, outp_ref, pqa_ref, send, recv, p_send, p_recv):
        del ghw_ref, ghp_ref
        x, y, c, chips = _place()
        q = 2 * x + y
        refs = (outw_ref, outp_ref)
        n_rows = (2 * outw_ref.shape[1], 2 * outp_ref.shape[1])
        pair = [_remote(refs[a].at[c, rows], refs[a].at[c, rows], send.at[i], recv.at[i], (x, y, 1 - c))
                for i, (a, rows) in enumerate(_half_chunks(0, n_rows, 8))]
        pqa_ref[q] = pq_ref[...]
        small = [_remote(pq_ref, pqa_ref.at[q], p_send.at[j], p_recv.at[j], (*chips[j], c)) for j in range(3)]
        for cp in pair + small:
            cp.start()
        for i, (a, rows) in enumerate(_half_chunks(0, n_rows, 8)):
            _remote(refs[a].at[1 - c, rows], refs[a].at[1 - c, rows], send.at[i], recv.at[i], (x, y, 1 - c)).wait_recv()
        for cp in small:
            cp.wait_recv()
        for cp in pair + small:
            cp.wait_send()

    return _pallas(
        body, name="pair_share", in_specs=[ANY, ANY, VMEM], out_specs=[ANY, ANY, VMEM],
        out_shape=[jax.ShapeDtypeStruct(ghw.shape, F32), jax.ShapeDtypeStruct(ghp.shape, F32),
                   jax.ShapeDtypeStruct((N_CHIPS, 8, D), F32)],
        scratch_shapes=[pltpu.SemaphoreType.DMA((N_BULK,)), pltpu.SemaphoreType.DMA((N_BULK,)),
                        pltpu.SemaphoreType.DMA((3,)), pltpu.SemaphoreType.DMA((3,))],
        input_output_aliases={0: 0, 1: 1},
        compiler_params=_params(),
    )(ghw, ghp, pq)


def _rows_of(shape):
    size = 1
    for s in shape:
        size *= s
    return -(-size // D)


def _pack(arrs, rows_multiple=8):
    parts = []
    total = 0
    for a in arrs:
        f = a.reshape(-1).astype(F32)
        r = _rows_of(a.shape)
        parts.append(jnp.pad(f, (0, r * D - f.shape[0])))
        total += r
    pad_rows = (-total) % rows_multiple
    if pad_rows:
        parts.append(jnp.zeros((pad_rows * D,), F32))
    return jnp.concatenate(parts).reshape(-1, D)


def _unpack(p, shapes):
    out = []
    r0 = 0
    for shp in shapes:
        r = _rows_of(shp)
        size = 1
        for s in shp:
            size *= s
        out.append(p[r0:r0 + r].reshape(-1)[:size].reshape(shp))
        r0 += r
    return out


WEIGHT_NAMES = ['c_ctx', 'ada_w', 'ada_b', 'norm_g', 'w_in', 'b_in', 'conv_w', 'conv_b', 'conv_ln_g', 'conv_ln_b',
                'conv_proj', 'decay_up_fwd', 'decay_bias_fwd', 'decay_up_bwd', 'decay_bias_bwd', 'gla_norm_g', 'gla_proj',
                'w_out', 'final_norm_g']
SMALL_NAMES = ['c_ctx', 'ada_b', 'norm_g', 'b_in', 'conv_w', 'conv_b', 'conv_ln_g', 'conv_ln_b', 'decay_up_fwd',
               'decay_bias_fwd', 'decay_up_bwd', 'decay_bias_bwd', 'gla_norm_g', 'final_norm_g']


def kernel(x, c, ctx, c_ctx, ada_w, ada_b, norm_g, w_in, b_in, conv_w, conv_b, conv_ln_g, conv_ln_b, conv_proj, decay_up_fwd, decay_bias_fwd, decay_up_bwd, decay_bias_bwd, gla_norm_g, gla_proj, w_out, final_norm_g, loss_target, m_c_ctx, m_ada_w, m_ada_b, m_norm_g, m_w_in, m_b_in, m_conv_w, m_conv_b, m_conv_ln_g, m_conv_ln_b, m_conv_proj, m_decay_up_fwd, m_decay_bias_fwd, m_decay_up_bwd, m_decay_bias_bwd, m_gla_norm_g, m_gla_proj, m_w_out, m_final_norm_g, v_c_ctx, v_ada_w, v_ada_b, v_norm_g, v_w_in, v_b_in, v_conv_w, v_conv_b, v_conv_ln_g, v_conv_ln_b, v_conv_proj, v_decay_up_fwd, v_decay_bias_fwd, v_decay_up_bwd, v_decay_bias_bwd, v_gla_norm_g, v_gla_proj, v_w_out, v_final_norm_g):
    w = dict(c_ctx=c_ctx, ada_w=ada_w, ada_b=ada_b, norm_g=norm_g, w_in=w_in, b_in=b_in, conv_w=conv_w, conv_b=conv_b,
             conv_ln_g=conv_ln_g, conv_ln_b=conv_ln_b, conv_proj=conv_proj, decay_up_fwd=decay_up_fwd,
             decay_bias_fwd=decay_bias_fwd, decay_up_bwd=decay_up_bwd, decay_bias_bwd=decay_bias_bwd,
             gla_norm_g=gla_norm_g, gla_proj=gla_proj, w_out=w_out, final_norm_g=final_norm_g)
    m = dict(c_ctx=m_c_ctx, ada_w=m_ada_w, ada_b=m_ada_b, norm_g=m_norm_g, w_in=m_w_in, b_in=m_b_in, conv_w=m_conv_w,
             conv_b=m_conv_b, conv_ln_g=m_conv_ln_g, conv_ln_b=m_conv_ln_b, conv_proj=m_conv_proj,
             decay_up_fwd=m_decay_up_fwd, decay_bias_fwd=m_decay_bias_fwd, decay_up_bwd=m_decay_up_bwd,
             decay_bias_bwd=m_decay_bias_bwd, gla_norm_g=m_gla_norm_g, gla_proj=m_gla_proj, w_out=m_w_out,
             final_norm_g=m_final_norm_g)
    v = dict(c_ctx=v_c_ctx, ada_w=v_ada_w, ada_b=v_ada_b, norm_g=v_norm_g, w_in=v_w_in, b_in=v_b_in, conv_w=v_conv_w,
             conv_b=v_conv_b, conv_ln_g=v_conv_ln_g, conv_ln_b=v_conv_ln_b, conv_proj=v_conv_proj,
             decay_up_fwd=v_decay_up_fwd, decay_bias_fwd=v_decay_bias_fwd, decay_up_bwd=v_decay_up_bwd,
             decay_bias_bwd=v_decay_bias_bwd, gla_norm_g=v_gla_norm_g, gla_proj=v_gla_proj, w_out=v_w_out,
             final_norm_g=v_final_norm_g)
    n = x.shape[0]
    ax, ay, ac = lax.axis_index("x"), lax.axis_index("y"), lax.axis_index("c")
    q = 2 * ax + ay
    dev = 4 * ax + 2 * ay + ac
    nsh = ada_w.shape[2]

    w_sh = w_in[0].astype(BF16)
    p_sh = jnp.concatenate([conv_proj[0], gla_proj[0], w_out[0]], 0).astype(BF16)
    fp = _pack([conv_w[0], decay_up_fwd[0], decay_up_bwd[0]], F_ROWS)
    c8 = jnp.pad(c, ((0, 8 - n), (0, 0)))
    cctx8 = jnp.pad(c_ctx[None], ((0, 7), (0, 0)))
    adab_sh = lax.dynamic_slice(ada_b, (0, q * nsh), (1, nsh))
    w_all, fall, call, mall = gather_weights(c8, cctx8, ada_w[0], adab_sh, w_sh, fp)

    mod_all = jnp.transpose(mall, (1, 0, 2)).reshape(MOD_ROWS, 3 * D)
    mod_mine = lax.dynamic_slice(mod_all, (8 * dev, 0), (n, 3 * D))
    mod_ctx = mod_all[64:65]
    shift = jnp.concatenate([mod_mine[:, 0:D], mod_ctx[:, 0:D]], 0)[:, None, :]
    scale1 = 1.0 + jnp.concatenate([mod_mine[:, D:2 * D], mod_ctx[:, D:2 * D]], 0)[:, None, :]
    gate = mod_mine[:, 2 * D:3 * D][:, None, :]

    own = lambda i, mine, got: jnp.where(q == i, mine, got)
    g1, g2, g3, g4, g5 = _group_cols(jnp.concatenate([own(i, w_sh, w_all[i]) for i in range(N_CHIPS)], 1))
    wts = dict(w1=g1, w2=g2, w3=g3, w4=g4, w5=g5)
    f_parts = [_unpack(fall[i], [conv_w.shape[1:], decay_up_fwd.shape[1:], decay_up_bwd.shape[1:]]) for i in range(N_CHIPS)]
    conv_w_full = jnp.concatenate([p[0] for p in f_parts], 1)
    upf_full = jnp.concatenate([p[1] for p in f_parts], 1)
    upb_full = jnp.concatenate([p[2] for p in f_parts], 1)
    b1, b2, b3, b4, b5 = _group_cols(b_in)
    small = dict(b1=b1, b2=b2, b3=b3, b4=b4, b5=b5, norm_g=norm_g,
                 conv_w=jnp.pad(conv_w_full, ((0, 1), (0, 0))), conv_b=conv_b, conv_ln_g=conv_ln_g, conv_ln_b=conv_ln_b,
                 upf=_split3(_pad_up(upf_full, 0)), upb=_split3(_pad_up(upb_full, 16)),
                 bias_f=decay_bias_fwd, bias_b=decay_bias_bwd,
                 gla_norm_g=gla_norm_g, final_norm_g=final_norm_g[None])

    core = ac.astype(jnp.int32).reshape(1)
    chip = q.astype(jnp.int32).reshape(1)
    loss_part, dh, dps, g, got, (pap16, rbp) = local_step(x, ctx, loss_target, (scale1, shift, gate), wts, small,
                                                          p_sh, q, core)
    loss = lax.psum(loss_part[0, 0], ("x", "y", "c"))

    gs = [g["w%d" % i] for i in range(1, 6)]
    d_b_in = _ungroup_cols(*[g["b%d" % i] for i in range(1, 6)])
    early = [d_b_in, g["conv_b"].sum(0), g["conv_ln_g"], g["conv_ln_b"], g["bias_f"], g["bias_b"],
             g["gla_norm_g"], g["final_norm_g"], g["conv_w"].sum(0)[:CONV_K], g["upf"][0:16], g["upb"][16:32]]
    early_shapes = [a.shape for a in early]
    got[2], sall1 = pair_swap([gs[2]], _pack(early))
    halves = pair_add_groups(core, gs, [got[i] for i in range(5)], tr=128)
    paw16 = _ungroup_to_shards(halves)
    grad_x2, dshift, dscale, g["norm_g"], rbw = dgrad_norm_bwd(
        dps, [wts["w%d" % i] for i in range(1, 6)], paw16, x.reshape(n * SEQ, D), ctx.reshape(n * NCTX, D), dh,
        scale1, norm_g, tm=256)

    dm_mine = jnp.concatenate([dshift[:n, 0], dscale[:n, 0], g["gate"][:, 0]], -1)
    dm_ctx = jnp.concatenate([dshift[n, 0], dscale[n, 0], jnp.zeros((D,), F32)], -1)
    late = [g["norm_g"], dm_mine, dm_ctx]
    late_shapes = [a.shape for a in late]
    sall2 = gather_small(_pack(late))
    (s_b_in, s_conv_b, s_ln_g, s_ln_b, s_bias_f, s_bias_b, s_gla_g, s_final_g, s_conv_w, s_upf,
     s_upb) = _unpack(sum_devices(sall1, name="sum_devices_early"), early_shapes)
    s_norm_g = _unpack(sum_devices(sall2, name="sum_devices_late"), late_shapes)[0]
    r_mine, r_ctx = 1, 1 + 3 * n
    dm_all = sall2[:, r_mine:r_ctx].reshape(N_DEV, n, 3 * D)
    dm_full = jnp.concatenate([jnp.pad(dm_all, ((0, 0), (0, 8 - n), (0, 0))).reshape(8 * N_DEV, 3 * D),
                               sall2[:, r_ctx:r_ctx + 3].reshape(N_DEV, 3 * D)], 0)
    dm_shard = lax.dynamic_slice(dm_full, (0, q * nsh), (MOD_ROWS, nsh))
    cctx_rows = jnp.broadcast_to(c_ctx[None], (8, D))
    g_ada_w, g_ada_b, pq = ada_bwd(call, cctx_rows, dm_shard, dm_full, ada_w[0])

    place = jnp.concatenate([chip, core])
    ghw = chip_add(place, paw16, rbw, name="chip_add_w", tr=128)
    ghp = chip_add(place, pap16, rbp, name="chip_add_p", tr=384)
    gw_mine, gp_mine, pq_all = pair_share(ghw, ghp, pq)
    gp_mine = gp_mine.reshape(768, D)
    g_c_ctx = cctx_grad(pq_all, cctx_rows)[0]

    grads = dict(
        c_ctx=g_c_ctx, ada_w=g_ada_w[None], ada_b=g_ada_b, norm_g=s_norm_g,
        w_in=gw_mine.reshape(1, D, W_IN_SHARD), b_in=s_b_in,
        conv_w=lax.dynamic_slice(s_conv_w, (0, q * 256), (CONV_K, 256))[None], conv_b=s_conv_b,
        conv_ln_g=s_ln_g, conv_ln_b=s_ln_b, conv_proj=gp_mine[0:256][None],
        decay_up_fwd=lax.dynamic_slice(s_upf, (0, q * 128), (16, 128))[None], decay_bias_fwd=s_bias_f,
        decay_up_bwd=lax.dynamic_slice(s_upb, (0, q * 128), (16, 128))[None], decay_bias_bwd=s_bias_b,
        gla_norm_g=s_gla_g, gla_proj=gp_mine[256:512][None], w_out=gp_mine[512:768][None],
        final_norm_g=s_final_g[0])

    delta, new_m, new_v = {}, {}, {}
    for name in ["ada_w", "conv_proj", "gla_proj", "w_out"]:
        delta[name], new_m[name], new_v[name] = adamw2d(w[name], grads[name].reshape(w[name].shape), m[name], v[name],
                                                        name="adamw_" + name, tr=128)
    tr_ = lambda a: jnp.swapaxes(a, 1, 2)
    g_w_in_t = tr_(grads["w_in"])
    grads["w_in"] = tr_(g_w_in_t)
    d_, m_, v_ = adamw2d(tr_(w_in), g_w_in_t, tr_(m_w_in), tr_(v_w_in), name="adamw_w_in", tr=W_IN_SHARD, tcols=128)
    delta["w_in"], new_m["w_in"], new_v["w_in"] = tr_(d_), tr_(m_), tr_(v_)
    d_, m_, v_ = adamw_many([w[nm] for nm in SMALL_NAMES], [grads[nm].reshape(w[nm].shape) for nm in SMALL_NAMES],
                            [m[nm] for nm in SMALL_NAMES], [v[nm] for nm in SMALL_NAMES])
    for nm, a, b, cc in zip(SMALL_NAMES, d_, m_, v_):
        delta[nm], new_m[nm], new_v[nm] = a, b, cc

    grad_x = grad_x2.reshape(x.shape)
    return (loss, grad_x, *[grads[nm].reshape(w[nm].shape) for nm in WEIGHT_NAMES], *[delta[nm] for nm in WEIGHT_NAMES],
            *[new_m[nm] for nm in WEIGHT_NAMES], *[new_v[nm] for nm in WEIGHT_NAMES])
```

```python
import jax
import jax.numpy as jnp
from jax import lax
from jax.experimental import pallas as pl
from jax.experimental.pallas import tpu as pltpu

F32 = jnp.float32
BF16 = jnp.bfloat16
MESH = pl.DeviceIdType.MESH
HI = lax.Precision.HIGHEST

D = 1024
SEQ = 2048
GRID_W = 64
GRID_H = SEQ // GRID_W
NCTX = 256
SEQ_ALL = SEQ + NCTX
EPS = 1e-6
CONV_K = 31
CONV_PAD = CONV_K // 2
HEADS = 4
HEAD_K = 128
HEAD_V = 256
GLA_DK = HEADS * HEAD_K
GATE_TAU = 16.0
Q_SCALE = HEAD_K ** -0.5
CHUNK = 64
NCHUNK = SEQ_ALL // CHUNK
NCHUNK_LAT = SEQ // CHUNK
NCHUNK_CTX = NCHUNK - NCHUNK_LAT
SUB = 64
NSUB = CHUNK // SUB
N_IN = 8224
W3 = 2176
O3_V, O3_Q, O3_K, O3_AB = 0, 1024, 1536, 2048

ADAM_LR, ADAM_B1, ADAM_B2, ADAM_EPS, ADAM_WD, ADAM_STEP = 0.001, 0.9, 0.999, 1e-08, 0.01, 10
VMEM_LIMIT = 56 * 1024 * 1024

N_CHIPS = 4
N_DEV = 8
W_IN_SHARD = N_IN // N_CHIPS
MOD_ROWS = 72


def _pallas(body, **kw):
    return pl.pallas_call(body, **kw)


def _params(sem=None, **kw):
    if sem is not None:
        kw["dimension_semantics"] = sem
    return pltpu.CompilerParams(vmem_limit_bytes=VMEM_LIMIT, **kw)


def _sigmoid(v):
    return 1.0 / (1.0 + jnp.exp(-v))


def _silu(v):
    return v * _sigmoid(v)


def _dsilu(v):
    s = _sigmoid(v)
    return s * (1.0 + v * (1.0 - s))


def _log_sigmoid(v):
    return jnp.minimum(v, 0.0) - jnp.log(1.0 + jnp.exp(-jnp.abs(v)))


def _dot(a, b, dims, precision=None):
    return lax.dot_general(a, b, (dims, ((), ())), preferred_element_type=F32, precision=precision)


def _nn(a, b, precision=None):
    return _dot(a, b, ((1,), (0,)), precision)


def _nt(a, b, precision=None):
    return _dot(a, b, ((1,), (1,)), precision)


def _tn(a, b, precision=None):
    return _dot(a, b, ((0,), (0,)), precision)


def _b16(v):
    return v.astype(BF16)


def proj_all(u, ws, bs, rows, p_sh, *, tm):
    k = u.shape[1]
    n_g = len(ws)
    tns = [w.shape[1] if w.shape[1] % 1024 else 1024 for w in ws]
    mts = [r // tm for r in rows]
    cnts = [(w.shape[1] // tn) * mt for w, tn, mt in zip(ws, tns, mts)]
    los = [sum(cnts[:g]) for g in range(n_g)]
    n_steps = sum(cnts)

    def rel(s, g):
        return jnp.clip(s - los[g], 0, cnts[g] - 1)

    def active(s, g):
        return (s >= los[g]) & (s < los[g] + cnts[g])

    def u_row(s):
        r = 0
        for g in range(n_g):
            r = r + jnp.where(active(s, g), rel(s, g) % mts[g], 0)
        return r

    def body(*refs):
        u_ref = refs[0]
        w_refs, b_refs = refs[1:1 + n_g], refs[1 + n_g:1 + 2 * n_g]
        p_ref = refs[1 + 2 * n_g]
        o_refs = refs[2 + 2 * n_g:2 + 3 * n_g]
        pall_ref = refs[2 + 3 * n_g]
        w_send, w_recv, h_send, h_recv = refs[3 + 3 * n_g:]
        s = pl.program_id(0)
        for g in range(n_g):
            @pl.when(active(s, g))
            def _(g=g):
                o_refs[g][...] = (_nn(u_ref[...], w_refs[g][...]) + b_refs[g][...]).astype(o_refs[g].dtype)

        x, y, c, chips = _place()
        q = 2 * x + y
        mine = _half_chunks(c, (0, p_ref.shape[0]), 16, which=(1,))
        other = _half_chunks(1 - c, (0, p_ref.shape[0]), 16, which=(1,))
        nb = len(mine)

        def bulk():
            return [[_remote(p_ref.at[rws], pall_ref.at[q, rws], w_send.at[pj * nb + pi], w_recv.at[pj * nb + pi],
                             (*chips[pj], c)) for pi, (_, rws) in enumerate(mine)] for pj in range(3)]

        @pl.when(s == 0)
        def _():
            for cp in sum(bulk(), []):
                cp.start()

        @pl.when(s == n_steps - 1)
        def _():
            handed = []
            for pj, (cx, cy) in enumerate(chips):
                for pi, (_, rws) in enumerate(mine):
                    bulk()[pj][pi].wait_recv()
                    cp = _remote(pall_ref.at[2 * cx + cy, rws], pall_ref.at[2 * cx + cy, rws],
                                 h_send.at[pj * nb + pi], h_recv.at[pj * nb + pi], (x, y, 1 - c))
                    cp.start()
                    handed.append(cp)
            for pj, (cx, cy) in enumerate(chips):
                for pi, (_, rws) in enumerate(other):
                    _remote(pall_ref.at[2 * cx + cy, rws], pall_ref.at[2 * cx + cy, rws],
                            h_send.at[pj * nb + pi], h_recv.at[pj * nb + pi], (x, y, 1 - c)).wait_recv()
            for cp in sum(bulk(), []) + handed:
                cp.wait_send()

    any_spec = pl.BlockSpec(memory_space=pl.ANY)
    in_specs = [pl.BlockSpec((tm, k), lambda s: (u_row(s), 0))]
    in_specs += [pl.BlockSpec((k, tns[g]), lambda s, g=g: (0, rel(s, g) // mts[g])) for g in range(n_g)]
    in_specs += [pl.BlockSpec((1, tns[g]), lambda s, g=g: (0, rel(s, g) // mts[g])) for g in range(n_g)]
    in_specs.append(any_spec)
    out_specs = [pl.BlockSpec((tm, tns[g]), lambda s, g=g: (rel(s, g) % mts[g], rel(s, g) // mts[g])) for g in range(n_g)]
    out_specs.append(any_spec)
    out_shape = [jax.ShapeDtypeStruct((rows[g], ws[g].shape[1]), BF16) for g in range(n_g)]
    out_shape.append(jax.ShapeDtypeStruct((N_CHIPS,) + p_sh.shape, p_sh.dtype))
    return _pallas(
        body, name="proj_all", grid=(n_steps,), in_specs=in_specs, out_specs=out_specs, out_shape=out_shape,
        scratch_shapes=[pltpu.SemaphoreType.DMA((3 * P_ROW_CHUNKS,)) for _ in range(4)],
        compiler_params=_params(("arbitrary",)),
    )(u, *ws, *bs, p_sh)


def matmul_tn(a, b, *, name, t, tn, tt, colsum=False, swap=None):
    m = a.shape[1]
    n = b.shape[1]
    nj, ns = n // tn, t // tt
    n_out = 2 if colsum else 1
    n_sw = 0 if swap is None else len(swap)

    def body(a_ref, b_ref, *rest):
        o_ref = rest[n_sw]
        cs_ref = rest[n_sw + 1] if colsum else None
        j, s = pl.program_id(0), pl.program_id(1)

        if swap is not None:
            g_refs = rest[:n_sw]
            got_refs = rest[n_sw + n_out:2 * n_sw + n_out]
            sems = rest[2 * n_sw + n_out:]

            @pl.when((j == 0) & (s == 0))
            def _():
                for cp in _pair_copies(g_refs, None, got_refs, None, *sems):
                    cp.start()

            @pl.when((j == nj - 1) & (s == ns - 1))
            def _():
                for cp in _pair_copies(g_refs, None, got_refs, None, *sems):
                    cp.wait_recv()
                for cp in _pair_copies(g_refs, None, got_refs, None, *sems):
                    cp.wait_send()

        @pl.when(s == 0)
        def _():
            o_ref[...] = jnp.zeros_like(o_ref)
            if colsum:
                cs_ref[...] = jnp.zeros_like(cs_ref)
        o_ref[...] += _tn(a_ref[...], b_ref[...])
        if colsum:
            cs_ref[...] += jnp.sum(b_ref[...].astype(F32), axis=0, keepdims=True)

    in_specs = [pl.BlockSpec((tt, m), lambda j, s: (s, 0)), pl.BlockSpec((tt, tn), lambda j, s: (s, j))]
    out_specs = [pl.BlockSpec((m, tn), lambda j, s: (0, j))]
    out_shape = [jax.ShapeDtypeStruct((m, n), F32)]
    if colsum:
        out_specs.append(pl.BlockSpec((1, tn), lambda j, s: (0, j)))
        out_shape.append(jax.ShapeDtypeStruct((1, n), F32))
    args, scratch = [a, b], []
    if swap is not None:
        any_spec = pl.BlockSpec(memory_space=pl.ANY)
        in_specs += [any_spec] * n_sw
        out_specs += [any_spec] * n_sw
        out_shape += _pair_got_shapes(swap, None)
        args += list(swap)
        scratch = [pltpu.SemaphoreType.DMA((_pair_count(swap, None),)), pltpu.SemaphoreType.DMA((_pair_count(swap, None),))]
    return _pallas(
        body, name=name, grid=(nj, ns), in_specs=in_specs, out_specs=out_specs, out_shape=out_shape,
        scratch_shapes=scratch,
        compiler_params=_params(("parallel" if swap is None else "arbitrary", "arbitrary")),
    )(*args)


def dgrad_norm_bwd(dps, wts, paw, x2, ctx2, dh, scale1, norm_g, *, tm):
    t, tc = x2.shape[0], ctx2.shape[0]
    t_all = t + tc
    n_lat, n_ctx = t // tm, tc // tm
    n_tiles = n_lat + n_ctx
    n_samples = scale1.shape[0] - 1
    tps = n_lat // n_samples
    n_grp = n_samples + 1
    n_g = len(dps)
    whole = [g for g in range(n_g) if dps[g].shape[0] == t_all]
    latent = [g for g in range(n_g) if dps[g].shape[0] != t_all]

    def body(*refs):
        dp_refs, w_refs = refs[:n_g], refs[n_g:2 * n_g]
        (paw_ref, x_ref, c_ref, dh_ref, sc_ref, g_ref, dx_ref, dsh_ref, dsc_ref, dg_ref, rbw_ref,
         du_buf, b_send, b_recv) = refs[2 * n_g:]
        i = pl.program_id(0)

        def exchange():
            x, y, c, chips = _place()
            chunks = _half_chunks(0, (2 * paw_ref.shape[1],), 16, which=(0,))
            return [_remote(paw_ref.at[2 * cx + cy, rows], rbw_ref.at[j, rows],
                            b_send.at[j * N_BULK + k], b_recv.at[j * N_BULK + k], (cx, cy, c))
                    for j, (cx, cy) in enumerate(chips) for k, (_, rows) in enumerate(chunks)]

        @pl.when(i == 0)
        def _():
            for cp in exchange():
                cp.start()

        acc = None
        for g in whole:
            part = _nt(dp_refs[g][...], w_refs[g][...])
            acc = part if acc is None else acc + part
        du_buf[...] = acc

        @pl.when(i < n_lat)
        def _():
            lat = None
            for g in latent:
                part = _nt(dp_refs[g][...], w_refs[g][...])
                lat = part if lat is None else lat + part
            du_buf[...] += lat

        duv = du_buf[...]
        xv = jnp.where(i < n_lat, x_ref[...], c_ref[...])
        rs = lax.rsqrt(jnp.mean(xv * xv, axis=-1, keepdims=True) + EPS)
        xh = xv * rs
        n = xh * g_ref[...]
        dn = duv * sc_ref[0]
        dxh = dn * g_ref[...]
        dx = rs * (dxh - xh * jnp.mean(dxh * xh, axis=-1, keepdims=True))

        @pl.when(i < n_lat)
        def _():
            dx_ref[...] = dx + dh_ref[...]

        @pl.when((i % tps == 0) & (i <= n_lat))
        def _():
            dsh_ref[...] = jnp.zeros_like(dsh_ref)
            dsc_ref[...] = jnp.zeros_like(dsc_ref)

        @pl.when(i == 0)
        def _():
            dg_ref[...] = jnp.zeros_like(dg_ref)

        dsh_ref[0] += jnp.sum(duv, axis=0, keepdims=True)
        dsc_ref[0] += jnp.sum(duv * n, axis=0, keepdims=True)
        dg_ref[...] += jnp.sum(dn * xh, axis=0, keepdims=True)

        @pl.when(i == n_tiles - 1)
        def _():
            for cp in exchange():
                cp.wait_recv()
            for cp in exchange():
                cp.wait_send()

    lat = lambda i: (jnp.minimum(i, n_lat - 1), 0)
    grp = lambda i: (jnp.minimum(i // tps, n_samples), 0, 0)
    in_specs = []
    for g, dp in enumerate(dps):
        nrow = dp.shape[0] // tm
        in_specs.append(pl.BlockSpec((tm, dp.shape[1]), lambda i, nrow=nrow: (jnp.minimum(i, nrow - 1), 0)))
    for w in wts:
        in_specs.append(pl.BlockSpec(w.shape, lambda i: (0, 0), pipeline_mode=pl.Buffered(1)))
    any_spec = pl.BlockSpec(memory_space=pl.ANY)
    in_specs += [any_spec,
                 pl.BlockSpec((tm, D), lat), pl.BlockSpec((tm, D), lambda i: (jnp.maximum(i - n_lat, 0), 0)),
                 pl.BlockSpec((tm, D), lat), pl.BlockSpec((1, 1, D), grp), pl.BlockSpec((1, D), lambda i: (0, 0))]
    return _pallas(
        body, name="dgrad_norm_bwd", grid=(n_tiles,), in_specs=in_specs,
        out_specs=[pl.BlockSpec((tm, D), lat), pl.BlockSpec((1, 1, D), grp), pl.BlockSpec((1, 1, D), grp),
                   pl.BlockSpec((1, D), lambda i: (0, 0)), any_spec],
        out_shape=[jax.ShapeDtypeStruct((t, D), F32), jax.ShapeDtypeStruct((n_grp, 1, D), F32),
                   jax.ShapeDtypeStruct((n_grp, 1, D), F32), jax.ShapeDtypeStruct((1, D), F32),
                   jax.ShapeDtypeStruct((3,) + paw.shape[1:], paw.dtype)],
        scratch_shapes=[pltpu.VMEM((tm, D), F32), pltpu.SemaphoreType.DMA((3 * N_BULK,)),
                        pltpu.SemaphoreType.DMA((3 * N_BULK,))],
        compiler_params=_params(("arbitrary",)),
    )(*dps, *wts, paw, x2, ctx2, dh, scale1, norm_g)


TM_NORM = 512


def norm_mod_fwd(x2, ctx2, scale1, shift, norm_g):
    t = x2.shape[0]
    n_lat = t // TM_NORM
    assert ctx2.shape[0] == TM_NORM
    n_samples = scale1.shape[0] - 1
    tps = n_lat // n_samples

    def body(x_ref, c_ref, sc_ref, sh_ref, g_ref, u_ref):
        i = pl.program_id(0)
        xv = jnp.where(i < n_lat, x_ref[...], c_ref[...])
        rs = lax.rsqrt(jnp.mean(xv * xv, axis=-1, keepdims=True) + EPS)
        u = xv * rs * g_ref[...] * sc_ref[0] + sh_ref[0]
        u_ref[...] = u.astype(u_ref.dtype)

    grp = lambda i: (jnp.minimum(i // tps, n_samples), 0, 0)
    return _pallas(
        body, name="norm_mod_fwd", grid=(n_lat + 1,),
        in_specs=[pl.BlockSpec((TM_NORM, D), lambda i: (jnp.minimum(i, n_lat - 1), 0)),
                  pl.BlockSpec((TM_NORM, D), lambda i: (0, 0)),
                  pl.BlockSpec((1, 1, D), grp), pl.BlockSpec((1, 1, D), grp),
                  pl.BlockSpec((1, D), lambda i: (0, 0))],
        out_specs=pl.BlockSpec((TM_NORM, D), lambda i: (i, 0)),
        out_shape=jax.ShapeDtypeStruct((t + TM_NORM, D), BF16),
        compiler_params=_params(("parallel",)),
    )(x2, ctx2, scale1, shift, norm_g)


CONV_CB = 256
CONV_NCB = D // CONV_CB
H_OFF = 16


H_CB = 128
H_SPAN = GRID_W + 2 * H_OFF - 8


def _conv_scratch(vertical):
    if vertical:
        return [pltpu.VMEM((GRID_H + 2 * CONV_PAD, GRID_W, CONV_CB), F32)]
    return [pltpu.VMEM((GRID_H, GRID_W + 2 * H_OFF, H_CB), F32), pltpu.VMEM((7, GRID_H, H_SPAN, H_CB), F32)]


def _conv_fill(bufs, img, vertical):
    pad_ref = bufs[0]
    pad_ref[...] = jnp.zeros_like(pad_ref)
    if vertical:
        pad_ref[pl.ds(CONV_PAD, GRID_H)] = img
        return
    pad_ref[:, pl.ds(H_OFF, GRID_W), :] = img

    def shift(r, carry):
        for s in range(1, 8):
            bufs[1][s - 1, r] = pad_ref[r, pl.ds(s, H_SPAN), :]
        return carry

    lax.fori_loop(0, GRID_H, shift, 0)


def _conv_window(bufs, k, vertical, r, w0=0, nw=GRID_W, lanes=slice(None)):
    if vertical:
        return bufs[0][r + k, pl.ds(w0, nw), lanes]
    off = H_OFF - CONV_PAD + k
    if off % 8 == 0:
        return bufs[0][r, pl.ds(off + w0, nw), lanes]
    return bufs[1][off % 8 - 1, r, pl.ds(off - off % 8 + w0, nw), lanes]


def _conv_col_blocks(vertical):
    if vertical:
        return [pl.ds(0, CONV_CB)]
    return [pl.ds(i * H_CB, H_CB) for i in range(CONV_CB // H_CB)]


def _rows(r):
    return pl.ds(pl.multiple_of(r * GRID_W, GRID_W), GRID_W)


def conv_fwd(p1, conv_w, conv_b, n_samples):
    t = n_samples * SEQ

    def make(vertical, prev):
        n_buf = len(_conv_scratch(vertical))

        def body(gv_ref, gg_ref, w_ref, b_ref, *rest):
            o_ref, bufs = rest[-1 - n_buf], rest[-n_buf:]
            for cols in _conv_col_blocks(vertical):
                a = gv_ref[:, cols].astype(F32) * _sigmoid(gg_ref[:, cols].astype(F32))
                _conv_fill(bufs, a.reshape(GRID_H, GRID_W, a.shape[-1]), vertical)

                def row(r, carry, cols=cols):
                    acc = jnp.zeros((GRID_W, cols.size), F32) + b_ref[:, cols]
                    for k in range(CONV_K):
                        acc = acc + _conv_window(bufs, k, vertical, r) * w_ref[pl.ds(k, 1), cols]
                    o_ref[_rows(r), cols] = acc
                    return carry

                lax.fori_loop(0, GRID_H, row, 0)

        cb0 = CONV_NCB // 2 if vertical else 0
        in_specs = [pl.BlockSpec((SEQ, CONV_CB), lambda b, j: (b, 2 * (cb0 + j))),
                    pl.BlockSpec((SEQ, CONV_CB), lambda b, j: (b, 2 * (cb0 + j) + 1)),
                    pl.BlockSpec((CONV_K + 1, CONV_CB), lambda b, j: (0, cb0 + j)),
                    pl.BlockSpec((1, CONV_CB), lambda b, j: (0, cb0 + j))]
        args = [p1, p1, conv_w, conv_b]
        aliases = {}
        if prev is not None:
            in_specs.append(pl.BlockSpec(memory_space=pl.ANY))
            args.append(prev)
            aliases = {4: 0}
        return _pallas(
            body, name="conv_fwd_v" if vertical else "conv_fwd_h", grid=(n_samples, CONV_NCB // 2),
            in_specs=in_specs,
            out_specs=pl.BlockSpec((SEQ, CONV_CB), lambda b, j: (b, cb0 + j)),
            out_shape=jax.ShapeDtypeStruct((t, D), F32),
            scratch_shapes=_conv_scratch(vertical),
            input_output_aliases=aliases,
            compiler_params=_params(("parallel", "parallel")),
        )(*args)

    return make(True, make(False, None))


def conv_bwd(p1, daconv, conv_w, n_samples):
    t = n_samples * SEQ

    def make(vertical, prev):
        n_buf = len(_conv_scratch(vertical))

        def body(gv_ref, gg_ref, dy_ref, w_ref, *rest):
            dp_ref, dw_ref, db_ref = rest[-3 - 2 * n_buf - 1:-2 * n_buf - 1]
            a_bufs, d_bufs, da_ref = rest[-2 * n_buf - 1:-n_buf - 1], rest[-n_buf - 1:-1], rest[-1]
            for cols in _conv_col_blocks(vertical):
                width = cols.size
                gv = gv_ref[:, cols].astype(F32)
                sg = _sigmoid(gg_ref[:, cols].astype(F32))
                _conv_fill(a_bufs, (gv * sg).reshape(GRID_H, GRID_W, width), vertical)
                _conv_fill(d_bufs, dy_ref[:, cols].reshape(GRID_H, GRID_W, width), vertical)

                def row(r, carry, cols=cols, width=width):
                    acc = jnp.zeros((GRID_W, width), F32)
                    for k in range(CONV_K):
                        acc = acc + _conv_window(d_bufs, CONV_K - 1 - k, vertical, r) * w_ref[pl.ds(k, 1), cols]
                    da_ref[_rows(r), cols] = acc
                    return carry

                lax.fori_loop(0, GRID_H, row, 0)
                da = da_ref[:, cols]
                dp_ref[:, pl.ds(cols.start, width)] = (da * sg).astype(dp_ref.dtype)
                dp_ref[:, pl.ds(CONV_CB + cols.start, width)] = (da * gv * sg * (1.0 - sg)).astype(dp_ref.dtype)

                for lb in range(width // 128):
                    lanes = pl.ds(lb * 128, 128)
                    dy_lanes = pl.ds(cols.start + lb * 128, 128)

                    def wrow(r, accs, lanes=lanes, dy_lanes=dy_lanes):
                        for w0 in range(0, GRID_W, 8):
                            dyv = dy_ref[pl.ds(pl.multiple_of(r * GRID_W, GRID_W) + w0, 8), dy_lanes]
                            accs = tuple(accs[k] + _conv_window(a_bufs, k, vertical, r, w0, 8, lanes) * dyv
                                         for k in range(CONV_K))
                        return accs

                    accs = lax.fori_loop(0, GRID_H, wrow, tuple(jnp.zeros((8, 128), F32) for _ in range(CONV_K)))
                    for k in range(CONV_K):
                        dw_ref[0, pl.ds(k, 1), dy_lanes] = jnp.sum(accs[k], axis=0, keepdims=True)
            dw_ref[0, pl.ds(CONV_K, 1), :] = jnp.zeros((1, CONV_CB), F32)
            db_ref[0] = jnp.sum(dy_ref[...], axis=0, keepdims=True)

        cb0 = CONV_NCB // 2 if vertical else 0
        in_specs = [pl.BlockSpec((SEQ, CONV_CB), lambda b, j: (b, 2 * (cb0 + j))),
                    pl.BlockSpec((SEQ, CONV_CB), lambda b, j: (b, 2 * (cb0 + j) + 1)),
                    pl.BlockSpec((SEQ, CONV_CB), lambda b, j: (b, cb0 + j)),
                    pl.BlockSpec((CONV_K + 1, CONV_CB), lambda b, j: (0, cb0 + j))]
        args = [p1, p1, daconv, conv_w]
        aliases = {}
        if prev is not None:
            in_specs += [pl.BlockSpec(memory_space=pl.ANY)] * 3
            args += list(prev)
            aliases = {4: 0, 5: 1, 6: 2}
        return _pallas(
            body, name="conv_bwd_v" if vertical else "conv_bwd_h", grid=(n_samples, CONV_NCB // 2),
            in_specs=in_specs,
            out_specs=[pl.BlockSpec((SEQ, 2 * CONV_CB), lambda b, j: (b, cb0 + j)),
                       pl.BlockSpec((1, CONV_K + 1, CONV_CB), lambda b, j: (b, 0, cb0 + j)),
                       pl.BlockSpec((1, 1, CONV_CB), lambda b, j: (b, 0, cb0 + j))],
            out_shape=[jax.ShapeDtypeStruct((t, 2 * D), BF16),
                       jax.ShapeDtypeStruct((n_samples, CONV_K + 1, D), F32),
                       jax.ShapeDtypeStruct((n_samples, 1, D), F32)],
            scratch_shapes=_conv_scratch(vertical) + _conv_scratch(vertical) + [pltpu.VMEM((SEQ, CONV_CB), F32)],
            input_output_aliases=aliases,
            compiler_params=_params(("parallel", "parallel")),
        )(*args)

    return make(True, make(False, None))


TM_EW = 256


def ln_gate_proj(aconv, z, ln_g, ln_b, conv_proj):
    t = aconv.shape[0]

    def body(a_ref, z_ref, g_ref, b_ref, w_ref, o_ref, y_ref):
        a = a_ref[...]
        mu = jnp.mean(a, axis=-1, keepdims=True)
        xc = a - mu
        rstd = lax.rsqrt(jnp.mean(xc * xc, axis=-1, keepdims=True) + EPS)
        l = xc * rstd * g_ref[...] + b_ref[...]
        ac = _b16(_silu(l) * _silu(z_ref[...].astype(F32)))
        o_ref[...] = ac
        y_ref[...] = _nn(ac, w_ref[...]).astype(y_ref.dtype)

    row = pl.BlockSpec((TM_OUT, D), lambda i: (i, 0))
    vec = pl.BlockSpec((1, D), lambda i: (0, 0))
    return _pallas(
        body, name="ln_gate_proj", grid=(t // TM_OUT,),
        in_specs=[row, row, vec, vec, pl.BlockSpec((D, D), lambda i: (0, 0))], out_specs=[row, row],
        out_shape=[jax.ShapeDtypeStruct((t, D), BF16), jax.ShapeDtypeStruct((t, D), BF16)],
        compiler_params=_params(("parallel",)),
    )(aconv, z, ln_g, ln_b, conv_proj)


def ln_gate_bwd(aconv, z, dyc, conv_proj, ln_g, ln_b, ac):
    t = aconv.shape[0]

    def body(a_ref, z_ref, d_ref, w_ref, g_ref, b_ref, ac_ref, da_ref, dz_ref, dg_ref, db_ref, gw_ref):
        @pl.when(pl.program_id(0) == 0)
        def _():
            gw_ref[...] = jnp.zeros_like(gw_ref)

        gw_ref[...] += _tn(ac_ref[...], d_ref[...])
        a = a_ref[...]
        zv = z_ref[...].astype(F32)
        dac_v = _nt(d_ref[...], w_ref[...])
        mu = jnp.mean(a, axis=-1, keepdims=True)
        xc = a - mu
        rstd = lax.rsqrt(jnp.mean(xc * xc, axis=-1, keepdims=True) + EPS)
        xh = xc * rstd
        l = xh * g_ref[...] + b_ref[...]
        dz_ref[...] = (dac_v * _silu(l) * _dsilu(zv)).astype(dz_ref.dtype)
        dl = dac_v * _silu(zv) * _dsilu(l)
        dxh = dl * g_ref[...]
        da_ref[...] = rstd * (dxh - jnp.mean(dxh, axis=-1, keepdims=True)
                              - xh * jnp.mean(dxh * xh, axis=-1, keepdims=True))

        @pl.when(pl.program_id(0) == 0)
        def _():
            dg_ref[...] = jnp.zeros_like(dg_ref)
            db_ref[...] = jnp.zeros_like(db_ref)

        dg_ref[...] += jnp.sum(dl * xh, axis=0, keepdims=True)
        db_ref[...] += jnp.sum(dl, axis=0, keepdims=True)

    row = pl.BlockSpec((TM_EW, D), lambda i: (i, 0))
    vec = pl.BlockSpec((1, D), lambda i: (0, 0))
    return _pallas(
        body, name="ln_gate_bwd", grid=(t // TM_EW,),
        in_specs=[row, row, row, pl.BlockSpec((D, D), lambda i: (0, 0)), vec, vec, row],
        out_specs=[row, row, vec, vec, pl.BlockSpec((D, D), lambda i: (0, 0))],
        out_shape=[jax.ShapeDtypeStruct((t, D), F32), jax.ShapeDtypeStruct((t, D), BF16),
                   jax.ShapeDtypeStruct((1, D), F32), jax.ShapeDtypeStruct((1, D), F32),
                   jax.ShapeDtypeStruct((D, D), F32)],
        compiler_params=_params(("arbitrary",)),
    )(aconv, z, dyc, conv_proj, ln_g, ln_b, ac)


TM_PREP = 256
PREP_LAT = SEQ // TM_PREP
PREP_ALL = SEQ_ALL // TM_PREP


def _chunk_tri(n, upper):
    r = lax.broadcasted_iota(jnp.int32, (n, n), 0)
    c = lax.broadcasted_iota(jnp.int32, (n, n), 1)
    same = (r // CHUNK) == (c // CHUNK)
    keep = (c >= r) if upper else (c <= r)
    return jnp.where(same & keep, 1.0, 0.0).astype(F32)


def _split3(v):
    hi = v.astype(BF16)
    r1 = v - hi.astype(F32)
    mid = r1.astype(BF16)
    lo = (r1 - mid.astype(F32)).astype(BF16)
    return jnp.stack([hi, mid, lo])


def _chunk_sums(v, upper):
    tri = _chunk_tri(v.shape[0], upper).astype(BF16)
    pieces = _split3(v)
    return (_nn(tri, pieces[0]) + _nn(tri, pieces[1])) + _nn(tri, pieces[2])


def _gate_logits(ab, up3_ref, bias_ref):
    assert ab.dtype == BF16
    return ((_nn(ab, up3_ref[0]) + _nn(ab, up3_ref[1])) + _nn(ab, up3_ref[2])) + bias_ref[...]


def _prep_tile_maps(n_samples):
    n_lat = n_samples * PREP_LAT

    def seq_map(i):
        return jnp.where(i < n_lat, i // PREP_LAT, i - n_lat), jnp.where(i < n_lat, i % PREP_LAT, PREP_LAT)

    return n_lat, seq_map


def gla_prep_fwd(p3, upf, upb, bias_f, bias_b, n_samples):
    n_lat, seq_map = _prep_tile_maps(n_samples)
    n_tiles = n_lat + n_samples

    def body(v_ref, q_ref, k_ref, ab_ref, upf_ref, upb_ref, bf_ref, bb_ref, qo, ko, vo, cf, cb):
        i = pl.program_id(0)
        qo[0] = jnp.where(i < n_lat, q_ref[...].astype(F32) * Q_SCALE, 0.0)
        ko[0] = k_ref[...]
        vo[0] = v_ref[...]
        ab = ab_ref[...]
        gf = _log_sigmoid(_gate_logits(ab, upf_ref, bf_ref)) * (1.0 / GATE_TAU)
        gb = _log_sigmoid(_gate_logits(ab, upb_ref, bb_ref)) * (1.0 / GATE_TAU)
        cf[0] = _chunk_sums(gf, False)
        cb[0] = _chunk_sums(gb, True)

    def o_spec(w):
        return pl.BlockSpec((1, TM_PREP, w), lambda i: (*seq_map(i), 0))

    full = lambda shape: pl.BlockSpec(shape, lambda i: (0,) * len(shape))
    return _pallas(
        body, name="gla_prep_fwd", grid=(n_tiles,),
        in_specs=[pl.BlockSpec((TM_PREP, 1024), lambda i: (i, O3_V // 1024)),
                  pl.BlockSpec((TM_PREP, 512), lambda i: (i, O3_Q // 512)),
                  pl.BlockSpec((TM_PREP, 512), lambda i: (i, O3_K // 512)),
                  pl.BlockSpec((TM_PREP, 128), lambda i: (i, O3_AB // 128)),
                  full((3, 128, GLA_DK)), full((3, 128, GLA_DK)), full((1, GLA_DK)), full((1, GLA_DK))],
        out_specs=[o_spec(GLA_DK), o_spec(GLA_DK), o_spec(D), o_spec(GLA_DK), o_spec(GLA_DK)],
        out_shape=[jax.ShapeDtypeStruct((n_samples, SEQ_ALL, GLA_DK), F32),
                   jax.ShapeDtypeStruct((n_samples, SEQ_ALL, GLA_DK), p3.dtype),
                   jax.ShapeDtypeStruct((n_samples, SEQ_ALL, D), p3.dtype),
                   jax.ShapeDtypeStruct((n_samples, SEQ_ALL, GLA_DK), F32),
                   jax.ShapeDtypeStruct((n_samples, SEQ_ALL, GLA_DK), F32)],
        compiler_params=_params(("parallel",)),
    )(p3, p3, p3, p3, upf, upb, bias_f, bias_b)


def gla_prep_bwd(p3, dq_f, dq_b, dk_f, dk_b, dv_f, dv_b, dc_f, dc_b, upf, upb, bias_f, bias_b, n_samples):
    n_lat, seq_map = _prep_tile_maps(n_samples)
    n_tiles = n_lat + n_samples

    def body(ab_ref, dqf, dqb, dkf, dkb, dvf, dvb, dcf, dcb, upf_ref, upb_ref, bf_ref, bb_ref,
             dp_ref, duf_ref, dub_ref, dbf_ref, dbb_ref):
        i = pl.program_id(0)
        both = lambda a, b: a[0].astype(F32) + b[0].astype(F32)
        dp_ref[:, pl.ds(O3_V, D)] = both(dvf, dvb).astype(dp_ref.dtype)
        dq = jnp.where(i < n_lat, both(dqf, dqb) * Q_SCALE, 0.0)
        dp_ref[:, pl.ds(O3_Q, GLA_DK)] = dq.astype(dp_ref.dtype)
        dp_ref[:, pl.ds(O3_K, GLA_DK)] = both(dkf, dkb).astype(dp_ref.dtype)
        ab = ab_ref[...]
        zf = _gate_logits(ab, upf_ref, bf_ref)
        zb = _gate_logits(ab, upb_ref, bb_ref)
        dgf = _chunk_sums(dcf[0], True)
        dgb = _chunk_sums(dcb[0], False)
        dzf = _b16(dgf * (1.0 / GATE_TAU) * _sigmoid(-zf))
        dzb = _b16(dgb * (1.0 / GATE_TAU) * _sigmoid(-zb))
        dab = _nt(dzf, upf_ref[0]) + _nt(dzb, upb_ref[0])
        dp_ref[:, pl.ds(O3_AB, 128)] = dab.astype(dp_ref.dtype)

        @pl.when(i == 0)
        def _():
            duf_ref[...] = jnp.zeros_like(duf_ref)
            dub_ref[...] = jnp.zeros_like(dub_ref)
            dbf_ref[...] = jnp.zeros_like(dbf_ref)
            dbb_ref[...] = jnp.zeros_like(dbb_ref)

        duf_ref[...] += _tn(ab, dzf)
        dub_ref[...] += _tn(ab, dzb)
        dbf_ref[...] += jnp.sum(dzf.astype(F32), axis=0, keepdims=True)
        dbb_ref[...] += jnp.sum(dzb.astype(F32), axis=0, keepdims=True)

    def s_spec(w):
        return pl.BlockSpec((1, TM_PREP, w), lambda i: (*seq_map(i), 0))

    full = lambda shape: pl.BlockSpec(shape, lambda i: (0,) * len(shape))
    return _pallas(
        body, name="gla_prep_bwd", grid=(n_tiles,),
        in_specs=[pl.BlockSpec((TM_PREP, 128), lambda i: (i, O3_AB // 128)),
                  s_spec(GLA_DK), s_spec(GLA_DK), s_spec(GLA_DK), s_spec(GLA_DK), s_spec(D), s_spec(D),
                  s_spec(GLA_DK), s_spec(GLA_DK),
                  full((3, 128, GLA_DK)), full((3, 128, GLA_DK)), full((1, GLA_DK)), full((1, GLA_DK))],
        out_specs=[pl.BlockSpec((TM_PREP, W3), lambda i: (i, 0)),
                   full((128, GLA_DK)), full((128, GLA_DK)), full((1, GLA_DK)), full((1, GLA_DK))],
        out_shape=[jax.ShapeDtypeStruct((n_tiles * TM_PREP, W3), BF16),
                   jax.ShapeDtypeStruct((128, GLA_DK), F32), jax.ShapeDtypeStruct((128, GLA_DK), F32),
                   jax.ShapeDtypeStruct((1, GLA_DK), F32), jax.ShapeDtypeStruct((1, GLA_DK), F32)],
        compiler_params=_params(("arbitrary",)),
    )(p3, dq_f, dq_b, dk_f, dk_b, dv_f, dv_b, dc_f, dc_b, upf, upb, bias_f, bias_b)


def _sub_blocks(rev):
    if NSUB == 1:
        return [((0, CHUNK), CHUNK // 2, (0, CHUNK))]
    out = []
    for s in range(NSUB):
        rows = (s * SUB, SUB)
        if rev:
            ref = (s + 1) * SUB if s < NSUB - 1 else None
            cols = (s * SUB, CHUNK - s * SUB)
        else:
            ref = s * SUB - 1 if s > 0 else None
            cols = (0, (s + 1) * SUB)
        out.append((rows, ref, cols))
    return out


def _sub_mask(rows, cols, rev):
    r = rows[0] + lax.broadcasted_iota(jnp.int32, (rows[1], cols[1]), 0)
    c = cols[0] + lax.broadcasted_iota(jnp.int32, (rows[1], cols[1]), 1)
    return (c >= r) if rev else (c <= r)


def _sub_operands(qc, kc, cc, rows, ref, cols):
    cref = jnp.zeros((1, HEAD_K), F32) if ref is None else cc[ref:ref + 1]
    eq = jnp.exp(cc[rows[0]:rows[0] + rows[1]] - cref)
    ek = jnp.exp(cref - cc[cols[0]:cols[0] + cols[1]])
    qs = qc[rows[0]:rows[0] + rows[1]] * eq
    kk = kc[cols[0]:cols[0] + cols[1]] * ek
    return qs, kk, eq, ek


SCAN_ROWS = 256
SCAN_CHUNKS = SCAN_ROWS // CHUNK
SCAN_STEPS = SEQ_ALL // SCAN_ROWS
LAT_BLOCKS = SEQ // SCAN_ROWS


def _scan_block(t, rev):
    if rev:
        return SCAN_STEPS - 1 - t
    return jnp.where(t == 0, SCAN_STEPS - 1, t - 1)


def _scan_lat_block(t, rev):
    if rev:
        return jnp.minimum(SCAN_STEPS - 1 - t, LAT_BLOCKS - 1)
    return jnp.maximum(t - 1, 0)


def _head_cols(h):
    return pl.ds(h * HEAD_K, HEAD_K), pl.ds(h * HEAD_V, HEAD_V)


def gla_scan_fwd(q, k, v, cum, *, rev, name):
    n = q.shape[0]

    def body(q_ref, k_ref, v_ref, c_ref, o_ref, s_ref, sfin_ref, st):
        t = pl.program_id(1)

        @pl.when(t == 0)
        def _():
            st[...] = jnp.zeros_like(st)

        def chunk(j, carry):
            lj = SCAN_CHUNKS - 1 - j if rev else j
            r0 = lj * CHUNK
            rws = pl.ds(r0, CHUNK)
            for h in range(HEADS):
                kcols, vcols = _head_cols(h)
                qc, kc, cc = q_ref[0, rws, kcols], k_ref[0, rws, kcols], c_ref[0, rws, kcols]
                vc = v_ref[0, rws, vcols]
                s_in = st[h]
                s_ref[0, h, j] = _b16(s_in)
                edge = cc[0:1] if rev else cc[CHUNK - 1:CHUNK]
                ke = kc * jnp.exp(edge - cc)
                st[h] = s_in * jnp.exp(edge) + _tn(_b16(vc), _b16(ke))
                o_inter = _nt(_b16(qc * jnp.exp(cc)), _b16(s_in))
                vb = _b16(vc)
                for rows, ref, cols in _sub_blocks(rev):
                    qs, kk, _, _ = _sub_operands(qc, kc, cc, rows, ref, cols)
                    a = jnp.where(_sub_mask(rows, cols, rev), _nt(_b16(qs), _b16(kk)), 0.0)
                    o_s = _nn(_b16(a), vb[cols[0]:cols[0] + cols[1]])
                    o_ref[0, pl.ds(r0 + rows[0], rows[1]), vcols] = _b16(o_inter[rows[0]:rows[0] + rows[1]] + o_s)
            return carry

        for j in range(SCAN_CHUNKS):
            chunk(j, 0)

        @pl.when(t == SCAN_STEPS - 1)
        def _():
            sfin_ref[0] = st[...]

    def spec(w):
        return pl.BlockSpec((1, SCAN_ROWS, w), lambda b, t: (b, _scan_block(t, rev), 0))

    return _pallas(
        body, name=name, grid=(n, SCAN_STEPS),
        in_specs=[spec(GLA_DK), spec(GLA_DK), spec(D), spec(GLA_DK)],
        out_specs=[pl.BlockSpec((1, SCAN_ROWS, D), lambda b, t: (b, _scan_lat_block(t, rev), 0)),
                   pl.BlockSpec((1, HEADS, SCAN_CHUNKS, HEAD_V, HEAD_K), lambda b, t: (b, 0, t, 0, 0)),
                   pl.BlockSpec((1, HEADS, HEAD_V, HEAD_K), lambda b, t: (b, 0, 0, 0))],
        out_shape=[jax.ShapeDtypeStruct((n, SEQ, D), BF16),
                   jax.ShapeDtypeStruct((n, HEADS, NCHUNK, HEAD_V, HEAD_K), BF16),
                   jax.ShapeDtypeStruct((n, HEADS, HEAD_V, HEAD_K), F32)],
        scratch_shapes=[pltpu.VMEM((HEADS, HEAD_V, HEAD_K), F32)],
        compiler_params=_params(("parallel", "arbitrary")),
    )(q, k, v, cum)


def gla_scan_bwd(q, k, v, cum, s_all, s_fin, do, *, rev, name, rider=None):
    n = q.shape[0]

    def body(q_ref, k_ref, v_ref, c_ref, s_ref, sfin_ref, do_ref, *rest):
        if rider is not None:
            ride_in, rest = rest[0], rest[1:]
        dq_ref, dk_ref, dv_ref, dc_ref = rest[:4]
        if rider is not None:
            ride_out, rest = rest[4], rest[:4] + rest[5:]
        dst, s_next, dq_acc, dk_acc, dv_acc = rest[4:9]
        t = SCAN_STEPS - 1 - pl.program_id(1)

        if rider is not None:
            def copies():
                send, recv = rest[9], rest[10]
                if rider[0] == "swap":
                    return _pair_copies([], ride_in, [], ride_out, send, recv)
                x, y, c, chips = _place()
                return [_remote(ride_in.at[2 * cx + cy, rows], ride_out.at[pj, rows], send.at[pj * P_ROW_CHUNKS + pi],
                                recv.at[pj * P_ROW_CHUNKS + pi], (cx, cy, c))
                        for pj, (cx, cy) in enumerate(chips)
                        for pi, (_, rows) in enumerate(_half_chunks(0, (0, 2 * ride_in.shape[1]), 16, which=(1,)))]

            @pl.when((pl.program_id(0) == 0) & (pl.program_id(1) == 0))
            def _():
                for cp in copies():
                    cp.start()

            @pl.when((pl.program_id(0) == n - 1) & (pl.program_id(1) == SCAN_STEPS - 1))
            def _():
                for cp in copies():
                    cp.wait_recv()
                for cp in copies():
                    cp.wait_send()

        @pl.when(pl.program_id(1) == 0)
        def _():
            dst[...] = jnp.zeros_like(dst)
            s_next[...] = sfin_ref[0]

        def chunk(jj, carry):
            j = SCAN_CHUNKS - 1 - jj
            lj = SCAN_CHUNKS - 1 - j if rev else j
            rws = pl.ds(lj * CHUNK, CHUNK)
            for h in range(HEADS):
                kcols, vcols = _head_cols(h)
                qc, kc, cc = q_ref[0, rws, kcols], k_ref[0, rws, kcols], c_ref[0, rws, kcols]
                vc = v_ref[0, rws, vcols]
                doc = jnp.where(t > 0, do_ref[0, rws, vcols], 0.0)
                s_in = s_ref[0, h, j]
                s_out = s_next[h]
                ds_out = dst[h]
                edge = cc[0:1] if rev else cc[CHUNK - 1:CHUNK]
                e_q = jnp.exp(cc)
                e_k = jnp.exp(edge - cc)
                dob = _b16(doc)
                dsb = _b16(ds_out)
                dst[h] = ds_out * jnp.exp(edge) + _tn(dob, _b16(qc * e_q))
                s_next[h] = s_in.astype(F32)
                dq_acc[h] = e_q * _nn(dob, s_in)
                dk_acc[h] = e_k * _nn(_b16(vc), dsb)
                dv_acc[h] = _nt(_b16(kc * e_k), dsb)
                vb = _b16(vc)
                for rows, ref, cols in _sub_blocks(rev):
                    qs, kk, eq, ek = _sub_operands(qc, kc, cc, rows, ref, cols)
                    mask = _sub_mask(rows, cols, rev)
                    rsl = slice(rows[0], rows[0] + rows[1])
                    csl = pl.ds(cols[0], cols[1])
                    qsb, kkb = _b16(qs), _b16(kk)
                    a = jnp.where(mask, _nt(qsb, kkb), 0.0)
                    da = _b16(jnp.where(mask, _nt(dob[rsl], vb[cols[0]:cols[0] + cols[1]]), 0.0))
                    dq_acc[h, pl.ds(rows[0], rows[1]), :] += _nn(da, kkb) * eq
                    dk_acc[h, csl, :] += _tn(da, qsb) * ek
                    dv_acc[h, csl, :] += _tn(_b16(a), dob[rsl])
                dq = dq_acc[h]
                dk = dk_acc[h]
                dc = qc * dq - kc * dk
                bnd = jnp.sum(ds_out * s_out, axis=0, keepdims=True)
                edge_row = 0 if rev else CHUNK - 1
                is_edge = lax.broadcasted_iota(jnp.int32, (CHUNK, HEAD_K), 0) == edge_row
                dq_ref[0, rws, kcols] = _b16(dq)
                dk_ref[0, rws, kcols] = _b16(dk)
                dv_ref[0, rws, vcols] = _b16(dv_acc[h])
                dc_ref[0, rws, kcols] = dc + jnp.where(is_edge, bnd, 0.0)
            return carry

        for jj in range(SCAN_CHUNKS):
            chunk(jj, 0)

    def step_of(u):
        return SCAN_STEPS - 1 - u

    def spec(w):
        return pl.BlockSpec((1, SCAN_ROWS, w), lambda b, u: (b, _scan_block(step_of(u), rev), 0))

    in_specs = [spec(GLA_DK), spec(GLA_DK), spec(D), spec(GLA_DK),
                pl.BlockSpec((1, HEADS, SCAN_CHUNKS, HEAD_V, HEAD_K), lambda b, u: (b, 0, step_of(u), 0, 0)),
                pl.BlockSpec((1, HEADS, HEAD_V, HEAD_K), lambda b, u: (b, 0, 0, 0)),
                pl.BlockSpec((1, SCAN_ROWS, D), lambda b, u: (b, _scan_lat_block(step_of(u), rev), 0))]
    out_specs = [spec(GLA_DK), spec(GLA_DK), spec(D), spec(GLA_DK)]
    out_shape = [jax.ShapeDtypeStruct((n, SEQ_ALL, GLA_DK), BF16), jax.ShapeDtypeStruct((n, SEQ_ALL, GLA_DK), BF16),
                 jax.ShapeDtypeStruct((n, SEQ_ALL, D), BF16), jax.ShapeDtypeStruct((n, SEQ_ALL, GLA_DK), F32)]
    scratch = [pltpu.VMEM((HEADS, HEAD_V, HEAD_K), F32), pltpu.VMEM((HEADS, HEAD_V, HEAD_K), F32),
               pltpu.VMEM((HEADS, CHUNK, HEAD_K), F32), pltpu.VMEM((HEADS, CHUNK, HEAD_K), F32),
               pltpu.VMEM((HEADS, CHUNK, HEAD_V), F32)]
    args = [q, k, v, cum, s_all, s_fin, do]
    if rider is not None:
        kind, arr = rider
        any_spec = pl.BlockSpec(memory_space=pl.ANY)
        in_specs.append(any_spec)
        out_specs.append(any_spec)
        args.append(arr)
        if kind == "swap":
            out_shape += _pair_got_shapes([], arr)
            n_cp = _pair_count([], arr)
        else:
            out_shape.append(jax.ShapeDtypeStruct((3,) + arr.shape[1:], arr.dtype))
            n_cp = 3 * P_ROW_CHUNKS
        scratch += [pltpu.SemaphoreType.DMA((n_cp,)), pltpu.SemaphoreType.DMA((n_cp,))]
    return _pallas(
        body, name=name, grid=(n, SCAN_STEPS), in_specs=in_specs, out_specs=out_specs, out_shape=out_shape,
        scratch_shapes=scratch,
        compiler_params=_params(("parallel" if rider is None else "arbitrary", "arbitrary")),
    )(*args)


def gla_out_proj(o_f, o_b, r, gnorm, gla_proj):
    n = o_f.shape[0]
    tiles = SEQ // TM_OUT

    def body(of_ref, ob_ref, r_ref, g_ref, w_ref, og_ref, y_ref):
        for h in range(HEADS):
            cols = pl.ds(h * HEAD_V, HEAD_V)
            o = of_ref[0, :, cols].astype(F32) + ob_ref[0, :, cols].astype(F32)
            rs = lax.rsqrt(jnp.mean(o * o, axis=-1, keepdims=True) + EPS)
            og_ref[:, cols] = (o * rs * g_ref[...] * _silu(r_ref[:, cols].astype(F32))).astype(og_ref.dtype)
        y_ref[...] = _nn(og_ref[...], w_ref[...]).astype(y_ref.dtype)

    ospec = pl.BlockSpec((1, TM_OUT, D), lambda b, j: (b, j, 0))
    row = pl.BlockSpec((TM_OUT, D), lambda b, j: (b * tiles + j, 0))
    return _pallas(
        body, name="gla_out_proj", grid=(n, tiles),
        in_specs=[ospec, ospec, row, pl.BlockSpec((1, HEAD_V), lambda b, j: (0, 0)),
                  pl.BlockSpec((D, D), lambda b, j: (0, 0))],
        out_specs=[row, row],
        out_shape=[jax.ShapeDtypeStruct((n * SEQ, D), BF16), jax.ShapeDtypeStruct((n * SEQ, D), BF16)],
        compiler_params=_params(("parallel", "parallel")),
    )(o_f, o_b, r, gnorm, gla_proj)


def gla_out_bwd(o_f, o_b, r, dyg, gla_proj, gnorm, og):
    n = o_f.shape[0]
    tiles = SEQ // TM_EW

    def body(of_ref, ob_ref, r_ref, d_ref, w_ref, g_ref, og_ref, do_ref, dr_ref, dg_ref, gw_ref, dog_buf):
        @pl.when((pl.program_id(0) == 0) & (pl.program_id(1) == 0))
        def _():
            dg_ref[...] = jnp.zeros_like(dg_ref)
            gw_ref[...] = jnp.zeros_like(gw_ref)

        gw_ref[...] += _tn(og_ref[...], d_ref[...])
        dog_buf[...] = _nt(d_ref[...], w_ref[...])
        for h in range(HEADS):
            cols = pl.ds(h * HEAD_V, HEAD_V)
            o = of_ref[0, :, cols].astype(F32) + ob_ref[0, :, cols].astype(F32)
            rv = r_ref[:, cols].astype(F32)
            dv = dog_buf[:, cols]
            rs = lax.rsqrt(jnp.mean(o * o, axis=-1, keepdims=True) + EPS)
            oh = o * rs
            dr_ref[:, cols] = (dv * oh * g_ref[...] * _dsilu(rv)).astype(dr_ref.dtype)
            dn = dv * _silu(rv)
            dg_ref[...] += jnp.sum(dn * oh, axis=0, keepdims=True)
            doh = dn * g_ref[...]
            do_ref[0, :, cols] = _b16(rs * (doh - oh * jnp.mean(doh * oh, axis=-1, keepdims=True)))

    ospec = pl.BlockSpec((1, TM_EW, D), lambda b, j: (b, j, 0))
    row = pl.BlockSpec((TM_EW, D), lambda b, j: (b * tiles + j, 0))
    vec = pl.BlockSpec((1, HEAD_V), lambda b, j: (0, 0))
    return _pallas(
        body, name="gla_out_bwd", grid=(n, tiles),
        in_specs=[ospec, ospec, row, row, pl.BlockSpec((D, D), lambda b, j: (0, 0)), vec, row],
        out_specs=[ospec, row, vec, pl.BlockSpec((D, D), lambda b, j: (0, 0))],
        out_shape=[jax.ShapeDtypeStruct((n, SEQ, D), BF16), jax.ShapeDtypeStruct((n * SEQ, D), BF16),
                   jax.ShapeDtypeStruct((1, HEAD_V), F32), jax.ShapeDtypeStruct((D, D), F32)],
        scratch_shapes=[pltpu.VMEM((TM_EW, D), F32)],
        compiler_params=_params(("arbitrary", "arbitrary")),
    )(o_f, o_b, r, dyg, gla_proj, gnorm, og)


TM_OUT = 512


def merge_out_final(p5, y_conv, y_gla, w_out, x2, gate, final_g, target, n_samples):
    t = x2.shape[0]
    tiles = SEQ // TM_OUT

    def body(mc_ref, mg_ref, yc_ref, yg_ref, w_ref, x_ref, gate_ref, g_ref, t_ref,
             mrg_ref, dh_ref, dmo_ref, dgate_ref, dg_ref, loss_ref):
        b, j = pl.program_id(0), pl.program_id(1)
        f = lambda ref: ref[...].astype(F32)
        merged = _b16(_sigmoid(f(mc_ref)) * f(yc_ref) + _sigmoid(f(mg_ref)) * f(yg_ref))
        mrg_ref[...] = merged
        mo_v = _nn(merged, w_ref[...])
        h = x_ref[...] + gate_ref[0] * mo_v
        rs = lax.rsqrt(jnp.mean(h * h, axis=-1, keepdims=True) + EPS)
        nh = h * rs
        err = nh * g_ref[...] - t_ref[...]
        dy = err * (1.0 / D)
        dn = dy * g_ref[...]
        dh = rs * (dn - nh * jnp.mean(dn * nh, axis=-1, keepdims=True))
        dh_ref[...] = dh
        dmo_ref[...] = (dh * gate_ref[0]).astype(dmo_ref.dtype)

        @pl.when(j == 0)
        def _():
            dgate_ref[...] = jnp.zeros_like(dgate_ref)

        @pl.when((b == 0) & (j == 0))
        def _():
            dg_ref[...] = jnp.zeros_like(dg_ref)
            loss_ref[...] = jnp.zeros_like(loss_ref)

        dgate_ref[0] += jnp.sum(dh * mo_v, axis=0, keepdims=True)
        dg_ref[...] += jnp.sum(dy * nh, axis=0, keepdims=True)
        loss_ref[...] += (0.5 / D) * jnp.sum(err * err)

    row = pl.BlockSpec((TM_OUT, D), lambda b, j: (b * tiles + j, 0))
    per = pl.BlockSpec((1, 1, D), lambda b, j: (b, 0, 0))
    vec = pl.BlockSpec((1, D), lambda b, j: (0, 0))
    return _pallas(
        body, name="merge_out_final", grid=(n_samples, tiles),
        in_specs=[row, pl.BlockSpec((TM_OUT, D), lambda b, j: (b * tiles + j, 1)), row, row,
                  pl.BlockSpec((D, D), lambda b, j: (0, 0)), row, per, vec, row],
        out_specs=[row, row, row, per, vec, pl.BlockSpec((8, 128), lambda b, j: (0, 0))],
        out_shape=[jax.ShapeDtypeStruct((t, D), BF16), jax.ShapeDtypeStruct((t, D), F32), jax.ShapeDtypeStruct((t, D), BF16),
                   jax.ShapeDtypeStruct((n_samples, 1, D), F32), jax.ShapeDtypeStruct((1, D), F32),
                   jax.ShapeDtypeStruct((8, 128), F32)],
        compiler_params=_params(("arbitrary", "arbitrary")),
    )(p5, p5, y_conv, y_gla, w_out, x2, gate, final_g, target)


def out_dgrad_merge_bwd(p5, y_conv, y_gla, dmo, w_out, merged):
    t = y_conv.shape[0]

    def body(mc_ref, mg_ref, yc_ref, yg_ref, d_ref, w_ref, mrg_ref, dyc_ref, dyg_ref, dp_ref, gw_ref):
        f = lambda ref: ref[...].astype(F32)

        @pl.when(pl.program_id(0) == 0)
        def _():
            gw_ref[...] = jnp.zeros_like(gw_ref)

        gw_ref[...] += _tn(mrg_ref[...], d_ref[...])
        d = _nt(d_ref[...], w_ref[...])
        sc = _sigmoid(f(mc_ref))
        sg = _sigmoid(f(mg_ref))
        dyc_ref[...] = (d * sc).astype(dyc_ref.dtype)
        dyg_ref[...] = (d * sg).astype(dyg_ref.dtype)
        dp_ref[:, pl.ds(0, D)] = (d * f(yc_ref) * sc * (1.0 - sc)).astype(dp_ref.dtype)
        dp_ref[:, pl.ds(D, D)] = (d * f(yg_ref) * sg * (1.0 - sg)).astype(dp_ref.dtype)

    row = pl.BlockSpec((TM_OUT, D), lambda i: (i, 0))
    return _pallas(
        body, name="out_dgrad_merge_bwd", grid=(t // TM_OUT,),
        in_specs=[row, pl.BlockSpec((TM_OUT, D), lambda i: (i, 1)), row, row, row, pl.BlockSpec((D, D), lambda i: (0, 0)),
                  row],
        out_specs=[row, row, pl.BlockSpec((TM_OUT, 2 * D), lambda i: (i, 0)), pl.BlockSpec((D, D), lambda i: (0, 0))],
        out_shape=[jax.ShapeDtypeStruct((t, D), BF16), jax.ShapeDtypeStruct((t, D), BF16),
                   jax.ShapeDtypeStruct((t, 2 * D), BF16), jax.ShapeDtypeStruct((D, D), F32)],
        compiler_params=_params(("arbitrary",)),
    )(p5, p5, y_conv, y_gla, dmo, w_out, merged)


def local_step(x, ctx, target, mod, wts, small, p_sh, chip, core):
    n = x.shape[0]
    t = n * SEQ
    t_all = t + n * NCTX
    x2 = x.reshape(t, D)
    ctx2 = ctx.reshape(n * NCTX, D)
    tgt2 = target.reshape(t, D)
    scale1, shift, gate = mod

    u = norm_mod_fwd(x2, ctx2, scale1, shift, small["norm_g"])
    p1, p2, p3, p4, p5, p_all = proj_all(u, [wts["w%d" % i] for i in range(1, 6)], [small["b%d" % i] for i in range(1, 6)],
                                         [t, t, t_all, t, t], p_sh, tm=512)
    p_full = jnp.stack([jnp.where(chip == i, p_sh, p_all[i]) for i in range(N_CHIPS)])
    wts = dict(wts, conv_proj=p_full[:, 0:256].reshape(D, D), gla_proj=p_full[:, 256:512].reshape(D, D),
               w_out=p_full[:, 512:768].reshape(D, D))

    aconv = conv_fwd(p1, small["conv_w"], small["conv_b"], n)
    ac, y_conv = ln_gate_proj(aconv, p2, small["conv_ln_g"], small["conv_ln_b"], wts["conv_proj"])

    qs, ks, vs, cum_f, cum_b = gla_prep_fwd(p3, small["upf"], small["upb"], small["bias_f"], small["bias_b"], n)
    o_f, s_f, sfin_f = gla_scan_fwd(qs, ks, vs, cum_f, rev=False, name="gla_scan_fwd_f")
    o_b, s_b, sfin_b = gla_scan_fwd(qs, ks, vs, cum_b, rev=True, name="gla_scan_fwd_b")
    og, y_gla = gla_out_proj(o_f, o_b, p4, small["gla_norm_g"], wts["gla_proj"])

    merged, dh, dmo, dgate, d_final_g, loss = merge_out_final(p5, y_conv, y_gla, wts["w_out"], x2, gate,
                                                              small["final_norm_g"], tgt2, n)

    g = {"final_norm_g": d_final_g}
    dyc, dyg, dp5, g["w_out"] = out_dgrad_merge_bwd(p5, y_conv, y_gla, dmo, wts["w_out"], merged)

    daconv, dp2, g["conv_ln_g"], g["conv_ln_b"], g["conv_proj"] = ln_gate_bwd(
        aconv, p2, dyc, wts["conv_proj"], small["conv_ln_g"], small["conv_ln_b"], ac)
    dp1, dconv_w, dconv_b = conv_bwd(p1, daconv, small["conv_w"], n)
    g["conv_w"], g["conv_b"] = dconv_w, dconv_b

    do, dp4, g["gla_norm_g"], g["gla_proj"] = gla_out_bwd(o_f, o_b, p4, dyg, wts["gla_proj"], small["gla_norm_g"], og)
    g["proj"] = jnp.concatenate([g["conv_proj"].reshape(N_CHIPS, 256, D), g["gla_proj"].reshape(N_CHIPS, 256, D),
                                 g["w_out"].reshape(N_CHIPS, 256, D)], 1)
    dq_f, dk_f, dv_f, dc_f, gotp = gla_scan_bwd(qs, ks, vs, cum_f, s_f, sfin_f, do, rev=False, name="gla_scan_bwd_f",
                                                rider=("swap", g["proj"]))
    pap16 = pair_add(core, g["proj"], gotp, name="pair_add_p", tr=384)
    dq_b, dk_b, dv_b, dc_b, rbp = gla_scan_bwd(qs, ks, vs, cum_b, s_b, sfin_b, do, rev=True, name="gla_scan_bwd_b",
                                               rider=("exchange", pap16))
    dp3, g["upf"], g["upb"], g["bias_f"], g["bias_b"] = gla_prep_bwd(
        p3, dq_f, dq_b, dk_f, dk_b, dv_f, dv_b, dc_f, dc_b,
        small["upf"], small["upb"], small["bias_f"], small["bias_b"], n)

    dps = [dp1, dp2, dp3, dp4, dp5]
    got = {}
    for i in [0, 1, 3, 4, 2]:
        dp = dps[i]
        rows = dp.shape[0]
        tn = W3 if dp.shape[1] == W3 else 1024
        others = [j for j in range(5) if j != i]
        swap = [g["w%d" % (j + 1)] for j in others] if i == 2 else None
        outs = matmul_tn(u, dp, name="w_in_wgrad_%d" % (i + 1), t=rows, tn=tn, tt=1024 if rows % 1024 == 0 else 768,
                         colsum=True, swap=swap)
        g["w%d" % (i + 1)], g["b%d" % (i + 1)] = outs[0], outs[1]
        if swap is not None:
            got = dict(zip(others, outs[2:2 + len(others)]))
    g["gate"] = dgate
    return loss, dh, dps, g, got, (pap16, rbp)


def _group_cols(w):
    gv, gg, z = w[..., 0:1024], w[..., 1024:2048], w[..., 2048:3072]
    q, k, v = w[..., 3072:3584], w[..., 3584:4096], w[..., 4096:5120]
    ab = w[..., 5120:5152]
    r, mc, mg = w[..., 5152:6176], w[..., 6176:7200], w[..., 7200:8224]
    g1 = jnp.concatenate([p for j in range(CONV_NCB)
                          for p in (gv[..., CONV_CB * j:CONV_CB * (j + 1)], gg[..., CONV_CB * j:CONV_CB * (j + 1)])], -1)
    pad = jnp.zeros(w.shape[:-1] + (W3 - 2080,), w.dtype)
    g3 = jnp.concatenate([v, q, k, ab, pad], -1)
    return g1, z, g3, r, jnp.concatenate([mc, mg], -1)


def _ungroup_cols(g1, g2, g3, g4, g5):
    gv = jnp.concatenate([g1[..., 2 * CONV_CB * j:2 * CONV_CB * j + CONV_CB] for j in range(CONV_NCB)], -1)
    gg = jnp.concatenate([g1[..., 2 * CONV_CB * j + CONV_CB:2 * CONV_CB * (j + 1)] for j in range(CONV_NCB)], -1)
    v, q, k, ab = g3[..., 0:1024], g3[..., 1024:1536], g3[..., 1536:2048], g3[..., 2048:2080]
    return jnp.concatenate([gv, gg, g2, q, k, v, ab, g4, g5[..., 0:1024], g5[..., 1024:2048]], -1)


def _natural_pieces():
    pieces = [(CONV_CB * j, CONV_CB, 0, 2 * CONV_CB * j) for j in range(CONV_NCB)]
    pieces += [(1024 + CONV_CB * j, CONV_CB, 0, 2 * CONV_CB * j + CONV_CB) for j in range(CONV_NCB)]
    pieces += [(2048, 1024, 1, 0), (3072, 512, 2, O3_Q), (3584, 512, 2, O3_K), (4096, 1024, 2, O3_V), (5120, 32, 2, O3_AB),
               (5152, 1024, 3, 0), (6176, 1024, 4, 0), (7200, 1024, 4, 1024)]
    return sorted(pieces)


def _ungroup_to_shards(groups):
    shards = []
    for i in range(N_CHIPS):
        lo, hi = i * W_IN_SHARD, (i + 1) * W_IN_SHARD
        parts = []
        for nat, width, g, gcol in _natural_pieces():
            a, b = max(nat, lo), min(nat + width, hi)
            if a < b:
                parts.append(groups[g][:, gcol + a - nat:gcol + b - nat])
        shards.append(jnp.concatenate(parts, 1))
    return jnp.stack(shards)


def _pad_up(up, row0):
    return jnp.zeros((128, GLA_DK), F32).at[row0:row0 + up.shape[0]].set(up)


def _adamw_math(w, g, m, v):
    m = ADAM_B1 * m + (1.0 - ADAM_B1) * g
    v = ADAM_B2 * v + (1.0 - ADAM_B2) * (g * g)
    m_hat = m / (1.0 - ADAM_B1 ** ADAM_STEP)
    v_hat = v / (1.0 - ADAM_B2 ** ADAM_STEP)
    delta = -ADAM_LR * (m_hat / (jnp.sqrt(v_hat) + ADAM_EPS) + ADAM_WD * w)
    return delta, m, v


def adamw2d(w, g, m, v, *, name, tr, tcols=None):
    rows, cols = w.shape[-2:]

    def body(w_ref, g_ref, m_ref, v_ref, d_ref, nm_ref, nv_ref):
        d_ref[...], nm_ref[...], nv_ref[...] = _adamw_math(w_ref[...], g_ref[...], m_ref[...], v_ref[...])

    tcols = cols if tcols is None else tcols
    if w.ndim == 3:
        spec = pl.BlockSpec((1, tr, tcols), lambda i, j: (0, i, j))
    else:
        spec = pl.BlockSpec((tr, tcols), lambda i, j: (i, j))
    return _pallas(
        body, name=name, grid=(rows // tr, cols // tcols), in_specs=[spec] * 4, out_specs=[spec] * 3,
        out_shape=[jax.ShapeDtypeStruct(w.shape, F32)] * 3, compiler_params=_params(("parallel", "parallel")),
    )(w, g, m, v)


def adamw_many(ws, gs, ms, vs):
    k = len(ws)
    two = lambda a: a.reshape((-1, a.shape[-1]))

    def body(*refs):
        w_refs, g_refs, m_refs, v_refs = refs[:k], refs[k:2 * k], refs[2 * k:3 * k], refs[3 * k:4 * k]
        d_refs, nm_refs, nv_refs = refs[4 * k:5 * k], refs[5 * k:6 * k], refs[6 * k:7 * k]
        for i in range(k):
            d_refs[i][...], nm_refs[i][...], nv_refs[i][...] = _adamw_math(
                w_refs[i][...], g_refs[i][...], m_refs[i][...], v_refs[i][...])

    shapes = [jax.ShapeDtypeStruct(two(a).shape, F32) for a in ws]
    outs = _pallas(body, name="adamw_small", out_shape=shapes * 3, compiler_params=_params())(
        *[two(a) for a in ws], *[two(a) for a in gs], *[two(a) for a in ms], *[two(a) for a in vs])
    back = lambda lst: [o.reshape(a.shape) for o, a in zip(lst, ws)]
    return back(outs[:k]), back(outs[k:2 * k]), back(outs[2 * k:])


def sum_devices(sall, *, name):
    rows = sall.shape[1]

    def body(s_ref, o_ref):
        acc = s_ref[0]
        for d in range(1, N_DEV):
            acc = acc + s_ref[d]
        o_ref[...] = acc

    return _pallas(body, name=name, out_shape=jax.ShapeDtypeStruct((rows, D), F32),
                   compiler_params=_params())(sall)


def pair_add(core, g, got, *, name, tr):
    n, rows, cols = got.shape
    g4 = g.reshape(n, 2, rows, cols)

    def body(core_ref, g_ref, got_ref, ob_ref):
        del core_ref
        ob_ref[0] = (g_ref[0, 0] + got_ref[0]).astype(BF16)

    spec = pl.BlockSpec((1, tr, cols), lambda i, t, core_ref: (i, t, 0))
    return _pallas(
        body, name=name,
        grid_spec=pltpu.PrefetchScalarGridSpec(
            num_scalar_prefetch=1, grid=(n, rows // tr),
            in_specs=[pl.BlockSpec((1, 1, tr, cols), lambda i, t, core_ref: (i, core_ref[0], t, 0)), spec],
            out_specs=spec),
        out_shape=jax.ShapeDtypeStruct(got.shape, BF16),
        compiler_params=_params(("parallel", "parallel")))(core, g4, got)


def pair_add_groups(core, gs, gots, *, tr):
    k = len(gs)
    rows = gots[0].shape[0]

    def body(core_ref, *refs):
        del core_ref
        for i in range(k):
            refs[2 * k + i][...] = (refs[i][0] + refs[k + i][...]).astype(BF16)

    g_specs = [pl.BlockSpec((1, tr, a.shape[1]), lambda t, core_ref: (core_ref[0], t, 0)) for a in gots]
    r_specs = [pl.BlockSpec((tr, a.shape[1]), lambda t, core_ref: (t, 0)) for a in gots]
    return _pallas(
        body, name="pair_add_w",
        grid_spec=pltpu.PrefetchScalarGridSpec(num_scalar_prefetch=1, grid=(rows // tr,),
                                               in_specs=g_specs + r_specs, out_specs=r_specs),
        out_shape=[jax.ShapeDtypeStruct(a.shape, BF16) for a in gots],
        compiler_params=_params(("parallel",)))(core, *[a.reshape(2, rows, a.shape[1]) for a in gs], *gots)


def chip_add(place, pa, rb, *, name, tr):
    _, rows, cols = pa.shape

    def body(place_ref, m_ref, r_ref, o_ref):
        del place_ref
        o_ref[0] = ((m_ref[0].astype(F32) + r_ref[0].astype(F32)) + r_ref[1].astype(F32)) + r_ref[2].astype(F32)

    return _pallas(
        body, name=name,
        grid_spec=pltpu.PrefetchScalarGridSpec(
            num_scalar_prefetch=1, grid=(rows // tr,),
            in_specs=[pl.BlockSpec((1, tr, cols), lambda t, place_ref: (place_ref[0], t, 0)),
                      pl.BlockSpec((3, tr, cols), lambda t, place_ref: (0, t, 0))],
            out_specs=pl.BlockSpec((1, tr, cols), lambda t, place_ref: (place_ref[1], t, 0))),
        out_shape=jax.ShapeDtypeStruct((2, rows, cols), F32),
        compiler_params=_params(("parallel",)))(place, pa, rb)


def ada_bwd(call, cctx_rows, dm_shard, dm_full, adaw):
    nsh = adaw.shape[1]

    def body(c_ref, cc_ref, dms_ref, dmf_ref, w_ref, gw_ref, gb_ref, pq_ref):
        a_lat = _silu(c_ref[...])
        a_ctx = _silu(cc_ref[...])
        dms = dms_ref[...]
        gw_ref[...] = _tn(a_lat, dms[0:64], HI) + _tn(a_ctx, dms[64:72], HI)
        gb_ref[...] = jnp.sum(dmf_ref[...], axis=0, keepdims=True)
        part = _nt(dms[64:72], w_ref[...], HI)
        pq_ref[...] = jnp.zeros_like(pq_ref) + jnp.sum(part, axis=0, keepdims=True)

    return _pallas(body, name="ada_bwd",
                   out_shape=[jax.ShapeDtypeStruct((D, nsh), F32), jax.ShapeDtypeStruct((1, 3 * D), F32),
                              jax.ShapeDtypeStruct((8, D), F32)],
                   compiler_params=_params())(call, cctx_rows, dm_shard, dm_full, adaw)


def cctx_grad(pq_all, cctx_rows):
    def body(p_ref, c_ref, o_ref):
        acc = p_ref[0]
        for qi in range(1, N_CHIPS):
            acc = acc + p_ref[qi]
        o_ref[...] = acc * _dsilu(c_ref[...])

    return _pallas(body, name="cctx_grad", out_shape=jax.ShapeDtypeStruct((8, D), F32),
                   compiler_params=_params())(pq_all, cctx_rows)


def _place():
    x, y, c = lax.axis_index("x"), lax.axis_index("y"), lax.axis_index("c")
    chips = [(1 - x, y), (x, 1 - y), (1 - x, 1 - y)]
    return x, y, c, chips


def _all_peers(x, y, c):
    return [((1 - x) if r & 4 else x, (1 - y) if r & 2 else y, (1 - c) if r & 1 else c) for r in range(1, N_DEV)]


def _remote(src, dst, send_sem, recv_sem, dev):
    return pltpu.make_async_remote_copy(src_ref=src, dst_ref=dst, send_sem=send_sem, recv_sem=recv_sem,
                                        device_id=dev, device_id_type=MESH)


ANY = pl.BlockSpec(memory_space=pl.ANY)
VMEM = pl.BlockSpec(memory_space=pltpu.VMEM)
F_ROWS = 16


W_ROW_CHUNKS = 4
P_ROW_CHUNKS = 2
N_BULK = W_ROW_CHUNKS + P_ROW_CHUNKS


def _half_chunks(core, n_rows, align, which=(0, 1)):
    out = []
    for a, k in ((0, W_ROW_CHUNKS), (1, P_ROW_CHUNKS)):
        if a not in which:
            continue
        half = n_rows[a] // 2
        size = half // k
        for i in range(k):
            start = core * half + i * size
            out.append((a, pl.ds(start if isinstance(start, int) else pl.multiple_of(start, align), size)))
    return out


def gather_weights(c8, cctx8, adaw, adab, w_sh, fp):
    nsh = adaw.shape[1]

    def body(c_ref, cctx_ref, adaw_ref, adab_ref, w_ref, fp_ref, wall_ref, fall_ref, call_ref, mall_ref,
             abuf, w_send, w_recv, h_send, h_recv, c_send, c_recv, m_send, m_recv, f_send, f_recv):
        x, y, c, chips = _place()
        q = 2 * x + y
        dev = 4 * x + 2 * y + c
        qs = [2 * cx + cy for cx, cy in chips]
        sib = (x, y, 1 - c)
        srcs, dsts = (w_ref,), (wall_ref,)
        n_rows = (w_ref.shape[0],)
        mine = _half_chunks(c, n_rows, 16, which=(0,))
        other = _half_chunks(1 - c, n_rows, 16, which=(0,))

        bulk = [[_remote(srcs[a].at[rows], dsts[a].at[q, rows], w_send.at[j * N_BULK + i], w_recv.at[j * N_BULK + i],
                         (*chips[j], c)) for i, (a, rows) in enumerate(mine)] for j in range(3)]
        fall_ref[q] = fp_ref[...]
        small = [_remote(fp_ref, fall_ref.at[q], f_send.at[j], f_recv.at[j], (*chips[j], c)) for j in range(3)]
        my_rows = pl.ds(pl.multiple_of(8 * dev, 8), 8)
        call_ref[my_rows, :] = c_ref[...]
        cond = [_remote(c_ref, call_ref.at[my_rows, :], c_send.at[r], c_recv.at[r], peer)
                for r, peer in enumerate(_all_peers(x, y, c))]
        for cp in sum(bulk, []) + small + cond:
            cp.start()
        for cp in cond:
            cp.wait_recv()

        abuf[pl.ds(0, 64), :] = _silu(call_ref[...])
        abuf[pl.ds(64, 8), :] = _silu(cctx_ref[...])
        mall_ref[q] = _nn(abuf[...], adaw_ref[...], HI) + adab_ref[...]
        mod = [_remote(mall_ref.at[q], mall_ref.at[q], m_send.at[j], m_recv.at[j], (*chips[j], c)) for j in range(3)]
        for cp in mod:
            cp.start()

        handed = []
        for j in range(3):
            for i, (a, rows) in enumerate(mine):
                bulk[j][i].wait_recv()
                cp = _remote(dsts[a].at[qs[j], rows], dsts[a].at[qs[j], rows],
                             h_send.at[j * N_BULK + i], h_recv.at[j * N_BULK + i], sib)
                cp.start()
                handed.append(cp)
        for j in range(3):
            for i, (a, rows) in enumerate(other):
                _remote(dsts[a].at[qs[j], rows], dsts[a].at[qs[j], rows],
                        h_send.at[j * N_BULK + i], h_recv.at[j * N_BULK + i], sib).wait_recv()
        for cp in mod + small:
            cp.wait_recv()
        for cp in sum(bulk, []) + small + cond + mod + handed:
            cp.wait_send()

    def dma(n):
        return pltpu.SemaphoreType.DMA((n,))

    return _pallas(
        body, name="gather_weights",
        in_specs=[VMEM, VMEM, VMEM, VMEM, ANY, VMEM],
        out_specs=[ANY, VMEM, VMEM, VMEM],
        out_shape=[jax.ShapeDtypeStruct((N_CHIPS,) + w_sh.shape, BF16),
                   jax.ShapeDtypeStruct((N_CHIPS, F_ROWS, D), F32),
                   jax.ShapeDtypeStruct((8 * N_DEV, D), F32), jax.ShapeDtypeStruct((N_CHIPS, MOD_ROWS, nsh), F32)],
        scratch_shapes=[pltpu.VMEM((MOD_ROWS, D), F32), dma(3 * N_BULK), dma(3 * N_BULK), dma(3 * N_BULK), dma(3 * N_BULK),
                        dma(7), dma(7), dma(3), dma(3), dma(3), dma(3)],
        compiler_params=_params(),
    )(c8, cctx8, adaw, adab, w_sh, fp)


def _pair_count(gs, gp):
    return len(gs) * W_ROW_CHUNKS + (0 if gp is None else N_CHIPS * P_ROW_CHUNKS)


def _pair_got_shapes(gs, gp):
    shapes = [jax.ShapeDtypeStruct((D // 2, a.shape[1]), F32) for a in gs]
    if gp is not None:
        shapes.append(jax.ShapeDtypeStruct((N_CHIPS, gp.shape[1] // 2, gp.shape[2]), F32))
    return shapes


def _pair_copies(g_refs, gp_ref, got_refs, gotp_ref, a_send, a_recv):
    x, y, c, _ = _place()
    sib = (x, y, 1 - c)
    pair = []
    half, size = D // 2, D // 2 // W_ROW_CHUNKS
    for gi in range(len(g_refs)):
        for i in range(W_ROW_CHUNKS):
            k = len(pair)
            rows_o = pl.ds(pl.multiple_of((1 - c) * half + i * size, 8), size)
            pair.append(_remote(g_refs[gi].at[rows_o], got_refs[gi].at[pl.ds(i * size, size)],
                                a_send.at[k], a_recv.at[k], sib))
    if gp_ref is not None:
        half, size = gp_ref.shape[1] // 2, gp_ref.shape[1] // 2 // P_ROW_CHUNKS
        for s in range(N_CHIPS):
            for i in range(P_ROW_CHUNKS):
                k = len(pair)
                rows_o = pl.ds(pl.multiple_of((1 - c) * half + i * size, 8), size)
                pair.append(_remote(gp_ref.at[s, rows_o], gotp_ref.at[s, pl.ds(i * size, size)],
                                    a_send.at[k], a_recv.at[k], sib))
    return pair


def pair_swap(gs, sm):
    n_gs = len(gs)

    def body(*refs):
        g_refs, sm_ref = refs[:n_gs], refs[n_gs]
        got_refs, sall_ref = refs[n_gs + 1:2 * n_gs + 1], refs[2 * n_gs + 1]
        a_send, a_recv, s_send, s_recv = refs[2 * n_gs + 2:]
        x, y, c, _ = _place()
        dev = 4 * x + 2 * y + c
        pair = _pair_copies(g_refs, None, got_refs, None, a_send, a_recv)
        sall_ref[dev] = sm_ref[...]
        small = [_remote(sm_ref, sall_ref.at[dev], s_send.at[r], s_recv.at[r], peer)
                 for r, peer in enumerate(_all_peers(x, y, c))]
        for cp in pair + small:
            cp.start()
        for cp in small + pair:
            cp.wait_recv()
        for cp in small + pair:
            cp.wait_send()

    return _pallas(
        body, name="pair_swap", in_specs=[ANY] * n_gs + [VMEM], out_specs=[ANY] * n_gs + [VMEM],
        out_shape=_pair_got_shapes(gs, None) + [jax.ShapeDtypeStruct((N_DEV,) + sm.shape, F32)],
        scratch_shapes=[pltpu.SemaphoreType.DMA((_pair_count(gs, None),)), pltpu.SemaphoreType.DMA((_pair_count(gs, None),)),
                        pltpu.SemaphoreType.DMA((N_DEV - 1,)), pltpu.SemaphoreType.DMA((N_DEV - 1,))],
        compiler_params=_params(),
    )(*gs, sm)


def gather_small(sm):
    rows = sm.shape[0]

    def body(sm_ref, sall_ref, s_send, s_recv):
        x, y, c, _ = _place()
        dev = 4 * x + 2 * y + c
        sall_ref[dev] = sm_ref[...]
        small = [_remote(sm_ref, sall_ref.at[dev], s_send.at[r], s_recv.at[r], peer)
                 for r, peer in enumerate(_all_peers(x, y, c))]
        for cp in small:
            cp.start()
        for cp in small:
            cp.wait_recv()
        for cp in small:
            cp.wait_send()

    return _pallas(
        body, name="gather_small", in_specs=[VMEM], out_specs=VMEM,
        out_shape=jax.ShapeDtypeStruct((N_DEV, rows, D), F32),
        scratch_shapes=[pltpu.SemaphoreType.DMA((7,)), pltpu.SemaphoreType.DMA((7,))],
        compiler_params=_params(),
    )(sm)


def pair_share(ghw, ghp, pq):
    def body(ghw_ref, ghp_ref, pq_ref, outw_ref, outp_ref, pqa_ref, send, recv, p_send, p_recv):
        del ghw_ref, ghp_ref
        x, y, c, chips = _place()
        q = 2 * x + y
        refs = (outw_ref, outp_ref)
        n_rows = (2 * outw_ref.shape[1], 2 * outp_ref.shape[1])
        pair = [_remote(refs[a].at[c, rows], refs[a].at[c, rows], send.at[i], recv.at[i], (x, y, 1 - c))
                for i, (a, rows) in enumerate(_half_chunks(0, n_rows, 8))]
        pqa_ref[q] = pq_ref[...]
        small = [_remote(pq_ref, pqa_ref.at[q], p_send.at[j], p_recv.at[j], (*chips[j], c)) for j in range(3)]
        for cp in pair + small:
            cp.start()
        for i, (a, rows) in enumerate(_half_chunks(0, n_rows, 8)):
            _remote(refs[a].at[1 - c, rows], refs[a].at[1 - c, rows], send.at[i], recv.at[i], (x, y, 1 - c)).wait_recv()
        for cp in small:
            cp.wait_recv()
        for cp in pair + small:
            cp.wait_send()

    return _pallas(
        body, name="pair_share", in_specs=[ANY, ANY, VMEM], out_specs=[ANY, ANY, VMEM],
        out_shape=[jax.ShapeDtypeStruct(ghw.shape, F32), jax.ShapeDtypeStruct(ghp.shape, F32),
                   jax.ShapeDtypeStruct((N_CHIPS, 8, D), F32)],
        scratch_shapes=[pltpu.SemaphoreType.DMA((N_BULK,)), pltpu.SemaphoreType.DMA((N_BULK,)),
                        pltpu.SemaphoreType.DMA((3,)), pltpu.SemaphoreType.DMA((3,))],
        input_output_aliases={0: 0, 1: 1},
        compiler_params=_params(),
    )(ghw, ghp, pq)


def _rows_of(shape):
    size = 1
    for s in shape:
        size *= s
    return -(-size // D)


def _pack(arrs, rows_multiple=8):
    parts = []
    total = 0
    for a in arrs:
        f = a.reshape(-1).astype(F32)
        r = _rows_of(a.shape)
        parts.append(jnp.pad(f, (0, r * D - f.shape[0])))
        total += r
    pad_rows = (-total) % rows_multiple
    if pad_rows:
        parts.append(jnp.zeros((pad_rows * D,), F32))
    return jnp.concatenate(parts).reshape(-1, D)


def _unpack(p, shapes):
    out = []
    r0 = 0
    for shp in shapes:
        r = _rows_of(shp)
        size = 1
        for s in shp:
            size *= s
        out.append(p[r0:r0 + r].reshape(-1)[:size].reshape(shp))
        r0 += r
    return out


WEIGHT_NAMES = ['c_ctx', 'ada_w', 'ada_b', 'norm_g', 'w_in', 'b_in', 'conv_w', 'conv_b', 'conv_ln_g', 'conv_ln_b',
                'conv_proj', 'decay_up_fwd', 'decay_bias_fwd', 'decay_up_bwd', 'decay_bias_bwd', 'gla_norm_g', 'gla_proj',
                'w_out', 'final_norm_g']
SMALL_NAMES = ['c_ctx', 'ada_b', 'norm_g', 'b_in', 'conv_w', 'conv_b', 'conv_ln_g', 'conv_ln_b', 'decay_up_fwd',
               'decay_bias_fwd', 'decay_up_bwd', 'decay_bias_bwd', 'gla_norm_g', 'final_norm_g']


def kernel(x, c, ctx, c_ctx, ada_w, ada_b, norm_g, w_in, b_in, conv_w, conv_b, conv_ln_g, conv_ln_b, conv_proj, decay_up_fwd, decay_bias_fwd, decay_up_bwd, decay_bias_bwd, gla_norm_g, gla_proj, w_out, final_norm_g, loss_target, m_c_ctx, m_ada_w, m_ada_b, m_norm_g, m_w_in, m_b_in, m_conv_w, m_conv_b, m_conv_ln_g, m_conv_ln_b, m_conv_proj, m_decay_up_fwd, m_decay_bias_fwd, m_decay_up_bwd, m_decay_bias_bwd, m_gla_norm_g, m_gla_proj, m_w_out, m_final_norm_g, v_c_ctx, v_ada_w, v_ada_b, v_norm_g, v_w_in, v_b_in, v_conv_w, v_conv_b, v_conv_ln_g, v_conv_ln_b, v_conv_proj, v_decay_up_fwd, v_decay_bias_fwd, v_decay_up_bwd, v_decay_bias_bwd, v_gla_norm_g, v_gla_proj, v_w_out, v_final_norm_g):
    w = dict(c_ctx=c_ctx, ada_w=ada_w, ada_b=ada_b, norm_g=norm_g, w_in=w_in, b_in=b_in, conv_w=conv_w, conv_b=conv_b,
             conv_ln_g=conv_ln_g, conv_ln_b=conv_ln_b, conv_proj=conv_proj, decay_up_fwd=decay_up_fwd,
             decay_bias_fwd=decay_bias_fwd, decay_up_bwd=decay_up_bwd, decay_bias_bwd=decay_bias_bwd,
             gla_norm_g=gla_norm_g, gla_proj=gla_proj, w_out=w_out, final_norm_g=final_norm_g)
    m = dict(c_ctx=m_c_ctx, ada_w=m_ada_w, ada_b=m_ada_b, norm_g=m_norm_g, w_in=m_w_in, b_in=m_b_in, conv_w=m_conv_w,
             conv_b=m_conv_b, conv_ln_g=m_conv_ln_g, conv_ln_b=m_conv_ln_b, conv_proj=m_conv_proj,
             decay_up_fwd=m_decay_up_fwd, decay_bias_fwd=m_decay_bias_fwd, decay_up_bwd=m_decay_up_bwd,
             decay_bias_bwd=m_decay_bias_bwd, gla_norm_g=m_gla_norm_g, gla_proj=m_gla_proj, w_out=m_w_out,
             final_norm_g=m_final_norm_g)
    v = dict(c_ctx=v_c_ctx, ada_w=v_ada_w, ada_b=v_ada_b, norm_g=v_norm_g, w_in=v_w_in, b_in=v_b_in, conv_w=v_conv_w,
             conv_b=v_conv_b, conv_ln_g=v_conv_ln_g, conv_ln_b=v_conv_ln_b, conv_proj=v_conv_proj,
             decay_up_fwd=v_decay_up_fwd, decay_bias_fwd=v_decay_bias_fwd, decay_up_bwd=v_decay_up_bwd,
             decay_bias_bwd=v_decay_bias_bwd, gla_norm_g=v_gla_norm_g, gla_proj=v_gla_proj, w_out=v_w_out,
             final_norm_g=v_final_norm_g)
    n = x.shape[0]
    ax, ay, ac = lax.axis_index("x"), lax.axis_index("y"), lax.axis_index("c")
    q = 2 * ax + ay
    dev = 4 * ax + 2 * ay + ac
    nsh = ada_w.shape[2]

    w_sh = w_in[0].astype(BF16)
    p_sh = jnp.concatenate([conv_proj[0], gla_proj[0], w_out[0]], 0).astype(BF16)
    fp = _pack([conv_w[0], decay_up_fwd[0], decay_up_bwd[0]], F_ROWS)
    c8 = jnp.pad(c, ((0, 8 - n), (0, 0)))
    cctx8 = jnp.pad(c_ctx[None], ((0, 7), (0, 0)))
    adab_sh = lax.dynamic_slice(ada_b, (0, q * nsh), (1, nsh))
    w_all, fall, call, mall = gather_weights(c8, cctx8, ada_w[0], adab_sh, w_sh, fp)

    mod_all = jnp.transpose(mall, (1, 0, 2)).reshape(MOD_ROWS, 3 * D)
    mod_mine = lax.dynamic_slice(mod_all, (8 * dev, 0), (n, 3 * D))
    mod_ctx = mod_all[64:65]
    shift = jnp.concatenate([mod_mine[:, 0:D], mod_ctx[:, 0:D]], 0)[:, None, :]
    scale1 = 1.0 + jnp.concatenate([mod_mine[:, D:2 * D], mod_ctx[:, D:2 * D]], 0)[:, None, :]
    gate = mod_mine[:, 2 * D:3 * D][:, None, :]

    own = lambda i, mine, got: jnp.where(q == i, mine, got)
    g1, g2, g3, g4, g5 = _group_cols(jnp.concatenate([own(i, w_sh, w_all[i]) for i in range(N_CHIPS)], 1))
    wts = dict(w1=g1, w2=g2, w3=g3, w4=g4, w5=g5)
    f_parts = [_unpack(fall[i], [conv_w.shape[1:], decay_up_fwd.shape[1:], decay_up_bwd.shape[1:]]) for i in range(N_CHIPS)]
    conv_w_full = jnp.concatenate([p[0] for p in f_parts], 1)
    upf_full = jnp.concatenate([p[1] for p in f_parts], 1)
    upb_full = jnp.concatenate([p[2] for p in f_parts], 1)
    b1, b2, b3, b4, b5 = _group_cols(b_in)
    small = dict(b1=b1, b2=b2, b3=b3, b4=b4, b5=b5, norm_g=norm_g,
                 conv_w=jnp.pad(conv_w_full, ((0, 1), (0, 0))), conv_b=conv_b, conv_ln_g=conv_ln_g, conv_ln_b=conv_ln_b,
                 upf=_split3(_pad_up(upf_full, 0)), upb=_split3(_pad_up(upb_full, 16)),
                 bias_f=decay_bias_fwd, bias_b=decay_bias_bwd,
                 gla_norm_g=gla_norm_g, final_norm_g=final_norm_g[None])

    core = ac.astype(jnp.int32).reshape(1)
    chip = q.astype(jnp.int32).reshape(1)
    loss_part, dh, dps, g, got, (pap16, rbp) = local_step(x, ctx, loss_target, (scale1, shift, gate), wts, small,
                                                          p_sh, q, core)
    loss = lax.psum(loss_part[0, 0], ("x", "y", "c"))

    gs = [g["w%d" % i] for i in range(1, 6)]
    d_b_in = _ungroup_cols(*[g["b%d" % i] for i in range(1, 6)])
    early = [d_b_in, g["conv_b"].sum(0), g["conv_ln_g"], g["conv_ln_b"], g["bias_f"], g["bias_b"],
             g["gla_norm_g"], g["final_norm_g"], g["conv_w"].sum(0)[:CONV_K], g["upf"][0:16], g["upb"][16:32]]
    early_shapes = [a.shape for a in early]
    got[2], sall1 = pair_swap([gs[2]], _pack(early))
    halves = pair_add_groups(core, gs, [got[i] for i in range(5)], tr=128)
    paw16 = _ungroup_to_shards(halves)
    grad_x2, dshift, dscale, g["norm_g"], rbw = dgrad_norm_bwd(
        dps, [wts["w%d" % i] for i in range(1, 6)], paw16, x.reshape(n * SEQ, D), ctx.reshape(n * NCTX, D), dh,
        scale1, norm_g, tm=256)

    dm_mine = jnp.concatenate([dshift[:n, 0], dscale[:n, 0], g["gate"][:, 0]], -1)
    dm_ctx = jnp.concatenate([dshift[n, 0], dscale[n, 0], jnp.zeros((D,), F32)], -1)
    late = [g["norm_g"], dm_mine, dm_ctx]
    late_shapes = [a.shape for a in late]
    sall2 = gather_small(_pack(late))
    (s_b_in, s_conv_b, s_ln_g, s_ln_b, s_bias_f, s_bias_b, s_gla_g, s_final_g, s_conv_w, s_upf,
     s_upb) = _unpack(sum_devices(sall1, name="sum_devices_early"), early_shapes)
    s_norm_g = _unpack(sum_devices(sall2, name="sum_devices_late"), late_shapes)[0]
    r_mine, r_ctx = 1, 1 + 3 * n
    dm_all = sall2[:, r_mine:r_ctx].reshape(N_DEV, n, 3 * D)
    dm_full = jnp.concatenate([jnp.pad(dm_all, ((0, 0), (0, 8 - n), (0, 0))).reshape(8 * N_DEV, 3 * D),
                               sall2[:, r_ctx:r_ctx + 3].reshape(N_DEV, 3 * D)], 0)
    dm_shard = lax.dynamic_slice(dm_full, (0, q * nsh), (MOD_ROWS, nsh))
    cctx_rows = jnp.broadcast_to(c_ctx[None], (8, D))
    g_ada_w, g_ada_b, pq = ada_bwd(call, cctx_rows, dm_shard, dm_full, ada_w[0])

    place = jnp.concatenate([chip, core])
    ghw = chip_add(place, paw16, rbw, name="chip_add_w", tr=128)
    ghp = chip_add(place, pap16, rbp, name="chip_add_p", tr=384)
    gw_mine, gp_mine, pq_all = pair_share(ghw, ghp, pq)
    gp_mine = gp_mine.reshape(768, D)
    g_c_ctx = cctx_grad(pq_all, cctx_rows)[0]

    grads = dict(
        c_ctx=g_c_ctx, ada_w=g_ada_w[None], ada_b=g_ada_b, norm_g=s_norm_g,
        w_in=gw_mine.reshape(1, D, W_IN_SHARD), b_in=s_b_in,
        conv_w=lax.dynamic_slice(s_conv_w, (0, q * 256), (CONV_K, 256))[None], conv_b=s_conv_b,
        conv_ln_g=s_ln_g, conv_ln_b=s_ln_b, conv_proj=gp_mine[0:256][None],
        decay_up_fwd=lax.dynamic_slice(s_upf, (0, q * 128), (16, 128))[None], decay_bias_fwd=s_bias_f,
        decay_up_bwd=lax.dynamic_slice(s_upb, (0, q * 128), (16, 128))[None], decay_bias_bwd=s_bias_b,
        gla_norm_g=s_gla_g, gla_proj=gp_mine[256:512][None], w_out=gp_mine[512:768][None],
        final_norm_g=s_final_g[0])

    delta, new_m, new_v = {}, {}, {}
    for name in ["ada_w", "conv_proj", "gla_proj", "w_out"]:
        delta[name], new_m[name], new_v[name] = adamw2d(w[name], grads[name].reshape(w[name].shape), m[name], v[name],
                                                        name="adamw_" + name, tr=128)
    tr_ = lambda a: jnp.swapaxes(a, 1, 2)
    g_w_in_t = tr_(grads["w_in"])
    grads["w_in"] = tr_(g_w_in_t)
    d_, m_, v_ = adamw2d(tr_(w_in), g_w_in_t, tr_(m_w_in), tr_(v_w_in), name="adamw_w_in", tr=W_IN_SHARD, tcols=128)
    delta["w_in"], new_m["w_in"], new_v["w_in"] = tr_(d_), tr_(m_), tr_(v_)
    d_, m_, v_ = adamw_many([w[nm] for nm in SMALL_NAMES], [grads[nm].reshape(w[nm].shape) for nm in SMALL_NAMES],
                            [m[nm] for nm in SMALL_NAMES], [v[nm] for nm in SMALL_NAMES])
    for nm, a, b, cc in zip(SMALL_NAMES, d_, m_, v_):
        delta[nm], new_m[nm], new_v[nm] = a, b, cc

    grad_x = grad_x2.reshape(x.shape)
    return (loss, grad_x, *[grads[nm].reshape(w[nm].shape) for nm in WEIGHT_NAMES], *[delta[nm] for nm in WEIGHT_NAMES],
            *[new_m[nm] for nm in WEIGHT_NAMES], *[new_v[nm] for nm in WEIGHT_NAMES])
```

```python
import jax
import jax.numpy as jnp
from jax import lax
from jax.experimental import pallas as pl
from jax.experimental.pallas import tpu as pltpu

F32 = jnp.float32
BF16 = jnp.bfloat16
MESH = pl.DeviceIdType.MESH
HI = lax.Precision.HIGHEST

D = 1024
SEQ = 2048
GRID_W = 64
GRID_H = SEQ // GRID_W
NCTX = 256
SEQ_ALL = SEQ + NCTX
EPS = 1e-6
CONV_K = 31
CONV_PAD = CONV_K // 2
HEADS = 4
HEAD_K = 128
HEAD_V = 256
GLA_DK = HEADS * HEAD_K
GATE_TAU = 16.0
Q_SCALE = HEAD_K ** -0.5
CHUNK = 64
NCHUNK = SEQ_ALL // CHUNK
NCHUNK_LAT = SEQ // CHUNK
NCHUNK_CTX = NCHUNK - NCHUNK_LAT
SUB = 64
NSUB = CHUNK // SUB
N_IN = 8224
W3 = 2176
O3_V, O3_Q, O3_K, O3_AB = 0, 1024, 1536, 2048

ADAM_LR, ADAM_B1, ADAM_B2, ADAM_EPS, ADAM_WD, ADAM_STEP = 0.001, 0.9, 0.999, 1e-08, 0.01, 10
VMEM_LIMIT = 56 * 1024 * 1024

N_CHIPS = 4
N_DEV = 8
W_IN_SHARD = N_IN // N_CHIPS
MOD_ROWS = 72


def _pallas(body, **kw):
    return pl.pallas_call(body, **kw)


def _params(sem=None, **kw):
    if sem is not None:
        kw["dimension_semantics"] = sem
    return pltpu.CompilerParams(vmem_limit_bytes=VMEM_LIMIT, **kw)


def _sigmoid(v):
    return 1.0 / (1.0 + jnp.exp(-v))


def _silu(v):
    return v * _sigmoid(v)


def _dsilu(v):
    s = _sigmoid(v)
    return s * (1.0 + v * (1.0 - s))


def _log_sigmoid(v):
    return jnp.minimum(v, 0.0) - jnp.log(1.0 + jnp.exp(-jnp.abs(v)))


def _dot(a, b, dims, precision=None):
    return lax.dot_general(a, b, (dims, ((), ())), preferred_element_type=F32, precision=precision)


def _nn(a, b, precision=None):
    return _dot(a, b, ((1,), (0,)), precision)


def _nt(a, b, precision=None):
    return _dot(a, b, ((1,), (1,)), precision)


def _tn(a, b, precision=None):
    return _dot(a, b, ((0,), (0,)), precision)


def _b16(v):
    return v.astype(BF16)


def proj_all(u, ws, bs, rows, p_sh, *, tm):
    k = u.shape[1]
    n_g = len(ws)
    tns = [w.shape[1] if w.shape[1] % 1024 else 1024 for w in ws]
    mts = [r // tm for r in rows]
    cnts = [(w.shape[1] // tn) * mt for w, tn, mt in zip(ws, tns, mts)]
    los = [sum(cnts[:g]) for g in range(n_g)]
    n_steps = sum(cnts)

    def rel(s, g):
        return jnp.clip(s - los[g], 0, cnts[g] - 1)

    def active(s, g):
        return (s >= los[g]) & (s < los[g] + cnts[g])

    def u_row(s):
        r = 0
        for g in range(n_g):
            r = r + jnp.where(active(s, g), rel(s, g) % mts[g], 0)
        return r

    def body(*refs):
        u_ref = refs[0]
        w_refs, b_refs = refs[1:1 + n_g], refs[1 + n_g:1 + 2 * n_g]
        p_ref = refs[1 + 2 * n_g]
        o_refs = refs[2 + 2 * n_g:2 + 3 * n_g]
        pall_ref = refs[2 + 3 * n_g]
        w_send, w_recv, h_send, h_recv = refs[3 + 3 * n_g:]
        s = pl.program_id(0)
        for g in range(n_g):
            @pl.when(active(s, g))
            def _(g=g):
                o_refs[g][...] = (_nn(u_ref[...], w_refs[g][...]) + b_refs[g][...]).astype(o_refs[g].dtype)

        x, y, c, chips = _place()
        q = 2 * x + y
        mine = _half_chunks(c, (0, p_ref.shape[0]), 16, which=(1,))
        other = _half_chunks(1 - c, (0, p_ref.shape[0]), 16, which=(1,))
        nb = len(mine)

        def bulk():
            return [[_remote(p_ref.at[rws], pall_ref.at[q, rws], w_send.at[pj * nb + pi], w_recv.at[pj * nb + pi],
                             (*chips[pj], c)) for pi, (_, rws) in enumerate(mine)] for pj in range(3)]

        @pl.when(s == 0)
        def _():
            for cp in sum(bulk(), []):
                cp.start()

        @pl.when(s == n_steps - 1)
        def _():
            handed = []
            for pj, (cx, cy) in enumerate(chips):
                for pi, (_, rws) in enumerate(mine):
                    bulk()[pj][pi].wait_recv()
                    cp = _remote(pall_ref.at[2 * cx + cy, rws], pall_ref.at[2 * cx + cy, rws],
                                 h_send.at[pj * nb + pi], h_recv.at[pj * nb + pi], (x, y, 1 - c))
                    cp.start()
                    handed.append(cp)
            for pj, (cx, cy) in enumerate(chips):
                for pi, (_, rws) in enumerate(other):
                    _remote(pall_ref.at[2 * cx + cy, rws], pall_ref.at[2 * cx + cy, rws],
                            h_send.at[pj * nb + pi], h_recv.at[pj * nb + pi], (x, y, 1 - c)).wait_recv()
            for cp in sum(bulk(), []) + handed:
                cp.wait_send()

    any_spec = pl.BlockSpec(memory_space=pl.ANY)
    in_specs = [pl.BlockSpec((tm, k), lambda s: (u_row(s), 0))]
    in_specs += [pl.BlockSpec((k, tns[g]), lambda s, g=g: (0, rel(s, g) // mts[g])) for g in range(n_g)]
    in_specs += [pl.BlockSpec((1, tns[g]), lambda s, g=g: (0, rel(s, g) // mts[g])) for g in range(n_g)]
    in_specs.append(any_spec)
    out_specs = [pl.BlockSpec((tm, tns[g]), lambda s, g=g: (rel(s, g) % mts[g], rel(s, g) // mts[g])) for g in range(n_g)]
    out_specs.append(any_spec)
    out_shape = [jax.ShapeDtypeStruct((rows[g], ws[g].shape[1]), BF16) for g in range(n_g)]
    out_shape.append(jax.ShapeDtypeStruct((N_CHIPS,) + p_sh.shape, p_sh.dtype))
    return _pallas(
        body, name="proj_all", grid=(n_steps,), in_specs=in_specs, out_specs=out_specs, out_shape=out_shape,
        scratch_shapes=[pltpu.SemaphoreType.DMA((3 * P_ROW_CHUNKS,)) for _ in range(4)],
        compiler_params=_params(("arbitrary",)),
    )(u, *ws, *bs, p_sh)


def matmul_tn(a, b, *, name, t, tn, tt, colsum=False, swap=None):
    m = a.shape[1]
    n = b.shape[1]
    nj, ns = n // tn, t // tt
    n_out = 2 if colsum else 1
    n_sw = 0 if swap is None else len(swap)

    def body(a_ref, b_ref, *rest):
        o_ref = rest[n_sw]
        cs_ref = rest[n_sw + 1] if colsum else None
        j, s = pl.program_id(0), pl.program_id(1)

        if swap is not None:
            g_refs = rest[:n_sw]
            got_refs = rest[n_sw + n_out:2 * n_sw + n_out]
            sems = rest[2 * n_sw + n_out:]

            @pl.when((j == 0) & (s == 0))
            def _():
                for cp in _pair_copies(g_refs, None, got_refs, None, *sems):
                    cp.start()

            @pl.when((j == nj - 1) & (s == ns - 1))
            def _():
                for cp in _pair_copies(g_refs, None, got_refs, None, *sems):
                    cp.wait_recv()
                for cp in _pair_copies(g_refs, None, got_refs, None, *sems):
                    cp.wait_send()

        @pl.when(s == 0)
        def _():
            o_ref[...] = jnp.zeros_like(o_ref)
            if colsum:
                cs_ref[...] = jnp.zeros_like(cs_ref)
        o_ref[...] += _tn(a_ref[...], b_ref[...])
        if colsum:
            cs_ref[...] += jnp.sum(b_ref[...].astype(F32), axis=0, keepdims=True)

    in_specs = [pl.BlockSpec((tt, m), lambda j, s: (s, 0)), pl.BlockSpec((tt, tn), lambda j, s: (s, j))]
    out_specs = [pl.BlockSpec((m, tn), lambda j, s: (0, j))]
    out_shape = [jax.ShapeDtypeStruct((m, n), F32)]
    if colsum:
        out_specs.append(pl.BlockSpec((1, tn), lambda j, s: (0, j)))
        out_shape.append(jax.ShapeDtypeStruct((1, n), F32))
    args, scratch = [a, b], []
    if swap is not None:
        any_spec = pl.BlockSpec(memory_space=pl.ANY)
        in_specs += [any_spec] * n_sw
        out_specs += [any_spec] * n_sw
        out_shape += _pair_got_shapes(swap, None)
        args += list(swap)
        scratch = [pltpu.SemaphoreType.DMA((_pair_count(swap, None),)), pltpu.SemaphoreType.DMA((_pair_count(swap, None),))]
    return _pallas(
        body, name=name, grid=(nj, ns), in_specs=in_specs, out_specs=out_specs, out_shape=out_shape,
        scratch_shapes=scratch,
        compiler_params=_params(("parallel" if swap is None else "arbitrary", "arbitrary")),
    )(*args)


def dgrad_norm_bwd(dps, wts, paw, x2, ctx2, dh, scale1, norm_g, *, tm):
    t, tc = x2.shape[0], ctx2.shape[0]
    t_all = t + tc
    n_lat, n_ctx = t // tm, tc // tm
    n_tiles = n_lat + n_ctx
    n_samples = scale1.shape[0] - 1
    tps = n_lat // n_samples
    n_grp = n_samples + 1
    n_g = len(dps)
    whole = [g for g in range(n_g) if dps[g].shape[0] == t_all]
    latent = [g for g in range(n_g) if dps[g].shape[0] != t_all]

    def body(*refs):
        dp_refs, w_refs = refs[:n_g], refs[n_g:2 * n_g]
        (paw_ref, x_ref, c_ref, dh_ref, sc_ref, g_ref, dx_ref, dsh_ref, dsc_ref, dg_ref, rbw_ref,
         du_buf, b_send, b_recv) = refs[2 * n_g:]
        i = pl.program_id(0)

        def exchange():
            x, y, c, chips = _place()
            chunks = _half_chunks(0, (2 * paw_ref.shape[1],), 16, which=(0,))
            return [_remote(paw_ref.at[2 * cx + cy, rows], rbw_ref.at[j, rows],
                            b_send.at[j * N_BULK + k], b_recv.at[j * N_BULK + k], (cx, cy, c))
                    for j, (cx, cy) in enumerate(chips) for k, (_, rows) in enumerate(chunks)]

        @pl.when(i == 0)
        def _():
            for cp in exchange():
                cp.start()

        acc = None
        for g in whole:
            part = _nt(dp_refs[g][...], w_refs[g][...])
            acc = part if acc is None else acc + part
        du_buf[...] = acc

        @pl.when(i < n_lat)
        def _():
            lat = None
            for g in latent:
                part = _nt(dp_refs[g][...], w_refs[g][...])
                lat = part if lat is None else lat + part
            du_buf[...] += lat

        duv = du_buf[...]
        xv = jnp.where(i < n_lat, x_ref[...], c_ref[...])
        rs = lax.rsqrt(jnp.mean(xv * xv, axis=-1, keepdims=True) + EPS)
        xh = xv * rs
        n = xh * g_ref[...]
        dn = duv * sc_ref[0]
        dxh = dn * g_ref[...]
        dx = rs * (dxh - xh * jnp.mean(dxh * xh, axis=-1, keepdims=True))

        @pl.when(i < n_lat)
        def _():
            dx_ref[...] = dx + dh_ref[...]

        @pl.when((i % tps == 0) & (i <= n_lat))
        def _():
            dsh_ref[...] = jnp.zeros_like(dsh_ref)
            dsc_ref[...] = jnp.zeros_like(dsc_ref)

        @pl.when(i == 0)
        def _():
            dg_ref[...] = jnp.zeros_like(dg_ref)

        dsh_ref[0] += jnp.sum(duv, axis=0, keepdims=True)
        dsc_ref[0] += jnp.sum(duv * n, axis=0, keepdims=True)
        dg_ref[...] += jnp.sum(dn * xh, axis=0, keepdims=True)

        @pl.when(i == n_tiles - 1)
        def _():
            for cp in exchange():
                cp.wait_recv()
            for cp in exchange():
                cp.wait_send()

    lat = lambda i: (jnp.minimum(i, n_lat - 1), 0)
    grp = lambda i: (jnp.minimum(i // tps, n_samples), 0, 0)
    in_specs = []
    for g, dp in enumerate(dps):
        nrow = dp.shape[0] // tm
        in_specs.append(pl.BlockSpec((tm, dp.shape[1]), lambda i, nrow=nrow: (jnp.minimum(i, nrow - 1), 0)))
    for w in wts:
        in_specs.append(pl.BlockSpec(w.shape, lambda i: (0, 0), pipeline_mode=pl.Buffered(1)))
    any_spec = pl.BlockSpec(memory_space=pl.ANY)
    in_specs += [any_spec,
                 pl.BlockSpec((tm, D), lat), pl.BlockSpec((tm, D), lambda i: (jnp.maximum(i - n_lat, 0), 0)),
                 pl.BlockSpec((tm, D), lat), pl.BlockSpec((1, 1, D), grp), pl.BlockSpec((1, D), lambda i: (0, 0))]
    return _pallas(
        body, name="dgrad_norm_bwd", grid=(n_tiles,), in_specs=in_specs,
        out_specs=[pl.BlockSpec((tm, D), lat), pl.BlockSpec((1, 1, D), grp), pl.BlockSpec((1, 1, D), grp),
                   pl.BlockSpec((1, D), lambda i: (0, 0)), any_spec],
        out_shape=[jax.ShapeDtypeStruct((t, D), F32), jax.ShapeDtypeStruct((n_grp, 1, D), F32),
                   jax.ShapeDtypeStruct((n_grp, 1, D), F32), jax.ShapeDtypeStruct((1, D), F32),
                   jax.ShapeDtypeStruct((3,) + paw.shape[1:], paw.dtype)],
        scratch_shapes=[pltpu.VMEM((tm, D), F32), pltpu.SemaphoreType.DMA((3 * N_BULK,)),
                        pltpu.SemaphoreType.DMA((3 * N_BULK,))],
        compiler_params=_params(("arbitrary",)),
    )(*dps, *wts, paw, x2, ctx2, dh, scale1, norm_g)


TM_NORM = 512


def norm_mod_fwd(x2, ctx2, scale1, shift, norm_g):
    t = x2.shape[0]
    n_lat = t // TM_NORM
    assert ctx2.shape[0] == TM_NORM
    n_samples = scale1.shape[0] - 1
    tps = n_lat // n_samples

    def body(x_ref, c_ref, sc_ref, sh_ref, g_ref, u_ref):
        i = pl.program_id(0)
        xv = jnp.where(i < n_lat, x_ref[...], c_ref[...])
        rs = lax.rsqrt(jnp.mean(xv * xv, axis=-1, keepdims=True) + EPS)
        u = xv * rs * g_ref[...] * sc_ref[0] + sh_ref[0]
        u_ref[...] = u.astype(u_ref.dtype)

    grp = lambda i: (jnp.minimum(i // tps, n_samples), 0, 0)
    return _pallas(
        body, name="norm_mod_fwd", grid=(n_lat + 1,),
        in_specs=[pl.BlockSpec((TM_NORM, D), lambda i: (jnp.minimum(i, n_lat - 1), 0)),
                  pl.BlockSpec((TM_NORM, D), lambda i: (0, 0)),
                  pl.BlockSpec((1, 1, D), grp), pl.BlockSpec((1, 1, D), grp),
                  pl.BlockSpec((1, D), lambda i: (0, 0))],
        out_specs=pl.BlockSpec((TM_NORM, D), lambda i: (i, 0)),
        out_shape=jax.ShapeDtypeStruct((t + TM_NORM, D), BF16),
        compiler_params=_params(("parallel",)),
    )(x2, ctx2, scale1, shift, norm_g)


CONV_CB = 256
CONV_NCB = D // CONV_CB
H_OFF = 16


H_CB = 128
H_SPAN = GRID_W + 2 * H_OFF - 8


def _conv_scratch(vertical):
    if vertical:
        return [pltpu.VMEM((GRID_H + 2 * CONV_PAD, GRID_W, CONV_CB), F32)]
    return [pltpu.VMEM((GRID_H, GRID_W + 2 * H_OFF, H_CB), F32), pltpu.VMEM((7, GRID_H, H_SPAN, H_CB), F32)]


def _conv_fill(bufs, img, vertical):
    pad_ref = bufs[0]
    pad_ref[...] = jnp.zeros_like(pad_ref)
    if vertical:
        pad_ref[pl.ds(CONV_PAD, GRID_H)] = img
        return
    pad_ref[:, pl.ds(H_OFF, GRID_W), :] = img

    def shift(r, carry):
        for s in range(1, 8):
            bufs[1][s - 1, r] = pad_ref[r, pl.ds(s, H_SPAN), :]
        return carry

    lax.fori_loop(0, GRID_H, shift, 0)


def _conv_window(bufs, k, vertical, r, w0=0, nw=GRID_W, lanes=slice(None)):
    if vertical:
        return bufs[0][r + k, pl.ds(w0, nw), lanes]
    off = H_OFF - CONV_PAD + k
    if off % 8 == 0:
        return bufs[0][r, pl.ds(off + w0, nw), lanes]
    return bufs[1][off % 8 - 1, r, pl.ds(off - off % 8 + w0, nw), lanes]


def _conv_col_blocks(vertical):
    if vertical:
        return [pl.ds(0, CONV_CB)]
    return [pl.ds(i * H_CB, H_CB) for i in range(CONV_CB // H_CB)]


def _rows(r):
    return pl.ds(pl.multiple_of(r * GRID_W, GRID_W), GRID_W)


def conv_fwd(p1, conv_w, conv_b, n_samples):
    t = n_samples * SEQ

    def make(vertical, prev):
        n_buf = len(_conv_scratch(vertical))

        def body(gv_ref, gg_ref, w_ref, b_ref, *rest):
            o_ref, bufs = rest[-1 - n_buf], rest[-n_buf:]
            for cols in _conv_col_blocks(vertical):
                a = gv_ref[:, cols].astype(F32) * _sigmoid(gg_ref[:, cols].astype(F32))
                _conv_fill(bufs, a.reshape(GRID_H, GRID_W, a.shape[-1]), vertical)

                def row(r, carry, cols=cols):
                    acc = jnp.zeros((GRID_W, cols.size), F32) + b_ref[:, cols]
                    for k in range(CONV_K):
                        acc = acc + _conv_window(bufs, k, vertical, r) * w_ref[pl.ds(k, 1), cols]
                    o_ref[_rows(r), cols] = acc
                    return carry

                lax.fori_loop(0, GRID_H, row, 0)

        cb0 = CONV_NCB // 2 if vertical else 0
        in_specs = [pl.BlockSpec((SEQ, CONV_CB), lambda b, j: (b, 2 * (cb0 + j))),
                    pl.BlockSpec((SEQ, CONV_CB), lambda b, j: (b, 2 * (cb0 + j) + 1)),
                    pl.BlockSpec((CONV_K + 1, CONV_CB), lambda b, j: (0, cb0 + j)),
                    pl.BlockSpec((1, CONV_CB), lambda b, j: (0, cb0 + j))]
        args = [p1, p1, conv_w, conv_b]
        aliases = {}
        if prev is not None:
            in_specs.append(pl.BlockSpec(memory_space=pl.ANY))
            args.append(prev)
            aliases = {4: 0}
        return _pallas(
            body, name="conv_fwd_v" if vertical else "conv_fwd_h", grid=(n_samples, CONV_NCB // 2),
            in_specs=in_specs,
            out_specs=pl.BlockSpec((SEQ, CONV_CB), lambda b, j: (b, cb0 + j)),
            out_shape=jax.ShapeDtypeStruct((t, D), F32),
            scratch_shapes=_conv_scratch(vertical),
            input_output_aliases=aliases,
            compiler_params=_params(("parallel", "parallel")),
        )(*args)

    return make(True, make(False, None))


def conv_bwd(p1, daconv, conv_w, n_samples):
    t = n_samples * SEQ

    def make(vertical, prev):
        n_buf = len(_conv_scratch(vertical))

        def body(gv_ref, gg_ref, dy_ref, w_ref, *rest):
            dp_ref, dw_ref, db_ref = rest[-3 - 2 * n_buf - 1:-2 * n_buf - 1]
            a_bufs, d_bufs, da_ref = rest[-2 * n_buf - 1:-n_buf - 1], rest[-n_buf - 1:-1], rest[-1]
            for cols in _conv_col_blocks(vertical):
                width = cols.size
                gv = gv_ref[:, cols].astype(F32)
                sg = _sigmoid(gg_ref[:, cols].astype(F32))
                _conv_fill(a_bufs, (gv * sg).reshape(GRID_H, GRID_W, width), vertical)
                _conv_fill(d_bufs, dy_ref[:, cols].reshape(GRID_H, GRID_W, width), vertical)

                def row(r, carry, cols=cols, width=width):
                    acc = jnp.zeros((GRID_W, width), F32)
                    for k in range(CONV_K):
                        acc = acc + _conv_window(d_bufs, CONV_K - 1 - k, vertical, r) * w_ref[pl.ds(k, 1), cols]
                    da_ref[_rows(r), cols] = acc
                    return carry

                lax.fori_loop(0, GRID_H, row, 0)
                da = da_ref[:, cols]
                dp_ref[:, pl.ds(cols.start, width)] = (da * sg).astype(dp_ref.dtype)
                dp_ref[:, pl.ds(CONV_CB + cols.start, width)] = (da * gv * sg * (1.0 - sg)).astype(dp_ref.dtype)

                for lb in range(width // 128):
                    lanes = pl.ds(lb * 128, 128)
                    dy_lanes = pl.ds(cols.start + lb * 128, 128)

                    def wrow(r, accs, lanes=lanes, dy_lanes=dy_lanes):
                        for w0 in range(0, GRID_W, 8):
                            dyv = dy_ref[pl.ds(pl.multiple_of(r * GRID_W, GRID_W) + w0, 8), dy_lanes]
                            accs = tuple(accs[k] + _conv_window(a_bufs, k, vertical, r, w0, 8, lanes) * dyv
                                         for k in range(CONV_K))
                        return accs

                    accs = lax.fori_loop(0, GRID_H, wrow, tuple(jnp.zeros((8, 128), F32) for _ in range(CONV_K)))
                    for k in range(CONV_K):
                        dw_ref[0, pl.ds(k, 1), dy_lanes] = jnp.sum(accs[k], axis=0, keepdims=True)
            dw_ref[0, pl.ds(CONV_K, 1), :] = jnp.zeros((1, CONV_CB), F32)
            db_ref[0] = jnp.sum(dy_ref[...], axis=0, keepdims=True)

        cb0 = CONV_NCB // 2 if vertical else 0
        in_specs = [pl.BlockSpec((SEQ, CONV_CB), lambda b, j: (b, 2 * (cb0 + j))),
                    pl.BlockSpec((SEQ, CONV_CB), lambda b, j: (b, 2 * (cb0 + j) + 1)),
                    pl.BlockSpec((SEQ, CONV_CB), lambda b, j: (b, cb0 + j)),
                    pl.BlockSpec((CONV_K + 1, CONV_CB), lambda b, j: (0, cb0 + j))]
        args = [p1, p1, daconv, conv_w]
        aliases = {}
        if prev is not None:
            in_specs += [pl.BlockSpec(memory_space=pl.ANY)] * 3
            args += list(prev)
            aliases = {4: 0, 5: 1, 6: 2}
        return _pallas(
            body, name="conv_bwd_v" if vertical else "conv_bwd_h", grid=(n_samples, CONV_NCB // 2),
            in_specs=in_specs,
            out_specs=[pl.BlockSpec((SEQ, 2 * CONV_CB), lambda b, j: (b, cb0 + j)),
                       pl.BlockSpec((1, CONV_K + 1, CONV_CB), lambda b, j: (b, 0, cb0 + j)),
                       pl.BlockSpec((1, 1, CONV_CB), lambda b, j: (b, 0, cb0 + j))],
            out_shape=[jax.ShapeDtypeStruct((t, 2 * D), BF16),
                       jax.ShapeDtypeStruct((n_samples, CONV_K + 1, D), F32),
                       jax.ShapeDtypeStruct((n_samples, 1, D), F32)],
            scratch_shapes=_conv_scratch(vertical) + _conv_scratch(vertical) + [pltpu.VMEM((SEQ, CONV_CB), F32)],
            input_output_aliases=aliases,
            compiler_params=_params(("parallel", "parallel")),
        )(*args)

    return make(True, make(False, None))


TM_EW = 256


def ln_gate_proj(aconv, z, ln_g, ln_b, conv_proj):
    t = aconv.shape[0]

    def body(a_ref, z_ref, g_ref, b_ref, w_ref, o_ref, y_ref):
        a = a_ref[...]
        mu = jnp.mean(a, axis=-1, keepdims=True)
        xc = a - mu
        rstd = lax.rsqrt(jnp.mean(xc * xc, axis=-1, keepdims=True) + EPS)
        l = xc * rstd * g_ref[...] + b_ref[...]
        ac = _b16(_silu(l) * _silu(z_ref[...].astype(F32)))
        o_ref[...] = ac
        y_ref[...] = _nn(ac, w_ref[...]).astype(y_ref.dtype)

    row = pl.BlockSpec((TM_OUT, D), lambda i: (i, 0))
    vec = pl.BlockSpec((1, D), lambda i: (0, 0))
    return _pallas(
        body, name="ln_gate_proj", grid=(t // TM_OUT,),
        in_specs=[row, row, vec, vec, pl.BlockSpec((D, D), lambda i: (0, 0))], out_specs=[row, row],
        out_shape=[jax.ShapeDtypeStruct((t, D), BF16), jax.ShapeDtypeStruct((t, D), BF16)],
        compiler_params=_params(("parallel",)),
    )(aconv, z, ln_g, ln_b, conv_proj)


def ln_gate_bwd(aconv, z, dyc, conv_proj, ln_g, ln_b, ac):
    t = aconv.shape[0]

    def body(a_ref, z_ref, d_ref, w_ref, g_ref, b_ref, ac_ref, da_ref, dz_ref, dg_ref, db_ref, gw_ref):
        @pl.when(pl.program_id(0) == 0)
        def _():
            gw_ref[...] = jnp.zeros_like(gw_ref)

        gw_ref[...] += _tn(ac_ref[...], d_ref[...])
        a = a_ref[...]
        zv = z_ref[...].astype(F32)
        dac_v = _nt(d_ref[...], w_ref[...])
        mu = jnp.mean(a, axis=-1, keepdims=True)
        xc = a - mu
        rstd = lax.rsqrt(jnp.mean(xc * xc, axis=-1, keepdims=True) + EPS)
        xh = xc * rstd
        l = xh * g_ref[...] + b_ref[...]
        dz_ref[...] = (dac_v * _silu(l) * _dsilu(zv)).astype(dz_ref.dtype)
        dl = dac_v * _silu(zv) * _dsilu(l)
        dxh = dl * g_ref[...]
        da_ref[...] = rstd * (dxh - jnp.mean(dxh, axis=-1, keepdims=True)
                              - xh * jnp.mean(dxh * xh, axis=-1, keepdims=True))

        @pl.when(pl.program_id(0) == 0)
        def _():
            dg_ref[...] = jnp.zeros_like(dg_ref)
            db_ref[...] = jnp.zeros_like(db_ref)

        dg_ref[...] += jnp.sum(dl * xh, axis=0, keepdims=True)
        db_ref[...] += jnp.sum(dl, axis=0, keepdims=True)

    row = pl.BlockSpec((TM_EW, D), lambda i: (i, 0))
    vec = pl.BlockSpec((1, D), lambda i: (0, 0))
    return _pallas(
        body, name="ln_gate_bwd", grid=(t // TM_EW,),
        in_specs=[row, row, row, pl.BlockSpec((D, D), lambda i: (0, 0)), vec, vec, row],
        out_specs=[row, row, vec, vec, pl.BlockSpec((D, D), lambda i: (0, 0))],
        out_shape=[jax.ShapeDtypeStruct((t, D), F32), jax.ShapeDtypeStruct((t, D), BF16),
                   jax.ShapeDtypeStruct((1, D), F32), jax.ShapeDtypeStruct((1, D), F32),
                   jax.ShapeDtypeStruct((D, D), F32)],
        compiler_params=_params(("arbitrary",)),
    )(aconv, z, dyc, conv_proj, ln_g, ln_b, ac)


TM_PREP = 256
PREP_LAT = SEQ // TM_PREP
PREP_ALL = SEQ_ALL // TM_PREP


def _chunk_tri(n, upper):
    r = lax.broadcasted_iota(jnp.int32, (n, n), 0)
    c = lax.broadcasted_iota(jnp.int32, (n, n), 1)
    same = (r // CHUNK) == (c // CHUNK)
    keep = (c >= r) if upper else (c <= r)
    return jnp.where(same & keep, 1.0, 0.0).astype(F32)


def _split3(v):
    hi = v.astype(BF16)
    r1 = v - hi.astype(F32)
    mid = r1.astype(BF16)
    lo = (r1 - mid.astype(F32)).astype(BF16)
    return jnp.stack([hi, mid, lo])


def _chunk_sums(v, upper):
    tri = _chunk_tri(v.shape[0], upper).astype(BF16)
    pieces = _split3(v)
    return (_nn(tri, pieces[0]) + _nn(tri, pieces[1])) + _nn(tri, pieces[2])


def _gate_logits(ab, up3_ref, bias_ref):
    assert ab.dtype == BF16
    return ((_nn(ab, up3_ref[0]) + _nn(ab, up3_ref[1])) + _nn(ab, up3_ref[2])) + bias_ref[...]


def _prep_tile_maps(n_samples):
    n_lat = n_samples * PREP_LAT

    def seq_map(i):
        return jnp.where(i < n_lat, i // PREP_LAT, i - n_lat), jnp.where(i < n_lat, i % PREP_LAT, PREP_LAT)

    return n_lat, seq_map


def gla_prep_fwd(p3, upf, upb, bias_f, bias_b, n_samples):
    n_lat, seq_map = _prep_tile_maps(n_samples)
    n_tiles = n_lat + n_samples

    def body(v_ref, q_ref, k_ref, ab_ref, upf_ref, upb_ref, bf_ref, bb_ref, qo, ko, vo, cf, cb):
        i = pl.program_id(0)
        qo[0] = jnp.where(i < n_lat, q_ref[...].astype(F32) * Q_SCALE, 0.0)
        ko[0] = k_ref[...]
        vo[0] = v_ref[...]
        ab = ab_ref[...]
        gf = _log_sigmoid(_gate_logits(ab, upf_ref, bf_ref)) * (1.0 / GATE_TAU)
        gb = _log_sigmoid(_gate_logits(ab, upb_ref, bb_ref)) * (1.0 / GATE_TAU)
        cf[0] = _chunk_sums(gf, False)
        cb[0] = _chunk_sums(gb, True)

    def o_spec(w):
        return pl.BlockSpec((1, TM_PREP, w), lambda i: (*seq_map(i), 0))

    full = lambda shape: pl.BlockSpec(shape, lambda i: (0,) * len(shape))
    return _pallas(
        body, name="gla_prep_fwd", grid=(n_tiles,),
        in_specs=[pl.BlockSpec((TM_PREP, 1024), lambda i: (i, O3_V // 1024)),
                  pl.BlockSpec((TM_PREP, 512), lambda i: (i, O3_Q // 512)),
                  pl.BlockSpec((TM_PREP, 512), lambda i: (i, O3_K // 512)),
                  pl.BlockSpec((TM_PREP, 128), lambda i: (i, O3_AB // 128)),
                  full((3, 128, GLA_DK)), full((3, 128, GLA_DK)), full((1, GLA_DK)), full((1, GLA_DK))],
        out_specs=[o_spec(GLA_DK), o_spec(GLA_DK), o_spec(D), o_spec(GLA_DK), o_spec(GLA_DK)],
        out_shape=[jax.ShapeDtypeStruct((n_samples, SEQ_ALL, GLA_DK), F32),
                   jax.ShapeDtypeStruct((n_samples, SEQ_ALL, GLA_DK), p3.dtype),
                   jax.ShapeDtypeStruct((n_samples, SEQ_ALL, D), p3.dtype),
                   jax.ShapeDtypeStruct((n_samples, SEQ_ALL, GLA_DK), F32),
                   jax.ShapeDtypeStruct((n_samples, SEQ_ALL, GLA_DK), F32)],
        compiler_params=_params(("parallel",)),
    )(p3, p3, p3, p3, upf, upb, bias_f, bias_b)


def gla_prep_bwd(p3, dq_f, dq_b, dk_f, dk_b, dv_f, dv_b, dc_f, dc_b, upf, upb, bias_f, bias_b, n_samples):
    n_lat, seq_map = _prep_tile_maps(n_samples)
    n_tiles = n_lat + n_samples

    def body(ab_ref, dqf, dqb, dkf, dkb, dvf, dvb, dcf, dcb, upf_ref, upb_ref, bf_ref, bb_ref,
             dp_ref, duf_ref, dub_ref, dbf_ref, dbb_ref):
        i = pl.program_id(0)
        both = lambda a, b: a[0].astype(F32) + b[0].astype(F32)
        dp_ref[:, pl.ds(O3_V, D)] = both(dvf, dvb).astype(dp_ref.dtype)
        dq = jnp.where(i < n_lat, both(dqf, dqb) * Q_SCALE, 0.0)
        dp_ref[:, pl.ds(O3_Q, GLA_DK)] = dq.astype(dp_ref.dtype)
        dp_ref[:, pl.ds(O3_K, GLA_DK)] = both(dkf, dkb).astype(dp_ref.dtype)
        ab = ab_ref[...]
        zf = _gate_logits(ab, upf_ref, bf_ref)
        zb = _gate_logits(ab, upb_ref, bb_ref)
        dgf = _chunk_sums(dcf[0], True)
        dgb = _chunk_sums(dcb[0], False)
        dzf = _b16(dgf * (1.0 / GATE_TAU) * _sigmoid(-zf))
        dzb = _b16(dgb * (1.0 / GATE_TAU) * _sigmoid(-zb))
        dab = _nt(dzf, upf_ref[0]) + _nt(dzb, upb_ref[0])
        dp_ref[:, pl.ds(O3_AB, 128)] = dab.astype(dp_ref.dtype)

        @pl.when(i == 0)
        def _():
            duf_ref[...] = jnp.zeros_like(duf_ref)
            dub_ref[...] = jnp.zeros_like(dub_ref)
            dbf_ref[...] = jnp.zeros_like(dbf_ref)
            dbb_ref[...] = jnp.zeros_like(dbb_ref)

        duf_ref[...] += _tn(ab, dzf)
        dub_ref[...] += _tn(ab, dzb)
        dbf_ref[...] += jnp.sum(dzf.astype(F32), axis=0, keepdims=True)
        dbb_ref[...] += jnp.sum(dzb.astype(F32), axis=0, keepdims=True)

    def s_spec(w):
        return pl.BlockSpec((1, TM_PREP, w), lambda i: (*seq_map(i), 0))

    full = lambda shape: pl.BlockSpec(shape, lambda i: (0,) * len(shape))
    return _pallas(
        body, name="gla_prep_bwd", grid=(n_tiles,),
        in_specs=[pl.BlockSpec((TM_PREP, 128), lambda i: (i, O3_AB // 128)),
                  s_spec(GLA_DK), s_spec(GLA_DK), s_spec(GLA_DK), s_spec(GLA_DK), s_spec(D), s_spec(D),
                  s_spec(GLA_DK), s_spec(GLA_DK),
                  full((3, 128, GLA_DK)), full((3, 128, GLA_DK)), full((1, GLA_DK)), full((1, GLA_DK))],
        out_specs=[pl.BlockSpec((TM_PREP, W3), lambda i: (i, 0)),
                   full((128, GLA_DK)), full((128, GLA_DK)), full((1, GLA_DK)), full((1, GLA_DK))],
        out_shape=[jax.ShapeDtypeStruct((n_tiles * TM_PREP, W3), BF16),
                   jax.ShapeDtypeStruct((128, GLA_DK), F32), jax.ShapeDtypeStruct((128, GLA_DK), F32),
                   jax.ShapeDtypeStruct((1, GLA_DK), F32), jax.ShapeDtypeStruct((1, GLA_DK), F32)],
        compiler_params=_params(("arbitrary",)),
    )(p3, dq_f, dq_b, dk_f, dk_b, dv_f, dv_b, dc_f, dc_b, upf, upb, bias_f, bias_b)


def _sub_blocks(rev):
    if NSUB == 1:
        return [((0, CHUNK), CHUNK // 2, (0, CHUNK))]
    out = []
    for s in range(NSUB):
        rows = (s * SUB, SUB)
        if rev:
            ref = (s + 1) * SUB if s < NSUB - 1 else None
            cols = (s * SUB, CHUNK - s * SUB)
        else:
            ref = s * SUB - 1 if s > 0 else None
            cols = (0, (s + 1) * SUB)
        out.append((rows, ref, cols))
    return out


def _sub_mask(rows, cols, rev):
    r = rows[0] + lax.broadcasted_iota(jnp.int32, (rows[1], cols[1]), 0)
    c = cols[0] + lax.broadcasted_iota(jnp.int32, (rows[1], cols[1]), 1)
    return (c >= r) if rev else (c <= r)


def _sub_operands(qc, kc, cc, rows, ref, cols):
    cref = jnp.zeros((1, HEAD_K), F32) if ref is None else cc[ref:ref + 1]
    eq = jnp.exp(cc[rows[0]:rows[0] + rows[1]] - cref)
    ek = jnp.exp(cref - cc[cols[0]:cols[0] + cols[1]])
    qs = qc[rows[0]:rows[0] + rows[1]] * eq
    kk = kc[cols[0]:cols[0] + cols[1]] * ek
    return qs, kk, eq, ek


SCAN_ROWS = 256
SCAN_CHUNKS = SCAN_ROWS // CHUNK
SCAN_STEPS = SEQ_ALL // SCAN_ROWS
LAT_BLOCKS = SEQ // SCAN_ROWS


def _scan_block(t, rev):
    if rev:
        return SCAN_STEPS - 1 - t
    return jnp.where(t == 0, SCAN_STEPS - 1, t - 1)


def _scan_lat_block(t, rev):
    if rev:
        return jnp.minimum(SCAN_STEPS - 1 - t, LAT_BLOCKS - 1)
    return jnp.maximum(t - 1, 0)


def _head_cols(h):
    return pl.ds(h * HEAD_K, HEAD_K), pl.ds(h * HEAD_V, HEAD_V)


def gla_scan_fwd(q, k, v, cum, *, rev, name):
    n = q.shape[0]

    def body(q_ref, k_ref, v_ref, c_ref, o_ref, s_ref, sfin_ref, st):
        t = pl.program_id(1)

        @pl.when(t == 0)
        def _():
            st[...] = jnp.zeros_like(st)

        def chunk(j, carry):
            lj = SCAN_CHUNKS - 1 - j if rev else j
            r0 = lj * CHUNK
            rws = pl.ds(r0, CHUNK)
            for h in range(HEADS):
                kcols, vcols = _head_cols(h)
                qc, kc, cc = q_ref[0, rws, kcols], k_ref[0, rws, kcols], c_ref[0, rws, kcols]
                vc = v_ref[0, rws, vcols]
                s_in = st[h]
                s_ref[0, h, j] = _b16(s_in)
                edge = cc[0:1] if rev else cc[CHUNK - 1:CHUNK]
                ke = kc * jnp.exp(edge - cc)
                st[h] = s_in * jnp.exp(edge) + _tn(_b16(vc), _b16(ke))
                o_inter = _nt(_b16(qc * jnp.exp(cc)), _b16(s_in))
                vb = _b16(vc)
                for rows, ref, cols in _sub_blocks(rev):
                    qs, kk, _, _ = _sub_operands(qc, kc, cc, rows, ref, cols)
                    a = jnp.where(_sub_mask(rows, cols, rev), _nt(_b16(qs), _b16(kk)), 0.0)
                    o_s = _nn(_b16(a), vb[cols[0]:cols[0] + cols[1]])
                    o_ref[0, pl.ds(r0 + rows[0], rows[1]), vcols] = _b16(o_inter[rows[0]:rows[0] + rows[1]] + o_s)
            return carry

        for j in range(SCAN_CHUNKS):
            chunk(j, 0)

        @pl.when(t == SCAN_STEPS - 1)
        def _():
            sfin_ref[0] = st[...]

    def spec(w):
        return pl.BlockSpec((1, SCAN_ROWS, w), lambda b, t: (b, _scan_block(t, rev), 0))

    return _pallas(
        body, name=name, grid=(n, SCAN_STEPS),
        in_specs=[spec(GLA_DK), spec(GLA_DK), spec(D), spec(GLA_DK)],
        out_specs=[pl.BlockSpec((1, SCAN_ROWS, D), lambda b, t: (b, _scan_lat_block(t, rev), 0)),
                   pl.BlockSpec((1, HEADS, SCAN_CHUNKS, HEAD_V, HEAD_K), lambda b, t: (b, 0, t, 0, 0)),
                   pl.BlockSpec((1, HEADS, HEAD_V, HEAD_K), lambda b, t: (b, 0, 0, 0))],
        out_shape=[jax.ShapeDtypeStruct((n, SEQ, D), BF16),
                   jax.ShapeDtypeStruct((n, HEADS, NCHUNK, HEAD_V, HEAD_K), BF16),
                   jax.ShapeDtypeStruct((n, HEADS, HEAD_V, HEAD_K), F32)],
        scratch_shapes=[pltpu.VMEM((HEADS, HEAD_V, HEAD_K), F32)],
        compiler_params=_params(("parallel", "arbitrary")),
    )(q, k, v, cum)


def gla_scan_bwd(q, k, v, cum, s_all, s_fin, do, *, rev, name, rider=None):
    n = q.shape[0]

    def body(q_ref, k_ref, v_ref, c_ref, s_ref, sfin_ref, do_ref, *rest):
        if rider is not None:
            ride_in, rest = rest[0], rest[1:]
        dq_ref, dk_ref, dv_ref, dc_ref = rest[:4]
        if rider is not None:
            ride_out, rest = rest[4], rest[:4] + rest[5:]
        dst, s_next, dq_acc, dk_acc, dv_acc = rest[4:9]
        t = SCAN_STEPS - 1 - pl.program_id(1)

        if rider is not None:
            def copies():
                send, recv = rest[9], rest[10]
                if rider[0] == "swap":
                    return _pair_copies([], ride_in, [], ride_out, send, recv)
                x, y, c, chips = _place()
                return [_remote(ride_in.at[2 * cx + cy, rows], ride_out.at[pj, rows], send.at[pj * P_ROW_CHUNKS + pi],
                                recv.at[pj * P_ROW_CHUNKS + pi], (cx, cy, c))
                        for pj, (cx, cy) in enumerate(chips)
                        for pi, (_, rows) in enumerate(_half_chunks(0, (0, 2 * ride_in.shape[1]), 16, which=(1,)))]

            @pl.when((pl.program_id(0) == 0) & (pl.program_id(1) == 0))
            def _():
                for cp in copies():
                    cp.start()

            @pl.when((pl.program_id(0) == n - 1) & (pl.program_id(1) == SCAN_STEPS - 1))
            def _():
                for cp in copies():
                    cp.wait_recv()
                for cp in copies():
                    cp.wait_send()

        @pl.when(pl.program_id(1) == 0)
        def _():
            dst[...] = jnp.zeros_like(dst)
            s_next[...] = sfin_ref[0]

        def chunk(jj, carry):
            j = SCAN_CHUNKS - 1 - jj
            lj = SCAN_CHUNKS - 1 - j if rev else j
            rws = pl.ds(lj * CHUNK, CHUNK)
            for h in range(HEADS):
                kcols, vcols = _head_cols(h)
                qc, kc, cc = q_ref[0, rws, kcols], k_ref[0, rws, kcols], c_ref[0, rws, kcols]
                vc = v_ref[0, rws, vcols]
                doc = jnp.where(t > 0, do_ref[0, rws, vcols], 0.0)
                s_in = s_ref[0, h, j]
                s_out = s_next[h]
                ds_out = dst[h]
                edge = cc[0:1] if rev else cc[CHUNK - 1:CHUNK]
                e_q = jnp.exp(cc)
                e_k = jnp.exp(edge - cc)
                dob = _b16(doc)
                dsb = _b16(ds_out)
                dst[h] = ds_out * jnp.exp(edge) + _tn(dob, _b16(qc * e_q))
                s_next[h] = s_in.astype(F32)
                dq_acc[h] = e_q * _nn(dob, s_in)
                dk_acc[h] = e_k * _nn(_b16(vc), dsb)
                dv_acc[h] = _nt(_b16(kc * e_k), dsb)
                vb = _b16(vc)
                for rows, ref, cols in _sub_blocks(rev):
                    qs, kk, eq, ek = _sub_operands(qc, kc, cc, rows, ref, cols)
                    mask = _sub_mask(rows, cols, rev)
                    rsl = slice(rows[0], rows[0] + rows[1])
                    csl = pl.ds(cols[0], cols[1])
                    qsb, kkb = _b16(qs), _b16(kk)
                    a = jnp.where(mask, _nt(qsb, kkb), 0.0)
                    da = _b16(jnp.where(mask, _nt(dob[rsl], vb[cols[0]:cols[0] + cols[1]]), 0.0))
                    dq_acc[h, pl.ds(rows[0], rows[1]), :] += _nn(da, kkb) * eq
                    dk_acc[h, csl, :] += _tn(da, qsb) * ek
                    dv_acc[h, csl, :] += _tn(_b16(a), dob[rsl])
                dq = dq_acc[h]
                dk = dk_acc[h]
                dc = qc * dq - kc * dk
                bnd = jnp.sum(ds_out * s_out, axis=0, keepdims=True)
                edge_row = 0 if rev else CHUNK - 1
                is_edge = lax.broadcasted_iota(jnp.int32, (CHUNK, HEAD_K), 0) == edge_row
                dq_ref[0, rws, kcols] = _b16(dq)
                dk_ref[0, rws, kcols] = _b16(dk)
                dv_ref[0, rws, vcols] = _b16(dv_acc[h])
                dc_ref[0, rws, kcols] = dc + jnp.where(is_edge, bnd, 0.0)
            return carry

        for jj in range(SCAN_CHUNKS):
            chunk(jj, 0)

    def step_of(u):
        return SCAN_STEPS - 1 - u

    def spec(w):
        return pl.BlockSpec((1, SCAN_ROWS, w), lambda b, u: (b, _scan_block(step_of(u), rev), 0))

    in_specs = [spec(GLA_DK), spec(GLA_DK), spec(D), spec(GLA_DK),
                pl.BlockSpec((1, HEADS, SCAN_CHUNKS, HEAD_V, HEAD_K), lambda b, u: (b, 0, step_of(u), 0, 0)),
                pl.BlockSpec((1, HEADS, HEAD_V, HEAD_K), lambda b, u: (b, 0, 0, 0)),
                pl.BlockSpec((1, SCAN_ROWS, D), lambda b, u: (b, _scan_lat_block(step_of(u), rev), 0))]
    out_specs = [spec(GLA_DK), spec(GLA_DK), spec(D), spec(GLA_DK)]
    out_shape = [jax.ShapeDtypeStruct((n, SEQ_ALL, GLA_DK), BF16), jax.ShapeDtypeStruct((n, SEQ_ALL, GLA_DK), BF16),
                 jax.ShapeDtypeStruct((n, SEQ_ALL, D), BF16), jax.ShapeDtypeStruct((n, SEQ_ALL, GLA_DK), F32)]
    scratch = [pltpu.VMEM((HEADS, HEAD_V, HEAD_K), F32), pltpu.VMEM((HEADS, HEAD_V, HEAD_K), F32),
               pltpu.VMEM((HEADS, CHUNK, HEAD_K), F32), pltpu.VMEM((HEADS, CHUNK, HEAD_K), F32),
               pltpu.VMEM((HEADS, CHUNK, HEAD_V), F32)]
    args = [q, k, v, cum, s_all, s_fin, do]
    if rider is not None:
        kind, arr = rider
        any_spec = pl.BlockSpec(memory_space=pl.ANY)
        in_specs.append(any_spec)
        out_specs.append(any_spec)
        args.append(arr)
        if kind == "swap":
            out_shape += _pair_got_shapes([], arr)
            n_cp = _pair_count([], arr)
        else:
            out_shape.append(jax.ShapeDtypeStruct((3,) + arr.shape[1:], arr.dtype))
            n_cp = 3 * P_ROW_CHUNKS
        scratch += [pltpu.SemaphoreType.DMA((n_cp,)), pltpu.SemaphoreType.DMA((n_cp,))]
    return _pallas(
        body, name=name, grid=(n, SCAN_STEPS), in_specs=in_specs, out_specs=out_specs, out_shape=out_shape,
        scratch_shapes=scratch,
        compiler_params=_params(("parallel" if rider is None else "arbitrary", "arbitrary")),
    )(*args)


def gla_out_proj(o_f, o_b, r, gnorm, gla_proj):
    n = o_f.shape[0]
    tiles = SEQ // TM_OUT

    def body(of_ref, ob_ref, r_ref, g_ref, w_ref, og_ref, y_ref):
        for h in range(HEADS):
            cols = pl.ds(h * HEAD_V, HEAD_V)
            o = of_ref[0, :, cols].astype(F32) + ob_ref[0, :, cols].astype(F32)
            rs = lax.rsqrt(jnp.mean(o * o, axis=-1, keepdims=True) + EPS)
            og_ref[:, cols] = (o * rs * g_ref[...] * _silu(r_ref[:, cols].astype(F32))).astype(og_ref.dtype)
        y_ref[...] = _nn(og_ref[...], w_ref[...]).astype(y_ref.dtype)

    ospec = pl.BlockSpec((1, TM_OUT, D), lambda b, j: (b, j, 0))
    row = pl.BlockSpec((TM_OUT, D), lambda b, j: (b * tiles + j, 0))
    return _pallas(
        body, name="gla_out_proj", grid=(n, tiles),
        in_specs=[ospec, ospec, row, pl.BlockSpec((1, HEAD_V), lambda b, j: (0, 0)),
                  pl.BlockSpec((D, D), lambda b, j: (0, 0))],
        out_specs=[row, row],
        out_shape=[jax.ShapeDtypeStruct((n * SEQ, D), BF16), jax.ShapeDtypeStruct((n * SEQ, D), BF16)],
        compiler_params=_params(("parallel", "parallel")),
    )(o_f, o_b, r, gnorm, gla_proj)


def gla_out_bwd(o_f, o_b, r, dyg, gla_proj, gnorm, og):
    n = o_f.shape[0]
    tiles = SEQ // TM_EW

    def body(of_ref, ob_ref, r_ref, d_ref, w_ref, g_ref, og_ref, do_ref, dr_ref, dg_ref, gw_ref, dog_buf):
        @pl.when((pl.program_id(0) == 0) & (pl.program_id(1) == 0))
        def _():
            dg_ref[...] = jnp.zeros_like(dg_ref)
            gw_ref[...] = jnp.zeros_like(gw_ref)

        gw_ref[...] += _tn(og_ref[...], d_ref[...])
        dog_buf[...] = _nt(d_ref[...], w_ref[...])
        for h in range(HEADS):
            cols = pl.ds(h * HEAD_V, HEAD_V)
            o = of_ref[0, :, cols].astype(F32) + ob_ref[0, :, cols].astype(F32)
            rv = r_ref[:, cols].astype(F32)
            dv = dog_buf[:, cols]
            rs = lax.rsqrt(jnp.mean(o * o, axis=-1, keepdims=True) + EPS)
            oh = o * rs
            dr_ref[:, cols] = (dv * oh * g_ref[...] * _dsilu(rv)).astype(dr_ref.dtype)
            dn = dv * _silu(rv)
            dg_ref[...] += jnp.sum(dn * oh, axis=0, keepdims=True)
            doh = dn * g_ref[...]
            do_ref[0, :, cols] = _b16(rs * (doh - oh * jnp.mean(doh * oh, axis=-1, keepdims=True)))

    ospec = pl.BlockSpec((1, TM_EW, D), lambda b, j: (b, j, 0))
    row = pl.BlockSpec((TM_EW, D), lambda b, j: (b * tiles + j, 0))
    vec = pl.BlockSpec((1, HEAD_V), lambda b, j: (0, 0))
    return _pallas(
        body, name="gla_out_bwd", grid=(n, tiles),
        in_specs=[ospec, ospec, row, row, pl.BlockSpec((D, D), lambda b, j: (0, 0)), vec, row],
        out_specs=[ospec, row, vec, pl.BlockSpec((D, D), lambda b, j: (0, 0))],
        out_shape=[jax.ShapeDtypeStruct((n, SEQ, D), BF16), jax.ShapeDtypeStruct((n * SEQ, D), BF16),
                   jax.ShapeDtypeStruct((1, HEAD_V), F32), jax.ShapeDtypeStruct((D, D), F32)],
        scratch_shapes=[pltpu.VMEM((TM_EW, D), F32)],
        compiler_params=_params(("arbitrary", "arbitrary")),
    )(o_f, o_b, r, dyg, gla_proj, gnorm, og)


TM_OUT = 512


def merge_out_final(p5, y_conv, y_gla, w_out, x2, gate, final_g, target, n_samples):
    t = x2.shape[0]
    tiles = SEQ // TM_OUT

    def body(mc_ref, mg_ref, yc_ref, yg_ref, w_ref, x_ref, gate_ref, g_ref, t_ref,
             mrg_ref, dh_ref, dmo_ref, dgate_ref, dg_ref, loss_ref):
        b, j = pl.program_id(0), pl.program_id(1)
        f = lambda ref: ref[...].astype(F32)
        merged = _b16(_sigmoid(f(mc_ref)) * f(yc_ref) + _sigmoid(f(mg_ref)) * f(yg_ref))
        mrg_ref[...] = merged
        mo_v = _nn(merged, w_ref[...])
        h = x_ref[...] + gate_ref[0] * mo_v
        rs = lax.rsqrt(jnp.mean(h * h, axis=-1, keepdims=True) + EPS)
        nh = h * rs
        err = nh * g_ref[...] - t_ref[...]
        dy = err * (1.0 / D)
        dn = dy * g_ref[...]
        dh = rs * (dn - nh * jnp.mean(dn * nh, axis=-1, keepdims=True))
        dh_ref[...] = dh
        dmo_ref[...] = (dh * gate_ref[0]).astype(dmo_ref.dtype)

        @pl.when(j == 0)
        def _():
            dgate_ref[...] = jnp.zeros_like(dgate_ref)

        @pl.when((b == 0) & (j == 0))
        def _():
            dg_ref[...] = jnp.zeros_like(dg_ref)
            loss_ref[...] = jnp.zeros_like(loss_ref)

        dgate_ref[0] += jnp.sum(dh * mo_v, axis=0, keepdims=True)
        dg_ref[...] += jnp.sum(dy * nh, axis=0, keepdims=True)
        loss_ref[...] += (0.5 / D) * jnp.sum(err * err)

    row = pl.BlockSpec((TM_OUT, D), lambda b, j: (b * tiles + j, 0))
    per = pl.BlockSpec((1, 1, D), lambda b, j: (b, 0, 0))
    vec = pl.BlockSpec((1, D), lambda b, j: (0, 0))
    return _pallas(
        body, name="merge_out_final", grid=(n_samples, tiles),
        in_specs=[row, pl.BlockSpec((TM_OUT, D), lambda b, j: (b * tiles + j, 1)), row, row,
                  pl.BlockSpec((D, D), lambda b, j: (0, 0)), row, per, vec, row],
        out_specs=[row, row, row, per, vec, pl.BlockSpec((8, 128), lambda b, j: (0, 0))],
        out_shape=[jax.ShapeDtypeStruct((t, D), BF16), jax.ShapeDtypeStruct((t, D), F32), jax.ShapeDtypeStruct((t, D), BF16),
                   jax.ShapeDtypeStruct((n_samples, 1, D), F32), jax.ShapeDtypeStruct((1, D), F32),
                   jax.ShapeDtypeStruct((8, 128), F32)],
        compiler_params=_params(("arbitrary", "arbitrary")),
    )(p5, p5, y_conv, y_gla, w_out, x2, gate, final_g, target)


def out_dgrad_merge_bwd(p5, y_conv, y_gla, dmo, w_out, merged):
    t = y_conv.shape[0]

    def body(mc_ref, mg_ref, yc_ref, yg_ref, d_ref, w_ref, mrg_ref, dyc_ref, dyg_ref, dp_ref, gw_ref):
        f = lambda ref: ref[...].astype(F32)

        @pl.when(pl.program_id(0) == 0)
        def _():
            gw_ref[...] = jnp.zeros_like(gw_ref)

        gw_ref[...] += _tn(mrg_ref[...], d_ref[...])
        d = _nt(d_ref[...], w_ref[...])
        sc = _sigmoid(f(mc_ref))
        sg = _sigmoid(f(mg_ref))
        dyc_ref[...] = (d * sc).astype(dyc_ref.dtype)
        dyg_ref[...] = (d * sg).astype(dyg_ref.dtype)
        dp_ref[:, pl.ds(0, D)] = (d * f(yc_ref) * sc * (1.0 - sc)).astype(dp_ref.dtype)
        dp_ref[:, pl.ds(D, D)] = (d * f(yg_ref) * sg * (1.0 - sg)).astype(dp_ref.dtype)

    row = pl.BlockSpec((TM_OUT, D), lambda i: (i, 0))
    return _pallas(
        body, name="out_dgrad_merge_bwd", grid=(t // TM_OUT,),
        in_specs=[row, pl.BlockSpec((TM_OUT, D), lambda i: (i, 1)), row, row, row, pl.BlockSpec((D, D), lambda i: (0, 0)),
                  row],
        out_specs=[row, row, pl.BlockSpec((TM_OUT, 2 * D), lambda i: (i, 0)), pl.BlockSpec((D, D), lambda i: (0, 0))],
        out_shape=[jax.ShapeDtypeStruct((t, D), BF16), jax.ShapeDtypeStruct((t, D), BF16),
                   jax.ShapeDtypeStruct((t, 2 * D), BF16), jax.ShapeDtypeStruct((D, D), F32)],
        compiler_params=_params(("arbitrary",)),
    )(p5, p5, y_conv, y_gla, dmo, w_out, merged)


def local_step(x, ctx, target, mod, wts, small, p_sh, chip, core):
    n = x.shape[0]
    t = n * SEQ
    t_all = t + n * NCTX
    x2 = x.reshape(t, D)
    ctx2 = ctx.reshape(n * NCTX, D)
    tgt2 = target.reshape(t, D)
    scale1, shift, gate = mod

    u = norm_mod_fwd(x2, ctx2, scale1, shift, small["norm_g"])
    p1, p2, p3, p4, p5, p_all = proj_all(u, [wts["w%d" % i] for i in range(1, 6)], [small["b%d" % i] for i in range(1, 6)],
                                         [t, t, t_all, t, t], p_sh, tm=512)
    p_full = jnp.stack([jnp.where(chip == i, p_sh, p_all[i]) for i in range(N_CHIPS)])
    wts = dict(wts, conv_proj=p_full[:, 0:256].reshape(D, D), gla_proj=p_full[:, 256:512].reshape(D, D),
               w_out=p_full[:, 512:768].reshape(D, D))

    aconv = conv_fwd(p1, small["conv_w"], small["conv_b"], n)
    ac, y_conv = ln_gate_proj(aconv, p2, small["conv_ln_g"], small["conv_ln_b"], wts["conv_proj"])

    qs, ks, vs, cum_f, cum_b = gla_prep_fwd(p3, small["upf"], small["upb"], small["bias_f"], small["bias_b"], n)
    o_f, s_f, sfin_f = gla_scan_fwd(qs, ks, vs, cum_f, rev=False, name="gla_scan_fwd_f")
    o_b, s_b, sfin_b = gla_scan_fwd(qs, ks, vs, cum_b, rev=True, name="gla_scan_fwd_b")
    og, y_gla = gla_out_proj(o_f, o_b, p4, small["gla_norm_g"], wts["gla_proj"])

    merged, dh, dmo, dgate, d_final_g, loss = merge_out_final(p5, y_conv, y_gla, wts["w_out"], x2, gate,
                                                              small["final_norm_g"], tgt2, n)

    g = {"final_norm_g": d_final_g}
    dyc, dyg, dp5, g["w_out"] = out_dgrad_merge_bwd(p5, y_conv, y_gla, dmo, wts["w_out"], merged)

    daconv, dp2, g["conv_ln_g"], g["conv_ln_b"], g["conv_proj"] = ln_gate_bwd(
        aconv, p2, dyc, wts["conv_proj"], small["conv_ln_g"], small["conv_ln_b"], ac)
    dp1, dconv_w, dconv_b = conv_bwd(p1, daconv, small["conv_w"], n)
    g["conv_w"], g["conv_b"] = dconv_w, dconv_b

    do, dp4, g["gla_norm_g"], g["gla_proj"] = gla_out_bwd(o_f, o_b, p4, dyg, wts["gla_proj"], small["gla_norm_g"], og)
    g["proj"] = jnp.concatenate([g["conv_proj"].reshape(N_CHIPS, 256, D), g["gla_proj"].reshape(N_CHIPS, 256, D),
                                 g["w_out"].reshape(N_CHIPS, 256, D)], 1)
    dq_f, dk_f, dv_f, dc_f, gotp = gla_scan_bwd(qs, ks, vs, cum_f, s_f, sfin_f, do, rev=False, name="gla_scan_bwd_f",
                                                rider=("swap", g["proj"]))
    pap16 = pair_add(core, g["proj"], gotp, name="pair_add_p", tr=384)
    dq_b, dk_b, dv_b, dc_b, rbp = gla_scan_bwd(qs, ks, vs, cum_b, s_b, sfin_b, do, rev=True, name="gla_scan_bwd_b",
                                               rider=("exchange", pap16))
    dp3, g["upf"], g["upb"], g["bias_f"], g["bias_b"] = gla_prep_bwd(
        p3, dq_f, dq_b, dk_f, dk_b, dv_f, dv_b, dc_f, dc_b,
        small["upf"], small["upb"], small["bias_f"], small["bias_b"], n)

    dps = [dp1, dp2, dp3, dp4, dp5]
    got = {}
    for i in [0, 1, 3, 4, 2]:
        dp = dps[i]
        rows = dp.shape[0]
        tn = W3 if dp.shape[1] == W3 else 1024
        others = [j for j in range(5) if j != i]
        swap = [g["w%d" % (j + 1)] for j in others] if i == 2 else None
        outs = matmul_tn(u, dp, name="w_in_wgrad_%d" % (i + 1), t=rows, tn=tn, tt=1024 if rows % 1024 == 0 else 768,
                         colsum=True, swap=swap)
        g["w%d" % (i + 1)], g["b%d" % (i + 1)] = outs[0], outs[1]
        if swap is not None:
            got = dict(zip(others, outs[2:2 + len(others)]))
    g["gate"] = dgate
    return loss, dh, dps, g, got, (pap16, rbp)


def _group_cols(w):
    gv, gg, z = w[..., 0:1024], w[..., 1024:2048], w[..., 2048:3072]
    q, k, v = w[..., 3072:3584], w[..., 3584:4096], w[..., 4096:5120]
    ab = w[..., 5120:5152]
    r, mc, mg = w[..., 5152:6176], w[..., 6176:7200], w[..., 7200:8224]
    g1 = jnp.concatenate([p for j in range(CONV_NCB)
                          for p in (gv[..., CONV_CB * j:CONV_CB * (j + 1)], gg[..., CONV_CB * j:CONV_CB * (j + 1)])], -1)
    pad = jnp.zeros(w.shape[:-1] + (W3 - 2080,), w.dtype)
    g3 = jnp.concatenate([v, q, k, ab, pad], -1)
    return g1, z, g3, r, jnp.concatenate([mc, mg], -1)


def _ungroup_cols(g1, g2, g3, g4, g5):
    gv = jnp.concatenate([g1[..., 2 * CONV_CB * j:2 * CONV_CB * j + CONV_CB] for j in range(CONV_NCB)], -1)
    gg = jnp.concatenate([g1[..., 2 * CONV_CB * j + CONV_CB:2 * CONV_CB * (j + 1)] for j in range(CONV_NCB)], -1)
    v, q, k, ab = g3[..., 0:1024], g3[..., 1024:1536], g3[..., 1536:2048], g3[..., 2048:2080]
    return jnp.concatenate([gv, gg, g2, q, k, v, ab, g4, g5[..., 0:1024], g5[..., 1024:2048]], -1)


def _natural_pieces():
    pieces = [(CONV_CB * j, CONV_CB, 0, 2 * CONV_CB * j) for j in range(CONV_NCB)]
    pieces += [(1024 + CONV_CB * j, CONV_CB, 0, 2 * CONV_CB * j + CONV_CB) for j in range(CONV_NCB)]
    pieces += [(2048, 1024, 1, 0), (3072, 512, 2, O3_Q), (3584, 512, 2, O3_K), (4096, 1024, 2, O3_V), (5120, 32, 2, O3_AB),
               (5152, 1024, 3, 0), (6176, 1024, 4, 0), (7200, 1024, 4, 1024)]
    return sorted(pieces)


def _ungroup_to_shards(groups):
    shards = []
    for i in range(N_CHIPS):
        lo, hi = i * W_IN_SHARD, (i + 1) * W_IN_SHARD
        parts = []
        for nat, width, g, gcol in _natural_pieces():
            a, b = max(nat, lo), min(nat + width, hi)
            if a < b:
                parts.append(groups[g][:, gcol + a - nat:gcol + b - nat])
        shards.append(jnp.concatenate(parts, 1))
    return jnp.stack(shards)


def _pad_up(up, row0):
    return jnp.zeros((128, GLA_DK), F32).at[row0:row0 + up.shape[0]].set(up)


def _adamw_math(w, g, m, v):
    m = ADAM_B1 * m + (1.0 - ADAM_B1) * g
    v = ADAM_B2 * v + (1.0 - ADAM_B2) * (g * g)
    m_hat = m / (1.0 - ADAM_B1 ** ADAM_STEP)
    v_hat = v / (1.0 - ADAM_B2 ** ADAM_STEP)
    delta = -ADAM_LR * (m_hat / (jnp.sqrt(v_hat) + ADAM_EPS) + ADAM_WD * w)
    return delta, m, v


def adamw2d(w, g, m, v, *, name, tr, tcols=None):
    rows, cols = w.shape[-2:]

    def body(w_ref, g_ref, m_ref, v_ref, d_ref, nm_ref, nv_ref):
        d_ref[...], nm_ref[...], nv_ref[...] = _adamw_math(w_ref[...], g_ref[...], m_ref[...], v_ref[...])

    tcols = cols if tcols is None else tcols
    if w.ndim == 3:
        spec = pl.BlockSpec((1, tr, tcols), lambda i, j: (0, i, j))
    else:
        spec = pl.BlockSpec((tr, tcols), lambda i, j: (i, j))
    return _pallas(
        body, name=name, grid=(rows // tr, cols // tcols), in_specs=[spec] * 4, out_specs=[spec] * 3,
        out_shape=[jax.ShapeDtypeStruct(w.shape, F32)] * 3, compiler_params=_params(("parallel", "parallel")),
    )(w, g, m, v)


def adamw_many(ws, gs, ms, vs):
    k = len(ws)
    two = lambda a: a.reshape((-1, a.shape[-1]))

    def body(*refs):
        w_refs, g_refs, m_refs, v_refs = refs[:k], refs[k:2 * k], refs[2 * k:3 * k], refs[3 * k:4 * k]
        d_refs, nm_refs, nv_refs = refs[4 * k:5 * k], refs[5 * k:6 * k], refs[6 * k:7 * k]
        for i in range(k):
            d_refs[i][...], nm_refs[i][...], nv_refs[i][...] = _adamw_math(
                w_refs[i][...], g_refs[i][...], m_refs[i][...], v_refs[i][...])

    shapes = [jax.ShapeDtypeStruct(two(a).shape, F32) for a in ws]
    outs = _pallas(body, name="adamw_small", out_shape=shapes * 3, compiler_params=_params())(
        *[two(a) for a in ws], *[two(a) for a in gs], *[two(a) for a in ms], *[two(a) for a in vs])
    back = lambda lst: [o.reshape(a.shape) for o, a in zip(lst, ws)]
    return back(outs[:k]), back(outs[k:2 * k]), back(outs[2 * k:])


def sum_devices(sall, *, name):
    rows = sall.shape[1]

    def body(s_ref, o_ref):
        acc = s_ref[0]
        for d in range(1, N_DEV):
            acc = acc + s_ref[d]
        o_ref[...] = acc

    return _pallas(body, name=name, out_shape=jax.ShapeDtypeStruct((rows, D), F32),
                   compiler_params=_params())(sall)


def pair_add(core, g, got, *, name, tr):
    n, rows, cols = got.shape
    g4 = g.reshape(n, 2, rows, cols)

    def body(core_ref, g_ref, got_ref, ob_ref):
        del core_ref
        ob_ref[0] = (g_ref[0, 0] + got_ref[0]).astype(BF16)

    spec = pl.BlockSpec((1, tr, cols), lambda i, t, core_ref: (i, t, 0))
    return _pallas(
        body, name=name,
        grid_spec=pltpu.PrefetchScalarGridSpec(
            num_scalar_prefetch=1, grid=(n, rows // tr),
            in_specs=[pl.BlockSpec((1, 1, tr, cols), lambda i, t, core_ref: (i, core_ref[0], t, 0)), spec],
            out_specs=spec),
        out_shape=jax.ShapeDtypeStruct(got.shape, BF16),
        compiler_params=_params(("parallel", "parallel")))(core, g4, got)


def pair_add_groups(core, gs, gots, *, tr):
    k = len(gs)
    rows = gots[0].shape[0]

    def body(core_ref, *refs):
        del core_ref
        for i in range(k):
            refs[2 * k + i][...] = (refs[i][0] + refs[k + i][...]).astype(BF16)

    g_specs = [pl.BlockSpec((1, tr, a.shape[1]), lambda t, core_ref: (core_ref[0], t, 0)) for a in gots]
    r_specs = [pl.BlockSpec((tr, a.shape[1]), lambda t, core_ref: (t, 0)) for a in gots]
    return _pallas(
        body, name="pair_add_w",
        grid_spec=pltpu.PrefetchScalarGridSpec(num_scalar_prefetch=1, grid=(rows // tr,),
                                               in_specs=g_specs + r_specs, out_specs=r_specs),
        out_shape=[jax.ShapeDtypeStruct(a.shape, BF16) for a in gots],
        compiler_params=_params(("parallel",)))(core, *[a.reshape(2, rows, a.shape[1]) for a in gs], *gots)


def chip_add(place, pa, rb, *, name, tr):
    _, rows, cols = pa.shape

    def body(place_ref, m_ref, r_ref, o_ref):
        del place_ref
        o_ref[0] = ((m_ref[0].astype(F32) + r_ref[0].astype(F32)) + r_ref[1].astype(F32)) + r_ref[2].astype(F32)

    return _pallas(
        body, name=name,
        grid_spec=pltpu.PrefetchScalarGridSpec(
            num_scalar_prefetch=1, grid=(rows // tr,),
            in_specs=[pl.BlockSpec((1, tr, cols), lambda t, place_ref: (place_ref[0], t, 0)),
                      pl.BlockSpec((3, tr, cols), lambda t, place_ref: (0, t, 0))],
            out_specs=pl.BlockSpec((1, tr, cols), lambda t, place_ref: (place_ref[1], t, 0))),
        out_shape=jax.ShapeDtypeStruct((2, rows, cols), F32),
        compiler_params=_params(("parallel",)))(place, pa, rb)


def ada_bwd(call, cctx_rows, dm_shard, dm_full, adaw):
    nsh = adaw.shape[1]

    def body(c_ref, cc_ref, dms_ref, dmf_ref, w_ref, gw_ref, gb_ref, pq_ref):
        a_lat = _silu(c_ref[...])
        a_ctx = _silu(cc_ref[...])
        dms = dms_ref[...]
        gw_ref[...] = _tn(a_lat, dms[0:64], HI) + _tn(a_ctx, dms[64:72], HI)
        gb_ref[...] = jnp.sum(dmf_ref[...], axis=0, keepdims=True)
        part = _nt(dms[64:72], w_ref[...], HI)
        pq_ref[...] = jnp.zeros_like(pq_ref) + jnp.sum(part, axis=0, keepdims=True)

    return _pallas(body, name="ada_bwd",
                   out_shape=[jax.ShapeDtypeStruct((D, nsh), F32), jax.ShapeDtypeStruct((1, 3 * D), F32),
                              jax.ShapeDtypeStruct((8, D), F32)],
                   compiler_params=_params())(call, cctx_rows, dm_shard, dm_full, adaw)


def cctx_grad(pq_all, cctx_rows):
    def body(p_ref, c_ref, o_ref):
        acc = p_ref[0]
        for qi in range(1, N_CHIPS):
            acc = acc + p_ref[qi]
        o_ref[...] = acc * _dsilu(c_ref[...])

    return _pallas(body, name="cctx_grad", out_shape=jax.ShapeDtypeStruct((8, D), F32),
                   compiler_params=_params())(pq_all, cctx_rows)


def _place():
    x, y, c = lax.axis_index("x"), lax.axis_index("y"), lax.axis_index("c")
    chips = [(1 - x, y), (x, 1 - y), (1 - x, 1 - y)]
    return x, y, c, chips


def _all_peers(x, y, c):
    return [((1 - x) if r & 4 else x, (1 - y) if r & 2 else y, (1 - c) if r & 1 else c) for r in range(1, N_DEV)]


def _remote(src, dst, send_sem, recv_sem, dev):
    return pltpu.make_async_remote_copy(src_ref=src, dst_ref=dst, send_sem=send_sem, recv_sem=recv_sem,
                                        device_id=dev, device_id_type=MESH)


ANY = pl.BlockSpec(memory_space=pl.ANY)
VMEM = pl.BlockSpec(memory_space=pltpu.VMEM)
F_ROWS = 16


W_ROW_CHUNKS = 8
P_ROW_CHUNKS = 4
N_BULK = W_ROW_CHUNKS + P_ROW_CHUNKS


def _half_chunks(core, n_rows, align, which=(0, 1)):
    out = []
    for a, k in ((0, W_ROW_CHUNKS), (1, P_ROW_CHUNKS)):
        if a not in which:
            continue
        half = n_rows[a] // 2
        size = half // k
        for i in range(k):
            start = core * half + i * size
            out.append((a, pl.ds(start if isinstance(start, int) else pl.multiple_of(start, align), size)))
    return out


def gather_weights(c8, cctx8, adaw, adab, w_sh, fp):
    nsh = adaw.shape[1]

    def body(c_ref, cctx_ref, adaw_ref, adab_ref, w_ref, fp_ref, wall_ref, fall_ref, call_ref, mall_ref,
             abuf, w_send, w_recv, h_send, h_recv, c_send, c_recv, m_send, m_recv, f_send, f_recv):
        x, y, c, chips = _place()
        q = 2 * x + y
        dev = 4 * x + 2 * y + c
        qs = [2 * cx + cy for cx, cy in chips]
        sib = (x, y, 1 - c)
        srcs, dsts = (w_ref,), (wall_ref,)
        n_rows = (w_ref.shape[0],)
        mine = _half_chunks(c, n_rows, 16, which=(0,))
        other = _half_chunks(1 - c, n_rows, 16, which=(0,))

        bulk = [[_remote(srcs[a].at[rows], dsts[a].at[q, rows], w_send.at[j * N_BULK + i], w_recv.at[j * N_BULK + i],
                         (*chips[j], c)) for i, (a, rows) in enumerate(mine)] for j in range(3)]
        fall_ref[q] = fp_ref[...]
        small = [_remote(fp_ref, fall_ref.at[q], f_send.at[j], f_recv.at[j], (*chips[j], c)) for j in range(3)]
        my_rows = pl.ds(pl.multiple_of(8 * dev, 8), 8)
        call_ref[my_rows, :] = c_ref[...]
        cond = [_remote(c_ref, call_ref.at[my_rows, :], c_send.at[r], c_recv.at[r], peer)
                for r, peer in enumerate(_all_peers(x, y, c))]
        for cp in sum(bulk, []) + small + cond:
            cp.start()
        for cp in cond:
            cp.wait_recv()

        abuf[pl.ds(0, 64), :] = _silu(call_ref[...])
        abuf[pl.ds(64, 8), :] = _silu(cctx_ref[...])
        mall_ref[q] = _nn(abuf[...], adaw_ref[...], HI) + adab_ref[...]
        mod = [_remote(mall_ref.at[q], mall_ref.at[q], m_send.at[j], m_recv.at[j], (*chips[j], c)) for j in range(3)]
        for cp in mod:
            cp.start()

        handed = []
        for j in range(3):
            for i, (a, rows) in enumerate(mine):
                bulk[j][i].wait_recv()
                cp = _remote(dsts[a].at[qs[j], rows], dsts[a].at[qs[j], rows],
                             h_send.at[j * N_BULK + i], h_recv.at[j * N_BULK + i], sib)
                cp.start()
                handed.append(cp)
        for j in range(3):
            for i, (a, rows) in enumerate(other):
                _remote(dsts[a].at[qs[j], rows], dsts[a].at[qs[j], rows],
                        h_send.at[j * N_BULK + i], h_recv.at[j * N_BULK + i], sib).wait_recv()
        for cp in mod + small:
            cp.wait_recv()
        for cp in sum(bulk, []) + small + cond + mod + handed:
            cp.wait_send()

    def dma(n):
        return pltpu.SemaphoreType.DMA((n,))

    return _pallas(
        body, name="gather_weights",
        in_specs=[VMEM, VMEM, VMEM, VMEM, ANY, VMEM],
        out_specs=[ANY, VMEM, VMEM, VMEM],
        out_shape=[jax.ShapeDtypeStruct((N_CHIPS,) + w_sh.shape, BF16),
                   jax.ShapeDtypeStruct((N_CHIPS, F_ROWS, D), F32),
                   jax.ShapeDtypeStruct((8 * N_DEV, D), F32), jax.ShapeDtypeStruct((N_CHIPS, MOD_ROWS, nsh), F32)],
        scratch_shapes=[pltpu.VMEM((MOD_ROWS, D), F32), dma(3 * N_BULK), dma(3 * N_BULK), dma(3 * N_BULK), dma(3 * N_BULK),
                        dma(7), dma(7), dma(3), dma(3), dma(3), dma(3)],
        compiler_params=_params(),
    )(c8, cctx8, adaw, adab, w_sh, fp)


def _pair_count(gs, gp):
    return len(gs) * W_ROW_CHUNKS + (0 if gp is None else N_CHIPS * P_ROW_CHUNKS)


def _pair_got_shapes(gs, gp):
    shapes = [jax.ShapeDtypeStruct((D // 2, a.shape[1]), F32) for a in gs]
    if gp is not None:
        shapes.append(jax.ShapeDtypeStruct((N_CHIPS, gp.shape[1] // 2, gp.shape[2]), F32))
    return shapes


def _pair_copies(g_refs, gp_ref, got_refs, gotp_ref, a_send, a_recv):
    x, y, c, _ = _place()
    sib = (x, y, 1 - c)
    pair = []
    half, size = D // 2, D // 2 // W_ROW_CHUNKS
    for gi in range(len(g_refs)):
        for i in range(W_ROW_CHUNKS):
            k = len(pair)
            rows_o = pl.ds(pl.multiple_of((1 - c) * half + i * size, 8), size)
            pair.append(_remote(g_refs[gi].at[rows_o], got_refs[gi].at[pl.ds(i * size, size)],
                                a_send.at[k], a_recv.at[k], sib))
    if gp_ref is not None:
        half, size = gp_ref.shape[1] // 2, gp_ref.shape[1] // 2 // P_ROW_CHUNKS
        for s in range(N_CHIPS):
            for i in range(P_ROW_CHUNKS):
                k = len(pair)
                rows_o = pl.ds(pl.multiple_of((1 - c) * half + i * size, 8), size)
                pair.append(_remote(gp_ref.at[s, rows_o], gotp_ref.at[s, pl.ds(i * size, size)],
                                    a_send.at[k], a_recv.at[k], sib))
    return pair


def pair_swap(gs, sm):
    n_gs = len(gs)

    def body(*refs):
        g_refs, sm_ref = refs[:n_gs], refs[n_gs]
        got_refs, sall_ref = refs[n_gs + 1:2 * n_gs + 1], refs[2 * n_gs + 1]
        a_send, a_recv, s_send, s_recv = refs[2 * n_gs + 2:]
        x, y, c, _ = _place()
        dev = 4 * x + 2 * y + c
        pair = _pair_copies(g_refs, None, got_refs, None, a_send, a_recv)
        sall_ref[dev] = sm_ref[...]
        small = [_remote(sm_ref, sall_ref.at[dev], s_send.at[r], s_recv.at[r], peer)
                 for r, peer in enumerate(_all_peers(x, y, c))]
        for cp in pair + small:
            cp.start()
        for cp in small + pair:
            cp.wait_recv()
        for cp in small + pair:
            cp.wait_send()

    return _pallas(
        body, name="pair_swap", in_specs=[ANY] * n_gs + [VMEM], out_specs=[ANY] * n_gs + [VMEM],
        out_shape=_pair_got_shapes(gs, None) + [jax.ShapeDtypeStruct((N_DEV,) + sm.shape, F32)],
        scratch_shapes=[pltpu.SemaphoreType.DMA((_pair_count(gs, None),)), pltpu.SemaphoreType.DMA((_pair_count(gs, None),)),
                        pltpu.SemaphoreType.DMA((N_DEV - 1,)), pltpu.SemaphoreType.DMA((N_DEV - 1,))],
        compiler_params=_params(),
    )(*gs, sm)


def gather_small(sm):
    rows = sm.shape[0]

    def body(sm_ref, sall_ref, s_send, s_recv):
        x, y, c, _ = _place()
        dev = 4 * x + 2 * y + c
        sall_ref[dev] = sm_ref[...]
        small = [_remote(sm_ref, sall_ref.at[dev], s_send.at[r], s_recv.at[r], peer)
                 for r, peer in enumerate(_all_peers(x, y, c))]
        for cp in small:
            cp.start()
        for cp in small:
            cp.wait_recv()
        for cp in small:
            cp.wait_send()

    return _pallas(
        body, name="gather_small", in_specs=[VMEM], out_specs=VMEM,
        out_shape=jax.ShapeDtypeStruct((N_DEV, rows, D), F32),
        scratch_shapes=[pltpu.SemaphoreType.DMA((7,)), pltpu.SemaphoreType.DMA((7,))],
        compiler_params=_params(),
    )(sm)


def pair_share(ghw, ghp, pq):
    def body(ghw_ref, ghp_ref, pq_ref, outw_ref, outp_ref, pqa_ref, send, recv, p_send, p_recv):
        del ghw_ref, ghp_ref
        x, y, c, chips = _place()
        q = 2 * x + y
        refs = (outw_ref, outp_ref)
        n_rows = (2 * outw_ref.shape[1], 2 * outp_ref.shape[1])
        pair = [_remote(refs[a].at[c, rows], refs[a].at[c, rows], send.at[i], recv.at[i], (x, y, 1 - c))
                for i, (a, rows) in enumerate(_half_chunks(0, n_rows, 8))]
        pqa_ref[q] = pq_ref[...]
        small = [_remote(pq_ref, pqa_ref.at[q], p_send.at[j], p_recv.at[j], (*chips[j], c)) for j in range(3)]
        for cp in pair + small:
            cp.start()
        for i, (a, rows) in enumerate(_half_chunks(0, n_rows, 8)):
            _remote(refs[a].at[1 - c, rows], refs[a].at[1 - c, rows], send.at[i], recv.at[i], (x, y, 1 - c)).wait_recv()
        for cp in small:
            cp.wait_recv()
        for cp in pair + small:
            cp.wait_send()

    return _pallas(
        body, name="pair_share", in_specs=[ANY, ANY, VMEM], out_specs=[ANY, ANY, VMEM],
        out_shape=[jax.ShapeDtypeStruct(ghw.shape, F32), jax.ShapeDtypeStruct(ghp.shape, F32),
                   jax.ShapeDtypeStruct((N_CHIPS, 8, D), F32)],
        scratch_shapes=[pltpu.SemaphoreType.DMA((N_BULK,)), pltpu.SemaphoreType.DMA((N_BULK,)),
                        pltpu.SemaphoreType.DMA((3,)), pltpu.SemaphoreType.DMA((3,))],
        input_output_aliases={0: 0, 1: 1},
        compiler_params=_params(),
    )(ghw, ghp, pq)


def _rows_of(shape):
    size = 1
    for s in shape:
        size *= s
    return -(-size // D)


def _pack(arrs, rows_multiple=8):
    parts = []
    total = 0
    for a in arrs:
        f = a.reshape(-1).astype(F32)
        r = _rows_of(a.shape)
        parts.append(jnp.pad(f, (0, r * D - f.shape[0])))
        total += r
    pad_rows = (-total) % rows_multiple
    if pad_rows:
        parts.append(jnp.zeros((pad_rows * D,), F32))
    return jnp.concatenate(parts).reshape(-1, D)


def _unpack(p, shapes):
    out = []
    r0 = 0
    for shp in shapes:
        r = _rows_of(shp)
        size = 1
        for s in shp:
            size *= s
        out.append(p[r0:r0 + r].reshape(-1)[:size].reshape(shp))
        r0 += r
    return out


WEIGHT_NAMES = ['c_ctx', 'ada_w', 'ada_b', 'norm_g', 'w_in', 'b_in', 'conv_w', 'conv_b', 'conv_ln_g', 'conv_ln_b',
                'conv_proj', 'decay_up_fwd', 'decay_bias_fwd', 'decay_up_bwd', 'decay_bias_bwd', 'gla_norm_g', 'gla_proj',
                'w_out', 'final_norm_g']
SMALL_NAMES = ['c_ctx', 'ada_b', 'norm_g', 'b_in', 'conv_w', 'conv_b', 'conv_ln_g', 'conv_ln_b', 'decay_up_fwd',
               'decay_bias_fwd', 'decay_up_bwd', 'decay_bias_bwd', 'gla_norm_g', 'final_norm_g']


def kernel(x, c, ctx, c_ctx, ada_w, ada_b, norm_g, w_in, b_in, conv_w, conv_b, conv_ln_g, conv_ln_b, conv_proj, decay_up_fwd, decay_bias_fwd, decay_up_bwd, decay_bias_bwd, gla_norm_g, gla_proj, w_out, final_norm_g, loss_target, m_c_ctx, m_ada_w, m_ada_b, m_norm_g, m_w_in, m_b_in, m_conv_w, m_conv_b, m_conv_ln_g, m_conv_ln_b, m_conv_proj, m_decay_up_fwd, m_decay_bias_fwd, m_decay_up_bwd, m_decay_bias_bwd, m_gla_norm_g, m_gla_proj, m_w_out, m_final_norm_g, v_c_ctx, v_ada_w, v_ada_b, v_norm_g, v_w_in, v_b_in, v_conv_w, v_conv_b, v_conv_ln_g, v_conv_ln_b, v_conv_proj, v_decay_up_fwd, v_decay_bias_fwd, v_decay_up_bwd, v_decay_bias_bwd, v_gla_norm_g, v_gla_proj, v_w_out, v_final_norm_g):
    w = dict(c_ctx=c_ctx, ada_w=ada_w, ada_b=ada_b, norm_g=norm_g, w_in=w_in, b_in=b_in, conv_w=conv_w, conv_b=conv_b,
             conv_ln_g=conv_ln_g, conv_ln_b=conv_ln_b, conv_proj=conv_proj, decay_up_fwd=decay_up_fwd,
             decay_bias_fwd=decay_bias_fwd, decay_up_bwd=decay_up_bwd, decay_bias_bwd=decay_bias_bwd,
             gla_norm_g=gla_norm_g, gla_proj=gla_proj, w_out=w_out, final_norm_g=final_norm_g)
    m = dict(c_ctx=m_c_ctx, ada_w=m_ada_w, ada_b=m_ada_b, norm_g=m_norm_g, w_in=m_w_in, b_in=m_b_in, conv_w=m_conv_w,
             conv_b=m_conv_b, conv_ln_g=m_conv_ln_g, conv_ln_b=m_conv_ln_b, conv_proj=m_conv_proj,
             decay_up_fwd=m_decay_up_fwd, decay_bias_fwd=m_decay_bias_fwd, decay_up_bwd=m_decay_up_bwd,
             decay_bias_bwd=m_decay_bias_bwd, gla_norm_g=m_gla_norm_g, gla_proj=m_gla_proj, w_out=m_w_out,
             final_norm_g=m_final_norm_g)
    v = dict(c_ctx=v_c_ctx, ada_w=v_ada_w, ada_b=v_ada_b, norm_g=v_norm_g, w_in=v_w_in, b_in=v_b_in, conv_w=v_conv_w,
             conv_b=v_conv_b, conv_ln_g=v_conv_ln_g, conv_ln_b=v_conv_ln_b, conv_proj=v_conv_proj,
             decay_up_fwd=v_decay_up_fwd, decay_bias_fwd=v_decay_bias_fwd, decay_up_bwd=v_decay_up_bwd,
             decay_bias_bwd=v_decay_bias_bwd, gla_norm_g=v_gla_norm_g, gla_proj=v_gla_proj, w_out=v_w_out,
             final_norm_g=v_final_norm_g)
    n = x.shape[0]
    ax, ay, ac = lax.axis_index("x"), lax.axis_index("y"), lax.axis_index("c")
    q = 2 * ax + ay
    dev = 4 * ax + 2 * ay + ac
    nsh = ada_w.shape[2]

    w_sh = w_in[0].astype(BF16)
    p_sh = jnp.concatenate([conv_proj[0], gla_proj[0], w_out[0]], 0).astype(BF16)
    fp = _pack([conv_w[0], decay_up_fwd[0], decay_up_bwd[0]], F_ROWS)
    c8 = jnp.pad(c, ((0, 8 - n), (0, 0)))
    cctx8 = jnp.pad(c_ctx[None], ((0, 7), (0, 0)))
    adab_sh = lax.dynamic_slice(ada_b, (0, q * nsh), (1, nsh))
    w_all, fall, call, mall = gather_weights(c8, cctx8, ada_w[0], adab_sh, w_sh, fp)

    mod_all = jnp.transpose(mall, (1, 0, 2)).reshape(MOD_ROWS, 3 * D)
    mod_mine = lax.dynamic_slice(mod_all, (8 * dev, 0), (n, 3 * D))
    mod_ctx = mod_all[64:65]
    shift = jnp.concatenate([mod_mine[:, 0:D], mod_ctx[:, 0:D]], 0)[:, None, :]
    scale1 = 1.0 + jnp.concatenate([mod_mine[:, D:2 * D], mod_ctx[:, D:2 * D]], 0)[:, None, :]
    gate = mod_mine[:, 2 * D:3 * D][:, None, :]

    own = lambda i, mine, got: jnp.where(q == i, mine, got)
    g1, g2, g3, g4, g5 = _group_cols(jnp.concatenate([own(i, w_sh, w_all[i]) for i in range(N_CHIPS)], 1))
    wts = dict(w1=g1, w2=g2, w3=g3, w4=g4, w5=g5)
    f_parts = [_unpack(fall[i], [conv_w.shape[1:], decay_up_fwd.shape[1:], decay_up_bwd.shape[1:]]) for i in range(N_CHIPS)]
    conv_w_full = jnp.concatenate([p[0] for p in f_parts], 1)
    upf_full = jnp.concatenate([p[1] for p in f_parts], 1)
    upb_full = jnp.concatenate([p[2] for p in f_parts], 1)
    b1, b2, b3, b4, b5 = _group_cols(b_in)
    small = dict(b1=b1, b2=b2, b3=b3, b4=b4, b5=b5, norm_g=norm_g,
                 conv_w=jnp.pad(conv_w_full, ((0, 1), (0, 0))), conv_b=conv_b, conv_ln_g=conv_ln_g, conv_ln_b=conv_ln_b,
                 upf=_split3(_pad_up(upf_full, 0)), upb=_split3(_pad_up(upb_full, 16)),
                 bias_f=decay_bias_fwd, bias_b=decay_bias_bwd,
                 gla_norm_g=gla_norm_g, final_norm_g=final_norm_g[None])

    core = ac.astype(jnp.int32).reshape(1)
    chip = q.astype(jnp.int32).reshape(1)
    loss_part, dh, dps, g, got, (pap16, rbp) = local_step(x, ctx, loss_target, (scale1, shift, gate), wts, small,
                                                          p_sh, q, core)
    loss = lax.psum(loss_part[0, 0], ("x", "y", "c"))

    gs = [g["w%d" % i] for i in range(1, 6)]
    d_b_in = _ungroup_cols(*[g["b%d" % i] for i in range(1, 6)])
    early = [d_b_in, g["conv_b"].sum(0), g["conv_ln_g"], g["conv_ln_b"], g["bias_f"], g["bias_b"],
             g["gla_norm_g"], g["final_norm_g"], g["conv_w"].sum(0)[:CONV_K], g["upf"][0:16], g["upb"][16:32]]
    early_shapes = [a.shape for a in early]
    got[2], sall1 = pair_swap([gs[2]], _pack(early))
    halves = pair_add_groups(core, gs, [got[i] for i in range(5)], tr=128)
    paw16 = _ungroup_to_shards(halves)
    grad_x2, dshift, dscale, g["norm_g"], rbw = dgrad_norm_bwd(
        dps, [wts["w%d" % i] for i in range(1, 6)], paw16, x.reshape(n * SEQ, D), ctx.reshape(n * NCTX, D), dh,
        scale1, norm_g, tm=256)

    dm_mine = jnp.concatenate([dshift[:n, 0], dscale[:n, 0], g["gate"][:, 0]], -1)
    dm_ctx = jnp.concatenate([dshift[n, 0], dscale[n, 0], jnp.zeros((D,), F32)], -1)
    late = [g["norm_g"], dm_mine, dm_ctx]
    late_shapes = [a.shape for a in late]
    sall2 = gather_small(_pack(late))
    (s_b_in, s_conv_b, s_ln_g, s_ln_b, s_bias_f, s_bias_b, s_gla_g, s_final_g, s_conv_w, s_upf,
     s_upb) = _unpack(sum_devices(sall1, name="sum_devices_early"), early_shapes)
    s_norm_g = _unpack(sum_devices(sall2, name="sum_devices_late"), late_shapes)[0]
    r_mine, r_ctx = 1, 1 + 3 * n
    dm_all = sall2[:, r_mine:r_ctx].reshape(N_DEV, n, 3 * D)
    dm_full = jnp.concatenate([jnp.pad(dm_all, ((0, 0), (0, 8 - n), (0, 0))).reshape(8 * N_DEV, 3 * D),
                               sall2[:, r_ctx:r_ctx + 3].reshape(N_DEV, 3 * D)], 0)
    dm_shard = lax.dynamic_slice(dm_full, (0, q * nsh), (MOD_ROWS, nsh))
    cctx_rows = jnp.broadcast_to(c_ctx[None], (8, D))
    g_ada_w, g_ada_b, pq = ada_bwd(call, cctx_rows, dm_shard, dm_full, ada_w[0])

    place = jnp.concatenate([chip, core])
    ghw = chip_add(place, paw16, rbw, name="chip_add_w", tr=128)
    ghp = chip_add(place, pap16, rbp, name="chip_add_p", tr=384)
    gw_mine, gp_mine, pq_all = pair_share(ghw, ghp, pq)
    gp_mine = gp_mine.reshape(768, D)
    g_c_ctx = cctx_grad(pq_all, cctx_rows)[0]

    grads = dict(
        c_ctx=g_c_ctx, ada_w=g_ada_w[None], ada_b=g_ada_b, norm_g=s_norm_g,
        w_in=gw_mine.reshape(1, D, W_IN_SHARD), b_in=s_b_in,
        conv_w=lax.dynamic_slice(s_conv_w, (0, q * 256), (CONV_K, 256))[None], conv_b=s_conv_b,
        conv_ln_g=s_ln_g, conv_ln_b=s_ln_b, conv_proj=gp_mine[0:256][None],
        decay_up_fwd=lax.dynamic_slice(s_upf, (0, q * 128), (16, 128))[None], decay_bias_fwd=s_bias_f,
        decay_up_bwd=lax.dynamic_slice(s_upb, (0, q * 128), (16, 128))[None], decay_bias_bwd=s_bias_b,
        gla_norm_g=s_gla_g, gla_proj=gp_mine[256:512][None], w_out=gp_mine[512:768][None],
        final_norm_g=s_final_g[0])

    delta, new_m, new_v = {}, {}, {}
    for name in ["ada_w", "conv_proj", "gla_proj", "w_out"]:
        delta[name], new_m[name], new_v[name] = adamw2d(w[name], grads[name].reshape(w[name].shape), m[name], v[name],
                                                        name="adamw_" + name, tr=128)
    tr_ = lambda a: jnp.swapaxes(a, 1, 2)
    g_w_in_t = tr_(grads["w_in"])
    grads["w_in"] = tr_(g_w_in_t)
    d_, m_, v_ = adamw2d(tr_(w_in), g_w_in_t, tr_(m_w_in), tr_(v_w_in), name="adamw_w_in", tr=W_IN_SHARD, tcols=128)
    delta["w_in"], new_m["w_in"], new_v["w_in"] = tr_(d_), tr_(m_), tr_(v_)
    d_, m_, v_ = adamw_many([w[nm] for nm in SMALL_NAMES], [grads[nm].reshape(w[nm].shape) for nm in SMALL_NAMES],
                            [m[nm] for nm in SMALL_NAMES], [v[nm] for nm in SMALL_NAMES])
    for nm, a, b, cc in zip(SMALL_NAMES, d_, m_, v_):
        delta[nm], new_m[nm], new_v[nm] = a, b, cc

    grad_x = grad_x2.reshape(x.shape)
    return (loss, grad_x, *[grads[nm].reshape(w[nm].shape) for nm in WEIGHT_NAMES], *[delta[nm] for nm in WEIGHT_NAMES],
            *[new_m[nm] for nm in WEIGHT_NAMES], *[new_v[nm] for nm in WEIGHT_NAMES])
```

```python
import jax
import jax.numpy as jnp
from jax import lax
from jax.experimental import pallas as pl
from jax.experimental.pallas import tpu as pltpu

F32 = jnp.float32
BF16 = jnp.bfloat16
MESH = pl.DeviceIdType.MESH
HI = lax.Precision.HIGHEST

D = 1024
SEQ = 2048
GRID_W = 64
GRID_H = SEQ // GRID_W
NCTX = 256
SEQ_ALL = SEQ + NCTX
EPS = 1e-6
CONV_K = 31
CONV_PAD = CONV_K // 2
HEADS = 4
HEAD_K = 128
HEAD_V = 256
GLA_DK = HEADS * HEAD_K
GATE_TAU = 16.0
Q_SCALE = HEAD_K ** -0.5
CHUNK = 64
NCHUNK = SEQ_ALL // CHUNK
NCHUNK_LAT = SEQ // CHUNK
NCHUNK_CTX = NCHUNK - NCHUNK_LAT
SUB = 64
NSUB = CHUNK // SUB
N_IN = 8224
W3 = 2176
O3_V, O3_Q, O3_K, O3_AB = 0, 1024, 1536, 2048

ADAM_LR, ADAM_B1, ADAM_B2, ADAM_EPS, ADAM_WD, ADAM_STEP = 0.001, 0.9, 0.999, 1e-08, 0.01, 10
VMEM_LIMIT = 56 * 1024 * 1024

N_CHIPS = 4
N_DEV = 8
W_IN_SHARD = N_IN // N_CHIPS
MOD_ROWS = 72


def _pallas(body, **kw):
    return pl.pallas_call(body, **kw)


def _params(sem=None, **kw):
    if sem is not None:
        kw["dimension_semantics"] = sem
    return pltpu.CompilerParams(vmem_limit_bytes=VMEM_LIMIT, **kw)


def _sigmoid(v):
    return 1.0 / (1.0 + jnp.exp(-v))


def _silu(v):
    return v * _sigmoid(v)


def _dsilu(v):
    s = _sigmoid(v)
    return s * (1.0 + v * (1.0 - s))


def _log_sigmoid(v):
    return jnp.minimum(v, 0.0) - jnp.log(1.0 + jnp.exp(-jnp.abs(v)))


def _dot(a, b, dims, precision=None):
    return lax.dot_general(a, b, (dims, ((), ())), preferred_element_type=F32, precision=precision)


def _nn(a, b, precision=None):
    return _dot(a, b, ((1,), (0,)), precision)


def _nt(a, b, precision=None):
    return _dot(a, b, ((1,), (1,)), precision)


def _tn(a, b, precision=None):
    return _dot(a, b, ((0,), (0,)), precision)


def _b16(v):
    return v.astype(BF16)


def proj_all(u, ws, bs, rows, p_sh, *, tm):
    k = u.shape[1]
    n_g = len(ws)
    tns = [w.shape[1] if w.shape[1] % 1024 else 1024 for w in ws]
    mts = [r // tm for r in rows]
    cnts = [(w.shape[1] // tn) * mt for w, tn, mt in zip(ws, tns, mts)]
    los = [sum(cnts[:g]) for g in range(n_g)]
    n_steps = sum(cnts)

    def rel(s, g):
        return jnp.clip(s - los[g], 0, cnts[g] - 1)

    def active(s, g):
        return (s >= los[g]) & (s < los[g] + cnts[g])

    def u_row(s):
        r = 0
        for g in range(n_g):
            r = r + jnp.where(active(s, g), rel(s, g) % mts[g], 0)
        return r

    def body(*refs):
        u_ref = refs[0]
        w_refs, b_refs = refs[1:1 + n_g], refs[1 + n_g:1 + 2 * n_g]
        p_ref = refs[1 + 2 * n_g]
        o_refs = refs[2 + 2 * n_g:2 + 3 * n_g]
        pall_ref = refs[2 + 3 * n_g]
        w_send, w_recv, h_send, h_recv = refs[3 + 3 * n_g:]
        s = pl.program_id(0)
        for g in range(n_g):
            @pl.when(active(s, g))
            def _(g=g):
                o_refs[g][...] = (_nn(u_ref[...], w_refs[g][...]) + b_refs[g][...]).astype(o_refs[g].dtype)

        x, y, c, chips = _place()
        q = 2 * x + y
        mine = _half_chunks(c, (0, p_ref.shape[0]), 16, which=(1,))
        other = _half_chunks(1 - c, (0, p_ref.shape[0]), 16, which=(1,))
        nb = len(mine)

        def bulk():
            return [[_remote(p_ref.at[rws], pall_ref.at[q, rws], w_send.at[pj * nb + pi], w_recv.at[pj * nb + pi],
                             (*chips[pj], c)) for pi, (_, rws) in enumerate(mine)] for pj in range(3)]

        @pl.when(s == 0)
        def _():
            for cp in sum(bulk(), []):
                cp.start()

        @pl.when(s == n_steps - 1)
        def _():
            handed = []
            for pj, (cx, cy) in enumerate(chips):
                for pi, (_, rws) in enumerate(mine):
                    bulk()[pj][pi].wait_recv()
                    cp = _remote(pall_ref.at[2 * cx + cy, rws], pall_ref.at[2 * cx + cy, rws],
                                 h_send.at[pj * nb + pi], h_recv.at[pj * nb + pi], (x, y, 1 - c))
                    cp.start()
                    handed.append(cp)
            for pj, (cx, cy) in enumerate(chips):
                for pi, (_, rws) in enumerate(other):
                    _remote(pall_ref.at[2 * cx + cy, rws], pall_ref.at[2 * cx + cy, rws],
                            h_send.at[pj * nb + pi], h_recv.at[pj * nb + pi], (x, y, 1 - c)).wait_recv()
            for cp in sum(bulk(), []) + handed:
                cp.wait_send()

    any_spec = pl.BlockSpec(memory_space=pl.ANY)
    in_specs = [pl.BlockSpec((tm, k), lambda s: (u_row(s), 0))]
    in_specs += [pl.BlockSpec((k, tns[g]), lambda s, g=g: (0, rel(s, g) // mts[g])) for g in range(n_g)]
    in_specs += [pl.BlockSpec((1, tns[g]), lambda s, g=g: (0, rel(s, g) // mts[g])) for g in range(n_g)]
    in_specs.append(any_spec)
    out_specs = [pl.BlockSpec((tm, tns[g]), lambda s, g=g: (rel(s, g) % mts[g], rel(s, g) // mts[g])) for g in range(n_g)]
    out_specs.append(any_spec)
    out_shape = [jax.ShapeDtypeStruct((rows[g], ws[g].shape[1]), BF16) for g in range(n_g)]
    out_shape.append(jax.ShapeDtypeStruct((N_CHIPS,) + p_sh.shape, p_sh.dtype))
    return _pallas(
        body, name="proj_all", grid=(n_steps,), in_specs=in_specs, out_specs=out_specs, out_shape=out_shape,
        scratch_shapes=[pltpu.SemaphoreType.DMA((3 * P_ROW_CHUNKS,)) for _ in range(4)],
        compiler_params=_params(("arbitrary",)),
    )(u, *ws, *bs, p_sh)


def matmul_tn(a, b, *, name, t, tn, tt, colsum=False, swap=None):
    m = a.shape[1]
    n = b.shape[1]
    nj, ns = n // tn, t // tt
    n_out = 2 if colsum else 1
    n_sw = 0 if swap is None else len(swap[0])
    n_ex = 0 if swap is None else n_sw + 1

    def body(a_ref, b_ref, *rest):
        o_ref = rest[n_ex]
        cs_ref = rest[n_ex + 1] if colsum else None
        j, s = pl.program_id(0), pl.program_id(1)

        if swap is not None:
            g_refs, sm_ref = rest[:n_sw], rest[n_sw]
            got_refs, sall_ref = rest[n_ex + n_out:n_ex + n_out + n_sw], rest[n_ex + n_out + n_sw]
            a_send, a_recv, s_send, s_recv, l_sem = rest[2 * n_ex + n_out:]

            def copies():
                x, y, c, _ = _place()
                dev = 4 * x + 2 * y + c
                small = [_remote(sm_ref, sall_ref.at[dev], s_send.at[r], s_recv.at[r], peer)
                         for r, peer in enumerate(_all_peers(x, y, c))]
                return (_pair_copies(g_refs, None, got_refs, None, a_send, a_recv) + small,
                        pltpu.make_async_copy(sm_ref, sall_ref.at[dev], l_sem))

            @pl.when((j == 0) & (s == 0))
            def _():
                remote, own = copies()
                own.start()
                for cp in remote:
                    cp.start()

            @pl.when((j == nj - 1) & (s == ns - 1))
            def _():
                remote, own = copies()
                for cp in remote:
                    cp.wait_recv()
                for cp in remote:
                    cp.wait_send()
                own.wait()

        @pl.when(s == 0)
        def _():
            o_ref[...] = jnp.zeros_like(o_ref)
            if colsum:
                cs_ref[...] = jnp.zeros_like(cs_ref)
        o_ref[...] += _tn(a_ref[...], b_ref[...])
        if colsum:
            cs_ref[...] += jnp.sum(b_ref[...].astype(F32), axis=0, keepdims=True)

    in_specs = [pl.BlockSpec((tt, m), lambda j, s: (s, 0)), pl.BlockSpec((tt, tn), lambda j, s: (s, j))]
    out_specs = [pl.BlockSpec((m, tn), lambda j, s: (0, j))]
    out_shape = [jax.ShapeDtypeStruct((m, n), F32)]
    if colsum:
        out_specs.append(pl.BlockSpec((1, tn), lambda j, s: (0, j)))
        out_shape.append(jax.ShapeDtypeStruct((1, n), F32))
    args, scratch = [a, b], []
    if swap is not None:
        gs, sm = swap
        any_spec = pl.BlockSpec(memory_space=pl.ANY)
        in_specs += [any_spec] * n_ex
        out_specs += [any_spec] * n_ex
        out_shape += _pair_got_shapes(gs, None) + [jax.ShapeDtypeStruct((N_DEV,) + sm.shape, F32)]
        args += [*gs, sm]
        scratch = [pltpu.SemaphoreType.DMA((_pair_count(gs, None),)), pltpu.SemaphoreType.DMA((_pair_count(gs, None),)),
                   pltpu.SemaphoreType.DMA((N_DEV - 1,)), pltpu.SemaphoreType.DMA((N_DEV - 1,)),
                   pltpu.SemaphoreType.DMA(())]
    return _pallas(
        body, name=name, grid=(nj, ns), in_specs=in_specs, out_specs=out_specs, out_shape=out_shape,
        scratch_shapes=scratch,
        compiler_params=_params(("parallel" if swap is None else "arbitrary", "arbitrary")),
    )(*args)


def dgrad_norm_bwd(dps, wts, paw, x2, ctx2, dh, scale1, norm_g, *, tm):
    t, tc = x2.shape[0], ctx2.shape[0]
    t_all = t + tc
    n_lat, n_ctx = t // tm, tc // tm
    n_tiles = n_lat + n_ctx
    n_samples = scale1.shape[0] - 1
    tps = n_lat // n_samples
    n_grp = n_samples + 1
    n_g = len(dps)
    whole = [g for g in range(n_g) if dps[g].shape[0] == t_all]
    latent = [g for g in range(n_g) if dps[g].shape[0] != t_all]

    def body(*refs):
        dp_refs, w_refs = refs[:n_g], refs[n_g:2 * n_g]
        (paw_ref, x_ref, c_ref, dh_ref, sc_ref, g_ref, dx_ref, dsh_ref, dsc_ref, dg_ref, rbw_ref,
         du_buf, b_send, b_recv) = refs[2 * n_g:]
        i = pl.program_id(0)

        def exchange():
            x, y, c, chips = _place()
            chunks = _half_chunks(0, (2 * paw_ref.shape[1],), 16, which=(0,))
            return [_remote(paw_ref.at[2 * cx + cy, rows], rbw_ref.at[j, rows],
                            b_send.at[j * N_BULK + k], b_recv.at[j * N_BULK + k], (cx, cy, c))
                    for j, (cx, cy) in enumerate(chips) for k, (_, rows) in enumerate(chunks)]

        @pl.when(i == 0)
        def _():
            for cp in exchange():
                cp.start()

        acc = None
        for g in whole:
            part = _nt(dp_refs[g][...], w_refs[g][...])
            acc = part if acc is None else acc + part
        du_buf[...] = acc

        @pl.when(i < n_lat)
        def _():
            lat = None
            for g in latent:
                part = _nt(dp_refs[g][...], w_refs[g][...])
                lat = part if lat is None else lat + part
            du_buf[...] += lat

        duv = du_buf[...]
        xv = jnp.where(i < n_lat, x_ref[...], c_ref[...])
        rs = lax.rsqrt(jnp.mean(xv * xv, axis=-1, keepdims=True) + EPS)
        xh = xv * rs
        n = xh * g_ref[...]
        dn = duv * sc_ref[0]
        dxh = dn * g_ref[...]
        dx = rs * (dxh - xh * jnp.mean(dxh * xh, axis=-1, keepdims=True))

        @pl.when(i < n_lat)
        def _():
            dx_ref[...] = dx + dh_ref[...]

        @pl.when((i % tps == 0) & (i <= n_lat))
        def _():
            dsh_ref[...] = jnp.zeros_like(dsh_ref)
            dsc_ref[...] = jnp.zeros_like(dsc_ref)

        @pl.when(i == 0)
        def _():
            dg_ref[...] = jnp.zeros_like(dg_ref)

        dsh_ref[0] += jnp.sum(duv, axis=0, keepdims=True)
        dsc_ref[0] += jnp.sum(duv * n, axis=0, keepdims=True)
        dg_ref[...] += jnp.sum(dn * xh, axis=0, keepdims=True)

        @pl.when(i == n_tiles - 1)
        def _():
            for cp in exchange():
                cp.wait_recv()
            for cp in exchange():
                cp.wait_send()

    lat = lambda i: (jnp.minimum(i, n_lat - 1), 0)
    grp = lambda i: (jnp.minimum(i // tps, n_samples), 0, 0)
    in_specs = []
    for g, dp in enumerate(dps):
        nrow = dp.shape[0] // tm
        in_specs.append(pl.BlockSpec((tm, dp.shape[1]), lambda i, nrow=nrow: (jnp.minimum(i, nrow - 1), 0)))
    for w in wts:
        in_specs.append(pl.BlockSpec(w.shape, lambda i: (0, 0), pipeline_mode=pl.Buffered(1)))
    any_spec = pl.BlockSpec(memory_space=pl.ANY)
    in_specs += [any_spec,
                 pl.BlockSpec((tm, D), lat), pl.BlockSpec((tm, D), lambda i: (jnp.maximum(i - n_lat, 0), 0)),
                 pl.BlockSpec((tm, D), lat), pl.BlockSpec((1, 1, D), grp), pl.BlockSpec((1, D), lambda i: (0, 0))]
    return _pallas(
        body, name="dgrad_norm_bwd", grid=(n_tiles,), in_specs=in_specs,
        out_specs=[pl.BlockSpec((tm, D), lat), pl.BlockSpec((1, 1, D), grp), pl.BlockSpec((1, 1, D), grp),
                   pl.BlockSpec((1, D), lambda i: (0, 0)), any_spec],
        out_shape=[jax.ShapeDtypeStruct((t, D), F32), jax.ShapeDtypeStruct((n_grp, 1, D), F32),
                   jax.ShapeDtypeStruct((n_grp, 1, D), F32), jax.ShapeDtypeStruct((1, D), F32),
                   jax.ShapeDtypeStruct((3,) + paw.shape[1:], paw.dtype)],
        scratch_shapes=[pltpu.VMEM((tm, D), F32), pltpu.SemaphoreType.DMA((3 * N_BULK,)),
                        pltpu.SemaphoreType.DMA((3 * N_BULK,))],
        compiler_params=_params(("arbitrary",)),
    )(*dps, *wts, paw, x2, ctx2, dh, scale1, norm_g)


TM_NORM = 512


def norm_mod_fwd(x2, ctx2, scale1, shift, norm_g):
    t = x2.shape[0]
    n_lat = t // TM_NORM
    assert ctx2.shape[0] == TM_NORM
    n_samples = scale1.shape[0] - 1
    tps = n_lat // n_samples

    def body(x_ref, c_ref, sc_ref, sh_ref, g_ref, u_ref):
        i = pl.program_id(0)
        xv = jnp.where(i < n_lat, x_ref[...], c_ref[...])
        rs = lax.rsqrt(jnp.mean(xv * xv, axis=-1, keepdims=True) + EPS)
        u = xv * rs * g_ref[...] * sc_ref[0] + sh_ref[0]
        u_ref[...] = u.astype(u_ref.dtype)

    grp = lambda i: (jnp.minimum(i // tps, n_samples), 0, 0)
    return _pallas(
        body, name="norm_mod_fwd", grid=(n_lat + 1,),
        in_specs=[pl.BlockSpec((TM_NORM, D), lambda i: (jnp.minimum(i, n_lat - 1), 0)),
                  pl.BlockSpec((TM_NORM, D), lambda i: (0, 0)),
                  pl.BlockSpec((1, 1, D), grp), pl.BlockSpec((1, 1, D), grp),
                  pl.BlockSpec((1, D), lambda i: (0, 0))],
        out_specs=pl.BlockSpec((TM_NORM, D), lambda i: (i, 0)),
        out_shape=jax.ShapeDtypeStruct((t + TM_NORM, D), BF16),
        compiler_params=_params(("parallel",)),
    )(x2, ctx2, scale1, shift, norm_g)


CONV_CB = 256
CONV_NCB = D // CONV_CB
H_OFF = 16


H_CB = 128
H_SPAN = GRID_W + 2 * H_OFF - 8


def _conv_scratch(vertical):
    if vertical:
        return [pltpu.VMEM((GRID_H + 2 * CONV_PAD, GRID_W, CONV_CB), F32)]
    return [pltpu.VMEM((GRID_H, GRID_W + 2 * H_OFF, H_CB), F32), pltpu.VMEM((7, GRID_H, H_SPAN, H_CB), F32)]


def _conv_fill(bufs, img, vertical):
    pad_ref = bufs[0]
    pad_ref[...] = jnp.zeros_like(pad_ref)
    if vertical:
        pad_ref[pl.ds(CONV_PAD, GRID_H)] = img
        return
    pad_ref[:, pl.ds(H_OFF, GRID_W), :] = img

    def shift(r, carry):
        for s in range(1, 8):
            bufs[1][s - 1, r] = pad_ref[r, pl.ds(s, H_SPAN), :]
        return carry

    lax.fori_loop(0, GRID_H, shift, 0)


def _conv_window(bufs, k, vertical, r, w0=0, nw=GRID_W, lanes=slice(None)):
    if vertical:
        return bufs[0][r + k, pl.ds(w0, nw), lanes]
    off = H_OFF - CONV_PAD + k
    if off % 8 == 0:
        return bufs[0][r, pl.ds(off + w0, nw), lanes]
    return bufs[1][off % 8 - 1, r, pl.ds(off - off % 8 + w0, nw), lanes]


def _conv_col_blocks(vertical):
    if vertical:
        return [pl.ds(0, CONV_CB)]
    return [pl.ds(i * H_CB, H_CB) for i in range(CONV_CB // H_CB)]


def _rows(r):
    return pl.ds(pl.multiple_of(r * GRID_W, GRID_W), GRID_W)


def conv_fwd(p1, conv_w, conv_b, n_samples):
    t = n_samples * SEQ

    def make(vertical, prev):
        n_buf = len(_conv_scratch(vertical))

        def body(gv_ref, gg_ref, w_ref, b_ref, *rest):
            o_ref, bufs = rest[-1 - n_buf], rest[-n_buf:]
            for cols in _conv_col_blocks(vertical):
                a = gv_ref[:, cols].astype(F32) * _sigmoid(gg_ref[:, cols].astype(F32))
                _conv_fill(bufs, a.reshape(GRID_H, GRID_W, a.shape[-1]), vertical)

                def row(r, carry, cols=cols):
                    acc = jnp.zeros((GRID_W, cols.size), F32) + b_ref[:, cols]
                    for k in range(CONV_K):
                        acc = acc + _conv_window(bufs, k, vertical, r) * w_ref[pl.ds(k, 1), cols]
                    o_ref[_rows(r), cols] = acc
                    return carry

                lax.fori_loop(0, GRID_H, row, 0)

        cb0 = CONV_NCB // 2 if vertical else 0
        in_specs = [pl.BlockSpec((SEQ, CONV_CB), lambda b, j: (b, 2 * (cb0 + j))),
                    pl.BlockSpec((SEQ, CONV_CB), lambda b, j: (b, 2 * (cb0 + j) + 1)),
                    pl.BlockSpec((CONV_K + 1, CONV_CB), lambda b, j: (0, cb0 + j)),
                    pl.BlockSpec((1, CONV_CB), lambda b, j: (0, cb0 + j))]
        args = [p1, p1, conv_w, conv_b]
        aliases = {}
        if prev is not None:
            in_specs.append(pl.BlockSpec(memory_space=pl.ANY))
            args.append(prev)
            aliases = {4: 0}
        return _pallas(
            body, name="conv_fwd_v" if vertical else "conv_fwd_h", grid=(n_samples, CONV_NCB // 2),
            in_specs=in_specs,
            out_specs=pl.BlockSpec((SEQ, CONV_CB), lambda b, j: (b, cb0 + j)),
            out_shape=jax.ShapeDtypeStruct((t, D), F32),
            scratch_shapes=_conv_scratch(vertical),
            input_output_aliases=aliases,
            compiler_params=_params(("parallel", "parallel")),
        )(*args)

    return make(True, make(False, None))


def conv_bwd(p1, daconv, conv_w, n_samples):
    t = n_samples * SEQ

    def make(vertical, prev):
        n_buf = len(_conv_scratch(vertical))

        def body(gv_ref, gg_ref, dy_ref, w_ref, *rest):
            dp_ref, dw_ref, db_ref = rest[-3 - 2 * n_buf - 1:-2 * n_buf - 1]
            a_bufs, d_bufs, da_ref = rest[-2 * n_buf - 1:-n_buf - 1], rest[-n_buf - 1:-1], rest[-1]
            for cols in _conv_col_blocks(vertical):
                width = cols.size
                gv = gv_ref[:, cols].astype(F32)
                sg = _sigmoid(gg_ref[:, cols].astype(F32))
                _conv_fill(a_bufs, (gv * sg).reshape(GRID_H, GRID_W, width), vertical)
                _conv_fill(d_bufs, dy_ref[:, cols].reshape(GRID_H, GRID_W, width), vertical)

                def row(r, carry, cols=cols, width=width):
                    acc = jnp.zeros((GRID_W, width), F32)
                    for k in range(CONV_K):
                        acc = acc + _conv_window(d_bufs, CONV_K - 1 - k, vertical, r) * w_ref[pl.ds(k, 1), cols]
                    da_ref[_rows(r), cols] = acc
                    return carry

                lax.fori_loop(0, GRID_H, row, 0)
                da = da_ref[:, cols]
                dp_ref[:, pl.ds(cols.start, width)] = (da * sg).astype(dp_ref.dtype)
                dp_ref[:, pl.ds(CONV_CB + cols.start, width)] = (da * gv * sg * (1.0 - sg)).astype(dp_ref.dtype)

                for lb in range(width // 128):
                    lanes = pl.ds(lb * 128, 128)
                    dy_lanes = pl.ds(cols.start + lb * 128, 128)

                    def wrow(r, accs, lanes=lanes, dy_lanes=dy_lanes):
                        for w0 in range(0, GRID_W, 8):
                            dyv = dy_ref[pl.ds(pl.multiple_of(r * GRID_W, GRID_W) + w0, 8), dy_lanes]
                            accs = tuple(accs[k] + _conv_window(a_bufs, k, vertical, r, w0, 8, lanes) * dyv
                                         for k in range(CONV_K))
                        return accs

                    accs = lax.fori_loop(0, GRID_H, wrow, tuple(jnp.zeros((8, 128), F32) for _ in range(CONV_K)))
                    for k in range(CONV_K):
                        dw_ref[0, pl.ds(k, 1), dy_lanes] = jnp.sum(accs[k], axis=0, keepdims=True)
            dw_ref[0, pl.ds(CONV_K, 1), :] = jnp.zeros((1, CONV_CB), F32)
            db_ref[0] = jnp.sum(dy_ref[...], axis=0, keepdims=True)

        cb0 = CONV_NCB // 2 if vertical else 0
        in_specs = [pl.BlockSpec((SEQ, CONV_CB), lambda b, j: (b, 2 * (cb0 + j))),
                    pl.BlockSpec((SEQ, CONV_CB), lambda b, j: (b, 2 * (cb0 + j) + 1)),
                    pl.BlockSpec((SEQ, CONV_CB), lambda b, j: (b, cb0 + j)),
                    pl.BlockSpec((CONV_K + 1, CONV_CB), lambda b, j: (0, cb0 + j))]
        args = [p1, p1, daconv, conv_w]
        aliases = {}
        if prev is not None:
            in_specs += [pl.BlockSpec(memory_space=pl.ANY)] * 3
            args += list(prev)
            aliases = {4: 0, 5: 1, 6: 2}
        return _pallas(
            body, name="conv_bwd_v" if vertical else "conv_bwd_h", grid=(n_samples, CONV_NCB // 2),
            in_specs=in_specs,
            out_specs=[pl.BlockSpec((SEQ, 2 * CONV_CB), lambda b, j: (b, cb0 + j)),
                       pl.BlockSpec((1, CONV_K + 1, CONV_CB), lambda b, j: (b, 0, cb0 + j)),
                       pl.BlockSpec((1, 1, CONV_CB), lambda b, j: (b, 0, cb0 + j))],
            out_shape=[jax.ShapeDtypeStruct((t, 2 * D), BF16),
                       jax.ShapeDtypeStruct((n_samples, CONV_K + 1, D), F32),
                       jax.ShapeDtypeStruct((n_samples, 1, D), F32)],
            scratch_shapes=_conv_scratch(vertical) + _conv_scratch(vertical) + [pltpu.VMEM((SEQ, CONV_CB), F32)],
            input_output_aliases=aliases,
            compiler_params=_params(("parallel", "parallel")),
        )(*args)

    return make(True, make(False, None))


TM_EW = 512


def ln_gate_proj(aconv, z, ln_g, ln_b, conv_proj):
    t = aconv.shape[0]

    def body(a_ref, z_ref, g_ref, b_ref, w_ref, o_ref, y_ref):
        a = a_ref[...]
        mu = jnp.mean(a, axis=-1, keepdims=True)
        xc = a - mu
        rstd = lax.rsqrt(jnp.mean(xc * xc, axis=-1, keepdims=True) + EPS)
        l = xc * rstd * g_ref[...] + b_ref[...]
        ac = _b16(_silu(l) * _silu(z_ref[...].astype(F32)))
        o_ref[...] = ac
        y_ref[...] = _nn(ac, w_ref[...]).astype(y_ref.dtype)

    row = pl.BlockSpec((TM_OUT, D), lambda i: (i, 0))
    vec = pl.BlockSpec((1, D), lambda i: (0, 0))
    return _pallas(
        body, name="ln_gate_proj", grid=(t // TM_OUT,),
        in_specs=[row, row, vec, vec, pl.BlockSpec((D, D), lambda i: (0, 0))], out_specs=[row, row],
        out_shape=[jax.ShapeDtypeStruct((t, D), BF16), jax.ShapeDtypeStruct((t, D), BF16)],
        compiler_params=_params(("parallel",)),
    )(aconv, z, ln_g, ln_b, conv_proj)


def ln_gate_bwd(aconv, z, dyc, conv_proj, ln_g, ln_b, ac):
    t = aconv.shape[0]

    def body(a_ref, z_ref, d_ref, w_ref, g_ref, b_ref, ac_ref, da_ref, dz_ref, dg_ref, db_ref, gw_ref):
        @pl.when(pl.program_id(0) == 0)
        def _():
            gw_ref[...] = jnp.zeros_like(gw_ref)

        gw_ref[...] += _tn(ac_ref[...], d_ref[...])
        a = a_ref[...]
        zv = z_ref[...].astype(F32)
        dac_v = _nt(d_ref[...], w_ref[...])
        mu = jnp.mean(a, axis=-1, keepdims=True)
        xc = a - mu
        rstd = lax.rsqrt(jnp.mean(xc * xc, axis=-1, keepdims=True) + EPS)
        xh = xc * rstd
        l = xh * g_ref[...] + b_ref[...]
        dz_ref[...] = (dac_v * _silu(l) * _dsilu(zv)).astype(dz_ref.dtype)
        dl = dac_v * _silu(zv) * _dsilu(l)
        dxh = dl * g_ref[...]
        da_ref[...] = rstd * (dxh - jnp.mean(dxh, axis=-1, keepdims=True)
                              - xh * jnp.mean(dxh * xh, axis=-1, keepdims=True))

        @pl.when(pl.program_id(0) == 0)
        def _():
            dg_ref[...] = jnp.zeros_like(dg_ref)
            db_ref[...] = jnp.zeros_like(db_ref)

        dg_ref[...] += jnp.sum(dl * xh, axis=0, keepdims=True)
        db_ref[...] += jnp.sum(dl, axis=0, keepdims=True)

    row = pl.BlockSpec((TM_EW, D), lambda i: (i, 0))
    vec = pl.BlockSpec((1, D), lambda i: (0, 0))
    return _pallas(
        body, name="ln_gate_bwd", grid=(t // TM_EW,),
        in_specs=[row, row, row, pl.BlockSpec((D, D), lambda i: (0, 0)), vec, vec, row],
        out_specs=[row, row, vec, vec, pl.BlockSpec((D, D), lambda i: (0, 0))],
        out_shape=[jax.ShapeDtypeStruct((t, D), F32), jax.ShapeDtypeStruct((t, D), BF16),
                   jax.ShapeDtypeStruct((1, D), F32), jax.ShapeDtypeStruct((1, D), F32),
                   jax.ShapeDtypeStruct((D, D), F32)],
        compiler_params=_params(("arbitrary",)),
    )(aconv, z, dyc, conv_proj, ln_g, ln_b, ac)


TM_PREP = 256
PREP_LAT = SEQ // TM_PREP
PREP_ALL = SEQ_ALL // TM_PREP


def _chunk_tri(n, upper):
    r = lax.broadcasted_iota(jnp.int32, (n, n), 0)
    c = lax.broadcasted_iota(jnp.int32, (n, n), 1)
    same = (r // CHUNK) == (c // CHUNK)
    keep = (c >= r) if upper else (c <= r)
    return jnp.where(same & keep, 1.0, 0.0).astype(F32)


def _split3(v):
    hi = v.astype(BF16)
    r1 = v - hi.astype(F32)
    mid = r1.astype(BF16)
    lo = (r1 - mid.astype(F32)).astype(BF16)
    return jnp.stack([hi, mid, lo])


def _chunk_sums(v, upper):
    tri = _chunk_tri(v.shape[0], upper).astype(BF16)
    pieces = _split3(v)
    return (_nn(tri, pieces[0]) + _nn(tri, pieces[1])) + _nn(tri, pieces[2])


def _gate_logits(ab, up3_ref, bias_ref):
    assert ab.dtype == BF16
    return ((_nn(ab, up3_ref[0]) + _nn(ab, up3_ref[1])) + _nn(ab, up3_ref[2])) + bias_ref[...]


def _prep_tile_maps(n_samples):
    n_lat = n_samples * PREP_LAT

    def seq_map(i):
        return jnp.where(i < n_lat, i // PREP_LAT, i - n_lat), jnp.where(i < n_lat, i % PREP_LAT, PREP_LAT)

    return n_lat, seq_map


def gla_prep_fwd(p3, upf, upb, bias_f, bias_b, n_samples):
    n_lat, seq_map = _prep_tile_maps(n_samples)
    n_tiles = n_lat + n_samples

    def body(v_ref, q_ref, k_ref, ab_ref, upf_ref, upb_ref, bf_ref, bb_ref, qo, ko, vo, cf, cb):
        i = pl.program_id(0)
        qo[0] = jnp.where(i < n_lat, q_ref[...].astype(F32) * Q_SCALE, 0.0)
        ko[0] = k_ref[...]
        vo[0] = v_ref[...]
        ab = ab_ref[...]
        gf = _log_sigmoid(_gate_logits(ab, upf_ref, bf_ref)) * (1.0 / GATE_TAU)
        gb = _log_sigmoid(_gate_logits(ab, upb_ref, bb_ref)) * (1.0 / GATE_TAU)
        cf[0] = _chunk_sums(gf, False)
        cb[0] = _chunk_sums(gb, True)

    def o_spec(w):
        return pl.BlockSpec((1, TM_PREP, w), lambda i: (*seq_map(i), 0))

    full = lambda shape: pl.BlockSpec(shape, lambda i: (0,) * len(shape))
    return _pallas(
        body, name="gla_prep_fwd", grid=(n_tiles,),
        in_specs=[pl.BlockSpec((TM_PREP, 1024), lambda i: (i, O3_V // 1024)),
                  pl.BlockSpec((TM_PREP, 512), lambda i: (i, O3_Q // 512)),
                  pl.BlockSpec((TM_PREP, 512), lambda i: (i, O3_K // 512)),
                  pl.BlockSpec((TM_PREP, 128), lambda i: (i, O3_AB // 128)),
                  full((3, 128, GLA_DK)), full((3, 128, GLA_DK)), full((1, GLA_DK)), full((1, GLA_DK))],
        out_specs=[o_spec(GLA_DK), o_spec(GLA_DK), o_spec(D), o_spec(GLA_DK), o_spec(GLA_DK)],
        out_shape=[jax.ShapeDtypeStruct((n_samples, SEQ_ALL, GLA_DK), F32),
                   jax.ShapeDtypeStruct((n_samples, SEQ_ALL, GLA_DK), p3.dtype),
                   jax.ShapeDtypeStruct((n_samples, SEQ_ALL, D), p3.dtype),
                   jax.ShapeDtypeStruct((n_samples, SEQ_ALL, GLA_DK), F32),
                   jax.ShapeDtypeStruct((n_samples, SEQ_ALL, GLA_DK), F32)],
        compiler_params=_params(("parallel",)),
    )(p3, p3, p3, p3, upf, upb, bias_f, bias_b)


def gla_prep_bwd(p3, dq_f, dq_b, dk_f, dk_b, dv_f, dv_b, dc_f, dc_b, upf, upb, bias_f, bias_b, n_samples):
    n_lat, seq_map = _prep_tile_maps(n_samples)
    n_tiles = n_lat + n_samples

    def body(ab_ref, dqf, dqb, dkf, dkb, dvf, dvb, dcf, dcb, upf_ref, upb_ref, bf_ref, bb_ref,
             dp_ref, duf_ref, dub_ref, dbf_ref, dbb_ref):
        i = pl.program_id(0)
        both = lambda a, b: a[0].astype(F32) + b[0].astype(F32)
        dp_ref[:, pl.ds(O3_V, D)] = both(dvf, dvb).astype(dp_ref.dtype)
        dq = jnp.where(i < n_lat, both(dqf, dqb) * Q_SCALE, 0.0)
        dp_ref[:, pl.ds(O3_Q, GLA_DK)] = dq.astype(dp_ref.dtype)
        dp_ref[:, pl.ds(O3_K, GLA_DK)] = both(dkf, dkb).astype(dp_ref.dtype)
        ab = ab_ref[...]
        zf = _gate_logits(ab, upf_ref, bf_ref)
        zb = _gate_logits(ab, upb_ref, bb_ref)
        dgf = _chunk_sums(dcf[0], True)
        dgb = _chunk_sums(dcb[0], False)
        dzf = _b16(dgf * (1.0 / GATE_TAU) * _sigmoid(-zf))
        dzb = _b16(dgb * (1.0 / GATE_TAU) * _sigmoid(-zb))
        dab = _nt(dzf, upf_ref[0]) + _nt(dzb, upb_ref[0])
        dp_ref[:, pl.ds(O3_AB, 128)] = dab.astype(dp_ref.dtype)

        @pl.when(i == 0)
        def _():
            duf_ref[...] = jnp.zeros_like(duf_ref)
            dub_ref[...] = jnp.zeros_like(dub_ref)
            dbf_ref[...] = jnp.zeros_like(dbf_ref)
            dbb_ref[...] = jnp.zeros_like(dbb_ref)

        duf_ref[...] += _tn(ab, dzf)
        dub_ref[...] += _tn(ab, dzb)
        dbf_ref[...] += jnp.sum(dzf.astype(F32), axis=0, keepdims=True)
        dbb_ref[...] += jnp.sum(dzb.astype(F32), axis=0, keepdims=True)

    def s_spec(w):
        return pl.BlockSpec((1, TM_PREP, w), lambda i: (*seq_map(i), 0))

    full = lambda shape: pl.BlockSpec(shape, lambda i: (0,) * len(shape))
    return _pallas(
        body, name="gla_prep_bwd", grid=(n_tiles,),
        in_specs=[pl.BlockSpec((TM_PREP, 128), lambda i: (i, O3_AB // 128)),
                  s_spec(GLA_DK), s_spec(GLA_DK), s_spec(GLA_DK), s_spec(GLA_DK), s_spec(D), s_spec(D),
                  s_spec(GLA_DK), s_spec(GLA_DK),
                  full((3, 128, GLA_DK)), full((3, 128, GLA_DK)), full((1, GLA_DK)), full((1, GLA_DK))],
        out_specs=[pl.BlockSpec((TM_PREP, W3), lambda i: (i, 0)),
                   full((128, GLA_DK)), full((128, GLA_DK)), full((1, GLA_DK)), full((1, GLA_DK))],
        out_shape=[jax.ShapeDtypeStruct((n_tiles * TM_PREP, W3), BF16),
                   jax.ShapeDtypeStruct((128, GLA_DK), F32), jax.ShapeDtypeStruct((128, GLA_DK), F32),
                   jax.ShapeDtypeStruct((1, GLA_DK), F32), jax.ShapeDtypeStruct((1, GLA_DK), F32)],
        compiler_params=_params(("arbitrary",)),
    )(p3, dq_f, dq_b, dk_f, dk_b, dv_f, dv_b, dc_f, dc_b, upf, upb, bias_f, bias_b)


def _sub_blocks(rev):
    if NSUB == 1:
        return [((0, CHUNK), CHUNK // 2, (0, CHUNK))]
    out = []
    for s in range(NSUB):
        rows = (s * SUB, SUB)
        if rev:
            ref = (s + 1) * SUB if s < NSUB - 1 else None
            cols = (s * SUB, CHUNK - s * SUB)
        else:
            ref = s * SUB - 1 if s > 0 else None
            cols = (0, (s + 1) * SUB)
        out.append((rows, ref, cols))
    return out


def _sub_mask(rows, cols, rev):
    r = rows[0] + lax.broadcasted_iota(jnp.int32, (rows[1], cols[1]), 0)
    c = cols[0] + lax.broadcasted_iota(jnp.int32, (rows[1], cols[1]), 1)
    return (c >= r) if rev else (c <= r)


def _sub_operands(qc, kc, cc, rows, ref, cols):
    cref = jnp.zeros((1, HEAD_K), F32) if ref is None else cc[ref:ref + 1]
    eq = jnp.exp(cc[rows[0]:rows[0] + rows[1]] - cref)
    ek = jnp.exp(cref - cc[cols[0]:cols[0] + cols[1]])
    qs = qc[rows[0]:rows[0] + rows[1]] * eq
    kk = kc[cols[0]:cols[0] + cols[1]] * ek
    return qs, kk, eq, ek


SCAN_ROWS = 256
SCAN_CHUNKS = SCAN_ROWS // CHUNK
SCAN_STEPS = SEQ_ALL // SCAN_ROWS
LAT_BLOCKS = SEQ // SCAN_ROWS


def _scan_block(t, rev):
    if rev:
        return SCAN_STEPS - 1 - t
    return jnp.where(t == 0, SCAN_STEPS - 1, t - 1)


def _scan_lat_block(t, rev):
    if rev:
        return jnp.minimum(SCAN_STEPS - 1 - t, LAT_BLOCKS - 1)
    return jnp.maximum(t - 1, 0)


def _head_cols(h):
    return pl.ds(h * HEAD_K, HEAD_K), pl.ds(h * HEAD_V, HEAD_V)


def gla_scan_fwd(q, k, v, cum, *, rev, name):
    n = q.shape[0]

    def body(q_ref, k_ref, v_ref, c_ref, o_ref, s_ref, sfin_ref, st):
        t = pl.program_id(1)

        @pl.when(t == 0)
        def _():
            st[...] = jnp.zeros_like(st)

        def chunk(j, carry):
            lj = SCAN_CHUNKS - 1 - j if rev else j
            r0 = lj * CHUNK
            rws = pl.ds(r0, CHUNK)
            for h in range(HEADS):
                kcols, vcols = _head_cols(h)
                qc, kc, cc = q_ref[0, rws, kcols], k_ref[0, rws, kcols], c_ref[0, rws, kcols]
                vc = v_ref[0, rws, vcols]
                s_in = st[h]
                s_ref[0, h, j] = _b16(s_in)
                edge = cc[0:1] if rev else cc[CHUNK - 1:CHUNK]
                ke = kc * jnp.exp(edge - cc)
                st[h] = s_in * jnp.exp(edge) + _tn(_b16(vc), _b16(ke))
                o_inter = _nt(_b16(qc * jnp.exp(cc)), _b16(s_in))
                vb = _b16(vc)
                for rows, ref, cols in _sub_blocks(rev):
                    qs, kk, _, _ = _sub_operands(qc, kc, cc, rows, ref, cols)
                    a = jnp.where(_sub_mask(rows, cols, rev), _nt(_b16(qs), _b16(kk)), 0.0)
                    o_s = _nn(_b16(a), vb[cols[0]:cols[0] + cols[1]])
                    o_ref[0, pl.ds(r0 + rows[0], rows[1]), vcols] = _b16(o_inter[rows[0]:rows[0] + rows[1]] + o_s)
            return carry

        for j in range(SCAN_CHUNKS):
            chunk(j, 0)

        @pl.when(t == SCAN_STEPS - 1)
        def _():
            sfin_ref[0] = st[...]

    def spec(w):
        return pl.BlockSpec((1, SCAN_ROWS, w), lambda b, t: (b, _scan_block(t, rev), 0))

    return _pallas(
        body, name=name, grid=(n, SCAN_STEPS),
        in_specs=[spec(GLA_DK), spec(GLA_DK), spec(D), spec(GLA_DK)],
        out_specs=[pl.BlockSpec((1, SCAN_ROWS, D), lambda b, t: (b, _scan_lat_block(t, rev), 0)),
                   pl.BlockSpec((1, HEADS, SCAN_CHUNKS, HEAD_V, HEAD_K), lambda b, t: (b, 0, t, 0, 0)),
                   pl.BlockSpec((1, HEADS, HEAD_V, HEAD_K), lambda b, t: (b, 0, 0, 0))],
        out_shape=[jax.ShapeDtypeStruct((n, SEQ, D), BF16),
                   jax.ShapeDtypeStruct((n, HEADS, NCHUNK, HEAD_V, HEAD_K), BF16),
                   jax.ShapeDtypeStruct((n, HEADS, HEAD_V, HEAD_K), F32)],
        scratch_shapes=[pltpu.VMEM((HEADS, HEAD_V, HEAD_K), F32)],
        compiler_params=_params(("parallel", "arbitrary")),
    )(q, k, v, cum)


def gla_scan_bwd(q, k, v, cum, s_all, s_fin, do, *, rev, name, rider=None):
    n = q.shape[0]

    def body(q_ref, k_ref, v_ref, c_ref, s_ref, sfin_ref, do_ref, *rest):
        if rider is not None:
            ride_in, rest = rest[0], rest[1:]
        dq_ref, dk_ref, dv_ref, dc_ref = rest[:4]
        if rider is not None:
            ride_out, rest = rest[4], rest[:4] + rest[5:]
        dst, s_next, dq_acc, dk_acc, dv_acc = rest[4:9]
        t = SCAN_STEPS - 1 - pl.program_id(1)

        if rider is not None:
            def copies():
                send, recv = rest[9], rest[10]
                if rider[0] == "swap":
                    return _pair_copies([], ride_in, [], ride_out, send, recv)
                x, y, c, chips = _place()
                return [_remote(ride_in.at[2 * cx + cy, rows], ride_out.at[pj, rows], send.at[pj * P_ROW_CHUNKS + pi],
                                recv.at[pj * P_ROW_CHUNKS + pi], (cx, cy, c))
                        for pj, (cx, cy) in enumerate(chips)
                        for pi, (_, rows) in enumerate(_half_chunks(0, (0, 2 * ride_in.shape[1]), 16, which=(1,)))]

            @pl.when((pl.program_id(0) == 0) & (pl.program_id(1) == 0))
            def _():
                for cp in copies():
                    cp.start()

            @pl.when((pl.program_id(0) == n - 1) & (pl.program_id(1) == SCAN_STEPS - 1))
            def _():
                for cp in copies():
                    cp.wait_recv()
                for cp in copies():
                    cp.wait_send()

        @pl.when(pl.program_id(1) == 0)
        def _():
            dst[...] = jnp.zeros_like(dst)
            s_next[...] = sfin_ref[0]

        def chunk(jj, carry):
            j = SCAN_CHUNKS - 1 - jj
            lj = SCAN_CHUNKS - 1 - j if rev else j
            rws = pl.ds(lj * CHUNK, CHUNK)
            for h in range(HEADS):
                kcols, vcols = _head_cols(h)
                qc, kc, cc = q_ref[0, rws, kcols], k_ref[0, rws, kcols], c_ref[0, rws, kcols]
                vc = v_ref[0, rws, vcols]
                doc = jnp.where(t > 0, do_ref[0, rws, vcols], 0.0)
                s_in = s_ref[0, h, j]
                s_out = s_next[h]
                ds_out = dst[h]
                edge = cc[0:1] if rev else cc[CHUNK - 1:CHUNK]
                e_q = jnp.exp(cc)
                e_k = jnp.exp(edge - cc)
                dob = _b16(doc)
                dsb = _b16(ds_out)
                dst[h] = ds_out * jnp.exp(edge) + _tn(dob, _b16(qc * e_q))
                s_next[h] = s_in.astype(F32)
                dq_acc[h] = e_q * _nn(dob, s_in)
                dk_acc[h] = e_k * _nn(_b16(vc), dsb)
                dv_acc[h] = _nt(_b16(kc * e_k), dsb)
                vb = _b16(vc)
                for rows, ref, cols in _sub_blocks(rev):
                    qs, kk, eq, ek = _sub_operands(qc, kc, cc, rows, ref, cols)
                    mask = _sub_mask(rows, cols, rev)
                    rsl = slice(rows[0], rows[0] + rows[1])
                    csl = pl.ds(cols[0], cols[1])
                    qsb, kkb = _b16(qs), _b16(kk)
                    a = jnp.where(mask, _nt(qsb, kkb), 0.0)
                    da = _b16(jnp.where(mask, _nt(dob[rsl], vb[cols[0]:cols[0] + cols[1]]), 0.0))
                    dq_acc[h, pl.ds(rows[0], rows[1]), :] += _nn(da, kkb) * eq
                    dk_acc[h, csl, :] += _tn(da, qsb) * ek
                    dv_acc[h, csl, :] += _tn(_b16(a), dob[rsl])
                dq = dq_acc[h]
                dk = dk_acc[h]
                dc = qc * dq - kc * dk
                bnd = jnp.sum(ds_out * s_out, axis=0, keepdims=True)
                edge_row = 0 if rev else CHUNK - 1
                is_edge = lax.broadcasted_iota(jnp.int32, (CHUNK, HEAD_K), 0) == edge_row
                dq_ref[0, rws, kcols] = _b16(dq)
                dk_ref[0, rws, kcols] = _b16(dk)
                dv_ref[0, rws, vcols] = _b16(dv_acc[h])
                dc_ref[0, rws, kcols] = dc + jnp.where(is_edge, bnd, 0.0)
            return carry

        for jj in range(SCAN_CHUNKS):
            chunk(jj, 0)

    def step_of(u):
        return SCAN_STEPS - 1 - u

    def spec(w):
        return pl.BlockSpec((1, SCAN_ROWS, w), lambda b, u: (b, _scan_block(step_of(u), rev), 0))

    in_specs = [spec(GLA_DK), spec(GLA_DK), spec(D), spec(GLA_DK),
                pl.BlockSpec((1, HEADS, SCAN_CHUNKS, HEAD_V, HEAD_K), lambda b, u: (b, 0, step_of(u), 0, 0)),
                pl.BlockSpec((1, HEADS, HEAD_V, HEAD_K), lambda b, u: (b, 0, 0, 0)),
                pl.BlockSpec((1, SCAN_ROWS, D), lambda b, u: (b, _scan_lat_block(step_of(u), rev), 0))]
    out_specs = [spec(GLA_DK), spec(GLA_DK), spec(D), spec(GLA_DK)]
    out_shape = [jax.ShapeDtypeStruct((n, SEQ_ALL, GLA_DK), BF16), jax.ShapeDtypeStruct((n, SEQ_ALL, GLA_DK), BF16),
                 jax.ShapeDtypeStruct((n, SEQ_ALL, D), BF16), jax.ShapeDtypeStruct((n, SEQ_ALL, GLA_DK), F32)]
    scratch = [pltpu.VMEM((HEADS, HEAD_V, HEAD_K), F32), pltpu.VMEM((HEADS, HEAD_V, HEAD_K), F32),
               pltpu.VMEM((HEADS, CHUNK, HEAD_K), F32), pltpu.VMEM((HEADS, CHUNK, HEAD_K), F32),
               pltpu.VMEM((HEADS, CHUNK, HEAD_V), F32)]
    args = [q, k, v, cum, s_all, s_fin, do]
    if rider is not None:
        kind, arr = rider
        any_spec = pl.BlockSpec(memory_space=pl.ANY)
        in_specs.append(any_spec)
        out_specs.append(any_spec)
        args.append(arr)
        if kind == "swap":
            out_shape += _pair_got_shapes([], arr)
            n_cp = _pair_count([], arr)
        else:
            out_shape.append(jax.ShapeDtypeStruct((3,) + arr.shape[1:], arr.dtype))
            n_cp = 3 * P_ROW_CHUNKS
        scratch += [pltpu.SemaphoreType.DMA((n_cp,)), pltpu.SemaphoreType.DMA((n_cp,))]
    return _pallas(
        body, name=name, grid=(n, SCAN_STEPS), in_specs=in_specs, out_specs=out_specs, out_shape=out_shape,
        scratch_shapes=scratch,
        compiler_params=_params(("parallel" if rider is None else "arbitrary", "arbitrary")),
    )(*args)


def gla_out_proj(o_f, o_b, r, gnorm, gla_proj):
    n = o_f.shape[0]
    tiles = SEQ // TM_OUT

    def body(of_ref, ob_ref, r_ref, g_ref, w_ref, og_ref, y_ref):
        for h in range(HEADS):
            cols = pl.ds(h * HEAD_V, HEAD_V)
            o = of_ref[0, :, cols].astype(F32) + ob_ref[0, :, cols].astype(F32)
            rs = lax.rsqrt(jnp.mean(o * o, axis=-1, keepdims=True) + EPS)
            og_ref[:, cols] = (o * rs * g_ref[...] * _silu(r_ref[:, cols].astype(F32))).astype(og_ref.dtype)
        y_ref[...] = _nn(og_ref[...], w_ref[...]).astype(y_ref.dtype)

    ospec = pl.BlockSpec((1, TM_OUT, D), lambda b, j: (b, j, 0))
    row = pl.BlockSpec((TM_OUT, D), lambda b, j: (b * tiles + j, 0))
    return _pallas(
        body, name="gla_out_proj", grid=(n, tiles),
        in_specs=[ospec, ospec, row, pl.BlockSpec((1, HEAD_V), lambda b, j: (0, 0)),
                  pl.BlockSpec((D, D), lambda b, j: (0, 0))],
        out_specs=[row, row],
        out_shape=[jax.ShapeDtypeStruct((n * SEQ, D), BF16), jax.ShapeDtypeStruct((n * SEQ, D), BF16)],
        compiler_params=_params(("parallel", "parallel")),
    )(o_f, o_b, r, gnorm, gla_proj)


def gla_out_bwd(o_f, o_b, r, dyg, gla_proj, gnorm, og):
    n = o_f.shape[0]
    tiles = SEQ // TM_EW

    def body(of_ref, ob_ref, r_ref, d_ref, w_ref, g_ref, og_ref, do_ref, dr_ref, dg_ref, gw_ref, dog_buf):
        @pl.when((pl.program_id(0) == 0) & (pl.program_id(1) == 0))
        def _():
            dg_ref[...] = jnp.zeros_like(dg_ref)
            gw_ref[...] = jnp.zeros_like(gw_ref)

        gw_ref[...] += _tn(og_ref[...], d_ref[...])
        dog_buf[...] = _nt(d_ref[...], w_ref[...])
        for h in range(HEADS):
            cols = pl.ds(h * HEAD_V, HEAD_V)
            o = of_ref[0, :, cols].astype(F32) + ob_ref[0, :, cols].astype(F32)
            rv = r_ref[:, cols].astype(F32)
            dv = dog_buf[:, cols]
            rs = lax.rsqrt(jnp.mean(o * o, axis=-1, keepdims=True) + EPS)
            oh = o * rs
            dr_ref[:, cols] = (dv * oh * g_ref[...] * _dsilu(rv)).astype(dr_ref.dtype)
            dn = dv * _silu(rv)
            dg_ref[...] += jnp.sum(dn * oh, axis=0, keepdims=True)
            doh = dn * g_ref[...]
            do_ref[0, :, cols] = _b16(rs * (doh - oh * jnp.mean(doh * oh, axis=-1, keepdims=True)))

    ospec = pl.BlockSpec((1, TM_EW, D), lambda b, j: (b, j, 0))
    row = pl.BlockSpec((TM_EW, D), lambda b, j: (b * tiles + j, 0))
    vec = pl.BlockSpec((1, HEAD_V), lambda b, j: (0, 0))
    return _pallas(
        body, name="gla_out_bwd", grid=(n, tiles),
        in_specs=[ospec, ospec, row, row, pl.BlockSpec((D, D), lambda b, j: (0, 0)), vec, row],
        out_specs=[ospec, row, vec, pl.BlockSpec((D, D), lambda b, j: (0, 0))],
        out_shape=[jax.ShapeDtypeStruct((n, SEQ, D), BF16), jax.ShapeDtypeStruct((n * SEQ, D), BF16),
                   jax.ShapeDtypeStruct((1, HEAD_V), F32), jax.ShapeDtypeStruct((D, D), F32)],
        scratch_shapes=[pltpu.VMEM((TM_EW, D), F32)],
        compiler_params=_params(("arbitrary", "arbitrary")),
    )(o_f, o_b, r, dyg, gla_proj, gnorm, og)


TM_OUT = 512


def merge_out_final(p5, y_conv, y_gla, w_out, x2, gate, final_g, target, n_samples):
    t = x2.shape[0]
    tiles = SEQ // TM_OUT

    def body(mc_ref, mg_ref, yc_ref, yg_ref, w_ref, x_ref, gate_ref, g_ref, t_ref,
             mrg_ref, dh_ref, dmo_ref, dgate_ref, dg_ref, loss_ref):
        b, j = pl.program_id(0), pl.program_id(1)
        f = lambda ref: ref[...].astype(F32)
        merged = _b16(_sigmoid(f(mc_ref)) * f(yc_ref) + _sigmoid(f(mg_ref)) * f(yg_ref))
        mrg_ref[...] = merged
        mo_v = _nn(merged, w_ref[...])
        h = x_ref[...] + gate_ref[0] * mo_v
        rs = lax.rsqrt(jnp.mean(h * h, axis=-1, keepdims=True) + EPS)
        nh = h * rs
        err = nh * g_ref[...] - t_ref[...]
        dy = err * (1.0 / D)
        dn = dy * g_ref[...]
        dh = rs * (dn - nh * jnp.mean(dn * nh, axis=-1, keepdims=True))
        dh_ref[...] = dh
        dmo_ref[...] = (dh * gate_ref[0]).astype(dmo_ref.dtype)

        @pl.when(j == 0)
        def _():
            dgate_ref[...] = jnp.zeros_like(dgate_ref)

        @pl.when((b == 0) & (j == 0))
        def _():
            dg_ref[...] = jnp.zeros_like(dg_ref)
            loss_ref[...] = jnp.zeros_like(loss_ref)

        dgate_ref[0] += jnp.sum(dh * mo_v, axis=0, keepdims=True)
        dg_ref[...] += jnp.sum(dy * nh, axis=0, keepdims=True)
        loss_ref[...] += (0.5 / D) * jnp.sum(err * err)

    row = pl.BlockSpec((TM_OUT, D), lambda b, j: (b * tiles + j, 0))
    per = pl.BlockSpec((1, 1, D), lambda b, j: (b, 0, 0))
    vec = pl.BlockSpec((1, D), lambda b, j: (0, 0))
    return _pallas(
        body, name="merge_out_final", grid=(n_samples, tiles),
        in_specs=[row, pl.BlockSpec((TM_OUT, D), lambda b, j: (b * tiles + j, 1)), row, row,
                  pl.BlockSpec((D, D), lambda b, j: (0, 0)), row, per, vec, row],
        out_specs=[row, row, row, per, vec, pl.BlockSpec((8, 128), lambda b, j: (0, 0))],
        out_shape=[jax.ShapeDtypeStruct((t, D), BF16), jax.ShapeDtypeStruct((t, D), F32), jax.ShapeDtypeStruct((t, D), BF16),
                   jax.ShapeDtypeStruct((n_samples, 1, D), F32), jax.ShapeDtypeStruct((1, D), F32),
                   jax.ShapeDtypeStruct((8, 128), F32)],
        compiler_params=_params(("arbitrary", "arbitrary")),
    )(p5, p5, y_conv, y_gla, w_out, x2, gate, final_g, target)


def out_dgrad_merge_bwd(p5, y_conv, y_gla, dmo, w_out, merged):
    t = y_conv.shape[0]

    def body(mc_ref, mg_ref, yc_ref, yg_ref, d_ref, w_ref, mrg_ref, dyc_ref, dyg_ref, dp_ref, gw_ref):
        f = lambda ref: ref[...].astype(F32)

        @pl.when(pl.program_id(0) == 0)
        def _():
            gw_ref[...] = jnp.zeros_like(gw_ref)

        gw_ref[...] += _tn(mrg_ref[...], d_ref[...])
        d = _nt(d_ref[...], w_ref[...])
        sc = _sigmoid(f(mc_ref))
        sg = _sigmoid(f(mg_ref))
        dyc_ref[...] = (d * sc).astype(dyc_ref.dtype)
        dyg_ref[...] = (d * sg).astype(dyg_ref.dtype)
        dp_ref[:, pl.ds(0, D)] = (d * f(yc_ref) * sc * (1.0 - sc)).astype(dp_ref.dtype)
        dp_ref[:, pl.ds(D, D)] = (d * f(yg_ref) * sg * (1.0 - sg)).astype(dp_ref.dtype)

    row = pl.BlockSpec((TM_OUT, D), lambda i: (i, 0))
    return _pallas(
        body, name="out_dgrad_merge_bwd", grid=(t // TM_OUT,),
        in_specs=[row, pl.BlockSpec((TM_OUT, D), lambda i: (i, 1)), row, row, row, pl.BlockSpec((D, D), lambda i: (0, 0)),
                  row],
        out_specs=[row, row, pl.BlockSpec((TM_OUT, 2 * D), lambda i: (i, 0)), pl.BlockSpec((D, D), lambda i: (0, 0))],
        out_shape=[jax.ShapeDtypeStruct((t, D), BF16), jax.ShapeDtypeStruct((t, D), BF16),
                   jax.ShapeDtypeStruct((t, 2 * D), BF16), jax.ShapeDtypeStruct((D, D), F32)],
        compiler_params=_params(("arbitrary",)),
    )(p5, p5, y_conv, y_gla, dmo, w_out, merged)


def local_step(x, ctx, target, mod, wts, small, p_sh, chip, core):
    n = x.shape[0]
    t = n * SEQ
    t_all = t + n * NCTX
    x2 = x.reshape(t, D)
    ctx2 = ctx.reshape(n * NCTX, D)
    tgt2 = target.reshape(t, D)
    scale1, shift, gate = mod

    u = norm_mod_fwd(x2, ctx2, scale1, shift, small["norm_g"])
    p1, p2, p3, p4, p5, p_all = proj_all(u, [wts["w%d" % i] for i in range(1, 6)], [small["b%d" % i] for i in range(1, 6)],
                                         [t, t, t_all, t, t], p_sh, tm=512)
    p_full = jnp.stack([jnp.where(chip == i, p_sh, p_all[i]) for i in range(N_CHIPS)])
    wts = dict(wts, conv_proj=p_full[:, 0:256].reshape(D, D), gla_proj=p_full[:, 256:512].reshape(D, D),
               w_out=p_full[:, 512:768].reshape(D, D))

    aconv = conv_fwd(p1, small["conv_w"], small["conv_b"], n)
    ac, y_conv = ln_gate_proj(aconv, p2, small["conv_ln_g"], small["conv_ln_b"], wts["conv_proj"])

    qs, ks, vs, cum_f, cum_b = gla_prep_fwd(p3, small["upf"], small["upb"], small["bias_f"], small["bias_b"], n)
    o_f, s_f, sfin_f = gla_scan_fwd(qs, ks, vs, cum_f, rev=False, name="gla_scan_fwd_f")
    o_b, s_b, sfin_b = gla_scan_fwd(qs, ks, vs, cum_b, rev=True, name="gla_scan_fwd_b")
    og, y_gla = gla_out_proj(o_f, o_b, p4, small["gla_norm_g"], wts["gla_proj"])

    merged, dh, dmo, dgate, d_final_g, loss = merge_out_final(p5, y_conv, y_gla, wts["w_out"], x2, gate,
                                                              small["final_norm_g"], tgt2, n)

    g = {"final_norm_g": d_final_g}
    dyc, dyg, dp5, g["w_out"] = out_dgrad_merge_bwd(p5, y_conv, y_gla, dmo, wts["w_out"], merged)

    daconv, dp2, g["conv_ln_g"], g["conv_ln_b"], g["conv_proj"] = ln_gate_bwd(
        aconv, p2, dyc, wts["conv_proj"], small["conv_ln_g"], small["conv_ln_b"], ac)
    dp1, dconv_w, dconv_b = conv_bwd(p1, daconv, small["conv_w"], n)
    g["conv_w"], g["conv_b"] = dconv_w, dconv_b

    do, dp4, g["gla_norm_g"], g["gla_proj"] = gla_out_bwd(o_f, o_b, p4, dyg, wts["gla_proj"], small["gla_norm_g"], og)
    g["proj"] = jnp.concatenate([g["conv_proj"].reshape(N_CHIPS, 256, D), g["gla_proj"].reshape(N_CHIPS, 256, D),
                                 g["w_out"].reshape(N_CHIPS, 256, D)], 1)
    dq_f, dk_f, dv_f, dc_f, gotp = gla_scan_bwd(qs, ks, vs, cum_f, s_f, sfin_f, do, rev=False, name="gla_scan_bwd_f",
                                                rider=("swap", g["proj"]))
    pap16 = pair_add(core, g["proj"], gotp, name="pair_add_p", tr=384)
    dq_b, dk_b, dv_b, dc_b, rbp = gla_scan_bwd(qs, ks, vs, cum_b, s_b, sfin_b, do, rev=True, name="gla_scan_bwd_b",
                                               rider=("exchange", pap16))
    dp3, g["upf"], g["upb"], g["bias_f"], g["bias_b"] = gla_prep_bwd(
        p3, dq_f, dq_b, dk_f, dk_b, dv_f, dv_b, dc_f, dc_b,
        small["upf"], small["upb"], small["bias_f"], small["bias_b"], n)

    dps = [dp1, dp2, dp3, dp4, dp5]
    early = [g["conv_b"].sum(0), g["conv_ln_g"], g["conv_ln_b"], g["bias_f"], g["bias_b"], g["gla_norm_g"],
             g["final_norm_g"], g["conv_w"].sum(0)[:CONV_K], g["upf"][0:16], g["upb"][16:32]]
    got = {}
    for i in [0, 1, 3, 4, 2]:
        dp = dps[i]
        rows = dp.shape[0]
        tn = W3 if dp.shape[1] == W3 else 1024
        others = [j for j in range(5) if j != i]
        swap = ([g["w%d" % (j + 1)] for j in others], _pack(early)) if i == 2 else None
        outs = matmul_tn(u, dp, name="w_in_wgrad_%d" % (i + 1), t=rows, tn=tn, tt=1024 if rows % 1024 == 0 else 768,
                         colsum=True, swap=swap)
        g["w%d" % (i + 1)], g["b%d" % (i + 1)] = outs[0], outs[1]
        if swap is not None:
            got = dict(zip(others, outs[2:2 + len(others)]))
            sall_early = outs[2 + len(others)]
    g["gate"] = dgate
    return loss, dh, dps, g, got, (pap16, rbp), (sall_early, [a.shape for a in early])


def _group_cols(w):
    gv, gg, z = w[..., 0:1024], w[..., 1024:2048], w[..., 2048:3072]
    q, k, v = w[..., 3072:3584], w[..., 3584:4096], w[..., 4096:5120]
    ab = w[..., 5120:5152]
    r, mc, mg = w[..., 5152:6176], w[..., 6176:7200], w[..., 7200:8224]
    g1 = jnp.concatenate([p for j in range(CONV_NCB)
                          for p in (gv[..., CONV_CB * j:CONV_CB * (j + 1)], gg[..., CONV_CB * j:CONV_CB * (j + 1)])], -1)
    pad = jnp.zeros(w.shape[:-1] + (W3 - 2080,), w.dtype)
    g3 = jnp.concatenate([v, q, k, ab, pad], -1)
    return g1, z, g3, r, jnp.concatenate([mc, mg], -1)


def _ungroup_cols(g1, g2, g3, g4, g5):
    gv = jnp.concatenate([g1[..., 2 * CONV_CB * j:2 * CONV_CB * j + CONV_CB] for j in range(CONV_NCB)], -1)
    gg = jnp.concatenate([g1[..., 2 * CONV_CB * j + CONV_CB:2 * CONV_CB * (j + 1)] for j in range(CONV_NCB)], -1)
    v, q, k, ab = g3[..., 0:1024], g3[..., 1024:1536], g3[..., 1536:2048], g3[..., 2048:2080]
    return jnp.concatenate([gv, gg, g2, q, k, v, ab, g4, g5[..., 0:1024], g5[..., 1024:2048]], -1)


def _natural_pieces():
    pieces = [(CONV_CB * j, CONV_CB, 0, 2 * CONV_CB * j) for j in range(CONV_NCB)]
    pieces += [(1024 + CONV_CB * j, CONV_CB, 0, 2 * CONV_CB * j + CONV_CB) for j in range(CONV_NCB)]
    pieces += [(2048, 1024, 1, 0), (3072, 512, 2, O3_Q), (3584, 512, 2, O3_K), (4096, 1024, 2, O3_V), (5120, 32, 2, O3_AB),
               (5152, 1024, 3, 0), (6176, 1024, 4, 0), (7200, 1024, 4, 1024)]
    return sorted(pieces)


def _ungroup_to_shards(groups):
    shards = []
    for i in range(N_CHIPS):
        lo, hi = i * W_IN_SHARD, (i + 1) * W_IN_SHARD
        parts = []
        for nat, width, g, gcol in _natural_pieces():
            a, b = max(nat, lo), min(nat + width, hi)
            if a < b:
                parts.append(groups[g][:, gcol + a - nat:gcol + b - nat])
        shards.append(jnp.concatenate(parts, 1))
    return jnp.stack(shards)


def _pad_up(up, row0):
    return jnp.zeros((128, GLA_DK), F32).at[row0:row0 + up.shape[0]].set(up)


def _adamw_math(w, g, m, v):
    m = ADAM_B1 * m + (1.0 - ADAM_B1) * g
    v = ADAM_B2 * v + (1.0 - ADAM_B2) * (g * g)
    m_hat = m / (1.0 - ADAM_B1 ** ADAM_STEP)
    v_hat = v / (1.0 - ADAM_B2 ** ADAM_STEP)
    delta = -ADAM_LR * (m_hat / (jnp.sqrt(v_hat) + ADAM_EPS) + ADAM_WD * w)
    return delta, m, v


def adamw2d(w, g, m, v, *, name, tr, tcols=None):
    rows, cols = w.shape[-2:]

    def body(w_ref, g_ref, m_ref, v_ref, d_ref, nm_ref, nv_ref):
        d_ref[...], nm_ref[...], nv_ref[...] = _adamw_math(w_ref[...], g_ref[...], m_ref[...], v_ref[...])

    tcols = cols if tcols is None else tcols
    if w.ndim == 3:
        spec = pl.BlockSpec((1, tr, tcols), lambda i, j: (0, i, j))
    else:
        spec = pl.BlockSpec((tr, tcols), lambda i, j: (i, j))
    return _pallas(
        body, name=name, grid=(rows // tr, cols // tcols), in_specs=[spec] * 4, out_specs=[spec] * 3,
        out_shape=[jax.ShapeDtypeStruct(w.shape, F32)] * 3, compiler_params=_params(("parallel", "parallel")),
    )(w, g, m, v)


def adamw_many(ws, gs, ms, vs):
    k = len(ws)
    two = lambda a: a.reshape((-1, a.shape[-1]))

    def body(*refs):
        w_refs, g_refs, m_refs, v_refs = refs[:k], refs[k:2 * k], refs[2 * k:3 * k], refs[3 * k:4 * k]
        d_refs, nm_refs, nv_refs = refs[4 * k:5 * k], refs[5 * k:6 * k], refs[6 * k:7 * k]
        for i in range(k):
            d_refs[i][...], nm_refs[i][...], nv_refs[i][...] = _adamw_math(
                w_refs[i][...], g_refs[i][...], m_refs[i][...], v_refs[i][...])

    shapes = [jax.ShapeDtypeStruct(two(a).shape, F32) for a in ws]
    outs = _pallas(body, name="adamw_small", out_shape=shapes * 3, compiler_params=_params())(
        *[two(a) for a in ws], *[two(a) for a in gs], *[two(a) for a in ms], *[two(a) for a in vs])
    back = lambda lst: [o.reshape(a.shape) for o, a in zip(lst, ws)]
    return back(outs[:k]), back(outs[k:2 * k]), back(outs[2 * k:])


def sum_devices(sall, *, name):
    rows = sall.shape[1]

    def body(s_ref, o_ref):
        acc = s_ref[0]
        for d in range(1, N_DEV):
            acc = acc + s_ref[d]
        o_ref[...] = acc

    return _pallas(body, name=name, out_shape=jax.ShapeDtypeStruct((rows, D), F32),
                   compiler_params=_params())(sall)


def pair_add(core, g, got, *, name, tr):
    n, rows, cols = got.shape
    g4 = g.reshape(n, 2, rows, cols)

    def body(core_ref, g_ref, got_ref, ob_ref):
        del core_ref
        ob_ref[0] = (g_ref[0, 0] + got_ref[0]).astype(BF16)

    spec = pl.BlockSpec((1, tr, cols), lambda i, t, core_ref: (i, t, 0))
    return _pallas(
        body, name=name,
        grid_spec=pltpu.PrefetchScalarGridSpec(
            num_scalar_prefetch=1, grid=(n, rows // tr),
            in_specs=[pl.BlockSpec((1, 1, tr, cols), lambda i, t, core_ref: (i, core_ref[0], t, 0)), spec],
            out_specs=spec),
        out_shape=jax.ShapeDtypeStruct(got.shape, BF16),
        compiler_params=_params(("parallel", "parallel")))(core, g4, got)


def pair_add_groups(core, gs, gots, *, tr):
    k = len(gs)
    rows = gots[0].shape[0]

    def body(core_ref, *refs):
        del core_ref
        for i in range(k):
            refs[2 * k + i][...] = (refs[i][0] + refs[k + i][...]).astype(BF16)

    g_specs = [pl.BlockSpec((1, tr, a.shape[1]), lambda t, core_ref: (core_ref[0], t, 0)) for a in gots]
    r_specs = [pl.BlockSpec((tr, a.shape[1]), lambda t, core_ref: (t, 0)) for a in gots]
    return _pallas(
        body, name="pair_add_w",
        grid_spec=pltpu.PrefetchScalarGridSpec(num_scalar_prefetch=1, grid=(rows // tr,),
                                               in_specs=g_specs + r_specs, out_specs=r_specs),
        out_shape=[jax.ShapeDtypeStruct(a.shape, BF16) for a in gots],
        compiler_params=_params(("parallel",)))(core, *[a.reshape(2, rows, a.shape[1]) for a in gs], *gots)


def chip_add(place, pa, rb, *, name, tr):
    _, rows, cols = pa.shape

    def body(place_ref, m_ref, r_ref, o_ref):
        del place_ref
        o_ref[0] = ((m_ref[0].astype(F32) + r_ref[0].astype(F32)) + r_ref[1].astype(F32)) + r_ref[2].astype(F32)

    return _pallas(
        body, name=name,
        grid_spec=pltpu.PrefetchScalarGridSpec(
            num_scalar_prefetch=1, grid=(rows // tr,),
            in_specs=[pl.BlockSpec((1, tr, cols), lambda t, place_ref: (place_ref[0], t, 0)),
                      pl.BlockSpec((3, tr, cols), lambda t, place_ref: (0, t, 0))],
            out_specs=pl.BlockSpec((1, tr, cols), lambda t, place_ref: (place_ref[1], t, 0))),
        out_shape=jax.ShapeDtypeStruct((2, rows, cols), F32),
        compiler_params=_params(("parallel",)))(place, pa, rb)


def ada_bwd(call, cctx_rows, dm_shard, dm_full, adaw):
    nsh = adaw.shape[1]

    def body(c_ref, cc_ref, dms_ref, dmf_ref, w_ref, gw_ref, gb_ref, pq_ref):
        a_lat = _silu(c_ref[...])
        a_ctx = _silu(cc_ref[...])
        dms = dms_ref[...]
        gw_ref[...] = _tn(a_lat, dms[0:64], HI) + _tn(a_ctx, dms[64:72], HI)
        gb_ref[...] = jnp.sum(dmf_ref[...], axis=0, keepdims=True)
        part = _nt(dms[64:72], w_ref[...], HI)
        pq_ref[...] = jnp.zeros_like(pq_ref) + jnp.sum(part, axis=0, keepdims=True)

    return _pallas(body, name="ada_bwd",
                   out_shape=[jax.ShapeDtypeStruct((D, nsh), F32), jax.ShapeDtypeStruct((1, 3 * D), F32),
                              jax.ShapeDtypeStruct((8, D), F32)],
                   compiler_params=_params())(call, cctx_rows, dm_shard, dm_full, adaw)


def cctx_grad(pq_all, cctx_rows):
    def body(p_ref, c_ref, o_ref):
        acc = p_ref[0]
        for qi in range(1, N_CHIPS):
            acc = acc + p_ref[qi]
        o_ref[...] = acc * _dsilu(c_ref[...])

    return _pallas(body, name="cctx_grad", out_shape=jax.ShapeDtypeStruct((8, D), F32),
                   compiler_params=_params())(pq_all, cctx_rows)


def _place():
    x, y, c = lax.axis_index("x"), lax.axis_index("y"), lax.axis_index("c")
    chips = [(1 - x, y), (x, 1 - y), (1 - x, 1 - y)]
    return x, y, c, chips


def _all_peers(x, y, c):
    return [((1 - x) if r & 4 else x, (1 - y) if r & 2 else y, (1 - c) if r & 1 else c) for r in range(1, N_DEV)]


def _remote(src, dst, send_sem, recv_sem, dev):
    return pltpu.make_async_remote_copy(src_ref=src, dst_ref=dst, send_sem=send_sem, recv_sem=recv_sem,
                                        device_id=dev, device_id_type=MESH)


ANY = pl.BlockSpec(memory_space=pl.ANY)
VMEM = pl.BlockSpec(memory_space=pltpu.VMEM)
F_ROWS = 16


W_ROW_CHUNKS = 4
P_ROW_CHUNKS = 2
N_BULK = W_ROW_CHUNKS + P_ROW_CHUNKS


def _half_chunks(core, n_rows, align, which=(0, 1)):
    out = []
    for a, k in ((0, W_ROW_CHUNKS), (1, P_ROW_CHUNKS)):
        if a not in which:
            continue
        half = n_rows[a] // 2
        size = half // k
        for i in range(k):
            start = core * half + i * size
            out.append((a, pl.ds(start if isinstance(start, int) else pl.multiple_of(start, align), size)))
    return out


def gather_weights(c8, cctx8, adaw, adab, w_sh, fp):
    nsh = adaw.shape[1]

    def body(c_ref, cctx_ref, adaw_ref, adab_ref, w_ref, fp_ref, wall_ref, fall_ref, call_ref, mall_ref,
             abuf, w_send, w_recv, h_send, h_recv, c_send, c_recv, m_send, m_recv, f_send, f_recv):
        x, y, c, chips = _place()
        q = 2 * x + y
        dev = 4 * x + 2 * y + c
        qs = [2 * cx + cy for cx, cy in chips]
        sib = (x, y, 1 - c)
        srcs, dsts = (w_ref,), (wall_ref,)
        n_rows = (w_ref.shape[0],)
        mine = _half_chunks(c, n_rows, 16, which=(0,))
        other = _half_chunks(1 - c, n_rows, 16, which=(0,))

        bulk = [[_remote(srcs[a].at[rows], dsts[a].at[q, rows], w_send.at[j * N_BULK + i], w_recv.at[j * N_BULK + i],
                         (*chips[j], c)) for i, (a, rows) in enumerate(mine)] for j in range(3)]
        fall_ref[q] = fp_ref[...]
        small = [_remote(fp_ref, fall_ref.at[q], f_send.at[j], f_recv.at[j], (*chips[j], c)) for j in range(3)]
        my_rows = pl.ds(pl.multiple_of(8 * dev, 8), 8)
        call_ref[my_rows, :] = c_ref[...]
        cond = [_remote(c_ref, call_ref.at[my_rows, :], c_send.at[r], c_recv.at[r], peer)
                for r, peer in enumerate(_all_peers(x, y, c))]
        for cp in sum(bulk, []) + small + cond:
            cp.start()
        for cp in cond:
            cp.wait_recv()

        abuf[pl.ds(0, 64), :] = _silu(call_ref[...])
        abuf[pl.ds(64, 8), :] = _silu(cctx_ref[...])
        mall_ref[q] = _nn(abuf[...], adaw_ref[...], HI) + adab_ref[...]
        mod = [_remote(mall_ref.at[q], mall_ref.at[q], m_send.at[j], m_recv.at[j], (*chips[j], c)) for j in range(3)]
        for cp in mod:
            cp.start()

        handed = []
        for j in range(3):
            for i, (a, rows) in enumerate(mine):
                bulk[j][i].wait_recv()
                cp = _remote(dsts[a].at[qs[j], rows], dsts[a].at[qs[j], rows],
                             h_send.at[j * N_BULK + i], h_recv.at[j * N_BULK + i], sib)
                cp.start()
                handed.append(cp)
        for j in range(3):
            for i, (a, rows) in enumerate(other):
                _remote(dsts[a].at[qs[j], rows], dsts[a].at[qs[j], rows],
                        h_send.at[j * N_BULK + i], h_recv.at[j * N_BULK + i], sib).wait_recv()
        for cp in mod + small:
            cp.wait_recv()
        for cp in sum(bulk, []) + small + cond + mod + handed:
            cp.wait_send()

    def dma(n):
        return pltpu.SemaphoreType.DMA((n,))

    return _pallas(
        body, name="gather_weights",
        in_specs=[VMEM, VMEM, VMEM, VMEM, ANY, VMEM],
        out_specs=[ANY, VMEM, VMEM, VMEM],
        out_shape=[jax.ShapeDtypeStruct((N_CHIPS,) + w_sh.shape, BF16),
                   jax.ShapeDtypeStruct((N_CHIPS, F_ROWS, D), F32),
                   jax.ShapeDtypeStruct((8 * N_DEV, D), F32), jax.ShapeDtypeStruct((N_CHIPS, MOD_ROWS, nsh), F32)],
        scratch_shapes=[pltpu.VMEM((MOD_ROWS, D), F32), dma(3 * N_BULK), dma(3 * N_BULK), dma(3 * N_BULK), dma(3 * N_BULK),
                        dma(7), dma(7), dma(3), dma(3), dma(3), dma(3)],
        compiler_params=_params(),
    )(c8, cctx8, adaw, adab, w_sh, fp)


def _pair_count(gs, gp):
    return len(gs) * W_ROW_CHUNKS + (0 if gp is None else N_CHIPS * P_ROW_CHUNKS)


def _pair_got_shapes(gs, gp):
    shapes = [jax.ShapeDtypeStruct((D // 2, a.shape[1]), F32) for a in gs]
    if gp is not None:
        shapes.append(jax.ShapeDtypeStruct((N_CHIPS, gp.shape[1] // 2, gp.shape[2]), F32))
    return shapes


def _pair_copies(g_refs, gp_ref, got_refs, gotp_ref, a_send, a_recv):
    x, y, c, _ = _place()
    sib = (x, y, 1 - c)
    pair = []
    half, size = D // 2, D // 2 // W_ROW_CHUNKS
    for gi in range(len(g_refs)):
        for i in range(W_ROW_CHUNKS):
            k = len(pair)
            rows_o = pl.ds(pl.multiple_of((1 - c) * half + i * size, 8), size)
            pair.append(_remote(g_refs[gi].at[rows_o], got_refs[gi].at[pl.ds(i * size, size)],
                                a_send.at[k], a_recv.at[k], sib))
    if gp_ref is not None:
        half, size = gp_ref.shape[1] // 2, gp_ref.shape[1] // 2 // P_ROW_CHUNKS
        for s in range(N_CHIPS):
            for i in range(P_ROW_CHUNKS):
                k = len(pair)
                rows_o = pl.ds(pl.multiple_of((1 - c) * half + i * size, 8), size)
                pair.append(_remote(gp_ref.at[s, rows_o], gotp_ref.at[s, pl.ds(i * size, size)],
                                    a_send.at[k], a_recv.at[k], sib))
    return pair


def pair_swap(gs):
    n_gs = len(gs)

    def body(*refs):
        g_refs, got_refs = refs[:n_gs], refs[n_gs:2 * n_gs]
        a_send, a_recv = refs[2 * n_gs:]
        pair = _pair_copies(g_refs, None, got_refs, None, a_send, a_recv)
        for cp in pair:
            cp.start()
        for cp in pair:
            cp.wait_recv()
        for cp in pair:
            cp.wait_send()

    return _pallas(
        body, name="pair_swap", in_specs=[ANY] * n_gs, out_specs=[ANY] * n_gs,
        out_shape=_pair_got_shapes(gs, None),
        scratch_shapes=[pltpu.SemaphoreType.DMA((_pair_count(gs, None),)), pltpu.SemaphoreType.DMA((_pair_count(gs, None),))],
        compiler_params=_params(),
    )(*gs)


def gather_small(sm):
    rows = sm.shape[0]

    def body(sm_ref, sall_ref, s_send, s_recv):
        x, y, c, _ = _place()
        dev = 4 * x + 2 * y + c
        sall_ref[dev] = sm_ref[...]
        small = [_remote(sm_ref, sall_ref.at[dev], s_send.at[r], s_recv.at[r], peer)
                 for r, peer in enumerate(_all_peers(x, y, c))]
        for cp in small:
            cp.start()
        for cp in small:
            cp.wait_recv()
        for cp in small:
            cp.wait_send()

    return _pallas(
        body, name="gather_small", in_specs=[VMEM], out_specs=VMEM,
        out_shape=jax.ShapeDtypeStruct((N_DEV, rows, D), F32),
        scratch_shapes=[pltpu.SemaphoreType.DMA((7,)), pltpu.SemaphoreType.DMA((7,))],
        compiler_params=_params(),
    )(sm)


def pair_share(ghw, ghp, pq):
    def body(ghw_ref, ghp_ref, pq_ref, outw_ref, outp_ref, pqa_ref, send, recv, p_send, p_recv):
        del ghw_ref, ghp_ref
        x, y, c, chips = _place()
        q = 2 * x + y
        refs = (outw_ref, outp_ref)
        n_rows = (2 * outw_ref.shape[1], 2 * outp_ref.shape[1])
        pair = [_remote(refs[a].at[c, rows], refs[a].at[c, rows], send.at[i], recv.at[i], (x, y, 1 - c))
                for i, (a, rows) in enumerate(_half_chunks(0, n_rows, 8))]
        pqa_ref[q] = pq_ref[...]
        small = [_remote(pq_ref, pqa_ref.at[q], p_send.at[j], p_recv.at[j], (*chips[j], c)) for j in range(3)]
        for cp in pair + small:
            cp.start()
        for i, (a, rows) in enumerate(_half_chunks(0, n_rows, 8)):
            _remote(refs[a].at[1 - c, rows], refs[a].at[1 - c, rows], send.at[i], recv.at[i], (x, y, 1 - c)).wait_recv()
        for cp in small:
            cp.wait_recv()
        for cp in pair + small:
            cp.wait_send()

    return _pallas(
        body, name="pair_share", in_specs=[ANY, ANY, VMEM], out_specs=[ANY, ANY, VMEM],
        out_shape=[jax.ShapeDtypeStruct(ghw.shape, F32), jax.ShapeDtypeStruct(ghp.shape, F32),
                   jax.ShapeDtypeStruct((N_CHIPS, 8, D), F32)],
        scratch_shapes=[pltpu.SemaphoreType.DMA((N_BULK,)), pltpu.SemaphoreType.DMA((N_BULK,)),
                        pltpu.SemaphoreType.DMA((3,)), pltpu.SemaphoreType.DMA((3,))],
        input_output_aliases={0: 0, 1: 1},
        compiler_params=_params(),
    )(ghw, ghp, pq)


def _rows_of(shape):
    size = 1
    for s in shape:
        size *= s
    return -(-size // D)


def _pack(arrs, rows_multiple=8):
    parts = []
    total = 0
    for a in arrs:
        f = a.reshape(-1).astype(F32)
        r = _rows_of(a.shape)
        parts.append(jnp.pad(f, (0, r * D - f.shape[0])))
        total += r
    pad_rows = (-total) % rows_multiple
    if pad_rows:
        parts.append(jnp.zeros((pad_rows * D,), F32))
    return jnp.concatenate(parts).reshape(-1, D)


def _unpack(p, shapes):
    out = []
    r0 = 0
    for shp in shapes:
        r = _rows_of(shp)
        size = 1
        for s in shp:
            size *= s
        out.append(p[r0:r0 + r].reshape(-1)[:size].reshape(shp))
        r0 += r
    return out


WEIGHT_NAMES = ['c_ctx', 'ada_w', 'ada_b', 'norm_g', 'w_in', 'b_in', 'conv_w', 'conv_b', 'conv_ln_g', 'conv_ln_b',
                'conv_proj', 'decay_up_fwd', 'decay_bias_fwd', 'decay_up_bwd', 'decay_bias_bwd', 'gla_norm_g', 'gla_proj',
                'w_out', 'final_norm_g']
SMALL_NAMES = ['c_ctx', 'ada_b', 'norm_g', 'b_in', 'conv_w', 'conv_b', 'conv_ln_g', 'conv_ln_b', 'decay_up_fwd',
               'decay_bias_fwd', 'decay_up_bwd', 'decay_bias_bwd', 'gla_norm_g', 'final_norm_g']


def kernel(x, c, ctx, c_ctx, ada_w, ada_b, norm_g, w_in, b_in, conv_w, conv_b, conv_ln_g, conv_ln_b, conv_proj, decay_up_fwd, decay_bias_fwd, decay_up_bwd, decay_bias_bwd, gla_norm_g, gla_proj, w_out, final_norm_g, loss_target, m_c_ctx, m_ada_w, m_ada_b, m_norm_g, m_w_in, m_b_in, m_conv_w, m_conv_b, m_conv_ln_g, m_conv_ln_b, m_conv_proj, m_decay_up_fwd, m_decay_bias_fwd, m_decay_up_bwd, m_decay_bias_bwd, m_gla_norm_g, m_gla_proj, m_w_out, m_final_norm_g, v_c_ctx, v_ada_w, v_ada_b, v_norm_g, v_w_in, v_b_in, v_conv_w, v_conv_b, v_conv_ln_g, v_conv_ln_b, v_conv_proj, v_decay_up_fwd, v_decay_bias_fwd, v_decay_up_bwd, v_decay_bias_bwd, v_gla_norm_g, v_gla_proj, v_w_out, v_final_norm_g):
    w = dict(c_ctx=c_ctx, ada_w=ada_w, ada_b=ada_b, norm_g=norm_g, w_in=w_in, b_in=b_in, conv_w=conv_w, conv_b=conv_b,
             conv_ln_g=conv_ln_g, conv_ln_b=conv_ln_b, conv_proj=conv_proj, decay_up_fwd=decay_up_fwd,
             decay_bias_fwd=decay_bias_fwd, decay_up_bwd=decay_up_bwd, decay_bias_bwd=decay_bias_bwd,
             gla_norm_g=gla_norm_g, gla_proj=gla_proj, w_out=w_out, final_norm_g=final_norm_g)
    m = dict(c_ctx=m_c_ctx, ada_w=m_ada_w, ada_b=m_ada_b, norm_g=m_norm_g, w_in=m_w_in, b_in=m_b_in, conv_w=m_conv_w,
             conv_b=m_conv_b, conv_ln_g=m_conv_ln_g, conv_ln_b=m_conv_ln_b, conv_proj=m_conv_proj,
             decay_up_fwd=m_decay_up_fwd, decay_bias_fwd=m_decay_bias_fwd, decay_up_bwd=m_decay_up_bwd,
             decay_bias_bwd=m_decay_bias_bwd, gla_norm_g=m_gla_norm_g, gla_proj=m_gla_proj, w_out=m_w_out,
             final_norm_g=m_final_norm_g)
    v = dict(c_ctx=v_c_ctx, ada_w=v_ada_w, ada_b=v_ada_b, norm_g=v_norm_g, w_in=v_w_in, b_in=v_b_in, conv_w=v_conv_w,
             conv_b=v_conv_b, conv_ln_g=v_conv_ln_g, conv_ln_b=v_conv_ln_b, conv_proj=v_conv_proj,
             decay_up_fwd=v_decay_up_fwd, decay_bias_fwd=v_decay_bias_fwd, decay_up_bwd=v_decay_up_bwd,
             decay_bias_bwd=v_decay_bias_bwd, gla_norm_g=v_gla_norm_g, gla_proj=v_gla_proj, w_out=v_w_out,
             final_norm_g=v_final_norm_g)
    n = x.shape[0]
    ax, ay, ac = lax.axis_index("x"), lax.axis_index("y"), lax.axis_index("c")
    q = 2 * ax + ay
    dev = 4 * ax + 2 * ay + ac
    nsh = ada_w.shape[2]

    w_sh = w_in[0].astype(BF16)
    p_sh = jnp.concatenate([conv_proj[0], gla_proj[0], w_out[0]], 0).astype(BF16)
    fp = _pack([conv_w[0], decay_up_fwd[0], decay_up_bwd[0]], F_ROWS)
    c8 = jnp.pad(c, ((0, 8 - n), (0, 0)))
    cctx8 = jnp.pad(c_ctx[None], ((0, 7), (0, 0)))
    adab_sh = lax.dynamic_slice(ada_b, (0, q * nsh), (1, nsh))
    w_all, fall, call, mall = gather_weights(c8, cctx8, ada_w[0], adab_sh, w_sh, fp)

    mod_all = jnp.transpose(mall, (1, 0, 2)).reshape(MOD_ROWS, 3 * D)
    mod_mine = lax.dynamic_slice(mod_all, (8 * dev, 0), (n, 3 * D))
    mod_ctx = mod_all[64:65]
    shift = jnp.concatenate([mod_mine[:, 0:D], mod_ctx[:, 0:D]], 0)[:, None, :]
    scale1 = 1.0 + jnp.concatenate([mod_mine[:, D:2 * D], mod_ctx[:, D:2 * D]], 0)[:, None, :]
    gate = mod_mine[:, 2 * D:3 * D][:, None, :]

    own = lambda i, mine, got: jnp.where(q == i, mine, got)
    g1, g2, g3, g4, g5 = _group_cols(jnp.concatenate([own(i, w_sh, w_all[i]) for i in range(N_CHIPS)], 1))
    wts = dict(w1=g1, w2=g2, w3=g3, w4=g4, w5=g5)
    f_parts = [_unpack(fall[i], [conv_w.shape[1:], decay_up_fwd.shape[1:], decay_up_bwd.shape[1:]]) for i in range(N_CHIPS)]
    conv_w_full = jnp.concatenate([p[0] for p in f_parts], 1)
    upf_full = jnp.concatenate([p[1] for p in f_parts], 1)
    upb_full = jnp.concatenate([p[2] for p in f_parts], 1)
    b1, b2, b3, b4, b5 = _group_cols(b_in)
    small = dict(b1=b1, b2=b2, b3=b3, b4=b4, b5=b5, norm_g=norm_g,
                 conv_w=jnp.pad(conv_w_full, ((0, 1), (0, 0))), conv_b=conv_b, conv_ln_g=conv_ln_g, conv_ln_b=conv_ln_b,
                 upf=_split3(_pad_up(upf_full, 0)), upb=_split3(_pad_up(upb_full, 16)),
                 bias_f=decay_bias_fwd, bias_b=decay_bias_bwd,
                 gla_norm_g=gla_norm_g, final_norm_g=final_norm_g[None])

    core = ac.astype(jnp.int32).reshape(1)
    chip = q.astype(jnp.int32).reshape(1)
    loss_part, dh, dps, g, got, (pap16, rbp), (sall1, early_shapes) = local_step(
        x, ctx, loss_target, (scale1, shift, gate), wts, small, p_sh, q, core)
    loss = lax.psum(loss_part[0, 0], ("x", "y", "c"))

    gs = [g["w%d" % i] for i in range(1, 6)]
    got[2] = pair_swap([gs[2]])[0]
    halves = pair_add_groups(core, gs, [got[i] for i in range(5)], tr=128)
    paw16 = _ungroup_to_shards(halves)
    grad_x2, dshift, dscale, g["norm_g"], rbw = dgrad_norm_bwd(
        dps, [wts["w%d" % i] for i in range(1, 6)], paw16, x.reshape(n * SEQ, D), ctx.reshape(n * NCTX, D), dh,
        scale1, norm_g, tm=256)

    dm_mine = jnp.concatenate([dshift[:n, 0], dscale[:n, 0], g["gate"][:, 0]], -1)
    dm_ctx = jnp.concatenate([dshift[n, 0], dscale[n, 0], jnp.zeros((D,), F32)], -1)
    d_b_in = _ungroup_cols(*[g["b%d" % i] for i in range(1, 6)])
    late = [g["norm_g"], dm_mine, dm_ctx, d_b_in]
    late_shapes = [a.shape for a in late]
    sall2 = gather_small(_pack(late))
    (s_conv_b, s_ln_g, s_ln_b, s_bias_f, s_bias_b, s_gla_g, s_final_g, s_conv_w, s_upf,
     s_upb) = _unpack(sum_devices(sall1, name="sum_devices_early"), early_shapes)
    s_late = _unpack(sum_devices(sall2, name="sum_devices_late"), late_shapes)
    s_norm_g, s_b_in = s_late[0], s_late[3]
    r_mine, r_ctx = 1, 1 + 3 * n
    dm_all = sall2[:, r_mine:r_ctx].reshape(N_DEV, n, 3 * D)
    dm_full = jnp.concatenate([jnp.pad(dm_all, ((0, 0), (0, 8 - n), (0, 0))).reshape(8 * N_DEV, 3 * D),
                               sall2[:, r_ctx:r_ctx + 3].reshape(N_DEV, 3 * D)], 0)
    dm_shard = lax.dynamic_slice(dm_full, (0, q * nsh), (MOD_ROWS, nsh))
    cctx_rows = jnp.broadcast_to(c_ctx[None], (8, D))
    g_ada_w, g_ada_b, pq = ada_bwd(call, cctx_rows, dm_shard, dm_full, ada_w[0])

    place = jnp.concatenate([chip, core])
    ghw = chip_add(place, paw16, rbw, name="chip_add_w", tr=128)
    ghp = chip_add(place, pap16, rbp, name="chip_add_p", tr=384)
    gw_mine, gp_mine, pq_all = pair_share(ghw, ghp, pq)
    gp_mine = gp_mine.reshape(768, D)
    g_c_ctx = cctx_grad(pq_all, cctx_rows)[0]

    grads = dict(
        c_ctx=g_c_ctx, ada_w=g_ada_w[None], ada_b=g_ada_b, norm_g=s_norm_g,
        w_in=gw_mine.reshape(1, D, W_IN_SHARD), b_in=s_b_in,
        conv_w=lax.dynamic_slice(s_conv_w, (0, q * 256), (CONV_K, 256))[None], conv_b=s_conv_b,
        conv_ln_g=s_ln_g, conv_ln_b=s_ln_b, conv_proj=gp_mine[0:256][None],
        decay_up_fwd=lax.dynamic_slice(s_upf, (0, q * 128), (16, 128))[None], decay_bias_fwd=s_bias_f,
        decay_up_bwd=lax.dynamic_slice(s_upb, (0, q * 128), (16, 128))[None], decay_bias_bwd=s_bias_b,
        gla_norm_g=s_gla_g, gla_proj=gp_mine[256:512][None], w_out=gp_mine[512:768][None],
        final_norm_g=s_final_g[0])

    delta, new_m, new_v = {}, {}, {}
    for name in ["ada_w", "conv_proj", "gla_proj", "w_out"]:
        delta[name], new_m[name], new_v[name] = adamw2d(w[name], grads[name].reshape(w[name].shape), m[name], v[name],
                                                        name="adamw_" + name, tr=128)
    tr_ = lambda a: jnp.swapaxes(a, 1, 2)
    g_w_in_t = tr_(grads["w_in"])
    grads["w_in"] = tr_(g_w_in_t)
    d_, m_, v_ = adamw2d(tr_(w_in), g_w_in_t, tr_(m_w_in), tr_(v_w_in), name="adamw_w_in", tr=W_IN_SHARD, tcols=128)
    delta["w_in"], new_m["w_in"], new_v["w_in"] = tr_(d_), tr_(m_), tr_(v_)
    d_, m_, v_ = adamw_many([w[nm] for nm in SMALL_NAMES], [grads[nm].reshape(w[nm].shape) for nm in SMALL_NAMES],
                            [m[nm] for nm in SMALL_NAMES], [v[nm] for nm in SMALL_NAMES])
    for nm, a, b, cc in zip(SMALL_NAMES, d_, m_, v_):
        delta[nm], new_m[nm], new_v[nm] = a, b, cc

    grad_x = grad_x2.reshape(x.shape)
    return (loss, grad_x, *[grads[nm].reshape(w[nm].shape) for nm in WEIGHT_NAMES], *[delta[nm] for nm in WEIGHT_NAMES],
            *[new_m[nm] for nm in WEIGHT_NAMES], *[new_v[nm] for nm in WEIGHT_NAMES])
```

```python
import jax
import jax.numpy as jnp
from jax import lax
from jax.experimental import pallas as pl
from jax.experimental.pallas import tpu as pltpu

F32 = jnp.float32
BF16 = jnp.bfloat16
MESH = pl.DeviceIdType.MESH
HI = lax.Precision.HIGHEST

D = 1024
SEQ = 2048
GRID_W = 64
GRID_H = SEQ // GRID_W
NCTX = 256
SEQ_ALL = SEQ + NCTX
EPS = 1e-6
CONV_K = 31
CONV_PAD = CONV_K // 2
HEADS = 4
HEAD_K = 128
HEAD_V = 256
GLA_DK = HEADS * HEAD_K
GATE_TAU = 16.0
Q_SCALE = HEAD_K ** -0.5
CHUNK = 64
NCHUNK = SEQ_ALL // CHUNK
NCHUNK_LAT = SEQ // CHUNK
NCHUNK_CTX = NCHUNK - NCHUNK_LAT
SUB = 64
NSUB = CHUNK // SUB
N_IN = 8224
W3 = 2176
O3_V, O3_Q, O3_K, O3_AB = 0, 1024, 1536, 2048

ADAM_LR, ADAM_B1, ADAM_B2, ADAM_EPS, ADAM_WD, ADAM_STEP = 0.001, 0.9, 0.999, 1e-08, 0.01, 10
VMEM_LIMIT = 56 * 1024 * 1024

N_CHIPS = 4
N_DEV = 8
W_IN_SHARD = N_IN // N_CHIPS
MOD_ROWS = 72


def _pallas(body, **kw):
    return pl.pallas_call(body, **kw)


def _params(sem=None, **kw):
    if sem is not None:
        kw["dimension_semantics"] = sem
    return pltpu.CompilerParams(vmem_limit_bytes=VMEM_LIMIT, **kw)


def _sigmoid(v):
    return 1.0 / (1.0 + jnp.exp(-v))


def _silu(v):
    return v * _sigmoid(v)


def _dsilu(v):
    s = _sigmoid(v)
    return s * (1.0 + v * (1.0 - s))


def _log_sigmoid(v):
    return jnp.minimum(v, 0.0) - jnp.log(1.0 + jnp.exp(-jnp.abs(v)))


def _dot(a, b, dims, precision=None):
    return lax.dot_general(a, b, (dims, ((), ())), preferred_element_type=F32, precision=precision)


def _nn(a, b, precision=None):
    return _dot(a, b, ((1,), (0,)), precision)


def _nt(a, b, precision=None):
    return _dot(a, b, ((1,), (1,)), precision)


def _tn(a, b, precision=None):
    return _dot(a, b, ((0,), (0,)), precision)


def _b16(v):
    return v.astype(BF16)


def proj_all(u, ws, bs, rows, p_sh, *, tm):
    k = u.shape[1]
    n_g = len(ws)
    tns = [w.shape[1] if w.shape[1] % 1024 else 1024 for w in ws]
    mts = [r // tm for r in rows]
    cnts = [(w.shape[1] // tn) * mt for w, tn, mt in zip(ws, tns, mts)]
    los = [sum(cnts[:g]) for g in range(n_g)]
    n_steps = sum(cnts)

    def rel(s, g):
        return jnp.clip(s - los[g], 0, cnts[g] - 1)

    def active(s, g):
        return (s >= los[g]) & (s < los[g] + cnts[g])

    def u_row(s):
        r = 0
        for g in range(n_g):
            r = r + jnp.where(active(s, g), rel(s, g) % mts[g], 0)
        return r

    def body(*refs):
        u_ref = refs[0]
        w_refs, b_refs = refs[1:1 + n_g], refs[1 + n_g:1 + 2 * n_g]
        p_ref = refs[1 + 2 * n_g]
        o_refs = refs[2 + 2 * n_g:2 + 3 * n_g]
        pall_ref = refs[2 + 3 * n_g]
        w_send, w_recv, h_send, h_recv = refs[3 + 3 * n_g:]
        s = pl.program_id(0)
        for g in range(n_g):
            @pl.when(active(s, g))
            def _(g=g):
                o_refs[g][...] = (_nn(u_ref[...], w_refs[g][...]) + b_refs[g][...]).astype(o_refs[g].dtype)

        x, y, c, chips = _place()
        q = 2 * x + y
        mine = _half_chunks(c, (0, p_ref.shape[0]), 16, which=(1,))
        other = _half_chunks(1 - c, (0, p_ref.shape[0]), 16, which=(1,))
        nb = len(mine)

        def bulk():
            return [[_remote(p_ref.at[rws], pall_ref.at[q, rws], w_send.at[pj * nb + pi], w_recv.at[pj * nb + pi],
                             (*chips[pj], c)) for pi, (_, rws) in enumerate(mine)] for pj in range(3)]

        @pl.when(s == 0)
        def _():
            for cp in sum(bulk(), []):
                cp.start()

        @pl.when(s == n_steps - 1)
        def _():
            handed = []
            for pj, (cx, cy) in enumerate(chips):
                for pi, (_, rws) in enumerate(mine):
                    bulk()[pj][pi].wait_recv()
                    cp = _remote(pall_ref.at[2 * cx + cy, rws], pall_ref.at[2 * cx + cy, rws],
                                 h_send.at[pj * nb + pi], h_recv.at[pj * nb + pi], (x, y, 1 - c))
                    cp.start()
                    handed.append(cp)
            for pj, (cx, cy) in enumerate(chips):
                for pi, (_, rws) in enumerate(other):
                    _remote(pall_ref.at[2 * cx + cy, rws], pall_ref.at[2 * cx + cy, rws],
                            h_send.at[pj * nb + pi], h_recv.at[pj * nb + pi], (x, y, 1 - c)).wait_recv()
            for cp in sum(bulk(), []) + handed:
                cp.wait_send()

    any_spec = pl.BlockSpec(memory_space=pl.ANY)
    in_specs = [pl.BlockSpec((tm, k), lambda s: (u_row(s), 0))]
    in_specs += [pl.BlockSpec((k, tns[g]), lambda s, g=g: (0, rel(s, g) // mts[g])) for g in range(n_g)]
    in_specs += [pl.BlockSpec((1, tns[g]), lambda s, g=g: (0, rel(s, g) // mts[g])) for g in range(n_g)]
    in_specs.append(any_spec)
    out_specs = [pl.BlockSpec((tm, tns[g]), lambda s, g=g: (rel(s, g) % mts[g], rel(s, g) // mts[g])) for g in range(n_g)]
    out_specs.append(any_spec)
    out_shape = [jax.ShapeDtypeStruct((rows[g], ws[g].shape[1]), BF16) for g in range(n_g)]
    out_shape.append(jax.ShapeDtypeStruct((N_CHIPS,) + p_sh.shape, p_sh.dtype))
    return _pallas(
        body, name="proj_all", grid=(n_steps,), in_specs=in_specs, out_specs=out_specs, out_shape=out_shape,
        scratch_shapes=[pltpu.SemaphoreType.DMA((3 * P_ROW_CHUNKS,)) for _ in range(4)],
        compiler_params=_params(("arbitrary",)),
    )(u, *ws, *bs, p_sh)


def matmul_tn(a, b, *, name, t, tn, tt, colsum=False, swap=None):
    m = a.shape[1]
    n = b.shape[1]
    nj, ns = n // tn, t // tt
    n_out = 2 if colsum else 1
    n_sw = 0 if swap is None else len(swap[0])
    n_ex = 0 if swap is None else n_sw + 1

    def body(a_ref, b_ref, *rest):
        o_ref = rest[n_ex]
        cs_ref = rest[n_ex + 1] if colsum else None
        j, s = pl.program_id(0), pl.program_id(1)

        if swap is not None:
            g_refs, sm_ref = rest[:n_sw], rest[n_sw]
            got_refs, sall_ref = rest[n_ex + n_out:n_ex + n_out + n_sw], rest[n_ex + n_out + n_sw]
            a_send, a_recv, s_send, s_recv, l_sem = rest[2 * n_ex + n_out:]

            def copies():
                x, y, c, _ = _place()
                dev = 4 * x + 2 * y + c
                small = [_remote(sm_ref, sall_ref.at[dev], s_send.at[r], s_recv.at[r], peer)
                         for r, peer in enumerate(_all_peers(x, y, c))]
                return (_pair_copies(g_refs, None, got_refs, None, a_send, a_recv) + small,
                        pltpu.make_async_copy(sm_ref, sall_ref.at[dev], l_sem))

            @pl.when((j == 0) & (s == 0))
            def _():
                remote, own = copies()
                own.start()
                for cp in remote:
                    cp.start()

            @pl.when((j == nj - 1) & (s == ns - 1))
            def _():
                remote, own = copies()
                for cp in remote:
                    cp.wait_recv()
                for cp in remote:
                    cp.wait_send()
                own.wait()

        @pl.when(s == 0)
        def _():
            o_ref[...] = jnp.zeros_like(o_ref)
            if colsum:
                cs_ref[...] = jnp.zeros_like(cs_ref)
        o_ref[...] += _tn(a_ref[...], b_ref[...])
        if colsum:
            cs_ref[...] += jnp.sum(b_ref[...].astype(F32), axis=0, keepdims=True)

    in_specs = [pl.BlockSpec((tt, m), lambda j, s: (s, 0)), pl.BlockSpec((tt, tn), lambda j, s: (s, j))]
    out_specs = [pl.BlockSpec((m, tn), lambda j, s: (0, j))]
    out_shape = [jax.ShapeDtypeStruct((m, n), F32)]
    if colsum:
        out_specs.append(pl.BlockSpec((1, tn), lambda j, s: (0, j)))
        out_shape.append(jax.ShapeDtypeStruct((1, n), F32))
    args, scratch = [a, b], []
    if swap is not None:
        gs, sm = swap
        any_spec = pl.BlockSpec(memory_space=pl.ANY)
        in_specs += [any_spec] * n_ex
        out_specs += [any_spec] * n_ex
        out_shape += _pair_got_shapes(gs, None) + [jax.ShapeDtypeStruct((N_DEV,) + sm.shape, F32)]
        args += [*gs, sm]
        scratch = [pltpu.SemaphoreType.DMA((_pair_count(gs, None),)), pltpu.SemaphoreType.DMA((_pair_count(gs, None),)),
                   pltpu.SemaphoreType.DMA((N_DEV - 1,)), pltpu.SemaphoreType.DMA((N_DEV - 1,)),
                   pltpu.SemaphoreType.DMA(())]
    return _pallas(
        body, name=name, grid=(nj, ns), in_specs=in_specs, out_specs=out_specs, out_shape=out_shape,
        scratch_shapes=scratch,
        compiler_params=_params(("parallel" if swap is None else "arbitrary", "arbitrary")),
    )(*args)


def dgrad_norm_bwd(dps, wts, paw, x2, ctx2, dh, scale1, norm_g, *, tm):
    t, tc = x2.shape[0], ctx2.shape[0]
    t_all = t + tc
    n_lat, n_ctx = t // tm, tc // tm
    n_tiles = n_lat + n_ctx
    n_samples = scale1.shape[0] - 1
    tps = n_lat // n_samples
    n_grp = n_samples + 1
    n_g = len(dps)
    whole = [g for g in range(n_g) if dps[g].shape[0] == t_all]
    latent = [g for g in range(n_g) if dps[g].shape[0] != t_all]

    def body(*refs):
        dp_refs, w_refs = refs[:n_g], refs[n_g:2 * n_g]
        (paw_ref, x_ref, c_ref, dh_ref, sc_ref, g_ref, dx_ref, dsh_ref, dsc_ref, dg_ref, rbw_ref,
         du_buf, b_send, b_recv) = refs[2 * n_g:]
        i = pl.program_id(0)

        def exchange():
            x, y, c, chips = _place()
            chunks = _half_chunks(0, (2 * paw_ref.shape[1],), 16, which=(0,))
            return [_remote(paw_ref.at[2 * cx + cy, rows], rbw_ref.at[j, rows],
                            b_send.at[j * N_BULK + k], b_recv.at[j * N_BULK + k], (cx, cy, c))
                    for j, (cx, cy) in enumerate(chips) for k, (_, rows) in enumerate(chunks)]

        @pl.when(i == 0)
        def _():
            for cp in exchange():
                cp.start()

        acc = None
        for g in whole:
            part = _nt(dp_refs[g][...], w_refs[g][...])
            acc = part if acc is None else acc + part
        du_buf[...] = acc

        @pl.when(i < n_lat)
        def _():
            lat = None
            for g in latent:
                part = _nt(dp_refs[g][...], w_refs[g][...])
                lat = part if lat is None else lat + part
            du_buf[...] += lat

        duv = du_buf[...]
        xv = jnp.where(i < n_lat, x_ref[...], c_ref[...])
        rs = lax.rsqrt(jnp.mean(xv * xv, axis=-1, keepdims=True) + EPS)
        xh = xv * rs
        n = xh * g_ref[...]
        dn = duv * sc_ref[0]
        dxh = dn * g_ref[...]
        dx = rs * (dxh - xh * jnp.mean(dxh * xh, axis=-1, keepdims=True))

        @pl.when(i < n_lat)
        def _():
            dx_ref[...] = dx + dh_ref[...]

        @pl.when((i % tps == 0) & (i <= n_lat))
        def _():
            dsh_ref[...] = jnp.zeros_like(dsh_ref)
            dsc_ref[...] = jnp.zeros_like(dsc_ref)

        @pl.when(i == 0)
        def _():
            dg_ref[...] = jnp.zeros_like(dg_ref)

        dsh_ref[0] += jnp.sum(duv, axis=0, keepdims=True)
        dsc_ref[0] += jnp.sum(duv * n, axis=0, keepdims=True)
        dg_ref[...] += jnp.sum(dn * xh, axis=0, keepdims=True)

        @pl.when(i == n_tiles - 1)
        def _():
            for cp in exchange():
                cp.wait_recv()
            for cp in exchange():
                cp.wait_send()

    lat = lambda i: (jnp.minimum(i, n_lat - 1), 0)
    grp = lambda i: (jnp.minimum(i // tps, n_samples), 0, 0)
    in_specs = []
    for g, dp in enumerate(dps):
        nrow = dp.shape[0] // tm
        in_specs.append(pl.BlockSpec((tm, dp.shape[1]), lambda i, nrow=nrow: (jnp.minimum(i, nrow - 1), 0)))
    for w in wts:
        in_specs.append(pl.BlockSpec(w.shape, lambda i: (0, 0), pipeline_mode=pl.Buffered(1)))
    any_spec = pl.BlockSpec(memory_space=pl.ANY)
    in_specs += [any_spec,
                 pl.BlockSpec((tm, D), lat), pl.BlockSpec((tm, D), lambda i: (jnp.maximum(i - n_lat, 0), 0)),
                 pl.BlockSpec((tm, D), lat), pl.BlockSpec((1, 1, D), grp), pl.BlockSpec((1, D), lambda i: (0, 0))]
    return _pallas(
        body, name="dgrad_norm_bwd", grid=(n_tiles,), in_specs=in_specs,
        out_specs=[pl.BlockSpec((tm, D), lat), pl.BlockSpec((1, 1, D), grp), pl.BlockSpec((1, 1, D), grp),
                   pl.BlockSpec((1, D), lambda i: (0, 0)), any_spec],
        out_shape=[jax.ShapeDtypeStruct((t, D), F32), jax.ShapeDtypeStruct((n_grp, 1, D), F32),
                   jax.ShapeDtypeStruct((n_grp, 1, D), F32), jax.ShapeDtypeStruct((1, D), F32),
                   jax.ShapeDtypeStruct((3,) + paw.shape[1:], paw.dtype)],
        scratch_shapes=[pltpu.VMEM((tm, D), F32), pltpu.SemaphoreType.DMA((3 * N_BULK,)),
                        pltpu.SemaphoreType.DMA((3 * N_BULK,))],
        compiler_params=_params(("arbitrary",)),
    )(*dps, *wts, paw, x2, ctx2, dh, scale1, norm_g)


TM_NORM = 512


def norm_mod_fwd(x2, ctx2, scale1, shift, norm_g):
    t = x2.shape[0]
    n_lat = t // TM_NORM
    assert ctx2.shape[0] == TM_NORM
    n_samples = scale1.shape[0] - 1
    tps = n_lat // n_samples

    def body(x_ref, c_ref, sc_ref, sh_ref, g_ref, u_ref):
        i = pl.program_id(0)
        xv = jnp.where(i < n_lat, x_ref[...], c_ref[...])
        rs = lax.rsqrt(jnp.mean(xv * xv, axis=-1, keepdims=True) + EPS)
        u = xv * rs * g_ref[...] * sc_ref[0] + sh_ref[0]
        u_ref[...] = u.astype(u_ref.dtype)

    grp = lambda i: (jnp.minimum(i // tps, n_samples), 0, 0)
    return _pallas(
        body, name="norm_mod_fwd", grid=(n_lat + 1,),
        in_specs=[pl.BlockSpec((TM_NORM, D), lambda i: (jnp.minimum(i, n_lat - 1), 0)),
                  pl.BlockSpec((TM_NORM, D), lambda i: (0, 0)),
                  pl.BlockSpec((1, 1, D), grp), pl.BlockSpec((1, 1, D), grp),
                  pl.BlockSpec((1, D), lambda i: (0, 0))],
        out_specs=pl.BlockSpec((TM_NORM, D), lambda i: (i, 0)),
        out_shape=jax.ShapeDtypeStruct((t + TM_NORM, D), BF16),
        compiler_params=_params(("parallel",)),
    )(x2, ctx2, scale1, shift, norm_g)


CONV_CB = 256
CONV_NCB = D // CONV_CB
H_OFF = 16


H_CB = 128
H_SPAN = GRID_W + 2 * H_OFF - 8


def _conv_scratch(vertical):
    if vertical:
        return [pltpu.VMEM((GRID_H + 2 * CONV_PAD, GRID_W, CONV_CB), F32)]
    return [pltpu.VMEM((GRID_H, GRID_W + 2 * H_OFF, H_CB), F32), pltpu.VMEM((7, GRID_H, H_SPAN, H_CB), F32)]


def _conv_fill(bufs, img, vertical):
    pad_ref = bufs[0]
    pad_ref[...] = jnp.zeros_like(pad_ref)
    if vertical:
        pad_ref[pl.ds(CONV_PAD, GRID_H)] = img
        return
    pad_ref[:, pl.ds(H_OFF, GRID_W), :] = img

    def shift(r, carry):
        for s in range(1, 8):
            bufs[1][s - 1, r] = pad_ref[r, pl.ds(s, H_SPAN), :]
        return carry

    lax.fori_loop(0, GRID_H, shift, 0)


def _conv_window(bufs, k, vertical, r, w0=0, nw=GRID_W, lanes=slice(None)):
    if vertical:
        return bufs[0][r + k, pl.ds(w0, nw), lanes]
    off = H_OFF - CONV_PAD + k
    if off % 8 == 0:
        return bufs[0][r, pl.ds(off + w0, nw), lanes]
    return bufs[1][off % 8 - 1, r, pl.ds(off - off % 8 + w0, nw), lanes]


def _conv_col_blocks(vertical):
    if vertical:
        return [pl.ds(0, CONV_CB)]
    return [pl.ds(i * H_CB, H_CB) for i in range(CONV_CB // H_CB)]


def _rows(r):
    return pl.ds(pl.multiple_of(r * GRID_W, GRID_W), GRID_W)


def conv_fwd(p1, conv_w, conv_b, n_samples):
    t = n_samples * SEQ

    def make(vertical, prev):
        n_buf = len(_conv_scratch(vertical))

        def body(gv_ref, gg_ref, w_ref, b_ref, *rest):
            o_ref, bufs = rest[-1 - n_buf], rest[-n_buf:]
            for cols in _conv_col_blocks(vertical):
                a = gv_ref[:, cols].astype(F32) * _sigmoid(gg_ref[:, cols].astype(F32))
                _conv_fill(bufs, a.reshape(GRID_H, GRID_W, a.shape[-1]), vertical)

                def row(r, carry, cols=cols):
                    acc = jnp.zeros((GRID_W, cols.size), F32) + b_ref[:, cols]
                    for k in range(CONV_K):
                        acc = acc + _conv_window(bufs, k, vertical, r) * w_ref[pl.ds(k, 1), cols]
                    o_ref[_rows(r), cols] = acc
                    return carry

                lax.fori_loop(0, GRID_H, row, 0)

        cb0 = CONV_NCB // 2 if vertical else 0
        in_specs = [pl.BlockSpec((SEQ, CONV_CB), lambda b, j: (b, 2 * (cb0 + j))),
                    pl.BlockSpec((SEQ, CONV_CB), lambda b, j: (b, 2 * (cb0 + j) + 1)),
                    pl.BlockSpec((CONV_K + 1, CONV_CB), lambda b, j: (0, cb0 + j)),
                    pl.BlockSpec((1, CONV_CB), lambda b, j: (0, cb0 + j))]
        args = [p1, p1, conv_w, conv_b]
        aliases = {}
        if prev is not None:
            in_specs.append(pl.BlockSpec(memory_space=pl.ANY))
            args.append(prev)
            aliases = {4: 0}
        return _pallas(
            body, name="conv_fwd_v" if vertical else "conv_fwd_h", grid=(n_samples, CONV_NCB // 2),
            in_specs=in_specs,
            out_specs=pl.BlockSpec((SEQ, CONV_CB), lambda b, j: (b, cb0 + j)),
            out_shape=jax.ShapeDtypeStruct((t, D), F32),
            scratch_shapes=_conv_scratch(vertical),
            input_output_aliases=aliases,
            compiler_params=_params(("parallel", "parallel")),
        )(*args)

    return make(True, make(False, None))


def conv_bwd(p1, daconv, conv_w, n_samples):
    t = n_samples * SEQ

    def make(vertical, prev):
        n_buf = len(_conv_scratch(vertical))

        def body(gv_ref, gg_ref, dy_ref, w_ref, *rest):
            dp_ref, dw_ref, db_ref = rest[-3 - 2 * n_buf - 1:-2 * n_buf - 1]
            a_bufs, d_bufs, da_ref = rest[-2 * n_buf - 1:-n_buf - 1], rest[-n_buf - 1:-1], rest[-1]
            for cols in _conv_col_blocks(vertical):
                width = cols.size
                gv = gv_ref[:, cols].astype(F32)
                sg = _sigmoid(gg_ref[:, cols].astype(F32))
                _conv_fill(a_bufs, (gv * sg).reshape(GRID_H, GRID_W, width), vertical)
                _conv_fill(d_bufs, dy_ref[:, cols].reshape(GRID_H, GRID_W, width), vertical)

                def row(r, carry, cols=cols, width=width):
                    acc = jnp.zeros((GRID_W, width), F32)
                    for k in range(CONV_K):
                        acc = acc + _conv_window(d_bufs, CONV_K - 1 - k, vertical, r) * w_ref[pl.ds(k, 1), cols]
                    da_ref[_rows(r), cols] = acc
                    return carry

                lax.fori_loop(0, GRID_H, row, 0)
                da = da_ref[:, cols]
                dp_ref[:, pl.ds(cols.start, width)] = (da * sg).astype(dp_ref.dtype)
                dp_ref[:, pl.ds(CONV_CB + cols.start, width)] = (da * gv * sg * (1.0 - sg)).astype(dp_ref.dtype)

                for lb in range(width // 128):
                    lanes = pl.ds(lb * 128, 128)
                    dy_lanes = pl.ds(cols.start + lb * 128, 128)

                    def wrow(r, accs, lanes=lanes, dy_lanes=dy_lanes):
                        for w0 in range(0, GRID_W, 8):
                            dyv = dy_ref[pl.ds(pl.multiple_of(r * GRID_W, GRID_W) + w0, 8), dy_lanes]
                            accs = tuple(accs[k] + _conv_window(a_bufs, k, vertical, r, w0, 8, lanes) * dyv
                                         for k in range(CONV_K))
                        return accs

                    accs = lax.fori_loop(0, GRID_H, wrow, tuple(jnp.zeros((8, 128), F32) for _ in range(CONV_K)))
                    for k in range(CONV_K):
                        dw_ref[0, pl.ds(k, 1), dy_lanes] = jnp.sum(accs[k], axis=0, keepdims=True)
            dw_ref[0, pl.ds(CONV_K, 1), :] = jnp.zeros((1, CONV_CB), F32)
            db_ref[0] = jnp.sum(dy_ref[...], axis=0, keepdims=True)

        cb0 = CONV_NCB // 2 if vertical else 0
        in_specs = [pl.BlockSpec((SEQ, CONV_CB), lambda b, j: (b, 2 * (cb0 + j))),
                    pl.BlockSpec((SEQ, CONV_CB), lambda b, j: (b, 2 * (cb0 + j) + 1)),
                    pl.BlockSpec((SEQ, CONV_CB), lambda b, j: (b, cb0 + j)),
                    pl.BlockSpec((CONV_K + 1, CONV_CB), lambda b, j: (0, cb0 + j))]
        args = [p1, p1, daconv, conv_w]
        aliases = {}
        if prev is not None:
            in_specs += [pl.BlockSpec(memory_space=pl.ANY)] * 3
            args += list(prev)
            aliases = {4: 0, 5: 1, 6: 2}
        return _pallas(
            body, name="conv_bwd_v" if vertical else "conv_bwd_h", grid=(n_samples, CONV_NCB // 2),
            in_specs=in_specs,
            out_specs=[pl.BlockSpec((SEQ, 2 * CONV_CB), lambda b, j: (b, cb0 + j)),
                       pl.BlockSpec((1, CONV_K + 1, CONV_CB), lambda b, j: (b, 0, cb0 + j)),
                       pl.BlockSpec((1, 1, CONV_CB), lambda b, j: (b, 0, cb0 + j))],
            out_shape=[jax.ShapeDtypeStruct((t, 2 * D), BF16),
                       jax.ShapeDtypeStruct((n_samples, CONV_K + 1, D), F32),
                       jax.ShapeDtypeStruct((n_samples, 1, D), F32)],
            scratch_shapes=_conv_scratch(vertical) + _conv_scratch(vertical) + [pltpu.VMEM((SEQ, CONV_CB), F32)],
            input_output_aliases=aliases,
            compiler_params=_params(("parallel", "parallel")),
        )(*args)

    return make(True, make(False, None))


TM_EW = 512


def ln_gate_proj(aconv, z, ln_g, ln_b, conv_proj):
    t = aconv.shape[0]

    def body(a_ref, z_ref, g_ref, b_ref, w_ref, o_ref, y_ref):
        a = a_ref[...]
        mu = jnp.mean(a, axis=-1, keepdims=True)
        xc = a - mu
        rstd = lax.rsqrt(jnp.mean(xc * xc, axis=-1, keepdims=True) + EPS)
        l = xc * rstd * g_ref[...] + b_ref[...]
        ac = _b16(_silu(l) * _silu(z_ref[...].astype(F32)))
        o_ref[...] = ac
        y_ref[...] = _nn(ac, w_ref[...]).astype(y_ref.dtype)

    row = pl.BlockSpec((TM_OUT, D), lambda i: (i, 0))
    vec = pl.BlockSpec((1, D), lambda i: (0, 0))
    return _pallas(
        body, name="ln_gate_proj", grid=(t // TM_OUT,),
        in_specs=[row, row, vec, vec, pl.BlockSpec((D, D), lambda i: (0, 0))], out_specs=[row, row],
        out_shape=[jax.ShapeDtypeStruct((t, D), BF16), jax.ShapeDtypeStruct((t, D), BF16)],
        compiler_params=_params(("parallel",)),
    )(aconv, z, ln_g, ln_b, conv_proj)


def ln_gate_bwd(aconv, z, dyc, conv_proj, ln_g, ln_b, ac):
    t = aconv.shape[0]

    def body(a_ref, z_ref, d_ref, w_ref, g_ref, b_ref, ac_ref, da_ref, dz_ref, dg_ref, db_ref, gw_ref):
        @pl.when(pl.program_id(0) == 0)
        def _():
            gw_ref[...] = jnp.zeros_like(gw_ref)

        gw_ref[...] += _tn(ac_ref[...], d_ref[...])
        a = a_ref[...]
        zv = z_ref[...].astype(F32)
        dac_v = _nt(d_ref[...], w_ref[...])
        mu = jnp.mean(a, axis=-1, keepdims=True)
        xc = a - mu
        rstd = lax.rsqrt(jnp.mean(xc * xc, axis=-1, keepdims=True) + EPS)
        xh = xc * rstd
        l = xh * g_ref[...] + b_ref[...]
        dz_ref[...] = (dac_v * _silu(l) * _dsilu(zv)).astype(dz_ref.dtype)
        dl = dac_v * _silu(zv) * _dsilu(l)
        dxh = dl * g_ref[...]
        da_ref[...] = rstd * (dxh - jnp.mean(dxh, axis=-1, keepdims=True)
                              - xh * jnp.mean(dxh * xh, axis=-1, keepdims=True))

        @pl.when(pl.program_id(0) == 0)
        def _():
            dg_ref[...] = jnp.zeros_like(dg_ref)
            db_ref[...] = jnp.zeros_like(db_ref)

        dg_ref[...] += jnp.sum(dl * xh, axis=0, keepdims=True)
        db_ref[...] += jnp.sum(dl, axis=0, keepdims=True)

    row = pl.BlockSpec((TM_EW, D), lambda i: (i, 0))
    vec = pl.BlockSpec((1, D), lambda i: (0, 0))
    return _pallas(
        body, name="ln_gate_bwd", grid=(t // TM_EW,),
        in_specs=[row, row, row, pl.BlockSpec((D, D), lambda i: (0, 0)), vec, vec, row],
        out_specs=[row, row, vec, vec, pl.BlockSpec((D, D), lambda i: (0, 0))],
        out_shape=[jax.ShapeDtypeStruct((t, D), F32), jax.ShapeDtypeStruct((t, D), BF16),
                   jax.ShapeDtypeStruct((1, D), F32), jax.ShapeDtypeStruct((1, D), F32),
                   jax.ShapeDtypeStruct((D, D), F32)],
        compiler_params=_params(("arbitrary",)),
    )(aconv, z, dyc, conv_proj, ln_g, ln_b, ac)


TM_PREP = 256
PREP_LAT = SEQ // TM_PREP
PREP_ALL = SEQ_ALL // TM_PREP


def _chunk_tri(n, upper):
    r = lax.broadcasted_iota(jnp.int32, (n, n), 0)
    c = lax.broadcasted_iota(jnp.int32, (n, n), 1)
    same = (r // CHUNK) == (c // CHUNK)
    keep = (c >= r) if upper else (c <= r)
    return jnp.where(same & keep, 1.0, 0.0).astype(F32)


def _split3(v):
    hi = v.astype(BF16)
    r1 = v - hi.astype(F32)
    mid = r1.astype(BF16)
    lo = (r1 - mid.astype(F32)).astype(BF16)
    return jnp.stack([hi, mid, lo])


def _chunk_sums(v, upper):
    tri = _chunk_tri(v.shape[0], upper).astype(BF16)
    pieces = _split3(v)
    return (_nn(tri, pieces[0]) + _nn(tri, pieces[1])) + _nn(tri, pieces[2])


def _gate_logits(ab, up3_ref, bias_ref):
    assert ab.dtype == BF16
    return ((_nn(ab, up3_ref[0]) + _nn(ab, up3_ref[1])) + _nn(ab, up3_ref[2])) + bias_ref[...]


def _prep_tile_maps(n_samples):
    n_lat = n_samples * PREP_LAT

    def seq_map(i):
        return jnp.where(i < n_lat, i // PREP_LAT, i - n_lat), jnp.where(i < n_lat, i % PREP_LAT, PREP_LAT)

    return n_lat, seq_map


def gla_prep_fwd(p3, upf, upb, bias_f, bias_b, n_samples):
    n_lat, seq_map = _prep_tile_maps(n_samples)
    n_tiles = n_lat + n_samples

    def body(v_ref, q_ref, k_ref, ab_ref, upf_ref, upb_ref, bf_ref, bb_ref, qo, ko, vo, cf, cb):
        i = pl.program_id(0)
        qo[0] = jnp.where(i < n_lat, q_ref[...].astype(F32) * Q_SCALE, 0.0)
        ko[0] = k_ref[...]
        vo[0] = v_ref[...]
        ab = ab_ref[...]
        gf = _log_sigmoid(_gate_logits(ab, upf_ref, bf_ref)) * (1.0 / GATE_TAU)
        gb = _log_sigmoid(_gate_logits(ab, upb_ref, bb_ref)) * (1.0 / GATE_TAU)
        cf[0] = _chunk_sums(gf, False)
        cb[0] = _chunk_sums(gb, True)

    def o_spec(w):
        return pl.BlockSpec((1, TM_PREP, w), lambda i: (*seq_map(i), 0))

    full = lambda shape: pl.BlockSpec(shape, lambda i: (0,) * len(shape))
    return _pallas(
        body, name="gla_prep_fwd", grid=(n_tiles,),
        in_specs=[pl.BlockSpec((TM_PREP, 1024), lambda i: (i, O3_V // 1024)),
                  pl.BlockSpec((TM_PREP, 512), lambda i: (i, O3_Q // 512)),
                  pl.BlockSpec((TM_PREP, 512), lambda i: (i, O3_K // 512)),
                  pl.BlockSpec((TM_PREP, 128), lambda i: (i, O3_AB // 128)),
                  full((3, 128, GLA_DK)), full((3, 128, GLA_DK)), full((1, GLA_DK)), full((1, GLA_DK))],
        out_specs=[o_spec(GLA_DK), o_spec(GLA_DK), o_spec(D), o_spec(GLA_DK), o_spec(GLA_DK)],
        out_shape=[jax.ShapeDtypeStruct((n_samples, SEQ_ALL, GLA_DK), F32),
                   jax.ShapeDtypeStruct((n_samples, SEQ_ALL, GLA_DK), p3.dtype),
                   jax.ShapeDtypeStruct((n_samples, SEQ_ALL, D), p3.dtype),
                   jax.ShapeDtypeStruct((n_samples, SEQ_ALL, GLA_DK), F32),
                   jax.ShapeDtypeStruct((n_samples, SEQ_ALL, GLA_DK), F32)],
        compiler_params=_params(("parallel",)),
    )(p3, p3, p3, p3, upf, upb, bias_f, bias_b)


def gla_prep_bwd(p3, dq_f, dq_b, dk_f, dk_b, dv_f, dv_b, dc_f, dc_b, upf, upb, bias_f, bias_b, n_samples):
    n_lat, seq_map = _prep_tile_maps(n_samples)
    n_tiles = n_lat + n_samples

    def body(ab_ref, dqf, dqb, dkf, dkb, dvf, dvb, dcf, dcb, upf_ref, upb_ref, bf_ref, bb_ref,
             dp_ref, duf_ref, dub_ref, dbf_ref, dbb_ref):
        i = pl.program_id(0)
        both = lambda a, b: a[0].astype(F32) + b[0].astype(F32)
        dp_ref[:, pl.ds(O3_V, D)] = both(dvf, dvb).astype(dp_ref.dtype)
        dq = jnp.where(i < n_lat, both(dqf, dqb) * Q_SCALE, 0.0)
        dp_ref[:, pl.ds(O3_Q, GLA_DK)] = dq.astype(dp_ref.dtype)
        dp_ref[:, pl.ds(O3_K, GLA_DK)] = both(dkf, dkb).astype(dp_ref.dtype)
        ab = ab_ref[...]
        zf = _gate_logits(ab, upf_ref, bf_ref)
        zb = _gate_logits(ab, upb_ref, bb_ref)
        dgf = _chunk_sums(dcf[0], True)
        dgb = _chunk_sums(dcb[0], False)
        dzf = _b16(dgf * (1.0 / GATE_TAU) * _sigmoid(-zf))
        dzb = _b16(dgb * (1.0 / GATE_TAU) * _sigmoid(-zb))
        dab = _nt(dzf, upf_ref[0]) + _nt(dzb, upb_ref[0])
        dp_ref[:, pl.ds(O3_AB, 128)] = dab.astype(dp_ref.dtype)

        @pl.when(i == 0)
        def _():
            duf_ref[...] = jnp.zeros_like(duf_ref)
            dub_ref[...] = jnp.zeros_like(dub_ref)
            dbf_ref[...] = jnp.zeros_like(dbf_ref)
            dbb_ref[...] = jnp.zeros_like(dbb_ref)

        duf_ref[...] += _tn(ab, dzf)
        dub_ref[...] += _tn(ab, dzb)
        dbf_ref[...] += jnp.sum(dzf.astype(F32), axis=0, keepdims=True)
        dbb_ref[...] += jnp.sum(dzb.astype(F32), axis=0, keepdims=True)

    def s_spec(w):
        return pl.BlockSpec((1, TM_PREP, w), lambda i: (*seq_map(i), 0))

    full = lambda shape: pl.BlockSpec(shape, lambda i: (0,) * len(shape))
    return _pallas(
        body, name="gla_prep_bwd", grid=(n_tiles,),
        in_specs=[pl.BlockSpec((TM_PREP, 128), lambda i: (i, O3_AB // 128)),
                  s_spec(GLA_DK), s_spec(GLA_DK), s_spec(GLA_DK), s_spec(GLA_DK), s_spec(D), s_spec(D),
                  s_spec(GLA_DK), s_spec(GLA_DK),
                  full((3, 128, GLA_DK)), full((3, 128, GLA_DK)), full((1, GLA_DK)), full((1, GLA_DK))],
        out_specs=[pl.BlockSpec((TM_PREP, W3), lambda i: (i, 0)),
                   full((128, GLA_DK)), full((128, GLA_DK)), full((1, GLA_DK)), full((1, GLA_DK))],
        out_shape=[jax.ShapeDtypeStruct((n_tiles * TM_PREP, W3), BF16),
                   jax.ShapeDtypeStruct((128, GLA_DK), F32), jax.ShapeDtypeStruct((128, GLA_DK), F32),
                   jax.ShapeDtypeStruct((1, GLA_DK), F32), jax.ShapeDtypeStruct((1, GLA_DK), F32)],
        compiler_params=_params(("arbitrary",)),
    )(p3, dq_f, dq_b, dk_f, dk_b, dv_f, dv_b, dc_f, dc_b, upf, upb, bias_f, bias_b)


def _sub_blocks(rev):
    if NSUB == 1:
        return [((0, CHUNK), CHUNK // 2, (0, CHUNK))]
    out = []
    for s in range(NSUB):
        rows = (s * SUB, SUB)
        if rev:
            ref = (s + 1) * SUB if s < NSUB - 1 else None
            cols = (s * SUB, CHUNK - s * SUB)
        else:
            ref = s * SUB - 1 if s > 0 else None
            cols = (0, (s + 1) * SUB)
        out.append((rows, ref, cols))
    return out


def _sub_mask(rows, cols, rev):
    r = rows[0] + lax.broadcasted_iota(jnp.int32, (rows[1], cols[1]), 0)
    c = cols[0] + lax.broadcasted_iota(jnp.int32, (rows[1], cols[1]), 1)
    return (c >= r) if rev else (c <= r)


def _sub_operands(qc, kc, cc, rows, ref, cols):
    cref = jnp.zeros((1, HEAD_K), F32) if ref is None else cc[ref:ref + 1]
    eq = jnp.exp(cc[rows[0]:rows[0] + rows[1]] - cref)
    ek = jnp.exp(cref - cc[cols[0]:cols[0] + cols[1]])
    qs = qc[rows[0]:rows[0] + rows[1]] * eq
    kk = kc[cols[0]:cols[0] + cols[1]] * ek
    return qs, kk, eq, ek


SCAN_ROWS = 256
SCAN_CHUNKS = SCAN_ROWS // CHUNK
SCAN_STEPS = SEQ_ALL // SCAN_ROWS
LAT_BLOCKS = SEQ // SCAN_ROWS


def _scan_block(t, rev):
    if rev:
        return SCAN_STEPS - 1 - t
    return jnp.where(t == 0, SCAN_STEPS - 1, t - 1)


def _scan_lat_block(t, rev):
    if rev:
        return jnp.minimum(SCAN_STEPS - 1 - t, LAT_BLOCKS - 1)
    return jnp.maximum(t - 1, 0)


def _head_cols(h):
    return pl.ds(h * HEAD_K, HEAD_K), pl.ds(h * HEAD_V, HEAD_V)


def gla_scan_fwd(q, k, v, cum, *, rev, name):
    n = q.shape[0]

    def body(q_ref, k_ref, v_ref, c_ref, o_ref, s_ref, sfin_ref, st):
        t = pl.program_id(1)

        @pl.when(t == 0)
        def _():
            st[...] = jnp.zeros_like(st)

        def chunk(j, carry):
            lj = SCAN_CHUNKS - 1 - j if rev else j
            r0 = lj * CHUNK
            rws = pl.ds(r0, CHUNK)
            for h in range(HEADS):
                kcols, vcols = _head_cols(h)
                qc, kc, cc = q_ref[0, rws, kcols], k_ref[0, rws, kcols], c_ref[0, rws, kcols]
                vc = v_ref[0, rws, vcols]
                s_in = st[h]
                s_ref[0, h, j] = _b16(s_in)
                edge = cc[0:1] if rev else cc[CHUNK - 1:CHUNK]
                ke = kc * jnp.exp(edge - cc)
                st[h] = s_in * jnp.exp(edge) + _tn(_b16(vc), _b16(ke))
                o_inter = _nt(_b16(qc * jnp.exp(cc)), _b16(s_in))
                vb = _b16(vc)
                for rows, ref, cols in _sub_blocks(rev):
                    qs, kk, _, _ = _sub_operands(qc, kc, cc, rows, ref, cols)
                    a = jnp.where(_sub_mask(rows, cols, rev), _nt(_b16(qs), _b16(kk)), 0.0)
                    o_s = _nn(_b16(a), vb[cols[0]:cols[0] + cols[1]])
                    o_ref[0, pl.ds(r0 + rows[0], rows[1]), vcols] = _b16(o_inter[rows[0]:rows[0] + rows[1]] + o_s)
            return carry

        for j in range(SCAN_CHUNKS):
            chunk(j, 0)

        @pl.when(t == SCAN_STEPS - 1)
        def _():
            sfin_ref[0] = st[...]

    def spec(w):
        return pl.BlockSpec((1, SCAN_ROWS, w), lambda b, t: (b, _scan_block(t, rev), 0))

    return _pallas(
        body, name=name, grid=(n, SCAN_STEPS),
        in_specs=[spec(GLA_DK), spec(GLA_DK), spec(D), spec(GLA_DK)],
        out_specs=[pl.BlockSpec((1, SCAN_ROWS, D), lambda b, t: (b, _scan_lat_block(t, rev), 0)),
                   pl.BlockSpec((1, HEADS, SCAN_CHUNKS, HEAD_V, HEAD_K), lambda b, t: (b, 0, t, 0, 0)),
                   pl.BlockSpec((1, HEADS, HEAD_V, HEAD_K), lambda b, t: (b, 0, 0, 0))],
        out_shape=[jax.ShapeDtypeStruct((n, SEQ, D), BF16),
                   jax.ShapeDtypeStruct((n, HEADS, NCHUNK, HEAD_V, HEAD_K), BF16),
                   jax.ShapeDtypeStruct((n, HEADS, HEAD_V, HEAD_K), F32)],
        scratch_shapes=[pltpu.VMEM((HEADS, HEAD_V, HEAD_K), F32)],
        compiler_params=_params(("parallel", "arbitrary")),
    )(q, k, v, cum)


def gla_scan_bwd(q, k, v, cum, s_all, s_fin, do, *, rev, name, rider=None):
    n = q.shape[0]

    def body(q_ref, k_ref, v_ref, c_ref, s_ref, sfin_ref, do_ref, *rest):
        if rider is not None:
            ride_in, rest = rest[0], rest[1:]
        dq_ref, dk_ref, dv_ref, dc_ref = rest[:4]
        if rider is not None:
            ride_out, rest = rest[4], rest[:4] + rest[5:]
        dst, s_next, dq_acc, dk_acc, dv_acc = rest[4:9]
        t = SCAN_STEPS - 1 - pl.program_id(1)

        if rider is not None:
            def copies():
                send, recv = rest[9], rest[10]
                if rider[0] == "swap":
                    return _pair_copies([], ride_in, [], ride_out, send, recv)
                x, y, c, chips = _place()
                return [_remote(ride_in.at[2 * cx + cy, rows], ride_out.at[pj, rows], send.at[pj * P_ROW_CHUNKS + pi],
                                recv.at[pj * P_ROW_CHUNKS + pi], (cx, cy, c))
                        for pj, (cx, cy) in enumerate(chips)
                        for pi, (_, rows) in enumerate(_half_chunks(0, (0, 2 * ride_in.shape[1]), 16, which=(1,)))]

            @pl.when((pl.program_id(0) == 0) & (pl.program_id(1) == 0))
            def _():
                for cp in copies():
                    cp.start()

            @pl.when((pl.program_id(0) == n - 1) & (pl.program_id(1) == SCAN_STEPS - 1))
            def _():
                for cp in copies():
                    cp.wait_recv()
                for cp in copies():
                    cp.wait_send()

        @pl.when(pl.program_id(1) == 0)
        def _():
            dst[...] = jnp.zeros_like(dst)
            s_next[...] = sfin_ref[0]

        def chunk(jj, carry):
            j = SCAN_CHUNKS - 1 - jj
            lj = SCAN_CHUNKS - 1 - j if rev else j
            rws = pl.ds(lj * CHUNK, CHUNK)
            for h in range(HEADS):
                kcols, vcols = _head_cols(h)
                qc, kc, cc = q_ref[0, rws, kcols], k_ref[0, rws, kcols], c_ref[0, rws, kcols]
                vc = v_ref[0, rws, vcols]
                doc = jnp.where(t > 0, do_ref[0, rws, vcols], 0.0)
                s_in = s_ref[0, h, j]
                s_out = s_next[h]
                ds_out = dst[h]
                edge = cc[0:1] if rev else cc[CHUNK - 1:CHUNK]
                e_q = jnp.exp(cc)
                e_k = jnp.exp(edge - cc)
                dob = _b16(doc)
                dsb = _b16(ds_out)
                dst[h] = ds_out * jnp.exp(edge) + _tn(dob, _b16(qc * e_q))
                s_next[h] = s_in.astype(F32)
                dq_acc[h] = e_q * _nn(dob, s_in)
                dk_acc[h] = e_k * _nn(_b16(vc), dsb)
                dv_acc[h] = _nt(_b16(kc * e_k), dsb)
                vb = _b16(vc)
                for rows, ref, cols in _sub_blocks(rev):
                    qs, kk, eq, ek = _sub_operands(qc, kc, cc, rows, ref, cols)
                    mask = _sub_mask(rows, cols, rev)
                    rsl = slice(rows[0], rows[0] + rows[1])
                    csl = pl.ds(cols[0], cols[1])
                    qsb, kkb = _b16(qs), _b16(kk)
                    a = jnp.where(mask, _nt(qsb, kkb), 0.0)
                    da = _b16(jnp.where(mask, _nt(dob[rsl], vb[cols[0]:cols[0] + cols[1]]), 0.0))
                    dq_acc[h, pl.ds(rows[0], rows[1]), :] += _nn(da, kkb) * eq
                    dk_acc[h, csl, :] += _tn(da, qsb) * ek
                    dv_acc[h, csl, :] += _tn(_b16(a), dob[rsl])
                dq = dq_acc[h]
                dk = dk_acc[h]
                dc = qc * dq - kc * dk
                bnd = jnp.sum(ds_out * s_out, axis=0, keepdims=True)
                edge_row = 0 if rev else CHUNK - 1
                is_edge = lax.broadcasted_iota(jnp.int32, (CHUNK, HEAD_K), 0) == edge_row
                dq_ref[0, rws, kcols] = _b16(dq)
                dk_ref[0, rws, kcols] = _b16(dk)
                dv_ref[0, rws, vcols] = _b16(dv_acc[h])
                dc_ref[0, rws, kcols] = dc + jnp.where(is_edge, bnd, 0.0)
            return carry

        for jj in range(SCAN_CHUNKS):
            chunk(jj, 0)

    def step_of(u):
        return SCAN_STEPS - 1 - u

    def spec(w):
        return pl.BlockSpec((1, SCAN_ROWS, w), lambda b, u: (b, _scan_block(step_of(u), rev), 0))

    in_specs = [spec(GLA_DK), spec(GLA_DK), spec(D), spec(GLA_DK),
                pl.BlockSpec((1, HEADS, SCAN_CHUNKS, HEAD_V, HEAD_K), lambda b, u: (b, 0, step_of(u), 0, 0)),
                pl.BlockSpec((1, HEADS, HEAD_V, HEAD_K), lambda b, u: (b, 0, 0, 0)),
                pl.BlockSpec((1, SCAN_ROWS, D), lambda b, u: (b, _scan_lat_block(step_of(u), rev), 0))]
    out_specs = [spec(GLA_DK), spec(GLA_DK), spec(D), spec(GLA_DK)]
    out_shape = [jax.ShapeDtypeStruct((n, SEQ_ALL, GLA_DK), BF16), jax.ShapeDtypeStruct((n, SEQ_ALL, GLA_DK), BF16),
                 jax.ShapeDtypeStruct((n, SEQ_ALL, D), BF16), jax.ShapeDtypeStruct((n, SEQ_ALL, GLA_DK), F32)]
    scratch = [pltpu.VMEM((HEADS, HEAD_V, HEAD_K), F32), pltpu.VMEM((HEADS, HEAD_V, HEAD_K), F32),
               pltpu.VMEM((HEADS, CHUNK, HEAD_K), F32), pltpu.VMEM((HEADS, CHUNK, HEAD_K), F32),
               pltpu.VMEM((HEADS, CHUNK, HEAD_V), F32)]
    args = [q, k, v, cum, s_all, s_fin, do]
    if rider is not None:
        kind, arr = rider
        any_spec = pl.BlockSpec(memory_space=pl.ANY)
        in_specs.append(any_spec)
        out_specs.append(any_spec)
        args.append(arr)
        if kind == "swap":
            out_shape += _pair_got_shapes([], arr)
            n_cp = _pair_count([], arr)
        else:
            out_shape.append(jax.ShapeDtypeStruct((3,) + arr.shape[1:], arr.dtype))
            n_cp = 3 * P_ROW_CHUNKS
        scratch += [pltpu.SemaphoreType.DMA((n_cp,)), pltpu.SemaphoreType.DMA((n_cp,))]
    return _pallas(
        body, name=name, grid=(n, SCAN_STEPS), in_specs=in_specs, out_specs=out_specs, out_shape=out_shape,
        scratch_shapes=scratch,
        compiler_params=_params(("parallel" if rider is None else "arbitrary", "arbitrary")),
    )(*args)


def gla_out_proj(o_f, o_b, r, gnorm, gla_proj):
    n = o_f.shape[0]
    tiles = SEQ // TM_OUT

    def body(of_ref, ob_ref, r_ref, g_ref, w_ref, og_ref, y_ref):
        for h in range(HEADS):
            cols = pl.ds(h * HEAD_V, HEAD_V)
            o = of_ref[0, :, cols].astype(F32) + ob_ref[0, :, cols].astype(F32)
            rs = lax.rsqrt(jnp.mean(o * o, axis=-1, keepdims=True) + EPS)
            og_ref[:, cols] = (o * rs * g_ref[...] * _silu(r_ref[:, cols].astype(F32))).astype(og_ref.dtype)
        y_ref[...] = _nn(og_ref[...], w_ref[...]).astype(y_ref.dtype)

    ospec = pl.BlockSpec((1, TM_OUT, D), lambda b, j: (b, j, 0))
    row = pl.BlockSpec((TM_OUT, D), lambda b, j: (b * tiles + j, 0))
    return _pallas(
        body, name="gla_out_proj", grid=(n, tiles),
        in_specs=[ospec, ospec, row, pl.BlockSpec((1, HEAD_V), lambda b, j: (0, 0)),
                  pl.BlockSpec((D, D), lambda b, j: (0, 0))],
        out_specs=[row, row],
        out_shape=[jax.ShapeDtypeStruct((n * SEQ, D), BF16), jax.ShapeDtypeStruct((n * SEQ, D), BF16)],
        compiler_params=_params(("parallel", "parallel")),
    )(o_f, o_b, r, gnorm, gla_proj)


def gla_out_bwd(o_f, o_b, r, dyg, gla_proj, gnorm, og):
    n = o_f.shape[0]
    tiles = SEQ // TM_EW

    def body(of_ref, ob_ref, r_ref, d_ref, w_ref, g_ref, og_ref, do_ref, dr_ref, dg_ref, gw_ref, dog_buf):
        @pl.when((pl.program_id(0) == 0) & (pl.program_id(1) == 0))
        def _():
            dg_ref[...] = jnp.zeros_like(dg_ref)
            gw_ref[...] = jnp.zeros_like(gw_ref)

        gw_ref[...] += _tn(og_ref[...], d_ref[...])
        dog_buf[...] = _nt(d_ref[...], w_ref[...])
        for h in range(HEADS):
            cols = pl.ds(h * HEAD_V, HEAD_V)
            o = of_ref[0, :, cols].astype(F32) + ob_ref[0, :, cols].astype(F32)
            rv = r_ref[:, cols].astype(F32)
            dv = dog_buf[:, cols]
            rs = lax.rsqrt(jnp.mean(o * o, axis=-1, keepdims=True) + EPS)
            oh = o * rs
            dr_ref[:, cols] = (dv * oh * g_ref[...] * _dsilu(rv)).astype(dr_ref.dtype)
            dn = dv * _silu(rv)
            dg_ref[...] += jnp.sum(dn * oh, axis=0, keepdims=True)
            doh = dn * g_ref[...]
            do_ref[0, :, cols] = _b16(rs * (doh - oh * jnp.mean(doh * oh, axis=-1, keepdims=True)))

    ospec = pl.BlockSpec((1, TM_EW, D), lambda b, j: (b, j, 0))
    row = pl.BlockSpec((TM_EW, D), lambda b, j: (b * tiles + j, 0))
    vec = pl.BlockSpec((1, HEAD_V), lambda b, j: (0, 0))
    return _pallas(
        body, name="gla_out_bwd", grid=(n, tiles),
        in_specs=[ospec, ospec, row, row, pl.BlockSpec((D, D), lambda b, j: (0, 0)), vec, row],
        out_specs=[ospec, row, vec, pl.BlockSpec((D, D), lambda b, j: (0, 0))],
        out_shape=[jax.ShapeDtypeStruct((n, SEQ, D), BF16), jax.ShapeDtypeStruct((n * SEQ, D), BF16),
                   jax.ShapeDtypeStruct((1, HEAD_V), F32), jax.ShapeDtypeStruct((D, D), F32)],
        scratch_shapes=[pltpu.VMEM((TM_EW, D), F32)],
        compiler_params=_params(("arbitrary", "arbitrary")),
    )(o_f, o_b, r, dyg, gla_proj, gnorm, og)


TM_OUT = 512


def merge_out_final(p5, y_conv, y_gla, w_out, x2, gate, final_g, target, n_samples):
    t = x2.shape[0]
    tiles = SEQ // TM_OUT

    def body(mc_ref, mg_ref, yc_ref, yg_ref, w_ref, x_ref, gate_ref, g_ref, t_ref,
             mrg_ref, dh_ref, dmo_ref, dgate_ref, dg_ref, loss_ref):
        b, j = pl.program_id(0), pl.program_id(1)
        f = lambda ref: ref[...].astype(F32)
        merged = _b16(_sigmoid(f(mc_ref)) * f(yc_ref) + _sigmoid(f(mg_ref)) * f(yg_ref))
        mrg_ref[...] = merged
        mo_v = _nn(merged, w_ref[...])
        h = x_ref[...] + gate_ref[0] * mo_v
        rs = lax.rsqrt(jnp.mean(h * h, axis=-1, keepdims=True) + EPS)
        nh = h * rs
        err = nh * g_ref[...] - t_ref[...]
        dy = err * (1.0 / D)
        dn = dy * g_ref[...]
        dh = rs * (dn - nh * jnp.mean(dn * nh, axis=-1, keepdims=True))
        dh_ref[...] = dh
        dmo_ref[...] = (dh * gate_ref[0]).astype(dmo_ref.dtype)

        @pl.when(j == 0)
        def _():
            dgate_ref[...] = jnp.zeros_like(dgate_ref)

        @pl.when((b == 0) & (j == 0))
        def _():
            dg_ref[...] = jnp.zeros_like(dg_ref)
            loss_ref[...] = jnp.zeros_like(loss_ref)

        dgate_ref[0] += jnp.sum(dh * mo_v, axis=0, keepdims=True)
        dg_ref[...] += jnp.sum(dy * nh, axis=0, keepdims=True)
        loss_ref[...] += (0.5 / D) * jnp.sum(err * err)

    row = pl.BlockSpec((TM_OUT, D), lambda b, j: (b * tiles + j, 0))
    per = pl.BlockSpec((1, 1, D), lambda b, j: (b, 0, 0))
    vec = pl.BlockSpec((1, D), lambda b, j: (0, 0))
    return _pallas(
        body, name="merge_out_final", grid=(n_samples, tiles),
        in_specs=[row, pl.BlockSpec((TM_OUT, D), lambda b, j: (b * tiles + j, 1)), row, row,
                  pl.BlockSpec((D, D), lambda b, j: (0, 0)), row, per, vec, row],
        out_specs=[row, row, row, per, vec, pl.BlockSpec((8, 128), lambda b, j: (0, 0))],
        out_shape=[jax.ShapeDtypeStruct((t, D), BF16), jax.ShapeDtypeStruct((t, D), F32), jax.ShapeDtypeStruct((t, D), BF16),
                   jax.ShapeDtypeStruct((n_samples, 1, D), F32), jax.ShapeDtypeStruct((1, D), F32),
                   jax.ShapeDtypeStruct((8, 128), F32)],
        compiler_params=_params(("arbitrary", "arbitrary")),
    )(p5, p5, y_conv, y_gla, w_out, x2, gate, final_g, target)


def out_dgrad_merge_bwd(p5, y_conv, y_gla, dmo, w_out, merged):
    t = y_conv.shape[0]

    def body(mc_ref, mg_ref, yc_ref, yg_ref, d_ref, w_ref, mrg_ref, dyc_ref, dyg_ref, dp_ref, gw_ref):
        f = lambda ref: ref[...].astype(F32)

        @pl.when(pl.program_id(0) == 0)
        def _():
            gw_ref[...] = jnp.zeros_like(gw_ref)

        gw_ref[...] += _tn(mrg_ref[...], d_ref[...])
        d = _nt(d_ref[...], w_ref[...])
        sc = _sigmoid(f(mc_ref))
        sg = _sigmoid(f(mg_ref))
        dyc_ref[...] = (d * sc).astype(dyc_ref.dtype)
        dyg_ref[...] = (d * sg).astype(dyg_ref.dtype)
        dp_ref[:, pl.ds(0, D)] = (d * f(yc_ref) * sc * (1.0 - sc)).astype(dp_ref.dtype)
        dp_ref[:, pl.ds(D, D)] = (d * f(yg_ref) * sg * (1.0 - sg)).astype(dp_ref.dtype)

    row = pl.BlockSpec((TM_OUT, D), lambda i: (i, 0))
    return _pallas(
        body, name="out_dgrad_merge_bwd", grid=(t // TM_OUT,),
        in_specs=[row, pl.BlockSpec((TM_OUT, D), lambda i: (i, 1)), row, row, row, pl.BlockSpec((D, D), lambda i: (0, 0)),
                  row],
        out_specs=[row, row, pl.BlockSpec((TM_OUT, 2 * D), lambda i: (i, 0)), pl.BlockSpec((D, D), lambda i: (0, 0))],
        out_shape=[jax.ShapeDtypeStruct((t, D), BF16), jax.ShapeDtypeStruct((t, D), BF16),
                   jax.ShapeDtypeStruct((t, 2 * D), BF16), jax.ShapeDtypeStruct((D, D), F32)],
        compiler_params=_params(("arbitrary",)),
    )(p5, p5, y_conv, y_gla, dmo, w_out, merged)


def local_step(x, ctx, target, mod, wts, small, p_sh, chip, core):
    n = x.shape[0]
    t = n * SEQ
    t_all = t + n * NCTX
    x2 = x.reshape(t, D)
    ctx2 = ctx.reshape(n * NCTX, D)
    tgt2 = target.reshape(t, D)
    scale1, shift, gate = mod

    u = norm_mod_fwd(x2, ctx2, scale1, shift, small["norm_g"])
    p1, p2, p3, p4, p5, p_all = proj_all(u, [wts["w%d" % i] for i in range(1, 6)], [small["b%d" % i] for i in range(1, 6)],
                                         [t, t, t_all, t, t], p_sh, tm=512)
    p_full = jnp.stack([jnp.where(chip == i, p_sh, p_all[i]) for i in range(N_CHIPS)])
    wts = dict(wts, conv_proj=p_full[:, 0:256].reshape(D, D), gla_proj=p_full[:, 256:512].reshape(D, D),
               w_out=p_full[:, 512:768].reshape(D, D))

    aconv = conv_fwd(p1, small["conv_w"], small["conv_b"], n)
    ac, y_conv = ln_gate_proj(aconv, p2, small["conv_ln_g"], small["conv_ln_b"], wts["conv_proj"])

    qs, ks, vs, cum_f, cum_b = gla_prep_fwd(p3, small["upf"], small["upb"], small["bias_f"], small["bias_b"], n)
    o_f, s_f, sfin_f = gla_scan_fwd(qs, ks, vs, cum_f, rev=False, name="gla_scan_fwd_f")
    o_b, s_b, sfin_b = gla_scan_fwd(qs, ks, vs, cum_b, rev=True, name="gla_scan_fwd_b")
    og, y_gla = gla_out_proj(o_f, o_b, p4, small["gla_norm_g"], wts["gla_proj"])

    merged, dh, dmo, dgate, d_final_g, loss = merge_out_final(p5, y_conv, y_gla, wts["w_out"], x2, gate,
                                                              small["final_norm_g"], tgt2, n)

    g = {"final_norm_g": d_final_g}
    dyc, dyg, dp5, g["w_out"] = out_dgrad_merge_bwd(p5, y_conv, y_gla, dmo, wts["w_out"], merged)

    daconv, dp2, g["conv_ln_g"], g["conv_ln_b"], g["conv_proj"] = ln_gate_bwd(
        aconv, p2, dyc, wts["conv_proj"], small["conv_ln_g"], small["conv_ln_b"], ac)
    dp1, dconv_w, dconv_b = conv_bwd(p1, daconv, small["conv_w"], n)
    g["conv_w"], g["conv_b"] = dconv_w, dconv_b

    do, dp4, g["gla_norm_g"], g["gla_proj"] = gla_out_bwd(o_f, o_b, p4, dyg, wts["gla_proj"], small["gla_norm_g"], og)
    g["proj"] = jnp.concatenate([g["conv_proj"].reshape(N_CHIPS, 256, D), g["gla_proj"].reshape(N_CHIPS, 256, D),
                                 g["w_out"].reshape(N_CHIPS, 256, D)], 1)
    dq_f, dk_f, dv_f, dc_f, gotp = gla_scan_bwd(qs, ks, vs, cum_f, s_f, sfin_f, do, rev=False, name="gla_scan_bwd_f",
                                                rider=("swap", g["proj"]))
    pap16 = pair_add(core, g["proj"], gotp, name="pair_add_p", tr=384)
    dq_b, dk_b, dv_b, dc_b, rbp = gla_scan_bwd(qs, ks, vs, cum_b, s_b, sfin_b, do, rev=True, name="gla_scan_bwd_b",
                                               rider=("exchange", pap16))
    dp3, g["upf"], g["upb"], g["bias_f"], g["bias_b"] = gla_prep_bwd(
        p3, dq_f, dq_b, dk_f, dk_b, dv_f, dv_b, dc_f, dc_b,
        small["upf"], small["upb"], small["bias_f"], small["bias_b"], n)

    dps = [dp1, dp2, dp3, dp4, dp5]
    early = [g["conv_b"].sum(0), g["conv_ln_g"], g["conv_ln_b"], g["bias_f"], g["bias_b"], g["gla_norm_g"],
             g["final_norm_g"], g["conv_w"].sum(0)[:CONV_K], g["upf"][0:16], g["upb"][16:32]]
    got = {}
    for i in [0, 1, 3, 4, 2]:
        dp = dps[i]
        rows = dp.shape[0]
        tn = W3 if dp.shape[1] == W3 else 1024
        others = [j for j in range(5) if j != i]
        swap = ([g["w%d" % (j + 1)] for j in others], _pack(early)) if i == 2 else None
        outs = matmul_tn(u, dp, name="w_in_wgrad_%d" % (i + 1), t=rows, tn=tn, tt=1024 if rows % 1024 == 0 else 768,
                         colsum=True, swap=swap)
        g["w%d" % (i + 1)], g["b%d" % (i + 1)] = outs[0], outs[1]
        if swap is not None:
            got = dict(zip(others, outs[2:2 + len(others)]))
            sall_early = outs[2 + len(others)]
    g["gate"] = dgate
    return loss, dh, dps, g, got, (pap16, rbp), (sall_early, [a.shape for a in early])


def _group_cols(w):
    gv, gg, z = w[..., 0:1024], w[..., 1024:2048], w[..., 2048:3072]
    q, k, v = w[..., 3072:3584], w[..., 3584:4096], w[..., 4096:5120]
    ab = w[..., 5120:5152]
    r, mc, mg = w[..., 5152:6176], w[..., 6176:7200], w[..., 7200:8224]
    g1 = jnp.concatenate([p for j in range(CONV_NCB)
                          for p in (gv[..., CONV_CB * j:CONV_CB * (j + 1)], gg[..., CONV_CB * j:CONV_CB * (j + 1)])], -1)
    pad = jnp.zeros(w.shape[:-1] + (W3 - 2080,), w.dtype)
    g3 = jnp.concatenate([v, q, k, ab, pad], -1)
    return g1, z, g3, r, jnp.concatenate([mc, mg], -1)


def _ungroup_cols(g1, g2, g3, g4, g5):
    gv = jnp.concatenate([g1[..., 2 * CONV_CB * j:2 * CONV_CB * j + CONV_CB] for j in range(CONV_NCB)], -1)
    gg = jnp.concatenate([g1[..., 2 * CONV_CB * j + CONV_CB:2 * CONV_CB * (j + 1)] for j in range(CONV_NCB)], -1)
    v, q, k, ab = g3[..., 0:1024], g3[..., 1024:1536], g3[..., 1536:2048], g3[..., 2048:2080]
    return jnp.concatenate([gv, gg, g2, q, k, v, ab, g4, g5[..., 0:1024], g5[..., 1024:2048]], -1)


def _natural_pieces():
    pieces = [(CONV_CB * j, CONV_CB, 0, 2 * CONV_CB * j) for j in range(CONV_NCB)]
    pieces += [(1024 + CONV_CB * j, CONV_CB, 0, 2 * CONV_CB * j + CONV_CB) for j in range(CONV_NCB)]
    pieces += [(2048, 1024, 1, 0), (3072, 512, 2, O3_Q), (3584, 512, 2, O3_K), (4096, 1024, 2, O3_V), (5120, 32, 2, O3_AB),
               (5152, 1024, 3, 0), (6176, 1024, 4, 0), (7200, 1024, 4, 1024)]
    return sorted(pieces)


def _ungroup_to_shards(groups):
    shards = []
    for i in range(N_CHIPS):
        lo, hi = i * W_IN_SHARD, (i + 1) * W_IN_SHARD
        parts = []
        for nat, width, g, gcol in _natural_pieces():
            a, b = max(nat, lo), min(nat + width, hi)
            if a < b:
                parts.append(groups[g][:, gcol + a - nat:gcol + b - nat])
        shards.append(jnp.concatenate(parts, 1))
    return jnp.stack(shards)


def _pad_up(up, row0):
    return jnp.zeros((128, GLA_DK), F32).at[row0:row0 + up.shape[0]].set(up)


def _adamw_math(w, g, m, v):
    m = ADAM_B1 * m + (1.0 - ADAM_B1) * g
    v = ADAM_B2 * v + (1.0 - ADAM_B2) * (g * g)
    m_hat = m / (1.0 - ADAM_B1 ** ADAM_STEP)
    v_hat = v / (1.0 - ADAM_B2 ** ADAM_STEP)
    delta = -ADAM_LR * (m_hat / (jnp.sqrt(v_hat) + ADAM_EPS) + ADAM_WD * w)
    return delta, m, v


def adamw2d(w, g, m, v, *, name, tr, tcols=None):
    rows, cols = w.shape[-2:]

    def body(w_ref, g_ref, m_ref, v_ref, d_ref, nm_ref, nv_ref):
        d_ref[...], nm_ref[...], nv_ref[...] = _adamw_math(w_ref[...], g_ref[...], m_ref[...], v_ref[...])

    tcols = cols if tcols is None else tcols
    if w.ndim == 3:
        spec = pl.BlockSpec((1, tr, tcols), lambda i, j: (0, i, j))
    else:
        spec = pl.BlockSpec((tr, tcols), lambda i, j: (i, j))
    return _pallas(
        body, name=name, grid=(rows // tr, cols // tcols), in_specs=[spec] * 4, out_specs=[spec] * 3,
        out_shape=[jax.ShapeDtypeStruct(w.shape, F32)] * 3, compiler_params=_params(("parallel", "parallel")),
    )(w, g, m, v)


def adamw_many(ws, gs, ms, vs):
    k = len(ws)
    two = lambda a: a.reshape((-1, a.shape[-1]))

    def body(*refs):
        w_refs, g_refs, m_refs, v_refs = refs[:k], refs[k:2 * k], refs[2 * k:3 * k], refs[3 * k:4 * k]
        d_refs, nm_refs, nv_refs = refs[4 * k:5 * k], refs[5 * k:6 * k], refs[6 * k:7 * k]
        for i in range(k):
            d_refs[i][...], nm_refs[i][...], nv_refs[i][...] = _adamw_math(
                w_refs[i][...], g_refs[i][...], m_refs[i][...], v_refs[i][...])

    shapes = [jax.ShapeDtypeStruct(two(a).shape, F32) for a in ws]
    outs = _pallas(body, name="adamw_small", out_shape=shapes * 3, compiler_params=_params())(
        *[two(a) for a in ws], *[two(a) for a in gs], *[two(a) for a in ms], *[two(a) for a in vs])
    back = lambda lst: [o.reshape(a.shape) for o, a in zip(lst, ws)]
    return back(outs[:k]), back(outs[k:2 * k]), back(outs[2 * k:])


def sum_devices(sall, *, name):
    rows = sall.shape[1]

    def body(s_ref, o_ref):
        acc = s_ref[0]
        for d in range(1, N_DEV):
            acc = acc + s_ref[d]
        o_ref[...] = acc

    return _pallas(body, name=name, out_shape=jax.ShapeDtypeStruct((rows, D), F32),
                   compiler_params=_params())(sall)


def pair_add(core, g, got, *, name, tr):
    n, rows, cols = got.shape
    g4 = g.reshape(n, 2, rows, cols)

    def body(core_ref, g_ref, got_ref, ob_ref):
        del core_ref
        ob_ref[0] = (g_ref[0, 0] + got_ref[0]).astype(BF16)

    spec = pl.BlockSpec((1, tr, cols), lambda i, t, core_ref: (i, t, 0))
    return _pallas(
        body, name=name,
        grid_spec=pltpu.PrefetchScalarGridSpec(
            num_scalar_prefetch=1, grid=(n, rows // tr),
            in_specs=[pl.BlockSpec((1, 1, tr, cols), lambda i, t, core_ref: (i, core_ref[0], t, 0)), spec],
            out_specs=spec),
        out_shape=jax.ShapeDtypeStruct(got.shape, BF16),
        compiler_params=_params(("parallel", "parallel")))(core, g4, got)


def pair_add_groups(core, gs, gots, *, tr):
    k = len(gs)
    rows = gots[0].shape[0]

    def body(core_ref, *refs):
        del core_ref
        for i in range(k):
            refs[2 * k + i][...] = (refs[i][0] + refs[k + i][...]).astype(BF16)

    g_specs = [pl.BlockSpec((1, tr, a.shape[1]), lambda t, core_ref: (core_ref[0], t, 0)) for a in gots]
    r_specs = [pl.BlockSpec((tr, a.shape[1]), lambda t, core_ref: (t, 0)) for a in gots]
    return _pallas(
        body, name="pair_add_w",
        grid_spec=pltpu.PrefetchScalarGridSpec(num_scalar_prefetch=1, grid=(rows // tr,),
                                               in_specs=g_specs + r_specs, out_specs=r_specs),
        out_shape=[jax.ShapeDtypeStruct(a.shape, BF16) for a in gots],
        compiler_params=_params(("parallel",)))(core, *[a.reshape(2, rows, a.shape[1]) for a in gs], *gots)


def chip_add(place, pa, rb, *, name, tr):
    _, rows, cols = pa.shape

    def body(place_ref, m_ref, r_ref, o_ref):
        del place_ref
        o_ref[0] = ((m_ref[0].astype(F32) + r_ref[0].astype(F32)) + r_ref[1].astype(F32)) + r_ref[2].astype(F32)

    return _pallas(
        body, name=name,
        grid_spec=pltpu.PrefetchScalarGridSpec(
            num_scalar_prefetch=1, grid=(rows // tr,),
            in_specs=[pl.BlockSpec((1, tr, cols), lambda t, place_ref: (place_ref[0], t, 0)),
                      pl.BlockSpec((3, tr, cols), lambda t, place_ref: (0, t, 0))],
            out_specs=pl.BlockSpec((1, tr, cols), lambda t, place_ref: (place_ref[1], t, 0))),
        out_shape=jax.ShapeDtypeStruct((2, rows, cols), F32),
        compiler_params=_params(("parallel",)))(place, pa, rb)


def ada_bwd(call, cctx_rows, dm_shard, dm_full, adaw):
    nsh = adaw.shape[1]

    def body(c_ref, cc_ref, dms_ref, dmf_ref, w_ref, gw_ref, gb_ref, pq_ref):
        a_lat = _silu(c_ref[...])
        a_ctx = _silu(cc_ref[...])
        dms = dms_ref[...]
        gw_ref[...] = _tn(a_lat, dms[0:64], HI) + _tn(a_ctx, dms[64:72], HI)
        gb_ref[...] = jnp.sum(dmf_ref[...], axis=0, keepdims=True)
        part = _nt(dms[64:72], w_ref[...], HI)
        pq_ref[...] = jnp.zeros_like(pq_ref) + jnp.sum(part, axis=0, keepdims=True)

    return _pallas(body, name="ada_bwd",
                   out_shape=[jax.ShapeDtypeStruct((D, nsh), F32), jax.ShapeDtypeStruct((1, 3 * D), F32),
                              jax.ShapeDtypeStruct((8, D), F32)],
                   compiler_params=_params())(call, cctx_rows, dm_shard, dm_full, adaw)


def cctx_grad(pq_all, cctx_rows):
    def body(p_ref, c_ref, o_ref):
        acc = p_ref[0]
        for qi in range(1, N_CHIPS):
            acc = acc + p_ref[qi]
        o_ref[...] = acc * _dsilu(c_ref[...])

    return _pallas(body, name="cctx_grad", out_shape=jax.ShapeDtypeStruct((8, D), F32),
                   compiler_params=_params())(pq_all, cctx_rows)


def _place():
    x, y, c = lax.axis_index("x"), lax.axis_index("y"), lax.axis_index("c")
    chips = [(1 - x, y), (x, 1 - y), (1 - x, 1 - y)]
    return x, y, c, chips


def _all_peers(x, y, c):
    return [((1 - x) if r & 4 else x, (1 - y) if r & 2 else y, (1 - c) if r & 1 else c) for r in range(1, N_DEV)]


def _remote(src, dst, send_sem, recv_sem, dev):
    return pltpu.make_async_remote_copy(src_ref=src, dst_ref=dst, send_sem=send_sem, recv_sem=recv_sem,
                                        device_id=dev, device_id_type=MESH)


ANY = pl.BlockSpec(memory_space=pl.ANY)
VMEM = pl.BlockSpec(memory_space=pltpu.VMEM)
F_ROWS = 16


W_ROW_CHUNKS = 4
P_ROW_CHUNKS = 2
N_BULK = W_ROW_CHUNKS + P_ROW_CHUNKS


def _half_chunks(core, n_rows, align, which=(0, 1)):
    out = []
    for a, k in ((0, W_ROW_CHUNKS), (1, P_ROW_CHUNKS)):
        if a not in which:
            continue
        half = n_rows[a] // 2
        size = half // k
        for i in range(k):
            start = core * half + i * size
            out.append((a, pl.ds(start if isinstance(start, int) else pl.multiple_of(start, align), size)))
    return out


def gather_weights(c8, cctx8, adaw, adab, w_sh, fp):
    nsh = adaw.shape[1]

    def body(c_ref, cctx_ref, adaw_ref, adab_ref, w_ref, fp_ref, wall_ref, fall_ref, call_ref, mall_ref,
             abuf, w_send, w_recv, h_send, h_recv, c_send, c_recv, m_send, m_recv, f_send, f_recv):
        x, y, c, chips = _place()
        q = 2 * x + y
        dev = 4 * x + 2 * y + c
        qs = [2 * cx + cy for cx, cy in chips]
        sib = (x, y, 1 - c)
        srcs, dsts = (w_ref,), (wall_ref,)
        n_rows = (w_ref.shape[0],)
        mine = _half_chunks(c, n_rows, 16, which=(0,))
        other = _half_chunks(1 - c, n_rows, 16, which=(0,))

        bulk = [[_remote(srcs[a].at[rows], dsts[a].at[q, rows], w_send.at[j * N_BULK + i], w_recv.at[j * N_BULK + i],
                         (*chips[j], c)) for i, (a, rows) in enumerate(mine)] for j in range(3)]
        fall_ref[q] = fp_ref[...]
        small = [_remote(fp_ref, fall_ref.at[q], f_send.at[j], f_recv.at[j], (*chips[j], c)) for j in range(3)]
        my_rows = pl.ds(pl.multiple_of(8 * dev, 8), 8)
        call_ref[my_rows, :] = c_ref[...]
        cond = [_remote(c_ref, call_ref.at[my_rows, :], c_send.at[r], c_recv.at[r], peer)
                for r, peer in enumerate(_all_peers(x, y, c))]
        for cp in sum(bulk, []) + small + cond:
            cp.start()
        for cp in cond:
            cp.wait_recv()

        abuf[pl.ds(0, 64), :] = _silu(call_ref[...])
        abuf[pl.ds(64, 8), :] = _silu(cctx_ref[...])
        mall_ref[q] = _nn(abuf[...], adaw_ref[...], HI) + adab_ref[...]
        mod = [_remote(mall_ref.at[q], mall_ref.at[q], m_send.at[j], m_recv.at[j], (*chips[j], c)) for j in range(3)]
        for cp in mod:
            cp.start()

        handed = []
        for j in range(3):
            for i, (a, rows) in enumerate(mine):
                bulk[j][i].wait_recv()
                cp = _remote(dsts[a].at[qs[j], rows], dsts[a].at[qs[j], rows],
                             h_send.at[j * N_BULK + i], h_recv.at[j * N_BULK + i], sib)
                cp.start()
                handed.append(cp)
        for j in range(3):
            for i, (a, rows) in enumerate(other):
                _remote(dsts[a].at[qs[j], rows], dsts[a].at[qs[j], rows],
                        h_send.at[j * N_BULK + i], h_recv.at[j * N_BULK + i], sib).wait_recv()
        for cp in mod + small:
            cp.wait_recv()
        for cp in sum(bulk, []) + small + cond + mod + handed:
            cp.wait_send()

    def dma(n):
        return pltpu.SemaphoreType.DMA((n,))

    return _pallas(
        body, name="gather_weights",
        in_specs=[VMEM, VMEM, VMEM, VMEM, ANY, VMEM],
        out_specs=[ANY, VMEM, VMEM, VMEM],
        out_shape=[jax.ShapeDtypeStruct((N_CHIPS,) + w_sh.shape, BF16),
                   jax.ShapeDtypeStruct((N_CHIPS, F_ROWS, D), F32),
                   jax.ShapeDtypeStruct((8 * N_DEV, D), F32), jax.ShapeDtypeStruct((N_CHIPS, MOD_ROWS, nsh), F32)],
        scratch_shapes=[pltpu.VMEM((MOD_ROWS, D), F32), dma(3 * N_BULK), dma(3 * N_BULK), dma(3 * N_BULK), dma(3 * N_BULK),
                        dma(7), dma(7), dma(3), dma(3), dma(3), dma(3)],
        compiler_params=_params(),
    )(c8, cctx8, adaw, adab, w_sh, fp)


def _pair_count(gs, gp):
    return len(gs) * W_ROW_CHUNKS + (0 if gp is None else N_CHIPS * P_ROW_CHUNKS)


def _pair_got_shapes(gs, gp):
    shapes = [jax.ShapeDtypeStruct((D // 2, a.shape[1]), F32) for a in gs]
    if gp is not None:
        shapes.append(jax.ShapeDtypeStruct((N_CHIPS, gp.shape[1] // 2, gp.shape[2]), F32))
    return shapes


def _pair_copies(g_refs, gp_ref, got_refs, gotp_ref, a_send, a_recv):
    x, y, c, _ = _place()
    sib = (x, y, 1 - c)
    pair = []
    half, size = D // 2, D // 2 // W_ROW_CHUNKS
    for gi in range(len(g_refs)):
        for i in range(W_ROW_CHUNKS):
            k = len(pair)
            rows_o = pl.ds(pl.multiple_of((1 - c) * half + i * size, 8), size)
            pair.append(_remote(g_refs[gi].at[rows_o], got_refs[gi].at[pl.ds(i * size, size)],
                                a_send.at[k], a_recv.at[k], sib))
    if gp_ref is not None:
        half, size = gp_ref.shape[1] // 2, gp_ref.shape[1] // 2 // P_ROW_CHUNKS
        for s in range(N_CHIPS):
            for i in range(P_ROW_CHUNKS):
                k = len(pair)
                rows_o = pl.ds(pl.multiple_of((1 - c) * half + i * size, 8), size)
                pair.append(_remote(gp_ref.at[s, rows_o], gotp_ref.at[s, pl.ds(i * size, size)],
                                    a_send.at[k], a_recv.at[k], sib))
    return pair


def pair_swap(gs):
    n_gs = len(gs)

    def body(*refs):
        g_refs, got_refs = refs[:n_gs], refs[n_gs:2 * n_gs]
        a_send, a_recv = refs[2 * n_gs:]
        pair = _pair_copies(g_refs, None, got_refs, None, a_send, a_recv)
        for cp in pair:
            cp.start()
        for cp in pair:
            cp.wait_recv()
        for cp in pair:
            cp.wait_send()

    return _pallas(
        body, name="pair_swap", in_specs=[ANY] * n_gs, out_specs=[ANY] * n_gs,
        out_shape=_pair_got_shapes(gs, None),
        scratch_shapes=[pltpu.SemaphoreType.DMA((_pair_count(gs, None),)), pltpu.SemaphoreType.DMA((_pair_count(gs, None),))],
        compiler_params=_params(),
    )(*gs)


def gather_small(sm):
    rows = sm.shape[0]

    def body(sm_ref, sall_ref, s_send, s_recv):
        x, y, c, _ = _place()
        dev = 4 * x + 2 * y + c
        sall_ref[dev] = sm_ref[...]
        small = [_remote(sm_ref, sall_ref.at[dev], s_send.at[r], s_recv.at[r], peer)
                 for r, peer in enumerate(_all_peers(x, y, c))]
        for cp in small:
            cp.start()
        for cp in small:
            cp.wait_recv()
        for cp in small:
            cp.wait_send()

    return _pallas(
        body, name="gather_small", in_specs=[VMEM], out_specs=VMEM,
        out_shape=jax.ShapeDtypeStruct((N_DEV, rows, D), F32),
        scratch_shapes=[pltpu.SemaphoreType.DMA((7,)), pltpu.SemaphoreType.DMA((7,))],
        compiler_params=_params(),
    )(sm)


def pair_share(ghw, ghp, pq):
    def body(ghw_ref, ghp_ref, pq_ref, outw_ref, outp_ref, pqa_ref, send, recv, p_send, p_recv):
        del ghw_ref, ghp_ref
        x, y, c, chips = _place()
        q = 2 * x + y
        refs = (outw_ref, outp_ref)
        n_rows = (2 * outw_ref.shape[1], 2 * outp_ref.shape[1])
        pair = [_remote(refs[a].at[c, rows], refs[a].at[c, rows], send.at[i], recv.at[i], (x, y, 1 - c))
                for i, (a, rows) in enumerate(_half_chunks(0, n_rows, 8))]
        pqa_ref[q] = pq_ref[...]
        small = [_remote(pq_ref, pqa_ref.at[q], p_send.at[j], p_recv.at[j], (*chips[j], c)) for j in range(3)]
        for cp in pair + small:
            cp.start()
        for i, (a, rows) in enumerate(_half_chunks(0, n_rows, 8)):
            _remote(refs[a].at[1 - c, rows], refs[a].at[1 - c, rows], send.at[i], recv.at[i], (x, y, 1 - c)).wait_recv()
        for cp in small:
            cp.wait_recv()
        for cp in pair + small:
            cp.wait_send()

    return _pallas(
        body, name="pair_share", in_specs=[ANY, ANY, VMEM], out_specs=[ANY, ANY, VMEM],
        out_shape=[jax.ShapeDtypeStruct(ghw.shape, F32), jax.ShapeDtypeStruct(ghp.shape, F32),
                   jax.ShapeDtypeStruct((N_CHIPS, 8, D), F32)],
        scratch_shapes=[pltpu.SemaphoreType.DMA((N_BULK,)), pltpu.SemaphoreType.DMA((N_BULK,)),
                        pltpu.SemaphoreType.DMA((3,)), pltpu.SemaphoreType.DMA((3,))],
        input_output_aliases={0: 0, 1: 1},
        compiler_params=_params(),
    )(ghw, ghp, pq)


def _rows_of(shape):
    size = 1
    for s in shape:
        size *= s
    return -(-size // D)


def _pack(arrs, rows_multiple=8):
    parts = []
    total = 0
    for a in arrs:
        f = a.reshape(-1).astype(F32)
        r = _rows_of(a.shape)
        parts.append(jnp.pad(f, (0, r * D - f.shape[0])))
        total += r
    pad_rows = (-total) % rows_multiple
    if pad_rows:
        parts.append(jnp.zeros((pad_rows * D,), F32))
    return jnp.concatenate(parts).reshape(-1, D)


def _unpack(p, shapes):
    out = []
    r0 = 0
    for shp in shapes:
        r = _rows_of(shp)
        size = 1
        for s in shp:
            size *= s
        out.append(p[r0:r0 + r].reshape(-1)[:size].reshape(shp))
        r0 += r
    return out


WEIGHT_NAMES = ['c_ctx', 'ada_w', 'ada_b', 'norm_g', 'w_in', 'b_in', 'conv_w', 'conv_b', 'conv_ln_g', 'conv_ln_b',
                'conv_proj', 'decay_up_fwd', 'decay_bias_fwd', 'decay_up_bwd', 'decay_bias_bwd', 'gla_norm_g', 'gla_proj',
                'w_out', 'final_norm_g']
SMALL_NAMES = ['c_ctx', 'ada_b', 'norm_g', 'b_in', 'conv_w', 'conv_b', 'conv_ln_g', 'conv_ln_b', 'decay_up_fwd',
               'decay_bias_fwd', 'decay_up_bwd', 'decay_bias_bwd', 'gla_norm_g', 'final_norm_g']


def kernel(x, c, ctx, c_ctx, ada_w, ada_b, norm_g, w_in, b_in, conv_w, conv_b, conv_ln_g, conv_ln_b, conv_proj, decay_up_fwd, decay_bias_fwd, decay_up_bwd, decay_bias_bwd, gla_norm_g, gla_proj, w_out, final_norm_g, loss_target, m_c_ctx, m_ada_w, m_ada_b, m_norm_g, m_w_in, m_b_in, m_conv_w, m_conv_b, m_conv_ln_g, m_conv_ln_b, m_conv_proj, m_decay_up_fwd, m_decay_bias_fwd, m_decay_up_bwd, m_decay_bias_bwd, m_gla_norm_g, m_gla_proj, m_w_out, m_final_norm_g, v_c_ctx, v_ada_w, v_ada_b, v_norm_g, v_w_in, v_b_in, v_conv_w, v_conv_b, v_conv_ln_g, v_conv_ln_b, v_conv_proj, v_decay_up_fwd, v_decay_bias_fwd, v_decay_up_bwd, v_decay_bias_bwd, v_gla_norm_g, v_gla_proj, v_w_out, v_final_norm_g):
    w = dict(c_ctx=c_ctx, ada_w=ada_w, ada_b=ada_b, norm_g=norm_g, w_in=w_in, b_in=b_in, conv_w=conv_w, conv_b=conv_b,
             conv_ln_g=conv_ln_g, conv_ln_b=conv_ln_b, conv_proj=conv_proj, decay_up_fwd=decay_up_fwd,
             decay_bias_fwd=decay_bias_fwd, decay_up_bwd=decay_up_bwd, decay_bias_bwd=decay_bias_bwd,
             gla_norm_g=gla_norm_g, gla_proj=gla_proj, w_out=w_out, final_norm_g=final_norm_g)
    m = dict(c_ctx=m_c_ctx, ada_w=m_ada_w, ada_b=m_ada_b, norm_g=m_norm_g, w_in=m_w_in, b_in=m_b_in, conv_w=m_conv_w,
             conv_b=m_conv_b, conv_ln_g=m_conv_ln_g, conv_ln_b=m_conv_ln_b, conv_proj=m_conv_proj,
             decay_up_fwd=m_decay_up_fwd, decay_bias_fwd=m_decay_bias_fwd, decay_up_bwd=m_decay_up_bwd,
             decay_bias_bwd=m_decay_bias_bwd, gla_norm_g=m_gla_norm_g, gla_proj=m_gla_proj, w_out=m_w_out,
             final_norm_g=m_final_norm_g)
    v = dict(c_ctx=v_c_ctx, ada_w=v_ada_w, ada_b=v_ada_b, norm_g=v_norm_g, w_in=v_w_in, b_in=v_b_in, conv_w=v_conv_w,
             conv_b=v_conv_b, conv_ln_g=v_conv_ln_g, conv_ln_b=v_conv_ln_b, conv_proj=v_conv_proj,
             decay_up_fwd=v_decay_up_fwd, decay_bias_fwd=v_decay_bias_fwd, decay_up_bwd=v_decay_up_bwd,
             decay_bias_bwd=v_decay_bias_bwd, gla_norm_g=v_gla_norm_g, gla_proj=v_gla_proj, w_out=v_w_out,
             final_norm_g=v_final_norm_g)
    n = x.shape[0]
    ax, ay, ac = lax.axis_index("x"), lax.axis_index("y"), lax.axis_index("c")
    q = 2 * ax + ay
    dev = 4 * ax + 2 * ay + ac
    nsh = ada_w.shape[2]

    w_sh = w_in[0].astype(BF16)
    p_sh = jnp.concatenate([conv_proj[0], gla_proj[0], w_out[0]], 0).astype(BF16)
    fp = _pack([conv_w[0], decay_up_fwd[0], decay_up_bwd[0]], F_ROWS)
    c8 = jnp.pad(c, ((0, 8 - n), (0, 0)))
    cctx8 = jnp.pad(c_ctx[None], ((0, 7), (0, 0)))
    adab_sh = lax.dynamic_slice(ada_b, (0, q * nsh), (1, nsh))
    w_all, fall, call, mall = gather_weights(c8, cctx8, ada_w[0], adab_sh, w_sh, fp)

    mod_all = jnp.transpose(mall, (1, 0, 2)).reshape(MOD_ROWS, 3 * D)
    mod_mine = lax.dynamic_slice(mod_all, (8 * dev, 0), (n, 3 * D))
    mod_ctx = mod_all[64:65]
    shift = jnp.concatenate([mod_mine[:, 0:D], mod_ctx[:, 0:D]], 0)[:, None, :]
    scale1 = 1.0 + jnp.concatenate([mod_mine[:, D:2 * D], mod_ctx[:, D:2 * D]], 0)[:, None, :]
    gate = mod_mine[:, 2 * D:3 * D][:, None, :]

    own = lambda i, mine, got: jnp.where(q == i, mine, got)
    g1, g2, g3, g4, g5 = _group_cols(jnp.concatenate([own(i, w_sh, w_all[i]) for i in range(N_CHIPS)], 1))
    wts = dict(w1=g1, w2=g2, w3=g3, w4=g4, w5=g5)
    f_parts = [_unpack(fall[i], [conv_w.shape[1:], decay_up_fwd.shape[1:], decay_up_bwd.shape[1:]]) for i in range(N_CHIPS)]
    conv_w_full = jnp.concatenate([p[0] for p in f_parts], 1)
    upf_full = jnp.concatenate([p[1] for p in f_parts], 1)
    upb_full = jnp.concatenate([p[2] for p in f_parts], 1)
    b1, b2, b3, b4, b5 = _group_cols(b_in)
    small = dict(b1=b1, b2=b2, b3=b3, b4=b4, b5=b5, norm_g=norm_g,
                 conv_w=jnp.pad(conv_w_full, ((0, 1), (0, 0))), conv_b=conv_b, conv_ln_g=conv_ln_g, conv_ln_b=conv_ln_b,
                 upf=_split3(_pad_up(upf_full, 0)), upb=_split3(_pad_up(upb_full, 16)),
                 bias_f=decay_bias_fwd, bias_b=decay_bias_bwd,
                 gla_norm_g=gla_norm_g, final_norm_g=final_norm_g[None])

    core = ac.astype(jnp.int32).reshape(1)
    chip = q.astype(jnp.int32).reshape(1)
    loss_part, dh, dps, g, got, (pap16, rbp), (sall1, early_shapes) = local_step(
        x, ctx, loss_target, (scale1, shift, gate), wts, small, p_sh, q, core)

    gs = [g["w%d" % i] for i in range(1, 6)]
    got[2] = pair_swap([gs[2]])[0]
    halves = pair_add_groups(core, gs, [got[i] for i in range(5)], tr=128)
    paw16 = _ungroup_to_shards(halves)
    grad_x2, dshift, dscale, g["norm_g"], rbw = dgrad_norm_bwd(
        dps, [wts["w%d" % i] for i in range(1, 6)], paw16, x.reshape(n * SEQ, D), ctx.reshape(n * NCTX, D), dh,
        scale1, norm_g, tm=256)

    dm_mine = jnp.concatenate([dshift[:n, 0], dscale[:n, 0], g["gate"][:, 0]], -1)
    dm_ctx = jnp.concatenate([dshift[n, 0], dscale[n, 0], jnp.zeros((D,), F32)], -1)
    d_b_in = _ungroup_cols(*[g["b%d" % i] for i in range(1, 6)])
    late = [g["norm_g"], dm_mine, dm_ctx, d_b_in, loss_part[0, 0:1]]
    late_shapes = [a.shape for a in late]
    sall2 = gather_small(_pack(late))
    (s_conv_b, s_ln_g, s_ln_b, s_bias_f, s_bias_b, s_gla_g, s_final_g, s_conv_w, s_upf,
     s_upb) = _unpack(sum_devices(sall1, name="sum_devices_early"), early_shapes)
    s_late = _unpack(sum_devices(sall2, name="sum_devices_late"), late_shapes)
    s_norm_g, s_b_in, loss = s_late[0], s_late[3], s_late[4][0]
    r_mine, r_ctx = 1, 1 + 3 * n
    dm_all = sall2[:, r_mine:r_ctx].reshape(N_DEV, n, 3 * D)
    dm_full = jnp.concatenate([jnp.pad(dm_all, ((0, 0), (0, 8 - n), (0, 0))).reshape(8 * N_DEV, 3 * D),
                               sall2[:, r_ctx:r_ctx + 3].reshape(N_DEV, 3 * D)], 0)
    dm_shard = lax.dynamic_slice(dm_full, (0, q * nsh), (MOD_ROWS, nsh))
    cctx_rows = jnp.broadcast_to(c_ctx[None], (8, D))
    g_ada_w, g_ada_b, pq = ada_bwd(call, cctx_rows, dm_shard, dm_full, ada_w[0])

    place = jnp.concatenate([chip, core])
    ghw = chip_add(place, paw16, rbw, name="chip_add_w", tr=128)
    ghp = chip_add(place, pap16, rbp, name="chip_add_p", tr=384)
    gw_mine, gp_mine, pq_all = pair_share(ghw, ghp, pq)
    gp_mine = gp_mine.reshape(768, D)
    g_c_ctx = cctx_grad(pq_all, cctx_rows)[0]

    grads = dict(
        c_ctx=g_c_ctx, ada_w=g_ada_w[None], ada_b=g_ada_b, norm_g=s_norm_g,
        w_in=gw_mine.reshape(1, D, W_IN_SHARD), b_in=s_b_in,
        conv_w=lax.dynamic_slice(s_conv_w, (0, q * 256), (CONV_K, 256))[None], conv_b=s_conv_b,
        conv_ln_g=s_ln_g, conv_ln_b=s_ln_b, conv_proj=gp_mine[0:256][None],
        decay_up_fwd=lax.dynamic_slice(s_upf, (0, q * 128), (16, 128))[None], decay_bias_fwd=s_bias_f,
        decay_up_bwd=lax.dynamic_slice(s_upb, (0, q * 128), (16, 128))[None], decay_bias_bwd=s_bias_b,
        gla_norm_g=s_gla_g, gla_proj=gp_mine[256:512][None], w_out=gp_mine[512:768][None],
        final_norm_g=s_final_g[0])

    delta, new_m, new_v = {}, {}, {}
    for name in ["ada_w", "conv_proj", "gla_proj", "w_out"]:
        delta[name], new_m[name], new_v[name] = adamw2d(w[name], grads[name].reshape(w[name].shape), m[name], v[name],
                                                        name="adamw_" + name, tr=128)
    tr_ = lambda a: jnp.swapaxes(a, 1, 2)
    g_w_in_t = tr_(grads["w_in"])
    grads["w_in"] = tr_(g_w_in_t)
    d_, m_, v_ = adamw2d(tr_(w_in), g_w_in_t, tr_(m_w_in), tr_(v_w_in), name="adamw_w_in", tr=W_IN_SHARD, tcols=128)
    delta["w_in"], new_m["w_in"], new_v["w_in"] = tr_(d_), tr_(m_), tr_(v_)
    d_, m_, v_ = adamw_many([w[nm] for nm in SMALL_NAMES], [grads[nm].reshape(w[nm].shape) for nm in SMALL_NAMES],
                            [m[nm] for nm in SMALL_NAMES], [v[nm] for nm in SMALL_NAMES])
    for nm, a, b, cc in zip(SMALL_NAMES, d_, m_, v_):
        delta[nm], new_m[nm], new_v[nm] = a, b, cc

    grad_x = grad_x2.reshape(x.shape)
    return (loss, grad_x, *[grads[nm].reshape(w[nm].shape) for nm in WEIGHT_NAMES], *[delta[nm] for nm in WEIGHT_NAMES],
            *[new_m[nm] for nm in WEIGHT_NAMES], *[new_v[nm] for nm in WEIGHT_NAMES])
```

```python
import jax
import jax.numpy as jnp
from jax import lax
from jax.experimental import pallas as pl
from jax.experimental.pallas import tpu as pltpu

F32 = jnp.float32
BF16 = jnp.bfloat16
MESH = pl.DeviceIdType.MESH
HI = lax.Precision.HIGHEST

D = 1024
SEQ = 2048
GRID_W = 64
GRID_H = SEQ // GRID_W
NCTX = 256
SEQ_ALL = SEQ + NCTX
EPS = 1e-6
CONV_K = 31
CONV_PAD = CONV_K // 2
HEADS = 4
HEAD_K = 128
HEAD_V = 256
GLA_DK = HEADS * HEAD_K
GATE_TAU = 16.0
Q_SCALE = HEAD_K ** -0.5
CHUNK = 64
NCHUNK = SEQ_ALL // CHUNK
NCHUNK_LAT = SEQ // CHUNK
NCHUNK_CTX = NCHUNK - NCHUNK_LAT
SUB = 64
NSUB = CHUNK // SUB
N_IN = 8224
W3 = 2176
O3_V, O3_Q, O3_K, O3_AB = 0, 1024, 1536, 2048

ADAM_LR, ADAM_B1, ADAM_B2, ADAM_EPS, ADAM_WD, ADAM_STEP = 0.001, 0.9, 0.999, 1e-08, 0.01, 10
VMEM_LIMIT = 56 * 1024 * 1024

N_CHIPS = 4
N_DEV = 8
W_IN_SHARD = N_IN // N_CHIPS
MOD_ROWS = 72


def _pallas(body, **kw):
    return pl.pallas_call(body, **kw)


def _params(sem=None, **kw):
    if sem is not None:
        kw["dimension_semantics"] = sem
    return pltpu.CompilerParams(vmem_limit_bytes=VMEM_LIMIT, **kw)


def _sigmoid(v):
    return 1.0 / (1.0 + jnp.exp(-v))


def _silu(v):
    return v * _sigmoid(v)


def _dsilu(v):
    s = _sigmoid(v)
    return s * (1.0 + v * (1.0 - s))


def _log_sigmoid(v):
    return jnp.minimum(v, 0.0) - jnp.log(1.0 + jnp.exp(-jnp.abs(v)))


def _dot(a, b, dims, precision=None):
    return lax.dot_general(a, b, (dims, ((), ())), preferred_element_type=F32, precision=precision)


def _nn(a, b, precision=None):
    return _dot(a, b, ((1,), (0,)), precision)


def _nt(a, b, precision=None):
    return _dot(a, b, ((1,), (1,)), precision)


def _tn(a, b, precision=None):
    return _dot(a, b, ((0,), (0,)), precision)


def _b16(v):
    return v.astype(BF16)


def proj_all(u, ws, bs, rows, p_sh, *, tm):
    k = u.shape[1]
    n_g = len(ws)
    tns = [w.shape[1] if w.shape[1] % 1024 else 1024 for w in ws]
    mts = [r // tm for r in rows]
    cnts = [(w.shape[1] // tn) * mt for w, tn, mt in zip(ws, tns, mts)]
    los = [sum(cnts[:g]) for g in range(n_g)]
    n_steps = sum(cnts)

    def rel(s, g):
        return jnp.clip(s - los[g], 0, cnts[g] - 1)

    def active(s, g):
        return (s >= los[g]) & (s < los[g] + cnts[g])

    def u_row(s):
        r = 0
        for g in range(n_g):
            r = r + jnp.where(active(s, g), rel(s, g) % mts[g], 0)
        return r

    def body(*refs):
        u_ref = refs[0]
        w_refs, b_refs = refs[1:1 + n_g], refs[1 + n_g:1 + 2 * n_g]
        p_ref = refs[1 + 2 * n_g]
        o_refs = refs[2 + 2 * n_g:2 + 3 * n_g]
        pall_ref = refs[2 + 3 * n_g]
        w_send, w_recv, h_send, h_recv = refs[3 + 3 * n_g:]
        s = pl.program_id(0)
        for g in range(n_g):
            @pl.when(active(s, g))
            def _(g=g):
                o_refs[g][...] = (_nn(u_ref[...], w_refs[g][...]) + b_refs[g][...]).astype(o_refs[g].dtype)

        x, y, c, chips = _place()
        q = 2 * x + y
        mine = _half_chunks(c, (0, p_ref.shape[0]), 16, which=(1,))
        other = _half_chunks(1 - c, (0, p_ref.shape[0]), 16, which=(1,))
        nb = len(mine)

        def bulk():
            return [[_remote(p_ref.at[rws], pall_ref.at[q, rws], w_send.at[pj * nb + pi], w_recv.at[pj * nb + pi],
                             (*chips[pj], c)) for pi, (_, rws) in enumerate(mine)] for pj in range(3)]

        @pl.when(s == 0)
        def _():
            for cp in sum(bulk(), []):
                cp.start()

        @pl.when(s == n_steps - 1)
        def _():
            handed = []
            for pj, (cx, cy) in enumerate(chips):
                for pi, (_, rws) in enumerate(mine):
                    bulk()[pj][pi].wait_recv()
                    cp = _remote(pall_ref.at[2 * cx + cy, rws], pall_ref.at[2 * cx + cy, rws],
                                 h_send.at[pj * nb + pi], h_recv.at[pj * nb + pi], (x, y, 1 - c))
                    cp.start()
                    handed.append(cp)
            for pj, (cx, cy) in enumerate(chips):
                for pi, (_, rws) in enumerate(other):
                    _remote(pall_ref.at[2 * cx + cy, rws], pall_ref.at[2 * cx + cy, rws],
                            h_send.at[pj * nb + pi], h_recv.at[pj * nb + pi], (x, y, 1 - c)).wait_recv()
            for cp in sum(bulk(), []) + handed:
                cp.wait_send()

    any_spec = pl.BlockSpec(memory_space=pl.ANY)
    in_specs = [pl.BlockSpec((tm, k), lambda s: (u_row(s), 0))]
    in_specs += [pl.BlockSpec((k, tns[g]), lambda s, g=g: (0, rel(s, g) // mts[g])) for g in range(n_g)]
    in_specs += [pl.BlockSpec((1, tns[g]), lambda s, g=g: (0, rel(s, g) // mts[g])) for g in range(n_g)]
    in_specs.append(any_spec)
    out_specs = [pl.BlockSpec((tm, tns[g]), lambda s, g=g: (rel(s, g) % mts[g], rel(s, g) // mts[g])) for g in range(n_g)]
    out_specs.append(any_spec)
    out_shape = [jax.ShapeDtypeStruct((rows[g], ws[g].shape[1]), BF16) for g in range(n_g)]
    out_shape.append(jax.ShapeDtypeStruct((N_CHIPS,) + p_sh.shape, p_sh.dtype))
    return _pallas(
        body, name="proj_all", grid=(n_steps,), in_specs=in_specs, out_specs=out_specs, out_shape=out_shape,
        scratch_shapes=[pltpu.SemaphoreType.DMA((3 * P_ROW_CHUNKS,)) for _ in range(4)],
        compiler_params=_params(("arbitrary",)),
    )(u, *ws, *bs, p_sh)


def matmul_tn(a, b, *, name, t, tn, tt, colsum=False, swap=None):
    m = a.shape[1]
    n = b.shape[1]
    nj, ns = n // tn, t // tt
    n_out = 2 if colsum else 1
    n_sw = 0 if swap is None else len(swap[0])
    n_ex = 0 if swap is None else n_sw + 1

    def body(a_ref, b_ref, *rest):
        o_ref = rest[n_ex]
        cs_ref = rest[n_ex + 1] if colsum else None
        j, s = pl.program_id(0), pl.program_id(1)

        if swap is not None:
            g_refs, sm_ref = rest[:n_sw], rest[n_sw]
            got_refs, sall_ref = rest[n_ex + n_out:n_ex + n_out + n_sw], rest[n_ex + n_out + n_sw]
            a_send, a_recv, s_send, s_recv, l_sem = rest[2 * n_ex + n_out:]

            def copies():
                x, y, c, _ = _place()
                dev = 4 * x + 2 * y + c
                small = [_remote(sm_ref, sall_ref.at[dev], s_send.at[r], s_recv.at[r], peer)
                         for r, peer in enumerate(_all_peers(x, y, c))]
                return (_pair_copies(g_refs, None, got_refs, None, a_send, a_recv) + small,
                        pltpu.make_async_copy(sm_ref, sall_ref.at[dev], l_sem))

            @pl.when((j == 0) & (s == 0))
            def _():
                remote, own = copies()
                own.start()
                for cp in remote:
                    cp.start()

            @pl.when((j == nj - 1) & (s == ns - 1))
            def _():
                remote, own = copies()
                for cp in remote:
                    cp.wait_recv()
                for cp in remote:
                    cp.wait_send()
                own.wait()

        @pl.when(s == 0)
        def _():
            o_ref[...] = jnp.zeros_like(o_ref)
            if colsum:
                cs_ref[...] = jnp.zeros_like(cs_ref)
        o_ref[...] += _tn(a_ref[...], b_ref[...])
        if colsum:
            cs_ref[...] += jnp.sum(b_ref[...].astype(F32), axis=0, keepdims=True)

    in_specs = [pl.BlockSpec((tt, m), lambda j, s: (s, 0)), pl.BlockSpec((tt, tn), lambda j, s: (s, j))]
    out_specs = [pl.BlockSpec((m, tn), lambda j, s: (0, j))]
    out_shape = [jax.ShapeDtypeStruct((m, n), F32)]
    if colsum:
        out_specs.append(pl.BlockSpec((1, tn), lambda j, s: (0, j)))
        out_shape.append(jax.ShapeDtypeStruct((1, n), F32))
    args, scratch = [a, b], []
    if swap is not None:
        gs, sm = swap
        any_spec = pl.BlockSpec(memory_space=pl.ANY)
        in_specs += [any_spec] * n_ex
        out_specs += [any_spec] * n_ex
        out_shape += _pair_got_shapes(gs, None) + [jax.ShapeDtypeStruct((N_DEV,) + sm.shape, F32)]
        args += [*gs, sm]
        scratch = [pltpu.SemaphoreType.DMA((_pair_count(gs, None),)), pltpu.SemaphoreType.DMA((_pair_count(gs, None),)),
                   pltpu.SemaphoreType.DMA((N_DEV - 1,)), pltpu.SemaphoreType.DMA((N_DEV - 1,)),
                   pltpu.SemaphoreType.DMA(())]
    return _pallas(
        body, name=name, grid=(nj, ns), in_specs=in_specs, out_specs=out_specs, out_shape=out_shape,
        scratch_shapes=scratch,
        compiler_params=_params(("parallel" if swap is None else "arbitrary", "arbitrary")),
    )(*args)


def dgrad_norm_bwd(dps, wts, paw, x2, ctx2, dh, scale1, norm_g, *, tm):
    t, tc = x2.shape[0], ctx2.shape[0]
    t_all = t + tc
    n_lat, n_ctx = t // tm, tc // tm
    n_tiles = n_lat + n_ctx
    n_samples = scale1.shape[0] - 1
    tps = n_lat // n_samples
    n_grp = n_samples + 1
    n_g = len(dps)
    whole = [g for g in range(n_g) if dps[g].shape[0] == t_all]
    latent = [g for g in range(n_g) if dps[g].shape[0] != t_all]

    def body(*refs):
        dp_refs, w_refs = refs[:n_g], refs[n_g:2 * n_g]
        (paw_ref, x_ref, c_ref, dh_ref, sc_ref, g_ref, dx_ref, dsh_ref, dsc_ref, dg_ref, rbw_ref,
         du_buf, b_send, b_recv) = refs[2 * n_g:]
        i = pl.program_id(0)

        def exchange():
            x, y, c, chips = _place()
            chunks = _half_chunks(0, (2 * paw_ref.shape[1],), 16, which=(0,))
            return [_remote(paw_ref.at[2 * cx + cy, rows], rbw_ref.at[j, rows],
                            b_send.at[j * N_BULK + k], b_recv.at[j * N_BULK + k], (cx, cy, c))
                    for j, (cx, cy) in enumerate(chips) for k, (_, rows) in enumerate(chunks)]

        @pl.when(i == 0)
        def _():
            for cp in exchange():
                cp.start()

        acc = None
        for g in whole:
            part = _nt(dp_refs[g][...], w_refs[g][...])
            acc = part if acc is None else acc + part
        du_buf[...] = acc

        @pl.when(i < n_lat)
        def _():
            lat = None
            for g in latent:
                part = _nt(dp_refs[g][...], w_refs[g][...])
                lat = part if lat is None else lat + part
            du_buf[...] += lat

        duv = du_buf[...]
        xv = jnp.where(i < n_lat, x_ref[...], c_ref[...])
        rs = lax.rsqrt(jnp.mean(xv * xv, axis=-1, keepdims=True) + EPS)
        xh = xv * rs
        n = xh * g_ref[...]
        dn = duv * sc_ref[0]
        dxh = dn * g_ref[...]
        dx = rs * (dxh - xh * jnp.mean(dxh * xh, axis=-1, keepdims=True))

        @pl.when(i < n_lat)
        def _():
            dx_ref[...] = dx + dh_ref[...]

        @pl.when((i % tps == 0) & (i <= n_lat))
        def _():
            dsh_ref[...] = jnp.zeros_like(dsh_ref)
            dsc_ref[...] = jnp.zeros_like(dsc_ref)

        @pl.when(i == 0)
        def _():
            dg_ref[...] = jnp.zeros_like(dg_ref)

        dsh_ref[0] += jnp.sum(duv, axis=0, keepdims=True)
        dsc_ref[0] += jnp.sum(duv * n, axis=0, keepdims=True)
        dg_ref[...] += jnp.sum(dn * xh, axis=0, keepdims=True)

        @pl.when(i == n_tiles - 1)
        def _():
            for cp in exchange():
                cp.wait_recv()
            for cp in exchange():
                cp.wait_send()

    lat = lambda i: (jnp.minimum(i, n_lat - 1), 0)
    grp = lambda i: (jnp.minimum(i // tps, n_samples), 0, 0)
    in_specs = []
    for g, dp in enumerate(dps):
        nrow = dp.shape[0] // tm
        in_specs.append(pl.BlockSpec((tm, dp.shape[1]), lambda i, nrow=nrow: (jnp.minimum(i, nrow - 1), 0)))
    for w in wts:
        in_specs.append(pl.BlockSpec(w.shape, lambda i: (0, 0), pipeline_mode=pl.Buffered(1)))
    any_spec = pl.BlockSpec(memory_space=pl.ANY)
    in_specs += [any_spec,
                 pl.BlockSpec((tm, D), lat), pl.BlockSpec((tm, D), lambda i: (jnp.maximum(i - n_lat, 0), 0)),
                 pl.BlockSpec((tm, D), lat), pl.BlockSpec((1, 1, D), grp), pl.BlockSpec((1, D), lambda i: (0, 0))]
    return _pallas(
        body, name="dgrad_norm_bwd", grid=(n_tiles,), in_specs=in_specs,
        out_specs=[pl.BlockSpec((tm, D), lat), pl.BlockSpec((1, 1, D), grp), pl.BlockSpec((1, 1, D), grp),
                   pl.BlockSpec((1, D), lambda i: (0, 0)), any_spec],
        out_shape=[jax.ShapeDtypeStruct((t, D), F32), jax.ShapeDtypeStruct((n_grp, 1, D), F32),
                   jax.ShapeDtypeStruct((n_grp, 1, D), F32), jax.ShapeDtypeStruct((1, D), F32),
                   jax.ShapeDtypeStruct((3,) + paw.shape[1:], paw.dtype)],
        scratch_shapes=[pltpu.VMEM((tm, D), F32), pltpu.SemaphoreType.DMA((3 * N_BULK,)),
                        pltpu.SemaphoreType.DMA((3 * N_BULK,))],
        compiler_params=_params(("arbitrary",)),
    )(*dps, *wts, paw, x2, ctx2, dh, scale1, norm_g)


TM_NORM = 512


def norm_mod_fwd(x2, ctx2, scale1, shift, norm_g):
    t = x2.shape[0]
    n_lat = t // TM_NORM
    assert ctx2.shape[0] == TM_NORM
    n_samples = scale1.shape[0] - 1
    tps = n_lat // n_samples

    def body(x_ref, c_ref, sc_ref, sh_ref, g_ref, u_ref):
        i = pl.program_id(0)
        xv = jnp.where(i < n_lat, x_ref[...], c_ref[...])
        rs = lax.rsqrt(jnp.mean(xv * xv, axis=-1, keepdims=True) + EPS)
        u = xv * rs * g_ref[...] * sc_ref[0] + sh_ref[0]
        u_ref[...] = u.astype(u_ref.dtype)

    grp = lambda i: (jnp.minimum(i // tps, n_samples), 0, 0)
    return _pallas(
        body, name="norm_mod_fwd", grid=(n_lat + 1,),
        in_specs=[pl.BlockSpec((TM_NORM, D), lambda i: (jnp.minimum(i, n_lat - 1), 0)),
                  pl.BlockSpec((TM_NORM, D), lambda i: (0, 0)),
                  pl.BlockSpec((1, 1, D), grp), pl.BlockSpec((1, 1, D), grp),
                  pl.BlockSpec((1, D), lambda i: (0, 0))],
        out_specs=pl.BlockSpec((TM_NORM, D), lambda i: (i, 0)),
        out_shape=jax.ShapeDtypeStruct((t + TM_NORM, D), BF16),
        compiler_params=_params(("parallel",)),
    )(x2, ctx2, scale1, shift, norm_g)


CONV_CB = 256
CONV_NCB = D // CONV_CB
H_OFF = 16


H_CB = 128
H_SPAN = GRID_W + 2 * H_OFF - 8


def _conv_scratch(vertical):
    if vertical:
        return [pltpu.VMEM((GRID_H + 2 * CONV_PAD, GRID_W, CONV_CB), F32)]
    return [pltpu.VMEM((GRID_H, GRID_W + 2 * H_OFF, H_CB), F32), pltpu.VMEM((7, GRID_H, H_SPAN, H_CB), F32)]


def _conv_fill(bufs, img, vertical):
    pad_ref = bufs[0]
    pad_ref[...] = jnp.zeros_like(pad_ref)
    if vertical:
        pad_ref[pl.ds(CONV_PAD, GRID_H)] = img
        return
    pad_ref[:, pl.ds(H_OFF, GRID_W), :] = img

    def shift(r, carry):
        for s in range(1, 8):
            bufs[1][s - 1, r] = pad_ref[r, pl.ds(s, H_SPAN), :]
        return carry

    lax.fori_loop(0, GRID_H, shift, 0)


def _conv_window(bufs, k, vertical, r, w0=0, nw=GRID_W, lanes=slice(None)):
    if vertical:
        return bufs[0][r + k, pl.ds(w0, nw), lanes]
    off = H_OFF - CONV_PAD + k
    if off % 8 == 0:
        return bufs[0][r, pl.ds(off + w0, nw), lanes]
    return bufs[1][off % 8 - 1, r, pl.ds(off - off % 8 + w0, nw), lanes]


def _conv_col_blocks(vertical):
    if vertical:
        return [pl.ds(0, CONV_CB)]
    return [pl.ds(i * H_CB, H_CB) for i in range(CONV_CB // H_CB)]


def _rows(r):
    return pl.ds(pl.multiple_of(r * GRID_W, GRID_W), GRID_W)


def conv_fwd(p1, conv_w, conv_b, n_samples):
    t = n_samples * SEQ

    def make(vertical, prev):
        n_buf = len(_conv_scratch(vertical))

        def body(gv_ref, gg_ref, w_ref, b_ref, *rest):
            o_ref, bufs = rest[-1 - n_buf], rest[-n_buf:]
            for cols in _conv_col_blocks(vertical):
                a = gv_ref[:, cols].astype(F32) * _sigmoid(gg_ref[:, cols].astype(F32))
                _conv_fill(bufs, a.reshape(GRID_H, GRID_W, a.shape[-1]), vertical)

                def row(r, carry, cols=cols):
                    acc = jnp.zeros((GRID_W, cols.size), F32) + b_ref[:, cols]
                    for k in range(CONV_K):
                        acc = acc + _conv_window(bufs, k, vertical, r) * w_ref[pl.ds(k, 1), cols]
                    o_ref[_rows(r), cols] = acc
                    return carry

                lax.fori_loop(0, GRID_H, row, 0)

        cb0 = CONV_NCB // 2 if vertical else 0
        in_specs = [pl.BlockSpec((SEQ, CONV_CB), lambda b, j: (b, 2 * (cb0 + j))),
                    pl.BlockSpec((SEQ, CONV_CB), lambda b, j: (b, 2 * (cb0 + j) + 1)),
                    pl.BlockSpec((CONV_K + 1, CONV_CB), lambda b, j: (0, cb0 + j)),
                    pl.BlockSpec((1, CONV_CB), lambda b, j: (0, cb0 + j))]
        args = [p1, p1, conv_w, conv_b]
        aliases = {}
        if prev is not None:
            in_specs.append(pl.BlockSpec(memory_space=pl.ANY))
            args.append(prev)
            aliases = {4: 0}
        return _pallas(
            body, name="conv_fwd_v" if vertical else "conv_fwd_h", grid=(n_samples, CONV_NCB // 2),
            in_specs=in_specs,
            out_specs=pl.BlockSpec((SEQ, CONV_CB), lambda b, j: (b, cb0 + j)),
            out_shape=jax.ShapeDtypeStruct((t, D), F32),
            scratch_shapes=_conv_scratch(vertical),
            input_output_aliases=aliases,
            compiler_params=_params(("parallel", "parallel")),
        )(*args)

    return make(True, make(False, None))


def conv_bwd(p1, daconv, conv_w, n_samples):
    t = n_samples * SEQ

    def make(vertical, prev):
        n_buf = len(_conv_scratch(vertical))

        def body(gv_ref, gg_ref, dy_ref, w_ref, *rest):
            dp_ref, dw_ref, db_ref = rest[-3 - 2 * n_buf - 1:-2 * n_buf - 1]
            a_bufs, d_bufs, da_ref = rest[-2 * n_buf - 1:-n_buf - 1], rest[-n_buf - 1:-1], rest[-1]
            for cols in _conv_col_blocks(vertical):
                width = cols.size
                gv = gv_ref[:, cols].astype(F32)
                sg = _sigmoid(gg_ref[:, cols].astype(F32))
                _conv_fill(a_bufs, (gv * sg).reshape(GRID_H, GRID_W, width), vertical)
                _conv_fill(d_bufs, dy_ref[:, cols].reshape(GRID_H, GRID_W, width), vertical)

                def row(r, carry, cols=cols, width=width):
                    acc = jnp.zeros((GRID_W, width), F32)
                    for k in range(CONV_K):
                        acc = acc + _conv_window(d_bufs, CONV_K - 1 - k, vertical, r) * w_ref[pl.ds(k, 1), cols]
                    da_ref[_rows(r), cols] = acc
                    return carry

                lax.fori_loop(0, GRID_H, row, 0)
                da = da_ref[:, cols]
                dp_ref[:, pl.ds(cols.start, width)] = (da * sg).astype(dp_ref.dtype)
                dp_ref[:, pl.ds(CONV_CB + cols.start, width)] = (da * gv * sg * (1.0 - sg)).astype(dp_ref.dtype)

                for lb in range(width // 128):
                    lanes = pl.ds(lb * 128, 128)
                    dy_lanes = pl.ds(cols.start + lb * 128, 128)

                    def wrow(r, accs, lanes=lanes, dy_lanes=dy_lanes):
                        for w0 in range(0, GRID_W, 8):
                            dyv = dy_ref[pl.ds(pl.multiple_of(r * GRID_W, GRID_W) + w0, 8), dy_lanes]
                            accs = tuple(accs[k] + _conv_window(a_bufs, k, vertical, r, w0, 8, lanes) * dyv
                                         for k in range(CONV_K))
                        return accs

                    accs = lax.fori_loop(0, GRID_H, wrow, tuple(jnp.zeros((8, 128), F32) for _ in range(CONV_K)))
                    for k in range(CONV_K):
                        dw_ref[0, pl.ds(k, 1), dy_lanes] = jnp.sum(accs[k], axis=0, keepdims=True)
            dw_ref[0, pl.ds(CONV_K, 1), :] = jnp.zeros((1, CONV_CB), F32)
            db_ref[0] = jnp.sum(dy_ref[...], axis=0, keepdims=True)

        cb0 = CONV_NCB // 2 if vertical else 0
        in_specs = [pl.BlockSpec((SEQ, CONV_CB), lambda b, j: (b, 2 * (cb0 + j))),
                    pl.BlockSpec((SEQ, CONV_CB), lambda b, j: (b, 2 * (cb0 + j) + 1)),
                    pl.BlockSpec((SEQ, CONV_CB), lambda b, j: (b, cb0 + j)),
                    pl.BlockSpec((CONV_K + 1, CONV_CB), lambda b, j: (0, cb0 + j))]
        args = [p1, p1, daconv, conv_w]
        aliases = {}
        if prev is not None:
            in_specs += [pl.BlockSpec(memory_space=pl.ANY)] * 3
            args += list(prev)
            aliases = {4: 0, 5: 1, 6: 2}
        return _pallas(
            body, name="conv_bwd_v" if vertical else "conv_bwd_h", grid=(n_samples, CONV_NCB // 2),
            in_specs=in_specs,
            out_specs=[pl.BlockSpec((SEQ, 2 * CONV_CB), lambda b, j: (b, cb0 + j)),
                       pl.BlockSpec((1, CONV_K + 1, CONV_CB), lambda b, j: (b, 0, cb0 + j)),
                       pl.BlockSpec((1, 1, CONV_CB), lambda b, j: (b, 0, cb0 + j))],
            out_shape=[jax.ShapeDtypeStruct((t, 2 * D), BF16),
                       jax.ShapeDtypeStruct((n_samples, CONV_K + 1, D), F32),
                       jax.ShapeDtypeStruct((n_samples, 1, D), F32)],
            scratch_shapes=_conv_scratch(vertical) + _conv_scratch(vertical) + [pltpu.VMEM((SEQ, CONV_CB), F32)],
            input_output_aliases=aliases,
            compiler_params=_params(("parallel", "parallel")),
        )(*args)

    return make(True, make(False, None))


TM_EW = 512


def ln_gate_proj(aconv, z, ln_g, ln_b, conv_proj):
    t = aconv.shape[0]

    def body(a_ref, z_ref, g_ref, b_ref, w_ref, o_ref, y_ref):
        a = a_ref[...]
        mu = jnp.mean(a, axis=-1, keepdims=True)
        xc = a - mu
        rstd = lax.rsqrt(jnp.mean(xc * xc, axis=-1, keepdims=True) + EPS)
        l = xc * rstd * g_ref[...] + b_ref[...]
        ac = _b16(_silu(l) * _silu(z_ref[...].astype(F32)))
        o_ref[...] = ac
        y_ref[...] = _nn(ac, w_ref[...]).astype(y_ref.dtype)

    row = pl.BlockSpec((TM_OUT, D), lambda i: (i, 0))
    vec = pl.BlockSpec((1, D), lambda i: (0, 0))
    return _pallas(
        body, name="ln_gate_proj", grid=(t // TM_OUT,),
        in_specs=[row, row, vec, vec, pl.BlockSpec((D, D), lambda i: (0, 0))], out_specs=[row, row],
        out_shape=[jax.ShapeDtypeStruct((t, D), BF16), jax.ShapeDtypeStruct((t, D), BF16)],
        compiler_params=_params(("parallel",)),
    )(aconv, z, ln_g, ln_b, conv_proj)


def ln_gate_bwd(aconv, z, dyc, conv_proj, ln_g, ln_b, ac):
    t = aconv.shape[0]

    def body(a_ref, z_ref, d_ref, w_ref, g_ref, b_ref, ac_ref, da_ref, dz_ref, dg_ref, db_ref, gw_ref):
        @pl.when(pl.program_id(0) == 0)
        def _():
            gw_ref[...] = jnp.zeros_like(gw_ref)

        gw_ref[...] += _tn(ac_ref[...], d_ref[...])
        a = a_ref[...]
        zv = z_ref[...].astype(F32)
        dac_v = _nt(d_ref[...], w_ref[...])
        mu = jnp.mean(a, axis=-1, keepdims=True)
        xc = a - mu
        rstd = lax.rsqrt(jnp.mean(xc * xc, axis=-1, keepdims=True) + EPS)
        xh = xc * rstd
        l = xh * g_ref[...] + b_ref[...]
        dz_ref[...] = (dac_v * _silu(l) * _dsilu(zv)).astype(dz_ref.dtype)
        dl = dac_v * _silu(zv) * _dsilu(l)
        dxh = dl * g_ref[...]
        da_ref[...] = rstd * (dxh - jnp.mean(dxh, axis=-1, keepdims=True)
                              - xh * jnp.mean(dxh * xh, axis=-1, keepdims=True))

        @pl.when(pl.program_id(0) == 0)
        def _():
            dg_ref[...] = jnp.zeros_like(dg_ref)
            db_ref[...] = jnp.zeros_like(db_ref)

        dg_ref[...] += jnp.sum(dl * xh, axis=0, keepdims=True)
        db_ref[...] += jnp.sum(dl, axis=0, keepdims=True)

    row = pl.BlockSpec((TM_EW, D), lambda i: (i, 0))
    vec = pl.BlockSpec((1, D), lambda i: (0, 0))
    return _pallas(
        body, name="ln_gate_bwd", grid=(t // TM_EW,),
        in_specs=[row, row, row, pl.BlockSpec((D, D), lambda i: (0, 0)), vec, vec, row],
        out_specs=[row, row, vec, vec, pl.BlockSpec((D, D), lambda i: (0, 0))],
        out_shape=[jax.ShapeDtypeStruct((t, D), F32), jax.ShapeDtypeStruct((t, D), BF16),
                   jax.ShapeDtypeStruct((1, D), F32), jax.ShapeDtypeStruct((1, D), F32),
                   jax.ShapeDtypeStruct((D, D), F32)],
        compiler_params=_params(("arbitrary",)),
    )(aconv, z, dyc, conv_proj, ln_g, ln_b, ac)


TM_PREP = 256
PREP_LAT = SEQ // TM_PREP
PREP_ALL = SEQ_ALL // TM_PREP


def _chunk_tri(n, upper):
    r = lax.broadcasted_iota(jnp.int32, (n, n), 0)
    c = lax.broadcasted_iota(jnp.int32, (n, n), 1)
    same = (r // CHUNK) == (c // CHUNK)
    keep = (c >= r) if upper else (c <= r)
    return jnp.where(same & keep, 1.0, 0.0).astype(F32)


def _split3(v):
    hi = v.astype(BF16)
    r1 = v - hi.astype(F32)
    mid = r1.astype(BF16)
    lo = (r1 - mid.astype(F32)).astype(BF16)
    return jnp.stack([hi, mid, lo])


def _chunk_sums(v, upper):
    tri = _chunk_tri(v.shape[0], upper).astype(BF16)
    pieces = _split3(v)
    return (_nn(tri, pieces[0]) + _nn(tri, pieces[1])) + _nn(tri, pieces[2])


def _gate_logits(ab, up3_ref, bias_ref):
    assert ab.dtype == BF16
    return ((_nn(ab, up3_ref[0]) + _nn(ab, up3_ref[1])) + _nn(ab, up3_ref[2])) + bias_ref[...]


def _prep_tile_maps(n_samples):
    n_lat = n_samples * PREP_LAT

    def seq_map(i):
        return jnp.where(i < n_lat, i // PREP_LAT, i - n_lat), jnp.where(i < n_lat, i % PREP_LAT, PREP_LAT)

    return n_lat, seq_map


def gla_prep_fwd(p3, upf, upb, bias_f, bias_b, n_samples):
    n_lat, seq_map = _prep_tile_maps(n_samples)
    n_tiles = n_lat + n_samples

    def body(v_ref, q_ref, k_ref, ab_ref, upf_ref, upb_ref, bf_ref, bb_ref, qo, ko, vo, cf, cb):
        i = pl.program_id(0)
        qo[0] = jnp.where(i < n_lat, q_ref[...].astype(F32) * Q_SCALE, 0.0)
        ko[0] = k_ref[...]
        vo[0] = v_ref[...]
        ab = ab_ref[...]
        gf = _log_sigmoid(_gate_logits(ab, upf_ref, bf_ref)) * (1.0 / GATE_TAU)
        gb = _log_sigmoid(_gate_logits(ab, upb_ref, bb_ref)) * (1.0 / GATE_TAU)
        cf[0] = _chunk_sums(gf, False)
        cb[0] = _chunk_sums(gb, True)

    def o_spec(w):
        return pl.BlockSpec((1, TM_PREP, w), lambda i: (*seq_map(i), 0))

    full = lambda shape: pl.BlockSpec(shape, lambda i: (0,) * len(shape))
    return _pallas(
        body, name="gla_prep_fwd", grid=(n_tiles,),
        in_specs=[pl.BlockSpec((TM_PREP, 1024), lambda i: (i, O3_V // 1024)),
                  pl.BlockSpec((TM_PREP, 512), lambda i: (i, O3_Q // 512)),
                  pl.BlockSpec((TM_PREP, 512), lambda i: (i, O3_K // 512)),
                  pl.BlockSpec((TM_PREP, 128), lambda i: (i, O3_AB // 128)),
                  full((3, 128, GLA_DK)), full((3, 128, GLA_DK)), full((1, GLA_DK)), full((1, GLA_DK))],
        out_specs=[o_spec(GLA_DK), o_spec(GLA_DK), o_spec(D), o_spec(GLA_DK), o_spec(GLA_DK)],
        out_shape=[jax.ShapeDtypeStruct((n_samples, SEQ_ALL, GLA_DK), F32),
                   jax.ShapeDtypeStruct((n_samples, SEQ_ALL, GLA_DK), p3.dtype),
                   jax.ShapeDtypeStruct((n_samples, SEQ_ALL, D), p3.dtype),
                   jax.ShapeDtypeStruct((n_samples, SEQ_ALL, GLA_DK), F32),
                   jax.ShapeDtypeStruct((n_samples, SEQ_ALL, GLA_DK), F32)],
        compiler_params=_params(("parallel",)),
    )(p3, p3, p3, p3, upf, upb, bias_f, bias_b)


def gla_prep_bwd(p3, dq_f, dq_b, dk_f, dk_b, dv_f, dv_b, dc_f, dc_b, upf, upb, bias_f, bias_b, n_samples):
    n_lat, seq_map = _prep_tile_maps(n_samples)
    n_tiles = n_lat + n_samples

    def body(ab_ref, dqf, dqb, dkf, dkb, dvf, dvb, dcf, dcb, upf_ref, upb_ref, bf_ref, bb_ref,
             dp_ref, duf_ref, dub_ref, dbf_ref, dbb_ref):
        i = pl.program_id(0)
        both = lambda a, b: a[0].astype(F32) + b[0].astype(F32)
        dp_ref[:, pl.ds(O3_V, D)] = both(dvf, dvb).astype(dp_ref.dtype)
        dq = jnp.where(i < n_lat, both(dqf, dqb) * Q_SCALE, 0.0)
        dp_ref[:, pl.ds(O3_Q, GLA_DK)] = dq.astype(dp_ref.dtype)
        dp_ref[:, pl.ds(O3_K, GLA_DK)] = both(dkf, dkb).astype(dp_ref.dtype)
        ab = ab_ref[...]
        zf = _gate_logits(ab, upf_ref, bf_ref)
        zb = _gate_logits(ab, upb_ref, bb_ref)
        dgf = _chunk_sums(dcf[0], True)
        dgb = _chunk_sums(dcb[0], False)
        dzf = _b16(dgf * (1.0 / GATE_TAU) * _sigmoid(-zf))
        dzb = _b16(dgb * (1.0 / GATE_TAU) * _sigmoid(-zb))
        dab = _nt(dzf, upf_ref[0]) + _nt(dzb, upb_ref[0])
        dp_ref[:, pl.ds(O3_AB, 128)] = dab.astype(dp_ref.dtype)

        @pl.when(i == 0)
        def _():
            duf_ref[...] = jnp.zeros_like(duf_ref)
            dub_ref[...] = jnp.zeros_like(dub_ref)
            dbf_ref[...] = jnp.zeros_like(dbf_ref)
            dbb_ref[...] = jnp.zeros_like(dbb_ref)

        duf_ref[...] += _tn(ab, dzf)
        dub_ref[...] += _tn(ab, dzb)
        dbf_ref[...] += jnp.sum(dzf.astype(F32), axis=0, keepdims=True)
        dbb_ref[...] += jnp.sum(dzb.astype(F32), axis=0, keepdims=True)

    def s_spec(w):
        return pl.BlockSpec((1, TM_PREP, w), lambda i: (*seq_map(i), 0))

    full = lambda shape: pl.BlockSpec(shape, lambda i: (0,) * len(shape))
    return _pallas(
        body, name="gla_prep_bwd", grid=(n_tiles,),
        in_specs=[pl.BlockSpec((TM_PREP, 128), lambda i: (i, O3_AB // 128)),
                  s_spec(GLA_DK), s_spec(GLA_DK), s_spec(GLA_DK), s_spec(GLA_DK), s_spec(D), s_spec(D),
                  s_spec(GLA_DK), s_spec(GLA_DK),
                  full((3, 128, GLA_DK)), full((3, 128, GLA_DK)), full((1, GLA_DK)), full((1, GLA_DK))],
        out_specs=[pl.BlockSpec((TM_PREP, W3), lambda i: (i, 0)),
                   full((128, GLA_DK)), full((128, GLA_DK)), full((1, GLA_DK)), full((1, GLA_DK))],
        out_shape=[jax.ShapeDtypeStruct((n_tiles * TM_PREP, W3), BF16),
                   jax.ShapeDtypeStruct((128, GLA_DK), F32), jax.ShapeDtypeStruct((128, GLA_DK), F32),
                   jax.ShapeDtypeStruct((1, GLA_DK), F32), jax.ShapeDtypeStruct((1, GLA_DK), F32)],
        compiler_params=_params(("arbitrary",)),
    )(p3, dq_f, dq_b, dk_f, dk_b, dv_f, dv_b, dc_f, dc_b, upf, upb, bias_f, bias_b)


def _sub_blocks(rev):
    if NSUB == 1:
        return [((0, CHUNK), CHUNK // 2, (0, CHUNK))]
    out = []
    for s in range(NSUB):
        rows = (s * SUB, SUB)
        if rev:
            ref = (s + 1) * SUB if s < NSUB - 1 else None
            cols = (s * SUB, CHUNK - s * SUB)
        else:
            ref = s * SUB - 1 if s > 0 else None
            cols = (0, (s + 1) * SUB)
        out.append((rows, ref, cols))
    return out


def _sub_mask(rows, cols, rev):
    r = rows[0] + lax.broadcasted_iota(jnp.int32, (rows[1], cols[1]), 0)
    c = cols[0] + lax.broadcasted_iota(jnp.int32, (rows[1], cols[1]), 1)
    return (c >= r) if rev else (c <= r)


def _sub_operands(qc, kc, cc, rows, ref, cols):
    cref = jnp.zeros((1, HEAD_K), F32) if ref is None else cc[ref:ref + 1]
    eq = jnp.exp(cc[rows[0]:rows[0] + rows[1]] - cref)
    ek = jnp.exp(cref - cc[cols[0]:cols[0] + cols[1]])
    qs = qc[rows[0]:rows[0] + rows[1]] * eq
    kk = kc[cols[0]:cols[0] + cols[1]] * ek
    return qs, kk, eq, ek


SCAN_ROWS = 256
SCAN_CHUNKS = SCAN_ROWS // CHUNK
SCAN_STEPS = SEQ_ALL // SCAN_ROWS
LAT_BLOCKS = SEQ // SCAN_ROWS


def _scan_block(t, rev):
    if rev:
        return SCAN_STEPS - 1 - t
    return jnp.where(t == 0, SCAN_STEPS - 1, t - 1)


def _scan_lat_block(t, rev):
    if rev:
        return jnp.minimum(SCAN_STEPS - 1 - t, LAT_BLOCKS - 1)
    return jnp.maximum(t - 1, 0)


def _head_cols(h):
    return pl.ds(h * HEAD_K, HEAD_K), pl.ds(h * HEAD_V, HEAD_V)


def gla_scan_fwd(q, k, v, cum, *, rev, name):
    n = q.shape[0]

    def body(q_ref, k_ref, v_ref, c_ref, o_ref, s_ref, sfin_ref, st):
        t = pl.program_id(1)

        @pl.when(t == 0)
        def _():
            st[...] = jnp.zeros_like(st)

        def chunk(j, carry):
            lj = SCAN_CHUNKS - 1 - j if rev else j
            r0 = lj * CHUNK
            rws = pl.ds(r0, CHUNK)
            for h in range(HEADS):
                kcols, vcols = _head_cols(h)
                qc, kc, cc = q_ref[0, rws, kcols], k_ref[0, rws, kcols], c_ref[0, rws, kcols]
                vc = v_ref[0, rws, vcols]
                s_in = st[h]
                s_ref[0, h, j] = _b16(s_in)
                edge = cc[0:1] if rev else cc[CHUNK - 1:CHUNK]
                ke = kc * jnp.exp(edge - cc)
                st[h] = s_in * jnp.exp(edge) + _tn(_b16(vc), _b16(ke))
                o_inter = _nt(_b16(qc * jnp.exp(cc)), _b16(s_in))
                vb = _b16(vc)
                for rows, ref, cols in _sub_blocks(rev):
                    qs, kk, _, _ = _sub_operands(qc, kc, cc, rows, ref, cols)
                    a = jnp.where(_sub_mask(rows, cols, rev), _nt(_b16(qs), _b16(kk)), 0.0)
                    o_s = _nn(_b16(a), vb[cols[0]:cols[0] + cols[1]])
                    o_ref[0, pl.ds(r0 + rows[0], rows[1]), vcols] = _b16(o_inter[rows[0]:rows[0] + rows[1]] + o_s)
            return carry

        for j in range(SCAN_CHUNKS):
            chunk(j, 0)

        @pl.when(t == SCAN_STEPS - 1)
        def _():
            sfin_ref[0] = st[...]

    def spec(w):
        return pl.BlockSpec((1, SCAN_ROWS, w), lambda b, t: (b, _scan_block(t, rev), 0))

    return _pallas(
        body, name=name, grid=(n, SCAN_STEPS),
        in_specs=[spec(GLA_DK), spec(GLA_DK), spec(D), spec(GLA_DK)],
        out_specs=[pl.BlockSpec((1, SCAN_ROWS, D), lambda b, t: (b, _scan_lat_block(t, rev), 0)),
                   pl.BlockSpec((1, HEADS, SCAN_CHUNKS, HEAD_V, HEAD_K), lambda b, t: (b, 0, t, 0, 0)),
                   pl.BlockSpec((1, HEADS, HEAD_V, HEAD_K), lambda b, t: (b, 0, 0, 0))],
        out_shape=[jax.ShapeDtypeStruct((n, SEQ, D), BF16),
                   jax.ShapeDtypeStruct((n, HEADS, NCHUNK, HEAD_V, HEAD_K), BF16),
                   jax.ShapeDtypeStruct((n, HEADS, HEAD_V, HEAD_K), F32)],
        scratch_shapes=[pltpu.VMEM((HEADS, HEAD_V, HEAD_K), F32)],
        compiler_params=_params(("parallel", "arbitrary")),
    )(q, k, v, cum)


def gla_scan_bwd(q, k, v, cum, s_all, s_fin, do, *, rev, name, rider=None):
    n = q.shape[0]

    def body(q_ref, k_ref, v_ref, c_ref, s_ref, sfin_ref, do_ref, *rest):
        if rider is not None:
            ride_in, rest = rest[0], rest[1:]
        dq_ref, dk_ref, dv_ref, dc_ref = rest[:4]
        if rider is not None:
            ride_out, rest = rest[4], rest[:4] + rest[5:]
        dst, s_next, dq_acc, dk_acc, dv_acc = rest[4:9]
        t = SCAN_STEPS - 1 - pl.program_id(1)

        if rider is not None:
            def copies():
                send, recv = rest[9], rest[10]
                if rider[0] == "swap":
                    return _pair_copies([], ride_in, [], ride_out, send, recv)
                x, y, c, chips = _place()
                return [_remote(ride_in.at[2 * cx + cy, rows], ride_out.at[pj, rows], send.at[pj * P_ROW_CHUNKS + pi],
                                recv.at[pj * P_ROW_CHUNKS + pi], (cx, cy, c))
                        for pj, (cx, cy) in enumerate(chips)
                        for pi, (_, rows) in enumerate(_half_chunks(0, (0, 2 * ride_in.shape[1]), 16, which=(1,)))]

            @pl.when((pl.program_id(0) == 0) & (pl.program_id(1) == 0))
            def _():
                for cp in copies():
                    cp.start()

            @pl.when((pl.program_id(0) == n - 1) & (pl.program_id(1) == SCAN_STEPS - 1))
            def _():
                for cp in copies():
                    cp.wait_recv()
                for cp in copies():
                    cp.wait_send()

        @pl.when(pl.program_id(1) == 0)
        def _():
            dst[...] = jnp.zeros_like(dst)
            s_next[...] = sfin_ref[0]

        def chunk(jj, carry):
            j = SCAN_CHUNKS - 1 - jj
            lj = SCAN_CHUNKS - 1 - j if rev else j
            rws = pl.ds(lj * CHUNK, CHUNK)
            for h in range(HEADS):
                kcols, vcols = _head_cols(h)
                qc, kc, cc = q_ref[0, rws, kcols], k_ref[0, rws, kcols], c_ref[0, rws, kcols]
                vc = v_ref[0, rws, vcols]
                doc = jnp.where(t > 0, do_ref[0, rws, vcols], 0.0)
                s_in = s_ref[0, h, j]
                s_out = s_next[h]
                ds_out = dst[h]
                edge = cc[0:1] if rev else cc[CHUNK - 1:CHUNK]
                e_q = jnp.exp(cc)
                e_k = jnp.exp(edge - cc)
                dob = _b16(doc)
                dsb = _b16(ds_out)
                dst[h] = ds_out * jnp.exp(edge) + _tn(dob, _b16(qc * e_q))
                s_next[h] = s_in.astype(F32)
                dq_acc[h] = e_q * _nn(dob, s_in)
                dk_acc[h] = e_k * _nn(_b16(vc), dsb)
                dv_acc[h] = _nt(_b16(kc * e_k), dsb)
                vb = _b16(vc)
                for rows, ref, cols in _sub_blocks(rev):
                    qs, kk, eq, ek = _sub_operands(qc, kc, cc, rows, ref, cols)
                    mask = _sub_mask(rows, cols, rev)
                    rsl = slice(rows[0], rows[0] + rows[1])
                    csl = pl.ds(cols[0], cols[1])
                    qsb, kkb = _b16(qs), _b16(kk)
                    a = jnp.where(mask, _nt(qsb, kkb), 0.0)
                    da = _b16(jnp.where(mask, _nt(dob[rsl], vb[cols[0]:cols[0] + cols[1]]), 0.0))
                    dq_acc[h, pl.ds(rows[0], rows[1]), :] += _nn(da, kkb) * eq
                    dk_acc[h, csl, :] += _tn(da, qsb) * ek
                    dv_acc[h, csl, :] += _tn(_b16(a), dob[rsl])
                dq = dq_acc[h]
                dk = dk_acc[h]
                dc = qc * dq - kc * dk
                bnd = jnp.sum(ds_out * s_out, axis=0, keepdims=True)
                edge_row = 0 if rev else CHUNK - 1
                is_edge = lax.broadcasted_iota(jnp.int32, (CHUNK, HEAD_K), 0) == edge_row
                dq_ref[0, rws, kcols] = _b16(dq)
                dk_ref[0, rws, kcols] = _b16(dk)
                dv_ref[0, rws, vcols] = _b16(dv_acc[h])
                dc_ref[0, rws, kcols] = dc + jnp.where(is_edge, bnd, 0.0)
            return carry

        for jj in range(SCAN_CHUNKS):
            chunk(jj, 0)

    def step_of(u):
        return SCAN_STEPS - 1 - u

    def spec(w):
        return pl.BlockSpec((1, SCAN_ROWS, w), lambda b, u: (b, _scan_block(step_of(u), rev), 0))

    in_specs = [spec(GLA_DK), spec(GLA_DK), spec(D), spec(GLA_DK),
                pl.BlockSpec((1, HEADS, SCAN_CHUNKS, HEAD_V, HEAD_K), lambda b, u: (b, 0, step_of(u), 0, 0)),
                pl.BlockSpec((1, HEADS, HEAD_V, HEAD_K), lambda b, u: (b, 0, 0, 0)),
                pl.BlockSpec((1, SCAN_ROWS, D), lambda b, u: (b, _scan_lat_block(step_of(u), rev), 0))]
    out_specs = [spec(GLA_DK), spec(GLA_DK), spec(D), spec(GLA_DK)]
    out_shape = [jax.ShapeDtypeStruct((n, SEQ_ALL, GLA_DK), BF16), jax.ShapeDtypeStruct((n, SEQ_ALL, GLA_DK), BF16),
                 jax.ShapeDtypeStruct((n, SEQ_ALL, D), BF16), jax.ShapeDtypeStruct((n, SEQ_ALL, GLA_DK), F32)]
    scratch = [pltpu.VMEM((HEADS, HEAD_V, HEAD_K), F32), pltpu.VMEM((HEADS, HEAD_V, HEAD_K), F32),
               pltpu.VMEM((HEADS, CHUNK, HEAD_K), F32), pltpu.VMEM((HEADS, CHUNK, HEAD_K), F32),
               pltpu.VMEM((HEADS, CHUNK, HEAD_V), F32)]
    args = [q, k, v, cum, s_all, s_fin, do]
    if rider is not None:
        kind, arr = rider
        any_spec = pl.BlockSpec(memory_space=pl.ANY)
        in_specs.append(any_spec)
        out_specs.append(any_spec)
        args.append(arr)
        if kind == "swap":
            out_shape += _pair_got_shapes([], arr)
            n_cp = _pair_count([], arr)
        else:
            out_shape.append(jax.ShapeDtypeStruct((3,) + arr.shape[1:], arr.dtype))
            n_cp = 3 * P_ROW_CHUNKS
        scratch += [pltpu.SemaphoreType.DMA((n_cp,)), pltpu.SemaphoreType.DMA((n_cp,))]
    return _pallas(
        body, name=name, grid=(n, SCAN_STEPS), in_specs=in_specs, out_specs=out_specs, out_shape=out_shape,
        scratch_shapes=scratch,
        compiler_params=_params(("parallel" if rider is None else "arbitrary", "arbitrary")),
    )(*args)


def gla_out_proj(o_f, o_b, r, gnorm, gla_proj):
    n = o_f.shape[0]
    tiles = SEQ // TM_OUT

    def body(of_ref, ob_ref, r_ref, g_ref, w_ref, og_ref, y_ref):
        for h in range(HEADS):
            cols = pl.ds(h * HEAD_V, HEAD_V)
            o = of_ref[0, :, cols].astype(F32) + ob_ref[0, :, cols].astype(F32)
            rs = lax.rsqrt(jnp.mean(o * o, axis=-1, keepdims=True) + EPS)
            og_ref[:, cols] = (o * rs * g_ref[...] * _silu(r_ref[:, cols].astype(F32))).astype(og_ref.dtype)
        y_ref[...] = _nn(og_ref[...], w_ref[...]).astype(y_ref.dtype)

    ospec = pl.BlockSpec((1, TM_OUT, D), lambda b, j: (b, j, 0))
    row = pl.BlockSpec((TM_OUT, D), lambda b, j: (b * tiles + j, 0))
    return _pallas(
        body, name="gla_out_proj", grid=(n, tiles),
        in_specs=[ospec, ospec, row, pl.BlockSpec((1, HEAD_V), lambda b, j: (0, 0)),
                  pl.BlockSpec((D, D), lambda b, j: (0, 0))],
        out_specs=[row, row],
        out_shape=[jax.ShapeDtypeStruct((n * SEQ, D), BF16), jax.ShapeDtypeStruct((n * SEQ, D), BF16)],
        compiler_params=_params(("parallel", "parallel")),
    )(o_f, o_b, r, gnorm, gla_proj)


def gla_out_bwd(o_f, o_b, r, dyg, gla_proj, gnorm, og):
    n = o_f.shape[0]
    tiles = SEQ // TM_EW

    def body(of_ref, ob_ref, r_ref, d_ref, w_ref, g_ref, og_ref, do_ref, dr_ref, dg_ref, gw_ref, dog_buf):
        @pl.when((pl.program_id(0) == 0) & (pl.program_id(1) == 0))
        def _():
            dg_ref[...] = jnp.zeros_like(dg_ref)
            gw_ref[...] = jnp.zeros_like(gw_ref)

        gw_ref[...] += _tn(og_ref[...], d_ref[...])
        dog_buf[...] = _nt(d_ref[...], w_ref[...])
        for h in range(HEADS):
            cols = pl.ds(h * HEAD_V, HEAD_V)
            o = of_ref[0, :, cols].astype(F32) + ob_ref[0, :, cols].astype(F32)
            rv = r_ref[:, cols].astype(F32)
            dv = dog_buf[:, cols]
            rs = lax.rsqrt(jnp.mean(o * o, axis=-1, keepdims=True) + EPS)
            oh = o * rs
            dr_ref[:, cols] = (dv * oh * g_ref[...] * _dsilu(rv)).astype(dr_ref.dtype)
            dn = dv * _silu(rv)
            dg_ref[...] += jnp.sum(dn * oh, axis=0, keepdims=True)
            doh = dn * g_ref[...]
            do_ref[0, :, cols] = _b16(rs * (doh - oh * jnp.mean(doh * oh, axis=-1, keepdims=True)))

    ospec = pl.BlockSpec((1, TM_EW, D), lambda b, j: (b, j, 0))
    row = pl.BlockSpec((TM_EW, D), lambda b, j: (b * tiles + j, 0))
    vec = pl.BlockSpec((1, HEAD_V), lambda b, j: (0, 0))
    return _pallas(
        body, name="gla_out_bwd", grid=(n, tiles),
        in_specs=[ospec, ospec, row, row, pl.BlockSpec((D, D), lambda b, j: (0, 0)), vec, row],
        out_specs=[ospec, row, vec, pl.BlockSpec((D, D), lambda b, j: (0, 0))],
        out_shape=[jax.ShapeDtypeStruct((n, SEQ, D), BF16), jax.ShapeDtypeStruct((n * SEQ, D), BF16),
                   jax.ShapeDtypeStruct((1, HEAD_V), F32), jax.ShapeDtypeStruct((D, D), F32)],
        scratch_shapes=[pltpu.VMEM((TM_EW, D), F32)],
        compiler_params=_params(("arbitrary", "arbitrary")),
    )(o_f, o_b, r, dyg, gla_proj, gnorm, og)


TM_OUT = 512


def merge_out_final(p5, y_conv, y_gla, w_out, x2, gate, final_g, target, n_samples):
    t = x2.shape[0]
    tiles = SEQ // TM_OUT

    def body(mc_ref, mg_ref, yc_ref, yg_ref, w_ref, x_ref, gate_ref, g_ref, t_ref,
             mrg_ref, dh_ref, dmo_ref, dgate_ref, dg_ref, loss_ref):
        b, j = pl.program_id(0), pl.program_id(1)
        f = lambda ref: ref[...].astype(F32)
        merged = _b16(_sigmoid(f(mc_ref)) * f(yc_ref) + _sigmoid(f(mg_ref)) * f(yg_ref))
        mrg_ref[...] = merged
        mo_v = _nn(merged, w_ref[...])
        h = x_ref[...] + gate_ref[0] * mo_v
        rs = lax.rsqrt(jnp.mean(h * h, axis=-1, keepdims=True) + EPS)
        nh = h * rs
        err = nh * g_ref[...] - t_ref[...]
        dy = err * (1.0 / D)
        dn = dy * g_ref[...]
        dh = rs * (dn - nh * jnp.mean(dn * nh, axis=-1, keepdims=True))
        dh_ref[...] = dh
        dmo_ref[...] = (dh * gate_ref[0]).astype(dmo_ref.dtype)

        @pl.when(j == 0)
        def _():
            dgate_ref[...] = jnp.zeros_like(dgate_ref)

        @pl.when((b == 0) & (j == 0))
        def _():
            dg_ref[...] = jnp.zeros_like(dg_ref)
            loss_ref[...] = jnp.zeros_like(loss_ref)

        dgate_ref[0] += jnp.sum(dh * mo_v, axis=0, keepdims=True)
        dg_ref[...] += jnp.sum(dy * nh, axis=0, keepdims=True)
        loss_ref[...] += (0.5 / D) * jnp.sum(err * err)

    row = pl.BlockSpec((TM_OUT, D), lambda b, j: (b * tiles + j, 0))
    per = pl.BlockSpec((1, 1, D), lambda b, j: (b, 0, 0))
    vec = pl.BlockSpec((1, D), lambda b, j: (0, 0))
    return _pallas(
        body, name="merge_out_final", grid=(n_samples, tiles),
        in_specs=[row, pl.BlockSpec((TM_OUT, D), lambda b, j: (b * tiles + j, 1)), row, row,
                  pl.BlockSpec((D, D), lambda b, j: (0, 0)), row, per, vec, row],
        out_specs=[row, row, row, per, vec, pl.BlockSpec((8, 128), lambda b, j: (0, 0))],
        out_shape=[jax.ShapeDtypeStruct((t, D), BF16), jax.ShapeDtypeStruct((t, D), F32), jax.ShapeDtypeStruct((t, D), BF16),
                   jax.ShapeDtypeStruct((n_samples, 1, D), F32), jax.ShapeDtypeStruct((1, D), F32),
                   jax.ShapeDtypeStruct((8, 128), F32)],
        compiler_params=_params(("arbitrary", "arbitrary")),
    )(p5, p5, y_conv, y_gla, w_out, x2, gate, final_g, target)


def out_dgrad_merge_bwd(p5, y_conv, y_gla, dmo, w_out, merged):
    t = y_conv.shape[0]

    def body(mc_ref, mg_ref, yc_ref, yg_ref, d_ref, w_ref, mrg_ref, dyc_ref, dyg_ref, dp_ref, gw_ref):
        f = lambda ref: ref[...].astype(F32)

        @pl.when(pl.program_id(0) == 0)
        def _():
            gw_ref[...] = jnp.zeros_like(gw_ref)

        gw_ref[...] += _tn(mrg_ref[...], d_ref[...])
        d = _nt(d_ref[...], w_ref[...])
        sc = _sigmoid(f(mc_ref))
        sg = _sigmoid(f(mg_ref))
        dyc_ref[...] = (d * sc).astype(dyc_ref.dtype)
        dyg_ref[...] = (d * sg).astype(dyg_ref.dtype)
        dp_ref[:, pl.ds(0, D)] = (d * f(yc_ref) * sc * (1.0 - sc)).astype(dp_ref.dtype)
        dp_ref[:, pl.ds(D, D)] = (d * f(yg_ref) * sg * (1.0 - sg)).astype(dp_ref.dtype)

    row = pl.BlockSpec((TM_OUT, D), lambda i: (i, 0))
    return _pallas(
        body, name="out_dgrad_merge_bwd", grid=(t // TM_OUT,),
        in_specs=[row, pl.BlockSpec((TM_OUT, D), lambda i: (i, 1)), row, row, row, pl.BlockSpec((D, D), lambda i: (0, 0)),
                  row],
        out_specs=[row, row, pl.BlockSpec((TM_OUT, 2 * D), lambda i: (i, 0)), pl.BlockSpec((D, D), lambda i: (0, 0))],
        out_shape=[jax.ShapeDtypeStruct((t, D), BF16), jax.ShapeDtypeStruct((t, D), BF16),
                   jax.ShapeDtypeStruct((t, 2 * D), BF16), jax.ShapeDtypeStruct((D, D), F32)],
        compiler_params=_params(("arbitrary",)),
    )(p5, p5, y_conv, y_gla, dmo, w_out, merged)


def local_step(x, ctx, target, mod, wts, small, p_sh, chip, core):
    n = x.shape[0]
    t = n * SEQ
    t_all = t + n * NCTX
    x2 = x.reshape(t, D)
    ctx2 = ctx.reshape(n * NCTX, D)
    tgt2 = target.reshape(t, D)
    scale1, shift, gate = mod

    u = norm_mod_fwd(x2, ctx2, scale1, shift, small["norm_g"])
    p1, p2, p3, p4, p5, p_all = proj_all(u, [wts["w%d" % i] for i in range(1, 6)], [small["b%d" % i] for i in range(1, 6)],
                                         [t, t, t_all, t, t], p_sh, tm=512)
    p_full = jnp.stack([jnp.where(chip == i, p_sh, p_all[i]) for i in range(N_CHIPS)])
    wts = dict(wts, conv_proj=p_full[:, 0:256].reshape(D, D), gla_proj=p_full[:, 256:512].reshape(D, D),
               w_out=p_full[:, 512:768].reshape(D, D))

    aconv = conv_fwd(p1, small["conv_w"], small["conv_b"], n)
    ac, y_conv = ln_gate_proj(aconv, p2, small["conv_ln_g"], small["conv_ln_b"], wts["conv_proj"])

    qs, ks, vs, cum_f, cum_b = gla_prep_fwd(p3, small["upf"], small["upb"], small["bias_f"], small["bias_b"], n)
    o_f, s_f, sfin_f = gla_scan_fwd(qs, ks, vs, cum_f, rev=False, name="gla_scan_fwd_f")
    o_b, s_b, sfin_b = gla_scan_fwd(qs, ks, vs, cum_b, rev=True, name="gla_scan_fwd_b")
    og, y_gla = gla_out_proj(o_f, o_b, p4, small["gla_norm_g"], wts["gla_proj"])

    merged, dh, dmo, dgate, d_final_g, loss = merge_out_final(p5, y_conv, y_gla, wts["w_out"], x2, gate,
                                                              small["final_norm_g"], tgt2, n)

    g = {"final_norm_g": d_final_g}
    dyc, dyg, dp5, g["w_out"] = out_dgrad_merge_bwd(p5, y_conv, y_gla, dmo, wts["w_out"], merged)

    daconv, dp2, g["conv_ln_g"], g["conv_ln_b"], g["conv_proj"] = ln_gate_bwd(
        aconv, p2, dyc, wts["conv_proj"], small["conv_ln_g"], small["conv_ln_b"], ac)
    dp1, dconv_w, dconv_b = conv_bwd(p1, daconv, small["conv_w"], n)
    g["conv_w"], g["conv_b"] = dconv_w, dconv_b

    do, dp4, g["gla_norm_g"], g["gla_proj"] = gla_out_bwd(o_f, o_b, p4, dyg, wts["gla_proj"], small["gla_norm_g"], og)
    g["proj"] = jnp.concatenate([g["conv_proj"].reshape(N_CHIPS, 256, D), g["gla_proj"].reshape(N_CHIPS, 256, D),
                                 g["w_out"].reshape(N_CHIPS, 256, D)], 1)
    dq_f, dk_f, dv_f, dc_f, gotp = gla_scan_bwd(qs, ks, vs, cum_f, s_f, sfin_f, do, rev=False, name="gla_scan_bwd_f",
                                                rider=("swap", g["proj"]))
    pap16 = pair_add(core, g["proj"], gotp, name="pair_add_p", tr=384)
    dq_b, dk_b, dv_b, dc_b, rbp = gla_scan_bwd(qs, ks, vs, cum_b, s_b, sfin_b, do, rev=True, name="gla_scan_bwd_b",
                                               rider=("exchange", pap16))
    dp3, g["upf"], g["upb"], g["bias_f"], g["bias_b"] = gla_prep_bwd(
        p3, dq_f, dq_b, dk_f, dk_b, dv_f, dv_b, dc_f, dc_b,
        small["upf"], small["upb"], small["bias_f"], small["bias_b"], n)

    dps = [dp1, dp2, dp3, dp4, dp5]
    early = [g["conv_b"].sum(0), g["conv_ln_g"], g["conv_ln_b"], g["bias_f"], g["bias_b"], g["gla_norm_g"],
             g["final_norm_g"], g["conv_w"].sum(0)[:CONV_K], g["upf"][0:16], g["upb"][16:32]]
    got = {}
    for i in [0, 1, 3, 4, 2]:
        dp = dps[i]
        rows = dp.shape[0]
        tn = W3 if dp.shape[1] == W3 else 1024
        others = [j for j in range(5) if j != i]
        swap = ([g["w%d" % (j + 1)] for j in others], _pack(early)) if i == 2 else None
        outs = matmul_tn(u, dp, name="w_in_wgrad_%d" % (i + 1), t=rows, tn=tn, tt=1024 if rows % 1024 == 0 else 768,
                         colsum=True, swap=swap)
        g["w%d" % (i + 1)], g["b%d" % (i + 1)] = outs[0], outs[1]
        if swap is not None:
            got = dict(zip(others, outs[2:2 + len(others)]))
            sall_early = outs[2 + len(others)]
    g["gate"] = dgate
    return loss, dh, dps, g, got, (pap16, rbp), (sall_early, [a.shape for a in early])


def _group_cols(w):
    gv, gg, z = w[..., 0:1024], w[..., 1024:2048], w[..., 2048:3072]
    q, k, v = w[..., 3072:3584], w[..., 3584:4096], w[..., 4096:5120]
    ab = w[..., 5120:5152]
    r, mc, mg = w[..., 5152:6176], w[..., 6176:7200], w[..., 7200:8224]
    g1 = jnp.concatenate([p for j in range(CONV_NCB)
                          for p in (gv[..., CONV_CB * j:CONV_CB * (j + 1)], gg[..., CONV_CB * j:CONV_CB * (j + 1)])], -1)
    pad = jnp.zeros(w.shape[:-1] + (W3 - 2080,), w.dtype)
    g3 = jnp.concatenate([v, q, k, ab, pad], -1)
    return g1, z, g3, r, jnp.concatenate([mc, mg], -1)


def _ungroup_cols(g1, g2, g3, g4, g5):
    gv = jnp.concatenate([g1[..., 2 * CONV_CB * j:2 * CONV_CB * j + CONV_CB] for j in range(CONV_NCB)], -1)
    gg = jnp.concatenate([g1[..., 2 * CONV_CB * j + CONV_CB:2 * CONV_CB * (j + 1)] for j in range(CONV_NCB)], -1)
    v, q, k, ab = g3[..., 0:1024], g3[..., 1024:1536], g3[..., 1536:2048], g3[..., 2048:2080]
    return jnp.concatenate([gv, gg, g2, q, k, v, ab, g4, g5[..., 0:1024], g5[..., 1024:2048]], -1)


def _natural_pieces():
    pieces = [(CONV_CB * j, CONV_CB, 0, 2 * CONV_CB * j) for j in range(CONV_NCB)]
    pieces += [(1024 + CONV_CB * j, CONV_CB, 0, 2 * CONV_CB * j + CONV_CB) for j in range(CONV_NCB)]
    pieces += [(2048, 1024, 1, 0), (3072, 512, 2, O3_Q), (3584, 512, 2, O3_K), (4096, 1024, 2, O3_V), (5120, 32, 2, O3_AB),
               (5152, 1024, 3, 0), (6176, 1024, 4, 0), (7200, 1024, 4, 1024)]
    return sorted(pieces)


def _ungroup_to_shards(groups):
    shards = []
    for i in range(N_CHIPS):
        lo, hi = i * W_IN_SHARD, (i + 1) * W_IN_SHARD
        parts = []
        for nat, width, g, gcol in _natural_pieces():
            a, b = max(nat, lo), min(nat + width, hi)
            if a < b:
                parts.append(groups[g][:, gcol + a - nat:gcol + b - nat])
        shards.append(jnp.concatenate(parts, 1))
    return jnp.stack(shards)


def _pad_up(up, row0):
    return jnp.zeros((128, GLA_DK), F32).at[row0:row0 + up.shape[0]].set(up)


def _adamw_math(w, g, m, v):
    m = ADAM_B1 * m + (1.0 - ADAM_B1) * g
    v = ADAM_B2 * v + (1.0 - ADAM_B2) * (g * g)
    m_hat = m / (1.0 - ADAM_B1 ** ADAM_STEP)
    v_hat = v / (1.0 - ADAM_B2 ** ADAM_STEP)
    delta = -ADAM_LR * (m_hat / (jnp.sqrt(v_hat) + ADAM_EPS) + ADAM_WD * w)
    return delta, m, v


def adamw2d(w, g, m, v, *, name, tr, tcols=None):
    rows, cols = w.shape[-2:]

    def body(w_ref, g_ref, m_ref, v_ref, d_ref, nm_ref, nv_ref):
        d_ref[...], nm_ref[...], nv_ref[...] = _adamw_math(w_ref[...], g_ref[...], m_ref[...], v_ref[...])

    tcols = cols if tcols is None else tcols
    if w.ndim == 3:
        spec = pl.BlockSpec((1, tr, tcols), lambda i, j: (0, i, j))
    else:
        spec = pl.BlockSpec((tr, tcols), lambda i, j: (i, j))
    return _pallas(
        body, name=name, grid=(rows // tr, cols // tcols), in_specs=[spec] * 4, out_specs=[spec] * 3,
        out_shape=[jax.ShapeDtypeStruct(w.shape, F32)] * 3, compiler_params=_params(("parallel", "parallel")),
    )(w, g, m, v)


def adamw_many(ws, gs, ms, vs):
    k = len(ws)
    two = lambda a: a.reshape((-1, a.shape[-1]))

    def body(*refs):
        w_refs, g_refs, m_refs, v_refs = refs[:k], refs[k:2 * k], refs[2 * k:3 * k], refs[3 * k:4 * k]
        d_refs, nm_refs, nv_refs = refs[4 * k:5 * k], refs[5 * k:6 * k], refs[6 * k:7 * k]
        for i in range(k):
            d_refs[i][...], nm_refs[i][...], nv_refs[i][...] = _adamw_math(
                w_refs[i][...], g_refs[i][...], m_refs[i][...], v_refs[i][...])

    shapes = [jax.ShapeDtypeStruct(two(a).shape, F32) for a in ws]
    outs = _pallas(body, name="adamw_small", out_shape=shapes * 3, compiler_params=_params())(
        *[two(a) for a in ws], *[two(a) for a in gs], *[two(a) for a in ms], *[two(a) for a in vs])
    back = lambda lst: [o.reshape(a.shape) for o, a in zip(lst, ws)]
    return back(outs[:k]), back(outs[k:2 * k]), back(outs[2 * k:])


def sum_devices(sall, *, name):
    rows = sall.shape[1]

    def body(s_ref, o_ref):
        acc = s_ref[0]
        for d in range(1, N_DEV):
            acc = acc + s_ref[d]
        o_ref[...] = acc

    return _pallas(body, name=name, out_shape=jax.ShapeDtypeStruct((rows, D), F32),
                   compiler_params=_params())(sall)


def pair_add(core, g, got, *, name, tr):
    n, rows, cols = got.shape
    g4 = g.reshape(n, 2, rows, cols)

    def body(core_ref, g_ref, got_ref, ob_ref):
        del core_ref
        ob_ref[0] = (g_ref[0, 0] + got_ref[0]).astype(BF16)

    spec = pl.BlockSpec((1, tr, cols), lambda i, t, core_ref: (i, t, 0))
    return _pallas(
        body, name=name,
        grid_spec=pltpu.PrefetchScalarGridSpec(
            num_scalar_prefetch=1, grid=(n, rows // tr),
            in_specs=[pl.BlockSpec((1, 1, tr, cols), lambda i, t, core_ref: (i, core_ref[0], t, 0)), spec],
            out_specs=spec),
        out_shape=jax.ShapeDtypeStruct(got.shape, BF16),
        compiler_params=_params(("parallel", "parallel")))(core, g4, got)


def pair_add_groups(core, gs, gots, *, tr):
    k = len(gs)
    rows = gots[0].shape[0]

    def body(core_ref, *refs):
        del core_ref
        for i in range(k):
            refs[2 * k + i][...] = (refs[i][0] + refs[k + i][...]).astype(BF16)

    g_specs = [pl.BlockSpec((1, tr, a.shape[1]), lambda t, core_ref: (core_ref[0], t, 0)) for a in gots]
    r_specs = [pl.BlockSpec((tr, a.shape[1]), lambda t, core_ref: (t, 0)) for a in gots]
    return _pallas(
        body, name="pair_add_w",
        grid_spec=pltpu.PrefetchScalarGridSpec(num_scalar_prefetch=1, grid=(rows // tr,),
                                               in_specs=g_specs + r_specs, out_specs=r_specs),
        out_shape=[jax.ShapeDtypeStruct(a.shape, BF16) for a in gots],
        compiler_params=_params(("parallel",)))(core, *[a.reshape(2, rows, a.shape[1]) for a in gs], *gots)


def chip_add(place, pa, rb, *, name, tr):
    _, rows, cols = pa.shape

    def body(place_ref, m_ref, r_ref, o_ref):
        del place_ref
        o_ref[0] = ((m_ref[0].astype(F32) + r_ref[0].astype(F32)) + r_ref[1].astype(F32)) + r_ref[2].astype(F32)

    return _pallas(
        body, name=name,
        grid_spec=pltpu.PrefetchScalarGridSpec(
            num_scalar_prefetch=1, grid=(rows // tr,),
            in_specs=[pl.BlockSpec((1, tr, cols), lambda t, place_ref: (place_ref[0], t, 0)),
                      pl.BlockSpec((3, tr, cols), lambda t, place_ref: (0, t, 0))],
            out_specs=pl.BlockSpec((1, tr, cols), lambda t, place_ref: (place_ref[1], t, 0))),
        out_shape=jax.ShapeDtypeStruct((2, rows, cols), F32),
        compiler_params=_params(("parallel",)))(place, pa, rb)


def ada_bwd(call, cctx_rows, dm_shard, dm_full, adaw):
    nsh = adaw.shape[1]

    def body(c_ref, cc_ref, dms_ref, dmf_ref, w_ref, gw_ref, gb_ref, pq_ref):
        a_lat = _silu(c_ref[...])
        a_ctx = _silu(cc_ref[...])
        dms = dms_ref[...]
        gw_ref[...] = _tn(a_lat, dms[0:64], HI) + _tn(a_ctx, dms[64:72], HI)
        gb_ref[...] = jnp.sum(dmf_ref[...], axis=0, keepdims=True)
        part = _nt(dms[64:72], w_ref[...], HI)
        pq_ref[...] = jnp.zeros_like(pq_ref) + jnp.sum(part, axis=0, keepdims=True)

    return _pallas(body, name="ada_bwd",
                   out_shape=[jax.ShapeDtypeStruct((D, nsh), F32), jax.ShapeDtypeStruct((1, 3 * D), F32),
                              jax.ShapeDtypeStruct((8, D), F32)],
                   compiler_params=_params())(call, cctx_rows, dm_shard, dm_full, adaw)


def cctx_grad(pq_all, cctx_rows):
    def body(p_ref, c_ref, o_ref):
        acc = p_ref[0]
        for qi in range(1, N_CHIPS):
            acc = acc + p_ref[qi]
        o_ref[...] = acc * _dsilu(c_ref[...])

    return _pallas(body, name="cctx_grad", out_shape=jax.ShapeDtypeStruct((8, D), F32),
                   compiler_params=_params())(pq_all, cctx_rows)


def _place():
    x, y, c = lax.axis_index("x"), lax.axis_index("y"), lax.axis_index("c")
    chips = [(1 - x, y), (x, 1 - y), (1 - x, 1 - y)]
    return x, y, c, chips


def _all_peers(x, y, c):
    return [((1 - x) if r & 4 else x, (1 - y) if r & 2 else y, (1 - c) if r & 1 else c) for r in range(1, N_DEV)]


def _remote(src, dst, send_sem, recv_sem, dev):
    return pltpu.make_async_remote_copy(src_ref=src, dst_ref=dst, send_sem=send_sem, recv_sem=recv_sem,
                                        device_id=dev, device_id_type=MESH)


ANY = pl.BlockSpec(memory_space=pl.ANY)
VMEM = pl.BlockSpec(memory_space=pltpu.VMEM)
F_ROWS = 16


W_ROW_CHUNKS = 4
P_ROW_CHUNKS = 2
N_BULK = W_ROW_CHUNKS + P_ROW_CHUNKS


def _half_chunks(core, n_rows, align, which=(0, 1)):
    out = []
    for a, k in ((0, W_ROW_CHUNKS), (1, P_ROW_CHUNKS)):
        if a not in which:
            continue
        half = n_rows[a] // 2
        size = half // k
        for i in range(k):
            start = core * half + i * size
            out.append((a, pl.ds(start if isinstance(start, int) else pl.multiple_of(start, align), size)))
    return out


def gather_weights(c8, cctx8, adaw, adab, w_sh, fp):
    nsh = adaw.shape[1]

    def body(c_ref, cctx_ref, adaw_ref, adab_ref, w_ref, fp_ref, wall_ref, fall_ref, call_ref, mall_ref,
             abuf, w_send, w_recv, h_send, h_recv, c_send, c_recv, m_send, m_recv, f_send, f_recv):
        x, y, c, chips = _place()
        q = 2 * x + y
        dev = 4 * x + 2 * y + c
        qs = [2 * cx + cy for cx, cy in chips]
        sib = (x, y, 1 - c)
        srcs, dsts = (w_ref,), (wall_ref,)
        n_rows = (w_ref.shape[0],)
        mine = _half_chunks(c, n_rows, 16, which=(0,))
        other = _half_chunks(1 - c, n_rows, 16, which=(0,))

        bulk = [[_remote(srcs[a].at[rows], dsts[a].at[q, rows], w_send.at[j * N_BULK + i], w_recv.at[j * N_BULK + i],
                         (*chips[j], c)) for i, (a, rows) in enumerate(mine)] for j in range(3)]
        fall_ref[q] = fp_ref[...]
        small = [_remote(fp_ref, fall_ref.at[q], f_send.at[j], f_recv.at[j], (*chips[j], c)) for j in range(3)]
        my_rows = pl.ds(pl.multiple_of(8 * dev, 8), 8)
        call_ref[my_rows, :] = c_ref[...]
        cond = [_remote(c_ref, call_ref.at[my_rows, :], c_send.at[r], c_recv.at[r], peer)
                for r, peer in enumerate(_all_peers(x, y, c))]
        for cp in sum(bulk, []) + small + cond:
            cp.start()
        for cp in cond:
            cp.wait_recv()

        abuf[pl.ds(0, 64), :] = _silu(call_ref[...])
        abuf[pl.ds(64, 8), :] = _silu(cctx_ref[...])
        mall_ref[q] = _nn(abuf[...], adaw_ref[...], HI) + adab_ref[...]
        mod = [_remote(mall_ref.at[q], mall_ref.at[q], m_send.at[j], m_recv.at[j], (*chips[j], c)) for j in range(3)]
        for cp in mod:
            cp.start()

        handed = []
        for j in range(3):
            for i, (a, rows) in enumerate(mine):
                bulk[j][i].wait_recv()
                cp = _remote(dsts[a].at[qs[j], rows], dsts[a].at[qs[j], rows],
                             h_send.at[j * N_BULK + i], h_recv.at[j * N_BULK + i], sib)
                cp.start()
                handed.append(cp)
        for j in range(3):
            for i, (a, rows) in enumerate(other):
                _remote(dsts[a].at[qs[j], rows], dsts[a].at[qs[j], rows],
                        h_send.at[j * N_BULK + i], h_recv.at[j * N_BULK + i], sib).wait_recv()
        for cp in mod + small:
            cp.wait_recv()
        for cp in sum(bulk, []) + small + cond + mod + handed:
            cp.wait_send()

    def dma(n):
        return pltpu.SemaphoreType.DMA((n,))

    return _pallas(
        body, name="gather_weights",
        in_specs=[VMEM, VMEM, VMEM, VMEM, ANY, VMEM],
        out_specs=[ANY, VMEM, VMEM, VMEM],
        out_shape=[jax.ShapeDtypeStruct((N_CHIPS,) + w_sh.shape, BF16),
                   jax.ShapeDtypeStruct((N_CHIPS, F_ROWS, D), F32),
                   jax.ShapeDtypeStruct((8 * N_DEV, D), F32), jax.ShapeDtypeStruct((N_CHIPS, MOD_ROWS, nsh), F32)],
        scratch_shapes=[pltpu.VMEM((MOD_ROWS, D), F32), dma(3 * N_BULK), dma(3 * N_BULK), dma(3 * N_BULK), dma(3 * N_BULK),
                        dma(7), dma(7), dma(3), dma(3), dma(3), dma(3)],
        compiler_params=_params(),
    )(c8, cctx8, adaw, adab, w_sh, fp)


def _pair_count(gs, gp):
    return len(gs) * W_ROW_CHUNKS + (0 if gp is None else N_CHIPS * P_ROW_CHUNKS)


def _pair_got_shapes(gs, gp):
    shapes = [jax.ShapeDtypeStruct((D // 2, a.shape[1]), F32) for a in gs]
    if gp is not None:
        shapes.append(jax.ShapeDtypeStruct((N_CHIPS, gp.shape[1] // 2, gp.shape[2]), F32))
    return shapes


def _pair_copies(g_refs, gp_ref, got_refs, gotp_ref, a_send, a_recv):
    x, y, c, _ = _place()
    sib = (x, y, 1 - c)
    pair = []
    half, size = D // 2, D // 2 // W_ROW_CHUNKS
    for gi in range(len(g_refs)):
        for i in range(W_ROW_CHUNKS):
            k = len(pair)
            rows_o = pl.ds(pl.multiple_of((1 - c) * half + i * size, 8), size)
            pair.append(_remote(g_refs[gi].at[rows_o], got_refs[gi].at[pl.ds(i * size, size)],
                                a_send.at[k], a_recv.at[k], sib))
    if gp_ref is not None:
        half, size = gp_ref.shape[1] // 2, gp_ref.shape[1] // 2 // P_ROW_CHUNKS
        for s in range(N_CHIPS):
            for i in range(P_ROW_CHUNKS):
                k = len(pair)
                rows_o = pl.ds(pl.multiple_of((1 - c) * half + i * size, 8), size)
                pair.append(_remote(gp_ref.at[s, rows_o], gotp_ref.at[s, pl.ds(i * size, size)],
                                    a_send.at[k], a_recv.at[k], sib))
    return pair


def pair_swap(gs):
    n_gs = len(gs)

    def body(*refs):
        g_refs, got_refs = refs[:n_gs], refs[n_gs:2 * n_gs]
        a_send, a_recv = refs[2 * n_gs:]
        pair = _pair_copies(g_refs, None, got_refs, None, a_send, a_recv)
        for cp in pair:
            cp.start()
        for cp in pair:
            cp.wait_recv()
        for cp in pair:
            cp.wait_send()

    return _pallas(
        body, name="pair_swap", in_specs=[ANY] * n_gs, out_specs=[ANY] * n_gs,
        out_shape=_pair_got_shapes(gs, None),
        scratch_shapes=[pltpu.SemaphoreType.DMA((_pair_count(gs, None),)), pltpu.SemaphoreType.DMA((_pair_count(gs, None),))],
        compiler_params=_params(),
    )(*gs)


def gather_small(sm):
    rows = sm.shape[0]

    def body(sm_ref, sall_ref, s_send, s_recv):
        x, y, c, _ = _place()
        dev = 4 * x + 2 * y + c
        sall_ref[dev] = sm_ref[...]
        small = [_remote(sm_ref, sall_ref.at[dev], s_send.at[r], s_recv.at[r], peer)
                 for r, peer in enumerate(_all_peers(x, y, c))]
        for cp in small:
            cp.start()
        for cp in small:
            cp.wait_recv()
        for cp in small:
            cp.wait_send()

    return _pallas(
        body, name="gather_small", in_specs=[VMEM], out_specs=VMEM,
        out_shape=jax.ShapeDtypeStruct((N_DEV, rows, D), F32),
        scratch_shapes=[pltpu.SemaphoreType.DMA((7,)), pltpu.SemaphoreType.DMA((7,))],
        compiler_params=_params(),
    )(sm)


def pair_share(ghw, ghp, pq):
    def body(ghw_ref, ghp_ref, pq_ref, outw_ref, outp_ref, pqa_ref, send, recv, p_send, p_recv):
        del ghw_ref, ghp_ref
        x, y, c, chips = _place()
        q = 2 * x + y
        refs = (outw_ref, outp_ref)
        n_rows = (2 * outw_ref.shape[1], 2 * outp_ref.shape[1])
        pair = [_remote(refs[a].at[c, rows], refs[a].at[c, rows], send.at[i], recv.at[i], (x, y, 1 - c))
                for i, (a, rows) in enumerate(_half_chunks(0, n_rows, 8))]
        pqa_ref[q] = pq_ref[...]
        small = [_remote(pq_ref, pqa_ref.at[q], p_send.at[j], p_recv.at[j], (*chips[j], c)) for j in range(3)]
        for cp in pair + small:
            cp.start()
        for i, (a, rows) in enumerate(_half_chunks(0, n_rows, 8)):
            _remote(refs[a].at[1 - c, rows], refs[a].at[1 - c, rows], send.at[i], recv.at[i], (x, y, 1 - c)).wait_recv()
        for cp in small:
            cp.wait_recv()
        for cp in pair + small:
            cp.wait_send()

    return _pallas(
        body, name="pair_share", in_specs=[ANY, ANY, VMEM], out_specs=[ANY, ANY, VMEM],
        out_shape=[jax.ShapeDtypeStruct(ghw.shape, F32), jax.ShapeDtypeStruct(ghp.shape, F32),
                   jax.ShapeDtypeStruct((N_CHIPS, 8, D), F32)],
        scratch_shapes=[pltpu.SemaphoreType.DMA((N_BULK,)), pltpu.SemaphoreType.DMA((N_BULK,)),
                        pltpu.SemaphoreType.DMA((3,)), pltpu.SemaphoreType.DMA((3,))],
        input_output_aliases={0: 0, 1: 1},
        compiler_params=_params(),
    )(ghw, ghp, pq)


def _rows_of(shape):
    size = 1
    for s in shape:
        size *= s
    return -(-size // D)


def _pack(arrs, rows_multiple=8):
    parts = []
    total = 0
    for a in arrs:
        f = a.reshape(-1).astype(F32)
        r = _rows_of(a.shape)
        parts.append(jnp.pad(f, (0, r * D - f.shape[0])))
        total += r
    pad_rows = (-total) % rows_multiple
    if pad_rows:
        parts.append(jnp.zeros((pad_rows * D,), F32))
    return jnp.concatenate(parts).reshape(-1, D)


def _unpack(p, shapes):
    out = []
    r0 = 0
    for shp in shapes:
        r = _rows_of(shp)
        size = 1
        for s in shp:
            size *= s
        out.append(p[r0:r0 + r].reshape(-1)[:size].reshape(shp))
        r0 += r
    return out


WEIGHT_NAMES = ['c_ctx', 'ada_w', 'ada_b', 'norm_g', 'w_in', 'b_in', 'conv_w', 'conv_b', 'conv_ln_g', 'conv_ln_b',
                'conv_proj', 'decay_up_fwd', 'decay_bias_fwd', 'decay_up_bwd', 'decay_bias_bwd', 'gla_norm_g', 'gla_proj',
                'w_out', 'final_norm_g']
SMALL_NAMES = ['c_ctx', 'ada_b', 'norm_g', 'b_in', 'conv_w', 'conv_b', 'conv_ln_g', 'conv_ln_b', 'decay_up_fwd',
               'decay_bias_fwd', 'decay_up_bwd', 'decay_bias_bwd', 'gla_norm_g', 'final_norm_g']


def kernel(x, c, ctx, c_ctx, ada_w, ada_b, norm_g, w_in, b_in, conv_w, conv_b, conv_ln_g, conv_ln_b, conv_proj, decay_up_fwd, decay_bias_fwd, decay_up_bwd, decay_bias_bwd, gla_norm_g, gla_proj, w_out, final_norm_g, loss_target, m_c_ctx, m_ada_w, m_ada_b, m_norm_g, m_w_in, m_b_in, m_conv_w, m_conv_b, m_conv_ln_g, m_conv_ln_b, m_conv_proj, m_decay_up_fwd, m_decay_bias_fwd, m_decay_up_bwd, m_decay_bias_bwd, m_gla_norm_g, m_gla_proj, m_w_out, m_final_norm_g, v_c_ctx, v_ada_w, v_ada_b, v_norm_g, v_w_in, v_b_in, v_conv_w, v_conv_b, v_conv_ln_g, v_conv_ln_b, v_conv_proj, v_decay_up_fwd, v_decay_bias_fwd, v_decay_up_bwd, v_decay_bias_bwd, v_gla_norm_g, v_gla_proj, v_w_out, v_final_norm_g):
    w = dict(c_ctx=c_ctx, ada_w=ada_w, ada_b=ada_b, norm_g=norm_g, w_in=w_in, b_in=b_in, conv_w=conv_w, conv_b=conv_b,
             conv_ln_g=conv_ln_g, conv_ln_b=conv_ln_b, conv_proj=conv_proj, decay_up_fwd=decay_up_fwd,
             decay_bias_fwd=decay_bias_fwd, decay_up_bwd=decay_up_bwd, decay_bias_bwd=decay_bias_bwd,
             gla_norm_g=gla_norm_g, gla_proj=gla_proj, w_out=w_out, final_norm_g=final_norm_g)
    m = dict(c_ctx=m_c_ctx, ada_w=m_ada_w, ada_b=m_ada_b, norm_g=m_norm_g, w_in=m_w_in, b_in=m_b_in, conv_w=m_conv_w,
             conv_b=m_conv_b, conv_ln_g=m_conv_ln_g, conv_ln_b=m_conv_ln_b, conv_proj=m_conv_proj,
             decay_up_fwd=m_decay_up_fwd, decay_bias_fwd=m_decay_bias_fwd, decay_up_bwd=m_decay_up_bwd,
             decay_bias_bwd=m_decay_bias_bwd, gla_norm_g=m_gla_norm_g, gla_proj=m_gla_proj, w_out=m_w_out,
             final_norm_g=m_final_norm_g)
    v = dict(c_ctx=v_c_ctx, ada_w=v_ada_w, ada_b=v_ada_b, norm_g=v_norm_g, w_in=v_w_in, b_in=v_b_in, conv_w=v_conv_w,
             conv_b=v_conv_b, conv_ln_g=v_conv_ln_g, conv_ln_b=v_conv_ln_b, conv_proj=v_conv_proj,
             decay_up_fwd=v_decay_up_fwd, decay_bias_fwd=v_decay_bias_fwd, decay_up_bwd=v_decay_up_bwd,
             decay_bias_bwd=v_decay_bias_bwd, gla_norm_g=v_gla_norm_g, gla_proj=v_gla_proj, w_out=v_w_out,
             final_norm_g=v_final_norm_g)
    n = x.shape[0]
    ax, ay, ac = lax.axis_index("x"), lax.axis_index("y"), lax.axis_index("c")
    q = 2 * ax + ay
    dev = 4 * ax + 2 * ay + ac
    nsh = ada_w.shape[2]

    w_sh = w_in[0].astype(BF16)
    p_sh = jnp.concatenate([conv_proj[0], gla_proj[0], w_out[0]], 0).astype(BF16)
    fp = _pack([conv_w[0], decay_up_fwd[0], decay_up_bwd[0]], F_ROWS)
    c8 = jnp.pad(c, ((0, 8 - n), (0, 0)))
    cctx8 = jnp.pad(c_ctx[None], ((0, 7), (0, 0)))
    adab_sh = lax.dynamic_slice(ada_b, (0, q * nsh), (1, nsh))
    w_all, fall, call, mall = gather_weights(c8, cctx8, ada_w[0], adab_sh, w_sh, fp)

    mod_all = jnp.transpose(mall, (1, 0, 2)).reshape(MOD_ROWS, 3 * D)
    mod_mine = lax.dynamic_slice(mod_all, (8 * dev, 0), (n, 3 * D))
    mod_ctx = mod_all[64:65]
    shift = jnp.concatenate([mod_mine[:, 0:D], mod_ctx[:, 0:D]], 0)[:, None, :]
    scale1 = 1.0 + jnp.concatenate([mod_mine[:, D:2 * D], mod_ctx[:, D:2 * D]], 0)[:, None, :]
    gate = mod_mine[:, 2 * D:3 * D][:, None, :]

    own = lambda i, mine, got: jnp.where(q == i, mine, got)
    g1, g2, g3, g4, g5 = _group_cols(jnp.concatenate([own(i, w_sh, w_all[i]) for i in range(N_CHIPS)], 1))
    wts = dict(w1=g1, w2=g2, w3=g3, w4=g4, w5=g5)
    f_parts = [_unpack(fall[i], [conv_w.shape[1:], decay_up_fwd.shape[1:], decay_up_bwd.shape[1:]]) for i in range(N_CHIPS)]
    conv_w_full = jnp.concatenate([p[0] for p in f_parts], 1)
    upf_full = jnp.concatenate([p[1] for p in f_parts], 1)
    upb_full = jnp.concatenate([p[2] for p in f_parts], 1)
    b1, b2, b3, b4, b5 = _group_cols(b_in)
    small = dict(b1=b1, b2=b2, b3=b3, b4=b4, b5=b5, norm_g=norm_g,
                 conv_w=jnp.pad(conv_w_full, ((0, 1), (0, 0))), conv_b=conv_b, conv_ln_g=conv_ln_g, conv_ln_b=conv_ln_b,
                 upf=_split3(_pad_up(upf_full, 0)), upb=_split3(_pad_up(upb_full, 16)),
                 bias_f=decay_bias_fwd, bias_b=decay_bias_bwd,
                 gla_norm_g=gla_norm_g, final_norm_g=final_norm_g[None])

    core = ac.astype(jnp.int32).reshape(1)
    chip = q.astype(jnp.int32).reshape(1)
    loss_part, dh, dps, g, got, (pap16, rbp), (sall1, early_shapes) = local_step(
        x, ctx, loss_target, (scale1, shift, gate), wts, small, p_sh, q, core)

    gs = [g["w%d" % i] for i in range(1, 6)]
    got[2] = pair_swap([gs[2]])[0]
    halves = pair_add_groups(core, gs, [got[i] for i in range(5)], tr=128)
    paw16 = _ungroup_to_shards(halves)
    grad_x2, dshift, dscale, g["norm_g"], rbw = dgrad_norm_bwd(
        dps, [wts["w%d" % i] for i in range(1, 6)], paw16, x.reshape(n * SEQ, D), ctx.reshape(n * NCTX, D), dh,
        scale1, norm_g, tm=256)

    dm_mine = jnp.concatenate([dshift[:n, 0], dscale[:n, 0], g["gate"][:, 0]], -1)
    dm_ctx = jnp.concatenate([dshift[n, 0], dscale[n, 0], jnp.zeros((D,), F32)], -1)
    d_b_in = _ungroup_cols(*[g["b%d" % i] for i in range(1, 6)])
    late = [g["norm_g"], dm_mine, dm_ctx, d_b_in, loss_part[0, 0:1]]
    late_shapes = [a.shape for a in late]
    sall2 = gather_small(_pack(late))
    (s_conv_b, s_ln_g, s_ln_b, s_bias_f, s_bias_b, s_gla_g, s_final_g, s_conv_w, s_upf,
     s_upb) = _unpack(sum_devices(sall1, name="sum_devices_early"), early_shapes)
    s_late = _unpack(sum_devices(sall2, name="sum_devices_late"), late_shapes)
    s_norm_g, s_b_in, loss = s_late[0], s_late[3], s_late[4][0]
    r_mine, r_ctx = 1, 1 + 3 * n
    dm_all = sall2[:, r_mine:r_ctx].reshape(N_DEV, n, 3 * D)
    dm_full = jnp.concatenate([jnp.pad(dm_all, ((0, 0), (0, 8 - n), (0, 0))).reshape(8 * N_DEV, 3 * D),
                               sall2[:, r_ctx:r_ctx + 3].reshape(N_DEV, 3 * D)], 0)
    dm_shard = lax.dynamic_slice(dm_full, (0, q * nsh), (MOD_ROWS, nsh))
    cctx_rows = jnp.broadcast_to(c_ctx[None], (8, D))
    g_ada_w, g_ada_b, pq = ada_bwd(call, cctx_rows, dm_shard, dm_full, ada_w[0])

    place = jnp.concatenate([chip, core])
    ghw = chip_add(place, paw16, rbw, name="chip_add_w", tr=128)
    ghp = chip_add(place, pap16, rbp, name="chip_add_p", tr=384)
    gw_mine, gp_mine, pq_all = pair_share(ghw, ghp, pq)
    gp_mine = gp_mine.reshape(768, D)
    g_c_ctx = cctx_grad(pq_all, cctx_rows)[0]

    grads = dict(
        c_ctx=g_c_ctx, ada_w=g_ada_w[None], ada_b=g_ada_b, norm_g=s_norm_g,
        w_in=gw_mine.reshape(1, D, W_IN_SHARD), b_in=s_b_in,
        conv_w=lax.dynamic_slice(s_conv_w, (0, q * 256), (CONV_K, 256))[None], conv_b=s_conv_b,
        conv_ln_g=s_ln_g, conv_ln_b=s_ln_b, conv_proj=gp_mine[0:256][None],
        decay_up_fwd=lax.dynamic_slice(s_upf, (0, q * 128), (16, 128))[None], decay_bias_fwd=s_bias_f,
        decay_up_bwd=lax.dynamic_slice(s_upb, (0, q * 128), (16, 128))[None], decay_bias_bwd=s_bias_b,
        gla_norm_g=s_gla_g, gla_proj=gp_mine[256:512][None], w_out=gp_mine[512:768][None],
        final_norm_g=s_final_g[0])

    delta, new_m, new_v = {}, {}, {}
    delta["ada_w"], new_m["ada_w"], new_v["ada_w"] = adamw2d(w["ada_w"], grads["ada_w"], m["ada_w"], v["ada_w"],
                                                             name="adamw_ada_w", tr=128)
    tr_ = lambda a: jnp.swapaxes(a, 1, 2)
    g_w_in_t = tr_(grads["w_in"])
    grads["w_in"] = tr_(g_w_in_t)
    d_, m_, v_ = adamw2d(tr_(w_in), g_w_in_t, tr_(m_w_in), tr_(v_w_in), name="adamw_w_in", tr=W_IN_SHARD, tcols=128)
    delta["w_in"], new_m["w_in"], new_v["w_in"] = tr_(d_), tr_(m_), tr_(v_)
    rest = SMALL_NAMES + ["conv_proj", "gla_proj", "w_out"]
    d_, m_, v_ = adamw_many([w[nm] for nm in rest], [grads[nm].reshape(w[nm].shape) for nm in rest],
                            [m[nm] for nm in rest], [v[nm] for nm in rest])
    for nm, a, b, cc in zip(rest, d_, m_, v_):
        delta[nm], new_m[nm], new_v[nm] = a, b, cc

    grad_x = grad_x2.reshape(x.shape)
    return (loss, grad_x, *[grads[nm].reshape(w[nm].shape) for nm in WEIGHT_NAMES], *[delta[nm] for nm in WEIGHT_NAMES],
            *[new_m[nm] for nm in WEIGHT_NAMES], *[new_v[nm] for nm in WEIGHT_NAMES])
```

```python
import jax
import jax.numpy as jnp
from jax import lax
from jax.experimental import pallas as pl
from jax.experimental.pallas import tpu as pltpu

F32 = jnp.float32
BF16 = jnp.bfloat16
MESH = pl.DeviceIdType.MESH
HI = lax.Precision.HIGHEST

D = 1024
SEQ = 2048
GRID_W = 64
GRID_H = SEQ // GRID_W
NCTX = 256
SEQ_ALL = SEQ + NCTX
EPS = 1e-6
CONV_K = 31
CONV_PAD = CONV_K // 2
HEADS = 4
HEAD_K = 128
HEAD_V = 256
GLA_DK = HEADS * HEAD_K
GATE_TAU = 16.0
Q_SCALE = HEAD_K ** -0.5
CHUNK = 64
NCHUNK = SEQ_ALL // CHUNK
NCHUNK_LAT = SEQ // CHUNK
NCHUNK_CTX = NCHUNK - NCHUNK_LAT
SUB = 64
NSUB = CHUNK // SUB
N_IN = 8224
W3 = 2176
O3_V, O3_Q, O3_K, O3_AB = 0, 1024, 1536, 2048

ADAM_LR, ADAM_B1, ADAM_B2, ADAM_EPS, ADAM_WD, ADAM_STEP = 0.001, 0.9, 0.999, 1e-08, 0.01, 10
VMEM_LIMIT = 56 * 1024 * 1024

N_CHIPS = 4
N_DEV = 8
W_IN_SHARD = N_IN // N_CHIPS
MOD_ROWS = 72


def _pallas(body, **kw):
    return pl.pallas_call(body, **kw)


def _params(sem=None, **kw):
    if sem is not None:
        kw["dimension_semantics"] = sem
    return pltpu.CompilerParams(vmem_limit_bytes=VMEM_LIMIT, **kw)


def _sigmoid(v):
    return 1.0 / (1.0 + jnp.exp(-v))


def _silu(v):
    return v * _sigmoid(v)


def _dsilu(v):
    s = _sigmoid(v)
    return s * (1.0 + v * (1.0 - s))


def _log_sigmoid(v):
    return jnp.minimum(v, 0.0) - jnp.log(1.0 + jnp.exp(-jnp.abs(v)))


def _dot(a, b, dims, precision=None):
    return lax.dot_general(a, b, (dims, ((), ())), preferred_element_type=F32, precision=precision)


def _nn(a, b, precision=None):
    return _dot(a, b, ((1,), (0,)), precision)


def _nt(a, b, precision=None):
    return _dot(a, b, ((1,), (1,)), precision)


def _tn(a, b, precision=None):
    return _dot(a, b, ((0,), (0,)), precision)


def _b16(v):
    return v.astype(BF16)


def proj_all(u, ws, bs, rows, p_sh, *, tm):
    k = u.shape[1]
    n_g = len(ws)
    tns = [w.shape[1] if w.shape[1] % 1024 else 1024 for w in ws]
    mts = [r // tm for r in rows]
    cnts = [(w.shape[1] // tn) * mt for w, tn, mt in zip(ws, tns, mts)]
    los = [sum(cnts[:g]) for g in range(n_g)]
    n_steps = sum(cnts)

    def rel(s, g):
        return jnp.clip(s - los[g], 0, cnts[g] - 1)

    def active(s, g):
        return (s >= los[g]) & (s < los[g] + cnts[g])

    def u_row(s):
        r = 0
        for g in range(n_g):
            r = r + jnp.where(active(s, g), rel(s, g) % mts[g], 0)
        return r

    def body(*refs):
        u_ref = refs[0]
        w_refs, b_refs = refs[1:1 + n_g], refs[1 + n_g:1 + 2 * n_g]
        p_ref = refs[1 + 2 * n_g]
        o_refs = refs[2 + 2 * n_g:2 + 3 * n_g]
        pall_ref = refs[2 + 3 * n_g]
        w_send, w_recv, h_send, h_recv = refs[3 + 3 * n_g:]
        s = pl.program_id(0)
        for g in range(n_g):
            @pl.when(active(s, g))
            def _(g=g):
                o_refs[g][...] = (_nn(u_ref[...], w_refs[g][...]) + b_refs[g][...]).astype(o_refs[g].dtype)

        x, y, c, chips = _place()
        q = 2 * x + y
        mine = _half_chunks(c, (0, p_ref.shape[0]), 16, which=(1,))
        other = _half_chunks(1 - c, (0, p_ref.shape[0]), 16, which=(1,))
        nb = len(mine)

        def bulk():
            return [[_remote(p_ref.at[rws], pall_ref.at[q, rws], w_send.at[pj * nb + pi], w_recv.at[pj * nb + pi],
                             (*chips[pj], c)) for pi, (_, rws) in enumerate(mine)] for pj in range(3)]

        @pl.when(s == 0)
        def _():
            for cp in sum(bulk(), []):
                cp.start()

        @pl.when(s == n_steps - 1)
        def _():
            handed = []
            for pj, (cx, cy) in enumerate(chips):
                for pi, (_, rws) in enumerate(mine):
                    bulk()[pj][pi].wait_recv()
                    cp = _remote(pall_ref.at[2 * cx + cy, rws], pall_ref.at[2 * cx + cy, rws],
                                 h_send.at[pj * nb + pi], h_recv.at[pj * nb + pi], (x, y, 1 - c))
                    cp.start()
                    handed.append(cp)
            for pj, (cx, cy) in enumerate(chips):
                for pi, (_, rws) in enumerate(other):
                    _remote(pall_ref.at[2 * cx + cy, rws], pall_ref.at[2 * cx + cy, rws],
                            h_send.at[pj * nb + pi], h_recv.at[pj * nb + pi], (x, y, 1 - c)).wait_recv()
            for cp in sum(bulk(), []) + handed:
                cp.wait_send()

    any_spec = pl.BlockSpec(memory_space=pl.ANY)
    in_specs = [pl.BlockSpec((tm, k), lambda s: (u_row(s), 0))]
    in_specs += [pl.BlockSpec((k, tns[g]), lambda s, g=g: (0, rel(s, g) // mts[g])) for g in range(n_g)]
    in_specs += [pl.BlockSpec((1, tns[g]), lambda s, g=g: (0, rel(s, g) // mts[g])) for g in range(n_g)]
    in_specs.append(any_spec)
    out_specs = [pl.BlockSpec((tm, tns[g]), lambda s, g=g: (rel(s, g) % mts[g], rel(s, g) // mts[g])) for g in range(n_g)]
    out_specs.append(any_spec)
    out_shape = [jax.ShapeDtypeStruct((rows[g], ws[g].shape[1]), BF16) for g in range(n_g)]
    out_shape.append(jax.ShapeDtypeStruct((N_CHIPS,) + p_sh.shape, p_sh.dtype))
    return _pallas(
        body, name="proj_all", grid=(n_steps,), in_specs=in_specs, out_specs=out_specs, out_shape=out_shape,
        scratch_shapes=[pltpu.SemaphoreType.DMA((3 * P_ROW_CHUNKS,)) for _ in range(4)],
        compiler_params=_params(("arbitrary",)),
    )(u, *ws, *bs, p_sh)


def matmul_tn(a, b, *, name, t, tn, tt, colsum=False, swap=None):
    m = a.shape[1]
    n = b.shape[1]
    nj, ns = n // tn, t // tt
    n_out = 2 if colsum else 1
    n_sw = 0 if swap is None else len(swap[0])
    n_ex = 0 if swap is None else n_sw + 1

    def body(a_ref, b_ref, *rest):
        o_ref = rest[n_ex]
        cs_ref = rest[n_ex + 1] if colsum else None
        j, s = pl.program_id(0), pl.program_id(1)

        if swap is not None:
            g_refs, sm_ref = rest[:n_sw], rest[n_sw]
            got_refs, sall_ref = rest[n_ex + n_out:n_ex + n_out + n_sw], rest[n_ex + n_out + n_sw]
            a_send, a_recv, s_send, s_recv, l_sem = rest[2 * n_ex + n_out:]

            def copies():
                x, y, c, _ = _place()
                dev = 4 * x + 2 * y + c
                small = [_remote(sm_ref, sall_ref.at[dev], s_send.at[r], s_recv.at[r], peer)
                         for r, peer in enumerate(_all_peers(x, y, c))]
                return (_pair_copies(g_refs, None, got_refs, None, a_send, a_recv) + small,
                        pltpu.make_async_copy(sm_ref, sall_ref.at[dev], l_sem))

            @pl.when((j == 0) & (s == 0))
            def _():
                remote, own = copies()
                own.start()
                for cp in remote:
                    cp.start()

            @pl.when((j == nj - 1) & (s == ns - 1))
            def _():
                remote, own = copies()
                for cp in remote:
                    cp.wait_recv()
                for cp in remote:
                    cp.wait_send()
                own.wait()

        @pl.when(s == 0)
        def _():
            o_ref[...] = jnp.zeros_like(o_ref)
            if colsum:
                cs_ref[...] = jnp.zeros_like(cs_ref)
        o_ref[...] += _tn(a_ref[...], b_ref[...])
        if colsum:
            cs_ref[...] += jnp.sum(b_ref[...].astype(F32), axis=0, keepdims=True)

    in_specs = [pl.BlockSpec((tt, m), lambda j, s: (s, 0)), pl.BlockSpec((tt, tn), lambda j, s: (s, j))]
    out_specs = [pl.BlockSpec((m, tn), lambda j, s: (0, j))]
    out_shape = [jax.ShapeDtypeStruct((m, n), F32)]
    if colsum:
        out_specs.append(pl.BlockSpec((1, tn), lambda j, s: (0, j)))
        out_shape.append(jax.ShapeDtypeStruct((1, n), F32))
    args, scratch = [a, b], []
    if swap is not None:
        gs, sm = swap
        any_spec = pl.BlockSpec(memory_space=pl.ANY)
        in_specs += [any_spec] * n_ex
        out_specs += [any_spec] * n_ex
        out_shape += _pair_got_shapes(gs, None) + [jax.ShapeDtypeStruct((N_DEV,) + sm.shape, F32)]
        args += [*gs, sm]
        scratch = [pltpu.SemaphoreType.DMA((_pair_count(gs, None),)), pltpu.SemaphoreType.DMA((_pair_count(gs, None),)),
                   pltpu.SemaphoreType.DMA((N_DEV - 1,)), pltpu.SemaphoreType.DMA((N_DEV - 1,)),
                   pltpu.SemaphoreType.DMA(())]
    return _pallas(
        body, name=name, grid=(nj, ns), in_specs=in_specs, out_specs=out_specs, out_shape=out_shape,
        scratch_shapes=scratch,
        compiler_params=_params(("parallel" if swap is None else "arbitrary", "arbitrary")),
    )(*args)


def dgrad_norm_bwd(dps, wts, paw, x2, ctx2, dh, scale1, norm_g, *, tm):
    t, tc = x2.shape[0], ctx2.shape[0]
    t_all = t + tc
    n_lat, n_ctx = t // tm, tc // tm
    n_tiles = n_lat + n_ctx
    n_samples = scale1.shape[0] - 1
    tps = n_lat // n_samples
    n_grp = n_samples + 1
    n_g = len(dps)
    whole = [g for g in range(n_g) if dps[g].shape[0] == t_all]
    latent = [g for g in range(n_g) if dps[g].shape[0] != t_all]

    def body(*refs):
        dp_refs, w_refs = refs[:n_g], refs[n_g:2 * n_g]
        (paw_ref, x_ref, c_ref, dh_ref, sc_ref, g_ref, dx_ref, dsh_ref, dsc_ref, dg_ref, rbw_ref,
         du_buf, b_send, b_recv) = refs[2 * n_g:]
        i = pl.program_id(0)

        def exchange():
            x, y, c, chips = _place()
            chunks = _half_chunks(0, (2 * paw_ref.shape[1],), 16, which=(0,))
            return [_remote(paw_ref.at[2 * cx + cy, rows], rbw_ref.at[j, rows],
                            b_send.at[j * N_BULK + k], b_recv.at[j * N_BULK + k], (cx, cy, c))
                    for j, (cx, cy) in enumerate(chips) for k, (_, rows) in enumerate(chunks)]

        @pl.when(i == 0)
        def _():
            for cp in exchange():
                cp.start()

        acc = None
        for g in whole:
            part = _nt(dp_refs[g][...], w_refs[g][...])
            acc = part if acc is None else acc + part
        du_buf[...] = acc

        @pl.when(i < n_lat)
        def _():
            lat = None
            for g in latent:
                part = _nt(dp_refs[g][...], w_refs[g][...])
                lat = part if lat is None else lat + part
            du_buf[...] += lat

        duv = du_buf[...]
        xv = jnp.where(i < n_lat, x_ref[...], c_ref[...])
        rs = lax.rsqrt(jnp.mean(xv * xv, axis=-1, keepdims=True) + EPS)
        xh = xv * rs
        n = xh * g_ref[...]
        dn = duv * sc_ref[0]
        dxh = dn * g_ref[...]
        dx = rs * (dxh - xh * jnp.mean(dxh * xh, axis=-1, keepdims=True))

        @pl.when(i < n_lat)
        def _():
            dx_ref[...] = dx + dh_ref[...]

        @pl.when((i % tps == 0) & (i <= n_lat))
        def _():
            dsh_ref[...] = jnp.zeros_like(dsh_ref)
            dsc_ref[...] = jnp.zeros_like(dsc_ref)

        @pl.when(i == 0)
        def _():
            dg_ref[...] = jnp.zeros_like(dg_ref)

        dsh_ref[0] += jnp.sum(duv, axis=0, keepdims=True)
        dsc_ref[0] += jnp.sum(duv * n, axis=0, keepdims=True)
        dg_ref[...] += jnp.sum(dn * xh, axis=0, keepdims=True)

        @pl.when(i == n_tiles - 1)
        def _():
            for cp in exchange():
                cp.wait_recv()
            for cp in exchange():
                cp.wait_send()

    lat = lambda i: (jnp.minimum(i, n_lat - 1), 0)
    grp = lambda i: (jnp.minimum(i // tps, n_samples), 0, 0)
    in_specs = []
    for g, dp in enumerate(dps):
        nrow = dp.shape[0] // tm
        in_specs.append(pl.BlockSpec((tm, dp.shape[1]), lambda i, nrow=nrow: (jnp.minimum(i, nrow - 1), 0)))
    for w in wts:
        in_specs.append(pl.BlockSpec(w.shape, lambda i: (0, 0), pipeline_mode=pl.Buffered(1)))
    any_spec = pl.BlockSpec(memory_space=pl.ANY)
    in_specs += [any_spec,
                 pl.BlockSpec((tm, D), lat), pl.BlockSpec((tm, D), lambda i: (jnp.maximum(i - n_lat, 0), 0)),
                 pl.BlockSpec((tm, D), lat), pl.BlockSpec((1, 1, D), grp), pl.BlockSpec((1, D), lambda i: (0, 0))]
    return _pallas(
        body, name="dgrad_norm_bwd", grid=(n_tiles,), in_specs=in_specs,
        out_specs=[pl.BlockSpec((tm, D), lat), pl.BlockSpec((1, 1, D), grp), pl.BlockSpec((1, 1, D), grp),
                   pl.BlockSpec((1, D), lambda i: (0, 0)), any_spec],
        out_shape=[jax.ShapeDtypeStruct((t, D), F32), jax.ShapeDtypeStruct((n_grp, 1, D), F32),
                   jax.ShapeDtypeStruct((n_grp, 1, D), F32), jax.ShapeDtypeStruct((1, D), F32),
                   jax.ShapeDtypeStruct((3,) + paw.shape[1:], paw.dtype)],
        scratch_shapes=[pltpu.VMEM((tm, D), F32), pltpu.SemaphoreType.DMA((3 * N_BULK,)),
                        pltpu.SemaphoreType.DMA((3 * N_BULK,))],
        compiler_params=_params(("arbitrary",)),
    )(*dps, *wts, paw, x2, ctx2, dh, scale1, norm_g)


TM_NORM = 512


def norm_mod_fwd(x2, ctx2, scale1, shift, norm_g):
    t = x2.shape[0]
    n_lat = t // TM_NORM
    assert ctx2.shape[0] == TM_NORM
    n_samples = scale1.shape[0] - 1
    tps = n_lat // n_samples

    def body(x_ref, c_ref, sc_ref, sh_ref, g_ref, u_ref):
        i = pl.program_id(0)
        xv = jnp.where(i < n_lat, x_ref[...], c_ref[...])
        rs = lax.rsqrt(jnp.mean(xv * xv, axis=-1, keepdims=True) + EPS)
        u = xv * rs * g_ref[...] * sc_ref[0] + sh_ref[0]
        u_ref[...] = u.astype(u_ref.dtype)

    grp = lambda i: (jnp.minimum(i // tps, n_samples), 0, 0)
    return _pallas(
        body, name="norm_mod_fwd", grid=(n_lat + 1,),
        in_specs=[pl.BlockSpec((TM_NORM, D), lambda i: (jnp.minimum(i, n_lat - 1), 0)),
                  pl.BlockSpec((TM_NORM, D), lambda i: (0, 0)),
                  pl.BlockSpec((1, 1, D), grp), pl.BlockSpec((1, 1, D), grp),
                  pl.BlockSpec((1, D), lambda i: (0, 0))],
        out_specs=pl.BlockSpec((TM_NORM, D), lambda i: (i, 0)),
        out_shape=jax.ShapeDtypeStruct((t + TM_NORM, D), BF16),
        compiler_params=_params(("parallel",)),
    )(x2, ctx2, scale1, shift, norm_g)


CONV_CB = 256
CONV_NCB = D // CONV_CB
H_OFF = 16


H_CB = 128
H_SPAN = GRID_W + 2 * H_OFF - 8


def _conv_scratch(vertical):
    if vertical:
        return [pltpu.VMEM((GRID_H + 2 * CONV_PAD, GRID_W, CONV_CB), F32)]
    return [pltpu.VMEM((GRID_H, GRID_W + 2 * H_OFF, H_CB), F32), pltpu.VMEM((7, GRID_H, H_SPAN, H_CB), F32)]


def _conv_fill(bufs, img, vertical):
    pad_ref = bufs[0]
    pad_ref[...] = jnp.zeros_like(pad_ref)
    if vertical:
        pad_ref[pl.ds(CONV_PAD, GRID_H)] = img
        return
    pad_ref[:, pl.ds(H_OFF, GRID_W), :] = img

    def shift(r, carry):
        for s in range(1, 8):
            bufs[1][s - 1, r] = pad_ref[r, pl.ds(s, H_SPAN), :]
        return carry

    lax.fori_loop(0, GRID_H, shift, 0)


def _conv_window(bufs, k, vertical, r, w0=0, nw=GRID_W, lanes=slice(None)):
    if vertical:
        return bufs[0][r + k, pl.ds(w0, nw), lanes]
    off = H_OFF - CONV_PAD + k
    if off % 8 == 0:
        return bufs[0][r, pl.ds(off + w0, nw), lanes]
    return bufs[1][off % 8 - 1, r, pl.ds(off - off % 8 + w0, nw), lanes]


def _conv_col_blocks(vertical):
    if vertical:
        return [pl.ds(0, CONV_CB)]
    return [pl.ds(i * H_CB, H_CB) for i in range(CONV_CB // H_CB)]


def _rows(r):
    return pl.ds(pl.multiple_of(r * GRID_W, GRID_W), GRID_W)


def conv_fwd(p1, conv_w, conv_b, n_samples):
    t = n_samples * SEQ

    def make(vertical, prev):
        n_buf = len(_conv_scratch(vertical))

        def body(gv_ref, gg_ref, w_ref, b_ref, *rest):
            o_ref, bufs = rest[-1 - n_buf], rest[-n_buf:]
            for cols in _conv_col_blocks(vertical):
                a = gv_ref[:, cols].astype(F32) * _sigmoid(gg_ref[:, cols].astype(F32))
                _conv_fill(bufs, a.reshape(GRID_H, GRID_W, a.shape[-1]), vertical)

                def row(r, carry, cols=cols):
                    acc = jnp.zeros((GRID_W, cols.size), F32) + b_ref[:, cols]
                    for k in range(CONV_K):
                        acc = acc + _conv_window(bufs, k, vertical, r) * w_ref[pl.ds(k, 1), cols]
                    o_ref[_rows(r), cols] = acc
                    return carry

                lax.fori_loop(0, GRID_H, row, 0)

        cb0 = CONV_NCB // 2 if vertical else 0
        in_specs = [pl.BlockSpec((SEQ, CONV_CB), lambda b, j: (b, 2 * (cb0 + j))),
                    pl.BlockSpec((SEQ, CONV_CB), lambda b, j: (b, 2 * (cb0 + j) + 1)),
                    pl.BlockSpec((CONV_K + 1, CONV_CB), lambda b, j: (0, cb0 + j)),
                    pl.BlockSpec((1, CONV_CB), lambda b, j: (0, cb0 + j))]
        args = [p1, p1, conv_w, conv_b]
        aliases = {}
        if prev is not None:
            in_specs.append(pl.BlockSpec(memory_space=pl.ANY))
            args.append(prev)
            aliases = {4: 0}
        return _pallas(
            body, name="conv_fwd_v" if vertical else "conv_fwd_h", grid=(n_samples, CONV_NCB // 2),
            in_specs=in_specs,
            out_specs=pl.BlockSpec((SEQ, CONV_CB), lambda b, j: (b, cb0 + j)),
            out_shape=jax.ShapeDtypeStruct((t, D), F32),
            scratch_shapes=_conv_scratch(vertical),
            input_output_aliases=aliases,
            compiler_params=_params(("parallel", "parallel")),
        )(*args)

    return make(True, make(False, None))


def conv_bwd(p1, daconv, conv_w, n_samples):
    t = n_samples * SEQ

    def make(vertical, prev):
        n_buf = len(_conv_scratch(vertical))

        def body(gv_ref, gg_ref, dy_ref, w_ref, *rest):
            dp_ref, dw_ref, db_ref = rest[-3 - 2 * n_buf - 1:-2 * n_buf - 1]
            a_bufs, d_bufs, da_ref = rest[-2 * n_buf - 1:-n_buf - 1], rest[-n_buf - 1:-1], rest[-1]
            for cols in _conv_col_blocks(vertical):
                width = cols.size
                gv = gv_ref[:, cols].astype(F32)
                sg = _sigmoid(gg_ref[:, cols].astype(F32))
                _conv_fill(a_bufs, (gv * sg).reshape(GRID_H, GRID_W, width), vertical)
                _conv_fill(d_bufs, dy_ref[:, cols].reshape(GRID_H, GRID_W, width), vertical)

                def row(r, carry, cols=cols, width=width):
                    acc = jnp.zeros((GRID_W, width), F32)
                    for k in range(CONV_K):
                        acc = acc + _conv_window(d_bufs, CONV_K - 1 - k, vertical, r) * w_ref[pl.ds(k, 1), cols]
                    da_ref[_rows(r), cols] = acc
                    return carry

                lax.fori_loop(0, GRID_H, row, 0)
                da = da_ref[:, cols]
                dp_ref[:, pl.ds(cols.start, width)] = (da * sg).astype(dp_ref.dtype)
                dp_ref[:, pl.ds(CONV_CB + cols.start, width)] = (da * gv * sg * (1.0 - sg)).astype(dp_ref.dtype)

                for lb in range(width // 128):
                    lanes = pl.ds(lb * 128, 128)
                    dy_lanes = pl.ds(cols.start + lb * 128, 128)

                    def wrow(r, accs, lanes=lanes, dy_lanes=dy_lanes):
                        for w0 in range(0, GRID_W, 8):
                            dyv = dy_ref[pl.ds(pl.multiple_of(r * GRID_W, GRID_W) + w0, 8), dy_lanes]
                            accs = tuple(accs[k] + _conv_window(a_bufs, k, vertical, r, w0, 8, lanes) * dyv
                                         for k in range(CONV_K))
                        return accs

                    accs = lax.fori_loop(0, GRID_H, wrow, tuple(jnp.zeros((8, 128), F32) for _ in range(CONV_K)))
                    for k in range(CONV_K):
                        dw_ref[0, pl.ds(k, 1), dy_lanes] = jnp.sum(accs[k], axis=0, keepdims=True)
            dw_ref[0, pl.ds(CONV_K, 1), :] = jnp.zeros((1, CONV_CB), F32)
            db_ref[0] = jnp.sum(dy_ref[...], axis=0, keepdims=True)

        cb0 = CONV_NCB // 2 if vertical else 0
        in_specs = [pl.BlockSpec((SEQ, CONV_CB), lambda b, j: (b, 2 * (cb0 + j))),
                    pl.BlockSpec((SEQ, CONV_CB), lambda b, j: (b, 2 * (cb0 + j) + 1)),
                    pl.BlockSpec((SEQ, CONV_CB), lambda b, j: (b, cb0 + j)),
                    pl.BlockSpec((CONV_K + 1, CONV_CB), lambda b, j: (0, cb0 + j))]
        args = [p1, p1, daconv, conv_w]
        aliases = {}
        if prev is not None:
            in_specs += [pl.BlockSpec(memory_space=pl.ANY)] * 3
            args += list(prev)
            aliases = {4: 0, 5: 1, 6: 2}
        return _pallas(
            body, name="conv_bwd_v" if vertical else "conv_bwd_h", grid=(n_samples, CONV_NCB // 2),
            in_specs=in_specs,
            out_specs=[pl.BlockSpec((SEQ, 2 * CONV_CB), lambda b, j: (b, cb0 + j)),
                       pl.BlockSpec((1, CONV_K + 1, CONV_CB), lambda b, j: (b, 0, cb0 + j)),
                       pl.BlockSpec((1, 1, CONV_CB), lambda b, j: (b, 0, cb0 + j))],
            out_shape=[jax.ShapeDtypeStruct((t, 2 * D), BF16),
                       jax.ShapeDtypeStruct((n_samples, CONV_K + 1, D), F32),
                       jax.ShapeDtypeStruct((n_samples, 1, D), F32)],
            scratch_shapes=_conv_scratch(vertical) + _conv_scratch(vertical) + [pltpu.VMEM((SEQ, CONV_CB), F32)],
            input_output_aliases=aliases,
            compiler_params=_params(("parallel", "parallel")),
        )(*args)

    return make(True, make(False, None))


TM_EW = 512


def ln_gate_proj(aconv, z, ln_g, ln_b, conv_proj):
    t = aconv.shape[0]

    def body(a_ref, z_ref, g_ref, b_ref, w_ref, o_ref, y_ref):
        a = a_ref[...]
        mu = jnp.mean(a, axis=-1, keepdims=True)
        xc = a - mu
        rstd = lax.rsqrt(jnp.mean(xc * xc, axis=-1, keepdims=True) + EPS)
        l = xc * rstd * g_ref[...] + b_ref[...]
        ac = _b16(_silu(l) * _silu(z_ref[...].astype(F32)))
        o_ref[...] = ac
        y_ref[...] = _nn(ac, w_ref[...]).astype(y_ref.dtype)

    row = pl.BlockSpec((TM_OUT, D), lambda i: (i, 0))
    vec = pl.BlockSpec((1, D), lambda i: (0, 0))
    return _pallas(
        body, name="ln_gate_proj", grid=(t // TM_OUT,),
        in_specs=[row, row, vec, vec, pl.BlockSpec((D, D), lambda i: (0, 0))], out_specs=[row, row],
        out_shape=[jax.ShapeDtypeStruct((t, D), BF16), jax.ShapeDtypeStruct((t, D), BF16)],
        compiler_params=_params(("parallel",)),
    )(aconv, z, ln_g, ln_b, conv_proj)


def ln_gate_bwd(aconv, z, dyc, conv_proj, ln_g, ln_b, ac):
    t = aconv.shape[0]

    def body(a_ref, z_ref, d_ref, w_ref, g_ref, b_ref, ac_ref, da_ref, dz_ref, dg_ref, db_ref, gw_ref):
        @pl.when(pl.program_id(0) == 0)
        def _():
            gw_ref[...] = jnp.zeros_like(gw_ref)

        gw_ref[...] += _tn(ac_ref[...], d_ref[...])
        a = a_ref[...]
        zv = z_ref[...].astype(F32)
        dac_v = _nt(d_ref[...], w_ref[...])
        mu = jnp.mean(a, axis=-1, keepdims=True)
        xc = a - mu
        rstd = lax.rsqrt(jnp.mean(xc * xc, axis=-1, keepdims=True) + EPS)
        xh = xc * rstd
        l = xh * g_ref[...] + b_ref[...]
        dz_ref[...] = (dac_v * _silu(l) * _dsilu(zv)).astype(dz_ref.dtype)
        dl = dac_v * _silu(zv) * _dsilu(l)
        dxh = dl * g_ref[...]
        da_ref[...] = rstd * (dxh - jnp.mean(dxh, axis=-1, keepdims=True)
                              - xh * jnp.mean(dxh * xh, axis=-1, keepdims=True))

        @pl.when(pl.program_id(0) == 0)
        def _():
            dg_ref[...] = jnp.zeros_like(dg_ref)
            db_ref[...] = jnp.zeros_like(db_ref)

        dg_ref[...] += jnp.sum(dl * xh, axis=0, keepdims=True)
        db_ref[...] += jnp.sum(dl, axis=0, keepdims=True)

    row = pl.BlockSpec((TM_EW, D), lambda i: (i, 0))
    vec = pl.BlockSpec((1, D), lambda i: (0, 0))
    return _pallas(
        body, name="ln_gate_bwd", grid=(t // TM_EW,),
        in_specs=[row, row, row, pl.BlockSpec((D, D), lambda i: (0, 0)), vec, vec, row],
        out_specs=[row, row, vec, vec, pl.BlockSpec((D, D), lambda i: (0, 0))],
        out_shape=[jax.ShapeDtypeStruct((t, D), F32), jax.ShapeDtypeStruct((t, D), BF16),
                   jax.ShapeDtypeStruct((1, D), F32), jax.ShapeDtypeStruct((1, D), F32),
                   jax.ShapeDtypeStruct((D, D), F32)],
        compiler_params=_params(("arbitrary",)),
    )(aconv, z, dyc, conv_proj, ln_g, ln_b, ac)


TM_PREP = 256
PREP_LAT = SEQ // TM_PREP
PREP_ALL = SEQ_ALL // TM_PREP


def _chunk_tri(n, upper):
    r = lax.broadcasted_iota(jnp.int32, (n, n), 0)
    c = lax.broadcasted_iota(jnp.int32, (n, n), 1)
    same = (r // CHUNK) == (c // CHUNK)
    keep = (c >= r) if upper else (c <= r)
    return jnp.where(same & keep, 1.0, 0.0).astype(F32)


def _split3(v):
    hi = v.astype(BF16)
    r1 = v - hi.astype(F32)
    mid = r1.astype(BF16)
    lo = (r1 - mid.astype(F32)).astype(BF16)
    return jnp.stack([hi, mid, lo])


def _chunk_sums(v, upper):
    tri = _chunk_tri(v.shape[0], upper).astype(BF16)
    pieces = _split3(v)
    return (_nn(tri, pieces[0]) + _nn(tri, pieces[1])) + _nn(tri, pieces[2])


def _gate_logits(ab, up3_ref, bias_ref):
    assert ab.dtype == BF16
    return ((_nn(ab, up3_ref[0]) + _nn(ab, up3_ref[1])) + _nn(ab, up3_ref[2])) + bias_ref[...]


def _prep_tile_maps(n_samples):
    n_lat = n_samples * PREP_LAT

    def seq_map(i):
        return jnp.where(i < n_lat, i // PREP_LAT, i - n_lat), jnp.where(i < n_lat, i % PREP_LAT, PREP_LAT)

    return n_lat, seq_map


def gla_prep_fwd(p3, upf, upb, bias_f, bias_b, n_samples):
    n_lat, seq_map = _prep_tile_maps(n_samples)
    n_tiles = n_lat + n_samples

    def body(v_ref, q_ref, k_ref, ab_ref, upf_ref, upb_ref, bf_ref, bb_ref, qo, ko, vo, cf, cb):
        i = pl.program_id(0)
        qo[0] = jnp.where(i < n_lat, q_ref[...].astype(F32) * Q_SCALE, 0.0)
        ko[0] = k_ref[...]
        vo[0] = v_ref[...]
        ab = ab_ref[...]
        gf = _log_sigmoid(_gate_logits(ab, upf_ref, bf_ref)) * (1.0 / GATE_TAU)
        gb = _log_sigmoid(_gate_logits(ab, upb_ref, bb_ref)) * (1.0 / GATE_TAU)
        cf[0] = _chunk_sums(gf, False)
        cb[0] = _chunk_sums(gb, True)

    def o_spec(w):
        return pl.BlockSpec((1, TM_PREP, w), lambda i: (*seq_map(i), 0))

    full = lambda shape: pl.BlockSpec(shape, lambda i: (0,) * len(shape))
    return _pallas(
        body, name="gla_prep_fwd", grid=(n_tiles,),
        in_specs=[pl.BlockSpec((TM_PREP, 1024), lambda i: (i, O3_V // 1024)),
                  pl.BlockSpec((TM_PREP, 512), lambda i: (i, O3_Q // 512)),
                  pl.BlockSpec((TM_PREP, 512), lambda i: (i, O3_K // 512)),
                  pl.BlockSpec((TM_PREP, 128), lambda i: (i, O3_AB // 128)),
                  full((3, 128, GLA_DK)), full((3, 128, GLA_DK)), full((1, GLA_DK)), full((1, GLA_DK))],
        out_specs=[o_spec(GLA_DK), o_spec(GLA_DK), o_spec(D), o_spec(GLA_DK), o_spec(GLA_DK)],
        out_shape=[jax.ShapeDtypeStruct((n_samples, SEQ_ALL, GLA_DK), F32),
                   jax.ShapeDtypeStruct((n_samples, SEQ_ALL, GLA_DK), p3.dtype),
                   jax.ShapeDtypeStruct((n_samples, SEQ_ALL, D), p3.dtype),
                   jax.ShapeDtypeStruct((n_samples, SEQ_ALL, GLA_DK), F32),
                   jax.ShapeDtypeStruct((n_samples, SEQ_ALL, GLA_DK), F32)],
        compiler_params=_params(("parallel",)),
    )(p3, p3, p3, p3, upf, upb, bias_f, bias_b)


def gla_prep_bwd(p3, dq_f, dq_b, dk_f, dk_b, dv_f, dv_b, dc_f, dc_b, upf, upb, bias_f, bias_b, n_samples):
    n_lat, seq_map = _prep_tile_maps(n_samples)
    n_tiles = n_lat + n_samples

    def body(ab_ref, dqf, dqb, dkf, dkb, dvf, dvb, dcf, dcb, upf_ref, upb_ref, bf_ref, bb_ref,
             dp_ref, duf_ref, dub_ref, dbf_ref, dbb_ref):
        i = pl.program_id(0)
        both = lambda a, b: a[0].astype(F32) + b[0].astype(F32)
        dp_ref[:, pl.ds(O3_V, D)] = both(dvf, dvb).astype(dp_ref.dtype)
        dq = jnp.where(i < n_lat, both(dqf, dqb) * Q_SCALE, 0.0)
        dp_ref[:, pl.ds(O3_Q, GLA_DK)] = dq.astype(dp_ref.dtype)
        dp_ref[:, pl.ds(O3_K, GLA_DK)] = both(dkf, dkb).astype(dp_ref.dtype)
        ab = ab_ref[...]
        zf = _gate_logits(ab, upf_ref, bf_ref)
        zb = _gate_logits(ab, upb_ref, bb_ref)
        dgf = _chunk_sums(dcf[0], True)
        dgb = _chunk_sums(dcb[0], False)
        dzf = _b16(dgf * (1.0 / GATE_TAU) * _sigmoid(-zf))
        dzb = _b16(dgb * (1.0 / GATE_TAU) * _sigmoid(-zb))
        dab = _nt(dzf, upf_ref[0]) + _nt(dzb, upb_ref[0])
        dp_ref[:, pl.ds(O3_AB, 128)] = dab.astype(dp_ref.dtype)

        @pl.when(i == 0)
        def _():
            duf_ref[...] = jnp.zeros_like(duf_ref)
            dub_ref[...] = jnp.zeros_like(dub_ref)
            dbf_ref[...] = jnp.zeros_like(dbf_ref)
            dbb_ref[...] = jnp.zeros_like(dbb_ref)

        duf_ref[...] += _tn(ab, dzf)
        dub_ref[...] += _tn(ab, dzb)
        dbf_ref[...] += jnp.sum(dzf.astype(F32), axis=0, keepdims=True)
        dbb_ref[...] += jnp.sum(dzb.astype(F32), axis=0, keepdims=True)

    def s_spec(w):
        return pl.BlockSpec((1, TM_PREP, w), lambda i: (*seq_map(i), 0))

    full = lambda shape: pl.BlockSpec(shape, lambda i: (0,) * len(shape))
    return _pallas(
        body, name="gla_prep_bwd", grid=(n_tiles,),
        in_specs=[pl.BlockSpec((TM_PREP, 128), lambda i: (i, O3_AB // 128)),
                  s_spec(GLA_DK), s_spec(GLA_DK), s_spec(GLA_DK), s_spec(GLA_DK), s_spec(D), s_spec(D),
                  s_spec(GLA_DK), s_spec(GLA_DK),
                  full((3, 128, GLA_DK)), full((3, 128, GLA_DK)), full((1, GLA_DK)), full((1, GLA_DK))],
        out_specs=[pl.BlockSpec((TM_PREP, W3), lambda i: (i, 0)),
                   full((128, GLA_DK)), full((128, GLA_DK)), full((1, GLA_DK)), full((1, GLA_DK))],
        out_shape=[jax.ShapeDtypeStruct((n_tiles * TM_PREP, W3), BF16),
                   jax.ShapeDtypeStruct((128, GLA_DK), F32), jax.ShapeDtypeStruct((128, GLA_DK), F32),
                   jax.ShapeDtypeStruct((1, GLA_DK), F32), jax.ShapeDtypeStruct((1, GLA_DK), F32)],
        compiler_params=_params(("arbitrary",)),
    )(p3, dq_f, dq_b, dk_f, dk_b, dv_f, dv_b, dc_f, dc_b, upf, upb, bias_f, bias_b)


def _sub_blocks(rev):
    if NSUB == 1:
        return [((0, CHUNK), CHUNK // 2, (0, CHUNK))]
    out = []
    for s in range(NSUB):
        rows = (s * SUB, SUB)
        if rev:
            ref = (s + 1) * SUB if s < NSUB - 1 else None
            cols = (s * SUB, CHUNK - s * SUB)
        else:
            ref = s * SUB - 1 if s > 0 else None
            cols = (0, (s + 1) * SUB)
        out.append((rows, ref, cols))
    return out


def _sub_mask(rows, cols, rev):
    r = rows[0] + lax.broadcasted_iota(jnp.int32, (rows[1], cols[1]), 0)
    c = cols[0] + lax.broadcasted_iota(jnp.int32, (rows[1], cols[1]), 1)
    return (c >= r) if rev else (c <= r)


def _sub_operands(qc, kc, cc, rows, ref, cols):
    cref = jnp.zeros((1, HEAD_K), F32) if ref is None else cc[ref:ref + 1]
    eq = jnp.exp(cc[rows[0]:rows[0] + rows[1]] - cref)
    ek = jnp.exp(cref - cc[cols[0]:cols[0] + cols[1]])
    qs = qc[rows[0]:rows[0] + rows[1]] * eq
    kk = kc[cols[0]:cols[0] + cols[1]] * ek
    return qs, kk, eq, ek


SCAN_ROWS = 256
SCAN_CHUNKS = SCAN_ROWS // CHUNK
SCAN_STEPS = SEQ_ALL // SCAN_ROWS
LAT_BLOCKS = SEQ // SCAN_ROWS


def _scan_block(t, rev):
    if rev:
        return SCAN_STEPS - 1 - t
    return jnp.where(t == 0, SCAN_STEPS - 1, t - 1)


def _scan_lat_block(t, rev):
    if rev:
        return jnp.minimum(SCAN_STEPS - 1 - t, LAT_BLOCKS - 1)
    return jnp.maximum(t - 1, 0)


def _head_cols(h):
    return pl.ds(h * HEAD_K, HEAD_K), pl.ds(h * HEAD_V, HEAD_V)


def gla_scan_fwd(q, k, v, cum, *, rev, name):
    n = q.shape[0]

    def body(q_ref, k_ref, v_ref, c_ref, o_ref, s_ref, sfin_ref, st):
        t = pl.program_id(1)

        @pl.when(t == 0)
        def _():
            st[...] = jnp.zeros_like(st)

        def chunk(j, carry):
            lj = SCAN_CHUNKS - 1 - j if rev else j
            r0 = lj * CHUNK
            rws = pl.ds(r0, CHUNK)
            for h in range(HEADS):
                kcols, vcols = _head_cols(h)
                qc, kc, cc = q_ref[0, rws, kcols], k_ref[0, rws, kcols], c_ref[0, rws, kcols]
                vc = v_ref[0, rws, vcols]
                s_in = st[h]
                s_ref[0, h, j] = _b16(s_in)
                edge = cc[0:1] if rev else cc[CHUNK - 1:CHUNK]
                ke = kc * jnp.exp(edge - cc)
                st[h] = s_in * jnp.exp(edge) + _tn(_b16(vc), _b16(ke))
                o_inter = _nt(_b16(qc * jnp.exp(cc)), _b16(s_in))
                vb = _b16(vc)
                for rows, ref, cols in _sub_blocks(rev):
                    qs, kk, _, _ = _sub_operands(qc, kc, cc, rows, ref, cols)
                    a = jnp.where(_sub_mask(rows, cols, rev), _nt(_b16(qs), _b16(kk)), 0.0)
                    o_s = _nn(_b16(a), vb[cols[0]:cols[0] + cols[1]])
                    o_ref[0, pl.ds(r0 + rows[0], rows[1]), vcols] = _b16(o_inter[rows[0]:rows[0] + rows[1]] + o_s)
            return carry

        for j in range(SCAN_CHUNKS):
            chunk(j, 0)

        @pl.when(t == SCAN_STEPS - 1)
        def _():
            sfin_ref[0] = st[...]

    def spec(w):
        return pl.BlockSpec((1, SCAN_ROWS, w), lambda b, t: (b, _scan_block(t, rev), 0))

    return _pallas(
        body, name=name, grid=(n, SCAN_STEPS),
        in_specs=[spec(GLA_DK), spec(GLA_DK), spec(D), spec(GLA_DK)],
        out_specs=[pl.BlockSpec((1, SCAN_ROWS, D), lambda b, t: (b, _scan_lat_block(t, rev), 0)),
                   pl.BlockSpec((1, HEADS, SCAN_CHUNKS, HEAD_V, HEAD_K), lambda b, t: (b, 0, t, 0, 0)),
                   pl.BlockSpec((1, HEADS, HEAD_V, HEAD_K), lambda b, t: (b, 0, 0, 0))],
        out_shape=[jax.ShapeDtypeStruct((n, SEQ, D), BF16),
                   jax.ShapeDtypeStruct((n, HEADS, NCHUNK, HEAD_V, HEAD_K), BF16),
                   jax.ShapeDtypeStruct((n, HEADS, HEAD_V, HEAD_K), F32)],
        scratch_shapes=[pltpu.VMEM((HEADS, HEAD_V, HEAD_K), F32)],
        compiler_params=_params(("parallel", "arbitrary")),
    )(q, k, v, cum)


def gla_scan_bwd(q, k, v, cum, s_all, s_fin, do, *, rev, name, rider=None):
    n = q.shape[0]

    def body(q_ref, k_ref, v_ref, c_ref, s_ref, sfin_ref, do_ref, *rest):
        if rider is not None:
            ride_in, rest = rest[0], rest[1:]
        dq_ref, dk_ref, dv_ref, dc_ref = rest[:4]
        if rider is not None:
            ride_out, rest = rest[4], rest[:4] + rest[5:]
        dst, s_next, dq_acc, dk_acc, dv_acc = rest[4:9]
        t = SCAN_STEPS - 1 - pl.program_id(1)

        if rider is not None:
            def copies():
                send, recv = rest[9], rest[10]
                if rider[0] == "swap":
                    return _pair_copies([], ride_in, [], ride_out, send, recv)
                x, y, c, chips = _place()
                return [_remote(ride_in.at[2 * cx + cy, rows], ride_out.at[pj, rows], send.at[pj * P_ROW_CHUNKS + pi],
                                recv.at[pj * P_ROW_CHUNKS + pi], (cx, cy, c))
                        for pj, (cx, cy) in enumerate(chips)
                        for pi, (_, rows) in enumerate(_half_chunks(0, (0, 2 * ride_in.shape[1]), 16, which=(1,)))]

            @pl.when((pl.program_id(0) == 0) & (pl.program_id(1) == 0))
            def _():
                for cp in copies():
                    cp.start()

            @pl.when((pl.program_id(0) == n - 1) & (pl.program_id(1) == SCAN_STEPS - 1))
            def _():
                for cp in copies():
                    cp.wait_recv()
                for cp in copies():
                    cp.wait_send()

        @pl.when(pl.program_id(1) == 0)
        def _():
            dst[...] = jnp.zeros_like(dst)
            s_next[...] = sfin_ref[0]

        def chunk(jj, carry):
            j = SCAN_CHUNKS - 1 - jj
            lj = SCAN_CHUNKS - 1 - j if rev else j
            rws = pl.ds(lj * CHUNK, CHUNK)
            for h in range(HEADS):
                kcols, vcols = _head_cols(h)
                qc, kc, cc = q_ref[0, rws, kcols], k_ref[0, rws, kcols], c_ref[0, rws, kcols]
                vc = v_ref[0, rws, vcols]
                doc = jnp.where(t > 0, do_ref[0, rws, vcols], 0.0)
                s_in = s_ref[0, h, j]
                s_out = s_next[h]
                ds_out = dst[h]
                edge = cc[0:1] if rev else cc[CHUNK - 1:CHUNK]
                e_q = jnp.exp(cc)
                e_k = jnp.exp(edge - cc)
                dob = _b16(doc)
                dsb = _b16(ds_out)
                dst[h] = ds_out * jnp.exp(edge) + _tn(dob, _b16(qc * e_q))
                s_next[h] = s_in.astype(F32)
                dq_acc[h] = e_q * _nn(dob, s_in)
                dk_acc[h] = e_k * _nn(_b16(vc), dsb)
                dv_acc[h] = _nt(_b16(kc * e_k), dsb)
                vb = _b16(vc)
                for rows, ref, cols in _sub_blocks(rev):
                    qs, kk, eq, ek = _sub_operands(qc, kc, cc, rows, ref, cols)
                    mask = _sub_mask(rows, cols, rev)
                    rsl = slice(rows[0], rows[0] + rows[1])
                    csl = pl.ds(cols[0], cols[1])
                    qsb, kkb = _b16(qs), _b16(kk)
                    a = jnp.where(mask, _nt(qsb, kkb), 0.0)
                    da = _b16(jnp.where(mask, _nt(dob[rsl], vb[cols[0]:cols[0] + cols[1]]), 0.0))
                    dq_acc[h, pl.ds(rows[0], rows[1]), :] += _nn(da, kkb) * eq
                    dk_acc[h, csl, :] += _tn(da, qsb) * ek
                    dv_acc[h, csl, :] += _tn(_b16(a), dob[rsl])
                dq = dq_acc[h]
                dk = dk_acc[h]
                dc = qc * dq - kc * dk
                bnd = jnp.sum(ds_out * s_out, axis=0, keepdims=True)
                edge_row = 0 if rev else CHUNK - 1
                is_edge = lax.broadcasted_iota(jnp.int32, (CHUNK, HEAD_K), 0) == edge_row
                dq_ref[0, rws, kcols] = _b16(dq)
                dk_ref[0, rws, kcols] = _b16(dk)
                dv_ref[0, rws, vcols] = _b16(dv_acc[h])
                dc_ref[0, rws, kcols] = dc + jnp.where(is_edge, bnd, 0.0)
            return carry

        for jj in range(SCAN_CHUNKS):
            chunk(jj, 0)

    def step_of(u):
        return SCAN_STEPS - 1 - u

    def spec(w):
        return pl.BlockSpec((1, SCAN_ROWS, w), lambda b, u: (b, _scan_block(step_of(u), rev), 0))

    in_specs = [spec(GLA_DK), spec(GLA_DK), spec(D), spec(GLA_DK),
                pl.BlockSpec((1, HEADS, SCAN_CHUNKS, HEAD_V, HEAD_K), lambda b, u: (b, 0, step_of(u), 0, 0)),
                pl.BlockSpec((1, HEADS, HEAD_V, HEAD_K), lambda b, u: (b, 0, 0, 0)),
                pl.BlockSpec((1, SCAN_ROWS, D), lambda b, u: (b, _scan_lat_block(step_of(u), rev), 0))]
    out_specs = [spec(GLA_DK), spec(GLA_DK), spec(D), spec(GLA_DK)]
    out_shape = [jax.ShapeDtypeStruct((n, SEQ_ALL, GLA_DK), BF16), jax.ShapeDtypeStruct((n, SEQ_ALL, GLA_DK), BF16),
                 jax.ShapeDtypeStruct((n, SEQ_ALL, D), BF16), jax.ShapeDtypeStruct((n, SEQ_ALL, GLA_DK), F32)]
    scratch = [pltpu.VMEM((HEADS, HEAD_V, HEAD_K), F32), pltpu.VMEM((HEADS, HEAD_V, HEAD_K), F32),
               pltpu.VMEM((HEADS, CHUNK, HEAD_K), F32), pltpu.VMEM((HEADS, CHUNK, HEAD_K), F32),
               pltpu.VMEM((HEADS, CHUNK, HEAD_V), F32)]
    args = [q, k, v, cum, s_all, s_fin, do]
    if rider is not None:
        kind, arr = rider
        any_spec = pl.BlockSpec(memory_space=pl.ANY)
        in_specs.append(any_spec)
        out_specs.append(any_spec)
        args.append(arr)
        if kind == "swap":
            out_shape += _pair_got_shapes([], arr)
            n_cp = _pair_count([], arr)
        else:
            out_shape.append(jax.ShapeDtypeStruct((3,) + arr.shape[1:], arr.dtype))
            n_cp = 3 * P_ROW_CHUNKS
        scratch += [pltpu.SemaphoreType.DMA((n_cp,)), pltpu.SemaphoreType.DMA((n_cp,))]
    return _pallas(
        body, name=name, grid=(n, SCAN_STEPS), in_specs=in_specs, out_specs=out_specs, out_shape=out_shape,
        scratch_shapes=scratch,
        compiler_params=_params(("parallel" if rider is None else "arbitrary", "arbitrary")),
    )(*args)


def gla_out_proj(o_f, o_b, r, gnorm, gla_proj):
    n = o_f.shape[0]
    tiles = SEQ // TM_OUT

    def body(of_ref, ob_ref, r_ref, g_ref, w_ref, og_ref, y_ref):
        for h in range(HEADS):
            cols = pl.ds(h * HEAD_V, HEAD_V)
            o = of_ref[0, :, cols].astype(F32) + ob_ref[0, :, cols].astype(F32)
            rs = lax.rsqrt(jnp.mean(o * o, axis=-1, keepdims=True) + EPS)
            og_ref[:, cols] = (o * rs * g_ref[...] * _silu(r_ref[:, cols].astype(F32))).astype(og_ref.dtype)
        y_ref[...] = _nn(og_ref[...], w_ref[...]).astype(y_ref.dtype)

    ospec = pl.BlockSpec((1, TM_OUT, D), lambda b, j: (b, j, 0))
    row = pl.BlockSpec((TM_OUT, D), lambda b, j: (b * tiles + j, 0))
    return _pallas(
        body, name="gla_out_proj", grid=(n, tiles),
        in_specs=[ospec, ospec, row, pl.BlockSpec((1, HEAD_V), lambda b, j: (0, 0)),
                  pl.BlockSpec((D, D), lambda b, j: (0, 0))],
        out_specs=[row, row],
        out_shape=[jax.ShapeDtypeStruct((n * SEQ, D), BF16), jax.ShapeDtypeStruct((n * SEQ, D), BF16)],
        compiler_params=_params(("parallel", "parallel")),
    )(o_f, o_b, r, gnorm, gla_proj)


def gla_out_bwd(o_f, o_b, r, dyg, gla_proj, gnorm, og):
    n = o_f.shape[0]
    tiles = SEQ // TM_EW

    def body(of_ref, ob_ref, r_ref, d_ref, w_ref, g_ref, og_ref, do_ref, dr_ref, dg_ref, gw_ref, dog_buf):
        @pl.when((pl.program_id(0) == 0) & (pl.program_id(1) == 0))
        def _():
            dg_ref[...] = jnp.zeros_like(dg_ref)
            gw_ref[...] = jnp.zeros_like(gw_ref)

        gw_ref[...] += _tn(og_ref[...], d_ref[...])
        dog_buf[...] = _nt(d_ref[...], w_ref[...])
        for h in range(HEADS):
            cols = pl.ds(h * HEAD_V, HEAD_V)
            o = of_ref[0, :, cols].astype(F32) + ob_ref[0, :, cols].astype(F32)
            rv = r_ref[:, cols].astype(F32)
            dv = dog_buf[:, cols]
            rs = lax.rsqrt(jnp.mean(o * o, axis=-1, keepdims=True) + EPS)
            oh = o * rs
            dr_ref[:, cols] = (dv * oh * g_ref[...] * _dsilu(rv)).astype(dr_ref.dtype)
            dn = dv * _silu(rv)
            dg_ref[...] += jnp.sum(dn * oh, axis=0, keepdims=True)
            doh = dn * g_ref[...]
            do_ref[0, :, cols] = _b16(rs * (doh - oh * jnp.mean(doh * oh, axis=-1, keepdims=True)))

    ospec = pl.BlockSpec((1, TM_EW, D), lambda b, j: (b, j, 0))
    row = pl.BlockSpec((TM_EW, D), lambda b, j: (b * tiles + j, 0))
    vec = pl.BlockSpec((1, HEAD_V), lambda b, j: (0, 0))
    return _pallas(
        body, name="gla_out_bwd", grid=(n, tiles),
        in_specs=[ospec, ospec, row, row, pl.BlockSpec((D, D), lambda b, j: (0, 0)), vec, row],
        out_specs=[ospec, row, vec, pl.BlockSpec((D, D), lambda b, j: (0, 0))],
        out_shape=[jax.ShapeDtypeStruct((n, SEQ, D), BF16), jax.ShapeDtypeStruct((n * SEQ, D), BF16),
                   jax.ShapeDtypeStruct((1, HEAD_V), F32), jax.ShapeDtypeStruct((D, D), F32)],
        scratch_shapes=[pltpu.VMEM((TM_EW, D), F32)],
        compiler_params=_params(("arbitrary", "arbitrary")),
    )(o_f, o_b, r, dyg, gla_proj, gnorm, og)


TM_OUT = 512


def merge_out_final(p5, y_conv, y_gla, w_out, x2, gate, final_g, target, n_samples):
    t = x2.shape[0]
    tiles = SEQ // TM_OUT

    def body(mc_ref, mg_ref, yc_ref, yg_ref, w_ref, x_ref, gate_ref, g_ref, t_ref,
             mrg_ref, dh_ref, dmo_ref, dgate_ref, dg_ref, loss_ref):
        b, j = pl.program_id(0), pl.program_id(1)
        f = lambda ref: ref[...].astype(F32)
        merged = _b16(_sigmoid(f(mc_ref)) * f(yc_ref) + _sigmoid(f(mg_ref)) * f(yg_ref))
        mrg_ref[...] = merged
        mo_v = _nn(merged, w_ref[...])
        h = x_ref[...] + gate_ref[0] * mo_v
        rs = lax.rsqrt(jnp.mean(h * h, axis=-1, keepdims=True) + EPS)
        nh = h * rs
        err = nh * g_ref[...] - t_ref[...]
        dy = err * (1.0 / D)
        dn = dy * g_ref[...]
        dh = rs * (dn - nh * jnp.mean(dn * nh, axis=-1, keepdims=True))
        dh_ref[...] = dh
        dmo_ref[...] = (dh * gate_ref[0]).astype(dmo_ref.dtype)

        @pl.when(j == 0)
        def _():
            dgate_ref[...] = jnp.zeros_like(dgate_ref)

        @pl.when((b == 0) & (j == 0))
        def _():
            dg_ref[...] = jnp.zeros_like(dg_ref)
            loss_ref[...] = jnp.zeros_like(loss_ref)

        dgate_ref[0] += jnp.sum(dh * mo_v, axis=0, keepdims=True)
        dg_ref[...] += jnp.sum(dy * nh, axis=0, keepdims=True)
        loss_ref[...] += (0.5 / D) * jnp.sum(err * err)

    row = pl.BlockSpec((TM_OUT, D), lambda b, j: (b * tiles + j, 0))
    per = pl.BlockSpec((1, 1, D), lambda b, j: (b, 0, 0))
    vec = pl.BlockSpec((1, D), lambda b, j: (0, 0))
    return _pallas(
        body, name="merge_out_final", grid=(n_samples, tiles),
        in_specs=[row, pl.BlockSpec((TM_OUT, D), lambda b, j: (b * tiles + j, 1)), row, row,
                  pl.BlockSpec((D, D), lambda b, j: (0, 0)), row, per, vec, row],
        out_specs=[row, row, row, per, vec, pl.BlockSpec((8, 128), lambda b, j: (0, 0))],
        out_shape=[jax.ShapeDtypeStruct((t, D), BF16), jax.ShapeDtypeStruct((t, D), F32), jax.ShapeDtypeStruct((t, D), BF16),
                   jax.ShapeDtypeStruct((n_samples, 1, D), F32), jax.ShapeDtypeStruct((1, D), F32),
                   jax.ShapeDtypeStruct((8, 128), F32)],
        compiler_params=_params(("arbitrary", "arbitrary")),
    )(p5, p5, y_conv, y_gla, w_out, x2, gate, final_g, target)


def out_dgrad_merge_bwd(p5, y_conv, y_gla, dmo, w_out, merged):
    t = y_conv.shape[0]

    def body(mc_ref, mg_ref, yc_ref, yg_ref, d_ref, w_ref, mrg_ref, dyc_ref, dyg_ref, dp_ref, gw_ref):
        f = lambda ref: ref[...].astype(F32)

        @pl.when(pl.program_id(0) == 0)
        def _():
            gw_ref[...] = jnp.zeros_like(gw_ref)

        gw_ref[...] += _tn(mrg_ref[...], d_ref[...])
        d = _nt(d_ref[...], w_ref[...])
        sc = _sigmoid(f(mc_ref))
        sg = _sigmoid(f(mg_ref))
        dyc_ref[...] = (d * sc).astype(dyc_ref.dtype)
        dyg_ref[...] = (d * sg).astype(dyg_ref.dtype)
        dp_ref[:, pl.ds(0, D)] = (d * f(yc_ref) * sc * (1.0 - sc)).astype(dp_ref.dtype)
        dp_ref[:, pl.ds(D, D)] = (d * f(yg_ref) * sg * (1.0 - sg)).astype(dp_ref.dtype)

    row = pl.BlockSpec((TM_OUT, D), lambda i: (i, 0))
    return _pallas(
        body, name="out_dgrad_merge_bwd", grid=(t // TM_OUT,),
        in_specs=[row, pl.BlockSpec((TM_OUT, D), lambda i: (i, 1)), row, row, row, pl.BlockSpec((D, D), lambda i: (0, 0)),
                  row],
        out_specs=[row, row, pl.BlockSpec((TM_OUT, 2 * D), lambda i: (i, 0)), pl.BlockSpec((D, D), lambda i: (0, 0))],
        out_shape=[jax.ShapeDtypeStruct((t, D), BF16), jax.ShapeDtypeStruct((t, D), BF16),
                   jax.ShapeDtypeStruct((t, 2 * D), BF16), jax.ShapeDtypeStruct((D, D), F32)],
        compiler_params=_params(("arbitrary",)),
    )(p5, p5, y_conv, y_gla, dmo, w_out, merged)


def local_step(x, ctx, target, mod, wts, small, p_sh, chip, core):
    n = x.shape[0]
    t = n * SEQ
    t_all = t + n * NCTX
    x2 = x.reshape(t, D)
    ctx2 = ctx.reshape(n * NCTX, D)
    tgt2 = target.reshape(t, D)
    scale1, shift, gate = mod

    u = norm_mod_fwd(x2, ctx2, scale1, shift, small["norm_g"])
    p1, p2, p3, p4, p5, p_all = proj_all(u, [wts["w%d" % i] for i in range(1, 6)], [small["b%d" % i] for i in range(1, 6)],
                                         [t, t, t_all, t, t], p_sh, tm=512)
    p_full = jnp.stack([jnp.where(chip == i, p_sh, p_all[i]) for i in range(N_CHIPS)])
    wts = dict(wts, conv_proj=p_full[:, 0:256].reshape(D, D), gla_proj=p_full[:, 256:512].reshape(D, D),
               w_out=p_full[:, 512:768].reshape(D, D))

    aconv = conv_fwd(p1, small["conv_w"], small["conv_b"], n)
    ac, y_conv = ln_gate_proj(aconv, p2, small["conv_ln_g"], small["conv_ln_b"], wts["conv_proj"])

    qs, ks, vs, cum_f, cum_b = gla_prep_fwd(p3, small["upf"], small["upb"], small["bias_f"], small["bias_b"], n)
    o_f, s_f, sfin_f = gla_scan_fwd(qs, ks, vs, cum_f, rev=False, name="gla_scan_fwd_f")
    o_b, s_b, sfin_b = gla_scan_fwd(qs, ks, vs, cum_b, rev=True, name="gla_scan_fwd_b")
    og, y_gla = gla_out_proj(o_f, o_b, p4, small["gla_norm_g"], wts["gla_proj"])

    merged, dh, dmo, dgate, d_final_g, loss = merge_out_final(p5, y_conv, y_gla, wts["w_out"], x2, gate,
                                                              small["final_norm_g"], tgt2, n)

    g = {"final_norm_g": d_final_g}
    dyc, dyg, dp5, g["w_out"] = out_dgrad_merge_bwd(p5, y_conv, y_gla, dmo, wts["w_out"], merged)

    daconv, dp2, g["conv_ln_g"], g["conv_ln_b"], g["conv_proj"] = ln_gate_bwd(
        aconv, p2, dyc, wts["conv_proj"], small["conv_ln_g"], small["conv_ln_b"], ac)
    dp1, dconv_w, dconv_b = conv_bwd(p1, daconv, small["conv_w"], n)
    g["conv_w"], g["conv_b"] = dconv_w, dconv_b

    do, dp4, g["gla_norm_g"], g["gla_proj"] = gla_out_bwd(o_f, o_b, p4, dyg, wts["gla_proj"], small["gla_norm_g"], og)
    g["proj"] = jnp.concatenate([g["conv_proj"].reshape(N_CHIPS, 256, D), g["gla_proj"].reshape(N_CHIPS, 256, D),
                                 g["w_out"].reshape(N_CHIPS, 256, D)], 1)
    dq_f, dk_f, dv_f, dc_f, gotp = gla_scan_bwd(qs, ks, vs, cum_f, s_f, sfin_f, do, rev=False, name="gla_scan_bwd_f",
                                                rider=("swap", g["proj"]))
    pap16 = pair_add(core, g["proj"], gotp, name="pair_add_p", tr=384)
    dq_b, dk_b, dv_b, dc_b, rbp = gla_scan_bwd(qs, ks, vs, cum_b, s_b, sfin_b, do, rev=True, name="gla_scan_bwd_b",
                                               rider=("exchange", pap16))
    dp3, g["upf"], g["upb"], g["bias_f"], g["bias_b"] = gla_prep_bwd(
        p3, dq_f, dq_b, dk_f, dk_b, dv_f, dv_b, dc_f, dc_b,
        small["upf"], small["upb"], small["bias_f"], small["bias_b"], n)

    dps = [dp1, dp2, dp3, dp4, dp5]
    early = [g["conv_b"].sum(0), g["conv_ln_g"], g["conv_ln_b"], g["bias_f"], g["bias_b"], g["gla_norm_g"],
             g["final_norm_g"], g["conv_w"].sum(0)[:CONV_K], g["upf"][0:16], g["upb"][16:32]]
    got = {}
    for i in [0, 1, 3, 4, 2]:
        dp = dps[i]
        rows = dp.shape[0]
        tn = W3 if dp.shape[1] == W3 else 1024
        others = [j for j in range(5) if j != i]
        swap = ([g["w%d" % (j + 1)] for j in others], _pack(early)) if i == 2 else None
        outs = matmul_tn(u, dp, name="w_in_wgrad_%d" % (i + 1), t=rows, tn=tn, tt=1024 if rows % 1024 == 0 else 768,
                         colsum=True, swap=swap)
        g["w%d" % (i + 1)], g["b%d" % (i + 1)] = outs[0], outs[1]
        if swap is not None:
            got = dict(zip(others, outs[2:2 + len(others)]))
            sall_early = outs[2 + len(others)]
    g["gate"] = dgate
    return loss, dh, dps, g, got, (pap16, rbp), (sall_early, [a.shape for a in early])


def _group_cols(w):
    gv, gg, z = w[..., 0:1024], w[..., 1024:2048], w[..., 2048:3072]
    q, k, v = w[..., 3072:3584], w[..., 3584:4096], w[..., 4096:5120]
    ab = w[..., 5120:5152]
    r, mc, mg = w[..., 5152:6176], w[..., 6176:7200], w[..., 7200:8224]
    g1 = jnp.concatenate([p for j in range(CONV_NCB)
                          for p in (gv[..., CONV_CB * j:CONV_CB * (j + 1)], gg[..., CONV_CB * j:CONV_CB * (j + 1)])], -1)
    pad = jnp.zeros(w.shape[:-1] + (W3 - 2080,), w.dtype)
    g3 = jnp.concatenate([v, q, k, ab, pad], -1)
    return g1, z, g3, r, jnp.concatenate([mc, mg], -1)


def _ungroup_cols(g1, g2, g3, g4, g5):
    gv = jnp.concatenate([g1[..., 2 * CONV_CB * j:2 * CONV_CB * j + CONV_CB] for j in range(CONV_NCB)], -1)
    gg = jnp.concatenate([g1[..., 2 * CONV_CB * j + CONV_CB:2 * CONV_CB * (j + 1)] for j in range(CONV_NCB)], -1)
    v, q, k, ab = g3[..., 0:1024], g3[..., 1024:1536], g3[..., 1536:2048], g3[..., 2048:2080]
    return jnp.concatenate([gv, gg, g2, q, k, v, ab, g4, g5[..., 0:1024], g5[..., 1024:2048]], -1)


def _natural_pieces():
    pieces = [(CONV_CB * j, CONV_CB, 0, 2 * CONV_CB * j) for j in range(CONV_NCB)]
    pieces += [(1024 + CONV_CB * j, CONV_CB, 0, 2 * CONV_CB * j + CONV_CB) for j in range(CONV_NCB)]
    pieces += [(2048, 1024, 1, 0), (3072, 512, 2, O3_Q), (3584, 512, 2, O3_K), (4096, 1024, 2, O3_V), (5120, 32, 2, O3_AB),
               (5152, 1024, 3, 0), (6176, 1024, 4, 0), (7200, 1024, 4, 1024)]
    return sorted(pieces)


def _ungroup_to_shards(groups):
    shards = []
    for i in range(N_CHIPS):
        lo, hi = i * W_IN_SHARD, (i + 1) * W_IN_SHARD
        parts = []
        for nat, width, g, gcol in _natural_pieces():
            a, b = max(nat, lo), min(nat + width, hi)
            if a < b:
                parts.append(groups[g][:, gcol + a - nat:gcol + b - nat])
        shards.append(jnp.concatenate(parts, 1))
    return jnp.stack(shards)


def _pad_up(up, row0):
    return jnp.zeros((128, GLA_DK), F32).at[row0:row0 + up.shape[0]].set(up)


def _adamw_math(w, g, m, v):
    m = ADAM_B1 * m + (1.0 - ADAM_B1) * g
    v = ADAM_B2 * v + (1.0 - ADAM_B2) * (g * g)
    m_hat = m / (1.0 - ADAM_B1 ** ADAM_STEP)
    v_hat = v / (1.0 - ADAM_B2 ** ADAM_STEP)
    delta = -ADAM_LR * (m_hat / (jnp.sqrt(v_hat) + ADAM_EPS) + ADAM_WD * w)
    return delta, m, v


def adamw2d(w, g, m, v, *, name, tr, tcols=None):
    rows, cols = w.shape[-2:]

    def body(w_ref, g_ref, m_ref, v_ref, d_ref, nm_ref, nv_ref):
        d_ref[...], nm_ref[...], nv_ref[...] = _adamw_math(w_ref[...], g_ref[...], m_ref[...], v_ref[...])

    tcols = cols if tcols is None else tcols
    if w.ndim == 3:
        spec = pl.BlockSpec((1, tr, tcols), lambda i, j: (0, i, j))
    else:
        spec = pl.BlockSpec((tr, tcols), lambda i, j: (i, j))
    return _pallas(
        body, name=name, grid=(rows // tr, cols // tcols), in_specs=[spec] * 4, out_specs=[spec] * 3,
        out_shape=[jax.ShapeDtypeStruct(w.shape, F32)] * 3, compiler_params=_params(("parallel", "parallel")),
    )(w, g, m, v)


def adamw_many(ws, gs, ms, vs):
    k = len(ws)
    two = lambda a: a.reshape((-1, a.shape[-1]))

    def body(*refs):
        w_refs, g_refs, m_refs, v_refs = refs[:k], refs[k:2 * k], refs[2 * k:3 * k], refs[3 * k:4 * k]
        d_refs, nm_refs, nv_refs = refs[4 * k:5 * k], refs[5 * k:6 * k], refs[6 * k:7 * k]
        for i in range(k):
            d_refs[i][...], nm_refs[i][...], nv_refs[i][...] = _adamw_math(
                w_refs[i][...], g_refs[i][...], m_refs[i][...], v_refs[i][...])

    shapes = [jax.ShapeDtypeStruct(two(a).shape, F32) for a in ws]
    outs = _pallas(body, name="adamw_small", out_shape=shapes * 3, compiler_params=_params())(
        *[two(a) for a in ws], *[two(a) for a in gs], *[two(a) for a in ms], *[two(a) for a in vs])
    back = lambda lst: [o.reshape(a.shape) for o, a in zip(lst, ws)]
    return back(outs[:k]), back(outs[k:2 * k]), back(outs[2 * k:])


def sum_devices(salls):
    k = len(salls)

    def body(*refs):
        for i in range(k):
            acc = refs[i][0]
            for d in range(1, N_DEV):
                acc = acc + refs[i][d]
            refs[k + i][...] = acc

    return _pallas(body, name="sum_devices", out_shape=[jax.ShapeDtypeStruct(a.shape[1:], F32) for a in salls],
                   compiler_params=_params())(*salls)


def pair_add(core, g, got, *, name, tr):
    n, rows, cols = got.shape
    g4 = g.reshape(n, 2, rows, cols)

    def body(core_ref, g_ref, got_ref, ob_ref):
        del core_ref
        ob_ref[0] = (g_ref[0, 0] + got_ref[0]).astype(BF16)

    spec = pl.BlockSpec((1, tr, cols), lambda i, t, core_ref: (i, t, 0))
    return _pallas(
        body, name=name,
        grid_spec=pltpu.PrefetchScalarGridSpec(
            num_scalar_prefetch=1, grid=(n, rows // tr),
            in_specs=[pl.BlockSpec((1, 1, tr, cols), lambda i, t, core_ref: (i, core_ref[0], t, 0)), spec],
            out_specs=spec),
        out_shape=jax.ShapeDtypeStruct(got.shape, BF16),
        compiler_params=_params(("parallel", "parallel")))(core, g4, got)


def pair_add_groups(core, gs, gots, *, tr):
    k = len(gs)
    rows = gots[0].shape[0]

    def body(core_ref, *refs):
        del core_ref
        for i in range(k):
            refs[2 * k + i][...] = (refs[i][0] + refs[k + i][...]).astype(BF16)

    g_specs = [pl.BlockSpec((1, tr, a.shape[1]), lambda t, core_ref: (core_ref[0], t, 0)) for a in gots]
    r_specs = [pl.BlockSpec((tr, a.shape[1]), lambda t, core_ref: (t, 0)) for a in gots]
    return _pallas(
        body, name="pair_add_w",
        grid_spec=pltpu.PrefetchScalarGridSpec(num_scalar_prefetch=1, grid=(rows // tr,),
                                               in_specs=g_specs + r_specs, out_specs=r_specs),
        out_shape=[jax.ShapeDtypeStruct(a.shape, BF16) for a in gots],
        compiler_params=_params(("parallel",)))(core, *[a.reshape(2, rows, a.shape[1]) for a in gs], *gots)


def chip_add(place, pas, rbs, *, steps):
    k = len(pas)

    def body(place_ref, *refs):
        del place_ref
        for i in range(k):
            m_ref, r_ref, o_ref = refs[i], refs[k + i], refs[2 * k + i]
            o_ref[0] = ((m_ref[0].astype(F32) + r_ref[0].astype(F32)) + r_ref[1].astype(F32)) + r_ref[2].astype(F32)

    trs = [pa.shape[1] // steps for pa in pas]
    return _pallas(
        body, name="chip_add",
        grid_spec=pltpu.PrefetchScalarGridSpec(
            num_scalar_prefetch=1, grid=(steps,),
            in_specs=[pl.BlockSpec((1, tr, pa.shape[2]), lambda t, place_ref: (place_ref[0], t, 0)) for pa, tr in zip(pas, trs)]
            + [pl.BlockSpec((3, tr, pa.shape[2]), lambda t, place_ref: (0, t, 0)) for pa, tr in zip(pas, trs)],
            out_specs=[pl.BlockSpec((1, tr, pa.shape[2]), lambda t, place_ref: (place_ref[1], t, 0))
                       for pa, tr in zip(pas, trs)]),
        out_shape=[jax.ShapeDtypeStruct((2,) + pa.shape[1:], F32) for pa in pas],
        compiler_params=_params(("parallel",)))(place, *pas, *rbs)


def ada_bwd(call, cctx_rows, dm_shard, dm_full, adaw):
    nsh = adaw.shape[1]

    def body(c_ref, cc_ref, dms_ref, dmf_ref, w_ref, gw_ref, gb_ref, pq_ref):
        a_lat = _silu(c_ref[...])
        a_ctx = _silu(cc_ref[...])
        dms = dms_ref[...]
        gw_ref[...] = _tn(a_lat, dms[0:64], HI) + _tn(a_ctx, dms[64:72], HI)
        gb_ref[...] = jnp.sum(dmf_ref[...], axis=0, keepdims=True)
        part = _nt(dms[64:72], w_ref[...], HI)
        pq_ref[...] = jnp.zeros_like(pq_ref) + jnp.sum(part, axis=0, keepdims=True)

    return _pallas(body, name="ada_bwd",
                   out_shape=[jax.ShapeDtypeStruct((D, nsh), F32), jax.ShapeDtypeStruct((1, 3 * D), F32),
                              jax.ShapeDtypeStruct((8, D), F32)],
                   compiler_params=_params())(call, cctx_rows, dm_shard, dm_full, adaw)


def cctx_grad(pq_all, cctx_rows):
    def body(p_ref, c_ref, o_ref):
        acc = p_ref[0]
        for qi in range(1, N_CHIPS):
            acc = acc + p_ref[qi]
        o_ref[...] = acc * _dsilu(c_ref[...])

    return _pallas(body, name="cctx_grad", out_shape=jax.ShapeDtypeStruct((8, D), F32),
                   compiler_params=_params())(pq_all, cctx_rows)


def _place():
    x, y, c = lax.axis_index("x"), lax.axis_index("y"), lax.axis_index("c")
    chips = [(1 - x, y), (x, 1 - y), (1 - x, 1 - y)]
    return x, y, c, chips


def _all_peers(x, y, c):
    return [((1 - x) if r & 4 else x, (1 - y) if r & 2 else y, (1 - c) if r & 1 else c) for r in range(1, N_DEV)]


def _remote(src, dst, send_sem, recv_sem, dev):
    return pltpu.make_async_remote_copy(src_ref=src, dst_ref=dst, send_sem=send_sem, recv_sem=recv_sem,
                                        device_id=dev, device_id_type=MESH)


ANY = pl.BlockSpec(memory_space=pl.ANY)
VMEM = pl.BlockSpec(memory_space=pltpu.VMEM)
F_ROWS = 16


W_ROW_CHUNKS = 4
P_ROW_CHUNKS = 2
N_BULK = W_ROW_CHUNKS + P_ROW_CHUNKS


def _half_chunks(core, n_rows, align, which=(0, 1)):
    out = []
    for a, k in ((0, W_ROW_CHUNKS), (1, P_ROW_CHUNKS)):
        if a not in which:
            continue
        half = n_rows[a] // 2
        size = half // k
        for i in range(k):
            start = core * half + i * size
            out.append((a, pl.ds(start if isinstance(start, int) else pl.multiple_of(start, align), size)))
    return out


def gather_weights(c8, cctx8, adaw, adab, w_sh, fp):
    nsh = adaw.shape[1]

    def body(c_ref, cctx_ref, adaw_ref, adab_ref, w_ref, fp_ref, wall_ref, fall_ref, call_ref, mall_ref,
             abuf, w_send, w_recv, h_send, h_recv, c_send, c_recv, m_send, m_recv, f_send, f_recv):
        x, y, c, chips = _place()
        q = 2 * x + y
        dev = 4 * x + 2 * y + c
        qs = [2 * cx + cy for cx, cy in chips]
        sib = (x, y, 1 - c)
        srcs, dsts = (w_ref,), (wall_ref,)
        n_rows = (w_ref.shape[0],)
        mine = _half_chunks(c, n_rows, 16, which=(0,))
        other = _half_chunks(1 - c, n_rows, 16, which=(0,))

        bulk = [[_remote(srcs[a].at[rows], dsts[a].at[q, rows], w_send.at[j * N_BULK + i], w_recv.at[j * N_BULK + i],
                         (*chips[j], c)) for i, (a, rows) in enumerate(mine)] for j in range(3)]
        fall_ref[q] = fp_ref[...]
        small = [_remote(fp_ref, fall_ref.at[q], f_send.at[j], f_recv.at[j], (*chips[j], c)) for j in range(3)]
        my_rows = pl.ds(pl.multiple_of(8 * dev, 8), 8)
        call_ref[my_rows, :] = c_ref[...]
        cond = [_remote(c_ref, call_ref.at[my_rows, :], c_send.at[r], c_recv.at[r], peer)
                for r, peer in enumerate(_all_peers(x, y, c))]
        for cp in sum(bulk, []) + small + cond:
            cp.start()
        for cp in cond:
            cp.wait_recv()

        abuf[pl.ds(0, 64), :] = _silu(call_ref[...])
        abuf[pl.ds(64, 8), :] = _silu(cctx_ref[...])
        mall_ref[q] = _nn(abuf[...], adaw_ref[...], HI) + adab_ref[...]
        mod = [_remote(mall_ref.at[q], mall_ref.at[q], m_send.at[j], m_recv.at[j], (*chips[j], c)) for j in range(3)]
        for cp in mod:
            cp.start()

        handed = []
        for j in range(3):
            for i, (a, rows) in enumerate(mine):
                bulk[j][i].wait_recv()
                cp = _remote(dsts[a].at[qs[j], rows], dsts[a].at[qs[j], rows],
                             h_send.at[j * N_BULK + i], h_recv.at[j * N_BULK + i], sib)
                cp.start()
                handed.append(cp)
        for j in range(3):
            for i, (a, rows) in enumerate(other):
                _remote(dsts[a].at[qs[j], rows], dsts[a].at[qs[j], rows],
                        h_send.at[j * N_BULK + i], h_recv.at[j * N_BULK + i], sib).wait_recv()
        for cp in mod + small:
            cp.wait_recv()
        for cp in sum(bulk, []) + small + cond + mod + handed:
            cp.wait_send()

    def dma(n):
        return pltpu.SemaphoreType.DMA((n,))

    return _pallas(
        body, name="gather_weights",
        in_specs=[VMEM, VMEM, VMEM, VMEM, ANY, VMEM],
        out_specs=[ANY, VMEM, VMEM, VMEM],
        out_shape=[jax.ShapeDtypeStruct((N_CHIPS,) + w_sh.shape, BF16),
                   jax.ShapeDtypeStruct((N_CHIPS, F_ROWS, D), F32),
                   jax.ShapeDtypeStruct((8 * N_DEV, D), F32), jax.ShapeDtypeStruct((N_CHIPS, MOD_ROWS, nsh), F32)],
        scratch_shapes=[pltpu.VMEM((MOD_ROWS, D), F32), dma(3 * N_BULK), dma(3 * N_BULK), dma(3 * N_BULK), dma(3 * N_BULK),
                        dma(7), dma(7), dma(3), dma(3), dma(3), dma(3)],
        compiler_params=_params(),
    )(c8, cctx8, adaw, adab, w_sh, fp)


def _pair_count(gs, gp):
    return len(gs) * W_ROW_CHUNKS + (0 if gp is None else N_CHIPS * P_ROW_CHUNKS)


def _pair_got_shapes(gs, gp):
    shapes = [jax.ShapeDtypeStruct((D // 2, a.shape[1]), F32) for a in gs]
    if gp is not None:
        shapes.append(jax.ShapeDtypeStruct((N_CHIPS, gp.shape[1] // 2, gp.shape[2]), F32))
    return shapes


def _pair_copies(g_refs, gp_ref, got_refs, gotp_ref, a_send, a_recv):
    x, y, c, _ = _place()
    sib = (x, y, 1 - c)
    pair = []
    half, size = D // 2, D // 2 // W_ROW_CHUNKS
    for gi in range(len(g_refs)):
        for i in range(W_ROW_CHUNKS):
            k = len(pair)
            rows_o = pl.ds(pl.multiple_of((1 - c) * half + i * size, 8), size)
            pair.append(_remote(g_refs[gi].at[rows_o], got_refs[gi].at[pl.ds(i * size, size)],
                                a_send.at[k], a_recv.at[k], sib))
    if gp_ref is not None:
        half, size = gp_ref.shape[1] // 2, gp_ref.shape[1] // 2 // P_ROW_CHUNKS
        for s in range(N_CHIPS):
            for i in range(P_ROW_CHUNKS):
                k = len(pair)
                rows_o = pl.ds(pl.multiple_of((1 - c) * half + i * size, 8), size)
                pair.append(_remote(gp_ref.at[s, rows_o], gotp_ref.at[s, pl.ds(i * size, size)],
                                    a_send.at[k], a_recv.at[k], sib))
    return pair


def pair_swap(gs):
    n_gs = len(gs)

    def body(*refs):
        g_refs, got_refs = refs[:n_gs], refs[n_gs:2 * n_gs]
        a_send, a_recv = refs[2 * n_gs:]
        pair = _pair_copies(g_refs, None, got_refs, None, a_send, a_recv)
        for cp in pair:
            cp.start()
        for cp in pair:
            cp.wait_recv()
        for cp in pair:
            cp.wait_send()

    return _pallas(
        body, name="pair_swap", in_specs=[ANY] * n_gs, out_specs=[ANY] * n_gs,
        out_shape=_pair_got_shapes(gs, None),
        scratch_shapes=[pltpu.SemaphoreType.DMA((_pair_count(gs, None),)), pltpu.SemaphoreType.DMA((_pair_count(gs, None),))],
        compiler_params=_params(),
    )(*gs)


def gather_small(sm):
    rows = sm.shape[0]

    def body(sm_ref, sall_ref, s_send, s_recv):
        x, y, c, _ = _place()
        dev = 4 * x + 2 * y + c
        sall_ref[dev] = sm_ref[...]
        small = [_remote(sm_ref, sall_ref.at[dev], s_send.at[r], s_recv.at[r], peer)
                 for r, peer in enumerate(_all_peers(x, y, c))]
        for cp in small:
            cp.start()
        for cp in small:
            cp.wait_recv()
        for cp in small:
            cp.wait_send()

    return _pallas(
        body, name="gather_small", in_specs=[VMEM], out_specs=VMEM,
        out_shape=jax.ShapeDtypeStruct((N_DEV, rows, D), F32),
        scratch_shapes=[pltpu.SemaphoreType.DMA((7,)), pltpu.SemaphoreType.DMA((7,))],
        compiler_params=_params(),
    )(sm)


def pair_share(ghw, ghp, pq):
    def body(ghw_ref, ghp_ref, pq_ref, outw_ref, outp_ref, pqa_ref, send, recv, p_send, p_recv):
        del ghw_ref, ghp_ref
        x, y, c, chips = _place()
        q = 2 * x + y
        refs = (outw_ref, outp_ref)
        n_rows = (2 * outw_ref.shape[1], 2 * outp_ref.shape[1])
        pair = [_remote(refs[a].at[c, rows], refs[a].at[c, rows], send.at[i], recv.at[i], (x, y, 1 - c))
                for i, (a, rows) in enumerate(_half_chunks(0, n_rows, 8))]
        pqa_ref[q] = pq_ref[...]
        small = [_remote(pq_ref, pqa_ref.at[q], p_send.at[j], p_recv.at[j], (*chips[j], c)) for j in range(3)]
        for cp in pair + small:
            cp.start()
        for i, (a, rows) in enumerate(_half_chunks(0, n_rows, 8)):
            _remote(refs[a].at[1 - c, rows], refs[a].at[1 - c, rows], send.at[i], recv.at[i], (x, y, 1 - c)).wait_recv()
        for cp in small:
            cp.wait_recv()
        for cp in pair + small:
            cp.wait_send()

    return _pallas(
        body, name="pair_share", in_specs=[ANY, ANY, VMEM], out_specs=[ANY, ANY, VMEM],
        out_shape=[jax.ShapeDtypeStruct(ghw.shape, F32), jax.ShapeDtypeStruct(ghp.shape, F32),
                   jax.ShapeDtypeStruct((N_CHIPS, 8, D), F32)],
        scratch_shapes=[pltpu.SemaphoreType.DMA((N_BULK,)), pltpu.SemaphoreType.DMA((N_BULK,)),
                        pltpu.SemaphoreType.DMA((3,)), pltpu.SemaphoreType.DMA((3,))],
        input_output_aliases={0: 0, 1: 1},
        compiler_params=_params(),
    )(ghw, ghp, pq)


def _rows_of(shape):
    size = 1
    for s in shape:
        size *= s
    return -(-size // D)


def _pack(arrs, rows_multiple=8):
    parts = []
    total = 0
    for a in arrs:
        f = a.reshape(-1).astype(F32)
        r = _rows_of(a.shape)
        parts.append(jnp.pad(f, (0, r * D - f.shape[0])))
        total += r
    pad_rows = (-total) % rows_multiple
    if pad_rows:
        parts.append(jnp.zeros((pad_rows * D,), F32))
    return jnp.concatenate(parts).reshape(-1, D)


def _unpack(p, shapes):
    out = []
    r0 = 0
    for shp in shapes:
        r = _rows_of(shp)
        size = 1
        for s in shp:
            size *= s
        out.append(p[r0:r0 + r].reshape(-1)[:size].reshape(shp))
        r0 += r
    return out


WEIGHT_NAMES = ['c_ctx', 'ada_w', 'ada_b', 'norm_g', 'w_in', 'b_in', 'conv_w', 'conv_b', 'conv_ln_g', 'conv_ln_b',
                'conv_proj', 'decay_up_fwd', 'decay_bias_fwd', 'decay_up_bwd', 'decay_bias_bwd', 'gla_norm_g', 'gla_proj',
                'w_out', 'final_norm_g']
SMALL_NAMES = ['c_ctx', 'ada_b', 'norm_g', 'b_in', 'conv_w', 'conv_b', 'conv_ln_g', 'conv_ln_b', 'decay_up_fwd',
               'decay_bias_fwd', 'decay_up_bwd', 'decay_bias_bwd', 'gla_norm_g', 'final_norm_g']


def kernel(x, c, ctx, c_ctx, ada_w, ada_b, norm_g, w_in, b_in, conv_w, conv_b, conv_ln_g, conv_ln_b, conv_proj, decay_up_fwd, decay_bias_fwd, decay_up_bwd, decay_bias_bwd, gla_norm_g, gla_proj, w_out, final_norm_g, loss_target, m_c_ctx, m_ada_w, m_ada_b, m_norm_g, m_w_in, m_b_in, m_conv_w, m_conv_b, m_conv_ln_g, m_conv_ln_b, m_conv_proj, m_decay_up_fwd, m_decay_bias_fwd, m_decay_up_bwd, m_decay_bias_bwd, m_gla_norm_g, m_gla_proj, m_w_out, m_final_norm_g, v_c_ctx, v_ada_w, v_ada_b, v_norm_g, v_w_in, v_b_in, v_conv_w, v_conv_b, v_conv_ln_g, v_conv_ln_b, v_conv_proj, v_decay_up_fwd, v_decay_bias_fwd, v_decay_up_bwd, v_decay_bias_bwd, v_gla_norm_g, v_gla_proj, v_w_out, v_final_norm_g):
    w = dict(c_ctx=c_ctx, ada_w=ada_w, ada_b=ada_b, norm_g=norm_g, w_in=w_in, b_in=b_in, conv_w=conv_w, conv_b=conv_b,
             conv_ln_g=conv_ln_g, conv_ln_b=conv_ln_b, conv_proj=conv_proj, decay_up_fwd=decay_up_fwd,
             decay_bias_fwd=decay_bias_fwd, decay_up_bwd=decay_up_bwd, decay_bias_bwd=decay_bias_bwd,
             gla_norm_g=gla_norm_g, gla_proj=gla_proj, w_out=w_out, final_norm_g=final_norm_g)
    m = dict(c_ctx=m_c_ctx, ada_w=m_ada_w, ada_b=m_ada_b, norm_g=m_norm_g, w_in=m_w_in, b_in=m_b_in, conv_w=m_conv_w,
             conv_b=m_conv_b, conv_ln_g=m_conv_ln_g, conv_ln_b=m_conv_ln_b, conv_proj=m_conv_proj,
             decay_up_fwd=m_decay_up_fwd, decay_bias_fwd=m_decay_bias_fwd, decay_up_bwd=m_decay_up_bwd,
             decay_bias_bwd=m_decay_bias_bwd, gla_norm_g=m_gla_norm_g, gla_proj=m_gla_proj, w_out=m_w_out,
             final_norm_g=m_final_norm_g)
    v = dict(c_ctx=v_c_ctx, ada_w=v_ada_w, ada_b=v_ada_b, norm_g=v_norm_g, w_in=v_w_in, b_in=v_b_in, conv_w=v_conv_w,
             conv_b=v_conv_b, conv_ln_g=v_conv_ln_g, conv_ln_b=v_conv_ln_b, conv_proj=v_conv_proj,
             decay_up_fwd=v_decay_up_fwd, decay_bias_fwd=v_decay_bias_fwd, decay_up_bwd=v_decay_up_bwd,
             decay_bias_bwd=v_decay_bias_bwd, gla_norm_g=v_gla_norm_g, gla_proj=v_gla_proj, w_out=v_w_out,
             final_norm_g=v_final_norm_g)
    n = x.shape[0]
    ax, ay, ac = lax.axis_index("x"), lax.axis_index("y"), lax.axis_index("c")
    q = 2 * ax + ay
    dev = 4 * ax + 2 * ay + ac
    nsh = ada_w.shape[2]

    w_sh = w_in[0].astype(BF16)
    p_sh = jnp.concatenate([conv_proj[0], gla_proj[0], w_out[0]], 0).astype(BF16)
    fp = _pack([conv_w[0], decay_up_fwd[0], decay_up_bwd[0]], F_ROWS)
    c8 = jnp.pad(c, ((0, 8 - n), (0, 0)))
    cctx8 = jnp.pad(c_ctx[None], ((0, 7), (0, 0)))
    adab_sh = lax.dynamic_slice(ada_b, (0, q * nsh), (1, nsh))
    w_all, fall, call, mall = gather_weights(c8, cctx8, ada_w[0], adab_sh, w_sh, fp)

    mod_all = jnp.transpose(mall, (1, 0, 2)).reshape(MOD_ROWS, 3 * D)
    mod_mine = lax.dynamic_slice(mod_all, (8 * dev, 0), (n, 3 * D))
    mod_ctx = mod_all[64:65]
    shift = jnp.concatenate([mod_mine[:, 0:D], mod_ctx[:, 0:D]], 0)[:, None, :]
    scale1 = 1.0 + jnp.concatenate([mod_mine[:, D:2 * D], mod_ctx[:, D:2 * D]], 0)[:, None, :]
    gate = mod_mine[:, 2 * D:3 * D][:, None, :]

    own = lambda i, mine, got: jnp.where(q == i, mine, got)
    g1, g2, g3, g4, g5 = _group_cols(jnp.concatenate([own(i, w_sh, w_all[i]) for i in range(N_CHIPS)], 1))
    wts = dict(w1=g1, w2=g2, w3=g3, w4=g4, w5=g5)
    f_parts = [_unpack(fall[i], [conv_w.shape[1:], decay_up_fwd.shape[1:], decay_up_bwd.shape[1:]]) for i in range(N_CHIPS)]
    conv_w_full = jnp.concatenate([p[0] for p in f_parts], 1)
    upf_full = jnp.concatenate([p[1] for p in f_parts], 1)
    upb_full = jnp.concatenate([p[2] for p in f_parts], 1)
    b1, b2, b3, b4, b5 = _group_cols(b_in)
    small = dict(b1=b1, b2=b2, b3=b3, b4=b4, b5=b5, norm_g=norm_g,
                 conv_w=jnp.pad(conv_w_full, ((0, 1), (0, 0))), conv_b=conv_b, conv_ln_g=conv_ln_g, conv_ln_b=conv_ln_b,
                 upf=_split3(_pad_up(upf_full, 0)), upb=_split3(_pad_up(upb_full, 16)),
                 bias_f=decay_bias_fwd, bias_b=decay_bias_bwd,
                 gla_norm_g=gla_norm_g, final_norm_g=final_norm_g[None])

    core = ac.astype(jnp.int32).reshape(1)
    chip = q.astype(jnp.int32).reshape(1)
    loss_part, dh, dps, g, got, (pap16, rbp), (sall1, early_shapes) = local_step(
        x, ctx, loss_target, (scale1, shift, gate), wts, small, p_sh, q, core)

    gs = [g["w%d" % i] for i in range(1, 6)]
    got[2] = pair_swap([gs[2]])[0]
    halves = pair_add_groups(core, gs, [got[i] for i in range(5)], tr=128)
    paw16 = _ungroup_to_shards(halves)
    grad_x2, dshift, dscale, g["norm_g"], rbw = dgrad_norm_bwd(
        dps, [wts["w%d" % i] for i in range(1, 6)], paw16, x.reshape(n * SEQ, D), ctx.reshape(n * NCTX, D), dh,
        scale1, norm_g, tm=256)

    dm_mine = jnp.concatenate([dshift[:n, 0], dscale[:n, 0], g["gate"][:, 0]], -1)
    dm_ctx = jnp.concatenate([dshift[n, 0], dscale[n, 0], jnp.zeros((D,), F32)], -1)
    d_b_in = _ungroup_cols(*[g["b%d" % i] for i in range(1, 6)])
    late = [g["norm_g"], dm_mine, dm_ctx, d_b_in, loss_part[0, 0:1]]
    late_shapes = [a.shape for a in late]
    sall2 = gather_small(_pack(late))
    sum_early, sum_late = sum_devices([sall1, sall2])
    (s_conv_b, s_ln_g, s_ln_b, s_bias_f, s_bias_b, s_gla_g, s_final_g, s_conv_w, s_upf,
     s_upb) = _unpack(sum_early, early_shapes)
    s_late = _unpack(sum_late, late_shapes)
    s_norm_g, s_b_in, loss = s_late[0], s_late[3], s_late[4][0]
    r_mine, r_ctx = 1, 1 + 3 * n
    dm_all = sall2[:, r_mine:r_ctx].reshape(N_DEV, n, 3 * D)
    dm_full = jnp.concatenate([jnp.pad(dm_all, ((0, 0), (0, 8 - n), (0, 0))).reshape(8 * N_DEV, 3 * D),
                               sall2[:, r_ctx:r_ctx + 3].reshape(N_DEV, 3 * D)], 0)
    dm_shard = lax.dynamic_slice(dm_full, (0, q * nsh), (MOD_ROWS, nsh))
    cctx_rows = jnp.broadcast_to(c_ctx[None], (8, D))
    g_ada_w, g_ada_b, pq = ada_bwd(call, cctx_rows, dm_shard, dm_full, ada_w[0])

    place = jnp.concatenate([chip, core])
    ghw, ghp = chip_add(place, [paw16, pap16], [rbw, rbp], steps=4)
    gw_mine, gp_mine, pq_all = pair_share(ghw, ghp, pq)
    gp_mine = gp_mine.reshape(768, D)
    g_c_ctx = cctx_grad(pq_all, cctx_rows)[0]

    grads = dict(
        c_ctx=g_c_ctx, ada_w=g_ada_w[None], ada_b=g_ada_b, norm_g=s_norm_g,
        w_in=gw_mine.reshape(1, D, W_IN_SHARD), b_in=s_b_in,
        conv_w=lax.dynamic_slice(s_conv_w, (0, q * 256), (CONV_K, 256))[None], conv_b=s_conv_b,
        conv_ln_g=s_ln_g, conv_ln_b=s_ln_b, conv_proj=gp_mine[0:256][None],
        decay_up_fwd=lax.dynamic_slice(s_upf, (0, q * 128), (16, 128))[None], decay_bias_fwd=s_bias_f,
        decay_up_bwd=lax.dynamic_slice(s_upb, (0, q * 128), (16, 128))[None], decay_bias_bwd=s_bias_b,
        gla_norm_g=s_gla_g, gla_proj=gp_mine[256:512][None], w_out=gp_mine[512:768][None],
        final_norm_g=s_final_g[0])

    delta, new_m, new_v = {}, {}, {}
    delta["ada_w"], new_m["ada_w"], new_v["ada_w"] = adamw2d(w["ada_w"], grads["ada_w"], m["ada_w"], v["ada_w"],
                                                             name="adamw_ada_w", tr=128)
    tr_ = lambda a: jnp.swapaxes(a, 1, 2)
    g_w_in_t = tr_(grads["w_in"])
    grads["w_in"] = tr_(g_w_in_t)
    d_, m_, v_ = adamw2d(tr_(w_in), g_w_in_t, tr_(m_w_in), tr_(v_w_in), name="adamw_w_in", tr=W_IN_SHARD, tcols=128)
    delta["w_in"], new_m["w_in"], new_v["w_in"] = tr_(d_), tr_(m_), tr_(v_)
    rest = SMALL_NAMES + ["conv_proj", "gla_proj", "w_out"]
    d_, m_, v_ = adamw_many([w[nm] for nm in rest], [grads[nm].reshape(w[nm].shape) for nm in rest],
                            [m[nm] for nm in rest], [v[nm] for nm in rest])
    for nm, a, b, cc in zip(rest, d_, m_, v_):
        delta[nm], new_m[nm], new_v[nm] = a, b, cc

    grad_x = grad_x2.reshape(x.shape)
    return (loss, grad_x, *[grads[nm].reshape(w[nm].shape) for nm in WEIGHT_NAMES], *[delta[nm] for nm in WEIGHT_NAMES],
            *[new_m[nm] for nm in WEIGHT_NAMES], *[new_v[nm] for nm in WEIGHT_NAMES])
```

```python
import jax
import jax.numpy as jnp
from jax import lax
from jax.experimental import pallas as pl
from jax.experimental.pallas import tpu as pltpu

F32 = jnp.float32
BF16 = jnp.bfloat16
MESH = pl.DeviceIdType.MESH
HI = lax.Precision.HIGHEST

D = 1024
SEQ = 2048
GRID_W = 64
GRID_H = SEQ // GRID_W
NCTX = 256
SEQ_ALL = SEQ + NCTX
EPS = 1e-6
CONV_K = 31
CONV_PAD = CONV_K // 2
HEADS = 4
HEAD_K = 128
HEAD_V = 256
GLA_DK = HEADS * HEAD_K
GATE_TAU = 16.0
Q_SCALE = HEAD_K ** -0.5
CHUNK = 64
NCHUNK = SEQ_ALL // CHUNK
NCHUNK_LAT = SEQ // CHUNK
NCHUNK_CTX = NCHUNK - NCHUNK_LAT
SUB = 64
NSUB = CHUNK // SUB
N_IN = 8224
W3 = 2176
O3_V, O3_Q, O3_K, O3_AB = 0, 1024, 1536, 2048

ADAM_LR, ADAM_B1, ADAM_B2, ADAM_EPS, ADAM_WD, ADAM_STEP = 0.001, 0.9, 0.999, 1e-08, 0.01, 10
VMEM_LIMIT = 56 * 1024 * 1024

N_CHIPS = 4
N_DEV = 8
W_IN_SHARD = N_IN // N_CHIPS
MOD_ROWS = 72


def _pallas(body, **kw):
    return pl.pallas_call(body, **kw)


def _params(sem=None, **kw):
    if sem is not None:
        kw["dimension_semantics"] = sem
    return pltpu.CompilerParams(vmem_limit_bytes=VMEM_LIMIT, **kw)


def _sigmoid(v):
    return 1.0 / (1.0 + jnp.exp(-v))


def _silu(v):
    return v * _sigmoid(v)


def _dsilu(v):
    s = _sigmoid(v)
    return s * (1.0 + v * (1.0 - s))


def _log_sigmoid(v):
    return jnp.minimum(v, 0.0) - jnp.log(1.0 + jnp.exp(-jnp.abs(v)))


def _dot(a, b, dims, precision=None):
    return lax.dot_general(a, b, (dims, ((), ())), preferred_element_type=F32, precision=precision)


def _nn(a, b, precision=None):
    return _dot(a, b, ((1,), (0,)), precision)


def _nt(a, b, precision=None):
    return _dot(a, b, ((1,), (1,)), precision)


def _tn(a, b, precision=None):
    return _dot(a, b, ((0,), (0,)), precision)


def _b16(v):
    return v.astype(BF16)


def proj_all(u, ws, bs, rows, p_sh, *, tm):
    k = u.shape[1]
    n_g = len(ws)
    tns = [w.shape[1] if w.shape[1] % 1024 else 1024 for w in ws]
    mts = [r // tm for r in rows]
    cnts = [(w.shape[1] // tn) * mt for w, tn, mt in zip(ws, tns, mts)]
    los = [sum(cnts[:g]) for g in range(n_g)]
    n_steps = sum(cnts)

    def rel(s, g):
        return jnp.clip(s - los[g], 0, cnts[g] - 1)

    def active(s, g):
        return (s >= los[g]) & (s < los[g] + cnts[g])

    def u_row(s):
        r = 0
        for g in range(n_g):
            r = r + jnp.where(active(s, g), rel(s, g) % mts[g], 0)
        return r

    def body(*refs):
        u_ref = refs[0]
        w_refs, b_refs = refs[1:1 + n_g], refs[1 + n_g:1 + 2 * n_g]
        p_ref = refs[1 + 2 * n_g]
        o_refs = refs[2 + 2 * n_g:2 + 3 * n_g]
        pall_ref = refs[2 + 3 * n_g]
        w_send, w_recv, h_send, h_recv = refs[3 + 3 * n_g:]
        s = pl.program_id(0)
        for g in range(n_g):
            @pl.when(active(s, g))
            def _(g=g):
                o_refs[g][...] = (_nn(u_ref[...], w_refs[g][...]) + b_refs[g][...]).astype(o_refs[g].dtype)

        x, y, c, chips = _place()
        q = 2 * x + y
        mine = _half_chunks(c, (0, p_ref.shape[0]), 16, which=(1,))
        other = _half_chunks(1 - c, (0, p_ref.shape[0]), 16, which=(1,))
        nb = len(mine)

        def bulk():
            return [[_remote(p_ref.at[rws], pall_ref.at[q, rws], w_send.at[pj * nb + pi], w_recv.at[pj * nb + pi],
                             (*chips[pj], c)) for pi, (_, rws) in enumerate(mine)] for pj in range(3)]

        @pl.when(s == 0)
        def _():
            for cp in sum(bulk(), []):
                cp.start()

        @pl.when(s == n_steps - 1)
        def _():
            handed = []
            for pj, (cx, cy) in enumerate(chips):
                for pi, (_, rws) in enumerate(mine):
                    bulk()[pj][pi].wait_recv()
                    cp = _remote(pall_ref.at[2 * cx + cy, rws], pall_ref.at[2 * cx + cy, rws],
                                 h_send.at[pj * nb + pi], h_recv.at[pj * nb + pi], (x, y, 1 - c))
                    cp.start()
                    handed.append(cp)
            for pj, (cx, cy) in enumerate(chips):
                for pi, (_, rws) in enumerate(other):
                    _remote(pall_ref.at[2 * cx + cy, rws], pall_ref.at[2 * cx + cy, rws],
                            h_send.at[pj * nb + pi], h_recv.at[pj * nb + pi], (x, y, 1 - c)).wait_recv()
            for cp in sum(bulk(), []) + handed:
                cp.wait_send()

    any_spec = pl.BlockSpec(memory_space=pl.ANY)
    in_specs = [pl.BlockSpec((tm, k), lambda s: (u_row(s), 0))]
    in_specs += [pl.BlockSpec((k, tns[g]), lambda s, g=g: (0, rel(s, g) // mts[g])) for g in range(n_g)]
    in_specs += [pl.BlockSpec((1, tns[g]), lambda s, g=g: (0, rel(s, g) // mts[g])) for g in range(n_g)]
    in_specs.append(any_spec)
    out_specs = [pl.BlockSpec((tm, tns[g]), lambda s, g=g: (rel(s, g) % mts[g], rel(s, g) // mts[g])) for g in range(n_g)]
    out_specs.append(any_spec)
    out_shape = [jax.ShapeDtypeStruct((rows[g], ws[g].shape[1]), BF16) for g in range(n_g)]
    out_shape.append(jax.ShapeDtypeStruct((N_CHIPS,) + p_sh.shape, p_sh.dtype))
    return _pallas(
        body, name="proj_all", grid=(n_steps,), in_specs=in_specs, out_specs=out_specs, out_shape=out_shape,
        scratch_shapes=[pltpu.SemaphoreType.DMA((3 * P_ROW_CHUNKS,)) for _ in range(4)],
        compiler_params=_params(("arbitrary",)),
    )(u, *ws, *bs, p_sh)


def matmul_tn(a, b, *, name, t, tn, tt, colsum=False, swap=None):
    m = a.shape[1]
    n = b.shape[1]
    nj, ns = n // tn, t // tt
    n_out = 2 if colsum else 1
    n_sw = 0 if swap is None else len(swap[0])
    n_ex = 0 if swap is None else n_sw + 1

    def body(a_ref, b_ref, *rest):
        o_ref = rest[n_ex]
        cs_ref = rest[n_ex + 1] if colsum else None
        j, s = pl.program_id(0), pl.program_id(1)

        if swap is not None:
            g_refs, sm_ref = rest[:n_sw], rest[n_sw]
            got_refs, sall_ref = rest[n_ex + n_out:n_ex + n_out + n_sw], rest[n_ex + n_out + n_sw]
            a_send, a_recv, s_send, s_recv, l_sem = rest[2 * n_ex + n_out:]

            def copies():
                x, y, c, _ = _place()
                dev = 4 * x + 2 * y + c
                small = [_remote(sm_ref, sall_ref.at[dev], s_send.at[r], s_recv.at[r], peer)
                         for r, peer in enumerate(_all_peers(x, y, c))]
                return (_pair_copies(g_refs, None, got_refs, None, a_send, a_recv) + small,
                        pltpu.make_async_copy(sm_ref, sall_ref.at[dev], l_sem))

            @pl.when((j == 0) & (s == 0))
            def _():
                remote, own = copies()
                own.start()
                for cp in remote:
                    cp.start()

            @pl.when((j == nj - 1) & (s == ns - 1))
            def _():
                remote, own = copies()
                for cp in remote:
                    cp.wait_recv()
                for cp in remote:
                    cp.wait_send()
                own.wait()

        @pl.when(s == 0)
        def _():
            o_ref[...] = jnp.zeros_like(o_ref)
            if colsum:
                cs_ref[...] = jnp.zeros_like(cs_ref)
        o_ref[...] += _tn(a_ref[...], b_ref[...])
        if colsum:
            cs_ref[...] += jnp.sum(b_ref[...].astype(F32), axis=0, keepdims=True)

    in_specs = [pl.BlockSpec((tt, m), lambda j, s: (s, 0)), pl.BlockSpec((tt, tn), lambda j, s: (s, j))]
    out_specs = [pl.BlockSpec((m, tn), lambda j, s: (0, j))]
    out_shape = [jax.ShapeDtypeStruct((m, n), F32)]
    if colsum:
        out_specs.append(pl.BlockSpec((1, tn), lambda j, s: (0, j)))
        out_shape.append(jax.ShapeDtypeStruct((1, n), F32))
    args, scratch = [a, b], []
    if swap is not None:
        gs, sm = swap
        any_spec = pl.BlockSpec(memory_space=pl.ANY)
        in_specs += [any_spec] * n_ex
        out_specs += [any_spec] * n_ex
        out_shape += _pair_got_shapes(gs, None) + [jax.ShapeDtypeStruct((N_DEV,) + sm.shape, F32)]
        args += [*gs, sm]
        scratch = [pltpu.SemaphoreType.DMA((_pair_count(gs, None),)), pltpu.SemaphoreType.DMA((_pair_count(gs, None),)),
                   pltpu.SemaphoreType.DMA((N_DEV - 1,)), pltpu.SemaphoreType.DMA((N_DEV - 1,)),
                   pltpu.SemaphoreType.DMA(())]
    return _pallas(
        body, name=name, grid=(nj, ns), in_specs=in_specs, out_specs=out_specs, out_shape=out_shape,
        scratch_shapes=scratch,
        compiler_params=_params(("parallel" if swap is None else "arbitrary", "arbitrary")),
    )(*args)


def dgrad_norm_bwd(dps, wts, paw, x2, ctx2, dh, scale1, norm_g, *, tm):
    t, tc = x2.shape[0], ctx2.shape[0]
    t_all = t + tc
    n_lat, n_ctx = t // tm, tc // tm
    n_tiles = n_lat + n_ctx
    n_samples = scale1.shape[0] - 1
    tps = n_lat // n_samples
    n_grp = n_samples + 1
    n_g = len(dps)
    whole = [g for g in range(n_g) if dps[g].shape[0] == t_all]
    latent = [g for g in range(n_g) if dps[g].shape[0] != t_all]

    def body(*refs):
        dp_refs, w_refs = refs[:n_g], refs[n_g:2 * n_g]
        (paw_ref, x_ref, c_ref, dh_ref, sc_ref, g_ref, dx_ref, dsh_ref, dsc_ref, dg_ref, rbw_ref,
         du_buf, b_send, b_recv) = refs[2 * n_g:]
        i = pl.program_id(0)

        def exchange():
            x, y, c, chips = _place()
            chunks = _half_chunks(0, (2 * paw_ref.shape[1],), 16, which=(0,))
            return [_remote(paw_ref.at[2 * cx + cy, rows], rbw_ref.at[j, rows],
                            b_send.at[j * N_BULK + k], b_recv.at[j * N_BULK + k], (cx, cy, c))
                    for j, (cx, cy) in enumerate(chips) for k, (_, rows) in enumerate(chunks)]

        @pl.when(i == 0)
        def _():
            for cp in exchange():
                cp.start()

        acc = None
        for g in whole:
            part = _nt(dp_refs[g][...], w_refs[g][...])
            acc = part if acc is None else acc + part
        du_buf[...] = acc

        @pl.when(i < n_lat)
        def _():
            lat = None
            for g in latent:
                part = _nt(dp_refs[g][...], w_refs[g][...])
                lat = part if lat is None else lat + part
            du_buf[...] += lat

        duv = du_buf[...]
        xv = jnp.where(i < n_lat, x_ref[...], c_ref[...])
        rs = lax.rsqrt(jnp.mean(xv * xv, axis=-1, keepdims=True) + EPS)
        xh = xv * rs
        n = xh * g_ref[...]
        dn = duv * sc_ref[0]
        dxh = dn * g_ref[...]
        dx = rs * (dxh - xh * jnp.mean(dxh * xh, axis=-1, keepdims=True))

        @pl.when(i < n_lat)
        def _():
            dx_ref[...] = dx + dh_ref[...]

        @pl.when((i % tps == 0) & (i <= n_lat))
        def _():
            dsh_ref[...] = jnp.zeros_like(dsh_ref)
            dsc_ref[...] = jnp.zeros_like(dsc_ref)

        @pl.when(i == 0)
        def _():
            dg_ref[...] = jnp.zeros_like(dg_ref)

        dsh_ref[0] += jnp.sum(duv, axis=0, keepdims=True)
        dsc_ref[0] += jnp.sum(duv * n, axis=0, keepdims=True)
        dg_ref[...] += jnp.sum(dn * xh, axis=0, keepdims=True)

        @pl.when(i == n_tiles - 1)
        def _():
            for cp in exchange():
                cp.wait_recv()
            for cp in exchange():
                cp.wait_send()

    lat = lambda i: (jnp.minimum(i, n_lat - 1), 0)
    grp = lambda i: (jnp.minimum(i // tps, n_samples), 0, 0)
    in_specs = []
    for g, dp in enumerate(dps):
        nrow = dp.shape[0] // tm
        in_specs.append(pl.BlockSpec((tm, dp.shape[1]), lambda i, nrow=nrow: (jnp.minimum(i, nrow - 1), 0)))
    for w in wts:
        in_specs.append(pl.BlockSpec(w.shape, lambda i: (0, 0), pipeline_mode=pl.Buffered(1)))
    any_spec = pl.BlockSpec(memory_space=pl.ANY)
    in_specs += [any_spec,
                 pl.BlockSpec((tm, D), lat), pl.BlockSpec((tm, D), lambda i: (jnp.maximum(i - n_lat, 0), 0)),
                 pl.BlockSpec((tm, D), lat), pl.BlockSpec((1, 1, D), grp), pl.BlockSpec((1, D), lambda i: (0, 0))]
    return _pallas(
        body, name="dgrad_norm_bwd", grid=(n_tiles,), in_specs=in_specs,
        out_specs=[pl.BlockSpec((tm, D), lat), pl.BlockSpec((1, 1, D), grp), pl.BlockSpec((1, 1, D), grp),
                   pl.BlockSpec((1, D), lambda i: (0, 0)), any_spec],
        out_shape=[jax.ShapeDtypeStruct((t, D), F32), jax.ShapeDtypeStruct((n_grp, 1, D), F32),
                   jax.ShapeDtypeStruct((n_grp, 1, D), F32), jax.ShapeDtypeStruct((1, D), F32),
                   jax.ShapeDtypeStruct((3,) + paw.shape[1:], paw.dtype)],
        scratch_shapes=[pltpu.VMEM((tm, D), F32), pltpu.SemaphoreType.DMA((3 * N_BULK,)),
                        pltpu.SemaphoreType.DMA((3 * N_BULK,))],
        compiler_params=_params(("arbitrary",)),
    )(*dps, *wts, paw, x2, ctx2, dh, scale1, norm_g)


TM_NORM = 512


def norm_mod_fwd(x2, ctx2, scale1, shift, norm_g):
    t = x2.shape[0]
    n_lat = t // TM_NORM
    assert ctx2.shape[0] == TM_NORM
    n_samples = scale1.shape[0] - 1
    tps = n_lat // n_samples

    def body(x_ref, c_ref, sc_ref, sh_ref, g_ref, u_ref):
        i = pl.program_id(0)
        xv = jnp.where(i < n_lat, x_ref[...], c_ref[...])
        rs = lax.rsqrt(jnp.mean(xv * xv, axis=-1, keepdims=True) + EPS)
        u = xv * rs * g_ref[...] * sc_ref[0] + sh_ref[0]
        u_ref[...] = u.astype(u_ref.dtype)

    grp = lambda i: (jnp.minimum(i // tps, n_samples), 0, 0)
    return _pallas(
        body, name="norm_mod_fwd", grid=(n_lat + 1,),
        in_specs=[pl.BlockSpec((TM_NORM, D), lambda i: (jnp.minimum(i, n_lat - 1), 0)),
                  pl.BlockSpec((TM_NORM, D), lambda i: (0, 0)),
                  pl.BlockSpec((1, 1, D), grp), pl.BlockSpec((1, 1, D), grp),
                  pl.BlockSpec((1, D), lambda i: (0, 0))],
        out_specs=pl.BlockSpec((TM_NORM, D), lambda i: (i, 0)),
        out_shape=jax.ShapeDtypeStruct((t + TM_NORM, D), BF16),
        compiler_params=_params(("parallel",)),
    )(x2, ctx2, scale1, shift, norm_g)


CONV_CB = 256
CONV_NCB = D // CONV_CB
H_OFF = 16


H_CB = 128
H_SPAN = GRID_W + 2 * H_OFF - 8


def _conv_scratch(vertical):
    if vertical:
        return [pltpu.VMEM((GRID_H + 2 * CONV_PAD, GRID_W, CONV_CB), F32)]
    return [pltpu.VMEM((GRID_H, GRID_W + 2 * H_OFF, H_CB), F32), pltpu.VMEM((7, GRID_H, H_SPAN, H_CB), F32)]


def _conv_fill(bufs, img, vertical):
    pad_ref = bufs[0]
    pad_ref[...] = jnp.zeros_like(pad_ref)
    if vertical:
        pad_ref[pl.ds(CONV_PAD, GRID_H)] = img
        return
    pad_ref[:, pl.ds(H_OFF, GRID_W), :] = img

    def shift(r, carry):
        for s in range(1, 8):
            bufs[1][s - 1, r] = pad_ref[r, pl.ds(s, H_SPAN), :]
        return carry

    lax.fori_loop(0, GRID_H, shift, 0)


def _conv_window(bufs, k, vertical, r, w0=0, nw=GRID_W, lanes=slice(None)):
    if vertical:
        return bufs[0][r + k, pl.ds(w0, nw), lanes]
    off = H_OFF - CONV_PAD + k
    if off % 8 == 0:
        return bufs[0][r, pl.ds(off + w0, nw), lanes]
    return bufs[1][off % 8 - 1, r, pl.ds(off - off % 8 + w0, nw), lanes]


def _conv_col_blocks(vertical):
    if vertical:
        return [pl.ds(0, CONV_CB)]
    return [pl.ds(i * H_CB, H_CB) for i in range(CONV_CB // H_CB)]


def _rows(r):
    return pl.ds(pl.multiple_of(r * GRID_W, GRID_W), GRID_W)


def conv_fwd(p1, conv_w, conv_b, n_samples):
    t = n_samples * SEQ

    def make(vertical, prev):
        n_buf = len(_conv_scratch(vertical))

        def body(gv_ref, gg_ref, w_ref, b_ref, *rest):
            o_ref, bufs = rest[-1 - n_buf], rest[-n_buf:]
            for cols in _conv_col_blocks(vertical):
                a = gv_ref[:, cols].astype(F32) * _sigmoid(gg_ref[:, cols].astype(F32))
                _conv_fill(bufs, a.reshape(GRID_H, GRID_W, a.shape[-1]), vertical)

                def row(r, carry, cols=cols):
                    acc = jnp.zeros((GRID_W, cols.size), F32) + b_ref[:, cols]
                    for k in range(CONV_K):
                        acc = acc + _conv_window(bufs, k, vertical, r) * w_ref[pl.ds(k, 1), cols]
                    o_ref[_rows(r), cols] = acc
                    return carry

                lax.fori_loop(0, GRID_H, row, 0)

        cb0 = CONV_NCB // 2 if vertical else 0
        in_specs = [pl.BlockSpec((SEQ, CONV_CB), lambda b, j: (b, 2 * (cb0 + j))),
                    pl.BlockSpec((SEQ, CONV_CB), lambda b, j: (b, 2 * (cb0 + j) + 1)),
                    pl.BlockSpec((CONV_K + 1, CONV_CB), lambda b, j: (0, cb0 + j)),
                    pl.BlockSpec((1, CONV_CB), lambda b, j: (0, cb0 + j))]
        args = [p1, p1, conv_w, conv_b]
        aliases = {}
        if prev is not None:
            in_specs.append(pl.BlockSpec(memory_space=pl.ANY))
            args.append(prev)
            aliases = {4: 0}
        return _pallas(
            body, name="conv_fwd_v" if vertical else "conv_fwd_h", grid=(n_samples, CONV_NCB // 2),
            in_specs=in_specs,
            out_specs=pl.BlockSpec((SEQ, CONV_CB), lambda b, j: (b, cb0 + j)),
            out_shape=jax.ShapeDtypeStruct((t, D), F32),
            scratch_shapes=_conv_scratch(vertical),
            input_output_aliases=aliases,
            compiler_params=_params(("parallel", "parallel")),
        )(*args)

    return make(True, make(False, None))


def conv_bwd(p1, daconv, conv_w, n_samples):
    t = n_samples * SEQ

    def make(vertical, prev):
        n_buf = len(_conv_scratch(vertical))

        def body(gv_ref, gg_ref, dy_ref, w_ref, *rest):
            dp_ref, dw_ref, db_ref = rest[-3 - 2 * n_buf - 1:-2 * n_buf - 1]
            a_bufs, d_bufs, da_ref = rest[-2 * n_buf - 1:-n_buf - 1], rest[-n_buf - 1:-1], rest[-1]
            for cols in _conv_col_blocks(vertical):
                width = cols.size
                gv = gv_ref[:, cols].astype(F32)
                sg = _sigmoid(gg_ref[:, cols].astype(F32))
                _conv_fill(a_bufs, (gv * sg).reshape(GRID_H, GRID_W, width), vertical)
                _conv_fill(d_bufs, dy_ref[:, cols].reshape(GRID_H, GRID_W, width), vertical)

                def row(r, carry, cols=cols, width=width):
                    acc = jnp.zeros((GRID_W, width), F32)
                    for k in range(CONV_K):
                        acc = acc + _conv_window(d_bufs, CONV_K - 1 - k, vertical, r) * w_ref[pl.ds(k, 1), cols]
                    da_ref[_rows(r), cols] = acc
                    return carry

                lax.fori_loop(0, GRID_H, row, 0)
                da = da_ref[:, cols]
                dp_ref[:, pl.ds(cols.start, width)] = (da * sg).astype(dp_ref.dtype)
                dp_ref[:, pl.ds(CONV_CB + cols.start, width)] = (da * gv * sg * (1.0 - sg)).astype(dp_ref.dtype)

                for lb in range(width // 128):
                    lanes = pl.ds(lb * 128, 128)
                    dy_lanes = pl.ds(cols.start + lb * 128, 128)

                    def wrow(r, accs, lanes=lanes, dy_lanes=dy_lanes):
                        for w0 in range(0, GRID_W, 8):
                            dyv = dy_ref[pl.ds(pl.multiple_of(r * GRID_W, GRID_W) + w0, 8), dy_lanes]
                            accs = tuple(accs[k] + _conv_window(a_bufs, k, vertical, r, w0, 8, lanes) * dyv
                                         for k in range(CONV_K))
                        return accs

                    accs = lax.fori_loop(0, GRID_H, wrow, tuple(jnp.zeros((8, 128), F32) for _ in range(CONV_K)))
                    for k in range(CONV_K):
                        dw_ref[0, pl.ds(k, 1), dy_lanes] = jnp.sum(accs[k], axis=0, keepdims=True)
            dw_ref[0, pl.ds(CONV_K, 1), :] = jnp.zeros((1, CONV_CB), F32)
            db_ref[0] = jnp.sum(dy_ref[...], axis=0, keepdims=True)

        cb0 = CONV_NCB // 2 if vertical else 0
        in_specs = [pl.BlockSpec((SEQ, CONV_CB), lambda b, j: (b, 2 * (cb0 + j))),
                    pl.BlockSpec((SEQ, CONV_CB), lambda b, j: (b, 2 * (cb0 + j) + 1)),
                    pl.BlockSpec((SEQ, CONV_CB), lambda b, j: (b, cb0 + j)),
                    pl.BlockSpec((CONV_K + 1, CONV_CB), lambda b, j: (0, cb0 + j))]
        args = [p1, p1, daconv, conv_w]
        aliases = {}
        if prev is not None:
            in_specs += [pl.BlockSpec(memory_space=pl.ANY)] * 3
            args += list(prev)
            aliases = {4: 0, 5: 1, 6: 2}
        return _pallas(
            body, name="conv_bwd_v" if vertical else "conv_bwd_h", grid=(n_samples, CONV_NCB // 2),
            in_specs=in_specs,
            out_specs=[pl.BlockSpec((SEQ, 2 * CONV_CB), lambda b, j: (b, cb0 + j)),
                       pl.BlockSpec((1, CONV_K + 1, CONV_CB), lambda b, j: (b, 0, cb0 + j)),
                       pl.BlockSpec((1, 1, CONV_CB), lambda b, j: (b, 0, cb0 + j))],
            out_shape=[jax.ShapeDtypeStruct((t, 2 * D), BF16),
                       jax.ShapeDtypeStruct((n_samples, CONV_K + 1, D), F32),
                       jax.ShapeDtypeStruct((n_samples, 1, D), F32)],
            scratch_shapes=_conv_scratch(vertical) + _conv_scratch(vertical) + [pltpu.VMEM((SEQ, CONV_CB), F32)],
            input_output_aliases=aliases,
            compiler_params=_params(("parallel", "parallel")),
        )(*args)

    return make(True, make(False, None))


TM_EW = 512


def ln_gate_proj(aconv, z, ln_g, ln_b, conv_proj):
    t = aconv.shape[0]

    def body(a_ref, z_ref, g_ref, b_ref, w_ref, o_ref, y_ref):
        a = a_ref[...]
        mu = jnp.mean(a, axis=-1, keepdims=True)
        xc = a - mu
        rstd = lax.rsqrt(jnp.mean(xc * xc, axis=-1, keepdims=True) + EPS)
        l = xc * rstd * g_ref[...] + b_ref[...]
        ac = _b16(_silu(l) * _silu(z_ref[...].astype(F32)))
        o_ref[...] = ac
        y_ref[...] = _nn(ac, w_ref[...]).astype(y_ref.dtype)

    row = pl.BlockSpec((TM_OUT, D), lambda i: (i, 0))
    vec = pl.BlockSpec((1, D), lambda i: (0, 0))
    return _pallas(
        body, name="ln_gate_proj", grid=(t // TM_OUT,),
        in_specs=[row, row, vec, vec, pl.BlockSpec((D, D), lambda i: (0, 0))], out_specs=[row, row],
        out_shape=[jax.ShapeDtypeStruct((t, D), BF16), jax.ShapeDtypeStruct((t, D), BF16)],
        compiler_params=_params(("parallel",)),
    )(aconv, z, ln_g, ln_b, conv_proj)


def ln_gate_bwd(aconv, z, dyc, conv_proj, ln_g, ln_b, ac):
    t = aconv.shape[0]

    def body(a_ref, z_ref, d_ref, w_ref, g_ref, b_ref, ac_ref, da_ref, dz_ref, dg_ref, db_ref, gw_ref):
        @pl.when(pl.program_id(0) == 0)
        def _():
            gw_ref[...] = jnp.zeros_like(gw_ref)

        gw_ref[...] += _tn(ac_ref[...], d_ref[...])
        a = a_ref[...]
        zv = z_ref[...].astype(F32)
        dac_v = _nt(d_ref[...], w_ref[...])
        mu = jnp.mean(a, axis=-1, keepdims=True)
        xc = a - mu
        rstd = lax.rsqrt(jnp.mean(xc * xc, axis=-1, keepdims=True) + EPS)
        xh = xc * rstd
        l = xh * g_ref[...] + b_ref[...]
        dz_ref[...] = (dac_v * _silu(l) * _dsilu(zv)).astype(dz_ref.dtype)
        dl = dac_v * _silu(zv) * _dsilu(l)
        dxh = dl * g_ref[...]
        da_ref[...] = rstd * (dxh - jnp.mean(dxh, axis=-1, keepdims=True)
                              - xh * jnp.mean(dxh * xh, axis=-1, keepdims=True))

        @pl.when(pl.program_id(0) == 0)
        def _():
            dg_ref[...] = jnp.zeros_like(dg_ref)
            db_ref[...] = jnp.zeros_like(db_ref)

        dg_ref[...] += jnp.sum(dl * xh, axis=0, keepdims=True)
        db_ref[...] += jnp.sum(dl, axis=0, keepdims=True)

    row = pl.BlockSpec((TM_EW, D), lambda i: (i, 0))
    vec = pl.BlockSpec((1, D), lambda i: (0, 0))
    return _pallas(
        body, name="ln_gate_bwd", grid=(t // TM_EW,),
        in_specs=[row, row, row, pl.BlockSpec((D, D), lambda i: (0, 0)), vec, vec, row],
        out_specs=[row, row, vec, vec, pl.BlockSpec((D, D), lambda i: (0, 0))],
        out_shape=[jax.ShapeDtypeStruct((t, D), F32), jax.ShapeDtypeStruct((t, D), BF16),
                   jax.ShapeDtypeStruct((1, D), F32), jax.ShapeDtypeStruct((1, D), F32),
                   jax.ShapeDtypeStruct((D, D), F32)],
        compiler_params=_params(("arbitrary",)),
    )(aconv, z, dyc, conv_proj, ln_g, ln_b, ac)


TM_PREP = 256
PREP_LAT = SEQ // TM_PREP
PREP_ALL = SEQ_ALL // TM_PREP


def _chunk_tri(n, upper):
    r = lax.broadcasted_iota(jnp.int32, (n, n), 0)
    c = lax.broadcasted_iota(jnp.int32, (n, n), 1)
    same = (r // CHUNK) == (c // CHUNK)
    keep = (c >= r) if upper else (c <= r)
    return jnp.where(same & keep, 1.0, 0.0).astype(F32)


def _split3(v):
    hi = v.astype(BF16)
    r1 = v - hi.astype(F32)
    mid = r1.astype(BF16)
    lo = (r1 - mid.astype(F32)).astype(BF16)
    return jnp.stack([hi, mid, lo])


def _chunk_sums(v, upper):
    tri = _chunk_tri(v.shape[0], upper).astype(BF16)
    pieces = _split3(v)
    return (_nn(tri, pieces[0]) + _nn(tri, pieces[1])) + _nn(tri, pieces[2])


def _gate_logits(ab, up3_ref, bias_ref):
    assert ab.dtype == BF16
    return ((_nn(ab, up3_ref[0]) + _nn(ab, up3_ref[1])) + _nn(ab, up3_ref[2])) + bias_ref[...]


def _prep_tile_maps(n_samples):
    n_lat = n_samples * PREP_LAT

    def seq_map(i):
        return jnp.where(i < n_lat, i // PREP_LAT, i - n_lat), jnp.where(i < n_lat, i % PREP_LAT, PREP_LAT)

    return n_lat, seq_map


def gla_prep_fwd(p3, upf, upb, bias_f, bias_b, n_samples):
    n_lat, seq_map = _prep_tile_maps(n_samples)
    n_tiles = n_lat + n_samples

    def body(v_ref, q_ref, k_ref, ab_ref, upf_ref, upb_ref, bf_ref, bb_ref, qo, ko, vo, cf, cb):
        i = pl.program_id(0)
        qo[0] = jnp.where(i < n_lat, q_ref[...].astype(F32) * Q_SCALE, 0.0)
        ko[0] = k_ref[...]
        vo[0] = v_ref[...]
        ab = ab_ref[...]
        gf = _log_sigmoid(_gate_logits(ab, upf_ref, bf_ref)) * (1.0 / GATE_TAU)
        gb = _log_sigmoid(_gate_logits(ab, upb_ref, bb_ref)) * (1.0 / GATE_TAU)
        cf[0] = _chunk_sums(gf, False)
        cb[0] = _chunk_sums(gb, True)

    def o_spec(w):
        return pl.BlockSpec((1, TM_PREP, w), lambda i: (*seq_map(i), 0))

    full = lambda shape: pl.BlockSpec(shape, lambda i: (0,) * len(shape))
    return _pallas(
        body, name="gla_prep_fwd", grid=(n_tiles,),
        in_specs=[pl.BlockSpec((TM_PREP, 1024), lambda i: (i, O3_V // 1024)),
                  pl.BlockSpec((TM_PREP, 512), lambda i: (i, O3_Q // 512)),
                  pl.BlockSpec((TM_PREP, 512), lambda i: (i, O3_K // 512)),
                  pl.BlockSpec((TM_PREP, 128), lambda i: (i, O3_AB // 128)),
                  full((3, 128, GLA_DK)), full((3, 128, GLA_DK)), full((1, GLA_DK)), full((1, GLA_DK))],
        out_specs=[o_spec(GLA_DK), o_spec(GLA_DK), o_spec(D), o_spec(GLA_DK), o_spec(GLA_DK)],
        out_shape=[jax.ShapeDtypeStruct((n_samples, SEQ_ALL, GLA_DK), F32),
                   jax.ShapeDtypeStruct((n_samples, SEQ_ALL, GLA_DK), p3.dtype),
                   jax.ShapeDtypeStruct((n_samples, SEQ_ALL, D), p3.dtype),
                   jax.ShapeDtypeStruct((n_samples, SEQ_ALL, GLA_DK), F32),
                   jax.ShapeDtypeStruct((n_samples, SEQ_ALL, GLA_DK), F32)],
        compiler_params=_params(("parallel",)),
    )(p3, p3, p3, p3, upf, upb, bias_f, bias_b)


def gla_prep_bwd(p3, dq_f, dq_b, dk_f, dk_b, dv_f, dv_b, dc_f, dc_b, upf, upb, bias_f, bias_b, n_samples):
    n_lat, seq_map = _prep_tile_maps(n_samples)
    n_tiles = n_lat + n_samples

    def body(ab_ref, dqf, dqb, dkf, dkb, dvf, dvb, dcf, dcb, upf_ref, upb_ref, bf_ref, bb_ref,
             dp_ref, duf_ref, dub_ref, dbf_ref, dbb_ref):
        i = pl.program_id(0)
        both = lambda a, b: a[0].astype(F32) + b[0].astype(F32)
        dp_ref[:, pl.ds(O3_V, D)] = both(dvf, dvb).astype(dp_ref.dtype)
        dq = jnp.where(i < n_lat, both(dqf, dqb) * Q_SCALE, 0.0)
        dp_ref[:, pl.ds(O3_Q, GLA_DK)] = dq.astype(dp_ref.dtype)
        dp_ref[:, pl.ds(O3_K, GLA_DK)] = both(dkf, dkb).astype(dp_ref.dtype)
        ab = ab_ref[...]
        zf = _gate_logits(ab, upf_ref, bf_ref)
        zb = _gate_logits(ab, upb_ref, bb_ref)
        dgf = _chunk_sums(dcf[0], True)
        dgb = _chunk_sums(dcb[0], False)
        dzf = _b16(dgf * (1.0 / GATE_TAU) * _sigmoid(-zf))
        dzb = _b16(dgb * (1.0 / GATE_TAU) * _sigmoid(-zb))
        dab = _nt(dzf, upf_ref[0]) + _nt(dzb, upb_ref[0])
        dp_ref[:, pl.ds(O3_AB, 128)] = dab.astype(dp_ref.dtype)

        @pl.when(i == 0)
        def _():
            duf_ref[...] = jnp.zeros_like(duf_ref)
            dub_ref[...] = jnp.zeros_like(dub_ref)
            dbf_ref[...] = jnp.zeros_like(dbf_ref)
            dbb_ref[...] = jnp.zeros_like(dbb_ref)

        duf_ref[...] += _tn(ab, dzf)
        dub_ref[...] += _tn(ab, dzb)
        dbf_ref[...] += jnp.sum(dzf.astype(F32), axis=0, keepdims=True)
        dbb_ref[...] += jnp.sum(dzb.astype(F32), axis=0, keepdims=True)

    def s_spec(w):
        return pl.BlockSpec((1, TM_PREP, w), lambda i: (*seq_map(i), 0))

    full = lambda shape: pl.BlockSpec(shape, lambda i: (0,) * len(shape))
    return _pallas(
        body, name="gla_prep_bwd", grid=(n_tiles,),
        in_specs=[pl.BlockSpec((TM_PREP, 128), lambda i: (i, O3_AB // 128)),
                  s_spec(GLA_DK), s_spec(GLA_DK), s_spec(GLA_DK), s_spec(GLA_DK), s_spec(D), s_spec(D),
                  s_spec(GLA_DK), s_spec(GLA_DK),
                  full((3, 128, GLA_DK)), full((3, 128, GLA_DK)), full((1, GLA_DK)), full((1, GLA_DK))],
        out_specs=[pl.BlockSpec((TM_PREP, W3), lambda i: (i, 0)),
                   full((128, GLA_DK)), full((128, GLA_DK)), full((1, GLA_DK)), full((1, GLA_DK))],
        out_shape=[jax.ShapeDtypeStruct((n_tiles * TM_PREP, W3), BF16),
                   jax.ShapeDtypeStruct((128, GLA_DK), F32), jax.ShapeDtypeStruct((128, GLA_DK), F32),
                   jax.ShapeDtypeStruct((1, GLA_DK), F32), jax.ShapeDtypeStruct((1, GLA_DK), F32)],
        compiler_params=_params(("arbitrary",)),
    )(p3, dq_f, dq_b, dk_f, dk_b, dv_f, dv_b, dc_f, dc_b, upf, upb, bias_f, bias_b)


def _sub_blocks(rev):
    if NSUB == 1:
        return [((0, CHUNK), CHUNK // 2, (0, CHUNK))]
    out = []
    for s in range(NSUB):
        rows = (s * SUB, SUB)
        if rev:
            ref = (s + 1) * SUB if s < NSUB - 1 else None
            cols = (s * SUB, CHUNK - s * SUB)
        else:
            ref = s * SUB - 1 if s > 0 else None
            cols = (0, (s + 1) * SUB)
        out.append((rows, ref, cols))
    return out


def _sub_mask(rows, cols, rev):
    r = rows[0] + lax.broadcasted_iota(jnp.int32, (rows[1], cols[1]), 0)
    c = cols[0] + lax.broadcasted_iota(jnp.int32, (rows[1], cols[1]), 1)
    return (c >= r) if rev else (c <= r)


def _sub_operands(qc, kc, cc, rows, ref, cols):
    cref = jnp.zeros((1, HEAD_K), F32) if ref is None else cc[ref:ref + 1]
    eq = jnp.exp(cc[rows[0]:rows[0] + rows[1]] - cref)
    ek = jnp.exp(cref - cc[cols[0]:cols[0] + cols[1]])
    qs = qc[rows[0]:rows[0] + rows[1]] * eq
    kk = kc[cols[0]:cols[0] + cols[1]] * ek
    return qs, kk, eq, ek


SCAN_ROWS = 256
SCAN_CHUNKS = SCAN_ROWS // CHUNK
SCAN_STEPS = SEQ_ALL // SCAN_ROWS
LAT_BLOCKS = SEQ // SCAN_ROWS


def _scan_block(t, rev):
    if rev:
        return SCAN_STEPS - 1 - t
    return jnp.where(t == 0, SCAN_STEPS - 1, t - 1)


def _scan_lat_block(t, rev):
    if rev:
        return jnp.minimum(SCAN_STEPS - 1 - t, LAT_BLOCKS - 1)
    return jnp.maximum(t - 1, 0)


def _head_cols(h):
    return pl.ds(h * HEAD_K, HEAD_K), pl.ds(h * HEAD_V, HEAD_V)


def gla_scan_fwd(q, k, v, cum, *, rev, name):
    n = q.shape[0]

    def body(q_ref, k_ref, v_ref, c_ref, o_ref, s_ref, sfin_ref, st):
        t = pl.program_id(1)

        @pl.when(t == 0)
        def _():
            st[...] = jnp.zeros_like(st)

        def chunk(j, carry):
            lj = SCAN_CHUNKS - 1 - j if rev else j
            r0 = lj * CHUNK
            rws = pl.ds(r0, CHUNK)
            for h in range(HEADS):
                kcols, vcols = _head_cols(h)
                qc, kc, cc = q_ref[0, rws, kcols], k_ref[0, rws, kcols], c_ref[0, rws, kcols]
                vc = v_ref[0, rws, vcols]
                s_in = st[h]
                s_ref[0, h, j] = _b16(s_in)
                edge = cc[0:1] if rev else cc[CHUNK - 1:CHUNK]
                ke = kc * jnp.exp(edge - cc)
                st[h] = s_in * jnp.exp(edge) + _tn(_b16(vc), _b16(ke))
                o_inter = _nt(_b16(qc * jnp.exp(cc)), _b16(s_in))
                vb = _b16(vc)
                for rows, ref, cols in _sub_blocks(rev):
                    qs, kk, _, _ = _sub_operands(qc, kc, cc, rows, ref, cols)
                    a = jnp.where(_sub_mask(rows, cols, rev), _nt(_b16(qs), _b16(kk)), 0.0)
                    o_s = _nn(_b16(a), vb[cols[0]:cols[0] + cols[1]])
                    o_ref[0, pl.ds(r0 + rows[0], rows[1]), vcols] = _b16(o_inter[rows[0]:rows[0] + rows[1]] + o_s)
            return carry

        for j in range(SCAN_CHUNKS):
            chunk(j, 0)

        @pl.when(t == SCAN_STEPS - 1)
        def _():
            sfin_ref[0] = st[...]

    def spec(w):
        return pl.BlockSpec((1, SCAN_ROWS, w), lambda b, t: (b, _scan_block(t, rev), 0))

    return _pallas(
        body, name=name, grid=(n, SCAN_STEPS),
        in_specs=[spec(GLA_DK), spec(GLA_DK), spec(D), spec(GLA_DK)],
        out_specs=[pl.BlockSpec((1, SCAN_ROWS, D), lambda b, t: (b, _scan_lat_block(t, rev), 0)),
                   pl.BlockSpec((1, HEADS, SCAN_CHUNKS, HEAD_V, HEAD_K), lambda b, t: (b, 0, t, 0, 0)),
                   pl.BlockSpec((1, HEADS, HEAD_V, HEAD_K), lambda b, t: (b, 0, 0, 0))],
        out_shape=[jax.ShapeDtypeStruct((n, SEQ, D), BF16),
                   jax.ShapeDtypeStruct((n, HEADS, NCHUNK, HEAD_V, HEAD_K), BF16),
                   jax.ShapeDtypeStruct((n, HEADS, HEAD_V, HEAD_K), F32)],
        scratch_shapes=[pltpu.VMEM((HEADS, HEAD_V, HEAD_K), F32)],
        compiler_params=_params(("parallel", "arbitrary")),
    )(q, k, v, cum)


def gla_scan_bwd(q, k, v, cum, s_all, s_fin, do, *, rev, name, rider=None):
    n = q.shape[0]

    def body(q_ref, k_ref, v_ref, c_ref, s_ref, sfin_ref, do_ref, *rest):
        if rider is not None:
            ride_in, rest = rest[0], rest[1:]
        dq_ref, dk_ref, dv_ref, dc_ref = rest[:4]
        if rider is not None:
            ride_out, rest = rest[4], rest[:4] + rest[5:]
        dst, s_next, dq_acc, dk_acc, dv_acc = rest[4:9]
        t = SCAN_STEPS - 1 - pl.program_id(1)

        if rider is not None:
            def copies():
                send, recv = rest[9], rest[10]
                if rider[0] == "swap":
                    return _pair_copies([], ride_in, [], ride_out, send, recv)
                x, y, c, chips = _place()
                return [_remote(ride_in.at[2 * cx + cy, rows], ride_out.at[pj, rows], send.at[pj * P_ROW_CHUNKS + pi],
                                recv.at[pj * P_ROW_CHUNKS + pi], (cx, cy, c))
                        for pj, (cx, cy) in enumerate(chips)
                        for pi, (_, rows) in enumerate(_half_chunks(0, (0, 2 * ride_in.shape[1]), 16, which=(1,)))]

            @pl.when((pl.program_id(0) == 0) & (pl.program_id(1) == 0))
            def _():
                for cp in copies():
                    cp.start()

            @pl.when((pl.program_id(0) == n - 1) & (pl.program_id(1) == SCAN_STEPS - 1))
            def _():
                for cp in copies():
                    cp.wait_recv()
                for cp in copies():
                    cp.wait_send()

        @pl.when(pl.program_id(1) == 0)
        def _():
            dst[...] = jnp.zeros_like(dst)
            s_next[...] = sfin_ref[0]

        def chunk(jj, carry):
            j = SCAN_CHUNKS - 1 - jj
            lj = SCAN_CHUNKS - 1 - j if rev else j
            rws = pl.ds(lj * CHUNK, CHUNK)
            for h in range(HEADS):
                kcols, vcols = _head_cols(h)
                qc, kc, cc = q_ref[0, rws, kcols], k_ref[0, rws, kcols], c_ref[0, rws, kcols]
                vc = v_ref[0, rws, vcols]
                doc = jnp.where(t > 0, do_ref[0, rws, vcols], 0.0)
                s_in = s_ref[0, h, j]
                s_out = s_next[h]
                ds_out = dst[h]
                edge = cc[0:1] if rev else cc[CHUNK - 1:CHUNK]
                e_q = jnp.exp(cc)
                e_k = jnp.exp(edge - cc)
                dob = _b16(doc)
                dsb = _b16(ds_out)
                dst[h] = ds_out * jnp.exp(edge) + _tn(dob, _b16(qc * e_q))
                s_next[h] = s_in.astype(F32)
                dq_acc[h] = e_q * _nn(dob, s_in)
                dk_acc[h] = e_k * _nn(_b16(vc), dsb)
                dv_acc[h] = _nt(_b16(kc * e_k), dsb)
                vb = _b16(vc)
                for rows, ref, cols in _sub_blocks(rev):
                    qs, kk, eq, ek = _sub_operands(qc, kc, cc, rows, ref, cols)
                    mask = _sub_mask(rows, cols, rev)
                    rsl = slice(rows[0], rows[0] + rows[1])
                    csl = pl.ds(cols[0], cols[1])
                    qsb, kkb = _b16(qs), _b16(kk)
                    a = jnp.where(mask, _nt(qsb, kkb), 0.0)
                    da = _b16(jnp.where(mask, _nt(dob[rsl], vb[cols[0]:cols[0] + cols[1]]), 0.0))
                    dq_acc[h, pl.ds(rows[0], rows[1]), :] += _nn(da, kkb) * eq
                    dk_acc[h, csl, :] += _tn(da, qsb) * ek
                    dv_acc[h, csl, :] += _tn(_b16(a), dob[rsl])
                dq = dq_acc[h]
                dk = dk_acc[h]
                dc = qc * dq - kc * dk
                bnd = jnp.sum(ds_out * s_out, axis=0, keepdims=True)
                edge_row = 0 if rev else CHUNK - 1
                is_edge = lax.broadcasted_iota(jnp.int32, (CHUNK, HEAD_K), 0) == edge_row
                dq_ref[0, rws, kcols] = _b16(dq)
                dk_ref[0, rws, kcols] = _b16(dk)
                dv_ref[0, rws, vcols] = _b16(dv_acc[h])
                dc_ref[0, rws, kcols] = dc + jnp.where(is_edge, bnd, 0.0)
            return carry

        for jj in range(SCAN_CHUNKS):
            chunk(jj, 0)

    def step_of(u):
        return SCAN_STEPS - 1 - u

    def spec(w):
        return pl.BlockSpec((1, SCAN_ROWS, w), lambda b, u: (b, _scan_block(step_of(u), rev), 0))

    in_specs = [spec(GLA_DK), spec(GLA_DK), spec(D), spec(GLA_DK),
                pl.BlockSpec((1, HEADS, SCAN_CHUNKS, HEAD_V, HEAD_K), lambda b, u: (b, 0, step_of(u), 0, 0)),
                pl.BlockSpec((1, HEADS, HEAD_V, HEAD_K), lambda b, u: (b, 0, 0, 0)),
                pl.BlockSpec((1, SCAN_ROWS, D), lambda b, u: (b, _scan_lat_block(step_of(u), rev), 0))]
    out_specs = [spec(GLA_DK), spec(GLA_DK), spec(D), spec(GLA_DK)]
    out_shape = [jax.ShapeDtypeStruct((n, SEQ_ALL, GLA_DK), BF16), jax.ShapeDtypeStruct((n, SEQ_ALL, GLA_DK), BF16),
                 jax.ShapeDtypeStruct((n, SEQ_ALL, D), BF16), jax.ShapeDtypeStruct((n, SEQ_ALL, GLA_DK), F32)]
    scratch = [pltpu.VMEM((HEADS, HEAD_V, HEAD_K), F32), pltpu.VMEM((HEADS, HEAD_V, HEAD_K), F32),
               pltpu.VMEM((HEADS, CHUNK, HEAD_K), F32), pltpu.VMEM((HEADS, CHUNK, HEAD_K), F32),
               pltpu.VMEM((HEADS, CHUNK, HEAD_V), F32)]
    args = [q, k, v, cum, s_all, s_fin, do]
    if rider is not None:
        kind, arr = rider
        any_spec = pl.BlockSpec(memory_space=pl.ANY)
        in_specs.append(any_spec)
        out_specs.append(any_spec)
        args.append(arr)
        if kind == "swap":
            out_shape += _pair_got_shapes([], arr)
            n_cp = _pair_count([], arr)
        else:
            out_shape.append(jax.ShapeDtypeStruct((3,) + arr.shape[1:], arr.dtype))
            n_cp = 3 * P_ROW_CHUNKS
        scratch += [pltpu.SemaphoreType.DMA((n_cp,)), pltpu.SemaphoreType.DMA((n_cp,))]
    return _pallas(
        body, name=name, grid=(n, SCAN_STEPS), in_specs=in_specs, out_specs=out_specs, out_shape=out_shape,
        scratch_shapes=scratch,
        compiler_params=_params(("parallel" if rider is None else "arbitrary", "arbitrary")),
    )(*args)


def gla_out_proj(o_f, o_b, r, gnorm, gla_proj):
    n = o_f.shape[0]
    tiles = SEQ // TM_OUT

    def body(of_ref, ob_ref, r_ref, g_ref, w_ref, og_ref, y_ref):
        for h in range(HEADS):
            cols = pl.ds(h * HEAD_V, HEAD_V)
            o = of_ref[0, :, cols].astype(F32) + ob_ref[0, :, cols].astype(F32)
            rs = lax.rsqrt(jnp.mean(o * o, axis=-1, keepdims=True) + EPS)
            og_ref[:, cols] = (o * rs * g_ref[...] * _silu(r_ref[:, cols].astype(F32))).astype(og_ref.dtype)
        y_ref[...] = _nn(og_ref[...], w_ref[...]).astype(y_ref.dtype)

    ospec = pl.BlockSpec((1, TM_OUT, D), lambda b, j: (b, j, 0))
    row = pl.BlockSpec((TM_OUT, D), lambda b, j: (b * tiles + j, 0))
    return _pallas(
        body, name="gla_out_proj", grid=(n, tiles),
        in_specs=[ospec, ospec, row, pl.BlockSpec((1, HEAD_V), lambda b, j: (0, 0)),
                  pl.BlockSpec((D, D), lambda b, j: (0, 0))],
        out_specs=[row, row],
        out_shape=[jax.ShapeDtypeStruct((n * SEQ, D), BF16), jax.ShapeDtypeStruct((n * SEQ, D), BF16)],
        compiler_params=_params(("parallel", "parallel")),
    )(o_f, o_b, r, gnorm, gla_proj)


def gla_out_bwd(o_f, o_b, r, dyg, gla_proj, gnorm, og):
    n = o_f.shape[0]
    tiles = SEQ // TM_EW

    def body(of_ref, ob_ref, r_ref, d_ref, w_ref, g_ref, og_ref, do_ref, dr_ref, dg_ref, gw_ref, dog_buf):
        @pl.when((pl.program_id(0) == 0) & (pl.program_id(1) == 0))
        def _():
            dg_ref[...] = jnp.zeros_like(dg_ref)
            gw_ref[...] = jnp.zeros_like(gw_ref)

        gw_ref[...] += _tn(og_ref[...], d_ref[...])
        dog_buf[...] = _nt(d_ref[...], w_ref[...])
        for h in range(HEADS):
            cols = pl.ds(h * HEAD_V, HEAD_V)
            o = of_ref[0, :, cols].astype(F32) + ob_ref[0, :, cols].astype(F32)
            rv = r_ref[:, cols].astype(F32)
            dv = dog_buf[:, cols]
            rs = lax.rsqrt(jnp.mean(o * o, axis=-1, keepdims=True) + EPS)
            oh = o * rs
            dr_ref[:, cols] = (dv * oh * g_ref[...] * _dsilu(rv)).astype(dr_ref.dtype)
            dn = dv * _silu(rv)
            dg_ref[...] += jnp.sum(dn * oh, axis=0, keepdims=True)
            doh = dn * g_ref[...]
            do_ref[0, :, cols] = _b16(rs * (doh - oh * jnp.mean(doh * oh, axis=-1, keepdims=True)))

    ospec = pl.BlockSpec((1, TM_EW, D), lambda b, j: (b, j, 0))
    row = pl.BlockSpec((TM_EW, D), lambda b, j: (b * tiles + j, 0))
    vec = pl.BlockSpec((1, HEAD_V), lambda b, j: (0, 0))
    return _pallas(
        body, name="gla_out_bwd", grid=(n, tiles),
        in_specs=[ospec, ospec, row, row, pl.BlockSpec((D, D), lambda b, j: (0, 0)), vec, row],
        out_specs=[ospec, row, vec, pl.BlockSpec((D, D), lambda b, j: (0, 0))],
        out_shape=[jax.ShapeDtypeStruct((n, SEQ, D), BF16), jax.ShapeDtypeStruct((n * SEQ, D), BF16),
                   jax.ShapeDtypeStruct((1, HEAD_V), F32), jax.ShapeDtypeStruct((D, D), F32)],
        scratch_shapes=[pltpu.VMEM((TM_EW, D), F32)],
        compiler_params=_params(("arbitrary", "arbitrary")),
    )(o_f, o_b, r, dyg, gla_proj, gnorm, og)


TM_OUT = 512


def merge_out_final(p5, y_conv, y_gla, w_out, x2, gate, final_g, target, n_samples):
    t = x2.shape[0]
    tiles = SEQ // TM_OUT

    def body(mc_ref, mg_ref, yc_ref, yg_ref, w_ref, x_ref, gate_ref, g_ref, t_ref,
             mrg_ref, dh_ref, dmo_ref, dgate_ref, dg_ref, loss_ref):
        b, j = pl.program_id(0), pl.program_id(1)
        f = lambda ref: ref[...].astype(F32)
        merged = _b16(_sigmoid(f(mc_ref)) * f(yc_ref) + _sigmoid(f(mg_ref)) * f(yg_ref))
        mrg_ref[...] = merged
        mo_v = _nn(merged, w_ref[...])
        h = x_ref[...] + gate_ref[0] * mo_v
        rs = lax.rsqrt(jnp.mean(h * h, axis=-1, keepdims=True) + EPS)
        nh = h * rs
        err = nh * g_ref[...] - t_ref[...]
        dy = err * (1.0 / D)
        dn = dy * g_ref[...]
        dh = rs * (dn - nh * jnp.mean(dn * nh, axis=-1, keepdims=True))
        dh_ref[...] = dh
        dmo_ref[...] = (dh * gate_ref[0]).astype(dmo_ref.dtype)

        @pl.when(j == 0)
        def _():
            dgate_ref[...] = jnp.zeros_like(dgate_ref)

        @pl.when((b == 0) & (j == 0))
        def _():
            dg_ref[...] = jnp.zeros_like(dg_ref)
            loss_ref[...] = jnp.zeros_like(loss_ref)

        dgate_ref[0] += jnp.sum(dh * mo_v, axis=0, keepdims=True)
        dg_ref[...] += jnp.sum(dy * nh, axis=0, keepdims=True)
        loss_ref[...] += (0.5 / D) * jnp.sum(err * err)

    row = pl.BlockSpec((TM_OUT, D), lambda b, j: (b * tiles + j, 0))
    per = pl.BlockSpec((1, 1, D), lambda b, j: (b, 0, 0))
    vec = pl.BlockSpec((1, D), lambda b, j: (0, 0))
    return _pallas(
        body, name="merge_out_final", grid=(n_samples, tiles),
        in_specs=[row, pl.BlockSpec((TM_OUT, D), lambda b, j: (b * tiles + j, 1)), row, row,
                  pl.BlockSpec((D, D), lambda b, j: (0, 0)), row, per, vec, row],
        out_specs=[row, row, row, per, vec, pl.BlockSpec((8, 128), lambda b, j: (0, 0))],
        out_shape=[jax.ShapeDtypeStruct((t, D), BF16), jax.ShapeDtypeStruct((t, D), F32), jax.ShapeDtypeStruct((t, D), BF16),
                   jax.ShapeDtypeStruct((n_samples, 1, D), F32), jax.ShapeDtypeStruct((1, D), F32),
                   jax.ShapeDtypeStruct((8, 128), F32)],
        compiler_params=_params(("arbitrary", "arbitrary")),
    )(p5, p5, y_conv, y_gla, w_out, x2, gate, final_g, target)


def out_dgrad_merge_bwd(p5, y_conv, y_gla, dmo, w_out, merged):
    t = y_conv.shape[0]

    def body(mc_ref, mg_ref, yc_ref, yg_ref, d_ref, w_ref, mrg_ref, dyc_ref, dyg_ref, dp_ref, gw_ref):
        f = lambda ref: ref[...].astype(F32)

        @pl.when(pl.program_id(0) == 0)
        def _():
            gw_ref[...] = jnp.zeros_like(gw_ref)

        gw_ref[...] += _tn(mrg_ref[...], d_ref[...])
        d = _nt(d_ref[...], w_ref[...])
        sc = _sigmoid(f(mc_ref))
        sg = _sigmoid(f(mg_ref))
        dyc_ref[...] = (d * sc).astype(dyc_ref.dtype)
        dyg_ref[...] = (d * sg).astype(dyg_ref.dtype)
        dp_ref[:, pl.ds(0, D)] = (d * f(yc_ref) * sc * (1.0 - sc)).astype(dp_ref.dtype)
        dp_ref[:, pl.ds(D, D)] = (d * f(yg_ref) * sg * (1.0 - sg)).astype(dp_ref.dtype)

    row = pl.BlockSpec((TM_OUT, D), lambda i: (i, 0))
    return _pallas(
        body, name="out_dgrad_merge_bwd", grid=(t // TM_OUT,),
        in_specs=[row, pl.BlockSpec((TM_OUT, D), lambda i: (i, 1)), row, row, row, pl.BlockSpec((D, D), lambda i: (0, 0)),
                  row],
        out_specs=[row, row, pl.BlockSpec((TM_OUT, 2 * D), lambda i: (i, 0)), pl.BlockSpec((D, D), lambda i: (0, 0))],
        out_shape=[jax.ShapeDtypeStruct((t, D), BF16), jax.ShapeDtypeStruct((t, D), BF16),
                   jax.ShapeDtypeStruct((t, 2 * D), BF16), jax.ShapeDtypeStruct((D, D), F32)],
        compiler_params=_params(("arbitrary",)),
    )(p5, p5, y_conv, y_gla, dmo, w_out, merged)


def local_step(x, ctx, target, mod, wts, small, p_sh, chip, core):
    n = x.shape[0]
    t = n * SEQ
    t_all = t + n * NCTX
    x2 = x.reshape(t, D)
    ctx2 = ctx.reshape(n * NCTX, D)
    tgt2 = target.reshape(t, D)
    scale1, shift, gate = mod

    u = norm_mod_fwd(x2, ctx2, scale1, shift, small["norm_g"])
    p1, p2, p3, p4, p5, p_all = proj_all(u, [wts["w%d" % i] for i in range(1, 6)], [small["b%d" % i] for i in range(1, 6)],
                                         [t, t, t_all, t, t], p_sh, tm=512)
    p_full = jnp.stack([jnp.where(chip == i, p_sh, p_all[i]) for i in range(N_CHIPS)])
    wts = dict(wts, conv_proj=p_full[:, 0:256].reshape(D, D), gla_proj=p_full[:, 256:512].reshape(D, D),
               w_out=p_full[:, 512:768].reshape(D, D))

    aconv = conv_fwd(p1, small["conv_w"], small["conv_b"], n)
    ac, y_conv = ln_gate_proj(aconv, p2, small["conv_ln_g"], small["conv_ln_b"], wts["conv_proj"])

    qs, ks, vs, cum_f, cum_b = gla_prep_fwd(p3, small["upf"], small["upb"], small["bias_f"], small["bias_b"], n)
    o_f, s_f, sfin_f = gla_scan_fwd(qs, ks, vs, cum_f, rev=False, name="gla_scan_fwd_f")
    o_b, s_b, sfin_b = gla_scan_fwd(qs, ks, vs, cum_b, rev=True, name="gla_scan_fwd_b")
    og, y_gla = gla_out_proj(o_f, o_b, p4, small["gla_norm_g"], wts["gla_proj"])

    merged, dh, dmo, dgate, d_final_g, loss = merge_out_final(p5, y_conv, y_gla, wts["w_out"], x2, gate,
                                                              small["final_norm_g"], tgt2, n)

    g = {"final_norm_g": d_final_g}
    dyc, dyg, dp5, g["w_out"] = out_dgrad_merge_bwd(p5, y_conv, y_gla, dmo, wts["w_out"], merged)

    daconv, dp2, g["conv_ln_g"], g["conv_ln_b"], g["conv_proj"] = ln_gate_bwd(
        aconv, p2, dyc, wts["conv_proj"], small["conv_ln_g"], small["conv_ln_b"], ac)
    dp1, dconv_w, dconv_b = conv_bwd(p1, daconv, small["conv_w"], n)
    g["conv_w"], g["conv_b"] = dconv_w, dconv_b

    do, dp4, g["gla_norm_g"], g["gla_proj"] = gla_out_bwd(o_f, o_b, p4, dyg, wts["gla_proj"], small["gla_norm_g"], og)
    g["proj"] = jnp.concatenate([g["conv_proj"].reshape(N_CHIPS, 256, D), g["gla_proj"].reshape(N_CHIPS, 256, D),
                                 g["w_out"].reshape(N_CHIPS, 256, D)], 1)
    dq_f, dk_f, dv_f, dc_f, gotp = gla_scan_bwd(qs, ks, vs, cum_f, s_f, sfin_f, do, rev=False, name="gla_scan_bwd_f",
                                                rider=("swap", g["proj"]))
    pap16 = pair_add(core, g["proj"], gotp, name="pair_add_p", tr=384)
    dq_b, dk_b, dv_b, dc_b, rbp = gla_scan_bwd(qs, ks, vs, cum_b, s_b, sfin_b, do, rev=True, name="gla_scan_bwd_b",
                                               rider=("exchange", pap16))
    dp3, g["upf"], g["upb"], g["bias_f"], g["bias_b"] = gla_prep_bwd(
        p3, dq_f, dq_b, dk_f, dk_b, dv_f, dv_b, dc_f, dc_b,
        small["upf"], small["upb"], small["bias_f"], small["bias_b"], n)

    dps = [dp1, dp2, dp3, dp4, dp5]
    early = [g["conv_b"].sum(0), g["conv_ln_g"], g["conv_ln_b"], g["bias_f"], g["bias_b"], g["gla_norm_g"],
             g["final_norm_g"], g["conv_w"].sum(0)[:CONV_K], g["upf"][0:16], g["upb"][16:32]]
    got = {}
    for i in [0, 1, 3, 4, 2]:
        dp = dps[i]
        rows = dp.shape[0]
        tn = W3 if dp.shape[1] == W3 else 1024
        others = [j for j in range(5) if j != i]
        swap = ([g["w%d" % (j + 1)] for j in others], _pack(early)) if i == 2 else None
        outs = matmul_tn(u, dp, name="w_in_wgrad_%d" % (i + 1), t=rows, tn=tn, tt=1024 if rows % 1024 == 0 else 768,
                         colsum=True, swap=swap)
        g["w%d" % (i + 1)], g["b%d" % (i + 1)] = outs[0], outs[1]
        if swap is not None:
            got = dict(zip(others, outs[2:2 + len(others)]))
            sall_early = outs[2 + len(others)]
    g["gate"] = dgate
    return loss, dh, dps, g, got, (pap16, rbp), (sall_early, [a.shape for a in early])


def _group_cols(w):
    gv, gg, z = w[..., 0:1024], w[..., 1024:2048], w[..., 2048:3072]
    q, k, v = w[..., 3072:3584], w[..., 3584:4096], w[..., 4096:5120]
    ab = w[..., 5120:5152]
    r, mc, mg = w[..., 5152:6176], w[..., 6176:7200], w[..., 7200:8224]
    g1 = jnp.concatenate([p for j in range(CONV_NCB)
                          for p in (gv[..., CONV_CB * j:CONV_CB * (j + 1)], gg[..., CONV_CB * j:CONV_CB * (j + 1)])], -1)
    pad = jnp.zeros(w.shape[:-1] + (W3 - 2080,), w.dtype)
    g3 = jnp.concatenate([v, q, k, ab, pad], -1)
    return g1, z, g3, r, jnp.concatenate([mc, mg], -1)


def _ungroup_cols(g1, g2, g3, g4, g5):
    gv = jnp.concatenate([g1[..., 2 * CONV_CB * j:2 * CONV_CB * j + CONV_CB] for j in range(CONV_NCB)], -1)
    gg = jnp.concatenate([g1[..., 2 * CONV_CB * j + CONV_CB:2 * CONV_CB * (j + 1)] for j in range(CONV_NCB)], -1)
    v, q, k, ab = g3[..., 0:1024], g3[..., 1024:1536], g3[..., 1536:2048], g3[..., 2048:2080]
    return jnp.concatenate([gv, gg, g2, q, k, v, ab, g4, g5[..., 0:1024], g5[..., 1024:2048]], -1)


def _natural_pieces():
    pieces = [(CONV_CB * j, CONV_CB, 0, 2 * CONV_CB * j) for j in range(CONV_NCB)]
    pieces += [(1024 + CONV_CB * j, CONV_CB, 0, 2 * CONV_CB * j + CONV_CB) for j in range(CONV_NCB)]
    pieces += [(2048, 1024, 1, 0), (3072, 512, 2, O3_Q), (3584, 512, 2, O3_K), (4096, 1024, 2, O3_V), (5120, 32, 2, O3_AB),
               (5152, 1024, 3, 0), (6176, 1024, 4, 0), (7200, 1024, 4, 1024)]
    return sorted(pieces)


def _ungroup_to_shards(groups):
    shards = []
    for i in range(N_CHIPS):
        lo, hi = i * W_IN_SHARD, (i + 1) * W_IN_SHARD
        parts = []
        for nat, width, g, gcol in _natural_pieces():
            a, b = max(nat, lo), min(nat + width, hi)
            if a < b:
                parts.append(groups[g][:, gcol + a - nat:gcol + b - nat])
        shards.append(jnp.concatenate(parts, 1))
    return jnp.stack(shards)


def _pad_up(up, row0):
    return jnp.zeros((128, GLA_DK), F32).at[row0:row0 + up.shape[0]].set(up)


def _adamw_math(w, g, m, v):
    m = ADAM_B1 * m + (1.0 - ADAM_B1) * g
    v = ADAM_B2 * v + (1.0 - ADAM_B2) * (g * g)
    m_hat = m / (1.0 - ADAM_B1 ** ADAM_STEP)
    v_hat = v / (1.0 - ADAM_B2 ** ADAM_STEP)
    delta = -ADAM_LR * (m_hat / (jnp.sqrt(v_hat) + ADAM_EPS) + ADAM_WD * w)
    return delta, m, v


def adamw2d(w, g, m, v, *, name, tr, tcols=None):
    rows, cols = w.shape[-2:]

    def body(w_ref, g_ref, m_ref, v_ref, d_ref, nm_ref, nv_ref):
        d_ref[...], nm_ref[...], nv_ref[...] = _adamw_math(w_ref[...], g_ref[...], m_ref[...], v_ref[...])

    tcols = cols if tcols is None else tcols
    if w.ndim == 3:
        spec = pl.BlockSpec((1, tr, tcols), lambda i, j: (0, i, j))
    else:
        spec = pl.BlockSpec((tr, tcols), lambda i, j: (i, j))
    return _pallas(
        body, name=name, grid=(rows // tr, cols // tcols), in_specs=[spec] * 4, out_specs=[spec] * 3,
        out_shape=[jax.ShapeDtypeStruct(w.shape, F32)] * 3, compiler_params=_params(("parallel", "parallel")),
    )(w, g, m, v)


def adamw_many(ws, gs, ms, vs):
    k = len(ws)
    two = lambda a: a.reshape((-1, a.shape[-1]))

    def body(*refs):
        w_refs, g_refs, m_refs, v_refs = refs[:k], refs[k:2 * k], refs[2 * k:3 * k], refs[3 * k:4 * k]
        d_refs, nm_refs, nv_refs = refs[4 * k:5 * k], refs[5 * k:6 * k], refs[6 * k:7 * k]
        for i in range(k):
            d_refs[i][...], nm_refs[i][...], nv_refs[i][...] = _adamw_math(
                w_refs[i][...], g_refs[i][...], m_refs[i][...], v_refs[i][...])

    shapes = [jax.ShapeDtypeStruct(two(a).shape, F32) for a in ws]
    outs = _pallas(body, name="adamw_small", out_shape=shapes * 3, compiler_params=_params())(
        *[two(a) for a in ws], *[two(a) for a in gs], *[two(a) for a in ms], *[two(a) for a in vs])
    back = lambda lst: [o.reshape(a.shape) for o, a in zip(lst, ws)]
    return back(outs[:k]), back(outs[k:2 * k]), back(outs[2 * k:])


def sum_devices(salls):
    k = len(salls)

    def body(*refs):
        for i in range(k):
            acc = refs[i][0]
            for d in range(1, N_DEV):
                acc = acc + refs[i][d]
            refs[k + i][...] = acc

    return _pallas(body, name="sum_devices", out_shape=[jax.ShapeDtypeStruct(a.shape[1:], F32) for a in salls],
                   compiler_params=_params())(*salls)


def pair_add(core, g, got, *, name, tr):
    n, rows, cols = got.shape
    g4 = g.reshape(n, 2, rows, cols)

    def body(core_ref, g_ref, got_ref, ob_ref):
        del core_ref
        ob_ref[0] = (g_ref[0, 0] + got_ref[0]).astype(BF16)

    spec = pl.BlockSpec((1, tr, cols), lambda i, t, core_ref: (i, t, 0))
    return _pallas(
        body, name=name,
        grid_spec=pltpu.PrefetchScalarGridSpec(
            num_scalar_prefetch=1, grid=(n, rows // tr),
            in_specs=[pl.BlockSpec((1, 1, tr, cols), lambda i, t, core_ref: (i, core_ref[0], t, 0)), spec],
            out_specs=spec),
        out_shape=jax.ShapeDtypeStruct(got.shape, BF16),
        compiler_params=_params(("parallel", "parallel")))(core, g4, got)


def pair_add_groups(core, gs, gots, *, tr, swap=None):
    k = len(gs)
    rows = gots[0].shape[0]
    n_sw = 0 if swap is None else len(swap)
    n_steps = rows // tr

    def body(core_ref, *refs):
        del core_ref
        for i in range(k):
            refs[2 * k + n_sw + i][...] = (refs[i][0] + refs[k + i][...]).astype(BF16)
        if swap is not None:
            g_refs = refs[2 * k:2 * k + n_sw]
            got_refs = refs[3 * k + n_sw:3 * k + 2 * n_sw]
            sems = refs[3 * k + 2 * n_sw:]
            t = pl.program_id(0)

            @pl.when(t == 0)
            def _():
                for cp in _pair_copies(g_refs, None, got_refs, None, *sems):
                    cp.start()

            @pl.when(t == n_steps - 1)
            def _():
                for cp in _pair_copies(g_refs, None, got_refs, None, *sems):
                    cp.wait_recv()
                for cp in _pair_copies(g_refs, None, got_refs, None, *sems):
                    cp.wait_send()

    g_specs = [pl.BlockSpec((1, tr, a.shape[1]), lambda t, core_ref: (core_ref[0], t, 0)) for a in gots]
    r_specs = [pl.BlockSpec((tr, a.shape[1]), lambda t, core_ref: (t, 0)) for a in gots]
    any_spec = pl.BlockSpec(memory_space=pl.ANY)
    out_shape = [jax.ShapeDtypeStruct(a.shape, BF16) for a in gots]
    scratch, extra = [], []
    if swap is not None:
        out_shape += _pair_got_shapes(swap, None)
        extra = list(swap)
        scratch = [pltpu.SemaphoreType.DMA((_pair_count(swap, None),)), pltpu.SemaphoreType.DMA((_pair_count(swap, None),))]
    return _pallas(
        body, name="pair_add_w",
        grid_spec=pltpu.PrefetchScalarGridSpec(num_scalar_prefetch=1, grid=(n_steps,),
                                               in_specs=g_specs + r_specs + [any_spec] * n_sw,
                                               out_specs=r_specs + [any_spec] * n_sw, scratch_shapes=scratch),
        out_shape=out_shape,
        compiler_params=_params(("parallel" if swap is None else "arbitrary",)),
    )(core, *[a.reshape(2, rows, a.shape[1]) for a in gs], *gots, *extra)


def chip_add(place, pas, rbs, *, steps):
    k = len(pas)

    def body(place_ref, *refs):
        del place_ref
        for i in range(k):
            m_ref, r_ref, o_ref = refs[i], refs[k + i], refs[2 * k + i]
            o_ref[0] = ((m_ref[0].astype(F32) + r_ref[0].astype(F32)) + r_ref[1].astype(F32)) + r_ref[2].astype(F32)

    trs = [pa.shape[1] // steps for pa in pas]
    return _pallas(
        body, name="chip_add",
        grid_spec=pltpu.PrefetchScalarGridSpec(
            num_scalar_prefetch=1, grid=(steps,),
            in_specs=[pl.BlockSpec((1, tr, pa.shape[2]), lambda t, place_ref: (place_ref[0], t, 0)) for pa, tr in zip(pas, trs)]
            + [pl.BlockSpec((3, tr, pa.shape[2]), lambda t, place_ref: (0, t, 0)) for pa, tr in zip(pas, trs)],
            out_specs=[pl.BlockSpec((1, tr, pa.shape[2]), lambda t, place_ref: (place_ref[1], t, 0))
                       for pa, tr in zip(pas, trs)]),
        out_shape=[jax.ShapeDtypeStruct((2,) + pa.shape[1:], F32) for pa in pas],
        compiler_params=_params(("parallel",)))(place, *pas, *rbs)


def ada_bwd(call, cctx_rows, dm_shard, dm_full, adaw):
    nsh = adaw.shape[1]

    def body(c_ref, cc_ref, dms_ref, dmf_ref, w_ref, gw_ref, gb_ref, pq_ref):
        a_lat = _silu(c_ref[...])
        a_ctx = _silu(cc_ref[...])
        dms = dms_ref[...]
        gw_ref[...] = _tn(a_lat, dms[0:64], HI) + _tn(a_ctx, dms[64:72], HI)
        gb_ref[...] = jnp.sum(dmf_ref[...], axis=0, keepdims=True)
        part = _nt(dms[64:72], w_ref[...], HI)
        pq_ref[...] = jnp.zeros_like(pq_ref) + jnp.sum(part, axis=0, keepdims=True)

    return _pallas(body, name="ada_bwd",
                   out_shape=[jax.ShapeDtypeStruct((D, nsh), F32), jax.ShapeDtypeStruct((1, 3 * D), F32),
                              jax.ShapeDtypeStruct((8, D), F32)],
                   compiler_params=_params())(call, cctx_rows, dm_shard, dm_full, adaw)


def cctx_grad(pq_all, cctx_rows):
    def body(p_ref, c_ref, o_ref):
        acc = p_ref[0]
        for qi in range(1, N_CHIPS):
            acc = acc + p_ref[qi]
        o_ref[...] = acc * _dsilu(c_ref[...])

    return _pallas(body, name="cctx_grad", out_shape=jax.ShapeDtypeStruct((8, D), F32),
                   compiler_params=_params())(pq_all, cctx_rows)


def _place():
    x, y, c = lax.axis_index("x"), lax.axis_index("y"), lax.axis_index("c")
    chips = [(1 - x, y), (x, 1 - y), (1 - x, 1 - y)]
    return x, y, c, chips


def _all_peers(x, y, c):
    return [((1 - x) if r & 4 else x, (1 - y) if r & 2 else y, (1 - c) if r & 1 else c) for r in range(1, N_DEV)]


def _remote(src, dst, send_sem, recv_sem, dev):
    return pltpu.make_async_remote_copy(src_ref=src, dst_ref=dst, send_sem=send_sem, recv_sem=recv_sem,
                                        device_id=dev, device_id_type=MESH)


ANY = pl.BlockSpec(memory_space=pl.ANY)
VMEM = pl.BlockSpec(memory_space=pltpu.VMEM)
F_ROWS = 16


W_ROW_CHUNKS = 4
P_ROW_CHUNKS = 2
N_BULK = W_ROW_CHUNKS + P_ROW_CHUNKS


def _half_chunks(core, n_rows, align, which=(0, 1)):
    out = []
    for a, k in ((0, W_ROW_CHUNKS), (1, P_ROW_CHUNKS)):
        if a not in which:
            continue
        half = n_rows[a] // 2
        size = half // k
        for i in range(k):
            start = core * half + i * size
            out.append((a, pl.ds(start if isinstance(start, int) else pl.multiple_of(start, align), size)))
    return out


def gather_weights(c8, cctx8, adaw, adab, w_sh, fp):
    nsh = adaw.shape[1]

    def body(c_ref, cctx_ref, adaw_ref, adab_ref, w_ref, fp_ref, wall_ref, fall_ref, call_ref, mall_ref,
             abuf, w_send, w_recv, h_send, h_recv, c_send, c_recv, m_send, m_recv, f_send, f_recv):
        x, y, c, chips = _place()
        q = 2 * x + y
        dev = 4 * x + 2 * y + c
        qs = [2 * cx + cy for cx, cy in chips]
        sib = (x, y, 1 - c)
        srcs, dsts = (w_ref,), (wall_ref,)
        n_rows = (w_ref.shape[0],)
        mine = _half_chunks(c, n_rows, 16, which=(0,))
        other = _half_chunks(1 - c, n_rows, 16, which=(0,))

        bulk = [[_remote(srcs[a].at[rows], dsts[a].at[q, rows], w_send.at[j * N_BULK + i], w_recv.at[j * N_BULK + i],
                         (*chips[j], c)) for i, (a, rows) in enumerate(mine)] for j in range(3)]
        fall_ref[q] = fp_ref[...]
        small = [_remote(fp_ref, fall_ref.at[q], f_send.at[j], f_recv.at[j], (*chips[j], c)) for j in range(3)]
        my_rows = pl.ds(pl.multiple_of(8 * dev, 8), 8)
        call_ref[my_rows, :] = c_ref[...]
        cond = [_remote(c_ref, call_ref.at[my_rows, :], c_send.at[r], c_recv.at[r], peer)
                for r, peer in enumerate(_all_peers(x, y, c))]
        for cp in sum(bulk, []) + small + cond:
            cp.start()
        for cp in cond:
            cp.wait_recv()

        abuf[pl.ds(0, 64), :] = _silu(call_ref[...])
        abuf[pl.ds(64, 8), :] = _silu(cctx_ref[...])
        mall_ref[q] = _nn(abuf[...], adaw_ref[...], HI) + adab_ref[...]
        mod = [_remote(mall_ref.at[q], mall_ref.at[q], m_send.at[j], m_recv.at[j], (*chips[j], c)) for j in range(3)]
        for cp in mod:
            cp.start()

        handed = []
        for j in range(3):
            for i, (a, rows) in enumerate(mine):
                bulk[j][i].wait_recv()
                cp = _remote(dsts[a].at[qs[j], rows], dsts[a].at[qs[j], rows],
                             h_send.at[j * N_BULK + i], h_recv.at[j * N_BULK + i], sib)
                cp.start()
                handed.append(cp)
        for j in range(3):
            for i, (a, rows) in enumerate(other):
                _remote(dsts[a].at[qs[j], rows], dsts[a].at[qs[j], rows],
                        h_send.at[j * N_BULK + i], h_recv.at[j * N_BULK + i], sib).wait_recv()
        for cp in mod + small:
            cp.wait_recv()
        for cp in sum(bulk, []) + small + cond + mod + handed:
            cp.wait_send()

    def dma(n):
        return pltpu.SemaphoreType.DMA((n,))

    return _pallas(
        body, name="gather_weights",
        in_specs=[VMEM, VMEM, VMEM, VMEM, ANY, VMEM],
        out_specs=[ANY, VMEM, VMEM, VMEM],
        out_shape=[jax.ShapeDtypeStruct((N_CHIPS,) + w_sh.shape, BF16),
                   jax.ShapeDtypeStruct((N_CHIPS, F_ROWS, D), F32),
                   jax.ShapeDtypeStruct((8 * N_DEV, D), F32), jax.ShapeDtypeStruct((N_CHIPS, MOD_ROWS, nsh), F32)],
        scratch_shapes=[pltpu.VMEM((MOD_ROWS, D), F32), dma(3 * N_BULK), dma(3 * N_BULK), dma(3 * N_BULK), dma(3 * N_BULK),
                        dma(7), dma(7), dma(3), dma(3), dma(3), dma(3)],
        compiler_params=_params(),
    )(c8, cctx8, adaw, adab, w_sh, fp)


def _pair_count(gs, gp):
    return len(gs) * W_ROW_CHUNKS + (0 if gp is None else N_CHIPS * P_ROW_CHUNKS)


def _pair_got_shapes(gs, gp):
    shapes = [jax.ShapeDtypeStruct((D // 2, a.shape[1]), F32) for a in gs]
    if gp is not None:
        shapes.append(jax.ShapeDtypeStruct((N_CHIPS, gp.shape[1] // 2, gp.shape[2]), F32))
    return shapes


def _pair_copies(g_refs, gp_ref, got_refs, gotp_ref, a_send, a_recv):
    x, y, c, _ = _place()
    sib = (x, y, 1 - c)
    pair = []
    half, size = D // 2, D // 2 // W_ROW_CHUNKS
    for gi in range(len(g_refs)):
        for i in range(W_ROW_CHUNKS):
            k = len(pair)
            rows_o = pl.ds(pl.multiple_of((1 - c) * half + i * size, 8), size)
            pair.append(_remote(g_refs[gi].at[rows_o], got_refs[gi].at[pl.ds(i * size, size)],
                                a_send.at[k], a_recv.at[k], sib))
    if gp_ref is not None:
        half, size = gp_ref.shape[1] // 2, gp_ref.shape[1] // 2 // P_ROW_CHUNKS
        for s in range(N_CHIPS):
            for i in range(P_ROW_CHUNKS):
                k = len(pair)
                rows_o = pl.ds(pl.multiple_of((1 - c) * half + i * size, 8), size)
                pair.append(_remote(gp_ref.at[s, rows_o], gotp_ref.at[s, pl.ds(i * size, size)],
                                    a_send.at[k], a_recv.at[k], sib))
    return pair


def gather_small(sm):
    rows = sm.shape[0]

    def body(sm_ref, sall_ref, s_send, s_recv):
        x, y, c, _ = _place()
        dev = 4 * x + 2 * y + c
        sall_ref[dev] = sm_ref[...]
        small = [_remote(sm_ref, sall_ref.at[dev], s_send.at[r], s_recv.at[r], peer)
                 for r, peer in enumerate(_all_peers(x, y, c))]
        for cp in small:
            cp.start()
        for cp in small:
            cp.wait_recv()
        for cp in small:
            cp.wait_send()

    return _pallas(
        body, name="gather_small", in_specs=[VMEM], out_specs=VMEM,
        out_shape=jax.ShapeDtypeStruct((N_DEV, rows, D), F32),
        scratch_shapes=[pltpu.SemaphoreType.DMA((7,)), pltpu.SemaphoreType.DMA((7,))],
        compiler_params=_params(),
    )(sm)


def pair_share(ghw, ghp, pq):
    def body(ghw_ref, ghp_ref, pq_ref, outw_ref, outp_ref, pqa_ref, send, recv, p_send, p_recv):
        del ghw_ref, ghp_ref
        x, y, c, chips = _place()
        q = 2 * x + y
        refs = (outw_ref, outp_ref)
        n_rows = (2 * outw_ref.shape[1], 2 * outp_ref.shape[1])
        pair = [_remote(refs[a].at[c, rows], refs[a].at[c, rows], send.at[i], recv.at[i], (x, y, 1 - c))
                for i, (a, rows) in enumerate(_half_chunks(0, n_rows, 8))]
        pqa_ref[q] = pq_ref[...]
        small = [_remote(pq_ref, pqa_ref.at[q], p_send.at[j], p_recv.at[j], (*chips[j], c)) for j in range(3)]
        for cp in pair + small:
            cp.start()
        for i, (a, rows) in enumerate(_half_chunks(0, n_rows, 8)):
            _remote(refs[a].at[1 - c, rows], refs[a].at[1 - c, rows], send.at[i], recv.at[i], (x, y, 1 - c)).wait_recv()
        for cp in small:
            cp.wait_recv()
        for cp in pair + small:
            cp.wait_send()

    return _pallas(
        body, name="pair_share", in_specs=[ANY, ANY, VMEM], out_specs=[ANY, ANY, VMEM],
        out_shape=[jax.ShapeDtypeStruct(ghw.shape, F32), jax.ShapeDtypeStruct(ghp.shape, F32),
                   jax.ShapeDtypeStruct((N_CHIPS, 8, D), F32)],
        scratch_shapes=[pltpu.SemaphoreType.DMA((N_BULK,)), pltpu.SemaphoreType.DMA((N_BULK,)),
                        pltpu.SemaphoreType.DMA((3,)), pltpu.SemaphoreType.DMA((3,))],
        input_output_aliases={0: 0, 1: 1},
        compiler_params=_params(),
    )(ghw, ghp, pq)


def _rows_of(shape):
    size = 1
    for s in shape:
        size *= s
    return -(-size // D)


def _pack(arrs, rows_multiple=8):
    parts = []
    total = 0
    for a in arrs:
        f = a.reshape(-1).astype(F32)
        r = _rows_of(a.shape)
        parts.append(jnp.pad(f, (0, r * D - f.shape[0])))
        total += r
    pad_rows = (-total) % rows_multiple
    if pad_rows:
        parts.append(jnp.zeros((pad_rows * D,), F32))
    return jnp.concatenate(parts).reshape(-1, D)


def _unpack(p, shapes):
    out = []
    r0 = 0
    for shp in shapes:
        r = _rows_of(shp)
        size = 1
        for s in shp:
            size *= s
        out.append(p[r0:r0 + r].reshape(-1)[:size].reshape(shp))
        r0 += r
    return out


WEIGHT_NAMES = ['c_ctx', 'ada_w', 'ada_b', 'norm_g', 'w_in', 'b_in', 'conv_w', 'conv_b', 'conv_ln_g', 'conv_ln_b',
                'conv_proj', 'decay_up_fwd', 'decay_bias_fwd', 'decay_up_bwd', 'decay_bias_bwd', 'gla_norm_g', 'gla_proj',
                'w_out', 'final_norm_g']
SMALL_NAMES = ['c_ctx', 'ada_b', 'norm_g', 'b_in', 'conv_w', 'conv_b', 'conv_ln_g', 'conv_ln_b', 'decay_up_fwd',
               'decay_bias_fwd', 'decay_up_bwd', 'decay_bias_bwd', 'gla_norm_g', 'final_norm_g']


def kernel(x, c, ctx, c_ctx, ada_w, ada_b, norm_g, w_in, b_in, conv_w, conv_b, conv_ln_g, conv_ln_b, conv_proj, decay_up_fwd, decay_bias_fwd, decay_up_bwd, decay_bias_bwd, gla_norm_g, gla_proj, w_out, final_norm_g, loss_target, m_c_ctx, m_ada_w, m_ada_b, m_norm_g, m_w_in, m_b_in, m_conv_w, m_conv_b, m_conv_ln_g, m_conv_ln_b, m_conv_proj, m_decay_up_fwd, m_decay_bias_fwd, m_decay_up_bwd, m_decay_bias_bwd, m_gla_norm_g, m_gla_proj, m_w_out, m_final_norm_g, v_c_ctx, v_ada_w, v_ada_b, v_norm_g, v_w_in, v_b_in, v_conv_w, v_conv_b, v_conv_ln_g, v_conv_ln_b, v_conv_proj, v_decay_up_fwd, v_decay_bias_fwd, v_decay_up_bwd, v_decay_bias_bwd, v_gla_norm_g, v_gla_proj, v_w_out, v_final_norm_g):
    w = dict(c_ctx=c_ctx, ada_w=ada_w, ada_b=ada_b, norm_g=norm_g, w_in=w_in, b_in=b_in, conv_w=conv_w, conv_b=conv_b,
             conv_ln_g=conv_ln_g, conv_ln_b=conv_ln_b, conv_proj=conv_proj, decay_up_fwd=decay_up_fwd,
             decay_bias_fwd=decay_bias_fwd, decay_up_bwd=decay_up_bwd, decay_bias_bwd=decay_bias_bwd,
             gla_norm_g=gla_norm_g, gla_proj=gla_proj, w_out=w_out, final_norm_g=final_norm_g)
    m = dict(c_ctx=m_c_ctx, ada_w=m_ada_w, ada_b=m_ada_b, norm_g=m_norm_g, w_in=m_w_in, b_in=m_b_in, conv_w=m_conv_w,
             conv_b=m_conv_b, conv_ln_g=m_conv_ln_g, conv_ln_b=m_conv_ln_b, conv_proj=m_conv_proj,
             decay_up_fwd=m_decay_up_fwd, decay_bias_fwd=m_decay_bias_fwd, decay_up_bwd=m_decay_up_bwd,
             decay_bias_bwd=m_decay_bias_bwd, gla_norm_g=m_gla_norm_g, gla_proj=m_gla_proj, w_out=m_w_out,
             final_norm_g=m_final_norm_g)
    v = dict(c_ctx=v_c_ctx, ada_w=v_ada_w, ada_b=v_ada_b, norm_g=v_norm_g, w_in=v_w_in, b_in=v_b_in, conv_w=v_conv_w,
             conv_b=v_conv_b, conv_ln_g=v_conv_ln_g, conv_ln_b=v_conv_ln_b, conv_proj=v_conv_proj,
             decay_up_fwd=v_decay_up_fwd, decay_bias_fwd=v_decay_bias_fwd, decay_up_bwd=v_decay_up_bwd,
             decay_bias_bwd=v_decay_bias_bwd, gla_norm_g=v_gla_norm_g, gla_proj=v_gla_proj, w_out=v_w_out,
             final_norm_g=v_final_norm_g)
    n = x.shape[0]
    ax, ay, ac = lax.axis_index("x"), lax.axis_index("y"), lax.axis_index("c")
    q = 2 * ax + ay
    dev = 4 * ax + 2 * ay + ac
    nsh = ada_w.shape[2]

    w_sh = w_in[0].astype(BF16)
    p_sh = jnp.concatenate([conv_proj[0], gla_proj[0], w_out[0]], 0).astype(BF16)
    fp = _pack([conv_w[0], decay_up_fwd[0], decay_up_bwd[0]], F_ROWS)
    c8 = jnp.pad(c, ((0, 8 - n), (0, 0)))
    cctx8 = jnp.pad(c_ctx[None], ((0, 7), (0, 0)))
    adab_sh = lax.dynamic_slice(ada_b, (0, q * nsh), (1, nsh))
    w_all, fall, call, mall = gather_weights(c8, cctx8, ada_w[0], adab_sh, w_sh, fp)

    mod_all = jnp.transpose(mall, (1, 0, 2)).reshape(MOD_ROWS, 3 * D)
    mod_mine = lax.dynamic_slice(mod_all, (8 * dev, 0), (n, 3 * D))
    mod_ctx = mod_all[64:65]
    shift = jnp.concatenate([mod_mine[:, 0:D], mod_ctx[:, 0:D]], 0)[:, None, :]
    scale1 = 1.0 + jnp.concatenate([mod_mine[:, D:2 * D], mod_ctx[:, D:2 * D]], 0)[:, None, :]
    gate = mod_mine[:, 2 * D:3 * D][:, None, :]

    own = lambda i, mine, got: jnp.where(q == i, mine, got)
    g1, g2, g3, g4, g5 = _group_cols(jnp.concatenate([own(i, w_sh, w_all[i]) for i in range(N_CHIPS)], 1))
    wts = dict(w1=g1, w2=g2, w3=g3, w4=g4, w5=g5)
    f_parts = [_unpack(fall[i], [conv_w.shape[1:], decay_up_fwd.shape[1:], decay_up_bwd.shape[1:]]) for i in range(N_CHIPS)]
    conv_w_full = jnp.concatenate([p[0] for p in f_parts], 1)
    upf_full = jnp.concatenate([p[1] for p in f_parts], 1)
    upb_full = jnp.concatenate([p[2] for p in f_parts], 1)
    b1, b2, b3, b4, b5 = _group_cols(b_in)
    small = dict(b1=b1, b2=b2, b3=b3, b4=b4, b5=b5, norm_g=norm_g,
                 conv_w=jnp.pad(conv_w_full, ((0, 1), (0, 0))), conv_b=conv_b, conv_ln_g=conv_ln_g, conv_ln_b=conv_ln_b,
                 upf=_split3(_pad_up(upf_full, 0)), upb=_split3(_pad_up(upb_full, 16)),
                 bias_f=decay_bias_fwd, bias_b=decay_bias_bwd,
                 gla_norm_g=gla_norm_g, final_norm_g=final_norm_g[None])

    core = ac.astype(jnp.int32).reshape(1)
    chip = q.astype(jnp.int32).reshape(1)
    loss_part, dh, dps, g, got, (pap16, rbp), (sall1, early_shapes) = local_step(
        x, ctx, loss_target, (scale1, shift, gate), wts, small, p_sh, q, core)

    gs = [g["w%d" % i] for i in range(1, 6)]
    done = [0, 1, 3, 4]
    *sums, got[2] = pair_add_groups(core, [gs[i] for i in done], [got[i] for i in done], tr=128, swap=[gs[2]])
    halves = dict(zip(done, sums))
    halves[2] = pair_add(core, gs[2][None], got[2][None], name="pair_add_w3", tr=128)[0]
    halves = [halves[i] for i in range(5)]
    paw16 = _ungroup_to_shards(halves)
    grad_x2, dshift, dscale, g["norm_g"], rbw = dgrad_norm_bwd(
        dps, [wts["w%d" % i] for i in range(1, 6)], paw16, x.reshape(n * SEQ, D), ctx.reshape(n * NCTX, D), dh,
        scale1, norm_g, tm=256)

    dm_mine = jnp.concatenate([dshift[:n, 0], dscale[:n, 0], g["gate"][:, 0]], -1)
    dm_ctx = jnp.concatenate([dshift[n, 0], dscale[n, 0], jnp.zeros((D,), F32)], -1)
    d_b_in = _ungroup_cols(*[g["b%d" % i] for i in range(1, 6)])
    late = [g["norm_g"], dm_mine, dm_ctx, d_b_in, loss_part[0, 0:1]]
    late_shapes = [a.shape for a in late]
    sall2 = gather_small(_pack(late))
    sum_early, sum_late = sum_devices([sall1, sall2])
    (s_conv_b, s_ln_g, s_ln_b, s_bias_f, s_bias_b, s_gla_g, s_final_g, s_conv_w, s_upf,
     s_upb) = _unpack(sum_early, early_shapes)
    s_late = _unpack(sum_late, late_shapes)
    s_norm_g, s_b_in, loss = s_late[0], s_late[3], s_late[4][0]
    r_mine, r_ctx = 1, 1 + 3 * n
    dm_all = sall2[:, r_mine:r_ctx].reshape(N_DEV, n, 3 * D)
    dm_full = jnp.concatenate([jnp.pad(dm_all, ((0, 0), (0, 8 - n), (0, 0))).reshape(8 * N_DEV, 3 * D),
                               sall2[:, r_ctx:r_ctx + 3].reshape(N_DEV, 3 * D)], 0)
    dm_shard = lax.dynamic_slice(dm_full, (0, q * nsh), (MOD_ROWS, nsh))
    cctx_rows = jnp.broadcast_to(c_ctx[None], (8, D))
    g_ada_w, g_ada_b, pq = ada_bwd(call, cctx_rows, dm_shard, dm_full, ada_w[0])

    place = jnp.concatenate([chip, core])
    ghw, ghp = chip_add(place, [paw16, pap16], [rbw, rbp], steps=4)
    gw_mine, gp_mine, pq_all = pair_share(ghw, ghp, pq)
    gp_mine = gp_mine.reshape(768, D)
    g_c_ctx = cctx_grad(pq_all, cctx_rows)[0]

    grads = dict(
        c_ctx=g_c_ctx, ada_w=g_ada_w[None], ada_b=g_ada_b, norm_g=s_norm_g,
        w_in=gw_mine.reshape(1, D, W_IN_SHARD), b_in=s_b_in,
        conv_w=lax.dynamic_slice(s_conv_w, (0, q * 256), (CONV_K, 256))[None], conv_b=s_conv_b,
        conv_ln_g=s_ln_g, conv_ln_b=s_ln_b, conv_proj=gp_mine[0:256][None],
        decay_up_fwd=lax.dynamic_slice(s_upf, (0, q * 128), (16, 128))[None], decay_bias_fwd=s_bias_f,
        decay_up_bwd=lax.dynamic_slice(s_upb, (0, q * 128), (16, 128))[None], decay_bias_bwd=s_bias_b,
        gla_norm_g=s_gla_g, gla_proj=gp_mine[256:512][None], w_out=gp_mine[512:768][None],
        final_norm_g=s_final_g[0])

    delta, new_m, new_v = {}, {}, {}
    delta["ada_w"], new_m["ada_w"], new_v["ada_w"] = adamw2d(w["ada_w"], grads["ada_w"], m["ada_w"], v["ada_w"],
                                                             name="adamw_ada_w", tr=128)
    tr_ = lambda a: jnp.swapaxes(a, 1, 2)
    g_w_in_t = tr_(grads["w_in"])
    grads["w_in"] = tr_(g_w_in_t)
    d_, m_, v_ = adamw2d(tr_(w_in), g_w_in_t, tr_(m_w_in), tr_(v_w_in), name="adamw_w_in", tr=W_IN_SHARD, tcols=128)
    delta["w_in"], new_m["w_in"], new_v["w_in"] = tr_(d_), tr_(m_), tr_(v_)
    rest = SMALL_NAMES + ["conv_proj", "gla_proj", "w_out"]
    d_, m_, v_ = adamw_many([w[nm] for nm in rest], [grads[nm].reshape(w[nm].shape) for nm in rest],
                            [m[nm] for nm in rest], [v[nm] for nm in rest])
    for nm, a, b, cc in zip(rest, d_, m_, v_):
        delta[nm], new_m[nm], new_v[nm] = a, b, cc

    grad_x = grad_x2.reshape(x.shape)
    return (loss, grad_x, *[grads[nm].reshape(w[nm].shape) for nm in WEIGHT_NAMES], *[delta[nm] for nm in WEIGHT_NAMES],
            *[new_m[nm] for nm in WEIGHT_NAMES], *[new_v[nm] for nm in WEIGHT_NAMES])
```

```python
import jax
import jax.numpy as jnp
from jax import lax
from jax.experimental import pallas as pl
from jax.experimental.pallas import tpu as pltpu

F32 = jnp.float32
BF16 = jnp.bfloat16
MESH = pl.DeviceIdType.MESH
HI = lax.Precision.HIGHEST

D = 1024
SEQ = 2048
GRID_W = 64
GRID_H = SEQ // GRID_W
NCTX = 256
SEQ_ALL = SEQ + NCTX
EPS = 1e-6
CONV_K = 31
CONV_PAD = CONV_K // 2
HEADS = 4
HEAD_K = 128
HEAD_V = 256
GLA_DK = HEADS * HEAD_K
GATE_TAU = 16.0
Q_SCALE = HEAD_K ** -0.5
CHUNK = 64
NCHUNK = SEQ_ALL // CHUNK
NCHUNK_LAT = SEQ // CHUNK
NCHUNK_CTX = NCHUNK - NCHUNK_LAT
SUB = 64
NSUB = CHUNK // SUB
N_IN = 8224
W3 = 2176
O3_V, O3_Q, O3_K, O3_AB = 0, 1024, 1536, 2048

ADAM_LR, ADAM_B1, ADAM_B2, ADAM_EPS, ADAM_WD, ADAM_STEP = 0.001, 0.9, 0.999, 1e-08, 0.01, 10
VMEM_LIMIT = 56 * 1024 * 1024

N_CHIPS = 4
N_DEV = 8
W_IN_SHARD = N_IN // N_CHIPS
MOD_ROWS = 72


def _pallas(body, **kw):
    return pl.pallas_call(body, **kw)


def _params(sem=None, **kw):
    if sem is not None:
        kw["dimension_semantics"] = sem
    return pltpu.CompilerParams(vmem_limit_bytes=VMEM_LIMIT, **kw)


def _sigmoid(v):
    return 1.0 / (1.0 + jnp.exp(-v))


def _silu(v):
    return v * _sigmoid(v)


def _dsilu(v):
    s = _sigmoid(v)
    return s * (1.0 + v * (1.0 - s))


def _log_sigmoid(v):
    return jnp.minimum(v, 0.0) - jnp.log(1.0 + jnp.exp(-jnp.abs(v)))


def _dot(a, b, dims, precision=None):
    return lax.dot_general(a, b, (dims, ((), ())), preferred_element_type=F32, precision=precision)


def _nn(a, b, precision=None):
    return _dot(a, b, ((1,), (0,)), precision)


def _nt(a, b, precision=None):
    return _dot(a, b, ((1,), (1,)), precision)


def _tn(a, b, precision=None):
    return _dot(a, b, ((0,), (0,)), precision)


def _b16(v):
    return v.astype(BF16)


def proj_all(u, ws, bs, rows, p_sh, *, tm):
    k = u.shape[1]
    n_g = len(ws)
    tns = [w.shape[1] if w.shape[1] % 1024 else 1024 for w in ws]
    mts = [r // tm for r in rows]
    cnts = [(w.shape[1] // tn) * mt for w, tn, mt in zip(ws, tns, mts)]
    los = [sum(cnts[:g]) for g in range(n_g)]
    n_steps = sum(cnts)

    def rel(s, g):
        return jnp.clip(s - los[g], 0, cnts[g] - 1)

    def active(s, g):
        return (s >= los[g]) & (s < los[g] + cnts[g])

    def u_row(s):
        r = 0
        for g in range(n_g):
            r = r + jnp.where(active(s, g), rel(s, g) % mts[g], 0)
        return r

    def body(*refs):
        u_ref = refs[0]
        w_refs, b_refs = refs[1:1 + n_g], refs[1 + n_g:1 + 2 * n_g]
        p_ref = refs[1 + 2 * n_g]
        o_refs = refs[2 + 2 * n_g:2 + 3 * n_g]
        pall_ref = refs[2 + 3 * n_g]
        w_send, w_recv, h_send, h_recv = refs[3 + 3 * n_g:]
        s = pl.program_id(0)
        for g in range(n_g):
            @pl.when(active(s, g))
            def _(g=g):
                o_refs[g][...] = (_nn(u_ref[...], w_refs[g][...]) + b_refs[g][...]).astype(o_refs[g].dtype)

        x, y, c, chips = _place()
        q = 2 * x + y
        mine = _half_chunks(c, (0, p_ref.shape[0]), 16, which=(1,))
        other = _half_chunks(1 - c, (0, p_ref.shape[0]), 16, which=(1,))
        nb = len(mine)

        def bulk():
            return [[_remote(p_ref.at[rws], pall_ref.at[q, rws], w_send.at[pj * nb + pi], w_recv.at[pj * nb + pi],
                             (*chips[pj], c)) for pi, (_, rws) in enumerate(mine)] for pj in range(3)]

        @pl.when(s == 0)
        def _():
            for cp in sum(bulk(), []):
                cp.start()

        @pl.when(s == n_steps - 1)
        def _():
            handed = []
            for pj, (cx, cy) in enumerate(chips):
                for pi, (_, rws) in enumerate(mine):
                    bulk()[pj][pi].wait_recv()
                    cp = _remote(pall_ref.at[2 * cx + cy, rws], pall_ref.at[2 * cx + cy, rws],
                                 h_send.at[pj * nb + pi], h_recv.at[pj * nb + pi], (x, y, 1 - c))
                    cp.start()
                    handed.append(cp)
            for pj, (cx, cy) in enumerate(chips):
                for pi, (_, rws) in enumerate(other):
                    _remote(pall_ref.at[2 * cx + cy, rws], pall_ref.at[2 * cx + cy, rws],
                            h_send.at[pj * nb + pi], h_recv.at[pj * nb + pi], (x, y, 1 - c)).wait_recv()
            for cp in sum(bulk(), []) + handed:
                cp.wait_send()

    any_spec = pl.BlockSpec(memory_space=pl.ANY)
    in_specs = [pl.BlockSpec((tm, k), lambda s: (u_row(s), 0))]
    in_specs += [pl.BlockSpec((k, tns[g]), lambda s, g=g: (0, rel(s, g) // mts[g])) for g in range(n_g)]
    in_specs += [pl.BlockSpec((1, tns[g]), lambda s, g=g: (0, rel(s, g) // mts[g])) for g in range(n_g)]
    in_specs.append(any_spec)
    out_specs = [pl.BlockSpec((tm, tns[g]), lambda s, g=g: (rel(s, g) % mts[g], rel(s, g) // mts[g])) for g in range(n_g)]
    out_specs.append(any_spec)
    out_shape = [jax.ShapeDtypeStruct((rows[g], ws[g].shape[1]), BF16) for g in range(n_g)]
    out_shape.append(jax.ShapeDtypeStruct((N_CHIPS,) + p_sh.shape, p_sh.dtype))
    return _pallas(
        body, name="proj_all", grid=(n_steps,), in_specs=in_specs, out_specs=out_specs, out_shape=out_shape,
        scratch_shapes=[pltpu.SemaphoreType.DMA((3 * P_ROW_CHUNKS,)) for _ in range(4)],
        compiler_params=_params(("arbitrary",)),
    )(u, *ws, *bs, p_sh)


def matmul_tn(a, b, *, name, t, tn, tt, colsum=False, swap=None):
    m = a.shape[1]
    n = b.shape[1]
    nj, ns = n // tn, t // tt
    n_out = 2 if colsum else 1
    n_sw = 0 if swap is None else len(swap[0])
    n_ex = 0 if swap is None else n_sw + 1

    def body(a_ref, b_ref, *rest):
        o_ref = rest[n_ex]
        cs_ref = rest[n_ex + 1] if colsum else None
        j, s = pl.program_id(0), pl.program_id(1)

        if swap is not None:
            g_refs, sm_ref = rest[:n_sw], rest[n_sw]
            got_refs, sall_ref = rest[n_ex + n_out:n_ex + n_out + n_sw], rest[n_ex + n_out + n_sw]
            a_send, a_recv, s_send, s_recv, l_sem = rest[2 * n_ex + n_out:]

            def copies():
                x, y, c, _ = _place()
                dev = 4 * x + 2 * y + c
                small = [_remote(sm_ref, sall_ref.at[dev], s_send.at[r], s_recv.at[r], peer)
                         for r, peer in enumerate(_all_peers(x, y, c))]
                return (_pair_copies(g_refs, None, got_refs, None, a_send, a_recv) + small,
                        pltpu.make_async_copy(sm_ref, sall_ref.at[dev], l_sem))

            @pl.when((j == 0) & (s == 0))
            def _():
                remote, own = copies()
                own.start()
                for cp in remote:
                    cp.start()

            @pl.when((j == nj - 1) & (s == ns - 1))
            def _():
                remote, own = copies()
                for cp in remote:
                    cp.wait_recv()
                for cp in remote:
                    cp.wait_send()
                own.wait()

        @pl.when(s == 0)
        def _():
            o_ref[...] = jnp.zeros_like(o_ref)
            if colsum:
                cs_ref[...] = jnp.zeros_like(cs_ref)
        o_ref[...] += _tn(a_ref[...], b_ref[...])
        if colsum:
            cs_ref[...] += jnp.sum(b_ref[...].astype(F32), axis=0, keepdims=True)

    in_specs = [pl.BlockSpec((tt, m), lambda j, s: (s, 0)), pl.BlockSpec((tt, tn), lambda j, s: (s, j))]
    out_specs = [pl.BlockSpec((m, tn), lambda j, s: (0, j))]
    out_shape = [jax.ShapeDtypeStruct((m, n), F32)]
    if colsum:
        out_specs.append(pl.BlockSpec((1, tn), lambda j, s: (0, j)))
        out_shape.append(jax.ShapeDtypeStruct((1, n), F32))
    args, scratch = [a, b], []
    if swap is not None:
        gs, sm = swap
        any_spec = pl.BlockSpec(memory_space=pl.ANY)
        in_specs += [any_spec] * n_ex
        out_specs += [any_spec] * n_ex
        out_shape += _pair_got_shapes(gs, None) + [jax.ShapeDtypeStruct((N_DEV,) + sm.shape, F32)]
        args += [*gs, sm]
        scratch = [pltpu.SemaphoreType.DMA((_pair_count(gs, None),)), pltpu.SemaphoreType.DMA((_pair_count(gs, None),)),
                   pltpu.SemaphoreType.DMA((N_DEV - 1,)), pltpu.SemaphoreType.DMA((N_DEV - 1,)),
                   pltpu.SemaphoreType.DMA(())]
    return _pallas(
        body, name=name, grid=(nj, ns), in_specs=in_specs, out_specs=out_specs, out_shape=out_shape,
        scratch_shapes=scratch,
        compiler_params=_params(("parallel" if swap is None else "arbitrary", "arbitrary")),
    )(*args)


def dgrad_norm_bwd(dps, wts, paw, x2, ctx2, dh, scale1, norm_g, *, tm):
    t, tc = x2.shape[0], ctx2.shape[0]
    t_all = t + tc
    n_lat, n_ctx = t // tm, tc // tm
    n_tiles = n_lat + n_ctx
    n_samples = scale1.shape[0] - 1
    tps = n_lat // n_samples
    n_grp = n_samples + 1
    n_g = len(dps)
    whole = [g for g in range(n_g) if dps[g].shape[0] == t_all]
    latent = [g for g in range(n_g) if dps[g].shape[0] != t_all]

    def body(*refs):
        dp_refs, w_refs = refs[:n_g], refs[n_g:2 * n_g]
        (paw_ref, x_ref, c_ref, dh_ref, sc_ref, g_ref, dx_ref, dsh_ref, dsc_ref, dg_ref, rbw_ref,
         du_buf, b_send, b_recv) = refs[2 * n_g:]
        i = pl.program_id(0)

        def exchange():
            x, y, c, chips = _place()
            chunks = _half_chunks(0, (2 * paw_ref.shape[1],), 16, which=(0,))
            return [_remote(paw_ref.at[2 * cx + cy, rows], rbw_ref.at[j, rows],
                            b_send.at[j * N_BULK + k], b_recv.at[j * N_BULK + k], (cx, cy, c))
                    for j, (cx, cy) in enumerate(chips) for k, (_, rows) in enumerate(chunks)]

        @pl.when(i == 0)
        def _():
            for cp in exchange():
                cp.start()

        acc = None
        for g in whole:
            part = _nt(dp_refs[g][...], w_refs[g][...])
            acc = part if acc is None else acc + part
        du_buf[...] = acc

        @pl.when(i < n_lat)
        def _():
            lat = None
            for g in latent:
                part = _nt(dp_refs[g][...], w_refs[g][...])
                lat = part if lat is None else lat + part
            du_buf[...] += lat

        duv = du_buf[...]
        xv = jnp.where(i < n_lat, x_ref[...], c_ref[...])
        rs = lax.rsqrt(jnp.mean(xv * xv, axis=-1, keepdims=True) + EPS)
        xh = xv * rs
        n = xh * g_ref[...]
        dn = duv * sc_ref[0]
        dxh = dn * g_ref[...]
        dx = rs * (dxh - xh * jnp.mean(dxh * xh, axis=-1, keepdims=True))

        @pl.when(i < n_lat)
        def _():
            dx_ref[...] = dx + dh_ref[...]

        @pl.when((i % tps == 0) & (i <= n_lat))
        def _():
            dsh_ref[...] = jnp.zeros_like(dsh_ref)
            dsc_ref[...] = jnp.zeros_like(dsc_ref)

        @pl.when(i == 0)
        def _():
            dg_ref[...] = jnp.zeros_like(dg_ref)

        dsh_ref[0] += jnp.sum(duv, axis=0, keepdims=True)
        dsc_ref[0] += jnp.sum(duv * n, axis=0, keepdims=True)
        dg_ref[...] += jnp.sum(dn * xh, axis=0, keepdims=True)

        @pl.when(i == n_tiles - 1)
        def _():
            for cp in exchange():
                cp.wait_recv()
            for cp in exchange():
                cp.wait_send()

    lat = lambda i: (jnp.minimum(i, n_lat - 1), 0)
    grp = lambda i: (jnp.minimum(i // tps, n_samples), 0, 0)
    in_specs = []
    for g, dp in enumerate(dps):
        nrow = dp.shape[0] // tm
        in_specs.append(pl.BlockSpec((tm, dp.shape[1]), lambda i, nrow=nrow: (jnp.minimum(i, nrow - 1), 0)))
    for w in wts:
        in_specs.append(pl.BlockSpec(w.shape, lambda i: (0, 0), pipeline_mode=pl.Buffered(1)))
    any_spec = pl.BlockSpec(memory_space=pl.ANY)
    in_specs += [any_spec,
                 pl.BlockSpec((tm, D), lat), pl.BlockSpec((tm, D), lambda i: (jnp.maximum(i - n_lat, 0), 0)),
                 pl.BlockSpec((tm, D), lat), pl.BlockSpec((1, 1, D), grp), pl.BlockSpec((1, D), lambda i: (0, 0))]
    return _pallas(
        body, name="dgrad_norm_bwd", grid=(n_tiles,), in_specs=in_specs,
        out_specs=[pl.BlockSpec((tm, D), lat), pl.BlockSpec((1, 1, D), grp), pl.BlockSpec((1, 1, D), grp),
                   pl.BlockSpec((1, D), lambda i: (0, 0)), any_spec],
        out_shape=[jax.ShapeDtypeStruct((t, D), F32), jax.ShapeDtypeStruct((n_grp, 1, D), F32),
                   jax.ShapeDtypeStruct((n_grp, 1, D), F32), jax.ShapeDtypeStruct((1, D), F32),
                   jax.ShapeDtypeStruct((3,) + paw.shape[1:], paw.dtype)],
        scratch_shapes=[pltpu.VMEM((tm, D), F32), pltpu.SemaphoreType.DMA((3 * N_BULK,)),
                        pltpu.SemaphoreType.DMA((3 * N_BULK,))],
        compiler_params=_params(("arbitrary",)),
    )(*dps, *wts, paw, x2, ctx2, dh, scale1, norm_g)


TM_NORM = 512


def norm_mod_fwd(x2, ctx2, scale1, shift, norm_g):
    t = x2.shape[0]
    n_lat = t // TM_NORM
    assert ctx2.shape[0] == TM_NORM
    n_samples = scale1.shape[0] - 1
    tps = n_lat // n_samples

    def body(x_ref, c_ref, sc_ref, sh_ref, g_ref, u_ref):
        i = pl.program_id(0)
        xv = jnp.where(i < n_lat, x_ref[...], c_ref[...])
        rs = lax.rsqrt(jnp.mean(xv * xv, axis=-1, keepdims=True) + EPS)
        u = xv * rs * g_ref[...] * sc_ref[0] + sh_ref[0]
        u_ref[...] = u.astype(u_ref.dtype)

    grp = lambda i: (jnp.minimum(i // tps, n_samples), 0, 0)
    return _pallas(
        body, name="norm_mod_fwd", grid=(n_lat + 1,),
        in_specs=[pl.BlockSpec((TM_NORM, D), lambda i: (jnp.minimum(i, n_lat - 1), 0)),
                  pl.BlockSpec((TM_NORM, D), lambda i: (0, 0)),
                  pl.BlockSpec((1, 1, D), grp), pl.BlockSpec((1, 1, D), grp),
                  pl.BlockSpec((1, D), lambda i: (0, 0))],
        out_specs=pl.BlockSpec((TM_NORM, D), lambda i: (i, 0)),
        out_shape=jax.ShapeDtypeStruct((t + TM_NORM, D), BF16),
        compiler_params=_params(("parallel",)),
    )(x2, ctx2, scale1, shift, norm_g)


CONV_CB = 256
CONV_NCB = D // CONV_CB
H_OFF = 16


H_CB = 128
H_SPAN = GRID_W + 2 * H_OFF - 8


def _conv_scratch(vertical):
    if vertical:
        return [pltpu.VMEM((GRID_H + 2 * CONV_PAD, GRID_W, CONV_CB), F32)]
    return [pltpu.VMEM((GRID_H, GRID_W + 2 * H_OFF, H_CB), F32), pltpu.VMEM((7, GRID_H, H_SPAN, H_CB), F32)]


def _conv_fill(bufs, img, vertical):
    pad_ref = bufs[0]
    pad_ref[...] = jnp.zeros_like(pad_ref)
    if vertical:
        pad_ref[pl.ds(CONV_PAD, GRID_H)] = img
        return
    pad_ref[:, pl.ds(H_OFF, GRID_W), :] = img

    def shift(r, carry):
        for s in range(1, 8):
            bufs[1][s - 1, r] = pad_ref[r, pl.ds(s, H_SPAN), :]
        return carry

    lax.fori_loop(0, GRID_H, shift, 0)


def _conv_window(bufs, k, vertical, r, w0=0, nw=GRID_W, lanes=slice(None)):
    if vertical:
        return bufs[0][r + k, pl.ds(w0, nw), lanes]
    off = H_OFF - CONV_PAD + k
    if off % 8 == 0:
        return bufs[0][r, pl.ds(off + w0, nw), lanes]
    return bufs[1][off % 8 - 1, r, pl.ds(off - off % 8 + w0, nw), lanes]


def _conv_col_blocks(vertical):
    if vertical:
        return [pl.ds(0, CONV_CB)]
    return [pl.ds(i * H_CB, H_CB) for i in range(CONV_CB // H_CB)]


def _rows(r):
    return pl.ds(pl.multiple_of(r * GRID_W, GRID_W), GRID_W)


def conv_fwd(p1, conv_w, conv_b, n_samples):
    t = n_samples * SEQ

    def make(vertical, prev):
        n_buf = len(_conv_scratch(vertical))

        def body(gv_ref, gg_ref, w_ref, b_ref, *rest):
            o_ref, bufs = rest[-1 - n_buf], rest[-n_buf:]
            for cols in _conv_col_blocks(vertical):
                a = gv_ref[:, cols].astype(F32) * _sigmoid(gg_ref[:, cols].astype(F32))
                _conv_fill(bufs, a.reshape(GRID_H, GRID_W, a.shape[-1]), vertical)

                def row(r, carry, cols=cols):
                    acc = jnp.zeros((GRID_W, cols.size), F32) + b_ref[:, cols]
                    for k in range(CONV_K):
                        acc = acc + _conv_window(bufs, k, vertical, r) * w_ref[pl.ds(k, 1), cols]
                    o_ref[_rows(r), cols] = acc
                    return carry

                lax.fori_loop(0, GRID_H, row, 0)

        cb0 = CONV_NCB // 2 if vertical else 0
        in_specs = [pl.BlockSpec((SEQ, CONV_CB), lambda b, j: (b, 2 * (cb0 + j))),
                    pl.BlockSpec((SEQ, CONV_CB), lambda b, j: (b, 2 * (cb0 + j) + 1)),
                    pl.BlockSpec((CONV_K + 1, CONV_CB), lambda b, j: (0, cb0 + j)),
                    pl.BlockSpec((1, CONV_CB), lambda b, j: (0, cb0 + j))]
        args = [p1, p1, conv_w, conv_b]
        aliases = {}
        if prev is not None:
            in_specs.append(pl.BlockSpec(memory_space=pl.ANY))
            args.append(prev)
            aliases = {4: 0}
        return _pallas(
            body, name="conv_fwd_v" if vertical else "conv_fwd_h", grid=(n_samples, CONV_NCB // 2),
            in_specs=in_specs,
            out_specs=pl.BlockSpec((SEQ, CONV_CB), lambda b, j: (b, cb0 + j)),
            out_shape=jax.ShapeDtypeStruct((t, D), F32),
            scratch_shapes=_conv_scratch(vertical),
            input_output_aliases=aliases,
            compiler_params=_params(("parallel", "parallel")),
        )(*args)

    return make(True, make(False, None))


def conv_bwd(p1, daconv, conv_w, n_samples):
    t = n_samples * SEQ

    def make(vertical, prev):
        n_buf = len(_conv_scratch(vertical))

        def body(gv_ref, gg_ref, dy_ref, w_ref, *rest):
            dp_ref, dw_ref, db_ref = rest[-3 - 2 * n_buf - 1:-2 * n_buf - 1]
            a_bufs, d_bufs, da_ref = rest[-2 * n_buf - 1:-n_buf - 1], rest[-n_buf - 1:-1], rest[-1]
            for cols in _conv_col_blocks(vertical):
                width = cols.size
                gv = gv_ref[:, cols].astype(F32)
                sg = _sigmoid(gg_ref[:, cols].astype(F32))
                _conv_fill(a_bufs, (gv * sg).reshape(GRID_H, GRID_W, width), vertical)
                _conv_fill(d_bufs, dy_ref[:, cols].reshape(GRID_H, GRID_W, width), vertical)

                def row(r, carry, cols=cols, width=width):
                    acc = jnp.zeros((GRID_W, width), F32)
                    for k in range(CONV_K):
                        acc = acc + _conv_window(d_bufs, CONV_K - 1 - k, vertical, r) * w_ref[pl.ds(k, 1), cols]
                    da_ref[_rows(r), cols] = acc
                    return carry

                lax.fori_loop(0, GRID_H, row, 0)
                da = da_ref[:, cols]
                dp_ref[:, pl.ds(cols.start, width)] = (da * sg).astype(dp_ref.dtype)
                dp_ref[:, pl.ds(CONV_CB + cols.start, width)] = (da * gv * sg * (1.0 - sg)).astype(dp_ref.dtype)

                for lb in range(width // 128):
                    lanes = pl.ds(lb * 128, 128)
                    dy_lanes = pl.ds(cols.start + lb * 128, 128)

                    def wrow(r, accs, lanes=lanes, dy_lanes=dy_lanes):
                        for w0 in range(0, GRID_W, 8):
                            dyv = dy_ref[pl.ds(pl.multiple_of(r * GRID_W, GRID_W) + w0, 8), dy_lanes]
                            accs = tuple(accs[k] + _conv_window(a_bufs, k, vertical, r, w0, 8, lanes) * dyv
                                         for k in range(CONV_K))
                        return accs

                    accs = lax.fori_loop(0, GRID_H, wrow, tuple(jnp.zeros((8, 128), F32) for _ in range(CONV_K)))
                    for k in range(CONV_K):
                        dw_ref[0, pl.ds(k, 1), dy_lanes] = jnp.sum(accs[k], axis=0, keepdims=True)
            dw_ref[0, pl.ds(CONV_K, 1), :] = jnp.zeros((1, CONV_CB), F32)
            db_ref[0] = jnp.sum(dy_ref[...], axis=0, keepdims=True)

        cb0 = CONV_NCB // 2 if vertical else 0
        in_specs = [pl.BlockSpec((SEQ, CONV_CB), lambda b, j: (b, 2 * (cb0 + j))),
                    pl.BlockSpec((SEQ, CONV_CB), lambda b, j: (b, 2 * (cb0 + j) + 1)),
                    pl.BlockSpec((SEQ, CONV_CB), lambda b, j: (b, cb0 + j)),
                    pl.BlockSpec((CONV_K + 1, CONV_CB), lambda b, j: (0, cb0 + j))]
        args = [p1, p1, daconv, conv_w]
        aliases = {}
        if prev is not None:
            in_specs += [pl.BlockSpec(memory_space=pl.ANY)] * 3
            args += list(prev)
            aliases = {4: 0, 5: 1, 6: 2}
        return _pallas(
            body, name="conv_bwd_v" if vertical else "conv_bwd_h", grid=(n_samples, CONV_NCB // 2),
            in_specs=in_specs,
            out_specs=[pl.BlockSpec((SEQ, 2 * CONV_CB), lambda b, j: (b, cb0 + j)),
                       pl.BlockSpec((1, CONV_K + 1, CONV_CB), lambda b, j: (b, 0, cb0 + j)),
                       pl.BlockSpec((1, 1, CONV_CB), lambda b, j: (b, 0, cb0 + j))],
            out_shape=[jax.ShapeDtypeStruct((t, 2 * D), BF16),
                       jax.ShapeDtypeStruct((n_samples, CONV_K + 1, D), F32),
                       jax.ShapeDtypeStruct((n_samples, 1, D), F32)],
            scratch_shapes=_conv_scratch(vertical) + _conv_scratch(vertical) + [pltpu.VMEM((SEQ, CONV_CB), F32)],
            input_output_aliases=aliases,
            compiler_params=_params(("parallel", "parallel")),
        )(*args)

    return make(True, make(False, None))


TM_EW = 512


def ln_gate_proj(aconv, z, ln_g, ln_b, conv_proj):
    t = aconv.shape[0]

    def body(a_ref, z_ref, g_ref, b_ref, w_ref, o_ref, y_ref):
        a = a_ref[...]
        mu = jnp.mean(a, axis=-1, keepdims=True)
        xc = a - mu
        rstd = lax.rsqrt(jnp.mean(xc * xc, axis=-1, keepdims=True) + EPS)
        l = xc * rstd * g_ref[...] + b_ref[...]
        ac = _b16(_silu(l) * _silu(z_ref[...].astype(F32)))
        o_ref[...] = ac
        y_ref[...] = _nn(ac, w_ref[...]).astype(y_ref.dtype)

    row = pl.BlockSpec((TM_OUT, D), lambda i: (i, 0))
    vec = pl.BlockSpec((1, D), lambda i: (0, 0))
    return _pallas(
        body, name="ln_gate_proj", grid=(t // TM_OUT,),
        in_specs=[row, row, vec, vec, pl.BlockSpec((D, D), lambda i: (0, 0))], out_specs=[row, row],
        out_shape=[jax.ShapeDtypeStruct((t, D), BF16), jax.ShapeDtypeStruct((t, D), BF16)],
        compiler_params=_params(("parallel",)),
    )(aconv, z, ln_g, ln_b, conv_proj)


def ln_gate_bwd(aconv, z, dyc, conv_proj, ln_g, ln_b, ac):
    t = aconv.shape[0]

    def body(a_ref, z_ref, d_ref, w_ref, g_ref, b_ref, ac_ref, da_ref, dz_ref, dg_ref, db_ref, gw_ref):
        @pl.when(pl.program_id(0) == 0)
        def _():
            gw_ref[...] = jnp.zeros_like(gw_ref)

        gw_ref[...] += _tn(ac_ref[...], d_ref[...])
        a = a_ref[...]
        zv = z_ref[...].astype(F32)
        dac_v = _nt(d_ref[...], w_ref[...])
        mu = jnp.mean(a, axis=-1, keepdims=True)
        xc = a - mu
        rstd = lax.rsqrt(jnp.mean(xc * xc, axis=-1, keepdims=True) + EPS)
        xh = xc * rstd
        l = xh * g_ref[...] + b_ref[...]
        dz_ref[...] = (dac_v * _silu(l) * _dsilu(zv)).astype(dz_ref.dtype)
        dl = dac_v * _silu(zv) * _dsilu(l)
        dxh = dl * g_ref[...]
        da_ref[...] = rstd * (dxh - jnp.mean(dxh, axis=-1, keepdims=True)
                              - xh * jnp.mean(dxh * xh, axis=-1, keepdims=True))

        @pl.when(pl.program_id(0) == 0)
        def _():
            dg_ref[...] = jnp.zeros_like(dg_ref)
            db_ref[...] = jnp.zeros_like(db_ref)

        dg_ref[...] += jnp.sum(dl * xh, axis=0, keepdims=True)
        db_ref[...] += jnp.sum(dl, axis=0, keepdims=True)

    row = pl.BlockSpec((TM_EW, D), lambda i: (i, 0))
    vec = pl.BlockSpec((1, D), lambda i: (0, 0))
    return _pallas(
        body, name="ln_gate_bwd", grid=(t // TM_EW,),
        in_specs=[row, row, row, pl.BlockSpec((D, D), lambda i: (0, 0)), vec, vec, row],
        out_specs=[row, row, vec, vec, pl.BlockSpec((D, D), lambda i: (0, 0))],
        out_shape=[jax.ShapeDtypeStruct((t, D), F32), jax.ShapeDtypeStruct((t, D), BF16),
                   jax.ShapeDtypeStruct((1, D), F32), jax.ShapeDtypeStruct((1, D), F32),
                   jax.ShapeDtypeStruct((D, D), F32)],
        compiler_params=_params(("arbitrary",)),
    )(aconv, z, dyc, conv_proj, ln_g, ln_b, ac)


TM_PREP = 256
PREP_LAT = SEQ // TM_PREP
PREP_ALL = SEQ_ALL // TM_PREP


def _chunk_tri(n, upper):
    r = lax.broadcasted_iota(jnp.int32, (n, n), 0)
    c = lax.broadcasted_iota(jnp.int32, (n, n), 1)
    same = (r // CHUNK) == (c // CHUNK)
    keep = (c >= r) if upper else (c <= r)
    return jnp.where(same & keep, 1.0, 0.0).astype(F32)


def _split3(v):
    hi = v.astype(BF16)
    r1 = v - hi.astype(F32)
    mid = r1.astype(BF16)
    lo = (r1 - mid.astype(F32)).astype(BF16)
    return jnp.stack([hi, mid, lo])


def _chunk_sums(v, upper):
    tri = _chunk_tri(v.shape[0], upper).astype(BF16)
    pieces = _split3(v)
    return (_nn(tri, pieces[0]) + _nn(tri, pieces[1])) + _nn(tri, pieces[2])


def _gate_logits(ab, up3_ref, bias_ref):
    assert ab.dtype == BF16
    return ((_nn(ab, up3_ref[0]) + _nn(ab, up3_ref[1])) + _nn(ab, up3_ref[2])) + bias_ref[...]


def _prep_tile_maps(n_samples):
    n_lat = n_samples * PREP_LAT

    def seq_map(i):
        return jnp.where(i < n_lat, i // PREP_LAT, i - n_lat), jnp.where(i < n_lat, i % PREP_LAT, PREP_LAT)

    return n_lat, seq_map


def gla_prep_fwd(p3, upf, upb, bias_f, bias_b, n_samples):
    n_lat, seq_map = _prep_tile_maps(n_samples)
    n_tiles = n_lat + n_samples

    def body(v_ref, q_ref, k_ref, ab_ref, upf_ref, upb_ref, bf_ref, bb_ref, qo, ko, vo, cf, cb):
        i = pl.program_id(0)
        qo[0] = jnp.where(i < n_lat, q_ref[...].astype(F32) * Q_SCALE, 0.0)
        ko[0] = k_ref[...]
        vo[0] = v_ref[...]
        ab = ab_ref[...]
        gf = _log_sigmoid(_gate_logits(ab, upf_ref, bf_ref)) * (1.0 / GATE_TAU)
        gb = _log_sigmoid(_gate_logits(ab, upb_ref, bb_ref)) * (1.0 / GATE_TAU)
        cf[0] = _chunk_sums(gf, False)
        cb[0] = _chunk_sums(gb, True)

    def o_spec(w):
        return pl.BlockSpec((1, TM_PREP, w), lambda i: (*seq_map(i), 0))

    full = lambda shape: pl.BlockSpec(shape, lambda i: (0,) * len(shape))
    return _pallas(
        body, name="gla_prep_fwd", grid=(n_tiles,),
        in_specs=[pl.BlockSpec((TM_PREP, 1024), lambda i: (i, O3_V // 1024)),
                  pl.BlockSpec((TM_PREP, 512), lambda i: (i, O3_Q // 512)),
                  pl.BlockSpec((TM_PREP, 512), lambda i: (i, O3_K // 512)),
                  pl.BlockSpec((TM_PREP, 128), lambda i: (i, O3_AB // 128)),
                  full((3, 128, GLA_DK)), full((3, 128, GLA_DK)), full((1, GLA_DK)), full((1, GLA_DK))],
        out_specs=[o_spec(GLA_DK), o_spec(GLA_DK), o_spec(D), o_spec(GLA_DK), o_spec(GLA_DK)],
        out_shape=[jax.ShapeDtypeStruct((n_samples, SEQ_ALL, GLA_DK), F32),
                   jax.ShapeDtypeStruct((n_samples, SEQ_ALL, GLA_DK), p3.dtype),
                   jax.ShapeDtypeStruct((n_samples, SEQ_ALL, D), p3.dtype),
                   jax.ShapeDtypeStruct((n_samples, SEQ_ALL, GLA_DK), F32),
                   jax.ShapeDtypeStruct((n_samples, SEQ_ALL, GLA_DK), F32)],
        compiler_params=_params(("parallel",)),
    )(p3, p3, p3, p3, upf, upb, bias_f, bias_b)


def gla_prep_bwd(p3, dq_f, dq_b, dk_f, dk_b, dv_f, dv_b, dc_f, dc_b, upf, upb, bias_f, bias_b, n_samples):
    n_lat, seq_map = _prep_tile_maps(n_samples)
    n_tiles = n_lat + n_samples

    def body(ab_ref, dqf, dqb, dkf, dkb, dvf, dvb, dcf, dcb, upf_ref, upb_ref, bf_ref, bb_ref,
             dp_ref, duf_ref, dub_ref, dbf_ref, dbb_ref):
        i = pl.program_id(0)
        both = lambda a, b: a[0].astype(F32) + b[0].astype(F32)
        dp_ref[:, pl.ds(O3_V, D)] = both(dvf, dvb).astype(dp_ref.dtype)
        dq = jnp.where(i < n_lat, both(dqf, dqb) * Q_SCALE, 0.0)
        dp_ref[:, pl.ds(O3_Q, GLA_DK)] = dq.astype(dp_ref.dtype)
        dp_ref[:, pl.ds(O3_K, GLA_DK)] = both(dkf, dkb).astype(dp_ref.dtype)
        ab = ab_ref[...]
        zf = _gate_logits(ab, upf_ref, bf_ref)
        zb = _gate_logits(ab, upb_ref, bb_ref)
        dgf = _chunk_sums(dcf[0], True)
        dgb = _chunk_sums(dcb[0], False)
        dzf = _b16(dgf * (1.0 / GATE_TAU) * _sigmoid(-zf))
        dzb = _b16(dgb * (1.0 / GATE_TAU) * _sigmoid(-zb))
        dab = _nt(dzf, upf_ref[0]) + _nt(dzb, upb_ref[0])
        dp_ref[:, pl.ds(O3_AB, 128)] = dab.astype(dp_ref.dtype)

        @pl.when(i == 0)
        def _():
            duf_ref[...] = jnp.zeros_like(duf_ref)
            dub_ref[...] = jnp.zeros_like(dub_ref)
            dbf_ref[...] = jnp.zeros_like(dbf_ref)
            dbb_ref[...] = jnp.zeros_like(dbb_ref)

        duf_ref[...] += _tn(ab, dzf)
        dub_ref[...] += _tn(ab, dzb)
        dbf_ref[...] += jnp.sum(dzf.astype(F32), axis=0, keepdims=True)
        dbb_ref[...] += jnp.sum(dzb.astype(F32), axis=0, keepdims=True)

    def s_spec(w):
        return pl.BlockSpec((1, TM_PREP, w), lambda i: (*seq_map(i), 0))

    full = lambda shape: pl.BlockSpec(shape, lambda i: (0,) * len(shape))
    return _pallas(
        body, name="gla_prep_bwd", grid=(n_tiles,),
        in_specs=[pl.BlockSpec((TM_PREP, 128), lambda i: (i, O3_AB // 128)),
                  s_spec(GLA_DK), s_spec(GLA_DK), s_spec(GLA_DK), s_spec(GLA_DK), s_spec(D), s_spec(D),
                  s_spec(GLA_DK), s_spec(GLA_DK),
                  full((3, 128, GLA_DK)), full((3, 128, GLA_DK)), full((1, GLA_DK)), full((1, GLA_DK))],
        out_specs=[pl.BlockSpec((TM_PREP, W3), lambda i: (i, 0)),
                   full((128, GLA_DK)), full((128, GLA_DK)), full((1, GLA_DK)), full((1, GLA_DK))],
        out_shape=[jax.ShapeDtypeStruct((n_tiles * TM_PREP, W3), BF16),
                   jax.ShapeDtypeStruct((128, GLA_DK), F32), jax.ShapeDtypeStruct((128, GLA_DK), F32),
                   jax.ShapeDtypeStruct((1, GLA_DK), F32), jax.ShapeDtypeStruct((1, GLA_DK), F32)],
        compiler_params=_params(("arbitrary",)),
    )(p3, dq_f, dq_b, dk_f, dk_b, dv_f, dv_b, dc_f, dc_b, upf, upb, bias_f, bias_b)


def _sub_blocks(rev):
    if NSUB == 1:
        return [((0, CHUNK), CHUNK // 2, (0, CHUNK))]
    out = []
    for s in range(NSUB):
        rows = (s * SUB, SUB)
        if rev:
            ref = (s + 1) * SUB if s < NSUB - 1 else None
            cols = (s * SUB, CHUNK - s * SUB)
        else:
            ref = s * SUB - 1 if s > 0 else None
            cols = (0, (s + 1) * SUB)
        out.append((rows, ref, cols))
    return out


def _sub_mask(rows, cols, rev):
    r = rows[0] + lax.broadcasted_iota(jnp.int32, (rows[1], cols[1]), 0)
    c = cols[0] + lax.broadcasted_iota(jnp.int32, (rows[1], cols[1]), 1)
    return (c >= r) if rev else (c <= r)


def _sub_operands(qc, kc, cc, rows, ref, cols):
    cref = jnp.zeros((1, HEAD_K), F32) if ref is None else cc[ref:ref + 1]
    eq = jnp.exp(cc[rows[0]:rows[0] + rows[1]] - cref)
    ek = jnp.exp(cref - cc[cols[0]:cols[0] + cols[1]])
    qs = qc[rows[0]:rows[0] + rows[1]] * eq
    kk = kc[cols[0]:cols[0] + cols[1]] * ek
    return qs, kk, eq, ek


SCAN_ROWS = 256
SCAN_CHUNKS = SCAN_ROWS // CHUNK
SCAN_STEPS = SEQ_ALL // SCAN_ROWS
LAT_BLOCKS = SEQ // SCAN_ROWS


def _scan_block(t, rev):
    if rev:
        return SCAN_STEPS - 1 - t
    return jnp.where(t == 0, SCAN_STEPS - 1, t - 1)


def _scan_lat_block(t, rev):
    if rev:
        return jnp.minimum(SCAN_STEPS - 1 - t, LAT_BLOCKS - 1)
    return jnp.maximum(t - 1, 0)


def _head_cols(h):
    return pl.ds(h * HEAD_K, HEAD_K), pl.ds(h * HEAD_V, HEAD_V)


def gla_scan_fwd(q, k, v, cum, *, rev, name):
    n = q.shape[0]

    def body(q_ref, k_ref, v_ref, c_ref, o_ref, s_ref, sfin_ref, st):
        t = pl.program_id(1)

        @pl.when(t == 0)
        def _():
            st[...] = jnp.zeros_like(st)

        def chunk(j, carry):
            lj = SCAN_CHUNKS - 1 - j if rev else j
            r0 = lj * CHUNK
            rws = pl.ds(r0, CHUNK)
            for h in range(HEADS):
                kcols, vcols = _head_cols(h)
                qc, kc, cc = q_ref[0, rws, kcols], k_ref[0, rws, kcols], c_ref[0, rws, kcols]
                vc = v_ref[0, rws, vcols]
                s_in = st[h]
                s_ref[0, h, j] = _b16(s_in)
                edge = cc[0:1] if rev else cc[CHUNK - 1:CHUNK]
                ke = kc * jnp.exp(edge - cc)
                st[h] = s_in * jnp.exp(edge) + _tn(_b16(vc), _b16(ke))
                o_inter = _nt(_b16(qc * jnp.exp(cc)), _b16(s_in))
                vb = _b16(vc)
                for rows, ref, cols in _sub_blocks(rev):
                    qs, kk, _, _ = _sub_operands(qc, kc, cc, rows, ref, cols)
                    a = jnp.where(_sub_mask(rows, cols, rev), _nt(_b16(qs), _b16(kk)), 0.0)
                    o_s = _nn(_b16(a), vb[cols[0]:cols[0] + cols[1]])
                    o_ref[0, pl.ds(r0 + rows[0], rows[1]), vcols] = _b16(o_inter[rows[0]:rows[0] + rows[1]] + o_s)
            return carry

        for j in range(SCAN_CHUNKS):
            chunk(j, 0)

        @pl.when(t == SCAN_STEPS - 1)
        def _():
            sfin_ref[0] = st[...]

    def spec(w):
        return pl.BlockSpec((1, SCAN_ROWS, w), lambda b, t: (b, _scan_block(t, rev), 0))

    return _pallas(
        body, name=name, grid=(n, SCAN_STEPS),
        in_specs=[spec(GLA_DK), spec(GLA_DK), spec(D), spec(GLA_DK)],
        out_specs=[pl.BlockSpec((1, SCAN_ROWS, D), lambda b, t: (b, _scan_lat_block(t, rev), 0)),
                   pl.BlockSpec((1, HEADS, SCAN_CHUNKS, HEAD_V, HEAD_K), lambda b, t: (b, 0, t, 0, 0)),
                   pl.BlockSpec((1, HEADS, HEAD_V, HEAD_K), lambda b, t: (b, 0, 0, 0))],
        out_shape=[jax.ShapeDtypeStruct((n, SEQ, D), BF16),
                   jax.ShapeDtypeStruct((n, HEADS, NCHUNK, HEAD_V, HEAD_K), BF16),
                   jax.ShapeDtypeStruct((n, HEADS, HEAD_V, HEAD_K), F32)],
        scratch_shapes=[pltpu.VMEM((HEADS, HEAD_V, HEAD_K), F32)],
        compiler_params=_params(("parallel", "arbitrary")),
    )(q, k, v, cum)


def gla_scan_bwd(q, k, v, cum, s_all, s_fin, do, *, rev, name, rider=None):
    n = q.shape[0]

    def body(q_ref, k_ref, v_ref, c_ref, s_ref, sfin_ref, do_ref, *rest):
        if rider is not None:
            ride_in, rest = rest[0], rest[1:]
        dq_ref, dk_ref, dv_ref, dc_ref = rest[:4]
        if rider is not None:
            ride_out, rest = rest[4], rest[:4] + rest[5:]
        dst, s_next, dq_acc, dk_acc, dv_acc = rest[4:9]
        t = SCAN_STEPS - 1 - pl.program_id(1)

        if rider is not None:
            def copies():
                send, recv = rest[9], rest[10]
                if rider[0] == "swap":
                    return _pair_copies([], ride_in, [], ride_out, send, recv)
                x, y, c, chips = _place()
                return [_remote(ride_in.at[2 * cx + cy, rows], ride_out.at[pj, rows], send.at[pj * P_ROW_CHUNKS + pi],
                                recv.at[pj * P_ROW_CHUNKS + pi], (cx, cy, c))
                        for pj, (cx, cy) in enumerate(chips)
                        for pi, (_, rows) in enumerate(_half_chunks(0, (0, 2 * ride_in.shape[1]), 16, which=(1,)))]

            @pl.when((pl.program_id(0) == 0) & (pl.program_id(1) == 0))
            def _():
                for cp in copies():
                    cp.start()

            @pl.when((pl.program_id(0) == n - 1) & (pl.program_id(1) == SCAN_STEPS - 1))
            def _():
                for cp in copies():
                    cp.wait_recv()
                for cp in copies():
                    cp.wait_send()

        @pl.when(pl.program_id(1) == 0)
        def _():
            dst[...] = jnp.zeros_like(dst)
            s_next[...] = sfin_ref[0]

        def chunk(jj, carry):
            j = SCAN_CHUNKS - 1 - jj
            lj = SCAN_CHUNKS - 1 - j if rev else j
            rws = pl.ds(lj * CHUNK, CHUNK)
            for h in range(HEADS):
                kcols, vcols = _head_cols(h)
                qc, kc, cc = q_ref[0, rws, kcols], k_ref[0, rws, kcols], c_ref[0, rws, kcols]
                vc = v_ref[0, rws, vcols]
                doc = jnp.where(t > 0, do_ref[0, rws, vcols], 0.0)
                s_in = s_ref[0, h, j]
                s_out = s_next[h]
                ds_out = dst[h]
                edge = cc[0:1] if rev else cc[CHUNK - 1:CHUNK]
                e_q = jnp.exp(cc)
                e_k = jnp.exp(edge - cc)
                dob = _b16(doc)
                dsb = _b16(ds_out)
                dst[h] = ds_out * jnp.exp(edge) + _tn(dob, _b16(qc * e_q))
                s_next[h] = s_in.astype(F32)
                dq_acc[h] = e_q * _nn(dob, s_in)
                dk_acc[h] = e_k * _nn(_b16(vc), dsb)
                dv_acc[h] = _nt(_b16(kc * e_k), dsb)
                vb = _b16(vc)
                for rows, ref, cols in _sub_blocks(rev):
                    qs, kk, eq, ek = _sub_operands(qc, kc, cc, rows, ref, cols)
                    mask = _sub_mask(rows, cols, rev)
                    rsl = slice(rows[0], rows[0] + rows[1])
                    csl = pl.ds(cols[0], cols[1])
                    qsb, kkb = _b16(qs), _b16(kk)
                    a = jnp.where(mask, _nt(qsb, kkb), 0.0)
                    da = _b16(jnp.where(mask, _nt(dob[rsl], vb[cols[0]:cols[0] + cols[1]]), 0.0))
                    dq_acc[h, pl.ds(rows[0], rows[1]), :] += _nn(da, kkb) * eq
                    dk_acc[h, csl, :] += _tn(da, qsb) * ek
                    dv_acc[h, csl, :] += _tn(_b16(a), dob[rsl])
                dq = dq_acc[h]
                dk = dk_acc[h]
                dc = qc * dq - kc * dk
                bnd = jnp.sum(ds_out * s_out, axis=0, keepdims=True)
                edge_row = 0 if rev else CHUNK - 1
                is_edge = lax.broadcasted_iota(jnp.int32, (CHUNK, HEAD_K), 0) == edge_row
                dq_ref[0, rws, kcols] = _b16(dq)
                dk_ref[0, rws, kcols] = _b16(dk)
                dv_ref[0, rws, vcols] = _b16(dv_acc[h])
                dc_ref[0, rws, kcols] = dc + jnp.where(is_edge, bnd, 0.0)
            return carry

        for jj in range(SCAN_CHUNKS):
            chunk(jj, 0)

    def step_of(u):
        return SCAN_STEPS - 1 - u

    def spec(w):
        return pl.BlockSpec((1, SCAN_ROWS, w), lambda b, u: (b, _scan_block(step_of(u), rev), 0))

    in_specs = [spec(GLA_DK), spec(GLA_DK), spec(D), spec(GLA_DK),
                pl.BlockSpec((1, HEADS, SCAN_CHUNKS, HEAD_V, HEAD_K), lambda b, u: (b, 0, step_of(u), 0, 0)),
                pl.BlockSpec((1, HEADS, HEAD_V, HEAD_K), lambda b, u: (b, 0, 0, 0)),
                pl.BlockSpec((1, SCAN_ROWS, D), lambda b, u: (b, _scan_lat_block(step_of(u), rev), 0))]
    out_specs = [spec(GLA_DK), spec(GLA_DK), spec(D), spec(GLA_DK)]
    out_shape = [jax.ShapeDtypeStruct((n, SEQ_ALL, GLA_DK), BF16), jax.ShapeDtypeStruct((n, SEQ_ALL, GLA_DK), BF16),
                 jax.ShapeDtypeStruct((n, SEQ_ALL, D), BF16), jax.ShapeDtypeStruct((n, SEQ_ALL, GLA_DK), F32)]
    scratch = [pltpu.VMEM((HEADS, HEAD_V, HEAD_K), F32), pltpu.VMEM((HEADS, HEAD_V, HEAD_K), F32),
               pltpu.VMEM((HEADS, CHUNK, HEAD_K), F32), pltpu.VMEM((HEADS, CHUNK, HEAD_K), F32),
               pltpu.VMEM((HEADS, CHUNK, HEAD_V), F32)]
    args = [q, k, v, cum, s_all, s_fin, do]
    if rider is not None:
        kind, arr = rider
        any_spec = pl.BlockSpec(memory_space=pl.ANY)
        in_specs.append(any_spec)
        out_specs.append(any_spec)
        args.append(arr)
        if kind == "swap":
            out_shape += _pair_got_shapes([], arr)
            n_cp = _pair_count([], arr)
        else:
            out_shape.append(jax.ShapeDtypeStruct((3,) + arr.shape[1:], arr.dtype))
            n_cp = 3 * P_ROW_CHUNKS
        scratch += [pltpu.SemaphoreType.DMA((n_cp,)), pltpu.SemaphoreType.DMA((n_cp,))]
    return _pallas(
        body, name=name, grid=(n, SCAN_STEPS), in_specs=in_specs, out_specs=out_specs, out_shape=out_shape,
        scratch_shapes=scratch,
        compiler_params=_params(("parallel" if rider is None else "arbitrary", "arbitrary")),
    )(*args)


def gla_out_proj(o_f, o_b, r, gnorm, gla_proj):
    n = o_f.shape[0]
    tiles = SEQ // TM_OUT

    def body(of_ref, ob_ref, r_ref, g_ref, w_ref, og_ref, y_ref):
        for h in range(HEADS):
            cols = pl.ds(h * HEAD_V, HEAD_V)
            o = of_ref[0, :, cols].astype(F32) + ob_ref[0, :, cols].astype(F32)
            rs = lax.rsqrt(jnp.mean(o * o, axis=-1, keepdims=True) + EPS)
            og_ref[:, cols] = (o * rs * g_ref[...] * _silu(r_ref[:, cols].astype(F32))).astype(og_ref.dtype)
        y_ref[...] = _nn(og_ref[...], w_ref[...]).astype(y_ref.dtype)

    ospec = pl.BlockSpec((1, TM_OUT, D), lambda b, j: (b, j, 0))
    row = pl.BlockSpec((TM_OUT, D), lambda b, j: (b * tiles + j, 0))
    return _pallas(
        body, name="gla_out_proj", grid=(n, tiles),
        in_specs=[ospec, ospec, row, pl.BlockSpec((1, HEAD_V), lambda b, j: (0, 0)),
                  pl.BlockSpec((D, D), lambda b, j: (0, 0))],
        out_specs=[row, row],
        out_shape=[jax.ShapeDtypeStruct((n * SEQ, D), BF16), jax.ShapeDtypeStruct((n * SEQ, D), BF16)],
        compiler_params=_params(("parallel", "parallel")),
    )(o_f, o_b, r, gnorm, gla_proj)


def gla_out_bwd(o_f, o_b, r, dyg, gla_proj, gnorm, og):
    n = o_f.shape[0]
    tiles = SEQ // TM_EW

    def body(of_ref, ob_ref, r_ref, d_ref, w_ref, g_ref, og_ref, do_ref, dr_ref, dg_ref, gw_ref, dog_buf):
        @pl.when((pl.program_id(0) == 0) & (pl.program_id(1) == 0))
        def _():
            dg_ref[...] = jnp.zeros_like(dg_ref)
            gw_ref[...] = jnp.zeros_like(gw_ref)

        gw_ref[...] += _tn(og_ref[...], d_ref[...])
        dog_buf[...] = _nt(d_ref[...], w_ref[...])
        for h in range(HEADS):
            cols = pl.ds(h * HEAD_V, HEAD_V)
            o = of_ref[0, :, cols].astype(F32) + ob_ref[0, :, cols].astype(F32)
            rv = r_ref[:, cols].astype(F32)
            dv = dog_buf[:, cols]
            rs = lax.rsqrt(jnp.mean(o * o, axis=-1, keepdims=True) + EPS)
            oh = o * rs
            dr_ref[:, cols] = (dv * oh * g_ref[...] * _dsilu(rv)).astype(dr_ref.dtype)
            dn = dv * _silu(rv)
            dg_ref[...] += jnp.sum(dn * oh, axis=0, keepdims=True)
            doh = dn * g_ref[...]
            do_ref[0, :, cols] = _b16(rs * (doh - oh * jnp.mean(doh * oh, axis=-1, keepdims=True)))

    ospec = pl.BlockSpec((1, TM_EW, D), lambda b, j: (b, j, 0))
    row = pl.BlockSpec((TM_EW, D), lambda b, j: (b * tiles + j, 0))
    vec = pl.BlockSpec((1, HEAD_V), lambda b, j: (0, 0))
    return _pallas(
        body, name="gla_out_bwd", grid=(n, tiles),
        in_specs=[ospec, ospec, row, row, pl.BlockSpec((D, D), lambda b, j: (0, 0)), vec, row],
        out_specs=[ospec, row, vec, pl.BlockSpec((D, D), lambda b, j: (0, 0))],
        out_shape=[jax.ShapeDtypeStruct((n, SEQ, D), BF16), jax.ShapeDtypeStruct((n * SEQ, D), BF16),
                   jax.ShapeDtypeStruct((1, HEAD_V), F32), jax.ShapeDtypeStruct((D, D), F32)],
        scratch_shapes=[pltpu.VMEM((TM_EW, D), F32)],
        compiler_params=_params(("arbitrary", "arbitrary")),
    )(o_f, o_b, r, dyg, gla_proj, gnorm, og)


TM_OUT = 512


def merge_out_final(p5, y_conv, y_gla, w_out, x2, gate, final_g, target, n_samples):
    t = x2.shape[0]
    tiles = SEQ // TM_OUT

    def body(mc_ref, mg_ref, yc_ref, yg_ref, w_ref, x_ref, gate_ref, g_ref, t_ref,
             mrg_ref, dh_ref, dmo_ref, dgate_ref, dg_ref, loss_ref):
        b, j = pl.program_id(0), pl.program_id(1)
        f = lambda ref: ref[...].astype(F32)
        merged = _b16(_sigmoid(f(mc_ref)) * f(yc_ref) + _sigmoid(f(mg_ref)) * f(yg_ref))
        mrg_ref[...] = merged
        mo_v = _nn(merged, w_ref[...])
        h = x_ref[...] + gate_ref[0] * mo_v
        rs = lax.rsqrt(jnp.mean(h * h, axis=-1, keepdims=True) + EPS)
        nh = h * rs
        err = nh * g_ref[...] - t_ref[...]
        dy = err * (1.0 / D)
        dn = dy * g_ref[...]
        dh = rs * (dn - nh * jnp.mean(dn * nh, axis=-1, keepdims=True))
        dh_ref[...] = dh
        dmo_ref[...] = (dh * gate_ref[0]).astype(dmo_ref.dtype)

        @pl.when(j == 0)
        def _():
            dgate_ref[...] = jnp.zeros_like(dgate_ref)

        @pl.when((b == 0) & (j == 0))
        def _():
            dg_ref[...] = jnp.zeros_like(dg_ref)
            loss_ref[...] = jnp.zeros_like(loss_ref)

        dgate_ref[0] += jnp.sum(dh * mo_v, axis=0, keepdims=True)
        dg_ref[...] += jnp.sum(dy * nh, axis=0, keepdims=True)
        loss_ref[...] += (0.5 / D) * jnp.sum(err * err)

    row = pl.BlockSpec((TM_OUT, D), lambda b, j: (b * tiles + j, 0))
    per = pl.BlockSpec((1, 1, D), lambda b, j: (b, 0, 0))
    vec = pl.BlockSpec((1, D), lambda b, j: (0, 0))
    return _pallas(
        body, name="merge_out_final", grid=(n_samples, tiles),
        in_specs=[row, pl.BlockSpec((TM_OUT, D), lambda b, j: (b * tiles + j, 1)), row, row,
                  pl.BlockSpec((D, D), lambda b, j: (0, 0)), row, per, vec, row],
        out_specs=[row, row, row, per, vec, pl.BlockSpec((8, 128), lambda b, j: (0, 0))],
        out_shape=[jax.ShapeDtypeStruct((t, D), BF16), jax.ShapeDtypeStruct((t, D), F32), jax.ShapeDtypeStruct((t, D), BF16),
                   jax.ShapeDtypeStruct((n_samples, 1, D), F32), jax.ShapeDtypeStruct((1, D), F32),
                   jax.ShapeDtypeStruct((8, 128), F32)],
        compiler_params=_params(("arbitrary", "arbitrary")),
    )(p5, p5, y_conv, y_gla, w_out, x2, gate, final_g, target)


def out_dgrad_merge_bwd(p5, y_conv, y_gla, dmo, w_out, merged):
    t = y_conv.shape[0]

    def body(mc_ref, mg_ref, yc_ref, yg_ref, d_ref, w_ref, mrg_ref, dyc_ref, dyg_ref, dp_ref, gw_ref):
        f = lambda ref: ref[...].astype(F32)

        @pl.when(pl.program_id(0) == 0)
        def _():
            gw_ref[...] = jnp.zeros_like(gw_ref)

        gw_ref[...] += _tn(mrg_ref[...], d_ref[...])
        d = _nt(d_ref[...], w_ref[...])
        sc = _sigmoid(f(mc_ref))
        sg = _sigmoid(f(mg_ref))
        dyc_ref[...] = (d * sc).astype(dyc_ref.dtype)
        dyg_ref[...] = (d * sg).astype(dyg_ref.dtype)
        dp_ref[:, pl.ds(0, D)] = (d * f(yc_ref) * sc * (1.0 - sc)).astype(dp_ref.dtype)
        dp_ref[:, pl.ds(D, D)] = (d * f(yg_ref) * sg * (1.0 - sg)).astype(dp_ref.dtype)

    row = pl.BlockSpec((TM_OUT, D), lambda i: (i, 0))
    return _pallas(
        body, name="out_dgrad_merge_bwd", grid=(t // TM_OUT,),
        in_specs=[row, pl.BlockSpec((TM_OUT, D), lambda i: (i, 1)), row, row, row, pl.BlockSpec((D, D), lambda i: (0, 0)),
                  row],
        out_specs=[row, row, pl.BlockSpec((TM_OUT, 2 * D), lambda i: (i, 0)), pl.BlockSpec((D, D), lambda i: (0, 0))],
        out_shape=[jax.ShapeDtypeStruct((t, D), BF16), jax.ShapeDtypeStruct((t, D), BF16),
                   jax.ShapeDtypeStruct((t, 2 * D), BF16), jax.ShapeDtypeStruct((D, D), F32)],
        compiler_params=_params(("arbitrary",)),
    )(p5, p5, y_conv, y_gla, dmo, w_out, merged)


def local_step(x, ctx, target, mod, wts, small, p_sh, chip, core):
    n = x.shape[0]
    t = n * SEQ
    t_all = t + n * NCTX
    x2 = x.reshape(t, D)
    ctx2 = ctx.reshape(n * NCTX, D)
    tgt2 = target.reshape(t, D)
    scale1, shift, gate = mod

    u = norm_mod_fwd(x2, ctx2, scale1, shift, small["norm_g"])
    p1, p2, p3, p4, p5, p_all = proj_all(u, [wts["w%d" % i] for i in range(1, 6)], [small["b%d" % i] for i in range(1, 6)],
                                         [t, t, t_all, t, t], p_sh, tm=512)
    p_full = jnp.stack([jnp.where(chip == i, p_sh, p_all[i]) for i in range(N_CHIPS)])
    wts = dict(wts, conv_proj=p_full[:, 0:256].reshape(D, D), gla_proj=p_full[:, 256:512].reshape(D, D),
               w_out=p_full[:, 512:768].reshape(D, D))

    aconv = conv_fwd(p1, small["conv_w"], small["conv_b"], n)
    ac, y_conv = ln_gate_proj(aconv, p2, small["conv_ln_g"], small["conv_ln_b"], wts["conv_proj"])

    qs, ks, vs, cum_f, cum_b = gla_prep_fwd(p3, small["upf"], small["upb"], small["bias_f"], small["bias_b"], n)
    o_f, s_f, sfin_f = gla_scan_fwd(qs, ks, vs, cum_f, rev=False, name="gla_scan_fwd_f")
    o_b, s_b, sfin_b = gla_scan_fwd(qs, ks, vs, cum_b, rev=True, name="gla_scan_fwd_b")
    og, y_gla = gla_out_proj(o_f, o_b, p4, small["gla_norm_g"], wts["gla_proj"])

    merged, dh, dmo, dgate, d_final_g, loss = merge_out_final(p5, y_conv, y_gla, wts["w_out"], x2, gate,
                                                              small["final_norm_g"], tgt2, n)

    g = {"final_norm_g": d_final_g}
    dyc, dyg, dp5, g["w_out"] = out_dgrad_merge_bwd(p5, y_conv, y_gla, dmo, wts["w_out"], merged)

    daconv, dp2, g["conv_ln_g"], g["conv_ln_b"], g["conv_proj"] = ln_gate_bwd(
        aconv, p2, dyc, wts["conv_proj"], small["conv_ln_g"], small["conv_ln_b"], ac)
    dp1, dconv_w, dconv_b = conv_bwd(p1, daconv, small["conv_w"], n)
    g["conv_w"], g["conv_b"] = dconv_w, dconv_b

    do, dp4, g["gla_norm_g"], g["gla_proj"] = gla_out_bwd(o_f, o_b, p4, dyg, wts["gla_proj"], small["gla_norm_g"], og)
    g["proj"] = jnp.concatenate([g["conv_proj"].reshape(N_CHIPS, 256, D), g["gla_proj"].reshape(N_CHIPS, 256, D),
                                 g["w_out"].reshape(N_CHIPS, 256, D)], 1)
    dq_f, dk_f, dv_f, dc_f, gotp = gla_scan_bwd(qs, ks, vs, cum_f, s_f, sfin_f, do, rev=False, name="gla_scan_bwd_f",
                                                rider=("swap", g["proj"]))
    pap16 = pair_add(core, g["proj"], gotp, name="pair_add_p", tr=384)
    dq_b, dk_b, dv_b, dc_b, rbp = gla_scan_bwd(qs, ks, vs, cum_b, s_b, sfin_b, do, rev=True, name="gla_scan_bwd_b",
                                               rider=("exchange", pap16))
    dp3, g["upf"], g["upb"], g["bias_f"], g["bias_b"] = gla_prep_bwd(
        p3, dq_f, dq_b, dk_f, dk_b, dv_f, dv_b, dc_f, dc_b,
        small["upf"], small["upb"], small["bias_f"], small["bias_b"], n)

    dps = [dp1, dp2, dp3, dp4, dp5]
    early = [g["conv_b"].sum(0), g["conv_ln_g"], g["conv_ln_b"], g["bias_f"], g["bias_b"], g["gla_norm_g"],
             g["final_norm_g"], g["conv_w"].sum(0)[:CONV_K], g["upf"][0:16], g["upb"][16:32]]
    got = {}
    for i in [0, 1, 3, 4, 2]:
        dp = dps[i]
        rows = dp.shape[0]
        tn = W3 if dp.shape[1] == W3 else 1024
        others = [j for j in range(5) if j != i]
        swap = ([g["w%d" % (j + 1)] for j in others], _pack(early)) if i == 2 else None
        outs = matmul_tn(u, dp, name="w_in_wgrad_%d" % (i + 1), t=rows, tn=tn, tt=1024 if rows % 1024 == 0 else 768,
                         colsum=True, swap=swap)
        g["w%d" % (i + 1)], g["b%d" % (i + 1)] = outs[0], outs[1]
        if swap is not None:
            got = dict(zip(others, outs[2:2 + len(others)]))
            sall_early = outs[2 + len(others)]
    g["gate"] = dgate
    return loss, dh, dps, g, got, (pap16, rbp), (sall_early, [a.shape for a in early])


def _group_cols(w):
    gv, gg, z = w[..., 0:1024], w[..., 1024:2048], w[..., 2048:3072]
    q, k, v = w[..., 3072:3584], w[..., 3584:4096], w[..., 4096:5120]
    ab = w[..., 5120:5152]
    r, mc, mg = w[..., 5152:6176], w[..., 6176:7200], w[..., 7200:8224]
    g1 = jnp.concatenate([p for j in range(CONV_NCB)
                          for p in (gv[..., CONV_CB * j:CONV_CB * (j + 1)], gg[..., CONV_CB * j:CONV_CB * (j + 1)])], -1)
    pad = jnp.zeros(w.shape[:-1] + (W3 - 2080,), w.dtype)
    g3 = jnp.concatenate([v, q, k, ab, pad], -1)
    return g1, z, g3, r, jnp.concatenate([mc, mg], -1)


def _ungroup_cols(g1, g2, g3, g4, g5):
    gv = jnp.concatenate([g1[..., 2 * CONV_CB * j:2 * CONV_CB * j + CONV_CB] for j in range(CONV_NCB)], -1)
    gg = jnp.concatenate([g1[..., 2 * CONV_CB * j + CONV_CB:2 * CONV_CB * (j + 1)] for j in range(CONV_NCB)], -1)
    v, q, k, ab = g3[..., 0:1024], g3[..., 1024:1536], g3[..., 1536:2048], g3[..., 2048:2080]
    return jnp.concatenate([gv, gg, g2, q, k, v, ab, g4, g5[..., 0:1024], g5[..., 1024:2048]], -1)


def _natural_pieces():
    pieces = [(CONV_CB * j, CONV_CB, 0, 2 * CONV_CB * j) for j in range(CONV_NCB)]
    pieces += [(1024 + CONV_CB * j, CONV_CB, 0, 2 * CONV_CB * j + CONV_CB) for j in range(CONV_NCB)]
    pieces += [(2048, 1024, 1, 0), (3072, 512, 2, O3_Q), (3584, 512, 2, O3_K), (4096, 1024, 2, O3_V), (5120, 32, 2, O3_AB),
               (5152, 1024, 3, 0), (6176, 1024, 4, 0), (7200, 1024, 4, 1024)]
    return sorted(pieces)


def _ungroup_to_shards(groups):
    shards = []
    for i in range(N_CHIPS):
        lo, hi = i * W_IN_SHARD, (i + 1) * W_IN_SHARD
        parts = []
        for nat, width, g, gcol in _natural_pieces():
            a, b = max(nat, lo), min(nat + width, hi)
            if a < b:
                parts.append(groups[g][:, gcol + a - nat:gcol + b - nat])
        shards.append(jnp.concatenate(parts, 1))
    return jnp.stack(shards)


def _pad_up(up, row0):
    return jnp.zeros((128, GLA_DK), F32).at[row0:row0 + up.shape[0]].set(up)


def _adamw_math(w, g, m, v):
    m = ADAM_B1 * m + (1.0 - ADAM_B1) * g
    v = ADAM_B2 * v + (1.0 - ADAM_B2) * (g * g)
    m_hat = m / (1.0 - ADAM_B1 ** ADAM_STEP)
    v_hat = v / (1.0 - ADAM_B2 ** ADAM_STEP)
    delta = -ADAM_LR * (m_hat / (jnp.sqrt(v_hat) + ADAM_EPS) + ADAM_WD * w)
    return delta, m, v


def adamw2d(w, g, m, v, *, name, tr, tcols=None):
    rows, cols = w.shape[-2:]

    def body(w_ref, g_ref, m_ref, v_ref, d_ref, nm_ref, nv_ref):
        d_ref[...], nm_ref[...], nv_ref[...] = _adamw_math(w_ref[...], g_ref[...], m_ref[...], v_ref[...])

    tcols = cols if tcols is None else tcols
    if w.ndim == 3:
        spec = pl.BlockSpec((1, tr, tcols), lambda i, j: (0, i, j))
    else:
        spec = pl.BlockSpec((tr, tcols), lambda i, j: (i, j))
    return _pallas(
        body, name=name, grid=(rows // tr, cols // tcols), in_specs=[spec] * 4, out_specs=[spec] * 3,
        out_shape=[jax.ShapeDtypeStruct(w.shape, F32)] * 3, compiler_params=_params(("parallel", "parallel")),
    )(w, g, m, v)


def adamw_many(ws, gs, ms, vs):
    k = len(ws)
    two = lambda a: a.reshape((-1, a.shape[-1]))

    def body(*refs):
        w_refs, g_refs, m_refs, v_refs = refs[:k], refs[k:2 * k], refs[2 * k:3 * k], refs[3 * k:4 * k]
        d_refs, nm_refs, nv_refs = refs[4 * k:5 * k], refs[5 * k:6 * k], refs[6 * k:7 * k]
        for i in range(k):
            d_refs[i][...], nm_refs[i][...], nv_refs[i][...] = _adamw_math(
                w_refs[i][...], g_refs[i][...], m_refs[i][...], v_refs[i][...])

    shapes = [jax.ShapeDtypeStruct(two(a).shape, F32) for a in ws]
    outs = _pallas(body, name="adamw_small", out_shape=shapes * 3, compiler_params=_params())(
        *[two(a) for a in ws], *[two(a) for a in gs], *[two(a) for a in ms], *[two(a) for a in vs])
    back = lambda lst: [o.reshape(a.shape) for o, a in zip(lst, ws)]
    return back(outs[:k]), back(outs[k:2 * k]), back(outs[2 * k:])


def sum_devices(salls):
    k = len(salls)

    def body(*refs):
        for i in range(k):
            acc = refs[i][0]
            for d in range(1, N_DEV):
                acc = acc + refs[i][d]
            refs[k + i][...] = acc

    return _pallas(body, name="sum_devices", out_shape=[jax.ShapeDtypeStruct(a.shape[1:], F32) for a in salls],
                   compiler_params=_params())(*salls)


def pair_add(core, g, got, *, name, tr):
    n, rows, cols = got.shape
    g4 = g.reshape(n, 2, rows, cols)

    def body(core_ref, g_ref, got_ref, ob_ref):
        del core_ref
        ob_ref[0] = (g_ref[0, 0] + got_ref[0]).astype(BF16)

    spec = pl.BlockSpec((1, tr, cols), lambda i, t, core_ref: (i, t, 0))
    return _pallas(
        body, name=name,
        grid_spec=pltpu.PrefetchScalarGridSpec(
            num_scalar_prefetch=1, grid=(n, rows // tr),
            in_specs=[pl.BlockSpec((1, 1, tr, cols), lambda i, t, core_ref: (i, core_ref[0], t, 0)), spec],
            out_specs=spec),
        out_shape=jax.ShapeDtypeStruct(got.shape, BF16),
        compiler_params=_params(("parallel", "parallel")))(core, g4, got)


def pair_add_groups(core, gs, gots, *, tr, swap=None):
    k = len(gs)
    rows = gots[0].shape[0]
    n_sw = 0 if swap is None else len(swap)
    n_steps = rows // tr

    def body(core_ref, *refs):
        del core_ref
        for i in range(k):
            refs[2 * k + n_sw + i][...] = (refs[i][0] + refs[k + i][...]).astype(BF16)
        if swap is not None:
            g_refs = refs[2 * k:2 * k + n_sw]
            got_refs = refs[3 * k + n_sw:3 * k + 2 * n_sw]
            sems = refs[3 * k + 2 * n_sw:]
            t = pl.program_id(0)

            @pl.when(t == 0)
            def _():
                for cp in _pair_copies(g_refs, None, got_refs, None, *sems):
                    cp.start()

            @pl.when(t == n_steps - 1)
            def _():
                for cp in _pair_copies(g_refs, None, got_refs, None, *sems):
                    cp.wait_recv()
                for cp in _pair_copies(g_refs, None, got_refs, None, *sems):
                    cp.wait_send()

    g_specs = [pl.BlockSpec((1, tr, a.shape[1]), lambda t, core_ref: (core_ref[0], t, 0)) for a in gots]
    r_specs = [pl.BlockSpec((tr, a.shape[1]), lambda t, core_ref: (t, 0)) for a in gots]
    any_spec = pl.BlockSpec(memory_space=pl.ANY)
    out_shape = [jax.ShapeDtypeStruct(a.shape, BF16) for a in gots]
    scratch, extra = [], []
    if swap is not None:
        out_shape += _pair_got_shapes(swap, None)
        extra = list(swap)
        scratch = [pltpu.SemaphoreType.DMA((_pair_count(swap, None),)), pltpu.SemaphoreType.DMA((_pair_count(swap, None),))]
    return _pallas(
        body, name="pair_add_w",
        grid_spec=pltpu.PrefetchScalarGridSpec(num_scalar_prefetch=1, grid=(n_steps,),
                                               in_specs=g_specs + r_specs + [any_spec] * n_sw,
                                               out_specs=r_specs + [any_spec] * n_sw, scratch_shapes=scratch),
        out_shape=out_shape,
        compiler_params=_params(("parallel" if swap is None else "arbitrary",)),
    )(core, *[a.reshape(2, rows, a.shape[1]) for a in gs], *gots, *extra)


def chip_add(place, pas, rbs, *, steps):
    k = len(pas)

    def body(place_ref, *refs):
        del place_ref
        for i in range(k):
            m_ref, r_ref, o_ref = refs[i], refs[k + i], refs[2 * k + i]
            o_ref[0] = ((m_ref[0].astype(F32) + r_ref[0].astype(F32)) + r_ref[1].astype(F32)) + r_ref[2].astype(F32)

    trs = [pa.shape[1] // steps for pa in pas]
    return _pallas(
        body, name="chip_add",
        grid_spec=pltpu.PrefetchScalarGridSpec(
            num_scalar_prefetch=1, grid=(steps,),
            in_specs=[pl.BlockSpec((1, tr, pa.shape[2]), lambda t, place_ref: (place_ref[0], t, 0)) for pa, tr in zip(pas, trs)]
            + [pl.BlockSpec((3, tr, pa.shape[2]), lambda t, place_ref: (0, t, 0)) for pa, tr in zip(pas, trs)],
            out_specs=[pl.BlockSpec((1, tr, pa.shape[2]), lambda t, place_ref: (place_ref[1], t, 0))
                       for pa, tr in zip(pas, trs)]),
        out_shape=[jax.ShapeDtypeStruct((2,) + pa.shape[1:], F32) for pa in pas],
        compiler_params=_params(("parallel",)))(place, *pas, *rbs)


def ada_bwd(call, cctx_rows, dm_shard, dm_full, adaw):
    nsh = adaw.shape[1]

    def body(c_ref, cc_ref, dms_ref, dmf_ref, w_ref, gw_ref, gb_ref, pq_ref):
        a_lat = _silu(c_ref[...])
        a_ctx = _silu(cc_ref[...])
        dms = dms_ref[...]
        gw_ref[...] = _tn(a_lat, dms[0:64], HI) + _tn(a_ctx, dms[64:72], HI)
        gb_ref[...] = jnp.sum(dmf_ref[...], axis=0, keepdims=True)
        part = _nt(dms[64:72], w_ref[...], HI)
        pq_ref[...] = jnp.zeros_like(pq_ref) + jnp.sum(part, axis=0, keepdims=True)

    return _pallas(body, name="ada_bwd",
                   out_shape=[jax.ShapeDtypeStruct((D, nsh), F32), jax.ShapeDtypeStruct((1, 3 * D), F32),
                              jax.ShapeDtypeStruct((8, D), F32)],
                   compiler_params=_params())(call, cctx_rows, dm_shard, dm_full, adaw)


def cctx_grad(pq_all, cctx_rows):
    def body(p_ref, c_ref, o_ref):
        acc = p_ref[0]
        for qi in range(1, N_CHIPS):
            acc = acc + p_ref[qi]
        o_ref[...] = acc * _dsilu(c_ref[...])

    return _pallas(body, name="cctx_grad", out_shape=jax.ShapeDtypeStruct((8, D), F32),
                   compiler_params=_params())(pq_all, cctx_rows)


def _place():
    x, y, c = lax.axis_index("x"), lax.axis_index("y"), lax.axis_index("c")
    chips = [(1 - x, y), (x, 1 - y), (1 - x, 1 - y)]
    return x, y, c, chips


def _all_peers(x, y, c):
    return [((1 - x) if r & 4 else x, (1 - y) if r & 2 else y, (1 - c) if r & 1 else c) for r in range(1, N_DEV)]


def _remote(src, dst, send_sem, recv_sem, dev):
    return pltpu.make_async_remote_copy(src_ref=src, dst_ref=dst, send_sem=send_sem, recv_sem=recv_sem,
                                        device_id=dev, device_id_type=MESH)


ANY = pl.BlockSpec(memory_space=pl.ANY)
VMEM = pl.BlockSpec(memory_space=pltpu.VMEM)
F_ROWS = 16


W_ROW_CHUNKS = 4
P_ROW_CHUNKS = 2
N_BULK = W_ROW_CHUNKS + P_ROW_CHUNKS


def _half_chunks(core, n_rows, align, which=(0, 1)):
    out = []
    for a, k in ((0, W_ROW_CHUNKS), (1, P_ROW_CHUNKS)):
        if a not in which:
            continue
        half = n_rows[a] // 2
        size = half // k
        for i in range(k):
            start = core * half + i * size
            out.append((a, pl.ds(start if isinstance(start, int) else pl.multiple_of(start, align), size)))
    return out


def gather_weights(c8, cctx8, adaw, adab, w_sh, fp):
    nsh = adaw.shape[1]

    def body(c_ref, cctx_ref, adaw_ref, adab_ref, w_ref, fp_ref, wall_ref, fall_ref, call_ref, mall_ref,
             abuf, w_send, w_recv, h_send, h_recv, c_send, c_recv, m_send, m_recv, f_send, f_recv):
        x, y, c, chips = _place()
        q = 2 * x + y
        dev = 4 * x + 2 * y + c
        qs = [2 * cx + cy for cx, cy in chips]
        sib = (x, y, 1 - c)
        srcs, dsts = (w_ref,), (wall_ref,)
        n_rows = (w_ref.shape[0],)
        mine = _half_chunks(c, n_rows, 16, which=(0,))
        other = _half_chunks(1 - c, n_rows, 16, which=(0,))

        bulk = [[_remote(srcs[a].at[rows], dsts[a].at[q, rows], w_send.at[j * N_BULK + i], w_recv.at[j * N_BULK + i],
                         (*chips[j], c)) for i, (a, rows) in enumerate(mine)] for j in range(3)]
        fall_ref[q] = fp_ref[...]
        small = [_remote(fp_ref, fall_ref.at[q], f_send.at[j], f_recv.at[j], (*chips[j], c)) for j in range(3)]
        my_rows = pl.ds(pl.multiple_of(8 * dev, 8), 8)
        call_ref[my_rows, :] = c_ref[...]
        cond = [_remote(c_ref, call_ref.at[my_rows, :], c_send.at[r], c_recv.at[r], peer)
                for r, peer in enumerate(_all_peers(x, y, c))]
        for cp in sum(bulk, []) + small + cond:
            cp.start()
        for cp in cond:
            cp.wait_recv()

        abuf[pl.ds(0, 64), :] = _silu(call_ref[...])
        abuf[pl.ds(64, 8), :] = _silu(cctx_ref[...])
        mall_ref[q] = _nn(abuf[...], adaw_ref[...], HI) + adab_ref[...]
        mod = [_remote(mall_ref.at[q], mall_ref.at[q], m_send.at[j], m_recv.at[j], (*chips[j], c)) for j in range(3)]
        for cp in mod:
            cp.start()

        handed = []
        for j in range(3):
            for i, (a, rows) in enumerate(mine):
                bulk[j][i].wait_recv()
                cp = _remote(dsts[a].at[qs[j], rows], dsts[a].at[qs[j], rows],
                             h_send.at[j * N_BULK + i], h_recv.at[j * N_BULK + i], sib)
                cp.start()
                handed.append(cp)
        for j in range(3):
            for i, (a, rows) in enumerate(other):
                _remote(dsts[a].at[qs[j], rows], dsts[a].at[qs[j], rows],
                        h_send.at[j * N_BULK + i], h_recv.at[j * N_BULK + i], sib).wait_recv()
        for cp in mod + small:
            cp.wait_recv()
        for cp in sum(bulk, []) + small + cond + mod + handed:
            cp.wait_send()

    def dma(n):
        return pltpu.SemaphoreType.DMA((n,))

    return _pallas(
        body, name="gather_weights",
        in_specs=[VMEM, VMEM, VMEM, VMEM, ANY, VMEM],
        out_specs=[ANY, VMEM, VMEM, VMEM],
        out_shape=[jax.ShapeDtypeStruct((N_CHIPS,) + w_sh.shape, BF16),
                   jax.ShapeDtypeStruct((N_CHIPS, F_ROWS, D), F32),
                   jax.ShapeDtypeStruct((8 * N_DEV, D), F32), jax.ShapeDtypeStruct((N_CHIPS, MOD_ROWS, nsh), F32)],
        scratch_shapes=[pltpu.VMEM((MOD_ROWS, D), F32), dma(3 * N_BULK), dma(3 * N_BULK), dma(3 * N_BULK), dma(3 * N_BULK),
                        dma(7), dma(7), dma(3), dma(3), dma(3), dma(3)],
        compiler_params=_params(),
    )(c8, cctx8, adaw, adab, w_sh, fp)


def _pair_count(gs, gp):
    return len(gs) * W_ROW_CHUNKS + (0 if gp is None else N_CHIPS * P_ROW_CHUNKS)


def _pair_got_shapes(gs, gp):
    shapes = [jax.ShapeDtypeStruct((D // 2, a.shape[1]), F32) for a in gs]
    if gp is not None:
        shapes.append(jax.ShapeDtypeStruct((N_CHIPS, gp.shape[1] // 2, gp.shape[2]), F32))
    return shapes


def _pair_copies(g_refs, gp_ref, got_refs, gotp_ref, a_send, a_recv):
    x, y, c, _ = _place()
    sib = (x, y, 1 - c)
    pair = []
    half, size = D // 2, D // 2 // W_ROW_CHUNKS
    for gi in range(len(g_refs)):
        for i in range(W_ROW_CHUNKS):
            k = len(pair)
            rows_o = pl.ds(pl.multiple_of((1 - c) * half + i * size, 8), size)
            pair.append(_remote(g_refs[gi].at[rows_o], got_refs[gi].at[pl.ds(i * size, size)],
                                a_send.at[k], a_recv.at[k], sib))
    if gp_ref is not None:
        half, size = gp_ref.shape[1] // 2, gp_ref.shape[1] // 2 // P_ROW_CHUNKS
        for s in range(N_CHIPS):
            for i in range(P_ROW_CHUNKS):
                k = len(pair)
                rows_o = pl.ds(pl.multiple_of((1 - c) * half + i * size, 8), size)
                pair.append(_remote(gp_ref.at[s, rows_o], gotp_ref.at[s, pl.ds(i * size, size)],
                                    a_send.at[k], a_recv.at[k], sib))
    return pair


def gather_small(sm):
    rows = sm.shape[0]

    def body(sm_ref, sall_ref, s_send, s_recv):
        x, y, c, _ = _place()
        dev = 4 * x + 2 * y + c
        sall_ref[dev] = sm_ref[...]
        small = [_remote(sm_ref, sall_ref.at[dev], s_send.at[r], s_recv.at[r], peer)
                 for r, peer in enumerate(_all_peers(x, y, c))]
        for cp in small:
            cp.start()
        for cp in small:
            cp.wait_recv()
        for cp in small:
            cp.wait_send()

    return _pallas(
        body, name="gather_small", in_specs=[VMEM], out_specs=VMEM,
        out_shape=jax.ShapeDtypeStruct((N_DEV, rows, D), F32),
        scratch_shapes=[pltpu.SemaphoreType.DMA((7,)), pltpu.SemaphoreType.DMA((7,))],
        compiler_params=_params(),
    )(sm)


def pair_share(ghw, ghp, pq):
    def body(ghw_ref, ghp_ref, pq_ref, outw_ref, outp_ref, pqa_ref, send, recv, p_send, p_recv):
        del ghw_ref, ghp_ref
        x, y, c, chips = _place()
        q = 2 * x + y
        refs = (outw_ref, outp_ref)
        n_rows = (2 * outw_ref.shape[1], 2 * outp_ref.shape[1])
        pair = [_remote(refs[a].at[c, rows], refs[a].at[c, rows], send.at[i], recv.at[i], (x, y, 1 - c))
                for i, (a, rows) in enumerate(_half_chunks(0, n_rows, 8))]
        pqa_ref[q] = pq_ref[...]
        small = [_remote(pq_ref, pqa_ref.at[q], p_send.at[j], p_recv.at[j], (*chips[j], c)) for j in range(3)]
        for cp in pair + small:
            cp.start()
        for i, (a, rows) in enumerate(_half_chunks(0, n_rows, 8)):
            _remote(refs[a].at[1 - c, rows], refs[a].at[1 - c, rows], send.at[i], recv.at[i], (x, y, 1 - c)).wait_recv()
        for cp in small:
            cp.wait_recv()
        for cp in pair + small:
            cp.wait_send()

    return _pallas(
        body, name="pair_share", in_specs=[ANY, ANY, VMEM], out_specs=[ANY, ANY, VMEM],
        out_shape=[jax.ShapeDtypeStruct(ghw.shape, F32), jax.ShapeDtypeStruct(ghp.shape, F32),
                   jax.ShapeDtypeStruct((N_CHIPS, 8, D), F32)],
        scratch_shapes=[pltpu.SemaphoreType.DMA((N_BULK,)), pltpu.SemaphoreType.DMA((N_BULK,)),
                        pltpu.SemaphoreType.DMA((3,)), pltpu.SemaphoreType.DMA((3,))],
        input_output_aliases={0: 0, 1: 1},
        compiler_params=_params(),
    )(ghw, ghp, pq)


def _rows_of(shape):
    size = 1
    for s in shape:
        size *= s
    return -(-size // D)


def _pack(arrs, rows_multiple=8):
    parts = []
    total = 0
    for a in arrs:
        f = a.reshape(-1).astype(F32)
        r = _rows_of(a.shape)
        parts.append(jnp.pad(f, (0, r * D - f.shape[0])))
        total += r
    pad_rows = (-total) % rows_multiple
    if pad_rows:
        parts.append(jnp.zeros((pad_rows * D,), F32))
    return jnp.concatenate(parts).reshape(-1, D)


def _unpack(p, shapes):
    out = []
    r0 = 0
    for shp in shapes:
        r = _rows_of(shp)
        size = 1
        for s in shp:
            size *= s
        out.append(p[r0:r0 + r].reshape(-1)[:size].reshape(shp))
        r0 += r
    return out


WEIGHT_NAMES = ['c_ctx', 'ada_w', 'ada_b', 'norm_g', 'w_in', 'b_in', 'conv_w', 'conv_b', 'conv_ln_g', 'conv_ln_b',
                'conv_proj', 'decay_up_fwd', 'decay_bias_fwd', 'decay_up_bwd', 'decay_bias_bwd', 'gla_norm_g', 'gla_proj',
                'w_out', 'final_norm_g']
SMALL_NAMES = ['c_ctx', 'ada_b', 'norm_g', 'b_in', 'conv_w', 'conv_b', 'conv_ln_g', 'conv_ln_b', 'decay_up_fwd',
               'decay_bias_fwd', 'decay_up_bwd', 'decay_bias_bwd', 'gla_norm_g', 'final_norm_g']


def kernel(x, c, ctx, c_ctx, ada_w, ada_b, norm_g, w_in, b_in, conv_w, conv_b, conv_ln_g, conv_ln_b, conv_proj, decay_up_fwd, decay_bias_fwd, decay_up_bwd, decay_bias_bwd, gla_norm_g, gla_proj, w_out, final_norm_g, loss_target, m_c_ctx, m_ada_w, m_ada_b, m_norm_g, m_w_in, m_b_in, m_conv_w, m_conv_b, m_conv_ln_g, m_conv_ln_b, m_conv_proj, m_decay_up_fwd, m_decay_bias_fwd, m_decay_up_bwd, m_decay_bias_bwd, m_gla_norm_g, m_gla_proj, m_w_out, m_final_norm_g, v_c_ctx, v_ada_w, v_ada_b, v_norm_g, v_w_in, v_b_in, v_conv_w, v_conv_b, v_conv_ln_g, v_conv_ln_b, v_conv_proj, v_decay_up_fwd, v_decay_bias_fwd, v_decay_up_bwd, v_decay_bias_bwd, v_gla_norm_g, v_gla_proj, v_w_out, v_final_norm_g):
    w = dict(c_ctx=c_ctx, ada_w=ada_w, ada_b=ada_b, norm_g=norm_g, w_in=w_in, b_in=b_in, conv_w=conv_w, conv_b=conv_b,
             conv_ln_g=conv_ln_g, conv_ln_b=conv_ln_b, conv_proj=conv_proj, decay_up_fwd=decay_up_fwd,
             decay_bias_fwd=decay_bias_fwd, decay_up_bwd=decay_up_bwd, decay_bias_bwd=decay_bias_bwd,
             gla_norm_g=gla_norm_g, gla_proj=gla_proj, w_out=w_out, final_norm_g=final_norm_g)
    m = dict(c_ctx=m_c_ctx, ada_w=m_ada_w, ada_b=m_ada_b, norm_g=m_norm_g, w_in=m_w_in, b_in=m_b_in, conv_w=m_conv_w,
             conv_b=m_conv_b, conv_ln_g=m_conv_ln_g, conv_ln_b=m_conv_ln_b, conv_proj=m_conv_proj,
             decay_up_fwd=m_decay_up_fwd, decay_bias_fwd=m_decay_bias_fwd, decay_up_bwd=m_decay_up_bwd,
             decay_bias_bwd=m_decay_bias_bwd, gla_norm_g=m_gla_norm_g, gla_proj=m_gla_proj, w_out=m_w_out,
             final_norm_g=m_final_norm_g)
    v = dict(c_ctx=v_c_ctx, ada_w=v_ada_w, ada_b=v_ada_b, norm_g=v_norm_g, w_in=v_w_in, b_in=v_b_in, conv_w=v_conv_w,
             conv_b=v_conv_b, conv_ln_g=v_conv_ln_g, conv_ln_b=v_conv_ln_b, conv_proj=v_conv_proj,
             decay_up_fwd=v_decay_up_fwd, decay_bias_fwd=v_decay_bias_fwd, decay_up_bwd=v_decay_up_bwd,
             decay_bias_bwd=v_decay_bias_bwd, gla_norm_g=v_gla_norm_g, gla_proj=v_gla_proj, w_out=v_w_out,
             final_norm_g=v_final_norm_g)
    n = x.shape[0]
    ax, ay, ac = lax.axis_index("x"), lax.axis_index("y"), lax.axis_index("c")
    q = 2 * ax + ay
    dev = 4 * ax + 2 * ay + ac
    nsh = ada_w.shape[2]

    w_sh = w_in[0].astype(BF16)
    p_sh = jnp.concatenate([conv_proj[0], gla_proj[0], w_out[0]], 0).astype(BF16)
    fp = _pack([conv_w[0], decay_up_fwd[0], decay_up_bwd[0]], F_ROWS)
    c8 = jnp.pad(c, ((0, 8 - n), (0, 0)))
    cctx8 = jnp.pad(c_ctx[None], ((0, 7), (0, 0)))
    adab_sh = lax.dynamic_slice(ada_b, (0, q * nsh), (1, nsh))
    w_all, fall, call, mall = gather_weights(c8, cctx8, ada_w[0], adab_sh, w_sh, fp)

    mod_all = jnp.transpose(mall, (1, 0, 2)).reshape(MOD_ROWS, 3 * D)
    mod_mine = lax.dynamic_slice(mod_all, (8 * dev, 0), (n, 3 * D))
    mod_ctx = mod_all[64:65]
    shift = jnp.concatenate([mod_mine[:, 0:D], mod_ctx[:, 0:D]], 0)[:, None, :]
    scale1 = 1.0 + jnp.concatenate([mod_mine[:, D:2 * D], mod_ctx[:, D:2 * D]], 0)[:, None, :]
    gate = mod_mine[:, 2 * D:3 * D][:, None, :]

    own = lambda i, mine, got: jnp.where(q == i, mine, got)
    g1, g2, g3, g4, g5 = _group_cols(jnp.concatenate([own(i, w_sh, w_all[i]) for i in range(N_CHIPS)], 1))
    wts = dict(w1=g1, w2=g2, w3=g3, w4=g4, w5=g5)
    f_parts = [_unpack(fall[i], [conv_w.shape[1:], decay_up_fwd.shape[1:], decay_up_bwd.shape[1:]]) for i in range(N_CHIPS)]
    conv_w_full = jnp.concatenate([p[0] for p in f_parts], 1)
    upf_full = jnp.concatenate([p[1] for p in f_parts], 1)
    upb_full = jnp.concatenate([p[2] for p in f_parts], 1)
    b1, b2, b3, b4, b5 = _group_cols(b_in)
    small = dict(b1=b1, b2=b2, b3=b3, b4=b4, b5=b5, norm_g=norm_g,
                 conv_w=jnp.pad(conv_w_full, ((0, 1), (0, 0))), conv_b=conv_b, conv_ln_g=conv_ln_g, conv_ln_b=conv_ln_b,
                 upf=_split3(_pad_up(upf_full, 0)), upb=_split3(_pad_up(upb_full, 16)),
                 bias_f=decay_bias_fwd, bias_b=decay_bias_bwd,
                 gla_norm_g=gla_norm_g, final_norm_g=final_norm_g[None])

    core = ac.astype(jnp.int32).reshape(1)
    chip = q.astype(jnp.int32).reshape(1)
    loss_part, dh, dps, g, got, (pap16, rbp), (sall1, early_shapes) = local_step(
        x, ctx, loss_target, (scale1, shift, gate), wts, small, p_sh, q, core)

    gs = [g["w%d" % i] for i in range(1, 6)]
    done = [0, 1, 3, 4]
    *sums, got[2] = pair_add_groups(core, [gs[i] for i in done], [got[i] for i in done], tr=128, swap=[gs[2]])
    halves = dict(zip(done, sums))
    halves[2] = pair_add(core, gs[2][None], got[2][None], name="pair_add_w3", tr=128)[0]
    halves = [halves[i] for i in range(5)]
    paw16 = _ungroup_to_shards(halves)
    grad_x2, dshift, dscale, g["norm_g"], rbw = dgrad_norm_bwd(
        dps, [wts["w%d" % i] for i in range(1, 6)], paw16, x.reshape(n * SEQ, D), ctx.reshape(n * NCTX, D), dh,
        scale1, norm_g, tm=256)

    dm_mine = jnp.concatenate([dshift[:n, 0], dscale[:n, 0], g["gate"][:, 0]], -1)
    dm_ctx = jnp.concatenate([dshift[n, 0], dscale[n, 0], jnp.zeros((D,), F32)], -1)
    d_b_in = _ungroup_cols(*[g["b%d" % i] for i in range(1, 6)])
    late = [g["norm_g"], dm_mine, dm_ctx, d_b_in, loss_part[0, 0:1]]
    late_shapes = [a.shape for a in late]
    sall2 = gather_small(_pack(late))
    sum_early, sum_late = sum_devices([sall1, sall2])
    (s_conv_b, s_ln_g, s_ln_b, s_bias_f, s_bias_b, s_gla_g, s_final_g, s_conv_w, s_upf,
     s_upb) = _unpack(sum_early, early_shapes)
    s_late = _unpack(sum_late, late_shapes)
    s_norm_g, s_b_in, loss = s_late[0], s_late[3], s_late[4][0]
    r_mine, r_ctx = 1, 1 + 3 * n
    dm_all = sall2[:, r_mine:r_ctx].reshape(N_DEV, n, 3 * D)
    dm_full = jnp.concatenate([jnp.pad(dm_all, ((0, 0), (0, 8 - n), (0, 0))).reshape(8 * N_DEV, 3 * D),
                               sall2[:, r_ctx:r_ctx + 3].reshape(N_DEV, 3 * D)], 0)
    dm_shard = lax.dynamic_slice(dm_full, (0, q * nsh), (MOD_ROWS, nsh))
    cctx_rows = jnp.broadcast_to(c_ctx[None], (8, D))
    g_ada_w, g_ada_b, pq = ada_bwd(call, cctx_rows, dm_shard, dm_full, ada_w[0])

    place = jnp.concatenate([chip, core])
    ghw, ghp = chip_add(place, [paw16, pap16], [rbw, rbp], steps=4)
    gw_mine, gp_mine, pq_all = pair_share(ghw, ghp, pq)
    gp_mine = gp_mine.reshape(768, D)
    g_c_ctx = cctx_grad(pq_all, cctx_rows)[0]

    grads = dict(
        c_ctx=g_c_ctx, ada_w=g_ada_w[None], ada_b=g_ada_b, norm_g=s_norm_g,
        w_in=gw_mine.reshape(1, D, W_IN_SHARD), b_in=s_b_in,
        conv_w=lax.dynamic_slice(s_conv_w, (0, q * 256), (CONV_K, 256))[None], conv_b=s_conv_b,
        conv_ln_g=s_ln_g, conv_ln_b=s_ln_b, conv_proj=gp_mine[0:256][None],
        decay_up_fwd=lax.dynamic_slice(s_upf, (0, q * 128), (16, 128))[None], decay_bias_fwd=s_bias_f,
        decay_up_bwd=lax.dynamic_slice(s_upb, (0, q * 128), (16, 128))[None], decay_bias_bwd=s_bias_b,
        gla_norm_g=s_gla_g, gla_proj=gp_mine[256:512][None], w_out=gp_mine[512:768][None],
        final_norm_g=s_final_g[0])

    delta, new_m, new_v = {}, {}, {}
    tr_ = lambda a: jnp.swapaxes(a, 1, 2)
    g_w_in_t = tr_(grads["w_in"])
    grads["w_in"] = tr_(g_w_in_t)
    d_, m_, v_ = adamw2d(tr_(w_in), g_w_in_t, tr_(m_w_in), tr_(v_w_in), name="adamw_w_in", tr=W_IN_SHARD, tcols=128)
    delta["w_in"], new_m["w_in"], new_v["w_in"] = tr_(d_), tr_(m_), tr_(v_)
    rest = SMALL_NAMES + ["conv_proj", "gla_proj", "w_out", "ada_w"]
    d_, m_, v_ = adamw_many([w[nm] for nm in rest], [grads[nm].reshape(w[nm].shape) for nm in rest],
                            [m[nm] for nm in rest], [v[nm] for nm in rest])
    for nm, a, b, cc in zip(rest, d_, m_, v_):
        delta[nm], new_m[nm], new_v[nm] = a, b, cc

    grad_x = grad_x2.reshape(x.shape)
    return (loss, grad_x, *[grads[nm].reshape(w[nm].shape) for nm in WEIGHT_NAMES], *[delta[nm] for nm in WEIGHT_NAMES],
            *[new_m[nm] for nm in WEIGHT_NAMES], *[new_v[nm] for nm in WEIGHT_NAMES])
```

```python
import jax
import jax.numpy as jnp
from jax import lax
from jax.experimental import pallas as pl
from jax.experimental.pallas import tpu as pltpu

F32 = jnp.float32
BF16 = jnp.bfloat16
MESH = pl.DeviceIdType.MESH
HI = lax.Precision.HIGHEST

D = 1024
SEQ = 2048
GRID_W = 64
GRID_H = SEQ // GRID_W
NCTX = 256
SEQ_ALL = SEQ + NCTX
EPS = 1e-6
CONV_K = 31
CONV_PAD = CONV_K // 2
HEADS = 4
HEAD_K = 128
HEAD_V = 256
GLA_DK = HEADS * HEAD_K
GATE_TAU = 16.0
Q_SCALE = HEAD_K ** -0.5
CHUNK = 64
NCHUNK = SEQ_ALL // CHUNK
NCHUNK_LAT = SEQ // CHUNK
NCHUNK_CTX = NCHUNK - NCHUNK_LAT
SUB = 64
NSUB = CHUNK // SUB
N_IN = 8224
W3 = 2176
O3_V, O3_Q, O3_K, O3_AB = 0, 1024, 1536, 2048

ADAM_LR, ADAM_B1, ADAM_B2, ADAM_EPS, ADAM_WD, ADAM_STEP = 0.001, 0.9, 0.999, 1e-08, 0.01, 10
VMEM_LIMIT = 56 * 1024 * 1024

N_CHIPS = 4
N_DEV = 8
W_IN_SHARD = N_IN // N_CHIPS
MOD_ROWS = 72


def _pallas(body, **kw):
    return pl.pallas_call(body, **kw)


def _params(sem=None, **kw):
    if sem is not None:
        kw["dimension_semantics"] = sem
    return pltpu.CompilerParams(vmem_limit_bytes=VMEM_LIMIT, **kw)


def _sigmoid(v):
    return 1.0 / (1.0 + jnp.exp(-v))


def _silu(v):
    return v * _sigmoid(v)


def _dsilu(v):
    s = _sigmoid(v)
    return s * (1.0 + v * (1.0 - s))


def _log_sigmoid(v):
    return jnp.minimum(v, 0.0) - jnp.log(1.0 + jnp.exp(-jnp.abs(v)))


def _dot(a, b, dims, precision=None):
    return lax.dot_general(a, b, (dims, ((), ())), preferred_element_type=F32, precision=precision)


def _nn(a, b, precision=None):
    return _dot(a, b, ((1,), (0,)), precision)


def _nt(a, b, precision=None):
    return _dot(a, b, ((1,), (1,)), precision)


def _tn(a, b, precision=None):
    return _dot(a, b, ((0,), (0,)), precision)


def _b16(v):
    return v.astype(BF16)


def proj_all(u, ws, bs, rows, p_sh, *, tm):
    k = u.shape[1]
    n_g = len(ws)
    tns = [w.shape[1] if w.shape[1] % 1024 else 1024 for w in ws]
    mts = [r // tm for r in rows]
    cnts = [(w.shape[1] // tn) * mt for w, tn, mt in zip(ws, tns, mts)]
    los = [sum(cnts[:g]) for g in range(n_g)]
    n_steps = sum(cnts)

    def rel(s, g):
        return jnp.clip(s - los[g], 0, cnts[g] - 1)

    def active(s, g):
        return (s >= los[g]) & (s < los[g] + cnts[g])

    def u_row(s):
        r = 0
        for g in range(n_g):
            r = r + jnp.where(active(s, g), rel(s, g) % mts[g], 0)
        return r

    def body(*refs):
        u_ref = refs[0]
        w_refs, b_refs = refs[1:1 + n_g], refs[1 + n_g:1 + 2 * n_g]
        p_ref = refs[1 + 2 * n_g]
        o_refs = refs[2 + 2 * n_g:2 + 3 * n_g]
        pall_ref = refs[2 + 3 * n_g]
        w_send, w_recv, h_send, h_recv = refs[3 + 3 * n_g:]
        s = pl.program_id(0)
        for g in range(n_g):
            @pl.when(active(s, g))
            def _(g=g):
                o_refs[g][...] = (_nn(u_ref[...], w_refs[g][...]) + b_refs[g][...]).astype(o_refs[g].dtype)

        x, y, c, chips = _place()
        q = 2 * x + y
        mine = _half_chunks(c, (0, p_ref.shape[0]), 16, which=(1,))
        other = _half_chunks(1 - c, (0, p_ref.shape[0]), 16, which=(1,))
        nb = len(mine)

        def bulk():
            return [[_remote(p_ref.at[rws], pall_ref.at[q, rws], w_send.at[pj * nb + pi], w_recv.at[pj * nb + pi],
                             (*chips[pj], c)) for pi, (_, rws) in enumerate(mine)] for pj in range(3)]

        @pl.when(s == 0)
        def _():
            for cp in sum(bulk(), []):
                cp.start()

        @pl.when(s == n_steps - 1)
        def _():
            handed = []
            for pj, (cx, cy) in enumerate(chips):
                for pi, (_, rws) in enumerate(mine):
                    bulk()[pj][pi].wait_recv()
                    cp = _remote(pall_ref.at[2 * cx + cy, rws], pall_ref.at[2 * cx + cy, rws],
                                 h_send.at[pj * nb + pi], h_recv.at[pj * nb + pi], (x, y, 1 - c))
                    cp.start()
                    handed.append(cp)
            for pj, (cx, cy) in enumerate(chips):
                for pi, (_, rws) in enumerate(other):
                    _remote(pall_ref.at[2 * cx + cy, rws], pall_ref.at[2 * cx + cy, rws],
                            h_send.at[pj * nb + pi], h_recv.at[pj * nb + pi], (x, y, 1 - c)).wait_recv()
            for cp in sum(bulk(), []) + handed:
                cp.wait_send()

    any_spec = pl.BlockSpec(memory_space=pl.ANY)
    in_specs = [pl.BlockSpec((tm, k), lambda s: (u_row(s), 0))]
    in_specs += [pl.BlockSpec((k, tns[g]), lambda s, g=g: (0, rel(s, g) // mts[g])) for g in range(n_g)]
    in_specs += [pl.BlockSpec((1, tns[g]), lambda s, g=g: (0, rel(s, g) // mts[g])) for g in range(n_g)]
    in_specs.append(any_spec)
    out_specs = [pl.BlockSpec((tm, tns[g]), lambda s, g=g: (rel(s, g) % mts[g], rel(s, g) // mts[g])) for g in range(n_g)]
    out_specs.append(any_spec)
    out_shape = [jax.ShapeDtypeStruct((rows[g], ws[g].shape[1]), BF16) for g in range(n_g)]
    out_shape.append(jax.ShapeDtypeStruct((N_CHIPS,) + p_sh.shape, p_sh.dtype))
    return _pallas(
        body, name="proj_all", grid=(n_steps,), in_specs=in_specs, out_specs=out_specs, out_shape=out_shape,
        scratch_shapes=[pltpu.SemaphoreType.DMA((3 * P_ROW_CHUNKS,)) for _ in range(4)],
        compiler_params=_params(("arbitrary",)),
    )(u, *ws, *bs, p_sh)


def matmul_tn(a, b, *, name, t, tn, tt, colsum=False, swap=None):
    m = a.shape[1]
    n = b.shape[1]
    nj, ns = n // tn, t // tt
    n_out = 2 if colsum else 1
    n_sw = 0 if swap is None else len(swap[0])
    n_ex = 0 if swap is None else n_sw + 1

    def body(a_ref, b_ref, *rest):
        o_ref = rest[n_ex]
        cs_ref = rest[n_ex + 1] if colsum else None
        j, s = pl.program_id(0), pl.program_id(1)

        if swap is not None:
            g_refs, sm_ref = rest[:n_sw], rest[n_sw]
            got_refs, sall_ref = rest[n_ex + n_out:n_ex + n_out + n_sw], rest[n_ex + n_out + n_sw]
            a_send, a_recv, s_send, s_recv, l_sem = rest[2 * n_ex + n_out:]

            def copies():
                x, y, c, _ = _place()
                dev = 4 * x + 2 * y + c
                small = [_remote(sm_ref, sall_ref.at[dev], s_send.at[r], s_recv.at[r], peer)
                         for r, peer in enumerate(_all_peers(x, y, c))]
                return (_pair_copies(g_refs, None, got_refs, None, a_send, a_recv) + small,
                        pltpu.make_async_copy(sm_ref, sall_ref.at[dev], l_sem))

            @pl.when((j == 0) & (s == 0))
            def _():
                remote, own = copies()
                own.start()
                for cp in remote:
                    cp.start()

            @pl.when((j == nj - 1) & (s == ns - 1))
            def _():
                remote, own = copies()
                for cp in remote:
                    cp.wait_recv()
                for cp in remote:
                    cp.wait_send()
                own.wait()

        @pl.when(s == 0)
        def _():
            o_ref[...] = jnp.zeros_like(o_ref)
            if colsum:
                cs_ref[...] = jnp.zeros_like(cs_ref)
        o_ref[...] += _tn(a_ref[...], b_ref[...])
        if colsum:
            cs_ref[...] += jnp.sum(b_ref[...].astype(F32), axis=0, keepdims=True)

    in_specs = [pl.BlockSpec((tt, m), lambda j, s: (s, 0)), pl.BlockSpec((tt, tn), lambda j, s: (s, j))]
    out_specs = [pl.BlockSpec((m, tn), lambda j, s: (0, j))]
    out_shape = [jax.ShapeDtypeStruct((m, n), F32)]
    if colsum:
        out_specs.append(pl.BlockSpec((1, tn), lambda j, s: (0, j)))
        out_shape.append(jax.ShapeDtypeStruct((1, n), F32))
    args, scratch = [a, b], []
    if swap is not None:
        gs, sm = swap
        any_spec = pl.BlockSpec(memory_space=pl.ANY)
        in_specs += [any_spec] * n_ex
        out_specs += [any_spec] * n_ex
        out_shape += _pair_got_shapes(gs, None) + [jax.ShapeDtypeStruct((N_DEV,) + sm.shape, F32)]
        args += [*gs, sm]
        scratch = [pltpu.SemaphoreType.DMA((_pair_count(gs, None),)), pltpu.SemaphoreType.DMA((_pair_count(gs, None),)),
                   pltpu.SemaphoreType.DMA((N_DEV - 1,)), pltpu.SemaphoreType.DMA((N_DEV - 1,)),
                   pltpu.SemaphoreType.DMA(())]
    return _pallas(
        body, name=name, grid=(nj, ns), in_specs=in_specs, out_specs=out_specs, out_shape=out_shape,
        scratch_shapes=scratch,
        compiler_params=_params(("parallel" if swap is None else "arbitrary", "arbitrary")),
    )(*args)


def dgrad_norm_bwd(dps, wts, paw, x2, ctx2, dh, scale1, norm_g, *, tm):
    t, tc = x2.shape[0], ctx2.shape[0]
    t_all = t + tc
    n_lat, n_ctx = t // tm, tc // tm
    n_tiles = n_lat + n_ctx
    n_samples = scale1.shape[0] - 1
    tps = n_lat // n_samples
    n_grp = n_samples + 1
    n_g = len(dps)
    whole = [g for g in range(n_g) if dps[g].shape[0] == t_all]
    latent = [g for g in range(n_g) if dps[g].shape[0] != t_all]

    def body(*refs):
        dp_refs, w_refs = refs[:n_g], refs[n_g:2 * n_g]
        (paw_ref, x_ref, c_ref, dh_ref, sc_ref, g_ref, dx_ref, dsh_ref, dsc_ref, dg_ref, rbw_ref,
         du_buf, b_send, b_recv) = refs[2 * n_g:]
        i = pl.program_id(0)

        def exchange():
            x, y, c, chips = _place()
            chunks = _half_chunks(0, (2 * paw_ref.shape[1],), 16, which=(0,))
            return [_remote(paw_ref.at[2 * cx + cy, rows], rbw_ref.at[j, rows],
                            b_send.at[j * N_BULK + k], b_recv.at[j * N_BULK + k], (cx, cy, c))
                    for j, (cx, cy) in enumerate(chips) for k, (_, rows) in enumerate(chunks)]

        @pl.when(i == 0)
        def _():
            for cp in exchange():
                cp.start()

        acc = None
        for g in whole:
            part = _nt(dp_refs[g][...], w_refs[g][...])
            acc = part if acc is None else acc + part
        du_buf[...] = acc

        @pl.when(i < n_lat)
        def _():
            lat = None
            for g in latent:
                part = _nt(dp_refs[g][...], w_refs[g][...])
                lat = part if lat is None else lat + part
            du_buf[...] += lat

        duv = du_buf[...]
        xv = jnp.where(i < n_lat, x_ref[...], c_ref[...])
        rs = lax.rsqrt(jnp.mean(xv * xv, axis=-1, keepdims=True) + EPS)
        xh = xv * rs
        n = xh * g_ref[...]
        dn = duv * sc_ref[0]
        dxh = dn * g_ref[...]
        dx = rs * (dxh - xh * jnp.mean(dxh * xh, axis=-1, keepdims=True))

        @pl.when(i < n_lat)
        def _():
            dx_ref[...] = dx + dh_ref[...]

        @pl.when((i % tps == 0) & (i <= n_lat))
        def _():
            dsh_ref[...] = jnp.zeros_like(dsh_ref)
            dsc_ref[...] = jnp.zeros_like(dsc_ref)

        @pl.when(i == 0)
        def _():
            dg_ref[...] = jnp.zeros_like(dg_ref)

        dsh_ref[0] += jnp.sum(duv, axis=0, keepdims=True)
        dsc_ref[0] += jnp.sum(duv * n, axis=0, keepdims=True)
        dg_ref[...] += jnp.sum(dn * xh, axis=0, keepdims=True)

        @pl.when(i == n_tiles - 1)
        def _():
            for cp in exchange():
                cp.wait_recv()
            for cp in exchange():
                cp.wait_send()

    lat = lambda i: (jnp.minimum(i, n_lat - 1), 0)
    grp = lambda i: (jnp.minimum(i // tps, n_samples), 0, 0)
    in_specs = []
    for g, dp in enumerate(dps):
        nrow = dp.shape[0] // tm
        in_specs.append(pl.BlockSpec((tm, dp.shape[1]), lambda i, nrow=nrow: (jnp.minimum(i, nrow - 1), 0)))
    for w in wts:
        in_specs.append(pl.BlockSpec(w.shape, lambda i: (0, 0), pipeline_mode=pl.Buffered(1)))
    any_spec = pl.BlockSpec(memory_space=pl.ANY)
    in_specs += [any_spec,
                 pl.BlockSpec((tm, D), lat), pl.BlockSpec((tm, D), lambda i: (jnp.maximum(i - n_lat, 0), 0)),
                 pl.BlockSpec((tm, D), lat), pl.BlockSpec((1, 1, D), grp), pl.BlockSpec((1, D), lambda i: (0, 0))]
    return _pallas(
        body, name="dgrad_norm_bwd", grid=(n_tiles,), in_specs=in_specs,
        out_specs=[pl.BlockSpec((tm, D), lat), pl.BlockSpec((1, 1, D), grp), pl.BlockSpec((1, 1, D), grp),
                   pl.BlockSpec((1, D), lambda i: (0, 0)), any_spec],
        out_shape=[jax.ShapeDtypeStruct((t, D), F32), jax.ShapeDtypeStruct((n_grp, 1, D), F32),
                   jax.ShapeDtypeStruct((n_grp, 1, D), F32), jax.ShapeDtypeStruct((1, D), F32),
                   jax.ShapeDtypeStruct((3,) + paw.shape[1:], paw.dtype)],
        scratch_shapes=[pltpu.VMEM((tm, D), F32), pltpu.SemaphoreType.DMA((3 * N_BULK,)),
                        pltpu.SemaphoreType.DMA((3 * N_BULK,))],
        compiler_params=_params(("arbitrary",)),
    )(*dps, *wts, paw, x2, ctx2, dh, scale1, norm_g)


TM_NORM = 512


def norm_mod_fwd(x2, ctx2, scale1, shift, norm_g):
    t = x2.shape[0]
    n_lat = t // TM_NORM
    assert ctx2.shape[0] == TM_NORM
    n_samples = scale1.shape[0] - 1
    tps = n_lat // n_samples

    def body(x_ref, c_ref, sc_ref, sh_ref, g_ref, u_ref):
        i = pl.program_id(0)
        xv = jnp.where(i < n_lat, x_ref[...], c_ref[...])
        rs = lax.rsqrt(jnp.mean(xv * xv, axis=-1, keepdims=True) + EPS)
        u = xv * rs * g_ref[...] * sc_ref[0] + sh_ref[0]
        u_ref[...] = u.astype(u_ref.dtype)

    grp = lambda i: (jnp.minimum(i // tps, n_samples), 0, 0)
    return _pallas(
        body, name="norm_mod_fwd", grid=(n_lat + 1,),
        in_specs=[pl.BlockSpec((TM_NORM, D), lambda i: (jnp.minimum(i, n_lat - 1), 0)),
                  pl.BlockSpec((TM_NORM, D), lambda i: (0, 0)),
                  pl.BlockSpec((1, 1, D), grp), pl.BlockSpec((1, 1, D), grp),
                  pl.BlockSpec((1, D), lambda i: (0, 0))],
        out_specs=pl.BlockSpec((TM_NORM, D), lambda i: (i, 0)),
        out_shape=jax.ShapeDtypeStruct((t + TM_NORM, D), BF16),
        compiler_params=_params(("parallel",)),
    )(x2, ctx2, scale1, shift, norm_g)


CONV_CB = 256
CONV_NCB = D // CONV_CB
H_OFF = 16


H_CB = 128
H_SPAN = GRID_W + 2 * H_OFF - 8


def _conv_scratch(vertical):
    if vertical:
        return [pltpu.VMEM((GRID_H + 2 * CONV_PAD, GRID_W, CONV_CB), F32)]
    return [pltpu.VMEM((GRID_H, GRID_W + 2 * H_OFF, H_CB), F32), pltpu.VMEM((7, GRID_H, H_SPAN, H_CB), F32)]


def _conv_fill(bufs, img, vertical):
    pad_ref = bufs[0]
    pad_ref[...] = jnp.zeros_like(pad_ref)
    if vertical:
        pad_ref[pl.ds(CONV_PAD, GRID_H)] = img
        return
    pad_ref[:, pl.ds(H_OFF, GRID_W), :] = img

    def shift(r, carry):
        for s in range(1, 8):
            bufs[1][s - 1, r] = pad_ref[r, pl.ds(s, H_SPAN), :]
        return carry

    lax.fori_loop(0, GRID_H, shift, 0)


def _conv_window(bufs, k, vertical, r, w0=0, nw=GRID_W, lanes=slice(None)):
    if vertical:
        return bufs[0][r + k, pl.ds(w0, nw), lanes]
    off = H_OFF - CONV_PAD + k
    if off % 8 == 0:
        return bufs[0][r, pl.ds(off + w0, nw), lanes]
    return bufs[1][off % 8 - 1, r, pl.ds(off - off % 8 + w0, nw), lanes]


def _conv_col_blocks(vertical):
    if vertical:
        return [pl.ds(0, CONV_CB)]
    return [pl.ds(i * H_CB, H_CB) for i in range(CONV_CB // H_CB)]


def _rows(r):
    return pl.ds(pl.multiple_of(r * GRID_W, GRID_W), GRID_W)


def conv_fwd(p1, conv_w, conv_b, n_samples):
    t = n_samples * SEQ

    def make(vertical, prev):
        n_buf = len(_conv_scratch(vertical))

        def body(gv_ref, gg_ref, w_ref, b_ref, *rest):
            o_ref, bufs = rest[-1 - n_buf], rest[-n_buf:]
            for cols in _conv_col_blocks(vertical):
                a = gv_ref[:, cols].astype(F32) * _sigmoid(gg_ref[:, cols].astype(F32))
                _conv_fill(bufs, a.reshape(GRID_H, GRID_W, a.shape[-1]), vertical)

                def row(r, carry, cols=cols):
                    acc = jnp.zeros((GRID_W, cols.size), F32) + b_ref[:, cols]
                    for k in range(CONV_K):
                        acc = acc + _conv_window(bufs, k, vertical, r) * w_ref[pl.ds(k, 1), cols]
                    o_ref[_rows(r), cols] = acc
                    return carry

                lax.fori_loop(0, GRID_H, row, 0)

        cb0 = CONV_NCB // 2 if vertical else 0
        in_specs = [pl.BlockSpec((SEQ, CONV_CB), lambda b, j: (b, 2 * (cb0 + j))),
                    pl.BlockSpec((SEQ, CONV_CB), lambda b, j: (b, 2 * (cb0 + j) + 1)),
                    pl.BlockSpec((CONV_K + 1, CONV_CB), lambda b, j: (0, cb0 + j)),
                    pl.BlockSpec((1, CONV_CB), lambda b, j: (0, cb0 + j))]
        args = [p1, p1, conv_w, conv_b]
        aliases = {}
        if prev is not None:
            in_specs.append(pl.BlockSpec(memory_space=pl.ANY))
            args.append(prev)
            aliases = {4: 0}
        return _pallas(
            body, name="conv_fwd_v" if vertical else "conv_fwd_h", grid=(n_samples, CONV_NCB // 2),
            in_specs=in_specs,
            out_specs=pl.BlockSpec((SEQ, CONV_CB), lambda b, j: (b, cb0 + j)),
            out_shape=jax.ShapeDtypeStruct((t, D), F32),
            scratch_shapes=_conv_scratch(vertical),
            input_output_aliases=aliases,
            compiler_params=_params(("parallel", "parallel")),
        )(*args)

    return make(True, make(False, None))


def conv_bwd(p1, daconv, conv_w, n_samples):
    t = n_samples * SEQ

    def make(vertical, prev):
        n_buf = len(_conv_scratch(vertical))

        def body(gv_ref, gg_ref, dy_ref, w_ref, *rest):
            dp_ref, dw_ref, db_ref = rest[-3 - 2 * n_buf - 1:-2 * n_buf - 1]
            a_bufs, d_bufs, da_ref = rest[-2 * n_buf - 1:-n_buf - 1], rest[-n_buf - 1:-1], rest[-1]
            for cols in _conv_col_blocks(vertical):
                width = cols.size
                gv = gv_ref[:, cols].astype(F32)
                sg = _sigmoid(gg_ref[:, cols].astype(F32))
                _conv_fill(a_bufs, (gv * sg).reshape(GRID_H, GRID_W, width), vertical)
                _conv_fill(d_bufs, dy_ref[:, cols].reshape(GRID_H, GRID_W, width), vertical)

                def row(r, carry, cols=cols, width=width):
                    acc = jnp.zeros((GRID_W, width), F32)
                    for k in range(CONV_K):
                        acc = acc + _conv_window(d_bufs, CONV_K - 1 - k, vertical, r) * w_ref[pl.ds(k, 1), cols]
                    da_ref[_rows(r), cols] = acc
                    return carry

                lax.fori_loop(0, GRID_H, row, 0)
                da = da_ref[:, cols]
                dp_ref[:, pl.ds(cols.start, width)] = (da * sg).astype(dp_ref.dtype)
                dp_ref[:, pl.ds(CONV_CB + cols.start, width)] = (da * gv * sg * (1.0 - sg)).astype(dp_ref.dtype)

                for lb in range(width // 128):
                    lanes = pl.ds(lb * 128, 128)
                    dy_lanes = pl.ds(cols.start + lb * 128, 128)

                    def wrow(r, accs, lanes=lanes, dy_lanes=dy_lanes):
                        for w0 in range(0, GRID_W, 8):
                            dyv = dy_ref[pl.ds(pl.multiple_of(r * GRID_W, GRID_W) + w0, 8), dy_lanes]
                            accs = tuple(accs[k] + _conv_window(a_bufs, k, vertical, r, w0, 8, lanes) * dyv
                                         for k in range(CONV_K))
                        return accs

                    accs = lax.fori_loop(0, GRID_H, wrow, tuple(jnp.zeros((8, 128), F32) for _ in range(CONV_K)))
                    for k in range(CONV_K):
                        dw_ref[0, pl.ds(k, 1), dy_lanes] = jnp.sum(accs[k], axis=0, keepdims=True)
            dw_ref[0, pl.ds(CONV_K, 1), :] = jnp.zeros((1, CONV_CB), F32)
            db_ref[0] = jnp.sum(dy_ref[...], axis=0, keepdims=True)

        cb0 = CONV_NCB // 2 if vertical else 0
        in_specs = [pl.BlockSpec((SEQ, CONV_CB), lambda b, j: (b, 2 * (cb0 + j))),
                    pl.BlockSpec((SEQ, CONV_CB), lambda b, j: (b, 2 * (cb0 + j) + 1)),
                    pl.BlockSpec((SEQ, CONV_CB), lambda b, j: (b, cb0 + j)),
                    pl.BlockSpec((CONV_K + 1, CONV_CB), lambda b, j: (0, cb0 + j))]
        args = [p1, p1, daconv, conv_w]
        aliases = {}
        if prev is not None:
            in_specs += [pl.BlockSpec(memory_space=pl.ANY)] * 3
            args += list(prev)
            aliases = {4: 0, 5: 1, 6: 2}
        return _pallas(
            body, name="conv_bwd_v" if vertical else "conv_bwd_h", grid=(n_samples, CONV_NCB // 2),
            in_specs=in_specs,
            out_specs=[pl.BlockSpec((SEQ, 2 * CONV_CB), lambda b, j: (b, cb0 + j)),
                       pl.BlockSpec((1, CONV_K + 1, CONV_CB), lambda b, j: (b, 0, cb0 + j)),
                       pl.BlockSpec((1, 1, CONV_CB), lambda b, j: (b, 0, cb0 + j))],
            out_shape=[jax.ShapeDtypeStruct((t, 2 * D), BF16),
                       jax.ShapeDtypeStruct((n_samples, CONV_K + 1, D), F32),
                       jax.ShapeDtypeStruct((n_samples, 1, D), F32)],
            scratch_shapes=_conv_scratch(vertical) + _conv_scratch(vertical) + [pltpu.VMEM((SEQ, CONV_CB), F32)],
            input_output_aliases=aliases,
            compiler_params=_params(("parallel", "parallel")),
        )(*args)

    return make(True, make(False, None))


TM_EW = 512


def ln_gate_proj(aconv, z, ln_g, ln_b, conv_proj):
    t = aconv.shape[0]

    def body(a_ref, z_ref, g_ref, b_ref, w_ref, o_ref, y_ref):
        a = a_ref[...]
        mu = jnp.mean(a, axis=-1, keepdims=True)
        xc = a - mu
        rstd = lax.rsqrt(jnp.mean(xc * xc, axis=-1, keepdims=True) + EPS)
        l = xc * rstd * g_ref[...] + b_ref[...]
        ac = _b16(_silu(l) * _silu(z_ref[...].astype(F32)))
        o_ref[...] = ac
        y_ref[...] = _nn(ac, w_ref[...]).astype(y_ref.dtype)

    row = pl.BlockSpec((TM_OUT, D), lambda i: (i, 0))
    vec = pl.BlockSpec((1, D), lambda i: (0, 0))
    return _pallas(
        body, name="ln_gate_proj", grid=(t // TM_OUT,),
        in_specs=[row, row, vec, vec, pl.BlockSpec((D, D), lambda i: (0, 0))], out_specs=[row, row],
        out_shape=[jax.ShapeDtypeStruct((t, D), BF16), jax.ShapeDtypeStruct((t, D), BF16)],
        compiler_params=_params(("parallel",)),
    )(aconv, z, ln_g, ln_b, conv_proj)


def ln_gate_bwd(aconv, z, dyc, conv_proj, ln_g, ln_b, ac):
    t = aconv.shape[0]

    def body(a_ref, z_ref, d_ref, w_ref, g_ref, b_ref, ac_ref, da_ref, dz_ref, dg_ref, db_ref, gw_ref):
        @pl.when(pl.program_id(0) == 0)
        def _():
            gw_ref[...] = jnp.zeros_like(gw_ref)

        gw_ref[...] += _tn(ac_ref[...], d_ref[...])
        a = a_ref[...]
        zv = z_ref[...].astype(F32)
        dac_v = _nt(d_ref[...], w_ref[...])
        mu = jnp.mean(a, axis=-1, keepdims=True)
        xc = a - mu
        rstd = lax.rsqrt(jnp.mean(xc * xc, axis=-1, keepdims=True) + EPS)
        xh = xc * rstd
        l = xh * g_ref[...] + b_ref[...]
        dz_ref[...] = (dac_v * _silu(l) * _dsilu(zv)).astype(dz_ref.dtype)
        dl = dac_v * _silu(zv) * _dsilu(l)
        dxh = dl * g_ref[...]
        da_ref[...] = rstd * (dxh - jnp.mean(dxh, axis=-1, keepdims=True)
                              - xh * jnp.mean(dxh * xh, axis=-1, keepdims=True))

        @pl.when(pl.program_id(0) == 0)
        def _():
            dg_ref[...] = jnp.zeros_like(dg_ref)
            db_ref[...] = jnp.zeros_like(db_ref)

        dg_ref[...] += jnp.sum(dl * xh, axis=0, keepdims=True)
        db_ref[...] += jnp.sum(dl, axis=0, keepdims=True)

    row = pl.BlockSpec((TM_EW, D), lambda i: (i, 0))
    vec = pl.BlockSpec((1, D), lambda i: (0, 0))
    return _pallas(
        body, name="ln_gate_bwd", grid=(t // TM_EW,),
        in_specs=[row, row, row, pl.BlockSpec((D, D), lambda i: (0, 0)), vec, vec, row],
        out_specs=[row, row, vec, vec, pl.BlockSpec((D, D), lambda i: (0, 0))],
        out_shape=[jax.ShapeDtypeStruct((t, D), F32), jax.ShapeDtypeStruct((t, D), BF16),
                   jax.ShapeDtypeStruct((1, D), F32), jax.ShapeDtypeStruct((1, D), F32),
                   jax.ShapeDtypeStruct((D, D), F32)],
        compiler_params=_params(("arbitrary",)),
    )(aconv, z, dyc, conv_proj, ln_g, ln_b, ac)


TM_PREP = 256
PREP_LAT = SEQ // TM_PREP
PREP_ALL = SEQ_ALL // TM_PREP


def _chunk_tri(n, upper):
    r = lax.broadcasted_iota(jnp.int32, (n, n), 0)
    c = lax.broadcasted_iota(jnp.int32, (n, n), 1)
    same = (r // CHUNK) == (c // CHUNK)
    keep = (c >= r) if upper else (c <= r)
    return jnp.where(same & keep, 1.0, 0.0).astype(F32)


def _split3(v):
    hi = v.astype(BF16)
    r1 = v - hi.astype(F32)
    mid = r1.astype(BF16)
    lo = (r1 - mid.astype(F32)).astype(BF16)
    return jnp.stack([hi, mid, lo])


def _chunk_sums(v, upper):
    tri = _chunk_tri(v.shape[0], upper).astype(BF16)
    pieces = _split3(v)
    return (_nn(tri, pieces[0]) + _nn(tri, pieces[1])) + _nn(tri, pieces[2])


def _gate_logits(ab, up3_ref, bias_ref):
    assert ab.dtype == BF16
    return ((_nn(ab, up3_ref[0]) + _nn(ab, up3_ref[1])) + _nn(ab, up3_ref[2])) + bias_ref[...]


def _prep_tile_maps(n_samples):
    n_lat = n_samples * PREP_LAT

    def seq_map(i):
        return jnp.where(i < n_lat, i // PREP_LAT, i - n_lat), jnp.where(i < n_lat, i % PREP_LAT, PREP_LAT)

    return n_lat, seq_map


def gla_prep_fwd(p3, upf, upb, bias_f, bias_b, n_samples):
    n_lat, seq_map = _prep_tile_maps(n_samples)
    n_tiles = n_lat + n_samples

    def body(v_ref, q_ref, k_ref, ab_ref, upf_ref, upb_ref, bf_ref, bb_ref, qo, ko, vo, cf, cb):
        i = pl.program_id(0)
        qo[0] = jnp.where(i < n_lat, q_ref[...].astype(F32) * Q_SCALE, 0.0)
        ko[0] = k_ref[...]
        vo[0] = v_ref[...]
        ab = ab_ref[...]
        gf = _log_sigmoid(_gate_logits(ab, upf_ref, bf_ref)) * (1.0 / GATE_TAU)
        gb = _log_sigmoid(_gate_logits(ab, upb_ref, bb_ref)) * (1.0 / GATE_TAU)
        cf[0] = _chunk_sums(gf, False)
        cb[0] = _chunk_sums(gb, True)

    def o_spec(w):
        return pl.BlockSpec((1, TM_PREP, w), lambda i: (*seq_map(i), 0))

    full = lambda shape: pl.BlockSpec(shape, lambda i: (0,) * len(shape))
    return _pallas(
        body, name="gla_prep_fwd", grid=(n_tiles,),
        in_specs=[pl.BlockSpec((TM_PREP, 1024), lambda i: (i, O3_V // 1024)),
                  pl.BlockSpec((TM_PREP, 512), lambda i: (i, O3_Q // 512)),
                  pl.BlockSpec((TM_PREP, 512), lambda i: (i, O3_K // 512)),
                  pl.BlockSpec((TM_PREP, 128), lambda i: (i, O3_AB // 128)),
                  full((3, 128, GLA_DK)), full((3, 128, GLA_DK)), full((1, GLA_DK)), full((1, GLA_DK))],
        out_specs=[o_spec(GLA_DK), o_spec(GLA_DK), o_spec(D), o_spec(GLA_DK), o_spec(GLA_DK)],
        out_shape=[jax.ShapeDtypeStruct((n_samples, SEQ_ALL, GLA_DK), F32),
                   jax.ShapeDtypeStruct((n_samples, SEQ_ALL, GLA_DK), p3.dtype),
                   jax.ShapeDtypeStruct((n_samples, SEQ_ALL, D), p3.dtype),
                   jax.ShapeDtypeStruct((n_samples, SEQ_ALL, GLA_DK), F32),
                   jax.ShapeDtypeStruct((n_samples, SEQ_ALL, GLA_DK), F32)],
        compiler_params=_params(("parallel",)),
    )(p3, p3, p3, p3, upf, upb, bias_f, bias_b)


def gla_prep_bwd(p3, dq_f, dq_b, dk_f, dk_b, dv_f, dv_b, dc_f, dc_b, upf, upb, bias_f, bias_b, n_samples):
    n_lat, seq_map = _prep_tile_maps(n_samples)
    n_tiles = n_lat + n_samples

    def body(ab_ref, dqf, dqb, dkf, dkb, dvf, dvb, dcf, dcb, upf_ref, upb_ref, bf_ref, bb_ref,
             dp_ref, duf_ref, dub_ref, dbf_ref, dbb_ref):
        i = pl.program_id(0)
        both = lambda a, b: a[0].astype(F32) + b[0].astype(F32)
        dp_ref[:, pl.ds(O3_V, D)] = both(dvf, dvb).astype(dp_ref.dtype)
        dq = jnp.where(i < n_lat, both(dqf, dqb) * Q_SCALE, 0.0)
        dp_ref[:, pl.ds(O3_Q, GLA_DK)] = dq.astype(dp_ref.dtype)
        dp_ref[:, pl.ds(O3_K, GLA_DK)] = both(dkf, dkb).astype(dp_ref.dtype)
        ab = ab_ref[...]
        zf = _gate_logits(ab, upf_ref, bf_ref)
        zb = _gate_logits(ab, upb_ref, bb_ref)
        dgf = _chunk_sums(dcf[0], True)
        dgb = _chunk_sums(dcb[0], False)
        dzf = _b16(dgf * (1.0 / GATE_TAU) * _sigmoid(-zf))
        dzb = _b16(dgb * (1.0 / GATE_TAU) * _sigmoid(-zb))
        dab = _nt(dzf, upf_ref[0]) + _nt(dzb, upb_ref[0])
        dp_ref[:, pl.ds(O3_AB, 128)] = dab.astype(dp_ref.dtype)

        @pl.when(i == 0)
        def _():
            duf_ref[...] = jnp.zeros_like(duf_ref)
            dub_ref[...] = jnp.zeros_like(dub_ref)
            dbf_ref[...] = jnp.zeros_like(dbf_ref)
            dbb_ref[...] = jnp.zeros_like(dbb_ref)

        duf_ref[...] += _tn(ab, dzf)
        dub_ref[...] += _tn(ab, dzb)
        dbf_ref[...] += jnp.sum(dzf.astype(F32), axis=0, keepdims=True)
        dbb_ref[...] += jnp.sum(dzb.astype(F32), axis=0, keepdims=True)

    def s_spec(w):
        return pl.BlockSpec((1, TM_PREP, w), lambda i: (*seq_map(i), 0))

    full = lambda shape: pl.BlockSpec(shape, lambda i: (0,) * len(shape))
    return _pallas(
        body, name="gla_prep_bwd", grid=(n_tiles,),
        in_specs=[pl.BlockSpec((TM_PREP, 128), lambda i: (i, O3_AB // 128)),
                  s_spec(GLA_DK), s_spec(GLA_DK), s_spec(GLA_DK), s_spec(GLA_DK), s_spec(D), s_spec(D),
                  s_spec(GLA_DK), s_spec(GLA_DK),
                  full((3, 128, GLA_DK)), full((3, 128, GLA_DK)), full((1, GLA_DK)), full((1, GLA_DK))],
        out_specs=[pl.BlockSpec((TM_PREP, W3), lambda i: (i, 0)),
                   full((128, GLA_DK)), full((128, GLA_DK)), full((1, GLA_DK)), full((1, GLA_DK))],
        out_shape=[jax.ShapeDtypeStruct((n_tiles * TM_PREP, W3), BF16),
                   jax.ShapeDtypeStruct((128, GLA_DK), F32), jax.ShapeDtypeStruct((128, GLA_DK), F32),
                   jax.ShapeDtypeStruct((1, GLA_DK), F32), jax.ShapeDtypeStruct((1, GLA_DK), F32)],
        compiler_params=_params(("arbitrary",)),
    )(p3, dq_f, dq_b, dk_f, dk_b, dv_f, dv_b, dc_f, dc_b, upf, upb, bias_f, bias_b)


def _sub_blocks(rev):
    if NSUB == 1:
        return [((0, CHUNK), CHUNK // 2, (0, CHUNK))]
    out = []
    for s in range(NSUB):
        rows = (s * SUB, SUB)
        if rev:
            ref = (s + 1) * SUB if s < NSUB - 1 else None
            cols = (s * SUB, CHUNK - s * SUB)
        else:
            ref = s * SUB - 1 if s > 0 else None
            cols = (0, (s + 1) * SUB)
        out.append((rows, ref, cols))
    return out


def _sub_mask(rows, cols, rev):
    r = rows[0] + lax.broadcasted_iota(jnp.int32, (rows[1], cols[1]), 0)
    c = cols[0] + lax.broadcasted_iota(jnp.int32, (rows[1], cols[1]), 1)
    return (c >= r) if rev else (c <= r)


def _sub_operands(qc, kc, cc, rows, ref, cols):
    cref = jnp.zeros((1, HEAD_K), F32) if ref is None else cc[ref:ref + 1]
    eq = jnp.exp(cc[rows[0]:rows[0] + rows[1]] - cref)
    ek = jnp.exp(cref - cc[cols[0]:cols[0] + cols[1]])
    qs = qc[rows[0]:rows[0] + rows[1]] * eq
    kk = kc[cols[0]:cols[0] + cols[1]] * ek
    return qs, kk, eq, ek


SCAN_ROWS = 256
SCAN_CHUNKS = SCAN_ROWS // CHUNK
SCAN_STEPS = SEQ_ALL // SCAN_ROWS
LAT_BLOCKS = SEQ // SCAN_ROWS


def _scan_block(t, rev):
    if rev:
        return SCAN_STEPS - 1 - t
    return jnp.where(t == 0, SCAN_STEPS - 1, t - 1)


def _scan_lat_block(t, rev):
    if rev:
        return jnp.minimum(SCAN_STEPS - 1 - t, LAT_BLOCKS - 1)
    return jnp.maximum(t - 1, 0)


def _head_cols(h):
    return pl.ds(h * HEAD_K, HEAD_K), pl.ds(h * HEAD_V, HEAD_V)


def gla_scan_fwd(q, k, v, cum_f, cum_b):
    n = q.shape[0]

    def body(qf_ref, kf_ref, vf_ref, cf_ref, qb_ref, kb_ref, vb_ref, cb_ref,
             of_ref, sf_ref, sfinf_ref, ob_ref, sb_ref, sfinb_ref, st_f, st_b):
        t = pl.program_id(1)

        @pl.when(t == 0)
        def _():
            st_f[...] = jnp.zeros_like(st_f)
            st_b[...] = jnp.zeros_like(st_b)

        def chunk(j, rev, q_ref, k_ref, v_ref, c_ref, o_ref, s_ref, st):
            lj = SCAN_CHUNKS - 1 - j if rev else j
            r0 = lj * CHUNK
            rws = pl.ds(r0, CHUNK)
            for h in range(HEADS):
                kcols, vcols = _head_cols(h)
                qc, kc, cc = q_ref[0, rws, kcols], k_ref[0, rws, kcols], c_ref[0, rws, kcols]
                vc = v_ref[0, rws, vcols]
                s_in = st[h]
                s_ref[0, h, j] = _b16(s_in)
                edge = cc[0:1] if rev else cc[CHUNK - 1:CHUNK]
                ke = kc * jnp.exp(edge - cc)
                st[h] = s_in * jnp.exp(edge) + _tn(_b16(vc), _b16(ke))
                o_inter = _nt(_b16(qc * jnp.exp(cc)), _b16(s_in))
                vb = _b16(vc)
                for rows, ref, cols in _sub_blocks(rev):
                    qs, kk, _, _ = _sub_operands(qc, kc, cc, rows, ref, cols)
                    a = jnp.where(_sub_mask(rows, cols, rev), _nt(_b16(qs), _b16(kk)), 0.0)
                    o_s = _nn(_b16(a), vb[cols[0]:cols[0] + cols[1]])
                    o_ref[0, pl.ds(r0 + rows[0], rows[1]), vcols] = _b16(o_inter[rows[0]:rows[0] + rows[1]] + o_s)

        for j in range(SCAN_CHUNKS):
            chunk(j, False, qf_ref, kf_ref, vf_ref, cf_ref, of_ref, sf_ref, st_f)
            chunk(j, True, qb_ref, kb_ref, vb_ref, cb_ref, ob_ref, sb_ref, st_b)

        @pl.when(t == SCAN_STEPS - 1)
        def _():
            sfinf_ref[0] = st_f[...]
            sfinb_ref[0] = st_b[...]

    def spec(w, rev):
        return pl.BlockSpec((1, SCAN_ROWS, w), lambda b, t: (b, _scan_block(t, rev), 0))

    def outs(rev):
        return [pl.BlockSpec((1, SCAN_ROWS, D), lambda b, t: (b, _scan_lat_block(t, rev), 0)),
                pl.BlockSpec((1, HEADS, SCAN_CHUNKS, HEAD_V, HEAD_K), lambda b, t: (b, 0, t, 0, 0)),
                pl.BlockSpec((1, HEADS, HEAD_V, HEAD_K), lambda b, t: (b, 0, 0, 0))]

    shapes = [jax.ShapeDtypeStruct((n, SEQ, D), BF16),
              jax.ShapeDtypeStruct((n, HEADS, NCHUNK, HEAD_V, HEAD_K), BF16),
              jax.ShapeDtypeStruct((n, HEADS, HEAD_V, HEAD_K), F32)]
    return _pallas(
        body, name="gla_scan_fwd", grid=(n, SCAN_STEPS),
        in_specs=[spec(w, r) for r in (False, True) for w in (GLA_DK, GLA_DK, D, GLA_DK)],
        out_specs=outs(False) + outs(True), out_shape=shapes + shapes,
        scratch_shapes=[pltpu.VMEM((HEADS, HEAD_V, HEAD_K), F32), pltpu.VMEM((HEADS, HEAD_V, HEAD_K), F32)],
        compiler_params=_params(("parallel", "arbitrary")),
    )(q, k, v, cum_f, q, k, v, cum_b)


def gla_scan_bwd(q, k, v, cum, s_all, s_fin, do, *, rev, name, rider=None):
    n = q.shape[0]

    def body(q_ref, k_ref, v_ref, c_ref, s_ref, sfin_ref, do_ref, *rest):
        if rider is not None:
            ride_in, rest = rest[0], rest[1:]
        dq_ref, dk_ref, dv_ref, dc_ref = rest[:4]
        if rider is not None:
            ride_out, rest = rest[4], rest[:4] + rest[5:]
        dst, s_next, dq_acc, dk_acc, dv_acc = rest[4:9]
        t = SCAN_STEPS - 1 - pl.program_id(1)

        if rider is not None:
            def copies():
                send, recv = rest[9], rest[10]
                if rider[0] == "swap":
                    return _pair_copies([], ride_in, [], ride_out, send, recv)
                x, y, c, chips = _place()
                return [_remote(ride_in.at[2 * cx + cy, rows], ride_out.at[pj, rows], send.at[pj * P_ROW_CHUNKS + pi],
                                recv.at[pj * P_ROW_CHUNKS + pi], (cx, cy, c))
                        for pj, (cx, cy) in enumerate(chips)
                        for pi, (_, rows) in enumerate(_half_chunks(0, (0, 2 * ride_in.shape[1]), 16, which=(1,)))]

            @pl.when((pl.program_id(0) == 0) & (pl.program_id(1) == 0))
            def _():
                for cp in copies():
                    cp.start()

            @pl.when((pl.program_id(0) == n - 1) & (pl.program_id(1) == SCAN_STEPS - 1))
            def _():
                for cp in copies():
                    cp.wait_recv()
                for cp in copies():
                    cp.wait_send()

        @pl.when(pl.program_id(1) == 0)
        def _():
            dst[...] = jnp.zeros_like(dst)
            s_next[...] = sfin_ref[0]

        def chunk(jj, carry):
            j = SCAN_CHUNKS - 1 - jj
            lj = SCAN_CHUNKS - 1 - j if rev else j
            rws = pl.ds(lj * CHUNK, CHUNK)
            for h in range(HEADS):
                kcols, vcols = _head_cols(h)
                qc, kc, cc = q_ref[0, rws, kcols], k_ref[0, rws, kcols], c_ref[0, rws, kcols]
                vc = v_ref[0, rws, vcols]
                doc = jnp.where(t > 0, do_ref[0, rws, vcols], 0.0)
                s_in = s_ref[0, h, j]
                s_out = s_next[h]
                ds_out = dst[h]
                edge = cc[0:1] if rev else cc[CHUNK - 1:CHUNK]
                e_q = jnp.exp(cc)
                e_k = jnp.exp(edge - cc)
                dob = _b16(doc)
                dsb = _b16(ds_out)
                dst[h] = ds_out * jnp.exp(edge) + _tn(dob, _b16(qc * e_q))
                s_next[h] = s_in.astype(F32)
                dq_acc[h] = e_q * _nn(dob, s_in)
                dk_acc[h] = e_k * _nn(_b16(vc), dsb)
                dv_acc[h] = _nt(_b16(kc * e_k), dsb)
                vb = _b16(vc)
                for rows, ref, cols in _sub_blocks(rev):
                    qs, kk, eq, ek = _sub_operands(qc, kc, cc, rows, ref, cols)
                    mask = _sub_mask(rows, cols, rev)
                    rsl = slice(rows[0], rows[0] + rows[1])
                    csl = pl.ds(cols[0], cols[1])
                    qsb, kkb = _b16(qs), _b16(kk)
                    a = jnp.where(mask, _nt(qsb, kkb), 0.0)
                    da = _b16(jnp.where(mask, _nt(dob[rsl], vb[cols[0]:cols[0] + cols[1]]), 0.0))
                    dq_acc[h, pl.ds(rows[0], rows[1]), :] += _nn(da, kkb) * eq
                    dk_acc[h, csl, :] += _tn(da, qsb) * ek
                    dv_acc[h, csl, :] += _tn(_b16(a), dob[rsl])
                dq = dq_acc[h]
                dk = dk_acc[h]
                dc = qc * dq - kc * dk
                bnd = jnp.sum(ds_out * s_out, axis=0, keepdims=True)
                edge_row = 0 if rev else CHUNK - 1
                is_edge = lax.broadcasted_iota(jnp.int32, (CHUNK, HEAD_K), 0) == edge_row
                dq_ref[0, rws, kcols] = _b16(dq)
                dk_ref[0, rws, kcols] = _b16(dk)
                dv_ref[0, rws, vcols] = _b16(dv_acc[h])
                dc_ref[0, rws, kcols] = dc + jnp.where(is_edge, bnd, 0.0)
            return carry

        for jj in range(SCAN_CHUNKS):
            chunk(jj, 0)

    def step_of(u):
        return SCAN_STEPS - 1 - u

    def spec(w):
        return pl.BlockSpec((1, SCAN_ROWS, w), lambda b, u: (b, _scan_block(step_of(u), rev), 0))

    in_specs = [spec(GLA_DK), spec(GLA_DK), spec(D), spec(GLA_DK),
                pl.BlockSpec((1, HEADS, SCAN_CHUNKS, HEAD_V, HEAD_K), lambda b, u: (b, 0, step_of(u), 0, 0)),
                pl.BlockSpec((1, HEADS, HEAD_V, HEAD_K), lambda b, u: (b, 0, 0, 0)),
                pl.BlockSpec((1, SCAN_ROWS, D), lambda b, u: (b, _scan_lat_block(step_of(u), rev), 0))]
    out_specs = [spec(GLA_DK), spec(GLA_DK), spec(D), spec(GLA_DK)]
    out_shape = [jax.ShapeDtypeStruct((n, SEQ_ALL, GLA_DK), BF16), jax.ShapeDtypeStruct((n, SEQ_ALL, GLA_DK), BF16),
                 jax.ShapeDtypeStruct((n, SEQ_ALL, D), BF16), jax.ShapeDtypeStruct((n, SEQ_ALL, GLA_DK), F32)]
    scratch = [pltpu.VMEM((HEADS, HEAD_V, HEAD_K), F32), pltpu.VMEM((HEADS, HEAD_V, HEAD_K), F32),
               pltpu.VMEM((HEADS, CHUNK, HEAD_K), F32), pltpu.VMEM((HEADS, CHUNK, HEAD_K), F32),
               pltpu.VMEM((HEADS, CHUNK, HEAD_V), F32)]
    args = [q, k, v, cum, s_all, s_fin, do]
    if rider is not None:
        kind, arr = rider
        any_spec = pl.BlockSpec(memory_space=pl.ANY)
        in_specs.append(any_spec)
        out_specs.append(any_spec)
        args.append(arr)
        if kind == "swap":
            out_shape += _pair_got_shapes([], arr)
            n_cp = _pair_count([], arr)
        else:
            out_shape.append(jax.ShapeDtypeStruct((3,) + arr.shape[1:], arr.dtype))
            n_cp = 3 * P_ROW_CHUNKS
        scratch += [pltpu.SemaphoreType.DMA((n_cp,)), pltpu.SemaphoreType.DMA((n_cp,))]
    return _pallas(
        body, name=name, grid=(n, SCAN_STEPS), in_specs=in_specs, out_specs=out_specs, out_shape=out_shape,
        scratch_shapes=scratch,
        compiler_params=_params(("parallel" if rider is None else "arbitrary", "arbitrary")),
    )(*args)


def gla_out_proj(o_f, o_b, r, gnorm, gla_proj):
    n = o_f.shape[0]
    tiles = SEQ // TM_OUT

    def body(of_ref, ob_ref, r_ref, g_ref, w_ref, og_ref, y_ref):
        for h in range(HEADS):
            cols = pl.ds(h * HEAD_V, HEAD_V)
            o = of_ref[0, :, cols].astype(F32) + ob_ref[0, :, cols].astype(F32)
            rs = lax.rsqrt(jnp.mean(o * o, axis=-1, keepdims=True) + EPS)
            og_ref[:, cols] = (o * rs * g_ref[...] * _silu(r_ref[:, cols].astype(F32))).astype(og_ref.dtype)
        y_ref[...] = _nn(og_ref[...], w_ref[...]).astype(y_ref.dtype)

    ospec = pl.BlockSpec((1, TM_OUT, D), lambda b, j: (b, j, 0))
    row = pl.BlockSpec((TM_OUT, D), lambda b, j: (b * tiles + j, 0))
    return _pallas(
        body, name="gla_out_proj", grid=(n, tiles),
        in_specs=[ospec, ospec, row, pl.BlockSpec((1, HEAD_V), lambda b, j: (0, 0)),
                  pl.BlockSpec((D, D), lambda b, j: (0, 0))],
        out_specs=[row, row],
        out_shape=[jax.ShapeDtypeStruct((n * SEQ, D), BF16), jax.ShapeDtypeStruct((n * SEQ, D), BF16)],
        compiler_params=_params(("parallel", "parallel")),
    )(o_f, o_b, r, gnorm, gla_proj)


def gla_out_bwd(o_f, o_b, r, dyg, gla_proj, gnorm, og):
    n = o_f.shape[0]
    tiles = SEQ // TM_EW

    def body(of_ref, ob_ref, r_ref, d_ref, w_ref, g_ref, og_ref, do_ref, dr_ref, dg_ref, gw_ref, dog_buf):
        @pl.when((pl.program_id(0) == 0) & (pl.program_id(1) == 0))
        def _():
            dg_ref[...] = jnp.zeros_like(dg_ref)
            gw_ref[...] = jnp.zeros_like(gw_ref)

        gw_ref[...] += _tn(og_ref[...], d_ref[...])
        dog_buf[...] = _nt(d_ref[...], w_ref[...])
        for h in range(HEADS):
            cols = pl.ds(h * HEAD_V, HEAD_V)
            o = of_ref[0, :, cols].astype(F32) + ob_ref[0, :, cols].astype(F32)
            rv = r_ref[:, cols].astype(F32)
            dv = dog_buf[:, cols]
            rs = lax.rsqrt(jnp.mean(o * o, axis=-1, keepdims=True) + EPS)
            oh = o * rs
            dr_ref[:, cols] = (dv * oh * g_ref[...] * _dsilu(rv)).astype(dr_ref.dtype)
            dn = dv * _silu(rv)
            dg_ref[...] += jnp.sum(dn * oh, axis=0, keepdims=True)
            doh = dn * g_ref[...]
            do_ref[0, :, cols] = _b16(rs * (doh - oh * jnp.mean(doh * oh, axis=-1, keepdims=True)))

    ospec = pl.BlockSpec((1, TM_EW, D), lambda b, j: (b, j, 0))
    row = pl.BlockSpec((TM_EW, D), lambda b, j: (b * tiles + j, 0))
    vec = pl.BlockSpec((1, HEAD_V), lambda b, j: (0, 0))
    return _pallas(
        body, name="gla_out_bwd", grid=(n, tiles),
        in_specs=[ospec, ospec, row, row, pl.BlockSpec((D, D), lambda b, j: (0, 0)), vec, row],
        out_specs=[ospec, row, vec, pl.BlockSpec((D, D), lambda b, j: (0, 0))],
        out_shape=[jax.ShapeDtypeStruct((n, SEQ, D), BF16), jax.ShapeDtypeStruct((n * SEQ, D), BF16),
                   jax.ShapeDtypeStruct((1, HEAD_V), F32), jax.ShapeDtypeStruct((D, D), F32)],
        scratch_shapes=[pltpu.VMEM((TM_EW, D), F32)],
        compiler_params=_params(("arbitrary", "arbitrary")),
    )(o_f, o_b, r, dyg, gla_proj, gnorm, og)


TM_OUT = 512


def merge_out_final(p5, y_conv, y_gla, w_out, x2, gate, final_g, target, n_samples):
    t = x2.shape[0]
    tiles = SEQ // TM_OUT

    def body(mc_ref, mg_ref, yc_ref, yg_ref, w_ref, x_ref, gate_ref, g_ref, t_ref,
             mrg_ref, dh_ref, dmo_ref, dgate_ref, dg_ref, loss_ref):
        b, j = pl.program_id(0), pl.program_id(1)
        f = lambda ref: ref[...].astype(F32)
        merged = _b16(_sigmoid(f(mc_ref)) * f(yc_ref) + _sigmoid(f(mg_ref)) * f(yg_ref))
        mrg_ref[...] = merged
        mo_v = _nn(merged, w_ref[...])
        h = x_ref[...] + gate_ref[0] * mo_v
        rs = lax.rsqrt(jnp.mean(h * h, axis=-1, keepdims=True) + EPS)
        nh = h * rs
        err = nh * g_ref[...] - t_ref[...]
        dy = err * (1.0 / D)
        dn = dy * g_ref[...]
        dh = rs * (dn - nh * jnp.mean(dn * nh, axis=-1, keepdims=True))
        dh_ref[...] = dh
        dmo_ref[...] = (dh * gate_ref[0]).astype(dmo_ref.dtype)

        @pl.when(j == 0)
        def _():
            dgate_ref[...] = jnp.zeros_like(dgate_ref)

        @pl.when((b == 0) & (j == 0))
        def _():
            dg_ref[...] = jnp.zeros_like(dg_ref)
            loss_ref[...] = jnp.zeros_like(loss_ref)

        dgate_ref[0] += jnp.sum(dh * mo_v, axis=0, keepdims=True)
        dg_ref[...] += jnp.sum(dy * nh, axis=0, keepdims=True)
        loss_ref[...] += (0.5 / D) * jnp.sum(err * err)

    row = pl.BlockSpec((TM_OUT, D), lambda b, j: (b * tiles + j, 0))
    per = pl.BlockSpec((1, 1, D), lambda b, j: (b, 0, 0))
    vec = pl.BlockSpec((1, D), lambda b, j: (0, 0))
    return _pallas(
        body, name="merge_out_final", grid=(n_samples, tiles),
        in_specs=[row, pl.BlockSpec((TM_OUT, D), lambda b, j: (b * tiles + j, 1)), row, row,
                  pl.BlockSpec((D, D), lambda b, j: (0, 0)), row, per, vec, row],
        out_specs=[row, row, row, per, vec, pl.BlockSpec((8, 128), lambda b, j: (0, 0))],
        out_shape=[jax.ShapeDtypeStruct((t, D), BF16), jax.ShapeDtypeStruct((t, D), F32), jax.ShapeDtypeStruct((t, D), BF16),
                   jax.ShapeDtypeStruct((n_samples, 1, D), F32), jax.ShapeDtypeStruct((1, D), F32),
                   jax.ShapeDtypeStruct((8, 128), F32)],
        compiler_params=_params(("arbitrary", "arbitrary")),
    )(p5, p5, y_conv, y_gla, w_out, x2, gate, final_g, target)


def out_dgrad_merge_bwd(p5, y_conv, y_gla, dmo, w_out, merged):
    t = y_conv.shape[0]

    def body(mc_ref, mg_ref, yc_ref, yg_ref, d_ref, w_ref, mrg_ref, dyc_ref, dyg_ref, dp_ref, gw_ref):
        f = lambda ref: ref[...].astype(F32)

        @pl.when(pl.program_id(0) == 0)
        def _():
            gw_ref[...] = jnp.zeros_like(gw_ref)

        gw_ref[...] += _tn(mrg_ref[...], d_ref[...])
        d = _nt(d_ref[...], w_ref[...])
        sc = _sigmoid(f(mc_ref))
        sg = _sigmoid(f(mg_ref))
        dyc_ref[...] = (d * sc).astype(dyc_ref.dtype)
        dyg_ref[...] = (d * sg).astype(dyg_ref.dtype)
        dp_ref[:, pl.ds(0, D)] = (d * f(yc_ref) * sc * (1.0 - sc)).astype(dp_ref.dtype)
        dp_ref[:, pl.ds(D, D)] = (d * f(yg_ref) * sg * (1.0 - sg)).astype(dp_ref.dtype)

    row = pl.BlockSpec((TM_OUT, D), lambda i: (i, 0))
    return _pallas(
        body, name="out_dgrad_merge_bwd", grid=(t // TM_OUT,),
        in_specs=[row, pl.BlockSpec((TM_OUT, D), lambda i: (i, 1)), row, row, row, pl.BlockSpec((D, D), lambda i: (0, 0)),
                  row],
        out_specs=[row, row, pl.BlockSpec((TM_OUT, 2 * D), lambda i: (i, 0)), pl.BlockSpec((D, D), lambda i: (0, 0))],
        out_shape=[jax.ShapeDtypeStruct((t, D), BF16), jax.ShapeDtypeStruct((t, D), BF16),
                   jax.ShapeDtypeStruct((t, 2 * D), BF16), jax.ShapeDtypeStruct((D, D), F32)],
        compiler_params=_params(("arbitrary",)),
    )(p5, p5, y_conv, y_gla, dmo, w_out, merged)


def local_step(x, ctx, target, mod, wts, small, p_sh, chip, core):
    n = x.shape[0]
    t = n * SEQ
    t_all = t + n * NCTX
    x2 = x.reshape(t, D)
    ctx2 = ctx.reshape(n * NCTX, D)
    tgt2 = target.reshape(t, D)
    scale1, shift, gate = mod

    u = norm_mod_fwd(x2, ctx2, scale1, shift, small["norm_g"])
    p1, p2, p3, p4, p5, p_all = proj_all(u, [wts["w%d" % i] for i in range(1, 6)], [small["b%d" % i] for i in range(1, 6)],
                                         [t, t, t_all, t, t], p_sh, tm=512)
    p_full = jnp.stack([jnp.where(chip == i, p_sh, p_all[i]) for i in range(N_CHIPS)])
    wts = dict(wts, conv_proj=p_full[:, 0:256].reshape(D, D), gla_proj=p_full[:, 256:512].reshape(D, D),
               w_out=p_full[:, 512:768].reshape(D, D))

    aconv = conv_fwd(p1, small["conv_w"], small["conv_b"], n)
    ac, y_conv = ln_gate_proj(aconv, p2, small["conv_ln_g"], small["conv_ln_b"], wts["conv_proj"])

    qs, ks, vs, cum_f, cum_b = gla_prep_fwd(p3, small["upf"], small["upb"], small["bias_f"], small["bias_b"], n)
    o_f, s_f, sfin_f, o_b, s_b, sfin_b = gla_scan_fwd(qs, ks, vs, cum_f, cum_b)
    og, y_gla = gla_out_proj(o_f, o_b, p4, small["gla_norm_g"], wts["gla_proj"])

    merged, dh, dmo, dgate, d_final_g, loss = merge_out_final(p5, y_conv, y_gla, wts["w_out"], x2, gate,
                                                              small["final_norm_g"], tgt2, n)

    g = {"final_norm_g": d_final_g}
    dyc, dyg, dp5, g["w_out"] = out_dgrad_merge_bwd(p5, y_conv, y_gla, dmo, wts["w_out"], merged)

    daconv, dp2, g["conv_ln_g"], g["conv_ln_b"], g["conv_proj"] = ln_gate_bwd(
        aconv, p2, dyc, wts["conv_proj"], small["conv_ln_g"], small["conv_ln_b"], ac)
    dp1, dconv_w, dconv_b = conv_bwd(p1, daconv, small["conv_w"], n)
    g["conv_w"], g["conv_b"] = dconv_w, dconv_b

    do, dp4, g["gla_norm_g"], g["gla_proj"] = gla_out_bwd(o_f, o_b, p4, dyg, wts["gla_proj"], small["gla_norm_g"], og)
    g["proj"] = jnp.concatenate([g["conv_proj"].reshape(N_CHIPS, 256, D), g["gla_proj"].reshape(N_CHIPS, 256, D),
                                 g["w_out"].reshape(N_CHIPS, 256, D)], 1)
    dq_f, dk_f, dv_f, dc_f, gotp = gla_scan_bwd(qs, ks, vs, cum_f, s_f, sfin_f, do, rev=False, name="gla_scan_bwd_f",
                                                rider=("swap", g["proj"]))
    pap16 = pair_add(core, g["proj"], gotp, name="pair_add_p", tr=384)
    dq_b, dk_b, dv_b, dc_b, rbp = gla_scan_bwd(qs, ks, vs, cum_b, s_b, sfin_b, do, rev=True, name="gla_scan_bwd_b",
                                               rider=("exchange", pap16))
    dp3, g["upf"], g["upb"], g["bias_f"], g["bias_b"] = gla_prep_bwd(
        p3, dq_f, dq_b, dk_f, dk_b, dv_f, dv_b, dc_f, dc_b,
        small["upf"], small["upb"], small["bias_f"], small["bias_b"], n)

    dps = [dp1, dp2, dp3, dp4, dp5]
    early = [g["conv_b"].sum(0), g["conv_ln_g"], g["conv_ln_b"], g["bias_f"], g["bias_b"], g["gla_norm_g"],
             g["final_norm_g"], g["conv_w"].sum(0)[:CONV_K], g["upf"][0:16], g["upb"][16:32]]
    got = {}
    for i in [0, 1, 3, 4, 2]:
        dp = dps[i]
        rows = dp.shape[0]
        tn = W3 if dp.shape[1] == W3 else 1024
        others = [j for j in range(5) if j != i]
        swap = ([g["w%d" % (j + 1)] for j in others], _pack(early)) if i == 2 else None
        outs = matmul_tn(u, dp, name="w_in_wgrad_%d" % (i + 1), t=rows, tn=tn, tt=1024 if rows % 1024 == 0 else 768,
                         colsum=True, swap=swap)
        g["w%d" % (i + 1)], g["b%d" % (i + 1)] = outs[0], outs[1]
        if swap is not None:
            got = dict(zip(others, outs[2:2 + len(others)]))
            sall_early = outs[2 + len(others)]
    g["gate"] = dgate
    return loss, dh, dps, g, got, (pap16, rbp), (sall_early, [a.shape for a in early])


def _group_cols(w):
    gv, gg, z = w[..., 0:1024], w[..., 1024:2048], w[..., 2048:3072]
    q, k, v = w[..., 3072:3584], w[..., 3584:4096], w[..., 4096:5120]
    ab = w[..., 5120:5152]
    r, mc, mg = w[..., 5152:6176], w[..., 6176:7200], w[..., 7200:8224]
    g1 = jnp.concatenate([p for j in range(CONV_NCB)
                          for p in (gv[..., CONV_CB * j:CONV_CB * (j + 1)], gg[..., CONV_CB * j:CONV_CB * (j + 1)])], -1)
    pad = jnp.zeros(w.shape[:-1] + (W3 - 2080,), w.dtype)
    g3 = jnp.concatenate([v, q, k, ab, pad], -1)
    return g1, z, g3, r, jnp.concatenate([mc, mg], -1)


def _ungroup_cols(g1, g2, g3, g4, g5):
    gv = jnp.concatenate([g1[..., 2 * CONV_CB * j:2 * CONV_CB * j + CONV_CB] for j in range(CONV_NCB)], -1)
    gg = jnp.concatenate([g1[..., 2 * CONV_CB * j + CONV_CB:2 * CONV_CB * (j + 1)] for j in range(CONV_NCB)], -1)
    v, q, k, ab = g3[..., 0:1024], g3[..., 1024:1536], g3[..., 1536:2048], g3[..., 2048:2080]
    return jnp.concatenate([gv, gg, g2, q, k, v, ab, g4, g5[..., 0:1024], g5[..., 1024:2048]], -1)


def _natural_pieces():
    pieces = [(CONV_CB * j, CONV_CB, 0, 2 * CONV_CB * j) for j in range(CONV_NCB)]
    pieces += [(1024 + CONV_CB * j, CONV_CB, 0, 2 * CONV_CB * j + CONV_CB) for j in range(CONV_NCB)]
    pieces += [(2048, 1024, 1, 0), (3072, 512, 2, O3_Q), (3584, 512, 2, O3_K), (4096, 1024, 2, O3_V), (5120, 32, 2, O3_AB),
               (5152, 1024, 3, 0), (6176, 1024, 4, 0), (7200, 1024, 4, 1024)]
    return sorted(pieces)


def _ungroup_to_shards(groups):
    shards = []
    for i in range(N_CHIPS):
        lo, hi = i * W_IN_SHARD, (i + 1) * W_IN_SHARD
        parts = []
        for nat, width, g, gcol in _natural_pieces():
            a, b = max(nat, lo), min(nat + width, hi)
            if a < b:
                parts.append(groups[g][:, gcol + a - nat:gcol + b - nat])
        shards.append(jnp.concatenate(parts, 1))
    return jnp.stack(shards)


def _pad_up(up, row0):
    return jnp.zeros((128, GLA_DK), F32).at[row0:row0 + up.shape[0]].set(up)


def _adamw_math(w, g, m, v):
    m = ADAM_B1 * m + (1.0 - ADAM_B1) * g
    v = ADAM_B2 * v + (1.0 - ADAM_B2) * (g * g)
    m_hat = m / (1.0 - ADAM_B1 ** ADAM_STEP)
    v_hat = v / (1.0 - ADAM_B2 ** ADAM_STEP)
    delta = -ADAM_LR * (m_hat / (jnp.sqrt(v_hat) + ADAM_EPS) + ADAM_WD * w)
    return delta, m, v


def adamw2d(w, g, m, v, *, name, tr, tcols=None):
    rows, cols = w.shape[-2:]

    def body(w_ref, g_ref, m_ref, v_ref, d_ref, nm_ref, nv_ref):
        d_ref[...], nm_ref[...], nv_ref[...] = _adamw_math(w_ref[...], g_ref[...], m_ref[...], v_ref[...])

    tcols = cols if tcols is None else tcols
    if w.ndim == 3:
        spec = pl.BlockSpec((1, tr, tcols), lambda i, j: (0, i, j))
    else:
        spec = pl.BlockSpec((tr, tcols), lambda i, j: (i, j))
    return _pallas(
        body, name=name, grid=(rows // tr, cols // tcols), in_specs=[spec] * 4, out_specs=[spec] * 3,
        out_shape=[jax.ShapeDtypeStruct(w.shape, F32)] * 3, compiler_params=_params(("parallel", "parallel")),
    )(w, g, m, v)


def adamw_many(ws, gs, ms, vs):
    k = len(ws)
    two = lambda a: a.reshape((-1, a.shape[-1]))

    def body(*refs):
        w_refs, g_refs, m_refs, v_refs = refs[:k], refs[k:2 * k], refs[2 * k:3 * k], refs[3 * k:4 * k]
        d_refs, nm_refs, nv_refs = refs[4 * k:5 * k], refs[5 * k:6 * k], refs[6 * k:7 * k]
        for i in range(k):
            d_refs[i][...], nm_refs[i][...], nv_refs[i][...] = _adamw_math(
                w_refs[i][...], g_refs[i][...], m_refs[i][...], v_refs[i][...])

    shapes = [jax.ShapeDtypeStruct(two(a).shape, F32) for a in ws]
    outs = _pallas(body, name="adamw_small", out_shape=shapes * 3, compiler_params=_params())(
        *[two(a) for a in ws], *[two(a) for a in gs], *[two(a) for a in ms], *[two(a) for a in vs])
    back = lambda lst: [o.reshape(a.shape) for o, a in zip(lst, ws)]
    return back(outs[:k]), back(outs[k:2 * k]), back(outs[2 * k:])


def sum_devices(salls):
    k = len(salls)

    def body(*refs):
        for i in range(k):
            acc = refs[i][0]
            for d in range(1, N_DEV):
                acc = acc + refs[i][d]
            refs[k + i][...] = acc

    return _pallas(body, name="sum_devices", out_shape=[jax.ShapeDtypeStruct(a.shape[1:], F32) for a in salls],
                   compiler_params=_params())(*salls)


def pair_add(core, g, got, *, name, tr):
    n, rows, cols = got.shape
    g4 = g.reshape(n, 2, rows, cols)

    def body(core_ref, g_ref, got_ref, ob_ref):
        del core_ref
        ob_ref[0] = (g_ref[0, 0] + got_ref[0]).astype(BF16)

    spec = pl.BlockSpec((1, tr, cols), lambda i, t, core_ref: (i, t, 0))
    return _pallas(
        body, name=name,
        grid_spec=pltpu.PrefetchScalarGridSpec(
            num_scalar_prefetch=1, grid=(n, rows // tr),
            in_specs=[pl.BlockSpec((1, 1, tr, cols), lambda i, t, core_ref: (i, core_ref[0], t, 0)), spec],
            out_specs=spec),
        out_shape=jax.ShapeDtypeStruct(got.shape, BF16),
        compiler_params=_params(("parallel", "parallel")))(core, g4, got)


def pair_add_groups(core, gs, gots, *, tr, swap=None):
    k = len(gs)
    rows = gots[0].shape[0]
    n_sw = 0 if swap is None else len(swap)
    n_steps = rows // tr

    def body(core_ref, *refs):
        del core_ref
        for i in range(k):
            refs[2 * k + n_sw + i][...] = (refs[i][0] + refs[k + i][...]).astype(BF16)
        if swap is not None:
            g_refs = refs[2 * k:2 * k + n_sw]
            got_refs = refs[3 * k + n_sw:3 * k + 2 * n_sw]
            sems = refs[3 * k + 2 * n_sw:]
            t = pl.program_id(0)

            @pl.when(t == 0)
            def _():
                for cp in _pair_copies(g_refs, None, got_refs, None, *sems):
                    cp.start()

            @pl.when(t == n_steps - 1)
            def _():
                for cp in _pair_copies(g_refs, None, got_refs, None, *sems):
                    cp.wait_recv()
                for cp in _pair_copies(g_refs, None, got_refs, None, *sems):
                    cp.wait_send()

    g_specs = [pl.BlockSpec((1, tr, a.shape[1]), lambda t, core_ref: (core_ref[0], t, 0)) for a in gots]
    r_specs = [pl.BlockSpec((tr, a.shape[1]), lambda t, core_ref: (t, 0)) for a in gots]
    any_spec = pl.BlockSpec(memory_space=pl.ANY)
    out_shape = [jax.ShapeDtypeStruct(a.shape, BF16) for a in gots]
    scratch, extra = [], []
    if swap is not None:
        out_shape += _pair_got_shapes(swap, None)
        extra = list(swap)
        scratch = [pltpu.SemaphoreType.DMA((_pair_count(swap, None),)), pltpu.SemaphoreType.DMA((_pair_count(swap, None),))]
    return _pallas(
        body, name="pair_add_w",
        grid_spec=pltpu.PrefetchScalarGridSpec(num_scalar_prefetch=1, grid=(n_steps,),
                                               in_specs=g_specs + r_specs + [any_spec] * n_sw,
                                               out_specs=r_specs + [any_spec] * n_sw, scratch_shapes=scratch),
        out_shape=out_shape,
        compiler_params=_params(("parallel" if swap is None else "arbitrary",)),
    )(core, *[a.reshape(2, rows, a.shape[1]) for a in gs], *gots, *extra)


def chip_add(place, pas, rbs, *, steps):
    k = len(pas)

    def body(place_ref, *refs):
        del place_ref
        for i in range(k):
            m_ref, r_ref, o_ref = refs[i], refs[k + i], refs[2 * k + i]
            o_ref[0] = ((m_ref[0].astype(F32) + r_ref[0].astype(F32)) + r_ref[1].astype(F32)) + r_ref[2].astype(F32)

    trs = [pa.shape[1] // steps for pa in pas]
    return _pallas(
        body, name="chip_add",
        grid_spec=pltpu.PrefetchScalarGridSpec(
            num_scalar_prefetch=1, grid=(steps,),
            in_specs=[pl.BlockSpec((1, tr, pa.shape[2]), lambda t, place_ref: (place_ref[0], t, 0)) for pa, tr in zip(pas, trs)]
            + [pl.BlockSpec((3, tr, pa.shape[2]), lambda t, place_ref: (0, t, 0)) for pa, tr in zip(pas, trs)],
            out_specs=[pl.BlockSpec((1, tr, pa.shape[2]), lambda t, place_ref: (place_ref[1], t, 0))
                       for pa, tr in zip(pas, trs)]),
        out_shape=[jax.ShapeDtypeStruct((2,) + pa.shape[1:], F32) for pa in pas],
        compiler_params=_params(("parallel",)))(place, *pas, *rbs)


def ada_bwd(call, cctx_rows, dm_shard, dm_full, adaw):
    nsh = adaw.shape[1]

    def body(c_ref, cc_ref, dms_ref, dmf_ref, w_ref, gw_ref, gb_ref, pq_ref):
        a_lat = _silu(c_ref[...])
        a_ctx = _silu(cc_ref[...])
        dms = dms_ref[...]
        gw_ref[...] = _tn(a_lat, dms[0:64], HI) + _tn(a_ctx, dms[64:72], HI)
        gb_ref[...] = jnp.sum(dmf_ref[...], axis=0, keepdims=True)
        part = _nt(dms[64:72], w_ref[...], HI)
        pq_ref[...] = jnp.zeros_like(pq_ref) + jnp.sum(part, axis=0, keepdims=True)

    return _pallas(body, name="ada_bwd",
                   out_shape=[jax.ShapeDtypeStruct((D, nsh), F32), jax.ShapeDtypeStruct((1, 3 * D), F32),
                              jax.ShapeDtypeStruct((8, D), F32)],
                   compiler_params=_params())(call, cctx_rows, dm_shard, dm_full, adaw)


def cctx_grad(pq_all, cctx_rows):
    def body(p_ref, c_ref, o_ref):
        acc = p_ref[0]
        for qi in range(1, N_CHIPS):
            acc = acc + p_ref[qi]
        o_ref[...] = acc * _dsilu(c_ref[...])

    return _pallas(body, name="cctx_grad", out_shape=jax.ShapeDtypeStruct((8, D), F32),
                   compiler_params=_params())(pq_all, cctx_rows)


def _place():
    x, y, c = lax.axis_index("x"), lax.axis_index("y"), lax.axis_index("c")
    chips = [(1 - x, y), (x, 1 - y), (1 - x, 1 - y)]
    return x, y, c, chips


def _all_peers(x, y, c):
    return [((1 - x) if r & 4 else x, (1 - y) if r & 2 else y, (1 - c) if r & 1 else c) for r in range(1, N_DEV)]


def _remote(src, dst, send_sem, recv_sem, dev):
    return pltpu.make_async_remote_copy(src_ref=src, dst_ref=dst, send_sem=send_sem, recv_sem=recv_sem,
                                        device_id=dev, device_id_type=MESH)


ANY = pl.BlockSpec(memory_space=pl.ANY)
VMEM = pl.BlockSpec(memory_space=pltpu.VMEM)
F_ROWS = 16


W_ROW_CHUNKS = 4
P_ROW_CHUNKS = 2
N_BULK = W_ROW_CHUNKS + P_ROW_CHUNKS


def _half_chunks(core, n_rows, align, which=(0, 1)):
    out = []
    for a, k in ((0, W_ROW_CHUNKS), (1, P_ROW_CHUNKS)):
        if a not in which:
            continue
        half = n_rows[a] // 2
        size = half // k
        for i in range(k):
            start = core * half + i * size
            out.append((a, pl.ds(start if isinstance(start, int) else pl.multiple_of(start, align), size)))
    return out


def gather_weights(c8, cctx8, adaw, adab, w_sh, fp):
    nsh = adaw.shape[1]

    def body(c_ref, cctx_ref, adaw_ref, adab_ref, w_ref, fp_ref, wall_ref, fall_ref, call_ref, mall_ref,
             abuf, w_send, w_recv, h_send, h_recv, c_send, c_recv, m_send, m_recv, f_send, f_recv):
        x, y, c, chips = _place()
        q = 2 * x + y
        dev = 4 * x + 2 * y + c
        qs = [2 * cx + cy for cx, cy in chips]
        sib = (x, y, 1 - c)
        srcs, dsts = (w_ref,), (wall_ref,)
        n_rows = (w_ref.shape[0],)
        mine = _half_chunks(c, n_rows, 16, which=(0,))
        other = _half_chunks(1 - c, n_rows, 16, which=(0,))

        bulk = [[_remote(srcs[a].at[rows], dsts[a].at[q, rows], w_send.at[j * N_BULK + i], w_recv.at[j * N_BULK + i],
                         (*chips[j], c)) for i, (a, rows) in enumerate(mine)] for j in range(3)]
        fall_ref[q] = fp_ref[...]
        small = [_remote(fp_ref, fall_ref.at[q], f_send.at[j], f_recv.at[j], (*chips[j], c)) for j in range(3)]
        my_rows = pl.ds(pl.multiple_of(8 * dev, 8), 8)
        call_ref[my_rows, :] = c_ref[...]
        cond = [_remote(c_ref, call_ref.at[my_rows, :], c_send.at[r], c_recv.at[r], peer)
                for r, peer in enumerate(_all_peers(x, y, c))]
        for cp in sum(bulk, []) + small + cond:
            cp.start()
        for cp in cond:
            cp.wait_recv()

        abuf[pl.ds(0, 64), :] = _silu(call_ref[...])
        abuf[pl.ds(64, 8), :] = _silu(cctx_ref[...])
        mall_ref[q] = _nn(abuf[...], adaw_ref[...], HI) + adab_ref[...]
        mod = [_remote(mall_ref.at[q], mall_ref.at[q], m_send.at[j], m_recv.at[j], (*chips[j], c)) for j in range(3)]
        for cp in mod:
            cp.start()

        handed = []
        for j in range(3):
            for i, (a, rows) in enumerate(mine):
                bulk[j][i].wait_recv()
                cp = _remote(dsts[a].at[qs[j], rows], dsts[a].at[qs[j], rows],
                             h_send.at[j * N_BULK + i], h_recv.at[j * N_BULK + i], sib)
                cp.start()
                handed.append(cp)
        for j in range(3):
            for i, (a, rows) in enumerate(other):
                _remote(dsts[a].at[qs[j], rows], dsts[a].at[qs[j], rows],
                        h_send.at[j * N_BULK + i], h_recv.at[j * N_BULK + i], sib).wait_recv()
        for cp in mod + small:
            cp.wait_recv()
        for cp in sum(bulk, []) + small + cond + mod + handed:
            cp.wait_send()

    def dma(n):
        return pltpu.SemaphoreType.DMA((n,))

    return _pallas(
        body, name="gather_weights",
        in_specs=[VMEM, VMEM, VMEM, VMEM, ANY, VMEM],
        out_specs=[ANY, VMEM, VMEM, VMEM],
        out_shape=[jax.ShapeDtypeStruct((N_CHIPS,) + w_sh.shape, BF16),
                   jax.ShapeDtypeStruct((N_CHIPS, F_ROWS, D), F32),
                   jax.ShapeDtypeStruct((8 * N_DEV, D), F32), jax.ShapeDtypeStruct((N_CHIPS, MOD_ROWS, nsh), F32)],
        scratch_shapes=[pltpu.VMEM((MOD_ROWS, D), F32), dma(3 * N_BULK), dma(3 * N_BULK), dma(3 * N_BULK), dma(3 * N_BULK),
                        dma(7), dma(7), dma(3), dma(3), dma(3), dma(3)],
        compiler_params=_params(),
    )(c8, cctx8, adaw, adab, w_sh, fp)


def _pair_count(gs, gp):
    return len(gs) * W_ROW_CHUNKS + (0 if gp is None else N_CHIPS * P_ROW_CHUNKS)


def _pair_got_shapes(gs, gp):
    shapes = [jax.ShapeDtypeStruct((D // 2, a.shape[1]), F32) for a in gs]
    if gp is not None:
        shapes.append(jax.ShapeDtypeStruct((N_CHIPS, gp.shape[1] // 2, gp.shape[2]), F32))
    return shapes


def _pair_copies(g_refs, gp_ref, got_refs, gotp_ref, a_send, a_recv):
    x, y, c, _ = _place()
    sib = (x, y, 1 - c)
    pair = []
    half, size = D // 2, D // 2 // W_ROW_CHUNKS
    for gi in range(len(g_refs)):
        for i in range(W_ROW_CHUNKS):
            k = len(pair)
            rows_o = pl.ds(pl.multiple_of((1 - c) * half + i * size, 8), size)
            pair.append(_remote(g_refs[gi].at[rows_o], got_refs[gi].at[pl.ds(i * size, size)],
                                a_send.at[k], a_recv.at[k], sib))
    if gp_ref is not None:
        half, size = gp_ref.shape[1] // 2, gp_ref.shape[1] // 2 // P_ROW_CHUNKS
        for s in range(N_CHIPS):
            for i in range(P_ROW_CHUNKS):
                k = len(pair)
                rows_o = pl.ds(pl.multiple_of((1 - c) * half + i * size, 8), size)
                pair.append(_remote(gp_ref.at[s, rows_o], gotp_ref.at[s, pl.ds(i * size, size)],
                                    a_send.at[k], a_recv.at[k], sib))
    return pair


def gather_small(sm):
    rows = sm.shape[0]

    def body(sm_ref, sall_ref, s_send, s_recv):
        x, y, c, _ = _place()
        dev = 4 * x + 2 * y + c
        sall_ref[dev] = sm_ref[...]
        small = [_remote(sm_ref, sall_ref.at[dev], s_send.at[r], s_recv.at[r], peer)
                 for r, peer in enumerate(_all_peers(x, y, c))]
        for cp in small:
            cp.start()
        for cp in small:
            cp.wait_recv()
        for cp in small:
            cp.wait_send()

    return _pallas(
        body, name="gather_small", in_specs=[VMEM], out_specs=VMEM,
        out_shape=jax.ShapeDtypeStruct((N_DEV, rows, D), F32),
        scratch_shapes=[pltpu.SemaphoreType.DMA((7,)), pltpu.SemaphoreType.DMA((7,))],
        compiler_params=_params(),
    )(sm)


def pair_share(ghw, ghp, pq):
    def body(ghw_ref, ghp_ref, pq_ref, outw_ref, outp_ref, pqa_ref, send, recv, p_send, p_recv):
        del ghw_ref, ghp_ref
        x, y, c, chips = _place()
        q = 2 * x + y
        refs = (outw_ref, outp_ref)
        n_rows = (2 * outw_ref.shape[1], 2 * outp_ref.shape[1])
        pair = [_remote(refs[a].at[c, rows], refs[a].at[c, rows], send.at[i], recv.at[i], (x, y, 1 - c))
                for i, (a, rows) in enumerate(_half_chunks(0, n_rows, 8))]
        pqa_ref[q] = pq_ref[...]
        small = [_remote(pq_ref, pqa_ref.at[q], p_send.at[j], p_recv.at[j], (*chips[j], c)) for j in range(3)]
        for cp in pair + small:
            cp.start()
        for i, (a, rows) in enumerate(_half_chunks(0, n_rows, 8)):
            _remote(refs[a].at[1 - c, rows], refs[a].at[1 - c, rows], send.at[i], recv.at[i], (x, y, 1 - c)).wait_recv()
        for cp in small:
            cp.wait_recv()
        for cp in pair + small:
            cp.wait_send()

    return _pallas(
        body, name="pair_share", in_specs=[ANY, ANY, VMEM], out_specs=[ANY, ANY, VMEM],
        out_shape=[jax.ShapeDtypeStruct(ghw.shape, F32), jax.ShapeDtypeStruct(ghp.shape, F32),
                   jax.ShapeDtypeStruct((N_CHIPS, 8, D), F32)],
        scratch_shapes=[pltpu.SemaphoreType.DMA((N_BULK,)), pltpu.SemaphoreType.DMA((N_BULK,)),
                        pltpu.SemaphoreType.DMA((3,)), pltpu.SemaphoreType.DMA((3,))],
        input_output_aliases={0: 0, 1: 1},
        compiler_params=_params(),
    )(ghw, ghp, pq)


def _rows_of(shape):
    size = 1
    for s in shape:
        size *= s
    return -(-size // D)


def _pack(arrs, rows_multiple=8):
    parts = []
    total = 0
    for a in arrs:
        f = a.reshape(-1).astype(F32)
        r = _rows_of(a.shape)
        parts.append(jnp.pad(f, (0, r * D - f.shape[0])))
        total += r
    pad_rows = (-total) % rows_multiple
    if pad_rows:
        parts.append(jnp.zeros((pad_rows * D,), F32))
    return jnp.concatenate(parts).reshape(-1, D)


def _unpack(p, shapes):
    out = []
    r0 = 0
    for shp in shapes:
        r = _rows_of(shp)
        size = 1
        for s in shp:
            size *= s
        out.append(p[r0:r0 + r].reshape(-1)[:size].reshape(shp))
        r0 += r
    return out


WEIGHT_NAMES = ['c_ctx', 'ada_w', 'ada_b', 'norm_g', 'w_in', 'b_in', 'conv_w', 'conv_b', 'conv_ln_g', 'conv_ln_b',
                'conv_proj', 'decay_up_fwd', 'decay_bias_fwd', 'decay_up_bwd', 'decay_bias_bwd', 'gla_norm_g', 'gla_proj',
                'w_out', 'final_norm_g']
SMALL_NAMES = ['c_ctx', 'ada_b', 'norm_g', 'b_in', 'conv_w', 'conv_b', 'conv_ln_g', 'conv_ln_b', 'decay_up_fwd',
               'decay_bias_fwd', 'decay_up_bwd', 'decay_bias_bwd', 'gla_norm_g', 'final_norm_g']


def kernel(x, c, ctx, c_ctx, ada_w, ada_b, norm_g, w_in, b_in, conv_w, conv_b, conv_ln_g, conv_ln_b, conv_proj, decay_up_fwd, decay_bias_fwd, decay_up_bwd, decay_bias_bwd, gla_norm_g, gla_proj, w_out, final_norm_g, loss_target, m_c_ctx, m_ada_w, m_ada_b, m_norm_g, m_w_in, m_b_in, m_conv_w, m_conv_b, m_conv_ln_g, m_conv_ln_b, m_conv_proj, m_decay_up_fwd, m_decay_bias_fwd, m_decay_up_bwd, m_decay_bias_bwd, m_gla_norm_g, m_gla_proj, m_w_out, m_final_norm_g, v_c_ctx, v_ada_w, v_ada_b, v_norm_g, v_w_in, v_b_in, v_conv_w, v_conv_b, v_conv_ln_g, v_conv_ln_b, v_conv_proj, v_decay_up_fwd, v_decay_bias_fwd, v_decay_up_bwd, v_decay_bias_bwd, v_gla_norm_g, v_gla_proj, v_w_out, v_final_norm_g):
    w = dict(c_ctx=c_ctx, ada_w=ada_w, ada_b=ada_b, norm_g=norm_g, w_in=w_in, b_in=b_in, conv_w=conv_w, conv_b=conv_b,
             conv_ln_g=conv_ln_g, conv_ln_b=conv_ln_b, conv_proj=conv_proj, decay_up_fwd=decay_up_fwd,
             decay_bias_fwd=decay_bias_fwd, decay_up_bwd=decay_up_bwd, decay_bias_bwd=decay_bias_bwd,
             gla_norm_g=gla_norm_g, gla_proj=gla_proj, w_out=w_out, final_norm_g=final_norm_g)
    m = dict(c_ctx=m_c_ctx, ada_w=m_ada_w, ada_b=m_ada_b, norm_g=m_norm_g, w_in=m_w_in, b_in=m_b_in, conv_w=m_conv_w,
             conv_b=m_conv_b, conv_ln_g=m_conv_ln_g, conv_ln_b=m_conv_ln_b, conv_proj=m_conv_proj,
             decay_up_fwd=m_decay_up_fwd, decay_bias_fwd=m_decay_bias_fwd, decay_up_bwd=m_decay_up_bwd,
             decay_bias_bwd=m_decay_bias_bwd, gla_norm_g=m_gla_norm_g, gla_proj=m_gla_proj, w_out=m_w_out,
             final_norm_g=m_final_norm_g)
    v = dict(c_ctx=v_c_ctx, ada_w=v_ada_w, ada_b=v_ada_b, norm_g=v_norm_g, w_in=v_w_in, b_in=v_b_in, conv_w=v_conv_w,
             conv_b=v_conv_b, conv_ln_g=v_conv_ln_g, conv_ln_b=v_conv_ln_b, conv_proj=v_conv_proj,
             decay_up_fwd=v_decay_up_fwd, decay_bias_fwd=v_decay_bias_fwd, decay_up_bwd=v_decay_up_bwd,
             decay_bias_bwd=v_decay_bias_bwd, gla_norm_g=v_gla_norm_g, gla_proj=v_gla_proj, w_out=v_w_out,
             final_norm_g=v_final_norm_g)
    n = x.shape[0]
    ax, ay, ac = lax.axis_index("x"), lax.axis_index("y"), lax.axis_index("c")
    q = 2 * ax + ay
    dev = 4 * ax + 2 * ay + ac
    nsh = ada_w.shape[2]

    w_sh = w_in[0].astype(BF16)
    p_sh = jnp.concatenate([conv_proj[0], gla_proj[0], w_out[0]], 0).astype(BF16)
    fp = _pack([conv_w[0], decay_up_fwd[0], decay_up_bwd[0]], F_ROWS)
    c8 = jnp.pad(c, ((0, 8 - n), (0, 0)))
    cctx8 = jnp.pad(c_ctx[None], ((0, 7), (0, 0)))
    adab_sh = lax.dynamic_slice(ada_b, (0, q * nsh), (1, nsh))
    w_all, fall, call, mall = gather_weights(c8, cctx8, ada_w[0], adab_sh, w_sh, fp)

    mod_all = jnp.transpose(mall, (1, 0, 2)).reshape(MOD_ROWS, 3 * D)
    mod_mine = lax.dynamic_slice(mod_all, (8 * dev, 0), (n, 3 * D))
    mod_ctx = mod_all[64:65]
    shift = jnp.concatenate([mod_mine[:, 0:D], mod_ctx[:, 0:D]], 0)[:, None, :]
    scale1 = 1.0 + jnp.concatenate([mod_mine[:, D:2 * D], mod_ctx[:, D:2 * D]], 0)[:, None, :]
    gate = mod_mine[:, 2 * D:3 * D][:, None, :]

    own = lambda i, mine, got: jnp.where(q == i, mine, got)
    g1, g2, g3, g4, g5 = _group_cols(jnp.concatenate([own(i, w_sh, w_all[i]) for i in range(N_CHIPS)], 1))
    wts = dict(w1=g1, w2=g2, w3=g3, w4=g4, w5=g5)
    f_parts = [_unpack(fall[i], [conv_w.shape[1:], decay_up_fwd.shape[1:], decay_up_bwd.shape[1:]]) for i in range(N_CHIPS)]
    conv_w_full = jnp.concatenate([p[0] for p in f_parts], 1)
    upf_full = jnp.concatenate([p[1] for p in f_parts], 1)
    upb_full = jnp.concatenate([p[2] for p in f_parts], 1)
    b1, b2, b3, b4, b5 = _group_cols(b_in)
    small = dict(b1=b1, b2=b2, b3=b3, b4=b4, b5=b5, norm_g=norm_g,
                 conv_w=jnp.pad(conv_w_full, ((0, 1), (0, 0))), conv_b=conv_b, conv_ln_g=conv_ln_g, conv_ln_b=conv_ln_b,
                 upf=_split3(_pad_up(upf_full, 0)), upb=_split3(_pad_up(upb_full, 16)),
                 bias_f=decay_bias_fwd, bias_b=decay_bias_bwd,
                 gla_norm_g=gla_norm_g, final_norm_g=final_norm_g[None])

    core = ac.astype(jnp.int32).reshape(1)
    chip = q.astype(jnp.int32).reshape(1)
    loss_part, dh, dps, g, got, (pap16, rbp), (sall1, early_shapes) = local_step(
        x, ctx, loss_target, (scale1, shift, gate), wts, small, p_sh, q, core)

    gs = [g["w%d" % i] for i in range(1, 6)]
    done = [0, 1, 3, 4]
    *sums, got[2] = pair_add_groups(core, [gs[i] for i in done], [got[i] for i in done], tr=128, swap=[gs[2]])
    halves = dict(zip(done, sums))
    halves[2] = pair_add(core, gs[2][None], got[2][None], name="pair_add_w3", tr=128)[0]
    halves = [halves[i] for i in range(5)]
    paw16 = _ungroup_to_shards(halves)
    grad_x2, dshift, dscale, g["norm_g"], rbw = dgrad_norm_bwd(
        dps, [wts["w%d" % i] for i in range(1, 6)], paw16, x.reshape(n * SEQ, D), ctx.reshape(n * NCTX, D), dh,
        scale1, norm_g, tm=256)

    dm_mine = jnp.concatenate([dshift[:n, 0], dscale[:n, 0], g["gate"][:, 0]], -1)
    dm_ctx = jnp.concatenate([dshift[n, 0], dscale[n, 0], jnp.zeros((D,), F32)], -1)
    d_b_in = _ungroup_cols(*[g["b%d" % i] for i in range(1, 6)])
    late = [g["norm_g"], dm_mine, dm_ctx, d_b_in, loss_part[0, 0:1]]
    late_shapes = [a.shape for a in late]
    sall2 = gather_small(_pack(late))
    sum_early, sum_late = sum_devices([sall1, sall2])
    (s_conv_b, s_ln_g, s_ln_b, s_bias_f, s_bias_b, s_gla_g, s_final_g, s_conv_w, s_upf,
     s_upb) = _unpack(sum_early, early_shapes)
    s_late = _unpack(sum_late, late_shapes)
    s_norm_g, s_b_in, loss = s_late[0], s_late[3], s_late[4][0]
    r_mine, r_ctx = 1, 1 + 3 * n
    dm_all = sall2[:, r_mine:r_ctx].reshape(N_DEV, n, 3 * D)
    dm_full = jnp.concatenate([jnp.pad(dm_all, ((0, 0), (0, 8 - n), (0, 0))).reshape(8 * N_DEV, 3 * D),
                               sall2[:, r_ctx:r_ctx + 3].reshape(N_DEV, 3 * D)], 0)
    dm_shard = lax.dynamic_slice(dm_full, (0, q * nsh), (MOD_ROWS, nsh))
    cctx_rows = jnp.broadcast_to(c_ctx[None], (8, D))
    g_ada_w, g_ada_b, pq = ada_bwd(call, cctx_rows, dm_shard, dm_full, ada_w[0])

    place = jnp.concatenate([chip, core])
    ghw, ghp = chip_add(place, [paw16, pap16], [rbw, rbp], steps=4)
    gw_mine, gp_mine, pq_all = pair_share(ghw, ghp, pq)
    gp_mine = gp_mine.reshape(768, D)
    g_c_ctx = cctx_grad(pq_all, cctx_rows)[0]

    grads = dict(
        c_ctx=g_c_ctx, ada_w=g_ada_w[None], ada_b=g_ada_b, norm_g=s_norm_g,
        w_in=gw_mine.reshape(1, D, W_IN_SHARD), b_in=s_b_in,
        conv_w=lax.dynamic_slice(s_conv_w, (0, q * 256), (CONV_K, 256))[None], conv_b=s_conv_b,
        conv_ln_g=s_ln_g, conv_ln_b=s_ln_b, conv_proj=gp_mine[0:256][None],
        decay_up_fwd=lax.dynamic_slice(s_upf, (0, q * 128), (16, 128))[None], decay_bias_fwd=s_bias_f,
        decay_up_bwd=lax.dynamic_slice(s_upb, (0, q * 128), (16, 128))[None], decay_bias_bwd=s_bias_b,
        gla_norm_g=s_gla_g, gla_proj=gp_mine[256:512][None], w_out=gp_mine[512:768][None],
        final_norm_g=s_final_g[0])

    delta, new_m, new_v = {}, {}, {}
    tr_ = lambda a: jnp.swapaxes(a, 1, 2)
    g_w_in_t = tr_(grads["w_in"])
    grads["w_in"] = tr_(g_w_in_t)
    d_, m_, v_ = adamw2d(tr_(w_in), g_w_in_t, tr_(m_w_in), tr_(v_w_in), name="adamw_w_in", tr=W_IN_SHARD, tcols=128)
    delta["w_in"], new_m["w_in"], new_v["w_in"] = tr_(d_), tr_(m_), tr_(v_)
    rest = SMALL_NAMES + ["conv_proj", "gla_proj", "w_out", "ada_w"]
    d_, m_, v_ = adamw_many([w[nm] for nm in rest], [grads[nm].reshape(w[nm].shape) for nm in rest],
                            [m[nm] for nm in rest], [v[nm] for nm in rest])
    for nm, a, b, cc in zip(rest, d_, m_, v_):
        delta[nm], new_m[nm], new_v[nm] = a, b, cc

    grad_x = grad_x2.reshape(x.shape)
    return (loss, grad_x, *[grads[nm].reshape(w[nm].shape) for nm in WEIGHT_NAMES], *[delta[nm] for nm in WEIGHT_NAMES],
            *[new_m[nm] for nm in WEIGHT_NAMES], *[new_v[nm] for nm in WEIGHT_NAMES])
```
